```python
import jax, jax.numpy as jnp
from jax import lax
import numpy as np

D_MODEL = 1024
BATCH = 8
SEQ = 4096
DEPTH = 1

CHUNK = 64
EPS = 1e-6
N_HEADS_A = 8
HEAD_DIM_A = 64
D_A = N_HEADS_A * HEAD_DIM_A
N_PREV_CHUNKS = 8
BAND = (N_PREV_CHUNKS + 1) * CHUNK
REL_CLIP = 128
N_REL = 2 * REL_CLIP + 1
SGU_CHUNK = 128
N_GROUPS_B = 4
GROUP_DIM_B = 128
D_B = N_GROUPS_B * GROUP_DIM_B
SPLITS = (D_A, D_A, D_A, D_A, D_B, D_B, D_B, D_MODEL, D_MODEL)
D_IN = sum(SPLITS)
NEG_INF = -1e30

kernel_name = 'hybrid_chunked_attn_gmlp_gated'


def rmsnorm(x, g):
    xf = x.astype(jnp.float32)
    y = xf * lax.rsqrt(jnp.mean(xf * xf, axis=-1, keepdims=True) + EPS)
    return (y * g.astype(jnp.float32)).astype(x.dtype)


def layernorm(x, g, b):
    xf = x.astype(jnp.float32)
    mu = jnp.mean(xf, axis=-1, keepdims=True)
    var = jnp.mean(jnp.square(xf - mu), axis=-1, keepdims=True)
    y = (xf - mu) * lax.rsqrt(var + EPS)
    return (y * g.astype(jnp.float32) + b.astype(jnp.float32)).astype(x.dtype)


def chunked_rel_attention(q, k, v, rel_bias):
    b, s, _ = q.shape
    nc = s // CHUNK
    qc = q.reshape(b, nc, CHUNK, N_HEADS_A, HEAD_DIM_A)

    def band(t):
        t = t.reshape(b, nc, CHUNK, N_HEADS_A, HEAD_DIM_A)
        tp = jnp.pad(t, ((0, 0), (N_PREV_CHUNKS, 0), (0, 0), (0, 0), (0, 0)))
        return jnp.concatenate([tp[:, j:j + nc] for j in range(N_PREV_CHUNKS + 1)], axis=2)

    kb, vb = band(k), band(v)
    q_off = jnp.arange(CHUNK)
    k_off = jnp.arange(BAND) - N_PREV_CHUNKS * CHUNK
    dist = q_off[:, None] - k_off[None, :]
    bias = rel_bias[:, jnp.clip(dist, -REL_CLIP, REL_CLIP) + REL_CLIP].astype(jnp.float32)
    key_chunk = jnp.arange(nc)[:, None] + k_off[None, :] // CHUNK
    valid = key_chunk >= 0
    scale = HEAD_DIM_A ** -0.5
    scores = jnp.einsum('bnqhd,bnkhd->bhnqk', qc, kb).astype(jnp.float32) * scale
    scores = scores + bias[None, :, None, :, :]
    scores = jnp.where(valid[None, None, :, None, :], scores, NEG_INF)
    p = jax.nn.softmax(scores, axis=-1).astype(v.dtype)
    out = jnp.einsum('bhnqk,bnkhd->bnqhd', p, vb)
    return out.reshape(b, s, D_A)


def spatial_gating(u, v, ln_g, ln_b, w_s, b_s):
    b, s, _ = v.shape
    nb = s // SGU_CHUNK
    vn = layernorm(v, ln_g, ln_b).reshape(b, nb, SGU_CHUNK, N_GROUPS_B, GROUP_DIM_B)
    tri = jnp.tril(jnp.ones((SGU_CHUNK, SGU_CHUNK), dtype=bool))
    w = jnp.where(tri[None], w_s, jnp.zeros_like(w_s))
    mixed = jnp.einsum('gts,bnsgc->bntgc', w, vn) + jnp.transpose(b_s)[:, :, None]
    return u * mixed.reshape(b, s, D_B)


def hybrid_layer(x, norm_g, w_in, b_gate, rel_bias, sgu_ln_g, sgu_ln_b, w_s, b_s, w_pa, w_pb, w_out):
    h = rmsnorm(x, norm_g)
    z = jnp.einsum('bsd,de->bse', h, w_in)
    idx = list(np.cumsum(SPLITS)[:-1])
    q, k, v, g_a, u_b, v_b, g_b, gate_a, gate_b = jnp.split(z, idx, axis=-1)
    y_a = chunked_rel_attention(q, k, v, rel_bias) * jax.nn.silu(g_a)
    y_b = spatial_gating(jax.nn.gelu(u_b), jax.nn.gelu(v_b), sgu_ln_g, sgu_ln_b, w_s, b_s) * jax.nn.silu(g_b)
    p_a = jnp.einsum('bse,ed->bsd', y_a, w_pa)
    p_b = jnp.einsum('bse,ed->bsd', y_b, w_pb)
    ga = jax.nn.sigmoid(gate_a + b_gate[:D_MODEL])
    gb = jax.nn.sigmoid(gate_b + b_gate[D_MODEL:])
    merged = ga * p_a + gb * p_b
    return x + jnp.einsum('bsd,de->bse', merged, w_out)


def _fwd_setup_inputs(seed: int = 0) -> dict:
    key = jax.random.key(seed)
    ks = jax.random.split(key, 16)
    f32 = jnp.float32
    nrm = lambda k, shape, s: jax.random.normal(k, shape, f32) * s
    return {
        'x': jax.random.normal(ks[0], (BATCH, SEQ, D_MODEL), f32),
        'norm_g': 1.0 + nrm(ks[1], (DEPTH, D_MODEL), 0.05),
        'w_in': nrm(ks[2], (DEPTH, D_MODEL, D_IN), D_MODEL ** -0.5),
        'b_gate': nrm(ks[3], (DEPTH, 2 * D_MODEL), 0.1),
        'rel_bias': nrm(ks[4], (DEPTH, N_HEADS_A, N_REL), 0.5),
        'sgu_ln_g': 1.0 + nrm(ks[5], (DEPTH, D_B), 0.05),
        'sgu_ln_b': nrm(ks[6], (DEPTH, D_B), 0.05),
        'w_s': nrm(ks[7], (DEPTH, N_GROUPS_B, SGU_CHUNK, SGU_CHUNK), SGU_CHUNK ** -0.5),
        'b_s': 1.0 + nrm(ks[8], (DEPTH, N_GROUPS_B, SGU_CHUNK), 0.1),
        'w_pa': nrm(ks[9], (DEPTH, D_A, D_MODEL), D_A ** -0.5),
        'w_pb': nrm(ks[10], (DEPTH, D_B, D_MODEL), D_B ** -0.5),
        'w_out': nrm(ks[11], (DEPTH, D_MODEL, D_MODEL), D_MODEL ** -0.5),
        'final_g': 1.0 + nrm(ks[12], (D_MODEL,), 0.05),
    }


def _fwd_reference(x, norm_g, w_in, b_gate, rel_bias, sgu_ln_g, sgu_ln_b, w_s, b_s, w_pa, w_pb, w_out, final_g):
    for l in range(DEPTH):
        x = hybrid_layer(x, norm_g[l], w_in[l], b_gate[l], rel_bias[l], sgu_ln_g[l], sgu_ln_b[l],
                         w_s[l], b_s[l], w_pa[l], w_pb[l], w_out[l])
    return rmsnorm(x, final_g)


import jax as _jax
import jax.numpy as _jnp

TWIN_FORMAT = 'train_step'
FWD_PARAMS = ['x', 'norm_g', 'w_in', 'b_gate', 'rel_bias', 'sgu_ln_g', 'sgu_ln_b', 'w_s', 'b_s', 'w_pa', 'w_pb', 'w_out', 'final_g']
TWIN_WEIGHTS = ['norm_g', 'w_in', 'b_gate', 'rel_bias', 'sgu_ln_g', 'sgu_ln_b', 'w_s', 'b_s', 'w_pa', 'w_pb', 'w_out', 'final_g']
TWIN_DIFF_INPUT = 'x'
TWIN_INPUTS = ['x', 'norm_g', 'w_in', 'b_gate', 'rel_bias', 'sgu_ln_g', 'sgu_ln_b', 'w_s', 'b_s', 'w_pa', 'w_pb', 'w_out', 'final_g', 'loss_target', 'm_norm_g', 'm_w_in', 'm_b_gate', 'm_rel_bias', 'm_sgu_ln_g', 'm_sgu_ln_b', 'm_w_s', 'm_b_s', 'm_w_pa', 'm_w_pb', 'm_w_out', 'm_final_g', 'v_norm_g', 'v_w_in', 'v_b_gate', 'v_rel_bias', 'v_sgu_ln_g', 'v_sgu_ln_b', 'v_w_s', 'v_b_s', 'v_w_pa', 'v_w_pb', 'v_w_out', 'v_final_g']
TWIN_OUTPUTS = ['loss', 'grad_x', 'grad_norm_g', 'grad_w_in', 'grad_b_gate', 'grad_rel_bias', 'grad_sgu_ln_g', 'grad_sgu_ln_b', 'grad_w_s', 'grad_b_s', 'grad_w_pa', 'grad_w_pb', 'grad_w_out', 'grad_final_g', 'delta_norm_g', 'delta_w_in', 'delta_b_gate', 'delta_rel_bias', 'delta_sgu_ln_g', 'delta_sgu_ln_b', 'delta_w_s', 'delta_b_s', 'delta_w_pa', 'delta_w_pb', 'delta_w_out', 'delta_final_g', 'new_m_norm_g', 'new_m_w_in', 'new_m_b_gate', 'new_m_rel_bias', 'new_m_sgu_ln_g', 'new_m_sgu_ln_b', 'new_m_w_s', 'new_m_b_s', 'new_m_w_pa', 'new_m_w_pb', 'new_m_w_out', 'new_m_final_g', 'new_v_norm_g', 'new_v_w_in', 'new_v_b_gate', 'new_v_rel_bias', 'new_v_sgu_ln_g', 'new_v_sgu_ln_b', 'new_v_w_s', 'new_v_b_s', 'new_v_w_pa', 'new_v_w_pb', 'new_v_w_out', 'new_v_final_g']
TWIN_LEAF_KINDS = {'loss': 'loss', 'grad_x': 'grad_x', 'grad_norm_g': 'grad_w', 'grad_w_in': 'grad_w', 'grad_b_gate': 'grad_w', 'grad_rel_bias': 'grad_w', 'grad_sgu_ln_g': 'grad_w', 'grad_sgu_ln_b': 'grad_w', 'grad_w_s': 'grad_w', 'grad_b_s': 'grad_w', 'grad_w_pa': 'grad_w', 'grad_w_pb': 'grad_w', 'grad_w_out': 'grad_w', 'grad_final_g': 'grad_w', 'delta_norm_g': 'delta_w', 'delta_w_in': 'delta_w', 'delta_b_gate': 'delta_w', 'delta_rel_bias': 'delta_w', 'delta_sgu_ln_g': 'delta_w', 'delta_sgu_ln_b': 'delta_w', 'delta_w_s': 'delta_w', 'delta_b_s': 'delta_w', 'delta_w_pa': 'delta_w', 'delta_w_pb': 'delta_w', 'delta_w_out': 'delta_w', 'delta_final_g': 'delta_w', 'new_m_norm_g': 'new_m', 'new_m_w_in': 'new_m', 'new_m_b_gate': 'new_m', 'new_m_rel_bias': 'new_m', 'new_m_sgu_ln_g': 'new_m', 'new_m_sgu_ln_b': 'new_m', 'new_m_w_s': 'new_m', 'new_m_b_s': 'new_m', 'new_m_w_pa': 'new_m', 'new_m_w_pb': 'new_m', 'new_m_w_out': 'new_m', 'new_m_final_g': 'new_m', 'new_v_norm_g': 'new_v', 'new_v_w_in': 'new_v', 'new_v_b_gate': 'new_v', 'new_v_rel_bias': 'new_v', 'new_v_sgu_ln_g': 'new_v', 'new_v_sgu_ln_b': 'new_v', 'new_v_w_s': 'new_v', 'new_v_b_s': 'new_v', 'new_v_w_pa': 'new_v', 'new_v_w_pb': 'new_v', 'new_v_w_out': 'new_v', 'new_v_final_g': 'new_v'}


def _forward(args):
    return _fwd_reference(*[args[k] for k in FWD_PARAMS])


def _output_shape():
    out = _jax.eval_shape(lambda: _forward(_fwd_setup_inputs(0)))
    return out.shape, out.dtype

N_MICROBATCH = 1
ADAM_LR = 0.001
ADAM_B1 = 0.9
ADAM_B2 = 0.999
ADAM_EPS = 1e-08
ADAM_WD = 0.01
ADAM_STEP = 10
PER_EXAMPLE_BATCH_AXIS = {'x': 0, 'loss_target': 0}
SHARED_INPUTS = []
_WEIGHT_DTYPES = {'norm_g': _jnp.float32, 'w_in': _jnp.float32, 'b_gate': _jnp.float32, 'rel_bias': _jnp.float32, 'sgu_ln_g': _jnp.float32, 'sgu_ln_b': _jnp.float32, 'w_s': _jnp.float32, 'b_s': _jnp.float32, 'w_pa': _jnp.float32, 'w_pb': _jnp.float32, 'w_out': _jnp.float32, 'final_g': _jnp.float32}
MOMENT_SCALE = {'norm_g': 7.720103e-02, 'w_in': 3.267088e-02, 'b_gate': 1.249485e-02, 'rel_bias': 5.703158e-03, 'sgu_ln_g': 3.663472e-02, 'sgu_ln_b': 3.826557e-02, 'w_s': 3.664324e-02, 'b_s': 5.439255e-02, 'w_pa': 9.180731e-03, 'w_pb': 4.510183e-02, 'w_out': 4.598485e-02, 'final_g': 3.208511e+01}


def _to_microbatches(a, axis):
    t = _jnp.moveaxis(a, axis, 0)
    t = t.reshape((N_MICROBATCH, t.shape[0] // N_MICROBATCH) + t.shape[1:])
    return _jnp.moveaxis(t, 1, axis + 1)


def setup_inputs(seed: int = 0) -> dict:
    inp = _fwd_setup_inputs(seed)
    key = _jax.random.fold_in(_jax.random.key(seed), 7919)
    shape, _ = _output_shape()
    out = dict(inp)
    out["loss_target"] = _jax.random.normal(_jax.random.fold_in(key, 0), shape, _jnp.float32)
    for i, name in enumerate(TWIN_WEIGHTS):
        w = inp[name].astype(_jnp.float32)
        if MOMENT_SCALE is None:
            s = _jnp.sqrt(_jnp.mean(_jnp.square(w)) + 1e-30)
        else:
            s = MOMENT_SCALE[name]
        km, kv = _jax.random.split(_jax.random.fold_in(key, i + 1))
        out[name] = w
        out["m_" + name] = s * _jax.random.normal(km, w.shape, _jnp.float32)
        out["v_" + name] = (s * s) * _jax.random.uniform(kv, w.shape, _jnp.float32, 0.5, 1.5)
    if N_MICROBATCH > 1:
        for name, axis in PER_EXAMPLE_BATCH_AXIS.items():
            out[name] = _to_microbatches(out[name], axis)
    return {'x': out['x'], 'norm_g': out['norm_g'], 'w_in': out['w_in'], 'b_gate': out['b_gate'], 'rel_bias': out['rel_bias'], 'sgu_ln_g': out['sgu_ln_g'], 'sgu_ln_b': out['sgu_ln_b'], 'w_s': out['w_s'], 'b_s': out['b_s'], 'w_pa': out['w_pa'], 'w_pb': out['w_pb'], 'w_out': out['w_out'], 'final_g': out['final_g'], 'loss_target': out['loss_target'], 'm_norm_g': out['m_norm_g'], 'm_w_in': out['m_w_in'], 'm_b_gate': out['m_b_gate'], 'm_rel_bias': out['m_rel_bias'], 'm_sgu_ln_g': out['m_sgu_ln_g'], 'm_sgu_ln_b': out['m_sgu_ln_b'], 'm_w_s': out['m_w_s'], 'm_b_s': out['m_b_s'], 'm_w_pa': out['m_w_pa'], 'm_w_pb': out['m_w_pb'], 'm_w_out': out['m_w_out'], 'm_final_g': out['m_final_g'], 'v_norm_g': out['v_norm_g'], 'v_w_in': out['v_w_in'], 'v_b_gate': out['v_b_gate'], 'v_rel_bias': out['v_rel_bias'], 'v_sgu_ln_g': out['v_sgu_ln_g'], 'v_sgu_ln_b': out['v_sgu_ln_b'], 'v_w_s': out['v_w_s'], 'v_b_s': out['v_b_s'], 'v_w_pa': out['v_w_pa'], 'v_w_pb': out['v_w_pb'], 'v_w_out': out['v_w_out'], 'v_final_g': out['v_final_g']}


def _loss(weights, diff, rest, loss_target):
    with _jax.named_scope("forward"):
        args = {**rest, TWIN_DIFF_INPUT: diff, **{k: w.astype(_WEIGHT_DTYPES[k]) for k, w in weights.items()}}
        y = _forward(args)
    with _jax.named_scope("loss_head"):
        err = _jnp.square(y.astype(_jnp.float32) - loss_target)
        return 0.5 * _jnp.sum(_jnp.mean(err, axis=-1)) if err.ndim else 0.5 * err


def _adamw(w, g, m, v):
    m = ADAM_B1 * m + (1.0 - ADAM_B1) * g
    v = ADAM_B2 * v + (1.0 - ADAM_B2) * _jnp.square(g)
    m_hat = m / (1.0 - ADAM_B1 ** ADAM_STEP)
    v_hat = v / (1.0 - ADAM_B2 ** ADAM_STEP)
    delta = -ADAM_LR * (m_hat / (_jnp.sqrt(v_hat) + ADAM_EPS) + ADAM_WD * w)
    return delta, m, v


def reference(x, norm_g, w_in, b_gate, rel_bias, sgu_ln_g, sgu_ln_b, w_s, b_s, w_pa, w_pb, w_out, final_g, loss_target, m_norm_g, m_w_in, m_b_gate, m_rel_bias, m_sgu_ln_g, m_sgu_ln_b, m_w_s, m_b_s, m_w_pa, m_w_pb, m_w_out, m_final_g, v_norm_g, v_w_in, v_b_gate, v_rel_bias, v_sgu_ln_g, v_sgu_ln_b, v_w_s, v_b_s, v_w_pa, v_w_pb, v_w_out, v_final_g):
    given = dict(x=x, norm_g=norm_g, w_in=w_in, b_gate=b_gate, rel_bias=rel_bias, sgu_ln_g=sgu_ln_g, sgu_ln_b=sgu_ln_b, w_s=w_s, b_s=b_s, w_pa=w_pa, w_pb=w_pb, w_out=w_out, final_g=final_g, loss_target=loss_target, m_norm_g=m_norm_g, m_w_in=m_w_in, m_b_gate=m_b_gate, m_rel_bias=m_rel_bias, m_sgu_ln_g=m_sgu_ln_g, m_sgu_ln_b=m_sgu_ln_b, m_w_s=m_w_s, m_b_s=m_b_s, m_w_pa=m_w_pa, m_w_pb=m_w_pb, m_w_out=m_w_out, m_final_g=m_final_g, v_norm_g=v_norm_g, v_w_in=v_w_in, v_b_gate=v_b_gate, v_rel_bias=v_rel_bias, v_sgu_ln_g=v_sgu_ln_g, v_sgu_ln_b=v_sgu_ln_b, v_w_s=v_w_s, v_b_s=v_b_s, v_w_pa=v_w_pa, v_w_pb=v_w_pb, v_w_out=v_w_out, v_final_g=v_final_g)
    weights = {n: given[n] for n in TWIN_WEIGHTS}
    shared = {n: given[n] for n in SHARED_INPUTS}
    per_example = {n: given[n] for n in ['x']}
    grad_fn = _jax.value_and_grad(_loss, argnums=(0, 1))

    def one_microbatch(ex, loss_target):
        ex = dict(ex)
        diff = ex.pop(TWIN_DIFF_INPUT)
        return grad_fn(weights, diff, {**shared, **ex}, loss_target)

    if N_MICROBATCH == 1:
        loss, (grad_w, grad_x) = one_microbatch(per_example, given["loss_target"])
    else:
        def body(carry, xs):
            loss_sum, grad_sum = carry
            l_k, (gw_k, gx_k) = one_microbatch(xs[0], xs[1])
            with _jax.named_scope("update"):
                return (loss_sum + l_k, _jax.tree.map(_jnp.add, grad_sum, gw_k)), gx_k

        init = (_jnp.zeros((), _jnp.float32), _jax.tree.map(_jnp.zeros_like, weights))
        (loss, grad_w), grad_x = _jax.lax.scan(body, init, (per_example, given["loss_target"]))
    with _jax.named_scope("update"):
        delta_w, new_m, new_v = {}, {}, {}
        for n in TWIN_WEIGHTS:
            delta_w[n], new_m[n], new_v[n] = _adamw(weights[n], grad_w[n], given["m_" + n], given["v_" + n])
    return (loss, grad_x, *[grad_w[n] for n in TWIN_WEIGHTS], *[delta_w[n] for n in TWIN_WEIGHTS],
            *[new_m[n] for n in TWIN_WEIGHTS], *[new_v[n] for n in TWIN_WEIGHTS])
```

```python
import functools
import math

import jax
import jax.numpy as jnp
from jax import lax
from jax.experimental import pallas as pl
from jax.experimental.pallas import tpu as pltpu

F32 = jnp.float32
BF = jnp.bfloat16
MESH = pl.DeviceIdType.MESH

D_MODEL = 1024
D_A = 512
D_B = 512
D_IN = 5632
N_HEADS = 8
HEAD_DIM = 64
CHUNK = 64
N_PREV = 8
SGU_CHUNK = 128
N_GROUPS = 4
N_REL = 257
EPS = 1e-6
NEG_INF = -1e30
SCALE = HEAD_DIM ** -0.5

QB = 2 * CHUNK
KB = (N_PREV + 2) * CHUNK
PADK = N_PREV * CHUNK
ROLL_W = 1024
KEEP = KB // QB - 1

ADAM_LR = 0.001
ADAM_B1 = 0.9
ADAM_B2 = 0.999
ADAM_EPS = 1e-08
ADAM_WD = 0.01
ADAM_STEP = 10
ADAM_C1 = 1.0 - ADAM_B1 ** ADAM_STEP
ADAM_C2 = 1.0 - ADAM_B2 ** ADAM_STEP

N_SHARD = 4
SHARD_IN = D_IN // N_SHARD
MIB = 1024 * 1024


def _params(vmem_mib, **kw):
    return pltpu.CompilerParams(vmem_limit_bytes=vmem_mib * MIB, **kw)


def _sigmoid(x):
    return 1.0 / (1.0 + jnp.exp(-x))


def _silu_and_grad(x):
    s = _sigmoid(x)
    return x * s, s * (1.0 + x * (1.0 - s))


_GELU_C = math.sqrt(2.0 / math.pi)
_GELU_A = 0.044715


def _gelu_and_grad(x):
    x2 = x * x
    t = jnp.tanh(_GELU_C * (x + _GELU_A * (x2 * x)))
    cdf = 0.5 * (1.0 + t)
    grad = cdf + 0.5 * x * (1.0 - t * t) * (_GELU_C * (1.0 + 3.0 * _GELU_A * x2))
    return x * cdf, grad


def _dot(a, b):
    return jnp.dot(a, b, preferred_element_type=F32)


def _dot_nt(a, b):
    return lax.dot_general(a, b, (((1,), (1,)), ((), ())), preferred_element_type=F32)


def _dot_tn(a, b):
    return lax.dot_general(a, b, (((0,), (0,)), ((), ())), preferred_element_type=F32)


def _mo(v, m):
    return v if isinstance(v, int) else pl.multiple_of(v, m)


def _unit_in(ref, s, p):
    return ref.at[pl.ds(_mo(p * 512, 512), 512), pl.ds(_mo(s * SHARD_IN, 128), SHARD_IN)]


def _unit_p(ref, s, p):
    return ref.at[pl.ds(_mo(p * 256, 256), 256), pl.ds(_mo(s * 256, 128), 256)]


def _unit_out(ref, s, p):
    return ref.at[pl.ds(_mo(s * 256 + p * 128, 128), 128), :]


_UNITS = (_unit_in, _unit_p, _unit_p, _unit_out)
_HALF_ROWS = (512, 256, 256, 128)
_UNIT_SHAPES = ((512, SHARD_IN), (256, 256), (256, 256), (128, D_MODEL))
_FULL_SHAPES = ((D_MODEL, D_IN), (D_A, D_MODEL), (D_B, D_MODEL), (D_MODEL, D_MODEL))
_SHARD_SHAPES = ((D_MODEL, SHARD_IN), (D_A, 256), (D_B, 256), (256, D_MODEL))


def _mesh_pos():
    x, y, c = lax.axis_index("x"), lax.axis_index("y"), lax.axis_index("c")
    chips = [(1 - x, y), (x, 1 - y), (1 - x, 1 - y)]
    return x, y, c, chips


def _ag_weights(w_in, w_pa, w_pb, w_out):
    def body(i0, i1, i2, i3, o0, o1, o2, o3, s0, s1, s2, s3, send_sems, recv_sems, local_sems):
        ins, outs, stage = (i0, i1, i2, i3), (o0, o1, o2, o3), (s0, s1, s2, s3)
        x, y, c, chips = _mesh_pos()
        s_me = 2 * x + y
        sibling = (x, y, 1 - c)
        for w in range(4):
            stage[w][...] = ins[w][...].astype(BF)

        def half(w, p):
            rows = _HALF_ROWS[w]
            return stage[w].at[pl.ds(_mo(p * rows, rows), rows), :]

        local = []
        for w in range(4):
            for p in range(2):
                cp = pltpu.make_async_copy(half(w, p), _UNITS[w](outs[w], s_me, p), local_sems.at[w, p])
                cp.start()
                local.append(cp)

        def rcopy(w, k, src, dst, to):
            return pltpu.make_async_remote_copy(src_ref=src, dst_ref=dst, send_sem=send_sems.at[w, k],
                                                recv_sem=recv_sems.at[w, k], device_id=to, device_id_type=MESH)

        sends = []
        for j, (cx, cy) in enumerate(chips):
            for w in range(4):
                cp = rcopy(w, j, half(w, c), _UNITS[w](outs[w], s_me, c), (cx, cy, c))
                cp.start()
                sends.append(cp)
        for j, (cx, cy) in enumerate(chips):
            s_j = 2 * cx + cy
            for w in range(4):
                landed = _UNITS[w](outs[w], s_j, c)
                rcopy(w, j, landed, landed, (cx, cy, c)).wait_recv()
                cp = rcopy(w, 3 + j, landed, landed, sibling)
                cp.start()
                sends.append(cp)
        for j, (cx, cy) in enumerate(chips):
            s_j = 2 * cx + cy
            for w in range(4):
                other = _UNITS[w](outs[w], s_j, 1 - c)
                rcopy(w, 3 + j, other, other, sibling).wait_recv()
        for cp in sends:
            cp.wait_send()
        for cp in local:
            cp.wait()

    vm = pl.BlockSpec(memory_space=pltpu.VMEM)
    hbm = pl.BlockSpec(memory_space=pl.ANY)
    return pl.pallas_call(
        body, name="ag_weights",
        out_shape=tuple(jax.ShapeDtypeStruct(s, BF) for s in _FULL_SHAPES),
        in_specs=[vm] * 4, out_specs=[hbm] * 4,
        scratch_shapes=[pltpu.VMEM(s, BF) for s in _SHARD_SHAPES]
        + [pltpu.SemaphoreType.DMA((4, 6)), pltpu.SemaphoreType.DMA((4, 6)), pltpu.SemaphoreType.DMA((4, 2))],
        compiler_params=_params(40),
    )(w_in, w_pa, w_pb, w_out)


def _inproj_fwd(x, norm_g, w_in_bf, tm=256):
    S = x.shape[0]

    def body(x_ref, g_ref, w_ref, ht_ref, qkv_ref, zr_ref):
        xv = x_ref[...]
        r = lax.rsqrt(jnp.mean(xv * xv, axis=-1, keepdims=True) + EPS)
        hf = (xv * r) * g_ref[...]
        ht_ref[...] = hf.T.astype(BF)
        h = hf.astype(BF)
        for j in range(D_IN // 512):
            z = _dot(h, w_ref[:, j * 512:(j + 1) * 512])
            if j < 3:
                qkv_ref[:, j * 512:(j + 1) * 512] = z.astype(BF)
            else:
                zr_ref[:, (j - 3) * 512:(j - 2) * 512] = z

    return pl.pallas_call(
        body, name="inproj_fwd", grid=(S // tm,),
        out_shape=(jax.ShapeDtypeStruct((D_MODEL, S), BF), jax.ShapeDtypeStruct((S, 3 * D_A), BF),
                   jax.ShapeDtypeStruct((S, D_IN - 3 * D_A), F32)),
        in_specs=[pl.BlockSpec((tm, D_MODEL), lambda i: (i, 0)),
                  pl.BlockSpec((1, D_MODEL), lambda i: (0, 0)),
                  pl.BlockSpec((D_MODEL, D_IN), lambda i: (0, 0))],
        out_specs=[pl.BlockSpec((D_MODEL, tm), lambda i: (0, i)),
                   pl.BlockSpec((tm, 3 * D_A), lambda i: (i, 0)),
                   pl.BlockSpec((tm, D_IN - 3 * D_A), lambda i: (i, 0))],
        compiler_params=_params(52, dimension_semantics=("arbitrary",)),
    )(x, norm_g, w_in_bf)


def _skew_table(gp_row):
    row = lax.broadcasted_iota(jnp.int32, (QB, ROLL_W), 0)
    t = jnp.broadcast_to(gp_row, (QB, ROLL_W))
    for b in range(7):
        t = jnp.where(((row >> b) & 1) == 1, pltpu.roll(t, 1 << b, axis=1), t)
    return t


def _unskew_sum(d):
    row = lax.broadcasted_iota(jnp.int32, (QB, ROLL_W), 0)
    for b in range(7):
        d = jnp.where(((row >> b) & 1) == 1, pltpu.roll(d, ROLL_W - (1 << b), axis=1), d)
    return jnp.sum(d, axis=0, keepdims=True)


def _band_mask(i):
    a = lax.broadcasted_iota(jnp.int32, (QB, KB), 0) // CHUNK
    col = lax.broadcasted_iota(jnp.int32, (QB, KB), 1)
    b = col // CHUNK
    return (b >= a) & (b <= a + N_PREV) & (col >= PADK - i * QB)


def _load_kv(qkv_hbm, k_scr, v_scr, sems, S):
    k_scr[0:PADK, :] = jnp.zeros((PADK, D_A), BF)
    v_scr[0:PADK, :] = jnp.zeros((PADK, D_A), BF)
    ck = pltpu.make_async_copy(qkv_hbm.at[:, pl.ds(D_A, D_A)], k_scr.at[pl.ds(PADK, S), :], sems.at[0])
    cv = pltpu.make_async_copy(qkv_hbm.at[:, pl.ds(2 * D_A, D_A)], v_scr.at[pl.ds(PADK, S), :], sems.at[1])
    ck.start()
    cv.start()
    return ck, cv


def _probs(qh, kh, bias_h, mask):
    s = _dot_nt(qh, kh) * SCALE + bias_h
    s = jnp.where(mask, s, NEG_INF)
    m = jnp.max(s, axis=-1, keepdims=True)
    e = jnp.exp(s - m)
    return e / jnp.sum(e, axis=-1, keepdims=True)


def _attn_fwd(qkv, gp):
    S = qkv.shape[0]

    def body(q_ref, qkv_hbm, gp_ref, o_ref, k_scr, v_scr, bias_scr, sems):
        i = pl.program_id(0)

        @pl.when(i == 0)
        def _():
            ck, cv = _load_kv(qkv_hbm, k_scr, v_scr, sems, S)
            for h in range(N_HEADS):
                bias_scr[h] = _skew_table(gp_ref[h:h + 1, :])[:, :KB]
            ck.wait()
            cv.wait()

        start = pl.multiple_of(i * QB, QB)
        kb = k_scr[pl.ds(start, KB), :]
        vb = v_scr[pl.ds(start, KB), :]
        q = q_ref[...]
        mask = _band_mask(i)
        for h in range(N_HEADS):
            hs = slice(h * HEAD_DIM, (h + 1) * HEAD_DIM)
            p = _probs(q[:, hs], kb[:, hs], bias_scr[h], mask)
            o_ref[:, hs] = _dot(p.astype(BF), vb[:, hs])

    return pl.pallas_call(
        body, name="attn_fwd", grid=(S // QB,),
        out_shape=jax.ShapeDtypeStruct((S, D_A), F32),
        in_specs=[pl.BlockSpec((QB, D_A), lambda i: (i, 0)),
                  pl.BlockSpec(memory_space=pl.ANY),
                  pl.BlockSpec((N_HEADS, ROLL_W), lambda i: (0, 0))],
        out_specs=pl.BlockSpec((QB, D_A), lambda i: (i, 0)),
        scratch_shapes=[pltpu.VMEM((S + PADK, D_A), BF), pltpu.VMEM((S + PADK, D_A), BF),
                        pltpu.VMEM((N_HEADS, QB, KB), F32), pltpu.SemaphoreType.DMA((2,))],
        compiler_params=_params(40, dimension_semantics=("arbitrary",)),
    )(qkv, qkv, gp)


def _attn_bwd(qkv, d_att, gp):
    S = qkv.shape[0]
    nq = S // QB

    def body(q_ref, do_ref, qkv_hbm, gp_ref, dq_ref, dk_ref, dv_ref, dgp_ref,
             k_scr, v_scr, bias_scr, dk_acc, dv_acc, dbias_acc, pad_scr, sems):
        i = pl.program_id(0)

        @pl.when(i == 0)
        def _():
            ck, cv = _load_kv(qkv_hbm, k_scr, v_scr, sems, S)
            for h in range(N_HEADS):
                bias_scr[h] = _skew_table(gp_ref[h:h + 1, :])[:, :KB]
            dk_acc[...] = jnp.zeros_like(dk_acc)
            dv_acc[...] = jnp.zeros_like(dv_acc)
            dbias_acc[...] = jnp.zeros_like(dbias_acc)
            ck.wait()
            cv.wait()

        @pl.when(i < nq)
        def _():
            start = pl.multiple_of(i * QB, QB)
            kb = k_scr[pl.ds(start, KB), :]
            vb = v_scr[pl.ds(start, KB), :]
            q = q_ref[...]
            do = do_ref[...]
            mask = _band_mask(i)
            for h in range(N_HEADS):
                hs = slice(h * HEAD_DIM, (h + 1) * HEAD_DIM)
                p = _probs(q[:, hs], kb[:, hs], bias_scr[h], mask)
                dp = _dot_nt(do[:, hs], vb[:, hs])
                ds = p * (dp - jnp.sum(dp * p, axis=-1, keepdims=True))
                dbias_acc[h] += ds
                dsb = (ds * SCALE).astype(BF)
                dq_ref[:, hs] = _dot(dsb, kb[:, hs]).astype(BF)
                dk_acc[:, hs] += _dot_tn(dsb, q[:, hs])
                dv_acc[:, hs] += _dot_tn(p.astype(BF), do[:, hs])

        dk_ref[...] = dk_acc[0:QB, :].astype(BF)
        dv_ref[...] = dv_acc[0:QB, :].astype(BF)
        dk_acc[0:KB - QB, :] = dk_acc[QB:KB, :]
        dv_acc[0:KB - QB, :] = dv_acc[QB:KB, :]
        dk_acc[KB - QB:KB, :] = jnp.zeros((QB, D_A), F32)
        dv_acc[KB - QB:KB, :] = jnp.zeros((QB, D_A), F32)

        @pl.when(i == nq + KEEP - 1)
        def _():
            lane = lax.broadcasted_iota(jnp.int32, (1, ROLL_W), 1)
            hi = (lane < 384) | (lane >= 832)
            lo = (lane > 640) & (lane < 832)
            pad_scr[...] = jnp.zeros_like(pad_scr)
            for h in range(N_HEADS):
                pad_scr[:, 0:KB] = dbias_acc[h]
                g = _unskew_sum(pad_scr[...])
                s_hi = jnp.sum(jnp.where(hi, g, 0.0), axis=-1, keepdims=True)
                s_lo = jnp.sum(jnp.where(lo, g, 0.0), axis=-1, keepdims=True)
                g = jnp.where(lane == 384, g + s_hi, g)
                g = jnp.where(lane == 640, g + s_lo, g)
                dgp_ref[h:h + 1, :] = g

    last = nq - 1
    return pl.pallas_call(
        body, name="attn_bwd", grid=(nq + KEEP,),
        out_shape=(jax.ShapeDtypeStruct((S, D_A), BF), jax.ShapeDtypeStruct((S, D_A), BF),
                   jax.ShapeDtypeStruct((S, D_A), BF), jax.ShapeDtypeStruct((N_HEADS, ROLL_W), F32)),
        in_specs=[pl.BlockSpec((QB, D_A), lambda i: (jnp.minimum(i, last), 0)),
                  pl.BlockSpec((QB, D_A), lambda i: (jnp.minimum(i, last), 0)),
                  pl.BlockSpec(memory_space=pl.ANY),
                  pl.BlockSpec((N_HEADS, ROLL_W), lambda i: (0, 0))],
        out_specs=[pl.BlockSpec((QB, D_A), lambda i: (jnp.minimum(i, last), 0)),
                   pl.BlockSpec((QB, D_A), lambda i: (jnp.maximum(i - KEEP, 0), 0)),
                   pl.BlockSpec((QB, D_A), lambda i: (jnp.maximum(i - KEEP, 0), 0)),
                   pl.BlockSpec((N_HEADS, ROLL_W), lambda i: (0, 0))],
        scratch_shapes=[pltpu.VMEM((S + PADK, D_A), BF), pltpu.VMEM((S + PADK, D_A), BF),
                        pltpu.VMEM((N_HEADS, QB, KB), F32),
                        pltpu.VMEM((KB, D_A), F32), pltpu.VMEM((KB, D_A), F32),
                        pltpu.VMEM((N_HEADS, QB, KB), F32), pltpu.VMEM((QB, ROLL_W), F32),
                        pltpu.SemaphoreType.DMA((2,))],
        compiler_params=_params(48, dimension_semantics=("arbitrary",)),
    )(qkv, d_att, qkv, gp)


def _sgu_core(ub, vb, lg, lb):
    u, du = _gelu_and_grad(ub)
    v, dv = _gelu_and_grad(vb)
    mu = jnp.mean(v, axis=-1, keepdims=True)
    vc = v - mu
    rstd = lax.rsqrt(jnp.mean(vc * vc, axis=-1, keepdims=True) + EPS)
    xh = vc * rstd
    vn = xh * lg + lb
    return u, du, dv, rstd, xh, vn


def _tri():
    r = lax.broadcasted_iota(jnp.int32, (SGU_CHUNK, SGU_CHUNK), 0)
    c = lax.broadcasted_iota(jnp.int32, (SGU_CHUNK, SGU_CHUNK), 1)
    return r >= c


def _sgu_fwd(zrest, ln_g, ln_b, w_s, b_s_t, tm=512):
    S = zrest.shape[0]

    def body(ub_ref, vb_ref, lg_ref, lb_ref, ws_ref, bst_ref, sg_ref):
        u, _, _, _, _, vn = _sgu_core(ub_ref[...], vb_ref[...], lg_ref[...], lb_ref[...])
        vnb = vn.astype(BF)
        tri = _tri()
        for g in range(N_GROUPS):
            cs = slice(g * 128, (g + 1) * 128)
            wt = jnp.where(tri, ws_ref[g], 0.0).astype(BF)
            bcol = bst_ref[:, g:g + 1]
            for n in range(tm // SGU_CHUNK):
                rs = slice(n * SGU_CHUNK, (n + 1) * SGU_CHUNK)
                mixed = _dot(wt, vnb[rs, cs]) + bcol
                sg_ref[rs, cs] = u[rs, cs] * mixed

    return pl.pallas_call(
        body, name="sgu_fwd", grid=(S // tm,),
        out_shape=jax.ShapeDtypeStruct((S, D_B), F32),
        in_specs=[pl.BlockSpec((tm, 512), lambda i: (i, 1)),
                  pl.BlockSpec((tm, 512), lambda i: (i, 2)),
                  pl.BlockSpec((1, D_B), lambda i: (0, 0)),
                  pl.BlockSpec((1, D_B), lambda i: (0, 0)),
                  pl.BlockSpec((N_GROUPS, 128, 128), lambda i: (0, 0, 0)),
                  pl.BlockSpec((128, N_GROUPS), lambda i: (0, 0))],
        out_specs=pl.BlockSpec((tm, D_B), lambda i: (i, 0)),
        compiler_params=_params(32, dimension_semantics=("arbitrary",)),
    )(zrest, zrest, ln_g, ln_b, w_s, b_s_t)


def _sgu_bwd(zrest, d_sg, ln_g, ln_b, w_s, b_s_t, tm=256):
    S = zrest.shape[0]
    nt = S // tm

    def body(ub_ref, vb_ref, dsg_ref, lg_ref, lb_ref, ws_ref, bst_ref,
             dzs_ref, gws_ref, gbs_ref, glg_ref, glb_ref, dvn_scr, bs_acc):
        i = pl.program_id(0)

        @pl.when(i == 0)
        def _():
            gws_ref[...] = jnp.zeros_like(gws_ref)
            glg_ref[...] = jnp.zeros_like(glg_ref)
            glb_ref[...] = jnp.zeros_like(glb_ref)
            bs_acc[...] = jnp.zeros_like(bs_acc)

        ub = ub_ref[...]
        u, du, dv, rstd, xh, vn = _sgu_core(ub, vb_ref[...], lg_ref[...], lb_ref[...])
        vnb = vn.astype(BF)
        dsg = dsg_ref[...]
        tri = _tri()
        for g in range(N_GROUPS):
            cs = slice(g * 128, (g + 1) * 128)
            wtf = jnp.where(tri, ws_ref[g], 0.0)
            wt = wtf.astype(BF)
            wtt = wtf.T.astype(BF)
            bcol = bst_ref[:, g:g + 1]
            for n in range(tm // SGU_CHUNK):
                rs = slice(n * SGU_CHUNK, (n + 1) * SGU_CHUNK)
                mixed = _dot(wt, vnb[rs, cs]) + bcol
                dzs_ref[rs, cs] = (dsg[rs, cs] * mixed * du[rs, cs]).astype(BF)
                dmix = dsg[rs, cs] * u[rs, cs]
                bs_acc[:, cs] += dmix
                dmb = dmix.astype(BF)
                gws_ref[g] += _dot_nt(dmb, vnb[rs, cs])
                dvn_scr[rs, cs] = _dot(wtt, dmb)
        dvn = dvn_scr[...]
        glg_ref[...] += jnp.sum(dvn * xh, axis=0, keepdims=True)
        glb_ref[...] += jnp.sum(dvn, axis=0, keepdims=True)
        dxh = dvn * lg_ref[...]
        dvv = rstd * (dxh - jnp.mean(dxh, axis=-1, keepdims=True)
                      - xh * jnp.mean(dxh * xh, axis=-1, keepdims=True))
        dzs_ref[:, D_B:2 * D_B] = (dvv * dv).astype(BF)

        @pl.when(i == nt - 1)
        def _():
            lane = lax.broadcasted_iota(jnp.int32, (SGU_CHUNK, 128), 1)
            out = jnp.zeros((SGU_CHUNK, 128), F32)
            for g in range(N_GROUPS):
                gws_ref[g] = jnp.where(tri, gws_ref[g], 0.0)
                col = jnp.sum(bs_acc[:, g * 128:(g + 1) * 128], axis=-1, keepdims=True)
                out = jnp.where(lane == g, col, out)
            gbs_ref[...] = out

    const2 = lambda i: (0, 0)
    return pl.pallas_call(
        body, name="sgu_bwd", grid=(nt,),
        out_shape=(jax.ShapeDtypeStruct((S, 2 * D_B), BF),
                   jax.ShapeDtypeStruct((N_GROUPS, 128, 128), F32),
                   jax.ShapeDtypeStruct((SGU_CHUNK, 128), F32),
                   jax.ShapeDtypeStruct((1, D_B), F32), jax.ShapeDtypeStruct((1, D_B), F32)),
        in_specs=[pl.BlockSpec((tm, 512), lambda i: (i, 1)),
                  pl.BlockSpec((tm, 512), lambda i: (i, 2)),
                  pl.BlockSpec((tm, D_B), lambda i: (i, 0)),
                  pl.BlockSpec((1, D_B), const2), pl.BlockSpec((1, D_B), const2),
                  pl.BlockSpec((N_GROUPS, 128, 128), lambda i: (0, 0, 0)),
                  pl.BlockSpec((128, N_GROUPS), const2)],
        out_specs=[pl.BlockSpec((tm, 2 * D_B), lambda i: (i, 0)),
                   pl.BlockSpec((N_GROUPS, 128, 128), lambda i: (0, 0, 0)),
                   pl.BlockSpec((SGU_CHUNK, 128), const2),
                   pl.BlockSpec((1, D_B), const2), pl.BlockSpec((1, D_B), const2)],
        scratch_shapes=[pltpu.VMEM((tm, D_B), F32), pltpu.VMEM((SGU_CHUNK, D_B), F32)],
        compiler_params=_params(32, dimension_semantics=("arbitrary",)),
    )(zrest, zrest, d_sg, ln_g, ln_b, w_s, b_s_t)


def _tail(att, sg, zrest, x, target, w_pa, w_pb, w_out, b_gate, final_g, tm=256):
    S = x.shape[0]
    nt = S // tm

    def body(att_ref, sg_ref, ga_ref, gb_ref, gta_ref, gtb_ref, x_ref, t_ref,
             wpa_ref, wpb_ref, wout_ref, bg_ref, fg_ref,
             dout_ref, datt_ref, dsg_ref, dzt_ref, gwout_hbm, gwpa_hbm, gwpb_hbm,
             gbg_ref, gfg_ref, loss_ref, acc_out, acc_pa, acc_pb, sems):
        i = pl.program_id(0)

        @pl.when(i == 0)
        def _():
            acc_out[...] = jnp.zeros_like(acc_out)
            acc_pa[...] = jnp.zeros_like(acc_pa)
            acc_pb[...] = jnp.zeros_like(acc_pb)
            gbg_ref[...] = jnp.zeros_like(gbg_ref)
            gfg_ref[...] = jnp.zeros_like(gfg_ref)
            loss_ref[...] = jnp.zeros_like(loss_ref)

        att = att_ref[...]
        sg = sg_ref[...]
        sa, dsa = _silu_and_grad(ga_ref[...])
        sb, dsb = _silu_and_grad(gb_ref[...])
        ya = (att * sa).astype(BF)
        yb = (sg * sb).astype(BF)
        pa = _dot(ya, wpa_ref[...])
        pb = _dot(yb, wpb_ref[...])
        ga = _sigmoid(gta_ref[...] + bg_ref[:, 0:D_MODEL])
        gb = _sigmoid(gtb_ref[...] + bg_ref[:, D_MODEL:2 * D_MODEL])
        merged = (ga * pa + gb * pb).astype(BF)
        out = x_ref[...] + _dot(merged, wout_ref[...])
        r2 = lax.rsqrt(jnp.mean(out * out, axis=-1, keepdims=True) + EPS)
        nrm = out * r2
        fg = fg_ref[...]
        err = nrm * fg - t_ref[...]
        loss_ref[...] += 0.5 * jnp.sum(jnp.mean(err * err, axis=-1, keepdims=True))
        dy = err * (1.0 / D_MODEL)
        gfg_ref[...] += jnp.sum(dy * nrm, axis=0, keepdims=True)
        dn = dy * fg
        d_out = r2 * (dn - nrm * jnp.mean(dn * nrm, axis=-1, keepdims=True))
        dout_ref[...] = d_out
        d_outb = d_out.astype(BF)
        acc_out[...] += _dot_tn(merged, d_outb)
        dm = _dot_nt(d_outb, wout_ref[...])
        d_pa = (dm * ga).astype(BF)
        d_pb = (dm * gb).astype(BF)
        d_gta = dm * pa * (ga * (1.0 - ga))
        d_gtb = dm * pb * (gb * (1.0 - gb))
        gbg_ref[:, 0:D_MODEL] += jnp.sum(d_gta, axis=0, keepdims=True)
        gbg_ref[:, D_MODEL:2 * D_MODEL] += jnp.sum(d_gtb, axis=0, keepdims=True)
        dzt_ref[:, 2 * D_A:2 * D_A + D_MODEL] = d_gta.astype(BF)
        dzt_ref[:, 2 * D_A + D_MODEL:] = d_gtb.astype(BF)
        acc_pa[...] += _dot_tn(ya, d_pa)
        acc_pb[...] += _dot_tn(yb, d_pb)
        d_ya = _dot_nt(d_pa, wpa_ref[...])
        d_yb = _dot_nt(d_pb, wpb_ref[...])
        datt_ref[...] = (d_ya * sa).astype(BF)
        dzt_ref[:, 0:D_A] = (d_ya * att * dsa).astype(BF)
        dsg_ref[...] = d_yb * sb
        dzt_ref[:, D_A:2 * D_A] = (d_yb * sg * dsb).astype(BF)

        @pl.when(i == nt - 1)
        def _():
            cps = [pltpu.make_async_copy(acc_out, gwout_hbm, sems.at[0]),
                   pltpu.make_async_copy(acc_pa, gwpa_hbm, sems.at[1]),
                   pltpu.make_async_copy(acc_pb, gwpb_hbm, sems.at[2])]
            for cp in cps:
                cp.start()
            for cp in cps:
                cp.wait()

    c2 = lambda i: (0, 0)
    hbm = pl.BlockSpec(memory_space=pl.ANY)
    return pl.pallas_call(
        body, name="tail", grid=(nt,),
        out_shape=(jax.ShapeDtypeStruct((S, D_MODEL), F32), jax.ShapeDtypeStruct((S, D_A), BF),
                   jax.ShapeDtypeStruct((S, D_B), F32), jax.ShapeDtypeStruct((S, 3072), BF),
                   jax.ShapeDtypeStruct((D_MODEL, D_MODEL), F32), jax.ShapeDtypeStruct((D_A, D_MODEL), F32),
                   jax.ShapeDtypeStruct((D_B, D_MODEL), F32),
                   jax.ShapeDtypeStruct((1, 2 * D_MODEL), F32), jax.ShapeDtypeStruct((1, D_MODEL), F32),
                   jax.ShapeDtypeStruct((1, 128), F32)),
        in_specs=[pl.BlockSpec((tm, D_A), lambda i: (i, 0)),
                  pl.BlockSpec((tm, D_B), lambda i: (i, 0)),
                  pl.BlockSpec((tm, 512), lambda i: (i, 0)),
                  pl.BlockSpec((tm, 512), lambda i: (i, 3)),
                  pl.BlockSpec((tm, D_MODEL), lambda i: (i, 2)),
                  pl.BlockSpec((tm, D_MODEL), lambda i: (i, 3)),
                  pl.BlockSpec((tm, D_MODEL), lambda i: (i, 0)),
                  pl.BlockSpec((tm, D_MODEL), lambda i: (i, 0)),
                  pl.BlockSpec((D_A, D_MODEL), c2), pl.BlockSpec((D_B, D_MODEL), c2),
                  pl.BlockSpec((D_MODEL, D_MODEL), c2),
                  pl.BlockSpec((1, 2 * D_MODEL), c2), pl.BlockSpec((1, D_MODEL), c2)],
        out_specs=[pl.BlockSpec((tm, D_MODEL), lambda i: (i, 0)),
                   pl.BlockSpec((tm, D_A), lambda i: (i, 0)),
                   pl.BlockSpec((tm, D_B), lambda i: (i, 0)),
                   pl.BlockSpec((tm, 3072), lambda i: (i, 0)),
                   hbm, hbm, hbm,
                   pl.BlockSpec((1, 2 * D_MODEL), c2), pl.BlockSpec((1, D_MODEL), c2),
                   pl.BlockSpec((1, 128), c2)],
        scratch_shapes=[pltpu.VMEM((D_MODEL, D_MODEL), F32), pltpu.VMEM((D_A, D_MODEL), F32),
                        pltpu.VMEM((D_B, D_MODEL), F32), pltpu.SemaphoreType.DMA((3,))],
        compiler_params=_params(56, dimension_semantics=("arbitrary",)),
    )(att, sg, zrest, zrest, zrest, zrest, x, target, w_pa, w_pb, w_out, b_gate, final_g)


_DZ_MAP = ((0, 0), (1, 0), (2, 0), (3, 0), (4, 0), (4, 1), (3, 1), (3, 2), (3, 3), (3, 4), (3, 5))


def _dh_gradx(dq, dk, dv, dzt, dzs, w_in_bf, x, norm_g, d_out, tm=256):
    S = x.shape[0]

    def body(dq_ref, dk_ref, dv_ref, dzt_ref, dzs_ref, w_ref, x_ref, g_ref, dout_ref, gx_ref, gn_ref):
        i = pl.program_id(0)

        @pl.when(i == 0)
        def _():
            gn_ref[...] = jnp.zeros_like(gn_ref)

        pieces = (dq_ref, dk_ref, dv_ref, dzt_ref, dzs_ref)
        dh = jnp.zeros((tm, D_MODEL), F32)
        for j, (pc, blk) in enumerate(_DZ_MAP):
            dh += _dot_nt(pieces[pc][:, blk * 512:(blk + 1) * 512], w_ref[:, j * 512:(j + 1) * 512])
        xv = x_ref[...]
        r = lax.rsqrt(jnp.mean(xv * xv, axis=-1, keepdims=True) + EPS)
        nrm = xv * r
        gn_ref[...] += jnp.sum(dh * nrm, axis=0, keepdims=True)
        dn = dh * g_ref[...]
        gx_ref[...] = r * (dn - nrm * jnp.mean(dn * nrm, axis=-1, keepdims=True)) + dout_ref[...]

    row = lambda w: pl.BlockSpec((tm, w), lambda i: (i, 0))
    c2 = lambda i: (0, 0)
    return pl.pallas_call(
        body, name="dh_gradx", grid=(S // tm,),
        out_shape=(jax.ShapeDtypeStruct((S, D_MODEL), F32), jax.ShapeDtypeStruct((1, D_MODEL), F32)),
        in_specs=[row(512), row(512), row(512), row(3072), row(1024),
                  pl.BlockSpec((D_MODEL, D_IN), c2), row(D_MODEL), pl.BlockSpec((1, D_MODEL), c2), row(D_MODEL)],
        out_specs=[row(D_MODEL), pl.BlockSpec((1, D_MODEL), c2)],
        compiler_params=_params(48, dimension_semantics=("arbitrary",)),
    )(dq, dk, dv, dzt, dzs, w_in_bf, x, norm_g, d_out)


def _gw_in(ht, dq, dk, dv, dzt, dzs, tk=1024):
    S = ht.shape[1]
    nk = S // tk

    def body(ht_ref, dq_ref, dk_ref, dv_ref, dzt_ref, dzs_ref, o_ref):
        j = pl.program_id(0)
        t = pl.program_id(1)
        pieces = (dq_ref, dk_ref, dv_ref, dzt_ref, dzs_ref)

        @pl.when(t == 0)
        def _():
            o_ref[...] = jnp.zeros_like(o_ref)

        for pc in range(5):
            hit = functools.reduce(jnp.logical_or, [j == jj for jj, (p, _) in enumerate(_DZ_MAP) if p == pc])

            @pl.when(hit)
            def _(pc=pc):
                o_ref[...] += _dot(ht_ref[...], pieces[pc][...])

    def piece_spec(pc):
        blocks = [jj for jj, (p, _) in enumerate(_DZ_MAP) if p == pc]
        table = [0] * len(_DZ_MAP)
        for jj, (p, blk) in enumerate(_DZ_MAP):
            table[jj] = blk if p == pc else None
        cur = _DZ_MAP[blocks[0]][1]
        filled = []
        for v in table:
            cur = cur if v is None else v
            filled.append(cur)

        def index_map(j, t):
            blk = jnp.int32(filled[0])
            for jj in range(1, len(filled)):
                blk = jnp.where(j >= jj, jnp.int32(filled[jj]), blk)
            return (t, blk)

        return pl.BlockSpec((tk, 512), index_map)

    return pl.pallas_call(
        body, name="gw_in", grid=(len(_DZ_MAP), nk),
        out_shape=jax.ShapeDtypeStruct((D_MODEL, D_IN), F32),
        in_specs=[pl.BlockSpec((D_MODEL, tk), lambda j, t: (0, t))] + [piece_spec(pc) for pc in range(5)],
        out_specs=pl.BlockSpec((D_MODEL, 512), lambda j, t: (0, j)),
        compiler_params=_params(40, dimension_semantics=("arbitrary", "arbitrary")),
    )(ht, dq, dk, dv, dzt, dzs)


def _grad_xchg1(grads):
    def body(g0, g1, g2, g3, r0, r1, r2, r3, send_sems, recv_sems):
        gs, rs = (g0, g1, g2, g3), (r0, r1, r2, r3)
        x, y, c, _ = _mesh_pos()
        sibling = (x, y, 1 - c)
        cps = []
        for w in range(4):
            for s in range(N_SHARD):
                cp = pltpu.make_async_remote_copy(
                    src_ref=_UNITS[w](gs[w], s, 1 - c), dst_ref=rs[w].at[s],
                    send_sem=send_sems.at[w, s], recv_sem=recv_sems.at[w, s],
                    device_id=sibling, device_id_type=MESH)
                cp.start()
                cps.append(cp)
        for cp in cps:
            cp.wait()

    hbm = pl.BlockSpec(memory_space=pl.ANY)
    return pl.pallas_call(
        body, name="grad_xchg1",
        out_shape=tuple(jax.ShapeDtypeStruct((N_SHARD,) + u, F32) for u in _UNIT_SHAPES),
        in_specs=[hbm] * 4, out_specs=[hbm] * 4,
        scratch_shapes=[pltpu.SemaphoreType.DMA((4, N_SHARD)), pltpu.SemaphoreType.DMA((4, N_SHARD))],
        compiler_params=_params(16),
    )(*grads)


def _grad_add1(w, g, recv, pos):
    ur, uc = _UNIT_SHAPES[w]
    if w == 3:
        g_map = lambda s, pos: (2 * s + pos[0], 0)
    else:
        g_map = lambda s, pos: (pos[0], s)

    def body(pos_ref, g_ref, r_ref, cs_ref, csb_ref):
        v = g_ref[...] + r_ref[0]
        cs_ref[0] = v
        csb_ref[0] = v.astype(BF)

    u3 = lambda s, pos: (s, 0, 0)
    return pl.pallas_call(
        body, name=f"grad_add1_{w}",
        grid_spec=pltpu.PrefetchScalarGridSpec(
            num_scalar_prefetch=1, grid=(N_SHARD,),
            in_specs=[pl.BlockSpec((ur, uc), g_map), pl.BlockSpec((1, ur, uc), u3)],
            out_specs=[pl.BlockSpec((1, ur, uc), u3), pl.BlockSpec((1, ur, uc), u3)]),
        out_shape=(jax.ShapeDtypeStruct((N_SHARD, ur, uc), F32), jax.ShapeDtypeStruct((N_SHARD, ur, uc), BF)),
        compiler_params=_params(40, dimension_semantics=("arbitrary",)),
    )(pos, g, recv)


def _grad_xchg2(csb):
    def body(c0, c1, c2, c3, r0, r1, r2, r3, send_sems, recv_sems):
        cs, rs = (c0, c1, c2, c3), (r0, r1, r2, r3)
        x, y, c, chips = _mesh_pos()
        cps = []
        for j, (cx, cy) in enumerate(chips):
            for w in range(4):
                cp = pltpu.make_async_remote_copy(
                    src_ref=cs[w].at[2 * cx + cy], dst_ref=rs[w].at[j],
                    send_sem=send_sems.at[w, j], recv_sem=recv_sems.at[w, j],
                    device_id=(cx, cy, c), device_id_type=MESH)
                cp.start()
                cps.append(cp)
        for cp in cps:
            cp.wait()

    hbm = pl.BlockSpec(memory_space=pl.ANY)
    return pl.pallas_call(
        body, name="grad_xchg2",
        out_shape=tuple(jax.ShapeDtypeStruct((3,) + u, BF) for u in _UNIT_SHAPES),
        in_specs=[hbm] * 4, out_specs=[hbm] * 4,
        scratch_shapes=[pltpu.SemaphoreType.DMA((4, 3)), pltpu.SemaphoreType.DMA((4, 3))],
        compiler_params=_params(16),
    )(*csb)


def _grad_add2(w, cs, recv, pos):
    ur, uc = _UNIT_SHAPES[w]
    tr = ur // 4 if w == 0 else ur

    def body(pos_ref, cs_ref, r_ref, o_ref):
        o_ref[...] = ((cs_ref[0] + r_ref[0].astype(F32)) + r_ref[1].astype(F32)) + r_ref[2].astype(F32)

    return pl.pallas_call(
        body, name=f"grad_add2_{w}",
        grid_spec=pltpu.PrefetchScalarGridSpec(
            num_scalar_prefetch=1, grid=(ur // tr,),
            in_specs=[pl.BlockSpec((1, tr, uc), lambda t, pos: (pos[1], t, 0)),
                      pl.BlockSpec((3, tr, uc), lambda t, pos: (0, t, 0))],
            out_specs=pl.BlockSpec((tr, uc), lambda t, pos: (t, 0))),
        out_shape=jax.ShapeDtypeStruct((ur, uc), F32),
        compiler_params=_params(32, dimension_semantics=("arbitrary",)),
    )(pos, cs, recv)


def _grad_xchg3(units):
    def body(u0, u1, u2, u3, o0, o1, o2, o3, send_sems, recv_sems, local_sems):
        us, os_ = (u0, u1, u2, u3), (o0, o1, o2, o3)
        x, y, c, _ = _mesh_pos()
        sibling = (x, y, 1 - c)
        cps = []
        for w in range(4):
            rows = _HALF_ROWS[w]
            mine = os_[w].at[pl.ds(_mo(c * rows, rows), rows), :]
            lc = pltpu.make_async_copy(us[w], mine, local_sems.at[w])
            lc.start()
            cps.append(lc)
            cp = pltpu.make_async_remote_copy(
                src_ref=us[w], dst_ref=mine, send_sem=send_sems.at[w], recv_sem=recv_sems.at[w],
                device_id=sibling, device_id_type=MESH)
            cp.start()
            cps.append(cp)
        for cp in cps:
            cp.wait()

    hbm = pl.BlockSpec(memory_space=pl.ANY)
    return pl.pallas_call(
        body, name="grad_xchg3",
        out_shape=tuple(jax.ShapeDtypeStruct(s, F32) for s in _SHARD_SHAPES),
        in_specs=[hbm] * 4, out_specs=[hbm] * 4,
        scratch_shapes=[pltpu.SemaphoreType.DMA((4,)), pltpu.SemaphoreType.DMA((4,)),
                        pltpu.SemaphoreType.DMA((4,))],
        compiler_params=_params(16),
    )(*units)


def _adamw_math(w, g, m, v):
    m = ADAM_B1 * m + (1.0 - ADAM_B1) * g
    v = ADAM_B2 * v + (1.0 - ADAM_B2) * (g * g)
    m_hat = m / ADAM_C1
    v_hat = v / ADAM_C2
    delta = -ADAM_LR * (m_hat / (jnp.sqrt(v_hat) + ADAM_EPS) + ADAM_WD * w)
    return delta, m, v


def _adamw(name, w, g, m, v, tr=256):
    rows, cols = w.shape

    def body(w_ref, g_ref, m_ref, v_ref, d_ref, nm_ref, nv_ref):
        d_ref[...], nm_ref[...], nv_ref[...] = _adamw_math(w_ref[...], g_ref[...], m_ref[...], v_ref[...])

    spec = pl.BlockSpec((tr, cols), lambda i: (i, 0))
    return pl.pallas_call(
        body, name=name, grid=(rows // tr,),
        out_shape=tuple(jax.ShapeDtypeStruct((rows, cols), F32) for _ in range(3)),
        in_specs=[spec] * 4, out_specs=[spec] * 3,
        compiler_params=_params(32, dimension_semantics=("arbitrary",)),
    )(w, g, m, v)


def _small_reduce_adamw(pg, pw, pm, pv):
    rows = pg.shape[0]

    def body(g_ref, w_ref, m_ref, v_ref, gs_ref, d_ref, nm_ref, nv_ref, gath, send_sems, recv_sems):
        x, y, c, chips = _mesh_pos()
        me, sibling = (x, y, c), (x, y, 1 - c)

        def blk(px, py, pc):
            return gath.at[4 * px + 2 * py + pc]

        def copy(k, block, to, src=None):
            return pltpu.make_async_remote_copy(
                src_ref=blk(*block) if src is None else src, dst_ref=blk(*block),
                send_sem=send_sems.at[k], recv_sem=recv_sems.at[k], device_id=to, device_id_type=MESH)

        gath[4 * x + 2 * y + c] = g_ref[...]
        first = [copy(0, me, sibling, src=g_ref)]
        first += [copy(1 + j, me, (*chip, c), src=g_ref) for j, chip in enumerate(chips)]
        for cp in first:
            cp.start()
        passed = [copy(4 + j, (*chip, c), sibling) for j, chip in enumerate(chips)]
        for j, chip in enumerate(chips):
            copy(1 + j, (*chip, c), me).wait_recv()
            passed[j].start()
        copy(0, sibling, me).wait_recv()
        for j, chip in enumerate(chips):
            copy(4 + j, (*chip, 1 - c), me).wait_recv()
        for cp in first + passed:
            cp.wait_send()
        total = gath[0]
        for k in range(1, 8):
            total = total + gath[k]
        gs_ref[...] = total
        d_ref[...], nm_ref[...], nv_ref[...] = _adamw_math(w_ref[...], total, m_ref[...], v_ref[...])

    vm = pl.BlockSpec(memory_space=pltpu.VMEM)
    return pl.pallas_call(
        body, name="small_reduce_adamw",
        out_shape=tuple(jax.ShapeDtypeStruct((rows, 128), F32) for _ in range(4)),
        in_specs=[vm] * 4, out_specs=[vm] * 4,
        scratch_shapes=[pltpu.VMEM((8, rows, 128), F32), pltpu.SemaphoreType.DMA((7,)),
                        pltpu.SemaphoreType.DMA((7,))],
        compiler_params=_params(32),
    )(pg, pw, pm, pv)


_REL_PAD = 384


def _rows(a):
    a = a.reshape(-1, 128)
    pad = (-a.shape[0]) % 8
    return jnp.pad(a, ((0, pad), (0, 0))) if pad else a


def _pack_small(norm_g, b_gate, rel_bias, ln_g, ln_b, w_s, b_s, final_g, loss_row):
    rel = jnp.pad(rel_bias.reshape(N_HEADS, N_REL), ((0, 0), (0, _REL_PAD - N_REL)))
    parts = [norm_g, b_gate, rel, ln_g, ln_b, w_s, b_s, final_g, loss_row]
    return jnp.concatenate([_rows(p) for p in parts], axis=0)


_SMALL_LAYOUT = (("norm_g", 8, 8), ("b_gate", 16, 16), ("rel_bias", 24, 24), ("sgu_ln_g", 4, 8),
                 ("sgu_ln_b", 4, 8), ("w_s", 512, 512), ("b_s", 4, 8), ("final_g", 8, 8), ("loss", 1, 8))


def _unpack_small(p):
    out, r = {}, 0
    for name, used, alloc in _SMALL_LAYOUT:
        out[name] = p[r:r + used]
        r += alloc
    return out


def _small_outputs(p):
    u = _unpack_small(p)
    return (u["norm_g"].reshape(1, D_MODEL), u["b_gate"].reshape(1, 2 * D_MODEL),
            u["rel_bias"].reshape(N_HEADS, _REL_PAD)[:, :N_REL].reshape(1, N_HEADS, N_REL),
            u["sgu_ln_g"].reshape(1, D_B), u["sgu_ln_b"].reshape(1, D_B),
            u["w_s"].reshape(1, N_GROUPS, 128, 128), u["b_s"].reshape(1, N_GROUPS, 128),
            u["final_g"].reshape(D_MODEL)), u["loss"]


def _bias_row(rel_bias):
    hi = rel_bias[:, N_REL - 1:N_REL]
    lo = rel_bias[:, 0:1]
    return jnp.concatenate([jnp.broadcast_to(hi, (N_HEADS, 384)), rel_bias[:, ::-1],
                            jnp.broadcast_to(lo, (N_HEADS, 191)), jnp.broadcast_to(hi, (N_HEADS, 192))], axis=1)


def kernel(x, norm_g, w_in, b_gate, rel_bias, sgu_ln_g, sgu_ln_b, w_s, b_s, w_pa, w_pb, w_out, final_g, loss_target, m_norm_g, m_w_in, m_b_gate, m_rel_bias, m_sgu_ln_g, m_sgu_ln_b, m_w_s, m_b_s, m_w_pa, m_w_pb, m_w_out, m_final_g, v_norm_g, v_w_in, v_b_gate, v_rel_bias, v_sgu_ln_g, v_sgu_ln_b, v_w_s, v_b_s, v_w_pa, v_w_pb, v_w_out, v_final_g):
    S = x.shape[1]
    xs = x.reshape(S, D_MODEL)
    tgt = loss_target.reshape(S, D_MODEL)
    big_w = (w_in[0], w_pa[0], w_pb[0], w_out[0])
    big_m = (m_w_in[0], m_w_pa[0], m_w_pb[0], m_w_out[0])
    big_v = (v_w_in[0], v_w_pa[0], v_w_pb[0], v_w_out[0])
    rel = rel_bias[0]
    ws = w_s[0]
    bst = b_s[0].T
    fg = final_g.reshape(1, D_MODEL)
    pos = jnp.stack([lax.axis_index("c"), 2 * lax.axis_index("x") + lax.axis_index("y")]).astype(jnp.int32)

    w_in_bf, w_pa_bf, w_pb_bf, w_out_bf = _ag_weights(*big_w)

    ht, qkv, zrest = _inproj_fwd(xs, norm_g, w_in_bf)
    gp = _bias_row(rel)
    att = _attn_fwd(qkv, gp)
    sg = _sgu_fwd(zrest, sgu_ln_g, sgu_ln_b, ws, bst)
    (d_out, d_att, d_sg, dzt, gw_out, gw_pa, gw_pb, g_bgate, g_final, loss_row) = _tail(
        att, sg, zrest, xs, tgt, w_pa_bf, w_pb_bf, w_out_bf, b_gate, fg)
    dq, dk, dv, d_gp = _attn_bwd(qkv, d_att, gp)
    dzs, g_ws, g_bs_t, g_lng, g_lnb = _sgu_bwd(zrest, d_sg, sgu_ln_g, sgu_ln_b, ws, bst)
    grad_x, g_norm = _dh_gradx(dq, dk, dv, dzt, dzs, w_in_bf, xs, norm_g, d_out)
    gw_in = _gw_in(ht, dq, dk, dv, dzt, dzs)

    grads = (gw_in, gw_pa, gw_pb, gw_out)
    recv1 = _grad_xchg1(grads)
    sums = [_grad_add1(w, grads[w], recv1[w], pos) for w in range(4)]
    recv2 = _grad_xchg2(tuple(s[1] for s in sums))
    units = [_grad_add2(w, sums[w][0], recv2[w], pos) for w in range(4)]
    g_shards = _grad_xchg3(tuple(units))
    names = ("adamw_w_in", "adamw_w_pa", "adamw_w_pb", "adamw_w_out")
    big = [_adamw(names[w], big_w[w], g_shards[w], big_m[w], big_v[w],
                  tr=128 if w == 3 else 256) for w in range(4)]

    g_rel = d_gp[:, 384:384 + N_REL][:, ::-1]
    g_bs = g_bs_t[:, :N_GROUPS].T
    pg = _pack_small(g_norm, g_bgate, g_rel, g_lng, g_lnb, g_ws, g_bs, g_final, loss_row)
    zero_row = jnp.zeros((1, 128), F32)
    pw = _pack_small(norm_g, b_gate, rel, sgu_ln_g, sgu_ln_b, ws, b_s, final_g, zero_row)
    pm = _pack_small(m_norm_g, m_b_gate, m_rel_bias, m_sgu_ln_g, m_sgu_ln_b, m_w_s, m_b_s, m_final_g, zero_row)
    pv = _pack_small(v_norm_g, v_b_gate, v_rel_bias, v_sgu_ln_g, v_sgu_ln_b, v_w_s, v_b_s, v_final_g, zero_row)
    gsum, sdelta, sm, sv = _small_reduce_adamw(pg, pw, pm, pv)
    sg_out, loss_rows = _small_outputs(gsum)
    sd_out, _ = _small_outputs(sdelta)
    sm_out, _ = _small_outputs(sm)
    sv_out, _ = _small_outputs(sv)
    loss = loss_rows[0, 0]

    def assemble(small, bigs):
        n_g, b_g, r_b, l_g, l_b, w_s_, b_s_, f_g = small
        b_in, b_pa, b_pb, b_out = (b[None] for b in bigs)
        return (n_g, b_in, b_g, r_b, l_g, l_b, w_s_, b_s_, b_pa, b_pb, b_out, f_g)

    grads_out = assemble(sg_out, g_shards)
    delta_out = assemble(sd_out, [b[0] for b in big])
    m_out = assemble(sm_out, [b[1] for b in big])
    v_out = assemble(sv_out, [b[2] for b in big])
    return (loss, grad_x.reshape(1, S, D_MODEL), *grads_out, *delta_out, *m_out, *v_out)
```

```python
import functools
import math

import jax
import jax.numpy as jnp
from jax import lax
from jax.experimental import pallas as pl
from jax.experimental.pallas import tpu as pltpu

F32 = jnp.float32
BF = jnp.bfloat16
MESH = pl.DeviceIdType.MESH

D_MODEL = 1024
D_A = 512
D_B = 512
D_IN = 5632
N_HEADS = 8
HEAD_DIM = 64
CHUNK = 64
N_PREV = 8
SGU_CHUNK = 128
N_GROUPS = 4
N_REL = 257
EPS = 1e-6
NEG_INF = -1e30
SCALE = HEAD_DIM ** -0.5

QB = 2 * CHUNK
KB = (N_PREV + 2) * CHUNK
PADK = N_PREV * CHUNK
ROLL_W = 1024
KEEP = KB // QB - 1

ADAM_LR = 0.001
ADAM_B1 = 0.9
ADAM_B2 = 0.999
ADAM_EPS = 1e-08
ADAM_WD = 0.01
ADAM_STEP = 10
ADAM_C1 = 1.0 - ADAM_B1 ** ADAM_STEP
ADAM_C2 = 1.0 - ADAM_B2 ** ADAM_STEP

N_SHARD = 4
SHARD_IN = D_IN // N_SHARD
MIB = 1024 * 1024


def _params(vmem_mib, **kw):
    return pltpu.CompilerParams(vmem_limit_bytes=vmem_mib * MIB, **kw)


def _sigmoid(x):
    return 1.0 / (1.0 + jnp.exp(-x))


def _silu_and_grad(x):
    s = _sigmoid(x)
    return x * s, s * (1.0 + x * (1.0 - s))


_GELU_C = math.sqrt(2.0 / math.pi)
_GELU_A = 0.044715


def _gelu_and_grad(x):
    x2 = x * x
    t = jnp.tanh(_GELU_C * (x + _GELU_A * (x2 * x)))
    cdf = 0.5 * (1.0 + t)
    grad = cdf + 0.5 * x * (1.0 - t * t) * (_GELU_C * (1.0 + 3.0 * _GELU_A * x2))
    return x * cdf, grad


def _dot(a, b):
    return jnp.dot(a, b, preferred_element_type=F32)


def _dot_nt(a, b):
    return lax.dot_general(a, b, (((1,), (1,)), ((), ())), preferred_element_type=F32)


def _dot_tn(a, b):
    return lax.dot_general(a, b, (((0,), (0,)), ((), ())), preferred_element_type=F32)


def _mo(v, m):
    return v if isinstance(v, int) else pl.multiple_of(v, m)


def _unit_in(ref, s, p):
    return ref.at[pl.ds(_mo(p * 512, 512), 512), pl.ds(_mo(s * SHARD_IN, 128), SHARD_IN)]


def _unit_p(ref, s, p):
    return ref.at[pl.ds(_mo(p * 256, 256), 256), pl.ds(_mo(s * 256, 128), 256)]


def _unit_out(ref, s, p):
    return ref.at[pl.ds(_mo(s * 256 + p * 128, 128), 128), :]


_UNITS = (_unit_in, _unit_p, _unit_p, _unit_out)
_HALF_ROWS = (512, 256, 256, 128)
_UNIT_SHAPES = ((512, SHARD_IN), (256, 256), (256, 256), (128, D_MODEL))
_FULL_SHAPES = ((D_MODEL, D_IN), (D_A, D_MODEL), (D_B, D_MODEL), (D_MODEL, D_MODEL))
_SHARD_SHAPES = ((D_MODEL, SHARD_IN), (D_A, 256), (D_B, 256), (256, D_MODEL))


def _mesh_pos():
    x, y, c = lax.axis_index("x"), lax.axis_index("y"), lax.axis_index("c")
    chips = [(1 - x, y), (x, 1 - y), (1 - x, 1 - y)]
    return x, y, c, chips


def _ag_weights(w_in, w_pa, w_pb, w_out):
    def body(i0, i1, i2, i3, o0, o1, o2, o3, s0, s1, s2, s3, send_sems, recv_sems, local_sems):
        ins, outs, stage = (i0, i1, i2, i3), (o0, o1, o2, o3), (s0, s1, s2, s3)
        x, y, c, chips = _mesh_pos()
        s_me = 2 * x + y
        sibling = (x, y, 1 - c)
        for w in range(4):
            stage[w][...] = ins[w][...].astype(BF)

        def half(w, p):
            rows = _HALF_ROWS[w]
            return stage[w].at[pl.ds(_mo(p * rows, rows), rows), :]

        local = []
        for w in range(4):
            for p in range(2):
                cp = pltpu.make_async_copy(half(w, p), _UNITS[w](outs[w], s_me, p), local_sems.at[w, p])
                cp.start()
                local.append(cp)

        def rcopy(w, k, src, dst, to):
            return pltpu.make_async_remote_copy(src_ref=src, dst_ref=dst, send_sem=send_sems.at[w, k],
                                                recv_sem=recv_sems.at[w, k], device_id=to, device_id_type=MESH)

        sends = []
        for j, (cx, cy) in enumerate(chips):
            for w in range(4):
                cp = rcopy(w, j, half(w, c), _UNITS[w](outs[w], s_me, c), (cx, cy, c))
                cp.start()
                sends.append(cp)
        for j, (cx, cy) in enumerate(chips):
            s_j = 2 * cx + cy
            for w in range(4):
                landed = _UNITS[w](outs[w], s_j, c)
                rcopy(w, j, landed, landed, (cx, cy, c)).wait_recv()
                cp = rcopy(w, 3 + j, landed, landed, sibling)
                cp.start()
                sends.append(cp)
        for j, (cx, cy) in enumerate(chips):
            s_j = 2 * cx + cy
            for w in range(4):
                other = _UNITS[w](outs[w], s_j, 1 - c)
                rcopy(w, 3 + j, other, other, sibling).wait_recv()
        for cp in sends:
            cp.wait_send()
        for cp in local:
            cp.wait()

    vm = pl.BlockSpec(memory_space=pltpu.VMEM)
    hbm = pl.BlockSpec(memory_space=pl.ANY)
    return pl.pallas_call(
        body, name="ag_weights",
        out_shape=tuple(jax.ShapeDtypeStruct(s, BF) for s in _FULL_SHAPES),
        in_specs=[vm] * 4, out_specs=[hbm] * 4,
        scratch_shapes=[pltpu.VMEM(s, BF) for s in _SHARD_SHAPES]
        + [pltpu.SemaphoreType.DMA((4, 6)), pltpu.SemaphoreType.DMA((4, 6)), pltpu.SemaphoreType.DMA((4, 2))],
        compiler_params=_params(40),
    )(w_in, w_pa, w_pb, w_out)


def _inproj_fwd(x, norm_g, w_in_bf, tm=256):
    S = x.shape[0]

    def body(x_ref, g_ref, w_ref, ht_ref, q_ref, k_ref, v_ref, zr_ref):
        xv = x_ref[...]
        r = lax.rsqrt(jnp.mean(xv * xv, axis=-1, keepdims=True) + EPS)
        hf = (xv * r) * g_ref[...]
        ht_ref[...] = hf.T.astype(BF)
        h = hf.astype(BF)
        heads = (q_ref, k_ref, v_ref)
        for j in range(D_IN // 512):
            z = _dot(h, w_ref[:, j * 512:(j + 1) * 512])
            if j < 3:
                zb = z.astype(BF)
                for hd in range(N_HEADS):
                    heads[j][hd] = zb[:, hd * HEAD_DIM:(hd + 1) * HEAD_DIM]
            else:
                zr_ref[:, (j - 3) * 512:(j - 2) * 512] = z

    head_major = jax.ShapeDtypeStruct((N_HEADS, S, HEAD_DIM), BF)
    head_spec = pl.BlockSpec((N_HEADS, tm, HEAD_DIM), lambda i: (0, i, 0))
    return pl.pallas_call(
        body, name="inproj_fwd", grid=(S // tm,),
        out_shape=(jax.ShapeDtypeStruct((D_MODEL, S), BF), head_major, head_major, head_major,
                   jax.ShapeDtypeStruct((S, D_IN - 3 * D_A), F32)),
        in_specs=[pl.BlockSpec((tm, D_MODEL), lambda i: (i, 0)),
                  pl.BlockSpec((1, D_MODEL), lambda i: (0, 0)),
                  pl.BlockSpec((D_MODEL, D_IN), lambda i: (0, 0))],
        out_specs=[pl.BlockSpec((D_MODEL, tm), lambda i: (0, i)),
                   head_spec, head_spec, head_spec,
                   pl.BlockSpec((tm, D_IN - 3 * D_A), lambda i: (i, 0))],
        compiler_params=_params(52, dimension_semantics=("arbitrary",)),
    )(x, norm_g, w_in_bf)


def _skew_table(gp_row):
    row = lax.broadcasted_iota(jnp.int32, (QB, ROLL_W), 0)
    t = jnp.broadcast_to(gp_row, (QB, ROLL_W))
    for b in range(7):
        t = jnp.where(((row >> b) & 1) == 1, pltpu.roll(t, 1 << b, axis=1), t)
    return t


def _unskew_sum(d):
    row = lax.broadcasted_iota(jnp.int32, (QB, ROLL_W), 0)
    for b in range(7):
        d = jnp.where(((row >> b) & 1) == 1, pltpu.roll(d, ROLL_W - (1 << b), axis=1), d)
    return jnp.sum(d, axis=0, keepdims=True)


def _struct_mask():
    a = lax.broadcasted_iota(jnp.int32, (QB, KB), 0) // CHUNK
    b = lax.broadcasted_iota(jnp.int32, (QB, KB), 1) // CHUNK
    return (b >= a) & (b <= a + N_PREV)


def _load_kv(k_hbm, v_hbm, gp_ref, k_scr, v_scr, bias_scr, sems, S):
    zeros = jnp.zeros((N_HEADS, PADK, HEAD_DIM), BF)
    k_scr[:, 0:PADK, :] = zeros
    v_scr[:, 0:PADK, :] = zeros
    ck = pltpu.make_async_copy(k_hbm, k_scr.at[:, pl.ds(PADK, S), :], sems.at[0])
    cv = pltpu.make_async_copy(v_hbm, v_scr.at[:, pl.ds(PADK, S), :], sems.at[1])
    ck.start()
    cv.start()
    keep = _struct_mask()
    for h in range(N_HEADS):
        bias_scr[h] = jnp.where(keep, _skew_table(gp_ref[h:h + 1, :])[:, :KB], NEG_INF)
    ck.wait()
    cv.wait()


_BATCH_NT = (((2,), (2,)), ((0,), (0,)))
_BATCH_NN = (((2,), (1,)), ((0,), (0,)))
_BATCH_TN = (((1,), (1,)), ((0,), (0,)))


def _bdot(a, b, dims):
    return lax.dot_general(a, b, dims, preferred_element_type=F32)


def _probs(q, kb, bias, i, front):
    s = _bdot(q * jnp.asarray(SCALE, BF), kb, _BATCH_NT) + bias
    if front:
        col = lax.broadcasted_iota(jnp.int32, (1, 1, KB), 2)
        s = jnp.where(col >= PADK - i * QB, s, NEG_INF)
    m = jnp.max(s, axis=-1, keepdims=True)
    e = jnp.exp(s - m)
    return e * (1.0 / jnp.sum(e, axis=-1, keepdims=True))


def _attn_fwd(q3, k3, v3, gp):
    S = q3.shape[1]

    def body(q_ref, k_hbm, v_hbm, gp_ref, o_ref, k_scr, v_scr, bias_scr, sems):
        i = pl.program_id(0)

        @pl.when(i == 0)
        def _():
            _load_kv(k_hbm, v_hbm, gp_ref, k_scr, v_scr, bias_scr, sems, S)

        def step(front):
            start = pl.multiple_of(i * QB, QB)
            kb = k_scr[:, pl.ds(start, KB), :]
            vb = v_scr[:, pl.ds(start, KB), :]
            p = _probs(q_ref[...], kb, bias_scr[...], i, front)
            o = _bdot(p.astype(BF), vb, _BATCH_NN)
            for h in range(N_HEADS):
                o_ref[:, h * HEAD_DIM:(h + 1) * HEAD_DIM] = o[h]

        pl.when(i < KEEP)(functools.partial(step, True))
        pl.when(i >= KEEP)(functools.partial(step, False))

    kv_scr = pltpu.VMEM((N_HEADS, S + PADK, HEAD_DIM), BF)
    return pl.pallas_call(
        body, name="attn_fwd", grid=(S // QB,),
        out_shape=jax.ShapeDtypeStruct((S, D_A), F32),
        in_specs=[pl.BlockSpec((N_HEADS, QB, HEAD_DIM), lambda i: (0, i, 0)),
                  pl.BlockSpec(memory_space=pl.ANY), pl.BlockSpec(memory_space=pl.ANY),
                  pl.BlockSpec((N_HEADS, ROLL_W), lambda i: (0, 0))],
        out_specs=pl.BlockSpec((QB, D_A), lambda i: (i, 0)),
        scratch_shapes=[kv_scr, kv_scr, pltpu.VMEM((N_HEADS, QB, KB), F32), pltpu.SemaphoreType.DMA((2,))],
        compiler_params=_params(48, dimension_semantics=("arbitrary",)),
    )(q3, k3, v3, gp)


def _attn_bwd(q3, k3, v3, d_att3, gp):
    S = q3.shape[1]
    nq = S // QB

    def body(q_ref, do_ref, k_hbm, v_hbm, gp_ref, dq_ref, dk_ref, dv_ref, dgp_ref,
             k_scr, v_scr, bias_scr, dk_acc, dv_acc, dbias_acc, pad_scr, sems):
        i = pl.program_id(0)

        @pl.when(i == 0)
        def _():
            _load_kv(k_hbm, v_hbm, gp_ref, k_scr, v_scr, bias_scr, sems, S)
            dk_acc[...] = jnp.zeros_like(dk_acc)
            dv_acc[...] = jnp.zeros_like(dv_acc)
            dbias_acc[...] = jnp.zeros_like(dbias_acc)

        def step(front):
            start = pl.multiple_of(i * QB, QB)
            kb = k_scr[:, pl.ds(start, KB), :]
            vb = v_scr[:, pl.ds(start, KB), :]
            q = q_ref[...]
            do = do_ref[...]
            p = _probs(q, kb, bias_scr[...], i, front)
            dp = _bdot(do, vb, _BATCH_NT)
            ds = p * (dp - jnp.sum(dp * p, axis=-1, keepdims=True))
            dbias_acc[...] += ds
            dsb = (ds * SCALE).astype(BF)
            dq = _bdot(dsb, kb, _BATCH_NN)
            for h in range(N_HEADS):
                dq_ref[:, h * HEAD_DIM:(h + 1) * HEAD_DIM] = dq[h].astype(BF)
            dk_acc[...] += _bdot(dsb, q, _BATCH_TN)
            dv_acc[...] += _bdot(p.astype(BF), do, _BATCH_TN)

        pl.when(i < KEEP)(functools.partial(step, True))
        pl.when((i >= KEEP) & (i < nq))(functools.partial(step, False))

        for h in range(N_HEADS):
            hs = slice(h * HEAD_DIM, (h + 1) * HEAD_DIM)
            dk_ref[:, hs] = dk_acc[h, 0:QB, :].astype(BF)
            dv_ref[:, hs] = dv_acc[h, 0:QB, :].astype(BF)
        dk_acc[:, 0:KB - QB, :] = dk_acc[:, QB:KB, :]
        dv_acc[:, 0:KB - QB, :] = dv_acc[:, QB:KB, :]
        dk_acc[:, KB - QB:KB, :] = jnp.zeros((N_HEADS, QB, HEAD_DIM), F32)
        dv_acc[:, KB - QB:KB, :] = jnp.zeros((N_HEADS, QB, HEAD_DIM), F32)

        @pl.when(i == nq + KEEP - 1)
        def _():
            lane = lax.broadcasted_iota(jnp.int32, (1, ROLL_W), 1)
            hi = (lane < 384) | (lane >= 832)
            lo = (lane > 640) & (lane < 832)
            pad_scr[...] = jnp.zeros_like(pad_scr)
            for h in range(N_HEADS):
                pad_scr[:, 0:KB] = dbias_acc[h]
                g = _unskew_sum(pad_scr[...])
                s_hi = jnp.sum(jnp.where(hi, g, 0.0), axis=-1, keepdims=True)
                s_lo = jnp.sum(jnp.where(lo, g, 0.0), axis=-1, keepdims=True)
                g = jnp.where(lane == 384, g + s_hi, g)
                g = jnp.where(lane == 640, g + s_lo, g)
                dgp_ref[h:h + 1, :] = g

    last = nq - 1
    kv_scr = pltpu.VMEM((N_HEADS, S + PADK, HEAD_DIM), BF)
    return pl.pallas_call(
        body, name="attn_bwd", grid=(nq + KEEP,),
        out_shape=(jax.ShapeDtypeStruct((S, D_A), BF), jax.ShapeDtypeStruct((S, D_A), BF),
                   jax.ShapeDtypeStruct((S, D_A), BF), jax.ShapeDtypeStruct((N_HEADS, ROLL_W), F32)),
        in_specs=[pl.BlockSpec((N_HEADS, QB, HEAD_DIM), lambda i: (0, jnp.minimum(i, last), 0)),
                  pl.BlockSpec((N_HEADS, QB, HEAD_DIM), lambda i: (0, jnp.minimum(i, last), 0)),
                  pl.BlockSpec(memory_space=pl.ANY), pl.BlockSpec(memory_space=pl.ANY),
                  pl.BlockSpec((N_HEADS, ROLL_W), lambda i: (0, 0))],
        out_specs=[pl.BlockSpec((QB, D_A), lambda i: (jnp.minimum(i, last), 0)),
                   pl.BlockSpec((QB, D_A), lambda i: (jnp.maximum(i - KEEP, 0), 0)),
                   pl.BlockSpec((QB, D_A), lambda i: (jnp.maximum(i - KEEP, 0), 0)),
                   pl.BlockSpec((N_HEADS, ROLL_W), lambda i: (0, 0))],
        scratch_shapes=[kv_scr, kv_scr, pltpu.VMEM((N_HEADS, QB, KB), F32),
                        pltpu.VMEM((N_HEADS, KB, HEAD_DIM), F32), pltpu.VMEM((N_HEADS, KB, HEAD_DIM), F32),
                        pltpu.VMEM((N_HEADS, QB, KB), F32), pltpu.VMEM((QB, ROLL_W), F32),
                        pltpu.SemaphoreType.DMA((2,))],
        compiler_params=_params(56, dimension_semantics=("arbitrary",)),
    )(q3, d_att3, k3, v3, gp)


def _sgu_core(ub, vb, lg, lb):
    u, du = _gelu_and_grad(ub)
    v, dv = _gelu_and_grad(vb)
    mu = jnp.mean(v, axis=-1, keepdims=True)
    vc = v - mu
    rstd = lax.rsqrt(jnp.mean(vc * vc, axis=-1, keepdims=True) + EPS)
    xh = vc * rstd
    vn = xh * lg + lb
    return u, du, dv, rstd, xh, vn


def _tri():
    r = lax.broadcasted_iota(jnp.int32, (SGU_CHUNK, SGU_CHUNK), 0)
    c = lax.broadcasted_iota(jnp.int32, (SGU_CHUNK, SGU_CHUNK), 1)
    return r >= c


def _sgu_fwd(zrest, ln_g, ln_b, w_s, b_s_t, tm=512):
    S = zrest.shape[0]

    def body(ub_ref, vb_ref, lg_ref, lb_ref, ws_ref, bst_ref, sg_ref):
        u, _, _, _, _, vn = _sgu_core(ub_ref[...], vb_ref[...], lg_ref[...], lb_ref[...])
        vnb = vn.astype(BF)
        tri = _tri()
        for g in range(N_GROUPS):
            cs = slice(g * 128, (g + 1) * 128)
            wt = jnp.where(tri, ws_ref[g], 0.0).astype(BF)
            bcol = bst_ref[:, g:g + 1]
            for n in range(tm // SGU_CHUNK):
                rs = slice(n * SGU_CHUNK, (n + 1) * SGU_CHUNK)
                mixed = _dot(wt, vnb[rs, cs]) + bcol
                sg_ref[rs, cs] = u[rs, cs] * mixed

    return pl.pallas_call(
        body, name="sgu_fwd", grid=(S // tm,),
        out_shape=jax.ShapeDtypeStruct((S, D_B), F32),
        in_specs=[pl.BlockSpec((tm, 512), lambda i: (i, 1)),
                  pl.BlockSpec((tm, 512), lambda i: (i, 2)),
                  pl.BlockSpec((1, D_B), lambda i: (0, 0)),
                  pl.BlockSpec((1, D_B), lambda i: (0, 0)),
                  pl.BlockSpec((N_GROUPS, 128, 128), lambda i: (0, 0, 0)),
                  pl.BlockSpec((128, N_GROUPS), lambda i: (0, 0))],
        out_specs=pl.BlockSpec((tm, D_B), lambda i: (i, 0)),
        compiler_params=_params(32, dimension_semantics=("arbitrary",)),
    )(zrest, zrest, ln_g, ln_b, w_s, b_s_t)


def _sgu_bwd(zrest, d_sg, ln_g, ln_b, w_s, b_s_t, tm=256):
    S = zrest.shape[0]
    nt = S // tm

    def body(ub_ref, vb_ref, dsg_ref, lg_ref, lb_ref, ws_ref, bst_ref,
             dzs_ref, gws_ref, gbs_ref, glg_ref, glb_ref, dvn_scr, bs_acc):
        i = pl.program_id(0)

        @pl.when(i == 0)
        def _():
            gws_ref[...] = jnp.zeros_like(gws_ref)
            glg_ref[...] = jnp.zeros_like(glg_ref)
            glb_ref[...] = jnp.zeros_like(glb_ref)
            bs_acc[...] = jnp.zeros_like(bs_acc)

        ub = ub_ref[...]
        u, du, dv, rstd, xh, vn = _sgu_core(ub, vb_ref[...], lg_ref[...], lb_ref[...])
        vnb = vn.astype(BF)
        dsg = dsg_ref[...]
        tri = _tri()
        for g in range(N_GROUPS):
            cs = slice(g * 128, (g + 1) * 128)
            wtf = jnp.where(tri, ws_ref[g], 0.0)
            wt = wtf.astype(BF)
            wtt = wtf.T.astype(BF)
            bcol = bst_ref[:, g:g + 1]
            for n in range(tm // SGU_CHUNK):
                rs = slice(n * SGU_CHUNK, (n + 1) * SGU_CHUNK)
                mixed = _dot(wt, vnb[rs, cs]) + bcol
                dzs_ref[rs, cs] = (dsg[rs, cs] * mixed * du[rs, cs]).astype(BF)
                dmix = dsg[rs, cs] * u[rs, cs]
                bs_acc[:, cs] += dmix
                dmb = dmix.astype(BF)
                gws_ref[g] += _dot_nt(dmb, vnb[rs, cs])
                dvn_scr[rs, cs] = _dot(wtt, dmb)
        dvn = dvn_scr[...]
        glg_ref[...] += jnp.sum(dvn * xh, axis=0, keepdims=True)
        glb_ref[...] += jnp.sum(dvn, axis=0, keepdims=True)
        dxh = dvn * lg_ref[...]
        dvv = rstd * (dxh - jnp.mean(dxh, axis=-1, keepdims=True)
                      - xh * jnp.mean(dxh * xh, axis=-1, keepdims=True))
        dzs_ref[:, D_B:2 * D_B] = (dvv * dv).astype(BF)

        @pl.when(i == nt - 1)
        def _():
            lane = lax.broadcasted_iota(jnp.int32, (SGU_CHUNK, 128), 1)
            out = jnp.zeros((SGU_CHUNK, 128), F32)
            for g in range(N_GROUPS):
                gws_ref[g] = jnp.where(tri, gws_ref[g], 0.0)
                col = jnp.sum(bs_acc[:, g * 128:(g + 1) * 128], axis=-1, keepdims=True)
                out = jnp.where(lane == g, col, out)
            gbs_ref[...] = out

    const2 = lambda i: (0, 0)
    return pl.pallas_call(
        body, name="sgu_bwd", grid=(nt,),
        out_shape=(jax.ShapeDtypeStruct((S, 2 * D_B), BF),
                   jax.ShapeDtypeStruct((N_GROUPS, 128, 128), F32),
                   jax.ShapeDtypeStruct((SGU_CHUNK, 128), F32),
                   jax.ShapeDtypeStruct((1, D_B), F32), jax.ShapeDtypeStruct((1, D_B), F32)),
        in_specs=[pl.BlockSpec((tm, 512), lambda i: (i, 1)),
                  pl.BlockSpec((tm, 512), lambda i: (i, 2)),
                  pl.BlockSpec((tm, D_B), lambda i: (i, 0)),
                  pl.BlockSpec((1, D_B), const2), pl.BlockSpec((1, D_B), const2),
                  pl.BlockSpec((N_GROUPS, 128, 128), lambda i: (0, 0, 0)),
                  pl.BlockSpec((128, N_GROUPS), const2)],
        out_specs=[pl.BlockSpec((tm, 2 * D_B), lambda i: (i, 0)),
                   pl.BlockSpec((N_GROUPS, 128, 128), lambda i: (0, 0, 0)),
                   pl.BlockSpec((SGU_CHUNK, 128), const2),
                   pl.BlockSpec((1, D_B), const2), pl.BlockSpec((1, D_B), const2)],
        scratch_shapes=[pltpu.VMEM((tm, D_B), F32), pltpu.VMEM((SGU_CHUNK, D_B), F32)],
        compiler_params=_params(32, dimension_semantics=("arbitrary",)),
    )(zrest, zrest, d_sg, ln_g, ln_b, w_s, b_s_t)


def _tail(att, sg, zrest, x, target, w_pa, w_pb, w_out, b_gate, final_g, tm=256):
    S = x.shape[0]
    nt = S // tm

    def body(att_ref, sg_ref, ga_ref, gb_ref, gta_ref, gtb_ref, x_ref, t_ref,
             wpa_ref, wpb_ref, wout_ref, bg_ref, fg_ref,
             dout_ref, datt_ref, dsg_ref, dzt_ref, gwout_hbm, gwpa_hbm, gwpb_hbm,
             gbg_ref, gfg_ref, loss_ref, acc_out, acc_pa, acc_pb, sems):
        i = pl.program_id(0)

        @pl.when(i == 0)
        def _():
            acc_out[...] = jnp.zeros_like(acc_out)
            acc_pa[...] = jnp.zeros_like(acc_pa)
            acc_pb[...] = jnp.zeros_like(acc_pb)
            gbg_ref[...] = jnp.zeros_like(gbg_ref)
            gfg_ref[...] = jnp.zeros_like(gfg_ref)
            loss_ref[...] = jnp.zeros_like(loss_ref)

        att = att_ref[...]
        sg = sg_ref[...]
        sa, dsa = _silu_and_grad(ga_ref[...])
        sb, dsb = _silu_and_grad(gb_ref[...])
        ya = (att * sa).astype(BF)
        yb = (sg * sb).astype(BF)
        pa = _dot(ya, wpa_ref[...])
        pb = _dot(yb, wpb_ref[...])
        ga = _sigmoid(gta_ref[...] + bg_ref[:, 0:D_MODEL])
        gb = _sigmoid(gtb_ref[...] + bg_ref[:, D_MODEL:2 * D_MODEL])
        merged = (ga * pa + gb * pb).astype(BF)
        out = x_ref[...] + _dot(merged, wout_ref[...])
        r2 = lax.rsqrt(jnp.mean(out * out, axis=-1, keepdims=True) + EPS)
        nrm = out * r2
        fg = fg_ref[...]
        err = nrm * fg - t_ref[...]
        loss_ref[...] += 0.5 * jnp.sum(jnp.mean(err * err, axis=-1, keepdims=True))
        dy = err * (1.0 / D_MODEL)
        gfg_ref[...] += jnp.sum(dy * nrm, axis=0, keepdims=True)
        dn = dy * fg
        d_out = r2 * (dn - nrm * jnp.mean(dn * nrm, axis=-1, keepdims=True))
        dout_ref[...] = d_out
        d_outb = d_out.astype(BF)
        acc_out[...] += _dot_tn(merged, d_outb)
        dm = _dot_nt(d_outb, wout_ref[...])
        d_pa = (dm * ga).astype(BF)
        d_pb = (dm * gb).astype(BF)
        d_gta = dm * pa * (ga * (1.0 - ga))
        d_gtb = dm * pb * (gb * (1.0 - gb))
        gbg_ref[:, 0:D_MODEL] += jnp.sum(d_gta, axis=0, keepdims=True)
        gbg_ref[:, D_MODEL:2 * D_MODEL] += jnp.sum(d_gtb, axis=0, keepdims=True)
        dzt_ref[:, 2 * D_A:2 * D_A + D_MODEL] = d_gta.astype(BF)
        dzt_ref[:, 2 * D_A + D_MODEL:] = d_gtb.astype(BF)
        acc_pa[...] += _dot_tn(ya, d_pa)
        acc_pb[...] += _dot_tn(yb, d_pb)
        d_ya = _dot_nt(d_pa, wpa_ref[...])
        d_yb = _dot_nt(d_pb, wpb_ref[...])
        d_att = (d_ya * sa).astype(BF)
        for hd in range(N_HEADS):
            datt_ref[hd] = d_att[:, hd * HEAD_DIM:(hd + 1) * HEAD_DIM]
        dzt_ref[:, 0:D_A] = (d_ya * att * dsa).astype(BF)
        dsg_ref[...] = d_yb * sb
        dzt_ref[:, D_A:2 * D_A] = (d_yb * sg * dsb).astype(BF)

        @pl.when(i == nt - 1)
        def _():
            cps = [pltpu.make_async_copy(acc_out, gwout_hbm, sems.at[0]),
                   pltpu.make_async_copy(acc_pa, gwpa_hbm, sems.at[1]),
                   pltpu.make_async_copy(acc_pb, gwpb_hbm, sems.at[2])]
            for cp in cps:
                cp.start()
            for cp in cps:
                cp.wait()

    c2 = lambda i: (0, 0)
    hbm = pl.BlockSpec(memory_space=pl.ANY)
    return pl.pallas_call(
        body, name="tail", grid=(nt,),
        out_shape=(jax.ShapeDtypeStruct((S, D_MODEL), F32), jax.ShapeDtypeStruct((N_HEADS, S, HEAD_DIM), BF),
                   jax.ShapeDtypeStruct((S, D_B), F32), jax.ShapeDtypeStruct((S, 3072), BF),
                   jax.ShapeDtypeStruct((D_MODEL, D_MODEL), F32), jax.ShapeDtypeStruct((D_A, D_MODEL), F32),
                   jax.ShapeDtypeStruct((D_B, D_MODEL), F32),
                   jax.ShapeDtypeStruct((1, 2 * D_MODEL), F32), jax.ShapeDtypeStruct((1, D_MODEL), F32),
                   jax.ShapeDtypeStruct((1, 128), F32)),
        in_specs=[pl.BlockSpec((tm, D_A), lambda i: (i, 0)),
                  pl.BlockSpec((tm, D_B), lambda i: (i, 0)),
                  pl.BlockSpec((tm, 512), lambda i: (i, 0)),
                  pl.BlockSpec((tm, 512), lambda i: (i, 3)),
                  pl.BlockSpec((tm, D_MODEL), lambda i: (i, 2)),
                  pl.BlockSpec((tm, D_MODEL), lambda i: (i, 3)),
                  pl.BlockSpec((tm, D_MODEL), lambda i: (i, 0)),
                  pl.BlockSpec((tm, D_MODEL), lambda i: (i, 0)),
                  pl.BlockSpec((D_A, D_MODEL), c2), pl.BlockSpec((D_B, D_MODEL), c2),
                  pl.BlockSpec((D_MODEL, D_MODEL), c2),
                  pl.BlockSpec((1, 2 * D_MODEL), c2), pl.BlockSpec((1, D_MODEL), c2)],
        out_specs=[pl.BlockSpec((tm, D_MODEL), lambda i: (i, 0)),
                   pl.BlockSpec((N_HEADS, tm, HEAD_DIM), lambda i: (0, i, 0)),
                   pl.BlockSpec((tm, D_B), lambda i: (i, 0)),
                   pl.BlockSpec((tm, 3072), lambda i: (i, 0)),
                   hbm, hbm, hbm,
                   pl.BlockSpec((1, 2 * D_MODEL), c2), pl.BlockSpec((1, D_MODEL), c2),
                   pl.BlockSpec((1, 128), c2)],
        scratch_shapes=[pltpu.VMEM((D_MODEL, D_MODEL), F32), pltpu.VMEM((D_A, D_MODEL), F32),
                        pltpu.VMEM((D_B, D_MODEL), F32), pltpu.SemaphoreType.DMA((3,))],
        compiler_params=_params(56, dimension_semantics=("arbitrary",)),
    )(att, sg, zrest, zrest, zrest, zrest, x, target, w_pa, w_pb, w_out, b_gate, final_g)


_DZ_MAP = ((0, 0), (1, 0), (2, 0), (3, 0), (4, 0), (4, 1), (3, 1), (3, 2), (3, 3), (3, 4), (3, 5))


def _dh_gradx(dq, dk, dv, dzt, dzs, w_in_bf, x, norm_g, d_out, tm=256):
    S = x.shape[0]

    def body(dq_ref, dk_ref, dv_ref, dzt_ref, dzs_ref, w_ref, x_ref, g_ref, dout_ref, gx_ref, gn_ref):
        i = pl.program_id(0)

        @pl.when(i == 0)
        def _():
            gn_ref[...] = jnp.zeros_like(gn_ref)

        pieces = (dq_ref, dk_ref, dv_ref, dzt_ref, dzs_ref)
        dh = jnp.zeros((tm, D_MODEL), F32)
        for j, (pc, blk) in enumerate(_DZ_MAP):
            dh += _dot_nt(pieces[pc][:, blk * 512:(blk + 1) * 512], w_ref[:, j * 512:(j + 1) * 512])
        xv = x_ref[...]
        r = lax.rsqrt(jnp.mean(xv * xv, axis=-1, keepdims=True) + EPS)
        nrm = xv * r
        gn_ref[...] += jnp.sum(dh * nrm, axis=0, keepdims=True)
        dn = dh * g_ref[...]
        gx_ref[...] = r * (dn - nrm * jnp.mean(dn * nrm, axis=-1, keepdims=True)) + dout_ref[...]

    row = lambda w: pl.BlockSpec((tm, w), lambda i: (i, 0))
    c2 = lambda i: (0, 0)
    return pl.pallas_call(
        body, name="dh_gradx", grid=(S // tm,),
        out_shape=(jax.ShapeDtypeStruct((S, D_MODEL), F32), jax.ShapeDtypeStruct((1, D_MODEL), F32)),
        in_specs=[row(512), row(512), row(512), row(3072), row(1024),
                  pl.BlockSpec((D_MODEL, D_IN), c2), row(D_MODEL), pl.BlockSpec((1, D_MODEL), c2), row(D_MODEL)],
        out_specs=[row(D_MODEL), pl.BlockSpec((1, D_MODEL), c2)],
        compiler_params=_params(48, dimension_semantics=("arbitrary",)),
    )(dq, dk, dv, dzt, dzs, w_in_bf, x, norm_g, d_out)


def _gw_in(ht, dq, dk, dv, dzt, dzs, tk=1024):
    S = ht.shape[1]
    nk = S // tk

    def body(ht_ref, dq_ref, dk_ref, dv_ref, dzt_ref, dzs_ref, o_ref):
        j = pl.program_id(0)
        t = pl.program_id(1)
        pieces = (dq_ref, dk_ref, dv_ref, dzt_ref, dzs_ref)

        @pl.when(t == 0)
        def _():
            o_ref[...] = jnp.zeros_like(o_ref)

        for pc in range(5):
            hit = functools.reduce(jnp.logical_or, [j == jj for jj, (p, _) in enumerate(_DZ_MAP) if p == pc])

            @pl.when(hit)
            def _(pc=pc):
                o_ref[...] += _dot(ht_ref[...], pieces[pc][...])

    def piece_spec(pc):
        blocks = [jj for jj, (p, _) in enumerate(_DZ_MAP) if p == pc]
        table = [0] * len(_DZ_MAP)
        for jj, (p, blk) in enumerate(_DZ_MAP):
            table[jj] = blk if p == pc else None
        cur = _DZ_MAP[blocks[0]][1]
        filled = []
        for v in table:
            cur = cur if v is None else v
            filled.append(cur)

        def index_map(j, t):
            blk = jnp.int32(filled[0])
            for jj in range(1, len(filled)):
                blk = jnp.where(j >= jj, jnp.int32(filled[jj]), blk)
            return (t, blk)

        return pl.BlockSpec((tk, 512), index_map)

    return pl.pallas_call(
        body, name="gw_in", grid=(len(_DZ_MAP), nk),
        out_shape=jax.ShapeDtypeStruct((D_MODEL, D_IN), F32),
        in_specs=[pl.BlockSpec((D_MODEL, tk), lambda j, t: (0, t))] + [piece_spec(pc) for pc in range(5)],
        out_specs=pl.BlockSpec((D_MODEL, 512), lambda j, t: (0, j)),
        compiler_params=_params(40, dimension_semantics=("arbitrary", "arbitrary")),
    )(ht, dq, dk, dv, dzt, dzs)


def _grad_xchg1(grads):
    def body(g0, g1, g2, g3, r0, r1, r2, r3, send_sems, recv_sems):
        gs, rs = (g0, g1, g2, g3), (r0, r1, r2, r3)
        x, y, c, _ = _mesh_pos()
        sibling = (x, y, 1 - c)
        cps = []
        for w in range(4):
            for s in range(N_SHARD):
                cp = pltpu.make_async_remote_copy(
                    src_ref=_UNITS[w](gs[w], s, 1 - c), dst_ref=rs[w].at[s],
                    send_sem=send_sems.at[w, s], recv_sem=recv_sems.at[w, s],
                    device_id=sibling, device_id_type=MESH)
                cp.start()
                cps.append(cp)
        for cp in cps:
            cp.wait()

    hbm = pl.BlockSpec(memory_space=pl.ANY)
    return pl.pallas_call(
        body, name="grad_xchg1",
        out_shape=tuple(jax.ShapeDtypeStruct((N_SHARD,) + u, F32) for u in _UNIT_SHAPES),
        in_specs=[hbm] * 4, out_specs=[hbm] * 4,
        scratch_shapes=[pltpu.SemaphoreType.DMA((4, N_SHARD)), pltpu.SemaphoreType.DMA((4, N_SHARD))],
        compiler_params=_params(16),
    )(*grads)


def _grad_add1(w, g, recv, pos):
    ur, uc = _UNIT_SHAPES[w]
    if w == 3:
        g_map = lambda s, pos: (2 * s + pos[0], 0)
    else:
        g_map = lambda s, pos: (pos[0], s)

    def body(pos_ref, g_ref, r_ref, cs_ref, csb_ref):
        v = g_ref[...] + r_ref[0]
        cs_ref[0] = v
        csb_ref[0] = v.astype(BF)

    u3 = lambda s, pos: (s, 0, 0)
    return pl.pallas_call(
        body, name=f"grad_add1_{w}",
        grid_spec=pltpu.PrefetchScalarGridSpec(
            num_scalar_prefetch=1, grid=(N_SHARD,),
            in_specs=[pl.BlockSpec((ur, uc), g_map), pl.BlockSpec((1, ur, uc), u3)],
            out_specs=[pl.BlockSpec((1, ur, uc), u3), pl.BlockSpec((1, ur, uc), u3)]),
        out_shape=(jax.ShapeDtypeStruct((N_SHARD, ur, uc), F32), jax.ShapeDtypeStruct((N_SHARD, ur, uc), BF)),
        compiler_params=_params(40, dimension_semantics=("arbitrary",)),
    )(pos, g, recv)


def _grad_xchg2(csb):
    def body(c0, c1, c2, c3, r0, r1, r2, r3, send_sems, recv_sems):
        cs, rs = (c0, c1, c2, c3), (r0, r1, r2, r3)
        x, y, c, chips = _mesh_pos()
        cps = []
        for j, (cx, cy) in enumerate(chips):
            for w in range(4):
                cp = pltpu.make_async_remote_copy(
                    src_ref=cs[w].at[2 * cx + cy], dst_ref=rs[w].at[j],
                    send_sem=send_sems.at[w, j], recv_sem=recv_sems.at[w, j],
                    device_id=(cx, cy, c), device_id_type=MESH)
                cp.start()
                cps.append(cp)
        for cp in cps:
            cp.wait()

    hbm = pl.BlockSpec(memory_space=pl.ANY)
    return pl.pallas_call(
        body, name="grad_xchg2",
        out_shape=tuple(jax.ShapeDtypeStruct((3,) + u, BF) for u in _UNIT_SHAPES),
        in_specs=[hbm] * 4, out_specs=[hbm] * 4,
        scratch_shapes=[pltpu.SemaphoreType.DMA((4, 3)), pltpu.SemaphoreType.DMA((4, 3))],
        compiler_params=_params(16),
    )(*csb)


def _grad_add2(w, cs, recv, pos):
    ur, uc = _UNIT_SHAPES[w]
    tr = ur // 4 if w == 0 else ur

    def body(pos_ref, cs_ref, r_ref, o_ref):
        o_ref[...] = ((cs_ref[0] + r_ref[0].astype(F32)) + r_ref[1].astype(F32)) + r_ref[2].astype(F32)

    return pl.pallas_call(
        body, name=f"grad_add2_{w}",
        grid_spec=pltpu.PrefetchScalarGridSpec(
            num_scalar_prefetch=1, grid=(ur // tr,),
            in_specs=[pl.BlockSpec((1, tr, uc), lambda t, pos: (pos[1], t, 0)),
                      pl.BlockSpec((3, tr, uc), lambda t, pos: (0, t, 0))],
            out_specs=pl.BlockSpec((tr, uc), lambda t, pos: (t, 0))),
        out_shape=jax.ShapeDtypeStruct((ur, uc), F32),
        compiler_params=_params(32, dimension_semantics=("arbitrary",)),
    )(pos, cs, recv)


def _grad_xchg3(units):
    def body(u0, u1, u2, u3, o0, o1, o2, o3, send_sems, recv_sems, local_sems):
        us, os_ = (u0, u1, u2, u3), (o0, o1, o2, o3)
        x, y, c, _ = _mesh_pos()
        sibling = (x, y, 1 - c)
        cps = []
        for w in range(4):
            rows = _HALF_ROWS[w]
            mine = os_[w].at[pl.ds(_mo(c * rows, rows), rows), :]
            lc = pltpu.make_async_copy(us[w], mine, local_sems.at[w])
            lc.start()
            cps.append(lc)
            cp = pltpu.make_async_remote_copy(
                src_ref=us[w], dst_ref=mine, send_sem=send_sems.at[w], recv_sem=recv_sems.at[w],
                device_id=sibling, device_id_type=MESH)
            cp.start()
            cps.append(cp)
        for cp in cps:
            cp.wait()

    hbm = pl.BlockSpec(memory_space=pl.ANY)
    return pl.pallas_call(
        body, name="grad_xchg3",
        out_shape=tuple(jax.ShapeDtypeStruct(s, F32) for s in _SHARD_SHAPES),
        in_specs=[hbm] * 4, out_specs=[hbm] * 4,
        scratch_shapes=[pltpu.SemaphoreType.DMA((4,)), pltpu.SemaphoreType.DMA((4,)),
                        pltpu.SemaphoreType.DMA((4,))],
        compiler_params=_params(16),
    )(*units)


def _adamw_math(w, g, m, v):
    m = ADAM_B1 * m + (1.0 - ADAM_B1) * g
    v = ADAM_B2 * v + (1.0 - ADAM_B2) * (g * g)
    m_hat = m / ADAM_C1
    v_hat = v / ADAM_C2
    delta = -ADAM_LR * (m_hat / (jnp.sqrt(v_hat) + ADAM_EPS) + ADAM_WD * w)
    return delta, m, v


def _adamw(name, w, g, m, v, tr=256):
    rows, cols = w.shape

    def body(w_ref, g_ref, m_ref, v_ref, d_ref, nm_ref, nv_ref):
        d_ref[...], nm_ref[...], nv_ref[...] = _adamw_math(w_ref[...], g_ref[...], m_ref[...], v_ref[...])

    spec = pl.BlockSpec((tr, cols), lambda i: (i, 0))
    return pl.pallas_call(
        body, name=name, grid=(rows // tr,),
        out_shape=tuple(jax.ShapeDtypeStruct((rows, cols), F32) for _ in range(3)),
        in_specs=[spec] * 4, out_specs=[spec] * 3,
        compiler_params=_params(32, dimension_semantics=("arbitrary",)),
    )(w, g, m, v)


def _small_reduce_adamw(pg, pw, pm, pv):
    rows = pg.shape[0]

    def body(g_ref, w_ref, m_ref, v_ref, gs_ref, d_ref, nm_ref, nv_ref, gath, send_sems, recv_sems):
        x, y, c, chips = _mesh_pos()
        me, sibling = (x, y, c), (x, y, 1 - c)

        def blk(px, py, pc):
            return gath.at[4 * px + 2 * py + pc]

        def copy(k, block, to, src=None):
            return pltpu.make_async_remote_copy(
                src_ref=blk(*block) if src is None else src, dst_ref=blk(*block),
                send_sem=send_sems.at[k], recv_sem=recv_sems.at[k], device_id=to, device_id_type=MESH)

        gath[4 * x + 2 * y + c] = g_ref[...]
        first = [copy(0, me, sibling, src=g_ref)]
        first += [copy(1 + j, me, (*chip, c), src=g_ref) for j, chip in enumerate(chips)]
        for cp in first:
            cp.start()
        passed = [copy(4 + j, (*chip, c), sibling) for j, chip in enumerate(chips)]
        for j, chip in enumerate(chips):
            copy(1 + j, (*chip, c), me).wait_recv()
            passed[j].start()
        copy(0, sibling, me).wait_recv()
        for j, chip in enumerate(chips):
            copy(4 + j, (*chip, 1 - c), me).wait_recv()
        for cp in first + passed:
            cp.wait_send()
        total = gath[0]
        for k in range(1, 8):
            total = total + gath[k]
        gs_ref[...] = total
        d_ref[...], nm_ref[...], nv_ref[...] = _adamw_math(w_ref[...], total, m_ref[...], v_ref[...])

    vm = pl.BlockSpec(memory_space=pltpu.VMEM)
    return pl.pallas_call(
        body, name="small_reduce_adamw",
        out_shape=tuple(jax.ShapeDtypeStruct((rows, 128), F32) for _ in range(4)),
        in_specs=[vm] * 4, out_specs=[vm] * 4,
        scratch_shapes=[pltpu.VMEM((8, rows, 128), F32), pltpu.SemaphoreType.DMA((7,)),
                        pltpu.SemaphoreType.DMA((7,))],
        compiler_params=_params(32),
    )(pg, pw, pm, pv)


_REL_PAD = 384


def _rows(a):
    a = a.reshape(-1, 128)
    pad = (-a.shape[0]) % 8
    return jnp.pad(a, ((0, pad), (0, 0))) if pad else a


def _pack_small(norm_g, b_gate, rel_bias, ln_g, ln_b, w_s, b_s, final_g, loss_row):
    rel = jnp.pad(rel_bias.reshape(N_HEADS, N_REL), ((0, 0), (0, _REL_PAD - N_REL)))
    parts = [norm_g, b_gate, rel, ln_g, ln_b, w_s, b_s, final_g, loss_row]
    return jnp.concatenate([_rows(p) for p in parts], axis=0)


_SMALL_LAYOUT = (("norm_g", 8, 8), ("b_gate", 16, 16), ("rel_bias", 24, 24), ("sgu_ln_g", 4, 8),
                 ("sgu_ln_b", 4, 8), ("w_s", 512, 512), ("b_s", 4, 8), ("final_g", 8, 8), ("loss", 1, 8))


def _unpack_small(p):
    out, r = {}, 0
    for name, used, alloc in _SMALL_LAYOUT:
        out[name] = p[r:r + used]
        r += alloc
    return out


def _small_outputs(p):
    u = _unpack_small(p)
    return (u["norm_g"].reshape(1, D_MODEL), u["b_gate"].reshape(1, 2 * D_MODEL),
            u["rel_bias"].reshape(N_HEADS, _REL_PAD)[:, :N_REL].reshape(1, N_HEADS, N_REL),
            u["sgu_ln_g"].reshape(1, D_B), u["sgu_ln_b"].reshape(1, D_B),
            u["w_s"].reshape(1, N_GROUPS, 128, 128), u["b_s"].reshape(1, N_GROUPS, 128),
            u["final_g"].reshape(D_MODEL)), u["loss"]


def _bias_row(rel_bias):
    hi = rel_bias[:, N_REL - 1:N_REL]
    lo = rel_bias[:, 0:1]
    return jnp.concatenate([jnp.broadcast_to(hi, (N_HEADS, 384)), rel_bias[:, ::-1],
                            jnp.broadcast_to(lo, (N_HEADS, 191)), jnp.broadcast_to(hi, (N_HEADS, 192))], axis=1)


def kernel(x, norm_g, w_in, b_gate, rel_bias, sgu_ln_g, sgu_ln_b, w_s, b_s, w_pa, w_pb, w_out, final_g, loss_target, m_norm_g, m_w_in, m_b_gate, m_rel_bias, m_sgu_ln_g, m_sgu_ln_b, m_w_s, m_b_s, m_w_pa, m_w_pb, m_w_out, m_final_g, v_norm_g, v_w_in, v_b_gate, v_rel_bias, v_sgu_ln_g, v_sgu_ln_b, v_w_s, v_b_s, v_w_pa, v_w_pb, v_w_out, v_final_g):
    S = x.shape[1]
    xs = x.reshape(S, D_MODEL)
    tgt = loss_target.reshape(S, D_MODEL)
    big_w = (w_in[0], w_pa[0], w_pb[0], w_out[0])
    big_m = (m_w_in[0], m_w_pa[0], m_w_pb[0], m_w_out[0])
    big_v = (v_w_in[0], v_w_pa[0], v_w_pb[0], v_w_out[0])
    rel = rel_bias[0]
    ws = w_s[0]
    bst = b_s[0].T
    fg = final_g.reshape(1, D_MODEL)
    pos = jnp.stack([lax.axis_index("c"), 2 * lax.axis_index("x") + lax.axis_index("y")]).astype(jnp.int32)

    w_in_bf, w_pa_bf, w_pb_bf, w_out_bf = _ag_weights(*big_w)

    ht, q3, k3, v3, zrest = _inproj_fwd(xs, norm_g, w_in_bf)
    gp = _bias_row(rel)
    att = _attn_fwd(q3, k3, v3, gp)
    sg = _sgu_fwd(zrest, sgu_ln_g, sgu_ln_b, ws, bst)
    (d_out, d_att, d_sg, dzt, gw_out, gw_pa, gw_pb, g_bgate, g_final, loss_row) = _tail(
        att, sg, zrest, xs, tgt, w_pa_bf, w_pb_bf, w_out_bf, b_gate, fg)
    dq, dk, dv, d_gp = _attn_bwd(q3, k3, v3, d_att, gp)
    dzs, g_ws, g_bs_t, g_lng, g_lnb = _sgu_bwd(zrest, d_sg, sgu_ln_g, sgu_ln_b, ws, bst)
    grad_x, g_norm = _dh_gradx(dq, dk, dv, dzt, dzs, w_in_bf, xs, norm_g, d_out)
    gw_in = _gw_in(ht, dq, dk, dv, dzt, dzs)

    grads = (gw_in, gw_pa, gw_pb, gw_out)
    recv1 = _grad_xchg1(grads)
    sums = [_grad_add1(w, grads[w], recv1[w], pos) for w in range(4)]
    recv2 = _grad_xchg2(tuple(s[1] for s in sums))
    units = [_grad_add2(w, sums[w][0], recv2[w], pos) for w in range(4)]
    g_shards = _grad_xchg3(tuple(units))
    names = ("adamw_w_in", "adamw_w_pa", "adamw_w_pb", "adamw_w_out")
    big = [_adamw(names[w], big_w[w], g_shards[w], big_m[w], big_v[w],
                  tr=128 if w == 3 else 256) for w in range(4)]

    g_rel = d_gp[:, 384:384 + N_REL][:, ::-1]
    g_bs = g_bs_t[:, :N_GROUPS].T
    pg = _pack_small(g_norm, g_bgate, g_rel, g_lng, g_lnb, g_ws, g_bs, g_final, loss_row)
    zero_row = jnp.zeros((1, 128), F32)
    pw = _pack_small(norm_g, b_gate, rel, sgu_ln_g, sgu_ln_b, ws, b_s, final_g, zero_row)
    pm = _pack_small(m_norm_g, m_b_gate, m_rel_bias, m_sgu_ln_g, m_sgu_ln_b, m_w_s, m_b_s, m_final_g, zero_row)
    pv = _pack_small(v_norm_g, v_b_gate, v_rel_bias, v_sgu_ln_g, v_sgu_ln_b, v_w_s, v_b_s, v_final_g, zero_row)
    gsum, sdelta, sm, sv = _small_reduce_adamw(pg, pw, pm, pv)
    sg_out, loss_rows = _small_outputs(gsum)
    sd_out, _ = _small_outputs(sdelta)
    sm_out, _ = _small_outputs(sm)
    sv_out, _ = _small_outputs(sv)
    loss = loss_rows[0, 0]

    def assemble(small, bigs):
        n_g, b_g, r_b, l_g, l_b, w_s_, b_s_, f_g = small
        b_in, b_pa, b_pb, b_out = (b[None] for b in bigs)
        return (n_g, b_in, b_g, r_b, l_g, l_b, w_s_, b_s_, b_pa, b_pb, b_out, f_g)

    grads_out = assemble(sg_out, g_shards)
    delta_out = assemble(sd_out, [b[0] for b in big])
    m_out = assemble(sm_out, [b[1] for b in big])
    v_out = assemble(sv_out, [b[2] for b in big])
    return (loss, grad_x.reshape(1, S, D_MODEL), *grads_out, *delta_out, *m_out, *v_out)
```

```python
import functools
import math

import jax
import jax.numpy as jnp
from jax import lax
from jax.experimental import pallas as pl
from jax.experimental.pallas import tpu as pltpu

F32 = jnp.float32
BF = jnp.bfloat16
MESH = pl.DeviceIdType.MESH

D_MODEL = 1024
D_A = 512
D_B = 512
D_IN = 5632
N_HEADS = 8
HEAD_DIM = 64
CHUNK = 64
N_PREV = 8
SGU_CHUNK = 128
N_GROUPS = 4
N_REL = 257
EPS = 1e-6
NEG_INF = -1e30
SCALE = HEAD_DIM ** -0.5

QB = 2 * CHUNK
KB = (N_PREV + 2) * CHUNK
PADK = N_PREV * CHUNK
ROLL_W = 1024
KEEP = KB // QB - 1

ADAM_LR = 0.001
ADAM_B1 = 0.9
ADAM_B2 = 0.999
ADAM_EPS = 1e-08
ADAM_WD = 0.01
ADAM_STEP = 10
ADAM_C1 = 1.0 - ADAM_B1 ** ADAM_STEP
ADAM_C2 = 1.0 - ADAM_B2 ** ADAM_STEP

N_SHARD = 4
SHARD_IN = D_IN // N_SHARD
MIB = 1024 * 1024


def _params(vmem_mib, **kw):
    return pltpu.CompilerParams(vmem_limit_bytes=vmem_mib * MIB, **kw)


def _sigmoid(x):
    return 1.0 / (1.0 + jnp.exp(-x))


def _silu_and_grad(x):
    s = _sigmoid(x)
    return x * s, s * (1.0 + x * (1.0 - s))


_GELU_C = math.sqrt(2.0 / math.pi)
_GELU_A = 0.044715


def _gelu_and_grad(x):
    x2 = x * x
    t = jnp.tanh(_GELU_C * (x + _GELU_A * (x2 * x)))
    cdf = 0.5 * (1.0 + t)
    grad = cdf + 0.5 * x * (1.0 - t * t) * (_GELU_C * (1.0 + 3.0 * _GELU_A * x2))
    return x * cdf, grad


def _dot(a, b):
    return jnp.dot(a, b, preferred_element_type=F32)


def _dot_nt(a, b):
    return lax.dot_general(a, b, (((1,), (1,)), ((), ())), preferred_element_type=F32)


def _dot_tn(a, b):
    return lax.dot_general(a, b, (((0,), (0,)), ((), ())), preferred_element_type=F32)


def _mo(v, m):
    return v if isinstance(v, int) else pl.multiple_of(v, m)


def _unit_in(ref, s, p):
    return ref.at[pl.ds(_mo(p * 512, 512), 512), pl.ds(_mo(s * SHARD_IN, 128), SHARD_IN)]


def _unit_p(ref, s, p):
    return ref.at[pl.ds(_mo(p * 256, 256), 256), pl.ds(_mo(s * 256, 128), 256)]


def _unit_out(ref, s, p):
    return ref.at[pl.ds(_mo(s * 256 + p * 128, 128), 128), :]


_UNITS = (_unit_in, _unit_p, _unit_p, _unit_out)
_HALF_ROWS = (512, 256, 256, 128)
_UNIT_SHAPES = ((512, SHARD_IN), (256, 256), (256, 256), (128, D_MODEL))
_FULL_SHAPES = ((D_MODEL, D_IN), (D_A, D_MODEL), (D_B, D_MODEL), (D_MODEL, D_MODEL))
_SHARD_SHAPES = ((D_MODEL, SHARD_IN), (D_A, 256), (D_B, 256), (256, D_MODEL))


def _mesh_pos():
    x, y, c = lax.axis_index("x"), lax.axis_index("y"), lax.axis_index("c")
    chips = [(1 - x, y), (x, 1 - y), (1 - x, 1 - y)]
    return x, y, c, chips


def _ag_weights(w_in, w_pa, w_pb, w_out):
    def body(i0, i1, i2, i3, o0, o1, o2, o3, s0, s1, s2, s3, send_sems, recv_sems, local_sems):
        ins, outs, stage = (i0, i1, i2, i3), (o0, o1, o2, o3), (s0, s1, s2, s3)
        x, y, c, chips = _mesh_pos()
        s_me = 2 * x + y
        sibling = (x, y, 1 - c)
        for w in range(4):
            stage[w][...] = ins[w][...].astype(BF)

        def half(w, p):
            rows = _HALF_ROWS[w]
            return stage[w].at[pl.ds(_mo(p * rows, rows), rows), :]

        local = []
        for w in range(4):
            for p in range(2):
                cp = pltpu.make_async_copy(half(w, p), _UNITS[w](outs[w], s_me, p), local_sems.at[w, p])
                cp.start()
                local.append(cp)

        def rcopy(w, k, src, dst, to):
            return pltpu.make_async_remote_copy(src_ref=src, dst_ref=dst, send_sem=send_sems.at[w, k],
                                                recv_sem=recv_sems.at[w, k], device_id=to, device_id_type=MESH)

        sends = []
        for j, (cx, cy) in enumerate(chips):
            for w in range(4):
                cp = rcopy(w, j, half(w, c), _UNITS[w](outs[w], s_me, c), (cx, cy, c))
                cp.start()
                sends.append(cp)
        for j, (cx, cy) in enumerate(chips):
            s_j = 2 * cx + cy
            for w in range(4):
                landed = _UNITS[w](outs[w], s_j, c)
                rcopy(w, j, landed, landed, (cx, cy, c)).wait_recv()
                cp = rcopy(w, 3 + j, landed, landed, sibling)
                cp.start()
                sends.append(cp)
        for j, (cx, cy) in enumerate(chips):
            s_j = 2 * cx + cy
            for w in range(4):
                other = _UNITS[w](outs[w], s_j, 1 - c)
                rcopy(w, 3 + j, other, other, sibling).wait_recv()
        for cp in sends:
            cp.wait_send()
        for cp in local:
            cp.wait()

    vm = pl.BlockSpec(memory_space=pltpu.VMEM)
    hbm = pl.BlockSpec(memory_space=pl.ANY)
    return pl.pallas_call(
        body, name="ag_weights",
        out_shape=tuple(jax.ShapeDtypeStruct(s, BF) for s in _FULL_SHAPES),
        in_specs=[vm] * 4, out_specs=[hbm] * 4,
        scratch_shapes=[pltpu.VMEM(s, BF) for s in _SHARD_SHAPES]
        + [pltpu.SemaphoreType.DMA((4, 6)), pltpu.SemaphoreType.DMA((4, 6)), pltpu.SemaphoreType.DMA((4, 2))],
        compiler_params=_params(40),
    )(w_in, w_pa, w_pb, w_out)


def _inproj_fwd(x, norm_g, w_in_bf, tm=256):
    S = x.shape[0]

    def body(x_ref, g_ref, w_ref, ht_ref, q_ref, k_ref, v_ref, zr_ref):
        xv = x_ref[...]
        r = lax.rsqrt(jnp.mean(xv * xv, axis=-1, keepdims=True) + EPS)
        hf = (xv * r) * g_ref[...]
        ht_ref[...] = hf.T.astype(BF)
        h = hf.astype(BF)
        heads = (q_ref, k_ref, v_ref)
        for j in range(D_IN // 512):
            z = _dot(h, w_ref[:, j * 512:(j + 1) * 512])
            if j < 3:
                zb = z.astype(BF)
                for hd in range(N_HEADS):
                    heads[j][hd] = zb[:, hd * HEAD_DIM:(hd + 1) * HEAD_DIM]
            else:
                zr_ref[:, (j - 3) * 512:(j - 2) * 512] = z

    head_major = jax.ShapeDtypeStruct((N_HEADS, S, HEAD_DIM), BF)
    head_spec = pl.BlockSpec((N_HEADS, tm, HEAD_DIM), lambda i: (0, i, 0))
    return pl.pallas_call(
        body, name="inproj_fwd", grid=(S // tm,),
        out_shape=(jax.ShapeDtypeStruct((D_MODEL, S), BF), head_major, head_major, head_major,
                   jax.ShapeDtypeStruct((S, D_IN - 3 * D_A), F32)),
        in_specs=[pl.BlockSpec((tm, D_MODEL), lambda i: (i, 0)),
                  pl.BlockSpec((1, D_MODEL), lambda i: (0, 0)),
                  pl.BlockSpec((D_MODEL, D_IN), lambda i: (0, 0))],
        out_specs=[pl.BlockSpec((D_MODEL, tm), lambda i: (0, i)),
                   head_spec, head_spec, head_spec,
                   pl.BlockSpec((tm, D_IN - 3 * D_A), lambda i: (i, 0))],
        compiler_params=_params(52, dimension_semantics=("arbitrary",)),
    )(x, norm_g, w_in_bf)


def _skew_table(gp_row):
    row = lax.broadcasted_iota(jnp.int32, (QB, ROLL_W), 0)
    t = jnp.broadcast_to(gp_row, (QB, ROLL_W))
    for b in range(7):
        t = jnp.where(((row >> b) & 1) == 1, pltpu.roll(t, 1 << b, axis=1), t)
    return t


def _unskew_sum(d):
    row = lax.broadcasted_iota(jnp.int32, (QB, ROLL_W), 0)
    for b in range(7):
        d = jnp.where(((row >> b) & 1) == 1, pltpu.roll(d, ROLL_W - (1 << b), axis=1), d)
    return jnp.sum(d, axis=0, keepdims=True)


def _struct_mask():
    a = lax.broadcasted_iota(jnp.int32, (QB, KB), 0) // CHUNK
    b = lax.broadcasted_iota(jnp.int32, (QB, KB), 1) // CHUNK
    return (b >= a) & (b <= a + N_PREV)


def _load_kv(k_hbm, v_hbm, gp_ref, k_scr, v_scr, bias_scr, sems, S):
    zeros = jnp.zeros((N_HEADS, PADK, HEAD_DIM), BF)
    k_scr[:, 0:PADK, :] = zeros
    v_scr[:, 0:PADK, :] = zeros
    ck = pltpu.make_async_copy(k_hbm, k_scr.at[:, pl.ds(PADK, S), :], sems.at[0])
    cv = pltpu.make_async_copy(v_hbm, v_scr.at[:, pl.ds(PADK, S), :], sems.at[1])
    ck.start()
    cv.start()
    keep = _struct_mask()
    for h in range(N_HEADS):
        bias_scr[h] = jnp.where(keep, _skew_table(gp_ref[h:h + 1, :])[:, :KB], NEG_INF)
    ck.wait()
    cv.wait()


_BATCH_NT = (((2,), (2,)), ((0,), (0,)))
_BATCH_NN = (((2,), (1,)), ((0,), (0,)))
_BATCH_TN = (((1,), (1,)), ((0,), (0,)))


def _bdot(a, b, dims):
    return lax.dot_general(a, b, dims, preferred_element_type=F32)


def _probs(q, kb, bias, i, front):
    s = _bdot(q * jnp.asarray(SCALE, BF), kb, _BATCH_NT) + bias
    if front:
        col = lax.broadcasted_iota(jnp.int32, (1, 1, KB), 2)
        s = jnp.where(col >= PADK - i * QB, s, NEG_INF)
    m = jnp.max(s, axis=-1, keepdims=True)
    e = jnp.exp(s - m)
    return e * (1.0 / jnp.sum(e, axis=-1, keepdims=True))


def _attn_fwd(q3, k3, v3, gp):
    S = q3.shape[1]

    def body(q_ref, k_hbm, v_hbm, gp_ref, o_ref, k_scr, v_scr, bias_scr, sems):
        i = pl.program_id(0)

        @pl.when(i == 0)
        def _():
            _load_kv(k_hbm, v_hbm, gp_ref, k_scr, v_scr, bias_scr, sems, S)

        def step(front):
            start = pl.multiple_of(i * QB, QB)
            kb = k_scr[:, pl.ds(start, KB), :]
            vb = v_scr[:, pl.ds(start, KB), :]
            p = _probs(q_ref[...], kb, bias_scr[...], i, front)
            o = _bdot(p.astype(BF), vb, _BATCH_NN)
            for h in range(N_HEADS):
                o_ref[:, h * HEAD_DIM:(h + 1) * HEAD_DIM] = o[h]

        pl.when(i < KEEP)(functools.partial(step, True))
        pl.when(i >= KEEP)(functools.partial(step, False))

    kv_scr = pltpu.VMEM((N_HEADS, S + PADK, HEAD_DIM), BF)
    return pl.pallas_call(
        body, name="attn_fwd", grid=(S // QB,),
        out_shape=jax.ShapeDtypeStruct((S, D_A), F32),
        in_specs=[pl.BlockSpec((N_HEADS, QB, HEAD_DIM), lambda i: (0, i, 0)),
                  pl.BlockSpec(memory_space=pl.ANY), pl.BlockSpec(memory_space=pl.ANY),
                  pl.BlockSpec((N_HEADS, ROLL_W), lambda i: (0, 0))],
        out_specs=pl.BlockSpec((QB, D_A), lambda i: (i, 0)),
        scratch_shapes=[kv_scr, kv_scr, pltpu.VMEM((N_HEADS, QB, KB), F32), pltpu.SemaphoreType.DMA((2,))],
        compiler_params=_params(48, dimension_semantics=("arbitrary",)),
    )(q3, k3, v3, gp)


def _attn_bwd(q3, k3, v3, d_att3, gp, after=()):
    S = q3.shape[1]
    nq = S // QB

    def body(q_ref, do_ref, k_hbm, v_hbm, gp_ref, dq_ref, dk_ref, dv_ref, dgp_ref,
             k_scr, v_scr, bias_scr, dk_acc, dv_acc, dbias_acc, pad_scr, sems):
        i = pl.program_id(0)

        @pl.when(i == 0)
        def _():
            _load_kv(k_hbm, v_hbm, gp_ref, k_scr, v_scr, bias_scr, sems, S)
            dk_acc[...] = jnp.zeros_like(dk_acc)
            dv_acc[...] = jnp.zeros_like(dv_acc)
            dbias_acc[...] = jnp.zeros_like(dbias_acc)

        def step(front):
            start = pl.multiple_of(i * QB, QB)
            kb = k_scr[:, pl.ds(start, KB), :]
            vb = v_scr[:, pl.ds(start, KB), :]
            q = q_ref[...]
            do = do_ref[...]
            p = _probs(q, kb, bias_scr[...], i, front)
            dp = _bdot(do, vb, _BATCH_NT)
            ds = p * (dp - jnp.sum(dp * p, axis=-1, keepdims=True))
            dbias_acc[...] += ds
            dsb = (ds * SCALE).astype(BF)
            dq = _bdot(dsb, kb, _BATCH_NN)
            for h in range(N_HEADS):
                dq_ref[:, h * HEAD_DIM:(h + 1) * HEAD_DIM] = dq[h].astype(BF)
            dk_acc[...] += _bdot(dsb, q, _BATCH_TN)
            dv_acc[...] += _bdot(p.astype(BF), do, _BATCH_TN)

        pl.when(i < KEEP)(functools.partial(step, True))
        pl.when((i >= KEEP) & (i < nq))(functools.partial(step, False))

        for h in range(N_HEADS):
            hs = slice(h * HEAD_DIM, (h + 1) * HEAD_DIM)
            dk_ref[:, hs] = dk_acc[h, 0:QB, :].astype(BF)
            dv_ref[:, hs] = dv_acc[h, 0:QB, :].astype(BF)
        dk_acc[:, 0:KB - QB, :] = dk_acc[:, QB:KB, :]
        dv_acc[:, 0:KB - QB, :] = dv_acc[:, QB:KB, :]
        dk_acc[:, KB - QB:KB, :] = jnp.zeros((N_HEADS, QB, HEAD_DIM), F32)
        dv_acc[:, KB - QB:KB, :] = jnp.zeros((N_HEADS, QB, HEAD_DIM), F32)

        @pl.when(i == nq + KEEP - 1)
        def _():
            lane = lax.broadcasted_iota(jnp.int32, (1, ROLL_W), 1)
            hi = (lane < 384) | (lane >= 832)
            lo = (lane > 640) & (lane < 832)
            pad_scr[...] = jnp.zeros_like(pad_scr)
            for h in range(N_HEADS):
                pad_scr[:, 0:KB] = dbias_acc[h]
                g = _unskew_sum(pad_scr[...])
                s_hi = jnp.sum(jnp.where(hi, g, 0.0), axis=-1, keepdims=True)
                s_lo = jnp.sum(jnp.where(lo, g, 0.0), axis=-1, keepdims=True)
                g = jnp.where(lane == 384, g + s_hi, g)
                g = jnp.where(lane == 640, g + s_lo, g)
                dgp_ref[h:h + 1, :] = g

    last = nq - 1
    kv_scr = pltpu.VMEM((N_HEADS, S + PADK, HEAD_DIM), BF)
    return pl.pallas_call(
        _after(body, 5, after), name="attn_bwd", grid=(nq + KEEP,),
        out_shape=(jax.ShapeDtypeStruct((S, D_A), BF), jax.ShapeDtypeStruct((S, D_A), BF),
                   jax.ShapeDtypeStruct((S, D_A), BF), jax.ShapeDtypeStruct((N_HEADS, ROLL_W), F32)),
        in_specs=[pl.BlockSpec((N_HEADS, QB, HEAD_DIM), lambda i: (0, jnp.minimum(i, last), 0)),
                  pl.BlockSpec((N_HEADS, QB, HEAD_DIM), lambda i: (0, jnp.minimum(i, last), 0)),
                  pl.BlockSpec(memory_space=pl.ANY), pl.BlockSpec(memory_space=pl.ANY),
                  pl.BlockSpec((N_HEADS, ROLL_W), lambda i: (0, 0))] + [_ANY] * len(after),
        out_specs=[pl.BlockSpec((QB, D_A), lambda i: (jnp.minimum(i, last), 0)),
                   pl.BlockSpec((QB, D_A), lambda i: (jnp.maximum(i - KEEP, 0), 0)),
                   pl.BlockSpec((QB, D_A), lambda i: (jnp.maximum(i - KEEP, 0), 0)),
                   pl.BlockSpec((N_HEADS, ROLL_W), lambda i: (0, 0))],
        scratch_shapes=[kv_scr, kv_scr, pltpu.VMEM((N_HEADS, QB, KB), F32),
                        pltpu.VMEM((N_HEADS, KB, HEAD_DIM), F32), pltpu.VMEM((N_HEADS, KB, HEAD_DIM), F32),
                        pltpu.VMEM((N_HEADS, QB, KB), F32), pltpu.VMEM((QB, ROLL_W), F32),
                        pltpu.SemaphoreType.DMA((2,))],
        compiler_params=_params(56, dimension_semantics=("arbitrary",)),
    )(q3, d_att3, k3, v3, gp, *after)


def _sgu_core(ub, vb, lg, lb):
    u, du = _gelu_and_grad(ub)
    v, dv = _gelu_and_grad(vb)
    mu = jnp.mean(v, axis=-1, keepdims=True)
    vc = v - mu
    rstd = lax.rsqrt(jnp.mean(vc * vc, axis=-1, keepdims=True) + EPS)
    xh = vc * rstd
    vn = xh * lg + lb
    return u, du, dv, rstd, xh, vn


def _tri():
    r = lax.broadcasted_iota(jnp.int32, (SGU_CHUNK, SGU_CHUNK), 0)
    c = lax.broadcasted_iota(jnp.int32, (SGU_CHUNK, SGU_CHUNK), 1)
    return r >= c


def _sgu_fwd(zrest, ln_g, ln_b, w_s, b_s_t, tm=512):
    S = zrest.shape[0]

    def body(ub_ref, vb_ref, lg_ref, lb_ref, ws_ref, bst_ref, sg_ref):
        u, _, _, _, _, vn = _sgu_core(ub_ref[...], vb_ref[...], lg_ref[...], lb_ref[...])
        vnb = vn.astype(BF)
        tri = _tri()
        for g in range(N_GROUPS):
            cs = slice(g * 128, (g + 1) * 128)
            wt = jnp.where(tri, ws_ref[g], 0.0).astype(BF)
            bcol = bst_ref[:, g:g + 1]
            for n in range(tm // SGU_CHUNK):
                rs = slice(n * SGU_CHUNK, (n + 1) * SGU_CHUNK)
                mixed = _dot(wt, vnb[rs, cs]) + bcol
                sg_ref[rs, cs] = u[rs, cs] * mixed

    return pl.pallas_call(
        body, name="sgu_fwd", grid=(S // tm,),
        out_shape=jax.ShapeDtypeStruct((S, D_B), F32),
        in_specs=[pl.BlockSpec((tm, 512), lambda i: (i, 1)),
                  pl.BlockSpec((tm, 512), lambda i: (i, 2)),
                  pl.BlockSpec((1, D_B), lambda i: (0, 0)),
                  pl.BlockSpec((1, D_B), lambda i: (0, 0)),
                  pl.BlockSpec((N_GROUPS, 128, 128), lambda i: (0, 0, 0)),
                  pl.BlockSpec((128, N_GROUPS), lambda i: (0, 0))],
        out_specs=pl.BlockSpec((tm, D_B), lambda i: (i, 0)),
        compiler_params=_params(32, dimension_semantics=("arbitrary",)),
    )(zrest, zrest, ln_g, ln_b, w_s, b_s_t)


def _sgu_bwd(zrest, d_sg, ln_g, ln_b, w_s, b_s_t, tm=256, after=()):
    S = zrest.shape[0]
    nt = S // tm

    def body(ub_ref, vb_ref, dsg_ref, lg_ref, lb_ref, ws_ref, bst_ref,
             dzs_ref, gws_ref, gbs_ref, glg_ref, glb_ref, dvn_scr, bs_acc):
        i = pl.program_id(0)

        @pl.when(i == 0)
        def _():
            gws_ref[...] = jnp.zeros_like(gws_ref)
            glg_ref[...] = jnp.zeros_like(glg_ref)
            glb_ref[...] = jnp.zeros_like(glb_ref)
            bs_acc[...] = jnp.zeros_like(bs_acc)

        ub = ub_ref[...]
        u, du, dv, rstd, xh, vn = _sgu_core(ub, vb_ref[...], lg_ref[...], lb_ref[...])
        vnb = vn.astype(BF)
        dsg = dsg_ref[...]
        tri = _tri()
        for g in range(N_GROUPS):
            cs = slice(g * 128, (g + 1) * 128)
            wtf = jnp.where(tri, ws_ref[g], 0.0)
            wt = wtf.astype(BF)
            wtt = wtf.T.astype(BF)
            bcol = bst_ref[:, g:g + 1]
            for n in range(tm // SGU_CHUNK):
                rs = slice(n * SGU_CHUNK, (n + 1) * SGU_CHUNK)
                mixed = _dot(wt, vnb[rs, cs]) + bcol
                dzs_ref[rs, cs] = (dsg[rs, cs] * mixed * du[rs, cs]).astype(BF)
                dmix = dsg[rs, cs] * u[rs, cs]
                bs_acc[:, cs] += dmix
                dmb = dmix.astype(BF)
                gws_ref[g] += _dot_nt(dmb, vnb[rs, cs])
                dvn_scr[rs, cs] = _dot(wtt, dmb)
        dvn = dvn_scr[...]
        glg_ref[...] += jnp.sum(dvn * xh, axis=0, keepdims=True)
        glb_ref[...] += jnp.sum(dvn, axis=0, keepdims=True)
        dxh = dvn * lg_ref[...]
        dvv = rstd * (dxh - jnp.mean(dxh, axis=-1, keepdims=True)
                      - xh * jnp.mean(dxh * xh, axis=-1, keepdims=True))
        dzs_ref[:, D_B:2 * D_B] = (dvv * dv).astype(BF)

        @pl.when(i == nt - 1)
        def _():
            lane = lax.broadcasted_iota(jnp.int32, (SGU_CHUNK, 128), 1)
            out = jnp.zeros((SGU_CHUNK, 128), F32)
            for g in range(N_GROUPS):
                gws_ref[g] = jnp.where(tri, gws_ref[g], 0.0)
                col = jnp.sum(bs_acc[:, g * 128:(g + 1) * 128], axis=-1, keepdims=True)
                out = jnp.where(lane == g, col, out)
            gbs_ref[...] = out

    const2 = lambda i: (0, 0)
    return pl.pallas_call(
        _after(body, 7, after), name="sgu_bwd", grid=(nt,),
        out_shape=(jax.ShapeDtypeStruct((S, 2 * D_B), BF),
                   jax.ShapeDtypeStruct((N_GROUPS, 128, 128), F32),
                   jax.ShapeDtypeStruct((SGU_CHUNK, 128), F32),
                   jax.ShapeDtypeStruct((1, D_B), F32), jax.ShapeDtypeStruct((1, D_B), F32)),
        in_specs=[pl.BlockSpec((tm, 512), lambda i: (i, 1)),
                  pl.BlockSpec((tm, 512), lambda i: (i, 2)),
                  pl.BlockSpec((tm, D_B), lambda i: (i, 0)),
                  pl.BlockSpec((1, D_B), const2), pl.BlockSpec((1, D_B), const2),
                  pl.BlockSpec((N_GROUPS, 128, 128), lambda i: (0, 0, 0)),
                  pl.BlockSpec((128, N_GROUPS), const2)] + [_ANY] * len(after),
        out_specs=[pl.BlockSpec((tm, 2 * D_B), lambda i: (i, 0)),
                   pl.BlockSpec((N_GROUPS, 128, 128), lambda i: (0, 0, 0)),
                   pl.BlockSpec((SGU_CHUNK, 128), const2),
                   pl.BlockSpec((1, D_B), const2), pl.BlockSpec((1, D_B), const2)],
        scratch_shapes=[pltpu.VMEM((tm, D_B), F32), pltpu.VMEM((SGU_CHUNK, D_B), F32)],
        compiler_params=_params(32, dimension_semantics=("arbitrary",)),
    )(zrest, zrest, d_sg, ln_g, ln_b, w_s, b_s_t, *after)


def _tail(att, sg, zrest, x, target, w_pa, w_pb, w_out, b_gate, final_g, tm=256):
    S = x.shape[0]
    nt = S // tm

    def body(att_ref, sg_ref, ga_ref, gb_ref, gta_ref, gtb_ref, x_ref, t_ref,
             wpa_ref, wpb_ref, wout_ref, bg_ref, fg_ref,
             dout_ref, datt_ref, dsg_ref, dzt_ref, gwout_hbm, gwpa_hbm, gwpb_hbm,
             gbg_ref, gfg_ref, loss_ref, acc_out, acc_pa, acc_pb, sems):
        i = pl.program_id(0)

        @pl.when(i == 0)
        def _():
            acc_out[...] = jnp.zeros_like(acc_out)
            acc_pa[...] = jnp.zeros_like(acc_pa)
            acc_pb[...] = jnp.zeros_like(acc_pb)
            gbg_ref[...] = jnp.zeros_like(gbg_ref)
            gfg_ref[...] = jnp.zeros_like(gfg_ref)
            loss_ref[...] = jnp.zeros_like(loss_ref)

        att = att_ref[...]
        sg = sg_ref[...]
        sa, dsa = _silu_and_grad(ga_ref[...])
        sb, dsb = _silu_and_grad(gb_ref[...])
        ya = (att * sa).astype(BF)
        yb = (sg * sb).astype(BF)
        pa = _dot(ya, wpa_ref[...])
        pb = _dot(yb, wpb_ref[...])
        ga = _sigmoid(gta_ref[...] + bg_ref[:, 0:D_MODEL])
        gb = _sigmoid(gtb_ref[...] + bg_ref[:, D_MODEL:2 * D_MODEL])
        merged = (ga * pa + gb * pb).astype(BF)
        out = x_ref[...] + _dot(merged, wout_ref[...])
        r2 = lax.rsqrt(jnp.mean(out * out, axis=-1, keepdims=True) + EPS)
        nrm = out * r2
        fg = fg_ref[...]
        err = nrm * fg - t_ref[...]
        loss_ref[...] += 0.5 * jnp.sum(jnp.mean(err * err, axis=-1, keepdims=True))
        dy = err * (1.0 / D_MODEL)
        gfg_ref[...] += jnp.sum(dy * nrm, axis=0, keepdims=True)
        dn = dy * fg
        d_out = r2 * (dn - nrm * jnp.mean(dn * nrm, axis=-1, keepdims=True))
        dout_ref[...] = d_out
        d_outb = d_out.astype(BF)
        acc_out[...] += _dot_tn(merged, d_outb)
        dm = _dot_nt(d_outb, wout_ref[...])
        d_pa = (dm * ga).astype(BF)
        d_pb = (dm * gb).astype(BF)
        d_gta = dm * pa * (ga * (1.0 - ga))
        d_gtb = dm * pb * (gb * (1.0 - gb))
        gbg_ref[:, 0:D_MODEL] += jnp.sum(d_gta, axis=0, keepdims=True)
        gbg_ref[:, D_MODEL:2 * D_MODEL] += jnp.sum(d_gtb, axis=0, keepdims=True)
        dzt_ref[:, 2 * D_A:2 * D_A + D_MODEL] = d_gta.astype(BF)
        dzt_ref[:, 2 * D_A + D_MODEL:] = d_gtb.astype(BF)
        acc_pa[...] += _dot_tn(ya, d_pa)
        acc_pb[...] += _dot_tn(yb, d_pb)
        d_ya = _dot_nt(d_pa, wpa_ref[...])
        d_yb = _dot_nt(d_pb, wpb_ref[...])
        d_att = (d_ya * sa).astype(BF)
        for hd in range(N_HEADS):
            datt_ref[hd] = d_att[:, hd * HEAD_DIM:(hd + 1) * HEAD_DIM]
        dzt_ref[:, 0:D_A] = (d_ya * att * dsa).astype(BF)
        dsg_ref[...] = d_yb * sb
        dzt_ref[:, D_A:2 * D_A] = (d_yb * sg * dsb).astype(BF)

        @pl.when(i == nt - 1)
        def _():
            cps = [pltpu.make_async_copy(acc_out, gwout_hbm, sems.at[0]),
                   pltpu.make_async_copy(acc_pa, gwpa_hbm, sems.at[1]),
                   pltpu.make_async_copy(acc_pb, gwpb_hbm, sems.at[2])]
            for cp in cps:
                cp.start()
            for cp in cps:
                cp.wait()

    c2 = lambda i: (0, 0)
    hbm = pl.BlockSpec(memory_space=pl.ANY)
    return pl.pallas_call(
        body, name="tail", grid=(nt,),
        out_shape=(jax.ShapeDtypeStruct((S, D_MODEL), F32), jax.ShapeDtypeStruct((N_HEADS, S, HEAD_DIM), BF),
                   jax.ShapeDtypeStruct((S, D_B), F32), jax.ShapeDtypeStruct((S, 3072), BF),
                   jax.ShapeDtypeStruct((D_MODEL, D_MODEL), F32), jax.ShapeDtypeStruct((D_A, D_MODEL), F32),
                   jax.ShapeDtypeStruct((D_B, D_MODEL), F32),
                   jax.ShapeDtypeStruct((1, 2 * D_MODEL), F32), jax.ShapeDtypeStruct((1, D_MODEL), F32),
                   jax.ShapeDtypeStruct((1, 128), F32)),
        in_specs=[pl.BlockSpec((tm, D_A), lambda i: (i, 0)),
                  pl.BlockSpec((tm, D_B), lambda i: (i, 0)),
                  pl.BlockSpec((tm, 512), lambda i: (i, 0)),
                  pl.BlockSpec((tm, 512), lambda i: (i, 3)),
                  pl.BlockSpec((tm, D_MODEL), lambda i: (i, 2)),
                  pl.BlockSpec((tm, D_MODEL), lambda i: (i, 3)),
                  pl.BlockSpec((tm, D_MODEL), lambda i: (i, 0)),
                  pl.BlockSpec((tm, D_MODEL), lambda i: (i, 0)),
                  pl.BlockSpec((D_A, D_MODEL), c2), pl.BlockSpec((D_B, D_MODEL), c2),
                  pl.BlockSpec((D_MODEL, D_MODEL), c2),
                  pl.BlockSpec((1, 2 * D_MODEL), c2), pl.BlockSpec((1, D_MODEL), c2)],
        out_specs=[pl.BlockSpec((tm, D_MODEL), lambda i: (i, 0)),
                   pl.BlockSpec((N_HEADS, tm, HEAD_DIM), lambda i: (0, i, 0)),
                   pl.BlockSpec((tm, D_B), lambda i: (i, 0)),
                   pl.BlockSpec((tm, 3072), lambda i: (i, 0)),
                   hbm, hbm, hbm,
                   pl.BlockSpec((1, 2 * D_MODEL), c2), pl.BlockSpec((1, D_MODEL), c2),
                   pl.BlockSpec((1, 128), c2)],
        scratch_shapes=[pltpu.VMEM((D_MODEL, D_MODEL), F32), pltpu.VMEM((D_A, D_MODEL), F32),
                        pltpu.VMEM((D_B, D_MODEL), F32), pltpu.SemaphoreType.DMA((3,))],
        compiler_params=_params(56, dimension_semantics=("arbitrary",)),
    )(att, sg, zrest, zrest, zrest, zrest, x, target, w_pa, w_pb, w_out, b_gate, final_g)


_DZ_MAP = ((0, 0), (1, 0), (2, 0), (3, 0), (4, 0), (4, 1), (3, 1), (3, 2), (3, 3), (3, 4), (3, 5))


def _dh_gradx(dq, dk, dv, dzt, dzs, w_in_bf, x, norm_g, d_out, tm=256, after=()):
    S = x.shape[0]

    def body(dq_ref, dk_ref, dv_ref, dzt_ref, dzs_ref, w_ref, x_ref, g_ref, dout_ref, gx_ref, gn_ref):
        i = pl.program_id(0)

        @pl.when(i == 0)
        def _():
            gn_ref[...] = jnp.zeros_like(gn_ref)

        pieces = (dq_ref, dk_ref, dv_ref, dzt_ref, dzs_ref)
        dh = jnp.zeros((tm, D_MODEL), F32)
        for j, (pc, blk) in enumerate(_DZ_MAP):
            dh += _dot_nt(pieces[pc][:, blk * 512:(blk + 1) * 512], w_ref[:, j * 512:(j + 1) * 512])
        xv = x_ref[...]
        r = lax.rsqrt(jnp.mean(xv * xv, axis=-1, keepdims=True) + EPS)
        nrm = xv * r
        gn_ref[...] += jnp.sum(dh * nrm, axis=0, keepdims=True)
        dn = dh * g_ref[...]
        gx_ref[...] = r * (dn - nrm * jnp.mean(dn * nrm, axis=-1, keepdims=True)) + dout_ref[...]

    row = lambda w: pl.BlockSpec((tm, w), lambda i: (i, 0))
    c2 = lambda i: (0, 0)
    return pl.pallas_call(
        _after(body, 9, after), name="dh_gradx", grid=(S // tm,),
        out_shape=(jax.ShapeDtypeStruct((S, D_MODEL), F32), jax.ShapeDtypeStruct((1, D_MODEL), F32)),
        in_specs=[row(512), row(512), row(512), row(3072), row(1024),
                  pl.BlockSpec((D_MODEL, D_IN), c2), row(D_MODEL), pl.BlockSpec((1, D_MODEL), c2), row(D_MODEL)]
        + [_ANY] * len(after),
        out_specs=[row(D_MODEL), pl.BlockSpec((1, D_MODEL), c2)],
        compiler_params=_params(48, dimension_semantics=("arbitrary",)),
    )(dq, dk, dv, dzt, dzs, w_in_bf, x, norm_g, d_out, *after)


def _gw_in(ht, dq, dk, dv, dzt, dzs, tk=1024, after=()):
    S = ht.shape[1]
    nk = S // tk

    def body(ht_ref, dq_ref, dk_ref, dv_ref, dzt_ref, dzs_ref, o_ref):
        j = pl.program_id(0)
        t = pl.program_id(1)
        pieces = (dq_ref, dk_ref, dv_ref, dzt_ref, dzs_ref)

        @pl.when(t == 0)
        def _():
            o_ref[...] = jnp.zeros_like(o_ref)

        for pc in range(5):
            hit = functools.reduce(jnp.logical_or, [j == jj for jj, (p, _) in enumerate(_DZ_MAP) if p == pc])

            @pl.when(hit)
            def _(pc=pc):
                o_ref[...] += _dot(ht_ref[...], pieces[pc][...])

    def piece_spec(pc):
        blocks = [jj for jj, (p, _) in enumerate(_DZ_MAP) if p == pc]
        table = [0] * len(_DZ_MAP)
        for jj, (p, blk) in enumerate(_DZ_MAP):
            table[jj] = blk if p == pc else None
        cur = _DZ_MAP[blocks[0]][1]
        filled = []
        for v in table:
            cur = cur if v is None else v
            filled.append(cur)

        def index_map(j, t):
            blk = jnp.int32(filled[0])
            for jj in range(1, len(filled)):
                blk = jnp.where(j >= jj, jnp.int32(filled[jj]), blk)
            return (t, blk)

        return pl.BlockSpec((tk, 512), index_map)

    return pl.pallas_call(
        _after(body, 6, after), name="gw_in", grid=(len(_DZ_MAP), nk),
        out_shape=jax.ShapeDtypeStruct((D_MODEL, D_IN), F32),
        in_specs=[pl.BlockSpec((D_MODEL, tk), lambda j, t: (0, t))] + [piece_spec(pc) for pc in range(5)]
        + [_ANY] * len(after),
        out_specs=pl.BlockSpec((D_MODEL, 512), lambda j, t: (0, j)),
        compiler_params=_params(40, dimension_semantics=("arbitrary", "arbitrary")),
    )(ht, dq, dk, dv, dzt, dzs, *after)


_HBM = pl.BlockSpec(memory_space=pltpu.HBM)
_SEM = pl.BlockSpec(memory_space=pltpu.SEMAPHORE)
_ANY = pl.BlockSpec(memory_space=pl.ANY)
_EFFECT = pltpu.SideEffectType.DATAFLOW_SIDE_EFFECTING


def _in_hbm(a):
    return pltpu.with_memory_space_constraint(a, pltpu.HBM)


def _after(body, n_in, after):
    if not after:
        return body
    return lambda *refs: body(*refs[:n_in], *refs[n_in + len(after):])


class _Started:
    def __init__(self, send, recv, bufs, token):
        self.send, self.recv, self.bufs, self.token = send, recv, bufs, token


def _split_start(name, bufs, n_copies, copies):
    nb = len(bufs)

    def body(*refs):
        for cp in copies(refs[:nb], refs[nb], refs[nb + 1]):
            cp.start()
        refs[-1][...] = jnp.zeros_like(refs[-1])

    outs = pl.pallas_call(
        body, name=name,
        out_shape=(pltpu.SemaphoreType.DMA((n_copies,)), pltpu.SemaphoreType.DMA((n_copies,)),
                   *[pltpu.HBM(b.shape, b.dtype) for b in bufs], jax.ShapeDtypeStruct((8, 128), F32)),
        in_specs=[_HBM] * nb,
        out_specs=(_SEM, _SEM, *[_HBM] * nb, pl.BlockSpec(memory_space=pltpu.VMEM)),
        input_output_aliases={k: 2 + k for k in range(nb)},
        compiler_params=pltpu.CompilerParams(has_side_effects=_EFFECT),
    )(*[_in_hbm(b) for b in bufs])
    return _Started(outs[0], outs[1], list(outs[2:2 + nb]), outs[-1])


def _split_wait(name, started, copies, after):
    nb = len(started.bufs)

    def body(*refs):
        for cp in copies(refs[:nb], refs[nb], refs[nb + 1]):
            cp.wait_send()
            cp.wait_recv()

    return list(pl.pallas_call(
        body, name=name,
        out_shape=tuple(pltpu.HBM(b.shape, b.dtype) for b in started.bufs),
        in_specs=[_HBM] * nb + [_SEM, _SEM, _ANY],
        out_specs=tuple([_HBM] * nb),
        input_output_aliases={k: k for k in range(nb)},
        compiler_params=pltpu.CompilerParams(has_side_effects=_EFFECT),
    )(*started.bufs, started.send, started.recv, after))


def _x1_copies(ws):
    def copies(refs, send_sems, recv_sems):
        x, y, c, _ = _mesh_pos()
        out = []
        for k, w in enumerate(ws):
            for s in range(N_SHARD):
                out.append(pltpu.make_async_remote_copy(
                    src_ref=_UNITS[w](refs[k], s, 1 - c), dst_ref=refs[len(ws) + k].at[s],
                    send_sem=send_sems.at[N_SHARD * k + s], recv_sem=recv_sems.at[N_SHARD * k + s],
                    device_id=(x, y, 1 - c), device_id_type=MESH))
        return out
    return copies


def _x2_copies(n):
    def copies(refs, send_sems, recv_sems):
        x, y, c, chips = _mesh_pos()
        out = []
        for j, (cx, cy) in enumerate(chips):
            for k in range(n):
                out.append(pltpu.make_async_remote_copy(
                    src_ref=refs[k].at[2 * cx + cy], dst_ref=refs[n + k].at[j],
                    send_sem=send_sems.at[3 * k + j], recv_sem=recv_sems.at[3 * k + j],
                    device_id=(cx, cy, c), device_id_type=MESH))
        return out
    return copies


def _x3_copies(ws):
    def copies(refs, send_sems, recv_sems):
        x, y, c, _ = _mesh_pos()
        out = []
        for k, w in enumerate(ws):
            rows = _HALF_ROWS[w]
            mine = refs[k].at[pl.ds(_mo(c * rows, rows), rows), :]
            out.append(pltpu.make_async_remote_copy(
                src_ref=mine, dst_ref=mine, send_sem=send_sems.at[k], recv_sem=recv_sems.at[k],
                device_id=(x, y, 1 - c), device_id_type=MESH))
        return out
    return copies


def _x1_lands(ws):
    return [lax.empty((N_SHARD,) + _UNIT_SHAPES[w], F32) for w in ws]


def _x2_lands(ws):
    return [lax.empty((3,) + _UNIT_SHAPES[w], BF) for w in ws]


def _grad_add1(w, g, recv, pos):
    ur, uc = _UNIT_SHAPES[w]
    if w == 3:
        g_map = lambda s, pos: (2 * s + pos[0], 0)
    else:
        g_map = lambda s, pos: (pos[0], s)

    def body(pos_ref, g_ref, r_ref, cs_ref, csb_ref):
        v = g_ref[...] + r_ref[0]
        cs_ref[0] = v
        csb_ref[0] = v.astype(BF)

    u3 = lambda s, pos: (s, 0, 0)
    return pl.pallas_call(
        body, name=f"grad_add1_{w}",
        grid_spec=pltpu.PrefetchScalarGridSpec(
            num_scalar_prefetch=1, grid=(N_SHARD,),
            in_specs=[pl.BlockSpec((ur, uc), g_map), pl.BlockSpec((1, ur, uc), u3)],
            out_specs=[pl.BlockSpec((1, ur, uc), u3), pl.BlockSpec((1, ur, uc), u3)]),
        out_shape=(jax.ShapeDtypeStruct((N_SHARD, ur, uc), F32), jax.ShapeDtypeStruct((N_SHARD, ur, uc), BF)),
        compiler_params=_params(40, dimension_semantics=("arbitrary",)),
    )(pos, g, recv)


def _grad_add2(w, cs, recv, pos):
    ur, uc = _UNIT_SHAPES[w]
    tr = ur // 4 if w == 0 else ur
    nt = ur // tr

    def body(pos_ref, cs_ref, r_ref, o_ref):
        o_ref[...] = ((cs_ref[0] + r_ref[0].astype(F32)) + r_ref[1].astype(F32)) + r_ref[2].astype(F32)

    return pl.pallas_call(
        body, name=f"grad_add2_{w}",
        grid_spec=pltpu.PrefetchScalarGridSpec(
            num_scalar_prefetch=1, grid=(nt,),
            in_specs=[pl.BlockSpec((1, tr, uc), lambda t, pos: (pos[1], t, 0)),
                      pl.BlockSpec((3, tr, uc), lambda t, pos: (0, t, 0))],
            out_specs=pl.BlockSpec((tr, uc), lambda t, pos: (pos[0] * nt + t, 0))),
        out_shape=jax.ShapeDtypeStruct(_SHARD_SHAPES[w], F32),
        compiler_params=_params(32, dimension_semantics=("arbitrary",)),
    )(pos, cs, recv)


def _grad_xchg3(ws, halves):
    n = len(ws)

    def body(*refs):
        cps = _x3_copies(ws)(refs[:n], refs[2 * n], refs[2 * n + 1])
        for cp in cps:
            cp.start()
        for cp in cps:
            cp.wait()

    return pl.pallas_call(
        body, name="grad_xchg3",
        out_shape=tuple(jax.ShapeDtypeStruct(_SHARD_SHAPES[w], F32) for w in ws),
        in_specs=[_ANY] * n, out_specs=[_ANY] * n,
        input_output_aliases={k: k for k in range(n)},
        scratch_shapes=[pltpu.SemaphoreType.DMA((n,)), pltpu.SemaphoreType.DMA((n,))],
        compiler_params=_params(16),
    )(*halves)


def _adamw_math(w, g, m, v):
    m = ADAM_B1 * m + (1.0 - ADAM_B1) * g
    v = ADAM_B2 * v + (1.0 - ADAM_B2) * (g * g)
    m_hat = m / ADAM_C1
    v_hat = v / ADAM_C2
    delta = -ADAM_LR * (m_hat / (jnp.sqrt(v_hat) + ADAM_EPS) + ADAM_WD * w)
    return delta, m, v


def _adamw(name, w, g, m, v, tr=256, after=()):
    rows, cols = w.shape

    def body(w_ref, g_ref, m_ref, v_ref, d_ref, nm_ref, nv_ref):
        d_ref[...], nm_ref[...], nv_ref[...] = _adamw_math(w_ref[...], g_ref[...], m_ref[...], v_ref[...])

    spec = pl.BlockSpec((tr, cols), lambda i: (i, 0))
    return pl.pallas_call(
        _after(body, 4, after), name=name, grid=(rows // tr,),
        out_shape=tuple(jax.ShapeDtypeStruct((rows, cols), F32) for _ in range(3)),
        in_specs=[spec] * 4 + [_ANY] * len(after), out_specs=[spec] * 3,
        compiler_params=_params(32, dimension_semantics=("arbitrary",)),
    )(w, g, m, v, *after)


def _small_reduce_adamw(pg, pw, pm, pv, after=()):
    rows = pg.shape[0]

    def body(g_ref, w_ref, m_ref, v_ref, gs_ref, d_ref, nm_ref, nv_ref, gath, send_sems, recv_sems):
        x, y, c, chips = _mesh_pos()
        me, sibling = (x, y, c), (x, y, 1 - c)

        def blk(px, py, pc):
            return gath.at[4 * px + 2 * py + pc]

        def copy(k, block, to, src=None):
            return pltpu.make_async_remote_copy(
                src_ref=blk(*block) if src is None else src, dst_ref=blk(*block),
                send_sem=send_sems.at[k], recv_sem=recv_sems.at[k], device_id=to, device_id_type=MESH)

        gath[4 * x + 2 * y + c] = g_ref[...]
        first = [copy(0, me, sibling, src=g_ref)]
        first += [copy(1 + j, me, (*chip, c), src=g_ref) for j, chip in enumerate(chips)]
        for cp in first:
            cp.start()
        passed = [copy(4 + j, (*chip, c), sibling) for j, chip in enumerate(chips)]
        for j, chip in enumerate(chips):
            copy(1 + j, (*chip, c), me).wait_recv()
            passed[j].start()
        copy(0, sibling, me).wait_recv()
        for j, chip in enumerate(chips):
            copy(4 + j, (*chip, 1 - c), me).wait_recv()
        for cp in first + passed:
            cp.wait_send()
        total = gath[0]
        for k in range(1, 8):
            total = total + gath[k]
        gs_ref[...] = total
        d_ref[...], nm_ref[...], nv_ref[...] = _adamw_math(w_ref[...], total, m_ref[...], v_ref[...])

    vm = pl.BlockSpec(memory_space=pltpu.VMEM)
    return pl.pallas_call(
        _after(body, 4, after), name="small_reduce_adamw",
        out_shape=tuple(jax.ShapeDtypeStruct((rows, 128), F32) for _ in range(4)),
        in_specs=[vm] * 4 + [_ANY] * len(after), out_specs=[vm] * 4,
        scratch_shapes=[pltpu.VMEM((8, rows, 128), F32), pltpu.SemaphoreType.DMA((7,)),
                        pltpu.SemaphoreType.DMA((7,))],
        compiler_params=_params(32),
    )(pg, pw, pm, pv, *after)


_REL_PAD = 384


def _rows(a):
    a = a.reshape(-1, 128)
    pad = (-a.shape[0]) % 8
    return jnp.pad(a, ((0, pad), (0, 0))) if pad else a


def _pack_small(norm_g, b_gate, rel_bias, ln_g, ln_b, w_s, b_s, final_g, loss_row):
    rel = jnp.pad(rel_bias.reshape(N_HEADS, N_REL), ((0, 0), (0, _REL_PAD - N_REL)))
    parts = [norm_g, b_gate, rel, ln_g, ln_b, w_s, b_s, final_g, loss_row]
    return jnp.concatenate([_rows(p) for p in parts], axis=0)


_SMALL_LAYOUT = (("norm_g", 8, 8), ("b_gate", 16, 16), ("rel_bias", 24, 24), ("sgu_ln_g", 4, 8),
                 ("sgu_ln_b", 4, 8), ("w_s", 512, 512), ("b_s", 4, 8), ("final_g", 8, 8), ("loss", 1, 8))


def _unpack_small(p):
    out, r = {}, 0
    for name, used, alloc in _SMALL_LAYOUT:
        out[name] = p[r:r + used]
        r += alloc
    return out


def _small_outputs(p):
    u = _unpack_small(p)
    return (u["norm_g"].reshape(1, D_MODEL), u["b_gate"].reshape(1, 2 * D_MODEL),
            u["rel_bias"].reshape(N_HEADS, _REL_PAD)[:, :N_REL].reshape(1, N_HEADS, N_REL),
            u["sgu_ln_g"].reshape(1, D_B), u["sgu_ln_b"].reshape(1, D_B),
            u["w_s"].reshape(1, N_GROUPS, 128, 128), u["b_s"].reshape(1, N_GROUPS, 128),
            u["final_g"].reshape(D_MODEL)), u["loss"]


def _bias_row(rel_bias):
    hi = rel_bias[:, N_REL - 1:N_REL]
    lo = rel_bias[:, 0:1]
    return jnp.concatenate([jnp.broadcast_to(hi, (N_HEADS, 384)), rel_bias[:, ::-1],
                            jnp.broadcast_to(lo, (N_HEADS, 191)), jnp.broadcast_to(hi, (N_HEADS, 192))], axis=1)


def kernel(x, norm_g, w_in, b_gate, rel_bias, sgu_ln_g, sgu_ln_b, w_s, b_s, w_pa, w_pb, w_out, final_g, loss_target, m_norm_g, m_w_in, m_b_gate, m_rel_bias, m_sgu_ln_g, m_sgu_ln_b, m_w_s, m_b_s, m_w_pa, m_w_pb, m_w_out, m_final_g, v_norm_g, v_w_in, v_b_gate, v_rel_bias, v_sgu_ln_g, v_sgu_ln_b, v_w_s, v_b_s, v_w_pa, v_w_pb, v_w_out, v_final_g):
    S = x.shape[1]
    xs = x.reshape(S, D_MODEL)
    tgt = loss_target.reshape(S, D_MODEL)
    big_w = (w_in[0], w_pa[0], w_pb[0], w_out[0])
    big_m = (m_w_in[0], m_w_pa[0], m_w_pb[0], m_w_out[0])
    big_v = (v_w_in[0], v_w_pa[0], v_w_pb[0], v_w_out[0])
    rel = rel_bias[0]
    ws = w_s[0]
    bst = b_s[0].T
    fg = final_g.reshape(1, D_MODEL)
    pos = jnp.stack([lax.axis_index("c"), 2 * lax.axis_index("x") + lax.axis_index("y")]).astype(jnp.int32)

    w_in_bf, w_pa_bf, w_pb_bf, w_out_bf = _ag_weights(*big_w)

    ht, q3, k3, v3, zrest = _inproj_fwd(xs, norm_g, w_in_bf)
    gp = _bias_row(rel)
    att = _attn_fwd(q3, k3, v3, gp)
    sg = _sgu_fwd(zrest, sgu_ln_g, sgu_ln_b, ws, bst)
    (d_out, d_att, d_sg, dzt, gw_out, gw_pa, gw_pb, g_bgate, g_final, loss_row) = _tail(
        att, sg, zrest, xs, tgt, w_pa_bf, w_pb_bf, w_out_bf, b_gate, fg)
    ws_s, ws_i = (1, 2, 3), (0,)
    names = ("adamw_w_in", "adamw_w_pa", "adamw_w_pb", "adamw_w_out")

    x1s = _split_start("gx1s_start", [gw_pa, gw_pb, gw_out] + _x1_lands(ws_s), 12, _x1_copies(ws_s))
    dq, dk, dv, d_gp = _attn_bwd(q3, k3, v3, d_att, gp, after=(x1s.token,))
    got = _split_wait("gx1s_wait", x1s, _x1_copies(ws_s), dq)
    sums_s = [_grad_add1(w, got[k], got[3 + k], pos) for k, w in enumerate(ws_s)]

    x2s = _split_start("gx2s_start", [s[1] for s in sums_s] + _x2_lands(ws_s), 9, _x2_copies(3))
    dzs, g_ws, g_bs_t, g_lng, g_lnb = _sgu_bwd(zrest, d_sg, sgu_ln_g, sgu_ln_b, ws, bst, after=(x2s.token,))
    gw_in = _gw_in(ht, dq, dk, dv, dzt, dzs)
    got = _split_wait("gx2s_wait", x2s, _x2_copies(3), gw_in)
    halves_s = [_grad_add2(w, sums_s[k][0], got[3 + k], pos) for k, w in enumerate(ws_s)]

    x3s = _split_start("gx3s_start", halves_s, 3, _x3_copies(ws_s))
    x1i = _split_start("gx1i_start", [gw_in] + _x1_lands(ws_i), 4, _x1_copies(ws_i))
    grad_x, g_norm = _dh_gradx(dq, dk, dv, dzt, dzs, w_in_bf, xs, norm_g, d_out, after=(x3s.token, x1i.token))
    g_shards_s = _split_wait("gx3s_wait", x3s, _x3_copies(ws_s), grad_x)
    got = _split_wait("gx1i_wait", x1i, _x1_copies(ws_i), grad_x)
    sum_i = _grad_add1(0, got[0], got[1], pos)

    x2i = _split_start("gx2i_start", [sum_i[1]] + _x2_lands(ws_i), 3, _x2_copies(1))
    big = [None] * 4
    for k, w in enumerate(ws_s):
        big[w] = _adamw(names[w], big_w[w], g_shards_s[k], big_m[w], big_v[w],
                        tr=128 if w == 3 else 256, after=(x2i.token,))

    g_rel = d_gp[:, 384:384 + N_REL][:, ::-1]
    g_bs = g_bs_t[:, :N_GROUPS].T
    pg = _pack_small(g_norm, g_bgate, g_rel, g_lng, g_lnb, g_ws, g_bs, g_final, loss_row)
    zero_row = jnp.zeros((1, 128), F32)
    pw = _pack_small(norm_g, b_gate, rel, sgu_ln_g, sgu_ln_b, ws, b_s, final_g, zero_row)
    pm = _pack_small(m_norm_g, m_b_gate, m_rel_bias, m_sgu_ln_g, m_sgu_ln_b, m_w_s, m_b_s, m_final_g, zero_row)
    pv = _pack_small(v_norm_g, v_b_gate, v_rel_bias, v_sgu_ln_g, v_sgu_ln_b, v_w_s, v_b_s, v_final_g, zero_row)
    gsum, sdelta, sm, sv = _small_reduce_adamw(pg, pw, pm, pv, after=(x2i.token,))

    got = _split_wait("gx2i_wait", x2i, _x2_copies(1), gsum)
    half_i = _grad_add2(0, sum_i[0], got[1], pos)
    g_shard_i, = _grad_xchg3(ws_i, [half_i])
    big[0] = _adamw(names[0], big_w[0], g_shard_i, big_m[0], big_v[0])
    g_shards = [g_shard_i] + list(g_shards_s)
    sg_out, loss_rows = _small_outputs(gsum)
    sd_out, _ = _small_outputs(sdelta)
    sm_out, _ = _small_outputs(sm)
    sv_out, _ = _small_outputs(sv)
    loss = loss_rows[0, 0]

    def assemble(small, bigs):
        n_g, b_g, r_b, l_g, l_b, w_s_, b_s_, f_g = small
        b_in, b_pa, b_pb, b_out = (b[None] for b in bigs)
        return (n_g, b_in, b_g, r_b, l_g, l_b, w_s_, b_s_, b_pa, b_pb, b_out, f_g)

    grads_out = assemble(sg_out, g_shards)
    delta_out = assemble(sd_out, [b[0] for b in big])
    m_out = assemble(sm_out, [b[1] for b in big])
    v_out = assemble(sv_out, [b[2] for b in big])
    return (loss, grad_x.reshape(1, S, D_MODEL), *grads_out, *delta_out, *m_out, *v_out)
```

```python
import functools
import math

import jax
import jax.numpy as jnp
from jax import lax
from jax.experimental import pallas as pl
from jax.experimental.pallas import tpu as pltpu

F32 = jnp.float32
BF = jnp.bfloat16
MESH = pl.DeviceIdType.MESH

D_MODEL = 1024
D_A = 512
D_B = 512
D_IN = 5632
N_HEADS = 8
HEAD_DIM = 64
CHUNK = 64
N_PREV = 8
SGU_CHUNK = 128
N_GROUPS = 4
N_REL = 257
EPS = 1e-6
NEG_INF = -1e30
SCALE = HEAD_DIM ** -0.5

QB = 2 * CHUNK
KB = (N_PREV + 2) * CHUNK
PADK = N_PREV * CHUNK
ROLL_W = 1024
KEEP = KB // QB - 1

ADAM_LR = 0.001
ADAM_B1 = 0.9
ADAM_B2 = 0.999
ADAM_EPS = 1e-08
ADAM_WD = 0.01
ADAM_STEP = 10
ADAM_C1 = 1.0 - ADAM_B1 ** ADAM_STEP
ADAM_C2 = 1.0 - ADAM_B2 ** ADAM_STEP

N_SHARD = 4
SHARD_IN = D_IN // N_SHARD
MIB = 1024 * 1024


def _params(vmem_mib, **kw):
    return pltpu.CompilerParams(vmem_limit_bytes=vmem_mib * MIB, **kw)


def _sigmoid(x):
    return 1.0 / (1.0 + jnp.exp(-x))


def _silu_and_grad(x):
    s = _sigmoid(x)
    return x * s, s * (1.0 + x * (1.0 - s))


_GELU_C = math.sqrt(2.0 / math.pi)
_GELU_A = 0.044715


def _gelu_and_grad(x):
    x2 = x * x
    t = jnp.tanh(_GELU_C * (x + _GELU_A * (x2 * x)))
    cdf = 0.5 * (1.0 + t)
    grad = cdf + 0.5 * x * (1.0 - t * t) * (_GELU_C * (1.0 + 3.0 * _GELU_A * x2))
    return x * cdf, grad


def _dot(a, b):
    return jnp.dot(a, b, preferred_element_type=F32)


def _dot_nt(a, b):
    return lax.dot_general(a, b, (((1,), (1,)), ((), ())), preferred_element_type=F32)


def _dot_tn(a, b):
    return lax.dot_general(a, b, (((0,), (0,)), ((), ())), preferred_element_type=F32)


def _mo(v, m):
    return v if isinstance(v, int) else pl.multiple_of(v, m)


def _unit_in(ref, s, p):
    return ref.at[pl.ds(_mo(p * 512, 512), 512), pl.ds(_mo(s * SHARD_IN, 128), SHARD_IN)]


def _unit_p(ref, s, p):
    return ref.at[pl.ds(_mo(p * 256, 256), 256), pl.ds(_mo(s * 256, 128), 256)]


def _unit_out(ref, s, p):
    return ref.at[pl.ds(_mo(s * 256 + p * 128, 128), 128), :]


_UNITS = (_unit_in, _unit_p, _unit_p, _unit_out)
_HALF_ROWS = (512, 256, 256, 128)
_UNIT_SHAPES = ((512, SHARD_IN), (256, 256), (256, 256), (128, D_MODEL))
_FULL_SHAPES = ((D_MODEL, D_IN), (D_A, D_MODEL), (D_B, D_MODEL), (D_MODEL, D_MODEL))
_SHARD_SHAPES = ((D_MODEL, SHARD_IN), (D_A, 256), (D_B, 256), (256, D_MODEL))


def _mesh_pos():
    x, y, c = lax.axis_index("x"), lax.axis_index("y"), lax.axis_index("c")
    chips = [(1 - x, y), (x, 1 - y), (1 - x, 1 - y)]
    return x, y, c, chips


def _ag_weights(w_in, w_pa, w_pb, w_out):
    def body(i0, i1, i2, i3, o0, o1, o2, o3, s0, s1, s2, s3, send_sems, recv_sems, local_sems):
        ins, outs, stage = (i0, i1, i2, i3), (o0, o1, o2, o3), (s0, s1, s2, s3)
        x, y, c, chips = _mesh_pos()
        s_me = 2 * x + y
        sibling = (x, y, 1 - c)
        for w in range(4):
            stage[w][...] = ins[w][...].astype(BF)

        def half(w, p):
            rows = _HALF_ROWS[w]
            return stage[w].at[pl.ds(_mo(p * rows, rows), rows), :]

        local = []
        for w in range(4):
            for p in range(2):
                cp = pltpu.make_async_copy(half(w, p), _UNITS[w](outs[w], s_me, p), local_sems.at[w, p])
                cp.start()
                local.append(cp)

        def rcopy(w, k, src, dst, to):
            return pltpu.make_async_remote_copy(src_ref=src, dst_ref=dst, send_sem=send_sems.at[w, k],
                                                recv_sem=recv_sems.at[w, k], device_id=to, device_id_type=MESH)

        sends = []
        for j, (cx, cy) in enumerate(chips):
            for w in range(4):
                cp = rcopy(w, j, half(w, c), _UNITS[w](outs[w], s_me, c), (cx, cy, c))
                cp.start()
                sends.append(cp)
        for j, (cx, cy) in enumerate(chips):
            s_j = 2 * cx + cy
            for w in range(4):
                landed = _UNITS[w](outs[w], s_j, c)
                rcopy(w, j, landed, landed, (cx, cy, c)).wait_recv()
                cp = rcopy(w, 3 + j, landed, landed, sibling)
                cp.start()
                sends.append(cp)
        for j, (cx, cy) in enumerate(chips):
            s_j = 2 * cx + cy
            for w in range(4):
                other = _UNITS[w](outs[w], s_j, 1 - c)
                rcopy(w, 3 + j, other, other, sibling).wait_recv()
        for cp in sends:
            cp.wait_send()
        for cp in local:
            cp.wait()

    vm = pl.BlockSpec(memory_space=pltpu.VMEM)
    hbm = pl.BlockSpec(memory_space=pl.ANY)
    return pl.pallas_call(
        body, name="ag_weights",
        out_shape=tuple(jax.ShapeDtypeStruct(s, BF) for s in _FULL_SHAPES),
        in_specs=[vm] * 4, out_specs=[hbm] * 4,
        scratch_shapes=[pltpu.VMEM(s, BF) for s in _SHARD_SHAPES]
        + [pltpu.SemaphoreType.DMA((4, 6)), pltpu.SemaphoreType.DMA((4, 6)), pltpu.SemaphoreType.DMA((4, 2))],
        compiler_params=_params(40),
    )(w_in, w_pa, w_pb, w_out)


def _inproj_fwd(x, norm_g, w_in_bf, tm=256):
    S = x.shape[0]

    def body(x_ref, g_ref, w_ref, ht_ref, q_ref, k_ref, v_ref, zr_ref):
        xv = x_ref[...]
        r = lax.rsqrt(jnp.mean(xv * xv, axis=-1, keepdims=True) + EPS)
        hf = (xv * r) * g_ref[...]
        ht_ref[...] = hf.T.astype(BF)
        h = hf.astype(BF)
        heads = (q_ref, k_ref, v_ref)
        for j in range(D_IN // 512):
            z = _dot(h, w_ref[:, j * 512:(j + 1) * 512])
            if j < 3:
                zb = z.astype(BF)
                for hd in range(N_HEADS):
                    heads[j][hd] = zb[:, hd * HEAD_DIM:(hd + 1) * HEAD_DIM]
            else:
                zr_ref[:, (j - 3) * 512:(j - 2) * 512] = z

    head_major = jax.ShapeDtypeStruct((N_HEADS, S, HEAD_DIM), BF)
    head_spec = pl.BlockSpec((N_HEADS, tm, HEAD_DIM), lambda i: (0, i, 0))
    return pl.pallas_call(
        body, name="inproj_fwd", grid=(S // tm,),
        out_shape=(jax.ShapeDtypeStruct((D_MODEL, S), BF), head_major, head_major, head_major,
                   jax.ShapeDtypeStruct((S, D_IN - 3 * D_A), F32)),
        in_specs=[pl.BlockSpec((tm, D_MODEL), lambda i: (i, 0)),
                  pl.BlockSpec((1, D_MODEL), lambda i: (0, 0)),
                  pl.BlockSpec((D_MODEL, D_IN), lambda i: (0, 0))],
        out_specs=[pl.BlockSpec((D_MODEL, tm), lambda i: (0, i)),
                   head_spec, head_spec, head_spec,
                   pl.BlockSpec((tm, D_IN - 3 * D_A), lambda i: (i, 0))],
        compiler_params=_params(52, dimension_semantics=("arbitrary",)),
    )(x, norm_g, w_in_bf)


def _skew_table(gp_row):
    row = lax.broadcasted_iota(jnp.int32, (QB, ROLL_W), 0)
    t = jnp.broadcast_to(gp_row, (QB, ROLL_W))
    for b in range(7):
        t = jnp.where(((row >> b) & 1) == 1, pltpu.roll(t, 1 << b, axis=1), t)
    return t


def _unskew_sum(d):
    row = lax.broadcasted_iota(jnp.int32, (QB, ROLL_W), 0)
    for b in range(7):
        d = jnp.where(((row >> b) & 1) == 1, pltpu.roll(d, ROLL_W - (1 << b), axis=1), d)
    return jnp.sum(d, axis=0, keepdims=True)


def _struct_mask():
    a = lax.broadcasted_iota(jnp.int32, (QB, KB), 0) // CHUNK
    b = lax.broadcasted_iota(jnp.int32, (QB, KB), 1) // CHUNK
    return (b >= a) & (b <= a + N_PREV)


def _load_kv(k_hbm, v_hbm, gp_ref, k_scr, v_scr, bias_scr, sems, S):
    zeros = jnp.zeros((N_HEADS, PADK, HEAD_DIM), BF)
    k_scr[:, 0:PADK, :] = zeros
    v_scr[:, 0:PADK, :] = zeros
    ck = pltpu.make_async_copy(k_hbm, k_scr.at[:, pl.ds(PADK, S), :], sems.at[0])
    cv = pltpu.make_async_copy(v_hbm, v_scr.at[:, pl.ds(PADK, S), :], sems.at[1])
    ck.start()
    cv.start()
    keep = _struct_mask()
    for h in range(N_HEADS):
        bias_scr[h] = jnp.where(keep, _skew_table(gp_ref[h:h + 1, :])[:, :KB], NEG_INF)
    ck.wait()
    cv.wait()


_BATCH_NT = (((2,), (2,)), ((0,), (0,)))
_BATCH_NN = (((2,), (1,)), ((0,), (0,)))
_BATCH_TN = (((1,), (1,)), ((0,), (0,)))


def _bdot(a, b, dims):
    return lax.dot_general(a, b, dims, preferred_element_type=F32)


def _probs(q, kb, bias, i, front):
    s = _bdot(q * jnp.asarray(SCALE, BF), kb, _BATCH_NT) + bias
    if front:
        col = lax.broadcasted_iota(jnp.int32, (1, 1, KB), 2)
        s = jnp.where(col >= PADK - i * QB, s, NEG_INF)
    m = jnp.max(s, axis=-1, keepdims=True)
    e = jnp.exp(s - m)
    return e * (1.0 / jnp.sum(e, axis=-1, keepdims=True))


def _attn_fwd(q3, k3, v3, gp):
    S = q3.shape[1]

    def body(q_ref, k_hbm, v_hbm, gp_ref, o_ref, k_scr, v_scr, bias_scr, sems):
        i = pl.program_id(0)

        @pl.when(i == 0)
        def _():
            _load_kv(k_hbm, v_hbm, gp_ref, k_scr, v_scr, bias_scr, sems, S)

        def step(front):
            start = pl.multiple_of(i * QB, QB)
            kb = k_scr[:, pl.ds(start, KB), :]
            vb = v_scr[:, pl.ds(start, KB), :]
            p = _probs(q_ref[...], kb, bias_scr[...], i, front)
            o = _bdot(p.astype(BF), vb, _BATCH_NN)
            for h in range(N_HEADS):
                o_ref[:, h * HEAD_DIM:(h + 1) * HEAD_DIM] = o[h]

        pl.when(i < KEEP)(functools.partial(step, True))
        pl.when(i >= KEEP)(functools.partial(step, False))

    kv_scr = pltpu.VMEM((N_HEADS, S + PADK, HEAD_DIM), BF)
    return pl.pallas_call(
        body, name="attn_fwd", grid=(S // QB,),
        out_shape=jax.ShapeDtypeStruct((S, D_A), F32),
        in_specs=[pl.BlockSpec((N_HEADS, QB, HEAD_DIM), lambda i: (0, i, 0)),
                  pl.BlockSpec(memory_space=pl.ANY), pl.BlockSpec(memory_space=pl.ANY),
                  pl.BlockSpec((N_HEADS, ROLL_W), lambda i: (0, 0))],
        out_specs=pl.BlockSpec((QB, D_A), lambda i: (i, 0)),
        scratch_shapes=[kv_scr, kv_scr, pltpu.VMEM((N_HEADS, QB, KB), F32), pltpu.SemaphoreType.DMA((2,))],
        compiler_params=_params(48, dimension_semantics=("arbitrary",)),
    )(q3, k3, v3, gp)


def _attn_bwd(q3, k3, v3, d_att3, gp, after=()):
    S = q3.shape[1]
    nq = S // QB

    def body(q_ref, do_ref, k_hbm, v_hbm, gp_ref, dq_ref, dk_ref, dv_ref, dgp_ref,
             k_scr, v_scr, bias_scr, dk_acc, dv_acc, dbias_acc, pad_scr, sems):
        i = pl.program_id(0)

        @pl.when(i == 0)
        def _():
            _load_kv(k_hbm, v_hbm, gp_ref, k_scr, v_scr, bias_scr, sems, S)
            dk_acc[...] = jnp.zeros_like(dk_acc)
            dv_acc[...] = jnp.zeros_like(dv_acc)
            dbias_acc[...] = jnp.zeros_like(dbias_acc)

        def step(front):
            start = pl.multiple_of(i * QB, QB)
            kb = k_scr[:, pl.ds(start, KB), :]
            vb = v_scr[:, pl.ds(start, KB), :]
            q = q_ref[...]
            do = do_ref[...]
            p = _probs(q, kb, bias_scr[...], i, front)
            dp = _bdot(do, vb, _BATCH_NT)
            ds = p * (dp - jnp.sum(dp * p, axis=-1, keepdims=True))
            dbias_acc[...] += ds
            dsb = (ds * SCALE).astype(BF)
            dq = _bdot(dsb, kb, _BATCH_NN)
            for h in range(N_HEADS):
                dq_ref[:, h * HEAD_DIM:(h + 1) * HEAD_DIM] = dq[h].astype(BF)
            dk_acc[...] += _bdot(dsb, q, _BATCH_TN)
            dv_acc[...] += _bdot(p.astype(BF), do, _BATCH_TN)

        pl.when(i < KEEP)(functools.partial(step, True))
        pl.when((i >= KEEP) & (i < nq))(functools.partial(step, False))

        for h in range(N_HEADS):
            hs = slice(h * HEAD_DIM, (h + 1) * HEAD_DIM)
            dk_ref[:, hs] = dk_acc[h, 0:QB, :].astype(BF)
            dv_ref[:, hs] = dv_acc[h, 0:QB, :].astype(BF)
        dk_acc[:, 0:KB - QB, :] = dk_acc[:, QB:KB, :]
        dv_acc[:, 0:KB - QB, :] = dv_acc[:, QB:KB, :]
        dk_acc[:, KB - QB:KB, :] = jnp.zeros((N_HEADS, QB, HEAD_DIM), F32)
        dv_acc[:, KB - QB:KB, :] = jnp.zeros((N_HEADS, QB, HEAD_DIM), F32)

        @pl.when(i == nq + KEEP - 1)
        def _():
            lane = lax.broadcasted_iota(jnp.int32, (1, ROLL_W), 1)
            hi = (lane < 384) | (lane >= 832)
            lo = (lane > 640) & (lane < 832)
            pad_scr[...] = jnp.zeros_like(pad_scr)
            for h in range(N_HEADS):
                pad_scr[:, 0:KB] = dbias_acc[h]
                g = _unskew_sum(pad_scr[...])
                s_hi = jnp.sum(jnp.where(hi, g, 0.0), axis=-1, keepdims=True)
                s_lo = jnp.sum(jnp.where(lo, g, 0.0), axis=-1, keepdims=True)
                g = jnp.where(lane == 384, g + s_hi, g)
                g = jnp.where(lane == 640, g + s_lo, g)
                dgp_ref[h:h + 1, :] = g

    last = nq - 1
    kv_scr = pltpu.VMEM((N_HEADS, S + PADK, HEAD_DIM), BF)
    return pl.pallas_call(
        _after(body, 5, after), name="attn_bwd", grid=(nq + KEEP,),
        out_shape=(jax.ShapeDtypeStruct((S, D_A), BF), jax.ShapeDtypeStruct((S, D_A), BF),
                   jax.ShapeDtypeStruct((S, D_A), BF), jax.ShapeDtypeStruct((N_HEADS, ROLL_W), F32)),
        in_specs=[pl.BlockSpec((N_HEADS, QB, HEAD_DIM), lambda i: (0, jnp.minimum(i, last), 0)),
                  pl.BlockSpec((N_HEADS, QB, HEAD_DIM), lambda i: (0, jnp.minimum(i, last), 0)),
                  pl.BlockSpec(memory_space=pl.ANY), pl.BlockSpec(memory_space=pl.ANY),
                  pl.BlockSpec((N_HEADS, ROLL_W), lambda i: (0, 0))] + [_ANY] * len(after),
        out_specs=[pl.BlockSpec((QB, D_A), lambda i: (jnp.minimum(i, last), 0)),
                   pl.BlockSpec((QB, D_A), lambda i: (jnp.maximum(i - KEEP, 0), 0)),
                   pl.BlockSpec((QB, D_A), lambda i: (jnp.maximum(i - KEEP, 0), 0)),
                   pl.BlockSpec((N_HEADS, ROLL_W), lambda i: (0, 0))],
        scratch_shapes=[kv_scr, kv_scr, pltpu.VMEM((N_HEADS, QB, KB), F32),
                        pltpu.VMEM((N_HEADS, KB, HEAD_DIM), F32), pltpu.VMEM((N_HEADS, KB, HEAD_DIM), F32),
                        pltpu.VMEM((N_HEADS, QB, KB), F32), pltpu.VMEM((QB, ROLL_W), F32),
                        pltpu.SemaphoreType.DMA((2,))],
        compiler_params=_params(56, dimension_semantics=("arbitrary",)),
    )(q3, d_att3, k3, v3, gp, *after)


def _sgu_core(ub, vb, lg, lb):
    u, du = _gelu_and_grad(ub)
    v, dv = _gelu_and_grad(vb)
    mu = jnp.mean(v, axis=-1, keepdims=True)
    vc = v - mu
    rstd = lax.rsqrt(jnp.mean(vc * vc, axis=-1, keepdims=True) + EPS)
    xh = vc * rstd
    vn = xh * lg + lb
    return u, du, dv, rstd, xh, vn


def _tri():
    r = lax.broadcasted_iota(jnp.int32, (SGU_CHUNK, SGU_CHUNK), 0)
    c = lax.broadcasted_iota(jnp.int32, (SGU_CHUNK, SGU_CHUNK), 1)
    return r >= c


def _sgu_fwd(zrest, ln_g, ln_b, w_s, b_s_t, tm=512):
    S = zrest.shape[0]

    def body(ub_ref, vb_ref, lg_ref, lb_ref, ws_ref, bst_ref, sg_ref):
        u, _, _, _, _, vn = _sgu_core(ub_ref[...], vb_ref[...], lg_ref[...], lb_ref[...])
        vnb = vn.astype(BF)
        tri = _tri()
        for g in range(N_GROUPS):
            cs = slice(g * 128, (g + 1) * 128)
            wt = jnp.where(tri, ws_ref[g], 0.0).astype(BF)
            bcol = bst_ref[:, g:g + 1]
            for n in range(tm // SGU_CHUNK):
                rs = slice(n * SGU_CHUNK, (n + 1) * SGU_CHUNK)
                mixed = _dot(wt, vnb[rs, cs]) + bcol
                sg_ref[rs, cs] = u[rs, cs] * mixed

    return pl.pallas_call(
        body, name="sgu_fwd", grid=(S // tm,),
        out_shape=jax.ShapeDtypeStruct((S, D_B), F32),
        in_specs=[pl.BlockSpec((tm, 512), lambda i: (i, 1)),
                  pl.BlockSpec((tm, 512), lambda i: (i, 2)),
                  pl.BlockSpec((1, D_B), lambda i: (0, 0)),
                  pl.BlockSpec((1, D_B), lambda i: (0, 0)),
                  pl.BlockSpec((N_GROUPS, 128, 128), lambda i: (0, 0, 0)),
                  pl.BlockSpec((128, N_GROUPS), lambda i: (0, 0))],
        out_specs=pl.BlockSpec((tm, D_B), lambda i: (i, 0)),
        compiler_params=_params(32, dimension_semantics=("arbitrary",)),
    )(zrest, zrest, ln_g, ln_b, w_s, b_s_t)


def _sgu_bwd(zrest, d_sg, ln_g, ln_b, w_s, b_s_t, tm=256, after=()):
    S = zrest.shape[0]
    nt = S // tm

    def body(ub_ref, vb_ref, dsg_ref, lg_ref, lb_ref, ws_ref, bst_ref,
             dzs_ref, gws_ref, gbs_ref, glg_ref, glb_ref, dvn_scr, bs_acc):
        i = pl.program_id(0)

        @pl.when(i == 0)
        def _():
            gws_ref[...] = jnp.zeros_like(gws_ref)
            glg_ref[...] = jnp.zeros_like(glg_ref)
            glb_ref[...] = jnp.zeros_like(glb_ref)
            bs_acc[...] = jnp.zeros_like(bs_acc)

        ub = ub_ref[...]
        u, du, dv, rstd, xh, vn = _sgu_core(ub, vb_ref[...], lg_ref[...], lb_ref[...])
        vnb = vn.astype(BF)
        dsg = dsg_ref[...]
        tri = _tri()
        for g in range(N_GROUPS):
            cs = slice(g * 128, (g + 1) * 128)
            wtf = jnp.where(tri, ws_ref[g], 0.0)
            wt = wtf.astype(BF)
            wtt = wtf.T.astype(BF)
            bcol = bst_ref[:, g:g + 1]
            for n in range(tm // SGU_CHUNK):
                rs = slice(n * SGU_CHUNK, (n + 1) * SGU_CHUNK)
                mixed = _dot(wt, vnb[rs, cs]) + bcol
                dzs_ref[rs, cs] = (dsg[rs, cs] * mixed * du[rs, cs]).astype(BF)
                dmix = dsg[rs, cs] * u[rs, cs]
                bs_acc[:, cs] += dmix
                dmb = dmix.astype(BF)
                gws_ref[g] += _dot_nt(dmb, vnb[rs, cs])
                dvn_scr[rs, cs] = _dot(wtt, dmb)
        dvn = dvn_scr[...]
        glg_ref[...] += jnp.sum(dvn * xh, axis=0, keepdims=True)
        glb_ref[...] += jnp.sum(dvn, axis=0, keepdims=True)
        dxh = dvn * lg_ref[...]
        dvv = rstd * (dxh - jnp.mean(dxh, axis=-1, keepdims=True)
                      - xh * jnp.mean(dxh * xh, axis=-1, keepdims=True))
        dzs_ref[:, D_B:2 * D_B] = (dvv * dv).astype(BF)

        @pl.when(i == nt - 1)
        def _():
            lane = lax.broadcasted_iota(jnp.int32, (SGU_CHUNK, 128), 1)
            out = jnp.zeros((SGU_CHUNK, 128), F32)
            for g in range(N_GROUPS):
                gws_ref[g] = jnp.where(tri, gws_ref[g], 0.0)
                col = jnp.sum(bs_acc[:, g * 128:(g + 1) * 128], axis=-1, keepdims=True)
                out = jnp.where(lane == g, col, out)
            gbs_ref[...] = out

    const2 = lambda i: (0, 0)
    return pl.pallas_call(
        _after(body, 7, after), name="sgu_bwd", grid=(nt,),
        out_shape=(jax.ShapeDtypeStruct((S, 2 * D_B), BF),
                   jax.ShapeDtypeStruct((N_GROUPS, 128, 128), F32),
                   jax.ShapeDtypeStruct((SGU_CHUNK, 128), F32),
                   jax.ShapeDtypeStruct((1, D_B), F32), jax.ShapeDtypeStruct((1, D_B), F32)),
        in_specs=[pl.BlockSpec((tm, 512), lambda i: (i, 1)),
                  pl.BlockSpec((tm, 512), lambda i: (i, 2)),
                  pl.BlockSpec((tm, D_B), lambda i: (i, 0)),
                  pl.BlockSpec((1, D_B), const2), pl.BlockSpec((1, D_B), const2),
                  pl.BlockSpec((N_GROUPS, 128, 128), lambda i: (0, 0, 0)),
                  pl.BlockSpec((128, N_GROUPS), const2)] + [_ANY] * len(after),
        out_specs=[pl.BlockSpec((tm, 2 * D_B), lambda i: (i, 0)),
                   pl.BlockSpec((N_GROUPS, 128, 128), lambda i: (0, 0, 0)),
                   pl.BlockSpec((SGU_CHUNK, 128), const2),
                   pl.BlockSpec((1, D_B), const2), pl.BlockSpec((1, D_B), const2)],
        scratch_shapes=[pltpu.VMEM((tm, D_B), F32), pltpu.VMEM((SGU_CHUNK, D_B), F32)],
        compiler_params=_params(32, dimension_semantics=("arbitrary",)),
    )(zrest, zrest, d_sg, ln_g, ln_b, w_s, b_s_t, *after)


def _tail(att, sg, zrest, x, target, w_pa, w_pb, w_out, b_gate, final_g, tm=256):
    S = x.shape[0]
    nt = S // tm

    def body(att_ref, sg_ref, ga_ref, gb_ref, gta_ref, gtb_ref, x_ref, t_ref,
             wpa_ref, wpb_ref, wout_ref, bg_ref, fg_ref,
             dout_ref, datt_ref, dsg_ref, dzt_ref, gwout_hbm, gwpa_hbm, gwpb_hbm,
             gbg_ref, gfg_ref, loss_ref, acc_out, acc_pa, acc_pb, sems):
        i = pl.program_id(0)

        @pl.when(i == 0)
        def _():
            acc_out[...] = jnp.zeros_like(acc_out)
            acc_pa[...] = jnp.zeros_like(acc_pa)
            acc_pb[...] = jnp.zeros_like(acc_pb)
            gbg_ref[...] = jnp.zeros_like(gbg_ref)
            gfg_ref[...] = jnp.zeros_like(gfg_ref)
            loss_ref[...] = jnp.zeros_like(loss_ref)

        att = att_ref[...]
        sg = sg_ref[...]
        sa, dsa = _silu_and_grad(ga_ref[...])
        sb, dsb = _silu_and_grad(gb_ref[...])
        ya = (att * sa).astype(BF)
        yb = (sg * sb).astype(BF)
        pa = _dot(ya, wpa_ref[...])
        pb = _dot(yb, wpb_ref[...])
        ga = _sigmoid(gta_ref[...] + bg_ref[:, 0:D_MODEL])
        gb = _sigmoid(gtb_ref[...] + bg_ref[:, D_MODEL:2 * D_MODEL])
        merged = (ga * pa + gb * pb).astype(BF)
        out = x_ref[...] + _dot(merged, wout_ref[...])
        r2 = lax.rsqrt(jnp.mean(out * out, axis=-1, keepdims=True) + EPS)
        nrm = out * r2
        fg = fg_ref[...]
        err = nrm * fg - t_ref[...]
        loss_ref[...] += 0.5 * jnp.sum(jnp.mean(err * err, axis=-1, keepdims=True))
        dy = err * (1.0 / D_MODEL)
        gfg_ref[...] += jnp.sum(dy * nrm, axis=0, keepdims=True)
        dn = dy * fg
        d_out = r2 * (dn - nrm * jnp.mean(dn * nrm, axis=-1, keepdims=True))
        dout_ref[...] = d_out
        d_outb = d_out.astype(BF)
        acc_out[...] += _dot_tn(merged, d_outb)
        dm = _dot_nt(d_outb, wout_ref[...])
        d_pa = (dm * ga).astype(BF)
        d_pb = (dm * gb).astype(BF)
        d_gta = dm * pa * (ga * (1.0 - ga))
        d_gtb = dm * pb * (gb * (1.0 - gb))
        gbg_ref[:, 0:D_MODEL] += jnp.sum(d_gta, axis=0, keepdims=True)
        gbg_ref[:, D_MODEL:2 * D_MODEL] += jnp.sum(d_gtb, axis=0, keepdims=True)
        dzt_ref[:, 2 * D_A:2 * D_A + D_MODEL] = d_gta.astype(BF)
        dzt_ref[:, 2 * D_A + D_MODEL:] = d_gtb.astype(BF)
        acc_pa[...] += _dot_tn(ya, d_pa)
        acc_pb[...] += _dot_tn(yb, d_pb)
        d_ya = _dot_nt(d_pa, wpa_ref[...])
        d_yb = _dot_nt(d_pb, wpb_ref[...])
        d_att = (d_ya * sa).astype(BF)
        for hd in range(N_HEADS):
            datt_ref[hd] = d_att[:, hd * HEAD_DIM:(hd + 1) * HEAD_DIM]
        dzt_ref[:, 0:D_A] = (d_ya * att * dsa).astype(BF)
        dsg_ref[...] = d_yb * sb
        dzt_ref[:, D_A:2 * D_A] = (d_yb * sg * dsb).astype(BF)

        @pl.when(i == nt - 1)
        def _():
            cps = [pltpu.make_async_copy(acc_out, gwout_hbm, sems.at[0]),
                   pltpu.make_async_copy(acc_pa, gwpa_hbm, sems.at[1]),
                   pltpu.make_async_copy(acc_pb, gwpb_hbm, sems.at[2])]
            for cp in cps:
                cp.start()
            for cp in cps:
                cp.wait()

    c2 = lambda i: (0, 0)
    hbm = pl.BlockSpec(memory_space=pl.ANY)
    return pl.pallas_call(
        body, name="tail", grid=(nt,),
        out_shape=(jax.ShapeDtypeStruct((S, D_MODEL), F32), jax.ShapeDtypeStruct((N_HEADS, S, HEAD_DIM), BF),
                   jax.ShapeDtypeStruct((S, D_B), F32), jax.ShapeDtypeStruct((S, 3072), BF),
                   jax.ShapeDtypeStruct((D_MODEL, D_MODEL), F32), jax.ShapeDtypeStruct((D_A, D_MODEL), F32),
                   jax.ShapeDtypeStruct((D_B, D_MODEL), F32),
                   jax.ShapeDtypeStruct((1, 2 * D_MODEL), F32), jax.ShapeDtypeStruct((1, D_MODEL), F32),
                   jax.ShapeDtypeStruct((1, 128), F32)),
        in_specs=[pl.BlockSpec((tm, D_A), lambda i: (i, 0)),
                  pl.BlockSpec((tm, D_B), lambda i: (i, 0)),
                  pl.BlockSpec((tm, 512), lambda i: (i, 0)),
                  pl.BlockSpec((tm, 512), lambda i: (i, 3)),
                  pl.BlockSpec((tm, D_MODEL), lambda i: (i, 2)),
                  pl.BlockSpec((tm, D_MODEL), lambda i: (i, 3)),
                  pl.BlockSpec((tm, D_MODEL), lambda i: (i, 0)),
                  pl.BlockSpec((tm, D_MODEL), lambda i: (i, 0)),
                  pl.BlockSpec((D_A, D_MODEL), c2), pl.BlockSpec((D_B, D_MODEL), c2),
                  pl.BlockSpec((D_MODEL, D_MODEL), c2),
                  pl.BlockSpec((1, 2 * D_MODEL), c2), pl.BlockSpec((1, D_MODEL), c2)],
        out_specs=[pl.BlockSpec((tm, D_MODEL), lambda i: (i, 0)),
                   pl.BlockSpec((N_HEADS, tm, HEAD_DIM), lambda i: (0, i, 0)),
                   pl.BlockSpec((tm, D_B), lambda i: (i, 0)),
                   pl.BlockSpec((tm, 3072), lambda i: (i, 0)),
                   hbm, hbm, hbm,
                   pl.BlockSpec((1, 2 * D_MODEL), c2), pl.BlockSpec((1, D_MODEL), c2),
                   pl.BlockSpec((1, 128), c2)],
        scratch_shapes=[pltpu.VMEM((D_MODEL, D_MODEL), F32), pltpu.VMEM((D_A, D_MODEL), F32),
                        pltpu.VMEM((D_B, D_MODEL), F32), pltpu.SemaphoreType.DMA((3,))],
        compiler_params=_params(56, dimension_semantics=("arbitrary",)),
    )(att, sg, zrest, zrest, zrest, zrest, x, target, w_pa, w_pb, w_out, b_gate, final_g)


_DZ_MAP = ((0, 0), (1, 0), (2, 0), (3, 0), (4, 0), (4, 1), (3, 1), (3, 2), (3, 3), (3, 4), (3, 5))


def _dh_gradx(dq, dk, dv, dzt, dzs, w_in_bf, x, norm_g, d_out, prev=None, tm=256, after=()):
    S = x.shape[0]
    nt = S // tm // 2
    first = 0 if prev is None else nt
    n_in = 9 if prev is None else 11

    def body(dq_ref, dk_ref, dv_ref, dzt_ref, dzs_ref, w_ref, x_ref, g_ref, dout_ref, *rest):
        gx_ref, gn_ref = rest[-2:]
        i = pl.program_id(0)

        @pl.when(i == 0)
        def _():
            gn_ref[...] = jnp.zeros_like(gn_ref) if prev is None else rest[1][...]

        pieces = (dq_ref, dk_ref, dv_ref, dzt_ref, dzs_ref)
        dh = jnp.zeros((tm, D_MODEL), F32)
        for j, (pc, blk) in enumerate(_DZ_MAP):
            dh += _dot_nt(pieces[pc][:, blk * 512:(blk + 1) * 512], w_ref[:, j * 512:(j + 1) * 512])
        xv = x_ref[...]
        r = lax.rsqrt(jnp.mean(xv * xv, axis=-1, keepdims=True) + EPS)
        nrm = xv * r
        gn_ref[...] += jnp.sum(dh * nrm, axis=0, keepdims=True)
        dn = dh * g_ref[...]
        gx_ref[...] = r * (dn - nrm * jnp.mean(dn * nrm, axis=-1, keepdims=True)) + dout_ref[...]

    row = lambda w: pl.BlockSpec((tm, w), lambda i: (i + first, 0))
    c2 = lambda i: (0, 0)
    more = [] if prev is None else [_ANY, pl.BlockSpec((1, D_MODEL), c2)]
    return pl.pallas_call(
        _after(body, n_in, after), name="dh_gradx_a" if prev is None else "dh_gradx_b", grid=(nt,),
        out_shape=(jax.ShapeDtypeStruct((S, D_MODEL), F32), jax.ShapeDtypeStruct((1, D_MODEL), F32)),
        in_specs=[row(512), row(512), row(512), row(3072), row(1024),
                  pl.BlockSpec((D_MODEL, D_IN), c2), row(D_MODEL), pl.BlockSpec((1, D_MODEL), c2), row(D_MODEL)]
        + more + [_ANY] * len(after),
        out_specs=[row(D_MODEL), pl.BlockSpec((1, D_MODEL), c2)],
        input_output_aliases={} if prev is None else {9: 0},
        compiler_params=_params(48, dimension_semantics=("arbitrary",)),
    )(dq, dk, dv, dzt, dzs, w_in_bf, x, norm_g, d_out, *(prev or ()), *after)


def _gw_in(ht, dq, dk, dv, dzt, dzs, tn=256, after=()):
    S = ht.shape[1]
    per = 512 // tn
    cols = tuple((pc, per * blk + h) for pc, blk in _DZ_MAP for h in range(per))

    def body(ht_ref, dq_ref, dk_ref, dv_ref, dzt_ref, dzs_ref, o_ref):
        j = pl.program_id(0)
        pieces = (dq_ref, dk_ref, dv_ref, dzt_ref, dzs_ref)
        for pc in range(5):
            hit = functools.reduce(jnp.logical_or, [j == jj for jj, (p, _) in enumerate(cols) if p == pc])

            @pl.when(hit)
            def _(pc=pc):
                o_ref[...] = _dot(ht_ref[...], pieces[pc][...])

    def piece_spec(pc):
        cur = next(blk for p, blk in cols if p == pc)
        held = []
        for p, blk in cols:
            cur = blk if p == pc else cur
            held.append(cur)

        def index_map(j):
            blk = jnp.int32(held[0])
            for jj in range(1, len(held)):
                if held[jj] != held[jj - 1]:
                    blk = jnp.where(j >= jj, jnp.int32(held[jj]), blk)
            return (0, blk)

        return pl.BlockSpec((S, tn), index_map)

    return pl.pallas_call(
        _after(body, 6, after), name="gw_in", grid=(len(cols),),
        out_shape=jax.ShapeDtypeStruct((D_MODEL, D_IN), F32),
        in_specs=[pl.BlockSpec((D_MODEL, S), lambda j: (0, 0))] + [piece_spec(pc) for pc in range(5)]
        + [_ANY] * len(after),
        out_specs=pl.BlockSpec((D_MODEL, tn), lambda j: (0, j)),
        compiler_params=_params(48, dimension_semantics=("arbitrary",)),
    )(ht, dq, dk, dv, dzt, dzs, *after)


_HBM = pl.BlockSpec(memory_space=pltpu.HBM)
_SEM = pl.BlockSpec(memory_space=pltpu.SEMAPHORE)
_ANY = pl.BlockSpec(memory_space=pl.ANY)
_EFFECT = pltpu.SideEffectType.DATAFLOW_SIDE_EFFECTING


def _in_hbm(a):
    return pltpu.with_memory_space_constraint(a, pltpu.HBM)


def _after(body, n_in, after):
    if not after:
        return body
    return lambda *refs: body(*refs[:n_in], *refs[n_in + len(after):])


class _Started:
    def __init__(self, send, recv, bufs, token):
        self.send, self.recv, self.bufs, self.token = send, recv, bufs, token


def _split_start(name, bufs, n_copies, copies):
    nb = len(bufs)

    def body(*refs):
        for cp in copies(refs[:nb], refs[nb], refs[nb + 1]):
            cp.start()
        refs[-1][...] = jnp.zeros_like(refs[-1])

    outs = pl.pallas_call(
        body, name=name,
        out_shape=(pltpu.SemaphoreType.DMA((n_copies,)), pltpu.SemaphoreType.DMA((n_copies,)),
                   *[pltpu.HBM(b.shape, b.dtype) for b in bufs], jax.ShapeDtypeStruct((8, 128), F32)),
        in_specs=[_HBM] * nb,
        out_specs=(_SEM, _SEM, *[_HBM] * nb, pl.BlockSpec(memory_space=pltpu.VMEM)),
        input_output_aliases={k: 2 + k for k in range(nb)},
        compiler_params=pltpu.CompilerParams(has_side_effects=_EFFECT),
    )(*[_in_hbm(b) for b in bufs])
    return _Started(outs[0], outs[1], list(outs[2:2 + nb]), outs[-1])


def _split_wait(name, started, copies, after):
    nb = len(started.bufs)

    def body(*refs):
        for cp in copies(refs[:nb], refs[nb], refs[nb + 1]):
            cp.wait_send()
            cp.wait_recv()

    return list(pl.pallas_call(
        body, name=name,
        out_shape=tuple(pltpu.HBM(b.shape, b.dtype) for b in started.bufs),
        in_specs=[_HBM] * nb + [_SEM, _SEM, _ANY],
        out_specs=tuple([_HBM] * nb),
        input_output_aliases={k: k for k in range(nb)},
        compiler_params=pltpu.CompilerParams(has_side_effects=_EFFECT),
    )(*started.bufs, started.send, started.recv, after))


def _x1_copies(ws):
    def copies(refs, send_sems, recv_sems):
        x, y, c, _ = _mesh_pos()
        out = []
        for k, w in enumerate(ws):
            for s in range(N_SHARD):
                out.append(pltpu.make_async_remote_copy(
                    src_ref=_UNITS[w](refs[k], s, 1 - c), dst_ref=refs[len(ws) + k].at[s],
                    send_sem=send_sems.at[N_SHARD * k + s], recv_sem=recv_sems.at[N_SHARD * k + s],
                    device_id=(x, y, 1 - c), device_id_type=MESH))
        return out
    return copies


def _x2_copies(n):
    def copies(refs, send_sems, recv_sems):
        x, y, c, chips = _mesh_pos()
        out = []
        for j, (cx, cy) in enumerate(chips):
            for k in range(n):
                out.append(pltpu.make_async_remote_copy(
                    src_ref=refs[k].at[2 * cx + cy], dst_ref=refs[n + k].at[j],
                    send_sem=send_sems.at[3 * k + j], recv_sem=recv_sems.at[3 * k + j],
                    device_id=(cx, cy, c), device_id_type=MESH))
        return out
    return copies


def _x3_copies(ws):
    def copies(refs, send_sems, recv_sems):
        x, y, c, _ = _mesh_pos()
        out = []
        for k, w in enumerate(ws):
            rows = _HALF_ROWS[w]
            mine = refs[k].at[pl.ds(_mo(c * rows, rows), rows), :]
            out.append(pltpu.make_async_remote_copy(
                src_ref=mine, dst_ref=mine, send_sem=send_sems.at[k], recv_sem=recv_sems.at[k],
                device_id=(x, y, 1 - c), device_id_type=MESH))
        return out
    return copies


def _x1_lands(ws):
    return [lax.empty((N_SHARD,) + _UNIT_SHAPES[w], F32) for w in ws]


def _x2_lands(ws):
    return [lax.empty((3,) + _UNIT_SHAPES[w], BF) for w in ws]


def _grad_add1(w, g, recv, pos):
    ur, uc = _UNIT_SHAPES[w]
    if w == 3:
        g_map = lambda s, pos: (2 * s + pos[0], 0)
    else:
        g_map = lambda s, pos: (pos[0], s)

    def body(pos_ref, g_ref, r_ref, cs_ref, csb_ref):
        v = g_ref[...] + r_ref[0]
        cs_ref[0] = v
        csb_ref[0] = v.astype(BF)

    u3 = lambda s, pos: (s, 0, 0)
    return pl.pallas_call(
        body, name=f"grad_add1_{w}",
        grid_spec=pltpu.PrefetchScalarGridSpec(
            num_scalar_prefetch=1, grid=(N_SHARD,),
            in_specs=[pl.BlockSpec((ur, uc), g_map), pl.BlockSpec((1, ur, uc), u3)],
            out_specs=[pl.BlockSpec((1, ur, uc), u3), pl.BlockSpec((1, ur, uc), u3)]),
        out_shape=(jax.ShapeDtypeStruct((N_SHARD, ur, uc), F32), jax.ShapeDtypeStruct((N_SHARD, ur, uc), BF)),
        compiler_params=_params(40, dimension_semantics=("arbitrary",)),
    )(pos, g, recv)


def _grad_add2(w, cs, recv, pos):
    ur, uc = _UNIT_SHAPES[w]
    tr = ur // 4 if w == 0 else ur
    nt = ur // tr

    def body(pos_ref, cs_ref, r_ref, o_ref):
        o_ref[...] = ((cs_ref[0] + r_ref[0].astype(F32)) + r_ref[1].astype(F32)) + r_ref[2].astype(F32)

    return pl.pallas_call(
        body, name=f"grad_add2_{w}",
        grid_spec=pltpu.PrefetchScalarGridSpec(
            num_scalar_prefetch=1, grid=(nt,),
            in_specs=[pl.BlockSpec((1, tr, uc), lambda t, pos: (pos[1], t, 0)),
                      pl.BlockSpec((3, tr, uc), lambda t, pos: (0, t, 0))],
            out_specs=pl.BlockSpec((tr, uc), lambda t, pos: (pos[0] * nt + t, 0))),
        out_shape=jax.ShapeDtypeStruct(_SHARD_SHAPES[w], F32),
        compiler_params=_params(32, dimension_semantics=("arbitrary",)),
    )(pos, cs, recv)


def _grad_xchg3(ws, halves):
    n = len(ws)

    def body(*refs):
        cps = _x3_copies(ws)(refs[:n], refs[2 * n], refs[2 * n + 1])
        for cp in cps:
            cp.start()
        for cp in cps:
            cp.wait()

    return pl.pallas_call(
        body, name="grad_xchg3",
        out_shape=tuple(jax.ShapeDtypeStruct(_SHARD_SHAPES[w], F32) for w in ws),
        in_specs=[_ANY] * n, out_specs=[_ANY] * n,
        input_output_aliases={k: k for k in range(n)},
        scratch_shapes=[pltpu.SemaphoreType.DMA((n,)), pltpu.SemaphoreType.DMA((n,))],
        compiler_params=_params(16),
    )(*halves)


def _adamw_math(w, g, m, v):
    m = ADAM_B1 * m + (1.0 - ADAM_B1) * g
    v = ADAM_B2 * v + (1.0 - ADAM_B2) * (g * g)
    m_hat = m / ADAM_C1
    v_hat = v / ADAM_C2
    delta = -ADAM_LR * (m_hat / (jnp.sqrt(v_hat) + ADAM_EPS) + ADAM_WD * w)
    return delta, m, v


def _adamw(name, w, g, m, v, tr=256, after=()):
    rows, cols = w.shape

    def body(w_ref, g_ref, m_ref, v_ref, d_ref, nm_ref, nv_ref):
        d_ref[...], nm_ref[...], nv_ref[...] = _adamw_math(w_ref[...], g_ref[...], m_ref[...], v_ref[...])

    spec = pl.BlockSpec((tr, cols), lambda i: (i, 0))
    return pl.pallas_call(
        _after(body, 4, after), name=name, grid=(rows // tr,),
        out_shape=tuple(jax.ShapeDtypeStruct((rows, cols), F32) for _ in range(3)),
        in_specs=[spec] * 4 + [_ANY] * len(after), out_specs=[spec] * 3,
        compiler_params=_params(32, dimension_semantics=("arbitrary",)),
    )(w, g, m, v, *after)


def _small_reduce_adamw(pg, pw, pm, pv, after=()):
    rows = pg.shape[0]

    def body(g_ref, w_ref, m_ref, v_ref, gs_ref, d_ref, nm_ref, nv_ref, gath, send_sems, recv_sems):
        x, y, c, chips = _mesh_pos()
        me, sibling = (x, y, c), (x, y, 1 - c)

        def blk(px, py, pc):
            return gath.at[4 * px + 2 * py + pc]

        def copy(k, block, to, src=None):
            return pltpu.make_async_remote_copy(
                src_ref=blk(*block) if src is None else src, dst_ref=blk(*block),
                send_sem=send_sems.at[k], recv_sem=recv_sems.at[k], device_id=to, device_id_type=MESH)

        gath[4 * x + 2 * y + c] = g_ref[...]
        first = [copy(0, me, sibling, src=g_ref)]
        first += [copy(1 + j, me, (*chip, c), src=g_ref) for j, chip in enumerate(chips)]
        for cp in first:
            cp.start()
        passed = [copy(4 + j, (*chip, c), sibling) for j, chip in enumerate(chips)]
        for j, chip in enumerate(chips):
            copy(1 + j, (*chip, c), me).wait_recv()
            passed[j].start()
        copy(0, sibling, me).wait_recv()
        for j, chip in enumerate(chips):
            copy(4 + j, (*chip, 1 - c), me).wait_recv()
        for cp in first + passed:
            cp.wait_send()
        total = gath[0]
        for k in range(1, 8):
            total = total + gath[k]
        gs_ref[...] = total
        d_ref[...], nm_ref[...], nv_ref[...] = _adamw_math(w_ref[...], total, m_ref[...], v_ref[...])

    vm = pl.BlockSpec(memory_space=pltpu.VMEM)
    return pl.pallas_call(
        _after(body, 4, after), name="small_reduce_adamw",
        out_shape=tuple(jax.ShapeDtypeStruct((rows, 128), F32) for _ in range(4)),
        in_specs=[vm] * 4 + [_ANY] * len(after), out_specs=[vm] * 4,
        scratch_shapes=[pltpu.VMEM((8, rows, 128), F32), pltpu.SemaphoreType.DMA((7,)),
                        pltpu.SemaphoreType.DMA((7,))],
        compiler_params=_params(32),
    )(pg, pw, pm, pv, *after)


_REL_PAD = 384


def _rows(a):
    a = a.reshape(-1, 128)
    pad = (-a.shape[0]) % 8
    return jnp.pad(a, ((0, pad), (0, 0))) if pad else a


def _pack_small(norm_g, b_gate, rel_bias, ln_g, ln_b, w_s, b_s, final_g, loss_row):
    rel = jnp.pad(rel_bias.reshape(N_HEADS, N_REL), ((0, 0), (0, _REL_PAD - N_REL)))
    parts = [norm_g, b_gate, rel, ln_g, ln_b, w_s, b_s, final_g, loss_row]
    return jnp.concatenate([_rows(p) for p in parts], axis=0)


_SMALL_LAYOUT = (("norm_g", 8, 8), ("b_gate", 16, 16), ("rel_bias", 24, 24), ("sgu_ln_g", 4, 8),
                 ("sgu_ln_b", 4, 8), ("w_s", 512, 512), ("b_s", 4, 8), ("final_g", 8, 8), ("loss", 1, 8))


def _unpack_small(p):
    out, r = {}, 0
    for name, used, alloc in _SMALL_LAYOUT:
        out[name] = p[r:r + used]
        r += alloc
    return out


def _small_outputs(p):
    u = _unpack_small(p)
    return (u["norm_g"].reshape(1, D_MODEL), u["b_gate"].reshape(1, 2 * D_MODEL),
            u["rel_bias"].reshape(N_HEADS, _REL_PAD)[:, :N_REL].reshape(1, N_HEADS, N_REL),
            u["sgu_ln_g"].reshape(1, D_B), u["sgu_ln_b"].reshape(1, D_B),
            u["w_s"].reshape(1, N_GROUPS, 128, 128), u["b_s"].reshape(1, N_GROUPS, 128),
            u["final_g"].reshape(D_MODEL)), u["loss"]


def _bias_row(rel_bias):
    hi = rel_bias[:, N_REL - 1:N_REL]
    lo = rel_bias[:, 0:1]
    return jnp.concatenate([jnp.broadcast_to(hi, (N_HEADS, 384)), rel_bias[:, ::-1],
                            jnp.broadcast_to(lo, (N_HEADS, 191)), jnp.broadcast_to(hi, (N_HEADS, 192))], axis=1)


def kernel(x, norm_g, w_in, b_gate, rel_bias, sgu_ln_g, sgu_ln_b, w_s, b_s, w_pa, w_pb, w_out, final_g, loss_target, m_norm_g, m_w_in, m_b_gate, m_rel_bias, m_sgu_ln_g, m_sgu_ln_b, m_w_s, m_b_s, m_w_pa, m_w_pb, m_w_out, m_final_g, v_norm_g, v_w_in, v_b_gate, v_rel_bias, v_sgu_ln_g, v_sgu_ln_b, v_w_s, v_b_s, v_w_pa, v_w_pb, v_w_out, v_final_g):
    S = x.shape[1]
    xs = x.reshape(S, D_MODEL)
    tgt = loss_target.reshape(S, D_MODEL)
    big_w = (w_in[0], w_pa[0], w_pb[0], w_out[0])
    big_m = (m_w_in[0], m_w_pa[0], m_w_pb[0], m_w_out[0])
    big_v = (v_w_in[0], v_w_pa[0], v_w_pb[0], v_w_out[0])
    rel = rel_bias[0]
    ws = w_s[0]
    bst = b_s[0].T
    fg = final_g.reshape(1, D_MODEL)
    pos = jnp.stack([lax.axis_index("c"), 2 * lax.axis_index("x") + lax.axis_index("y")]).astype(jnp.int32)

    w_in_bf, w_pa_bf, w_pb_bf, w_out_bf = _ag_weights(*big_w)

    ht, q3, k3, v3, zrest = _inproj_fwd(xs, norm_g, w_in_bf)
    gp = _bias_row(rel)
    att = _attn_fwd(q3, k3, v3, gp)
    sg = _sgu_fwd(zrest, sgu_ln_g, sgu_ln_b, ws, bst)
    (d_out, d_att, d_sg, dzt, gw_out, gw_pa, gw_pb, g_bgate, g_final, loss_row) = _tail(
        att, sg, zrest, xs, tgt, w_pa_bf, w_pb_bf, w_out_bf, b_gate, fg)
    ws_s, ws_i = (1, 2, 3), (0,)
    names = ("adamw_w_in", "adamw_w_pa", "adamw_w_pb", "adamw_w_out")

    x1s = _split_start("gx1s_start", [gw_pa, gw_pb, gw_out] + _x1_lands(ws_s), 12, _x1_copies(ws_s))
    dq, dk, dv, d_gp = _attn_bwd(q3, k3, v3, d_att, gp, after=(x1s.token,))
    got = _split_wait("gx1s_wait", x1s, _x1_copies(ws_s), dq)
    sums_s = [_grad_add1(w, got[k], got[3 + k], pos) for k, w in enumerate(ws_s)]

    x2s = _split_start("gx2s_start", [s[1] for s in sums_s] + _x2_lands(ws_s), 9, _x2_copies(3))
    dzs, g_ws, g_bs_t, g_lng, g_lnb = _sgu_bwd(zrest, d_sg, sgu_ln_g, sgu_ln_b, ws, bst, after=(x2s.token,))
    gw_in = _gw_in(ht, dq, dk, dv, dzt, dzs)
    got = _split_wait("gx2s_wait", x2s, _x2_copies(3), gw_in)
    halves_s = [_grad_add2(w, sums_s[k][0], got[3 + k], pos) for k, w in enumerate(ws_s)]

    x3s = _split_start("gx3s_start", halves_s, 3, _x3_copies(ws_s))
    x1i = _split_start("gx1i_start", [gw_in] + _x1_lands(ws_i), 4, _x1_copies(ws_i))
    dh_args = (dq, dk, dv, dzt, dzs, w_in_bf, xs, norm_g, d_out)
    part = _dh_gradx(*dh_args, after=(x3s.token, x1i.token))
    g_shards_s = _split_wait("gx3s_wait", x3s, _x3_copies(ws_s), part[0])
    got = _split_wait("gx1i_wait", x1i, _x1_copies(ws_i), part[0])
    sum_i = _grad_add1(0, got[0], got[1], pos)

    x2i = _split_start("gx2i_start", [sum_i[1]] + _x2_lands(ws_i), 3, _x2_copies(1))
    grad_x, g_norm = _dh_gradx(*dh_args, prev=part, after=(x2i.token,))
    big = [None] * 4
    for k, w in enumerate(ws_s):
        big[w] = _adamw(names[w], big_w[w], g_shards_s[k], big_m[w], big_v[w],
                        tr=128 if w == 3 else 256, after=(x2i.token,))

    g_rel = d_gp[:, 384:384 + N_REL][:, ::-1]
    g_bs = g_bs_t[:, :N_GROUPS].T
    pg = _pack_small(g_norm, g_bgate, g_rel, g_lng, g_lnb, g_ws, g_bs, g_final, loss_row)
    zero_row = jnp.zeros((1, 128), F32)
    pw = _pack_small(norm_g, b_gate, rel, sgu_ln_g, sgu_ln_b, ws, b_s, final_g, zero_row)
    pm = _pack_small(m_norm_g, m_b_gate, m_rel_bias, m_sgu_ln_g, m_sgu_ln_b, m_w_s, m_b_s, m_final_g, zero_row)
    pv = _pack_small(v_norm_g, v_b_gate, v_rel_bias, v_sgu_ln_g, v_sgu_ln_b, v_w_s, v_b_s, v_final_g, zero_row)
    gsum, sdelta, sm, sv = _small_reduce_adamw(pg, pw, pm, pv, after=(x2i.token,))

    got = _split_wait("gx2i_wait", x2i, _x2_copies(1), gsum)
    half_i = _grad_add2(0, sum_i[0], got[1], pos)
    g_shard_i, = _grad_xchg3(ws_i, [half_i])
    big[0] = _adamw(names[0], big_w[0], g_shard_i, big_m[0], big_v[0])
    g_shards = [g_shard_i] + list(g_shards_s)
    sg_out, loss_rows = _small_outputs(gsum)
    sd_out, _ = _small_outputs(sdelta)
    sm_out, _ = _small_outputs(sm)
    sv_out, _ = _small_outputs(sv)
    loss = loss_rows[0, 0]

    def assemble(small, bigs):
        n_g, b_g, r_b, l_g, l_b, w_s_, b_s_, f_g = small
        b_in, b_pa, b_pb, b_out = (b[None] for b in bigs)
        return (n_g, b_in, b_g, r_b, l_g, l_b, w_s_, b_s_, b_pa, b_pb, b_out, f_g)

    grads_out = assemble(sg_out, g_shards)
    delta_out = assemble(sd_out, [b[0] for b in big])
    m_out = assemble(sm_out, [b[1] for b in big])
    v_out = assemble(sv_out, [b[2] for b in big])
    return (loss, grad_x.reshape(1, S, D_MODEL), *grads_out, *delta_out, *m_out, *v_out)
```

```python
import functools
import math

import jax
import jax.numpy as jnp
from jax import lax
from jax.experimental import pallas as pl
from jax.experimental.pallas import tpu as pltpu

F32 = jnp.float32
BF = jnp.bfloat16
MESH = pl.DeviceIdType.MESH

D_MODEL = 1024
D_A = 512
D_B = 512
D_IN = 5632
N_HEADS = 8
HEAD_DIM = 64
CHUNK = 64
N_PREV = 8
SGU_CHUNK = 128
N_GROUPS = 4
N_REL = 257
EPS = 1e-6
NEG_INF = -1e30
SCALE = HEAD_DIM ** -0.5

QB = 2 * CHUNK
KB = (N_PREV + 2) * CHUNK
PADK = N_PREV * CHUNK
ROLL_W = 1024
KEEP = KB // QB - 1

ADAM_LR = 0.001
ADAM_B1 = 0.9
ADAM_B2 = 0.999
ADAM_EPS = 1e-08
ADAM_WD = 0.01
ADAM_STEP = 10
ADAM_C1 = 1.0 - ADAM_B1 ** ADAM_STEP
ADAM_C2 = 1.0 - ADAM_B2 ** ADAM_STEP

N_SHARD = 4
SHARD_IN = D_IN // N_SHARD
MIB = 1024 * 1024


VMEM_RESERVE_MIB = 60


def _params(vmem_mib, **kw):
    assert vmem_mib <= VMEM_RESERVE_MIB
    return pltpu.CompilerParams(vmem_limit_bytes=VMEM_RESERVE_MIB * MIB, **kw)


def _sigmoid(x):
    return 1.0 / (1.0 + jnp.exp(-x))


def _silu_and_grad(x):
    s = _sigmoid(x)
    return x * s, s * (1.0 + x * (1.0 - s))


_GELU_C = math.sqrt(2.0 / math.pi)
_GELU_A = 0.044715


def _gelu_and_grad(x):
    x2 = x * x
    t = jnp.tanh(_GELU_C * (x + _GELU_A * (x2 * x)))
    cdf = 0.5 * (1.0 + t)
    grad = cdf + 0.5 * x * (1.0 - t * t) * (_GELU_C * (1.0 + 3.0 * _GELU_A * x2))
    return x * cdf, grad


def _dot(a, b):
    return jnp.dot(a, b, preferred_element_type=F32)


def _dot_nt(a, b):
    return lax.dot_general(a, b, (((1,), (1,)), ((), ())), preferred_element_type=F32)


def _dot_tn(a, b):
    return lax.dot_general(a, b, (((0,), (0,)), ((), ())), preferred_element_type=F32)


def _mo(v, m):
    return v if isinstance(v, int) else pl.multiple_of(v, m)


def _unit_in(ref, s, p):
    return ref.at[pl.ds(_mo(p * 512, 512), 512), pl.ds(_mo(s * SHARD_IN, 128), SHARD_IN)]


def _unit_p(ref, s, p):
    return ref.at[pl.ds(_mo(p * 256, 256), 256), pl.ds(_mo(s * 256, 128), 256)]


def _unit_out(ref, s, p):
    return ref.at[pl.ds(_mo(s * 256 + p * 128, 128), 128), :]


_UNITS = (_unit_in, _unit_p, _unit_p, _unit_out)
_HALF_ROWS = (512, 256, 256, 128)
_UNIT_SHAPES = ((512, SHARD_IN), (256, 256), (256, 256), (128, D_MODEL))
_FULL_SHAPES = ((D_MODEL, D_IN), (D_A, D_MODEL), (D_B, D_MODEL), (D_MODEL, D_MODEL))
_SHARD_SHAPES = ((D_MODEL, SHARD_IN), (D_A, 256), (D_B, 256), (256, D_MODEL))


def _mesh_pos():
    x, y, c = lax.axis_index("x"), lax.axis_index("y"), lax.axis_index("c")
    chips = [(1 - x, y), (x, 1 - y), (1 - x, 1 - y)]
    return x, y, c, chips


def _ag_weights(w_in, w_pa, w_pb, w_out):
    def body(i0, i1, i2, i3, o0, o1, o2, o3, s0, s1, s2, s3, send_sems, recv_sems, local_sems):
        ins, outs, stage = (i0, i1, i2, i3), (o0, o1, o2, o3), (s0, s1, s2, s3)
        x, y, c, chips = _mesh_pos()
        s_me = 2 * x + y
        sibling = (x, y, 1 - c)
        for w in range(4):
            stage[w][...] = ins[w][...].astype(BF)

        def half(w, p):
            rows = _HALF_ROWS[w]
            return stage[w].at[pl.ds(_mo(p * rows, rows), rows), :]

        local = []
        for w in range(4):
            for p in range(2):
                cp = pltpu.make_async_copy(half(w, p), _UNITS[w](outs[w], s_me, p), local_sems.at[w, p])
                cp.start()
                local.append(cp)

        def rcopy(w, k, src, dst, to):
            return pltpu.make_async_remote_copy(src_ref=src, dst_ref=dst, send_sem=send_sems.at[w, k],
                                                recv_sem=recv_sems.at[w, k], device_id=to, device_id_type=MESH)

        sends = []
        for j, (cx, cy) in enumerate(chips):
            for w in range(4):
                cp = rcopy(w, j, half(w, c), _UNITS[w](outs[w], s_me, c), (cx, cy, c))
                cp.start()
                sends.append(cp)
        for j, (cx, cy) in enumerate(chips):
            s_j = 2 * cx + cy
            for w in range(4):
                landed = _UNITS[w](outs[w], s_j, c)
                rcopy(w, j, landed, landed, (cx, cy, c)).wait_recv()
                cp = rcopy(w, 3 + j, landed, landed, sibling)
                cp.start()
                sends.append(cp)
        for j, (cx, cy) in enumerate(chips):
            s_j = 2 * cx + cy
            for w in range(4):
                other = _UNITS[w](outs[w], s_j, 1 - c)
                rcopy(w, 3 + j, other, other, sibling).wait_recv()
        for cp in sends:
            cp.wait_send()
        for cp in local:
            cp.wait()

    vm = pl.BlockSpec(memory_space=pltpu.VMEM)
    hbm = pl.BlockSpec(memory_space=pl.ANY)
    return pl.pallas_call(
        body, name="ag_weights",
        out_shape=tuple(jax.ShapeDtypeStruct(s, BF) for s in _FULL_SHAPES),
        in_specs=[vm] * 4, out_specs=[hbm] * 4,
        scratch_shapes=[pltpu.VMEM(s, BF) for s in _SHARD_SHAPES]
        + [pltpu.SemaphoreType.DMA((4, 6)), pltpu.SemaphoreType.DMA((4, 6)), pltpu.SemaphoreType.DMA((4, 2))],
        compiler_params=_params(40),
    )(w_in, w_pa, w_pb, w_out)


def _inproj_fwd(x, norm_g, w_in_bf, tm=256):
    S = x.shape[0]

    def body(x_ref, g_ref, w_ref, ht_ref, q_ref, k_ref, v_ref, zr_ref):
        xv = x_ref[...]
        r = lax.rsqrt(jnp.mean(xv * xv, axis=-1, keepdims=True) + EPS)
        hf = (xv * r) * g_ref[...]
        ht_ref[...] = hf.T.astype(BF)
        h = hf.astype(BF)
        heads = (q_ref, k_ref, v_ref)
        for j in range(D_IN // 512):
            z = _dot(h, w_ref[:, j * 512:(j + 1) * 512])
            if j < 3:
                zb = z.astype(BF)
                for hd in range(N_HEADS):
                    heads[j][hd] = zb[:, hd * HEAD_DIM:(hd + 1) * HEAD_DIM]
            else:
                zr_ref[:, (j - 3) * 512:(j - 2) * 512] = z

    head_major = jax.ShapeDtypeStruct((N_HEADS, S, HEAD_DIM), BF)
    head_spec = pl.BlockSpec((N_HEADS, tm, HEAD_DIM), lambda i: (0, i, 0))
    return pl.pallas_call(
        body, name="inproj_fwd", grid=(S // tm,),
        out_shape=(jax.ShapeDtypeStruct((D_MODEL, S), BF), head_major, head_major, head_major,
                   jax.ShapeDtypeStruct((S, D_IN - 3 * D_A), F32)),
        in_specs=[pl.BlockSpec((tm, D_MODEL), lambda i: (i, 0)),
                  pl.BlockSpec((1, D_MODEL), lambda i: (0, 0)),
                  pl.BlockSpec((D_MODEL, D_IN), lambda i: (0, 0))],
        out_specs=[pl.BlockSpec((D_MODEL, tm), lambda i: (0, i)),
                   head_spec, head_spec, head_spec,
                   pl.BlockSpec((tm, D_IN - 3 * D_A), lambda i: (i, 0))],
        compiler_params=_params(52, dimension_semantics=("arbitrary",)),
    )(x, norm_g, w_in_bf)


def _skew_table(gp_row):
    row = lax.broadcasted_iota(jnp.int32, (QB, ROLL_W), 0)
    t = jnp.broadcast_to(gp_row, (QB, ROLL_W))
    for b in range(7):
        t = jnp.where(((row >> b) & 1) == 1, pltpu.roll(t, 1 << b, axis=1), t)
    return t


def _unskew_sum(d):
    row = lax.broadcasted_iota(jnp.int32, (QB, ROLL_W), 0)
    for b in range(7):
        d = jnp.where(((row >> b) & 1) == 1, pltpu.roll(d, ROLL_W - (1 << b), axis=1), d)
    return jnp.sum(d, axis=0, keepdims=True)


def _struct_mask():
    a = lax.broadcasted_iota(jnp.int32, (QB, KB), 0) // CHUNK
    b = lax.broadcasted_iota(jnp.int32, (QB, KB), 1) // CHUNK
    return (b >= a) & (b <= a + N_PREV)


def _load_kv(k_hbm, v_hbm, gp_ref, k_scr, v_scr, bias_scr, sems, S):
    zeros = jnp.zeros((N_HEADS, PADK, HEAD_DIM), BF)
    k_scr[:, 0:PADK, :] = zeros
    v_scr[:, 0:PADK, :] = zeros
    ck = pltpu.make_async_copy(k_hbm, k_scr.at[:, pl.ds(PADK, S), :], sems.at[0])
    cv = pltpu.make_async_copy(v_hbm, v_scr.at[:, pl.ds(PADK, S), :], sems.at[1])
    ck.start()
    cv.start()
    keep = _struct_mask()
    for h in range(N_HEADS):
        bias_scr[h] = jnp.where(keep, _skew_table(gp_ref[h:h + 1, :])[:, :KB], NEG_INF)
    ck.wait()
    cv.wait()


_BATCH_NT = (((2,), (2,)), ((0,), (0,)))
_BATCH_NN = (((2,), (1,)), ((0,), (0,)))
_BATCH_TN = (((1,), (1,)), ((0,), (0,)))


def _bdot(a, b, dims):
    return lax.dot_general(a, b, dims, preferred_element_type=F32)


def _probs(q, kb, bias, i, front):
    s = _bdot(q * jnp.asarray(SCALE, BF), kb, _BATCH_NT) + bias
    if front:
        col = lax.broadcasted_iota(jnp.int32, (1, 1, KB), 2)
        s = jnp.where(col >= PADK - i * QB, s, NEG_INF)
    m = jnp.max(s, axis=-1, keepdims=True)
    e = jnp.exp(s - m)
    return e * (1.0 / jnp.sum(e, axis=-1, keepdims=True))


def _attn_fwd(q3, k3, v3, gp):
    S = q3.shape[1]

    def body(q_ref, k_hbm, v_hbm, gp_ref, o_ref, k_scr, v_scr, bias_scr, sems):
        i = pl.program_id(0)

        @pl.when(i == 0)
        def _():
            _load_kv(k_hbm, v_hbm, gp_ref, k_scr, v_scr, bias_scr, sems, S)

        def step(front):
            start = pl.multiple_of(i * QB, QB)
            kb = k_scr[:, pl.ds(start, KB), :]
            vb = v_scr[:, pl.ds(start, KB), :]
            p = _probs(q_ref[...], kb, bias_scr[...], i, front)
            o = _bdot(p.astype(BF), vb, _BATCH_NN)
            for h in range(N_HEADS):
                o_ref[:, h * HEAD_DIM:(h + 1) * HEAD_DIM] = o[h]

        pl.when(i < KEEP)(functools.partial(step, True))
        pl.when(i >= KEEP)(functools.partial(step, False))

    kv_scr = pltpu.VMEM((N_HEADS, S + PADK, HEAD_DIM), BF)
    return pl.pallas_call(
        body, name="attn_fwd", grid=(S // QB,),
        out_shape=jax.ShapeDtypeStruct((S, D_A), F32),
        in_specs=[pl.BlockSpec((N_HEADS, QB, HEAD_DIM), lambda i: (0, i, 0)),
                  pl.BlockSpec(memory_space=pl.ANY), pl.BlockSpec(memory_space=pl.ANY),
                  pl.BlockSpec((N_HEADS, ROLL_W), lambda i: (0, 0))],
        out_specs=pl.BlockSpec((QB, D_A), lambda i: (i, 0)),
        scratch_shapes=[kv_scr, kv_scr, pltpu.VMEM((N_HEADS, QB, KB), F32), pltpu.SemaphoreType.DMA((2,))],
        compiler_params=_params(48, dimension_semantics=("arbitrary",)),
    )(q3, k3, v3, gp)


def _attn_bwd(q3, k3, v3, d_att3, gp, after=()):
    S = q3.shape[1]
    nq = S // QB

    def body(q_ref, do_ref, k_hbm, v_hbm, gp_ref, dq_ref, dk_ref, dv_ref, dgp_ref,
             k_scr, v_scr, bias_scr, dk_acc, dv_acc, dbias_acc, pad_scr, sems):
        i = pl.program_id(0)

        @pl.when(i == 0)
        def _():
            _load_kv(k_hbm, v_hbm, gp_ref, k_scr, v_scr, bias_scr, sems, S)
            dk_acc[...] = jnp.zeros_like(dk_acc)
            dv_acc[...] = jnp.zeros_like(dv_acc)
            dbias_acc[...] = jnp.zeros_like(dbias_acc)

        def step(front):
            start = pl.multiple_of(i * QB, QB)
            kb = k_scr[:, pl.ds(start, KB), :]
            vb = v_scr[:, pl.ds(start, KB), :]
            q = q_ref[...]
            do = do_ref[...]
            p = _probs(q, kb, bias_scr[...], i, front)
            dp = _bdot(do, vb, _BATCH_NT)
            ds = p * (dp - jnp.sum(dp * p, axis=-1, keepdims=True))
            dbias_acc[...] += ds
            dsb = (ds * SCALE).astype(BF)
            dq = _bdot(dsb, kb, _BATCH_NN)
            for h in range(N_HEADS):
                dq_ref[:, h * HEAD_DIM:(h + 1) * HEAD_DIM] = dq[h].astype(BF)
            dk_acc[...] += _bdot(dsb, q, _BATCH_TN)
            dv_acc[...] += _bdot(p.astype(BF), do, _BATCH_TN)

        pl.when(i < KEEP)(functools.partial(step, True))
        pl.when((i >= KEEP) & (i < nq))(functools.partial(step, False))

        for h in range(N_HEADS):
            hs = slice(h * HEAD_DIM, (h + 1) * HEAD_DIM)
            dk_ref[:, hs] = dk_acc[h, 0:QB, :].astype(BF)
            dv_ref[:, hs] = dv_acc[h, 0:QB, :].astype(BF)
        dk_acc[:, 0:KB - QB, :] = dk_acc[:, QB:KB, :]
        dv_acc[:, 0:KB - QB, :] = dv_acc[:, QB:KB, :]
        dk_acc[:, KB - QB:KB, :] = jnp.zeros((N_HEADS, QB, HEAD_DIM), F32)
        dv_acc[:, KB - QB:KB, :] = jnp.zeros((N_HEADS, QB, HEAD_DIM), F32)

        @pl.when(i == nq + KEEP - 1)
        def _():
            lane = lax.broadcasted_iota(jnp.int32, (1, ROLL_W), 1)
            hi = (lane < 384) | (lane >= 832)
            lo = (lane > 640) & (lane < 832)
            pad_scr[...] = jnp.zeros_like(pad_scr)
            for h in range(N_HEADS):
                pad_scr[:, 0:KB] = dbias_acc[h]
                g = _unskew_sum(pad_scr[...])
                s_hi = jnp.sum(jnp.where(hi, g, 0.0), axis=-1, keepdims=True)
                s_lo = jnp.sum(jnp.where(lo, g, 0.0), axis=-1, keepdims=True)
                g = jnp.where(lane == 384, g + s_hi, g)
                g = jnp.where(lane == 640, g + s_lo, g)
                dgp_ref[h:h + 1, :] = g

    last = nq - 1
    kv_scr = pltpu.VMEM((N_HEADS, S + PADK, HEAD_DIM), BF)
    return pl.pallas_call(
        _after(body, 5, after), name="attn_bwd", grid=(nq + KEEP,),
        out_shape=(jax.ShapeDtypeStruct((S, D_A), BF), jax.ShapeDtypeStruct((S, D_A), BF),
                   jax.ShapeDtypeStruct((S, D_A), BF), jax.ShapeDtypeStruct((N_HEADS, ROLL_W), F32)),
        in_specs=[pl.BlockSpec((N_HEADS, QB, HEAD_DIM), lambda i: (0, jnp.minimum(i, last), 0)),
                  pl.BlockSpec((N_HEADS, QB, HEAD_DIM), lambda i: (0, jnp.minimum(i, last), 0)),
                  pl.BlockSpec(memory_space=pl.ANY), pl.BlockSpec(memory_space=pl.ANY),
                  pl.BlockSpec((N_HEADS, ROLL_W), lambda i: (0, 0))] + [_ANY] * len(after),
        out_specs=[pl.BlockSpec((QB, D_A), lambda i: (jnp.minimum(i, last), 0)),
                   pl.BlockSpec((QB, D_A), lambda i: (jnp.maximum(i - KEEP, 0), 0)),
                   pl.BlockSpec((QB, D_A), lambda i: (jnp.maximum(i - KEEP, 0), 0)),
                   pl.BlockSpec((N_HEADS, ROLL_W), lambda i: (0, 0))],
        scratch_shapes=[kv_scr, kv_scr, pltpu.VMEM((N_HEADS, QB, KB), F32),
                        pltpu.VMEM((N_HEADS, KB, HEAD_DIM), F32), pltpu.VMEM((N_HEADS, KB, HEAD_DIM), F32),
                        pltpu.VMEM((N_HEADS, QB, KB), F32), pltpu.VMEM((QB, ROLL_W), F32),
                        pltpu.SemaphoreType.DMA((2,))],
        compiler_params=_params(56, dimension_semantics=("arbitrary",)),
    )(q3, d_att3, k3, v3, gp, *after)


def _sgu_core(ub, vb, lg, lb):
    u, du = _gelu_and_grad(ub)
    v, dv = _gelu_and_grad(vb)
    mu = jnp.mean(v, axis=-1, keepdims=True)
    vc = v - mu
    rstd = lax.rsqrt(jnp.mean(vc * vc, axis=-1, keepdims=True) + EPS)
    xh = vc * rstd
    vn = xh * lg + lb
    return u, du, dv, rstd, xh, vn


def _tri():
    r = lax.broadcasted_iota(jnp.int32, (SGU_CHUNK, SGU_CHUNK), 0)
    c = lax.broadcasted_iota(jnp.int32, (SGU_CHUNK, SGU_CHUNK), 1)
    return r >= c


def _sgu_fwd(zrest, ln_g, ln_b, w_s, b_s_t, tm=512):
    S = zrest.shape[0]

    def body(ub_ref, vb_ref, lg_ref, lb_ref, ws_ref, bst_ref, sg_ref):
        u, _, _, _, _, vn = _sgu_core(ub_ref[...], vb_ref[...], lg_ref[...], lb_ref[...])
        vnb = vn.astype(BF)
        tri = _tri()
        for g in range(N_GROUPS):
            cs = slice(g * 128, (g + 1) * 128)
            wt = jnp.where(tri, ws_ref[g], 0.0).astype(BF)
            bcol = bst_ref[:, g:g + 1]
            for n in range(tm // SGU_CHUNK):
                rs = slice(n * SGU_CHUNK, (n + 1) * SGU_CHUNK)
                mixed = _dot(wt, vnb[rs, cs]) + bcol
                sg_ref[rs, cs] = u[rs, cs] * mixed

    return pl.pallas_call(
        body, name="sgu_fwd", grid=(S // tm,),
        out_shape=jax.ShapeDtypeStruct((S, D_B), F32),
        in_specs=[pl.BlockSpec((tm, 512), lambda i: (i, 1)),
                  pl.BlockSpec((tm, 512), lambda i: (i, 2)),
                  pl.BlockSpec((1, D_B), lambda i: (0, 0)),
                  pl.BlockSpec((1, D_B), lambda i: (0, 0)),
                  pl.BlockSpec((N_GROUPS, 128, 128), lambda i: (0, 0, 0)),
                  pl.BlockSpec((128, N_GROUPS), lambda i: (0, 0))],
        out_specs=pl.BlockSpec((tm, D_B), lambda i: (i, 0)),
        compiler_params=_params(32, dimension_semantics=("arbitrary",)),
    )(zrest, zrest, ln_g, ln_b, w_s, b_s_t)


def _sgu_bwd(zrest, d_sg, ln_g, ln_b, w_s, b_s_t, tm=256, after=()):
    S = zrest.shape[0]
    nt = S // tm

    def body(ub_ref, vb_ref, dsg_ref, lg_ref, lb_ref, ws_ref, bst_ref,
             dzs_ref, gws_ref, gbs_ref, glg_ref, glb_ref, dvn_scr, bs_acc):
        i = pl.program_id(0)

        @pl.when(i == 0)
        def _():
            gws_ref[...] = jnp.zeros_like(gws_ref)
            glg_ref[...] = jnp.zeros_like(glg_ref)
            glb_ref[...] = jnp.zeros_like(glb_ref)
            bs_acc[...] = jnp.zeros_like(bs_acc)

        ub = ub_ref[...]
        u, du, dv, rstd, xh, vn = _sgu_core(ub, vb_ref[...], lg_ref[...], lb_ref[...])
        vnb = vn.astype(BF)
        dsg = dsg_ref[...]
        tri = _tri()
        for g in range(N_GROUPS):
            cs = slice(g * 128, (g + 1) * 128)
            wtf = jnp.where(tri, ws_ref[g], 0.0)
            wt = wtf.astype(BF)
            wtt = wtf.T.astype(BF)
            bcol = bst_ref[:, g:g + 1]
            for n in range(tm // SGU_CHUNK):
                rs = slice(n * SGU_CHUNK, (n + 1) * SGU_CHUNK)
                mixed = _dot(wt, vnb[rs, cs]) + bcol
                dzs_ref[rs, cs] = (dsg[rs, cs] * mixed * du[rs, cs]).astype(BF)
                dmix = dsg[rs, cs] * u[rs, cs]
                bs_acc[:, cs] += dmix
                dmb = dmix.astype(BF)
                gws_ref[g] += _dot_nt(dmb, vnb[rs, cs])
                dvn_scr[rs, cs] = _dot(wtt, dmb)
        dvn = dvn_scr[...]
        glg_ref[...] += jnp.sum(dvn * xh, axis=0, keepdims=True)
        glb_ref[...] += jnp.sum(dvn, axis=0, keepdims=True)
        dxh = dvn * lg_ref[...]
        dvv = rstd * (dxh - jnp.mean(dxh, axis=-1, keepdims=True)
                      - xh * jnp.mean(dxh * xh, axis=-1, keepdims=True))
        dzs_ref[:, D_B:2 * D_B] = (dvv * dv).astype(BF)

        @pl.when(i == nt - 1)
        def _():
            lane = lax.broadcasted_iota(jnp.int32, (SGU_CHUNK, 128), 1)
            out = jnp.zeros((SGU_CHUNK, 128), F32)
            for g in range(N_GROUPS):
                gws_ref[g] = jnp.where(tri, gws_ref[g], 0.0)
                col = jnp.sum(bs_acc[:, g * 128:(g + 1) * 128], axis=-1, keepdims=True)
                out = jnp.where(lane == g, col, out)
            gbs_ref[...] = out

    const2 = lambda i: (0, 0)
    return pl.pallas_call(
        _after(body, 7, after), name="sgu_bwd", grid=(nt,),
        out_shape=(jax.ShapeDtypeStruct((S, 2 * D_B), BF),
                   jax.ShapeDtypeStruct((N_GROUPS, 128, 128), F32),
                   jax.ShapeDtypeStruct((SGU_CHUNK, 128), F32),
                   jax.ShapeDtypeStruct((1, D_B), F32), jax.ShapeDtypeStruct((1, D_B), F32)),
        in_specs=[pl.BlockSpec((tm, 512), lambda i: (i, 1)),
                  pl.BlockSpec((tm, 512), lambda i: (i, 2)),
                  pl.BlockSpec((tm, D_B), lambda i: (i, 0)),
                  pl.BlockSpec((1, D_B), const2), pl.BlockSpec((1, D_B), const2),
                  pl.BlockSpec((N_GROUPS, 128, 128), lambda i: (0, 0, 0)),
                  pl.BlockSpec((128, N_GROUPS), const2)] + [_ANY] * len(after),
        out_specs=[pl.BlockSpec((tm, 2 * D_B), lambda i: (i, 0)),
                   pl.BlockSpec((N_GROUPS, 128, 128), lambda i: (0, 0, 0)),
                   pl.BlockSpec((SGU_CHUNK, 128), const2),
                   pl.BlockSpec((1, D_B), const2), pl.BlockSpec((1, D_B), const2)],
        scratch_shapes=[pltpu.VMEM((tm, D_B), F32), pltpu.VMEM((SGU_CHUNK, D_B), F32)],
        compiler_params=_params(32, dimension_semantics=("arbitrary",)),
    )(zrest, zrest, d_sg, ln_g, ln_b, w_s, b_s_t, *after)


def _tail(att, sg, zrest, x, target, w_pa, w_pb, w_out, b_gate, final_g, tm=256):
    S = x.shape[0]
    nt = S // tm

    def body(att_ref, sg_ref, ga_ref, gb_ref, gta_ref, gtb_ref, x_ref, t_ref,
             wpa_ref, wpb_ref, wout_ref, bg_ref, fg_ref,
             dout_ref, datt_ref, dsg_ref, dzt_ref, gwout_hbm, gwpa_hbm, gwpb_hbm,
             gbg_ref, gfg_ref, loss_ref, acc_out, acc_pa, acc_pb, sems):
        i = pl.program_id(0)

        @pl.when(i == 0)
        def _():
            acc_out[...] = jnp.zeros_like(acc_out)
            acc_pa[...] = jnp.zeros_like(acc_pa)
            acc_pb[...] = jnp.zeros_like(acc_pb)
            gbg_ref[...] = jnp.zeros_like(gbg_ref)
            gfg_ref[...] = jnp.zeros_like(gfg_ref)
            loss_ref[...] = jnp.zeros_like(loss_ref)

        att = att_ref[...]
        sg = sg_ref[...]
        sa, dsa = _silu_and_grad(ga_ref[...])
        sb, dsb = _silu_and_grad(gb_ref[...])
        ya = (att * sa).astype(BF)
        yb = (sg * sb).astype(BF)
        pa = _dot(ya, wpa_ref[...])
        pb = _dot(yb, wpb_ref[...])
        ga = _sigmoid(gta_ref[...] + bg_ref[:, 0:D_MODEL])
        gb = _sigmoid(gtb_ref[...] + bg_ref[:, D_MODEL:2 * D_MODEL])
        merged = (ga * pa + gb * pb).astype(BF)
        out = x_ref[...] + _dot(merged, wout_ref[...])
        r2 = lax.rsqrt(jnp.mean(out * out, axis=-1, keepdims=True) + EPS)
        nrm = out * r2
        fg = fg_ref[...]
        err = nrm * fg - t_ref[...]
        loss_ref[...] += 0.5 * jnp.sum(jnp.mean(err * err, axis=-1, keepdims=True))
        dy = err * (1.0 / D_MODEL)
        gfg_ref[...] += jnp.sum(dy * nrm, axis=0, keepdims=True)
        dn = dy * fg
        d_out = r2 * (dn - nrm * jnp.mean(dn * nrm, axis=-1, keepdims=True))
        dout_ref[...] = d_out
        d_outb = d_out.astype(BF)
        acc_out[...] += _dot_tn(merged, d_outb)
        dm = _dot_nt(d_outb, wout_ref[...])
        d_pa = (dm * ga).astype(BF)
        d_pb = (dm * gb).astype(BF)
        d_gta = dm * pa * (ga * (1.0 - ga))
        d_gtb = dm * pb * (gb * (1.0 - gb))
        gbg_ref[:, 0:D_MODEL] += jnp.sum(d_gta, axis=0, keepdims=True)
        gbg_ref[:, D_MODEL:2 * D_MODEL] += jnp.sum(d_gtb, axis=0, keepdims=True)
        dzt_ref[:, 2 * D_A:2 * D_A + D_MODEL] = d_gta.astype(BF)
        dzt_ref[:, 2 * D_A + D_MODEL:] = d_gtb.astype(BF)
        acc_pa[...] += _dot_tn(ya, d_pa)
        acc_pb[...] += _dot_tn(yb, d_pb)
        d_ya = _dot_nt(d_pa, wpa_ref[...])
        d_yb = _dot_nt(d_pb, wpb_ref[...])
        d_att = (d_ya * sa).astype(BF)
        for hd in range(N_HEADS):
            datt_ref[hd] = d_att[:, hd * HEAD_DIM:(hd + 1) * HEAD_DIM]
        dzt_ref[:, 0:D_A] = (d_ya * att * dsa).astype(BF)
        dsg_ref[...] = d_yb * sb
        dzt_ref[:, D_A:2 * D_A] = (d_yb * sg * dsb).astype(BF)

        @pl.when(i == nt - 1)
        def _():
            cps = [pltpu.make_async_copy(acc_out, gwout_hbm, sems.at[0]),
                   pltpu.make_async_copy(acc_pa, gwpa_hbm, sems.at[1]),
                   pltpu.make_async_copy(acc_pb, gwpb_hbm, sems.at[2])]
            for cp in cps:
                cp.start()
            for cp in cps:
                cp.wait()

    c2 = lambda i: (0, 0)
    hbm = pl.BlockSpec(memory_space=pl.ANY)
    return pl.pallas_call(
        body, name="tail", grid=(nt,),
        out_shape=(jax.ShapeDtypeStruct((S, D_MODEL), F32), jax.ShapeDtypeStruct((N_HEADS, S, HEAD_DIM), BF),
                   jax.ShapeDtypeStruct((S, D_B), F32), jax.ShapeDtypeStruct((S, 3072), BF),
                   jax.ShapeDtypeStruct((D_MODEL, D_MODEL), F32), jax.ShapeDtypeStruct((D_A, D_MODEL), F32),
                   jax.ShapeDtypeStruct((D_B, D_MODEL), F32),
                   jax.ShapeDtypeStruct((1, 2 * D_MODEL), F32), jax.ShapeDtypeStruct((1, D_MODEL), F32),
                   jax.ShapeDtypeStruct((1, 128), F32)),
        in_specs=[pl.BlockSpec((tm, D_A), lambda i: (i, 0)),
                  pl.BlockSpec((tm, D_B), lambda i: (i, 0)),
                  pl.BlockSpec((tm, 512), lambda i: (i, 0)),
                  pl.BlockSpec((tm, 512), lambda i: (i, 3)),
                  pl.BlockSpec((tm, D_MODEL), lambda i: (i, 2)),
                  pl.BlockSpec((tm, D_MODEL), lambda i: (i, 3)),
                  pl.BlockSpec((tm, D_MODEL), lambda i: (i, 0)),
                  pl.BlockSpec((tm, D_MODEL), lambda i: (i, 0)),
                  pl.BlockSpec((D_A, D_MODEL), c2), pl.BlockSpec((D_B, D_MODEL), c2),
                  pl.BlockSpec((D_MODEL, D_MODEL), c2),
                  pl.BlockSpec((1, 2 * D_MODEL), c2), pl.BlockSpec((1, D_MODEL), c2)],
        out_specs=[pl.BlockSpec((tm, D_MODEL), lambda i: (i, 0)),
                   pl.BlockSpec((N_HEADS, tm, HEAD_DIM), lambda i: (0, i, 0)),
                   pl.BlockSpec((tm, D_B), lambda i: (i, 0)),
                   pl.BlockSpec((tm, 3072), lambda i: (i, 0)),
                   hbm, hbm, hbm,
                   pl.BlockSpec((1, 2 * D_MODEL), c2), pl.BlockSpec((1, D_MODEL), c2),
                   pl.BlockSpec((1, 128), c2)],
        scratch_shapes=[pltpu.VMEM((D_MODEL, D_MODEL), F32), pltpu.VMEM((D_A, D_MODEL), F32),
                        pltpu.VMEM((D_B, D_MODEL), F32), pltpu.SemaphoreType.DMA((3,))],
        compiler_params=_params(56, dimension_semantics=("arbitrary",)),
    )(att, sg, zrest, zrest, zrest, zrest, x, target, w_pa, w_pb, w_out, b_gate, final_g)


_DZ_MAP = ((0, 0), (1, 0), (2, 0), (3, 0), (4, 0), (4, 1), (3, 1), (3, 2), (3, 3), (3, 4), (3, 5))


def _dh_gradx(dq, dk, dv, dzt, dzs, w_in_bf, x, norm_g, d_out, prev=None, tm=256, after=()):
    S = x.shape[0]
    nt = S // tm // 2
    first = 0 if prev is None else nt
    n_in = 9 if prev is None else 11

    def body(dq_ref, dk_ref, dv_ref, dzt_ref, dzs_ref, w_ref, x_ref, g_ref, dout_ref, *rest):
        gx_ref, gn_ref = rest[-2:]
        i = pl.program_id(0)

        @pl.when(i == 0)
        def _():
            gn_ref[...] = jnp.zeros_like(gn_ref) if prev is None else rest[1][...]

        pieces = (dq_ref, dk_ref, dv_ref, dzt_ref, dzs_ref)
        dh = jnp.zeros((tm, D_MODEL), F32)
        for j, (pc, blk) in enumerate(_DZ_MAP):
            dh += _dot_nt(pieces[pc][:, blk * 512:(blk + 1) * 512], w_ref[:, j * 512:(j + 1) * 512])
        xv = x_ref[...]
        r = lax.rsqrt(jnp.mean(xv * xv, axis=-1, keepdims=True) + EPS)
        nrm = xv * r
        gn_ref[...] += jnp.sum(dh * nrm, axis=0, keepdims=True)
        dn = dh * g_ref[...]
        gx_ref[...] = r * (dn - nrm * jnp.mean(dn * nrm, axis=-1, keepdims=True)) + dout_ref[...]

    row = lambda w: pl.BlockSpec((tm, w), lambda i: (i + first, 0))
    c2 = lambda i: (0, 0)
    more = [] if prev is None else [_ANY, pl.BlockSpec((1, D_MODEL), c2)]
    return pl.pallas_call(
        _after(body, n_in, after), name="dh_gradx_a" if prev is None else "dh_gradx_b", grid=(nt,),
        out_shape=(jax.ShapeDtypeStruct((S, D_MODEL), F32), jax.ShapeDtypeStruct((1, D_MODEL), F32)),
        in_specs=[row(512), row(512), row(512), row(3072), row(1024),
                  pl.BlockSpec((D_MODEL, D_IN), c2), row(D_MODEL), pl.BlockSpec((1, D_MODEL), c2), row(D_MODEL)]
        + more + [_ANY] * len(after),
        out_specs=[row(D_MODEL), pl.BlockSpec((1, D_MODEL), c2)],
        input_output_aliases={} if prev is None else {9: 0},
        compiler_params=_params(48, dimension_semantics=("arbitrary",)),
    )(dq, dk, dv, dzt, dzs, w_in_bf, x, norm_g, d_out, *(prev or ()), *after)


def _gw_in(ht, dq, dk, dv, dzt, dzs, tn=256, after=()):
    S = ht.shape[1]
    per = 512 // tn
    cols = tuple((pc, per * blk + h) for pc, blk in _DZ_MAP for h in range(per))

    def body(ht_ref, dq_ref, dk_ref, dv_ref, dzt_ref, dzs_ref, o_ref):
        j = pl.program_id(0)
        pieces = (dq_ref, dk_ref, dv_ref, dzt_ref, dzs_ref)
        for pc in range(5):
            hit = functools.reduce(jnp.logical_or, [j == jj for jj, (p, _) in enumerate(cols) if p == pc])

            @pl.when(hit)
            def _(pc=pc):
                o_ref[...] = _dot(ht_ref[...], pieces[pc][...])

    def piece_spec(pc):
        cur = next(blk for p, blk in cols if p == pc)
        held = []
        for p, blk in cols:
            cur = blk if p == pc else cur
            held.append(cur)

        def index_map(j):
            blk = jnp.int32(held[0])
            for jj in range(1, len(held)):
                if held[jj] != held[jj - 1]:
                    blk = jnp.where(j >= jj, jnp.int32(held[jj]), blk)
            return (0, blk)

        return pl.BlockSpec((S, tn), index_map)

    return pl.pallas_call(
        _after(body, 6, after), name="gw_in", grid=(len(cols),),
        out_shape=jax.ShapeDtypeStruct((D_MODEL, D_IN), F32),
        in_specs=[pl.BlockSpec((D_MODEL, S), lambda j: (0, 0))] + [piece_spec(pc) for pc in range(5)]
        + [_ANY] * len(after),
        out_specs=pl.BlockSpec((D_MODEL, tn), lambda j: (0, j)),
        compiler_params=_params(48, dimension_semantics=("arbitrary",)),
    )(ht, dq, dk, dv, dzt, dzs, *after)


_HBM = pl.BlockSpec(memory_space=pltpu.HBM)
_SEM = pl.BlockSpec(memory_space=pltpu.SEMAPHORE)
_ANY = pl.BlockSpec(memory_space=pl.ANY)
_EFFECT = pltpu.SideEffectType.DATAFLOW_SIDE_EFFECTING


def _in_hbm(a):
    return pltpu.with_memory_space_constraint(a, pltpu.HBM)


def _after(body, n_in, after):
    if not after:
        return body
    return lambda *refs: body(*refs[:n_in], *refs[n_in + len(after):])


class _Started:
    def __init__(self, send, recv, bufs, token):
        self.send, self.recv, self.bufs, self.token = send, recv, bufs, token


def _split_start(name, bufs, n_copies, copies):
    nb = len(bufs)

    def body(*refs):
        for cp in copies(refs[:nb], refs[nb], refs[nb + 1]):
            cp.start()
        refs[-1][...] = jnp.zeros_like(refs[-1])

    outs = pl.pallas_call(
        body, name=name,
        out_shape=(pltpu.SemaphoreType.DMA((n_copies,)), pltpu.SemaphoreType.DMA((n_copies,)),
                   *[pltpu.HBM(b.shape, b.dtype) for b in bufs], jax.ShapeDtypeStruct((8, 128), F32)),
        in_specs=[_HBM] * nb,
        out_specs=(_SEM, _SEM, *[_HBM] * nb, pl.BlockSpec(memory_space=pltpu.VMEM)),
        input_output_aliases={k: 2 + k for k in range(nb)},
        compiler_params=_params(1, has_side_effects=_EFFECT),
    )(*[_in_hbm(b) for b in bufs])
    return _Started(outs[0], outs[1], list(outs[2:2 + nb]), outs[-1])


def _split_wait(name, started, copies, after):
    nb = len(started.bufs)

    def body(*refs):
        for cp in copies(refs[:nb], refs[nb], refs[nb + 1]):
            cp.wait_send()
            cp.wait_recv()

    return list(pl.pallas_call(
        body, name=name,
        out_shape=tuple(pltpu.HBM(b.shape, b.dtype) for b in started.bufs),
        in_specs=[_HBM] * nb + [_SEM, _SEM, _ANY],
        out_specs=tuple([_HBM] * nb),
        input_output_aliases={k: k for k in range(nb)},
        compiler_params=_params(1, has_side_effects=_EFFECT),
    )(*started.bufs, started.send, started.recv, after))


def _x1_copies(ws):
    def copies(refs, send_sems, recv_sems):
        x, y, c, _ = _mesh_pos()
        out = []
        for k, w in enumerate(ws):
            for s in range(N_SHARD):
                out.append(pltpu.make_async_remote_copy(
                    src_ref=_UNITS[w](refs[k], s, 1 - c), dst_ref=refs[len(ws) + k].at[s],
                    send_sem=send_sems.at[N_SHARD * k + s], recv_sem=recv_sems.at[N_SHARD * k + s],
                    device_id=(x, y, 1 - c), device_id_type=MESH))
        return out
    return copies


def _x2_copies(n):
    def copies(refs, send_sems, recv_sems):
        x, y, c, chips = _mesh_pos()
        out = []
        for j, (cx, cy) in enumerate(chips):
            for k in range(n):
                out.append(pltpu.make_async_remote_copy(
                    src_ref=refs[k].at[2 * cx + cy], dst_ref=refs[n + k].at[j],
                    send_sem=send_sems.at[3 * k + j], recv_sem=recv_sems.at[3 * k + j],
                    device_id=(cx, cy, c), device_id_type=MESH))
        return out
    return copies


def _x3_copies(ws):
    def copies(refs, send_sems, recv_sems):
        x, y, c, _ = _mesh_pos()
        out = []
        for k, w in enumerate(ws):
            rows = _HALF_ROWS[w]
            mine = refs[k].at[pl.ds(_mo(c * rows, rows), rows), :]
            out.append(pltpu.make_async_remote_copy(
                src_ref=mine, dst_ref=mine, send_sem=send_sems.at[k], recv_sem=recv_sems.at[k],
                device_id=(x, y, 1 - c), device_id_type=MESH))
        return out
    return copies


def _x1_lands(ws):
    return [lax.empty((N_SHARD,) + _UNIT_SHAPES[w], F32) for w in ws]


def _x2_lands(ws):
    return [lax.empty((3,) + _UNIT_SHAPES[w], BF) for w in ws]


def _grad_add1(w, g, recv, pos):
    ur, uc = _UNIT_SHAPES[w]
    if w == 3:
        g_map = lambda s, pos: (2 * s + pos[0], 0)
    else:
        g_map = lambda s, pos: (pos[0], s)

    def body(pos_ref, g_ref, r_ref, cs_ref, csb_ref):
        v = g_ref[...] + r_ref[0]
        cs_ref[0] = v
        csb_ref[0] = v.astype(BF)

    u3 = lambda s, pos: (s, 0, 0)
    return pl.pallas_call(
        body, name=f"grad_add1_{w}",
        grid_spec=pltpu.PrefetchScalarGridSpec(
            num_scalar_prefetch=1, grid=(N_SHARD,),
            in_specs=[pl.BlockSpec((ur, uc), g_map), pl.BlockSpec((1, ur, uc), u3)],
            out_specs=[pl.BlockSpec((1, ur, uc), u3), pl.BlockSpec((1, ur, uc), u3)]),
        out_shape=(jax.ShapeDtypeStruct((N_SHARD, ur, uc), F32), jax.ShapeDtypeStruct((N_SHARD, ur, uc), BF)),
        compiler_params=_params(40, dimension_semantics=("arbitrary",)),
    )(pos, g, recv)


def _grad_add2(w, cs, recv, pos):
    ur, uc = _UNIT_SHAPES[w]
    tr = ur // 4 if w == 0 else ur
    nt = ur // tr

    def body(pos_ref, cs_ref, r_ref, o_ref):
        o_ref[...] = ((cs_ref[0] + r_ref[0].astype(F32)) + r_ref[1].astype(F32)) + r_ref[2].astype(F32)

    return pl.pallas_call(
        body, name=f"grad_add2_{w}",
        grid_spec=pltpu.PrefetchScalarGridSpec(
            num_scalar_prefetch=1, grid=(nt,),
            in_specs=[pl.BlockSpec((1, tr, uc), lambda t, pos: (pos[1], t, 0)),
                      pl.BlockSpec((3, tr, uc), lambda t, pos: (0, t, 0))],
            out_specs=pl.BlockSpec((tr, uc), lambda t, pos: (pos[0] * nt + t, 0))),
        out_shape=jax.ShapeDtypeStruct(_SHARD_SHAPES[w], F32),
        compiler_params=_params(32, dimension_semantics=("arbitrary",)),
    )(pos, cs, recv)


def _grad_xchg3(ws, halves):
    n = len(ws)

    def body(*refs):
        cps = _x3_copies(ws)(refs[:n], refs[2 * n], refs[2 * n + 1])
        for cp in cps:
            cp.start()
        for cp in cps:
            cp.wait()

    return pl.pallas_call(
        body, name="grad_xchg3",
        out_shape=tuple(jax.ShapeDtypeStruct(_SHARD_SHAPES[w], F32) for w in ws),
        in_specs=[_ANY] * n, out_specs=[_ANY] * n,
        input_output_aliases={k: k for k in range(n)},
        scratch_shapes=[pltpu.SemaphoreType.DMA((n,)), pltpu.SemaphoreType.DMA((n,))],
        compiler_params=_params(16),
    )(*halves)


def _adamw_math(w, g, m, v):
    m = ADAM_B1 * m + (1.0 - ADAM_B1) * g
    v = ADAM_B2 * v + (1.0 - ADAM_B2) * (g * g)
    m_hat = m / ADAM_C1
    v_hat = v / ADAM_C2
    delta = -ADAM_LR * (m_hat / (jnp.sqrt(v_hat) + ADAM_EPS) + ADAM_WD * w)
    return delta, m, v


def _adamw(name, w, g, m, v, tr=256, after=()):
    rows, cols = w.shape

    def body(w_ref, g_ref, m_ref, v_ref, d_ref, nm_ref, nv_ref):
        d_ref[...], nm_ref[...], nv_ref[...] = _adamw_math(w_ref[...], g_ref[...], m_ref[...], v_ref[...])

    spec = pl.BlockSpec((tr, cols), lambda i: (i, 0))
    return pl.pallas_call(
        _after(body, 4, after), name=name, grid=(rows // tr,),
        out_shape=tuple(jax.ShapeDtypeStruct((rows, cols), F32) for _ in range(3)),
        in_specs=[spec] * 4 + [_ANY] * len(after), out_specs=[spec] * 3,
        compiler_params=_params(32, dimension_semantics=("arbitrary",)),
    )(w, g, m, v, *after)


_REL_PAD = 384
_VEC_FIELDS = (("norm_g", 0, D_MODEL), ("b_gate", 1024, 2 * D_MODEL), ("sgu_ln_g", 3072, D_B),
               ("sgu_ln_b", 3584, D_B), ("b_s", 4096, N_GROUPS * 128), ("final_g", 4608, D_MODEL))
_LOSS_OFF = 5632
_REL_OFF = 5760
_NV = _REL_OFF + N_HEADS * _REL_PAD
_N_FIELDS = len(_VEC_FIELDS) + 2


def _small_reduce_adamw(grads, loss_row, params, after=()):
    n_in = _N_FIELDS + 1 + 3 * _N_FIELDS
    b_s_field = [f[0] for f in _VEC_FIELDS].index("b_s")

    def body(*refs):
        g_refs, loss_ref = refs[:_N_FIELDS], refs[_N_FIELDS]
        p_refs = [refs[_N_FIELDS + 1 + k * _N_FIELDS:_N_FIELDS + 1 + (k + 1) * _N_FIELDS] for k in range(3)]
        outs = refs[n_in:n_in + 4 * _N_FIELDS + 1]
        mine_v, gath_v, gath_w, wmv, send_sems, recv_sems = refs[n_in + 4 * _N_FIELDS + 1:]
        x, y, c, chips = _mesh_pos()
        me, sibling = (x, y, c), (x, y, 1 - c)

        def assemble(dst, fields, transposed_b_s):
            for f, (_, off, n) in enumerate(_VEC_FIELDS):
                if transposed_b_s and f == b_s_field:
                    t = fields[f][...].T
                    for g in range(N_GROUPS):
                        dst[:, off + 128 * g:off + 128 * (g + 1)] = t[g:g + 1, :]
                else:
                    dst[:, off:off + n] = fields[f][...]
            for r in range(N_HEADS):
                dst[:, _REL_OFF + _REL_PAD * r:_REL_OFF + _REL_PAD * (r + 1)] = fields[len(_VEC_FIELDS)][r:r + 1, :]

        assemble(mine_v, g_refs, True)
        mine_v[:, _LOSS_OFF:_LOSS_OFF + 128] = loss_ref[...]
        mine_w = g_refs[-1]
        my_k = 4 * x + 2 * y + c
        gath_v[my_k] = mine_v[...]
        gath_w[my_k] = mine_w[...]

        def copy(k, gath, block, to, src=None):
            dst = gath.at[4 * block[0] + 2 * block[1] + block[2]]
            return pltpu.make_async_remote_copy(
                src_ref=dst if src is None else src, dst_ref=dst,
                send_sem=send_sems.at[k], recv_sem=recv_sems.at[k], device_id=to, device_id_type=MESH)

        bufs = ((gath_v, mine_v), (gath_w, mine_w))
        first, passed = [], []
        for b, (gath, mine) in enumerate(bufs):
            first.append(copy(7 * b, gath, me, sibling, src=mine))
            first += [copy(7 * b + 1 + j, gath, me, (*chip, c), src=mine) for j, chip in enumerate(chips)]
        for cp in first:
            cp.start()
        for b, (gath, _) in enumerate(bufs):
            for j, chip in enumerate(chips):
                copy(7 * b + 1 + j, gath, (*chip, c), me).wait_recv()
                cp = copy(7 * b + 4 + j, gath, (*chip, c), sibling)
                cp.start()
                passed.append(cp)
        for b, (gath, _) in enumerate(bufs):
            copy(7 * b, gath, sibling, me).wait_recv()
            for j, chip in enumerate(chips):
                copy(7 * b + 4 + j, gath, (*chip, 1 - c), me).wait_recv()
        for cp in first + passed:
            cp.wait_send()

        tot_v, tot_w = gath_v[0], gath_w[0]
        for k in range(1, 8):
            tot_v = tot_v + gath_v[k]
            tot_w = tot_w + gath_w[k]
        for k in range(3):
            assemble(wmv.at[k], p_refs[k], False)
            wmv[k, :, _LOSS_OFF:_LOSS_OFF + 128] = jnp.zeros((1, 128), F32)
        res_v = (tot_v,) + _adamw_math(wmv[0], tot_v, wmv[1], wmv[2])
        res_w = (tot_w,) + _adamw_math(p_refs[0][-1][...], tot_w, p_refs[1][-1][...], p_refs[2][-1][...])
        for kind in range(4):
            o = outs[kind * _N_FIELDS:(kind + 1) * _N_FIELDS]
            for f, (_, off, n) in enumerate(_VEC_FIELDS):
                o[f][...] = res_v[kind][:, off:off + n]
            for r in range(N_HEADS):
                o[len(_VEC_FIELDS)][r:r + 1, :] = res_v[kind][:, _REL_OFF + _REL_PAD * r:_REL_OFF + _REL_PAD * (r + 1)]
            o[-1][...] = res_w[kind]
        outs[-1][...] = tot_v[:, _LOSS_OFF:_LOSS_OFF + 128]

    field_shapes = [(1, n) for _, _, n in _VEC_FIELDS] + [(N_HEADS, _REL_PAD), (N_GROUPS * 128, 128)]
    vm = pl.BlockSpec(memory_space=pltpu.VMEM)
    operands = list(grads) + [loss_row] + [a for p in params for a in p]
    assert len(operands) == n_in
    outs = pl.pallas_call(
        _after(body, n_in, after), name="small_reduce_adamw",
        out_shape=tuple(jax.ShapeDtypeStruct(s, F32) for _ in range(4) for s in field_shapes)
        + (jax.ShapeDtypeStruct((1, 128), F32),),
        in_specs=[vm] * n_in + [_ANY] * len(after), out_specs=[vm] * (4 * _N_FIELDS + 1),
        scratch_shapes=[pltpu.VMEM((1, _NV), F32), pltpu.VMEM((8, 1, _NV), F32),
                        pltpu.VMEM((8, N_GROUPS * 128, 128), F32), pltpu.VMEM((3, 1, _NV), F32),
                        pltpu.SemaphoreType.DMA((14,)), pltpu.SemaphoreType.DMA((14,))],
        compiler_params=_params(32),
    )(*operands, *after)
    return [outs[k * _N_FIELDS:(k + 1) * _N_FIELDS] for k in range(4)], outs[-1]


def _small_fields(norm_g, b_gate, ln_g, ln_b, b_s, final_g, rel_bias, w_s):
    rel = jnp.pad(rel_bias.reshape(N_HEADS, N_REL), ((0, 0), (0, _REL_PAD - N_REL)))
    return (norm_g, b_gate, ln_g, ln_b, b_s.reshape(1, N_GROUPS * 128), final_g.reshape(1, D_MODEL),
            rel, w_s.reshape(N_GROUPS * 128, 128))


def _small_outputs(fields):
    n_g, b_g, l_g, l_b, b_s, f_g, rel, w_s = fields
    return (n_g, b_g, rel[:, :N_REL].reshape(1, N_HEADS, N_REL), l_g, l_b,
            w_s.reshape(1, N_GROUPS, 128, 128), b_s.reshape(1, N_GROUPS, 128), f_g.reshape(D_MODEL))


def _bias_row(rel_bias):
    hi = rel_bias[:, N_REL - 1:N_REL]
    lo = rel_bias[:, 0:1]
    return jnp.concatenate([jnp.broadcast_to(hi, (N_HEADS, 384)), rel_bias[:, ::-1],
                            jnp.broadcast_to(lo, (N_HEADS, 191)), jnp.broadcast_to(hi, (N_HEADS, 192))], axis=1)


def kernel(x, norm_g, w_in, b_gate, rel_bias, sgu_ln_g, sgu_ln_b, w_s, b_s, w_pa, w_pb, w_out, final_g, loss_target, m_norm_g, m_w_in, m_b_gate, m_rel_bias, m_sgu_ln_g, m_sgu_ln_b, m_w_s, m_b_s, m_w_pa, m_w_pb, m_w_out, m_final_g, v_norm_g, v_w_in, v_b_gate, v_rel_bias, v_sgu_ln_g, v_sgu_ln_b, v_w_s, v_b_s, v_w_pa, v_w_pb, v_w_out, v_final_g):
    S = x.shape[1]
    xs = x.reshape(S, D_MODEL)
    tgt = loss_target.reshape(S, D_MODEL)
    big_w = (w_in[0], w_pa[0], w_pb[0], w_out[0])
    big_m = (m_w_in[0], m_w_pa[0], m_w_pb[0], m_w_out[0])
    big_v = (v_w_in[0], v_w_pa[0], v_w_pb[0], v_w_out[0])
    rel = rel_bias[0]
    ws = w_s[0]
    bst = b_s[0].T
    fg = final_g.reshape(1, D_MODEL)
    pos = jnp.stack([lax.axis_index("c"), 2 * lax.axis_index("x") + lax.axis_index("y")]).astype(jnp.int32)

    w_in_bf, w_pa_bf, w_pb_bf, w_out_bf = _ag_weights(*big_w)

    ht, q3, k3, v3, zrest = _inproj_fwd(xs, norm_g, w_in_bf)
    gp = _bias_row(rel)
    att = _attn_fwd(q3, k3, v3, gp)
    sg = _sgu_fwd(zrest, sgu_ln_g, sgu_ln_b, ws, bst)
    (d_out, d_att, d_sg, dzt, gw_out, gw_pa, gw_pb, g_bgate, g_final, loss_row) = _tail(
        att, sg, zrest, xs, tgt, w_pa_bf, w_pb_bf, w_out_bf, b_gate, fg)
    ws_s, ws_i = (1, 2, 3), (0,)
    names = ("adamw_w_in", "adamw_w_pa", "adamw_w_pb", "adamw_w_out")

    x1s = _split_start("gx1s_start", [gw_pa, gw_pb, gw_out] + _x1_lands(ws_s), 12, _x1_copies(ws_s))
    dq, dk, dv, d_gp = _attn_bwd(q3, k3, v3, d_att, gp, after=(x1s.token,))
    got = _split_wait("gx1s_wait", x1s, _x1_copies(ws_s), dq)
    sums_s = [_grad_add1(w, got[k], got[3 + k], pos) for k, w in enumerate(ws_s)]

    x2s = _split_start("gx2s_start", [s[1] for s in sums_s] + _x2_lands(ws_s), 9, _x2_copies(3))
    dzs, g_ws, g_bs_t, g_lng, g_lnb = _sgu_bwd(zrest, d_sg, sgu_ln_g, sgu_ln_b, ws, bst, after=(x2s.token,))
    gw_in = _gw_in(ht, dq, dk, dv, dzt, dzs)
    got = _split_wait("gx2s_wait", x2s, _x2_copies(3), gw_in)
    halves_s = [_grad_add2(w, sums_s[k][0], got[3 + k], pos) for k, w in enumerate(ws_s)]

    x3s = _split_start("gx3s_start", halves_s, 3, _x3_copies(ws_s))
    x1i = _split_start("gx1i_start", [gw_in] + _x1_lands(ws_i), 4, _x1_copies(ws_i))
    dh_args = (dq, dk, dv, dzt, dzs, w_in_bf, xs, norm_g, d_out)
    part = _dh_gradx(*dh_args, after=(x3s.token, x1i.token))
    g_shards_s = _split_wait("gx3s_wait", x3s, _x3_copies(ws_s), part[0])
    got = _split_wait("gx1i_wait", x1i, _x1_copies(ws_i), part[0])
    sum_i = _grad_add1(0, got[0], got[1], pos)

    x2i = _split_start("gx2i_start", [sum_i[1]] + _x2_lands(ws_i), 3, _x2_copies(1))
    grad_x, g_norm = _dh_gradx(*dh_args, prev=part, after=(x2i.token,))
    big = [None] * 4
    for k, w in enumerate(ws_s):
        big[w] = _adamw(names[w], big_w[w], g_shards_s[k], big_m[w], big_v[w],
                        tr=128 if w == 3 else 256, after=(x2i.token,))

    g_rel = jnp.pad(d_gp[:, 384:384 + N_REL][:, ::-1], ((0, 0), (0, _REL_PAD - N_REL)))
    small_grads = (g_norm, g_bgate, g_lng, g_lnb, g_bs_t, g_final, g_rel, g_ws.reshape(N_GROUPS * 128, 128))
    small_params = (_small_fields(norm_g, b_gate, sgu_ln_g, sgu_ln_b, b_s, final_g, rel_bias, w_s),
                    _small_fields(m_norm_g, m_b_gate, m_sgu_ln_g, m_sgu_ln_b, m_b_s, m_final_g, m_rel_bias, m_w_s),
                    _small_fields(v_norm_g, v_b_gate, v_sgu_ln_g, v_sgu_ln_b, v_b_s, v_final_g, v_rel_bias, v_w_s))
    (gsum, sdelta, sm, sv), loss_out = _small_reduce_adamw(small_grads, loss_row, small_params, after=(x2i.token,))

    got = _split_wait("gx2i_wait", x2i, _x2_copies(1), loss_out)
    half_i = _grad_add2(0, sum_i[0], got[1], pos)
    g_shard_i, = _grad_xchg3(ws_i, [half_i])
    big[0] = _adamw(names[0], big_w[0], g_shard_i, big_m[0], big_v[0])
    g_shards = [g_shard_i] + list(g_shards_s)
    sg_out, sd_out, sm_out, sv_out = (_small_outputs(f) for f in (gsum, sdelta, sm, sv))
    loss = loss_out[0, 0]

    def assemble(small, bigs):
        n_g, b_g, r_b, l_g, l_b, w_s_, b_s_, f_g = small
        b_in, b_pa, b_pb, b_out = (b[None] for b in bigs)
        return (n_g, b_in, b_g, r_b, l_g, l_b, w_s_, b_s_, b_pa, b_pb, b_out, f_g)

    grads_out = assemble(sg_out, g_shards)
    delta_out = assemble(sd_out, [b[0] for b in big])
    m_out = assemble(sm_out, [b[1] for b in big])
    v_out = assemble(sv_out, [b[2] for b in big])
    return (loss, grad_x.reshape(1, S, D_MODEL), *grads_out, *delta_out, *m_out, *v_out)
```

```python
import functools
import math

import jax
import jax.numpy as jnp
from jax import lax
from jax.experimental import pallas as pl
from jax.experimental.pallas import tpu as pltpu

F32 = jnp.float32
BF = jnp.bfloat16
MESH = pl.DeviceIdType.MESH

D_MODEL = 1024
D_A = 512
D_B = 512
D_IN = 5632
N_HEADS = 8
HEAD_DIM = 64
CHUNK = 64
N_PREV = 8
SGU_CHUNK = 128
N_GROUPS = 4
N_REL = 257
EPS = 1e-6
NEG_INF = -1e30
SCALE = HEAD_DIM ** -0.5

QB = 2 * CHUNK
KB = (N_PREV + 2) * CHUNK
PADK = N_PREV * CHUNK
ROLL_W = 1024
N_RING = KB // QB
KEEP = N_RING - 1

ADAM_LR = 0.001
ADAM_B1 = 0.9
ADAM_B2 = 0.999
ADAM_EPS = 1e-08
ADAM_WD = 0.01
ADAM_STEP = 10
ADAM_C1 = 1.0 - ADAM_B1 ** ADAM_STEP
ADAM_C2 = 1.0 - ADAM_B2 ** ADAM_STEP

N_SHARD = 4
SHARD_IN = D_IN // N_SHARD
MIB = 1024 * 1024


VMEM_RESERVE_MIB = 60


def _params(vmem_mib, **kw):
    assert vmem_mib <= VMEM_RESERVE_MIB
    return pltpu.CompilerParams(vmem_limit_bytes=VMEM_RESERVE_MIB * MIB, **kw)


def _sigmoid(x):
    return 1.0 / (1.0 + jnp.exp(-x))


def _silu_and_grad(x):
    s = _sigmoid(x)
    return x * s, s * (1.0 + x * (1.0 - s))


_GELU_C = math.sqrt(2.0 / math.pi)
_GELU_A = 0.044715


def _gelu_and_grad(x):
    x2 = x * x
    t = jnp.tanh(_GELU_C * (x + _GELU_A * (x2 * x)))
    cdf = 0.5 * (1.0 + t)
    grad = cdf + 0.5 * x * (1.0 - t * t) * (_GELU_C * (1.0 + 3.0 * _GELU_A * x2))
    return x * cdf, grad


def _dot(a, b):
    return jnp.dot(a, b, preferred_element_type=F32)


def _dot_nt(a, b):
    return lax.dot_general(a, b, (((1,), (1,)), ((), ())), preferred_element_type=F32)


def _dot_tn(a, b):
    return lax.dot_general(a, b, (((0,), (0,)), ((), ())), preferred_element_type=F32)


def _mo(v, m):
    return v if isinstance(v, int) else pl.multiple_of(v, m)


def _unit_in(ref, s, p):
    return ref.at[pl.ds(_mo(p * 512, 512), 512), pl.ds(_mo(s * SHARD_IN, 128), SHARD_IN)]


def _unit_p(ref, s, p):
    return ref.at[pl.ds(_mo(p * 256, 256), 256), pl.ds(_mo(s * 256, 128), 256)]


def _unit_out(ref, s, p):
    return ref.at[pl.ds(_mo(s * 256 + p * 128, 128), 128), :]


_UNITS = (_unit_in, _unit_p, _unit_p, _unit_out)
_HALF_ROWS = (512, 256, 256, 128)
_UNIT_SHAPES = ((512, SHARD_IN), (256, 256), (256, 256), (128, D_MODEL))
_FULL_SHAPES = ((D_MODEL, D_IN), (D_A, D_MODEL), (D_B, D_MODEL), (D_MODEL, D_MODEL))
_SHARD_SHAPES = ((D_MODEL, SHARD_IN), (D_A, 256), (D_B, 256), (256, D_MODEL))


def _mesh_pos():
    x, y, c = lax.axis_index("x"), lax.axis_index("y"), lax.axis_index("c")
    chips = [(1 - x, y), (x, 1 - y), (1 - x, 1 - y)]
    return x, y, c, chips


def _ag_weights(w_in, w_pa, w_pb, w_out):
    def body(i0, i1, i2, i3, o0, o1, o2, o3, s0, s1, s2, s3, send_sems, recv_sems, local_sems):
        ins, outs, stage = (i0, i1, i2, i3), (o0, o1, o2, o3), (s0, s1, s2, s3)
        x, y, c, chips = _mesh_pos()
        s_me = 2 * x + y
        sibling = (x, y, 1 - c)
        for w in range(4):
            stage[w][...] = ins[w][...].astype(BF)

        def half(w, p):
            rows = _HALF_ROWS[w]
            return stage[w].at[pl.ds(_mo(p * rows, rows), rows), :]

        local = []
        for w in range(4):
            for p in range(2):
                cp = pltpu.make_async_copy(half(w, p), _UNITS[w](outs[w], s_me, p), local_sems.at[w, p])
                cp.start()
                local.append(cp)

        def rcopy(w, k, src, dst, to):
            return pltpu.make_async_remote_copy(src_ref=src, dst_ref=dst, send_sem=send_sems.at[w, k],
                                                recv_sem=recv_sems.at[w, k], device_id=to, device_id_type=MESH)

        sends = []
        for j, (cx, cy) in enumerate(chips):
            for w in range(4):
                cp = rcopy(w, j, half(w, c), _UNITS[w](outs[w], s_me, c), (cx, cy, c))
                cp.start()
                sends.append(cp)
        for j, (cx, cy) in enumerate(chips):
            s_j = 2 * cx + cy
            for w in range(4):
                landed = _UNITS[w](outs[w], s_j, c)
                rcopy(w, j, landed, landed, (cx, cy, c)).wait_recv()
                cp = rcopy(w, 3 + j, landed, landed, sibling)
                cp.start()
                sends.append(cp)
        for j, (cx, cy) in enumerate(chips):
            s_j = 2 * cx + cy
            for w in range(4):
                other = _UNITS[w](outs[w], s_j, 1 - c)
                rcopy(w, 3 + j, other, other, sibling).wait_recv()
        for cp in sends:
            cp.wait_send()
        for cp in local:
            cp.wait()

    vm = pl.BlockSpec(memory_space=pltpu.VMEM)
    hbm = pl.BlockSpec(memory_space=pl.ANY)
    return pl.pallas_call(
        body, name="ag_weights",
        out_shape=tuple(jax.ShapeDtypeStruct(s, BF) for s in _FULL_SHAPES),
        in_specs=[vm] * 4, out_specs=[hbm] * 4,
        scratch_shapes=[pltpu.VMEM(s, BF) for s in _SHARD_SHAPES]
        + [pltpu.SemaphoreType.DMA((4, 6)), pltpu.SemaphoreType.DMA((4, 6)), pltpu.SemaphoreType.DMA((4, 2))],
        compiler_params=_params(40),
    )(w_in, w_pa, w_pb, w_out)


def _inproj_fwd(x, norm_g, w_in_bf, tm=512):
    S = x.shape[0]

    def body(x_ref, g_ref, w_ref, ht_ref, q_ref, k_ref, v_ref, zr_ref):
        xv = x_ref[...]
        r = lax.rsqrt(jnp.mean(xv * xv, axis=-1, keepdims=True) + EPS)
        hf = (xv * r) * g_ref[...]
        ht_ref[...] = hf.T.astype(BF)
        h = hf.astype(BF)
        heads = (q_ref, k_ref, v_ref)
        for j in range(D_IN // 512):
            z = _dot(h, w_ref[:, j * 512:(j + 1) * 512])
            if j < 3:
                zb = z.astype(BF)
                for hd in range(N_HEADS):
                    heads[j][hd] = zb[:, hd * HEAD_DIM:(hd + 1) * HEAD_DIM]
            else:
                zr_ref[:, (j - 3) * 512:(j - 2) * 512] = z

    head_major = jax.ShapeDtypeStruct((N_HEADS, S, HEAD_DIM), BF)
    head_spec = pl.BlockSpec((N_HEADS, tm, HEAD_DIM), lambda i: (0, i, 0))
    return pl.pallas_call(
        body, name="inproj_fwd", grid=(S // tm,),
        out_shape=(jax.ShapeDtypeStruct((D_MODEL, S), BF), head_major, head_major, head_major,
                   jax.ShapeDtypeStruct((S, D_IN - 3 * D_A), F32)),
        in_specs=[pl.BlockSpec((tm, D_MODEL), lambda i: (i, 0)),
                  pl.BlockSpec((1, D_MODEL), lambda i: (0, 0)),
                  pl.BlockSpec((D_MODEL, D_IN), lambda i: (0, 0), pipeline_mode=pl.Buffered(1))],
        out_specs=[pl.BlockSpec((D_MODEL, tm), lambda i: (0, i)),
                   head_spec, head_spec, head_spec,
                   pl.BlockSpec((tm, D_IN - 3 * D_A), lambda i: (i, 0))],
        compiler_params=_params(52, dimension_semantics=("arbitrary",)),
    )(x, norm_g, w_in_bf)


def _skew_table(gp_row):
    row = lax.broadcasted_iota(jnp.int32, (QB, ROLL_W), 0)
    t = jnp.broadcast_to(gp_row, (QB, ROLL_W))
    for b in range(7):
        t = jnp.where(((row >> b) & 1) == 1, pltpu.roll(t, 1 << b, axis=1), t)
    return t


def _unskew_sum(d):
    row = lax.broadcasted_iota(jnp.int32, (QB, ROLL_W), 0)
    for b in range(7):
        d = jnp.where(((row >> b) & 1) == 1, pltpu.roll(d, ROLL_W - (1 << b), axis=1), d)
    return jnp.sum(d, axis=0, keepdims=True)


def _struct_mask():
    a = lax.broadcasted_iota(jnp.int32, (QB, KB), 0) // CHUNK
    b = lax.broadcasted_iota(jnp.int32, (QB, KB), 1) // CHUNK
    return (b >= a) & (b <= a + N_PREV)


def _load_kv(k_hbm, v_hbm, gp_ref, k_scr, v_scr, bias_scr, sems, S):
    zeros = jnp.zeros((N_HEADS, PADK, HEAD_DIM), BF)
    k_scr[:, 0:PADK, :] = zeros
    v_scr[:, 0:PADK, :] = zeros
    ck = pltpu.make_async_copy(k_hbm, k_scr.at[:, pl.ds(PADK, S), :], sems.at[0])
    cv = pltpu.make_async_copy(v_hbm, v_scr.at[:, pl.ds(PADK, S), :], sems.at[1])
    ck.start()
    cv.start()
    keep = _struct_mask()
    for h in range(N_HEADS):
        bias_scr[h] = jnp.where(keep, _skew_table(gp_ref[h:h + 1, :])[:, :KB], NEG_INF)
    ck.wait()
    cv.wait()


_BATCH_NT = (((2,), (2,)), ((0,), (0,)))
_BATCH_NN = (((2,), (1,)), ((0,), (0,)))
_BATCH_TN = (((1,), (1,)), ((0,), (0,)))


def _bdot(a, b, dims):
    return lax.dot_general(a, b, dims, preferred_element_type=F32)


def _scaled(q):
    return q * jnp.asarray(SCALE, BF)


def _probs(qs, kb, bias, i, front):
    s = _bdot(qs, kb, _BATCH_NT) + bias
    if front:
        col = lax.broadcasted_iota(jnp.int32, (1, 1, KB), 2)
        s = jnp.where(col >= PADK - i * QB, s, NEG_INF)
    m = jnp.max(s, axis=-1, keepdims=True)
    e = jnp.exp(s - m)
    return e * (1.0 / jnp.sum(e, axis=-1, keepdims=True))


def _attn_fwd(q3, k3, v3, gp):
    S = q3.shape[1]

    def body(q_ref, k_hbm, v_hbm, gp_ref, o_ref, k_scr, v_scr, bias_scr, sems):
        i = pl.program_id(0)

        @pl.when(i == 0)
        def _():
            _load_kv(k_hbm, v_hbm, gp_ref, k_scr, v_scr, bias_scr, sems, S)

        def step(front):
            start = pl.multiple_of(i * QB, QB)
            kb = k_scr[:, pl.ds(start, KB), :]
            vb = v_scr[:, pl.ds(start, KB), :]
            p = _probs(_scaled(q_ref[...]), kb, bias_scr[...], i, front)
            o = _bdot(p.astype(BF), vb, _BATCH_NN)
            for h in range(N_HEADS):
                o_ref[:, h * HEAD_DIM:(h + 1) * HEAD_DIM] = o[h]

        pl.when(i < KEEP)(functools.partial(step, True))
        pl.when(i >= KEEP)(functools.partial(step, False))

    kv_scr = pltpu.VMEM((N_HEADS, S + PADK, HEAD_DIM), BF)
    return pl.pallas_call(
        body, name="attn_fwd", grid=(S // QB,),
        out_shape=jax.ShapeDtypeStruct((S, D_A), F32),
        in_specs=[pl.BlockSpec((N_HEADS, QB, HEAD_DIM), lambda i: (0, i, 0)),
                  pl.BlockSpec(memory_space=pl.ANY), pl.BlockSpec(memory_space=pl.ANY),
                  pl.BlockSpec((N_HEADS, ROLL_W), lambda i: (0, 0))],
        out_specs=pl.BlockSpec((QB, D_A), lambda i: (i, 0)),
        scratch_shapes=[kv_scr, kv_scr, pltpu.VMEM((N_HEADS, QB, KB), F32), pltpu.SemaphoreType.DMA((2,))],
        compiler_params=_params(48, dimension_semantics=("arbitrary",)),
    )(q3, k3, v3, gp)


def _attn_bwd(q3, k3, v3, d_att3, gp, after=()):
    S = q3.shape[1]
    nq = S // QB

    def body(q_ref, do_ref, k_hbm, v_hbm, gp_ref, dq_ref, dk_ref, dv_ref, dgp_ref,
             k_scr, v_scr, bias_scr, dk_acc, dv_acc, dbias_acc, pad_scr, sems):
        i = pl.program_id(0)

        @pl.when(i == 0)
        def _():
            _load_kv(k_hbm, v_hbm, gp_ref, k_scr, v_scr, bias_scr, sems, S)
            dk_acc[...] = jnp.zeros_like(dk_acc)
            dv_acc[...] = jnp.zeros_like(dv_acc)
            dbias_acc[...] = jnp.zeros_like(dbias_acc)

        def step(front):
            start = pl.multiple_of(i * QB, QB)
            kb = k_scr[:, pl.ds(start, KB), :]
            vb = v_scr[:, pl.ds(start, KB), :]
            qs = _scaled(q_ref[...])
            do = do_ref[...]
            p = _probs(qs, kb, bias_scr[...], i, front)
            dp = _bdot(do, vb, _BATCH_NT)
            ds = p * (dp - jnp.sum(dp * p, axis=-1, keepdims=True))
            dbias_acc[...] += ds
            dsb = ds.astype(BF)
            dq = _bdot(dsb, kb, _BATCH_NN) * SCALE
            for h in range(N_HEADS):
                dq_ref[:, h * HEAD_DIM:(h + 1) * HEAD_DIM] = dq[h].astype(BF)
            dk_acc[...] += _bdot(dsb, qs, _BATCH_TN)
            dv_acc[...] += _bdot(p.astype(BF), do, _BATCH_TN)

        pl.when(i < KEEP)(functools.partial(step, True))
        pl.when((i >= KEEP) & (i < nq))(functools.partial(step, False))

        for h in range(N_HEADS):
            hs = slice(h * HEAD_DIM, (h + 1) * HEAD_DIM)
            dk_ref[:, hs] = dk_acc[h, 0:QB, :].astype(BF)
            dv_ref[:, hs] = dv_acc[h, 0:QB, :].astype(BF)
        dk_acc[:, 0:KB - QB, :] = dk_acc[:, QB:KB, :]
        dv_acc[:, 0:KB - QB, :] = dv_acc[:, QB:KB, :]
        dk_acc[:, KB - QB:KB, :] = jnp.zeros((N_HEADS, QB, HEAD_DIM), F32)
        dv_acc[:, KB - QB:KB, :] = jnp.zeros((N_HEADS, QB, HEAD_DIM), F32)

        @pl.when(i == nq + KEEP - 1)
        def _():
            lane = lax.broadcasted_iota(jnp.int32, (1, ROLL_W), 1)
            hi = (lane < 384) | (lane >= 832)
            lo = (lane > 640) & (lane < 832)
            pad_scr[...] = jnp.zeros_like(pad_scr)
            for h in range(N_HEADS):
                pad_scr[:, 0:KB] = dbias_acc[h]
                g = _unskew_sum(pad_scr[...])
                s_hi = jnp.sum(jnp.where(hi, g, 0.0), axis=-1, keepdims=True)
                s_lo = jnp.sum(jnp.where(lo, g, 0.0), axis=-1, keepdims=True)
                g = jnp.where(lane == 384, g + s_hi, g)
                g = jnp.where(lane == 640, g + s_lo, g)
                dgp_ref[h:h + 1, :] = g

    last = nq - 1
    kv_scr = pltpu.VMEM((N_HEADS, S + PADK, HEAD_DIM), BF)
    return pl.pallas_call(
        _after(body, 5, after), name="attn_bwd", grid=(nq + KEEP,),
        out_shape=(jax.ShapeDtypeStruct((S, D_A), BF), jax.ShapeDtypeStruct((S, D_A), BF),
                   jax.ShapeDtypeStruct((S, D_A), BF), jax.ShapeDtypeStruct((N_HEADS, ROLL_W), F32)),
        in_specs=[pl.BlockSpec((N_HEADS, QB, HEAD_DIM), lambda i: (0, jnp.minimum(i, last), 0)),
                  pl.BlockSpec((N_HEADS, QB, HEAD_DIM), lambda i: (0, jnp.minimum(i, last), 0)),
                  pl.BlockSpec(memory_space=pl.ANY), pl.BlockSpec(memory_space=pl.ANY),
                  pl.BlockSpec((N_HEADS, ROLL_W), lambda i: (0, 0))] + [_ANY] * len(after),
        out_specs=[pl.BlockSpec((QB, D_A), lambda i: (jnp.minimum(i, last), 0)),
                   pl.BlockSpec((QB, D_A), lambda i: (jnp.maximum(i - KEEP, 0), 0)),
                   pl.BlockSpec((QB, D_A), lambda i: (jnp.maximum(i - KEEP, 0), 0)),
                   pl.BlockSpec((N_HEADS, ROLL_W), lambda i: (0, 0))],
        scratch_shapes=[kv_scr, kv_scr, pltpu.VMEM((N_HEADS, QB, KB), F32),
                        pltpu.VMEM((N_HEADS, KB, HEAD_DIM), F32), pltpu.VMEM((N_HEADS, KB, HEAD_DIM), F32),
                        pltpu.VMEM((N_HEADS, QB, KB), F32), pltpu.VMEM((QB, ROLL_W), F32),
                        pltpu.SemaphoreType.DMA((2,))],
        compiler_params=_params(56, dimension_semantics=("arbitrary",)),
    )(q3, d_att3, k3, v3, gp, *after)


def _sgu_core(ub, vb, lg, lb):
    u, du = _gelu_and_grad(ub)
    v, dv = _gelu_and_grad(vb)
    mu = jnp.mean(v, axis=-1, keepdims=True)
    vc = v - mu
    rstd = lax.rsqrt(jnp.mean(vc * vc, axis=-1, keepdims=True) + EPS)
    xh = vc * rstd
    vn = xh * lg + lb
    return u, du, dv, rstd, xh, vn


def _tri():
    r = lax.broadcasted_iota(jnp.int32, (SGU_CHUNK, SGU_CHUNK), 0)
    c = lax.broadcasted_iota(jnp.int32, (SGU_CHUNK, SGU_CHUNK), 1)
    return r >= c


def _sgu_fwd(zrest, ln_g, ln_b, w_s, b_s_t, tm=512):
    S = zrest.shape[0]

    def body(ub_ref, vb_ref, lg_ref, lb_ref, ws_ref, bst_ref, sg_ref):
        u, _, _, _, _, vn = _sgu_core(ub_ref[...], vb_ref[...], lg_ref[...], lb_ref[...])
        vnb = vn.astype(BF)
        tri = _tri()
        for g in range(N_GROUPS):
            cs = slice(g * 128, (g + 1) * 128)
            wt = jnp.where(tri, ws_ref[g], 0.0).astype(BF)
            bcol = bst_ref[:, g:g + 1]
            for n in range(tm // SGU_CHUNK):
                rs = slice(n * SGU_CHUNK, (n + 1) * SGU_CHUNK)
                mixed = _dot(wt, vnb[rs, cs]) + bcol
                sg_ref[rs, cs] = u[rs, cs] * mixed

    return pl.pallas_call(
        body, name="sgu_fwd", grid=(S // tm,),
        out_shape=jax.ShapeDtypeStruct((S, D_B), F32),
        in_specs=[pl.BlockSpec((tm, 512), lambda i: (i, 1)),
                  pl.BlockSpec((tm, 512), lambda i: (i, 2)),
                  pl.BlockSpec((1, D_B), lambda i: (0, 0)),
                  pl.BlockSpec((1, D_B), lambda i: (0, 0)),
                  pl.BlockSpec((N_GROUPS, 128, 128), lambda i: (0, 0, 0)),
                  pl.BlockSpec((128, N_GROUPS), lambda i: (0, 0))],
        out_specs=pl.BlockSpec((tm, D_B), lambda i: (i, 0)),
        compiler_params=_params(32, dimension_semantics=("arbitrary",)),
    )(zrest, zrest, ln_g, ln_b, w_s, b_s_t)


def _sgu_bwd(zrest, d_sg, ln_g, ln_b, w_s, b_s_t, tm=256, after=()):
    S = zrest.shape[0]
    nt = S // tm

    def body(ub_ref, vb_ref, dsg_ref, lg_ref, lb_ref, ws_ref, bst_ref,
             dzs_ref, gws_ref, gbs_ref, glg_ref, glb_ref, dvn_scr, bs_acc):
        i = pl.program_id(0)

        @pl.when(i == 0)
        def _():
            gws_ref[...] = jnp.zeros_like(gws_ref)
            glg_ref[...] = jnp.zeros_like(glg_ref)
            glb_ref[...] = jnp.zeros_like(glb_ref)
            bs_acc[...] = jnp.zeros_like(bs_acc)

        ub = ub_ref[...]
        u, du, dv, rstd, xh, vn = _sgu_core(ub, vb_ref[...], lg_ref[...], lb_ref[...])
        vnb = vn.astype(BF)
        dsg = dsg_ref[...]
        tri = _tri()
        for g in range(N_GROUPS):
            cs = slice(g * 128, (g + 1) * 128)
            wtf = jnp.where(tri, ws_ref[g], 0.0)
            wt = wtf.astype(BF)
            wtt = wtf.T.astype(BF)
            bcol = bst_ref[:, g:g + 1]
            for n in range(tm // SGU_CHUNK):
                rs = slice(n * SGU_CHUNK, (n + 1) * SGU_CHUNK)
                mixed = _dot(wt, vnb[rs, cs]) + bcol
                dzs_ref[rs, cs] = (dsg[rs, cs] * mixed * du[rs, cs]).astype(BF)
                dmix = dsg[rs, cs] * u[rs, cs]
                bs_acc[:, cs] += dmix
                dmb = dmix.astype(BF)
                gws_ref[g] += _dot_nt(dmb, vnb[rs, cs])
                dvn_scr[rs, cs] = _dot(wtt, dmb)
        dvn = dvn_scr[...]
        glg_ref[...] += jnp.sum(dvn * xh, axis=0, keepdims=True)
        glb_ref[...] += jnp.sum(dvn, axis=0, keepdims=True)
        dxh = dvn * lg_ref[...]
        dvv = rstd * (dxh - jnp.mean(dxh, axis=-1, keepdims=True)
                      - xh * jnp.mean(dxh * xh, axis=-1, keepdims=True))
        dzs_ref[:, D_B:2 * D_B] = (dvv * dv).astype(BF)

        @pl.when(i == nt - 1)
        def _():
            lane = lax.broadcasted_iota(jnp.int32, (SGU_CHUNK, 128), 1)
            out = jnp.zeros((SGU_CHUNK, 128), F32)
            for g in range(N_GROUPS):
                gws_ref[g] = jnp.where(tri, gws_ref[g], 0.0)
                col = jnp.sum(bs_acc[:, g * 128:(g + 1) * 128], axis=-1, keepdims=True)
                out = jnp.where(lane == g, col, out)
            gbs_ref[...] = out

    const2 = lambda i: (0, 0)
    return pl.pallas_call(
        _after(body, 7, after), name="sgu_bwd", grid=(nt,),
        out_shape=(jax.ShapeDtypeStruct((S, 2 * D_B), BF),
                   jax.ShapeDtypeStruct((N_GROUPS, 128, 128), F32),
                   jax.ShapeDtypeStruct((SGU_CHUNK, 128), F32),
                   jax.ShapeDtypeStruct((1, D_B), F32), jax.ShapeDtypeStruct((1, D_B), F32)),
        in_specs=[pl.BlockSpec((tm, 512), lambda i: (i, 1)),
                  pl.BlockSpec((tm, 512), lambda i: (i, 2)),
                  pl.BlockSpec((tm, D_B), lambda i: (i, 0)),
                  pl.BlockSpec((1, D_B), const2), pl.BlockSpec((1, D_B), const2),
                  pl.BlockSpec((N_GROUPS, 128, 128), lambda i: (0, 0, 0)),
                  pl.BlockSpec((128, N_GROUPS), const2)] + [_ANY] * len(after),
        out_specs=[pl.BlockSpec((tm, 2 * D_B), lambda i: (i, 0)),
                   pl.BlockSpec((N_GROUPS, 128, 128), lambda i: (0, 0, 0)),
                   pl.BlockSpec((SGU_CHUNK, 128), const2),
                   pl.BlockSpec((1, D_B), const2), pl.BlockSpec((1, D_B), const2)],
        scratch_shapes=[pltpu.VMEM((tm, D_B), F32), pltpu.VMEM((SGU_CHUNK, D_B), F32)],
        compiler_params=_params(32, dimension_semantics=("arbitrary",)),
    )(zrest, zrest, d_sg, ln_g, ln_b, w_s, b_s_t, *after)


def _tail(att, sg, zrest, x, target, w_pa, w_pb, w_out, b_gate, final_g, tm=256):
    S = x.shape[0]
    nt = S // tm

    def body(att_ref, sg_ref, ga_ref, gb_ref, gta_ref, gtb_ref, x_ref, t_ref,
             wpa_ref, wpb_ref, wout_ref, bg_ref, fg_ref,
             dout_ref, datt_ref, dsg_ref, dzt_ref, gwout_hbm, gwpa_hbm, gwpb_hbm,
             gbg_ref, gfg_ref, loss_ref, acc_out, acc_pa, acc_pb, sems):
        i = pl.program_id(0)

        @pl.when(i == 0)
        def _():
            acc_out[...] = jnp.zeros_like(acc_out)
            acc_pa[...] = jnp.zeros_like(acc_pa)
            acc_pb[...] = jnp.zeros_like(acc_pb)
            gbg_ref[...] = jnp.zeros_like(gbg_ref)
            gfg_ref[...] = jnp.zeros_like(gfg_ref)
            loss_ref[...] = jnp.zeros_like(loss_ref)

        att = att_ref[...]
        sg = sg_ref[...]
        sa, dsa = _silu_and_grad(ga_ref[...])
        sb, dsb = _silu_and_grad(gb_ref[...])
        ya = (att * sa).astype(BF)
        yb = (sg * sb).astype(BF)
        pa = _dot(ya, wpa_ref[...])
        pb = _dot(yb, wpb_ref[...])
        ga = _sigmoid(gta_ref[...] + bg_ref[:, 0:D_MODEL])
        gb = _sigmoid(gtb_ref[...] + bg_ref[:, D_MODEL:2 * D_MODEL])
        merged = (ga * pa + gb * pb).astype(BF)
        out = x_ref[...] + _dot(merged, wout_ref[...])
        r2 = lax.rsqrt(jnp.mean(out * out, axis=-1, keepdims=True) + EPS)
        nrm = out * r2
        fg = fg_ref[...]
        err = nrm * fg - t_ref[...]
        loss_ref[...] += 0.5 * jnp.sum(jnp.mean(err * err, axis=-1, keepdims=True))
        dy = err * (1.0 / D_MODEL)
        gfg_ref[...] += jnp.sum(dy * nrm, axis=0, keepdims=True)
        dn = dy * fg
        d_out = r2 * (dn - nrm * jnp.mean(dn * nrm, axis=-1, keepdims=True))
        dout_ref[...] = d_out
        d_outb = d_out.astype(BF)
        acc_out[...] += _dot_tn(merged, d_outb)
        dm = _dot_nt(d_outb, wout_ref[...])
        d_pa = (dm * ga).astype(BF)
        d_pb = (dm * gb).astype(BF)
        d_gta = dm * pa * (ga * (1.0 - ga))
        d_gtb = dm * pb * (gb * (1.0 - gb))
        gbg_ref[:, 0:D_MODEL] += jnp.sum(d_gta, axis=0, keepdims=True)
        gbg_ref[:, D_MODEL:2 * D_MODEL] += jnp.sum(d_gtb, axis=0, keepdims=True)
        dzt_ref[:, 2 * D_A:2 * D_A + D_MODEL] = d_gta.astype(BF)
        dzt_ref[:, 2 * D_A + D_MODEL:] = d_gtb.astype(BF)
        acc_pa[...] += _dot_tn(ya, d_pa)
        acc_pb[...] += _dot_tn(yb, d_pb)
        d_ya = _dot_nt(d_pa, wpa_ref[...])
        d_yb = _dot_nt(d_pb, wpb_ref[...])
        d_att = (d_ya * sa).astype(BF)
        for hd in range(N_HEADS):
            datt_ref[hd] = d_att[:, hd * HEAD_DIM:(hd + 1) * HEAD_DIM]
        dzt_ref[:, 0:D_A] = (d_ya * att * dsa).astype(BF)
        dsg_ref[...] = d_yb * sb
        dzt_ref[:, D_A:2 * D_A] = (d_yb * sg * dsb).astype(BF)

        @pl.when(i == nt - 1)
        def _():
            cps = [pltpu.make_async_copy(acc_out, gwout_hbm, sems.at[0]),
                   pltpu.make_async_copy(acc_pa, gwpa_hbm, sems.at[1]),
                   pltpu.make_async_copy(acc_pb, gwpb_hbm, sems.at[2])]
            for cp in cps:
                cp.start()
            for cp in cps:
                cp.wait()

    c2 = lambda i: (0, 0)
    hbm = pl.BlockSpec(memory_space=pl.ANY)
    return pl.pallas_call(
        body, name="tail", grid=(nt,),
        out_shape=(jax.ShapeDtypeStruct((S, D_MODEL), F32), jax.ShapeDtypeStruct((N_HEADS, S, HEAD_DIM), BF),
                   jax.ShapeDtypeStruct((S, D_B), F32), jax.ShapeDtypeStruct((S, 3072), BF),
                   jax.ShapeDtypeStruct((D_MODEL, D_MODEL), F32), jax.ShapeDtypeStruct((D_A, D_MODEL), F32),
                   jax.ShapeDtypeStruct((D_B, D_MODEL), F32),
                   jax.ShapeDtypeStruct((1, 2 * D_MODEL), F32), jax.ShapeDtypeStruct((1, D_MODEL), F32),
                   jax.ShapeDtypeStruct((1, 128), F32)),
        in_specs=[pl.BlockSpec((tm, D_A), lambda i: (i, 0)),
                  pl.BlockSpec((tm, D_B), lambda i: (i, 0)),
                  pl.BlockSpec((tm, 512), lambda i: (i, 0)),
                  pl.BlockSpec((tm, 512), lambda i: (i, 3)),
                  pl.BlockSpec((tm, D_MODEL), lambda i: (i, 2)),
                  pl.BlockSpec((tm, D_MODEL), lambda i: (i, 3)),
                  pl.BlockSpec((tm, D_MODEL), lambda i: (i, 0)),
                  pl.BlockSpec((tm, D_MODEL), lambda i: (i, 0)),
                  pl.BlockSpec((D_A, D_MODEL), c2), pl.BlockSpec((D_B, D_MODEL), c2),
                  pl.BlockSpec((D_MODEL, D_MODEL), c2),
                  pl.BlockSpec((1, 2 * D_MODEL), c2), pl.BlockSpec((1, D_MODEL), c2)],
        out_specs=[pl.BlockSpec((tm, D_MODEL), lambda i: (i, 0)),
                   pl.BlockSpec((N_HEADS, tm, HEAD_DIM), lambda i: (0, i, 0)),
                   pl.BlockSpec((tm, D_B), lambda i: (i, 0)),
                   pl.BlockSpec((tm, 3072), lambda i: (i, 0)),
                   hbm, hbm, hbm,
                   pl.BlockSpec((1, 2 * D_MODEL), c2), pl.BlockSpec((1, D_MODEL), c2),
                   pl.BlockSpec((1, 128), c2)],
        scratch_shapes=[pltpu.VMEM((D_MODEL, D_MODEL), F32), pltpu.VMEM((D_A, D_MODEL), F32),
                        pltpu.VMEM((D_B, D_MODEL), F32), pltpu.SemaphoreType.DMA((3,))],
        compiler_params=_params(56, dimension_semantics=("arbitrary",)),
    )(att, sg, zrest, zrest, zrest, zrest, x, target, w_pa, w_pb, w_out, b_gate, final_g)


_DZ_MAP = ((0, 0), (1, 0), (2, 0), (3, 0), (4, 0), (4, 1), (3, 1), (3, 2), (3, 3), (3, 4), (3, 5))


def _dh_gradx(dq, dk, dv, dzt, dzs, w_in_bf, x, norm_g, d_out, prev=None, tm=512, after=()):
    S = x.shape[0]
    nt = S // tm // 2
    first = 0 if prev is None else nt
    n_in = 9 if prev is None else 11

    def body(dq_ref, dk_ref, dv_ref, dzt_ref, dzs_ref, w_ref, x_ref, g_ref, dout_ref, *rest):
        gx_ref, gn_ref = rest[-2:]
        i = pl.program_id(0)

        @pl.when(i == 0)
        def _():
            gn_ref[...] = jnp.zeros_like(gn_ref) if prev is None else rest[1][...]

        pieces = (dq_ref, dk_ref, dv_ref, dzt_ref, dzs_ref)
        dh = jnp.zeros((tm, D_MODEL), F32)
        for j, (pc, blk) in enumerate(_DZ_MAP):
            dh += _dot_nt(pieces[pc][:, blk * 512:(blk + 1) * 512], w_ref[:, j * 512:(j + 1) * 512])
        xv = x_ref[...]
        r = lax.rsqrt(jnp.mean(xv * xv, axis=-1, keepdims=True) + EPS)
        nrm = xv * r
        gn_ref[...] += jnp.sum(dh * nrm, axis=0, keepdims=True)
        dn = dh * g_ref[...]
        gx_ref[...] = r * (dn - nrm * jnp.mean(dn * nrm, axis=-1, keepdims=True)) + dout_ref[...]

    row = lambda w: pl.BlockSpec((tm, w), lambda i: (i + first, 0))
    c2 = lambda i: (0, 0)
    more = [] if prev is None else [_ANY, pl.BlockSpec((1, D_MODEL), c2)]
    return pl.pallas_call(
        _after(body, n_in, after), name="dh_gradx_a" if prev is None else "dh_gradx_b", grid=(nt,),
        out_shape=(jax.ShapeDtypeStruct((S, D_MODEL), F32), jax.ShapeDtypeStruct((1, D_MODEL), F32)),
        in_specs=[row(512), row(512), row(512), row(3072), row(1024),
                  pl.BlockSpec((D_MODEL, D_IN), c2, pipeline_mode=pl.Buffered(1)), row(D_MODEL),
                  pl.BlockSpec((1, D_MODEL), c2), row(D_MODEL)]
        + more + [_ANY] * len(after),
        out_specs=[row(D_MODEL), pl.BlockSpec((1, D_MODEL), c2)],
        input_output_aliases={} if prev is None else {9: 0},
        compiler_params=_params(48, dimension_semantics=("arbitrary",)),
    )(dq, dk, dv, dzt, dzs, w_in_bf, x, norm_g, d_out, *(prev or ()), *after)


def _gw_in(ht, dq, dk, dv, dzt, dzs, tn=256, after=()):
    S = ht.shape[1]
    per = 512 // tn
    cols = tuple((pc, per * blk + h) for pc, blk in _DZ_MAP for h in range(per))

    def body(ht_ref, dq_ref, dk_ref, dv_ref, dzt_ref, dzs_ref, o_ref):
        j = pl.program_id(0)
        pieces = (dq_ref, dk_ref, dv_ref, dzt_ref, dzs_ref)
        for pc in range(5):
            hit = functools.reduce(jnp.logical_or, [j == jj for jj, (p, _) in enumerate(cols) if p == pc])

            @pl.when(hit)
            def _(pc=pc):
                o_ref[...] = _dot(ht_ref[...], pieces[pc][...])

    def piece_spec(pc):
        cur = next(blk for p, blk in cols if p == pc)
        held = []
        for p, blk in cols:
            cur = blk if p == pc else cur
            held.append(cur)

        def index_map(j):
            blk = jnp.int32(held[0])
            for jj in range(1, len(held)):
                if held[jj] != held[jj - 1]:
                    blk = jnp.where(j >= jj, jnp.int32(held[jj]), blk)
            return (0, blk)

        return pl.BlockSpec((S, tn), index_map)

    return pl.pallas_call(
        _after(body, 6, after), name="gw_in", grid=(len(cols),),
        out_shape=jax.ShapeDtypeStruct((D_MODEL, D_IN), F32),
        in_specs=[pl.BlockSpec((D_MODEL, S), lambda j: (0, 0))] + [piece_spec(pc) for pc in range(5)]
        + [_ANY] * len(after),
        out_specs=pl.BlockSpec((D_MODEL, tn), lambda j: (0, j)),
        compiler_params=_params(48, dimension_semantics=("arbitrary",)),
    )(ht, dq, dk, dv, dzt, dzs, *after)


_HBM = pl.BlockSpec(memory_space=pltpu.HBM)
_SEM = pl.BlockSpec(memory_space=pltpu.SEMAPHORE)
_ANY = pl.BlockSpec(memory_space=pl.ANY)
_EFFECT = pltpu.SideEffectType.DATAFLOW_SIDE_EFFECTING


def _in_hbm(a):
    return pltpu.with_memory_space_constraint(a, pltpu.HBM)


def _after(body, n_in, after):
    if not after:
        return body
    return lambda *refs: body(*refs[:n_in], *refs[n_in + len(after):])


class _Started:
    def __init__(self, send, recv, bufs, token):
        self.send, self.recv, self.bufs, self.token = send, recv, bufs, token


def _split_start(name, bufs, n_copies, copies):
    nb = len(bufs)

    def body(*refs):
        for cp in copies(refs[:nb], refs[nb], refs[nb + 1]):
            cp.start()
        refs[-1][...] = jnp.zeros_like(refs[-1])

    outs = pl.pallas_call(
        body, name=name,
        out_shape=(pltpu.SemaphoreType.DMA((n_copies,)), pltpu.SemaphoreType.DMA((n_copies,)),
                   *[pltpu.HBM(b.shape, b.dtype) for b in bufs], jax.ShapeDtypeStruct((8, 128), F32)),
        in_specs=[_HBM] * nb,
        out_specs=(_SEM, _SEM, *[_HBM] * nb, pl.BlockSpec(memory_space=pltpu.VMEM)),
        input_output_aliases={k: 2 + k for k in range(nb)},
        compiler_params=_params(1, has_side_effects=_EFFECT),
    )(*[_in_hbm(b) for b in bufs])
    return _Started(outs[0], outs[1], list(outs[2:2 + nb]), outs[-1])


def _split_wait(name, started, copies, after):
    nb = len(started.bufs)

    def body(*refs):
        for cp in copies(refs[:nb], refs[nb], refs[nb + 1]):
            cp.wait_send()
            cp.wait_recv()

    return list(pl.pallas_call(
        body, name=name,
        out_shape=tuple(pltpu.HBM(b.shape, b.dtype) for b in started.bufs),
        in_specs=[_HBM] * nb + [_SEM, _SEM, _ANY],
        out_specs=tuple([_HBM] * nb),
        input_output_aliases={k: k for k in range(nb)},
        compiler_params=_params(1, has_side_effects=_EFFECT),
    )(*started.bufs, started.send, started.recv, after))


def _x1_copies(ws):
    def copies(refs, send_sems, recv_sems):
        x, y, c, _ = _mesh_pos()
        out = []
        for k, w in enumerate(ws):
            for s in range(N_SHARD):
                out.append(pltpu.make_async_remote_copy(
                    src_ref=_UNITS[w](refs[k], s, 1 - c), dst_ref=refs[len(ws) + k].at[s],
                    send_sem=send_sems.at[N_SHARD * k + s], recv_sem=recv_sems.at[N_SHARD * k + s],
                    device_id=(x, y, 1 - c), device_id_type=MESH))
        return out
    return copies


def _x2_copies(n):
    def copies(refs, send_sems, recv_sems):
        x, y, c, chips = _mesh_pos()
        out = []
        for j, (cx, cy) in enumerate(chips):
            for k in range(n):
                out.append(pltpu.make_async_remote_copy(
                    src_ref=refs[k].at[2 * cx + cy], dst_ref=refs[n + k].at[j],
                    send_sem=send_sems.at[3 * k + j], recv_sem=recv_sems.at[3 * k + j],
                    device_id=(cx, cy, c), device_id_type=MESH))
        return out
    return copies


def _x3_copies(ws):
    def copies(refs, send_sems, recv_sems):
        x, y, c, _ = _mesh_pos()
        out = []
        for k, w in enumerate(ws):
            rows = _HALF_ROWS[w]
            mine = refs[k].at[pl.ds(_mo(c * rows, rows), rows), :]
            out.append(pltpu.make_async_remote_copy(
                src_ref=mine, dst_ref=mine, send_sem=send_sems.at[k], recv_sem=recv_sems.at[k],
                device_id=(x, y, 1 - c), device_id_type=MESH))
        return out
    return copies


def _x1_lands(ws):
    return [lax.empty((N_SHARD,) + _UNIT_SHAPES[w], F32) for w in ws]


def _x2_lands(ws):
    return [lax.empty((3,) + _UNIT_SHAPES[w], BF) for w in ws]


def _grad_add1(w, g, recv, pos):
    ur, uc = _UNIT_SHAPES[w]
    if w == 3:
        g_map = lambda s, pos: (2 * s + pos[0], 0)
    else:
        g_map = lambda s, pos: (pos[0], s)

    def body(pos_ref, g_ref, r_ref, cs_ref, csb_ref):
        v = g_ref[...] + r_ref[0]
        cs_ref[0] = v
        csb_ref[0] = v.astype(BF)

    u3 = lambda s, pos: (s, 0, 0)
    return pl.pallas_call(
        body, name=f"grad_add1_{w}",
        grid_spec=pltpu.PrefetchScalarGridSpec(
            num_scalar_prefetch=1, grid=(N_SHARD,),
            in_specs=[pl.BlockSpec((ur, uc), g_map), pl.BlockSpec((1, ur, uc), u3)],
            out_specs=[pl.BlockSpec((1, ur, uc), u3), pl.BlockSpec((1, ur, uc), u3)]),
        out_shape=(jax.ShapeDtypeStruct((N_SHARD, ur, uc), F32), jax.ShapeDtypeStruct((N_SHARD, ur, uc), BF)),
        compiler_params=_params(40, dimension_semantics=("arbitrary",)),
    )(pos, g, recv)


def _grad_add1_group(ws, gs, recvs):
    n = len(ws)

    def body(*refs):
        c = lax.axis_index("c")
        for k, w in enumerate(ws):
            g, r, cs, csb = refs[k], refs[n + k], refs[2 * n + k], refs[3 * n + k]
            for s in range(N_SHARD):
                v = _UNITS[w](g, s, c)[...] + r[s]
                cs[s] = v
                csb[s] = v.astype(BF)

    vm = pl.BlockSpec(memory_space=pltpu.VMEM)
    outs = pl.pallas_call(
        body, name="grad_add1_group",
        out_shape=tuple(jax.ShapeDtypeStruct((N_SHARD,) + _UNIT_SHAPES[w], dt) for dt in (F32, BF) for w in ws),
        in_specs=[vm] * (2 * n), out_specs=[vm] * (2 * n),
        compiler_params=_params(32),
    )(*gs, *recvs)
    return list(outs[:n]), list(outs[n:])


def _grad_add2_group(ws, css, recvs):
    n = len(ws)

    def body(*refs):
        x, y, c, _ = _mesh_pos()
        for k, w in enumerate(ws):
            cs, r, o = refs[k], refs[n + k], refs[2 * n + k]
            rows = _HALF_ROWS[w]
            total = ((cs[2 * x + y] + r[0].astype(F32)) + r[1].astype(F32)) + r[2].astype(F32)
            o[pl.ds(_mo(c * rows, rows), rows), :] = total

    vm = pl.BlockSpec(memory_space=pltpu.VMEM)
    return list(pl.pallas_call(
        body, name="grad_add2_group",
        out_shape=tuple(jax.ShapeDtypeStruct(_SHARD_SHAPES[w], F32) for w in ws),
        in_specs=[vm] * (2 * n), out_specs=[vm] * n,
        compiler_params=_params(32),
    )(*css, *recvs))


def _grad_add2(w, cs, recv, pos):
    ur, uc = _UNIT_SHAPES[w]
    tr = ur // 4 if w == 0 else ur
    nt = ur // tr

    def body(pos_ref, cs_ref, r_ref, o_ref):
        o_ref[...] = ((cs_ref[0] + r_ref[0].astype(F32)) + r_ref[1].astype(F32)) + r_ref[2].astype(F32)

    return pl.pallas_call(
        body, name=f"grad_add2_{w}",
        grid_spec=pltpu.PrefetchScalarGridSpec(
            num_scalar_prefetch=1, grid=(nt,),
            in_specs=[pl.BlockSpec((1, tr, uc), lambda t, pos: (pos[1], t, 0)),
                      pl.BlockSpec((3, tr, uc), lambda t, pos: (0, t, 0))],
            out_specs=pl.BlockSpec((tr, uc), lambda t, pos: (pos[0] * nt + t, 0))),
        out_shape=jax.ShapeDtypeStruct(_SHARD_SHAPES[w], F32),
        compiler_params=_params(32, dimension_semantics=("arbitrary",)),
    )(pos, cs, recv)


def _grad_xchg3(ws, halves):
    n = len(ws)

    def body(*refs):
        cps = _x3_copies(ws)(refs[:n], refs[2 * n], refs[2 * n + 1])
        for cp in cps:
            cp.start()
        for cp in cps:
            cp.wait()

    return pl.pallas_call(
        body, name="grad_xchg3",
        out_shape=tuple(jax.ShapeDtypeStruct(_SHARD_SHAPES[w], F32) for w in ws),
        in_specs=[_ANY] * n, out_specs=[_ANY] * n,
        input_output_aliases={k: k for k in range(n)},
        scratch_shapes=[pltpu.SemaphoreType.DMA((n,)), pltpu.SemaphoreType.DMA((n,))],
        compiler_params=_params(16),
    )(*halves)


def _adamw_math(w, g, m, v):
    m = ADAM_B1 * m + (1.0 - ADAM_B1) * g
    v = ADAM_B2 * v + (1.0 - ADAM_B2) * (g * g)
    m_hat = m / ADAM_C1
    v_hat = v / ADAM_C2
    delta = -ADAM_LR * (m_hat / (jnp.sqrt(v_hat) + ADAM_EPS) + ADAM_WD * w)
    return delta, m, v


def _adamw_group(ws_, gs, ms, vs, after=()):
    n = len(ws_)

    def body(*refs):
        for k in range(n):
            w, g, m, v = (refs[j * n + k] for j in range(4))
            d, nm, nv = (refs[(4 + j) * n + k] for j in range(3))
            d[...], nm[...], nv[...] = _adamw_math(w[...], g[...], m[...], v[...])

    vm = pl.BlockSpec(memory_space=pltpu.VMEM)
    outs = pl.pallas_call(
        _after(body, 4 * n, after), name="adamw_group",
        out_shape=tuple(jax.ShapeDtypeStruct(a.shape, F32) for _ in range(3) for a in ws_),
        in_specs=[vm] * (4 * n) + [_ANY] * len(after), out_specs=[vm] * (3 * n),
        compiler_params=_params(32),
    )(*ws_, *gs, *ms, *vs, *after)
    return [(outs[k], outs[n + k], outs[2 * n + k]) for k in range(n)]


def _adamw(name, w, g, m, v, tr=256, after=()):
    rows, cols = w.shape

    def body(w_ref, g_ref, m_ref, v_ref, d_ref, nm_ref, nv_ref):
        d_ref[...], nm_ref[...], nv_ref[...] = _adamw_math(w_ref[...], g_ref[...], m_ref[...], v_ref[...])

    spec = pl.BlockSpec((tr, cols), lambda i: (i, 0))
    return pl.pallas_call(
        _after(body, 4, after), name=name, grid=(rows // tr,),
        out_shape=tuple(jax.ShapeDtypeStruct((rows, cols), F32) for _ in range(3)),
        in_specs=[spec] * 4 + [_ANY] * len(after), out_specs=[spec] * 3,
        compiler_params=_params(32, dimension_semantics=("arbitrary",)),
    )(w, g, m, v, *after)


_REL_PAD = 384
_VEC_FIELDS = (("norm_g", 0, D_MODEL), ("b_gate", 1024, 2 * D_MODEL), ("sgu_ln_g", 3072, D_B),
               ("sgu_ln_b", 3584, D_B), ("b_s", 4096, N_GROUPS * 128), ("final_g", 4608, D_MODEL))
_LOSS_OFF = 5632
_REL_OFF = 5760
_NV = _REL_OFF + N_HEADS * _REL_PAD
_N_FIELDS = len(_VEC_FIELDS) + 2


def _small_reduce_adamw(grads, loss_row, params, after=()):
    n_in = _N_FIELDS + 1 + 3 * _N_FIELDS
    b_s_field = [f[0] for f in _VEC_FIELDS].index("b_s")

    def body(*refs):
        g_refs, loss_ref = refs[:_N_FIELDS], refs[_N_FIELDS]
        p_refs = [refs[_N_FIELDS + 1 + k * _N_FIELDS:_N_FIELDS + 1 + (k + 1) * _N_FIELDS] for k in range(3)]
        outs = refs[n_in:n_in + 4 * _N_FIELDS + 1]
        mine_v, gath_v, gath_w, wmv, send_sems, recv_sems = refs[n_in + 4 * _N_FIELDS + 1:]
        x, y, c, chips = _mesh_pos()
        me, sibling = (x, y, c), (x, y, 1 - c)

        def assemble(dst, fields, transposed_b_s):
            for f, (_, off, n) in enumerate(_VEC_FIELDS):
                if transposed_b_s and f == b_s_field:
                    t = fields[f][...].T
                    for g in range(N_GROUPS):
                        dst[:, off + 128 * g:off + 128 * (g + 1)] = t[g:g + 1, :]
                else:
                    dst[:, off:off + n] = fields[f][...]
            for r in range(N_HEADS):
                dst[:, _REL_OFF + _REL_PAD * r:_REL_OFF + _REL_PAD * (r + 1)] = fields[len(_VEC_FIELDS)][r:r + 1, :]

        assemble(mine_v, g_refs, True)
        mine_v[:, _LOSS_OFF:_LOSS_OFF + 128] = loss_ref[...]
        mine_w = g_refs[-1]
        my_k = 4 * x + 2 * y + c
        gath_v[my_k] = mine_v[...]
        gath_w[my_k] = mine_w[...]

        def copy(k, gath, block, to, src=None):
            dst = gath.at[4 * block[0] + 2 * block[1] + block[2]]
            return pltpu.make_async_remote_copy(
                src_ref=dst if src is None else src, dst_ref=dst,
                send_sem=send_sems.at[k], recv_sem=recv_sems.at[k], device_id=to, device_id_type=MESH)

        bufs = ((gath_v, mine_v), (gath_w, mine_w))
        first, passed = [], []
        for b, (gath, mine) in enumerate(bufs):
            first.append(copy(7 * b, gath, me, sibling, src=mine))
            first += [copy(7 * b + 1 + j, gath, me, (*chip, c), src=mine) for j, chip in enumerate(chips)]
        for cp in first:
            cp.start()
        for b, (gath, _) in enumerate(bufs):
            for j, chip in enumerate(chips):
                copy(7 * b + 1 + j, gath, (*chip, c), me).wait_recv()
                cp = copy(7 * b + 4 + j, gath, (*chip, c), sibling)
                cp.start()
                passed.append(cp)
        for b, (gath, _) in enumerate(bufs):
            copy(7 * b, gath, sibling, me).wait_recv()
            for j, chip in enumerate(chips):
                copy(7 * b + 4 + j, gath, (*chip, 1 - c), me).wait_recv()
        for cp in first + passed:
            cp.wait_send()

        tot_v, tot_w = gath_v[0], gath_w[0]
        for k in range(1, 8):
            tot_v = tot_v + gath_v[k]
            tot_w = tot_w + gath_w[k]
        for k in range(3):
            assemble(wmv.at[k], p_refs[k], False)
            wmv[k, :, _LOSS_OFF:_LOSS_OFF + 128] = jnp.zeros((1, 128), F32)
        res_v = (tot_v,) + _adamw_math(wmv[0], tot_v, wmv[1], wmv[2])
        res_w = (tot_w,) + _adamw_math(p_refs[0][-1][...], tot_w, p_refs[1][-1][...], p_refs[2][-1][...])
        for kind in range(4):
            o = outs[kind * _N_FIELDS:(kind + 1) * _N_FIELDS]
            for f, (_, off, n) in enumerate(_VEC_FIELDS):
                o[f][...] = res_v[kind][:, off:off + n]
            for r in range(N_HEADS):
                o[len(_VEC_FIELDS)][r:r + 1, :] = res_v[kind][:, _REL_OFF + _REL_PAD * r:_REL_OFF + _REL_PAD * (r + 1)]
            o[-1][...] = res_w[kind]
        outs[-1][...] = tot_v[:, _LOSS_OFF:_LOSS_OFF + 128]

    field_shapes = [(1, n) for _, _, n in _VEC_FIELDS] + [(N_HEADS, _REL_PAD), (N_GROUPS * 128, 128)]
    vm = pl.BlockSpec(memory_space=pltpu.VMEM)
    operands = list(grads) + [loss_row] + [a for p in params for a in p]
    assert len(operands) == n_in
    outs = pl.pallas_call(
        _after(body, n_in, after), name="small_reduce_adamw",
        out_shape=tuple(jax.ShapeDtypeStruct(s, F32) for _ in range(4) for s in field_shapes)
        + (jax.ShapeDtypeStruct((1, 128), F32),),
        in_specs=[vm] * n_in + [_ANY] * len(after), out_specs=[vm] * (4 * _N_FIELDS + 1),
        scratch_shapes=[pltpu.VMEM((1, _NV), F32), pltpu.VMEM((8, 1, _NV), F32),
                        pltpu.VMEM((8, N_GROUPS * 128, 128), F32), pltpu.VMEM((3, 1, _NV), F32),
                        pltpu.SemaphoreType.DMA((14,)), pltpu.SemaphoreType.DMA((14,))],
        compiler_params=_params(32),
    )(*operands, *after)
    return [outs[k * _N_FIELDS:(k + 1) * _N_FIELDS] for k in range(4)], outs[-1]


def _small_fields(norm_g, b_gate, ln_g, ln_b, b_s, final_g, rel_bias, w_s):
    rel = jnp.pad(rel_bias.reshape(N_HEADS, N_REL), ((0, 0), (0, _REL_PAD - N_REL)))
    return (norm_g, b_gate, ln_g, ln_b, b_s.reshape(1, N_GROUPS * 128), final_g.reshape(1, D_MODEL),
            rel, w_s.reshape(N_GROUPS * 128, 128))


def _small_outputs(fields):
    n_g, b_g, l_g, l_b, b_s, f_g, rel, w_s = fields
    return (n_g, b_g, rel[:, :N_REL].reshape(1, N_HEADS, N_REL), l_g, l_b,
            w_s.reshape(1, N_GROUPS, 128, 128), b_s.reshape(1, N_GROUPS, 128), f_g.reshape(D_MODEL))


def _bias_row(rel_bias):
    hi = rel_bias[:, N_REL - 1:N_REL]
    lo = rel_bias[:, 0:1]
    return jnp.concatenate([jnp.broadcast_to(hi, (N_HEADS, 384)), rel_bias[:, ::-1],
                            jnp.broadcast_to(lo, (N_HEADS, 191)), jnp.broadcast_to(hi, (N_HEADS, 192))], axis=1)


def kernel(x, norm_g, w_in, b_gate, rel_bias, sgu_ln_g, sgu_ln_b, w_s, b_s, w_pa, w_pb, w_out, final_g, loss_target, m_norm_g, m_w_in, m_b_gate, m_rel_bias, m_sgu_ln_g, m_sgu_ln_b, m_w_s, m_b_s, m_w_pa, m_w_pb, m_w_out, m_final_g, v_norm_g, v_w_in, v_b_gate, v_rel_bias, v_sgu_ln_g, v_sgu_ln_b, v_w_s, v_b_s, v_w_pa, v_w_pb, v_w_out, v_final_g):
    S = x.shape[1]
    xs = x.reshape(S, D_MODEL)
    tgt = loss_target.reshape(S, D_MODEL)
    big_w = (w_in[0], w_pa[0], w_pb[0], w_out[0])
    big_m = (m_w_in[0], m_w_pa[0], m_w_pb[0], m_w_out[0])
    big_v = (v_w_in[0], v_w_pa[0], v_w_pb[0], v_w_out[0])
    rel = rel_bias[0]
    ws = w_s[0]
    bst = b_s[0].T
    fg = final_g.reshape(1, D_MODEL)
    pos = jnp.stack([lax.axis_index("c"), 2 * lax.axis_index("x") + lax.axis_index("y")]).astype(jnp.int32)

    w_in_bf, w_pa_bf, w_pb_bf, w_out_bf = _ag_weights(*big_w)

    ht, q3, k3, v3, zrest = _inproj_fwd(xs, norm_g, w_in_bf)
    gp = _bias_row(rel)
    att = _attn_fwd(q3, k3, v3, gp)
    sg = _sgu_fwd(zrest, sgu_ln_g, sgu_ln_b, ws, bst)
    (d_out, d_att, d_sg, dzt, gw_out, gw_pa, gw_pb, g_bgate, g_final, loss_row) = _tail(
        att, sg, zrest, xs, tgt, w_pa_bf, w_pb_bf, w_out_bf, b_gate, fg)
    ws_s, ws_i = (1, 2, 3), (0,)
    names = ("adamw_w_in", "adamw_w_pa", "adamw_w_pb", "adamw_w_out")

    x1s = _split_start("gx1s_start", [gw_pa, gw_pb, gw_out] + _x1_lands(ws_s), 12, _x1_copies(ws_s))
    dq, dk, dv, d_gp = _attn_bwd(q3, k3, v3, d_att, gp, after=(x1s.token,))
    got = _split_wait("gx1s_wait", x1s, _x1_copies(ws_s), dq)
    cs_s, csb_s = _grad_add1_group(ws_s, got[:3], got[3:])

    x2s = _split_start("gx2s_start", csb_s + _x2_lands(ws_s), 9, _x2_copies(3))
    dzs, g_ws, g_bs_t, g_lng, g_lnb = _sgu_bwd(zrest, d_sg, sgu_ln_g, sgu_ln_b, ws, bst, after=(x2s.token,))
    gw_in = _gw_in(ht, dq, dk, dv, dzt, dzs)
    got = _split_wait("gx2s_wait", x2s, _x2_copies(3), gw_in)
    halves_s = _grad_add2_group(ws_s, cs_s, got[3:])

    x3s = _split_start("gx3s_start", halves_s, 3, _x3_copies(ws_s))
    x1i = _split_start("gx1i_start", [gw_in] + _x1_lands(ws_i), 4, _x1_copies(ws_i))
    dh_args = (dq, dk, dv, dzt, dzs, w_in_bf, xs, norm_g, d_out)
    part = _dh_gradx(*dh_args, after=(x3s.token, x1i.token))
    g_shards_s = _split_wait("gx3s_wait", x3s, _x3_copies(ws_s), part[0])
    got = _split_wait("gx1i_wait", x1i, _x1_copies(ws_i), part[0])
    sum_i = _grad_add1(0, got[0], got[1], pos)

    x2i = _split_start("gx2i_start", [sum_i[1]] + _x2_lands(ws_i), 3, _x2_copies(1))
    grad_x, g_norm = _dh_gradx(*dh_args, prev=part, after=(x2i.token,))
    big = [None] * 4
    big[1:] = _adamw_group(big_w[1:], g_shards_s, big_m[1:], big_v[1:], after=(x2i.token,))

    g_rel = jnp.pad(d_gp[:, 384:384 + N_REL][:, ::-1], ((0, 0), (0, _REL_PAD - N_REL)))
    small_grads = (g_norm, g_bgate, g_lng, g_lnb, g_bs_t, g_final, g_rel, g_ws.reshape(N_GROUPS * 128, 128))
    small_params = (_small_fields(norm_g, b_gate, sgu_ln_g, sgu_ln_b, b_s, final_g, rel_bias, w_s),
                    _small_fields(m_norm_g, m_b_gate, m_sgu_ln_g, m_sgu_ln_b, m_b_s, m_final_g, m_rel_bias, m_w_s),
                    _small_fields(v_norm_g, v_b_gate, v_sgu_ln_g, v_sgu_ln_b, v_b_s, v_final_g, v_rel_bias, v_w_s))
    (gsum, sdelta, sm, sv), loss_out = _small_reduce_adamw(small_grads, loss_row, small_params, after=(x2i.token,))

    got = _split_wait("gx2i_wait", x2i, _x2_copies(1), loss_out)
    half_i = _grad_add2(0, sum_i[0], got[1], pos)
    g_shard_i, = _grad_xchg3(ws_i, [half_i])
    big[0] = _adamw(names[0], big_w[0], g_shard_i, big_m[0], big_v[0])
    g_shards = [g_shard_i] + list(g_shards_s)
    sg_out, sd_out, sm_out, sv_out = (_small_outputs(f) for f in (gsum, sdelta, sm, sv))
    loss = loss_out[0, 0]

    def assemble(small, bigs):
        n_g, b_g, r_b, l_g, l_b, w_s_, b_s_, f_g = small
        b_in, b_pa, b_pb, b_out = (b[None] for b in bigs)
        return (n_g, b_in, b_g, r_b, l_g, l_b, w_s_, b_s_, b_pa, b_pb, b_out, f_g)

    grads_out = assemble(sg_out, g_shards)
    delta_out = assemble(sd_out, [b[0] for b in big])
    m_out = assemble(sm_out, [b[1] for b in big])
    v_out = assemble(sv_out, [b[2] for b in big])
    return (loss, grad_x.reshape(1, S, D_MODEL), *grads_out, *delta_out, *m_out, *v_out)
```

```python
import functools
import math

import jax
import jax.numpy as jnp
from jax import lax
from jax.experimental import pallas as pl
from jax.experimental.pallas import tpu as pltpu

F32 = jnp.float32
BF = jnp.bfloat16
MESH = pl.DeviceIdType.MESH

D_MODEL = 1024
D_A = 512
D_B = 512
D_IN = 5632
N_HEADS = 8
HEAD_DIM = 64
CHUNK = 64
N_PREV = 8
SGU_CHUNK = 128
N_GROUPS = 4
N_REL = 257
EPS = 1e-6
NEG_INF = -1e30
SCALE = HEAD_DIM ** -0.5

QB = 2 * CHUNK
KB = (N_PREV + 2) * CHUNK
PADK = N_PREV * CHUNK
ROLL_W = 1024
N_RING = KB // QB
KEEP = N_RING - 1

ADAM_LR = 0.001
ADAM_B1 = 0.9
ADAM_B2 = 0.999
ADAM_EPS = 1e-08
ADAM_WD = 0.01
ADAM_STEP = 10
ADAM_C1 = 1.0 - ADAM_B1 ** ADAM_STEP
ADAM_C2 = 1.0 - ADAM_B2 ** ADAM_STEP

N_SHARD = 4
SHARD_IN = D_IN // N_SHARD
MIB = 1024 * 1024


VMEM_RESERVE_MIB = 60


def _params(vmem_mib, **kw):
    assert vmem_mib <= VMEM_RESERVE_MIB
    return pltpu.CompilerParams(vmem_limit_bytes=VMEM_RESERVE_MIB * MIB, **kw)


def _sigmoid(x):
    return 1.0 / (1.0 + jnp.exp(-x))


def _silu_and_grad(x):
    s = _sigmoid(x)
    return x * s, s * (1.0 + x * (1.0 - s))


_GELU_C = math.sqrt(2.0 / math.pi)
_GELU_A = 0.044715


def _gelu_and_grad(x):
    x2 = x * x
    t = jnp.tanh(_GELU_C * (x + _GELU_A * (x2 * x)))
    cdf = 0.5 * (1.0 + t)
    grad = cdf + 0.5 * x * (1.0 - t * t) * (_GELU_C * (1.0 + 3.0 * _GELU_A * x2))
    return x * cdf, grad


def _dot(a, b):
    return jnp.dot(a, b, preferred_element_type=F32)


def _dot_nt(a, b):
    return lax.dot_general(a, b, (((1,), (1,)), ((), ())), preferred_element_type=F32)


def _dot_tn(a, b):
    return lax.dot_general(a, b, (((0,), (0,)), ((), ())), preferred_element_type=F32)


def _mo(v, m):
    return v if isinstance(v, int) else pl.multiple_of(v, m)


def _unit_in(ref, s, p):
    return ref.at[pl.ds(_mo(p * 512, 512), 512), pl.ds(_mo(s * SHARD_IN, 128), SHARD_IN)]


def _unit_p(ref, s, p):
    return ref.at[pl.ds(_mo(p * 256, 256), 256), pl.ds(_mo(s * 256, 128), 256)]


def _unit_out(ref, s, p):
    return ref.at[pl.ds(_mo(s * 256 + p * 128, 128), 128), :]


_UNITS = (_unit_in, _unit_p, _unit_p, _unit_out)
_HALF_ROWS = (512, 256, 256, 128)
_UNIT_SHAPES = ((512, SHARD_IN), (256, 256), (256, 256), (128, D_MODEL))
_FULL_SHAPES = ((D_MODEL, D_IN), (D_A, D_MODEL), (D_B, D_MODEL), (D_MODEL, D_MODEL))
_SHARD_SHAPES = ((D_MODEL, SHARD_IN), (D_A, 256), (D_B, 256), (256, D_MODEL))


def _mesh_pos():
    x, y, c = lax.axis_index("x"), lax.axis_index("y"), lax.axis_index("c")
    chips = [(1 - x, y), (x, 1 - y), (1 - x, 1 - y)]
    return x, y, c, chips


def _ag_weights(ws, shards):
    n = len(ws)

    def body(*refs):
        ins, outs, stage = refs[:n], refs[n:2 * n], refs[2 * n:3 * n]
        send_sems, recv_sems, local_sems = refs[3 * n:]
        x, y, c, chips = _mesh_pos()
        s_me = 2 * x + y
        sibling = (x, y, 1 - c)
        for k in range(n):
            stage[k][...] = ins[k][...].astype(BF)

        def half(k, p):
            rows = _HALF_ROWS[ws[k]]
            return stage[k].at[pl.ds(_mo(p * rows, rows), rows), :]

        def unit(k, s, p):
            return _UNITS[ws[k]](outs[k], s, p)

        local = []
        for k in range(n):
            for p in range(2):
                cp = pltpu.make_async_copy(half(k, p), unit(k, s_me, p), local_sems.at[k, p])
                cp.start()
                local.append(cp)

        def rcopy(k, i, src, dst, to):
            return pltpu.make_async_remote_copy(src_ref=src, dst_ref=dst, send_sem=send_sems.at[k, i],
                                                recv_sem=recv_sems.at[k, i], device_id=to, device_id_type=MESH)

        sends = []
        for j, (cx, cy) in enumerate(chips):
            for k in range(n):
                cp = rcopy(k, j, half(k, c), unit(k, s_me, c), (cx, cy, c))
                cp.start()
                sends.append(cp)
        for j, (cx, cy) in enumerate(chips):
            for k in range(n):
                landed = unit(k, 2 * cx + cy, c)
                rcopy(k, j, landed, landed, (cx, cy, c)).wait_recv()
                cp = rcopy(k, 3 + j, landed, landed, sibling)
                cp.start()
                sends.append(cp)
        for j, (cx, cy) in enumerate(chips):
            for k in range(n):
                other = unit(k, 2 * cx + cy, 1 - c)
                rcopy(k, 3 + j, other, other, sibling).wait_recv()
        for cp in sends:
            cp.wait_send()
        for cp in local:
            cp.wait()

    vm = pl.BlockSpec(memory_space=pltpu.VMEM)
    return pl.pallas_call(
        body, name="ag_weights",
        out_shape=tuple(jax.ShapeDtypeStruct(_FULL_SHAPES[w], BF) for w in ws),
        in_specs=[vm] * n, out_specs=[_ANY] * n,
        scratch_shapes=[pltpu.VMEM(_SHARD_SHAPES[w], BF) for w in ws]
        + [pltpu.SemaphoreType.DMA((n, 6)), pltpu.SemaphoreType.DMA((n, 6)), pltpu.SemaphoreType.DMA((n, 2))],
        compiler_params=_params(40),
    )(*shards)


def _shard_of(ref, w, s):
    if w == 0:
        return ref.at[:, pl.ds(_mo(s * SHARD_IN, 128), SHARD_IN)]
    if w == 3:
        return ref.at[pl.ds(_mo(s * 256, 256), 256), :]
    return ref.at[:, pl.ds(_mo(s * 256, 128), 256)]


def _stage_weights(ws, shards, pos):
    n = len(ws)

    def body(pos_ref, *refs):
        for k in range(n):
            refs[n + k][...] = refs[k][...].astype(BF)

    def spec(w):
        shape = _SHARD_SHAPES[w]
        if w == 3:
            return pl.BlockSpec(shape, lambda i, pos: (pos[1], 0))
        return pl.BlockSpec(shape, lambda i, pos: (0, pos[1]))

    return list(pl.pallas_call(
        body, name="stage_weights",
        grid_spec=pltpu.PrefetchScalarGridSpec(
            num_scalar_prefetch=1, grid=(1,),
            in_specs=[pl.BlockSpec(_SHARD_SHAPES[w], lambda i, pos: (0, 0)) for w in ws],
            out_specs=[spec(w) for w in ws]),
        out_shape=tuple(jax.ShapeDtypeStruct(_FULL_SHAPES[w], BF) for w in ws),
        compiler_params=_params(16, dimension_semantics=("arbitrary",)),
    )(pos, *shards))


def _gather_copies(ws):
    def copies(refs, send_sems, recv_sems):
        x, y, c, chips = _mesh_pos()
        out = []
        for j, (cx, cy) in enumerate(chips):
            for k, w in enumerate(ws):
                mine = _shard_of(refs[k], w, 2 * x + y)
                out.append(pltpu.make_async_remote_copy(
                    src_ref=mine, dst_ref=mine, send_sem=send_sems.at[3 * k + j], recv_sem=recv_sems.at[3 * k + j],
                    device_id=(cx, cy, c), device_id_type=MESH))
        return out
    return copies


def _inproj_fwd(x, norm_g, w_in_bf, tm=512, after=()):
    S = x.shape[0]

    def body(x_ref, g_ref, w_ref, ht_ref, q_ref, k_ref, v_ref, zr_ref):
        xv = x_ref[...]
        r = lax.rsqrt(jnp.mean(xv * xv, axis=-1, keepdims=True) + EPS)
        hf = (xv * r) * g_ref[...]
        ht_ref[...] = hf.T.astype(BF)
        h = hf.astype(BF)
        heads = (q_ref, k_ref, v_ref)
        for j in range(D_IN // 512):
            z = _dot(h, w_ref[:, j * 512:(j + 1) * 512])
            if j < 3:
                zb = z.astype(BF)
                for hd in range(N_HEADS):
                    heads[j][hd] = zb[:, hd * HEAD_DIM:(hd + 1) * HEAD_DIM]
            else:
                zr_ref[:, (j - 3) * 512:(j - 2) * 512] = z

    head_major = jax.ShapeDtypeStruct((N_HEADS, S, HEAD_DIM), BF)
    head_spec = pl.BlockSpec((N_HEADS, tm, HEAD_DIM), lambda i: (0, i, 0))
    return pl.pallas_call(
        _after(body, 3, after), name="inproj_fwd", grid=(S // tm,),
        out_shape=(jax.ShapeDtypeStruct((D_MODEL, S), BF), head_major, head_major, head_major,
                   jax.ShapeDtypeStruct((S, D_IN - 3 * D_A), F32)),
        in_specs=[pl.BlockSpec((tm, D_MODEL), lambda i: (i, 0)),
                  pl.BlockSpec((1, D_MODEL), lambda i: (0, 0)),
                  pl.BlockSpec((D_MODEL, D_IN), lambda i: (0, 0), pipeline_mode=pl.Buffered(1))]
        + [_ANY] * len(after),
        out_specs=[pl.BlockSpec((D_MODEL, tm), lambda i: (0, i)),
                   head_spec, head_spec, head_spec,
                   pl.BlockSpec((tm, D_IN - 3 * D_A), lambda i: (i, 0))],
        compiler_params=_params(52, dimension_semantics=("arbitrary",)),
    )(x, norm_g, w_in_bf, *after)


def _skew_table(gp_row):
    row = lax.broadcasted_iota(jnp.int32, (QB, ROLL_W), 0)
    t = jnp.broadcast_to(gp_row, (QB, ROLL_W))
    for b in range(7):
        t = jnp.where(((row >> b) & 1) == 1, pltpu.roll(t, 1 << b, axis=1), t)
    return t


def _unskew_sum(d):
    row = lax.broadcasted_iota(jnp.int32, (QB, ROLL_W), 0)
    for b in range(7):
        d = jnp.where(((row >> b) & 1) == 1, pltpu.roll(d, ROLL_W - (1 << b), axis=1), d)
    return jnp.sum(d, axis=0, keepdims=True)


def _struct_mask():
    a = lax.broadcasted_iota(jnp.int32, (QB, KB), 0) // CHUNK
    b = lax.broadcasted_iota(jnp.int32, (QB, KB), 1) // CHUNK
    return (b >= a) & (b <= a + N_PREV)


def _load_kv(k_hbm, v_hbm, gp_ref, k_scr, v_scr, bias_scr, sems, S):
    zeros = jnp.zeros((N_HEADS, PADK, HEAD_DIM), BF)
    k_scr[:, 0:PADK, :] = zeros
    v_scr[:, 0:PADK, :] = zeros
    ck = pltpu.make_async_copy(k_hbm, k_scr.at[:, pl.ds(PADK, S), :], sems.at[0])
    cv = pltpu.make_async_copy(v_hbm, v_scr.at[:, pl.ds(PADK, S), :], sems.at[1])
    ck.start()
    cv.start()
    keep = _struct_mask()
    for h in range(N_HEADS):
        bias_scr[h] = jnp.where(keep, _skew_table(gp_ref[h:h + 1, :])[:, :KB], NEG_INF)
    ck.wait()
    cv.wait()


_BATCH_NT = (((2,), (2,)), ((0,), (0,)))
_BATCH_NN = (((2,), (1,)), ((0,), (0,)))
_BATCH_TN = (((1,), (1,)), ((0,), (0,)))


def _bdot(a, b, dims):
    return lax.dot_general(a, b, dims, preferred_element_type=F32)


def _scaled(q):
    return q * jnp.asarray(SCALE, BF)


def _probs(qs, kb, bias, i, front):
    s = _bdot(qs, kb, _BATCH_NT) + bias
    if front:
        col = lax.broadcasted_iota(jnp.int32, (1, 1, KB), 2)
        s = jnp.where(col >= PADK - i * QB, s, NEG_INF)
    m = jnp.max(s, axis=-1, keepdims=True)
    e = jnp.exp(s - m)
    return e * (1.0 / jnp.sum(e, axis=-1, keepdims=True))


def _attn_fwd(q3, k3, v3, gp):
    S = q3.shape[1]

    def body(q_ref, k_hbm, v_hbm, gp_ref, o_ref, k_scr, v_scr, bias_scr, sems):
        i = pl.program_id(0)

        @pl.when(i == 0)
        def _():
            _load_kv(k_hbm, v_hbm, gp_ref, k_scr, v_scr, bias_scr, sems, S)

        def step(front):
            start = pl.multiple_of(i * QB, QB)
            kb = k_scr[:, pl.ds(start, KB), :]
            vb = v_scr[:, pl.ds(start, KB), :]
            p = _probs(_scaled(q_ref[...]), kb, bias_scr[...], i, front)
            o = _bdot(p.astype(BF), vb, _BATCH_NN)
            for h in range(N_HEADS):
                o_ref[:, h * HEAD_DIM:(h + 1) * HEAD_DIM] = o[h]

        pl.when(i < KEEP)(functools.partial(step, True))
        pl.when(i >= KEEP)(functools.partial(step, False))

    kv_scr = pltpu.VMEM((N_HEADS, S + PADK, HEAD_DIM), BF)
    return pl.pallas_call(
        body, name="attn_fwd", grid=(S // QB,),
        out_shape=jax.ShapeDtypeStruct((S, D_A), F32),
        in_specs=[pl.BlockSpec((N_HEADS, QB, HEAD_DIM), lambda i: (0, i, 0)),
                  pl.BlockSpec(memory_space=pl.ANY), pl.BlockSpec(memory_space=pl.ANY),
                  pl.BlockSpec((N_HEADS, ROLL_W), lambda i: (0, 0))],
        out_specs=pl.BlockSpec((QB, D_A), lambda i: (i, 0)),
        scratch_shapes=[kv_scr, kv_scr, pltpu.VMEM((N_HEADS, QB, KB), F32), pltpu.SemaphoreType.DMA((2,))],
        compiler_params=_params(48, dimension_semantics=("arbitrary",)),
    )(q3, k3, v3, gp)


def _attn_bwd(q3, k3, v3, d_att3, gp, after=()):
    S = q3.shape[1]
    nq = S // QB

    def body(q_ref, do_ref, k_hbm, v_hbm, gp_ref, dq_ref, dk_ref, dv_ref, dgp_ref,
             k_scr, v_scr, bias_scr, dk_acc, dv_acc, dbias_acc, pad_scr, sems):
        i = pl.program_id(0)

        @pl.when(i == 0)
        def _():
            _load_kv(k_hbm, v_hbm, gp_ref, k_scr, v_scr, bias_scr, sems, S)
            dk_acc[...] = jnp.zeros_like(dk_acc)
            dv_acc[...] = jnp.zeros_like(dv_acc)
            dbias_acc[...] = jnp.zeros_like(dbias_acc)

        def step(front):
            start = pl.multiple_of(i * QB, QB)
            kb = k_scr[:, pl.ds(start, KB), :]
            vb = v_scr[:, pl.ds(start, KB), :]
            qs = _scaled(q_ref[...])
            do = do_ref[...]
            p = _probs(qs, kb, bias_scr[...], i, front)
            dp = _bdot(do, vb, _BATCH_NT)
            ds = p * (dp - jnp.sum(dp * p, axis=-1, keepdims=True))
            dbias_acc[...] += ds
            dsb = ds.astype(BF)
            dq = _bdot(dsb, kb, _BATCH_NN) * SCALE
            for h in range(N_HEADS):
                dq_ref[:, h * HEAD_DIM:(h + 1) * HEAD_DIM] = dq[h].astype(BF)
            dk_acc[...] += _bdot(dsb, qs, _BATCH_TN)
            dv_acc[...] += _bdot(p.astype(BF), do, _BATCH_TN)

        pl.when(i < KEEP)(functools.partial(step, True))
        pl.when((i >= KEEP) & (i < nq))(functools.partial(step, False))

        for h in range(N_HEADS):
            hs = slice(h * HEAD_DIM, (h + 1) * HEAD_DIM)
            dk_ref[:, hs] = dk_acc[h, 0:QB, :].astype(BF)
            dv_ref[:, hs] = dv_acc[h, 0:QB, :].astype(BF)
        dk_acc[:, 0:KB - QB, :] = dk_acc[:, QB:KB, :]
        dv_acc[:, 0:KB - QB, :] = dv_acc[:, QB:KB, :]
        dk_acc[:, KB - QB:KB, :] = jnp.zeros((N_HEADS, QB, HEAD_DIM), F32)
        dv_acc[:, KB - QB:KB, :] = jnp.zeros((N_HEADS, QB, HEAD_DIM), F32)

        @pl.when(i == nq + KEEP - 1)
        def _():
            lane = lax.broadcasted_iota(jnp.int32, (1, ROLL_W), 1)
            hi = (lane < 384) | (lane >= 832)
            lo = (lane > 640) & (lane < 832)
            pad_scr[...] = jnp.zeros_like(pad_scr)
            for h in range(N_HEADS):
                pad_scr[:, 0:KB] = dbias_acc[h]
                g = _unskew_sum(pad_scr[...])
                s_hi = jnp.sum(jnp.where(hi, g, 0.0), axis=-1, keepdims=True)
                s_lo = jnp.sum(jnp.where(lo, g, 0.0), axis=-1, keepdims=True)
                g = jnp.where(lane == 384, g + s_hi, g)
                g = jnp.where(lane == 640, g + s_lo, g)
                dgp_ref[h:h + 1, :] = g

    last = nq - 1
    kv_scr = pltpu.VMEM((N_HEADS, S + PADK, HEAD_DIM), BF)
    return pl.pallas_call(
        _after(body, 5, after), name="attn_bwd", grid=(nq + KEEP,),
        out_shape=(jax.ShapeDtypeStruct((S, D_A), BF), jax.ShapeDtypeStruct((S, D_A), BF),
                   jax.ShapeDtypeStruct((S, D_A), BF), jax.ShapeDtypeStruct((N_HEADS, ROLL_W), F32)),
        in_specs=[pl.BlockSpec((N_HEADS, QB, HEAD_DIM), lambda i: (0, jnp.minimum(i, last), 0)),
                  pl.BlockSpec((N_HEADS, QB, HEAD_DIM), lambda i: (0, jnp.minimum(i, last), 0)),
                  pl.BlockSpec(memory_space=pl.ANY), pl.BlockSpec(memory_space=pl.ANY),
                  pl.BlockSpec((N_HEADS, ROLL_W), lambda i: (0, 0))] + [_ANY] * len(after),
        out_specs=[pl.BlockSpec((QB, D_A), lambda i: (jnp.minimum(i, last), 0)),
                   pl.BlockSpec((QB, D_A), lambda i: (jnp.maximum(i - KEEP, 0), 0)),
                   pl.BlockSpec((QB, D_A), lambda i: (jnp.maximum(i - KEEP, 0), 0)),
                   pl.BlockSpec((N_HEADS, ROLL_W), lambda i: (0, 0))],
        scratch_shapes=[kv_scr, kv_scr, pltpu.VMEM((N_HEADS, QB, KB), F32),
                        pltpu.VMEM((N_HEADS, KB, HEAD_DIM), F32), pltpu.VMEM((N_HEADS, KB, HEAD_DIM), F32),
                        pltpu.VMEM((N_HEADS, QB, KB), F32), pltpu.VMEM((QB, ROLL_W), F32),
                        pltpu.SemaphoreType.DMA((2,))],
        compiler_params=_params(56, dimension_semantics=("arbitrary",)),
    )(q3, d_att3, k3, v3, gp, *after)


def _sgu_core(ub, vb, lg, lb):
    u, du = _gelu_and_grad(ub)
    v, dv = _gelu_and_grad(vb)
    mu = jnp.mean(v, axis=-1, keepdims=True)
    vc = v - mu
    rstd = lax.rsqrt(jnp.mean(vc * vc, axis=-1, keepdims=True) + EPS)
    xh = vc * rstd
    vn = xh * lg + lb
    return u, du, dv, rstd, xh, vn


def _tri():
    r = lax.broadcasted_iota(jnp.int32, (SGU_CHUNK, SGU_CHUNK), 0)
    c = lax.broadcasted_iota(jnp.int32, (SGU_CHUNK, SGU_CHUNK), 1)
    return r >= c


def _sgu_fwd(zrest, ln_g, ln_b, w_s, b_s_t, tm=512):
    S = zrest.shape[0]

    def body(ub_ref, vb_ref, lg_ref, lb_ref, ws_ref, bst_ref, sg_ref):
        u, _, _, _, _, vn = _sgu_core(ub_ref[...], vb_ref[...], lg_ref[...], lb_ref[...])
        vnb = vn.astype(BF)
        tri = _tri()
        for g in range(N_GROUPS):
            cs = slice(g * 128, (g + 1) * 128)
            wt = jnp.where(tri, ws_ref[g], 0.0).astype(BF)
            bcol = bst_ref[:, g:g + 1]
            for n in range(tm // SGU_CHUNK):
                rs = slice(n * SGU_CHUNK, (n + 1) * SGU_CHUNK)
                mixed = _dot(wt, vnb[rs, cs]) + bcol
                sg_ref[rs, cs] = u[rs, cs] * mixed

    return pl.pallas_call(
        body, name="sgu_fwd", grid=(S // tm,),
        out_shape=jax.ShapeDtypeStruct((S, D_B), F32),
        in_specs=[pl.BlockSpec((tm, 512), lambda i: (i, 1)),
                  pl.BlockSpec((tm, 512), lambda i: (i, 2)),
                  pl.BlockSpec((1, D_B), lambda i: (0, 0)),
                  pl.BlockSpec((1, D_B), lambda i: (0, 0)),
                  pl.BlockSpec((N_GROUPS, 128, 128), lambda i: (0, 0, 0)),
                  pl.BlockSpec((128, N_GROUPS), lambda i: (0, 0))],
        out_specs=pl.BlockSpec((tm, D_B), lambda i: (i, 0)),
        compiler_params=_params(32, dimension_semantics=("arbitrary",)),
    )(zrest, zrest, ln_g, ln_b, w_s, b_s_t)


def _sgu_bwd(zrest, d_sg, ln_g, ln_b, w_s, b_s_t, tm=256, after=()):
    S = zrest.shape[0]
    nt = S // tm

    def body(ub_ref, vb_ref, dsg_ref, lg_ref, lb_ref, ws_ref, bst_ref,
             dzs_ref, gws_ref, gbs_ref, glg_ref, glb_ref, dvn_scr, bs_acc):
        i = pl.program_id(0)

        @pl.when(i == 0)
        def _():
            gws_ref[...] = jnp.zeros_like(gws_ref)
            glg_ref[...] = jnp.zeros_like(glg_ref)
            glb_ref[...] = jnp.zeros_like(glb_ref)
            bs_acc[...] = jnp.zeros_like(bs_acc)

        ub = ub_ref[...]
        u, du, dv, rstd, xh, vn = _sgu_core(ub, vb_ref[...], lg_ref[...], lb_ref[...])
        vnb = vn.astype(BF)
        dsg = dsg_ref[...]
        tri = _tri()
        for g in range(N_GROUPS):
            cs = slice(g * 128, (g + 1) * 128)
            wtf = jnp.where(tri, ws_ref[g], 0.0)
            wt = wtf.astype(BF)
            wtt = wtf.T.astype(BF)
            bcol = bst_ref[:, g:g + 1]
            for n in range(tm // SGU_CHUNK):
                rs = slice(n * SGU_CHUNK, (n + 1) * SGU_CHUNK)
                mixed = _dot(wt, vnb[rs, cs]) + bcol
                dzs_ref[rs, cs] = (dsg[rs, cs] * mixed * du[rs, cs]).astype(BF)
                dmix = dsg[rs, cs] * u[rs, cs]
                bs_acc[:, cs] += dmix
                dmb = dmix.astype(BF)
                gws_ref[g] += _dot_nt(dmb, vnb[rs, cs])
                dvn_scr[rs, cs] = _dot(wtt, dmb)
        dvn = dvn_scr[...]
        glg_ref[...] += jnp.sum(dvn * xh, axis=0, keepdims=True)
        glb_ref[...] += jnp.sum(dvn, axis=0, keepdims=True)
        dxh = dvn * lg_ref[...]
        dvv = rstd * (dxh - jnp.mean(dxh, axis=-1, keepdims=True)
                      - xh * jnp.mean(dxh * xh, axis=-1, keepdims=True))
        dzs_ref[:, D_B:2 * D_B] = (dvv * dv).astype(BF)

        @pl.when(i == nt - 1)
        def _():
            lane = lax.broadcasted_iota(jnp.int32, (SGU_CHUNK, 128), 1)
            out = jnp.zeros((SGU_CHUNK, 128), F32)
            for g in range(N_GROUPS):
                gws_ref[g] = jnp.where(tri, gws_ref[g], 0.0)
                col = jnp.sum(bs_acc[:, g * 128:(g + 1) * 128], axis=-1, keepdims=True)
                out = jnp.where(lane == g, col, out)
            gbs_ref[...] = out

    const2 = lambda i: (0, 0)
    return pl.pallas_call(
        _after(body, 7, after), name="sgu_bwd", grid=(nt,),
        out_shape=(jax.ShapeDtypeStruct((S, 2 * D_B), BF),
                   jax.ShapeDtypeStruct((N_GROUPS, 128, 128), F32),
                   jax.ShapeDtypeStruct((SGU_CHUNK, 128), F32),
                   jax.ShapeDtypeStruct((1, D_B), F32), jax.ShapeDtypeStruct((1, D_B), F32)),
        in_specs=[pl.BlockSpec((tm, 512), lambda i: (i, 1)),
                  pl.BlockSpec((tm, 512), lambda i: (i, 2)),
                  pl.BlockSpec((tm, D_B), lambda i: (i, 0)),
                  pl.BlockSpec((1, D_B), const2), pl.BlockSpec((1, D_B), const2),
                  pl.BlockSpec((N_GROUPS, 128, 128), lambda i: (0, 0, 0)),
                  pl.BlockSpec((128, N_GROUPS), const2)] + [_ANY] * len(after),
        out_specs=[pl.BlockSpec((tm, 2 * D_B), lambda i: (i, 0)),
                   pl.BlockSpec((N_GROUPS, 128, 128), lambda i: (0, 0, 0)),
                   pl.BlockSpec((SGU_CHUNK, 128), const2),
                   pl.BlockSpec((1, D_B), const2), pl.BlockSpec((1, D_B), const2)],
        scratch_shapes=[pltpu.VMEM((tm, D_B), F32), pltpu.VMEM((SGU_CHUNK, D_B), F32)],
        compiler_params=_params(32, dimension_semantics=("arbitrary",)),
    )(zrest, zrest, d_sg, ln_g, ln_b, w_s, b_s_t, *after)


def _tail(att, sg, zrest, x, target, w_pa, w_pb, w_out, b_gate, final_g, tm=256):
    S = x.shape[0]
    nt = S // tm

    def body(att_ref, sg_ref, ga_ref, gb_ref, gta_ref, gtb_ref, x_ref, t_ref,
             wpa_ref, wpb_ref, wout_ref, bg_ref, fg_ref,
             dout_ref, datt_ref, dsg_ref, dzt_ref, gwout_hbm, gwpa_hbm, gwpb_hbm,
             gbg_ref, gfg_ref, loss_ref, acc_out, acc_pa, acc_pb, sems):
        i = pl.program_id(0)

        @pl.when(i == 0)
        def _():
            acc_out[...] = jnp.zeros_like(acc_out)
            acc_pa[...] = jnp.zeros_like(acc_pa)
            acc_pb[...] = jnp.zeros_like(acc_pb)
            gbg_ref[...] = jnp.zeros_like(gbg_ref)
            gfg_ref[...] = jnp.zeros_like(gfg_ref)
            loss_ref[...] = jnp.zeros_like(loss_ref)

        att = att_ref[...]
        sg = sg_ref[...]
        sa, dsa = _silu_and_grad(ga_ref[...])
        sb, dsb = _silu_and_grad(gb_ref[...])
        ya = (att * sa).astype(BF)
        yb = (sg * sb).astype(BF)
        pa = _dot(ya, wpa_ref[...])
        pb = _dot(yb, wpb_ref[...])
        ga = _sigmoid(gta_ref[...] + bg_ref[:, 0:D_MODEL])
        gb = _sigmoid(gtb_ref[...] + bg_ref[:, D_MODEL:2 * D_MODEL])
        merged = (ga * pa + gb * pb).astype(BF)
        out = x_ref[...] + _dot(merged, wout_ref[...])
        r2 = lax.rsqrt(jnp.mean(out * out, axis=-1, keepdims=True) + EPS)
        nrm = out * r2
        fg = fg_ref[...]
        err = nrm * fg - t_ref[...]
        loss_ref[...] += 0.5 * jnp.sum(jnp.mean(err * err, axis=-1, keepdims=True))
        dy = err * (1.0 / D_MODEL)
        gfg_ref[...] += jnp.sum(dy * nrm, axis=0, keepdims=True)
        dn = dy * fg
        d_out = r2 * (dn - nrm * jnp.mean(dn * nrm, axis=-1, keepdims=True))
        dout_ref[...] = d_out
        d_outb = d_out.astype(BF)
        acc_out[...] += _dot_tn(merged, d_outb)
        dm = _dot_nt(d_outb, wout_ref[...])
        d_pa = (dm * ga).astype(BF)
        d_pb = (dm * gb).astype(BF)
        d_gta = dm * pa * (ga * (1.0 - ga))
        d_gtb = dm * pb * (gb * (1.0 - gb))
        gbg_ref[:, 0:D_MODEL] += jnp.sum(d_gta, axis=0, keepdims=True)
        gbg_ref[:, D_MODEL:2 * D_MODEL] += jnp.sum(d_gtb, axis=0, keepdims=True)
        dzt_ref[:, 2 * D_A:2 * D_A + D_MODEL] = d_gta.astype(BF)
        dzt_ref[:, 2 * D_A + D_MODEL:] = d_gtb.astype(BF)
        acc_pa[...] += _dot_tn(ya, d_pa)
        acc_pb[...] += _dot_tn(yb, d_pb)
        d_ya = _dot_nt(d_pa, wpa_ref[...])
        d_yb = _dot_nt(d_pb, wpb_ref[...])
        d_att = (d_ya * sa).astype(BF)
        for hd in range(N_HEADS):
            datt_ref[hd] = d_att[:, hd * HEAD_DIM:(hd + 1) * HEAD_DIM]
        dzt_ref[:, 0:D_A] = (d_ya * att * dsa).astype(BF)
        dsg_ref[...] = d_yb * sb
        dzt_ref[:, D_A:2 * D_A] = (d_yb * sg * dsb).astype(BF)

        @pl.when(i == nt - 1)
        def _():
            cps = [pltpu.make_async_copy(acc_out, gwout_hbm, sems.at[0]),
                   pltpu.make_async_copy(acc_pa, gwpa_hbm, sems.at[1]),
                   pltpu.make_async_copy(acc_pb, gwpb_hbm, sems.at[2])]
            for cp in cps:
                cp.start()
            for cp in cps:
                cp.wait()

    c2 = lambda i: (0, 0)
    hbm = pl.BlockSpec(memory_space=pl.ANY)
    return pl.pallas_call(
        body, name="tail", grid=(nt,),
        out_shape=(jax.ShapeDtypeStruct((S, D_MODEL), F32), jax.ShapeDtypeStruct((N_HEADS, S, HEAD_DIM), BF),
                   jax.ShapeDtypeStruct((S, D_B), F32), jax.ShapeDtypeStruct((S, 3072), BF),
                   jax.ShapeDtypeStruct((D_MODEL, D_MODEL), F32), jax.ShapeDtypeStruct((D_A, D_MODEL), F32),
                   jax.ShapeDtypeStruct((D_B, D_MODEL), F32),
                   jax.ShapeDtypeStruct((1, 2 * D_MODEL), F32), jax.ShapeDtypeStruct((1, D_MODEL), F32),
                   jax.ShapeDtypeStruct((1, 128), F32)),
        in_specs=[pl.BlockSpec((tm, D_A), lambda i: (i, 0)),
                  pl.BlockSpec((tm, D_B), lambda i: (i, 0)),
                  pl.BlockSpec((tm, 512), lambda i: (i, 0)),
                  pl.BlockSpec((tm, 512), lambda i: (i, 3)),
                  pl.BlockSpec((tm, D_MODEL), lambda i: (i, 2)),
                  pl.BlockSpec((tm, D_MODEL), lambda i: (i, 3)),
                  pl.BlockSpec((tm, D_MODEL), lambda i: (i, 0)),
                  pl.BlockSpec((tm, D_MODEL), lambda i: (i, 0)),
                  pl.BlockSpec((D_A, D_MODEL), c2), pl.BlockSpec((D_B, D_MODEL), c2),
                  pl.BlockSpec((D_MODEL, D_MODEL), c2),
                  pl.BlockSpec((1, 2 * D_MODEL), c2), pl.BlockSpec((1, D_MODEL), c2)],
        out_specs=[pl.BlockSpec((tm, D_MODEL), lambda i: (i, 0)),
                   pl.BlockSpec((N_HEADS, tm, HEAD_DIM), lambda i: (0, i, 0)),
                   pl.BlockSpec((tm, D_B), lambda i: (i, 0)),
                   pl.BlockSpec((tm, 3072), lambda i: (i, 0)),
                   hbm, hbm, hbm,
                   pl.BlockSpec((1, 2 * D_MODEL), c2), pl.BlockSpec((1, D_MODEL), c2),
                   pl.BlockSpec((1, 128), c2)],
        scratch_shapes=[pltpu.VMEM((D_MODEL, D_MODEL), F32), pltpu.VMEM((D_A, D_MODEL), F32),
                        pltpu.VMEM((D_B, D_MODEL), F32), pltpu.SemaphoreType.DMA((3,))],
        compiler_params=_params(56, dimension_semantics=("arbitrary",)),
    )(att, sg, zrest, zrest, zrest, zrest, x, target, w_pa, w_pb, w_out, b_gate, final_g)


_DZ_MAP = ((0, 0), (1, 0), (2, 0), (3, 0), (4, 0), (4, 1), (3, 1), (3, 2), (3, 3), (3, 4), (3, 5))


def _dh_gradx(dq, dk, dv, dzt, dzs, w_in_bf, x, norm_g, d_out, prev=None, tm=512, after=()):
    S = x.shape[0]
    n_first = S // tm // 4
    nt = n_first if prev is None else S // tm - n_first
    first = 0 if prev is None else n_first
    n_in = 9 if prev is None else 11

    def body(dq_ref, dk_ref, dv_ref, dzt_ref, dzs_ref, w_ref, x_ref, g_ref, dout_ref, *rest):
        gx_ref, gn_ref = rest[-2:]
        i = pl.program_id(0)

        @pl.when(i == 0)
        def _():
            gn_ref[...] = jnp.zeros_like(gn_ref) if prev is None else rest[1][...]

        pieces = (dq_ref, dk_ref, dv_ref, dzt_ref, dzs_ref)
        dh = jnp.zeros((tm, D_MODEL), F32)
        for j, (pc, blk) in enumerate(_DZ_MAP):
            dh += _dot_nt(pieces[pc][:, blk * 512:(blk + 1) * 512], w_ref[:, j * 512:(j + 1) * 512])
        xv = x_ref[...]
        r = lax.rsqrt(jnp.mean(xv * xv, axis=-1, keepdims=True) + EPS)
        nrm = xv * r
        gn_ref[...] += jnp.sum(dh * nrm, axis=0, keepdims=True)
        dn = dh * g_ref[...]
        gx_ref[...] = r * (dn - nrm * jnp.mean(dn * nrm, axis=-1, keepdims=True)) + dout_ref[...]

    row = lambda w: pl.BlockSpec((tm, w), lambda i: (i + first, 0))
    c2 = lambda i: (0, 0)
    more = [] if prev is None else [_ANY, pl.BlockSpec((1, D_MODEL), c2)]
    return pl.pallas_call(
        _after(body, n_in, after), name="dh_gradx_a" if prev is None else "dh_gradx_b", grid=(nt,),
        out_shape=(jax.ShapeDtypeStruct((S, D_MODEL), F32), jax.ShapeDtypeStruct((1, D_MODEL), F32)),
        in_specs=[row(512), row(512), row(512), row(3072), row(1024),
                  pl.BlockSpec((D_MODEL, D_IN), c2, pipeline_mode=pl.Buffered(1)), row(D_MODEL),
                  pl.BlockSpec((1, D_MODEL), c2), row(D_MODEL)]
        + more + [_ANY] * len(after),
        out_specs=[row(D_MODEL), pl.BlockSpec((1, D_MODEL), c2)],
        input_output_aliases={} if prev is None else {9: 0},
        compiler_params=_params(48, dimension_semantics=("arbitrary",)),
    )(dq, dk, dv, dzt, dzs, w_in_bf, x, norm_g, d_out, *(prev or ()), *after)


def _gw_in(ht, dq, dk, dv, dzt, dzs, tn=256, after=()):
    S = ht.shape[1]
    per = 512 // tn
    cols = tuple((pc, per * blk + h) for pc, blk in _DZ_MAP for h in range(per))

    def body(ht_ref, dq_ref, dk_ref, dv_ref, dzt_ref, dzs_ref, o_ref):
        j = pl.program_id(0)
        pieces = (dq_ref, dk_ref, dv_ref, dzt_ref, dzs_ref)
        for pc in range(5):
            hit = functools.reduce(jnp.logical_or, [j == jj for jj, (p, _) in enumerate(cols) if p == pc])

            @pl.when(hit)
            def _(pc=pc):
                o_ref[...] = _dot(ht_ref[...], pieces[pc][...])

    def piece_spec(pc):
        cur = next(blk for p, blk in cols if p == pc)
        held = []
        for p, blk in cols:
            cur = blk if p == pc else cur
            held.append(cur)

        def index_map(j):
            blk = jnp.int32(held[0])
            for jj in range(1, len(held)):
                if held[jj] != held[jj - 1]:
                    blk = jnp.where(j >= jj, jnp.int32(held[jj]), blk)
            return (0, blk)

        return pl.BlockSpec((S, tn), index_map)

    return pl.pallas_call(
        _after(body, 6, after), name="gw_in", grid=(len(cols),),
        out_shape=jax.ShapeDtypeStruct((D_MODEL, D_IN), F32),
        in_specs=[pl.BlockSpec((D_MODEL, S), lambda j: (0, 0))] + [piece_spec(pc) for pc in range(5)]
        + [_ANY] * len(after),
        out_specs=pl.BlockSpec((D_MODEL, tn), lambda j: (0, j)),
        compiler_params=_params(48, dimension_semantics=("arbitrary",)),
    )(ht, dq, dk, dv, dzt, dzs, *after)


_HBM = pl.BlockSpec(memory_space=pltpu.HBM)
_SEM = pl.BlockSpec(memory_space=pltpu.SEMAPHORE)
_ANY = pl.BlockSpec(memory_space=pl.ANY)
_EFFECT = pltpu.SideEffectType.DATAFLOW_SIDE_EFFECTING


def _in_hbm(a):
    return pltpu.with_memory_space_constraint(a, pltpu.HBM)


def _after(body, n_in, after):
    if not after:
        return body
    return lambda *refs: body(*refs[:n_in], *refs[n_in + len(after):])


class _Started:
    def __init__(self, send, recv, bufs, token):
        self.send, self.recv, self.bufs, self.token = send, recv, bufs, token


def _split_start(name, bufs, n_copies, copies):
    nb = len(bufs)

    def body(*refs):
        for cp in copies(refs[:nb], refs[nb], refs[nb + 1]):
            cp.start()
        refs[-1][...] = jnp.zeros_like(refs[-1])

    outs = pl.pallas_call(
        body, name=name,
        out_shape=(pltpu.SemaphoreType.DMA((n_copies,)), pltpu.SemaphoreType.DMA((n_copies,)),
                   *[pltpu.HBM(b.shape, b.dtype) for b in bufs], jax.ShapeDtypeStruct((8, 128), F32)),
        in_specs=[_HBM] * nb,
        out_specs=(_SEM, _SEM, *[_HBM] * nb, pl.BlockSpec(memory_space=pltpu.VMEM)),
        input_output_aliases={k: 2 + k for k in range(nb)},
        compiler_params=_params(1, has_side_effects=_EFFECT),
    )(*[_in_hbm(b) for b in bufs])
    return _Started(outs[0], outs[1], list(outs[2:2 + nb]), outs[-1])


def _split_wait(name, started, copies, after):
    nb = len(started.bufs)

    def body(*refs):
        for cp in copies(refs[:nb], refs[nb], refs[nb + 1]):
            cp.wait_send()
            cp.wait_recv()

    return list(pl.pallas_call(
        body, name=name,
        out_shape=tuple(pltpu.HBM(b.shape, b.dtype) for b in started.bufs),
        in_specs=[_HBM] * nb + [_SEM, _SEM, _ANY],
        out_specs=tuple([_HBM] * nb),
        input_output_aliases={k: k for k in range(nb)},
        compiler_params=_params(1, has_side_effects=_EFFECT),
    )(*started.bufs, started.send, started.recv, after))


def _x1_copies(ws):
    def copies(refs, send_sems, recv_sems):
        x, y, c, _ = _mesh_pos()
        out = []
        for k, w in enumerate(ws):
            for s in range(N_SHARD):
                out.append(pltpu.make_async_remote_copy(
                    src_ref=_UNITS[w](refs[k], s, 1 - c), dst_ref=refs[len(ws) + k].at[s],
                    send_sem=send_sems.at[N_SHARD * k + s], recv_sem=recv_sems.at[N_SHARD * k + s],
                    device_id=(x, y, 1 - c), device_id_type=MESH))
        return out
    return copies


def _x2_copies(n):
    def copies(refs, send_sems, recv_sems):
        x, y, c, chips = _mesh_pos()
        out = []
        for j, (cx, cy) in enumerate(chips):
            for k in range(n):
                out.append(pltpu.make_async_remote_copy(
                    src_ref=refs[k].at[2 * cx + cy], dst_ref=refs[n + k].at[j],
                    send_sem=send_sems.at[3 * k + j], recv_sem=recv_sems.at[3 * k + j],
                    device_id=(cx, cy, c), device_id_type=MESH))
        return out
    return copies


def _x3_copies(ws):
    def copies(refs, send_sems, recv_sems):
        x, y, c, _ = _mesh_pos()
        out = []
        for k, w in enumerate(ws):
            rows = _HALF_ROWS[w]
            mine = refs[k].at[pl.ds(_mo(c * rows, rows), rows), :]
            out.append(pltpu.make_async_remote_copy(
                src_ref=mine, dst_ref=mine, send_sem=send_sems.at[k], recv_sem=recv_sems.at[k],
                device_id=(x, y, 1 - c), device_id_type=MESH))
        return out
    return copies


def _x1_lands(ws):
    return [lax.empty((N_SHARD,) + _UNIT_SHAPES[w], F32) for w in ws]


def _x2_lands(ws):
    return [lax.empty((3,) + _UNIT_SHAPES[w], BF) for w in ws]


def _grad_add1(w, g, recv, pos):
    ur, uc = _UNIT_SHAPES[w]
    if w == 3:
        g_map = lambda s, pos: (2 * s + pos[0], 0)
    else:
        g_map = lambda s, pos: (pos[0], s)

    def body(pos_ref, g_ref, r_ref, cs_ref, csb_ref):
        v = g_ref[...] + r_ref[0]
        cs_ref[0] = v
        csb_ref[0] = v.astype(BF)

    u3 = lambda s, pos: (s, 0, 0)
    return pl.pallas_call(
        body, name=f"grad_add1_{w}",
        grid_spec=pltpu.PrefetchScalarGridSpec(
            num_scalar_prefetch=1, grid=(N_SHARD,),
            in_specs=[pl.BlockSpec((ur, uc), g_map), pl.BlockSpec((1, ur, uc), u3)],
            out_specs=[pl.BlockSpec((1, ur, uc), u3), pl.BlockSpec((1, ur, uc), u3)]),
        out_shape=(jax.ShapeDtypeStruct((N_SHARD, ur, uc), F32), jax.ShapeDtypeStruct((N_SHARD, ur, uc), BF)),
        compiler_params=_params(40, dimension_semantics=("arbitrary",)),
    )(pos, g, recv)


def _grad_add1_group(ws, gs, recvs):
    n = len(ws)

    def body(*refs):
        c = lax.axis_index("c")
        for k, w in enumerate(ws):
            g, r, cs, csb = refs[k], refs[n + k], refs[2 * n + k], refs[3 * n + k]
            for s in range(N_SHARD):
                v = _UNITS[w](g, s, c)[...] + r[s]
                cs[s] = v
                csb[s] = v.astype(BF)

    vm = pl.BlockSpec(memory_space=pltpu.VMEM)
    outs = pl.pallas_call(
        body, name="grad_add1_group",
        out_shape=tuple(jax.ShapeDtypeStruct((N_SHARD,) + _UNIT_SHAPES[w], dt) for dt in (F32, BF) for w in ws),
        in_specs=[vm] * (2 * n), out_specs=[vm] * (2 * n),
        compiler_params=_params(32),
    )(*gs, *recvs)
    return list(outs[:n]), list(outs[n:])


def _grad_add2_group(ws, css, recvs):
    n = len(ws)

    def body(*refs):
        x, y, c, _ = _mesh_pos()
        for k, w in enumerate(ws):
            cs, r, o = refs[k], refs[n + k], refs[2 * n + k]
            rows = _HALF_ROWS[w]
            total = ((cs[2 * x + y] + r[0].astype(F32)) + r[1].astype(F32)) + r[2].astype(F32)
            o[pl.ds(_mo(c * rows, rows), rows), :] = total

    vm = pl.BlockSpec(memory_space=pltpu.VMEM)
    return list(pl.pallas_call(
        body, name="grad_add2_group",
        out_shape=tuple(jax.ShapeDtypeStruct(_SHARD_SHAPES[w], F32) for w in ws),
        in_specs=[vm] * (2 * n), out_specs=[vm] * n,
        compiler_params=_params(32),
    )(*css, *recvs))


def _grad_add2(w, cs, recv, pos):
    ur, uc = _UNIT_SHAPES[w]
    tr = ur // 4 if w == 0 else ur
    nt = ur // tr

    def body(pos_ref, cs_ref, r_ref, o_ref):
        o_ref[...] = ((cs_ref[0] + r_ref[0].astype(F32)) + r_ref[1].astype(F32)) + r_ref[2].astype(F32)

    return pl.pallas_call(
        body, name=f"grad_add2_{w}",
        grid_spec=pltpu.PrefetchScalarGridSpec(
            num_scalar_prefetch=1, grid=(nt,),
            in_specs=[pl.BlockSpec((1, tr, uc), lambda t, pos: (pos[1], t, 0)),
                      pl.BlockSpec((3, tr, uc), lambda t, pos: (0, t, 0))],
            out_specs=pl.BlockSpec((tr, uc), lambda t, pos: (pos[0] * nt + t, 0))),
        out_shape=jax.ShapeDtypeStruct(_SHARD_SHAPES[w], F32),
        compiler_params=_params(32, dimension_semantics=("arbitrary",)),
    )(pos, cs, recv)


def _grad_xchg3(ws, halves):
    n = len(ws)

    def body(*refs):
        cps = _x3_copies(ws)(refs[:n], refs[2 * n], refs[2 * n + 1])
        for cp in cps:
            cp.start()
        for cp in cps:
            cp.wait()

    return pl.pallas_call(
        body, name="grad_xchg3",
        out_shape=tuple(jax.ShapeDtypeStruct(_SHARD_SHAPES[w], F32) for w in ws),
        in_specs=[_ANY] * n, out_specs=[_ANY] * n,
        input_output_aliases={k: k for k in range(n)},
        scratch_shapes=[pltpu.SemaphoreType.DMA((n,)), pltpu.SemaphoreType.DMA((n,))],
        compiler_params=_params(16),
    )(*halves)


def _adamw_math(w, g, m, v):
    m = ADAM_B1 * m + (1.0 - ADAM_B1) * g
    v = ADAM_B2 * v + (1.0 - ADAM_B2) * (g * g)
    m_hat = m / ADAM_C1
    v_hat = v / ADAM_C2
    delta = -ADAM_LR * (m_hat / (jnp.sqrt(v_hat) + ADAM_EPS) + ADAM_WD * w)
    return delta, m, v


def _adamw_group(ws_, gs, ms, vs, after=()):
    n = len(ws_)

    def body(*refs):
        for k in range(n):
            w, g, m, v = (refs[j * n + k] for j in range(4))
            d, nm, nv = (refs[(4 + j) * n + k] for j in range(3))
            d[...], nm[...], nv[...] = _adamw_math(w[...], g[...], m[...], v[...])

    vm = pl.BlockSpec(memory_space=pltpu.VMEM)
    outs = pl.pallas_call(
        _after(body, 4 * n, after), name="adamw_group",
        out_shape=tuple(jax.ShapeDtypeStruct(a.shape, F32) for _ in range(3) for a in ws_),
        in_specs=[vm] * (4 * n) + [_ANY] * len(after), out_specs=[vm] * (3 * n),
        compiler_params=_params(32),
    )(*ws_, *gs, *ms, *vs, *after)
    return [(outs[k], outs[n + k], outs[2 * n + k]) for k in range(n)]


def _adamw(name, w, g, m, v, tr=256, after=()):
    rows, cols = w.shape

    def body(w_ref, g_ref, m_ref, v_ref, d_ref, nm_ref, nv_ref):
        d_ref[...], nm_ref[...], nv_ref[...] = _adamw_math(w_ref[...], g_ref[...], m_ref[...], v_ref[...])

    spec = pl.BlockSpec((tr, cols), lambda i: (i, 0))
    return pl.pallas_call(
        _after(body, 4, after), name=name, grid=(rows // tr,),
        out_shape=tuple(jax.ShapeDtypeStruct((rows, cols), F32) for _ in range(3)),
        in_specs=[spec] * 4 + [_ANY] * len(after), out_specs=[spec] * 3,
        compiler_params=_params(32, dimension_semantics=("arbitrary",)),
    )(w, g, m, v, *after)


_REL_PAD = 384
_VEC_FIELDS = (("norm_g", 0, D_MODEL), ("b_gate", 1024, 2 * D_MODEL), ("sgu_ln_g", 3072, D_B),
               ("sgu_ln_b", 3584, D_B), ("b_s", 4096, N_GROUPS * 128), ("final_g", 4608, D_MODEL))
_LOSS_OFF = 5632
_REL_OFF = 5760
_NV = _REL_OFF + N_HEADS * _REL_PAD
_N_FIELDS = len(_VEC_FIELDS) + 2


def _small_reduce_adamw(grads, loss_row, params, after=()):
    n_in = _N_FIELDS + 1 + 3 * _N_FIELDS
    b_s_field = [f[0] for f in _VEC_FIELDS].index("b_s")

    def body(*refs):
        g_refs, loss_ref = refs[:_N_FIELDS], refs[_N_FIELDS]
        p_refs = [refs[_N_FIELDS + 1 + k * _N_FIELDS:_N_FIELDS + 1 + (k + 1) * _N_FIELDS] for k in range(3)]
        outs = refs[n_in:n_in + 4 * _N_FIELDS + 1]
        mine_v, gath_v, gath_w, wmv, send_sems, recv_sems = refs[n_in + 4 * _N_FIELDS + 1:]
        x, y, c, chips = _mesh_pos()
        me, sibling = (x, y, c), (x, y, 1 - c)

        def assemble(dst, fields, transposed_b_s):
            for f, (_, off, n) in enumerate(_VEC_FIELDS):
                if transposed_b_s and f == b_s_field:
                    t = fields[f][...].T
                    for g in range(N_GROUPS):
                        dst[:, off + 128 * g:off + 128 * (g + 1)] = t[g:g + 1, :]
                else:
                    dst[:, off:off + n] = fields[f][...]
            for r in range(N_HEADS):
                dst[:, _REL_OFF + _REL_PAD * r:_REL_OFF + _REL_PAD * (r + 1)] = fields[len(_VEC_FIELDS)][r:r + 1, :]

        assemble(mine_v, g_refs, True)
        mine_v[:, _LOSS_OFF:_LOSS_OFF + 128] = loss_ref[...]
        mine_w = g_refs[-1]
        my_k = 4 * x + 2 * y + c
        gath_v[my_k] = mine_v[...]
        gath_w[my_k] = mine_w[...]

        def copy(k, gath, block, to, src=None):
            dst = gath.at[4 * block[0] + 2 * block[1] + block[2]]
            return pltpu.make_async_remote_copy(
                src_ref=dst if src is None else src, dst_ref=dst,
                send_sem=send_sems.at[k], recv_sem=recv_sems.at[k], device_id=to, device_id_type=MESH)

        bufs = ((gath_v, mine_v), (gath_w, mine_w))
        first, passed = [], []
        for b, (gath, mine) in enumerate(bufs):
            first.append(copy(7 * b, gath, me, sibling, src=mine))
            first += [copy(7 * b + 1 + j, gath, me, (*chip, c), src=mine) for j, chip in enumerate(chips)]
        for cp in first:
            cp.start()
        for b, (gath, _) in enumerate(bufs):
            for j, chip in enumerate(chips):
                copy(7 * b + 1 + j, gath, (*chip, c), me).wait_recv()
                cp = copy(7 * b + 4 + j, gath, (*chip, c), sibling)
                cp.start()
                passed.append(cp)
        for b, (gath, _) in enumerate(bufs):
            copy(7 * b, gath, sibling, me).wait_recv()
            for j, chip in enumerate(chips):
                copy(7 * b + 4 + j, gath, (*chip, 1 - c), me).wait_recv()
        for cp in first + passed:
            cp.wait_send()

        tot_v, tot_w = gath_v[0], gath_w[0]
        for k in range(1, 8):
            tot_v = tot_v + gath_v[k]
            tot_w = tot_w + gath_w[k]
        for k in range(3):
            assemble(wmv.at[k], p_refs[k], False)
            wmv[k, :, _LOSS_OFF:_LOSS_OFF + 128] = jnp.zeros((1, 128), F32)
        res_v = (tot_v,) + _adamw_math(wmv[0], tot_v, wmv[1], wmv[2])
        res_w = (tot_w,) + _adamw_math(p_refs[0][-1][...], tot_w, p_refs[1][-1][...], p_refs[2][-1][...])
        for kind in range(4):
            o = outs[kind * _N_FIELDS:(kind + 1) * _N_FIELDS]
            for f, (_, off, n) in enumerate(_VEC_FIELDS):
                o[f][...] = res_v[kind][:, off:off + n]
            for r in range(N_HEADS):
                o[len(_VEC_FIELDS)][r:r + 1, :] = res_v[kind][:, _REL_OFF + _REL_PAD * r:_REL_OFF + _REL_PAD * (r + 1)]
            o[-1][...] = res_w[kind]
        outs[-1][...] = tot_v[:, _LOSS_OFF:_LOSS_OFF + 128]

    field_shapes = [(1, n) for _, _, n in _VEC_FIELDS] + [(N_HEADS, _REL_PAD), (N_GROUPS * 128, 128)]
    vm = pl.BlockSpec(memory_space=pltpu.VMEM)
    operands = list(grads) + [loss_row] + [a for p in params for a in p]
    assert len(operands) == n_in
    outs = pl.pallas_call(
        _after(body, n_in, after), name="small_reduce_adamw",
        out_shape=tuple(jax.ShapeDtypeStruct(s, F32) for _ in range(4) for s in field_shapes)
        + (jax.ShapeDtypeStruct((1, 128), F32),),
        in_specs=[vm] * n_in + [_ANY] * len(after), out_specs=[vm] * (4 * _N_FIELDS + 1),
        scratch_shapes=[pltpu.VMEM((1, _NV), F32), pltpu.VMEM((8, 1, _NV), F32),
                        pltpu.VMEM((8, N_GROUPS * 128, 128), F32), pltpu.VMEM((3, 1, _NV), F32),
                        pltpu.SemaphoreType.DMA((14,)), pltpu.SemaphoreType.DMA((14,))],
        compiler_params=_params(32),
    )(*operands, *after)
    return [outs[k * _N_FIELDS:(k + 1) * _N_FIELDS] for k in range(4)], outs[-1]


def _small_fields(norm_g, b_gate, ln_g, ln_b, b_s, final_g, rel_bias, w_s):
    rel = jnp.pad(rel_bias.reshape(N_HEADS, N_REL), ((0, 0), (0, _REL_PAD - N_REL)))
    return (norm_g, b_gate, ln_g, ln_b, b_s.reshape(1, N_GROUPS * 128), final_g.reshape(1, D_MODEL),
            rel, w_s.reshape(N_GROUPS * 128, 128))


def _small_outputs(fields):
    n_g, b_g, l_g, l_b, b_s, f_g, rel, w_s = fields
    return (n_g, b_g, rel[:, :N_REL].reshape(1, N_HEADS, N_REL), l_g, l_b,
            w_s.reshape(1, N_GROUPS, 128, 128), b_s.reshape(1, N_GROUPS, 128), f_g.reshape(D_MODEL))


def _bias_row(rel_bias):
    hi = rel_bias[:, N_REL - 1:N_REL]
    lo = rel_bias[:, 0:1]
    return jnp.concatenate([jnp.broadcast_to(hi, (N_HEADS, 384)), rel_bias[:, ::-1],
                            jnp.broadcast_to(lo, (N_HEADS, 191)), jnp.broadcast_to(hi, (N_HEADS, 192))], axis=1)


def kernel(x, norm_g, w_in, b_gate, rel_bias, sgu_ln_g, sgu_ln_b, w_s, b_s, w_pa, w_pb, w_out, final_g, loss_target, m_norm_g, m_w_in, m_b_gate, m_rel_bias, m_sgu_ln_g, m_sgu_ln_b, m_w_s, m_b_s, m_w_pa, m_w_pb, m_w_out, m_final_g, v_norm_g, v_w_in, v_b_gate, v_rel_bias, v_sgu_ln_g, v_sgu_ln_b, v_w_s, v_b_s, v_w_pa, v_w_pb, v_w_out, v_final_g):
    S = x.shape[1]
    xs = x.reshape(S, D_MODEL)
    tgt = loss_target.reshape(S, D_MODEL)
    big_w = (w_in[0], w_pa[0], w_pb[0], w_out[0])
    big_m = (m_w_in[0], m_w_pa[0], m_w_pb[0], m_w_out[0])
    big_v = (v_w_in[0], v_w_pa[0], v_w_pb[0], v_w_out[0])
    rel = rel_bias[0]
    ws = w_s[0]
    bst = b_s[0].T
    fg = final_g.reshape(1, D_MODEL)
    pos = jnp.stack([lax.axis_index("c"), 2 * lax.axis_index("x") + lax.axis_index("y")]).astype(jnp.int32)

    staged = _stage_weights((1, 2, 3), big_w[1:], pos)
    w_in_bf, = _ag_weights((0,), big_w[:1])
    ag_s = _split_start("ag_small_start", staged, 9, _gather_copies((1, 2, 3)))

    ht, q3, k3, v3, zrest = _inproj_fwd(xs, norm_g, w_in_bf, after=(ag_s.token,))
    gp = _bias_row(rel)
    att = _attn_fwd(q3, k3, v3, gp)
    sg = _sgu_fwd(zrest, sgu_ln_g, sgu_ln_b, ws, bst)
    w_pa_bf, w_pb_bf, w_out_bf = _split_wait("ag_small_wait", ag_s, _gather_copies((1, 2, 3)), sg)
    (d_out, d_att, d_sg, dzt, gw_out, gw_pa, gw_pb, g_bgate, g_final, loss_row) = _tail(
        att, sg, zrest, xs, tgt, w_pa_bf, w_pb_bf, w_out_bf, b_gate, fg)
    ws_s, ws_i = (1, 2, 3), (0,)
    names = ("adamw_w_in", "adamw_w_pa", "adamw_w_pb", "adamw_w_out")

    x1s = _split_start("gx1s_start", [gw_pa, gw_pb, gw_out] + _x1_lands(ws_s), 12, _x1_copies(ws_s))
    dq, dk, dv, d_gp = _attn_bwd(q3, k3, v3, d_att, gp, after=(x1s.token,))
    got = _split_wait("gx1s_wait", x1s, _x1_copies(ws_s), dq)
    cs_s, csb_s = _grad_add1_group(ws_s, got[:3], got[3:])

    x2s = _split_start("gx2s_start", csb_s + _x2_lands(ws_s), 9, _x2_copies(3))
    dzs, g_ws, g_bs_t, g_lng, g_lnb = _sgu_bwd(zrest, d_sg, sgu_ln_g, sgu_ln_b, ws, bst, after=(x2s.token,))
    gw_in = _gw_in(ht, dq, dk, dv, dzt, dzs)
    got = _split_wait("gx2s_wait", x2s, _x2_copies(3), gw_in)
    halves_s = _grad_add2_group(ws_s, cs_s, got[3:])

    x3s = _split_start("gx3s_start", halves_s, 3, _x3_copies(ws_s))
    x1i = _split_start("gx1i_start", [gw_in] + _x1_lands(ws_i), 4, _x1_copies(ws_i))
    dh_args = (dq, dk, dv, dzt, dzs, w_in_bf, xs, norm_g, d_out)
    part = _dh_gradx(*dh_args, after=(x3s.token, x1i.token))
    g_shards_s = _split_wait("gx3s_wait", x3s, _x3_copies(ws_s), part[0])
    got = _split_wait("gx1i_wait", x1i, _x1_copies(ws_i), part[0])
    sum_i = _grad_add1(0, got[0], got[1], pos)

    x2i = _split_start("gx2i_start", [sum_i[1]] + _x2_lands(ws_i), 3, _x2_copies(1))
    grad_x, g_norm = _dh_gradx(*dh_args, prev=part, after=(x2i.token,))
    big = [None] * 4
    big[1:] = _adamw_group(big_w[1:], g_shards_s, big_m[1:], big_v[1:], after=(x2i.token,))

    g_rel = jnp.pad(d_gp[:, 384:384 + N_REL][:, ::-1], ((0, 0), (0, _REL_PAD - N_REL)))
    small_grads = (g_norm, g_bgate, g_lng, g_lnb, g_bs_t, g_final, g_rel, g_ws.reshape(N_GROUPS * 128, 128))
    small_params = (_small_fields(norm_g, b_gate, sgu_ln_g, sgu_ln_b, b_s, final_g, rel_bias, w_s),
                    _small_fields(m_norm_g, m_b_gate, m_sgu_ln_g, m_sgu_ln_b, m_b_s, m_final_g, m_rel_bias, m_w_s),
                    _small_fields(v_norm_g, v_b_gate, v_sgu_ln_g, v_sgu_ln_b, v_b_s, v_final_g, v_rel_bias, v_w_s))
    (gsum, sdelta, sm, sv), loss_out = _small_reduce_adamw(small_grads, loss_row, small_params, after=(x2i.token,))

    got = _split_wait("gx2i_wait", x2i, _x2_copies(1), loss_out)
    half_i = _grad_add2(0, sum_i[0], got[1], pos)
    g_shard_i, = _grad_xchg3(ws_i, [half_i])
    big[0] = _adamw(names[0], big_w[0], g_shard_i, big_m[0], big_v[0])
    g_shards = [g_shard_i] + list(g_shards_s)
    sg_out, sd_out, sm_out, sv_out = (_small_outputs(f) for f in (gsum, sdelta, sm, sv))
    loss = loss_out[0, 0]

    def assemble(small, bigs):
        n_g, b_g, r_b, l_g, l_b, w_s_, b_s_, f_g = small
        b_in, b_pa, b_pb, b_out = (b[None] for b in bigs)
        return (n_g, b_in, b_g, r_b, l_g, l_b, w_s_, b_s_, b_pa, b_pb, b_out, f_g)

    grads_out = assemble(sg_out, g_shards)
    delta_out = assemble(sd_out, [b[0] for b in big])
    m_out = assemble(sm_out, [b[1] for b in big])
    v_out = assemble(sv_out, [b[2] for b in big])
    return (loss, grad_x.reshape(1, S, D_MODEL), *grads_out, *delta_out, *m_out, *v_out)
```

```python
import functools
import math

import jax
import jax.numpy as jnp
from jax import lax
from jax.experimental import pallas as pl
from jax.experimental.pallas import tpu as pltpu

F32 = jnp.float32
BF = jnp.bfloat16
MESH = pl.DeviceIdType.MESH

D_MODEL = 1024
D_A = 512
D_B = 512
D_IN = 5632
N_HEADS = 8
HEAD_DIM = 64
CHUNK = 64
N_PREV = 8
SGU_CHUNK = 128
N_GROUPS = 4
N_REL = 257
EPS = 1e-6
NEG_INF = -1e30
SCALE = HEAD_DIM ** -0.5

QB = 2 * CHUNK
KB = (N_PREV + 2) * CHUNK
PADK = N_PREV * CHUNK
ROLL_W = 1024
N_RING = KB // QB
KEEP = N_RING - 1

ADAM_LR = 0.001
ADAM_B1 = 0.9
ADAM_B2 = 0.999
ADAM_EPS = 1e-08
ADAM_WD = 0.01
ADAM_STEP = 10
ADAM_C1 = 1.0 - ADAM_B1 ** ADAM_STEP
ADAM_C2 = 1.0 - ADAM_B2 ** ADAM_STEP

N_SHARD = 4
SHARD_IN = D_IN // N_SHARD
MIB = 1024 * 1024


VMEM_RESERVE_MIB = 60


def _params(vmem_mib, **kw):
    assert vmem_mib <= VMEM_RESERVE_MIB
    return pltpu.CompilerParams(vmem_limit_bytes=VMEM_RESERVE_MIB * MIB, **kw)


def _sigmoid(x):
    return 1.0 / (1.0 + jnp.exp(-x))


def _silu_and_grad(x):
    s = _sigmoid(x)
    return x * s, s * (1.0 + x * (1.0 - s))


_GELU_C = math.sqrt(2.0 / math.pi)
_GELU_A = 0.044715


def _gelu_and_grad(x):
    x2 = x * x
    t = jnp.tanh(_GELU_C * (x + _GELU_A * (x2 * x)))
    cdf = 0.5 * (1.0 + t)
    grad = cdf + 0.5 * x * (1.0 - t * t) * (_GELU_C * (1.0 + 3.0 * _GELU_A * x2))
    return x * cdf, grad


def _dot(a, b):
    return jnp.dot(a, b, preferred_element_type=F32)


def _dot_nt(a, b):
    return lax.dot_general(a, b, (((1,), (1,)), ((), ())), preferred_element_type=F32)


def _dot_tn(a, b):
    return lax.dot_general(a, b, (((0,), (0,)), ((), ())), preferred_element_type=F32)


def _mo(v, m):
    return v if isinstance(v, int) else pl.multiple_of(v, m)


def _unit_in(ref, s, p):
    return ref.at[pl.ds(_mo(p * 512, 512), 512), pl.ds(_mo(s * SHARD_IN, 128), SHARD_IN)]


def _unit_p(ref, s, p):
    return ref.at[pl.ds(_mo(p * 256, 256), 256), pl.ds(_mo(s * 256, 128), 256)]


def _unit_out(ref, s, p):
    return ref.at[pl.ds(_mo(s * 256 + p * 128, 128), 128), :]


_UNITS = (_unit_in, _unit_p, _unit_p, _unit_out)
_HALF_ROWS = (512, 256, 256, 128)
_UNIT_SHAPES = ((512, SHARD_IN), (256, 256), (256, 256), (128, D_MODEL))
_FULL_SHAPES = ((D_MODEL, D_IN), (D_A, D_MODEL), (D_B, D_MODEL), (D_MODEL, D_MODEL))
_SHARD_SHAPES = ((D_MODEL, SHARD_IN), (D_A, 256), (D_B, 256), (256, D_MODEL))


def _mesh_pos():
    x, y, c = lax.axis_index("x"), lax.axis_index("y"), lax.axis_index("c")
    chips = [(1 - x, y), (x, 1 - y), (1 - x, 1 - y)]
    return x, y, c, chips


def _ag_weights(ws, shards):
    n = len(ws)

    def body(*refs):
        ins, outs, stage = refs[:n], refs[n:2 * n], refs[2 * n:3 * n]
        send_sems, recv_sems, local_sems = refs[3 * n:]
        x, y, c, chips = _mesh_pos()
        s_me = 2 * x + y
        sibling = (x, y, 1 - c)
        for k in range(n):
            stage[k][...] = ins[k][...].astype(BF)

        def half(k, p):
            rows = _HALF_ROWS[ws[k]]
            return stage[k].at[pl.ds(_mo(p * rows, rows), rows), :]

        def unit(k, s, p):
            return _UNITS[ws[k]](outs[k], s, p)

        local = []
        for k in range(n):
            for p in range(2):
                cp = pltpu.make_async_copy(half(k, p), unit(k, s_me, p), local_sems.at[k, p])
                cp.start()
                local.append(cp)

        def rcopy(k, i, src, dst, to):
            return pltpu.make_async_remote_copy(src_ref=src, dst_ref=dst, send_sem=send_sems.at[k, i],
                                                recv_sem=recv_sems.at[k, i], device_id=to, device_id_type=MESH)

        sends = []
        for j, (cx, cy) in enumerate(chips):
            for k in range(n):
                cp = rcopy(k, j, half(k, c), unit(k, s_me, c), (cx, cy, c))
                cp.start()
                sends.append(cp)
        for j, (cx, cy) in enumerate(chips):
            for k in range(n):
                landed = unit(k, 2 * cx + cy, c)
                rcopy(k, j, landed, landed, (cx, cy, c)).wait_recv()
                cp = rcopy(k, 3 + j, landed, landed, sibling)
                cp.start()
                sends.append(cp)
        for j, (cx, cy) in enumerate(chips):
            for k in range(n):
                other = unit(k, 2 * cx + cy, 1 - c)
                rcopy(k, 3 + j, other, other, sibling).wait_recv()
        for cp in sends:
            cp.wait_send()
        for cp in local:
            cp.wait()

    vm = pl.BlockSpec(memory_space=pltpu.VMEM)
    return pl.pallas_call(
        body, name="ag_weights",
        out_shape=tuple(jax.ShapeDtypeStruct(_FULL_SHAPES[w], BF) for w in ws),
        in_specs=[vm] * n, out_specs=[_ANY] * n,
        scratch_shapes=[pltpu.VMEM(_SHARD_SHAPES[w], BF) for w in ws]
        + [pltpu.SemaphoreType.DMA((n, 6)), pltpu.SemaphoreType.DMA((n, 6)), pltpu.SemaphoreType.DMA((n, 2))],
        compiler_params=_params(40),
    )(*shards)


def _shard_of(ref, w, s):
    if w == 0:
        return ref.at[:, pl.ds(_mo(s * SHARD_IN, 128), SHARD_IN)]
    if w == 3:
        return ref.at[pl.ds(_mo(s * 256, 256), 256), :]
    return ref.at[:, pl.ds(_mo(s * 256, 128), 256)]


def _stage_weights(ws, shards, pos):
    n = len(ws)

    def body(pos_ref, *refs):
        for k in range(n):
            refs[n + k][...] = refs[k][...].astype(BF)

    def spec(w):
        shape = _SHARD_SHAPES[w]
        if w == 3:
            return pl.BlockSpec(shape, lambda i, pos: (pos[1], 0))
        return pl.BlockSpec(shape, lambda i, pos: (0, pos[1]))

    return list(pl.pallas_call(
        body, name="stage_weights",
        grid_spec=pltpu.PrefetchScalarGridSpec(
            num_scalar_prefetch=1, grid=(1,),
            in_specs=[pl.BlockSpec(_SHARD_SHAPES[w], lambda i, pos: (0, 0)) for w in ws],
            out_specs=[spec(w) for w in ws]),
        out_shape=tuple(jax.ShapeDtypeStruct(_FULL_SHAPES[w], BF) for w in ws),
        compiler_params=_params(16, dimension_semantics=("arbitrary",)),
    )(pos, *shards))


def _gather_copies(ws):
    def copies(refs, send_sems, recv_sems):
        x, y, c, chips = _mesh_pos()
        out = []
        for j, (cx, cy) in enumerate(chips):
            for k, w in enumerate(ws):
                mine = _shard_of(refs[k], w, 2 * x + y)
                out.append(pltpu.make_async_remote_copy(
                    src_ref=mine, dst_ref=mine, send_sem=send_sems.at[3 * k + j], recv_sem=recv_sems.at[3 * k + j],
                    device_id=(cx, cy, c), device_id_type=MESH))
        return out
    return copies


def _inproj_fwd(x, norm_g, w_in_bf, tm=512, after=()):
    S = x.shape[0]

    def body(x_ref, g_ref, w_ref, ht_ref, q_ref, k_ref, v_ref, zr_ref):
        xv = x_ref[...]
        r = lax.rsqrt(jnp.mean(xv * xv, axis=-1, keepdims=True) + EPS)
        hf = (xv * r) * g_ref[...]
        ht_ref[...] = hf.T.astype(BF)
        h = hf.astype(BF)
        heads = (q_ref, k_ref, v_ref)
        for j in range(D_IN // 512):
            z = _dot(h, w_ref[:, j * 512:(j + 1) * 512])
            if j < 3:
                zb = z.astype(BF)
                for hd in range(N_HEADS):
                    heads[j][hd] = zb[:, hd * HEAD_DIM:(hd + 1) * HEAD_DIM]
            else:
                zr_ref[:, (j - 3) * 512:(j - 2) * 512] = z

    head_major = jax.ShapeDtypeStruct((N_HEADS, S, HEAD_DIM), BF)
    head_spec = pl.BlockSpec((N_HEADS, tm, HEAD_DIM), lambda i: (0, i, 0))
    return pl.pallas_call(
        _after(body, 3, after), name="inproj_fwd", grid=(S // tm,),
        out_shape=(jax.ShapeDtypeStruct((D_MODEL, S), BF), head_major, head_major, head_major,
                   jax.ShapeDtypeStruct((S, D_IN - 3 * D_A), F32)),
        in_specs=[pl.BlockSpec((tm, D_MODEL), lambda i: (i, 0)),
                  pl.BlockSpec((1, D_MODEL), lambda i: (0, 0)),
                  pl.BlockSpec((D_MODEL, D_IN), lambda i: (0, 0), pipeline_mode=pl.Buffered(1))]
        + [_ANY] * len(after),
        out_specs=[pl.BlockSpec((D_MODEL, tm), lambda i: (0, i)),
                   head_spec, head_spec, head_spec,
                   pl.BlockSpec((tm, D_IN - 3 * D_A), lambda i: (i, 0))],
        compiler_params=_params(52, dimension_semantics=("arbitrary",)),
    )(x, norm_g, w_in_bf, *after)


def _skew_table(gp_row):
    row = lax.broadcasted_iota(jnp.int32, (QB, ROLL_W), 0)
    t = jnp.broadcast_to(gp_row, (QB, ROLL_W))
    for b in range(7):
        t = jnp.where(((row >> b) & 1) == 1, pltpu.roll(t, 1 << b, axis=1), t)
    return t


def _unskew_sum(d):
    row = lax.broadcasted_iota(jnp.int32, (QB, ROLL_W), 0)
    for b in range(7):
        d = jnp.where(((row >> b) & 1) == 1, pltpu.roll(d, ROLL_W - (1 << b), axis=1), d)
    return jnp.sum(d, axis=0, keepdims=True)


def _struct_mask():
    a = lax.broadcasted_iota(jnp.int32, (QB, KB), 0) // CHUNK
    b = lax.broadcasted_iota(jnp.int32, (QB, KB), 1) // CHUNK
    return (b >= a) & (b <= a + N_PREV)


def _load_kv(k_hbm, v_hbm, gp_ref, k_scr, v_scr, bias_scr, sems, S):
    zeros = jnp.zeros((N_HEADS, PADK, HEAD_DIM), BF)
    k_scr[:, 0:PADK, :] = zeros
    v_scr[:, 0:PADK, :] = zeros
    ck = pltpu.make_async_copy(k_hbm, k_scr.at[:, pl.ds(PADK, S), :], sems.at[0])
    cv = pltpu.make_async_copy(v_hbm, v_scr.at[:, pl.ds(PADK, S), :], sems.at[1])
    ck.start()
    cv.start()
    keep = _struct_mask()
    for h in range(N_HEADS):
        bias_scr[h] = jnp.where(keep, _skew_table(gp_ref[h:h + 1, :])[:, :KB], NEG_INF)
    ck.wait()
    cv.wait()


_BATCH_NT = (((2,), (2,)), ((0,), (0,)))
_BATCH_NN = (((2,), (1,)), ((0,), (0,)))
_BATCH_TN = (((1,), (1,)), ((0,), (0,)))


def _bdot(a, b, dims):
    return lax.dot_general(a, b, dims, preferred_element_type=F32)


def _scaled(q):
    return q * jnp.asarray(SCALE, BF)


def _probs(qs, kb, bias, i, front):
    s = _bdot(qs, kb, _BATCH_NT) + bias
    if front:
        col = lax.broadcasted_iota(jnp.int32, (1, 1, KB), 2)
        s = jnp.where(col >= PADK - i * QB, s, NEG_INF)
    m = jnp.max(s, axis=-1, keepdims=True)
    e = jnp.exp(s - m)
    return e * (1.0 / jnp.sum(e, axis=-1, keepdims=True))


def _attn_fwd(q3, k3, v3, gp):
    S = q3.shape[1]

    def body(q_ref, k_hbm, v_hbm, gp_ref, o_ref, k_scr, v_scr, bias_scr, sems):
        i = pl.program_id(0)

        @pl.when(i == 0)
        def _():
            _load_kv(k_hbm, v_hbm, gp_ref, k_scr, v_scr, bias_scr, sems, S)

        def step(front):
            start = pl.multiple_of(i * QB, QB)
            kb = k_scr[:, pl.ds(start, KB), :]
            vb = v_scr[:, pl.ds(start, KB), :]
            p = _probs(_scaled(q_ref[...]), kb, bias_scr[...], i, front)
            o = _bdot(p.astype(BF), vb, _BATCH_NN)
            for h in range(N_HEADS):
                o_ref[:, h * HEAD_DIM:(h + 1) * HEAD_DIM] = o[h]

        pl.when(i < KEEP)(functools.partial(step, True))
        pl.when(i >= KEEP)(functools.partial(step, False))

    kv_scr = pltpu.VMEM((N_HEADS, S + PADK, HEAD_DIM), BF)
    return pl.pallas_call(
        body, name="attn_fwd", grid=(S // QB,),
        out_shape=jax.ShapeDtypeStruct((S, D_A), F32),
        in_specs=[pl.BlockSpec((N_HEADS, QB, HEAD_DIM), lambda i: (0, i, 0)),
                  pl.BlockSpec(memory_space=pl.ANY), pl.BlockSpec(memory_space=pl.ANY),
                  pl.BlockSpec((N_HEADS, ROLL_W), lambda i: (0, 0))],
        out_specs=pl.BlockSpec((QB, D_A), lambda i: (i, 0)),
        scratch_shapes=[kv_scr, kv_scr, pltpu.VMEM((N_HEADS, QB, KB), F32), pltpu.SemaphoreType.DMA((2,))],
        compiler_params=_params(48, dimension_semantics=("arbitrary",)),
    )(q3, k3, v3, gp)


def _attn_bwd(q3, k3, v3, d_att3, gp, after=()):
    S = q3.shape[1]
    nq = S // QB

    def body(q_ref, do_ref, k_hbm, v_hbm, gp_ref, dq_ref, dk_ref, dv_ref, dgp_ref,
             k_scr, v_scr, bias_scr, dk_acc, dv_acc, dbias_acc, pad_scr, sems):
        i = pl.program_id(0)

        @pl.when(i == 0)
        def _():
            _load_kv(k_hbm, v_hbm, gp_ref, k_scr, v_scr, bias_scr, sems, S)
            dk_acc[...] = jnp.zeros_like(dk_acc)
            dv_acc[...] = jnp.zeros_like(dv_acc)
            dbias_acc[...] = jnp.zeros_like(dbias_acc)

        def step(front):
            start = pl.multiple_of(i * QB, QB)
            kb = k_scr[:, pl.ds(start, KB), :]
            vb = v_scr[:, pl.ds(start, KB), :]
            qs = _scaled(q_ref[...])
            do = do_ref[...]
            p = _probs(qs, kb, bias_scr[...], i, front)
            dp = _bdot(do, vb, _BATCH_NT)
            ds = p * (dp - jnp.sum(dp * p, axis=-1, keepdims=True))
            dbias_acc[...] += ds
            dsb = ds.astype(BF)
            dq = _bdot(dsb, kb, _BATCH_NN) * SCALE
            for h in range(N_HEADS):
                dq_ref[:, h * HEAD_DIM:(h + 1) * HEAD_DIM] = dq[h].astype(BF)
            dk_acc[...] += _bdot(dsb, qs, _BATCH_TN)
            dv_acc[...] += _bdot(p.astype(BF), do, _BATCH_TN)

        pl.when(i < KEEP)(functools.partial(step, True))
        pl.when((i >= KEEP) & (i < nq))(functools.partial(step, False))

        for h in range(N_HEADS):
            hs = slice(h * HEAD_DIM, (h + 1) * HEAD_DIM)
            dk_ref[:, hs] = dk_acc[h, 0:QB, :].astype(BF)
            dv_ref[:, hs] = dv_acc[h, 0:QB, :].astype(BF)
        dk_acc[:, 0:KB - QB, :] = dk_acc[:, QB:KB, :]
        dv_acc[:, 0:KB - QB, :] = dv_acc[:, QB:KB, :]
        dk_acc[:, KB - QB:KB, :] = jnp.zeros((N_HEADS, QB, HEAD_DIM), F32)
        dv_acc[:, KB - QB:KB, :] = jnp.zeros((N_HEADS, QB, HEAD_DIM), F32)

        @pl.when(i == nq + KEEP - 1)
        def _():
            lane = lax.broadcasted_iota(jnp.int32, (1, ROLL_W), 1)
            hi = (lane < 384) | (lane >= 832)
            lo = (lane > 640) & (lane < 832)
            pad_scr[...] = jnp.zeros_like(pad_scr)
            for h in range(N_HEADS):
                pad_scr[:, 0:KB] = dbias_acc[h]
                g = _unskew_sum(pad_scr[...])
                s_hi = jnp.sum(jnp.where(hi, g, 0.0), axis=-1, keepdims=True)
                s_lo = jnp.sum(jnp.where(lo, g, 0.0), axis=-1, keepdims=True)
                g = jnp.where(lane == 384, g + s_hi, g)
                g = jnp.where(lane == 640, g + s_lo, g)
                dgp_ref[h:h + 1, :] = g

    last = nq - 1
    kv_scr = pltpu.VMEM((N_HEADS, S + PADK, HEAD_DIM), BF)
    return pl.pallas_call(
        _after(body, 5, after), name="attn_bwd", grid=(nq + KEEP,),
        out_shape=(jax.ShapeDtypeStruct((S, D_A), BF), jax.ShapeDtypeStruct((S, D_A), BF),
                   jax.ShapeDtypeStruct((S, D_A), BF), jax.ShapeDtypeStruct((N_HEADS, ROLL_W), F32)),
        in_specs=[pl.BlockSpec((N_HEADS, QB, HEAD_DIM), lambda i: (0, jnp.minimum(i, last), 0)),
                  pl.BlockSpec((N_HEADS, QB, HEAD_DIM), lambda i: (0, jnp.minimum(i, last), 0)),
                  pl.BlockSpec(memory_space=pl.ANY), pl.BlockSpec(memory_space=pl.ANY),
                  pl.BlockSpec((N_HEADS, ROLL_W), lambda i: (0, 0))] + [_ANY] * len(after),
        out_specs=[pl.BlockSpec((QB, D_A), lambda i: (jnp.minimum(i, last), 0)),
                   pl.BlockSpec((QB, D_A), lambda i: (jnp.maximum(i - KEEP, 0), 0)),
                   pl.BlockSpec((QB, D_A), lambda i: (jnp.maximum(i - KEEP, 0), 0)),
                   pl.BlockSpec((N_HEADS, ROLL_W), lambda i: (0, 0))],
        scratch_shapes=[kv_scr, kv_scr, pltpu.VMEM((N_HEADS, QB, KB), F32),
                        pltpu.VMEM((N_HEADS, KB, HEAD_DIM), F32), pltpu.VMEM((N_HEADS, KB, HEAD_DIM), F32),
                        pltpu.VMEM((N_HEADS, QB, KB), F32), pltpu.VMEM((QB, ROLL_W), F32),
                        pltpu.SemaphoreType.DMA((2,))],
        compiler_params=_params(56, dimension_semantics=("arbitrary",)),
    )(q3, d_att3, k3, v3, gp, *after)


def _sgu_core(ub, vb, lg, lb):
    u, du = _gelu_and_grad(ub)
    v, dv = _gelu_and_grad(vb)
    mu = jnp.mean(v, axis=-1, keepdims=True)
    vc = v - mu
    rstd = lax.rsqrt(jnp.mean(vc * vc, axis=-1, keepdims=True) + EPS)
    xh = vc * rstd
    vn = xh * lg + lb
    return u, du, dv, rstd, xh, vn


def _tri():
    r = lax.broadcasted_iota(jnp.int32, (SGU_CHUNK, SGU_CHUNK), 0)
    c = lax.broadcasted_iota(jnp.int32, (SGU_CHUNK, SGU_CHUNK), 1)
    return r >= c


def _sgu_fwd(zrest, ln_g, ln_b, w_s, b_s_t, tm=512):
    S = zrest.shape[0]

    def body(ub_ref, vb_ref, lg_ref, lb_ref, ws_ref, bst_ref, sg_ref):
        u, _, _, _, _, vn = _sgu_core(ub_ref[...], vb_ref[...], lg_ref[...], lb_ref[...])
        vnb = vn.astype(BF)
        tri = _tri()
        for g in range(N_GROUPS):
            cs = slice(g * 128, (g + 1) * 128)
            wt = jnp.where(tri, ws_ref[g], 0.0).astype(BF)
            bcol = bst_ref[:, g:g + 1]
            for n in range(tm // SGU_CHUNK):
                rs = slice(n * SGU_CHUNK, (n + 1) * SGU_CHUNK)
                mixed = _dot(wt, vnb[rs, cs]) + bcol
                sg_ref[rs, cs] = u[rs, cs] * mixed

    return pl.pallas_call(
        body, name="sgu_fwd", grid=(S // tm,),
        out_shape=jax.ShapeDtypeStruct((S, D_B), F32),
        in_specs=[pl.BlockSpec((tm, 512), lambda i: (i, 1)),
                  pl.BlockSpec((tm, 512), lambda i: (i, 2)),
                  pl.BlockSpec((1, D_B), lambda i: (0, 0)),
                  pl.BlockSpec((1, D_B), lambda i: (0, 0)),
                  pl.BlockSpec((N_GROUPS, 128, 128), lambda i: (0, 0, 0)),
                  pl.BlockSpec((128, N_GROUPS), lambda i: (0, 0))],
        out_specs=pl.BlockSpec((tm, D_B), lambda i: (i, 0)),
        compiler_params=_params(32, dimension_semantics=("arbitrary",)),
    )(zrest, zrest, ln_g, ln_b, w_s, b_s_t)


def _sgu_bwd(zrest, d_sg, ln_g, ln_b, w_s, b_s_t, tm=256, after=()):
    S = zrest.shape[0]
    nt = S // tm

    def body(ub_ref, vb_ref, dsg_ref, lg_ref, lb_ref, ws_ref, bst_ref,
             dzs_ref, gws_ref, gbs_ref, glg_ref, glb_ref, dvn_scr, bs_acc):
        i = pl.program_id(0)

        @pl.when(i == 0)
        def _():
            gws_ref[...] = jnp.zeros_like(gws_ref)
            glg_ref[...] = jnp.zeros_like(glg_ref)
            glb_ref[...] = jnp.zeros_like(glb_ref)
            bs_acc[...] = jnp.zeros_like(bs_acc)

        ub = ub_ref[...]
        u, du, dv, rstd, xh, vn = _sgu_core(ub, vb_ref[...], lg_ref[...], lb_ref[...])
        vnb = vn.astype(BF)
        dsg = dsg_ref[...]
        tri = _tri()
        for g in range(N_GROUPS):
            cs = slice(g * 128, (g + 1) * 128)
            wtf = jnp.where(tri, ws_ref[g], 0.0)
            wt = wtf.astype(BF)
            wtt = wtf.T.astype(BF)
            bcol = bst_ref[:, g:g + 1]
            for n in range(tm // SGU_CHUNK):
                rs = slice(n * SGU_CHUNK, (n + 1) * SGU_CHUNK)
                mixed = _dot(wt, vnb[rs, cs]) + bcol
                dzs_ref[rs, cs] = (dsg[rs, cs] * mixed * du[rs, cs]).astype(BF)
                dmix = dsg[rs, cs] * u[rs, cs]
                bs_acc[:, cs] += dmix
                dmb = dmix.astype(BF)
                gws_ref[g] += _dot_nt(dmb, vnb[rs, cs])
                dvn_scr[rs, cs] = _dot(wtt, dmb)
        dvn = dvn_scr[...]
        glg_ref[...] += jnp.sum(dvn * xh, axis=0, keepdims=True)
        glb_ref[...] += jnp.sum(dvn, axis=0, keepdims=True)
        dxh = dvn * lg_ref[...]
        dvv = rstd * (dxh - jnp.mean(dxh, axis=-1, keepdims=True)
                      - xh * jnp.mean(dxh * xh, axis=-1, keepdims=True))
        dzs_ref[:, D_B:2 * D_B] = (dvv * dv).astype(BF)

        @pl.when(i == nt - 1)
        def _():
            lane = lax.broadcasted_iota(jnp.int32, (SGU_CHUNK, 128), 1)
            out = jnp.zeros((SGU_CHUNK, 128), F32)
            for g in range(N_GROUPS):
                gws_ref[g] = jnp.where(tri, gws_ref[g], 0.0)
                col = jnp.sum(bs_acc[:, g * 128:(g + 1) * 128], axis=-1, keepdims=True)
                out = jnp.where(lane == g, col, out)
            gbs_ref[...] = out

    const2 = lambda i: (0, 0)
    return pl.pallas_call(
        _after(body, 7, after), name="sgu_bwd", grid=(nt,),
        out_shape=(jax.ShapeDtypeStruct((S, 2 * D_B), BF),
                   jax.ShapeDtypeStruct((N_GROUPS, 128, 128), F32),
                   jax.ShapeDtypeStruct((SGU_CHUNK, 128), F32),
                   jax.ShapeDtypeStruct((1, D_B), F32), jax.ShapeDtypeStruct((1, D_B), F32)),
        in_specs=[pl.BlockSpec((tm, 512), lambda i: (i, 1)),
                  pl.BlockSpec((tm, 512), lambda i: (i, 2)),
                  pl.BlockSpec((tm, D_B), lambda i: (i, 0)),
                  pl.BlockSpec((1, D_B), const2), pl.BlockSpec((1, D_B), const2),
                  pl.BlockSpec((N_GROUPS, 128, 128), lambda i: (0, 0, 0)),
                  pl.BlockSpec((128, N_GROUPS), const2)] + [_ANY] * len(after),
        out_specs=[pl.BlockSpec((tm, 2 * D_B), lambda i: (i, 0)),
                   pl.BlockSpec((N_GROUPS, 128, 128), lambda i: (0, 0, 0)),
                   pl.BlockSpec((SGU_CHUNK, 128), const2),
                   pl.BlockSpec((1, D_B), const2), pl.BlockSpec((1, D_B), const2)],
        scratch_shapes=[pltpu.VMEM((tm, D_B), F32), pltpu.VMEM((SGU_CHUNK, D_B), F32)],
        compiler_params=_params(32, dimension_semantics=("arbitrary",)),
    )(zrest, zrest, d_sg, ln_g, ln_b, w_s, b_s_t, *after)


def _tail(att, sg, zrest, x, target, w_pa, w_pb, w_out, b_gate, final_g, tm=256):
    S = x.shape[0]
    nt = S // tm

    def body(att_ref, sg_ref, ga_ref, gb_ref, gta_ref, gtb_ref, x_ref, t_ref,
             wpa_ref, wpb_ref, wout_ref, bg_ref, fg_ref,
             dout_ref, datt_ref, dsg_ref, dzt_ref, gwout_hbm, gwpa_hbm, gwpb_hbm,
             gbg_ref, gfg_ref, loss_ref, acc_out, acc_pa, acc_pb, sems):
        i = pl.program_id(0)

        @pl.when(i == 0)
        def _():
            acc_out[...] = jnp.zeros_like(acc_out)
            acc_pa[...] = jnp.zeros_like(acc_pa)
            acc_pb[...] = jnp.zeros_like(acc_pb)
            gbg_ref[...] = jnp.zeros_like(gbg_ref)
            gfg_ref[...] = jnp.zeros_like(gfg_ref)
            loss_ref[...] = jnp.zeros_like(loss_ref)

        att = att_ref[...]
        sg = sg_ref[...]
        sa, dsa = _silu_and_grad(ga_ref[...])
        sb, dsb = _silu_and_grad(gb_ref[...])
        ya = (att * sa).astype(BF)
        yb = (sg * sb).astype(BF)
        pa = _dot(ya, wpa_ref[...])
        pb = _dot(yb, wpb_ref[...])
        ga = _sigmoid(gta_ref[...] + bg_ref[:, 0:D_MODEL])
        gb = _sigmoid(gtb_ref[...] + bg_ref[:, D_MODEL:2 * D_MODEL])
        merged = (ga * pa + gb * pb).astype(BF)
        out = x_ref[...] + _dot(merged, wout_ref[...])
        r2 = lax.rsqrt(jnp.mean(out * out, axis=-1, keepdims=True) + EPS)
        nrm = out * r2
        fg = fg_ref[...]
        err = nrm * fg - t_ref[...]
        loss_ref[...] += 0.5 * jnp.sum(jnp.mean(err * err, axis=-1, keepdims=True))
        dy = err * (1.0 / D_MODEL)
        gfg_ref[...] += jnp.sum(dy * nrm, axis=0, keepdims=True)
        dn = dy * fg
        d_out = r2 * (dn - nrm * jnp.mean(dn * nrm, axis=-1, keepdims=True))
        dout_ref[...] = d_out
        d_outb = d_out.astype(BF)
        acc_out[...] += _dot_tn(merged, d_outb)
        dm = _dot_nt(d_outb, wout_ref[...])
        d_pa = (dm * ga).astype(BF)
        d_pb = (dm * gb).astype(BF)
        d_gta = dm * pa * (ga * (1.0 - ga))
        d_gtb = dm * pb * (gb * (1.0 - gb))
        gbg_ref[:, 0:D_MODEL] += jnp.sum(d_gta, axis=0, keepdims=True)
        gbg_ref[:, D_MODEL:2 * D_MODEL] += jnp.sum(d_gtb, axis=0, keepdims=True)
        dzt_ref[:, 2 * D_A:2 * D_A + D_MODEL] = d_gta.astype(BF)
        dzt_ref[:, 2 * D_A + D_MODEL:] = d_gtb.astype(BF)
        acc_pa[...] += _dot_tn(ya, d_pa)
        acc_pb[...] += _dot_tn(yb, d_pb)
        d_ya = _dot_nt(d_pa, wpa_ref[...])
        d_yb = _dot_nt(d_pb, wpb_ref[...])
        d_att = (d_ya * sa).astype(BF)
        for hd in range(N_HEADS):
            datt_ref[hd] = d_att[:, hd * HEAD_DIM:(hd + 1) * HEAD_DIM]
        dzt_ref[:, 0:D_A] = (d_ya * att * dsa).astype(BF)
        dsg_ref[...] = d_yb * sb
        dzt_ref[:, D_A:2 * D_A] = (d_yb * sg * dsb).astype(BF)

        @pl.when(i == nt - 1)
        def _():
            cps = [pltpu.make_async_copy(acc_out, gwout_hbm, sems.at[0]),
                   pltpu.make_async_copy(acc_pa, gwpa_hbm, sems.at[1]),
                   pltpu.make_async_copy(acc_pb, gwpb_hbm, sems.at[2])]
            for cp in cps:
                cp.start()
            for cp in cps:
                cp.wait()

    c2 = lambda i: (0, 0)
    hbm = pl.BlockSpec(memory_space=pl.ANY)
    return pl.pallas_call(
        body, name="tail", grid=(nt,),
        out_shape=(jax.ShapeDtypeStruct((S, D_MODEL), F32), jax.ShapeDtypeStruct((N_HEADS, S, HEAD_DIM), BF),
                   jax.ShapeDtypeStruct((S, D_B), F32), jax.ShapeDtypeStruct((S, 3072), BF),
                   jax.ShapeDtypeStruct((D_MODEL, D_MODEL), F32), jax.ShapeDtypeStruct((D_A, D_MODEL), F32),
                   jax.ShapeDtypeStruct((D_B, D_MODEL), F32),
                   jax.ShapeDtypeStruct((1, 2 * D_MODEL), F32), jax.ShapeDtypeStruct((1, D_MODEL), F32),
                   jax.ShapeDtypeStruct((1, 128), F32)),
        in_specs=[pl.BlockSpec((tm, D_A), lambda i: (i, 0)),
                  pl.BlockSpec((tm, D_B), lambda i: (i, 0)),
                  pl.BlockSpec((tm, 512), lambda i: (i, 0)),
                  pl.BlockSpec((tm, 512), lambda i: (i, 3)),
                  pl.BlockSpec((tm, D_MODEL), lambda i: (i, 2)),
                  pl.BlockSpec((tm, D_MODEL), lambda i: (i, 3)),
                  pl.BlockSpec((tm, D_MODEL), lambda i: (i, 0)),
                  pl.BlockSpec((tm, D_MODEL), lambda i: (i, 0)),
                  pl.BlockSpec((D_A, D_MODEL), c2), pl.BlockSpec((D_B, D_MODEL), c2),
                  pl.BlockSpec((D_MODEL, D_MODEL), c2),
                  pl.BlockSpec((1, 2 * D_MODEL), c2), pl.BlockSpec((1, D_MODEL), c2)],
        out_specs=[pl.BlockSpec((tm, D_MODEL), lambda i: (i, 0)),
                   pl.BlockSpec((N_HEADS, tm, HEAD_DIM), lambda i: (0, i, 0)),
                   pl.BlockSpec((tm, D_B), lambda i: (i, 0)),
                   pl.BlockSpec((tm, 3072), lambda i: (i, 0)),
                   hbm, hbm, hbm,
                   pl.BlockSpec((1, 2 * D_MODEL), c2), pl.BlockSpec((1, D_MODEL), c2),
                   pl.BlockSpec((1, 128), c2)],
        scratch_shapes=[pltpu.VMEM((D_MODEL, D_MODEL), F32), pltpu.VMEM((D_A, D_MODEL), F32),
                        pltpu.VMEM((D_B, D_MODEL), F32), pltpu.SemaphoreType.DMA((3,))],
        compiler_params=_params(56, dimension_semantics=("arbitrary",)),
    )(att, sg, zrest, zrest, zrest, zrest, x, target, w_pa, w_pb, w_out, b_gate, final_g)


_DZ_MAP = ((0, 0), (1, 0), (2, 0), (3, 0), (4, 0), (4, 1), (3, 1), (3, 2), (3, 3), (3, 4), (3, 5))


def _dh_gradx(dq, dk, dv, dzt, dzs, w_in_bf, x, norm_g, d_out, prev=None, tm=512, after=()):
    S = x.shape[0]
    n_first = S // tm // 4
    nt = n_first if prev is None else S // tm - n_first
    first = 0 if prev is None else n_first
    n_in = 9 if prev is None else 11

    def body(dq_ref, dk_ref, dv_ref, dzt_ref, dzs_ref, w_ref, x_ref, g_ref, dout_ref, *rest):
        gx_ref, gn_ref = rest[-2:]
        i = pl.program_id(0)

        @pl.when(i == 0)
        def _():
            gn_ref[...] = jnp.zeros_like(gn_ref) if prev is None else rest[1][...]

        pieces = (dq_ref, dk_ref, dv_ref, dzt_ref, dzs_ref)
        dh = jnp.zeros((tm, D_MODEL), F32)
        for j, (pc, blk) in enumerate(_DZ_MAP):
            dh += _dot_nt(pieces[pc][:, blk * 512:(blk + 1) * 512], w_ref[:, j * 512:(j + 1) * 512])
        xv = x_ref[...]
        r = lax.rsqrt(jnp.mean(xv * xv, axis=-1, keepdims=True) + EPS)
        nrm = xv * r
        gn_ref[...] += jnp.sum(dh * nrm, axis=0, keepdims=True)
        dn = dh * g_ref[...]
        gx_ref[...] = r * (dn - nrm * jnp.mean(dn * nrm, axis=-1, keepdims=True)) + dout_ref[...]

    row = lambda w: pl.BlockSpec((tm, w), lambda i: (i + first, 0))
    c2 = lambda i: (0, 0)
    more = [] if prev is None else [_ANY, pl.BlockSpec((1, D_MODEL), c2)]
    return pl.pallas_call(
        _after(body, n_in, after), name="dh_gradx_a" if prev is None else "dh_gradx_b", grid=(nt,),
        out_shape=(jax.ShapeDtypeStruct((S, D_MODEL), F32), jax.ShapeDtypeStruct((1, D_MODEL), F32)),
        in_specs=[row(512), row(512), row(512), row(3072), row(1024),
                  pl.BlockSpec((D_MODEL, D_IN), c2, pipeline_mode=pl.Buffered(1)), row(D_MODEL),
                  pl.BlockSpec((1, D_MODEL), c2), row(D_MODEL)]
        + more + [_ANY] * len(after),
        out_specs=[row(D_MODEL), pl.BlockSpec((1, D_MODEL), c2)],
        input_output_aliases={} if prev is None else {9: 0},
        compiler_params=_params(48, dimension_semantics=("arbitrary",)),
    )(dq, dk, dv, dzt, dzs, w_in_bf, x, norm_g, d_out, *(prev or ()), *after)


def _gw_in(ht, dq, dk, dv, dzt, dzs, tn=256, after=()):
    S = ht.shape[1]
    per = 512 // tn
    cols = tuple((pc, per * blk + h) for pc, blk in _DZ_MAP for h in range(per))

    def body(ht_ref, dq_ref, dk_ref, dv_ref, dzt_ref, dzs_ref, o_ref):
        j = pl.program_id(0)
        pieces = (dq_ref, dk_ref, dv_ref, dzt_ref, dzs_ref)
        for pc in range(5):
            hit = functools.reduce(jnp.logical_or, [j == jj for jj, (p, _) in enumerate(cols) if p == pc])

            @pl.when(hit)
            def _(pc=pc):
                o_ref[...] = _dot(ht_ref[...], pieces[pc][...])

    def piece_spec(pc):
        cur = next(blk for p, blk in cols if p == pc)
        held = []
        for p, blk in cols:
            cur = blk if p == pc else cur
            held.append(cur)

        def index_map(j):
            blk = jnp.int32(held[0])
            for jj in range(1, len(held)):
                if held[jj] != held[jj - 1]:
                    blk = jnp.where(j >= jj, jnp.int32(held[jj]), blk)
            return (0, blk)

        return pl.BlockSpec((S, tn), index_map)

    return pl.pallas_call(
        _after(body, 6, after), name="gw_in", grid=(len(cols),),
        out_shape=jax.ShapeDtypeStruct((D_MODEL, D_IN), F32),
        in_specs=[pl.BlockSpec((D_MODEL, S), lambda j: (0, 0))] + [piece_spec(pc) for pc in range(5)]
        + [_ANY] * len(after),
        out_specs=pl.BlockSpec((D_MODEL, tn), lambda j: (0, j)),
        compiler_params=_params(48, dimension_semantics=("arbitrary",)),
    )(ht, dq, dk, dv, dzt, dzs, *after)


_HBM = pl.BlockSpec(memory_space=pltpu.HBM)
_SEM = pl.BlockSpec(memory_space=pltpu.SEMAPHORE)
_ANY = pl.BlockSpec(memory_space=pl.ANY)
_EFFECT = pltpu.SideEffectType.DATAFLOW_SIDE_EFFECTING


def _in_hbm(a):
    return pltpu.with_memory_space_constraint(a, pltpu.HBM)


def _after(body, n_in, after):
    if not after:
        return body
    return lambda *refs: body(*refs[:n_in], *refs[n_in + len(after):])


class _Started:
    def __init__(self, send, recv, bufs, token):
        self.send, self.recv, self.bufs, self.token = send, recv, bufs, token


def _split_start(name, bufs, n_copies, copies, after=()):
    nb = len(bufs)

    def body(*refs):
        refs = refs[:nb] + refs[nb + len(after):]
        for cp in copies(refs[:nb], refs[nb], refs[nb + 1]):
            cp.start()
        refs[-1][...] = jnp.zeros_like(refs[-1])

    outs = pl.pallas_call(
        body, name=name,
        out_shape=(pltpu.SemaphoreType.DMA((n_copies,)), pltpu.SemaphoreType.DMA((n_copies,)),
                   *[pltpu.HBM(b.shape, b.dtype) for b in bufs], jax.ShapeDtypeStruct((8, 128), F32)),
        in_specs=[_HBM] * nb + [_ANY] * len(after),
        out_specs=(_SEM, _SEM, *[_HBM] * nb, pl.BlockSpec(memory_space=pltpu.VMEM)),
        input_output_aliases={k: 2 + k for k in range(nb)},
        compiler_params=_params(1, has_side_effects=_EFFECT),
    )(*[_in_hbm(b) for b in bufs], *after)
    return _Started(outs[0], outs[1], list(outs[2:2 + nb]), outs[-1])


def _split_wait(name, started, copies, after):
    nb = len(started.bufs)

    def body(*refs):
        for cp in copies(refs[:nb], refs[nb], refs[nb + 1]):
            cp.wait_send()
            cp.wait_recv()

    return list(pl.pallas_call(
        body, name=name,
        out_shape=tuple(pltpu.HBM(b.shape, b.dtype) for b in started.bufs),
        in_specs=[_HBM] * nb + [_SEM, _SEM, _ANY],
        out_specs=tuple([_HBM] * nb),
        input_output_aliases={k: k for k in range(nb)},
        compiler_params=_params(1, has_side_effects=_EFFECT),
    )(*started.bufs, started.send, started.recv, after))


def _x1_copies(ws):
    def copies(refs, send_sems, recv_sems):
        x, y, c, _ = _mesh_pos()
        out = []
        for k, w in enumerate(ws):
            for s in range(N_SHARD):
                out.append(pltpu.make_async_remote_copy(
                    src_ref=_UNITS[w](refs[k], s, 1 - c), dst_ref=refs[len(ws) + k].at[s],
                    send_sem=send_sems.at[N_SHARD * k + s], recv_sem=recv_sems.at[N_SHARD * k + s],
                    device_id=(x, y, 1 - c), device_id_type=MESH))
        return out
    return copies


def _x2_copies(n):
    def copies(refs, send_sems, recv_sems):
        x, y, c, chips = _mesh_pos()
        out = []
        for j, (cx, cy) in enumerate(chips):
            for k in range(n):
                out.append(pltpu.make_async_remote_copy(
                    src_ref=refs[k].at[2 * cx + cy], dst_ref=refs[n + k].at[j],
                    send_sem=send_sems.at[3 * k + j], recv_sem=recv_sems.at[3 * k + j],
                    device_id=(cx, cy, c), device_id_type=MESH))
        return out
    return copies


def _x3_copies(ws):
    def copies(refs, send_sems, recv_sems):
        x, y, c, _ = _mesh_pos()
        out = []
        for k, w in enumerate(ws):
            rows = _HALF_ROWS[w]
            mine = refs[k].at[pl.ds(_mo(c * rows, rows), rows), :]
            out.append(pltpu.make_async_remote_copy(
                src_ref=mine, dst_ref=mine, send_sem=send_sems.at[k], recv_sem=recv_sems.at[k],
                device_id=(x, y, 1 - c), device_id_type=MESH))
        return out
    return copies


def _x1_lands(ws):
    return [lax.empty((N_SHARD,) + _UNIT_SHAPES[w], F32) for w in ws]


def _x2_lands(ws):
    return [lax.empty((3,) + _UNIT_SHAPES[w], BF) for w in ws]


def _grad_add1(w, g, recv, pos):
    ur, uc = _UNIT_SHAPES[w]
    if w == 3:
        g_map = lambda s, pos: (2 * s + pos[0], 0)
    else:
        g_map = lambda s, pos: (pos[0], s)

    def body(pos_ref, g_ref, r_ref, cs_ref, csb_ref):
        v = g_ref[...] + r_ref[0]
        cs_ref[0] = v
        csb_ref[0] = v.astype(BF)

    u3 = lambda s, pos: (s, 0, 0)
    return pl.pallas_call(
        body, name=f"grad_add1_{w}",
        grid_spec=pltpu.PrefetchScalarGridSpec(
            num_scalar_prefetch=1, grid=(N_SHARD,),
            in_specs=[pl.BlockSpec((ur, uc), g_map), pl.BlockSpec((1, ur, uc), u3)],
            out_specs=[pl.BlockSpec((1, ur, uc), u3), pl.BlockSpec((1, ur, uc), u3)]),
        out_shape=(jax.ShapeDtypeStruct((N_SHARD, ur, uc), F32), jax.ShapeDtypeStruct((N_SHARD, ur, uc), BF)),
        compiler_params=_params(40, dimension_semantics=("arbitrary",)),
    )(pos, g, recv)


def _grad_add1_group(ws, gs, recvs):
    n = len(ws)

    def body(*refs):
        c = lax.axis_index("c")
        for k, w in enumerate(ws):
            g, r, cs, csb = refs[k], refs[n + k], refs[2 * n + k], refs[3 * n + k]
            for s in range(N_SHARD):
                v = _UNITS[w](g, s, c)[...] + r[s]
                cs[s] = v
                csb[s] = v.astype(BF)

    vm = pl.BlockSpec(memory_space=pltpu.VMEM)
    outs = pl.pallas_call(
        body, name="grad_add1_group",
        out_shape=tuple(jax.ShapeDtypeStruct((N_SHARD,) + _UNIT_SHAPES[w], dt) for dt in (F32, BF) for w in ws),
        in_specs=[vm] * (2 * n), out_specs=[vm] * (2 * n),
        compiler_params=_params(32),
    )(*gs, *recvs)
    return list(outs[:n]), list(outs[n:])


def _grad_add2_group(ws, css, recvs):
    n = len(ws)

    def body(*refs):
        x, y, c, _ = _mesh_pos()
        for k, w in enumerate(ws):
            cs, r, o = refs[k], refs[n + k], refs[2 * n + k]
            rows = _HALF_ROWS[w]
            total = ((cs[2 * x + y] + r[0].astype(F32)) + r[1].astype(F32)) + r[2].astype(F32)
            o[pl.ds(_mo(c * rows, rows), rows), :] = total

    vm = pl.BlockSpec(memory_space=pltpu.VMEM)
    return list(pl.pallas_call(
        body, name="grad_add2_group",
        out_shape=tuple(jax.ShapeDtypeStruct(_SHARD_SHAPES[w], F32) for w in ws),
        in_specs=[vm] * (2 * n), out_specs=[vm] * n,
        compiler_params=_params(32),
    )(*css, *recvs))


def _grad_add2(w, cs, recv, pos):
    ur, uc = _UNIT_SHAPES[w]
    tr = ur // 4 if w == 0 else ur
    nt = ur // tr

    def body(pos_ref, cs_ref, r_ref, o_ref):
        o_ref[...] = ((cs_ref[0] + r_ref[0].astype(F32)) + r_ref[1].astype(F32)) + r_ref[2].astype(F32)

    return pl.pallas_call(
        body, name=f"grad_add2_{w}",
        grid_spec=pltpu.PrefetchScalarGridSpec(
            num_scalar_prefetch=1, grid=(nt,),
            in_specs=[pl.BlockSpec((1, tr, uc), lambda t, pos: (pos[1], t, 0)),
                      pl.BlockSpec((3, tr, uc), lambda t, pos: (0, t, 0))],
            out_specs=pl.BlockSpec((tr, uc), lambda t, pos: (pos[0] * nt + t, 0))),
        out_shape=jax.ShapeDtypeStruct(_SHARD_SHAPES[w], F32),
        compiler_params=_params(32, dimension_semantics=("arbitrary",)),
    )(pos, cs, recv)


def _grad_xchg3(ws, halves):
    n = len(ws)

    def body(*refs):
        cps = _x3_copies(ws)(refs[:n], refs[2 * n], refs[2 * n + 1])
        for cp in cps:
            cp.start()
        for cp in cps:
            cp.wait()

    return pl.pallas_call(
        body, name="grad_xchg3",
        out_shape=tuple(jax.ShapeDtypeStruct(_SHARD_SHAPES[w], F32) for w in ws),
        in_specs=[_ANY] * n, out_specs=[_ANY] * n,
        input_output_aliases={k: k for k in range(n)},
        scratch_shapes=[pltpu.SemaphoreType.DMA((n,)), pltpu.SemaphoreType.DMA((n,))],
        compiler_params=_params(16),
    )(*halves)


def _adamw_math(w, g, m, v):
    m = ADAM_B1 * m + (1.0 - ADAM_B1) * g
    v = ADAM_B2 * v + (1.0 - ADAM_B2) * (g * g)
    m_hat = m / ADAM_C1
    v_hat = v / ADAM_C2
    delta = -ADAM_LR * (m_hat / (jnp.sqrt(v_hat) + ADAM_EPS) + ADAM_WD * w)
    return delta, m, v


def _adamw_group(ws_, gs, ms, vs, after=()):
    n = len(ws_)

    def body(*refs):
        for k in range(n):
            w, g, m, v = (refs[j * n + k] for j in range(4))
            d, nm, nv, gc = (refs[(4 + j) * n + k] for j in range(4))
            gv = g[...]
            d[...], nm[...], nv[...] = _adamw_math(w[...], gv, m[...], v[...])
            gc[...] = gv

    vm = pl.BlockSpec(memory_space=pltpu.VMEM)
    outs = pl.pallas_call(
        _after(body, 4 * n, after), name="adamw_group",
        out_shape=tuple(jax.ShapeDtypeStruct(a.shape, F32) for _ in range(4) for a in ws_),
        in_specs=[vm] * (4 * n) + [_ANY] * len(after), out_specs=[vm] * (4 * n),
        compiler_params=_params(32),
    )(*ws_, *gs, *ms, *vs, *after)
    return [tuple(outs[j * n + k] for j in range(4)) for k in range(n)]


def _adamw(name, w, g, m, v, tr=256, after=()):
    rows, cols = w.shape

    def body(w_ref, g_ref, m_ref, v_ref, d_ref, nm_ref, nv_ref, gc_ref):
        gv = g_ref[...]
        d_ref[...], nm_ref[...], nv_ref[...] = _adamw_math(w_ref[...], gv, m_ref[...], v_ref[...])
        gc_ref[...] = gv

    spec = pl.BlockSpec((tr, cols), lambda i: (i, 0))
    return pl.pallas_call(
        _after(body, 4, after), name=name, grid=(rows // tr,),
        out_shape=tuple(jax.ShapeDtypeStruct((rows, cols), F32) for _ in range(4)),
        in_specs=[spec] * 4 + [_ANY] * len(after), out_specs=[spec] * 4,
        compiler_params=_params(32, dimension_semantics=("arbitrary",)),
    )(w, g, m, v, *after)


_REL_PAD = 384
_VEC_FIELDS = (("norm_g", 0, D_MODEL), ("b_gate", 1024, 2 * D_MODEL), ("sgu_ln_g", 3072, D_B),
               ("sgu_ln_b", 3584, D_B), ("b_s", 4096, N_GROUPS * 128), ("final_g", 4608, D_MODEL))
_LOSS_OFF = 5632
_REL_OFF = 5760
_NV = _REL_OFF + N_HEADS * _REL_PAD
_N_FIELDS = len(_VEC_FIELDS) + 2


_B_S_FIELD = [f[0] for f in _VEC_FIELDS].index("b_s")


def _assemble_row(dst, fields, transposed_b_s):
    for f, (_, off, n) in enumerate(_VEC_FIELDS):
        if transposed_b_s and f == _B_S_FIELD:
            t = fields[f][...].T
            for g in range(N_GROUPS):
                dst[:, off + 128 * g:off + 128 * (g + 1)] = t[g:g + 1, :]
        else:
            dst[:, off:off + n] = fields[f][...]
    for r in range(N_HEADS):
        dst[:, _REL_OFF + _REL_PAD * r:_REL_OFF + _REL_PAD * (r + 1)] = fields[len(_VEC_FIELDS)][r:r + 1, :]


def _small_reduce(grads, loss_row, after=()):
    n_in = _N_FIELDS + 1

    def body(*refs):
        g_refs, loss_ref = refs[:_N_FIELDS], refs[_N_FIELDS]
        out_v, out_w = refs[n_in:n_in + 2]
        mine_v, gath_v, gath_w, send_sems, recv_sems = refs[n_in + 2:]
        x, y, c, chips = _mesh_pos()
        me, sibling = (x, y, c), (x, y, 1 - c)

        _assemble_row(mine_v, g_refs, True)
        mine_v[:, _LOSS_OFF:_LOSS_OFF + 128] = loss_ref[...]
        mine_w = g_refs[-1]
        my_k = 4 * x + 2 * y + c
        gath_v[my_k] = mine_v[...]
        gath_w[my_k] = mine_w[...]

        def copy(k, gath, block, to, src=None):
            dst = gath.at[4 * block[0] + 2 * block[1] + block[2]]
            return pltpu.make_async_remote_copy(
                src_ref=dst if src is None else src, dst_ref=dst,
                send_sem=send_sems.at[k], recv_sem=recv_sems.at[k], device_id=to, device_id_type=MESH)

        bufs = ((gath_v, mine_v), (gath_w, mine_w))
        first, passed = [], []
        for b, (gath, mine) in enumerate(bufs):
            first.append(copy(7 * b, gath, me, sibling, src=mine))
            first += [copy(7 * b + 1 + j, gath, me, (*chip, c), src=mine) for j, chip in enumerate(chips)]
        for cp in first:
            cp.start()
        for b, (gath, _) in enumerate(bufs):
            for j, chip in enumerate(chips):
                copy(7 * b + 1 + j, gath, (*chip, c), me).wait_recv()
                cp = copy(7 * b + 4 + j, gath, (*chip, c), sibling)
                cp.start()
                passed.append(cp)
        for b, (gath, _) in enumerate(bufs):
            copy(7 * b, gath, sibling, me).wait_recv()
            for j, chip in enumerate(chips):
                copy(7 * b + 4 + j, gath, (*chip, 1 - c), me).wait_recv()
        for cp in first + passed:
            cp.wait_send()

        tot_v, tot_w = gath_v[0], gath_w[0]
        for k in range(1, 8):
            tot_v = tot_v + gath_v[k]
            tot_w = tot_w + gath_w[k]
        out_v[...] = tot_v
        out_w[...] = tot_w

    vm = pl.BlockSpec(memory_space=pltpu.VMEM)
    return pl.pallas_call(
        _after(body, n_in, after), name="small_reduce",
        out_shape=(jax.ShapeDtypeStruct((1, _NV), F32), jax.ShapeDtypeStruct((N_GROUPS * 128, 128), F32)),
        in_specs=[vm] * n_in + [_ANY] * len(after), out_specs=[vm] * 2,
        scratch_shapes=[pltpu.VMEM((1, _NV), F32), pltpu.VMEM((8, 1, _NV), F32),
                        pltpu.VMEM((8, N_GROUPS * 128, 128), F32),
                        pltpu.SemaphoreType.DMA((14,)), pltpu.SemaphoreType.DMA((14,))],
        compiler_params=_params(32),
    )(*grads, loss_row, *after)


def _small_adamw(tot_v, tot_w, params):
    n_in = 2 + 3 * _N_FIELDS

    def body(*refs):
        tv_ref, tw_ref = refs[:2]
        p_refs = [refs[2 + k * _N_FIELDS:2 + (k + 1) * _N_FIELDS] for k in range(3)]
        outs = refs[n_in:n_in + 4 * _N_FIELDS + 1]
        wmv = refs[-1]
        for k in range(3):
            _assemble_row(wmv.at[k], p_refs[k], False)
            wmv[k, :, _LOSS_OFF:_LOSS_OFF + 128] = jnp.zeros((1, 128), F32)
        tot_v, tot_w = tv_ref[...], tw_ref[...]
        res_v = (tot_v,) + _adamw_math(wmv[0], tot_v, wmv[1], wmv[2])
        res_w = (tot_w,) + _adamw_math(p_refs[0][-1][...], tot_w, p_refs[1][-1][...], p_refs[2][-1][...])
        for kind in range(4):
            o = outs[kind * _N_FIELDS:(kind + 1) * _N_FIELDS]
            for f, (_, off, n) in enumerate(_VEC_FIELDS):
                o[f][...] = res_v[kind][:, off:off + n]
            for r in range(N_HEADS):
                o[len(_VEC_FIELDS)][r:r + 1, :] = res_v[kind][:, _REL_OFF + _REL_PAD * r:_REL_OFF + _REL_PAD * (r + 1)]
            o[-1][...] = res_w[kind]
        outs[-1][...] = tot_v[:, _LOSS_OFF:_LOSS_OFF + 128]

    field_shapes = [(1, n) for _, _, n in _VEC_FIELDS] + [(N_HEADS, _REL_PAD), (N_GROUPS * 128, 128)]
    vm = pl.BlockSpec(memory_space=pltpu.VMEM)
    operands = [tot_v, tot_w] + [a for p in params for a in p]
    assert len(operands) == n_in
    outs = pl.pallas_call(
        body, name="small_adamw",
        out_shape=tuple(jax.ShapeDtypeStruct(s, F32) for _ in range(4) for s in field_shapes)
        + (jax.ShapeDtypeStruct((1, 128), F32),),
        in_specs=[vm] * n_in, out_specs=[vm] * (4 * _N_FIELDS + 1),
        scratch_shapes=[pltpu.VMEM((3, 1, _NV), F32)],
        compiler_params=_params(32),
    )(*operands)
    return [outs[k * _N_FIELDS:(k + 1) * _N_FIELDS] for k in range(4)], outs[-1]


def _small_fields(norm_g, b_gate, ln_g, ln_b, b_s, final_g, rel_bias, w_s):
    rel = jnp.pad(rel_bias.reshape(N_HEADS, N_REL), ((0, 0), (0, _REL_PAD - N_REL)))
    return (norm_g, b_gate, ln_g, ln_b, b_s.reshape(1, N_GROUPS * 128), final_g.reshape(1, D_MODEL),
            rel, w_s.reshape(N_GROUPS * 128, 128))


def _small_outputs(fields):
    n_g, b_g, l_g, l_b, b_s, f_g, rel, w_s = fields
    return (n_g, b_g, rel[:, :N_REL].reshape(1, N_HEADS, N_REL), l_g, l_b,
            w_s.reshape(1, N_GROUPS, 128, 128), b_s.reshape(1, N_GROUPS, 128), f_g.reshape(D_MODEL))


def _bias_row(rel_bias):
    hi = rel_bias[:, N_REL - 1:N_REL]
    lo = rel_bias[:, 0:1]
    return jnp.concatenate([jnp.broadcast_to(hi, (N_HEADS, 384)), rel_bias[:, ::-1],
                            jnp.broadcast_to(lo, (N_HEADS, 191)), jnp.broadcast_to(hi, (N_HEADS, 192))], axis=1)


def kernel(x, norm_g, w_in, b_gate, rel_bias, sgu_ln_g, sgu_ln_b, w_s, b_s, w_pa, w_pb, w_out, final_g, loss_target, m_norm_g, m_w_in, m_b_gate, m_rel_bias, m_sgu_ln_g, m_sgu_ln_b, m_w_s, m_b_s, m_w_pa, m_w_pb, m_w_out, m_final_g, v_norm_g, v_w_in, v_b_gate, v_rel_bias, v_sgu_ln_g, v_sgu_ln_b, v_w_s, v_b_s, v_w_pa, v_w_pb, v_w_out, v_final_g):
    S = x.shape[1]
    xs = x.reshape(S, D_MODEL)
    tgt = loss_target.reshape(S, D_MODEL)
    big_w = (w_in[0], w_pa[0], w_pb[0], w_out[0])
    big_m = (m_w_in[0], m_w_pa[0], m_w_pb[0], m_w_out[0])
    big_v = (v_w_in[0], v_w_pa[0], v_w_pb[0], v_w_out[0])
    rel = rel_bias[0]
    ws = w_s[0]
    bst = b_s[0].T
    fg = final_g.reshape(1, D_MODEL)
    pos = jnp.stack([lax.axis_index("c"), 2 * lax.axis_index("x") + lax.axis_index("y")]).astype(jnp.int32)

    staged = _stage_weights((1, 2, 3), big_w[1:], pos)
    w_in_bf, = _ag_weights((0,), big_w[:1])
    ag_s = _split_start("ag_small_start", staged, 9, _gather_copies((1, 2, 3)), after=(w_in_bf,))

    ht, q3, k3, v3, zrest = _inproj_fwd(xs, norm_g, w_in_bf, after=(ag_s.token,))
    gp = _bias_row(rel)
    att = _attn_fwd(q3, k3, v3, gp)
    sg = _sgu_fwd(zrest, sgu_ln_g, sgu_ln_b, ws, bst)
    w_pa_bf, w_pb_bf, w_out_bf = _split_wait("ag_small_wait", ag_s, _gather_copies((1, 2, 3)), sg)
    (d_out, d_att, d_sg, dzt, gw_out, gw_pa, gw_pb, g_bgate, g_final, loss_row) = _tail(
        att, sg, zrest, xs, tgt, w_pa_bf, w_pb_bf, w_out_bf, b_gate, fg)
    ws_s, ws_i = (1, 2, 3), (0,)
    names = ("adamw_w_in", "adamw_w_pa", "adamw_w_pb", "adamw_w_out")

    x1s = _split_start("gx1s_start", [gw_pa, gw_pb, gw_out] + _x1_lands(ws_s), 12, _x1_copies(ws_s))
    dq, dk, dv, d_gp = _attn_bwd(q3, k3, v3, d_att, gp, after=(x1s.token,))
    got = _split_wait("gx1s_wait", x1s, _x1_copies(ws_s), dq)
    cs_s, csb_s = _grad_add1_group(ws_s, got[:3], got[3:])

    x2s = _split_start("gx2s_start", csb_s + _x2_lands(ws_s), 9, _x2_copies(3))
    dzs, g_ws, g_bs_t, g_lng, g_lnb = _sgu_bwd(zrest, d_sg, sgu_ln_g, sgu_ln_b, ws, bst, after=(x2s.token,))
    gw_in = _gw_in(ht, dq, dk, dv, dzt, dzs)
    got = _split_wait("gx2s_wait", x2s, _x2_copies(3), gw_in)
    halves_s = _grad_add2_group(ws_s, cs_s, got[3:])

    x3s = _split_start("gx3s_start", halves_s, 3, _x3_copies(ws_s))
    x1i = _split_start("gx1i_start", [gw_in] + _x1_lands(ws_i), 4, _x1_copies(ws_i))
    dh_args = (dq, dk, dv, dzt, dzs, w_in_bf, xs, norm_g, d_out)
    part = _dh_gradx(*dh_args, after=(x3s.token, x1i.token))
    g_shards_s = _split_wait("gx3s_wait", x3s, _x3_copies(ws_s), part[0])
    got = _split_wait("gx1i_wait", x1i, _x1_copies(ws_i), part[0])
    sum_i = _grad_add1(0, got[0], got[1], pos)

    x2i = _split_start("gx2i_start", [sum_i[1]] + _x2_lands(ws_i), 3, _x2_copies(1))
    grad_x, g_norm = _dh_gradx(*dh_args, prev=part, after=(x2i.token,))
    big = [None] * 4
    big[1:] = _adamw_group(big_w[1:], g_shards_s, big_m[1:], big_v[1:], after=(x2i.token,))

    g_rel = jnp.pad(d_gp[:, 384:384 + N_REL][:, ::-1], ((0, 0), (0, _REL_PAD - N_REL)))
    small_grads = (g_norm, g_bgate, g_lng, g_lnb, g_bs_t, g_final, g_rel, g_ws.reshape(N_GROUPS * 128, 128))
    small_params = (_small_fields(norm_g, b_gate, sgu_ln_g, sgu_ln_b, b_s, final_g, rel_bias, w_s),
                    _small_fields(m_norm_g, m_b_gate, m_sgu_ln_g, m_sgu_ln_b, m_b_s, m_final_g, m_rel_bias, m_w_s),
                    _small_fields(v_norm_g, v_b_gate, v_sgu_ln_g, v_sgu_ln_b, v_b_s, v_final_g, v_rel_bias, v_w_s))
    tot_v, tot_w = _small_reduce(small_grads, loss_row, after=(x2i.token,))
    (gsum, sdelta, sm, sv), loss_out = _small_adamw(tot_v, tot_w, small_params)

    got = _split_wait("gx2i_wait", x2i, _x2_copies(1), loss_out)
    half_i = _grad_add2(0, sum_i[0], got[1], pos)
    g_shard_i, = _grad_xchg3(ws_i, [half_i])
    big[0] = _adamw(names[0], big_w[0], g_shard_i, big_m[0], big_v[0])
    sg_out, sd_out, sm_out, sv_out = (_small_outputs(f) for f in (gsum, sdelta, sm, sv))
    loss = loss_out[0, 0]

    def assemble(small, bigs):
        n_g, b_g, r_b, l_g, l_b, w_s_, b_s_, f_g = small
        b_in, b_pa, b_pb, b_out = (b[None] for b in bigs)
        return (n_g, b_in, b_g, r_b, l_g, l_b, w_s_, b_s_, b_pa, b_pb, b_out, f_g)

    grads_out = assemble(sg_out, [b[3] for b in big])
    delta_out = assemble(sd_out, [b[0] for b in big])
    m_out = assemble(sm_out, [b[1] for b in big])
    v_out = assemble(sv_out, [b[2] for b in big])
    return (loss, grad_x.reshape(1, S, D_MODEL), *grads_out, *delta_out, *m_out, *v_out)
```

```python
import functools
import math

import jax
import jax.numpy as jnp
from jax import lax
from jax.experimental import pallas as pl
from jax.experimental.pallas import tpu as pltpu

F32 = jnp.float32
BF = jnp.bfloat16
MESH = pl.DeviceIdType.MESH

D_MODEL = 1024
D_A = 512
D_B = 512
D_IN = 5632
N_HEADS = 8
HEAD_DIM = 64
CHUNK = 64
N_PREV = 8
SGU_CHUNK = 128
N_GROUPS = 4
N_REL = 257
EPS = 1e-6
NEG_INF = -1e30
SCALE = HEAD_DIM ** -0.5

QB = 2 * CHUNK
KB = (N_PREV + 2) * CHUNK
PADK = N_PREV * CHUNK
ROLL_W = 1024
N_RING = KB // QB
KEEP = N_RING - 1

ADAM_LR = 0.001
ADAM_B1 = 0.9
ADAM_B2 = 0.999
ADAM_EPS = 1e-08
ADAM_WD = 0.01
ADAM_STEP = 10
ADAM_C1 = 1.0 - ADAM_B1 ** ADAM_STEP
ADAM_C2 = 1.0 - ADAM_B2 ** ADAM_STEP

N_SHARD = 4
SHARD_IN = D_IN // N_SHARD
MIB = 1024 * 1024


VMEM_RESERVE_MIB = 60


def _params(vmem_mib, **kw):
    assert vmem_mib <= VMEM_RESERVE_MIB
    return pltpu.CompilerParams(vmem_limit_bytes=VMEM_RESERVE_MIB * MIB, **kw)


def _sigmoid(x):
    return 1.0 / (1.0 + jnp.exp(-x))


def _silu_and_grad(x):
    s = _sigmoid(x)
    return x * s, s * (1.0 + x * (1.0 - s))


_GELU_C = math.sqrt(2.0 / math.pi)
_GELU_A = 0.044715


def _gelu_and_grad(x):
    x2 = x * x
    t = jnp.tanh(_GELU_C * (x + _GELU_A * (x2 * x)))
    cdf = 0.5 * (1.0 + t)
    grad = cdf + 0.5 * x * (1.0 - t * t) * (_GELU_C * (1.0 + 3.0 * _GELU_A * x2))
    return x * cdf, grad


def _dot(a, b):
    return jnp.dot(a, b, preferred_element_type=F32)


def _dot_nt(a, b):
    return lax.dot_general(a, b, (((1,), (1,)), ((), ())), preferred_element_type=F32)


def _dot_tn(a, b):
    return lax.dot_general(a, b, (((0,), (0,)), ((), ())), preferred_element_type=F32)


def _mo(v, m):
    return v if isinstance(v, int) else pl.multiple_of(v, m)


def _unit_in(ref, s, p):
    return ref.at[pl.ds(_mo(p * 512, 512), 512), pl.ds(_mo(s * SHARD_IN, 128), SHARD_IN)]


def _unit_p(ref, s, p):
    return ref.at[pl.ds(_mo(p * 256, 256), 256), pl.ds(_mo(s * 256, 128), 256)]


def _unit_out(ref, s, p):
    return ref.at[pl.ds(_mo(s * 256 + p * 128, 128), 128), :]


_UNITS = (_unit_in, _unit_p, _unit_p, _unit_out)
_HALF_ROWS = (512, 256, 256, 128)
_UNIT_SHAPES = ((512, SHARD_IN), (256, 256), (256, 256), (128, D_MODEL))
_FULL_SHAPES = ((D_MODEL, D_IN), (D_A, D_MODEL), (D_B, D_MODEL), (D_MODEL, D_MODEL))
_SHARD_SHAPES = ((D_MODEL, SHARD_IN), (D_A, 256), (D_B, 256), (256, D_MODEL))


def _mesh_pos():
    x, y, c = lax.axis_index("x"), lax.axis_index("y"), lax.axis_index("c")
    chips = [(1 - x, y), (x, 1 - y), (1 - x, 1 - y)]
    return x, y, c, chips


def _ag_weights(ws, shards):
    n = len(ws)

    def body(*refs):
        ins, outs, stage = refs[:n], refs[n:2 * n], refs[2 * n:3 * n]
        send_sems, recv_sems, local_sems = refs[3 * n:]
        x, y, c, chips = _mesh_pos()
        s_me = 2 * x + y
        sibling = (x, y, 1 - c)
        for k in range(n):
            stage[k][...] = ins[k][...].astype(BF)

        def half(k, p):
            rows = _HALF_ROWS[ws[k]]
            return stage[k].at[pl.ds(_mo(p * rows, rows), rows), :]

        def unit(k, s, p):
            return _UNITS[ws[k]](outs[k], s, p)

        local = []
        for k in range(n):
            for p in range(2):
                cp = pltpu.make_async_copy(half(k, p), unit(k, s_me, p), local_sems.at[k, p])
                cp.start()
                local.append(cp)

        def rcopy(k, i, src, dst, to):
            return pltpu.make_async_remote_copy(src_ref=src, dst_ref=dst, send_sem=send_sems.at[k, i],
                                                recv_sem=recv_sems.at[k, i], device_id=to, device_id_type=MESH)

        sends = []
        for j, (cx, cy) in enumerate(chips):
            for k in range(n):
                cp = rcopy(k, j, half(k, c), unit(k, s_me, c), (cx, cy, c))
                cp.start()
                sends.append(cp)
        for j, (cx, cy) in enumerate(chips):
            for k in range(n):
                landed = unit(k, 2 * cx + cy, c)
                rcopy(k, j, landed, landed, (cx, cy, c)).wait_recv()
                cp = rcopy(k, 3 + j, landed, landed, sibling)
                cp.start()
                sends.append(cp)
        for j, (cx, cy) in enumerate(chips):
            for k in range(n):
                other = unit(k, 2 * cx + cy, 1 - c)
                rcopy(k, 3 + j, other, other, sibling).wait_recv()
        for cp in sends:
            cp.wait_send()
        for cp in local:
            cp.wait()

    vm = pl.BlockSpec(memory_space=pltpu.VMEM)
    return pl.pallas_call(
        body, name="ag_weights",
        out_shape=tuple(jax.ShapeDtypeStruct(_FULL_SHAPES[w], BF) for w in ws),
        in_specs=[vm] * n, out_specs=[_ANY] * n,
        scratch_shapes=[pltpu.VMEM(_SHARD_SHAPES[w], BF) for w in ws]
        + [pltpu.SemaphoreType.DMA((n, 6)), pltpu.SemaphoreType.DMA((n, 6)), pltpu.SemaphoreType.DMA((n, 2))],
        compiler_params=_params(40),
    )(*shards)


def _shard_of(ref, w, s):
    if w == 0:
        return ref.at[:, pl.ds(_mo(s * SHARD_IN, 128), SHARD_IN)]
    if w == 3:
        return ref.at[pl.ds(_mo(s * 256, 256), 256), :]
    return ref.at[:, pl.ds(_mo(s * 256, 128), 256)]


def _stage_weights(ws, shards, pos):
    n = len(ws)

    def body(pos_ref, *refs):
        for k in range(n):
            refs[n + k][...] = refs[k][...].astype(BF)

    def spec(w):
        shape = _SHARD_SHAPES[w]
        if w == 3:
            return pl.BlockSpec(shape, lambda i, pos: (pos[1], 0))
        return pl.BlockSpec(shape, lambda i, pos: (0, pos[1]))

    return list(pl.pallas_call(
        body, name="stage_weights",
        grid_spec=pltpu.PrefetchScalarGridSpec(
            num_scalar_prefetch=1, grid=(1,),
            in_specs=[pl.BlockSpec(_SHARD_SHAPES[w], lambda i, pos: (0, 0)) for w in ws],
            out_specs=[spec(w) for w in ws]),
        out_shape=tuple(jax.ShapeDtypeStruct(_FULL_SHAPES[w], BF) for w in ws),
        compiler_params=_params(16, dimension_semantics=("arbitrary",)),
    )(pos, *shards))


def _gather_copies(ws):
    def copies(refs, send_sems, recv_sems):
        x, y, c, chips = _mesh_pos()
        out = []
        for j, (cx, cy) in enumerate(chips):
            for k, w in enumerate(ws):
                mine = _shard_of(refs[k], w, 2 * x + y)
                out.append(pltpu.make_async_remote_copy(
                    src_ref=mine, dst_ref=mine, send_sem=send_sems.at[3 * k + j], recv_sem=recv_sems.at[3 * k + j],
                    device_id=(cx, cy, c), device_id_type=MESH))
        return out
    return copies


def _inproj_fwd(x, norm_g, w_in_bf, tm=512, after=()):
    S = x.shape[0]

    def body(x_ref, g_ref, w_ref, ht_ref, q_ref, k_ref, v_ref, zr_ref):
        xv = x_ref[...]
        r = lax.rsqrt(jnp.mean(xv * xv, axis=-1, keepdims=True) + EPS)
        hf = (xv * r) * g_ref[...]
        ht_ref[...] = hf.T.astype(BF)
        h = hf.astype(BF)
        heads = (q_ref, k_ref, v_ref)
        for j in range(D_IN // 512):
            z = _dot(h, w_ref[:, j * 512:(j + 1) * 512])
            if j < 3:
                zb = z.astype(BF)
                for hd in range(N_HEADS):
                    heads[j][hd] = zb[:, hd * HEAD_DIM:(hd + 1) * HEAD_DIM]
            else:
                zr_ref[:, (j - 3) * 512:(j - 2) * 512] = z

    head_major = jax.ShapeDtypeStruct((N_HEADS, S, HEAD_DIM), BF)
    head_spec = pl.BlockSpec((N_HEADS, tm, HEAD_DIM), lambda i: (0, i, 0))
    return pl.pallas_call(
        _after(body, 3, after), name="inproj_fwd", grid=(S // tm,),
        out_shape=(jax.ShapeDtypeStruct((D_MODEL, S), BF), head_major, head_major, head_major,
                   jax.ShapeDtypeStruct((S, D_IN - 3 * D_A), F32)),
        in_specs=[pl.BlockSpec((tm, D_MODEL), lambda i: (i, 0)),
                  pl.BlockSpec((1, D_MODEL), lambda i: (0, 0)),
                  pl.BlockSpec((D_MODEL, D_IN), lambda i: (0, 0), pipeline_mode=pl.Buffered(1))]
        + [_ANY] * len(after),
        out_specs=[pl.BlockSpec((D_MODEL, tm), lambda i: (0, i)),
                   head_spec, head_spec, head_spec,
                   pl.BlockSpec((tm, D_IN - 3 * D_A), lambda i: (i, 0))],
        compiler_params=_params(52, dimension_semantics=("arbitrary",)),
    )(x, norm_g, w_in_bf, *after)


def _skew_table(gp_row):
    row = lax.broadcasted_iota(jnp.int32, (QB, ROLL_W), 0)
    t = jnp.broadcast_to(gp_row, (QB, ROLL_W))
    for b in range(7):
        t = jnp.where(((row >> b) & 1) == 1, pltpu.roll(t, 1 << b, axis=1), t)
    return t


def _unskew_sum(d):
    row = lax.broadcasted_iota(jnp.int32, (QB, ROLL_W), 0)
    for b in range(7):
        d = jnp.where(((row >> b) & 1) == 1, pltpu.roll(d, ROLL_W - (1 << b), axis=1), d)
    return jnp.sum(d, axis=0, keepdims=True)


def _struct_mask():
    a = lax.broadcasted_iota(jnp.int32, (QB, KB), 0) // CHUNK
    b = lax.broadcasted_iota(jnp.int32, (QB, KB), 1) // CHUNK
    return (b >= a) & (b <= a + N_PREV)


def _load_kv(k_hbm, v_hbm, k_scr, v_scr, sems, S, meanwhile=lambda: None):
    zeros = jnp.zeros((N_HEADS, PADK, HEAD_DIM), BF)
    k_scr[:, 0:PADK, :] = zeros
    v_scr[:, 0:PADK, :] = zeros
    ck = pltpu.make_async_copy(k_hbm, k_scr.at[:, pl.ds(PADK, S), :], sems.at[0])
    cv = pltpu.make_async_copy(v_hbm, v_scr.at[:, pl.ds(PADK, S), :], sems.at[1])
    ck.start()
    cv.start()
    meanwhile()
    ck.wait()
    cv.wait()


_BATCH_NT = (((2,), (2,)), ((0,), (0,)))
_BATCH_NN = (((2,), (1,)), ((0,), (0,)))
_BATCH_TN = (((1,), (1,)), ((0,), (0,)))


def _bdot(a, b, dims):
    return lax.dot_general(a, b, dims, preferred_element_type=F32)


def _scaled(q):
    return q * jnp.asarray(SCALE, BF)


def _scores(qs, kb, bias, i, front):
    s = _bdot(qs, kb, _BATCH_NT) + bias
    if front:
        col = lax.broadcasted_iota(jnp.int32, (1, 1, KB), 2)
        s = jnp.where(col >= PADK - i * QB, s, NEG_INF)
    return s


def _attn_fwd(q3, k3, v3, gp):
    S = q3.shape[1]

    def body(q_ref, k_hbm, v_hbm, gp_ref, o_ref, lse_ref, bias_ref, k_scr, v_scr, sems):
        i = pl.program_id(0)

        @pl.when(i == 0)
        def _():
            def build_bias():
                keep = _struct_mask()
                for h in range(N_HEADS):
                    bias_ref[h] = jnp.where(keep, _skew_table(gp_ref[h:h + 1, :])[:, :KB], NEG_INF)
            _load_kv(k_hbm, v_hbm, k_scr, v_scr, sems, S, build_bias)

        def step(front):
            start = pl.multiple_of(i * QB, QB)
            kb = k_scr[:, pl.ds(start, KB), :]
            vb = v_scr[:, pl.ds(start, KB), :]
            s = _scores(_scaled(q_ref[...]), kb, bias_ref[...], i, front)
            m = jnp.max(s, axis=-1, keepdims=True)
            e = jnp.exp(s - m)
            l = jnp.sum(e, axis=-1, keepdims=True)
            p = e * (1.0 / l)
            o = _bdot(p.astype(BF), vb, _BATCH_NN)
            lse_ref[...] = jnp.broadcast_to(m + jnp.log(l), (N_HEADS, QB, 128))
            for h in range(N_HEADS):
                o_ref[:, h * HEAD_DIM:(h + 1) * HEAD_DIM] = o[h]

        pl.when(i < KEEP)(functools.partial(step, True))
        pl.when(i >= KEEP)(functools.partial(step, False))

    kv_scr = pltpu.VMEM((N_HEADS, S + PADK, HEAD_DIM), BF)
    return pl.pallas_call(
        body, name="attn_fwd", grid=(S // QB,),
        out_shape=(jax.ShapeDtypeStruct((S, D_A), F32), jax.ShapeDtypeStruct((N_HEADS, S, 128), F32),
                   jax.ShapeDtypeStruct((N_HEADS, QB, KB), F32)),
        in_specs=[pl.BlockSpec((N_HEADS, QB, HEAD_DIM), lambda i: (0, i, 0)),
                  pl.BlockSpec(memory_space=pl.ANY), pl.BlockSpec(memory_space=pl.ANY),
                  pl.BlockSpec((N_HEADS, ROLL_W), lambda i: (0, 0))],
        out_specs=[pl.BlockSpec((QB, D_A), lambda i: (i, 0)),
                   pl.BlockSpec((N_HEADS, QB, 128), lambda i: (0, i, 0)),
                   pl.BlockSpec((N_HEADS, QB, KB), lambda i: (0, 0, 0))],
        scratch_shapes=[kv_scr, kv_scr, pltpu.SemaphoreType.DMA((2,))],
        compiler_params=_params(48, dimension_semantics=("arbitrary",)),
    )(q3, k3, v3, gp)


def _attn_bwd(q3, k3, v3, d_att3, lse, bias, after=()):
    S = q3.shape[1]
    nq = S // QB

    def body(q_ref, do_ref, k_hbm, v_hbm, lse_ref, bias_ref, dq_ref, dk_ref, dv_ref, dgp_ref,
             k_scr, v_scr, dk_acc, dv_acc, dbias_acc, pad_scr, sems):
        i = pl.program_id(0)

        @pl.when(i == 0)
        def _():
            def clear():
                dk_acc[...] = jnp.zeros_like(dk_acc)
                dv_acc[...] = jnp.zeros_like(dv_acc)
                dbias_acc[...] = jnp.zeros_like(dbias_acc)
            _load_kv(k_hbm, v_hbm, k_scr, v_scr, sems, S, clear)

        def step(front):
            start = pl.multiple_of(i * QB, QB)
            kb = k_scr[:, pl.ds(start, KB), :]
            vb = v_scr[:, pl.ds(start, KB), :]
            qs = _scaled(q_ref[...])
            do = do_ref[...]
            p = jnp.exp(_scores(qs, kb, bias_ref[...], i, front) - jnp.tile(lse_ref[...], (1, 1, KB // 128)))
            dp = _bdot(do, vb, _BATCH_NT)
            ds = p * (dp - jnp.sum(dp * p, axis=-1, keepdims=True))
            dbias_acc[...] += ds
            dsb = ds.astype(BF)
            dq = _bdot(dsb, kb, _BATCH_NN) * SCALE
            for h in range(N_HEADS):
                dq_ref[:, h * HEAD_DIM:(h + 1) * HEAD_DIM] = dq[h].astype(BF)
            dk_acc[...] += _bdot(dsb, qs, _BATCH_TN)
            dv_acc[...] += _bdot(p.astype(BF), do, _BATCH_TN)

        pl.when(i < KEEP)(functools.partial(step, True))
        pl.when((i >= KEEP) & (i < nq))(functools.partial(step, False))

        for h in range(N_HEADS):
            hs = slice(h * HEAD_DIM, (h + 1) * HEAD_DIM)
            dk_ref[:, hs] = dk_acc[h, 0:QB, :].astype(BF)
            dv_ref[:, hs] = dv_acc[h, 0:QB, :].astype(BF)
        dk_acc[:, 0:KB - QB, :] = dk_acc[:, QB:KB, :]
        dv_acc[:, 0:KB - QB, :] = dv_acc[:, QB:KB, :]
        dk_acc[:, KB - QB:KB, :] = jnp.zeros((N_HEADS, QB, HEAD_DIM), F32)
        dv_acc[:, KB - QB:KB, :] = jnp.zeros((N_HEADS, QB, HEAD_DIM), F32)

        @pl.when(i == nq + KEEP - 1)
        def _():
            lane = lax.broadcasted_iota(jnp.int32, (1, ROLL_W), 1)
            hi = (lane < 384) | (lane >= 832)
            lo = (lane > 640) & (lane < 832)
            pad_scr[...] = jnp.zeros_like(pad_scr)
            for h in range(N_HEADS):
                pad_scr[:, 0:KB] = dbias_acc[h]
                g = _unskew_sum(pad_scr[...])
                s_hi = jnp.sum(jnp.where(hi, g, 0.0), axis=-1, keepdims=True)
                s_lo = jnp.sum(jnp.where(lo, g, 0.0), axis=-1, keepdims=True)
                g = jnp.where(lane == 384, g + s_hi, g)
                g = jnp.where(lane == 640, g + s_lo, g)
                dgp_ref[h:h + 1, :] = g

    last = nq - 1
    kv_scr = pltpu.VMEM((N_HEADS, S + PADK, HEAD_DIM), BF)
    return pl.pallas_call(
        _after(body, 6, after), name="attn_bwd", grid=(nq + KEEP,),
        out_shape=(jax.ShapeDtypeStruct((S, D_A), BF), jax.ShapeDtypeStruct((S, D_A), BF),
                   jax.ShapeDtypeStruct((S, D_A), BF), jax.ShapeDtypeStruct((N_HEADS, ROLL_W), F32)),
        in_specs=[pl.BlockSpec((N_HEADS, QB, HEAD_DIM), lambda i: (0, jnp.minimum(i, last), 0)),
                  pl.BlockSpec((N_HEADS, QB, HEAD_DIM), lambda i: (0, jnp.minimum(i, last), 0)),
                  pl.BlockSpec(memory_space=pl.ANY), pl.BlockSpec(memory_space=pl.ANY),
                  pl.BlockSpec((N_HEADS, QB, 128), lambda i: (0, jnp.minimum(i, last), 0)),
                  pl.BlockSpec((N_HEADS, QB, KB), lambda i: (0, 0, 0))] + [_ANY] * len(after),
        out_specs=[pl.BlockSpec((QB, D_A), lambda i: (jnp.minimum(i, last), 0)),
                   pl.BlockSpec((QB, D_A), lambda i: (jnp.maximum(i - KEEP, 0), 0)),
                   pl.BlockSpec((QB, D_A), lambda i: (jnp.maximum(i - KEEP, 0), 0)),
                   pl.BlockSpec((N_HEADS, ROLL_W), lambda i: (0, 0))],
        scratch_shapes=[kv_scr, kv_scr,
                        pltpu.VMEM((N_HEADS, KB, HEAD_DIM), F32), pltpu.VMEM((N_HEADS, KB, HEAD_DIM), F32),
                        pltpu.VMEM((N_HEADS, QB, KB), F32), pltpu.VMEM((QB, ROLL_W), F32),
                        pltpu.SemaphoreType.DMA((2,))],
        compiler_params=_params(56, dimension_semantics=("arbitrary",)),
    )(q3, d_att3, k3, v3, lse, bias, *after)


def _sgu_core(ub, vb, lg, lb):
    u, du = _gelu_and_grad(ub)
    v, dv = _gelu_and_grad(vb)
    mu = jnp.mean(v, axis=-1, keepdims=True)
    vc = v - mu
    rstd = lax.rsqrt(jnp.mean(vc * vc, axis=-1, keepdims=True) + EPS)
    xh = vc * rstd
    vn = xh * lg + lb
    return u, du, dv, rstd, xh, vn


def _tri():
    r = lax.broadcasted_iota(jnp.int32, (SGU_CHUNK, SGU_CHUNK), 0)
    c = lax.broadcasted_iota(jnp.int32, (SGU_CHUNK, SGU_CHUNK), 1)
    return r >= c


def _sgu_fwd(zrest, ln_g, ln_b, w_s, b_s_t, tm=512):
    S = zrest.shape[0]

    def body(ub_ref, vb_ref, lg_ref, lb_ref, ws_ref, bst_ref, sg_ref):
        u, _, _, _, _, vn = _sgu_core(ub_ref[...], vb_ref[...], lg_ref[...], lb_ref[...])
        vnb = vn.astype(BF)
        tri = _tri()
        for g in range(N_GROUPS):
            cs = slice(g * 128, (g + 1) * 128)
            wt = jnp.where(tri, ws_ref[g], 0.0).astype(BF)
            bcol = bst_ref[:, g:g + 1]
            for n in range(tm // SGU_CHUNK):
                rs = slice(n * SGU_CHUNK, (n + 1) * SGU_CHUNK)
                mixed = _dot(wt, vnb[rs, cs]) + bcol
                sg_ref[rs, cs] = u[rs, cs] * mixed

    return pl.pallas_call(
        body, name="sgu_fwd", grid=(S // tm,),
        out_shape=jax.ShapeDtypeStruct((S, D_B), F32),
        in_specs=[pl.BlockSpec((tm, 512), lambda i: (i, 1)),
                  pl.BlockSpec((tm, 512), lambda i: (i, 2)),
                  pl.BlockSpec((1, D_B), lambda i: (0, 0)),
                  pl.BlockSpec((1, D_B), lambda i: (0, 0)),
                  pl.BlockSpec((N_GROUPS, 128, 128), lambda i: (0, 0, 0)),
                  pl.BlockSpec((128, N_GROUPS), lambda i: (0, 0))],
        out_specs=pl.BlockSpec((tm, D_B), lambda i: (i, 0)),
        compiler_params=_params(32, dimension_semantics=("arbitrary",)),
    )(zrest, zrest, ln_g, ln_b, w_s, b_s_t)


def _sgu_bwd(zrest, d_sg, ln_g, ln_b, w_s, b_s_t, tm=256, after=()):
    S = zrest.shape[0]
    nt = S // tm

    def body(ub_ref, vb_ref, dsg_ref, lg_ref, lb_ref, ws_ref, bst_ref,
             dzs_ref, gws_ref, gbs_ref, glg_ref, glb_ref, dvn_scr, bs_acc):
        i = pl.program_id(0)

        @pl.when(i == 0)
        def _():
            gws_ref[...] = jnp.zeros_like(gws_ref)
            glg_ref[...] = jnp.zeros_like(glg_ref)
            glb_ref[...] = jnp.zeros_like(glb_ref)
            bs_acc[...] = jnp.zeros_like(bs_acc)

        ub = ub_ref[...]
        u, du, dv, rstd, xh, vn = _sgu_core(ub, vb_ref[...], lg_ref[...], lb_ref[...])
        vnb = vn.astype(BF)
        dsg = dsg_ref[...]
        tri = _tri()
        for g in range(N_GROUPS):
            cs = slice(g * 128, (g + 1) * 128)
            wtf = jnp.where(tri, ws_ref[g], 0.0)
            wt = wtf.astype(BF)
            wtt = wtf.T.astype(BF)
            bcol = bst_ref[:, g:g + 1]
            for n in range(tm // SGU_CHUNK):
                rs = slice(n * SGU_CHUNK, (n + 1) * SGU_CHUNK)
                mixed = _dot(wt, vnb[rs, cs]) + bcol
                dzs_ref[rs, cs] = (dsg[rs, cs] * mixed * du[rs, cs]).astype(BF)
                dmix = dsg[rs, cs] * u[rs, cs]
                bs_acc[:, cs] += dmix
                dmb = dmix.astype(BF)
                gws_ref[g] += _dot_nt(dmb, vnb[rs, cs])
                dvn_scr[rs, cs] = _dot(wtt, dmb)
        dvn = dvn_scr[...]
        glg_ref[...] += jnp.sum(dvn * xh, axis=0, keepdims=True)
        glb_ref[...] += jnp.sum(dvn, axis=0, keepdims=True)
        dxh = dvn * lg_ref[...]
        dvv = rstd * (dxh - jnp.mean(dxh, axis=-1, keepdims=True)
                      - xh * jnp.mean(dxh * xh, axis=-1, keepdims=True))
        dzs_ref[:, D_B:2 * D_B] = (dvv * dv).astype(BF)

        @pl.when(i == nt - 1)
        def _():
            lane = lax.broadcasted_iota(jnp.int32, (SGU_CHUNK, 128), 1)
            out = jnp.zeros((SGU_CHUNK, 128), F32)
            for g in range(N_GROUPS):
                gws_ref[g] = jnp.where(tri, gws_ref[g], 0.0)
                col = jnp.sum(bs_acc[:, g * 128:(g + 1) * 128], axis=-1, keepdims=True)
                out = jnp.where(lane == g, col, out)
            gbs_ref[...] = out

    const2 = lambda i: (0, 0)
    return pl.pallas_call(
        _after(body, 7, after), name="sgu_bwd", grid=(nt,),
        out_shape=(jax.ShapeDtypeStruct((S, 2 * D_B), BF),
                   jax.ShapeDtypeStruct((N_GROUPS, 128, 128), F32),
                   jax.ShapeDtypeStruct((SGU_CHUNK, 128), F32),
                   jax.ShapeDtypeStruct((1, D_B), F32), jax.ShapeDtypeStruct((1, D_B), F32)),
        in_specs=[pl.BlockSpec((tm, 512), lambda i: (i, 1)),
                  pl.BlockSpec((tm, 512), lambda i: (i, 2)),
                  pl.BlockSpec((tm, D_B), lambda i: (i, 0)),
                  pl.BlockSpec((1, D_B), const2), pl.BlockSpec((1, D_B), const2),
                  pl.BlockSpec((N_GROUPS, 128, 128), lambda i: (0, 0, 0)),
                  pl.BlockSpec((128, N_GROUPS), const2)] + [_ANY] * len(after),
        out_specs=[pl.BlockSpec((tm, 2 * D_B), lambda i: (i, 0)),
                   pl.BlockSpec((N_GROUPS, 128, 128), lambda i: (0, 0, 0)),
                   pl.BlockSpec((SGU_CHUNK, 128), const2),
                   pl.BlockSpec((1, D_B), const2), pl.BlockSpec((1, D_B), const2)],
        scratch_shapes=[pltpu.VMEM((tm, D_B), F32), pltpu.VMEM((SGU_CHUNK, D_B), F32)],
        compiler_params=_params(32, dimension_semantics=("arbitrary",)),
    )(zrest, zrest, d_sg, ln_g, ln_b, w_s, b_s_t, *after)


def _tail(att, sg, zrest, x, target, w_pa, w_pb, w_out, b_gate, final_g, tm=256):
    S = x.shape[0]
    nt = S // tm

    def body(att_ref, sg_ref, ga_ref, gb_ref, gta_ref, gtb_ref, x_ref, t_ref,
             wpa_ref, wpb_ref, wout_ref, bg_ref, fg_ref,
             dout_ref, datt_ref, dsg_ref, dzt_ref, gwout_hbm, gwpa_hbm, gwpb_hbm,
             gbg_ref, gfg_ref, loss_ref, acc_out, acc_pa, acc_pb, sems):
        i = pl.program_id(0)

        @pl.when(i == 0)
        def _():
            acc_out[...] = jnp.zeros_like(acc_out)
            acc_pa[...] = jnp.zeros_like(acc_pa)
            acc_pb[...] = jnp.zeros_like(acc_pb)
            gbg_ref[...] = jnp.zeros_like(gbg_ref)
            gfg_ref[...] = jnp.zeros_like(gfg_ref)
            loss_ref[...] = jnp.zeros_like(loss_ref)

        att = att_ref[...]
        sg = sg_ref[...]
        sa, dsa = _silu_and_grad(ga_ref[...])
        sb, dsb = _silu_and_grad(gb_ref[...])
        ya = (att * sa).astype(BF)
        yb = (sg * sb).astype(BF)
        pa = _dot(ya, wpa_ref[...])
        pb = _dot(yb, wpb_ref[...])
        ga = _sigmoid(gta_ref[...] + bg_ref[:, 0:D_MODEL])
        gb = _sigmoid(gtb_ref[...] + bg_ref[:, D_MODEL:2 * D_MODEL])
        merged = (ga * pa + gb * pb).astype(BF)
        out = x_ref[...] + _dot(merged, wout_ref[...])
        r2 = lax.rsqrt(jnp.mean(out * out, axis=-1, keepdims=True) + EPS)
        nrm = out * r2
        fg = fg_ref[...]
        err = nrm * fg - t_ref[...]
        loss_ref[...] += 0.5 * jnp.sum(jnp.mean(err * err, axis=-1, keepdims=True))
        dy = err * (1.0 / D_MODEL)
        gfg_ref[...] += jnp.sum(dy * nrm, axis=0, keepdims=True)
        dn = dy * fg
        d_out = r2 * (dn - nrm * jnp.mean(dn * nrm, axis=-1, keepdims=True))
        dout_ref[...] = d_out
        d_outb = d_out.astype(BF)
        acc_out[...] += _dot_tn(merged, d_outb)
        dm = _dot_nt(d_outb, wout_ref[...])
        d_pa = (dm * ga).astype(BF)
        d_pb = (dm * gb).astype(BF)
        d_gta = dm * pa * (ga * (1.0 - ga))
        d_gtb = dm * pb * (gb * (1.0 - gb))
        gbg_ref[:, 0:D_MODEL] += jnp.sum(d_gta, axis=0, keepdims=True)
        gbg_ref[:, D_MODEL:2 * D_MODEL] += jnp.sum(d_gtb, axis=0, keepdims=True)
        dzt_ref[:, 2 * D_A:2 * D_A + D_MODEL] = d_gta.astype(BF)
        dzt_ref[:, 2 * D_A + D_MODEL:] = d_gtb.astype(BF)
        acc_pa[...] += _dot_tn(ya, d_pa)
        acc_pb[...] += _dot_tn(yb, d_pb)
        d_ya = _dot_nt(d_pa, wpa_ref[...])
        d_yb = _dot_nt(d_pb, wpb_ref[...])
        d_att = (d_ya * sa).astype(BF)
        for hd in range(N_HEADS):
            datt_ref[hd] = d_att[:, hd * HEAD_DIM:(hd + 1) * HEAD_DIM]
        dzt_ref[:, 0:D_A] = (d_ya * att * dsa).astype(BF)
        dsg_ref[...] = d_yb * sb
        dzt_ref[:, D_A:2 * D_A] = (d_yb * sg * dsb).astype(BF)

        @pl.when(i == nt - 1)
        def _():
            cps = [pltpu.make_async_copy(acc_out, gwout_hbm, sems.at[0]),
                   pltpu.make_async_copy(acc_pa, gwpa_hbm, sems.at[1]),
                   pltpu.make_async_copy(acc_pb, gwpb_hbm, sems.at[2])]
            for cp in cps:
                cp.start()
            for cp in cps:
                cp.wait()

    c2 = lambda i: (0, 0)
    hbm = pl.BlockSpec(memory_space=pl.ANY)
    return pl.pallas_call(
        body, name="tail", grid=(nt,),
        out_shape=(jax.ShapeDtypeStruct((S, D_MODEL), F32), jax.ShapeDtypeStruct((N_HEADS, S, HEAD_DIM), BF),
                   jax.ShapeDtypeStruct((S, D_B), F32), jax.ShapeDtypeStruct((S, 3072), BF),
                   jax.ShapeDtypeStruct((D_MODEL, D_MODEL), F32), jax.ShapeDtypeStruct((D_A, D_MODEL), F32),
                   jax.ShapeDtypeStruct((D_B, D_MODEL), F32),
                   jax.ShapeDtypeStruct((1, 2 * D_MODEL), F32), jax.ShapeDtypeStruct((1, D_MODEL), F32),
                   jax.ShapeDtypeStruct((1, 128), F32)),
        in_specs=[pl.BlockSpec((tm, D_A), lambda i: (i, 0)),
                  pl.BlockSpec((tm, D_B), lambda i: (i, 0)),
                  pl.BlockSpec((tm, 512), lambda i: (i, 0)),
                  pl.BlockSpec((tm, 512), lambda i: (i, 3)),
                  pl.BlockSpec((tm, D_MODEL), lambda i: (i, 2)),
                  pl.BlockSpec((tm, D_MODEL), lambda i: (i, 3)),
                  pl.BlockSpec((tm, D_MODEL), lambda i: (i, 0)),
                  pl.BlockSpec((tm, D_MODEL), lambda i: (i, 0)),
                  pl.BlockSpec((D_A, D_MODEL), c2), pl.BlockSpec((D_B, D_MODEL), c2),
                  pl.BlockSpec((D_MODEL, D_MODEL), c2),
                  pl.BlockSpec((1, 2 * D_MODEL), c2), pl.BlockSpec((1, D_MODEL), c2)],
        out_specs=[pl.BlockSpec((tm, D_MODEL), lambda i: (i, 0)),
                   pl.BlockSpec((N_HEADS, tm, HEAD_DIM), lambda i: (0, i, 0)),
                   pl.BlockSpec((tm, D_B), lambda i: (i, 0)),
                   pl.BlockSpec((tm, 3072), lambda i: (i, 0)),
                   hbm, hbm, hbm,
                   pl.BlockSpec((1, 2 * D_MODEL), c2), pl.BlockSpec((1, D_MODEL), c2),
                   pl.BlockSpec((1, 128), c2)],
        scratch_shapes=[pltpu.VMEM((D_MODEL, D_MODEL), F32), pltpu.VMEM((D_A, D_MODEL), F32),
                        pltpu.VMEM((D_B, D_MODEL), F32), pltpu.SemaphoreType.DMA((3,))],
        compiler_params=_params(56, dimension_semantics=("arbitrary",)),
    )(att, sg, zrest, zrest, zrest, zrest, x, target, w_pa, w_pb, w_out, b_gate, final_g)


_DZ_MAP = ((0, 0), (1, 0), (2, 0), (3, 0), (4, 0), (4, 1), (3, 1), (3, 2), (3, 3), (3, 4), (3, 5))


def _dh_gradx(dq, dk, dv, dzt, dzs, w_in_bf, x, norm_g, d_out, prev=None, tm=512, after=()):
    S = x.shape[0]
    n_first = S // tm // 4
    nt = n_first if prev is None else S // tm - n_first
    first = 0 if prev is None else n_first
    n_in = 9 if prev is None else 11

    def body(dq_ref, dk_ref, dv_ref, dzt_ref, dzs_ref, w_ref, x_ref, g_ref, dout_ref, *rest):
        gx_ref, gn_ref = rest[-2:]
        i = pl.program_id(0)

        @pl.when(i == 0)
        def _():
            gn_ref[...] = jnp.zeros_like(gn_ref) if prev is None else rest[1][...]

        pieces = (dq_ref, dk_ref, dv_ref, dzt_ref, dzs_ref)
        dh = jnp.zeros((tm, D_MODEL), F32)
        for j, (pc, blk) in enumerate(_DZ_MAP):
            dh += _dot_nt(pieces[pc][:, blk * 512:(blk + 1) * 512], w_ref[:, j * 512:(j + 1) * 512])
        xv = x_ref[...]
        r = lax.rsqrt(jnp.mean(xv * xv, axis=-1, keepdims=True) + EPS)
        nrm = xv * r
        gn_ref[...] += jnp.sum(dh * nrm, axis=0, keepdims=True)
        dn = dh * g_ref[...]
        gx_ref[...] = r * (dn - nrm * jnp.mean(dn * nrm, axis=-1, keepdims=True)) + dout_ref[...]

    row = lambda w: pl.BlockSpec((tm, w), lambda i: (i + first, 0))
    c2 = lambda i: (0, 0)
    more = [] if prev is None else [_ANY, pl.BlockSpec((1, D_MODEL), c2)]
    return pl.pallas_call(
        _after(body, n_in, after), name="dh_gradx_a" if prev is None else "dh_gradx_b", grid=(nt,),
        out_shape=(jax.ShapeDtypeStruct((S, D_MODEL), F32), jax.ShapeDtypeStruct((1, D_MODEL), F32)),
        in_specs=[row(512), row(512), row(512), row(3072), row(1024),
                  pl.BlockSpec((D_MODEL, D_IN), c2, pipeline_mode=pl.Buffered(1)), row(D_MODEL),
                  pl.BlockSpec((1, D_MODEL), c2), row(D_MODEL)]
        + more + [_ANY] * len(after),
        out_specs=[row(D_MODEL), pl.BlockSpec((1, D_MODEL), c2)],
        input_output_aliases={} if prev is None else {9: 0},
        compiler_params=_params(48, dimension_semantics=("arbitrary",)),
    )(dq, dk, dv, dzt, dzs, w_in_bf, x, norm_g, d_out, *(prev or ()), *after)


def _gw_in(ht, dq, dk, dv, dzt, dzs, tn=256, after=()):
    S = ht.shape[1]
    per = 512 // tn
    cols = tuple((pc, per * blk + h) for pc, blk in _DZ_MAP for h in range(per))

    def body(ht_ref, dq_ref, dk_ref, dv_ref, dzt_ref, dzs_ref, o_ref):
        j = pl.program_id(0)
        pieces = (dq_ref, dk_ref, dv_ref, dzt_ref, dzs_ref)
        for pc in range(5):
            hit = functools.reduce(jnp.logical_or, [j == jj for jj, (p, _) in enumerate(cols) if p == pc])

            @pl.when(hit)
            def _(pc=pc):
                o_ref[...] = _dot(ht_ref[...], pieces[pc][...])

    def piece_spec(pc):
        cur = next(blk for p, blk in cols if p == pc)
        held = []
        for p, blk in cols:
            cur = blk if p == pc else cur
            held.append(cur)

        def index_map(j):
            blk = jnp.int32(held[0])
            for jj in range(1, len(held)):
                if held[jj] != held[jj - 1]:
                    blk = jnp.where(j >= jj, jnp.int32(held[jj]), blk)
            return (0, blk)

        return pl.BlockSpec((S, tn), index_map)

    return pl.pallas_call(
        _after(body, 6, after), name="gw_in", grid=(len(cols),),
        out_shape=jax.ShapeDtypeStruct((D_MODEL, D_IN), F32),
        in_specs=[pl.BlockSpec((D_MODEL, S), lambda j: (0, 0))] + [piece_spec(pc) for pc in range(5)]
        + [_ANY] * len(after),
        out_specs=pl.BlockSpec((D_MODEL, tn), lambda j: (0, j)),
        compiler_params=_params(48, dimension_semantics=("arbitrary",)),
    )(ht, dq, dk, dv, dzt, dzs, *after)


_HBM = pl.BlockSpec(memory_space=pltpu.HBM)
_SEM = pl.BlockSpec(memory_space=pltpu.SEMAPHORE)
_ANY = pl.BlockSpec(memory_space=pl.ANY)
_EFFECT = pltpu.SideEffectType.DATAFLOW_SIDE_EFFECTING


def _in_hbm(a):
    return pltpu.with_memory_space_constraint(a, pltpu.HBM)


def _after(body, n_in, after):
    if not after:
        return body
    return lambda *refs: body(*refs[:n_in], *refs[n_in + len(after):])


class _Started:
    def __init__(self, send, recv, bufs, token):
        self.send, self.recv, self.bufs, self.token = send, recv, bufs, token


def _split_start(name, bufs, n_copies, copies, after=()):
    nb = len(bufs)

    def body(*refs):
        refs = refs[:nb] + refs[nb + len(after):]
        for cp in copies(refs[:nb], refs[nb], refs[nb + 1]):
            cp.start()
        refs[-1][...] = jnp.zeros_like(refs[-1])

    outs = pl.pallas_call(
        body, name=name,
        out_shape=(pltpu.SemaphoreType.DMA((n_copies,)), pltpu.SemaphoreType.DMA((n_copies,)),
                   *[pltpu.HBM(b.shape, b.dtype) for b in bufs], jax.ShapeDtypeStruct((8, 128), F32)),
        in_specs=[_HBM] * nb + [_ANY] * len(after),
        out_specs=(_SEM, _SEM, *[_HBM] * nb, pl.BlockSpec(memory_space=pltpu.VMEM)),
        input_output_aliases={k: 2 + k for k in range(nb)},
        compiler_params=_params(1, has_side_effects=_EFFECT),
    )(*[_in_hbm(b) for b in bufs], *after)
    return _Started(outs[0], outs[1], list(outs[2:2 + nb]), outs[-1])


def _split_wait(name, started, copies, after):
    nb = len(started.bufs)

    def body(*refs):
        for cp in copies(refs[:nb], refs[nb], refs[nb + 1]):
            cp.wait_send()
            cp.wait_recv()

    return list(pl.pallas_call(
        body, name=name,
        out_shape=tuple(pltpu.HBM(b.shape, b.dtype) for b in started.bufs),
        in_specs=[_HBM] * nb + [_SEM, _SEM, _ANY],
        out_specs=tuple([_HBM] * nb),
        input_output_aliases={k: k for k in range(nb)},
        compiler_params=_params(1, has_side_effects=_EFFECT),
    )(*started.bufs, started.send, started.recv, after))


def _x1_copies(ws):
    def copies(refs, send_sems, recv_sems):
        x, y, c, _ = _mesh_pos()
        out = []
        for k, w in enumerate(ws):
            for s in range(N_SHARD):
                out.append(pltpu.make_async_remote_copy(
                    src_ref=_UNITS[w](refs[k], s, 1 - c), dst_ref=refs[len(ws) + k].at[s],
                    send_sem=send_sems.at[N_SHARD * k + s], recv_sem=recv_sems.at[N_SHARD * k + s],
                    device_id=(x, y, 1 - c), device_id_type=MESH))
        return out
    return copies


def _x2_copies(n):
    def copies(refs, send_sems, recv_sems):
        x, y, c, chips = _mesh_pos()
        out = []
        for j, (cx, cy) in enumerate(chips):
            for k in range(n):
                out.append(pltpu.make_async_remote_copy(
                    src_ref=refs[k].at[2 * cx + cy], dst_ref=refs[n + k].at[j],
                    send_sem=send_sems.at[3 * k + j], recv_sem=recv_sems.at[3 * k + j],
                    device_id=(cx, cy, c), device_id_type=MESH))
        return out
    return copies


def _x3_copies(ws):
    def copies(refs, send_sems, recv_sems):
        x, y, c, _ = _mesh_pos()
        out = []
        for k, w in enumerate(ws):
            rows = _HALF_ROWS[w]
            mine = refs[k].at[pl.ds(_mo(c * rows, rows), rows), :]
            out.append(pltpu.make_async_remote_copy(
                src_ref=mine, dst_ref=mine, send_sem=send_sems.at[k], recv_sem=recv_sems.at[k],
                device_id=(x, y, 1 - c), device_id_type=MESH))
        return out
    return copies


def _x1_lands(ws):
    return [lax.empty((N_SHARD,) + _UNIT_SHAPES[w], F32) for w in ws]


def _x2_lands(ws):
    return [lax.empty((3,) + _UNIT_SHAPES[w], BF) for w in ws]


def _grad_add1(w, g, recv, pos):
    ur, uc = _UNIT_SHAPES[w]
    if w == 3:
        g_map = lambda s, pos: (2 * s + pos[0], 0)
    else:
        g_map = lambda s, pos: (pos[0], s)

    def body(pos_ref, g_ref, r_ref, cs_ref, csb_ref):
        v = g_ref[...] + r_ref[0]
        cs_ref[0] = v
        csb_ref[0] = v.astype(BF)

    u3 = lambda s, pos: (s, 0, 0)
    return pl.pallas_call(
        body, name=f"grad_add1_{w}",
        grid_spec=pltpu.PrefetchScalarGridSpec(
            num_scalar_prefetch=1, grid=(N_SHARD,),
            in_specs=[pl.BlockSpec((ur, uc), g_map), pl.BlockSpec((1, ur, uc), u3)],
            out_specs=[pl.BlockSpec((1, ur, uc), u3), pl.BlockSpec((1, ur, uc), u3)]),
        out_shape=(jax.ShapeDtypeStruct((N_SHARD, ur, uc), F32), jax.ShapeDtypeStruct((N_SHARD, ur, uc), BF)),
        compiler_params=_params(40, dimension_semantics=("arbitrary",)),
    )(pos, g, recv)


def _grad_add1_group(ws, gs, recvs):
    n = len(ws)

    def body(*refs):
        c = lax.axis_index("c")
        for k, w in enumerate(ws):
            g, r, cs, csb = refs[k], refs[n + k], refs[2 * n + k], refs[3 * n + k]
            for s in range(N_SHARD):
                v = _UNITS[w](g, s, c)[...] + r[s]
                cs[s] = v
                csb[s] = v.astype(BF)

    vm = pl.BlockSpec(memory_space=pltpu.VMEM)
    outs = pl.pallas_call(
        body, name="grad_add1_group",
        out_shape=tuple(jax.ShapeDtypeStruct((N_SHARD,) + _UNIT_SHAPES[w], dt) for dt in (F32, BF) for w in ws),
        in_specs=[vm] * (2 * n), out_specs=[vm] * (2 * n),
        compiler_params=_params(32),
    )(*gs, *recvs)
    return list(outs[:n]), list(outs[n:])


def _grad_add2_group(ws, css, recvs):
    n = len(ws)

    def body(*refs):
        x, y, c, _ = _mesh_pos()
        for k, w in enumerate(ws):
            cs, r, o = refs[k], refs[n + k], refs[2 * n + k]
            rows = _HALF_ROWS[w]
            total = ((cs[2 * x + y] + r[0].astype(F32)) + r[1].astype(F32)) + r[2].astype(F32)
            o[pl.ds(_mo(c * rows, rows), rows), :] = total

    vm = pl.BlockSpec(memory_space=pltpu.VMEM)
    return list(pl.pallas_call(
        body, name="grad_add2_group",
        out_shape=tuple(jax.ShapeDtypeStruct(_SHARD_SHAPES[w], F32) for w in ws),
        in_specs=[vm] * (2 * n), out_specs=[vm] * n,
        compiler_params=_params(32),
    )(*css, *recvs))


def _grad_add2(w, cs, recv, pos):
    ur, uc = _UNIT_SHAPES[w]
    tr = ur // 4 if w == 0 else ur
    nt = ur // tr

    def body(pos_ref, cs_ref, r_ref, o_ref):
        o_ref[...] = ((cs_ref[0] + r_ref[0].astype(F32)) + r_ref[1].astype(F32)) + r_ref[2].astype(F32)

    return pl.pallas_call(
        body, name=f"grad_add2_{w}",
        grid_spec=pltpu.PrefetchScalarGridSpec(
            num_scalar_prefetch=1, grid=(nt,),
            in_specs=[pl.BlockSpec((1, tr, uc), lambda t, pos: (pos[1], t, 0)),
                      pl.BlockSpec((3, tr, uc), lambda t, pos: (0, t, 0))],
            out_specs=pl.BlockSpec((tr, uc), lambda t, pos: (pos[0] * nt + t, 0))),
        out_shape=jax.ShapeDtypeStruct(_SHARD_SHAPES[w], F32),
        compiler_params=_params(32, dimension_semantics=("arbitrary",)),
    )(pos, cs, recv)


def _grad_xchg3(ws, halves):
    n = len(ws)

    def body(*refs):
        cps = _x3_copies(ws)(refs[:n], refs[2 * n], refs[2 * n + 1])
        for cp in cps:
            cp.start()
        for cp in cps:
            cp.wait()

    return pl.pallas_call(
        body, name="grad_xchg3",
        out_shape=tuple(jax.ShapeDtypeStruct(_SHARD_SHAPES[w], F32) for w in ws),
        in_specs=[_ANY] * n, out_specs=[_ANY] * n,
        input_output_aliases={k: k for k in range(n)},
        scratch_shapes=[pltpu.SemaphoreType.DMA((n,)), pltpu.SemaphoreType.DMA((n,))],
        compiler_params=_params(16),
    )(*halves)


def _adamw_math(w, g, m, v):
    m = ADAM_B1 * m + (1.0 - ADAM_B1) * g
    v = ADAM_B2 * v + (1.0 - ADAM_B2) * (g * g)
    m_hat = m / ADAM_C1
    v_hat = v / ADAM_C2
    delta = -ADAM_LR * (m_hat / (jnp.sqrt(v_hat) + ADAM_EPS) + ADAM_WD * w)
    return delta, m, v


def _adamw_group(ws_, gs, ms, vs, after=()):
    n = len(ws_)

    def body(*refs):
        for k in range(n):
            w, g, m, v = (refs[j * n + k] for j in range(4))
            d, nm, nv, gc = (refs[(4 + j) * n + k] for j in range(4))
            gv = g[...]
            d[...], nm[...], nv[...] = _adamw_math(w[...], gv, m[...], v[...])
            gc[...] = gv

    vm = pl.BlockSpec(memory_space=pltpu.VMEM)
    outs = pl.pallas_call(
        _after(body, 4 * n, after), name="adamw_group",
        out_shape=tuple(jax.ShapeDtypeStruct(a.shape, F32) for _ in range(4) for a in ws_),
        in_specs=[vm] * (4 * n) + [_ANY] * len(after), out_specs=[vm] * (4 * n),
        compiler_params=_params(32),
    )(*ws_, *gs, *ms, *vs, *after)
    return [tuple(outs[j * n + k] for j in range(4)) for k in range(n)]


def _adamw(name, w, g, m, v, tr=256, after=()):
    rows, cols = w.shape

    def body(w_ref, g_ref, m_ref, v_ref, d_ref, nm_ref, nv_ref, gc_ref):
        gv = g_ref[...]
        d_ref[...], nm_ref[...], nv_ref[...] = _adamw_math(w_ref[...], gv, m_ref[...], v_ref[...])
        gc_ref[...] = gv

    spec = pl.BlockSpec((tr, cols), lambda i: (i, 0))
    return pl.pallas_call(
        _after(body, 4, after), name=name, grid=(rows // tr,),
        out_shape=tuple(jax.ShapeDtypeStruct((rows, cols), F32) for _ in range(4)),
        in_specs=[spec] * 4 + [_ANY] * len(after), out_specs=[spec] * 4,
        compiler_params=_params(32, dimension_semantics=("arbitrary",)),
    )(w, g, m, v, *after)


_REL_PAD = 384
_VEC_FIELDS = (("norm_g", 0, D_MODEL), ("b_gate", 1024, 2 * D_MODEL), ("sgu_ln_g", 3072, D_B),
               ("sgu_ln_b", 3584, D_B), ("b_s", 4096, N_GROUPS * 128), ("final_g", 4608, D_MODEL))
_LOSS_OFF = 5632
_REL_OFF = 5760
_NV = _REL_OFF + N_HEADS * _REL_PAD
_N_FIELDS = len(_VEC_FIELDS) + 2


_B_S_FIELD = [f[0] for f in _VEC_FIELDS].index("b_s")


def _assemble_row(dst, fields, transposed_b_s):
    for f, (_, off, n) in enumerate(_VEC_FIELDS):
        if transposed_b_s and f == _B_S_FIELD:
            t = fields[f][...].T
            for g in range(N_GROUPS):
                dst[:, off + 128 * g:off + 128 * (g + 1)] = t[g:g + 1, :]
        else:
            dst[:, off:off + n] = fields[f][...]
    for r in range(N_HEADS):
        dst[:, _REL_OFF + _REL_PAD * r:_REL_OFF + _REL_PAD * (r + 1)] = fields[len(_VEC_FIELDS)][r:r + 1, :]


def _small_reduce(grads, loss_row, after=()):
    n_in = _N_FIELDS + 1

    def body(*refs):
        g_refs, loss_ref = refs[:_N_FIELDS], refs[_N_FIELDS]
        out_v, out_w = refs[n_in:n_in + 2]
        mine_v, gath_v, gath_w, send_sems, recv_sems = refs[n_in + 2:]
        x, y, c, chips = _mesh_pos()
        me, sibling = (x, y, c), (x, y, 1 - c)

        _assemble_row(mine_v, g_refs, True)
        mine_v[:, _LOSS_OFF:_LOSS_OFF + 128] = loss_ref[...]
        mine_w = g_refs[-1]
        my_k = 4 * x + 2 * y + c
        gath_v[my_k] = mine_v[...]
        gath_w[my_k] = mine_w[...]

        def copy(k, gath, block, to, src=None):
            dst = gath.at[4 * block[0] + 2 * block[1] + block[2]]
            return pltpu.make_async_remote_copy(
                src_ref=dst if src is None else src, dst_ref=dst,
                send_sem=send_sems.at[k], recv_sem=recv_sems.at[k], device_id=to, device_id_type=MESH)

        bufs = ((gath_v, mine_v), (gath_w, mine_w))
        first, passed = [], []
        for b, (gath, mine) in enumerate(bufs):
            first.append(copy(7 * b, gath, me, sibling, src=mine))
            first += [copy(7 * b + 1 + j, gath, me, (*chip, c), src=mine) for j, chip in enumerate(chips)]
        for cp in first:
            cp.start()
        for b, (gath, _) in enumerate(bufs):
            for j, chip in enumerate(chips):
                copy(7 * b + 1 + j, gath, (*chip, c), me).wait_recv()
                cp = copy(7 * b + 4 + j, gath, (*chip, c), sibling)
                cp.start()
                passed.append(cp)
        for b, (gath, _) in enumerate(bufs):
            copy(7 * b, gath, sibling, me).wait_recv()
            for j, chip in enumerate(chips):
                copy(7 * b + 4 + j, gath, (*chip, 1 - c), me).wait_recv()
        for cp in first + passed:
            cp.wait_send()

        tot_v, tot_w = gath_v[0], gath_w[0]
        for k in range(1, 8):
            tot_v = tot_v + gath_v[k]
            tot_w = tot_w + gath_w[k]
        out_v[...] = tot_v
        out_w[...] = tot_w

    vm = pl.BlockSpec(memory_space=pltpu.VMEM)
    return pl.pallas_call(
        _after(body, n_in, after), name="small_reduce",
        out_shape=(jax.ShapeDtypeStruct((1, _NV), F32), jax.ShapeDtypeStruct((N_GROUPS * 128, 128), F32)),
        in_specs=[vm] * n_in + [_ANY] * len(after), out_specs=[vm] * 2,
        scratch_shapes=[pltpu.VMEM((1, _NV), F32), pltpu.VMEM((8, 1, _NV), F32),
                        pltpu.VMEM((8, N_GROUPS * 128, 128), F32),
                        pltpu.SemaphoreType.DMA((14,)), pltpu.SemaphoreType.DMA((14,))],
        compiler_params=_params(32),
    )(*grads, loss_row, *after)


def _small_adamw(tot_v, tot_w, params):
    n_in = 2 + 3 * _N_FIELDS

    def body(*refs):
        tv_ref, tw_ref = refs[:2]
        p_refs = [refs[2 + k * _N_FIELDS:2 + (k + 1) * _N_FIELDS] for k in range(3)]
        outs = refs[n_in:n_in + 4 * _N_FIELDS + 1]
        wmv = refs[-1]
        for k in range(3):
            _assemble_row(wmv.at[k], p_refs[k], False)
            wmv[k, :, _LOSS_OFF:_LOSS_OFF + 128] = jnp.zeros((1, 128), F32)
        tot_v, tot_w = tv_ref[...], tw_ref[...]
        res_v = (tot_v,) + _adamw_math(wmv[0], tot_v, wmv[1], wmv[2])
        res_w = (tot_w,) + _adamw_math(p_refs[0][-1][...], tot_w, p_refs[1][-1][...], p_refs[2][-1][...])
        for kind in range(4):
            o = outs[kind * _N_FIELDS:(kind + 1) * _N_FIELDS]
            for f, (_, off, n) in enumerate(_VEC_FIELDS):
                o[f][...] = res_v[kind][:, off:off + n]
            for r in range(N_HEADS):
                o[len(_VEC_FIELDS)][r:r + 1, :] = res_v[kind][:, _REL_OFF + _REL_PAD * r:_REL_OFF + _REL_PAD * (r + 1)]
            o[-1][...] = res_w[kind]
        outs[-1][...] = tot_v[:, _LOSS_OFF:_LOSS_OFF + 128]

    field_shapes = [(1, n) for _, _, n in _VEC_FIELDS] + [(N_HEADS, _REL_PAD), (N_GROUPS * 128, 128)]
    vm = pl.BlockSpec(memory_space=pltpu.VMEM)
    operands = [tot_v, tot_w] + [a for p in params for a in p]
    assert len(operands) == n_in
    outs = pl.pallas_call(
        body, name="small_adamw",
        out_shape=tuple(jax.ShapeDtypeStruct(s, F32) for _ in range(4) for s in field_shapes)
        + (jax.ShapeDtypeStruct((1, 128), F32),),
        in_specs=[vm] * n_in, out_specs=[vm] * (4 * _N_FIELDS + 1),
        scratch_shapes=[pltpu.VMEM((3, 1, _NV), F32)],
        compiler_params=_params(32),
    )(*operands)
    return [outs[k * _N_FIELDS:(k + 1) * _N_FIELDS] for k in range(4)], outs[-1]


def _small_fields(norm_g, b_gate, ln_g, ln_b, b_s, final_g, rel_bias, w_s):
    rel = jnp.pad(rel_bias.reshape(N_HEADS, N_REL), ((0, 0), (0, _REL_PAD - N_REL)))
    return (norm_g, b_gate, ln_g, ln_b, b_s.reshape(1, N_GROUPS * 128), final_g.reshape(1, D_MODEL),
            rel, w_s.reshape(N_GROUPS * 128, 128))


def _small_outputs(fields):
    n_g, b_g, l_g, l_b, b_s, f_g, rel, w_s = fields
    return (n_g, b_g, rel[:, :N_REL].reshape(1, N_HEADS, N_REL), l_g, l_b,
            w_s.reshape(1, N_GROUPS, 128, 128), b_s.reshape(1, N_GROUPS, 128), f_g.reshape(D_MODEL))


def _bias_row(rel_bias):
    hi = rel_bias[:, N_REL - 1:N_REL]
    lo = rel_bias[:, 0:1]
    return jnp.concatenate([jnp.broadcast_to(hi, (N_HEADS, 384)), rel_bias[:, ::-1],
                            jnp.broadcast_to(lo, (N_HEADS, 191)), jnp.broadcast_to(hi, (N_HEADS, 192))], axis=1)


def kernel(x, norm_g, w_in, b_gate, rel_bias, sgu_ln_g, sgu_ln_b, w_s, b_s, w_pa, w_pb, w_out, final_g, loss_target, m_norm_g, m_w_in, m_b_gate, m_rel_bias, m_sgu_ln_g, m_sgu_ln_b, m_w_s, m_b_s, m_w_pa, m_w_pb, m_w_out, m_final_g, v_norm_g, v_w_in, v_b_gate, v_rel_bias, v_sgu_ln_g, v_sgu_ln_b, v_w_s, v_b_s, v_w_pa, v_w_pb, v_w_out, v_final_g):
    S = x.shape[1]
    xs = x.reshape(S, D_MODEL)
    tgt = loss_target.reshape(S, D_MODEL)
    big_w = (w_in[0], w_pa[0], w_pb[0], w_out[0])
    big_m = (m_w_in[0], m_w_pa[0], m_w_pb[0], m_w_out[0])
    big_v = (v_w_in[0], v_w_pa[0], v_w_pb[0], v_w_out[0])
    rel = rel_bias[0]
    ws = w_s[0]
    bst = b_s[0].T
    fg = final_g.reshape(1, D_MODEL)
    pos = jnp.stack([lax.axis_index("c"), 2 * lax.axis_index("x") + lax.axis_index("y")]).astype(jnp.int32)

    staged = _stage_weights((1, 2, 3), big_w[1:], pos)
    w_in_bf, = _ag_weights((0,), big_w[:1])
    ag_s = _split_start("ag_small_start", staged, 9, _gather_copies((1, 2, 3)), after=(w_in_bf,))

    ht, q3, k3, v3, zrest = _inproj_fwd(xs, norm_g, w_in_bf, after=(ag_s.token,))
    gp = _bias_row(rel)
    att, lse, band_bias = _attn_fwd(q3, k3, v3, gp)
    sg = _sgu_fwd(zrest, sgu_ln_g, sgu_ln_b, ws, bst)
    w_pa_bf, w_pb_bf, w_out_bf = _split_wait("ag_small_wait", ag_s, _gather_copies((1, 2, 3)), sg)
    (d_out, d_att, d_sg, dzt, gw_out, gw_pa, gw_pb, g_bgate, g_final, loss_row) = _tail(
        att, sg, zrest, xs, tgt, w_pa_bf, w_pb_bf, w_out_bf, b_gate, fg)
    ws_s, ws_i = (1, 2, 3), (0,)
    names = ("adamw_w_in", "adamw_w_pa", "adamw_w_pb", "adamw_w_out")

    x1s = _split_start("gx1s_start", [gw_pa, gw_pb, gw_out] + _x1_lands(ws_s), 12, _x1_copies(ws_s))
    dq, dk, dv, d_gp = _attn_bwd(q3, k3, v3, d_att, lse, band_bias, after=(x1s.token,))
    got = _split_wait("gx1s_wait", x1s, _x1_copies(ws_s), dq)
    cs_s, csb_s = _grad_add1_group(ws_s, got[:3], got[3:])

    x2s = _split_start("gx2s_start", csb_s + _x2_lands(ws_s), 9, _x2_copies(3))
    dzs, g_ws, g_bs_t, g_lng, g_lnb = _sgu_bwd(zrest, d_sg, sgu_ln_g, sgu_ln_b, ws, bst, after=(x2s.token,))
    gw_in = _gw_in(ht, dq, dk, dv, dzt, dzs)
    got = _split_wait("gx2s_wait", x2s, _x2_copies(3), gw_in)
    halves_s = _grad_add2_group(ws_s, cs_s, got[3:])

    x3s = _split_start("gx3s_start", halves_s, 3, _x3_copies(ws_s))
    x1i = _split_start("gx1i_start", [gw_in] + _x1_lands(ws_i), 4, _x1_copies(ws_i))
    dh_args = (dq, dk, dv, dzt, dzs, w_in_bf, xs, norm_g, d_out)
    part = _dh_gradx(*dh_args, after=(x3s.token, x1i.token))
    g_shards_s = _split_wait("gx3s_wait", x3s, _x3_copies(ws_s), part[0])
    got = _split_wait("gx1i_wait", x1i, _x1_copies(ws_i), part[0])
    sum_i = _grad_add1(0, got[0], got[1], pos)

    x2i = _split_start("gx2i_start", [sum_i[1]] + _x2_lands(ws_i), 3, _x2_copies(1))
    grad_x, g_norm = _dh_gradx(*dh_args, prev=part, after=(x2i.token,))
    big = [None] * 4
    big[1:] = _adamw_group(big_w[1:], g_shards_s, big_m[1:], big_v[1:], after=(x2i.token,))

    g_rel = jnp.pad(d_gp[:, 384:384 + N_REL][:, ::-1], ((0, 0), (0, _REL_PAD - N_REL)))
    small_grads = (g_norm, g_bgate, g_lng, g_lnb, g_bs_t, g_final, g_rel, g_ws.reshape(N_GROUPS * 128, 128))
    small_params = (_small_fields(norm_g, b_gate, sgu_ln_g, sgu_ln_b, b_s, final_g, rel_bias, w_s),
                    _small_fields(m_norm_g, m_b_gate, m_sgu_ln_g, m_sgu_ln_b, m_b_s, m_final_g, m_rel_bias, m_w_s),
                    _small_fields(v_norm_g, v_b_gate, v_sgu_ln_g, v_sgu_ln_b, v_b_s, v_final_g, v_rel_bias, v_w_s))
    tot_v, tot_w = _small_reduce(small_grads, loss_row, after=(x2i.token,))
    (gsum, sdelta, sm, sv), loss_out = _small_adamw(tot_v, tot_w, small_params)

    got = _split_wait("gx2i_wait", x2i, _x2_copies(1), loss_out)
    half_i = _grad_add2(0, sum_i[0], got[1], pos)
    g_shard_i, = _grad_xchg3(ws_i, [half_i])
    big[0] = _adamw(names[0], big_w[0], g_shard_i, big_m[0], big_v[0])
    sg_out, sd_out, sm_out, sv_out = (_small_outputs(f) for f in (gsum, sdelta, sm, sv))
    loss = loss_out[0, 0]

    def assemble(small, bigs):
        n_g, b_g, r_b, l_g, l_b, w_s_, b_s_, f_g = small
        b_in, b_pa, b_pb, b_out = (b[None] for b in bigs)
        return (n_g, b_in, b_g, r_b, l_g, l_b, w_s_, b_s_, b_pa, b_pb, b_out, f_g)

    grads_out = assemble(sg_out, [b[3] for b in big])
    delta_out = assemble(sd_out, [b[0] for b in big])
    m_out = assemble(sm_out, [b[1] for b in big])
    v_out = assemble(sv_out, [b[2] for b in big])
    return (loss, grad_x.reshape(1, S, D_MODEL), *grads_out, *delta_out, *m_out, *v_out)
```

```python
import functools
import math

import jax
import jax.numpy as jnp
from jax import lax
from jax.experimental import pallas as pl
from jax.experimental.pallas import tpu as pltpu

F32 = jnp.float32
BF = jnp.bfloat16
MESH = pl.DeviceIdType.MESH

D_MODEL = 1024
D_A = 512
D_B = 512
D_IN = 5632
N_HEADS = 8
HEAD_DIM = 64
CHUNK = 64
N_PREV = 8
SGU_CHUNK = 128
N_GROUPS = 4
N_REL = 257
EPS = 1e-6
NEG_INF = -1e30
SCALE = HEAD_DIM ** -0.5

QB = 2 * CHUNK
KB = (N_PREV + 2) * CHUNK
PADK = N_PREV * CHUNK
ROLL_W = 1024
N_RING = KB // QB
KEEP = N_RING - 1

ADAM_LR = 0.001
ADAM_B1 = 0.9
ADAM_B2 = 0.999
ADAM_EPS = 1e-08
ADAM_WD = 0.01
ADAM_STEP = 10
ADAM_C1 = 1.0 - ADAM_B1 ** ADAM_STEP
ADAM_C2 = 1.0 - ADAM_B2 ** ADAM_STEP

N_SHARD = 4
SHARD_IN = D_IN // N_SHARD
MIB = 1024 * 1024


VMEM_RESERVE_MIB = 60


def _params(vmem_mib, **kw):
    assert vmem_mib <= VMEM_RESERVE_MIB
    return pltpu.CompilerParams(vmem_limit_bytes=VMEM_RESERVE_MIB * MIB, **kw)


def _sigmoid(x):
    return 1.0 / (1.0 + jnp.exp(-x))


def _silu_and_grad(x):
    s = _sigmoid(x)
    return x * s, s * (1.0 + x * (1.0 - s))


_GELU_C = math.sqrt(2.0 / math.pi)
_GELU_A = 0.044715


def _gelu_and_grad(x):
    x2 = x * x
    t = jnp.tanh(_GELU_C * (x + _GELU_A * (x2 * x)))
    cdf = 0.5 * (1.0 + t)
    grad = cdf + 0.5 * x * (1.0 - t * t) * (_GELU_C * (1.0 + 3.0 * _GELU_A * x2))
    return x * cdf, grad


def _dot(a, b):
    return jnp.dot(a, b, preferred_element_type=F32)


def _dot_nt(a, b):
    return lax.dot_general(a, b, (((1,), (1,)), ((), ())), preferred_element_type=F32)


def _dot_tn(a, b):
    return lax.dot_general(a, b, (((0,), (0,)), ((), ())), preferred_element_type=F32)


def _mo(v, m):
    return v if isinstance(v, int) else pl.multiple_of(v, m)


def _unit_in(ref, s, p):
    return ref.at[pl.ds(_mo(p * 512, 512), 512), pl.ds(_mo(s * SHARD_IN, 128), SHARD_IN)]


def _unit_p(ref, s, p):
    return ref.at[pl.ds(_mo(p * 256, 256), 256), pl.ds(_mo(s * 256, 128), 256)]


def _unit_out(ref, s, p):
    return ref.at[pl.ds(_mo(s * 256 + p * 128, 128), 128), :]


_UNITS = (_unit_in, _unit_p, _unit_p, _unit_out)
_HALF_ROWS = (512, 256, 256, 128)
_UNIT_SHAPES = ((512, SHARD_IN), (256, 256), (256, 256), (128, D_MODEL))
_FULL_SHAPES = ((D_MODEL, D_IN), (D_A, D_MODEL), (D_B, D_MODEL), (D_MODEL, D_MODEL))
_SHARD_SHAPES = ((D_MODEL, SHARD_IN), (D_A, 256), (D_B, 256), (256, D_MODEL))


def _mesh_pos():
    x, y, c = lax.axis_index("x"), lax.axis_index("y"), lax.axis_index("c")
    chips = [(1 - x, y), (x, 1 - y), (1 - x, 1 - y)]
    return x, y, c, chips


def _ag_weights(ws, shards):
    n = len(ws)

    def body(*refs):
        ins, outs, stage = refs[:n], refs[n:2 * n], refs[2 * n:3 * n]
        send_sems, recv_sems, local_sems = refs[3 * n:]
        x, y, c, chips = _mesh_pos()
        s_me = 2 * x + y
        sibling = (x, y, 1 - c)
        for k in range(n):
            stage[k][...] = ins[k][...].astype(BF)

        def half(k, p):
            rows = _HALF_ROWS[ws[k]]
            return stage[k].at[pl.ds(_mo(p * rows, rows), rows), :]

        def unit(k, s, p):
            return _UNITS[ws[k]](outs[k], s, p)

        local = []
        for k in range(n):
            for p in range(2):
                cp = pltpu.make_async_copy(half(k, p), unit(k, s_me, p), local_sems.at[k, p])
                cp.start()
                local.append(cp)

        def rcopy(k, i, src, dst, to):
            return pltpu.make_async_remote_copy(src_ref=src, dst_ref=dst, send_sem=send_sems.at[k, i],
                                                recv_sem=recv_sems.at[k, i], device_id=to, device_id_type=MESH)

        sends = []
        for j, (cx, cy) in enumerate(chips):
            for k in range(n):
                cp = rcopy(k, j, half(k, c), unit(k, s_me, c), (cx, cy, c))
                cp.start()
                sends.append(cp)
        for j, (cx, cy) in enumerate(chips):
            for k in range(n):
                landed = unit(k, 2 * cx + cy, c)
                rcopy(k, j, landed, landed, (cx, cy, c)).wait_recv()
                cp = rcopy(k, 3 + j, landed, landed, sibling)
                cp.start()
                sends.append(cp)
        for j, (cx, cy) in enumerate(chips):
            for k in range(n):
                other = unit(k, 2 * cx + cy, 1 - c)
                rcopy(k, 3 + j, other, other, sibling).wait_recv()
        for cp in sends:
            cp.wait_send()
        for cp in local:
            cp.wait()

    vm = pl.BlockSpec(memory_space=pltpu.VMEM)
    return pl.pallas_call(
        body, name="ag_weights",
        out_shape=tuple(jax.ShapeDtypeStruct(_FULL_SHAPES[w], BF) for w in ws),
        in_specs=[vm] * n, out_specs=[_ANY] * n,
        scratch_shapes=[pltpu.VMEM(_SHARD_SHAPES[w], BF) for w in ws]
        + [pltpu.SemaphoreType.DMA((n, 6)), pltpu.SemaphoreType.DMA((n, 6)), pltpu.SemaphoreType.DMA((n, 2))],
        compiler_params=_params(40),
    )(*shards)


def _shard_of(ref, w, s):
    if w == 0:
        return ref.at[:, pl.ds(_mo(s * SHARD_IN, 128), SHARD_IN)]
    if w == 3:
        return ref.at[pl.ds(_mo(s * 256, 256), 256), :]
    return ref.at[:, pl.ds(_mo(s * 256, 128), 256)]


def _stage_weights(ws, shards, pos):
    n = len(ws)

    def body(pos_ref, *refs):
        for k in range(n):
            refs[n + k][...] = refs[k][...].astype(BF)

    def spec(w):
        shape = _SHARD_SHAPES[w]
        if w == 3:
            return pl.BlockSpec(shape, lambda i, pos: (pos[1], 0))
        return pl.BlockSpec(shape, lambda i, pos: (0, pos[1]))

    return list(pl.pallas_call(
        body, name="stage_weights",
        grid_spec=pltpu.PrefetchScalarGridSpec(
            num_scalar_prefetch=1, grid=(1,),
            in_specs=[pl.BlockSpec(_SHARD_SHAPES[w], lambda i, pos: (0, 0)) for w in ws],
            out_specs=[spec(w) for w in ws]),
        out_shape=tuple(jax.ShapeDtypeStruct(_FULL_SHAPES[w], BF) for w in ws),
        compiler_params=_params(16, dimension_semantics=("arbitrary",)),
    )(pos, *shards))


def _gather_copies(ws):
    def copies(refs, send_sems, recv_sems):
        x, y, c, chips = _mesh_pos()
        out = []
        for j, (cx, cy) in enumerate(chips):
            for k, w in enumerate(ws):
                mine = _shard_of(refs[k], w, 2 * x + y)
                out.append(pltpu.make_async_remote_copy(
                    src_ref=mine, dst_ref=mine, send_sem=send_sems.at[3 * k + j], recv_sem=recv_sems.at[3 * k + j],
                    device_id=(cx, cy, c), device_id_type=MESH))
        return out
    return copies


def _inproj_fwd(x, norm_g, w_in_bf, tm=512, after=()):
    S = x.shape[0]

    def body(x_ref, g_ref, w_ref, ht_ref, q_ref, k_ref, v_ref, zr_ref):
        xv = x_ref[...]
        r = lax.rsqrt(jnp.mean(xv * xv, axis=-1, keepdims=True) + EPS)
        hf = (xv * r) * g_ref[...]
        ht_ref[...] = hf.T.astype(BF)
        h = hf.astype(BF)
        heads = (q_ref, k_ref, v_ref)
        for j in range(D_IN // 512):
            z = _dot(h, w_ref[:, j * 512:(j + 1) * 512])
            if j < 3:
                zb = z.astype(BF)
                for hd in range(N_HEADS):
                    heads[j][hd] = zb[:, hd * HEAD_DIM:(hd + 1) * HEAD_DIM]
            else:
                zr_ref[:, (j - 3) * 512:(j - 2) * 512] = z

    head_major = jax.ShapeDtypeStruct((N_HEADS, S, HEAD_DIM), BF)
    head_spec = pl.BlockSpec((N_HEADS, tm, HEAD_DIM), lambda i: (0, i, 0))
    return pl.pallas_call(
        _after(body, 3, after), name="inproj_fwd", grid=(S // tm,),
        out_shape=(jax.ShapeDtypeStruct((D_MODEL, S), BF), head_major, head_major, head_major,
                   jax.ShapeDtypeStruct((S, D_IN - 3 * D_A), F32)),
        in_specs=[pl.BlockSpec((tm, D_MODEL), lambda i: (i, 0)),
                  pl.BlockSpec((1, D_MODEL), lambda i: (0, 0)),
                  pl.BlockSpec((D_MODEL, D_IN), lambda i: (0, 0), pipeline_mode=pl.Buffered(1))]
        + [_ANY] * len(after),
        out_specs=[pl.BlockSpec((D_MODEL, tm), lambda i: (0, i)),
                   head_spec, head_spec, head_spec,
                   pl.BlockSpec((tm, D_IN - 3 * D_A), lambda i: (i, 0))],
        compiler_params=_params(52, dimension_semantics=("arbitrary",)),
    )(x, norm_g, w_in_bf, *after)


def _skew_table(gp_row):
    row = lax.broadcasted_iota(jnp.int32, (QB, ROLL_W), 0)
    t = jnp.broadcast_to(gp_row, (QB, ROLL_W))
    for b in range(7):
        t = jnp.where(((row >> b) & 1) == 1, pltpu.roll(t, 1 << b, axis=1), t)
    return t


def _unskew_sum(d):
    row = lax.broadcasted_iota(jnp.int32, (QB, ROLL_W), 0)
    for b in range(7):
        d = jnp.where(((row >> b) & 1) == 1, pltpu.roll(d, ROLL_W - (1 << b), axis=1), d)
    return jnp.sum(d, axis=0, keepdims=True)


def _struct_mask():
    a = lax.broadcasted_iota(jnp.int32, (QB, KB), 0) // CHUNK
    b = lax.broadcasted_iota(jnp.int32, (QB, KB), 1) // CHUNK
    return (b >= a) & (b <= a + N_PREV)


def _load_kv(k_hbm, v_hbm, k_scr, v_scr, sems, S, meanwhile=lambda: None):
    zeros = jnp.zeros((N_HEADS, PADK, HEAD_DIM), BF)
    k_scr[:, 0:PADK, :] = zeros
    v_scr[:, 0:PADK, :] = zeros
    ck = pltpu.make_async_copy(k_hbm, k_scr.at[:, pl.ds(PADK, S), :], sems.at[0])
    cv = pltpu.make_async_copy(v_hbm, v_scr.at[:, pl.ds(PADK, S), :], sems.at[1])
    ck.start()
    cv.start()
    meanwhile()
    ck.wait()
    cv.wait()


_BATCH_NT = (((2,), (2,)), ((0,), (0,)))
_BATCH_NN = (((2,), (1,)), ((0,), (0,)))
_BATCH_TN = (((1,), (1,)), ((0,), (0,)))


def _bdot(a, b, dims):
    return lax.dot_general(a, b, dims, preferred_element_type=F32)


def _scaled(q):
    return q * jnp.asarray(SCALE, BF)


def _scores(qs, kb, bias, i, front):
    s = _bdot(qs, kb, _BATCH_NT) + bias
    if front:
        col = lax.broadcasted_iota(jnp.int32, (1, 1, KB), 2)
        s = jnp.where(col >= PADK - i * QB, s, NEG_INF)
    return s


def _attn_fwd(q3, k3, v3, gp):
    S = q3.shape[1]

    def body(q_ref, k_hbm, v_hbm, gp_ref, o_ref, lse_ref, bias_ref, k_scr, v_scr, sems):
        i = pl.program_id(0)

        @pl.when(i == 0)
        def _():
            def build_bias():
                keep = _struct_mask()
                for h in range(N_HEADS):
                    bias_ref[h] = jnp.where(keep, _skew_table(gp_ref[h:h + 1, :])[:, :KB], NEG_INF)
            _load_kv(k_hbm, v_hbm, k_scr, v_scr, sems, S, build_bias)

        def step(front):
            start = pl.multiple_of(i * QB, QB)
            kb = k_scr[:, pl.ds(start, KB), :]
            vb = v_scr[:, pl.ds(start, KB), :]
            s = _scores(_scaled(q_ref[...]), kb, bias_ref[...], i, front)
            m = jnp.max(s, axis=-1, keepdims=True)
            e = jnp.exp(s - m)
            l = jnp.sum(e, axis=-1, keepdims=True)
            p = e * (1.0 / l)
            o = _bdot(p.astype(BF), vb, _BATCH_NN)
            lse_ref[...] = jnp.broadcast_to(m + jnp.log(l), (N_HEADS, QB, 128))
            for h in range(N_HEADS):
                o_ref[:, h * HEAD_DIM:(h + 1) * HEAD_DIM] = o[h]

        pl.when(i < KEEP)(functools.partial(step, True))
        pl.when(i >= KEEP)(functools.partial(step, False))

    kv_scr = pltpu.VMEM((N_HEADS, S + PADK, HEAD_DIM), BF)
    return pl.pallas_call(
        body, name="attn_fwd", grid=(S // QB,),
        out_shape=(jax.ShapeDtypeStruct((S, D_A), F32), jax.ShapeDtypeStruct((N_HEADS, S, 128), F32),
                   jax.ShapeDtypeStruct((N_HEADS, QB, KB), F32)),
        in_specs=[pl.BlockSpec((N_HEADS, QB, HEAD_DIM), lambda i: (0, i, 0)),
                  pl.BlockSpec(memory_space=pl.ANY), pl.BlockSpec(memory_space=pl.ANY),
                  pl.BlockSpec((N_HEADS, ROLL_W), lambda i: (0, 0))],
        out_specs=[pl.BlockSpec((QB, D_A), lambda i: (i, 0)),
                   pl.BlockSpec((N_HEADS, QB, 128), lambda i: (0, i, 0)),
                   pl.BlockSpec((N_HEADS, QB, KB), lambda i: (0, 0, 0))],
        scratch_shapes=[kv_scr, kv_scr, pltpu.SemaphoreType.DMA((2,))],
        compiler_params=_params(48, dimension_semantics=("arbitrary",)),
    )(q3, k3, v3, gp)


def _attn_bwd(q3, k3, v3, d_att3, lse, bias, after=()):
    S = q3.shape[1]
    nq = S // QB

    def body(q_ref, do_ref, k_hbm, v_hbm, lse_ref, bias_ref, dq_ref, dk_ref, dv_ref, dgp_ref,
             k_scr, v_scr, dk_acc, dv_acc, dbias_acc, pad_scr, sems):
        i = pl.program_id(0)

        @pl.when(i == 0)
        def _():
            def clear():
                dk_acc[...] = jnp.zeros_like(dk_acc)
                dv_acc[...] = jnp.zeros_like(dv_acc)
                dbias_acc[...] = jnp.zeros_like(dbias_acc)
            _load_kv(k_hbm, v_hbm, k_scr, v_scr, sems, S, clear)

        def step(front):
            start = pl.multiple_of(i * QB, QB)
            kb = k_scr[:, pl.ds(start, KB), :]
            vb = v_scr[:, pl.ds(start, KB), :]
            qs = _scaled(q_ref[...])
            do = do_ref[...]
            p = jnp.exp(_scores(qs, kb, bias_ref[...], i, front) - jnp.tile(lse_ref[...], (1, 1, KB // 128)))
            dp = _bdot(do, vb, _BATCH_NT)
            ds = p * (dp - jnp.sum(dp * p, axis=-1, keepdims=True))
            dbias_acc[...] += ds
            dsb = ds.astype(BF)
            dq = _bdot(dsb, kb, _BATCH_NN) * SCALE
            for h in range(N_HEADS):
                dq_ref[:, h * HEAD_DIM:(h + 1) * HEAD_DIM] = dq[h].astype(BF)
            dk_acc[...] += _bdot(dsb, qs, _BATCH_TN)
            dv_acc[...] += _bdot(p.astype(BF), do, _BATCH_TN)

        pl.when(i < KEEP)(functools.partial(step, True))
        pl.when((i >= KEEP) & (i < nq))(functools.partial(step, False))

        for h in range(N_HEADS):
            hs = slice(h * HEAD_DIM, (h + 1) * HEAD_DIM)
            dk_ref[:, hs] = dk_acc[h, 0:QB, :].astype(BF)
            dv_ref[:, hs] = dv_acc[h, 0:QB, :].astype(BF)
        dk_acc[:, 0:KB - QB, :] = dk_acc[:, QB:KB, :]
        dv_acc[:, 0:KB - QB, :] = dv_acc[:, QB:KB, :]
        dk_acc[:, KB - QB:KB, :] = jnp.zeros((N_HEADS, QB, HEAD_DIM), F32)
        dv_acc[:, KB - QB:KB, :] = jnp.zeros((N_HEADS, QB, HEAD_DIM), F32)

        @pl.when(i == nq + KEEP - 1)
        def _():
            lane = lax.broadcasted_iota(jnp.int32, (1, ROLL_W), 1)
            hi = (lane < 384) | (lane >= 832)
            lo = (lane > 640) & (lane < 832)
            pad_scr[...] = jnp.zeros_like(pad_scr)
            for h in range(N_HEADS):
                pad_scr[:, 0:KB] = dbias_acc[h]
                g = _unskew_sum(pad_scr[...])
                s_hi = jnp.sum(jnp.where(hi, g, 0.0), axis=-1, keepdims=True)
                s_lo = jnp.sum(jnp.where(lo, g, 0.0), axis=-1, keepdims=True)
                g = jnp.where(lane == 384, g + s_hi, g)
                g = jnp.where(lane == 640, g + s_lo, g)
                dgp_ref[h:h + 1, :] = g

    last = nq - 1
    kv_scr = pltpu.VMEM((N_HEADS, S + PADK, HEAD_DIM), BF)
    return pl.pallas_call(
        _after(body, 6, after), name="attn_bwd", grid=(nq + KEEP,),
        out_shape=(jax.ShapeDtypeStruct((S, D_A), BF), jax.ShapeDtypeStruct((S, D_A), BF),
                   jax.ShapeDtypeStruct((S, D_A), BF), jax.ShapeDtypeStruct((N_HEADS, ROLL_W), F32)),
        in_specs=[pl.BlockSpec((N_HEADS, QB, HEAD_DIM), lambda i: (0, jnp.minimum(i, last), 0)),
                  pl.BlockSpec((N_HEADS, QB, HEAD_DIM), lambda i: (0, jnp.minimum(i, last), 0)),
                  pl.BlockSpec(memory_space=pl.ANY), pl.BlockSpec(memory_space=pl.ANY),
                  pl.BlockSpec((N_HEADS, QB, 128), lambda i: (0, jnp.minimum(i, last), 0)),
                  pl.BlockSpec((N_HEADS, QB, KB), lambda i: (0, 0, 0))] + [_ANY] * len(after),
        out_specs=[pl.BlockSpec((QB, D_A), lambda i: (jnp.minimum(i, last), 0)),
                   pl.BlockSpec((QB, D_A), lambda i: (jnp.maximum(i - KEEP, 0), 0)),
                   pl.BlockSpec((QB, D_A), lambda i: (jnp.maximum(i - KEEP, 0), 0)),
                   pl.BlockSpec((N_HEADS, ROLL_W), lambda i: (0, 0))],
        scratch_shapes=[kv_scr, kv_scr,
                        pltpu.VMEM((N_HEADS, KB, HEAD_DIM), F32), pltpu.VMEM((N_HEADS, KB, HEAD_DIM), F32),
                        pltpu.VMEM((N_HEADS, QB, KB), F32), pltpu.VMEM((QB, ROLL_W), F32),
                        pltpu.SemaphoreType.DMA((2,))],
        compiler_params=_params(56, dimension_semantics=("arbitrary",)),
    )(q3, d_att3, k3, v3, lse, bias, *after)


def _sgu_core(ub, vb, lg, lb):
    u, du = _gelu_and_grad(ub)
    v, dv = _gelu_and_grad(vb)
    mu = jnp.mean(v, axis=-1, keepdims=True)
    vc = v - mu
    rstd = lax.rsqrt(jnp.mean(vc * vc, axis=-1, keepdims=True) + EPS)
    xh = vc * rstd
    vn = xh * lg + lb
    return u, du, dv, rstd, xh, vn


def _tri():
    r = lax.broadcasted_iota(jnp.int32, (SGU_CHUNK, SGU_CHUNK), 0)
    c = lax.broadcasted_iota(jnp.int32, (SGU_CHUNK, SGU_CHUNK), 1)
    return r >= c


def _sgu_fwd(zrest, ln_g, ln_b, w_s, b_s_t, tm=512):
    S = zrest.shape[0]

    def body(ub_ref, vb_ref, lg_ref, lb_ref, ws_ref, bst_ref, sg_ref):
        u, _, _, _, _, vn = _sgu_core(ub_ref[...], vb_ref[...], lg_ref[...], lb_ref[...])
        vnb = vn.astype(BF)
        tri = _tri()
        for g in range(N_GROUPS):
            cs = slice(g * 128, (g + 1) * 128)
            wt = jnp.where(tri, ws_ref[g], 0.0).astype(BF)
            bcol = bst_ref[:, g:g + 1]
            for n in range(tm // SGU_CHUNK):
                rs = slice(n * SGU_CHUNK, (n + 1) * SGU_CHUNK)
                mixed = _dot(wt, vnb[rs, cs]) + bcol
                sg_ref[rs, cs] = u[rs, cs] * mixed

    return pl.pallas_call(
        body, name="sgu_fwd", grid=(S // tm,),
        out_shape=jax.ShapeDtypeStruct((S, D_B), F32),
        in_specs=[pl.BlockSpec((tm, 512), lambda i: (i, 1)),
                  pl.BlockSpec((tm, 512), lambda i: (i, 2)),
                  pl.BlockSpec((1, D_B), lambda i: (0, 0)),
                  pl.BlockSpec((1, D_B), lambda i: (0, 0)),
                  pl.BlockSpec((N_GROUPS, 128, 128), lambda i: (0, 0, 0)),
                  pl.BlockSpec((128, N_GROUPS), lambda i: (0, 0))],
        out_specs=pl.BlockSpec((tm, D_B), lambda i: (i, 0)),
        compiler_params=_params(32, dimension_semantics=("arbitrary",)),
    )(zrest, zrest, ln_g, ln_b, w_s, b_s_t)


def _sgu_bwd(zrest, d_sg, ln_g, ln_b, w_s, b_s_t, tm=256, after=()):
    S = zrest.shape[0]
    nt = S // tm

    def body(ub_ref, vb_ref, dsg_ref, lg_ref, lb_ref, ws_ref, bst_ref,
             dzs_ref, gws_ref, gbs_ref, glg_ref, glb_ref, dvn_scr, bs_acc):
        i = pl.program_id(0)

        @pl.when(i == 0)
        def _():
            gws_ref[...] = jnp.zeros_like(gws_ref)
            glg_ref[...] = jnp.zeros_like(glg_ref)
            glb_ref[...] = jnp.zeros_like(glb_ref)
            bs_acc[...] = jnp.zeros_like(bs_acc)

        ub = ub_ref[...]
        u, du, dv, rstd, xh, vn = _sgu_core(ub, vb_ref[...], lg_ref[...], lb_ref[...])
        vnb = vn.astype(BF)
        dsg = dsg_ref[...]
        tri = _tri()
        for g in range(N_GROUPS):
            cs = slice(g * 128, (g + 1) * 128)
            wtf = jnp.where(tri, ws_ref[g], 0.0)
            wt = wtf.astype(BF)
            wtt = wtf.T.astype(BF)
            bcol = bst_ref[:, g:g + 1]
            for n in range(tm // SGU_CHUNK):
                rs = slice(n * SGU_CHUNK, (n + 1) * SGU_CHUNK)
                mixed = _dot(wt, vnb[rs, cs]) + bcol
                dzs_ref[rs, cs] = (dsg[rs, cs] * mixed * du[rs, cs]).astype(BF)
                dmix = dsg[rs, cs] * u[rs, cs]
                bs_acc[:, cs] += dmix
                dmb = dmix.astype(BF)
                gws_ref[g] += _dot_nt(dmb, vnb[rs, cs])
                dvn_scr[rs, cs] = _dot(wtt, dmb)
        dvn = dvn_scr[...]
        glg_ref[...] += jnp.sum(dvn * xh, axis=0, keepdims=True)
        glb_ref[...] += jnp.sum(dvn, axis=0, keepdims=True)
        dxh = dvn * lg_ref[...]
        dvv = rstd * (dxh - jnp.mean(dxh, axis=-1, keepdims=True)
                      - xh * jnp.mean(dxh * xh, axis=-1, keepdims=True))
        dzs_ref[:, D_B:2 * D_B] = (dvv * dv).astype(BF)

        @pl.when(i == nt - 1)
        def _():
            lane = lax.broadcasted_iota(jnp.int32, (SGU_CHUNK, 128), 1)
            out = jnp.zeros((SGU_CHUNK, 128), F32)
            for g in range(N_GROUPS):
                gws_ref[g] = jnp.where(tri, gws_ref[g], 0.0)
                col = jnp.sum(bs_acc[:, g * 128:(g + 1) * 128], axis=-1, keepdims=True)
                out = jnp.where(lane == g, col, out)
            gbs_ref[...] = out

    const2 = lambda i: (0, 0)
    return pl.pallas_call(
        _after(body, 7, after), name="sgu_bwd", grid=(nt,),
        out_shape=(jax.ShapeDtypeStruct((S, 2 * D_B), BF),
                   jax.ShapeDtypeStruct((N_GROUPS, 128, 128), F32),
                   jax.ShapeDtypeStruct((SGU_CHUNK, 128), F32),
                   jax.ShapeDtypeStruct((1, D_B), F32), jax.ShapeDtypeStruct((1, D_B), F32)),
        in_specs=[pl.BlockSpec((tm, 512), lambda i: (i, 1)),
                  pl.BlockSpec((tm, 512), lambda i: (i, 2)),
                  pl.BlockSpec((tm, D_B), lambda i: (i, 0)),
                  pl.BlockSpec((1, D_B), const2), pl.BlockSpec((1, D_B), const2),
                  pl.BlockSpec((N_GROUPS, 128, 128), lambda i: (0, 0, 0)),
                  pl.BlockSpec((128, N_GROUPS), const2)] + [_ANY] * len(after),
        out_specs=[pl.BlockSpec((tm, 2 * D_B), lambda i: (i, 0)),
                   pl.BlockSpec((N_GROUPS, 128, 128), lambda i: (0, 0, 0)),
                   pl.BlockSpec((SGU_CHUNK, 128), const2),
                   pl.BlockSpec((1, D_B), const2), pl.BlockSpec((1, D_B), const2)],
        scratch_shapes=[pltpu.VMEM((tm, D_B), F32), pltpu.VMEM((SGU_CHUNK, D_B), F32)],
        compiler_params=_params(32, dimension_semantics=("arbitrary",)),
    )(zrest, zrest, d_sg, ln_g, ln_b, w_s, b_s_t, *after)


def _tail(att, sg, zrest, x, target, w_pa, w_pb, w_out, b_gate, final_g, tm=256):
    S = x.shape[0]
    nt = S // tm

    def body(att_ref, sg_ref, ga_ref, gb_ref, gta_ref, gtb_ref, x_ref, t_ref,
             wpa_ref, wpb_ref, wout_ref, bg_ref, fg_ref,
             dout_ref, datt_ref, dsg_ref, dzt_ref, gwout_hbm, gwpa_hbm, gwpb_hbm,
             gbg_ref, gfg_ref, loss_ref, acc_out, acc_pa, acc_pb, sems):
        i = pl.program_id(0)

        @pl.when(i == 0)
        def _():
            acc_out[...] = jnp.zeros_like(acc_out)
            acc_pa[...] = jnp.zeros_like(acc_pa)
            acc_pb[...] = jnp.zeros_like(acc_pb)
            gbg_ref[...] = jnp.zeros_like(gbg_ref)
            gfg_ref[...] = jnp.zeros_like(gfg_ref)
            loss_ref[...] = jnp.zeros_like(loss_ref)

        att = att_ref[...]
        sg = sg_ref[...]
        sa, dsa = _silu_and_grad(ga_ref[...])
        sb, dsb = _silu_and_grad(gb_ref[...])
        ya = (att * sa).astype(BF)
        yb = (sg * sb).astype(BF)
        pa = _dot(ya, wpa_ref[...])
        pb = _dot(yb, wpb_ref[...])
        ga = _sigmoid(gta_ref[...] + bg_ref[:, 0:D_MODEL])
        gb = _sigmoid(gtb_ref[...] + bg_ref[:, D_MODEL:2 * D_MODEL])
        merged = (ga * pa + gb * pb).astype(BF)
        out = x_ref[...] + _dot(merged, wout_ref[...])
        r2 = lax.rsqrt(jnp.mean(out * out, axis=-1, keepdims=True) + EPS)
        nrm = out * r2
        fg = fg_ref[...]
        err = nrm * fg - t_ref[...]
        loss_ref[...] += 0.5 * jnp.sum(jnp.mean(err * err, axis=-1, keepdims=True))
        dy = err * (1.0 / D_MODEL)
        gfg_ref[...] += jnp.sum(dy * nrm, axis=0, keepdims=True)
        dn = dy * fg
        d_out = r2 * (dn - nrm * jnp.mean(dn * nrm, axis=-1, keepdims=True))
        dout_ref[...] = d_out
        d_outb = d_out.astype(BF)
        acc_out[...] += _dot_tn(merged, d_outb)
        dm = _dot_nt(d_outb, wout_ref[...])
        d_pa = (dm * ga).astype(BF)
        d_pb = (dm * gb).astype(BF)
        d_gta = dm * pa * (ga * (1.0 - ga))
        d_gtb = dm * pb * (gb * (1.0 - gb))
        gbg_ref[:, 0:D_MODEL] += jnp.sum(d_gta, axis=0, keepdims=True)
        gbg_ref[:, D_MODEL:2 * D_MODEL] += jnp.sum(d_gtb, axis=0, keepdims=True)
        dzt_ref[:, 2 * D_A:2 * D_A + D_MODEL] = d_gta.astype(BF)
        dzt_ref[:, 2 * D_A + D_MODEL:] = d_gtb.astype(BF)
        acc_pa[...] += _dot_tn(ya, d_pa)
        acc_pb[...] += _dot_tn(yb, d_pb)
        d_ya = _dot_nt(d_pa, wpa_ref[...])
        d_yb = _dot_nt(d_pb, wpb_ref[...])
        d_att = (d_ya * sa).astype(BF)
        for hd in range(N_HEADS):
            datt_ref[hd] = d_att[:, hd * HEAD_DIM:(hd + 1) * HEAD_DIM]
        dzt_ref[:, 0:D_A] = (d_ya * att * dsa).astype(BF)
        dsg_ref[...] = d_yb * sb
        dzt_ref[:, D_A:2 * D_A] = (d_yb * sg * dsb).astype(BF)

        @pl.when(i == nt - 1)
        def _():
            cps = [pltpu.make_async_copy(acc_out, gwout_hbm, sems.at[0]),
                   pltpu.make_async_copy(acc_pa, gwpa_hbm, sems.at[1]),
                   pltpu.make_async_copy(acc_pb, gwpb_hbm, sems.at[2])]
            for cp in cps:
                cp.start()
            for cp in cps:
                cp.wait()

    c2 = lambda i: (0, 0)
    hbm = pl.BlockSpec(memory_space=pl.ANY)
    return pl.pallas_call(
        body, name="tail", grid=(nt,),
        out_shape=(jax.ShapeDtypeStruct((S, D_MODEL), F32), jax.ShapeDtypeStruct((N_HEADS, S, HEAD_DIM), BF),
                   jax.ShapeDtypeStruct((S, D_B), F32), jax.ShapeDtypeStruct((S, 3072), BF),
                   jax.ShapeDtypeStruct((D_MODEL, D_MODEL), F32), jax.ShapeDtypeStruct((D_A, D_MODEL), F32),
                   jax.ShapeDtypeStruct((D_B, D_MODEL), F32),
                   jax.ShapeDtypeStruct((1, 2 * D_MODEL), F32), jax.ShapeDtypeStruct((1, D_MODEL), F32),
                   jax.ShapeDtypeStruct((1, 128), F32)),
        in_specs=[pl.BlockSpec((tm, D_A), lambda i: (i, 0)),
                  pl.BlockSpec((tm, D_B), lambda i: (i, 0)),
                  pl.BlockSpec((tm, 512), lambda i: (i, 0)),
                  pl.BlockSpec((tm, 512), lambda i: (i, 3)),
                  pl.BlockSpec((tm, D_MODEL), lambda i: (i, 2)),
                  pl.BlockSpec((tm, D_MODEL), lambda i: (i, 3)),
                  pl.BlockSpec((tm, D_MODEL), lambda i: (i, 0)),
                  pl.BlockSpec((tm, D_MODEL), lambda i: (i, 0)),
                  pl.BlockSpec((D_A, D_MODEL), c2), pl.BlockSpec((D_B, D_MODEL), c2),
                  pl.BlockSpec((D_MODEL, D_MODEL), c2),
                  pl.BlockSpec((1, 2 * D_MODEL), c2), pl.BlockSpec((1, D_MODEL), c2)],
        out_specs=[pl.BlockSpec((tm, D_MODEL), lambda i: (i, 0)),
                   pl.BlockSpec((N_HEADS, tm, HEAD_DIM), lambda i: (0, i, 0)),
                   pl.BlockSpec((tm, D_B), lambda i: (i, 0)),
                   pl.BlockSpec((tm, 3072), lambda i: (i, 0)),
                   hbm, hbm, hbm,
                   pl.BlockSpec((1, 2 * D_MODEL), c2), pl.BlockSpec((1, D_MODEL), c2),
                   pl.BlockSpec((1, 128), c2)],
        scratch_shapes=[pltpu.VMEM((D_MODEL, D_MODEL), F32), pltpu.VMEM((D_A, D_MODEL), F32),
                        pltpu.VMEM((D_B, D_MODEL), F32), pltpu.SemaphoreType.DMA((3,))],
        compiler_params=_params(56, dimension_semantics=("arbitrary",)),
    )(att, sg, zrest, zrest, zrest, zrest, x, target, w_pa, w_pb, w_out, b_gate, final_g)


def _tail_sgu(att, zrest, x, target, w_pa, w_pb, w_out, b_gate, final_g, ln_g, ln_b, w_s, b_s_t, tm=256):
    S = x.shape[0]
    nt = S // tm
    chunks = tm // SGU_CHUNK

    def body(att_ref, ga_ref, ub_ref, vb_ref, gb_ref, gta_ref, gtb_ref, x_ref, t_ref,
             wpa_ref, wpb_ref, wout_ref, bg_ref, fg_ref, lg_ref, lb_ref, ws_ref, bst_ref,
             dout_ref, datt_ref, dzt_ref, dzs_ref, gwout_hbm, gwpa_hbm, gwpb_hbm,
             gbg_ref, gfg_ref, loss_ref, gws_ref, gbs_ref, glg_ref, glb_ref,
             acc_out, acc_pa, acc_pb, sg_scr, mix_scr, dvn_scr, bs_acc, sems):
        i = pl.program_id(0)

        @pl.when(i == 0)
        def _():
            for r in (acc_out, acc_pa, acc_pb, gbg_ref, gfg_ref, loss_ref, gws_ref, glg_ref, glb_ref, bs_acc):
                r[...] = jnp.zeros_like(r)

        u, du, dv, rstd, xh, vn = _sgu_core(ub_ref[...], vb_ref[...], lg_ref[...], lb_ref[...])
        vnb = vn.astype(BF)
        tri = _tri()
        blocks = [(g, slice(n * SGU_CHUNK, (n + 1) * SGU_CHUNK), slice(g * 128, (g + 1) * 128))
                  for g in range(N_GROUPS) for n in range(chunks)]
        wts = [jnp.where(tri, ws_ref[g], 0.0) for g in range(N_GROUPS)]
        for g, rs, cs in blocks:
            mixed = _dot(wts[g].astype(BF), vnb[rs, cs]) + bst_ref[:, g:g + 1]
            mix_scr[rs, cs] = mixed
            sg_scr[rs, cs] = u[rs, cs] * mixed

        att = att_ref[...]
        sg = sg_scr[...]
        sa, dsa = _silu_and_grad(ga_ref[...])
        sb, dsb = _silu_and_grad(gb_ref[...])
        ya = (att * sa).astype(BF)
        yb = (sg * sb).astype(BF)
        pa = _dot(ya, wpa_ref[...])
        pb = _dot(yb, wpb_ref[...])
        ga = _sigmoid(gta_ref[...] + bg_ref[:, 0:D_MODEL])
        gb = _sigmoid(gtb_ref[...] + bg_ref[:, D_MODEL:2 * D_MODEL])
        merged = (ga * pa + gb * pb).astype(BF)
        out = x_ref[...] + _dot(merged, wout_ref[...])
        r2 = lax.rsqrt(jnp.mean(out * out, axis=-1, keepdims=True) + EPS)
        nrm = out * r2
        fg = fg_ref[...]
        err = nrm * fg - t_ref[...]
        loss_ref[...] += 0.5 * jnp.sum(jnp.mean(err * err, axis=-1, keepdims=True))
        dy = err * (1.0 / D_MODEL)
        gfg_ref[...] += jnp.sum(dy * nrm, axis=0, keepdims=True)
        dn = dy * fg
        d_out = r2 * (dn - nrm * jnp.mean(dn * nrm, axis=-1, keepdims=True))
        dout_ref[...] = d_out
        d_outb = d_out.astype(BF)
        acc_out[...] += _dot_tn(merged, d_outb)
        dm = _dot_nt(d_outb, wout_ref[...])
        d_pa = (dm * ga).astype(BF)
        d_pb = (dm * gb).astype(BF)
        d_gta = dm * pa * (ga * (1.0 - ga))
        d_gtb = dm * pb * (gb * (1.0 - gb))
        gbg_ref[:, 0:D_MODEL] += jnp.sum(d_gta, axis=0, keepdims=True)
        gbg_ref[:, D_MODEL:2 * D_MODEL] += jnp.sum(d_gtb, axis=0, keepdims=True)
        dzt_ref[:, 2 * D_A:2 * D_A + D_MODEL] = d_gta.astype(BF)
        dzt_ref[:, 2 * D_A + D_MODEL:] = d_gtb.astype(BF)
        acc_pa[...] += _dot_tn(ya, d_pa)
        acc_pb[...] += _dot_tn(yb, d_pb)
        d_ya = _dot_nt(d_pa, wpa_ref[...])
        d_yb = _dot_nt(d_pb, wpb_ref[...])
        d_att = (d_ya * sa).astype(BF)
        for hd in range(N_HEADS):
            datt_ref[hd] = d_att[:, hd * HEAD_DIM:(hd + 1) * HEAD_DIM]
        dzt_ref[:, 0:D_A] = (d_ya * att * dsa).astype(BF)
        dzt_ref[:, D_A:2 * D_A] = (d_yb * sg * dsb).astype(BF)

        dsg = d_yb * sb
        dzs_ref[:, 0:D_B] = (dsg * mix_scr[...] * du).astype(BF)
        dmix = dsg * u
        for g, rs, cs in blocks:
            dmb = dmix[rs, cs].astype(BF)
            bs_acc[:, cs] += dmix[rs, cs]
            gws_ref[g] += _dot_nt(dmb, vnb[rs, cs])
            dvn_scr[rs, cs] = _dot(wts[g].T.astype(BF), dmb)
        dvn = dvn_scr[...]
        glg_ref[...] += jnp.sum(dvn * xh, axis=0, keepdims=True)
        glb_ref[...] += jnp.sum(dvn, axis=0, keepdims=True)
        dxh = dvn * lg_ref[...]
        dvv = rstd * (dxh - jnp.mean(dxh, axis=-1, keepdims=True)
                      - xh * jnp.mean(dxh * xh, axis=-1, keepdims=True))
        dzs_ref[:, D_B:2 * D_B] = (dvv * dv).astype(BF)

        @pl.when(i == nt - 1)
        def _():
            cps = [pltpu.make_async_copy(acc_out, gwout_hbm, sems.at[0]),
                   pltpu.make_async_copy(acc_pa, gwpa_hbm, sems.at[1]),
                   pltpu.make_async_copy(acc_pb, gwpb_hbm, sems.at[2])]
            for cp in cps:
                cp.start()
            lane = lax.broadcasted_iota(jnp.int32, (SGU_CHUNK, 128), 1)
            cols = jnp.zeros((SGU_CHUNK, 128), F32)
            for g in range(N_GROUPS):
                gws_ref[g] = jnp.where(tri, gws_ref[g], 0.0)
                col = jnp.sum(bs_acc[:, g * 128:(g + 1) * 128], axis=-1, keepdims=True)
                cols = jnp.where(lane == g, col, cols)
            gbs_ref[...] = cols
            for cp in cps:
                cp.wait()

    c2 = lambda i: (0, 0)
    c3 = lambda i: (0, 0, 0)
    zcol = lambda w, blk: pl.BlockSpec((tm, w), lambda i: (i, blk))
    row = lambda w: pl.BlockSpec((tm, w), lambda i: (i, 0))
    return pl.pallas_call(
        body, name="tail", grid=(nt,),
        out_shape=(jax.ShapeDtypeStruct((S, D_MODEL), F32), jax.ShapeDtypeStruct((N_HEADS, S, HEAD_DIM), BF),
                   jax.ShapeDtypeStruct((S, 3072), BF), jax.ShapeDtypeStruct((S, 2 * D_B), BF),
                   jax.ShapeDtypeStruct((D_MODEL, D_MODEL), F32), jax.ShapeDtypeStruct((D_A, D_MODEL), F32),
                   jax.ShapeDtypeStruct((D_B, D_MODEL), F32),
                   jax.ShapeDtypeStruct((1, 2 * D_MODEL), F32), jax.ShapeDtypeStruct((1, D_MODEL), F32),
                   jax.ShapeDtypeStruct((1, 128), F32),
                   jax.ShapeDtypeStruct((N_GROUPS, 128, 128), F32), jax.ShapeDtypeStruct((SGU_CHUNK, 128), F32),
                   jax.ShapeDtypeStruct((1, D_B), F32), jax.ShapeDtypeStruct((1, D_B), F32)),
        in_specs=[row(D_A), zcol(512, 0), zcol(512, 1), zcol(512, 2), zcol(512, 3),
                  zcol(D_MODEL, 2), zcol(D_MODEL, 3), row(D_MODEL), row(D_MODEL),
                  pl.BlockSpec((D_A, D_MODEL), c2), pl.BlockSpec((D_B, D_MODEL), c2),
                  pl.BlockSpec((D_MODEL, D_MODEL), c2),
                  pl.BlockSpec((1, 2 * D_MODEL), c2), pl.BlockSpec((1, D_MODEL), c2),
                  pl.BlockSpec((1, D_B), c2), pl.BlockSpec((1, D_B), c2),
                  pl.BlockSpec((N_GROUPS, 128, 128), c3), pl.BlockSpec((128, N_GROUPS), c2)],
        out_specs=[row(D_MODEL), pl.BlockSpec((N_HEADS, tm, HEAD_DIM), lambda i: (0, i, 0)),
                   row(3072), row(2 * D_B), _ANY, _ANY, _ANY,
                   pl.BlockSpec((1, 2 * D_MODEL), c2), pl.BlockSpec((1, D_MODEL), c2),
                   pl.BlockSpec((1, 128), c2),
                   pl.BlockSpec((N_GROUPS, 128, 128), c3), pl.BlockSpec((SGU_CHUNK, 128), c2),
                   pl.BlockSpec((1, D_B), c2), pl.BlockSpec((1, D_B), c2)],
        scratch_shapes=[pltpu.VMEM((D_MODEL, D_MODEL), F32), pltpu.VMEM((D_A, D_MODEL), F32),
                        pltpu.VMEM((D_B, D_MODEL), F32),
                        pltpu.VMEM((tm, D_B), F32), pltpu.VMEM((tm, D_B), F32), pltpu.VMEM((tm, D_B), F32),
                        pltpu.VMEM((SGU_CHUNK, D_B), F32), pltpu.SemaphoreType.DMA((3,))],
        compiler_params=_params(58, dimension_semantics=("arbitrary",)),
    )(att, zrest, zrest, zrest, zrest, zrest, zrest, x, target, w_pa, w_pb, w_out, b_gate, final_g,
      ln_g, ln_b, w_s, b_s_t)


_DZ_MAP = ((0, 0), (1, 0), (2, 0), (3, 0), (4, 0), (4, 1), (3, 1), (3, 2), (3, 3), (3, 4), (3, 5))


def _dh_gradx(dq, dk, dv, dzt, dzs, w_in_bf, x, norm_g, d_out, prev=None, tm=512, after=()):
    S = x.shape[0]
    n_first = S // tm // 4
    nt = n_first if prev is None else S // tm - n_first
    first = 0 if prev is None else n_first
    n_in = 9 if prev is None else 11

    def body(dq_ref, dk_ref, dv_ref, dzt_ref, dzs_ref, w_ref, x_ref, g_ref, dout_ref, *rest):
        gx_ref, gn_ref = rest[-2:]
        i = pl.program_id(0)

        @pl.when(i == 0)
        def _():
            gn_ref[...] = jnp.zeros_like(gn_ref) if prev is None else rest[1][...]

        pieces = (dq_ref, dk_ref, dv_ref, dzt_ref, dzs_ref)
        dh = jnp.zeros((tm, D_MODEL), F32)
        for j, (pc, blk) in enumerate(_DZ_MAP):
            dh += _dot_nt(pieces[pc][:, blk * 512:(blk + 1) * 512], w_ref[:, j * 512:(j + 1) * 512])
        xv = x_ref[...]
        r = lax.rsqrt(jnp.mean(xv * xv, axis=-1, keepdims=True) + EPS)
        nrm = xv * r
        gn_ref[...] += jnp.sum(dh * nrm, axis=0, keepdims=True)
        dn = dh * g_ref[...]
        gx_ref[...] = r * (dn - nrm * jnp.mean(dn * nrm, axis=-1, keepdims=True)) + dout_ref[...]

    row = lambda w: pl.BlockSpec((tm, w), lambda i: (i + first, 0))
    c2 = lambda i: (0, 0)
    more = [] if prev is None else [_ANY, pl.BlockSpec((1, D_MODEL), c2)]
    return pl.pallas_call(
        _after(body, n_in, after), name="dh_gradx_a" if prev is None else "dh_gradx_b", grid=(nt,),
        out_shape=(jax.ShapeDtypeStruct((S, D_MODEL), F32), jax.ShapeDtypeStruct((1, D_MODEL), F32)),
        in_specs=[row(512), row(512), row(512), row(3072), row(1024),
                  pl.BlockSpec((D_MODEL, D_IN), c2, pipeline_mode=pl.Buffered(1)), row(D_MODEL),
                  pl.BlockSpec((1, D_MODEL), c2), row(D_MODEL)]
        + more + [_ANY] * len(after),
        out_specs=[row(D_MODEL), pl.BlockSpec((1, D_MODEL), c2)],
        input_output_aliases={} if prev is None else {9: 0},
        compiler_params=_params(48, dimension_semantics=("arbitrary",)),
    )(dq, dk, dv, dzt, dzs, w_in_bf, x, norm_g, d_out, *(prev or ()), *after)


def _gw_in(ht, dq, dk, dv, dzt, dzs, tn=256, after=()):
    S = ht.shape[1]
    per = 512 // tn
    cols = tuple((pc, per * blk + h) for pc, blk in _DZ_MAP for h in range(per))

    def body(ht_ref, dq_ref, dk_ref, dv_ref, dzt_ref, dzs_ref, o_ref):
        j = pl.program_id(0)
        pieces = (dq_ref, dk_ref, dv_ref, dzt_ref, dzs_ref)
        for pc in range(5):
            hit = functools.reduce(jnp.logical_or, [j == jj for jj, (p, _) in enumerate(cols) if p == pc])

            @pl.when(hit)
            def _(pc=pc):
                o_ref[...] = _dot(ht_ref[...], pieces[pc][...])

    def piece_spec(pc):
        cur = next(blk for p, blk in cols if p == pc)
        held = []
        for p, blk in cols:
            cur = blk if p == pc else cur
            held.append(cur)

        def index_map(j):
            blk = jnp.int32(held[0])
            for jj in range(1, len(held)):
                if held[jj] != held[jj - 1]:
                    blk = jnp.where(j >= jj, jnp.int32(held[jj]), blk)
            return (0, blk)

        return pl.BlockSpec((S, tn), index_map)

    return pl.pallas_call(
        _after(body, 6, after), name="gw_in", grid=(len(cols),),
        out_shape=jax.ShapeDtypeStruct((D_MODEL, D_IN), F32),
        in_specs=[pl.BlockSpec((D_MODEL, S), lambda j: (0, 0))] + [piece_spec(pc) for pc in range(5)]
        + [_ANY] * len(after),
        out_specs=pl.BlockSpec((D_MODEL, tn), lambda j: (0, j)),
        compiler_params=_params(48, dimension_semantics=("arbitrary",)),
    )(ht, dq, dk, dv, dzt, dzs, *after)


_HBM = pl.BlockSpec(memory_space=pltpu.HBM)
_SEM = pl.BlockSpec(memory_space=pltpu.SEMAPHORE)
_ANY = pl.BlockSpec(memory_space=pl.ANY)
_EFFECT = pltpu.SideEffectType.DATAFLOW_SIDE_EFFECTING


def _in_hbm(a):
    return pltpu.with_memory_space_constraint(a, pltpu.HBM)


def _after(body, n_in, after):
    if not after:
        return body
    return lambda *refs: body(*refs[:n_in], *refs[n_in + len(after):])


class _Started:
    def __init__(self, send, recv, bufs, token):
        self.send, self.recv, self.bufs, self.token = send, recv, bufs, token


def _split_start(name, bufs, n_copies, copies, after=()):
    nb = len(bufs)

    def body(*refs):
        refs = refs[:nb] + refs[nb + len(after):]
        for cp in copies(refs[:nb], refs[nb], refs[nb + 1]):
            cp.start()
        refs[-1][...] = jnp.zeros_like(refs[-1])

    outs = pl.pallas_call(
        body, name=name,
        out_shape=(pltpu.SemaphoreType.DMA((n_copies,)), pltpu.SemaphoreType.DMA((n_copies,)),
                   *[pltpu.HBM(b.shape, b.dtype) for b in bufs], jax.ShapeDtypeStruct((8, 128), F32)),
        in_specs=[_HBM] * nb + [_ANY] * len(after),
        out_specs=(_SEM, _SEM, *[_HBM] * nb, pl.BlockSpec(memory_space=pltpu.VMEM)),
        input_output_aliases={k: 2 + k for k in range(nb)},
        compiler_params=_params(1, has_side_effects=_EFFECT),
    )(*[_in_hbm(b) for b in bufs], *after)
    return _Started(outs[0], outs[1], list(outs[2:2 + nb]), outs[-1])


def _split_wait(name, started, copies, after):
    nb = len(started.bufs)

    def body(*refs):
        for cp in copies(refs[:nb], refs[nb], refs[nb + 1]):
            cp.wait_send()
            cp.wait_recv()

    return list(pl.pallas_call(
        body, name=name,
        out_shape=tuple(pltpu.HBM(b.shape, b.dtype) for b in started.bufs),
        in_specs=[_HBM] * nb + [_SEM, _SEM, _ANY],
        out_specs=tuple([_HBM] * nb),
        input_output_aliases={k: k for k in range(nb)},
        compiler_params=_params(1, has_side_effects=_EFFECT),
    )(*started.bufs, started.send, started.recv, after))


def _x1_copies(ws):
    def copies(refs, send_sems, recv_sems):
        x, y, c, _ = _mesh_pos()
        out = []
        for k, w in enumerate(ws):
            for s in range(N_SHARD):
                out.append(pltpu.make_async_remote_copy(
                    src_ref=_UNITS[w](refs[k], s, 1 - c), dst_ref=refs[len(ws) + k].at[s],
                    send_sem=send_sems.at[N_SHARD * k + s], recv_sem=recv_sems.at[N_SHARD * k + s],
                    device_id=(x, y, 1 - c), device_id_type=MESH))
        return out
    return copies


def _x2_copies(n):
    def copies(refs, send_sems, recv_sems):
        x, y, c, chips = _mesh_pos()
        out = []
        for j, (cx, cy) in enumerate(chips):
            for k in range(n):
                out.append(pltpu.make_async_remote_copy(
                    src_ref=refs[k].at[2 * cx + cy], dst_ref=refs[n + k].at[j],
                    send_sem=send_sems.at[3 * k + j], recv_sem=recv_sems.at[3 * k + j],
                    device_id=(cx, cy, c), device_id_type=MESH))
        return out
    return copies


def _x3_copies(ws):
    def copies(refs, send_sems, recv_sems):
        x, y, c, _ = _mesh_pos()
        out = []
        for k, w in enumerate(ws):
            rows = _HALF_ROWS[w]
            mine = refs[k].at[pl.ds(_mo(c * rows, rows), rows), :]
            out.append(pltpu.make_async_remote_copy(
                src_ref=mine, dst_ref=mine, send_sem=send_sems.at[k], recv_sem=recv_sems.at[k],
                device_id=(x, y, 1 - c), device_id_type=MESH))
        return out
    return copies


def _x1_lands(ws):
    return [lax.empty((N_SHARD,) + _UNIT_SHAPES[w], F32) for w in ws]


def _x2_lands(ws):
    return [lax.empty((3,) + _UNIT_SHAPES[w], BF) for w in ws]


def _grad_add1(w, g, recv, pos):
    ur, uc = _UNIT_SHAPES[w]
    if w == 3:
        g_map = lambda s, pos: (2 * s + pos[0], 0)
    else:
        g_map = lambda s, pos: (pos[0], s)

    def body(pos_ref, g_ref, r_ref, cs_ref, csb_ref):
        v = g_ref[...] + r_ref[0]
        cs_ref[0] = v
        csb_ref[0] = v.astype(BF)

    u3 = lambda s, pos: (s, 0, 0)
    return pl.pallas_call(
        body, name=f"grad_add1_{w}",
        grid_spec=pltpu.PrefetchScalarGridSpec(
            num_scalar_prefetch=1, grid=(N_SHARD,),
            in_specs=[pl.BlockSpec((ur, uc), g_map), pl.BlockSpec((1, ur, uc), u3)],
            out_specs=[pl.BlockSpec((1, ur, uc), u3), pl.BlockSpec((1, ur, uc), u3)]),
        out_shape=(jax.ShapeDtypeStruct((N_SHARD, ur, uc), F32), jax.ShapeDtypeStruct((N_SHARD, ur, uc), BF)),
        compiler_params=_params(40, dimension_semantics=("arbitrary",)),
    )(pos, g, recv)


def _grad_add1_group(ws, gs, recvs):
    n = len(ws)

    def body(*refs):
        c = lax.axis_index("c")
        for k, w in enumerate(ws):
            g, r, cs, csb = refs[k], refs[n + k], refs[2 * n + k], refs[3 * n + k]
            for s in range(N_SHARD):
                v = _UNITS[w](g, s, c)[...] + r[s]
                cs[s] = v
                csb[s] = v.astype(BF)

    vm = pl.BlockSpec(memory_space=pltpu.VMEM)
    outs = pl.pallas_call(
        body, name="grad_add1_group",
        out_shape=tuple(jax.ShapeDtypeStruct((N_SHARD,) + _UNIT_SHAPES[w], dt) for dt in (F32, BF) for w in ws),
        in_specs=[vm] * (2 * n), out_specs=[vm] * (2 * n),
        compiler_params=_params(32),
    )(*gs, *recvs)
    return list(outs[:n]), list(outs[n:])


def _grad_add2_group(ws, css, recvs):
    n = len(ws)

    def body(*refs):
        x, y, c, _ = _mesh_pos()
        for k, w in enumerate(ws):
            cs, r, o = refs[k], refs[n + k], refs[2 * n + k]
            rows = _HALF_ROWS[w]
            total = ((cs[2 * x + y] + r[0].astype(F32)) + r[1].astype(F32)) + r[2].astype(F32)
            o[pl.ds(_mo(c * rows, rows), rows), :] = total

    vm = pl.BlockSpec(memory_space=pltpu.VMEM)
    return list(pl.pallas_call(
        body, name="grad_add2_group",
        out_shape=tuple(jax.ShapeDtypeStruct(_SHARD_SHAPES[w], F32) for w in ws),
        in_specs=[vm] * (2 * n), out_specs=[vm] * n,
        compiler_params=_params(32),
    )(*css, *recvs))


def _grad_add2(w, cs, recv, pos):
    ur, uc = _UNIT_SHAPES[w]
    tr = ur // 4 if w == 0 else ur
    nt = ur // tr

    def body(pos_ref, cs_ref, r_ref, o_ref):
        o_ref[...] = ((cs_ref[0] + r_ref[0].astype(F32)) + r_ref[1].astype(F32)) + r_ref[2].astype(F32)

    return pl.pallas_call(
        body, name=f"grad_add2_{w}",
        grid_spec=pltpu.PrefetchScalarGridSpec(
            num_scalar_prefetch=1, grid=(nt,),
            in_specs=[pl.BlockSpec((1, tr, uc), lambda t, pos: (pos[1], t, 0)),
                      pl.BlockSpec((3, tr, uc), lambda t, pos: (0, t, 0))],
            out_specs=pl.BlockSpec((tr, uc), lambda t, pos: (pos[0] * nt + t, 0))),
        out_shape=jax.ShapeDtypeStruct(_SHARD_SHAPES[w], F32),
        compiler_params=_params(32, dimension_semantics=("arbitrary",)),
    )(pos, cs, recv)


def _grad_xchg3(ws, halves):
    n = len(ws)

    def body(*refs):
        cps = _x3_copies(ws)(refs[:n], refs[2 * n], refs[2 * n + 1])
        for cp in cps:
            cp.start()
        for cp in cps:
            cp.wait()

    return pl.pallas_call(
        body, name="grad_xchg3",
        out_shape=tuple(jax.ShapeDtypeStruct(_SHARD_SHAPES[w], F32) for w in ws),
        in_specs=[_ANY] * n, out_specs=[_ANY] * n,
        input_output_aliases={k: k for k in range(n)},
        scratch_shapes=[pltpu.SemaphoreType.DMA((n,)), pltpu.SemaphoreType.DMA((n,))],
        compiler_params=_params(16),
    )(*halves)


def _adamw_math(w, g, m, v):
    m = ADAM_B1 * m + (1.0 - ADAM_B1) * g
    v = ADAM_B2 * v + (1.0 - ADAM_B2) * (g * g)
    m_hat = m / ADAM_C1
    v_hat = v / ADAM_C2
    delta = -ADAM_LR * (m_hat / (jnp.sqrt(v_hat) + ADAM_EPS) + ADAM_WD * w)
    return delta, m, v


def _adamw_group(ws_, gs, ms, vs, after=()):
    n = len(ws_)

    def body(*refs):
        for k in range(n):
            w, g, m, v = (refs[j * n + k] for j in range(4))
            d, nm, nv, gc = (refs[(4 + j) * n + k] for j in range(4))
            gv = g[...]
            d[...], nm[...], nv[...] = _adamw_math(w[...], gv, m[...], v[...])
            gc[...] = gv

    vm = pl.BlockSpec(memory_space=pltpu.VMEM)
    outs = pl.pallas_call(
        _after(body, 4 * n, after), name="adamw_group",
        out_shape=tuple(jax.ShapeDtypeStruct(a.shape, F32) for _ in range(4) for a in ws_),
        in_specs=[vm] * (4 * n) + [_ANY] * len(after), out_specs=[vm] * (4 * n),
        compiler_params=_params(32),
    )(*ws_, *gs, *ms, *vs, *after)
    return [tuple(outs[j * n + k] for j in range(4)) for k in range(n)]


def _adamw(name, w, g, m, v, tr=256, after=()):
    rows, cols = w.shape

    def body(w_ref, g_ref, m_ref, v_ref, d_ref, nm_ref, nv_ref, gc_ref):
        gv = g_ref[...]
        d_ref[...], nm_ref[...], nv_ref[...] = _adamw_math(w_ref[...], gv, m_ref[...], v_ref[...])
        gc_ref[...] = gv

    spec = pl.BlockSpec((tr, cols), lambda i: (i, 0))
    return pl.pallas_call(
        _after(body, 4, after), name=name, grid=(rows // tr,),
        out_shape=tuple(jax.ShapeDtypeStruct((rows, cols), F32) for _ in range(4)),
        in_specs=[spec] * 4 + [_ANY] * len(after), out_specs=[spec] * 4,
        compiler_params=_params(32, dimension_semantics=("arbitrary",)),
    )(w, g, m, v, *after)


_REL_PAD = 384
_VEC_FIELDS = (("norm_g", 0, D_MODEL), ("b_gate", 1024, 2 * D_MODEL), ("sgu_ln_g", 3072, D_B),
               ("sgu_ln_b", 3584, D_B), ("b_s", 4096, N_GROUPS * 128), ("final_g", 4608, D_MODEL))
_LOSS_OFF = 5632
_REL_OFF = 5760
_NV = _REL_OFF + N_HEADS * _REL_PAD
_N_FIELDS = len(_VEC_FIELDS) + 2


_B_S_FIELD = [f[0] for f in _VEC_FIELDS].index("b_s")


def _assemble_row(dst, fields, transposed_b_s):
    for f, (_, off, n) in enumerate(_VEC_FIELDS):
        if transposed_b_s and f == _B_S_FIELD:
            t = fields[f][...].T
            for g in range(N_GROUPS):
                dst[:, off + 128 * g:off + 128 * (g + 1)] = t[g:g + 1, :]
        else:
            dst[:, off:off + n] = fields[f][...]
    for r in range(N_HEADS):
        dst[:, _REL_OFF + _REL_PAD * r:_REL_OFF + _REL_PAD * (r + 1)] = fields[len(_VEC_FIELDS)][r:r + 1, :]


def _small_reduce(grads, loss_row, after=()):
    n_in = _N_FIELDS + 1

    def body(*refs):
        g_refs, loss_ref = refs[:_N_FIELDS], refs[_N_FIELDS]
        out_v, out_w = refs[n_in:n_in + 2]
        mine_v, gath_v, gath_w, send_sems, recv_sems = refs[n_in + 2:]
        x, y, c, chips = _mesh_pos()
        me, sibling = (x, y, c), (x, y, 1 - c)

        _assemble_row(mine_v, g_refs, True)
        mine_v[:, _LOSS_OFF:_LOSS_OFF + 128] = loss_ref[...]
        mine_w = g_refs[-1]
        my_k = 4 * x + 2 * y + c
        gath_v[my_k] = mine_v[...]
        gath_w[my_k] = mine_w[...]

        def copy(k, gath, block, to, src=None):
            dst = gath.at[4 * block[0] + 2 * block[1] + block[2]]
            return pltpu.make_async_remote_copy(
                src_ref=dst if src is None else src, dst_ref=dst,
                send_sem=send_sems.at[k], recv_sem=recv_sems.at[k], device_id=to, device_id_type=MESH)

        bufs = ((gath_v, mine_v), (gath_w, mine_w))
        first, passed = [], []
        for b, (gath, mine) in enumerate(bufs):
            first.append(copy(7 * b, gath, me, sibling, src=mine))
            first += [copy(7 * b + 1 + j, gath, me, (*chip, c), src=mine) for j, chip in enumerate(chips)]
        for cp in first:
            cp.start()
        for b, (gath, _) in enumerate(bufs):
            for j, chip in enumerate(chips):
                copy(7 * b + 1 + j, gath, (*chip, c), me).wait_recv()
                cp = copy(7 * b + 4 + j, gath, (*chip, c), sibling)
                cp.start()
                passed.append(cp)
        for b, (gath, _) in enumerate(bufs):
            copy(7 * b, gath, sibling, me).wait_recv()
            for j, chip in enumerate(chips):
                copy(7 * b + 4 + j, gath, (*chip, 1 - c), me).wait_recv()
        for cp in first + passed:
            cp.wait_send()

        tot_v, tot_w = gath_v[0], gath_w[0]
        for k in range(1, 8):
            tot_v = tot_v + gath_v[k]
            tot_w = tot_w + gath_w[k]
        out_v[...] = tot_v
        out_w[...] = tot_w

    vm = pl.BlockSpec(memory_space=pltpu.VMEM)
    return pl.pallas_call(
        _after(body, n_in, after), name="small_reduce",
        out_shape=(jax.ShapeDtypeStruct((1, _NV), F32), jax.ShapeDtypeStruct((N_GROUPS * 128, 128), F32)),
        in_specs=[vm] * n_in + [_ANY] * len(after), out_specs=[vm] * 2,
        scratch_shapes=[pltpu.VMEM((1, _NV), F32), pltpu.VMEM((8, 1, _NV), F32),
                        pltpu.VMEM((8, N_GROUPS * 128, 128), F32),
                        pltpu.SemaphoreType.DMA((14,)), pltpu.SemaphoreType.DMA((14,))],
        compiler_params=_params(32),
    )(*grads, loss_row, *after)


def _small_adamw(tot_v, tot_w, params):
    n_in = 2 + 3 * _N_FIELDS

    def body(*refs):
        tv_ref, tw_ref = refs[:2]
        p_refs = [refs[2 + k * _N_FIELDS:2 + (k + 1) * _N_FIELDS] for k in range(3)]
        outs = refs[n_in:n_in + 4 * _N_FIELDS + 1]
        wmv = refs[-1]
        for k in range(3):
            _assemble_row(wmv.at[k], p_refs[k], False)
            wmv[k, :, _LOSS_OFF:_LOSS_OFF + 128] = jnp.zeros((1, 128), F32)
        tot_v, tot_w = tv_ref[...], tw_ref[...]
        res_v = (tot_v,) + _adamw_math(wmv[0], tot_v, wmv[1], wmv[2])
        res_w = (tot_w,) + _adamw_math(p_refs[0][-1][...], tot_w, p_refs[1][-1][...], p_refs[2][-1][...])
        for kind in range(4):
            o = outs[kind * _N_FIELDS:(kind + 1) * _N_FIELDS]
            for f, (_, off, n) in enumerate(_VEC_FIELDS):
                o[f][...] = res_v[kind][:, off:off + n]
            for r in range(N_HEADS):
                o[len(_VEC_FIELDS)][r:r + 1, :] = res_v[kind][:, _REL_OFF + _REL_PAD * r:_REL_OFF + _REL_PAD * (r + 1)]
            o[-1][...] = res_w[kind]
        outs[-1][...] = tot_v[:, _LOSS_OFF:_LOSS_OFF + 128]

    field_shapes = [(1, n) for _, _, n in _VEC_FIELDS] + [(N_HEADS, _REL_PAD), (N_GROUPS * 128, 128)]
    vm = pl.BlockSpec(memory_space=pltpu.VMEM)
    operands = [tot_v, tot_w] + [a for p in params for a in p]
    assert len(operands) == n_in
    outs = pl.pallas_call(
        body, name="small_adamw",
        out_shape=tuple(jax.ShapeDtypeStruct(s, F32) for _ in range(4) for s in field_shapes)
        + (jax.ShapeDtypeStruct((1, 128), F32),),
        in_specs=[vm] * n_in, out_specs=[vm] * (4 * _N_FIELDS + 1),
        scratch_shapes=[pltpu.VMEM((3, 1, _NV), F32)],
        compiler_params=_params(32),
    )(*operands)
    return [outs[k * _N_FIELDS:(k + 1) * _N_FIELDS] for k in range(4)], outs[-1]


def _small_fields(norm_g, b_gate, ln_g, ln_b, b_s, final_g, rel_bias, w_s):
    rel = jnp.pad(rel_bias.reshape(N_HEADS, N_REL), ((0, 0), (0, _REL_PAD - N_REL)))
    return (norm_g, b_gate, ln_g, ln_b, b_s.reshape(1, N_GROUPS * 128), final_g.reshape(1, D_MODEL),
            rel, w_s.reshape(N_GROUPS * 128, 128))


def _small_outputs(fields):
    n_g, b_g, l_g, l_b, b_s, f_g, rel, w_s = fields
    return (n_g, b_g, rel[:, :N_REL].reshape(1, N_HEADS, N_REL), l_g, l_b,
            w_s.reshape(1, N_GROUPS, 128, 128), b_s.reshape(1, N_GROUPS, 128), f_g.reshape(D_MODEL))


def _bias_row(rel_bias):
    hi = rel_bias[:, N_REL - 1:N_REL]
    lo = rel_bias[:, 0:1]
    return jnp.concatenate([jnp.broadcast_to(hi, (N_HEADS, 384)), rel_bias[:, ::-1],
                            jnp.broadcast_to(lo, (N_HEADS, 191)), jnp.broadcast_to(hi, (N_HEADS, 192))], axis=1)


def kernel(x, norm_g, w_in, b_gate, rel_bias, sgu_ln_g, sgu_ln_b, w_s, b_s, w_pa, w_pb, w_out, final_g, loss_target, m_norm_g, m_w_in, m_b_gate, m_rel_bias, m_sgu_ln_g, m_sgu_ln_b, m_w_s, m_b_s, m_w_pa, m_w_pb, m_w_out, m_final_g, v_norm_g, v_w_in, v_b_gate, v_rel_bias, v_sgu_ln_g, v_sgu_ln_b, v_w_s, v_b_s, v_w_pa, v_w_pb, v_w_out, v_final_g):
    S = x.shape[1]
    xs = x.reshape(S, D_MODEL)
    tgt = loss_target.reshape(S, D_MODEL)
    big_w = (w_in[0], w_pa[0], w_pb[0], w_out[0])
    big_m = (m_w_in[0], m_w_pa[0], m_w_pb[0], m_w_out[0])
    big_v = (v_w_in[0], v_w_pa[0], v_w_pb[0], v_w_out[0])
    rel = rel_bias[0]
    ws = w_s[0]
    bst = b_s[0].T
    fg = final_g.reshape(1, D_MODEL)
    pos = jnp.stack([lax.axis_index("c"), 2 * lax.axis_index("x") + lax.axis_index("y")]).astype(jnp.int32)

    staged = _stage_weights((1, 2, 3), big_w[1:], pos)
    w_in_bf, = _ag_weights((0,), big_w[:1])
    ag_s = _split_start("ag_small_start", staged, 9, _gather_copies((1, 2, 3)), after=(w_in_bf,))

    ht, q3, k3, v3, zrest = _inproj_fwd(xs, norm_g, w_in_bf, after=(ag_s.token,))
    gp = _bias_row(rel)
    att, lse, band_bias = _attn_fwd(q3, k3, v3, gp)
    w_pa_bf, w_pb_bf, w_out_bf = _split_wait("ag_small_wait", ag_s, _gather_copies((1, 2, 3)), att)
    (d_out, d_att, dzt, dzs, gw_out, gw_pa, gw_pb, g_bgate, g_final, loss_row,
     g_ws, g_bs_t, g_lng, g_lnb) = _tail_sgu(
        att, zrest, xs, tgt, w_pa_bf, w_pb_bf, w_out_bf, b_gate, fg, sgu_ln_g, sgu_ln_b, ws, bst)
    ws_s, ws_i = (1, 2, 3), (0,)
    names = ("adamw_w_in", "adamw_w_pa", "adamw_w_pb", "adamw_w_out")

    x1s = _split_start("gx1s_start", [gw_pa, gw_pb, gw_out] + _x1_lands(ws_s), 12, _x1_copies(ws_s))
    dq, dk, dv, d_gp = _attn_bwd(q3, k3, v3, d_att, lse, band_bias, after=(x1s.token,))
    got = _split_wait("gx1s_wait", x1s, _x1_copies(ws_s), dq)
    cs_s, csb_s = _grad_add1_group(ws_s, got[:3], got[3:])

    x2s = _split_start("gx2s_start", csb_s + _x2_lands(ws_s), 9, _x2_copies(3))
    gw_in = _gw_in(ht, dq, dk, dv, dzt, dzs, after=(x2s.token,))
    got = _split_wait("gx2s_wait", x2s, _x2_copies(3), gw_in)
    halves_s = _grad_add2_group(ws_s, cs_s, got[3:])

    x3s = _split_start("gx3s_start", halves_s, 3, _x3_copies(ws_s))
    x1i = _split_start("gx1i_start", [gw_in] + _x1_lands(ws_i), 4, _x1_copies(ws_i))
    dh_args = (dq, dk, dv, dzt, dzs, w_in_bf, xs, norm_g, d_out)
    part = _dh_gradx(*dh_args, after=(x3s.token, x1i.token))
    g_shards_s = _split_wait("gx3s_wait", x3s, _x3_copies(ws_s), part[0])
    got = _split_wait("gx1i_wait", x1i, _x1_copies(ws_i), part[0])
    sum_i = _grad_add1(0, got[0], got[1], pos)

    x2i = _split_start("gx2i_start", [sum_i[1]] + _x2_lands(ws_i), 3, _x2_copies(1))
    grad_x, g_norm = _dh_gradx(*dh_args, prev=part, after=(x2i.token,))
    big = [None] * 4
    big[1:] = _adamw_group(big_w[1:], g_shards_s, big_m[1:], big_v[1:], after=(x2i.token,))

    g_rel = jnp.pad(d_gp[:, 384:384 + N_REL][:, ::-1], ((0, 0), (0, _REL_PAD - N_REL)))
    small_grads = (g_norm, g_bgate, g_lng, g_lnb, g_bs_t, g_final, g_rel, g_ws.reshape(N_GROUPS * 128, 128))
    small_params = (_small_fields(norm_g, b_gate, sgu_ln_g, sgu_ln_b, b_s, final_g, rel_bias, w_s),
                    _small_fields(m_norm_g, m_b_gate, m_sgu_ln_g, m_sgu_ln_b, m_b_s, m_final_g, m_rel_bias, m_w_s),
                    _small_fields(v_norm_g, v_b_gate, v_sgu_ln_g, v_sgu_ln_b, v_b_s, v_final_g, v_rel_bias, v_w_s))
    tot_v, tot_w = _small_reduce(small_grads, loss_row, after=(x2i.token,))
    (gsum, sdelta, sm, sv), loss_out = _small_adamw(tot_v, tot_w, small_params)

    got = _split_wait("gx2i_wait", x2i, _x2_copies(1), loss_out)
    half_i = _grad_add2(0, sum_i[0], got[1], pos)
    g_shard_i, = _grad_xchg3(ws_i, [half_i])
    big[0] = _adamw(names[0], big_w[0], g_shard_i, big_m[0], big_v[0])
    sg_out, sd_out, sm_out, sv_out = (_small_outputs(f) for f in (gsum, sdelta, sm, sv))
    loss = loss_out[0, 0]

    def assemble(small, bigs):
        n_g, b_g, r_b, l_g, l_b, w_s_, b_s_, f_g = small
        b_in, b_pa, b_pb, b_out = (b[None] for b in bigs)
        return (n_g, b_in, b_g, r_b, l_g, l_b, w_s_, b_s_, b_pa, b_pb, b_out, f_g)

    grads_out = assemble(sg_out, [b[3] for b in big])
    delta_out = assemble(sd_out, [b[0] for b in big])
    m_out = assemble(sm_out, [b[1] for b in big])
    v_out = assemble(sv_out, [b[2] for b in big])
    return (loss, grad_x.reshape(1, S, D_MODEL), *grads_out, *delta_out, *m_out, *v_out)
```

```python
import functools
import math

import jax
import jax.numpy as jnp
from jax import lax
from jax.experimental import pallas as pl
from jax.experimental.pallas import tpu as pltpu

F32 = jnp.float32
BF = jnp.bfloat16
MESH = pl.DeviceIdType.MESH

D_MODEL = 1024
D_A = 512
D_B = 512
D_IN = 5632
N_HEADS = 8
HEAD_DIM = 64
CHUNK = 64
N_PREV = 8
SGU_CHUNK = 128
N_GROUPS = 4
N_REL = 257
EPS = 1e-6
NEG_INF = -1e30
SCALE = HEAD_DIM ** -0.5

QB = 2 * CHUNK
KB = (N_PREV + 2) * CHUNK
PADK = N_PREV * CHUNK
ROLL_W = 1024
N_RING = KB // QB
KEEP = N_RING - 1

ADAM_LR = 0.001
ADAM_B1 = 0.9
ADAM_B2 = 0.999
ADAM_EPS = 1e-08
ADAM_WD = 0.01
ADAM_STEP = 10
ADAM_C1 = 1.0 - ADAM_B1 ** ADAM_STEP
ADAM_C2 = 1.0 - ADAM_B2 ** ADAM_STEP

N_SHARD = 4
SHARD_IN = D_IN // N_SHARD
MIB = 1024 * 1024


VMEM_RESERVE_MIB = 60


def _params(vmem_mib, **kw):
    assert vmem_mib <= VMEM_RESERVE_MIB
    return pltpu.CompilerParams(vmem_limit_bytes=VMEM_RESERVE_MIB * MIB, **kw)


def _sigmoid(x):
    return 1.0 / (1.0 + jnp.exp(-x))


def _silu_and_grad(x):
    s = _sigmoid(x)
    return x * s, s * (1.0 + x * (1.0 - s))


_GELU_C = math.sqrt(2.0 / math.pi)
_GELU_A = 0.044715


def _gelu_and_grad(x):
    x2 = x * x
    t = jnp.tanh(_GELU_C * (x + _GELU_A * (x2 * x)))
    cdf = 0.5 * (1.0 + t)
    grad = cdf + 0.5 * x * (1.0 - t * t) * (_GELU_C * (1.0 + 3.0 * _GELU_A * x2))
    return x * cdf, grad


def _dot(a, b):
    return jnp.dot(a, b, preferred_element_type=F32)


def _dot_nt(a, b):
    return lax.dot_general(a, b, (((1,), (1,)), ((), ())), preferred_element_type=F32)


def _dot_tn(a, b):
    return lax.dot_general(a, b, (((0,), (0,)), ((), ())), preferred_element_type=F32)


def _mo(v, m):
    return v if isinstance(v, int) else pl.multiple_of(v, m)


def _unit_in(ref, s, p):
    return ref.at[pl.ds(_mo(p * 512, 512), 512), pl.ds(_mo(s * SHARD_IN, 128), SHARD_IN)]


def _unit_p(ref, s, p):
    return ref.at[pl.ds(_mo(p * 256, 256), 256), pl.ds(_mo(s * 256, 128), 256)]


def _unit_out(ref, s, p):
    return ref.at[pl.ds(_mo(s * 256 + p * 128, 128), 128), :]


_UNITS = (_unit_in, _unit_p, _unit_p, _unit_out)
_HALF_ROWS = (512, 256, 256, 128)
_UNIT_SHAPES = ((512, SHARD_IN), (256, 256), (256, 256), (128, D_MODEL))
_FULL_SHAPES = ((D_MODEL, D_IN), (D_A, D_MODEL), (D_B, D_MODEL), (D_MODEL, D_MODEL))
_SHARD_SHAPES = ((D_MODEL, SHARD_IN), (D_A, 256), (D_B, 256), (256, D_MODEL))


def _mesh_pos():
    x, y, c = lax.axis_index("x"), lax.axis_index("y"), lax.axis_index("c")
    chips = [(1 - x, y), (x, 1 - y), (1 - x, 1 - y)]
    return x, y, c, chips


def _ag_weights(ws, shards):
    n = len(ws)

    def body(*refs):
        ins, outs, stage = refs[:n], refs[n:2 * n], refs[2 * n:3 * n]
        send_sems, recv_sems, local_sems = refs[3 * n:]
        x, y, c, chips = _mesh_pos()
        s_me = 2 * x + y
        sibling = (x, y, 1 - c)
        for k in range(n):
            stage[k][...] = ins[k][...].astype(BF)

        def half(k, p):
            rows = _HALF_ROWS[ws[k]]
            return stage[k].at[pl.ds(_mo(p * rows, rows), rows), :]

        def unit(k, s, p):
            return _UNITS[ws[k]](outs[k], s, p)

        local = []
        for k in range(n):
            for p in range(2):
                cp = pltpu.make_async_copy(half(k, p), unit(k, s_me, p), local_sems.at[k, p])
                cp.start()
                local.append(cp)

        def rcopy(k, i, src, dst, to):
            return pltpu.make_async_remote_copy(src_ref=src, dst_ref=dst, send_sem=send_sems.at[k, i],
                                                recv_sem=recv_sems.at[k, i], device_id=to, device_id_type=MESH)

        sends = []
        for j, (cx, cy) in enumerate(chips):
            for k in range(n):
                cp = rcopy(k, j, half(k, c), unit(k, s_me, c), (cx, cy, c))
                cp.start()
                sends.append(cp)
        for j, (cx, cy) in enumerate(chips):
            for k in range(n):
                landed = unit(k, 2 * cx + cy, c)
                rcopy(k, j, landed, landed, (cx, cy, c)).wait_recv()
                cp = rcopy(k, 3 + j, landed, landed, sibling)
                cp.start()
                sends.append(cp)
        for j, (cx, cy) in enumerate(chips):
            for k in range(n):
                other = unit(k, 2 * cx + cy, 1 - c)
                rcopy(k, 3 + j, other, other, sibling).wait_recv()
        for cp in sends:
            cp.wait_send()
        for cp in local:
            cp.wait()

    vm = pl.BlockSpec(memory_space=pltpu.VMEM)
    return pl.pallas_call(
        body, name="ag_weights",
        out_shape=tuple(jax.ShapeDtypeStruct(_FULL_SHAPES[w], BF) for w in ws),
        in_specs=[vm] * n, out_specs=[_ANY] * n,
        scratch_shapes=[pltpu.VMEM(_SHARD_SHAPES[w], BF) for w in ws]
        + [pltpu.SemaphoreType.DMA((n, 6)), pltpu.SemaphoreType.DMA((n, 6)), pltpu.SemaphoreType.DMA((n, 2))],
        compiler_params=_params(40),
    )(*shards)


def _shard_of(ref, w, s):
    if w == 0:
        return ref.at[:, pl.ds(_mo(s * SHARD_IN, 128), SHARD_IN)]
    if w == 3:
        return ref.at[pl.ds(_mo(s * 256, 256), 256), :]
    return ref.at[:, pl.ds(_mo(s * 256, 128), 256)]


def _stage_weights(ws, shards, pos):
    n = len(ws)

    def body(pos_ref, *refs):
        for k in range(n):
            refs[n + k][...] = refs[k][...].astype(BF)

    def spec(w):
        shape = _SHARD_SHAPES[w]
        if w == 3:
            return pl.BlockSpec(shape, lambda i, pos: (pos[1], 0))
        return pl.BlockSpec(shape, lambda i, pos: (0, pos[1]))

    return list(pl.pallas_call(
        body, name="stage_weights",
        grid_spec=pltpu.PrefetchScalarGridSpec(
            num_scalar_prefetch=1, grid=(1,),
            in_specs=[pl.BlockSpec(_SHARD_SHAPES[w], lambda i, pos: (0, 0)) for w in ws],
            out_specs=[spec(w) for w in ws]),
        out_shape=tuple(jax.ShapeDtypeStruct(_FULL_SHAPES[w], BF) for w in ws),
        compiler_params=_params(16, dimension_semantics=("arbitrary",)),
    )(pos, *shards))


def _gather_copies(ws):
    def copies(refs, send_sems, recv_sems):
        x, y, c, chips = _mesh_pos()
        out = []
        for j, (cx, cy) in enumerate(chips):
            for k, w in enumerate(ws):
                mine = _shard_of(refs[k], w, 2 * x + y)
                out.append(pltpu.make_async_remote_copy(
                    src_ref=mine, dst_ref=mine, send_sem=send_sems.at[3 * k + j], recv_sem=recv_sems.at[3 * k + j],
                    device_id=(cx, cy, c), device_id_type=MESH))
        return out
    return copies


def _inproj_fwd(x, norm_g, w_in_bf, tm=512, after=()):
    S = x.shape[0]

    def body(x_ref, g_ref, w_ref, ht_ref, q_ref, k_ref, v_ref, zr_ref):
        xv = x_ref[...]
        r = lax.rsqrt(jnp.mean(xv * xv, axis=-1, keepdims=True) + EPS)
        hf = (xv * r) * g_ref[...]
        ht_ref[...] = hf.T.astype(BF)
        h = hf.astype(BF)
        heads = (q_ref, k_ref, v_ref)
        for j in range(D_IN // 512):
            z = _dot(h, w_ref[:, j * 512:(j + 1) * 512])
            if j < 3:
                zb = z.astype(BF)
                for hd in range(N_HEADS):
                    heads[j][hd] = zb[:, hd * HEAD_DIM:(hd + 1) * HEAD_DIM]
            else:
                zr_ref[:, (j - 3) * 512:(j - 2) * 512] = z

    head_major = jax.ShapeDtypeStruct((N_HEADS, S, HEAD_DIM), BF)
    head_spec = pl.BlockSpec((N_HEADS, tm, HEAD_DIM), lambda i: (0, i, 0))
    return pl.pallas_call(
        _after(body, 3, after), name="inproj_fwd", grid=(S // tm,),
        out_shape=(jax.ShapeDtypeStruct((D_MODEL, S), BF), head_major, head_major, head_major,
                   jax.ShapeDtypeStruct((S, D_IN - 3 * D_A), F32)),
        in_specs=[pl.BlockSpec((tm, D_MODEL), lambda i: (i, 0)),
                  pl.BlockSpec((1, D_MODEL), lambda i: (0, 0)),
                  pl.BlockSpec((D_MODEL, D_IN), lambda i: (0, 0), pipeline_mode=pl.Buffered(1))]
        + [_ANY] * len(after),
        out_specs=[pl.BlockSpec((D_MODEL, tm), lambda i: (0, i)),
                   head_spec, head_spec, head_spec,
                   pl.BlockSpec((tm, D_IN - 3 * D_A), lambda i: (i, 0))],
        compiler_params=_params(52, dimension_semantics=("arbitrary",)),
    )(x, norm_g, w_in_bf, *after)


def _skew_table(gp_row):
    row = lax.broadcasted_iota(jnp.int32, (QB, ROLL_W), 0)
    t = jnp.broadcast_to(gp_row, (QB, ROLL_W))
    for b in range(7):
        t = jnp.where(((row >> b) & 1) == 1, pltpu.roll(t, 1 << b, axis=1), t)
    return t


def _unskew_sum(d):
    row = lax.broadcasted_iota(jnp.int32, (QB, ROLL_W), 0)
    for b in range(7):
        d = jnp.where(((row >> b) & 1) == 1, pltpu.roll(d, ROLL_W - (1 << b), axis=1), d)
    return jnp.sum(d, axis=0, keepdims=True)


def _struct_mask():
    a = lax.broadcasted_iota(jnp.int32, (QB, KB), 0) // CHUNK
    b = lax.broadcasted_iota(jnp.int32, (QB, KB), 1) // CHUNK
    return (b >= a) & (b <= a + N_PREV)


def _load_kv(k_hbm, v_hbm, k_scr, v_scr, sems, S, meanwhile=lambda: None):
    zeros = jnp.zeros((N_HEADS, PADK, HEAD_DIM), BF)
    k_scr[:, 0:PADK, :] = zeros
    v_scr[:, 0:PADK, :] = zeros
    ck = pltpu.make_async_copy(k_hbm, k_scr.at[:, pl.ds(PADK, S), :], sems.at[0])
    cv = pltpu.make_async_copy(v_hbm, v_scr.at[:, pl.ds(PADK, S), :], sems.at[1])
    ck.start()
    cv.start()
    meanwhile()
    ck.wait()
    cv.wait()


_BATCH_NT = (((2,), (2,)), ((0,), (0,)))
_BATCH_NN = (((2,), (1,)), ((0,), (0,)))
_BATCH_TN = (((1,), (1,)), ((0,), (0,)))


def _bdot(a, b, dims):
    return lax.dot_general(a, b, dims, preferred_element_type=F32)


def _scaled(q):
    return q * jnp.asarray(SCALE, BF)


def _scores(qs, kb, bias, i, front):
    s = _bdot(qs, kb, _BATCH_NT) + bias
    if front:
        col = lax.broadcasted_iota(jnp.int32, (1, 1, KB), 2)
        s = jnp.where(col >= PADK - i * QB, s, NEG_INF)
    return s


def _attn_fwd(q3, k3, v3, gp):
    S = q3.shape[1]

    def body(q_ref, k_hbm, v_hbm, gp_ref, o_ref, lse_ref, bias_ref, k_scr, v_scr, sems):
        i = pl.program_id(0)

        @pl.when(i == 0)
        def _():
            def build_bias():
                keep = _struct_mask()
                for h in range(N_HEADS):
                    bias_ref[h] = jnp.where(keep, _skew_table(gp_ref[h:h + 1, :])[:, :KB], NEG_INF)
            _load_kv(k_hbm, v_hbm, k_scr, v_scr, sems, S, build_bias)

        def step(front):
            start = pl.multiple_of(i * QB, QB)
            kb = k_scr[:, pl.ds(start, KB), :]
            vb = v_scr[:, pl.ds(start, KB), :]
            s = _scores(_scaled(q_ref[...]), kb, bias_ref[...], i, front)
            m = jnp.max(s, axis=-1, keepdims=True)
            e = jnp.exp(s - m)
            l = jnp.sum(e, axis=-1, keepdims=True)
            p = e * (1.0 / l)
            o = _bdot(p.astype(BF), vb, _BATCH_NN)
            lse_ref[...] = jnp.broadcast_to(m + jnp.log(l), (N_HEADS, QB, 128))
            for h in range(N_HEADS):
                o_ref[:, h * HEAD_DIM:(h + 1) * HEAD_DIM] = o[h]

        pl.when(i < KEEP)(functools.partial(step, True))
        pl.when(i >= KEEP)(functools.partial(step, False))

    kv_scr = pltpu.VMEM((N_HEADS, S + PADK, HEAD_DIM), BF)
    return pl.pallas_call(
        body, name="attn_fwd", grid=(S // QB,),
        out_shape=(jax.ShapeDtypeStruct((S, D_A), F32), jax.ShapeDtypeStruct((N_HEADS, S, 128), F32),
                   jax.ShapeDtypeStruct((N_HEADS, QB, KB), F32)),
        in_specs=[pl.BlockSpec((N_HEADS, QB, HEAD_DIM), lambda i: (0, i, 0)),
                  pl.BlockSpec(memory_space=pl.ANY), pl.BlockSpec(memory_space=pl.ANY),
                  pl.BlockSpec((N_HEADS, ROLL_W), lambda i: (0, 0))],
        out_specs=[pl.BlockSpec((QB, D_A), lambda i: (i, 0)),
                   pl.BlockSpec((N_HEADS, QB, 128), lambda i: (0, i, 0)),
                   pl.BlockSpec((N_HEADS, QB, KB), lambda i: (0, 0, 0))],
        scratch_shapes=[kv_scr, kv_scr, pltpu.SemaphoreType.DMA((2,))],
        compiler_params=_params(48, dimension_semantics=("arbitrary",)),
    )(q3, k3, v3, gp)


def _attn_bwd(q3, k3, v3, d_att3, lse, bias, after=()):
    S = q3.shape[1]
    nq = S // QB

    def body(q_ref, do_ref, k_hbm, v_hbm, lse_ref, bias_ref, dq_ref, dk_ref, dv_ref, dgp_ref,
             k_scr, v_scr, dk_acc, dv_acc, dbias_acc, pad_scr, sems):
        i = pl.program_id(0)

        @pl.when(i == 0)
        def _():
            def clear():
                dk_acc[...] = jnp.zeros_like(dk_acc)
                dv_acc[...] = jnp.zeros_like(dv_acc)
                dbias_acc[...] = jnp.zeros_like(dbias_acc)
            _load_kv(k_hbm, v_hbm, k_scr, v_scr, sems, S, clear)

        def step(front):
            start = pl.multiple_of(i * QB, QB)
            kb = k_scr[:, pl.ds(start, KB), :]
            vb = v_scr[:, pl.ds(start, KB), :]
            qs = _scaled(q_ref[...])
            do = do_ref[...]
            p = jnp.exp(_scores(qs, kb, bias_ref[...], i, front) - jnp.tile(lse_ref[...], (1, 1, KB // 128)))
            dp = _bdot(do, vb, _BATCH_NT)
            ds = p * (dp - jnp.sum(dp * p, axis=-1, keepdims=True))
            dbias_acc[...] += ds
            dsb = ds.astype(BF)
            dq = _bdot(dsb, kb, _BATCH_NN) * SCALE
            for h in range(N_HEADS):
                dq_ref[:, h * HEAD_DIM:(h + 1) * HEAD_DIM] = dq[h].astype(BF)
            dk_acc[...] += _bdot(dsb, qs, _BATCH_TN)
            dv_acc[...] += _bdot(p.astype(BF), do, _BATCH_TN)

        pl.when(i < KEEP)(functools.partial(step, True))
        pl.when((i >= KEEP) & (i < nq))(functools.partial(step, False))

        for h in range(N_HEADS):
            hs = slice(h * HEAD_DIM, (h + 1) * HEAD_DIM)
            dk_ref[:, hs] = dk_acc[h, 0:QB, :].astype(BF)
            dv_ref[:, hs] = dv_acc[h, 0:QB, :].astype(BF)
        dk_acc[:, 0:KB - QB, :] = dk_acc[:, QB:KB, :]
        dv_acc[:, 0:KB - QB, :] = dv_acc[:, QB:KB, :]
        dk_acc[:, KB - QB:KB, :] = jnp.zeros((N_HEADS, QB, HEAD_DIM), F32)
        dv_acc[:, KB - QB:KB, :] = jnp.zeros((N_HEADS, QB, HEAD_DIM), F32)

        @pl.when(i == nq + KEEP - 1)
        def _():
            lane = lax.broadcasted_iota(jnp.int32, (1, ROLL_W), 1)
            hi = (lane < 384) | (lane >= 832)
            lo = (lane > 640) & (lane < 832)
            pad_scr[...] = jnp.zeros_like(pad_scr)
            for h in range(N_HEADS):
                pad_scr[:, 0:KB] = dbias_acc[h]
                g = _unskew_sum(pad_scr[...])
                s_hi = jnp.sum(jnp.where(hi, g, 0.0), axis=-1, keepdims=True)
                s_lo = jnp.sum(jnp.where(lo, g, 0.0), axis=-1, keepdims=True)
                g = jnp.where(lane == 384, g + s_hi, g)
                g = jnp.where(lane == 640, g + s_lo, g)
                dgp_ref[h:h + 1, :] = g

    last = nq - 1
    kv_scr = pltpu.VMEM((N_HEADS, S + PADK, HEAD_DIM), BF)
    return pl.pallas_call(
        _after(body, 6, after), name="attn_bwd", grid=(nq + KEEP,),
        out_shape=(jax.ShapeDtypeStruct((S, D_A), BF), jax.ShapeDtypeStruct((S, D_A), BF),
                   jax.ShapeDtypeStruct((S, D_A), BF), jax.ShapeDtypeStruct((N_HEADS, ROLL_W), F32)),
        in_specs=[pl.BlockSpec((N_HEADS, QB, HEAD_DIM), lambda i: (0, jnp.minimum(i, last), 0)),
                  pl.BlockSpec((N_HEADS, QB, HEAD_DIM), lambda i: (0, jnp.minimum(i, last), 0)),
                  pl.BlockSpec(memory_space=pl.ANY), pl.BlockSpec(memory_space=pl.ANY),
                  pl.BlockSpec((N_HEADS, QB, 128), lambda i: (0, jnp.minimum(i, last), 0)),
                  pl.BlockSpec((N_HEADS, QB, KB), lambda i: (0, 0, 0))] + [_ANY] * len(after),
        out_specs=[pl.BlockSpec((QB, D_A), lambda i: (jnp.minimum(i, last), 0)),
                   pl.BlockSpec((QB, D_A), lambda i: (jnp.maximum(i - KEEP, 0), 0)),
                   pl.BlockSpec((QB, D_A), lambda i: (jnp.maximum(i - KEEP, 0), 0)),
                   pl.BlockSpec((N_HEADS, ROLL_W), lambda i: (0, 0))],
        scratch_shapes=[kv_scr, kv_scr,
                        pltpu.VMEM((N_HEADS, KB, HEAD_DIM), F32), pltpu.VMEM((N_HEADS, KB, HEAD_DIM), F32),
                        pltpu.VMEM((N_HEADS, QB, KB), F32), pltpu.VMEM((QB, ROLL_W), F32),
                        pltpu.SemaphoreType.DMA((2,))],
        compiler_params=_params(56, dimension_semantics=("arbitrary",)),
    )(q3, d_att3, k3, v3, lse, bias, *after)


def _sgu_core(ub, vb, lg, lb):
    u, du = _gelu_and_grad(ub)
    v, dv = _gelu_and_grad(vb)
    mu = jnp.mean(v, axis=-1, keepdims=True)
    vc = v - mu
    rstd = lax.rsqrt(jnp.mean(vc * vc, axis=-1, keepdims=True) + EPS)
    xh = vc * rstd
    vn = xh * lg + lb
    return u, du, dv, rstd, xh, vn


def _tri():
    r = lax.broadcasted_iota(jnp.int32, (SGU_CHUNK, SGU_CHUNK), 0)
    c = lax.broadcasted_iota(jnp.int32, (SGU_CHUNK, SGU_CHUNK), 1)
    return r >= c


def _sgu_fwd(zrest, ln_g, ln_b, w_s, b_s_t, tm=512):
    S = zrest.shape[0]

    def body(ub_ref, vb_ref, lg_ref, lb_ref, ws_ref, bst_ref, sg_ref):
        u, _, _, _, _, vn = _sgu_core(ub_ref[...], vb_ref[...], lg_ref[...], lb_ref[...])
        vnb = vn.astype(BF)
        tri = _tri()
        for g in range(N_GROUPS):
            cs = slice(g * 128, (g + 1) * 128)
            wt = jnp.where(tri, ws_ref[g], 0.0).astype(BF)
            bcol = bst_ref[:, g:g + 1]
            for n in range(tm // SGU_CHUNK):
                rs = slice(n * SGU_CHUNK, (n + 1) * SGU_CHUNK)
                mixed = _dot(wt, vnb[rs, cs]) + bcol
                sg_ref[rs, cs] = u[rs, cs] * mixed

    return pl.pallas_call(
        body, name="sgu_fwd", grid=(S // tm,),
        out_shape=jax.ShapeDtypeStruct((S, D_B), F32),
        in_specs=[pl.BlockSpec((tm, 512), lambda i: (i, 1)),
                  pl.BlockSpec((tm, 512), lambda i: (i, 2)),
                  pl.BlockSpec((1, D_B), lambda i: (0, 0)),
                  pl.BlockSpec((1, D_B), lambda i: (0, 0)),
                  pl.BlockSpec((N_GROUPS, 128, 128), lambda i: (0, 0, 0)),
                  pl.BlockSpec((128, N_GROUPS), lambda i: (0, 0))],
        out_specs=pl.BlockSpec((tm, D_B), lambda i: (i, 0)),
        compiler_params=_params(32, dimension_semantics=("arbitrary",)),
    )(zrest, zrest, ln_g, ln_b, w_s, b_s_t)


def _sgu_bwd(zrest, d_sg, ln_g, ln_b, w_s, b_s_t, tm=256, after=()):
    S = zrest.shape[0]
    nt = S // tm

    def body(ub_ref, vb_ref, dsg_ref, lg_ref, lb_ref, ws_ref, bst_ref,
             dzs_ref, gws_ref, gbs_ref, glg_ref, glb_ref, dvn_scr, bs_acc):
        i = pl.program_id(0)

        @pl.when(i == 0)
        def _():
            gws_ref[...] = jnp.zeros_like(gws_ref)
            glg_ref[...] = jnp.zeros_like(glg_ref)
            glb_ref[...] = jnp.zeros_like(glb_ref)
            bs_acc[...] = jnp.zeros_like(bs_acc)

        ub = ub_ref[...]
        u, du, dv, rstd, xh, vn = _sgu_core(ub, vb_ref[...], lg_ref[...], lb_ref[...])
        vnb = vn.astype(BF)
        dsg = dsg_ref[...]
        tri = _tri()
        for g in range(N_GROUPS):
            cs = slice(g * 128, (g + 1) * 128)
            wtf = jnp.where(tri, ws_ref[g], 0.0)
            wt = wtf.astype(BF)
            wtt = wtf.T.astype(BF)
            bcol = bst_ref[:, g:g + 1]
            for n in range(tm // SGU_CHUNK):
                rs = slice(n * SGU_CHUNK, (n + 1) * SGU_CHUNK)
                mixed = _dot(wt, vnb[rs, cs]) + bcol
                dzs_ref[rs, cs] = (dsg[rs, cs] * mixed * du[rs, cs]).astype(BF)
                dmix = dsg[rs, cs] * u[rs, cs]
                bs_acc[:, cs] += dmix
                dmb = dmix.astype(BF)
                gws_ref[g] += _dot_nt(dmb, vnb[rs, cs])
                dvn_scr[rs, cs] = _dot(wtt, dmb)
        dvn = dvn_scr[...]
        glg_ref[...] += jnp.sum(dvn * xh, axis=0, keepdims=True)
        glb_ref[...] += jnp.sum(dvn, axis=0, keepdims=True)
        dxh = dvn * lg_ref[...]
        dvv = rstd * (dxh - jnp.mean(dxh, axis=-1, keepdims=True)
                      - xh * jnp.mean(dxh * xh, axis=-1, keepdims=True))
        dzs_ref[:, D_B:2 * D_B] = (dvv * dv).astype(BF)

        @pl.when(i == nt - 1)
        def _():
            lane = lax.broadcasted_iota(jnp.int32, (SGU_CHUNK, 128), 1)
            out = jnp.zeros((SGU_CHUNK, 128), F32)
            for g in range(N_GROUPS):
                gws_ref[g] = jnp.where(tri, gws_ref[g], 0.0)
                col = jnp.sum(bs_acc[:, g * 128:(g + 1) * 128], axis=-1, keepdims=True)
                out = jnp.where(lane == g, col, out)
            gbs_ref[...] = out

    const2 = lambda i: (0, 0)
    return pl.pallas_call(
        _after(body, 7, after), name="sgu_bwd", grid=(nt,),
        out_shape=(jax.ShapeDtypeStruct((S, 2 * D_B), BF),
                   jax.ShapeDtypeStruct((N_GROUPS, 128, 128), F32),
                   jax.ShapeDtypeStruct((SGU_CHUNK, 128), F32),
                   jax.ShapeDtypeStruct((1, D_B), F32), jax.ShapeDtypeStruct((1, D_B), F32)),
        in_specs=[pl.BlockSpec((tm, 512), lambda i: (i, 1)),
                  pl.BlockSpec((tm, 512), lambda i: (i, 2)),
                  pl.BlockSpec((tm, D_B), lambda i: (i, 0)),
                  pl.BlockSpec((1, D_B), const2), pl.BlockSpec((1, D_B), const2),
                  pl.BlockSpec((N_GROUPS, 128, 128), lambda i: (0, 0, 0)),
                  pl.BlockSpec((128, N_GROUPS), const2)] + [_ANY] * len(after),
        out_specs=[pl.BlockSpec((tm, 2 * D_B), lambda i: (i, 0)),
                   pl.BlockSpec((N_GROUPS, 128, 128), lambda i: (0, 0, 0)),
                   pl.BlockSpec((SGU_CHUNK, 128), const2),
                   pl.BlockSpec((1, D_B), const2), pl.BlockSpec((1, D_B), const2)],
        scratch_shapes=[pltpu.VMEM((tm, D_B), F32), pltpu.VMEM((SGU_CHUNK, D_B), F32)],
        compiler_params=_params(32, dimension_semantics=("arbitrary",)),
    )(zrest, zrest, d_sg, ln_g, ln_b, w_s, b_s_t, *after)


def _tail(att, sg, zrest, x, target, w_pa, w_pb, w_out, b_gate, final_g, tm=256):
    S = x.shape[0]
    nt = S // tm

    def body(att_ref, sg_ref, ga_ref, gb_ref, gta_ref, gtb_ref, x_ref, t_ref,
             wpa_ref, wpb_ref, wout_ref, bg_ref, fg_ref,
             dout_ref, datt_ref, dsg_ref, dzt_ref, gwout_hbm, gwpa_hbm, gwpb_hbm,
             gbg_ref, gfg_ref, loss_ref, acc_out, acc_pa, acc_pb, sems):
        i = pl.program_id(0)

        @pl.when(i == 0)
        def _():
            acc_out[...] = jnp.zeros_like(acc_out)
            acc_pa[...] = jnp.zeros_like(acc_pa)
            acc_pb[...] = jnp.zeros_like(acc_pb)
            gbg_ref[...] = jnp.zeros_like(gbg_ref)
            gfg_ref[...] = jnp.zeros_like(gfg_ref)
            loss_ref[...] = jnp.zeros_like(loss_ref)

        att = att_ref[...]
        sg = sg_ref[...]
        sa, dsa = _silu_and_grad(ga_ref[...])
        sb, dsb = _silu_and_grad(gb_ref[...])
        ya = (att * sa).astype(BF)
        yb = (sg * sb).astype(BF)
        pa = _dot(ya, wpa_ref[...])
        pb = _dot(yb, wpb_ref[...])
        ga = _sigmoid(gta_ref[...] + bg_ref[:, 0:D_MODEL])
        gb = _sigmoid(gtb_ref[...] + bg_ref[:, D_MODEL:2 * D_MODEL])
        merged = (ga * pa + gb * pb).astype(BF)
        out = x_ref[...] + _dot(merged, wout_ref[...])
        r2 = lax.rsqrt(jnp.mean(out * out, axis=-1, keepdims=True) + EPS)
        nrm = out * r2
        fg = fg_ref[...]
        err = nrm * fg - t_ref[...]
        loss_ref[...] += 0.5 * jnp.sum(jnp.mean(err * err, axis=-1, keepdims=True))
        dy = err * (1.0 / D_MODEL)
        gfg_ref[...] += jnp.sum(dy * nrm, axis=0, keepdims=True)
        dn = dy * fg
        d_out = r2 * (dn - nrm * jnp.mean(dn * nrm, axis=-1, keepdims=True))
        dout_ref[...] = d_out
        d_outb = d_out.astype(BF)
        acc_out[...] += _dot_tn(merged, d_outb)
        dm = _dot_nt(d_outb, wout_ref[...])
        d_pa = (dm * ga).astype(BF)
        d_pb = (dm * gb).astype(BF)
        d_gta = dm * pa * (ga * (1.0 - ga))
        d_gtb = dm * pb * (gb * (1.0 - gb))
        gbg_ref[:, 0:D_MODEL] += jnp.sum(d_gta, axis=0, keepdims=True)
        gbg_ref[:, D_MODEL:2 * D_MODEL] += jnp.sum(d_gtb, axis=0, keepdims=True)
        dzt_ref[:, 2 * D_A:2 * D_A + D_MODEL] = d_gta.astype(BF)
        dzt_ref[:, 2 * D_A + D_MODEL:] = d_gtb.astype(BF)
        acc_pa[...] += _dot_tn(ya, d_pa)
        acc_pb[...] += _dot_tn(yb, d_pb)
        d_ya = _dot_nt(d_pa, wpa_ref[...])
        d_yb = _dot_nt(d_pb, wpb_ref[...])
        d_att = (d_ya * sa).astype(BF)
        for hd in range(N_HEADS):
            datt_ref[hd] = d_att[:, hd * HEAD_DIM:(hd + 1) * HEAD_DIM]
        dzt_ref[:, 0:D_A] = (d_ya * att * dsa).astype(BF)
        dsg_ref[...] = d_yb * sb
        dzt_ref[:, D_A:2 * D_A] = (d_yb * sg * dsb).astype(BF)

        @pl.when(i == nt - 1)
        def _():
            cps = [pltpu.make_async_copy(acc_out, gwout_hbm, sems.at[0]),
                   pltpu.make_async_copy(acc_pa, gwpa_hbm, sems.at[1]),
                   pltpu.make_async_copy(acc_pb, gwpb_hbm, sems.at[2])]
            for cp in cps:
                cp.start()
            for cp in cps:
                cp.wait()

    c2 = lambda i: (0, 0)
    hbm = pl.BlockSpec(memory_space=pl.ANY)
    return pl.pallas_call(
        body, name="tail", grid=(nt,),
        out_shape=(jax.ShapeDtypeStruct((S, D_MODEL), F32), jax.ShapeDtypeStruct((N_HEADS, S, HEAD_DIM), BF),
                   jax.ShapeDtypeStruct((S, D_B), F32), jax.ShapeDtypeStruct((S, 3072), BF),
                   jax.ShapeDtypeStruct((D_MODEL, D_MODEL), F32), jax.ShapeDtypeStruct((D_A, D_MODEL), F32),
                   jax.ShapeDtypeStruct((D_B, D_MODEL), F32),
                   jax.ShapeDtypeStruct((1, 2 * D_MODEL), F32), jax.ShapeDtypeStruct((1, D_MODEL), F32),
                   jax.ShapeDtypeStruct((1, 128), F32)),
        in_specs=[pl.BlockSpec((tm, D_A), lambda i: (i, 0)),
                  pl.BlockSpec((tm, D_B), lambda i: (i, 0)),
                  pl.BlockSpec((tm, 512), lambda i: (i, 0)),
                  pl.BlockSpec((tm, 512), lambda i: (i, 3)),
                  pl.BlockSpec((tm, D_MODEL), lambda i: (i, 2)),
                  pl.BlockSpec((tm, D_MODEL), lambda i: (i, 3)),
                  pl.BlockSpec((tm, D_MODEL), lambda i: (i, 0)),
                  pl.BlockSpec((tm, D_MODEL), lambda i: (i, 0)),
                  pl.BlockSpec((D_A, D_MODEL), c2), pl.BlockSpec((D_B, D_MODEL), c2),
                  pl.BlockSpec((D_MODEL, D_MODEL), c2),
                  pl.BlockSpec((1, 2 * D_MODEL), c2), pl.BlockSpec((1, D_MODEL), c2)],
        out_specs=[pl.BlockSpec((tm, D_MODEL), lambda i: (i, 0)),
                   pl.BlockSpec((N_HEADS, tm, HEAD_DIM), lambda i: (0, i, 0)),
                   pl.BlockSpec((tm, D_B), lambda i: (i, 0)),
                   pl.BlockSpec((tm, 3072), lambda i: (i, 0)),
                   hbm, hbm, hbm,
                   pl.BlockSpec((1, 2 * D_MODEL), c2), pl.BlockSpec((1, D_MODEL), c2),
                   pl.BlockSpec((1, 128), c2)],
        scratch_shapes=[pltpu.VMEM((D_MODEL, D_MODEL), F32), pltpu.VMEM((D_A, D_MODEL), F32),
                        pltpu.VMEM((D_B, D_MODEL), F32), pltpu.SemaphoreType.DMA((3,))],
        compiler_params=_params(56, dimension_semantics=("arbitrary",)),
    )(att, sg, zrest, zrest, zrest, zrest, x, target, w_pa, w_pb, w_out, b_gate, final_g)


def _tail_sgu(att, zrest, x, target, w_pa, w_pb, w_out, b_gate, final_g, ln_g, ln_b, w_s, b_s_t, tm=256):
    S = x.shape[0]
    nt = S // tm
    chunks = tm // SGU_CHUNK

    def body(att_ref, ga_ref, ub_ref, vb_ref, gb_ref, gta_ref, gtb_ref, x_ref, t_ref,
             wpa_ref, wpb_ref, wout_ref, bg_ref, fg_ref, lg_ref, lb_ref, ws_ref, bst_ref,
             dout_ref, datt_ref, dzt_ref, dzs_ref, gwout_hbm, gwpa_hbm, gwpb_hbm,
             gbg_ref, gfg_ref, loss_ref, gws_ref, gbs_ref, glg_ref, glb_ref,
             acc_out, acc_pa, acc_pb, sg_scr, mix_scr, dvn_scr, bs_acc, sems):
        i = pl.program_id(0)

        @pl.when(i == 0)
        def _():
            for r in (acc_out, acc_pa, acc_pb, gbg_ref, gfg_ref, loss_ref, gws_ref, glg_ref, glb_ref, bs_acc):
                r[...] = jnp.zeros_like(r)

        u, du, dv, rstd, xh, vn = _sgu_core(ub_ref[...], vb_ref[...], lg_ref[...], lb_ref[...])
        vnb = vn.astype(BF)
        tri = _tri()
        blocks = [(g, slice(n * SGU_CHUNK, (n + 1) * SGU_CHUNK), slice(g * 128, (g + 1) * 128))
                  for g in range(N_GROUPS) for n in range(chunks)]
        wts = [jnp.where(tri, ws_ref[g], 0.0) for g in range(N_GROUPS)]
        for g, rs, cs in blocks:
            mixed = _dot(wts[g].astype(BF), vnb[rs, cs]) + bst_ref[:, g:g + 1]
            mix_scr[rs, cs] = mixed
            sg_scr[rs, cs] = u[rs, cs] * mixed

        att = att_ref[...]
        sg = sg_scr[...]
        sa, dsa = _silu_and_grad(ga_ref[...])
        sb, dsb = _silu_and_grad(gb_ref[...])
        ya = (att * sa).astype(BF)
        yb = (sg * sb).astype(BF)
        pa = _dot(ya, wpa_ref[...])
        pb = _dot(yb, wpb_ref[...])
        ga = _sigmoid(gta_ref[...] + bg_ref[:, 0:D_MODEL])
        gb = _sigmoid(gtb_ref[...] + bg_ref[:, D_MODEL:2 * D_MODEL])
        merged = (ga * pa + gb * pb).astype(BF)
        out = x_ref[...] + _dot(merged, wout_ref[...])
        r2 = lax.rsqrt(jnp.mean(out * out, axis=-1, keepdims=True) + EPS)
        nrm = out * r2
        fg = fg_ref[...]
        err = nrm * fg - t_ref[...]
        loss_ref[...] += 0.5 * jnp.sum(jnp.mean(err * err, axis=-1, keepdims=True))
        dy = err * (1.0 / D_MODEL)
        gfg_ref[...] += jnp.sum(dy * nrm, axis=0, keepdims=True)
        dn = dy * fg
        d_out = r2 * (dn - nrm * jnp.mean(dn * nrm, axis=-1, keepdims=True))
        dout_ref[...] = d_out
        d_outb = d_out.astype(BF)
        acc_out[...] += _dot_tn(merged, d_outb)
        dm = _dot_nt(d_outb, wout_ref[...])
        d_pa = (dm * ga).astype(BF)
        d_pb = (dm * gb).astype(BF)
        d_gta = dm * pa * (ga * (1.0 - ga))
        d_gtb = dm * pb * (gb * (1.0 - gb))
        gbg_ref[:, 0:D_MODEL] += jnp.sum(d_gta, axis=0, keepdims=True)
        gbg_ref[:, D_MODEL:2 * D_MODEL] += jnp.sum(d_gtb, axis=0, keepdims=True)
        dzt_ref[:, 2 * D_A:2 * D_A + D_MODEL] = d_gta.astype(BF)
        dzt_ref[:, 2 * D_A + D_MODEL:] = d_gtb.astype(BF)
        acc_pa[...] += _dot_tn(ya, d_pa)
        acc_pb[...] += _dot_tn(yb, d_pb)
        d_ya = _dot_nt(d_pa, wpa_ref[...])
        d_yb = _dot_nt(d_pb, wpb_ref[...])
        d_att = (d_ya * sa).astype(BF)
        for hd in range(N_HEADS):
            datt_ref[hd] = d_att[:, hd * HEAD_DIM:(hd + 1) * HEAD_DIM]
        dzt_ref[:, 0:D_A] = (d_ya * att * dsa).astype(BF)
        dzt_ref[:, D_A:2 * D_A] = (d_yb * sg * dsb).astype(BF)

        dsg = d_yb * sb
        dzs_ref[:, 0:D_B] = (dsg * mix_scr[...] * du).astype(BF)
        dmix = dsg * u
        for g, rs, cs in blocks:
            dmb = dmix[rs, cs].astype(BF)
            bs_acc[:, cs] += dmix[rs, cs]
            gws_ref[g] += _dot_nt(dmb, vnb[rs, cs])
            dvn_scr[rs, cs] = _dot(wts[g].T.astype(BF), dmb)
        dvn = dvn_scr[...]
        glg_ref[...] += jnp.sum(dvn * xh, axis=0, keepdims=True)
        glb_ref[...] += jnp.sum(dvn, axis=0, keepdims=True)
        dxh = dvn * lg_ref[...]
        dvv = rstd * (dxh - jnp.mean(dxh, axis=-1, keepdims=True)
                      - xh * jnp.mean(dxh * xh, axis=-1, keepdims=True))
        dzs_ref[:, D_B:2 * D_B] = (dvv * dv).astype(BF)

        @pl.when(i == nt - 1)
        def _():
            cps = [pltpu.make_async_copy(acc_out, gwout_hbm, sems.at[0]),
                   pltpu.make_async_copy(acc_pa, gwpa_hbm, sems.at[1]),
                   pltpu.make_async_copy(acc_pb, gwpb_hbm, sems.at[2])]
            for cp in cps:
                cp.start()
            lane = lax.broadcasted_iota(jnp.int32, (SGU_CHUNK, 128), 1)
            cols = jnp.zeros((SGU_CHUNK, 128), F32)
            for g in range(N_GROUPS):
                gws_ref[g] = jnp.where(tri, gws_ref[g], 0.0)
                col = jnp.sum(bs_acc[:, g * 128:(g + 1) * 128], axis=-1, keepdims=True)
                cols = jnp.where(lane == g, col, cols)
            gbs_ref[...] = cols
            for cp in cps:
                cp.wait()

    c2 = lambda i: (0, 0)
    c3 = lambda i: (0, 0, 0)
    zcol = lambda w, blk: pl.BlockSpec((tm, w), lambda i: (i, blk))
    row = lambda w: pl.BlockSpec((tm, w), lambda i: (i, 0))
    return pl.pallas_call(
        body, name="tail", grid=(nt,),
        out_shape=(jax.ShapeDtypeStruct((S, D_MODEL), F32), jax.ShapeDtypeStruct((N_HEADS, S, HEAD_DIM), BF),
                   jax.ShapeDtypeStruct((S, 3072), BF), jax.ShapeDtypeStruct((S, 2 * D_B), BF),
                   jax.ShapeDtypeStruct((D_MODEL, D_MODEL), F32), jax.ShapeDtypeStruct((D_A, D_MODEL), F32),
                   jax.ShapeDtypeStruct((D_B, D_MODEL), F32),
                   jax.ShapeDtypeStruct((1, 2 * D_MODEL), F32), jax.ShapeDtypeStruct((1, D_MODEL), F32),
                   jax.ShapeDtypeStruct((1, 128), F32),
                   jax.ShapeDtypeStruct((N_GROUPS, 128, 128), F32), jax.ShapeDtypeStruct((SGU_CHUNK, 128), F32),
                   jax.ShapeDtypeStruct((1, D_B), F32), jax.ShapeDtypeStruct((1, D_B), F32)),
        in_specs=[row(D_A), zcol(512, 0), zcol(512, 1), zcol(512, 2), zcol(512, 3),
                  zcol(D_MODEL, 2), zcol(D_MODEL, 3), row(D_MODEL), row(D_MODEL),
                  pl.BlockSpec((D_A, D_MODEL), c2), pl.BlockSpec((D_B, D_MODEL), c2),
                  pl.BlockSpec((D_MODEL, D_MODEL), c2),
                  pl.BlockSpec((1, 2 * D_MODEL), c2), pl.BlockSpec((1, D_MODEL), c2),
                  pl.BlockSpec((1, D_B), c2), pl.BlockSpec((1, D_B), c2),
                  pl.BlockSpec((N_GROUPS, 128, 128), c3), pl.BlockSpec((128, N_GROUPS), c2)],
        out_specs=[row(D_MODEL), pl.BlockSpec((N_HEADS, tm, HEAD_DIM), lambda i: (0, i, 0)),
                   row(3072), row(2 * D_B), _ANY, _ANY, _ANY,
                   pl.BlockSpec((1, 2 * D_MODEL), c2), pl.BlockSpec((1, D_MODEL), c2),
                   pl.BlockSpec((1, 128), c2),
                   pl.BlockSpec((N_GROUPS, 128, 128), c3), pl.BlockSpec((SGU_CHUNK, 128), c2),
                   pl.BlockSpec((1, D_B), c2), pl.BlockSpec((1, D_B), c2)],
        scratch_shapes=[pltpu.VMEM((D_MODEL, D_MODEL), F32), pltpu.VMEM((D_A, D_MODEL), F32),
                        pltpu.VMEM((D_B, D_MODEL), F32),
                        pltpu.VMEM((tm, D_B), F32), pltpu.VMEM((tm, D_B), F32), pltpu.VMEM((tm, D_B), F32),
                        pltpu.VMEM((SGU_CHUNK, D_B), F32), pltpu.SemaphoreType.DMA((3,))],
        compiler_params=_params(58, dimension_semantics=("arbitrary",)),
    )(att, zrest, zrest, zrest, zrest, zrest, zrest, x, target, w_pa, w_pb, w_out, b_gate, final_g,
      ln_g, ln_b, w_s, b_s_t)


_DZ_MAP = ((0, 0), (1, 0), (2, 0), (3, 0), (4, 0), (4, 1), (3, 1), (3, 2), (3, 3), (3, 4), (3, 5))


def _dh_gradx(dq, dk, dv, dzt, dzs, w_in_bf, x, norm_g, d_out, prev=None, whole=True, tm=512, after=()):
    S = x.shape[0]
    n_first = S // tm if whole else S // tm // 4
    nt = n_first if prev is None else S // tm - n_first
    first = 0 if prev is None else n_first
    n_in = 9 if prev is None else 11

    def body(dq_ref, dk_ref, dv_ref, dzt_ref, dzs_ref, w_ref, x_ref, g_ref, dout_ref, *rest):
        gx_ref, gn_ref = rest[-2:]
        i = pl.program_id(0)

        @pl.when(i == 0)
        def _():
            gn_ref[...] = jnp.zeros_like(gn_ref) if prev is None else rest[1][...]

        pieces = (dq_ref, dk_ref, dv_ref, dzt_ref, dzs_ref)
        dh = jnp.zeros((tm, D_MODEL), F32)
        for j, (pc, blk) in enumerate(_DZ_MAP):
            dh += _dot_nt(pieces[pc][:, blk * 512:(blk + 1) * 512], w_ref[:, j * 512:(j + 1) * 512])
        xv = x_ref[...]
        r = lax.rsqrt(jnp.mean(xv * xv, axis=-1, keepdims=True) + EPS)
        nrm = xv * r
        gn_ref[...] += jnp.sum(dh * nrm, axis=0, keepdims=True)
        dn = dh * g_ref[...]
        gx_ref[...] = r * (dn - nrm * jnp.mean(dn * nrm, axis=-1, keepdims=True)) + dout_ref[...]

    row = lambda w: pl.BlockSpec((tm, w), lambda i: (i + first, 0))
    c2 = lambda i: (0, 0)
    more = [] if prev is None else [_ANY, pl.BlockSpec((1, D_MODEL), c2)]
    return pl.pallas_call(
        _after(body, n_in, after), name="dh_gradx_a" if prev is None else "dh_gradx_b", grid=(nt,),
        out_shape=(jax.ShapeDtypeStruct((S, D_MODEL), F32), jax.ShapeDtypeStruct((1, D_MODEL), F32)),
        in_specs=[row(512), row(512), row(512), row(3072), row(1024),
                  pl.BlockSpec((D_MODEL, D_IN), c2, pipeline_mode=pl.Buffered(1)), row(D_MODEL),
                  pl.BlockSpec((1, D_MODEL), c2), row(D_MODEL)]
        + more + [_ANY] * len(after),
        out_specs=[row(D_MODEL), pl.BlockSpec((1, D_MODEL), c2)],
        input_output_aliases={} if prev is None else {9: 0},
        compiler_params=_params(48, dimension_semantics=("arbitrary",)),
    )(dq, dk, dv, dzt, dzs, w_in_bf, x, norm_g, d_out, *(prev or ()), *after)


def _gw_in(ht, dq, dk, dv, dzt, dzs, tn=256, after=()):
    S = ht.shape[1]
    per = 512 // tn
    cols = tuple((pc, per * blk + h) for pc, blk in _DZ_MAP for h in range(per))

    def body(ht_ref, dq_ref, dk_ref, dv_ref, dzt_ref, dzs_ref, o_ref, ob_ref):
        j = pl.program_id(0)
        pieces = (dq_ref, dk_ref, dv_ref, dzt_ref, dzs_ref)
        for pc in range(5):
            hit = functools.reduce(jnp.logical_or, [j == jj for jj, (p, _) in enumerate(cols) if p == pc])

            @pl.when(hit)
            def _(pc=pc):
                g = _dot(ht_ref[...], pieces[pc][...])
                o_ref[...] = g
                ob_ref[...] = g.astype(BF)

    def piece_spec(pc):
        cur = next(blk for p, blk in cols if p == pc)
        held = []
        for p, blk in cols:
            cur = blk if p == pc else cur
            held.append(cur)

        def index_map(j):
            blk = jnp.int32(held[0])
            for jj in range(1, len(held)):
                if held[jj] != held[jj - 1]:
                    blk = jnp.where(j >= jj, jnp.int32(held[jj]), blk)
            return (0, blk)

        return pl.BlockSpec((S, tn), index_map)

    return pl.pallas_call(
        _after(body, 6, after), name="gw_in", grid=(len(cols),),
        out_shape=(jax.ShapeDtypeStruct((D_MODEL, D_IN), F32), jax.ShapeDtypeStruct((D_MODEL, D_IN), BF)),
        in_specs=[pl.BlockSpec((D_MODEL, S), lambda j: (0, 0))] + [piece_spec(pc) for pc in range(5)]
        + [_ANY] * len(after),
        out_specs=[pl.BlockSpec((D_MODEL, tn), lambda j: (0, j)), pl.BlockSpec((D_MODEL, tn), lambda j: (0, j))],
        compiler_params=_params(48, dimension_semantics=("arbitrary",)),
    )(ht, dq, dk, dv, dzt, dzs, *after)


_HBM = pl.BlockSpec(memory_space=pltpu.HBM)
_SEM = pl.BlockSpec(memory_space=pltpu.SEMAPHORE)
_ANY = pl.BlockSpec(memory_space=pl.ANY)
_EFFECT = pltpu.SideEffectType.DATAFLOW_SIDE_EFFECTING


def _in_hbm(a):
    return pltpu.with_memory_space_constraint(a, pltpu.HBM)


def _after(body, n_in, after):
    if not after:
        return body
    return lambda *refs: body(*refs[:n_in], *refs[n_in + len(after):])


class _Started:
    def __init__(self, send, recv, bufs, token):
        self.send, self.recv, self.bufs, self.token = send, recv, bufs, token


def _split_start(name, bufs, n_copies, copies, after=()):
    nb = len(bufs)

    def body(*refs):
        refs = refs[:nb] + refs[nb + len(after):]
        for cp in copies(refs[:nb], refs[nb], refs[nb + 1]):
            cp.start()
        refs[-1][...] = jnp.zeros_like(refs[-1])

    outs = pl.pallas_call(
        body, name=name,
        out_shape=(pltpu.SemaphoreType.DMA((n_copies,)), pltpu.SemaphoreType.DMA((n_copies,)),
                   *[pltpu.HBM(b.shape, b.dtype) for b in bufs], jax.ShapeDtypeStruct((8, 128), F32)),
        in_specs=[_HBM] * nb + [_ANY] * len(after),
        out_specs=(_SEM, _SEM, *[_HBM] * nb, pl.BlockSpec(memory_space=pltpu.VMEM)),
        input_output_aliases={k: 2 + k for k in range(nb)},
        compiler_params=_params(1, has_side_effects=_EFFECT),
    )(*[_in_hbm(b) for b in bufs], *after)
    return _Started(outs[0], outs[1], list(outs[2:2 + nb]), outs[-1])


def _split_wait(name, started, copies, after):
    nb = len(started.bufs)

    def body(*refs):
        for cp in copies(refs[:nb], refs[nb], refs[nb + 1]):
            cp.wait_send()
            cp.wait_recv()

    return list(pl.pallas_call(
        body, name=name,
        out_shape=tuple(pltpu.HBM(b.shape, b.dtype) for b in started.bufs),
        in_specs=[_HBM] * nb + [_SEM, _SEM, _ANY],
        out_specs=tuple([_HBM] * nb),
        input_output_aliases={k: k for k in range(nb)},
        compiler_params=_params(1, has_side_effects=_EFFECT),
    )(*started.bufs, started.send, started.recv, after))


def _x1_copies(ws):
    def copies(refs, send_sems, recv_sems):
        x, y, c, _ = _mesh_pos()
        out = []
        for k, w in enumerate(ws):
            for s in range(N_SHARD):
                out.append(pltpu.make_async_remote_copy(
                    src_ref=_UNITS[w](refs[k], s, 1 - c), dst_ref=refs[len(ws) + k].at[s],
                    send_sem=send_sems.at[N_SHARD * k + s], recv_sem=recv_sems.at[N_SHARD * k + s],
                    device_id=(x, y, 1 - c), device_id_type=MESH))
        return out
    return copies


def _x2_copies(n):
    def copies(refs, send_sems, recv_sems):
        x, y, c, chips = _mesh_pos()
        out = []
        for j, (cx, cy) in enumerate(chips):
            for k in range(n):
                out.append(pltpu.make_async_remote_copy(
                    src_ref=refs[k].at[2 * cx + cy], dst_ref=refs[n + k].at[j],
                    send_sem=send_sems.at[3 * k + j], recv_sem=recv_sems.at[3 * k + j],
                    device_id=(cx, cy, c), device_id_type=MESH))
        return out
    return copies


def _x3_copies(ws):
    def copies(refs, send_sems, recv_sems):
        x, y, c, _ = _mesh_pos()
        out = []
        for k, w in enumerate(ws):
            rows = _HALF_ROWS[w]
            mine = refs[k].at[pl.ds(_mo(c * rows, rows), rows), :]
            out.append(pltpu.make_async_remote_copy(
                src_ref=mine, dst_ref=mine, send_sem=send_sems.at[k], recv_sem=recv_sems.at[k],
                device_id=(x, y, 1 - c), device_id_type=MESH))
        return out
    return copies


def _x1_lands(ws, dtype=F32):
    return [lax.empty((N_SHARD,) + _UNIT_SHAPES[w], dtype) for w in ws]


def _x2_lands(ws):
    return [lax.empty((3,) + _UNIT_SHAPES[w], BF) for w in ws]


def _grad_add1(w, g, recv, pos):
    ur, uc = _UNIT_SHAPES[w]
    if w == 3:
        g_map = lambda s, pos: (2 * s + pos[0], 0)
    else:
        g_map = lambda s, pos: (pos[0], s)

    def body(pos_ref, g_ref, r_ref, cs_ref, csb_ref):
        v = g_ref[...] + r_ref[0].astype(F32)
        cs_ref[0] = v
        csb_ref[0] = v.astype(BF)

    u3 = lambda s, pos: (s, 0, 0)
    return pl.pallas_call(
        body, name=f"grad_add1_{w}",
        grid_spec=pltpu.PrefetchScalarGridSpec(
            num_scalar_prefetch=1, grid=(N_SHARD,),
            in_specs=[pl.BlockSpec((ur, uc), g_map), pl.BlockSpec((1, ur, uc), u3)],
            out_specs=[pl.BlockSpec((1, ur, uc), u3), pl.BlockSpec((1, ur, uc), u3)]),
        out_shape=(jax.ShapeDtypeStruct((N_SHARD, ur, uc), F32), jax.ShapeDtypeStruct((N_SHARD, ur, uc), BF)),
        compiler_params=_params(40, dimension_semantics=("arbitrary",)),
    )(pos, g, recv)


def _grad_add1_group(ws, gs, recvs):
    n = len(ws)

    def body(*refs):
        c = lax.axis_index("c")
        for k, w in enumerate(ws):
            g, r, cs, csb = refs[k], refs[n + k], refs[2 * n + k], refs[3 * n + k]
            for s in range(N_SHARD):
                v = _UNITS[w](g, s, c)[...] + r[s]
                cs[s] = v
                csb[s] = v.astype(BF)

    vm = pl.BlockSpec(memory_space=pltpu.VMEM)
    outs = pl.pallas_call(
        body, name="grad_add1_group",
        out_shape=tuple(jax.ShapeDtypeStruct((N_SHARD,) + _UNIT_SHAPES[w], dt) for dt in (F32, BF) for w in ws),
        in_specs=[vm] * (2 * n), out_specs=[vm] * (2 * n),
        compiler_params=_params(32),
    )(*gs, *recvs)
    return list(outs[:n]), list(outs[n:])


def _grad_add2_group(ws, css, recvs):
    n = len(ws)

    def body(*refs):
        x, y, c, _ = _mesh_pos()
        for k, w in enumerate(ws):
            cs, r, o = refs[k], refs[n + k], refs[2 * n + k]
            rows = _HALF_ROWS[w]
            total = ((cs[2 * x + y] + r[0].astype(F32)) + r[1].astype(F32)) + r[2].astype(F32)
            o[pl.ds(_mo(c * rows, rows), rows), :] = total

    vm = pl.BlockSpec(memory_space=pltpu.VMEM)
    return list(pl.pallas_call(
        body, name="grad_add2_group",
        out_shape=tuple(jax.ShapeDtypeStruct(_SHARD_SHAPES[w], F32) for w in ws),
        in_specs=[vm] * (2 * n), out_specs=[vm] * n,
        compiler_params=_params(32),
    )(*css, *recvs))


def _grad_add2(w, cs, recv, pos):
    ur, uc = _UNIT_SHAPES[w]
    tr = ur // 4 if w == 0 else ur
    nt = ur // tr

    def body(pos_ref, cs_ref, r_ref, o_ref):
        o_ref[...] = ((cs_ref[0] + r_ref[0].astype(F32)) + r_ref[1].astype(F32)) + r_ref[2].astype(F32)

    return pl.pallas_call(
        body, name=f"grad_add2_{w}",
        grid_spec=pltpu.PrefetchScalarGridSpec(
            num_scalar_prefetch=1, grid=(nt,),
            in_specs=[pl.BlockSpec((1, tr, uc), lambda t, pos: (pos[1], t, 0)),
                      pl.BlockSpec((3, tr, uc), lambda t, pos: (0, t, 0))],
            out_specs=pl.BlockSpec((tr, uc), lambda t, pos: (pos[0] * nt + t, 0))),
        out_shape=jax.ShapeDtypeStruct(_SHARD_SHAPES[w], F32),
        compiler_params=_params(32, dimension_semantics=("arbitrary",)),
    )(pos, cs, recv)


def _grad_xchg3(ws, halves):
    n = len(ws)

    def body(*refs):
        cps = _x3_copies(ws)(refs[:n], refs[2 * n], refs[2 * n + 1])
        for cp in cps:
            cp.start()
        for cp in cps:
            cp.wait()

    return pl.pallas_call(
        body, name="grad_xchg3",
        out_shape=tuple(jax.ShapeDtypeStruct(_SHARD_SHAPES[w], F32) for w in ws),
        in_specs=[_ANY] * n, out_specs=[_ANY] * n,
        input_output_aliases={k: k for k in range(n)},
        scratch_shapes=[pltpu.SemaphoreType.DMA((n,)), pltpu.SemaphoreType.DMA((n,))],
        compiler_params=_params(16),
    )(*halves)


def _adamw_math(w, g, m, v):
    m = ADAM_B1 * m + (1.0 - ADAM_B1) * g
    v = ADAM_B2 * v + (1.0 - ADAM_B2) * (g * g)
    m_hat = m / ADAM_C1
    v_hat = v / ADAM_C2
    delta = -ADAM_LR * (m_hat / (jnp.sqrt(v_hat) + ADAM_EPS) + ADAM_WD * w)
    return delta, m, v


def _adamw_group(ws_, gs, ms, vs, after=()):
    n = len(ws_)

    def body(*refs):
        for k in range(n):
            w, g, m, v = (refs[j * n + k] for j in range(4))
            d, nm, nv, gc = (refs[(4 + j) * n + k] for j in range(4))
            gv = g[...]
            d[...], nm[...], nv[...] = _adamw_math(w[...], gv, m[...], v[...])
            gc[...] = gv

    vm = pl.BlockSpec(memory_space=pltpu.VMEM)
    outs = pl.pallas_call(
        _after(body, 4 * n, after), name="adamw_group",
        out_shape=tuple(jax.ShapeDtypeStruct(a.shape, F32) for _ in range(4) for a in ws_),
        in_specs=[vm] * (4 * n) + [_ANY] * len(after), out_specs=[vm] * (4 * n),
        compiler_params=_params(32),
    )(*ws_, *gs, *ms, *vs, *after)
    return [tuple(outs[j * n + k] for j in range(4)) for k in range(n)]


def _adamw(name, w, g, m, v, tr=256, after=()):
    rows, cols = w.shape

    def body(w_ref, g_ref, m_ref, v_ref, d_ref, nm_ref, nv_ref, gc_ref):
        gv = g_ref[...]
        d_ref[...], nm_ref[...], nv_ref[...] = _adamw_math(w_ref[...], gv, m_ref[...], v_ref[...])
        gc_ref[...] = gv

    spec = pl.BlockSpec((tr, cols), lambda i: (i, 0))
    return pl.pallas_call(
        _after(body, 4, after), name=name, grid=(rows // tr,),
        out_shape=tuple(jax.ShapeDtypeStruct((rows, cols), F32) for _ in range(4)),
        in_specs=[spec] * 4 + [_ANY] * len(after), out_specs=[spec] * 4,
        compiler_params=_params(32, dimension_semantics=("arbitrary",)),
    )(w, g, m, v, *after)


_REL_PAD = 384
_VEC_FIELDS = (("norm_g", 0, D_MODEL), ("b_gate", 1024, 2 * D_MODEL), ("sgu_ln_g", 3072, D_B),
               ("sgu_ln_b", 3584, D_B), ("b_s", 4096, N_GROUPS * 128), ("final_g", 4608, D_MODEL))
_LOSS_OFF = 5632
_REL_OFF = 5760
_NV = _REL_OFF + N_HEADS * _REL_PAD
_N_FIELDS = len(_VEC_FIELDS) + 2


_B_S_FIELD = [f[0] for f in _VEC_FIELDS].index("b_s")


def _assemble_row(dst, fields, transposed_b_s):
    for f, (_, off, n) in enumerate(_VEC_FIELDS):
        if transposed_b_s and f == _B_S_FIELD:
            t = fields[f][...].T
            for g in range(N_GROUPS):
                dst[:, off + 128 * g:off + 128 * (g + 1)] = t[g:g + 1, :]
        else:
            dst[:, off:off + n] = fields[f][...]
    for r in range(N_HEADS):
        dst[:, _REL_OFF + _REL_PAD * r:_REL_OFF + _REL_PAD * (r + 1)] = fields[len(_VEC_FIELDS)][r:r + 1, :]


def _small_reduce(grads, loss_row, after=()):
    n_in = _N_FIELDS + 1

    def body(*refs):
        g_refs, loss_ref = refs[:_N_FIELDS], refs[_N_FIELDS]
        out_v, out_w = refs[n_in:n_in + 2]
        mine_v, gath_v, gath_w, send_sems, recv_sems = refs[n_in + 2:]
        x, y, c, chips = _mesh_pos()
        me, sibling = (x, y, c), (x, y, 1 - c)

        _assemble_row(mine_v, g_refs, True)
        mine_v[:, _LOSS_OFF:_LOSS_OFF + 128] = loss_ref[...]
        mine_w = g_refs[-1]
        my_k = 4 * x + 2 * y + c
        gath_v[my_k] = mine_v[...]
        gath_w[my_k] = mine_w[...]

        def copy(k, gath, block, to, src=None):
            dst = gath.at[4 * block[0] + 2 * block[1] + block[2]]
            return pltpu.make_async_remote_copy(
                src_ref=dst if src is None else src, dst_ref=dst,
                send_sem=send_sems.at[k], recv_sem=recv_sems.at[k], device_id=to, device_id_type=MESH)

        bufs = ((gath_v, mine_v), (gath_w, mine_w))
        first, passed = [], []
        for b, (gath, mine) in enumerate(bufs):
            first.append(copy(7 * b, gath, me, sibling, src=mine))
            first += [copy(7 * b + 1 + j, gath, me, (*chip, c), src=mine) for j, chip in enumerate(chips)]
        for cp in first:
            cp.start()
        for b, (gath, _) in enumerate(bufs):
            for j, chip in enumerate(chips):
                copy(7 * b + 1 + j, gath, (*chip, c), me).wait_recv()
                cp = copy(7 * b + 4 + j, gath, (*chip, c), sibling)
                cp.start()
                passed.append(cp)
        for b, (gath, _) in enumerate(bufs):
            copy(7 * b, gath, sibling, me).wait_recv()
            for j, chip in enumerate(chips):
                copy(7 * b + 4 + j, gath, (*chip, 1 - c), me).wait_recv()
        for cp in first + passed:
            cp.wait_send()

        tot_v, tot_w = gath_v[0], gath_w[0]
        for k in range(1, 8):
            tot_v = tot_v + gath_v[k]
            tot_w = tot_w + gath_w[k]
        out_v[...] = tot_v
        out_w[...] = tot_w

    vm = pl.BlockSpec(memory_space=pltpu.VMEM)
    return pl.pallas_call(
        _after(body, n_in, after), name="small_reduce",
        out_shape=(jax.ShapeDtypeStruct((1, _NV), F32), jax.ShapeDtypeStruct((N_GROUPS * 128, 128), F32)),
        in_specs=[vm] * n_in + [_ANY] * len(after), out_specs=[vm] * 2,
        scratch_shapes=[pltpu.VMEM((1, _NV), F32), pltpu.VMEM((8, 1, _NV), F32),
                        pltpu.VMEM((8, N_GROUPS * 128, 128), F32),
                        pltpu.SemaphoreType.DMA((14,)), pltpu.SemaphoreType.DMA((14,))],
        compiler_params=_params(32),
    )(*grads, loss_row, *after)


def _small_adamw(tot_v, tot_w, params):
    n_in = 2 + 3 * _N_FIELDS

    def body(*refs):
        tv_ref, tw_ref = refs[:2]
        p_refs = [refs[2 + k * _N_FIELDS:2 + (k + 1) * _N_FIELDS] for k in range(3)]
        outs = refs[n_in:n_in + 4 * _N_FIELDS + 1]
        wmv = refs[-1]
        for k in range(3):
            _assemble_row(wmv.at[k], p_refs[k], False)
            wmv[k, :, _LOSS_OFF:_LOSS_OFF + 128] = jnp.zeros((1, 128), F32)
        tot_v, tot_w = tv_ref[...], tw_ref[...]
        res_v = (tot_v,) + _adamw_math(wmv[0], tot_v, wmv[1], wmv[2])
        res_w = (tot_w,) + _adamw_math(p_refs[0][-1][...], tot_w, p_refs[1][-1][...], p_refs[2][-1][...])
        for kind in range(4):
            o = outs[kind * _N_FIELDS:(kind + 1) * _N_FIELDS]
            for f, (_, off, n) in enumerate(_VEC_FIELDS):
                o[f][...] = res_v[kind][:, off:off + n]
            for r in range(N_HEADS):
                o[len(_VEC_FIELDS)][r:r + 1, :] = res_v[kind][:, _REL_OFF + _REL_PAD * r:_REL_OFF + _REL_PAD * (r + 1)]
            o[-1][...] = res_w[kind]
        outs[-1][...] = tot_v[:, _LOSS_OFF:_LOSS_OFF + 128]

    field_shapes = [(1, n) for _, _, n in _VEC_FIELDS] + [(N_HEADS, _REL_PAD), (N_GROUPS * 128, 128)]
    vm = pl.BlockSpec(memory_space=pltpu.VMEM)
    operands = [tot_v, tot_w] + [a for p in params for a in p]
    assert len(operands) == n_in
    outs = pl.pallas_call(
        body, name="small_adamw",
        out_shape=tuple(jax.ShapeDtypeStruct(s, F32) for _ in range(4) for s in field_shapes)
        + (jax.ShapeDtypeStruct((1, 128), F32),),
        in_specs=[vm] * n_in, out_specs=[vm] * (4 * _N_FIELDS + 1),
        scratch_shapes=[pltpu.VMEM((3, 1, _NV), F32)],
        compiler_params=_params(32),
    )(*operands)
    return [outs[k * _N_FIELDS:(k + 1) * _N_FIELDS] for k in range(4)], outs[-1]


def _small_fields(norm_g, b_gate, ln_g, ln_b, b_s, final_g, rel_bias, w_s):
    rel = jnp.pad(rel_bias.reshape(N_HEADS, N_REL), ((0, 0), (0, _REL_PAD - N_REL)))
    return (norm_g, b_gate, ln_g, ln_b, b_s.reshape(1, N_GROUPS * 128), final_g.reshape(1, D_MODEL),
            rel, w_s.reshape(N_GROUPS * 128, 128))


def _small_outputs(fields):
    n_g, b_g, l_g, l_b, b_s, f_g, rel, w_s = fields
    return (n_g, b_g, rel[:, :N_REL].reshape(1, N_HEADS, N_REL), l_g, l_b,
            w_s.reshape(1, N_GROUPS, 128, 128), b_s.reshape(1, N_GROUPS, 128), f_g.reshape(D_MODEL))


def _bias_row(rel_bias):
    hi = rel_bias[:, N_REL - 1:N_REL]
    lo = rel_bias[:, 0:1]
    return jnp.concatenate([jnp.broadcast_to(hi, (N_HEADS, 384)), rel_bias[:, ::-1],
                            jnp.broadcast_to(lo, (N_HEADS, 191)), jnp.broadcast_to(hi, (N_HEADS, 192))], axis=1)


def kernel(x, norm_g, w_in, b_gate, rel_bias, sgu_ln_g, sgu_ln_b, w_s, b_s, w_pa, w_pb, w_out, final_g, loss_target, m_norm_g, m_w_in, m_b_gate, m_rel_bias, m_sgu_ln_g, m_sgu_ln_b, m_w_s, m_b_s, m_w_pa, m_w_pb, m_w_out, m_final_g, v_norm_g, v_w_in, v_b_gate, v_rel_bias, v_sgu_ln_g, v_sgu_ln_b, v_w_s, v_b_s, v_w_pa, v_w_pb, v_w_out, v_final_g):
    S = x.shape[1]
    xs = x.reshape(S, D_MODEL)
    tgt = loss_target.reshape(S, D_MODEL)
    big_w = (w_in[0], w_pa[0], w_pb[0], w_out[0])
    big_m = (m_w_in[0], m_w_pa[0], m_w_pb[0], m_w_out[0])
    big_v = (v_w_in[0], v_w_pa[0], v_w_pb[0], v_w_out[0])
    rel = rel_bias[0]
    ws = w_s[0]
    bst = b_s[0].T
    fg = final_g.reshape(1, D_MODEL)
    pos = jnp.stack([lax.axis_index("c"), 2 * lax.axis_index("x") + lax.axis_index("y")]).astype(jnp.int32)

    staged = _stage_weights((1, 2, 3), big_w[1:], pos)
    w_in_bf, = _ag_weights((0,), big_w[:1])
    ag_s = _split_start("ag_small_start", staged, 9, _gather_copies((1, 2, 3)), after=(w_in_bf,))

    ht, q3, k3, v3, zrest = _inproj_fwd(xs, norm_g, w_in_bf, after=(ag_s.token,))
    gp = _bias_row(rel)
    att, lse, band_bias = _attn_fwd(q3, k3, v3, gp)
    w_pa_bf, w_pb_bf, w_out_bf = _split_wait("ag_small_wait", ag_s, _gather_copies((1, 2, 3)), att)
    (d_out, d_att, dzt, dzs, gw_out, gw_pa, gw_pb, g_bgate, g_final, loss_row,
     g_ws, g_bs_t, g_lng, g_lnb) = _tail_sgu(
        att, zrest, xs, tgt, w_pa_bf, w_pb_bf, w_out_bf, b_gate, fg, sgu_ln_g, sgu_ln_b, ws, bst)
    ws_s, ws_i = (1, 2, 3), (0,)
    names = ("adamw_w_in", "adamw_w_pa", "adamw_w_pb", "adamw_w_out")

    x1s = _split_start("gx1s_start", [gw_pa, gw_pb, gw_out] + _x1_lands(ws_s), 12, _x1_copies(ws_s))
    dq, dk, dv, d_gp = _attn_bwd(q3, k3, v3, d_att, lse, band_bias, after=(x1s.token,))
    got = _split_wait("gx1s_wait", x1s, _x1_copies(ws_s), dq)
    cs_s, csb_s = _grad_add1_group(ws_s, got[:3], got[3:])

    x2s = _split_start("gx2s_start", csb_s + _x2_lands(ws_s), 9, _x2_copies(3))
    gw_in, gw_in_bf = _gw_in(ht, dq, dk, dv, dzt, dzs, after=(x2s.token,))
    x1i = _split_start("gx1i_start", [gw_in_bf] + _x1_lands(ws_i, BF), 4, _x1_copies(ws_i))
    got = _split_wait("gx2s_wait", x2s, _x2_copies(3), x1i.token)
    halves_s = _grad_add2_group(ws_s, cs_s, got[3:])
    x3s = _split_start("gx3s_start", halves_s, 3, _x3_copies(ws_s))
    got = _split_wait("gx1i_wait", x1i, _x1_copies(ws_i), x3s.token)
    sum_i = _grad_add1(0, gw_in, got[1], pos)

    x2i = _split_start("gx2i_start", [sum_i[1]] + _x2_lands(ws_i), 3, _x2_copies(1))
    grad_x, g_norm = _dh_gradx(dq, dk, dv, dzt, dzs, w_in_bf, xs, norm_g, d_out, after=(x2i.token,))
    g_shards_s = _split_wait("gx3s_wait", x3s, _x3_copies(ws_s), grad_x)
    big = [None] * 4
    big[1:] = _adamw_group(big_w[1:], g_shards_s, big_m[1:], big_v[1:], after=(x2i.token,))

    g_rel = jnp.pad(d_gp[:, 384:384 + N_REL][:, ::-1], ((0, 0), (0, _REL_PAD - N_REL)))
    small_grads = (g_norm, g_bgate, g_lng, g_lnb, g_bs_t, g_final, g_rel, g_ws.reshape(N_GROUPS * 128, 128))
    small_params = (_small_fields(norm_g, b_gate, sgu_ln_g, sgu_ln_b, b_s, final_g, rel_bias, w_s),
                    _small_fields(m_norm_g, m_b_gate, m_sgu_ln_g, m_sgu_ln_b, m_b_s, m_final_g, m_rel_bias, m_w_s),
                    _small_fields(v_norm_g, v_b_gate, v_sgu_ln_g, v_sgu_ln_b, v_b_s, v_final_g, v_rel_bias, v_w_s))
    tot_v, tot_w = _small_reduce(small_grads, loss_row, after=(x2i.token,))
    (gsum, sdelta, sm, sv), loss_out = _small_adamw(tot_v, tot_w, small_params)

    got = _split_wait("gx2i_wait", x2i, _x2_copies(1), loss_out)
    half_i = _grad_add2(0, sum_i[0], got[1], pos)
    g_shard_i, = _grad_xchg3(ws_i, [half_i])
    big[0] = _adamw(names[0], big_w[0], g_shard_i, big_m[0], big_v[0])
    sg_out, sd_out, sm_out, sv_out = (_small_outputs(f) for f in (gsum, sdelta, sm, sv))
    loss = loss_out[0, 0]

    def assemble(small, bigs):
        n_g, b_g, r_b, l_g, l_b, w_s_, b_s_, f_g = small
        b_in, b_pa, b_pb, b_out = (b[None] for b in bigs)
        return (n_g, b_in, b_g, r_b, l_g, l_b, w_s_, b_s_, b_pa, b_pb, b_out, f_g)

    grads_out = assemble(sg_out, [b[3] for b in big])
    delta_out = assemble(sd_out, [b[0] for b in big])
    m_out = assemble(sm_out, [b[1] for b in big])
    v_out = assemble(sv_out, [b[2] for b in big])
    return (loss, grad_x.reshape(1, S, D_MODEL), *grads_out, *delta_out, *m_out, *v_out)
```

```python
import functools
import math

import jax
import jax.numpy as jnp
from jax import lax
from jax.experimental import pallas as pl
from jax.experimental.pallas import tpu as pltpu

F32 = jnp.float32
BF = jnp.bfloat16
MESH = pl.DeviceIdType.MESH

D_MODEL = 1024
D_A = 512
D_B = 512
D_IN = 5632
N_HEADS = 8
HEAD_DIM = 64
CHUNK = 64
N_PREV = 8
SGU_CHUNK = 128
N_GROUPS = 4
N_REL = 257
EPS = 1e-6
NEG_INF = -1e30
SCALE = HEAD_DIM ** -0.5

QB = 2 * CHUNK
KB = (N_PREV + 2) * CHUNK
PADK = N_PREV * CHUNK
ROLL_W = 1024
N_RING = KB // QB
KEEP = N_RING - 1

ADAM_LR = 0.001
ADAM_B1 = 0.9
ADAM_B2 = 0.999
ADAM_EPS = 1e-08
ADAM_WD = 0.01
ADAM_STEP = 10
ADAM_C1 = 1.0 - ADAM_B1 ** ADAM_STEP
ADAM_C2 = 1.0 - ADAM_B2 ** ADAM_STEP

N_SHARD = 4
SHARD_IN = D_IN // N_SHARD
MIB = 1024 * 1024


VMEM_RESERVE_MIB = 60


def _params(vmem_mib, **kw):
    assert vmem_mib <= VMEM_RESERVE_MIB
    return pltpu.CompilerParams(vmem_limit_bytes=VMEM_RESERVE_MIB * MIB, **kw)


def _sigmoid(x):
    return 1.0 / (1.0 + jnp.exp(-x))


def _silu_and_grad(x):
    s = _sigmoid(x)
    return x * s, s * (1.0 + x * (1.0 - s))


_GELU_C = math.sqrt(2.0 / math.pi)
_GELU_A = 0.044715


def _gelu_and_grad(x):
    x2 = x * x
    t = jnp.tanh(_GELU_C * (x + _GELU_A * (x2 * x)))
    cdf = 0.5 * (1.0 + t)
    grad = cdf + 0.5 * x * (1.0 - t * t) * (_GELU_C * (1.0 + 3.0 * _GELU_A * x2))
    return x * cdf, grad


def _dot(a, b):
    return jnp.dot(a, b, preferred_element_type=F32)


def _dot_nt(a, b):
    return lax.dot_general(a, b, (((1,), (1,)), ((), ())), preferred_element_type=F32)


def _dot_tn(a, b):
    return lax.dot_general(a, b, (((0,), (0,)), ((), ())), preferred_element_type=F32)


def _mo(v, m):
    return v if isinstance(v, int) else pl.multiple_of(v, m)


def _unit_in(ref, s, p):
    return ref.at[pl.ds(_mo(p * 512, 512), 512), pl.ds(_mo(s * SHARD_IN, 128), SHARD_IN)]


def _unit_p(ref, s, p):
    return ref.at[pl.ds(_mo(p * 256, 256), 256), pl.ds(_mo(s * 256, 128), 256)]


def _unit_out(ref, s, p):
    return ref.at[pl.ds(_mo(s * 256 + p * 128, 128), 128), :]


_UNITS = (_unit_in, _unit_p, _unit_p, _unit_out)
_HALF_ROWS = (512, 256, 256, 128)
_UNIT_SHAPES = ((512, SHARD_IN), (256, 256), (256, 256), (128, D_MODEL))
_FULL_SHAPES = ((D_MODEL, D_IN), (D_A, D_MODEL), (D_B, D_MODEL), (D_MODEL, D_MODEL))
_SHARD_SHAPES = ((D_MODEL, SHARD_IN), (D_A, 256), (D_B, 256), (256, D_MODEL))


def _mesh_pos():
    x, y, c = lax.axis_index("x"), lax.axis_index("y"), lax.axis_index("c")
    chips = [(1 - x, y), (x, 1 - y), (1 - x, 1 - y)]
    return x, y, c, chips


def _ag_weights(ws, shards):
    n = len(ws)

    def body(*refs):
        ins, outs, stage = refs[:n], refs[n:2 * n], refs[2 * n:3 * n]
        send_sems, recv_sems, local_sems = refs[3 * n:]
        x, y, c, chips = _mesh_pos()
        s_me = 2 * x + y
        sibling = (x, y, 1 - c)
        for k in range(n):
            stage[k][...] = ins[k][...].astype(BF)

        def half(k, p):
            rows = _HALF_ROWS[ws[k]]
            return stage[k].at[pl.ds(_mo(p * rows, rows), rows), :]

        def unit(k, s, p):
            return _UNITS[ws[k]](outs[k], s, p)

        local = []
        for k in range(n):
            for p in range(2):
                cp = pltpu.make_async_copy(half(k, p), unit(k, s_me, p), local_sems.at[k, p])
                cp.start()
                local.append(cp)

        def rcopy(k, i, src, dst, to):
            return pltpu.make_async_remote_copy(src_ref=src, dst_ref=dst, send_sem=send_sems.at[k, i],
                                                recv_sem=recv_sems.at[k, i], device_id=to, device_id_type=MESH)

        sends = []
        for j, (cx, cy) in enumerate(chips):
            for k in range(n):
                cp = rcopy(k, j, half(k, c), unit(k, s_me, c), (cx, cy, c))
                cp.start()
                sends.append(cp)
        for j, (cx, cy) in enumerate(chips):
            for k in range(n):
                landed = unit(k, 2 * cx + cy, c)
                rcopy(k, j, landed, landed, (cx, cy, c)).wait_recv()
                cp = rcopy(k, 3 + j, landed, landed, sibling)
                cp.start()
                sends.append(cp)
        for j, (cx, cy) in enumerate(chips):
            for k in range(n):
                other = unit(k, 2 * cx + cy, 1 - c)
                rcopy(k, 3 + j, other, other, sibling).wait_recv()
        for cp in sends:
            cp.wait_send()
        for cp in local:
            cp.wait()

    vm = pl.BlockSpec(memory_space=pltpu.VMEM)
    return pl.pallas_call(
        body, name="ag_weights",
        out_shape=tuple(jax.ShapeDtypeStruct(_FULL_SHAPES[w], BF) for w in ws),
        in_specs=[vm] * n, out_specs=[_ANY] * n,
        scratch_shapes=[pltpu.VMEM(_SHARD_SHAPES[w], BF) for w in ws]
        + [pltpu.SemaphoreType.DMA((n, 6)), pltpu.SemaphoreType.DMA((n, 6)), pltpu.SemaphoreType.DMA((n, 2))],
        compiler_params=_params(40),
    )(*shards)


def _shard_of(ref, w, s):
    if w == 0:
        return ref.at[:, pl.ds(_mo(s * SHARD_IN, 128), SHARD_IN)]
    if w == 3:
        return ref.at[pl.ds(_mo(s * 256, 256), 256), :]
    return ref.at[:, pl.ds(_mo(s * 256, 128), 256)]


def _stage_weights(ws, shards, pos):
    n = len(ws)

    def body(pos_ref, *refs):
        for k in range(n):
            refs[n + k][...] = refs[k][...].astype(BF)

    def spec(w):
        shape = _SHARD_SHAPES[w]
        if w == 3:
            return pl.BlockSpec(shape, lambda i, pos: (pos[1], 0))
        return pl.BlockSpec(shape, lambda i, pos: (0, pos[1]))

    return list(pl.pallas_call(
        body, name="stage_weights",
        grid_spec=pltpu.PrefetchScalarGridSpec(
            num_scalar_prefetch=1, grid=(1,),
            in_specs=[pl.BlockSpec(_SHARD_SHAPES[w], lambda i, pos: (0, 0)) for w in ws],
            out_specs=[spec(w) for w in ws]),
        out_shape=tuple(jax.ShapeDtypeStruct(_FULL_SHAPES[w], BF) for w in ws),
        compiler_params=_params(16, dimension_semantics=("arbitrary",)),
    )(pos, *shards))


def _gather_copies(ws):
    def copies(refs, send_sems, recv_sems):
        x, y, c, chips = _mesh_pos()
        out = []
        for j, (cx, cy) in enumerate(chips):
            for k, w in enumerate(ws):
                mine = _shard_of(refs[k], w, 2 * x + y)
                out.append(pltpu.make_async_remote_copy(
                    src_ref=mine, dst_ref=mine, send_sem=send_sems.at[3 * k + j], recv_sem=recv_sems.at[3 * k + j],
                    device_id=(cx, cy, c), device_id_type=MESH))
        return out
    return copies


def _inproj_fwd(x, norm_g, w_in_bf, tm=512, after=()):
    S = x.shape[0]

    def body(x_ref, g_ref, w_ref, ht_ref, q_ref, k_ref, v_ref, zr_ref):
        xv = x_ref[...]
        r = lax.rsqrt(jnp.mean(xv * xv, axis=-1, keepdims=True) + EPS)
        hf = (xv * r) * g_ref[...]
        ht_ref[...] = hf.T.astype(BF)
        h = hf.astype(BF)
        heads = (q_ref, k_ref, v_ref)
        for j in range(D_IN // 512):
            z = _dot(h, w_ref[:, j * 512:(j + 1) * 512])
            if j < 3:
                zb = z.astype(BF)
                for hd in range(N_HEADS):
                    heads[j][hd] = zb[:, hd * HEAD_DIM:(hd + 1) * HEAD_DIM]
            else:
                zr_ref[:, (j - 3) * 512:(j - 2) * 512] = z

    head_major = jax.ShapeDtypeStruct((N_HEADS, S, HEAD_DIM), BF)
    head_spec = pl.BlockSpec((N_HEADS, tm, HEAD_DIM), lambda i: (0, i, 0))
    return pl.pallas_call(
        _after(body, 3, after), name="inproj_fwd", grid=(S // tm,),
        out_shape=(jax.ShapeDtypeStruct((D_MODEL, S), BF), head_major, head_major, head_major,
                   jax.ShapeDtypeStruct((S, D_IN - 3 * D_A), F32)),
        in_specs=[pl.BlockSpec((tm, D_MODEL), lambda i: (i, 0)),
                  pl.BlockSpec((1, D_MODEL), lambda i: (0, 0)),
                  pl.BlockSpec((D_MODEL, D_IN), lambda i: (0, 0), pipeline_mode=pl.Buffered(1))]
        + [_ANY] * len(after),
        out_specs=[pl.BlockSpec((D_MODEL, tm), lambda i: (0, i)),
                   head_spec, head_spec, head_spec,
                   pl.BlockSpec((tm, D_IN - 3 * D_A), lambda i: (i, 0))],
        compiler_params=_params(52, dimension_semantics=("arbitrary",)),
    )(x, norm_g, w_in_bf, *after)


def _skew_table(gp_row):
    row = lax.broadcasted_iota(jnp.int32, (QB, ROLL_W), 0)
    t = jnp.broadcast_to(gp_row, (QB, ROLL_W))
    for b in range(7):
        t = jnp.where(((row >> b) & 1) == 1, pltpu.roll(t, 1 << b, axis=1), t)
    return t


def _unskew_sum(d):
    row = lax.broadcasted_iota(jnp.int32, (QB, ROLL_W), 0)
    for b in range(7):
        d = jnp.where(((row >> b) & 1) == 1, pltpu.roll(d, ROLL_W - (1 << b), axis=1), d)
    return jnp.sum(d, axis=0, keepdims=True)


def _struct_mask():
    a = lax.broadcasted_iota(jnp.int32, (QB, KB), 0) // CHUNK
    b = lax.broadcasted_iota(jnp.int32, (QB, KB), 1) // CHUNK
    return (b >= a) & (b <= a + N_PREV)


def _load_kv(k_hbm, v_hbm, k_scr, v_scr, sems, S, meanwhile=lambda: None):
    zeros = jnp.zeros((N_HEADS, PADK, HEAD_DIM), BF)
    k_scr[:, 0:PADK, :] = zeros
    v_scr[:, 0:PADK, :] = zeros
    ck = pltpu.make_async_copy(k_hbm, k_scr.at[:, pl.ds(PADK, S), :], sems.at[0])
    cv = pltpu.make_async_copy(v_hbm, v_scr.at[:, pl.ds(PADK, S), :], sems.at[1])
    ck.start()
    cv.start()
    meanwhile()
    ck.wait()
    cv.wait()


_BATCH_NT = (((2,), (2,)), ((0,), (0,)))
_BATCH_NN = (((2,), (1,)), ((0,), (0,)))
_BATCH_TN = (((1,), (1,)), ((0,), (0,)))


def _bdot(a, b, dims):
    return lax.dot_general(a, b, dims, preferred_element_type=F32)


def _scaled(q):
    return q * jnp.asarray(SCALE, BF)


def _scores(qs, kb, bias, i, front):
    s = _bdot(qs, kb, _BATCH_NT) + bias
    if front:
        col = lax.broadcasted_iota(jnp.int32, (1, 1, KB), 2)
        s = jnp.where(col >= PADK - i * QB, s, NEG_INF)
    return s


def _attn_fwd(q3, k3, v3, gp):
    S = q3.shape[1]

    def body(q_ref, k_hbm, v_hbm, gp_ref, o_ref, lse_ref, bias_ref, k_scr, v_scr, sems):
        i = pl.program_id(0)

        @pl.when(i == 0)
        def _():
            def build_bias():
                keep = _struct_mask()
                for h in range(N_HEADS):
                    bias_ref[h] = jnp.where(keep, _skew_table(gp_ref[h:h + 1, :])[:, :KB], NEG_INF)
            _load_kv(k_hbm, v_hbm, k_scr, v_scr, sems, S, build_bias)

        def step(front):
            start = pl.multiple_of(i * QB, QB)
            kb = k_scr[:, pl.ds(start, KB), :]
            vb = v_scr[:, pl.ds(start, KB), :]
            s = _scores(_scaled(q_ref[...]), kb, bias_ref[...], i, front)
            m = jnp.max(s, axis=-1, keepdims=True)
            e = jnp.exp(s - m)
            l = jnp.sum(e, axis=-1, keepdims=True)
            p = e * (1.0 / l)
            o = _bdot(p.astype(BF), vb, _BATCH_NN)
            lse_ref[...] = jnp.broadcast_to(m + jnp.log(l), (N_HEADS, QB, 128))
            for h in range(N_HEADS):
                o_ref[:, h * HEAD_DIM:(h + 1) * HEAD_DIM] = o[h]

        pl.when(i < KEEP)(functools.partial(step, True))
        pl.when(i >= KEEP)(functools.partial(step, False))

    kv_scr = pltpu.VMEM((N_HEADS, S + PADK, HEAD_DIM), BF)
    return pl.pallas_call(
        body, name="attn_fwd", grid=(S // QB,),
        out_shape=(jax.ShapeDtypeStruct((S, D_A), F32), jax.ShapeDtypeStruct((N_HEADS, S, 128), F32),
                   jax.ShapeDtypeStruct((N_HEADS, QB, KB), F32)),
        in_specs=[pl.BlockSpec((N_HEADS, QB, HEAD_DIM), lambda i: (0, i, 0)),
                  pl.BlockSpec(memory_space=pl.ANY), pl.BlockSpec(memory_space=pl.ANY),
                  pl.BlockSpec((N_HEADS, ROLL_W), lambda i: (0, 0))],
        out_specs=[pl.BlockSpec((QB, D_A), lambda i: (i, 0)),
                   pl.BlockSpec((N_HEADS, QB, 128), lambda i: (0, i, 0)),
                   pl.BlockSpec((N_HEADS, QB, KB), lambda i: (0, 0, 0))],
        scratch_shapes=[kv_scr, kv_scr, pltpu.SemaphoreType.DMA((2,))],
        compiler_params=_params(48, dimension_semantics=("arbitrary",)),
    )(q3, k3, v3, gp)


def _attn_bwd(q3, k3, v3, d_att3, lse, bias, after=()):
    S = q3.shape[1]
    nq = S // QB

    def body(q_ref, do_ref, k_hbm, v_hbm, lse_ref, bias_ref, dq_ref, dk_ref, dv_ref, dgp_ref,
             k_scr, v_scr, dk_acc, dv_acc, dbias_acc, pad_scr, sems):
        i = pl.program_id(0)

        @pl.when(i == 0)
        def _():
            def clear():
                dk_acc[...] = jnp.zeros_like(dk_acc)
                dv_acc[...] = jnp.zeros_like(dv_acc)
                dbias_acc[...] = jnp.zeros_like(dbias_acc)
            _load_kv(k_hbm, v_hbm, k_scr, v_scr, sems, S, clear)

        def step(front):
            start = pl.multiple_of(i * QB, QB)
            kb = k_scr[:, pl.ds(start, KB), :]
            vb = v_scr[:, pl.ds(start, KB), :]
            qs = _scaled(q_ref[...])
            do = do_ref[...]
            p = jnp.exp(_scores(qs, kb, bias_ref[...], i, front) - jnp.tile(lse_ref[...], (1, 1, KB // 128)))
            dp = _bdot(do, vb, _BATCH_NT)
            ds = p * (dp - jnp.sum(dp * p, axis=-1, keepdims=True))
            dbias_acc[...] += ds
            dsb = ds.astype(BF)
            dq = _bdot(dsb, kb, _BATCH_NN) * SCALE
            for h in range(N_HEADS):
                dq_ref[:, h * HEAD_DIM:(h + 1) * HEAD_DIM] = dq[h].astype(BF)
            dk_acc[...] += _bdot(dsb, qs, _BATCH_TN)
            dv_acc[...] += _bdot(p.astype(BF), do, _BATCH_TN)

        pl.when(i < KEEP)(functools.partial(step, True))
        pl.when((i >= KEEP) & (i < nq))(functools.partial(step, False))

        for h in range(N_HEADS):
            hs = slice(h * HEAD_DIM, (h + 1) * HEAD_DIM)
            dk_ref[:, hs] = dk_acc[h, 0:QB, :].astype(BF)
            dv_ref[:, hs] = dv_acc[h, 0:QB, :].astype(BF)
        dk_acc[:, 0:KB - QB, :] = dk_acc[:, QB:KB, :]
        dv_acc[:, 0:KB - QB, :] = dv_acc[:, QB:KB, :]
        dk_acc[:, KB - QB:KB, :] = jnp.zeros((N_HEADS, QB, HEAD_DIM), F32)
        dv_acc[:, KB - QB:KB, :] = jnp.zeros((N_HEADS, QB, HEAD_DIM), F32)

        @pl.when(i == nq + KEEP - 1)
        def _():
            lane = lax.broadcasted_iota(jnp.int32, (1, ROLL_W), 1)
            hi = (lane < 384) | (lane >= 832)
            lo = (lane > 640) & (lane < 832)
            pad_scr[...] = jnp.zeros_like(pad_scr)
            for h in range(N_HEADS):
                pad_scr[:, 0:KB] = dbias_acc[h]
                g = _unskew_sum(pad_scr[...])
                s_hi = jnp.sum(jnp.where(hi, g, 0.0), axis=-1, keepdims=True)
                s_lo = jnp.sum(jnp.where(lo, g, 0.0), axis=-1, keepdims=True)
                g = jnp.where(lane == 384, g + s_hi, g)
                g = jnp.where(lane == 640, g + s_lo, g)
                dgp_ref[h:h + 1, :] = g

    last = nq - 1
    kv_scr = pltpu.VMEM((N_HEADS, S + PADK, HEAD_DIM), BF)
    return pl.pallas_call(
        _after(body, 6, after), name="attn_bwd", grid=(nq + KEEP,),
        out_shape=(jax.ShapeDtypeStruct((S, D_A), BF), jax.ShapeDtypeStruct((S, D_A), BF),
                   jax.ShapeDtypeStruct((S, D_A), BF), jax.ShapeDtypeStruct((N_HEADS, ROLL_W), F32)),
        in_specs=[pl.BlockSpec((N_HEADS, QB, HEAD_DIM), lambda i: (0, jnp.minimum(i, last), 0)),
                  pl.BlockSpec((N_HEADS, QB, HEAD_DIM), lambda i: (0, jnp.minimum(i, last), 0)),
                  pl.BlockSpec(memory_space=pl.ANY), pl.BlockSpec(memory_space=pl.ANY),
                  pl.BlockSpec((N_HEADS, QB, 128), lambda i: (0, jnp.minimum(i, last), 0)),
                  pl.BlockSpec((N_HEADS, QB, KB), lambda i: (0, 0, 0))] + [_ANY] * len(after),
        out_specs=[pl.BlockSpec((QB, D_A), lambda i: (jnp.minimum(i, last), 0)),
                   pl.BlockSpec((QB, D_A), lambda i: (jnp.maximum(i - KEEP, 0), 0)),
                   pl.BlockSpec((QB, D_A), lambda i: (jnp.maximum(i - KEEP, 0), 0)),
                   pl.BlockSpec((N_HEADS, ROLL_W), lambda i: (0, 0))],
        scratch_shapes=[kv_scr, kv_scr,
                        pltpu.VMEM((N_HEADS, KB, HEAD_DIM), F32), pltpu.VMEM((N_HEADS, KB, HEAD_DIM), F32),
                        pltpu.VMEM((N_HEADS, QB, KB), F32), pltpu.VMEM((QB, ROLL_W), F32),
                        pltpu.SemaphoreType.DMA((2,))],
        compiler_params=_params(56, dimension_semantics=("arbitrary",)),
    )(q3, d_att3, k3, v3, lse, bias, *after)


def _sgu_core(ub, vb, lg, lb):
    u, du = _gelu_and_grad(ub)
    v, dv = _gelu_and_grad(vb)
    mu = jnp.mean(v, axis=-1, keepdims=True)
    vc = v - mu
    rstd = lax.rsqrt(jnp.mean(vc * vc, axis=-1, keepdims=True) + EPS)
    xh = vc * rstd
    vn = xh * lg + lb
    return u, du, dv, rstd, xh, vn


def _tri():
    r = lax.broadcasted_iota(jnp.int32, (SGU_CHUNK, SGU_CHUNK), 0)
    c = lax.broadcasted_iota(jnp.int32, (SGU_CHUNK, SGU_CHUNK), 1)
    return r >= c


def _tail_sgu(att, zrest, x, target, w_pa, w_pb, w_out, b_gate, final_g, ln_g, ln_b, w_s, b_s_t, tm=256):
    S = x.shape[0]
    nt = S // tm
    chunks = tm // SGU_CHUNK

    def body(att_ref, ga_ref, ub_ref, vb_ref, gb_ref, gta_ref, gtb_ref, x_ref, t_ref,
             wpa_ref, wpb_ref, wout_ref, bg_ref, fg_ref, lg_ref, lb_ref, ws_ref, bst_ref,
             dout_ref, datt_ref, dzt_ref, dzs_ref, gwout_hbm, gwpa_hbm, gwpb_hbm,
             gbg_ref, gfg_ref, loss_ref, gws_ref, gbs_ref, glg_ref, glb_ref,
             acc_out, acc_pa, acc_pb, sg_scr, mix_scr, dvn_scr, bs_acc, sems):
        i = pl.program_id(0)

        @pl.when(i == 0)
        def _():
            for r in (acc_out, acc_pa, acc_pb, gbg_ref, gfg_ref, loss_ref, gws_ref, glg_ref, glb_ref, bs_acc):
                r[...] = jnp.zeros_like(r)

        u, du, dv, rstd, xh, vn = _sgu_core(ub_ref[...], vb_ref[...], lg_ref[...], lb_ref[...])
        vnb = vn.astype(BF)
        tri = _tri()
        blocks = [(g, slice(n * SGU_CHUNK, (n + 1) * SGU_CHUNK), slice(g * 128, (g + 1) * 128))
                  for g in range(N_GROUPS) for n in range(chunks)]
        wts = [jnp.where(tri, ws_ref[g], 0.0) for g in range(N_GROUPS)]
        for g, rs, cs in blocks:
            mixed = _dot(wts[g].astype(BF), vnb[rs, cs]) + bst_ref[:, g:g + 1]
            mix_scr[rs, cs] = mixed
            sg_scr[rs, cs] = u[rs, cs] * mixed

        att = att_ref[...]
        sg = sg_scr[...]
        sa, dsa = _silu_and_grad(ga_ref[...])
        sb, dsb = _silu_and_grad(gb_ref[...])
        ya = (att * sa).astype(BF)
        yb = (sg * sb).astype(BF)
        pa = _dot(ya, wpa_ref[...])
        pb = _dot(yb, wpb_ref[...])
        ga = _sigmoid(gta_ref[...] + bg_ref[:, 0:D_MODEL])
        gb = _sigmoid(gtb_ref[...] + bg_ref[:, D_MODEL:2 * D_MODEL])
        merged = (ga * pa + gb * pb).astype(BF)
        out = x_ref[...] + _dot(merged, wout_ref[...])
        r2 = lax.rsqrt(jnp.mean(out * out, axis=-1, keepdims=True) + EPS)
        nrm = out * r2
        fg = fg_ref[...]
        err = nrm * fg - t_ref[...]
        loss_ref[...] += 0.5 * jnp.sum(jnp.mean(err * err, axis=-1, keepdims=True))
        dy = err * (1.0 / D_MODEL)
        gfg_ref[...] += jnp.sum(dy * nrm, axis=0, keepdims=True)
        dn = dy * fg
        d_out = r2 * (dn - nrm * jnp.mean(dn * nrm, axis=-1, keepdims=True))
        dout_ref[...] = d_out
        d_outb = d_out.astype(BF)
        acc_out[...] += _dot_tn(merged, d_outb)
        dm = _dot_nt(d_outb, wout_ref[...])
        d_pa = (dm * ga).astype(BF)
        d_pb = (dm * gb).astype(BF)
        d_gta = dm * pa * (ga * (1.0 - ga))
        d_gtb = dm * pb * (gb * (1.0 - gb))
        gbg_ref[:, 0:D_MODEL] += jnp.sum(d_gta, axis=0, keepdims=True)
        gbg_ref[:, D_MODEL:2 * D_MODEL] += jnp.sum(d_gtb, axis=0, keepdims=True)
        dzt_ref[:, 2 * D_A:2 * D_A + D_MODEL] = d_gta.astype(BF)
        dzt_ref[:, 2 * D_A + D_MODEL:] = d_gtb.astype(BF)
        acc_pa[...] += _dot_tn(ya, d_pa)
        acc_pb[...] += _dot_tn(yb, d_pb)
        d_ya = _dot_nt(d_pa, wpa_ref[...])
        d_yb = _dot_nt(d_pb, wpb_ref[...])
        d_att = (d_ya * sa).astype(BF)
        for hd in range(N_HEADS):
            datt_ref[hd] = d_att[:, hd * HEAD_DIM:(hd + 1) * HEAD_DIM]
        dzt_ref[:, 0:D_A] = (d_ya * att * dsa).astype(BF)
        dzt_ref[:, D_A:2 * D_A] = (d_yb * sg * dsb).astype(BF)

        dsg = d_yb * sb
        dzs_ref[:, 0:D_B] = (dsg * mix_scr[...] * du).astype(BF)
        dmix = dsg * u
        for g, rs, cs in blocks:
            dmb = dmix[rs, cs].astype(BF)
            bs_acc[:, cs] += dmix[rs, cs]
            gws_ref[g] += _dot_nt(dmb, vnb[rs, cs])
            dvn_scr[rs, cs] = _dot(wts[g].T.astype(BF), dmb)
        dvn = dvn_scr[...]
        glg_ref[...] += jnp.sum(dvn * xh, axis=0, keepdims=True)
        glb_ref[...] += jnp.sum(dvn, axis=0, keepdims=True)
        dxh = dvn * lg_ref[...]
        dvv = rstd * (dxh - jnp.mean(dxh, axis=-1, keepdims=True)
                      - xh * jnp.mean(dxh * xh, axis=-1, keepdims=True))
        dzs_ref[:, D_B:2 * D_B] = (dvv * dv).astype(BF)

        @pl.when(i == nt - 1)
        def _():
            cps = [pltpu.make_async_copy(acc_out, gwout_hbm, sems.at[0]),
                   pltpu.make_async_copy(acc_pa, gwpa_hbm, sems.at[1]),
                   pltpu.make_async_copy(acc_pb, gwpb_hbm, sems.at[2])]
            for cp in cps:
                cp.start()
            lane = lax.broadcasted_iota(jnp.int32, (SGU_CHUNK, 128), 1)
            cols = jnp.zeros((SGU_CHUNK, 128), F32)
            for g in range(N_GROUPS):
                gws_ref[g] = jnp.where(tri, gws_ref[g], 0.0)
                col = jnp.sum(bs_acc[:, g * 128:(g + 1) * 128], axis=-1, keepdims=True)
                cols = jnp.where(lane == g, col, cols)
            gbs_ref[...] = cols
            for cp in cps:
                cp.wait()

    c2 = lambda i: (0, 0)
    c3 = lambda i: (0, 0, 0)
    zcol = lambda w, blk: pl.BlockSpec((tm, w), lambda i: (i, blk))
    row = lambda w: pl.BlockSpec((tm, w), lambda i: (i, 0))
    return pl.pallas_call(
        body, name="tail", grid=(nt,),
        out_shape=(jax.ShapeDtypeStruct((S, D_MODEL), F32), jax.ShapeDtypeStruct((N_HEADS, S, HEAD_DIM), BF),
                   jax.ShapeDtypeStruct((S, 3072), BF), jax.ShapeDtypeStruct((S, 2 * D_B), BF),
                   jax.ShapeDtypeStruct((D_MODEL, D_MODEL), F32), jax.ShapeDtypeStruct((D_A, D_MODEL), F32),
                   jax.ShapeDtypeStruct((D_B, D_MODEL), F32),
                   jax.ShapeDtypeStruct((1, 2 * D_MODEL), F32), jax.ShapeDtypeStruct((1, D_MODEL), F32),
                   jax.ShapeDtypeStruct((1, 128), F32),
                   jax.ShapeDtypeStruct((N_GROUPS, 128, 128), F32), jax.ShapeDtypeStruct((SGU_CHUNK, 128), F32),
                   jax.ShapeDtypeStruct((1, D_B), F32), jax.ShapeDtypeStruct((1, D_B), F32)),
        in_specs=[row(D_A), zcol(512, 0), zcol(512, 1), zcol(512, 2), zcol(512, 3),
                  zcol(D_MODEL, 2), zcol(D_MODEL, 3), row(D_MODEL), row(D_MODEL),
                  pl.BlockSpec((D_A, D_MODEL), c2), pl.BlockSpec((D_B, D_MODEL), c2),
                  pl.BlockSpec((D_MODEL, D_MODEL), c2),
                  pl.BlockSpec((1, 2 * D_MODEL), c2), pl.BlockSpec((1, D_MODEL), c2),
                  pl.BlockSpec((1, D_B), c2), pl.BlockSpec((1, D_B), c2),
                  pl.BlockSpec((N_GROUPS, 128, 128), c3), pl.BlockSpec((128, N_GROUPS), c2)],
        out_specs=[row(D_MODEL), pl.BlockSpec((N_HEADS, tm, HEAD_DIM), lambda i: (0, i, 0)),
                   row(3072), row(2 * D_B), _ANY, _ANY, _ANY,
                   pl.BlockSpec((1, 2 * D_MODEL), c2), pl.BlockSpec((1, D_MODEL), c2),
                   pl.BlockSpec((1, 128), c2),
                   pl.BlockSpec((N_GROUPS, 128, 128), c3), pl.BlockSpec((SGU_CHUNK, 128), c2),
                   pl.BlockSpec((1, D_B), c2), pl.BlockSpec((1, D_B), c2)],
        scratch_shapes=[pltpu.VMEM((D_MODEL, D_MODEL), F32), pltpu.VMEM((D_A, D_MODEL), F32),
                        pltpu.VMEM((D_B, D_MODEL), F32),
                        pltpu.VMEM((tm, D_B), F32), pltpu.VMEM((tm, D_B), F32), pltpu.VMEM((tm, D_B), F32),
                        pltpu.VMEM((SGU_CHUNK, D_B), F32), pltpu.SemaphoreType.DMA((3,))],
        compiler_params=_params(58, dimension_semantics=("arbitrary",)),
    )(att, zrest, zrest, zrest, zrest, zrest, zrest, x, target, w_pa, w_pb, w_out, b_gate, final_g,
      ln_g, ln_b, w_s, b_s_t)


_DZ_MAP = ((0, 0), (1, 0), (2, 0), (3, 0), (4, 0), (4, 1), (3, 1), (3, 2), (3, 3), (3, 4), (3, 5))


def _dh_gradx(dq, dk, dv, dzt, dzs, w_in_bf, x, norm_g, d_out, tm=512, after=()):
    S = x.shape[0]

    def body(dq_ref, dk_ref, dv_ref, dzt_ref, dzs_ref, w_ref, x_ref, g_ref, dout_ref, gx_ref, gn_ref):
        i = pl.program_id(0)

        @pl.when(i == 0)
        def _():
            gn_ref[...] = jnp.zeros_like(gn_ref)

        pieces = (dq_ref, dk_ref, dv_ref, dzt_ref, dzs_ref)
        dh = jnp.zeros((tm, D_MODEL), F32)
        for j, (pc, blk) in enumerate(_DZ_MAP):
            dh += _dot_nt(pieces[pc][:, blk * 512:(blk + 1) * 512], w_ref[:, j * 512:(j + 1) * 512])
        xv = x_ref[...]
        r = lax.rsqrt(jnp.mean(xv * xv, axis=-1, keepdims=True) + EPS)
        nrm = xv * r
        gn_ref[...] += jnp.sum(dh * nrm, axis=0, keepdims=True)
        dn = dh * g_ref[...]
        gx_ref[...] = r * (dn - nrm * jnp.mean(dn * nrm, axis=-1, keepdims=True)) + dout_ref[...]

    row = lambda w: pl.BlockSpec((tm, w), lambda i: (i, 0))
    c2 = lambda i: (0, 0)
    return pl.pallas_call(
        _after(body, 9, after), name="dh_gradx", grid=(S // tm,),
        out_shape=(jax.ShapeDtypeStruct((S, D_MODEL), F32), jax.ShapeDtypeStruct((1, D_MODEL), F32)),
        in_specs=[row(512), row(512), row(512), row(3072), row(1024),
                  pl.BlockSpec((D_MODEL, D_IN), c2, pipeline_mode=pl.Buffered(1)), row(D_MODEL),
                  pl.BlockSpec((1, D_MODEL), c2), row(D_MODEL)]
        + [_ANY] * len(after),
        out_specs=[row(D_MODEL), pl.BlockSpec((1, D_MODEL), c2)],
        compiler_params=_params(48, dimension_semantics=("arbitrary",)),
    )(dq, dk, dv, dzt, dzs, w_in_bf, x, norm_g, d_out, *after)


def _gw_in(ht, dq, dk, dv, dzt, dzs, tn=256, after=()):
    S = ht.shape[1]
    per = 512 // tn
    cols = tuple((pc, per * blk + h) for pc, blk in _DZ_MAP for h in range(per))

    def body(ht_ref, dq_ref, dk_ref, dv_ref, dzt_ref, dzs_ref, o_ref, ob_ref):
        j = pl.program_id(0)
        pieces = (dq_ref, dk_ref, dv_ref, dzt_ref, dzs_ref)
        for pc in range(5):
            hit = functools.reduce(jnp.logical_or, [j == jj for jj, (p, _) in enumerate(cols) if p == pc])

            @pl.when(hit)
            def _(pc=pc):
                g = _dot(ht_ref[...], pieces[pc][...])
                o_ref[...] = g
                ob_ref[...] = g.astype(BF)

    def piece_spec(pc):
        cur = next(blk for p, blk in cols if p == pc)
        held = []
        for p, blk in cols:
            cur = blk if p == pc else cur
            held.append(cur)

        def index_map(j):
            blk = jnp.int32(held[0])
            for jj in range(1, len(held)):
                if held[jj] != held[jj - 1]:
                    blk = jnp.where(j >= jj, jnp.int32(held[jj]), blk)
            return (0, blk)

        return pl.BlockSpec((S, tn), index_map)

    return pl.pallas_call(
        _after(body, 6, after), name="gw_in", grid=(len(cols),),
        out_shape=(jax.ShapeDtypeStruct((D_MODEL, D_IN), F32), jax.ShapeDtypeStruct((D_MODEL, D_IN), BF)),
        in_specs=[pl.BlockSpec((D_MODEL, S), lambda j: (0, 0))] + [piece_spec(pc) for pc in range(5)]
        + [_ANY] * len(after),
        out_specs=[pl.BlockSpec((D_MODEL, tn), lambda j: (0, j)), pl.BlockSpec((D_MODEL, tn), lambda j: (0, j))],
        compiler_params=_params(48, dimension_semantics=("arbitrary",)),
    )(ht, dq, dk, dv, dzt, dzs, *after)


_HBM = pl.BlockSpec(memory_space=pltpu.HBM)
_SEM = pl.BlockSpec(memory_space=pltpu.SEMAPHORE)
_ANY = pl.BlockSpec(memory_space=pl.ANY)
_EFFECT = pltpu.SideEffectType.DATAFLOW_SIDE_EFFECTING


def _in_hbm(a):
    return pltpu.with_memory_space_constraint(a, pltpu.HBM)


def _after(body, n_in, after):
    if not after:
        return body
    return lambda *refs: body(*refs[:n_in], *refs[n_in + len(after):])


class _Started:
    def __init__(self, send, recv, bufs, token):
        self.send, self.recv, self.bufs, self.token = send, recv, bufs, token


def _split_start(name, bufs, n_copies, copies, after=()):
    nb = len(bufs)

    def body(*refs):
        refs = refs[:nb] + refs[nb + len(after):]
        for cp in copies(refs[:nb], refs[nb], refs[nb + 1]):
            cp.start()
        refs[-1][...] = jnp.zeros_like(refs[-1])

    outs = pl.pallas_call(
        body, name=name,
        out_shape=(pltpu.SemaphoreType.DMA((n_copies,)), pltpu.SemaphoreType.DMA((n_copies,)),
                   *[pltpu.HBM(b.shape, b.dtype) for b in bufs], jax.ShapeDtypeStruct((8, 128), F32)),
        in_specs=[_HBM] * nb + [_ANY] * len(after),
        out_specs=(_SEM, _SEM, *[_HBM] * nb, pl.BlockSpec(memory_space=pltpu.VMEM)),
        input_output_aliases={k: 2 + k for k in range(nb)},
        compiler_params=_params(1, has_side_effects=_EFFECT),
    )(*[_in_hbm(b) for b in bufs], *after)
    return _Started(outs[0], outs[1], list(outs[2:2 + nb]), outs[-1])


def _split_wait(name, started, copies, after):
    nb = len(started.bufs)

    def body(*refs):
        for cp in copies(refs[:nb], refs[nb], refs[nb + 1]):
            cp.wait_send()
            cp.wait_recv()

    return list(pl.pallas_call(
        body, name=name,
        out_shape=tuple(pltpu.HBM(b.shape, b.dtype) for b in started.bufs),
        in_specs=[_HBM] * nb + [_SEM, _SEM, _ANY],
        out_specs=tuple([_HBM] * nb),
        input_output_aliases={k: k for k in range(nb)},
        compiler_params=_params(1, has_side_effects=_EFFECT),
    )(*started.bufs, started.send, started.recv, after))


def _x1_copies(ws):
    def copies(refs, send_sems, recv_sems):
        x, y, c, _ = _mesh_pos()
        out = []
        for k, w in enumerate(ws):
            for s in range(N_SHARD):
                out.append(pltpu.make_async_remote_copy(
                    src_ref=_UNITS[w](refs[k], s, 1 - c), dst_ref=refs[len(ws) + k].at[s],
                    send_sem=send_sems.at[N_SHARD * k + s], recv_sem=recv_sems.at[N_SHARD * k + s],
                    device_id=(x, y, 1 - c), device_id_type=MESH))
        return out
    return copies


def _x2_copies(n):
    def copies(refs, send_sems, recv_sems):
        x, y, c, chips = _mesh_pos()
        out = []
        for j, (cx, cy) in enumerate(chips):
            for k in range(n):
                out.append(pltpu.make_async_remote_copy(
                    src_ref=refs[k].at[2 * cx + cy], dst_ref=refs[n + k].at[j],
                    send_sem=send_sems.at[3 * k + j], recv_sem=recv_sems.at[3 * k + j],
                    device_id=(cx, cy, c), device_id_type=MESH))
        return out
    return copies


def _x3_copies(ws):
    def copies(refs, send_sems, recv_sems):
        x, y, c, _ = _mesh_pos()
        out = []
        for k, w in enumerate(ws):
            rows = _HALF_ROWS[w]
            mine = refs[k].at[pl.ds(_mo(c * rows, rows), rows), :]
            out.append(pltpu.make_async_remote_copy(
                src_ref=mine, dst_ref=mine, send_sem=send_sems.at[k], recv_sem=recv_sems.at[k],
                device_id=(x, y, 1 - c), device_id_type=MESH))
        return out
    return copies


def _x1_lands(ws, dtype=F32):
    return [lax.empty((N_SHARD,) + _UNIT_SHAPES[w], dtype) for w in ws]


def _x2_lands(ws):
    return [lax.empty((3,) + _UNIT_SHAPES[w], BF) for w in ws]


def _grad_add1(w, g, recv, pos):
    ur, uc = _UNIT_SHAPES[w]

    def body(pos_ref, g_ref, r_ref, own_ref, csb_ref):
        v = g_ref[...] + r_ref[0].astype(F32)
        csb_ref[0] = v.astype(BF)

        @pl.when(pl.program_id(0) == pos_ref[1])
        def _():
            own_ref[...] = v

    u3 = lambda s, pos: (s, 0, 0)
    return pl.pallas_call(
        body, name=f"grad_add1_{w}",
        grid_spec=pltpu.PrefetchScalarGridSpec(
            num_scalar_prefetch=1, grid=(N_SHARD,),
            in_specs=[pl.BlockSpec((ur, uc), lambda s, pos: (pos[0], s)), pl.BlockSpec((1, ur, uc), u3)],
            out_specs=[pl.BlockSpec((ur, uc), lambda s, pos: (0, 0)), pl.BlockSpec((1, ur, uc), u3)]),
        out_shape=(jax.ShapeDtypeStruct((ur, uc), F32), jax.ShapeDtypeStruct((N_SHARD, ur, uc), BF)),
        compiler_params=_params(40, dimension_semantics=("arbitrary",)),
    )(pos, g, recv)


def _grad_add1_group(ws, gs, recvs):
    n = len(ws)

    def body(*refs):
        c = lax.axis_index("c")
        for k, w in enumerate(ws):
            g, r, cs, csb = refs[k], refs[n + k], refs[2 * n + k], refs[3 * n + k]
            for s in range(N_SHARD):
                v = _UNITS[w](g, s, c)[...] + r[s]
                cs[s] = v
                csb[s] = v.astype(BF)

    vm = pl.BlockSpec(memory_space=pltpu.VMEM)
    outs = pl.pallas_call(
        body, name="grad_add1_group",
        out_shape=tuple(jax.ShapeDtypeStruct((N_SHARD,) + _UNIT_SHAPES[w], dt) for dt in (F32, BF) for w in ws),
        in_specs=[vm] * (2 * n), out_specs=[vm] * (2 * n),
        compiler_params=_params(32),
    )(*gs, *recvs)
    return list(outs[:n]), list(outs[n:])


def _grad_add2_group(ws, css, recvs):
    n = len(ws)

    def body(*refs):
        x, y, c, _ = _mesh_pos()
        for k, w in enumerate(ws):
            cs, r, o = refs[k], refs[n + k], refs[2 * n + k]
            rows = _HALF_ROWS[w]
            total = ((cs[2 * x + y] + r[0].astype(F32)) + r[1].astype(F32)) + r[2].astype(F32)
            o[pl.ds(_mo(c * rows, rows), rows), :] = total

    vm = pl.BlockSpec(memory_space=pltpu.VMEM)
    return list(pl.pallas_call(
        body, name="grad_add2_group",
        out_shape=tuple(jax.ShapeDtypeStruct(_SHARD_SHAPES[w], F32) for w in ws),
        in_specs=[vm] * (2 * n), out_specs=[vm] * n,
        compiler_params=_params(32),
    )(*css, *recvs))


def _grad_add2(w, own, recv, pos):
    ur, uc = _UNIT_SHAPES[w]
    nt = 4
    tr = ur // nt

    def body(pos_ref, own_ref, r_ref, o_ref):
        o_ref[...] = ((own_ref[...] + r_ref[0].astype(F32)) + r_ref[1].astype(F32)) + r_ref[2].astype(F32)

    return pl.pallas_call(
        body, name=f"grad_add2_{w}",
        grid_spec=pltpu.PrefetchScalarGridSpec(
            num_scalar_prefetch=1, grid=(nt,),
            in_specs=[pl.BlockSpec((tr, uc), lambda t, pos: (t, 0)),
                      pl.BlockSpec((3, tr, uc), lambda t, pos: (0, t, 0))],
            out_specs=pl.BlockSpec((tr, uc), lambda t, pos: (pos[0] * nt + t, 0))),
        out_shape=jax.ShapeDtypeStruct(_SHARD_SHAPES[w], F32),
        compiler_params=_params(32, dimension_semantics=("arbitrary",)),
    )(pos, own, recv)


def _grad_xchg3(ws, halves):
    n = len(ws)

    def body(*refs):
        cps = _x3_copies(ws)(refs[:n], refs[2 * n], refs[2 * n + 1])
        for cp in cps:
            cp.start()
        for cp in cps:
            cp.wait()

    return pl.pallas_call(
        body, name="grad_xchg3",
        out_shape=tuple(jax.ShapeDtypeStruct(_SHARD_SHAPES[w], F32) for w in ws),
        in_specs=[_ANY] * n, out_specs=[_ANY] * n,
        input_output_aliases={k: k for k in range(n)},
        scratch_shapes=[pltpu.SemaphoreType.DMA((n,)), pltpu.SemaphoreType.DMA((n,))],
        compiler_params=_params(16),
    )(*halves)


def _adamw_math(w, g, m, v):
    m = ADAM_B1 * m + (1.0 - ADAM_B1) * g
    v = ADAM_B2 * v + (1.0 - ADAM_B2) * (g * g)
    m_hat = m / ADAM_C1
    v_hat = v / ADAM_C2
    delta = -ADAM_LR * (m_hat / (jnp.sqrt(v_hat) + ADAM_EPS) + ADAM_WD * w)
    return delta, m, v


def _adamw_group(ws_, gs, ms, vs, after=()):
    n = len(ws_)

    def body(*refs):
        for k in range(n):
            w, g, m, v = (refs[j * n + k] for j in range(4))
            d, nm, nv, gc = (refs[(4 + j) * n + k] for j in range(4))
            gv = g[...]
            d[...], nm[...], nv[...] = _adamw_math(w[...], gv, m[...], v[...])
            gc[...] = gv

    vm = pl.BlockSpec(memory_space=pltpu.VMEM)
    outs = pl.pallas_call(
        _after(body, 4 * n, after), name="adamw_group",
        out_shape=tuple(jax.ShapeDtypeStruct(a.shape, F32) for _ in range(4) for a in ws_),
        in_specs=[vm] * (4 * n) + [_ANY] * len(after), out_specs=[vm] * (4 * n),
        compiler_params=_params(32),
    )(*ws_, *gs, *ms, *vs, *after)
    return [tuple(outs[j * n + k] for j in range(4)) for k in range(n)]


def _adamw(name, w, g, m, v, tr=256, after=()):
    rows, cols = w.shape

    def body(w_ref, g_ref, m_ref, v_ref, d_ref, nm_ref, nv_ref, gc_ref):
        gv = g_ref[...]
        d_ref[...], nm_ref[...], nv_ref[...] = _adamw_math(w_ref[...], gv, m_ref[...], v_ref[...])
        gc_ref[...] = gv

    spec = pl.BlockSpec((tr, cols), lambda i: (i, 0))
    return pl.pallas_call(
        _after(body, 4, after), name=name, grid=(rows // tr,),
        out_shape=tuple(jax.ShapeDtypeStruct((rows, cols), F32) for _ in range(4)),
        in_specs=[spec] * 4 + [_ANY] * len(after), out_specs=[spec] * 4,
        compiler_params=_params(32, dimension_semantics=("arbitrary",)),
    )(w, g, m, v, *after)


_REL_PAD = 384
_VEC_FIELDS = (("norm_g", 0, D_MODEL), ("b_gate", 1024, 2 * D_MODEL), ("sgu_ln_g", 3072, D_B),
               ("sgu_ln_b", 3584, D_B), ("b_s", 4096, N_GROUPS * 128), ("final_g", 4608, D_MODEL))
_LOSS_OFF = 5632
_REL_OFF = 5760
_NV = _REL_OFF + N_HEADS * _REL_PAD
_N_FIELDS = len(_VEC_FIELDS) + 2


_B_S_FIELD = [f[0] for f in _VEC_FIELDS].index("b_s")


def _assemble_row(dst, fields, transposed_b_s):
    for f, (_, off, n) in enumerate(_VEC_FIELDS):
        if transposed_b_s and f == _B_S_FIELD:
            t = fields[f][...].T
            for g in range(N_GROUPS):
                dst[:, off + 128 * g:off + 128 * (g + 1)] = t[g:g + 1, :]
        else:
            dst[:, off:off + n] = fields[f][...]
    for r in range(N_HEADS):
        dst[:, _REL_OFF + _REL_PAD * r:_REL_OFF + _REL_PAD * (r + 1)] = fields[len(_VEC_FIELDS)][r:r + 1, :]


def _small_reduce(grads, loss_row, after=()):
    n_in = _N_FIELDS + 1

    def body(*refs):
        g_refs, loss_ref = refs[:_N_FIELDS], refs[_N_FIELDS]
        out_v, out_w = refs[n_in:n_in + 2]
        mine_v, gath_v, gath_w, send_sems, recv_sems = refs[n_in + 2:]
        x, y, c, chips = _mesh_pos()
        me, sibling = (x, y, c), (x, y, 1 - c)

        _assemble_row(mine_v, g_refs, True)
        mine_v[:, _LOSS_OFF:_LOSS_OFF + 128] = loss_ref[...]
        mine_w = g_refs[-1]
        my_k = 4 * x + 2 * y + c
        gath_v[my_k] = mine_v[...]
        gath_w[my_k] = mine_w[...]

        def copy(k, gath, block, to, src=None):
            dst = gath.at[4 * block[0] + 2 * block[1] + block[2]]
            return pltpu.make_async_remote_copy(
                src_ref=dst if src is None else src, dst_ref=dst,
                send_sem=send_sems.at[k], recv_sem=recv_sems.at[k], device_id=to, device_id_type=MESH)

        bufs = ((gath_v, mine_v), (gath_w, mine_w))
        first, passed = [], []
        for b, (gath, mine) in enumerate(bufs):
            first.append(copy(7 * b, gath, me, sibling, src=mine))
            first += [copy(7 * b + 1 + j, gath, me, (*chip, c), src=mine) for j, chip in enumerate(chips)]
        for cp in first:
            cp.start()
        for b, (gath, _) in enumerate(bufs):
            for j, chip in enumerate(chips):
                copy(7 * b + 1 + j, gath, (*chip, c), me).wait_recv()
                cp = copy(7 * b + 4 + j, gath, (*chip, c), sibling)
                cp.start()
                passed.append(cp)
        for b, (gath, _) in enumerate(bufs):
            copy(7 * b, gath, sibling, me).wait_recv()
            for j, chip in enumerate(chips):
                copy(7 * b + 4 + j, gath, (*chip, 1 - c), me).wait_recv()
        for cp in first + passed:
            cp.wait_send()

        tot_v, tot_w = gath_v[0], gath_w[0]
        for k in range(1, 8):
            tot_v = tot_v + gath_v[k]
            tot_w = tot_w + gath_w[k]
        out_v[...] = tot_v
        out_w[...] = tot_w

    vm = pl.BlockSpec(memory_space=pltpu.VMEM)
    return pl.pallas_call(
        _after(body, n_in, after), name="small_reduce",
        out_shape=(jax.ShapeDtypeStruct((1, _NV), F32), jax.ShapeDtypeStruct((N_GROUPS * 128, 128), F32)),
        in_specs=[vm] * n_in + [_ANY] * len(after), out_specs=[vm] * 2,
        scratch_shapes=[pltpu.VMEM((1, _NV), F32), pltpu.VMEM((8, 1, _NV), F32),
                        pltpu.VMEM((8, N_GROUPS * 128, 128), F32),
                        pltpu.SemaphoreType.DMA((14,)), pltpu.SemaphoreType.DMA((14,))],
        compiler_params=_params(32),
    )(*grads, loss_row, *after)


def _small_adamw(tot_v, tot_w, params):
    n_in = 2 + 3 * _N_FIELDS

    def body(*refs):
        tv_ref, tw_ref = refs[:2]
        p_refs = [refs[2 + k * _N_FIELDS:2 + (k + 1) * _N_FIELDS] for k in range(3)]
        outs = refs[n_in:n_in + 4 * _N_FIELDS + 1]
        wmv = refs[-1]
        for k in range(3):
            _assemble_row(wmv.at[k], p_refs[k], False)
            wmv[k, :, _LOSS_OFF:_LOSS_OFF + 128] = jnp.zeros((1, 128), F32)
        tot_v, tot_w = tv_ref[...], tw_ref[...]
        res_v = (tot_v,) + _adamw_math(wmv[0], tot_v, wmv[1], wmv[2])
        res_w = (tot_w,) + _adamw_math(p_refs[0][-1][...], tot_w, p_refs[1][-1][...], p_refs[2][-1][...])
        for kind in range(4):
            o = outs[kind * _N_FIELDS:(kind + 1) * _N_FIELDS]
            for f, (_, off, n) in enumerate(_VEC_FIELDS):
                o[f][...] = res_v[kind][:, off:off + n]
            for r in range(N_HEADS):
                o[len(_VEC_FIELDS)][r:r + 1, :] = res_v[kind][:, _REL_OFF + _REL_PAD * r:_REL_OFF + _REL_PAD * (r + 1)]
            o[-1][...] = res_w[kind]
        outs[-1][...] = tot_v[:, _LOSS_OFF:_LOSS_OFF + 128]

    field_shapes = [(1, n) for _, _, n in _VEC_FIELDS] + [(N_HEADS, _REL_PAD), (N_GROUPS * 128, 128)]
    vm = pl.BlockSpec(memory_space=pltpu.VMEM)
    operands = [tot_v, tot_w] + [a for p in params for a in p]
    assert len(operands) == n_in
    outs = pl.pallas_call(
        body, name="small_adamw",
        out_shape=tuple(jax.ShapeDtypeStruct(s, F32) for _ in range(4) for s in field_shapes)
        + (jax.ShapeDtypeStruct((1, 128), F32),),
        in_specs=[vm] * n_in, out_specs=[vm] * (4 * _N_FIELDS + 1),
        scratch_shapes=[pltpu.VMEM((3, 1, _NV), F32)],
        compiler_params=_params(32),
    )(*operands)
    return [outs[k * _N_FIELDS:(k + 1) * _N_FIELDS] for k in range(4)], outs[-1]


def _small_fields(norm_g, b_gate, ln_g, ln_b, b_s, final_g, rel_bias, w_s):
    rel = jnp.pad(rel_bias.reshape(N_HEADS, N_REL), ((0, 0), (0, _REL_PAD - N_REL)))
    return (norm_g, b_gate, ln_g, ln_b, b_s.reshape(1, N_GROUPS * 128), final_g.reshape(1, D_MODEL),
            rel, w_s.reshape(N_GROUPS * 128, 128))


def _small_outputs(fields):
    n_g, b_g, l_g, l_b, b_s, f_g, rel, w_s = fields
    return (n_g, b_g, rel[:, :N_REL].reshape(1, N_HEADS, N_REL), l_g, l_b,
            w_s.reshape(1, N_GROUPS, 128, 128), b_s.reshape(1, N_GROUPS, 128), f_g.reshape(D_MODEL))


def _bias_row(rel_bias):
    hi = rel_bias[:, N_REL - 1:N_REL]
    lo = rel_bias[:, 0:1]
    return jnp.concatenate([jnp.broadcast_to(hi, (N_HEADS, 384)), rel_bias[:, ::-1],
                            jnp.broadcast_to(lo, (N_HEADS, 191)), jnp.broadcast_to(hi, (N_HEADS, 192))], axis=1)


def kernel(x, norm_g, w_in, b_gate, rel_bias, sgu_ln_g, sgu_ln_b, w_s, b_s, w_pa, w_pb, w_out, final_g, loss_target, m_norm_g, m_w_in, m_b_gate, m_rel_bias, m_sgu_ln_g, m_sgu_ln_b, m_w_s, m_b_s, m_w_pa, m_w_pb, m_w_out, m_final_g, v_norm_g, v_w_in, v_b_gate, v_rel_bias, v_sgu_ln_g, v_sgu_ln_b, v_w_s, v_b_s, v_w_pa, v_w_pb, v_w_out, v_final_g):
    S = x.shape[1]
    xs = x.reshape(S, D_MODEL)
    tgt = loss_target.reshape(S, D_MODEL)
    big_w = (w_in[0], w_pa[0], w_pb[0], w_out[0])
    big_m = (m_w_in[0], m_w_pa[0], m_w_pb[0], m_w_out[0])
    big_v = (v_w_in[0], v_w_pa[0], v_w_pb[0], v_w_out[0])
    rel = rel_bias[0]
    ws = w_s[0]
    bst = b_s[0].T
    fg = final_g.reshape(1, D_MODEL)
    pos = jnp.stack([lax.axis_index("c"), 2 * lax.axis_index("x") + lax.axis_index("y")]).astype(jnp.int32)

    staged = _stage_weights((1, 2, 3), big_w[1:], pos)
    w_in_bf, = _ag_weights((0,), big_w[:1])
    ag_s = _split_start("ag_small_start", staged, 9, _gather_copies((1, 2, 3)), after=(w_in_bf,))

    ht, q3, k3, v3, zrest = _inproj_fwd(xs, norm_g, w_in_bf, after=(ag_s.token,))
    gp = _bias_row(rel)
    att, lse, band_bias = _attn_fwd(q3, k3, v3, gp)
    w_pa_bf, w_pb_bf, w_out_bf = _split_wait("ag_small_wait", ag_s, _gather_copies((1, 2, 3)), att)
    (d_out, d_att, dzt, dzs, gw_out, gw_pa, gw_pb, g_bgate, g_final, loss_row,
     g_ws, g_bs_t, g_lng, g_lnb) = _tail_sgu(
        att, zrest, xs, tgt, w_pa_bf, w_pb_bf, w_out_bf, b_gate, fg, sgu_ln_g, sgu_ln_b, ws, bst)
    ws_s, ws_i = (1, 2, 3), (0,)

    x1s = _split_start("gx1s_start", [gw_pa, gw_pb, gw_out] + _x1_lands(ws_s), 12, _x1_copies(ws_s))
    dq, dk, dv, d_gp = _attn_bwd(q3, k3, v3, d_att, lse, band_bias, after=(x1s.token,))
    got = _split_wait("gx1s_wait", x1s, _x1_copies(ws_s), dq)
    cs_s, csb_s = _grad_add1_group(ws_s, got[:3], got[3:])

    x2s = _split_start("gx2s_start", csb_s + _x2_lands(ws_s), 9, _x2_copies(3))
    gw_in, gw_in_bf = _gw_in(ht, dq, dk, dv, dzt, dzs, after=(x2s.token,))
    x1i = _split_start("gx1i_start", [gw_in_bf] + _x1_lands(ws_i, BF), 4, _x1_copies(ws_i))
    got = _split_wait("gx2s_wait", x2s, _x2_copies(3), x1i.token)
    halves_s = _grad_add2_group(ws_s, cs_s, got[3:])
    x3s = _split_start("gx3s_start", halves_s, 3, _x3_copies(ws_s))
    got = _split_wait("gx1i_wait", x1i, _x1_copies(ws_i), x3s.token)
    sum_i = _grad_add1(0, gw_in, got[1], pos)

    x2i = _split_start("gx2i_start", [sum_i[1]] + _x2_lands(ws_i), 3, _x2_copies(1))
    grad_x, g_norm = _dh_gradx(dq, dk, dv, dzt, dzs, w_in_bf, xs, norm_g, d_out, after=(x2i.token,))
    g_shards_s = _split_wait("gx3s_wait", x3s, _x3_copies(ws_s), grad_x)
    big = [None] * 4
    big[1:] = _adamw_group(big_w[1:], g_shards_s, big_m[1:], big_v[1:], after=(x2i.token,))

    g_rel = jnp.pad(d_gp[:, 384:384 + N_REL][:, ::-1], ((0, 0), (0, _REL_PAD - N_REL)))
    small_grads = (g_norm, g_bgate, g_lng, g_lnb, g_bs_t, g_final, g_rel, g_ws.reshape(N_GROUPS * 128, 128))
    small_params = (_small_fields(norm_g, b_gate, sgu_ln_g, sgu_ln_b, b_s, final_g, rel_bias, w_s),
                    _small_fields(m_norm_g, m_b_gate, m_sgu_ln_g, m_sgu_ln_b, m_b_s, m_final_g, m_rel_bias, m_w_s),
                    _small_fields(v_norm_g, v_b_gate, v_sgu_ln_g, v_sgu_ln_b, v_b_s, v_final_g, v_rel_bias, v_w_s))
    tot_v, tot_w = _small_reduce(small_grads, loss_row, after=(x2i.token,))
    (gsum, sdelta, sm, sv), loss_out = _small_adamw(tot_v, tot_w, small_params)

    got = _split_wait("gx2i_wait", x2i, _x2_copies(1), loss_out)
    half_i = _grad_add2(0, sum_i[0], got[1], pos)
    g_shard_i, = _grad_xchg3(ws_i, [half_i])
    big[0] = _adamw("adamw_w_in", big_w[0], g_shard_i, big_m[0], big_v[0])
    sg_out, sd_out, sm_out, sv_out = (_small_outputs(f) for f in (gsum, sdelta, sm, sv))
    loss = loss_out[0, 0]

    def assemble(small, bigs):
        n_g, b_g, r_b, l_g, l_b, w_s_, b_s_, f_g = small
        b_in, b_pa, b_pb, b_out = (b[None] for b in bigs)
        return (n_g, b_in, b_g, r_b, l_g, l_b, w_s_, b_s_, b_pa, b_pb, b_out, f_g)

    grads_out = assemble(sg_out, [b[3] for b in big])
    delta_out = assemble(sd_out, [b[0] for b in big])
    m_out = assemble(sm_out, [b[1] for b in big])
    v_out = assemble(sv_out, [b[2] for b in big])
    return (loss, grad_x.reshape(1, S, D_MODEL), *grads_out, *delta_out, *m_out, *v_out)
```

```python
import functools
import math

import jax
import jax.numpy as jnp
from jax import lax
from jax.experimental import pallas as pl
from jax.experimental.pallas import tpu as pltpu

F32 = jnp.float32
BF = jnp.bfloat16
MESH = pl.DeviceIdType.MESH

D_MODEL = 1024
D_A = 512
D_B = 512
D_IN = 5632
N_HEADS = 8
HEAD_DIM = 64
CHUNK = 64
N_PREV = 8
SGU_CHUNK = 128
N_GROUPS = 4
N_REL = 257
EPS = 1e-6
NEG_INF = -1e30
SCALE = HEAD_DIM ** -0.5

QB = 2 * CHUNK
KB = (N_PREV + 2) * CHUNK
PADK = N_PREV * CHUNK
ROLL_W = 1024
N_RING = KB // QB
KEEP = N_RING - 1

ADAM_LR = 0.001
ADAM_B1 = 0.9
ADAM_B2 = 0.999
ADAM_EPS = 1e-08
ADAM_WD = 0.01
ADAM_STEP = 10
ADAM_C1 = 1.0 - ADAM_B1 ** ADAM_STEP
ADAM_C2 = 1.0 - ADAM_B2 ** ADAM_STEP

N_SHARD = 4
SHARD_IN = D_IN // N_SHARD
MIB = 1024 * 1024


VMEM_RESERVE_MIB = 60


def _params(vmem_mib, **kw):
    assert vmem_mib <= VMEM_RESERVE_MIB
    return pltpu.CompilerParams(vmem_limit_bytes=VMEM_RESERVE_MIB * MIB, **kw)


def _sigmoid(x):
    return 1.0 / (1.0 + jnp.exp(-x))


def _silu_and_grad(x):
    s = _sigmoid(x)
    return x * s, s * (1.0 + x * (1.0 - s))


_GELU_C = math.sqrt(2.0 / math.pi)
_GELU_A = 0.044715


def _gelu_and_grad(x):
    x2 = x * x
    t = jnp.tanh(_GELU_C * (x + _GELU_A * (x2 * x)))
    cdf = 0.5 * (1.0 + t)
    grad = cdf + 0.5 * x * (1.0 - t * t) * (_GELU_C * (1.0 + 3.0 * _GELU_A * x2))
    return x * cdf, grad


def _dot(a, b):
    return jnp.dot(a, b, preferred_element_type=F32)


def _dot_nt(a, b):
    return lax.dot_general(a, b, (((1,), (1,)), ((), ())), preferred_element_type=F32)


def _dot_tn(a, b):
    return lax.dot_general(a, b, (((0,), (0,)), ((), ())), preferred_element_type=F32)


def _mo(v, m):
    return v if isinstance(v, int) else pl.multiple_of(v, m)


def _unit_in(ref, s, p):
    return ref.at[pl.ds(_mo(p * 512, 512), 512), pl.ds(_mo(s * SHARD_IN, 128), SHARD_IN)]


def _unit_p(ref, s, p):
    return ref.at[pl.ds(_mo(p * 256, 256), 256), pl.ds(_mo(s * 256, 128), 256)]


def _unit_out(ref, s, p):
    return ref.at[pl.ds(_mo(s * 256 + p * 128, 128), 128), :]


_UNITS = (_unit_in, _unit_p, _unit_p, _unit_out)
_HALF_ROWS = (512, 256, 256, 128)
_UNIT_SHAPES = ((512, SHARD_IN), (256, 256), (256, 256), (128, D_MODEL))
_FULL_SHAPES = ((D_MODEL, D_IN), (D_A, D_MODEL), (D_B, D_MODEL), (D_MODEL, D_MODEL))
_SHARD_SHAPES = ((D_MODEL, SHARD_IN), (D_A, 256), (D_B, 256), (256, D_MODEL))


def _mesh_pos():
    x, y, c = lax.axis_index("x"), lax.axis_index("y"), lax.axis_index("c")
    chips = [(1 - x, y), (x, 1 - y), (1 - x, 1 - y)]
    return x, y, c, chips


def _ag_weights(ws, shards):
    n = len(ws)

    def body(*refs):
        ins, outs, stage = refs[:n], refs[n:2 * n], refs[2 * n:3 * n]
        send_sems, recv_sems, local_sems = refs[3 * n:]
        x, y, c, chips = _mesh_pos()
        s_me = 2 * x + y
        sibling = (x, y, 1 - c)
        for k in range(n):
            stage[k][...] = ins[k][...].astype(BF)

        def half(k, p):
            rows = _HALF_ROWS[ws[k]]
            return stage[k].at[pl.ds(_mo(p * rows, rows), rows), :]

        def unit(k, s, p):
            return _UNITS[ws[k]](outs[k], s, p)

        local = []
        for k in range(n):
            for p in range(2):
                cp = pltpu.make_async_copy(half(k, p), unit(k, s_me, p), local_sems.at[k, p])
                cp.start()
                local.append(cp)

        def rcopy(k, i, src, dst, to):
            return pltpu.make_async_remote_copy(src_ref=src, dst_ref=dst, send_sem=send_sems.at[k, i],
                                                recv_sem=recv_sems.at[k, i], device_id=to, device_id_type=MESH)

        sends = []
        for j, (cx, cy) in enumerate(chips):
            for k in range(n):
                cp = rcopy(k, j, half(k, c), unit(k, s_me, c), (cx, cy, c))
                cp.start()
                sends.append(cp)
        for j, (cx, cy) in enumerate(chips):
            for k in range(n):
                landed = unit(k, 2 * cx + cy, c)
                rcopy(k, j, landed, landed, (cx, cy, c)).wait_recv()
                cp = rcopy(k, 3 + j, landed, landed, sibling)
                cp.start()
                sends.append(cp)
        for j, (cx, cy) in enumerate(chips):
            for k in range(n):
                other = unit(k, 2 * cx + cy, 1 - c)
                rcopy(k, 3 + j, other, other, sibling).wait_recv()
        for cp in sends:
            cp.wait_send()
        for cp in local:
            cp.wait()

    vm = pl.BlockSpec(memory_space=pltpu.VMEM)
    return pl.pallas_call(
        body, name="ag_weights",
        out_shape=tuple(jax.ShapeDtypeStruct(_FULL_SHAPES[w], BF) for w in ws),
        in_specs=[vm] * n, out_specs=[_ANY] * n,
        scratch_shapes=[pltpu.VMEM(_SHARD_SHAPES[w], BF) for w in ws]
        + [pltpu.SemaphoreType.DMA((n, 6)), pltpu.SemaphoreType.DMA((n, 6)), pltpu.SemaphoreType.DMA((n, 2))],
        compiler_params=_params(40),
    )(*shards)


def _shard_of(ref, w, s):
    if w == 0:
        return ref.at[:, pl.ds(_mo(s * SHARD_IN, 128), SHARD_IN)]
    if w == 3:
        return ref.at[pl.ds(_mo(s * 256, 256), 256), :]
    return ref.at[:, pl.ds(_mo(s * 256, 128), 256)]


def _stage_weights(ws, shards, pos):
    n = len(ws)

    def body(pos_ref, *refs):
        for k in range(n):
            refs[n + k][...] = refs[k][...].astype(BF)

    def spec(w):
        shape = _SHARD_SHAPES[w]
        if w == 3:
            return pl.BlockSpec(shape, lambda i, pos: (pos[1], 0))
        return pl.BlockSpec(shape, lambda i, pos: (0, pos[1]))

    return list(pl.pallas_call(
        body, name="stage_weights",
        grid_spec=pltpu.PrefetchScalarGridSpec(
            num_scalar_prefetch=1, grid=(1,),
            in_specs=[pl.BlockSpec(_SHARD_SHAPES[w], lambda i, pos: (0, 0)) for w in ws],
            out_specs=[spec(w) for w in ws]),
        out_shape=tuple(jax.ShapeDtypeStruct(_FULL_SHAPES[w], BF) for w in ws),
        compiler_params=_params(16, dimension_semantics=("arbitrary",)),
    )(pos, *shards))


def _gather_copies(ws):
    def copies(refs, send_sems, recv_sems):
        x, y, c, chips = _mesh_pos()
        out = []
        for j, (cx, cy) in enumerate(chips):
            for k, w in enumerate(ws):
                mine = _shard_of(refs[k], w, 2 * x + y)
                out.append(pltpu.make_async_remote_copy(
                    src_ref=mine, dst_ref=mine, send_sem=send_sems.at[3 * k + j], recv_sem=recv_sems.at[3 * k + j],
                    device_id=(cx, cy, c), device_id_type=MESH))
        return out
    return copies


def _inproj_fwd(x, norm_g, w_in_bf, tm=512, after=()):
    S = x.shape[0]

    def body(x_ref, g_ref, w_ref, ht_ref, q_ref, k_ref, v_ref, zr_ref):
        xv = x_ref[...]
        r = lax.rsqrt(jnp.mean(xv * xv, axis=-1, keepdims=True) + EPS)
        hf = (xv * r) * g_ref[...]
        ht_ref[...] = hf.T.astype(BF)
        h = hf.astype(BF)
        heads = (q_ref, k_ref, v_ref)
        for j in range(D_IN // 512):
            z = _dot(h, w_ref[:, j * 512:(j + 1) * 512])
            if j < 3:
                zb = z.astype(BF)
                for hd in range(N_HEADS):
                    heads[j][hd] = zb[:, hd * HEAD_DIM:(hd + 1) * HEAD_DIM]
            else:
                zr_ref[:, (j - 3) * 512:(j - 2) * 512] = z

    head_major = jax.ShapeDtypeStruct((N_HEADS, S, HEAD_DIM), BF)
    head_spec = pl.BlockSpec((N_HEADS, tm, HEAD_DIM), lambda i: (0, i, 0))
    return pl.pallas_call(
        _after(body, 3, after), name="inproj_fwd", grid=(S // tm,),
        out_shape=(jax.ShapeDtypeStruct((D_MODEL, S), BF), head_major, head_major, head_major,
                   jax.ShapeDtypeStruct((S, D_IN - 3 * D_A), F32)),
        in_specs=[pl.BlockSpec((tm, D_MODEL), lambda i: (i, 0)),
                  pl.BlockSpec((1, D_MODEL), lambda i: (0, 0)),
                  pl.BlockSpec((D_MODEL, D_IN), lambda i: (0, 0), pipeline_mode=pl.Buffered(1))]
        + [_ANY] * len(after),
        out_specs=[pl.BlockSpec((D_MODEL, tm), lambda i: (0, i)),
                   head_spec, head_spec, head_spec,
                   pl.BlockSpec((tm, D_IN - 3 * D_A), lambda i: (i, 0))],
        compiler_params=_params(52, dimension_semantics=("arbitrary",)),
    )(x, norm_g, w_in_bf, *after)


def _skew_table(gp_row):
    row = lax.broadcasted_iota(jnp.int32, (QB, ROLL_W), 0)
    t = jnp.broadcast_to(gp_row, (QB, ROLL_W))
    for b in range(7):
        t = jnp.where(((row >> b) & 1) == 1, pltpu.roll(t, 1 << b, axis=1), t)
    return t


def _unskew_sum(d):
    row = lax.broadcasted_iota(jnp.int32, (QB, ROLL_W), 0)
    for b in range(7):
        d = jnp.where(((row >> b) & 1) == 1, pltpu.roll(d, ROLL_W - (1 << b), axis=1), d)
    return jnp.sum(d, axis=0, keepdims=True)


def _struct_mask():
    a = lax.broadcasted_iota(jnp.int32, (QB, KB), 0) // CHUNK
    b = lax.broadcasted_iota(jnp.int32, (QB, KB), 1) // CHUNK
    return (b >= a) & (b <= a + N_PREV)


def _load_kv(k_hbm, v_hbm, k_scr, v_scr, sems, S, meanwhile=lambda: None):
    zeros = jnp.zeros((N_HEADS, PADK, HEAD_DIM), BF)
    k_scr[:, 0:PADK, :] = zeros
    v_scr[:, 0:PADK, :] = zeros
    ck = pltpu.make_async_copy(k_hbm, k_scr.at[:, pl.ds(PADK, S), :], sems.at[0])
    cv = pltpu.make_async_copy(v_hbm, v_scr.at[:, pl.ds(PADK, S), :], sems.at[1])
    ck.start()
    cv.start()
    meanwhile()
    ck.wait()
    cv.wait()


_BATCH_NT = (((2,), (2,)), ((0,), (0,)))
_BATCH_NN = (((2,), (1,)), ((0,), (0,)))
_BATCH_TN = (((1,), (1,)), ((0,), (0,)))


def _bdot(a, b, dims):
    return lax.dot_general(a, b, dims, preferred_element_type=F32)


def _scaled(q):
    return q * jnp.asarray(SCALE, BF)


def _scores(qs, kb, bias, i, front):
    s = _bdot(qs, kb, _BATCH_NT) + bias
    if front:
        col = lax.broadcasted_iota(jnp.int32, (1, 1, KB), 2)
        s = jnp.where(col >= PADK - i * QB, s, NEG_INF)
    return s


def _attn_fwd(q3, k3, v3, gp):
    S = q3.shape[1]

    def body(q_ref, k_hbm, v_hbm, gp_ref, o_ref, lse_ref, bias_ref, k_scr, v_scr, sems):
        i = pl.program_id(0)

        @pl.when(i == 0)
        def _():
            def build_bias():
                keep = _struct_mask()
                for h in range(N_HEADS):
                    bias_ref[h] = jnp.where(keep, _skew_table(gp_ref[h:h + 1, :])[:, :KB], NEG_INF)
            _load_kv(k_hbm, v_hbm, k_scr, v_scr, sems, S, build_bias)

        def step(front):
            start = pl.multiple_of(i * QB, QB)
            kb = k_scr[:, pl.ds(start, KB), :]
            vb = v_scr[:, pl.ds(start, KB), :]
            s = _scores(_scaled(q_ref[...]), kb, bias_ref[...], i, front)
            m = jnp.max(s, axis=-1, keepdims=True)
            e = jnp.exp(s - m)
            l = jnp.sum(e, axis=-1, keepdims=True)
            p = e * (1.0 / l)
            o = _bdot(p.astype(BF), vb, _BATCH_NN)
            lse_ref[...] = jnp.broadcast_to(m + jnp.log(l), (N_HEADS, QB, 128))
            for h in range(N_HEADS):
                o_ref[:, h * HEAD_DIM:(h + 1) * HEAD_DIM] = o[h]

        pl.when(i < KEEP)(functools.partial(step, True))
        pl.when(i >= KEEP)(functools.partial(step, False))

    kv_scr = pltpu.VMEM((N_HEADS, S + PADK, HEAD_DIM), BF)
    return pl.pallas_call(
        body, name="attn_fwd", grid=(S // QB,),
        out_shape=(jax.ShapeDtypeStruct((S, D_A), F32), jax.ShapeDtypeStruct((N_HEADS, S, 128), F32),
                   jax.ShapeDtypeStruct((N_HEADS, QB, KB), F32)),
        in_specs=[pl.BlockSpec((N_HEADS, QB, HEAD_DIM), lambda i: (0, i, 0)),
                  pl.BlockSpec(memory_space=pl.ANY), pl.BlockSpec(memory_space=pl.ANY),
                  pl.BlockSpec((N_HEADS, ROLL_W), lambda i: (0, 0))],
        out_specs=[pl.BlockSpec((QB, D_A), lambda i: (i, 0)),
                   pl.BlockSpec((N_HEADS, QB, 128), lambda i: (0, i, 0)),
                   pl.BlockSpec((N_HEADS, QB, KB), lambda i: (0, 0, 0))],
        scratch_shapes=[kv_scr, kv_scr, pltpu.SemaphoreType.DMA((2,))],
        compiler_params=_params(48, dimension_semantics=("arbitrary",)),
    )(q3, k3, v3, gp)


def _attn_bwd(q3, k3, v3, d_att3, lse, bias, after=()):
    S = q3.shape[1]
    nq = S // QB

    def body(q_ref, do_ref, k_hbm, v_hbm, lse_ref, bias_ref, dq_ref, dk_ref, dv_ref, dgp_ref,
             k_scr, v_scr, dk_acc, dv_acc, dbias_acc, pad_scr, sems):
        i = pl.program_id(0)

        @pl.when(i == 0)
        def _():
            def clear():
                dk_acc[...] = jnp.zeros_like(dk_acc)
                dv_acc[...] = jnp.zeros_like(dv_acc)
                dbias_acc[...] = jnp.zeros_like(dbias_acc)
            _load_kv(k_hbm, v_hbm, k_scr, v_scr, sems, S, clear)

        def slot(r):
            return slice((r % N_RING) * QB, (r % N_RING + 1) * QB)

        def step(front, r):
            start = pl.multiple_of(i * QB, QB)
            kb = k_scr[:, pl.ds(start, KB), :]
            vb = v_scr[:, pl.ds(start, KB), :]
            qs = _scaled(q_ref[...])
            do = do_ref[...]
            p = jnp.exp(_scores(qs, kb, bias_ref[...], i, front) - jnp.tile(lse_ref[...], (1, 1, KB // 128)))
            dp = _bdot(do, vb, _BATCH_NT)
            ds = p * (dp - jnp.sum(dp * p, axis=-1, keepdims=True))
            dbias_acc[...] += ds
            dsb = ds.astype(BF)
            dq = _bdot(dsb, kb, _BATCH_NN) * SCALE
            for h in range(N_HEADS):
                dq_ref[:, h * HEAD_DIM:(h + 1) * HEAD_DIM] = dq[h].astype(BF)
            dk = _bdot(dsb, qs, _BATCH_TN)
            dv = _bdot(p.astype(BF), do, _BATCH_TN)
            for b in range(N_RING):
                dk_acc[:, slot(r + b), :] += dk[:, b * QB:(b + 1) * QB, :]
                dv_acc[:, slot(r + b), :] += dv[:, b * QB:(b + 1) * QB, :]

        def emit(r):
            for h in range(N_HEADS):
                hs = slice(h * HEAD_DIM, (h + 1) * HEAD_DIM)
                dk_ref[:, hs] = dk_acc[h, slot(r), :].astype(BF)
                dv_ref[:, hs] = dv_acc[h, slot(r), :].astype(BF)
            dk_acc[:, slot(r), :] = jnp.zeros((N_HEADS, QB, HEAD_DIM), F32)
            dv_acc[:, slot(r), :] = jnp.zeros((N_HEADS, QB, HEAD_DIM), F32)

        phase = lax.rem(i, N_RING)
        for r in range(KEEP):
            pl.when(i == r)(functools.partial(step, True, r))
        for r in range(N_RING):
            pl.when((i >= KEEP) & (i < nq) & (phase == r))(functools.partial(step, False, r))
        for r in range(N_RING):
            pl.when(phase == r)(functools.partial(emit, r))

        @pl.when(i == nq + KEEP - 1)
        def _():
            lane = lax.broadcasted_iota(jnp.int32, (1, ROLL_W), 1)
            hi = (lane < 384) | (lane >= 832)
            lo = (lane > 640) & (lane < 832)
            pad_scr[...] = jnp.zeros_like(pad_scr)
            for h in range(N_HEADS):
                pad_scr[:, 0:KB] = dbias_acc[h]
                g = _unskew_sum(pad_scr[...])
                s_hi = jnp.sum(jnp.where(hi, g, 0.0), axis=-1, keepdims=True)
                s_lo = jnp.sum(jnp.where(lo, g, 0.0), axis=-1, keepdims=True)
                g = jnp.where(lane == 384, g + s_hi, g)
                g = jnp.where(lane == 640, g + s_lo, g)
                dgp_ref[h:h + 1, :] = g

    last = nq - 1
    kv_scr = pltpu.VMEM((N_HEADS, S + PADK, HEAD_DIM), BF)
    return pl.pallas_call(
        _after(body, 6, after), name="attn_bwd", grid=(nq + KEEP,),
        out_shape=(jax.ShapeDtypeStruct((S, D_A), BF), jax.ShapeDtypeStruct((S, D_A), BF),
                   jax.ShapeDtypeStruct((S, D_A), BF), jax.ShapeDtypeStruct((N_HEADS, ROLL_W), F32)),
        in_specs=[pl.BlockSpec((N_HEADS, QB, HEAD_DIM), lambda i: (0, jnp.minimum(i, last), 0)),
                  pl.BlockSpec((N_HEADS, QB, HEAD_DIM), lambda i: (0, jnp.minimum(i, last), 0)),
                  pl.BlockSpec(memory_space=pl.ANY), pl.BlockSpec(memory_space=pl.ANY),
                  pl.BlockSpec((N_HEADS, QB, 128), lambda i: (0, jnp.minimum(i, last), 0)),
                  pl.BlockSpec((N_HEADS, QB, KB), lambda i: (0, 0, 0))] + [_ANY] * len(after),
        out_specs=[pl.BlockSpec((QB, D_A), lambda i: (jnp.minimum(i, last), 0)),
                   pl.BlockSpec((QB, D_A), lambda i: (jnp.maximum(i - KEEP, 0), 0)),
                   pl.BlockSpec((QB, D_A), lambda i: (jnp.maximum(i - KEEP, 0), 0)),
                   pl.BlockSpec((N_HEADS, ROLL_W), lambda i: (0, 0))],
        scratch_shapes=[kv_scr, kv_scr,
                        pltpu.VMEM((N_HEADS, KB, HEAD_DIM), F32), pltpu.VMEM((N_HEADS, KB, HEAD_DIM), F32),
                        pltpu.VMEM((N_HEADS, QB, KB), F32), pltpu.VMEM((QB, ROLL_W), F32),
                        pltpu.SemaphoreType.DMA((2,))],
        compiler_params=_params(56, dimension_semantics=("arbitrary",)),
    )(q3, d_att3, k3, v3, lse, bias, *after)


def _sgu_core(ub, vb, lg, lb):
    u, du = _gelu_and_grad(ub)
    v, dv = _gelu_and_grad(vb)
    mu = jnp.mean(v, axis=-1, keepdims=True)
    vc = v - mu
    rstd = lax.rsqrt(jnp.mean(vc * vc, axis=-1, keepdims=True) + EPS)
    xh = vc * rstd
    vn = xh * lg + lb
    return u, du, dv, rstd, xh, vn


def _tri():
    r = lax.broadcasted_iota(jnp.int32, (SGU_CHUNK, SGU_CHUNK), 0)
    c = lax.broadcasted_iota(jnp.int32, (SGU_CHUNK, SGU_CHUNK), 1)
    return r >= c


def _tail_sgu(att, zrest, x, target, w_pa, w_pb, w_out, b_gate, final_g, ln_g, ln_b, w_s, b_s_t, tm=256):
    S = x.shape[0]
    nt = S // tm
    chunks = tm // SGU_CHUNK

    def body(att_ref, ga_ref, ub_ref, vb_ref, gb_ref, gta_ref, gtb_ref, x_ref, t_ref,
             wpa_ref, wpb_ref, wout_ref, bg_ref, fg_ref, lg_ref, lb_ref, ws_ref, bst_ref,
             dout_ref, datt_ref, dzt_ref, dzs_ref, gwout_hbm, gwpa_hbm, gwpb_hbm,
             gbg_ref, gfg_ref, loss_ref, gws_ref, gbs_ref, glg_ref, glb_ref,
             acc_out, acc_pa, acc_pb, sg_scr, mix_scr, dvn_scr, bs_acc, sems):
        i = pl.program_id(0)

        @pl.when(i == 0)
        def _():
            for r in (acc_out, acc_pa, acc_pb, gbg_ref, gfg_ref, loss_ref, gws_ref, glg_ref, glb_ref, bs_acc):
                r[...] = jnp.zeros_like(r)

        u, du, dv, rstd, xh, vn = _sgu_core(ub_ref[...], vb_ref[...], lg_ref[...], lb_ref[...])
        vnb = vn.astype(BF)
        tri = _tri()
        blocks = [(g, slice(n * SGU_CHUNK, (n + 1) * SGU_CHUNK), slice(g * 128, (g + 1) * 128))
                  for g in range(N_GROUPS) for n in range(chunks)]
        wts = [jnp.where(tri, ws_ref[g], 0.0) for g in range(N_GROUPS)]
        for g, rs, cs in blocks:
            mixed = _dot(wts[g].astype(BF), vnb[rs, cs]) + bst_ref[:, g:g + 1]
            mix_scr[rs, cs] = mixed
            sg_scr[rs, cs] = u[rs, cs] * mixed

        att = att_ref[...]
        sg = sg_scr[...]
        sa, dsa = _silu_and_grad(ga_ref[...])
        sb, dsb = _silu_and_grad(gb_ref[...])
        ya = (att * sa).astype(BF)
        yb = (sg * sb).astype(BF)
        pa = _dot(ya, wpa_ref[...])
        pb = _dot(yb, wpb_ref[...])
        ga = _sigmoid(gta_ref[...] + bg_ref[:, 0:D_MODEL])
        gb = _sigmoid(gtb_ref[...] + bg_ref[:, D_MODEL:2 * D_MODEL])
        merged = (ga * pa + gb * pb).astype(BF)
        out = x_ref[...] + _dot(merged, wout_ref[...])
        r2 = lax.rsqrt(jnp.mean(out * out, axis=-1, keepdims=True) + EPS)
        nrm = out * r2
        fg = fg_ref[...]
        err = nrm * fg - t_ref[...]
        loss_ref[...] += 0.5 * jnp.sum(jnp.mean(err * err, axis=-1, keepdims=True))
        dy = err * (1.0 / D_MODEL)
        gfg_ref[...] += jnp.sum(dy * nrm, axis=0, keepdims=True)
        dn = dy * fg
        d_out = r2 * (dn - nrm * jnp.mean(dn * nrm, axis=-1, keepdims=True))
        dout_ref[...] = d_out
        d_outb = d_out.astype(BF)
        acc_out[...] += _dot_tn(merged, d_outb)
        dm = _dot_nt(d_outb, wout_ref[...])
        d_pa = (dm * ga).astype(BF)
        d_pb = (dm * gb).astype(BF)
        d_gta = dm * pa * (ga * (1.0 - ga))
        d_gtb = dm * pb * (gb * (1.0 - gb))
        gbg_ref[:, 0:D_MODEL] += jnp.sum(d_gta, axis=0, keepdims=True)
        gbg_ref[:, D_MODEL:2 * D_MODEL] += jnp.sum(d_gtb, axis=0, keepdims=True)
        dzt_ref[:, 2 * D_A:2 * D_A + D_MODEL] = d_gta.astype(BF)
        dzt_ref[:, 2 * D_A + D_MODEL:] = d_gtb.astype(BF)
        acc_pa[...] += _dot_tn(ya, d_pa)
        acc_pb[...] += _dot_tn(yb, d_pb)
        d_ya = _dot_nt(d_pa, wpa_ref[...])
        d_yb = _dot_nt(d_pb, wpb_ref[...])
        d_att = (d_ya * sa).astype(BF)
        for hd in range(N_HEADS):
            datt_ref[hd] = d_att[:, hd * HEAD_DIM:(hd + 1) * HEAD_DIM]
        dzt_ref[:, 0:D_A] = (d_ya * att * dsa).astype(BF)
        dzt_ref[:, D_A:2 * D_A] = (d_yb * sg * dsb).astype(BF)

        dsg = d_yb * sb
        dzs_ref[:, 0:D_B] = (dsg * mix_scr[...] * du).astype(BF)
        dmix = dsg * u
        for g, rs, cs in blocks:
            dmb = dmix[rs, cs].astype(BF)
            bs_acc[:, cs] += dmix[rs, cs]
            gws_ref[g] += _dot_nt(dmb, vnb[rs, cs])
            dvn_scr[rs, cs] = _dot(wts[g].T.astype(BF), dmb)
        dvn = dvn_scr[...]
        glg_ref[...] += jnp.sum(dvn * xh, axis=0, keepdims=True)
        glb_ref[...] += jnp.sum(dvn, axis=0, keepdims=True)
        dxh = dvn * lg_ref[...]
        dvv = rstd * (dxh - jnp.mean(dxh, axis=-1, keepdims=True)
                      - xh * jnp.mean(dxh * xh, axis=-1, keepdims=True))
        dzs_ref[:, D_B:2 * D_B] = (dvv * dv).astype(BF)

        @pl.when(i == nt - 1)
        def _():
            cps = [pltpu.make_async_copy(acc_out, gwout_hbm, sems.at[0]),
                   pltpu.make_async_copy(acc_pa, gwpa_hbm, sems.at[1]),
                   pltpu.make_async_copy(acc_pb, gwpb_hbm, sems.at[2])]
            for cp in cps:
                cp.start()
            lane = lax.broadcasted_iota(jnp.int32, (SGU_CHUNK, 128), 1)
            cols = jnp.zeros((SGU_CHUNK, 128), F32)
            for g in range(N_GROUPS):
                gws_ref[g] = jnp.where(tri, gws_ref[g], 0.0)
                col = jnp.sum(bs_acc[:, g * 128:(g + 1) * 128], axis=-1, keepdims=True)
                cols = jnp.where(lane == g, col, cols)
            gbs_ref[...] = cols
            for cp in cps:
                cp.wait()

    c2 = lambda i: (0, 0)
    c3 = lambda i: (0, 0, 0)
    zcol = lambda w, blk: pl.BlockSpec((tm, w), lambda i: (i, blk))
    row = lambda w: pl.BlockSpec((tm, w), lambda i: (i, 0))
    return pl.pallas_call(
        body, name="tail", grid=(nt,),
        out_shape=(jax.ShapeDtypeStruct((S, D_MODEL), F32), jax.ShapeDtypeStruct((N_HEADS, S, HEAD_DIM), BF),
                   jax.ShapeDtypeStruct((S, 3072), BF), jax.ShapeDtypeStruct((S, 2 * D_B), BF),
                   jax.ShapeDtypeStruct((D_MODEL, D_MODEL), F32), jax.ShapeDtypeStruct((D_A, D_MODEL), F32),
                   jax.ShapeDtypeStruct((D_B, D_MODEL), F32),
                   jax.ShapeDtypeStruct((1, 2 * D_MODEL), F32), jax.ShapeDtypeStruct((1, D_MODEL), F32),
                   jax.ShapeDtypeStruct((1, 128), F32),
                   jax.ShapeDtypeStruct((N_GROUPS, 128, 128), F32), jax.ShapeDtypeStruct((SGU_CHUNK, 128), F32),
                   jax.ShapeDtypeStruct((1, D_B), F32), jax.ShapeDtypeStruct((1, D_B), F32)),
        in_specs=[row(D_A), zcol(512, 0), zcol(512, 1), zcol(512, 2), zcol(512, 3),
                  zcol(D_MODEL, 2), zcol(D_MODEL, 3), row(D_MODEL), row(D_MODEL),
                  pl.BlockSpec((D_A, D_MODEL), c2), pl.BlockSpec((D_B, D_MODEL), c2),
                  pl.BlockSpec((D_MODEL, D_MODEL), c2),
                  pl.BlockSpec((1, 2 * D_MODEL), c2), pl.BlockSpec((1, D_MODEL), c2),
                  pl.BlockSpec((1, D_B), c2), pl.BlockSpec((1, D_B), c2),
                  pl.BlockSpec((N_GROUPS, 128, 128), c3), pl.BlockSpec((128, N_GROUPS), c2)],
        out_specs=[row(D_MODEL), pl.BlockSpec((N_HEADS, tm, HEAD_DIM), lambda i: (0, i, 0)),
                   row(3072), row(2 * D_B), _ANY, _ANY, _ANY,
                   pl.BlockSpec((1, 2 * D_MODEL), c2), pl.BlockSpec((1, D_MODEL), c2),
                   pl.BlockSpec((1, 128), c2),
                   pl.BlockSpec((N_GROUPS, 128, 128), c3), pl.BlockSpec((SGU_CHUNK, 128), c2),
                   pl.BlockSpec((1, D_B), c2), pl.BlockSpec((1, D_B), c2)],
        scratch_shapes=[pltpu.VMEM((D_MODEL, D_MODEL), F32), pltpu.VMEM((D_A, D_MODEL), F32),
                        pltpu.VMEM((D_B, D_MODEL), F32),
                        pltpu.VMEM((tm, D_B), F32), pltpu.VMEM((tm, D_B), F32), pltpu.VMEM((tm, D_B), F32),
                        pltpu.VMEM((SGU_CHUNK, D_B), F32), pltpu.SemaphoreType.DMA((3,))],
        compiler_params=_params(58, dimension_semantics=("arbitrary",)),
    )(att, zrest, zrest, zrest, zrest, zrest, zrest, x, target, w_pa, w_pb, w_out, b_gate, final_g,
      ln_g, ln_b, w_s, b_s_t)


_DZ_MAP = ((0, 0), (1, 0), (2, 0), (3, 0), (4, 0), (4, 1), (3, 1), (3, 2), (3, 3), (3, 4), (3, 5))


def _dh_gradx(dq, dk, dv, dzt, dzs, w_in_bf, x, norm_g, d_out, tm=512, after=()):
    S = x.shape[0]

    def body(dq_ref, dk_ref, dv_ref, dzt_ref, dzs_ref, w_ref, x_ref, g_ref, dout_ref, gx_ref, gn_ref):
        i = pl.program_id(0)

        @pl.when(i == 0)
        def _():
            gn_ref[...] = jnp.zeros_like(gn_ref)

        pieces = (dq_ref, dk_ref, dv_ref, dzt_ref, dzs_ref)
        dh = jnp.zeros((tm, D_MODEL), F32)
        for j, (pc, blk) in enumerate(_DZ_MAP):
            dh += _dot_nt(pieces[pc][:, blk * 512:(blk + 1) * 512], w_ref[:, j * 512:(j + 1) * 512])
        xv = x_ref[...]
        r = lax.rsqrt(jnp.mean(xv * xv, axis=-1, keepdims=True) + EPS)
        nrm = xv * r
        gn_ref[...] += jnp.sum(dh * nrm, axis=0, keepdims=True)
        dn = dh * g_ref[...]
        gx_ref[...] = r * (dn - nrm * jnp.mean(dn * nrm, axis=-1, keepdims=True)) + dout_ref[...]

    row = lambda w: pl.BlockSpec((tm, w), lambda i: (i, 0))
    c2 = lambda i: (0, 0)
    return pl.pallas_call(
        _after(body, 9, after), name="dh_gradx", grid=(S // tm,),
        out_shape=(jax.ShapeDtypeStruct((S, D_MODEL), F32), jax.ShapeDtypeStruct((1, D_MODEL), F32)),
        in_specs=[row(512), row(512), row(512), row(3072), row(1024),
                  pl.BlockSpec((D_MODEL, D_IN), c2, pipeline_mode=pl.Buffered(1)), row(D_MODEL),
                  pl.BlockSpec((1, D_MODEL), c2), row(D_MODEL)]
        + [_ANY] * len(after),
        out_specs=[row(D_MODEL), pl.BlockSpec((1, D_MODEL), c2)],
        compiler_params=_params(48, dimension_semantics=("arbitrary",)),
    )(dq, dk, dv, dzt, dzs, w_in_bf, x, norm_g, d_out, *after)


def _gw_in(ht, dq, dk, dv, dzt, dzs, tn=512, after=()):
    S = ht.shape[1]
    per = 512 // tn
    cols = tuple((pc, per * blk + h) for pc, blk in _DZ_MAP for h in range(per))

    def body(ht_ref, dq_ref, dk_ref, dv_ref, dzt_ref, dzs_ref, o_ref, ob_ref):
        j = pl.program_id(0)
        pieces = (dq_ref, dk_ref, dv_ref, dzt_ref, dzs_ref)
        for pc in range(5):
            hit = functools.reduce(jnp.logical_or, [j == jj for jj, (p, _) in enumerate(cols) if p == pc])

            @pl.when(hit)
            def _(pc=pc):
                g = _dot(ht_ref[...], pieces[pc][...])
                o_ref[...] = g
                ob_ref[...] = g.astype(BF)

    def piece_spec(pc):
        cur = next(blk for p, blk in cols if p == pc)
        held = []
        for p, blk in cols:
            cur = blk if p == pc else cur
            held.append(cur)

        def index_map(j):
            blk = jnp.int32(held[0])
            for jj in range(1, len(held)):
                if held[jj] != held[jj - 1]:
                    blk = jnp.where(j >= jj, jnp.int32(held[jj]), blk)
            return (0, blk)

        return pl.BlockSpec((S, tn), index_map)

    return pl.pallas_call(
        _after(body, 6, after), name="gw_in", grid=(len(cols),),
        out_shape=(jax.ShapeDtypeStruct((D_MODEL, D_IN), F32), jax.ShapeDtypeStruct((D_MODEL, D_IN), BF)),
        in_specs=[pl.BlockSpec((D_MODEL, S), lambda j: (0, 0), pipeline_mode=pl.Buffered(1))]
        + [piece_spec(pc) for pc in range(5)]
        + [_ANY] * len(after),
        out_specs=[pl.BlockSpec((D_MODEL, tn), lambda j: (0, j)), pl.BlockSpec((D_MODEL, tn), lambda j: (0, j))],
        compiler_params=_params(56, dimension_semantics=("arbitrary",)),
    )(ht, dq, dk, dv, dzt, dzs, *after)


_HBM = pl.BlockSpec(memory_space=pltpu.HBM)
_SEM = pl.BlockSpec(memory_space=pltpu.SEMAPHORE)
_ANY = pl.BlockSpec(memory_space=pl.ANY)
_EFFECT = pltpu.SideEffectType.DATAFLOW_SIDE_EFFECTING


def _in_hbm(a):
    return pltpu.with_memory_space_constraint(a, pltpu.HBM)


def _after(body, n_in, after):
    if not after:
        return body
    return lambda *refs: body(*refs[:n_in], *refs[n_in + len(after):])


class _Started:
    def __init__(self, send, recv, bufs, token):
        self.send, self.recv, self.bufs, self.token = send, recv, bufs, token


def _split_start(name, bufs, n_copies, copies, after=()):
    nb = len(bufs)

    def body(*refs):
        refs = refs[:nb] + refs[nb + len(after):]
        for cp in copies(refs[:nb], refs[nb], refs[nb + 1]):
            cp.start()
        refs[-1][...] = jnp.zeros_like(refs[-1])

    outs = pl.pallas_call(
        body, name=name,
        out_shape=(pltpu.SemaphoreType.DMA((n_copies,)), pltpu.SemaphoreType.DMA((n_copies,)),
                   *[pltpu.HBM(b.shape, b.dtype) for b in bufs], jax.ShapeDtypeStruct((8, 128), F32)),
        in_specs=[_HBM] * nb + [_ANY] * len(after),
        out_specs=(_SEM, _SEM, *[_HBM] * nb, pl.BlockSpec(memory_space=pltpu.VMEM)),
        input_output_aliases={k: 2 + k for k in range(nb)},
        compiler_params=_params(1, has_side_effects=_EFFECT),
    )(*[_in_hbm(b) for b in bufs], *after)
    return _Started(outs[0], outs[1], list(outs[2:2 + nb]), outs[-1])


def _split_wait(name, started, copies, after):
    nb = len(started.bufs)

    def body(*refs):
        for cp in copies(refs[:nb], refs[nb], refs[nb + 1]):
            cp.wait_send()
            cp.wait_recv()

    return list(pl.pallas_call(
        body, name=name,
        out_shape=tuple(pltpu.HBM(b.shape, b.dtype) for b in started.bufs),
        in_specs=[_HBM] * nb + [_SEM, _SEM, _ANY],
        out_specs=tuple([_HBM] * nb),
        input_output_aliases={k: k for k in range(nb)},
        compiler_params=_params(1, has_side_effects=_EFFECT),
    )(*started.bufs, started.send, started.recv, after))


def _x1_copies(ws):
    def copies(refs, send_sems, recv_sems):
        x, y, c, _ = _mesh_pos()
        out = []
        for k, w in enumerate(ws):
            for s in range(N_SHARD):
                out.append(pltpu.make_async_remote_copy(
                    src_ref=_UNITS[w](refs[k], s, 1 - c), dst_ref=refs[len(ws) + k].at[s],
                    send_sem=send_sems.at[N_SHARD * k + s], recv_sem=recv_sems.at[N_SHARD * k + s],
                    device_id=(x, y, 1 - c), device_id_type=MESH))
        return out
    return copies


def _x2_copies(n):
    def copies(refs, send_sems, recv_sems):
        x, y, c, chips = _mesh_pos()
        out = []
        for j, (cx, cy) in enumerate(chips):
            for k in range(n):
                out.append(pltpu.make_async_remote_copy(
                    src_ref=refs[k].at[2 * cx + cy], dst_ref=refs[n + k].at[j],
                    send_sem=send_sems.at[3 * k + j], recv_sem=recv_sems.at[3 * k + j],
                    device_id=(cx, cy, c), device_id_type=MESH))
        return out
    return copies


def _x3_copies(ws):
    def copies(refs, send_sems, recv_sems):
        x, y, c, _ = _mesh_pos()
        out = []
        for k, w in enumerate(ws):
            rows = _HALF_ROWS[w]
            mine = refs[k].at[pl.ds(_mo(c * rows, rows), rows), :]
            out.append(pltpu.make_async_remote_copy(
                src_ref=mine, dst_ref=mine, send_sem=send_sems.at[k], recv_sem=recv_sems.at[k],
                device_id=(x, y, 1 - c), device_id_type=MESH))
        return out
    return copies


def _x1_lands(ws, dtype=F32):
    return [lax.empty((N_SHARD,) + _UNIT_SHAPES[w], dtype) for w in ws]


def _x2_lands(ws):
    return [lax.empty((3,) + _UNIT_SHAPES[w], BF) for w in ws]


def _grad_add1(w, g, recv, pos):
    ur, uc = _UNIT_SHAPES[w]

    def body(pos_ref, g_ref, r_ref, own_ref, csb_ref):
        v = g_ref[...] + r_ref[0].astype(F32)
        csb_ref[0] = v.astype(BF)

        @pl.when(pl.program_id(0) == pos_ref[1])
        def _():
            own_ref[...] = v

    u3 = lambda s, pos: (s, 0, 0)
    return pl.pallas_call(
        body, name=f"grad_add1_{w}",
        grid_spec=pltpu.PrefetchScalarGridSpec(
            num_scalar_prefetch=1, grid=(N_SHARD,),
            in_specs=[pl.BlockSpec((ur, uc), lambda s, pos: (pos[0], s)), pl.BlockSpec((1, ur, uc), u3)],
            out_specs=[pl.BlockSpec((ur, uc), lambda s, pos: (0, 0)), pl.BlockSpec((1, ur, uc), u3)]),
        out_shape=(jax.ShapeDtypeStruct((ur, uc), F32), jax.ShapeDtypeStruct((N_SHARD, ur, uc), BF)),
        compiler_params=_params(40, dimension_semantics=("arbitrary",)),
    )(pos, g, recv)


def _grad_add1_group(ws, gs, recvs):
    n = len(ws)

    def body(*refs):
        c = lax.axis_index("c")
        for k, w in enumerate(ws):
            g, r, cs, csb = refs[k], refs[n + k], refs[2 * n + k], refs[3 * n + k]
            for s in range(N_SHARD):
                v = _UNITS[w](g, s, c)[...] + r[s]
                cs[s] = v
                csb[s] = v.astype(BF)

    vm = pl.BlockSpec(memory_space=pltpu.VMEM)
    outs = pl.pallas_call(
        body, name="grad_add1_group",
        out_shape=tuple(jax.ShapeDtypeStruct((N_SHARD,) + _UNIT_SHAPES[w], dt) for dt in (F32, BF) for w in ws),
        in_specs=[vm] * (2 * n), out_specs=[vm] * (2 * n),
        compiler_params=_params(32),
    )(*gs, *recvs)
    return list(outs[:n]), list(outs[n:])


def _grad_add2_group(ws, css, recvs):
    n = len(ws)

    def body(*refs):
        x, y, c, _ = _mesh_pos()
        for k, w in enumerate(ws):
            cs, r, o = refs[k], refs[n + k], refs[2 * n + k]
            rows = _HALF_ROWS[w]
            total = ((cs[2 * x + y] + r[0].astype(F32)) + r[1].astype(F32)) + r[2].astype(F32)
            o[pl.ds(_mo(c * rows, rows), rows), :] = total

    vm = pl.BlockSpec(memory_space=pltpu.VMEM)
    return list(pl.pallas_call(
        body, name="grad_add2_group",
        out_shape=tuple(jax.ShapeDtypeStruct(_SHARD_SHAPES[w], F32) for w in ws),
        in_specs=[vm] * (2 * n), out_specs=[vm] * n,
        compiler_params=_params(32),
    )(*css, *recvs))


def _grad_add2(w, own, recv, pos):
    ur, uc = _UNIT_SHAPES[w]
    nt = 4
    tr = ur // nt

    def body(pos_ref, own_ref, r_ref, o_ref):
        o_ref[...] = ((own_ref[...] + r_ref[0].astype(F32)) + r_ref[1].astype(F32)) + r_ref[2].astype(F32)

    return pl.pallas_call(
        body, name=f"grad_add2_{w}",
        grid_spec=pltpu.PrefetchScalarGridSpec(
            num_scalar_prefetch=1, grid=(nt,),
            in_specs=[pl.BlockSpec((tr, uc), lambda t, pos: (t, 0)),
                      pl.BlockSpec((3, tr, uc), lambda t, pos: (0, t, 0))],
            out_specs=pl.BlockSpec((tr, uc), lambda t, pos: (pos[0] * nt + t, 0))),
        out_shape=jax.ShapeDtypeStruct(_SHARD_SHAPES[w], F32),
        compiler_params=_params(32, dimension_semantics=("arbitrary",)),
    )(pos, own, recv)


def _grad_xchg3(ws, halves):
    n = len(ws)

    def body(*refs):
        cps = _x3_copies(ws)(refs[:n], refs[2 * n], refs[2 * n + 1])
        for cp in cps:
            cp.start()
        for cp in cps:
            cp.wait()

    return pl.pallas_call(
        body, name="grad_xchg3",
        out_shape=tuple(jax.ShapeDtypeStruct(_SHARD_SHAPES[w], F32) for w in ws),
        in_specs=[_ANY] * n, out_specs=[_ANY] * n,
        input_output_aliases={k: k for k in range(n)},
        scratch_shapes=[pltpu.SemaphoreType.DMA((n,)), pltpu.SemaphoreType.DMA((n,))],
        compiler_params=_params(16),
    )(*halves)


def _adamw_math(w, g, m, v):
    m = ADAM_B1 * m + (1.0 - ADAM_B1) * g
    v = ADAM_B2 * v + (1.0 - ADAM_B2) * (g * g)
    m_hat = m / ADAM_C1
    v_hat = v / ADAM_C2
    delta = -ADAM_LR * (m_hat / (jnp.sqrt(v_hat) + ADAM_EPS) + ADAM_WD * w)
    return delta, m, v


def _adamw_group(ws_, gs, ms, vs, after=()):
    n = len(ws_)

    def body(*refs):
        for k in range(n):
            w, g, m, v = (refs[j * n + k] for j in range(4))
            d, nm, nv, gc = (refs[(4 + j) * n + k] for j in range(4))
            gv = g[...]
            d[...], nm[...], nv[...] = _adamw_math(w[...], gv, m[...], v[...])
            gc[...] = gv

    vm = pl.BlockSpec(memory_space=pltpu.VMEM)
    outs = pl.pallas_call(
        _after(body, 4 * n, after), name="adamw_group",
        out_shape=tuple(jax.ShapeDtypeStruct(a.shape, F32) for _ in range(4) for a in ws_),
        in_specs=[vm] * (4 * n) + [_ANY] * len(after), out_specs=[vm] * (4 * n),
        compiler_params=_params(32),
    )(*ws_, *gs, *ms, *vs, *after)
    return [tuple(outs[j * n + k] for j in range(4)) for k in range(n)]


def _adamw(name, w, g, m, v, tr=256, after=()):
    rows, cols = w.shape

    def body(w_ref, g_ref, m_ref, v_ref, d_ref, nm_ref, nv_ref, gc_ref):
        gv = g_ref[...]
        d_ref[...], nm_ref[...], nv_ref[...] = _adamw_math(w_ref[...], gv, m_ref[...], v_ref[...])
        gc_ref[...] = gv

    spec = pl.BlockSpec((tr, cols), lambda i: (i, 0))
    return pl.pallas_call(
        _after(body, 4, after), name=name, grid=(rows // tr,),
        out_shape=tuple(jax.ShapeDtypeStruct((rows, cols), F32) for _ in range(4)),
        in_specs=[spec] * 4 + [_ANY] * len(after), out_specs=[spec] * 4,
        compiler_params=_params(32, dimension_semantics=("arbitrary",)),
    )(w, g, m, v, *after)


_REL_PAD = 384
_VEC_FIELDS = (("norm_g", 0, D_MODEL), ("b_gate", 1024, 2 * D_MODEL), ("sgu_ln_g", 3072, D_B),
               ("sgu_ln_b", 3584, D_B), ("b_s", 4096, N_GROUPS * 128), ("final_g", 4608, D_MODEL))
_LOSS_OFF = 5632
_REL_OFF = 5760
_NV = _REL_OFF + N_HEADS * _REL_PAD
_N_FIELDS = len(_VEC_FIELDS) + 2


_B_S_FIELD = [f[0] for f in _VEC_FIELDS].index("b_s")


def _assemble_row(dst, fields, transposed_b_s):
    for f, (_, off, n) in enumerate(_VEC_FIELDS):
        if transposed_b_s and f == _B_S_FIELD:
            t = fields[f][...].T
            for g in range(N_GROUPS):
                dst[:, off + 128 * g:off + 128 * (g + 1)] = t[g:g + 1, :]
        else:
            dst[:, off:off + n] = fields[f][...]
    for r in range(N_HEADS):
        dst[:, _REL_OFF + _REL_PAD * r:_REL_OFF + _REL_PAD * (r + 1)] = fields[len(_VEC_FIELDS)][r:r + 1, :]


def _small_reduce(grads, loss_row, after=()):
    n_in = _N_FIELDS + 1

    def body(*refs):
        g_refs, loss_ref = refs[:_N_FIELDS], refs[_N_FIELDS]
        out_v, out_w = refs[n_in:n_in + 2]
        mine_v, gath_v, gath_w, send_sems, recv_sems = refs[n_in + 2:]
        x, y, c, chips = _mesh_pos()
        me, sibling = (x, y, c), (x, y, 1 - c)

        _assemble_row(mine_v, g_refs, True)
        mine_v[:, _LOSS_OFF:_LOSS_OFF + 128] = loss_ref[...]
        mine_w = g_refs[-1]
        my_k = 4 * x + 2 * y + c
        gath_v[my_k] = mine_v[...]
        gath_w[my_k] = mine_w[...]

        def copy(k, gath, block, to, src=None):
            dst = gath.at[4 * block[0] + 2 * block[1] + block[2]]
            return pltpu.make_async_remote_copy(
                src_ref=dst if src is None else src, dst_ref=dst,
                send_sem=send_sems.at[k], recv_sem=recv_sems.at[k], device_id=to, device_id_type=MESH)

        bufs = ((gath_v, mine_v), (gath_w, mine_w))
        first, passed = [], []
        for b, (gath, mine) in enumerate(bufs):
            first.append(copy(7 * b, gath, me, sibling, src=mine))
            first += [copy(7 * b + 1 + j, gath, me, (*chip, c), src=mine) for j, chip in enumerate(chips)]
        for cp in first:
            cp.start()
        for b, (gath, _) in enumerate(bufs):
            for j, chip in enumerate(chips):
                copy(7 * b + 1 + j, gath, (*chip, c), me).wait_recv()
                cp = copy(7 * b + 4 + j, gath, (*chip, c), sibling)
                cp.start()
                passed.append(cp)
        for b, (gath, _) in enumerate(bufs):
            copy(7 * b, gath, sibling, me).wait_recv()
            for j, chip in enumerate(chips):
                copy(7 * b + 4 + j, gath, (*chip, 1 - c), me).wait_recv()
        for cp in first + passed:
            cp.wait_send()

        tot_v, tot_w = gath_v[0], gath_w[0]
        for k in range(1, 8):
            tot_v = tot_v + gath_v[k]
            tot_w = tot_w + gath_w[k]
        out_v[...] = tot_v
        out_w[...] = tot_w

    vm = pl.BlockSpec(memory_space=pltpu.VMEM)
    return pl.pallas_call(
        _after(body, n_in, after), name="small_reduce",
        out_shape=(jax.ShapeDtypeStruct((1, _NV), F32), jax.ShapeDtypeStruct((N_GROUPS * 128, 128), F32)),
        in_specs=[vm] * n_in + [_ANY] * len(after), out_specs=[vm] * 2,
        scratch_shapes=[pltpu.VMEM((1, _NV), F32), pltpu.VMEM((8, 1, _NV), F32),
                        pltpu.VMEM((8, N_GROUPS * 128, 128), F32),
                        pltpu.SemaphoreType.DMA((14,)), pltpu.SemaphoreType.DMA((14,))],
        compiler_params=_params(32),
    )(*grads, loss_row, *after)


def _small_adamw(tot_v, tot_w, params):
    n_in = 2 + 3 * _N_FIELDS

    def body(*refs):
        tv_ref, tw_ref = refs[:2]
        p_refs = [refs[2 + k * _N_FIELDS:2 + (k + 1) * _N_FIELDS] for k in range(3)]
        outs = refs[n_in:n_in + 4 * _N_FIELDS + 1]
        wmv = refs[-1]
        for k in range(3):
            _assemble_row(wmv.at[k], p_refs[k], False)
            wmv[k, :, _LOSS_OFF:_LOSS_OFF + 128] = jnp.zeros((1, 128), F32)
        tot_v, tot_w = tv_ref[...], tw_ref[...]
        res_v = (tot_v,) + _adamw_math(wmv[0], tot_v, wmv[1], wmv[2])
        res_w = (tot_w,) + _adamw_math(p_refs[0][-1][...], tot_w, p_refs[1][-1][...], p_refs[2][-1][...])
        for kind in range(4):
            o = outs[kind * _N_FIELDS:(kind + 1) * _N_FIELDS]
            for f, (_, off, n) in enumerate(_VEC_FIELDS):
                o[f][...] = res_v[kind][:, off:off + n]
            for r in range(N_HEADS):
                o[len(_VEC_FIELDS)][r:r + 1, :] = res_v[kind][:, _REL_OFF + _REL_PAD * r:_REL_OFF + _REL_PAD * (r + 1)]
            o[-1][...] = res_w[kind]
        outs[-1][...] = tot_v[:, _LOSS_OFF:_LOSS_OFF + 128]

    field_shapes = [(1, n) for _, _, n in _VEC_FIELDS] + [(N_HEADS, _REL_PAD), (N_GROUPS * 128, 128)]
    vm = pl.BlockSpec(memory_space=pltpu.VMEM)
    operands = [tot_v, tot_w] + [a for p in params for a in p]
    assert len(operands) == n_in
    outs = pl.pallas_call(
        body, name="small_adamw",
        out_shape=tuple(jax.ShapeDtypeStruct(s, F32) for _ in range(4) for s in field_shapes)
        + (jax.ShapeDtypeStruct((1, 128), F32),),
        in_specs=[vm] * n_in, out_specs=[vm] * (4 * _N_FIELDS + 1),
        scratch_shapes=[pltpu.VMEM((3, 1, _NV), F32)],
        compiler_params=_params(32),
    )(*operands)
    return [outs[k * _N_FIELDS:(k + 1) * _N_FIELDS] for k in range(4)], outs[-1]


def _small_fields(norm_g, b_gate, ln_g, ln_b, b_s, final_g, rel_bias, w_s):
    rel = jnp.pad(rel_bias.reshape(N_HEADS, N_REL), ((0, 0), (0, _REL_PAD - N_REL)))
    return (norm_g, b_gate, ln_g, ln_b, b_s.reshape(1, N_GROUPS * 128), final_g.reshape(1, D_MODEL),
            rel, w_s.reshape(N_GROUPS * 128, 128))


def _small_outputs(fields):
    n_g, b_g, l_g, l_b, b_s, f_g, rel, w_s = fields
    return (n_g, b_g, rel[:, :N_REL].reshape(1, N_HEADS, N_REL), l_g, l_b,
            w_s.reshape(1, N_GROUPS, 128, 128), b_s.reshape(1, N_GROUPS, 128), f_g.reshape(D_MODEL))


def _bias_row(rel_bias):
    hi = rel_bias[:, N_REL - 1:N_REL]
    lo = rel_bias[:, 0:1]
    return jnp.concatenate([jnp.broadcast_to(hi, (N_HEADS, 384)), rel_bias[:, ::-1],
                            jnp.broadcast_to(lo, (N_HEADS, 191)), jnp.broadcast_to(hi, (N_HEADS, 192))], axis=1)


def kernel(x, norm_g, w_in, b_gate, rel_bias, sgu_ln_g, sgu_ln_b, w_s, b_s, w_pa, w_pb, w_out, final_g, loss_target, m_norm_g, m_w_in, m_b_gate, m_rel_bias, m_sgu_ln_g, m_sgu_ln_b, m_w_s, m_b_s, m_w_pa, m_w_pb, m_w_out, m_final_g, v_norm_g, v_w_in, v_b_gate, v_rel_bias, v_sgu_ln_g, v_sgu_ln_b, v_w_s, v_b_s, v_w_pa, v_w_pb, v_w_out, v_final_g):
    S = x.shape[1]
    xs = x.reshape(S, D_MODEL)
    tgt = loss_target.reshape(S, D_MODEL)
    big_w = (w_in[0], w_pa[0], w_pb[0], w_out[0])
    big_m = (m_w_in[0], m_w_pa[0], m_w_pb[0], m_w_out[0])
    big_v = (v_w_in[0], v_w_pa[0], v_w_pb[0], v_w_out[0])
    rel = rel_bias[0]
    ws = w_s[0]
    bst = b_s[0].T
    fg = final_g.reshape(1, D_MODEL)
    pos = jnp.stack([lax.axis_index("c"), 2 * lax.axis_index("x") + lax.axis_index("y")]).astype(jnp.int32)

    staged = _stage_weights((1, 2, 3), big_w[1:], pos)
    w_in_bf, = _ag_weights((0,), big_w[:1])
    ag_s = _split_start("ag_small_start", staged, 9, _gather_copies((1, 2, 3)), after=(w_in_bf,))

    ht, q3, k3, v3, zrest = _inproj_fwd(xs, norm_g, w_in_bf, after=(ag_s.token,))
    gp = _bias_row(rel)
    att, lse, band_bias = _attn_fwd(q3, k3, v3, gp)
    w_pa_bf, w_pb_bf, w_out_bf = _split_wait("ag_small_wait", ag_s, _gather_copies((1, 2, 3)), att)
    (d_out, d_att, dzt, dzs, gw_out, gw_pa, gw_pb, g_bgate, g_final, loss_row,
     g_ws, g_bs_t, g_lng, g_lnb) = _tail_sgu(
        att, zrest, xs, tgt, w_pa_bf, w_pb_bf, w_out_bf, b_gate, fg, sgu_ln_g, sgu_ln_b, ws, bst)
    ws_s, ws_i = (1, 2, 3), (0,)

    x1s = _split_start("gx1s_start", [gw_pa, gw_pb, gw_out] + _x1_lands(ws_s), 12, _x1_copies(ws_s))
    dq, dk, dv, d_gp = _attn_bwd(q3, k3, v3, d_att, lse, band_bias, after=(x1s.token,))
    got = _split_wait("gx1s_wait", x1s, _x1_copies(ws_s), dq)
    cs_s, csb_s = _grad_add1_group(ws_s, got[:3], got[3:])

    x2s = _split_start("gx2s_start", csb_s + _x2_lands(ws_s), 9, _x2_copies(3))
    gw_in, gw_in_bf = _gw_in(ht, dq, dk, dv, dzt, dzs, after=(x2s.token,))
    x1i = _split_start("gx1i_start", [gw_in_bf] + _x1_lands(ws_i, BF), 4, _x1_copies(ws_i))
    got = _split_wait("gx2s_wait", x2s, _x2_copies(3), x1i.token)
    halves_s = _grad_add2_group(ws_s, cs_s, got[3:])
    x3s = _split_start("gx3s_start", halves_s, 3, _x3_copies(ws_s))
    got = _split_wait("gx1i_wait", x1i, _x1_copies(ws_i), x3s.token)
    sum_i = _grad_add1(0, gw_in, got[1], pos)

    x2i = _split_start("gx2i_start", [sum_i[1]] + _x2_lands(ws_i), 3, _x2_copies(1))
    grad_x, g_norm = _dh_gradx(dq, dk, dv, dzt, dzs, w_in_bf, xs, norm_g, d_out, after=(x2i.token,))
    g_shards_s = _split_wait("gx3s_wait", x3s, _x3_copies(ws_s), grad_x)
    big = [None] * 4
    big[1:] = _adamw_group(big_w[1:], g_shards_s, big_m[1:], big_v[1:], after=(x2i.token,))

    g_rel = jnp.pad(d_gp[:, 384:384 + N_REL][:, ::-1], ((0, 0), (0, _REL_PAD - N_REL)))
    small_grads = (g_norm, g_bgate, g_lng, g_lnb, g_bs_t, g_final, g_rel, g_ws.reshape(N_GROUPS * 128, 128))
    small_params = (_small_fields(norm_g, b_gate, sgu_ln_g, sgu_ln_b, b_s, final_g, rel_bias, w_s),
                    _small_fields(m_norm_g, m_b_gate, m_sgu_ln_g, m_sgu_ln_b, m_b_s, m_final_g, m_rel_bias, m_w_s),
                    _small_fields(v_norm_g, v_b_gate, v_sgu_ln_g, v_sgu_ln_b, v_b_s, v_final_g, v_rel_bias, v_w_s))
    tot_v, tot_w = _small_reduce(small_grads, loss_row, after=(x2i.token,))
    (gsum, sdelta, sm, sv), loss_out = _small_adamw(tot_v, tot_w, small_params)

    got = _split_wait("gx2i_wait", x2i, _x2_copies(1), loss_out)
    half_i = _grad_add2(0, sum_i[0], got[1], pos)
    g_shard_i, = _grad_xchg3(ws_i, [half_i])
    big[0] = _adamw("adamw_w_in", big_w[0], g_shard_i, big_m[0], big_v[0])
    sg_out, sd_out, sm_out, sv_out = (_small_outputs(f) for f in (gsum, sdelta, sm, sv))
    loss = loss_out[0, 0]

    def assemble(small, bigs):
        n_g, b_g, r_b, l_g, l_b, w_s_, b_s_, f_g = small
        b_in, b_pa, b_pb, b_out = (b[None] for b in bigs)
        return (n_g, b_in, b_g, r_b, l_g, l_b, w_s_, b_s_, b_pa, b_pb, b_out, f_g)

    grads_out = assemble(sg_out, [b[3] for b in big])
    delta_out = assemble(sd_out, [b[0] for b in big])
    m_out = assemble(sm_out, [b[1] for b in big])
    v_out = assemble(sv_out, [b[2] for b in big])
    return (loss, grad_x.reshape(1, S, D_MODEL), *grads_out, *delta_out, *m_out, *v_out)
```

```python
import functools
import math

import jax
import jax.numpy as jnp
from jax import lax
from jax.experimental import pallas as pl
from jax.experimental.pallas import tpu as pltpu

F32 = jnp.float32
BF = jnp.bfloat16
MESH = pl.DeviceIdType.MESH

D_MODEL = 1024
D_A = 512
D_B = 512
D_IN = 5632
N_HEADS = 8
HEAD_DIM = 64
CHUNK = 64
N_PREV = 8
SGU_CHUNK = 128
N_GROUPS = 4
N_REL = 257
EPS = 1e-6
NEG_INF = -1e30
SCALE = HEAD_DIM ** -0.5

QB = 2 * CHUNK
KB = (N_PREV + 2) * CHUNK
PADK = N_PREV * CHUNK
ROLL_W = 1024
N_RING = KB // QB
KEEP = N_RING - 1

ADAM_LR = 0.001
ADAM_B1 = 0.9
ADAM_B2 = 0.999
ADAM_EPS = 1e-08
ADAM_WD = 0.01
ADAM_STEP = 10
ADAM_C1 = 1.0 - ADAM_B1 ** ADAM_STEP
ADAM_C2 = 1.0 - ADAM_B2 ** ADAM_STEP

N_SHARD = 4
SHARD_IN = D_IN // N_SHARD
MIB = 1024 * 1024


VMEM_RESERVE_MIB = 60


def _params(vmem_mib, **kw):
    assert vmem_mib <= VMEM_RESERVE_MIB
    return pltpu.CompilerParams(vmem_limit_bytes=VMEM_RESERVE_MIB * MIB, **kw)


def _sigmoid(x):
    return 1.0 / (1.0 + jnp.exp(-x))


def _silu_and_grad(x):
    s = _sigmoid(x)
    return x * s, s * (1.0 + x * (1.0 - s))


_GELU_C = math.sqrt(2.0 / math.pi)
_GELU_A = 0.044715


def _gelu_and_grad(x):
    x2 = x * x
    t = jnp.tanh(_GELU_C * (x + _GELU_A * (x2 * x)))
    cdf = 0.5 * (1.0 + t)
    grad = cdf + 0.5 * x * (1.0 - t * t) * (_GELU_C * (1.0 + 3.0 * _GELU_A * x2))
    return x * cdf, grad


def _dot(a, b):
    return jnp.dot(a, b, preferred_element_type=F32)


def _dot_nt(a, b):
    return lax.dot_general(a, b, (((1,), (1,)), ((), ())), preferred_element_type=F32)


def _dot_tn(a, b):
    return lax.dot_general(a, b, (((0,), (0,)), ((), ())), preferred_element_type=F32)


def _mo(v, m):
    return v if isinstance(v, int) else pl.multiple_of(v, m)


def _unit_in(ref, s, p):
    return ref.at[pl.ds(_mo(p * 512, 512), 512), pl.ds(_mo(s * SHARD_IN, 128), SHARD_IN)]


def _unit_p(ref, s, p):
    return ref.at[pl.ds(_mo(p * 256, 256), 256), pl.ds(_mo(s * 256, 128), 256)]


def _unit_out(ref, s, p):
    return ref.at[pl.ds(_mo(s * 256 + p * 128, 128), 128), :]


_UNITS = (_unit_in, _unit_p, _unit_p, _unit_out)
_HALF_ROWS = (512, 256, 256, 128)
_UNIT_SHAPES = ((512, SHARD_IN), (256, 256), (256, 256), (128, D_MODEL))
_FULL_SHAPES = ((D_MODEL, D_IN), (D_A, D_MODEL), (D_B, D_MODEL), (D_MODEL, D_MODEL))
_SHARD_SHAPES = ((D_MODEL, SHARD_IN), (D_A, 256), (D_B, 256), (256, D_MODEL))


def _mesh_pos():
    x, y, c = lax.axis_index("x"), lax.axis_index("y"), lax.axis_index("c")
    chips = [(1 - x, y), (x, 1 - y), (1 - x, 1 - y)]
    return x, y, c, chips


def _ag_weights(ws, shards):
    n = len(ws)

    def body(*refs):
        ins, outs, stage = refs[:n], refs[n:2 * n], refs[2 * n:3 * n]
        send_sems, recv_sems, local_sems = refs[3 * n:]
        x, y, c, chips = _mesh_pos()
        s_me = 2 * x + y
        sibling = (x, y, 1 - c)
        def rows_of(k, p):
            rows = _HALF_ROWS[ws[k]]
            return pl.ds(_mo(p * rows, rows), rows)

        def half(k, p):
            return stage[k].at[rows_of(k, p), :]

        def unit(k, s, p):
            return _UNITS[ws[k]](outs[k], s, p)

        def rcopy(k, i, src, dst, to):
            return pltpu.make_async_remote_copy(src_ref=src, dst_ref=dst, send_sem=send_sems.at[k, i],
                                                recv_sem=recv_sems.at[k, i], device_id=to, device_id_type=MESH)

        for k in range(n):
            stage[k][rows_of(k, c), :] = ins[k][rows_of(k, c), :].astype(BF)
        sends = []
        for j, (cx, cy) in enumerate(chips):
            for k in range(n):
                cp = rcopy(k, j, half(k, c), unit(k, s_me, c), (cx, cy, c))
                cp.start()
                sends.append(cp)
        for k in range(n):
            stage[k][rows_of(k, 1 - c), :] = ins[k][rows_of(k, 1 - c), :].astype(BF)
        local = []
        for k in range(n):
            for p in range(2):
                cp = pltpu.make_async_copy(half(k, p), unit(k, s_me, p), local_sems.at[k, p])
                cp.start()
                local.append(cp)
        for j, (cx, cy) in enumerate(chips):
            for k in range(n):
                landed = unit(k, 2 * cx + cy, c)
                rcopy(k, j, landed, landed, (cx, cy, c)).wait_recv()
                cp = rcopy(k, 3 + j, landed, landed, sibling)
                cp.start()
                sends.append(cp)
        for j, (cx, cy) in enumerate(chips):
            for k in range(n):
                other = unit(k, 2 * cx + cy, 1 - c)
                rcopy(k, 3 + j, other, other, sibling).wait_recv()
        for cp in sends:
            cp.wait_send()
        for cp in local:
            cp.wait()

    vm = pl.BlockSpec(memory_space=pltpu.VMEM)
    return pl.pallas_call(
        body, name="ag_weights",
        out_shape=tuple(jax.ShapeDtypeStruct(_FULL_SHAPES[w], BF) for w in ws),
        in_specs=[vm] * n, out_specs=[_ANY] * n,
        scratch_shapes=[pltpu.VMEM(_SHARD_SHAPES[w], BF) for w in ws]
        + [pltpu.SemaphoreType.DMA((n, 6)), pltpu.SemaphoreType.DMA((n, 6)), pltpu.SemaphoreType.DMA((n, 2))],
        compiler_params=_params(40),
    )(*shards)


def _shard_of(ref, w, s):
    if w == 0:
        return ref.at[:, pl.ds(_mo(s * SHARD_IN, 128), SHARD_IN)]
    if w == 3:
        return ref.at[pl.ds(_mo(s * 256, 256), 256), :]
    return ref.at[:, pl.ds(_mo(s * 256, 128), 256)]


def _stage_weights(ws, shards, pos):
    n = len(ws)

    def body(pos_ref, *refs):
        for k in range(n):
            refs[n + k][...] = refs[k][...].astype(BF)

    def spec(w):
        shape = _SHARD_SHAPES[w]
        if w == 3:
            return pl.BlockSpec(shape, lambda i, pos: (pos[1], 0))
        return pl.BlockSpec(shape, lambda i, pos: (0, pos[1]))

    return list(pl.pallas_call(
        body, name="stage_weights",
        grid_spec=pltpu.PrefetchScalarGridSpec(
            num_scalar_prefetch=1, grid=(1,),
            in_specs=[pl.BlockSpec(_SHARD_SHAPES[w], lambda i, pos: (0, 0)) for w in ws],
            out_specs=[spec(w) for w in ws]),
        out_shape=tuple(jax.ShapeDtypeStruct(_FULL_SHAPES[w], BF) for w in ws),
        compiler_params=_params(16, dimension_semantics=("arbitrary",)),
    )(pos, *shards))


def _gather_copies(ws):
    def copies(refs, send_sems, recv_sems):
        x, y, c, chips = _mesh_pos()
        out = []
        for j, (cx, cy) in enumerate(chips):
            for k, w in enumerate(ws):
                mine = _shard_of(refs[k], w, 2 * x + y)
                out.append(pltpu.make_async_remote_copy(
                    src_ref=mine, dst_ref=mine, send_sem=send_sems.at[3 * k + j], recv_sem=recv_sems.at[3 * k + j],
                    device_id=(cx, cy, c), device_id_type=MESH))
        return out
    return copies


def _inproj_fwd(x, norm_g, w_in_bf, tm=512, after=()):
    S = x.shape[0]

    def body(x_ref, g_ref, w_ref, ht_ref, q_ref, k_ref, v_ref, zr_ref):
        xv = x_ref[...]
        r = lax.rsqrt(jnp.mean(xv * xv, axis=-1, keepdims=True) + EPS)
        hf = (xv * r) * g_ref[...]
        ht_ref[...] = hf.T.astype(BF)
        h = hf.astype(BF)
        heads = (q_ref, k_ref, v_ref)
        for j in range(D_IN // 512):
            z = _dot(h, w_ref[:, j * 512:(j + 1) * 512])
            if j < 3:
                zb = z.astype(BF)
                for hd in range(N_HEADS):
                    heads[j][hd] = zb[:, hd * HEAD_DIM:(hd + 1) * HEAD_DIM]
            else:
                zr_ref[:, (j - 3) * 512:(j - 2) * 512] = z

    head_major = jax.ShapeDtypeStruct((N_HEADS, S, HEAD_DIM), BF)
    head_spec = pl.BlockSpec((N_HEADS, tm, HEAD_DIM), lambda i: (0, i, 0))
    return pl.pallas_call(
        _after(body, 3, after), name="inproj_fwd", grid=(S // tm,),
        out_shape=(jax.ShapeDtypeStruct((D_MODEL, S), BF), head_major, head_major, head_major,
                   jax.ShapeDtypeStruct((S, D_IN - 3 * D_A), F32)),
        in_specs=[pl.BlockSpec((tm, D_MODEL), lambda i: (i, 0)),
                  pl.BlockSpec((1, D_MODEL), lambda i: (0, 0)),
                  pl.BlockSpec((D_MODEL, D_IN), lambda i: (0, 0), pipeline_mode=pl.Buffered(1))]
        + [_ANY] * len(after),
        out_specs=[pl.BlockSpec((D_MODEL, tm), lambda i: (0, i)),
                   head_spec, head_spec, head_spec,
                   pl.BlockSpec((tm, D_IN - 3 * D_A), lambda i: (i, 0))],
        compiler_params=_params(52, dimension_semantics=("arbitrary",)),
    )(x, norm_g, w_in_bf, *after)


def _skew_table(gp_row):
    row = lax.broadcasted_iota(jnp.int32, (QB, ROLL_W), 0)
    t = jnp.broadcast_to(gp_row, (QB, ROLL_W))
    for b in range(7):
        t = jnp.where(((row >> b) & 1) == 1, pltpu.roll(t, 1 << b, axis=1), t)
    return t


def _unskew_sum(d):
    row = lax.broadcasted_iota(jnp.int32, (QB, ROLL_W), 0)
    for b in range(7):
        d = jnp.where(((row >> b) & 1) == 1, pltpu.roll(d, ROLL_W - (1 << b), axis=1), d)
    return jnp.sum(d, axis=0, keepdims=True)


def _struct_mask():
    a = lax.broadcasted_iota(jnp.int32, (QB, KB), 0) // CHUNK
    b = lax.broadcasted_iota(jnp.int32, (QB, KB), 1) // CHUNK
    return (b >= a) & (b <= a + N_PREV)


def _load_kv(k_hbm, v_hbm, k_scr, v_scr, sems, S, meanwhile=lambda: None):
    zeros = jnp.zeros((N_HEADS, PADK, HEAD_DIM), BF)
    k_scr[:, 0:PADK, :] = zeros
    v_scr[:, 0:PADK, :] = zeros
    ck = pltpu.make_async_copy(k_hbm, k_scr.at[:, pl.ds(PADK, S), :], sems.at[0])
    cv = pltpu.make_async_copy(v_hbm, v_scr.at[:, pl.ds(PADK, S), :], sems.at[1])
    ck.start()
    cv.start()
    meanwhile()
    ck.wait()
    cv.wait()


_BATCH_NT = (((2,), (2,)), ((0,), (0,)))
_BATCH_NN = (((2,), (1,)), ((0,), (0,)))
_BATCH_TN = (((1,), (1,)), ((0,), (0,)))


def _bdot(a, b, dims):
    return lax.dot_general(a, b, dims, preferred_element_type=F32)


def _scaled(q):
    return q * jnp.asarray(SCALE, BF)


def _scores(qs, kb, bias, i, front):
    s = _bdot(qs, kb, _BATCH_NT) + bias
    if front:
        col = lax.broadcasted_iota(jnp.int32, (1, 1, KB), 2)
        s = jnp.where(col >= PADK - i * QB, s, NEG_INF)
    return s


def _attn_fwd(q3, k3, v3, gp):
    S = q3.shape[1]

    def body(q_ref, k_hbm, v_hbm, gp_ref, o_ref, lse_ref, bias_ref, k_scr, v_scr, sems):
        i = pl.program_id(0)

        @pl.when(i == 0)
        def _():
            def build_bias():
                keep = _struct_mask()
                for h in range(N_HEADS):
                    bias_ref[h] = jnp.where(keep, _skew_table(gp_ref[h:h + 1, :])[:, :KB], NEG_INF)
            _load_kv(k_hbm, v_hbm, k_scr, v_scr, sems, S, build_bias)

        def step(front):
            start = pl.multiple_of(i * QB, QB)
            kb = k_scr[:, pl.ds(start, KB), :]
            vb = v_scr[:, pl.ds(start, KB), :]
            s = _scores(_scaled(q_ref[...]), kb, bias_ref[...], i, front)
            m = jnp.max(s, axis=-1, keepdims=True)
            e = jnp.exp(s - m)
            l = jnp.sum(e, axis=-1, keepdims=True)
            p = e * (1.0 / l)
            o = _bdot(p.astype(BF), vb, _BATCH_NN)
            lse_ref[...] = jnp.broadcast_to(m + jnp.log(l), (N_HEADS, QB, 128))
            for h in range(N_HEADS):
                o_ref[:, h * HEAD_DIM:(h + 1) * HEAD_DIM] = o[h]

        pl.when(i < KEEP)(functools.partial(step, True))
        pl.when(i >= KEEP)(functools.partial(step, False))

    kv_scr = pltpu.VMEM((N_HEADS, S + PADK, HEAD_DIM), BF)
    return pl.pallas_call(
        body, name="attn_fwd", grid=(S // QB,),
        out_shape=(jax.ShapeDtypeStruct((S, D_A), F32), jax.ShapeDtypeStruct((N_HEADS, S, 128), F32),
                   jax.ShapeDtypeStruct((N_HEADS, QB, KB), F32)),
        in_specs=[pl.BlockSpec((N_HEADS, QB, HEAD_DIM), lambda i: (0, i, 0)),
                  pl.BlockSpec(memory_space=pl.ANY), pl.BlockSpec(memory_space=pl.ANY),
                  pl.BlockSpec((N_HEADS, ROLL_W), lambda i: (0, 0))],
        out_specs=[pl.BlockSpec((QB, D_A), lambda i: (i, 0)),
                   pl.BlockSpec((N_HEADS, QB, 128), lambda i: (0, i, 0)),
                   pl.BlockSpec((N_HEADS, QB, KB), lambda i: (0, 0, 0))],
        scratch_shapes=[kv_scr, kv_scr, pltpu.SemaphoreType.DMA((2,))],
        compiler_params=_params(48, dimension_semantics=("arbitrary",)),
    )(q3, k3, v3, gp)


def _attn_bwd(q3, k3, v3, d_att3, lse, bias, after=()):
    S = q3.shape[1]
    nq = S // QB

    def body(q_ref, do_ref, k_hbm, v_hbm, lse_ref, bias_ref, dq_ref, dk_ref, dv_ref, dgp_ref,
             k_scr, v_scr, dk_acc, dv_acc, dbias_acc, pad_scr, sems):
        i = pl.program_id(0)

        @pl.when(i == 0)
        def _():
            def clear():
                dk_acc[...] = jnp.zeros_like(dk_acc)
                dv_acc[...] = jnp.zeros_like(dv_acc)
                dbias_acc[...] = jnp.zeros_like(dbias_acc)
            _load_kv(k_hbm, v_hbm, k_scr, v_scr, sems, S, clear)

        def step(front):
            start = pl.multiple_of(i * QB, QB)
            kb = k_scr[:, pl.ds(start, KB), :]
            vb = v_scr[:, pl.ds(start, KB), :]
            qs = _scaled(q_ref[...])
            do = do_ref[...]
            p = jnp.exp(_scores(qs, kb, bias_ref[...], i, front) - jnp.tile(lse_ref[...], (1, 1, KB // 128)))
            dp = _bdot(do, vb, _BATCH_NT)
            ds = p * (dp - jnp.sum(dp * p, axis=-1, keepdims=True))
            dbias_acc[...] += ds
            dsb = ds.astype(BF)
            dq = _bdot(dsb, kb, _BATCH_NN) * SCALE
            for h in range(N_HEADS):
                dq_ref[:, h * HEAD_DIM:(h + 1) * HEAD_DIM] = dq[h].astype(BF)
            dk_acc[...] += _bdot(dsb, qs, _BATCH_TN)
            dv_acc[...] += _bdot(p.astype(BF), do, _BATCH_TN)

        pl.when(i < KEEP)(functools.partial(step, True))
        pl.when((i >= KEEP) & (i < nq))(functools.partial(step, False))

        for h in range(N_HEADS):
            hs = slice(h * HEAD_DIM, (h + 1) * HEAD_DIM)
            dk_ref[:, hs] = dk_acc[h, 0:QB, :].astype(BF)
            dv_ref[:, hs] = dv_acc[h, 0:QB, :].astype(BF)
        dk_acc[:, 0:KB - QB, :] = dk_acc[:, QB:KB, :]
        dv_acc[:, 0:KB - QB, :] = dv_acc[:, QB:KB, :]
        dk_acc[:, KB - QB:KB, :] = jnp.zeros((N_HEADS, QB, HEAD_DIM), F32)
        dv_acc[:, KB - QB:KB, :] = jnp.zeros((N_HEADS, QB, HEAD_DIM), F32)

        @pl.when(i == nq + KEEP - 1)
        def _():
            lane = lax.broadcasted_iota(jnp.int32, (1, ROLL_W), 1)
            hi = (lane < 384) | (lane >= 832)
            lo = (lane > 640) & (lane < 832)
            pad_scr[...] = jnp.zeros_like(pad_scr)
            for h in range(N_HEADS):
                pad_scr[:, 0:KB] = dbias_acc[h]
                g = _unskew_sum(pad_scr[...])
                s_hi = jnp.sum(jnp.where(hi, g, 0.0), axis=-1, keepdims=True)
                s_lo = jnp.sum(jnp.where(lo, g, 0.0), axis=-1, keepdims=True)
                g = jnp.where(lane == 384, g + s_hi, g)
                g = jnp.where(lane == 640, g + s_lo, g)
                dgp_ref[h:h + 1, :] = g

    last = nq - 1
    kv_scr = pltpu.VMEM((N_HEADS, S + PADK, HEAD_DIM), BF)
    return pl.pallas_call(
        _after(body, 6, after), name="attn_bwd", grid=(nq + KEEP,),
        out_shape=(jax.ShapeDtypeStruct((S, D_A), BF), jax.ShapeDtypeStruct((S, D_A), BF),
                   jax.ShapeDtypeStruct((S, D_A), BF), jax.ShapeDtypeStruct((N_HEADS, ROLL_W), F32)),
        in_specs=[pl.BlockSpec((N_HEADS, QB, HEAD_DIM), lambda i: (0, jnp.minimum(i, last), 0)),
                  pl.BlockSpec((N_HEADS, QB, HEAD_DIM), lambda i: (0, jnp.minimum(i, last), 0)),
                  pl.BlockSpec(memory_space=pl.ANY), pl.BlockSpec(memory_space=pl.ANY),
                  pl.BlockSpec((N_HEADS, QB, 128), lambda i: (0, jnp.minimum(i, last), 0)),
                  pl.BlockSpec((N_HEADS, QB, KB), lambda i: (0, 0, 0))] + [_ANY] * len(after),
        out_specs=[pl.BlockSpec((QB, D_A), lambda i: (jnp.minimum(i, last), 0)),
                   pl.BlockSpec((QB, D_A), lambda i: (jnp.maximum(i - KEEP, 0), 0)),
                   pl.BlockSpec((QB, D_A), lambda i: (jnp.maximum(i - KEEP, 0), 0)),
                   pl.BlockSpec((N_HEADS, ROLL_W), lambda i: (0, 0))],
        scratch_shapes=[kv_scr, kv_scr,
                        pltpu.VMEM((N_HEADS, KB, HEAD_DIM), F32), pltpu.VMEM((N_HEADS, KB, HEAD_DIM), F32),
                        pltpu.VMEM((N_HEADS, QB, KB), F32), pltpu.VMEM((QB, ROLL_W), F32),
                        pltpu.SemaphoreType.DMA((2,))],
        compiler_params=_params(56, dimension_semantics=("arbitrary",)),
    )(q3, d_att3, k3, v3, lse, bias, *after)


def _sgu_core(ub, vb, lg, lb):
    u, du = _gelu_and_grad(ub)
    v, dv = _gelu_and_grad(vb)
    mu = jnp.mean(v, axis=-1, keepdims=True)
    vc = v - mu
    rstd = lax.rsqrt(jnp.mean(vc * vc, axis=-1, keepdims=True) + EPS)
    xh = vc * rstd
    vn = xh * lg + lb
    return u, du, dv, rstd, xh, vn


def _tri():
    r = lax.broadcasted_iota(jnp.int32, (SGU_CHUNK, SGU_CHUNK), 0)
    c = lax.broadcasted_iota(jnp.int32, (SGU_CHUNK, SGU_CHUNK), 1)
    return r >= c


def _tail_sgu(att, zrest, x, target, w_pa, w_pb, w_out, b_gate, final_g, ln_g, ln_b, w_s, b_s_t, tm=256):
    S = x.shape[0]
    nt = S // tm
    chunks = tm // SGU_CHUNK

    def body(att_ref, ga_ref, ub_ref, vb_ref, gb_ref, gta_ref, gtb_ref, x_ref, t_ref,
             wpa_ref, wpb_ref, wout_ref, bg_ref, fg_ref, lg_ref, lb_ref, ws_ref, bst_ref,
             dout_ref, datt_ref, dzt_ref, dzs_ref, gwout_hbm, gwpa_hbm, gwpb_hbm,
             gbg_ref, gfg_ref, loss_ref, gws_ref, gbs_ref, glg_ref, glb_ref,
             acc_out, acc_pa, acc_pb, sg_scr, mix_scr, dvn_scr, bs_acc, sems):
        i = pl.program_id(0)

        @pl.when(i == 0)
        def _():
            for r in (acc_out, acc_pa, acc_pb, gbg_ref, gfg_ref, loss_ref, gws_ref, glg_ref, glb_ref, bs_acc):
                r[...] = jnp.zeros_like(r)

        u, du, dv, rstd, xh, vn = _sgu_core(ub_ref[...], vb_ref[...], lg_ref[...], lb_ref[...])
        vnb = vn.astype(BF)
        tri = _tri()
        blocks = [(g, slice(n * SGU_CHUNK, (n + 1) * SGU_CHUNK), slice(g * 128, (g + 1) * 128))
                  for g in range(N_GROUPS) for n in range(chunks)]
        wts = [jnp.where(tri, ws_ref[g], 0.0) for g in range(N_GROUPS)]
        for g, rs, cs in blocks:
            mixed = _dot(wts[g].astype(BF), vnb[rs, cs]) + bst_ref[:, g:g + 1]
            mix_scr[rs, cs] = mixed
            sg_scr[rs, cs] = u[rs, cs] * mixed

        att = att_ref[...]
        sg = sg_scr[...]
        sa, dsa = _silu_and_grad(ga_ref[...])
        sb, dsb = _silu_and_grad(gb_ref[...])
        ya = (att * sa).astype(BF)
        yb = (sg * sb).astype(BF)
        pa = _dot(ya, wpa_ref[...])
        pb = _dot(yb, wpb_ref[...])
        ga = _sigmoid(gta_ref[...] + bg_ref[:, 0:D_MODEL])
        gb = _sigmoid(gtb_ref[...] + bg_ref[:, D_MODEL:2 * D_MODEL])
        merged = (ga * pa + gb * pb).astype(BF)
        out = x_ref[...] + _dot(merged, wout_ref[...])
        r2 = lax.rsqrt(jnp.mean(out * out, axis=-1, keepdims=True) + EPS)
        nrm = out * r2
        fg = fg_ref[...]
        err = nrm * fg - t_ref[...]
        loss_ref[...] += 0.5 * jnp.sum(jnp.mean(err * err, axis=-1, keepdims=True))
        dy = err * (1.0 / D_MODEL)
        gfg_ref[...] += jnp.sum(dy * nrm, axis=0, keepdims=True)
        dn = dy * fg
        d_out = r2 * (dn - nrm * jnp.mean(dn * nrm, axis=-1, keepdims=True))
        dout_ref[...] = d_out
        d_outb = d_out.astype(BF)
        acc_out[...] += _dot_tn(merged, d_outb)
        dm = _dot_nt(d_outb, wout_ref[...])
        d_pa = (dm * ga).astype(BF)
        d_pb = (dm * gb).astype(BF)
        d_gta = dm * pa * (ga * (1.0 - ga))
        d_gtb = dm * pb * (gb * (1.0 - gb))
        gbg_ref[:, 0:D_MODEL] += jnp.sum(d_gta, axis=0, keepdims=True)
        gbg_ref[:, D_MODEL:2 * D_MODEL] += jnp.sum(d_gtb, axis=0, keepdims=True)
        dzt_ref[:, 2 * D_A:2 * D_A + D_MODEL] = d_gta.astype(BF)
        dzt_ref[:, 2 * D_A + D_MODEL:] = d_gtb.astype(BF)
        acc_pa[...] += _dot_tn(ya, d_pa)
        acc_pb[...] += _dot_tn(yb, d_pb)
        d_ya = _dot_nt(d_pa, wpa_ref[...])
        d_yb = _dot_nt(d_pb, wpb_ref[...])
        d_att = (d_ya * sa).astype(BF)
        for hd in range(N_HEADS):
            datt_ref[hd] = d_att[:, hd * HEAD_DIM:(hd + 1) * HEAD_DIM]
        dzt_ref[:, 0:D_A] = (d_ya * att * dsa).astype(BF)
        dzt_ref[:, D_A:2 * D_A] = (d_yb * sg * dsb).astype(BF)

        dsg = d_yb * sb
        dzs_ref[:, 0:D_B] = (dsg * mix_scr[...] * du).astype(BF)
        dmix = dsg * u
        for g, rs, cs in blocks:
            dmb = dmix[rs, cs].astype(BF)
            bs_acc[:, cs] += dmix[rs, cs]
            gws_ref[g] += _dot_nt(dmb, vnb[rs, cs])
            dvn_scr[rs, cs] = _dot(wts[g].T.astype(BF), dmb)
        dvn = dvn_scr[...]
        glg_ref[...] += jnp.sum(dvn * xh, axis=0, keepdims=True)
        glb_ref[...] += jnp.sum(dvn, axis=0, keepdims=True)
        dxh = dvn * lg_ref[...]
        dvv = rstd * (dxh - jnp.mean(dxh, axis=-1, keepdims=True)
                      - xh * jnp.mean(dxh * xh, axis=-1, keepdims=True))
        dzs_ref[:, D_B:2 * D_B] = (dvv * dv).astype(BF)

        @pl.when(i == nt - 1)
        def _():
            cps = [pltpu.make_async_copy(acc_out, gwout_hbm, sems.at[0]),
                   pltpu.make_async_copy(acc_pa, gwpa_hbm, sems.at[1]),
                   pltpu.make_async_copy(acc_pb, gwpb_hbm, sems.at[2])]
            for cp in cps:
                cp.start()
            lane = lax.broadcasted_iota(jnp.int32, (SGU_CHUNK, 128), 1)
            cols = jnp.zeros((SGU_CHUNK, 128), F32)
            for g in range(N_GROUPS):
                gws_ref[g] = jnp.where(tri, gws_ref[g], 0.0)
                col = jnp.sum(bs_acc[:, g * 128:(g + 1) * 128], axis=-1, keepdims=True)
                cols = jnp.where(lane == g, col, cols)
            gbs_ref[...] = cols
            for cp in cps:
                cp.wait()

    c2 = lambda i: (0, 0)
    c3 = lambda i: (0, 0, 0)
    zcol = lambda w, blk: pl.BlockSpec((tm, w), lambda i: (i, blk))
    row = lambda w: pl.BlockSpec((tm, w), lambda i: (i, 0))
    return pl.pallas_call(
        body, name="tail", grid=(nt,),
        out_shape=(jax.ShapeDtypeStruct((S, D_MODEL), F32), jax.ShapeDtypeStruct((N_HEADS, S, HEAD_DIM), BF),
                   jax.ShapeDtypeStruct((S, 3072), BF), jax.ShapeDtypeStruct((S, 2 * D_B), BF),
                   jax.ShapeDtypeStruct((D_MODEL, D_MODEL), F32), jax.ShapeDtypeStruct((D_A, D_MODEL), F32),
                   jax.ShapeDtypeStruct((D_B, D_MODEL), F32),
                   jax.ShapeDtypeStruct((1, 2 * D_MODEL), F32), jax.ShapeDtypeStruct((1, D_MODEL), F32),
                   jax.ShapeDtypeStruct((1, 128), F32),
                   jax.ShapeDtypeStruct((N_GROUPS, 128, 128), F32), jax.ShapeDtypeStruct((SGU_CHUNK, 128), F32),
                   jax.ShapeDtypeStruct((1, D_B), F32), jax.ShapeDtypeStruct((1, D_B), F32)),
        in_specs=[row(D_A), zcol(512, 0), zcol(512, 1), zcol(512, 2), zcol(512, 3),
                  zcol(D_MODEL, 2), zcol(D_MODEL, 3), row(D_MODEL), row(D_MODEL),
                  pl.BlockSpec((D_A, D_MODEL), c2), pl.BlockSpec((D_B, D_MODEL), c2),
                  pl.BlockSpec((D_MODEL, D_MODEL), c2),
                  pl.BlockSpec((1, 2 * D_MODEL), c2), pl.BlockSpec((1, D_MODEL), c2),
                  pl.BlockSpec((1, D_B), c2), pl.BlockSpec((1, D_B), c2),
                  pl.BlockSpec((N_GROUPS, 128, 128), c3), pl.BlockSpec((128, N_GROUPS), c2)],
        out_specs=[row(D_MODEL), pl.BlockSpec((N_HEADS, tm, HEAD_DIM), lambda i: (0, i, 0)),
                   row(3072), row(2 * D_B), _ANY, _ANY, _ANY,
                   pl.BlockSpec((1, 2 * D_MODEL), c2), pl.BlockSpec((1, D_MODEL), c2),
                   pl.BlockSpec((1, 128), c2),
                   pl.BlockSpec((N_GROUPS, 128, 128), c3), pl.BlockSpec((SGU_CHUNK, 128), c2),
                   pl.BlockSpec((1, D_B), c2), pl.BlockSpec((1, D_B), c2)],
        scratch_shapes=[pltpu.VMEM((D_MODEL, D_MODEL), F32), pltpu.VMEM((D_A, D_MODEL), F32),
                        pltpu.VMEM((D_B, D_MODEL), F32),
                        pltpu.VMEM((tm, D_B), F32), pltpu.VMEM((tm, D_B), F32), pltpu.VMEM((tm, D_B), F32),
                        pltpu.VMEM((SGU_CHUNK, D_B), F32), pltpu.SemaphoreType.DMA((3,))],
        compiler_params=_params(58, dimension_semantics=("arbitrary",)),
    )(att, zrest, zrest, zrest, zrest, zrest, zrest, x, target, w_pa, w_pb, w_out, b_gate, final_g,
      ln_g, ln_b, w_s, b_s_t)


_DZ_MAP = ((0, 0), (1, 0), (2, 0), (3, 0), (4, 0), (4, 1), (3, 1), (3, 2), (3, 3), (3, 4), (3, 5))


def _dh_gradx(dq, dk, dv, dzt, dzs, w_in_bf, x, norm_g, d_out, tm=512, after=()):
    S = x.shape[0]

    def body(dq_ref, dk_ref, dv_ref, dzt_ref, dzs_ref, w_ref, x_ref, g_ref, dout_ref, gx_ref, gn_ref):
        i = pl.program_id(0)

        @pl.when(i == 0)
        def _():
            gn_ref[...] = jnp.zeros_like(gn_ref)

        pieces = (dq_ref, dk_ref, dv_ref, dzt_ref, dzs_ref)
        dh = jnp.zeros((tm, D_MODEL), F32)
        for j, (pc, blk) in enumerate(_DZ_MAP):
            dh += _dot_nt(pieces[pc][:, blk * 512:(blk + 1) * 512], w_ref[:, j * 512:(j + 1) * 512])
        xv = x_ref[...]
        r = lax.rsqrt(jnp.mean(xv * xv, axis=-1, keepdims=True) + EPS)
        nrm = xv * r
        gn_ref[...] += jnp.sum(dh * nrm, axis=0, keepdims=True)
        dn = dh * g_ref[...]
        gx_ref[...] = r * (dn - nrm * jnp.mean(dn * nrm, axis=-1, keepdims=True)) + dout_ref[...]

    row = lambda w: pl.BlockSpec((tm, w), lambda i: (i, 0))
    c2 = lambda i: (0, 0)
    return pl.pallas_call(
        _after(body, 9, after), name="dh_gradx", grid=(S // tm,),
        out_shape=(jax.ShapeDtypeStruct((S, D_MODEL), F32), jax.ShapeDtypeStruct((1, D_MODEL), F32)),
        in_specs=[row(512), row(512), row(512), row(3072), row(1024),
                  pl.BlockSpec((D_MODEL, D_IN), c2, pipeline_mode=pl.Buffered(1)), row(D_MODEL),
                  pl.BlockSpec((1, D_MODEL), c2), row(D_MODEL)]
        + [_ANY] * len(after),
        out_specs=[row(D_MODEL), pl.BlockSpec((1, D_MODEL), c2)],
        compiler_params=_params(48, dimension_semantics=("arbitrary",)),
    )(dq, dk, dv, dzt, dzs, w_in_bf, x, norm_g, d_out, *after)


def _gw_in(ht, dq, dk, dv, dzt, dzs, tn=512, after=()):
    S = ht.shape[1]
    per = 512 // tn
    cols = tuple((pc, per * blk + h) for pc, blk in _DZ_MAP for h in range(per))

    def body(ht_ref, dq_ref, dk_ref, dv_ref, dzt_ref, dzs_ref, o_ref, ob_ref):
        j = pl.program_id(0)
        pieces = (dq_ref, dk_ref, dv_ref, dzt_ref, dzs_ref)
        for pc in range(5):
            hit = functools.reduce(jnp.logical_or, [j == jj for jj, (p, _) in enumerate(cols) if p == pc])

            @pl.when(hit)
            def _(pc=pc):
                g = _dot(ht_ref[...], pieces[pc][...])
                o_ref[...] = g
                ob_ref[...] = g.astype(BF)

    def piece_spec(pc):
        cur = next(blk for p, blk in cols if p == pc)
        held = []
        for p, blk in cols:
            cur = blk if p == pc else cur
            held.append(cur)

        def index_map(j):
            blk = jnp.int32(held[0])
            for jj in range(1, len(held)):
                if held[jj] != held[jj - 1]:
                    blk = jnp.where(j >= jj, jnp.int32(held[jj]), blk)
            return (0, blk)

        return pl.BlockSpec((S, tn), index_map)

    return pl.pallas_call(
        _after(body, 6, after), name="gw_in", grid=(len(cols),),
        out_shape=(jax.ShapeDtypeStruct((D_MODEL, D_IN), F32), jax.ShapeDtypeStruct((D_MODEL, D_IN), BF)),
        in_specs=[pl.BlockSpec((D_MODEL, S), lambda j: (0, 0), pipeline_mode=pl.Buffered(1))]
        + [piece_spec(pc) for pc in range(5)]
        + [_ANY] * len(after),
        out_specs=[pl.BlockSpec((D_MODEL, tn), lambda j: (0, j)), pl.BlockSpec((D_MODEL, tn), lambda j: (0, j))],
        compiler_params=_params(56, dimension_semantics=("arbitrary",)),
    )(ht, dq, dk, dv, dzt, dzs, *after)


_HBM = pl.BlockSpec(memory_space=pltpu.HBM)
_SEM = pl.BlockSpec(memory_space=pltpu.SEMAPHORE)
_ANY = pl.BlockSpec(memory_space=pl.ANY)
_EFFECT = pltpu.SideEffectType.DATAFLOW_SIDE_EFFECTING


def _in_hbm(a):
    return pltpu.with_memory_space_constraint(a, pltpu.HBM)


def _after(body, n_in, after):
    if not after:
        return body
    return lambda *refs: body(*refs[:n_in], *refs[n_in + len(after):])


class _Started:
    def __init__(self, send, recv, bufs, token):
        self.send, self.recv, self.bufs, self.token = send, recv, bufs, token


def _split_start(name, bufs, n_copies, copies, after=()):
    nb = len(bufs)

    def body(*refs):
        refs = refs[:nb] + refs[nb + len(after):]
        for cp in copies(refs[:nb], refs[nb], refs[nb + 1]):
            cp.start()
        refs[-1][...] = jnp.zeros_like(refs[-1])

    outs = pl.pallas_call(
        body, name=name,
        out_shape=(pltpu.SemaphoreType.DMA((n_copies,)), pltpu.SemaphoreType.DMA((n_copies,)),
                   *[pltpu.HBM(b.shape, b.dtype) for b in bufs], jax.ShapeDtypeStruct((8, 128), F32)),
        in_specs=[_HBM] * nb + [_ANY] * len(after),
        out_specs=(_SEM, _SEM, *[_HBM] * nb, pl.BlockSpec(memory_space=pltpu.VMEM)),
        input_output_aliases={k: 2 + k for k in range(nb)},
        compiler_params=_params(1, has_side_effects=_EFFECT),
    )(*[_in_hbm(b) for b in bufs], *after)
    return _Started(outs[0], outs[1], list(outs[2:2 + nb]), outs[-1])


def _split_wait(name, started, copies, after):
    nb = len(started.bufs)

    def body(*refs):
        for cp in copies(refs[:nb], refs[nb], refs[nb + 1]):
            cp.wait_send()
            cp.wait_recv()

    return list(pl.pallas_call(
        body, name=name,
        out_shape=tuple(pltpu.HBM(b.shape, b.dtype) for b in started.bufs),
        in_specs=[_HBM] * nb + [_SEM, _SEM, _ANY],
        out_specs=tuple([_HBM] * nb),
        input_output_aliases={k: k for k in range(nb)},
        compiler_params=_params(1, has_side_effects=_EFFECT),
    )(*started.bufs, started.send, started.recv, after))


def _x1_copies(ws):
    def copies(refs, send_sems, recv_sems):
        x, y, c, _ = _mesh_pos()
        out = []
        for k, w in enumerate(ws):
            for s in range(N_SHARD):
                out.append(pltpu.make_async_remote_copy(
                    src_ref=_UNITS[w](refs[k], s, 1 - c), dst_ref=refs[len(ws) + k].at[s],
                    send_sem=send_sems.at[N_SHARD * k + s], recv_sem=recv_sems.at[N_SHARD * k + s],
                    device_id=(x, y, 1 - c), device_id_type=MESH))
        return out
    return copies


def _x2_copies(n):
    def copies(refs, send_sems, recv_sems):
        x, y, c, chips = _mesh_pos()
        out = []
        for j, (cx, cy) in enumerate(chips):
            for k in range(n):
                out.append(pltpu.make_async_remote_copy(
                    src_ref=refs[k].at[2 * cx + cy], dst_ref=refs[n + k].at[j],
                    send_sem=send_sems.at[3 * k + j], recv_sem=recv_sems.at[3 * k + j],
                    device_id=(cx, cy, c), device_id_type=MESH))
        return out
    return copies


def _x3_copies(ws):
    def copies(refs, send_sems, recv_sems):
        x, y, c, _ = _mesh_pos()
        out = []
        for k, w in enumerate(ws):
            rows = _HALF_ROWS[w]
            mine = refs[k].at[pl.ds(_mo(c * rows, rows), rows), :]
            out.append(pltpu.make_async_remote_copy(
                src_ref=mine, dst_ref=mine, send_sem=send_sems.at[k], recv_sem=recv_sems.at[k],
                device_id=(x, y, 1 - c), device_id_type=MESH))
        return out
    return copies


def _x1_lands(ws, dtype=F32):
    return [lax.empty((N_SHARD,) + _UNIT_SHAPES[w], dtype) for w in ws]


def _x2_lands(ws):
    return [lax.empty((3,) + _UNIT_SHAPES[w], BF) for w in ws]


def _grad_add1(w, g, recv, pos):
    ur, uc = _UNIT_SHAPES[w]

    def body(pos_ref, g_ref, r_ref, own_ref, csb_ref):
        v = g_ref[...] + r_ref[0].astype(F32)
        csb_ref[0] = v.astype(BF)

        @pl.when(pl.program_id(0) == pos_ref[1])
        def _():
            own_ref[...] = v

    u3 = lambda s, pos: (s, 0, 0)
    return pl.pallas_call(
        body, name=f"grad_add1_{w}",
        grid_spec=pltpu.PrefetchScalarGridSpec(
            num_scalar_prefetch=1, grid=(N_SHARD,),
            in_specs=[pl.BlockSpec((ur, uc), lambda s, pos: (pos[0], s)), pl.BlockSpec((1, ur, uc), u3)],
            out_specs=[pl.BlockSpec((ur, uc), lambda s, pos: (0, 0)), pl.BlockSpec((1, ur, uc), u3)]),
        out_shape=(jax.ShapeDtypeStruct((ur, uc), F32), jax.ShapeDtypeStruct((N_SHARD, ur, uc), BF)),
        compiler_params=_params(40, dimension_semantics=("arbitrary",)),
    )(pos, g, recv)


def _grad_add1_group(ws, gs, recvs):
    n = len(ws)

    def body(*refs):
        c = lax.axis_index("c")
        for k, w in enumerate(ws):
            g, r, cs, csb = refs[k], refs[n + k], refs[2 * n + k], refs[3 * n + k]
            for s in range(N_SHARD):
                v = _UNITS[w](g, s, c)[...] + r[s]
                cs[s] = v
                csb[s] = v.astype(BF)

    vm = pl.BlockSpec(memory_space=pltpu.VMEM)
    outs = pl.pallas_call(
        body, name="grad_add1_group",
        out_shape=tuple(jax.ShapeDtypeStruct((N_SHARD,) + _UNIT_SHAPES[w], dt) for dt in (F32, BF) for w in ws),
        in_specs=[vm] * (2 * n), out_specs=[vm] * (2 * n),
        compiler_params=_params(32),
    )(*gs, *recvs)
    return list(outs[:n]), list(outs[n:])


def _grad_add2_group(ws, css, recvs):
    n = len(ws)

    def body(*refs):
        x, y, c, _ = _mesh_pos()
        for k, w in enumerate(ws):
            cs, r, o = refs[k], refs[n + k], refs[2 * n + k]
            rows = _HALF_ROWS[w]
            total = ((cs[2 * x + y] + r[0].astype(F32)) + r[1].astype(F32)) + r[2].astype(F32)
            o[pl.ds(_mo(c * rows, rows), rows), :] = total

    vm = pl.BlockSpec(memory_space=pltpu.VMEM)
    return list(pl.pallas_call(
        body, name="grad_add2_group",
        out_shape=tuple(jax.ShapeDtypeStruct(_SHARD_SHAPES[w], F32) for w in ws),
        in_specs=[vm] * (2 * n), out_specs=[vm] * n,
        compiler_params=_params(32),
    )(*css, *recvs))


def _grad_add2(w, own, recv, pos):
    ur, uc = _UNIT_SHAPES[w]
    nt = 4
    tr = ur // nt

    def body(pos_ref, own_ref, r_ref, o_ref):
        o_ref[...] = ((own_ref[...] + r_ref[0].astype(F32)) + r_ref[1].astype(F32)) + r_ref[2].astype(F32)

    return pl.pallas_call(
        body, name=f"grad_add2_{w}",
        grid_spec=pltpu.PrefetchScalarGridSpec(
            num_scalar_prefetch=1, grid=(nt,),
            in_specs=[pl.BlockSpec((tr, uc), lambda t, pos: (t, 0)),
                      pl.BlockSpec((3, tr, uc), lambda t, pos: (0, t, 0))],
            out_specs=pl.BlockSpec((tr, uc), lambda t, pos: (pos[0] * nt + t, 0))),
        out_shape=jax.ShapeDtypeStruct(_SHARD_SHAPES[w], F32),
        compiler_params=_params(32, dimension_semantics=("arbitrary",)),
    )(pos, own, recv)


def _adamw_math(w, g, m, v):
    m = ADAM_B1 * m + (1.0 - ADAM_B1) * g
    v = ADAM_B2 * v + (1.0 - ADAM_B2) * (g * g)
    m_hat = m / ADAM_C1
    v_hat = v / ADAM_C2
    delta = -ADAM_LR * (m_hat / (jnp.sqrt(v_hat) + ADAM_EPS) + ADAM_WD * w)
    return delta, m, v


def _adamw_group(ws_, gs, ms, vs, after=()):
    n = len(ws_)

    def body(*refs):
        for k in range(n):
            w, g, m, v = (refs[j * n + k] for j in range(4))
            d, nm, nv, gc = (refs[(4 + j) * n + k] for j in range(4))
            gv = g[...]
            d[...], nm[...], nv[...] = _adamw_math(w[...], gv, m[...], v[...])
            gc[...] = gv

    vm = pl.BlockSpec(memory_space=pltpu.VMEM)
    outs = pl.pallas_call(
        _after(body, 4 * n, after), name="adamw_group",
        out_shape=tuple(jax.ShapeDtypeStruct(a.shape, F32) for _ in range(4) for a in ws_),
        in_specs=[vm] * (4 * n) + [_ANY] * len(after), out_specs=[vm] * (4 * n),
        compiler_params=_params(32),
    )(*ws_, *gs, *ms, *vs, *after)
    return [tuple(outs[j * n + k] for j in range(4)) for k in range(n)]


def _adamw(name, w, g, m, v, tr=256, after=()):
    rows, cols = w.shape

    def body(w_ref, g_ref, m_ref, v_ref, d_ref, nm_ref, nv_ref, gc_ref):
        gv = g_ref[...]
        d_ref[...], nm_ref[...], nv_ref[...] = _adamw_math(w_ref[...], gv, m_ref[...], v_ref[...])
        gc_ref[...] = gv

    spec = pl.BlockSpec((tr, cols), lambda i: (i, 0))
    return pl.pallas_call(
        _after(body, 4, after), name=name, grid=(rows // tr,),
        out_shape=tuple(jax.ShapeDtypeStruct((rows, cols), F32) for _ in range(4)),
        in_specs=[spec] * 4 + [_ANY] * len(after), out_specs=[spec] * 4,
        compiler_params=_params(32, dimension_semantics=("arbitrary",)),
    )(w, g, m, v, *after)


_REL_PAD = 384
_VEC_FIELDS = (("norm_g", 0, D_MODEL), ("b_gate", 1024, 2 * D_MODEL), ("sgu_ln_g", 3072, D_B),
               ("sgu_ln_b", 3584, D_B), ("b_s", 4096, N_GROUPS * 128), ("final_g", 4608, D_MODEL))
_LOSS_OFF = 5632
_REL_OFF = 5760
_NV = _REL_OFF + N_HEADS * _REL_PAD
_N_FIELDS = len(_VEC_FIELDS) + 2


_B_S_FIELD = [f[0] for f in _VEC_FIELDS].index("b_s")


def _assemble_row(dst, fields, transposed_b_s):
    for f, (_, off, n) in enumerate(_VEC_FIELDS):
        if transposed_b_s and f == _B_S_FIELD:
            t = fields[f][...].T
            for g in range(N_GROUPS):
                dst[:, off + 128 * g:off + 128 * (g + 1)] = t[g:g + 1, :]
        else:
            dst[:, off:off + n] = fields[f][...]
    for r in range(N_HEADS):
        dst[:, _REL_OFF + _REL_PAD * r:_REL_OFF + _REL_PAD * (r + 1)] = fields[len(_VEC_FIELDS)][r:r + 1, :]


def _small_reduce(grads, loss_row, after=()):
    n_in = _N_FIELDS + 1

    def body(*refs):
        g_refs, loss_ref = refs[:_N_FIELDS], refs[_N_FIELDS]
        out_v, out_w = refs[n_in:n_in + 2]
        mine_v, gath_v, gath_w, send_sems, recv_sems = refs[n_in + 2:]
        x, y, c, chips = _mesh_pos()
        me, sibling = (x, y, c), (x, y, 1 - c)

        _assemble_row(mine_v, g_refs, True)
        mine_v[:, _LOSS_OFF:_LOSS_OFF + 128] = loss_ref[...]
        mine_w = g_refs[-1]
        my_k = 4 * x + 2 * y + c
        gath_v[my_k] = mine_v[...]
        gath_w[my_k] = mine_w[...]

        def copy(k, gath, block, to, src=None):
            dst = gath.at[4 * block[0] + 2 * block[1] + block[2]]
            return pltpu.make_async_remote_copy(
                src_ref=dst if src is None else src, dst_ref=dst,
                send_sem=send_sems.at[k], recv_sem=recv_sems.at[k], device_id=to, device_id_type=MESH)

        bufs = ((gath_v, mine_v), (gath_w, mine_w))
        first, passed = [], []
        for b, (gath, mine) in enumerate(bufs):
            first.append(copy(7 * b, gath, me, sibling, src=mine))
            first += [copy(7 * b + 1 + j, gath, me, (*chip, c), src=mine) for j, chip in enumerate(chips)]
        for cp in first:
            cp.start()
        for b, (gath, _) in enumerate(bufs):
            for j, chip in enumerate(chips):
                copy(7 * b + 1 + j, gath, (*chip, c), me).wait_recv()
                cp = copy(7 * b + 4 + j, gath, (*chip, c), sibling)
                cp.start()
                passed.append(cp)
        for b, (gath, _) in enumerate(bufs):
            copy(7 * b, gath, sibling, me).wait_recv()
            for j, chip in enumerate(chips):
                copy(7 * b + 4 + j, gath, (*chip, 1 - c), me).wait_recv()
        for cp in first + passed:
            cp.wait_send()

        tot_v, tot_w = gath_v[0], gath_w[0]
        for k in range(1, 8):
            tot_v = tot_v + gath_v[k]
            tot_w = tot_w + gath_w[k]
        out_v[...] = tot_v
        out_w[...] = tot_w

    vm = pl.BlockSpec(memory_space=pltpu.VMEM)
    return pl.pallas_call(
        _after(body, n_in, after), name="small_reduce",
        out_shape=(jax.ShapeDtypeStruct((1, _NV), F32), jax.ShapeDtypeStruct((N_GROUPS * 128, 128), F32)),
        in_specs=[vm] * n_in + [_ANY] * len(after), out_specs=[vm] * 2,
        scratch_shapes=[pltpu.VMEM((1, _NV), F32), pltpu.VMEM((8, 1, _NV), F32),
                        pltpu.VMEM((8, N_GROUPS * 128, 128), F32),
                        pltpu.SemaphoreType.DMA((14,)), pltpu.SemaphoreType.DMA((14,))],
        compiler_params=_params(32),
    )(*grads, loss_row, *after)


def _small_adamw(tot_v, tot_w, params):
    n_in = 2 + 3 * _N_FIELDS

    def body(*refs):
        tv_ref, tw_ref = refs[:2]
        p_refs = [refs[2 + k * _N_FIELDS:2 + (k + 1) * _N_FIELDS] for k in range(3)]
        outs = refs[n_in:n_in + 4 * _N_FIELDS + 1]
        wmv = refs[-1]
        for k in range(3):
            _assemble_row(wmv.at[k], p_refs[k], False)
            wmv[k, :, _LOSS_OFF:_LOSS_OFF + 128] = jnp.zeros((1, 128), F32)
        tot_v, tot_w = tv_ref[...], tw_ref[...]
        res_v = (tot_v,) + _adamw_math(wmv[0], tot_v, wmv[1], wmv[2])
        res_w = (tot_w,) + _adamw_math(p_refs[0][-1][...], tot_w, p_refs[1][-1][...], p_refs[2][-1][...])
        for kind in range(4):
            o = outs[kind * _N_FIELDS:(kind + 1) * _N_FIELDS]
            for f, (_, off, n) in enumerate(_VEC_FIELDS):
                o[f][...] = res_v[kind][:, off:off + n]
            for r in range(N_HEADS):
                o[len(_VEC_FIELDS)][r:r + 1, :] = res_v[kind][:, _REL_OFF + _REL_PAD * r:_REL_OFF + _REL_PAD * (r + 1)]
            o[-1][...] = res_w[kind]
        outs[-1][...] = tot_v[:, _LOSS_OFF:_LOSS_OFF + 128]

    field_shapes = [(1, n) for _, _, n in _VEC_FIELDS] + [(N_HEADS, _REL_PAD), (N_GROUPS * 128, 128)]
    vm = pl.BlockSpec(memory_space=pltpu.VMEM)
    operands = [tot_v, tot_w] + [a for p in params for a in p]
    assert len(operands) == n_in
    outs = pl.pallas_call(
        body, name="small_adamw",
        out_shape=tuple(jax.ShapeDtypeStruct(s, F32) for _ in range(4) for s in field_shapes)
        + (jax.ShapeDtypeStruct((1, 128), F32),),
        in_specs=[vm] * n_in, out_specs=[vm] * (4 * _N_FIELDS + 1),
        scratch_shapes=[pltpu.VMEM((3, 1, _NV), F32)],
        compiler_params=_params(32),
    )(*operands)
    return [outs[k * _N_FIELDS:(k + 1) * _N_FIELDS] for k in range(4)], outs[-1]


def _small_fields(norm_g, b_gate, ln_g, ln_b, b_s, final_g, rel_bias, w_s):
    rel = jnp.pad(rel_bias.reshape(N_HEADS, N_REL), ((0, 0), (0, _REL_PAD - N_REL)))
    return (norm_g, b_gate, ln_g, ln_b, b_s.reshape(1, N_GROUPS * 128), final_g.reshape(1, D_MODEL),
            rel, w_s.reshape(N_GROUPS * 128, 128))


def _small_outputs(fields):
    n_g, b_g, l_g, l_b, b_s, f_g, rel, w_s = fields
    return (n_g, b_g, rel[:, :N_REL].reshape(1, N_HEADS, N_REL), l_g, l_b,
            w_s.reshape(1, N_GROUPS, 128, 128), b_s.reshape(1, N_GROUPS, 128), f_g.reshape(D_MODEL))


def _bias_row(rel_bias):
    hi = rel_bias[:, N_REL - 1:N_REL]
    lo = rel_bias[:, 0:1]
    return jnp.concatenate([jnp.broadcast_to(hi, (N_HEADS, 384)), rel_bias[:, ::-1],
                            jnp.broadcast_to(lo, (N_HEADS, 191)), jnp.broadcast_to(hi, (N_HEADS, 192))], axis=1)


def kernel(x, norm_g, w_in, b_gate, rel_bias, sgu_ln_g, sgu_ln_b, w_s, b_s, w_pa, w_pb, w_out, final_g, loss_target, m_norm_g, m_w_in, m_b_gate, m_rel_bias, m_sgu_ln_g, m_sgu_ln_b, m_w_s, m_b_s, m_w_pa, m_w_pb, m_w_out, m_final_g, v_norm_g, v_w_in, v_b_gate, v_rel_bias, v_sgu_ln_g, v_sgu_ln_b, v_w_s, v_b_s, v_w_pa, v_w_pb, v_w_out, v_final_g):
    S = x.shape[1]
    xs = x.reshape(S, D_MODEL)
    tgt = loss_target.reshape(S, D_MODEL)
    big_w = (w_in[0], w_pa[0], w_pb[0], w_out[0])
    big_m = (m_w_in[0], m_w_pa[0], m_w_pb[0], m_w_out[0])
    big_v = (v_w_in[0], v_w_pa[0], v_w_pb[0], v_w_out[0])
    rel = rel_bias[0]
    ws = w_s[0]
    bst = b_s[0].T
    fg = final_g.reshape(1, D_MODEL)
    pos = jnp.stack([lax.axis_index("c"), 2 * lax.axis_index("x") + lax.axis_index("y")]).astype(jnp.int32)

    staged = _stage_weights((1, 2, 3), big_w[1:], pos)
    w_in_bf, = _ag_weights((0,), big_w[:1])
    ag_s = _split_start("ag_small_start", staged, 9, _gather_copies((1, 2, 3)), after=(w_in_bf,))

    ht, q3, k3, v3, zrest = _inproj_fwd(xs, norm_g, w_in_bf, after=(ag_s.token,))
    gp = _bias_row(rel)
    att, lse, band_bias = _attn_fwd(q3, k3, v3, gp)
    w_pa_bf, w_pb_bf, w_out_bf = _split_wait("ag_small_wait", ag_s, _gather_copies((1, 2, 3)), att)
    (d_out, d_att, dzt, dzs, gw_out, gw_pa, gw_pb, g_bgate, g_final, loss_row,
     g_ws, g_bs_t, g_lng, g_lnb) = _tail_sgu(
        att, zrest, xs, tgt, w_pa_bf, w_pb_bf, w_out_bf, b_gate, fg, sgu_ln_g, sgu_ln_b, ws, bst)
    ws_s, ws_i = (1, 2, 3), (0,)

    x1s = _split_start("gx1s_start", [gw_pa, gw_pb, gw_out] + _x1_lands(ws_s), 12, _x1_copies(ws_s))
    dq, dk, dv, d_gp = _attn_bwd(q3, k3, v3, d_att, lse, band_bias, after=(x1s.token,))
    got = _split_wait("gx1s_wait", x1s, _x1_copies(ws_s), dq)
    cs_s, csb_s = _grad_add1_group(ws_s, got[:3], got[3:])

    x2s = _split_start("gx2s_start", csb_s + _x2_lands(ws_s), 9, _x2_copies(3))
    gw_in, gw_in_bf = _gw_in(ht, dq, dk, dv, dzt, dzs, after=(x2s.token,))
    x1i = _split_start("gx1i_start", [gw_in_bf] + _x1_lands(ws_i, BF), 4, _x1_copies(ws_i))
    got = _split_wait("gx2s_wait", x2s, _x2_copies(3), x1i.token)
    halves_s = _grad_add2_group(ws_s, cs_s, got[3:])
    x3s = _split_start("gx3s_start", halves_s, 3, _x3_copies(ws_s))
    got = _split_wait("gx1i_wait", x1i, _x1_copies(ws_i), x3s.token)
    sum_i = _grad_add1(0, gw_in, got[1], pos)

    x2i = _split_start("gx2i_start", [sum_i[1]] + _x2_lands(ws_i), 3, _x2_copies(1))
    grad_x, g_norm = _dh_gradx(dq, dk, dv, dzt, dzs, w_in_bf, xs, norm_g, d_out, after=(x2i.token,))
    g_shards_s = _split_wait("gx3s_wait", x3s, _x3_copies(ws_s), grad_x)
    got = _split_wait("gx2i_wait", x2i, _x2_copies(1), grad_x)
    half_i = _grad_add2(0, sum_i[0], got[1], pos)
    x3i = _split_start("gx3i_start", [half_i], 1, _x3_copies(ws_i))
    big = [None] * 4
    big[1:] = _adamw_group(big_w[1:], g_shards_s, big_m[1:], big_v[1:], after=(x3i.token,))

    g_rel = jnp.pad(d_gp[:, 384:384 + N_REL][:, ::-1], ((0, 0), (0, _REL_PAD - N_REL)))
    small_grads = (g_norm, g_bgate, g_lng, g_lnb, g_bs_t, g_final, g_rel, g_ws.reshape(N_GROUPS * 128, 128))
    small_params = (_small_fields(norm_g, b_gate, sgu_ln_g, sgu_ln_b, b_s, final_g, rel_bias, w_s),
                    _small_fields(m_norm_g, m_b_gate, m_sgu_ln_g, m_sgu_ln_b, m_b_s, m_final_g, m_rel_bias, m_w_s),
                    _small_fields(v_norm_g, v_b_gate, v_sgu_ln_g, v_sgu_ln_b, v_b_s, v_final_g, v_rel_bias, v_w_s))
    tot_v, tot_w = _small_reduce(small_grads, loss_row, after=(x3i.token,))
    (gsum, sdelta, sm, sv), loss_out = _small_adamw(tot_v, tot_w, small_params)

    g_shard_i, = _split_wait("gx3i_wait", x3i, _x3_copies(ws_i), loss_out)
    big[0] = _adamw("adamw_w_in", big_w[0], g_shard_i, big_m[0], big_v[0])
    sg_out, sd_out, sm_out, sv_out = (_small_outputs(f) for f in (gsum, sdelta, sm, sv))
    loss = loss_out[0, 0]

    def assemble(small, bigs):
        n_g, b_g, r_b, l_g, l_b, w_s_, b_s_, f_g = small
        b_in, b_pa, b_pb, b_out = (b[None] for b in bigs)
        return (n_g, b_in, b_g, r_b, l_g, l_b, w_s_, b_s_, b_pa, b_pb, b_out, f_g)

    grads_out = assemble(sg_out, [b[3] for b in big])
    delta_out = assemble(sd_out, [b[0] for b in big])
    m_out = assemble(sm_out, [b[1] for b in big])
    v_out = assemble(sv_out, [b[2] for b in big])
    return (loss, grad_x.reshape(1, S, D_MODEL), *grads_out, *delta_out, *m_out, *v_out)
```

```python
import functools
import math

import jax
import jax.numpy as jnp
from jax import lax
from jax.experimental import pallas as pl
from jax.experimental.pallas import tpu as pltpu

F32 = jnp.float32
BF = jnp.bfloat16
MESH = pl.DeviceIdType.MESH

D_MODEL = 1024
D_A = 512
D_B = 512
D_IN = 5632
N_HEADS = 8
HEAD_DIM = 64
CHUNK = 64
N_PREV = 8
SGU_CHUNK = 128
N_GROUPS = 4
N_REL = 257
EPS = 1e-6
NEG_INF = -1e30
SCALE = HEAD_DIM ** -0.5

QB = 2 * CHUNK
KB = (N_PREV + 2) * CHUNK
PADK = N_PREV * CHUNK
ROLL_W = 1024
N_RING = KB // QB
KEEP = N_RING - 1

ADAM_LR = 0.001
ADAM_B1 = 0.9
ADAM_B2 = 0.999
ADAM_EPS = 1e-08
ADAM_WD = 0.01
ADAM_STEP = 10
ADAM_C1 = 1.0 - ADAM_B1 ** ADAM_STEP
ADAM_C2 = 1.0 - ADAM_B2 ** ADAM_STEP

N_SHARD = 4
SHARD_IN = D_IN // N_SHARD
MIB = 1024 * 1024


VMEM_RESERVE_MIB = 60


def _params(vmem_mib, **kw):
    assert vmem_mib <= VMEM_RESERVE_MIB
    return pltpu.CompilerParams(vmem_limit_bytes=VMEM_RESERVE_MIB * MIB, **kw)


def _sigmoid(x):
    return 1.0 / (1.0 + jnp.exp(-x))


def _silu_and_grad(x):
    s = _sigmoid(x)
    return x * s, s * (1.0 + x * (1.0 - s))


_GELU_C = math.sqrt(2.0 / math.pi)
_GELU_A = 0.044715


def _gelu_and_grad(x):
    x2 = x * x
    t = jnp.tanh(_GELU_C * (x + _GELU_A * (x2 * x)))
    cdf = 0.5 * (1.0 + t)
    grad = cdf + 0.5 * x * (1.0 - t * t) * (_GELU_C * (1.0 + 3.0 * _GELU_A * x2))
    return x * cdf, grad


def _dot(a, b):
    return jnp.dot(a, b, preferred_element_type=F32)


def _dot_nt(a, b):
    return lax.dot_general(a, b, (((1,), (1,)), ((), ())), preferred_element_type=F32)


def _dot_tn(a, b):
    return lax.dot_general(a, b, (((0,), (0,)), ((), ())), preferred_element_type=F32)


def _mo(v, m):
    return v if isinstance(v, int) else pl.multiple_of(v, m)


def _unit_in(ref, s, p):
    return ref.at[pl.ds(_mo(p * 512, 512), 512), pl.ds(_mo(s * SHARD_IN, 128), SHARD_IN)]


def _unit_p(ref, s, p):
    return ref.at[pl.ds(_mo(p * 256, 256), 256), pl.ds(_mo(s * 256, 128), 256)]


def _unit_out(ref, s, p):
    return ref.at[pl.ds(_mo(s * 256 + p * 128, 128), 128), :]


_UNITS = (_unit_in, _unit_p, _unit_p, _unit_out)
_HALF_ROWS = (512, 256, 256, 128)
_UNIT_SHAPES = ((512, SHARD_IN), (256, 256), (256, 256), (128, D_MODEL))
_FULL_SHAPES = ((D_MODEL, D_IN), (D_A, D_MODEL), (D_B, D_MODEL), (D_MODEL, D_MODEL))
_SHARD_SHAPES = ((D_MODEL, SHARD_IN), (D_A, 256), (D_B, 256), (256, D_MODEL))


def _mesh_pos():
    x, y, c = lax.axis_index("x"), lax.axis_index("y"), lax.axis_index("c")
    chips = [(1 - x, y), (x, 1 - y), (1 - x, 1 - y)]
    return x, y, c, chips


def _ag_weights(ws, shards):
    n = len(ws)

    def body(*refs):
        ins, outs, stage = refs[:n], refs[n:2 * n], refs[2 * n:3 * n]
        send_sems, recv_sems, local_sems = refs[3 * n:]
        x, y, c, chips = _mesh_pos()
        s_me = 2 * x + y
        sibling = (x, y, 1 - c)
        def rows_of(k, p):
            rows = _HALF_ROWS[ws[k]]
            return pl.ds(_mo(p * rows, rows), rows)

        def half(k, p):
            return stage[k].at[rows_of(k, p), :]

        def unit(k, s, p):
            return _UNITS[ws[k]](outs[k], s, p)

        def rcopy(k, i, src, dst, to):
            return pltpu.make_async_remote_copy(src_ref=src, dst_ref=dst, send_sem=send_sems.at[k, i],
                                                recv_sem=recv_sems.at[k, i], device_id=to, device_id_type=MESH)

        for k in range(n):
            stage[k][rows_of(k, c), :] = ins[k][rows_of(k, c), :].astype(BF)
        sends = []
        for j, (cx, cy) in enumerate(chips):
            for k in range(n):
                cp = rcopy(k, j, half(k, c), unit(k, s_me, c), (cx, cy, c))
                cp.start()
                sends.append(cp)
        for k in range(n):
            stage[k][rows_of(k, 1 - c), :] = ins[k][rows_of(k, 1 - c), :].astype(BF)
        local = []
        for k in range(n):
            for p in range(2):
                cp = pltpu.make_async_copy(half(k, p), unit(k, s_me, p), local_sems.at[k, p])
                cp.start()
                local.append(cp)
        for j, (cx, cy) in enumerate(chips):
            for k in range(n):
                landed = unit(k, 2 * cx + cy, c)
                rcopy(k, j, landed, landed, (cx, cy, c)).wait_recv()
                cp = rcopy(k, 3 + j, landed, landed, sibling)
                cp.start()
                sends.append(cp)
        for j, (cx, cy) in enumerate(chips):
            for k in range(n):
                other = unit(k, 2 * cx + cy, 1 - c)
                rcopy(k, 3 + j, other, other, sibling).wait_recv()
        for cp in sends:
            cp.wait_send()
        for cp in local:
            cp.wait()

    vm = pl.BlockSpec(memory_space=pltpu.VMEM)
    return pl.pallas_call(
        body, name="ag_weights",
        out_shape=tuple(jax.ShapeDtypeStruct(_FULL_SHAPES[w], BF) for w in ws),
        in_specs=[vm] * n, out_specs=[_ANY] * n,
        scratch_shapes=[pltpu.VMEM(_SHARD_SHAPES[w], BF) for w in ws]
        + [pltpu.SemaphoreType.DMA((n, 6)), pltpu.SemaphoreType.DMA((n, 6)), pltpu.SemaphoreType.DMA((n, 2))],
        compiler_params=_params(40),
    )(*shards)


def _shard_of(ref, w, s):
    if w == 0:
        return ref.at[:, pl.ds(_mo(s * SHARD_IN, 128), SHARD_IN)]
    if w == 3:
        return ref.at[pl.ds(_mo(s * 256, 256), 256), :]
    return ref.at[:, pl.ds(_mo(s * 256, 128), 256)]


def _stage_weights(ws, shards, pos):
    n = len(ws)

    def body(pos_ref, *refs):
        for k in range(n):
            refs[n + k][...] = refs[k][...].astype(BF)

    def spec(w):
        shape = _SHARD_SHAPES[w]
        if w == 3:
            return pl.BlockSpec(shape, lambda i, pos: (pos[1], 0))
        return pl.BlockSpec(shape, lambda i, pos: (0, pos[1]))

    return list(pl.pallas_call(
        body, name="stage_weights",
        grid_spec=pltpu.PrefetchScalarGridSpec(
            num_scalar_prefetch=1, grid=(1,),
            in_specs=[pl.BlockSpec(_SHARD_SHAPES[w], lambda i, pos: (0, 0)) for w in ws],
            out_specs=[spec(w) for w in ws]),
        out_shape=tuple(jax.ShapeDtypeStruct(_FULL_SHAPES[w], BF) for w in ws),
        compiler_params=_params(16, dimension_semantics=("arbitrary",)),
    )(pos, *shards))


def _gather_copies(ws):
    def copies(refs, send_sems, recv_sems):
        x, y, c, chips = _mesh_pos()
        out = []
        for j, (cx, cy) in enumerate(chips):
            for k, w in enumerate(ws):
                mine = _shard_of(refs[k], w, 2 * x + y)
                out.append(pltpu.make_async_remote_copy(
                    src_ref=mine, dst_ref=mine, send_sem=send_sems.at[3 * k + j], recv_sem=recv_sems.at[3 * k + j],
                    device_id=(cx, cy, c), device_id_type=MESH))
        return out
    return copies


def _inproj_fwd(x, norm_g, w_in_bf, tm=512, after=()):
    S = x.shape[0]

    n_blk = D_IN // 512

    def body(x_ref, g_ref, w_hbm, ht_ref, q_ref, k_ref, v_ref, zr_ref, w_scr, sems):
        i = pl.program_id(0)

        def fetch(j):
            cols = pl.ds(j * 512, 512)
            return pltpu.make_async_copy(w_hbm.at[:, cols], w_scr.at[:, cols], sems.at[j])

        @pl.when(i == 0)
        def _():
            for j in range(n_blk):
                fetch(j).start()

        def step(first):
            xv = x_ref[...]
            r = lax.rsqrt(jnp.mean(xv * xv, axis=-1, keepdims=True) + EPS)
            hf = (xv * r) * g_ref[...]
            ht_ref[...] = hf.T.astype(BF)
            h = hf.astype(BF)
            heads = (q_ref, k_ref, v_ref)
            for j in range(n_blk):
                if first:
                    fetch(j).wait()
                z = _dot(h, w_scr[:, j * 512:(j + 1) * 512])
                if j < 3:
                    zb = z.astype(BF)
                    for hd in range(N_HEADS):
                        heads[j][hd] = zb[:, hd * HEAD_DIM:(hd + 1) * HEAD_DIM]
                else:
                    zr_ref[:, (j - 3) * 512:(j - 2) * 512] = z

        pl.when(i == 0)(functools.partial(step, True))
        pl.when(i > 0)(functools.partial(step, False))

    head_major = jax.ShapeDtypeStruct((N_HEADS, S, HEAD_DIM), BF)
    head_spec = pl.BlockSpec((N_HEADS, tm, HEAD_DIM), lambda i: (0, i, 0))
    return pl.pallas_call(
        _after(body, 3, after), name="inproj_fwd", grid=(S // tm,),
        out_shape=(jax.ShapeDtypeStruct((D_MODEL, S), BF), head_major, head_major, head_major,
                   jax.ShapeDtypeStruct((S, D_IN - 3 * D_A), F32)),
        in_specs=[pl.BlockSpec((tm, D_MODEL), lambda i: (i, 0)),
                  pl.BlockSpec((1, D_MODEL), lambda i: (0, 0)),
                  _ANY]
        + [_ANY] * len(after),
        out_specs=[pl.BlockSpec((D_MODEL, tm), lambda i: (0, i)),
                   head_spec, head_spec, head_spec,
                   pl.BlockSpec((tm, D_IN - 3 * D_A), lambda i: (i, 0))],
        scratch_shapes=[pltpu.VMEM((D_MODEL, D_IN), BF), pltpu.SemaphoreType.DMA((n_blk,))],
        compiler_params=_params(52, dimension_semantics=("arbitrary",)),
    )(x, norm_g, w_in_bf, *after)


def _skew_table(gp_row):
    row = lax.broadcasted_iota(jnp.int32, (QB, ROLL_W), 0)
    t = jnp.broadcast_to(gp_row, (QB, ROLL_W))
    for b in range(7):
        t = jnp.where(((row >> b) & 1) == 1, pltpu.roll(t, 1 << b, axis=1), t)
    return t


def _unskew_sum(d):
    row = lax.broadcasted_iota(jnp.int32, (QB, ROLL_W), 0)
    for b in range(7):
        d = jnp.where(((row >> b) & 1) == 1, pltpu.roll(d, ROLL_W - (1 << b), axis=1), d)
    return jnp.sum(d, axis=0, keepdims=True)


def _struct_mask():
    a = lax.broadcasted_iota(jnp.int32, (QB, KB), 0) // CHUNK
    b = lax.broadcasted_iota(jnp.int32, (QB, KB), 1) // CHUNK
    return (b >= a) & (b <= a + N_PREV)


def _load_kv(k_hbm, v_hbm, k_scr, v_scr, sems, S, meanwhile=lambda: None):
    zeros = jnp.zeros((N_HEADS, PADK, HEAD_DIM), BF)
    k_scr[:, 0:PADK, :] = zeros
    v_scr[:, 0:PADK, :] = zeros
    ck = pltpu.make_async_copy(k_hbm, k_scr.at[:, pl.ds(PADK, S), :], sems.at[0])
    cv = pltpu.make_async_copy(v_hbm, v_scr.at[:, pl.ds(PADK, S), :], sems.at[1])
    ck.start()
    cv.start()
    meanwhile()
    ck.wait()
    cv.wait()


_BATCH_NT = (((2,), (2,)), ((0,), (0,)))
_BATCH_NN = (((2,), (1,)), ((0,), (0,)))
_BATCH_TN = (((1,), (1,)), ((0,), (0,)))


def _bdot(a, b, dims):
    return lax.dot_general(a, b, dims, preferred_element_type=F32)


def _scaled(q):
    return q * jnp.asarray(SCALE, BF)


def _scores(qs, kb, bias, i, front):
    s = _bdot(qs, kb, _BATCH_NT) + bias
    if front:
        col = lax.broadcasted_iota(jnp.int32, (1, 1, KB), 2)
        s = jnp.where(col >= PADK - i * QB, s, NEG_INF)
    return s


def _attn_fwd(q3, k3, v3, gp):
    S = q3.shape[1]

    def body(q_ref, k_hbm, v_hbm, gp_ref, o_ref, lse_ref, bias_ref, k_scr, v_scr, sems):
        i = pl.program_id(0)

        @pl.when(i == 0)
        def _():
            def build_bias():
                keep = _struct_mask()
                for h in range(N_HEADS):
                    bias_ref[h] = jnp.where(keep, _skew_table(gp_ref[h:h + 1, :])[:, :KB], NEG_INF)
            _load_kv(k_hbm, v_hbm, k_scr, v_scr, sems, S, build_bias)

        def step(front):
            start = pl.multiple_of(i * QB, QB)
            kb = k_scr[:, pl.ds(start, KB), :]
            vb = v_scr[:, pl.ds(start, KB), :]
            s = _scores(_scaled(q_ref[...]), kb, bias_ref[...], i, front)
            m = jnp.max(s, axis=-1, keepdims=True)
            e = jnp.exp(s - m)
            l = jnp.sum(e, axis=-1, keepdims=True)
            p = e * (1.0 / l)
            o = _bdot(p.astype(BF), vb, _BATCH_NN)
            lse_ref[...] = jnp.broadcast_to(m + jnp.log(l), (N_HEADS, QB, 128))
            for h in range(N_HEADS):
                o_ref[:, h * HEAD_DIM:(h + 1) * HEAD_DIM] = o[h]

        pl.when(i < KEEP)(functools.partial(step, True))
        pl.when(i >= KEEP)(functools.partial(step, False))

    kv_scr = pltpu.VMEM((N_HEADS, S + PADK, HEAD_DIM), BF)
    return pl.pallas_call(
        body, name="attn_fwd", grid=(S // QB,),
        out_shape=(jax.ShapeDtypeStruct((S, D_A), F32), jax.ShapeDtypeStruct((N_HEADS, S, 128), F32),
                   jax.ShapeDtypeStruct((N_HEADS, QB, KB), F32)),
        in_specs=[pl.BlockSpec((N_HEADS, QB, HEAD_DIM), lambda i: (0, i, 0)),
                  pl.BlockSpec(memory_space=pl.ANY), pl.BlockSpec(memory_space=pl.ANY),
                  pl.BlockSpec((N_HEADS, ROLL_W), lambda i: (0, 0))],
        out_specs=[pl.BlockSpec((QB, D_A), lambda i: (i, 0)),
                   pl.BlockSpec((N_HEADS, QB, 128), lambda i: (0, i, 0)),
                   pl.BlockSpec((N_HEADS, QB, KB), lambda i: (0, 0, 0))],
        scratch_shapes=[kv_scr, kv_scr, pltpu.SemaphoreType.DMA((2,))],
        compiler_params=_params(48, dimension_semantics=("arbitrary",)),
    )(q3, k3, v3, gp)


def _attn_bwd(q3, k3, v3, d_att3, lse, bias, after=()):
    S = q3.shape[1]
    nq = S // QB

    def body(q_ref, do_ref, k_hbm, v_hbm, lse_ref, bias_ref, dq_ref, dk_ref, dv_ref, dgp_ref,
             k_scr, v_scr, dk_acc, dv_acc, dbias_acc, pad_scr, sems):
        i = pl.program_id(0)

        @pl.when(i == 0)
        def _():
            def clear():
                dk_acc[...] = jnp.zeros_like(dk_acc)
                dv_acc[...] = jnp.zeros_like(dv_acc)
                dbias_acc[...] = jnp.zeros_like(dbias_acc)
            _load_kv(k_hbm, v_hbm, k_scr, v_scr, sems, S, clear)

        def step(front):
            start = pl.multiple_of(i * QB, QB)
            kb = k_scr[:, pl.ds(start, KB), :]
            vb = v_scr[:, pl.ds(start, KB), :]
            qs = _scaled(q_ref[...])
            do = do_ref[...]
            p = jnp.exp(_scores(qs, kb, bias_ref[...], i, front) - jnp.tile(lse_ref[...], (1, 1, KB // 128)))
            dp = _bdot(do, vb, _BATCH_NT)
            ds = p * (dp - jnp.sum(dp * p, axis=-1, keepdims=True))
            dbias_acc[...] += ds
            dsb = ds.astype(BF)
            dq = _bdot(dsb, kb, _BATCH_NN) * SCALE
            for h in range(N_HEADS):
                dq_ref[:, h * HEAD_DIM:(h + 1) * HEAD_DIM] = dq[h].astype(BF)
            dk_acc[...] += _bdot(dsb, qs, _BATCH_TN)
            dv_acc[...] += _bdot(p.astype(BF), do, _BATCH_TN)

        pl.when(i < KEEP)(functools.partial(step, True))
        pl.when((i >= KEEP) & (i < nq))(functools.partial(step, False))

        for h in range(N_HEADS):
            hs = slice(h * HEAD_DIM, (h + 1) * HEAD_DIM)
            dk_ref[:, hs] = dk_acc[h, 0:QB, :].astype(BF)
            dv_ref[:, hs] = dv_acc[h, 0:QB, :].astype(BF)
        dk_acc[:, 0:KB - QB, :] = dk_acc[:, QB:KB, :]
        dv_acc[:, 0:KB - QB, :] = dv_acc[:, QB:KB, :]
        dk_acc[:, KB - QB:KB, :] = jnp.zeros((N_HEADS, QB, HEAD_DIM), F32)
        dv_acc[:, KB - QB:KB, :] = jnp.zeros((N_HEADS, QB, HEAD_DIM), F32)

        @pl.when(i == nq + KEEP - 1)
        def _():
            lane = lax.broadcasted_iota(jnp.int32, (1, ROLL_W), 1)
            hi = (lane < 384) | (lane >= 832)
            lo = (lane > 640) & (lane < 832)
            pad_scr[...] = jnp.zeros_like(pad_scr)
            for h in range(N_HEADS):
                pad_scr[:, 0:KB] = dbias_acc[h]
                g = _unskew_sum(pad_scr[...])
                s_hi = jnp.sum(jnp.where(hi, g, 0.0), axis=-1, keepdims=True)
                s_lo = jnp.sum(jnp.where(lo, g, 0.0), axis=-1, keepdims=True)
                g = jnp.where(lane == 384, g + s_hi, g)
                g = jnp.where(lane == 640, g + s_lo, g)
                dgp_ref[h:h + 1, :] = g

    last = nq - 1
    kv_scr = pltpu.VMEM((N_HEADS, S + PADK, HEAD_DIM), BF)
    return pl.pallas_call(
        _after(body, 6, after), name="attn_bwd", grid=(nq + KEEP,),
        out_shape=(jax.ShapeDtypeStruct((S, D_A), BF), jax.ShapeDtypeStruct((S, D_A), BF),
                   jax.ShapeDtypeStruct((S, D_A), BF), jax.ShapeDtypeStruct((N_HEADS, ROLL_W), F32)),
        in_specs=[pl.BlockSpec((N_HEADS, QB, HEAD_DIM), lambda i: (0, jnp.minimum(i, last), 0)),
                  pl.BlockSpec((N_HEADS, QB, HEAD_DIM), lambda i: (0, jnp.minimum(i, last), 0)),
                  pl.BlockSpec(memory_space=pl.ANY), pl.BlockSpec(memory_space=pl.ANY),
                  pl.BlockSpec((N_HEADS, QB, 128), lambda i: (0, jnp.minimum(i, last), 0)),
                  pl.BlockSpec((N_HEADS, QB, KB), lambda i: (0, 0, 0))] + [_ANY] * len(after),
        out_specs=[pl.BlockSpec((QB, D_A), lambda i: (jnp.minimum(i, last), 0)),
                   pl.BlockSpec((QB, D_A), lambda i: (jnp.maximum(i - KEEP, 0), 0)),
                   pl.BlockSpec((QB, D_A), lambda i: (jnp.maximum(i - KEEP, 0), 0)),
                   pl.BlockSpec((N_HEADS, ROLL_W), lambda i: (0, 0))],
        scratch_shapes=[kv_scr, kv_scr,
                        pltpu.VMEM((N_HEADS, KB, HEAD_DIM), F32), pltpu.VMEM((N_HEADS, KB, HEAD_DIM), F32),
                        pltpu.VMEM((N_HEADS, QB, KB), F32), pltpu.VMEM((QB, ROLL_W), F32),
                        pltpu.SemaphoreType.DMA((2,))],
        compiler_params=_params(56, dimension_semantics=("arbitrary",)),
    )(q3, d_att3, k3, v3, lse, bias, *after)


def _sgu_core(ub, vb, lg, lb):
    u, du = _gelu_and_grad(ub)
    v, dv = _gelu_and_grad(vb)
    mu = jnp.mean(v, axis=-1, keepdims=True)
    vc = v - mu
    rstd = lax.rsqrt(jnp.mean(vc * vc, axis=-1, keepdims=True) + EPS)
    xh = vc * rstd
    vn = xh * lg + lb
    return u, du, dv, rstd, xh, vn


def _tri():
    r = lax.broadcasted_iota(jnp.int32, (SGU_CHUNK, SGU_CHUNK), 0)
    c = lax.broadcasted_iota(jnp.int32, (SGU_CHUNK, SGU_CHUNK), 1)
    return r >= c


def _tail_sgu(att, zrest, x, target, w_pa, w_pb, w_out, b_gate, final_g, ln_g, ln_b, w_s, b_s_t, tm=256):
    S = x.shape[0]
    nt = S // tm
    chunks = tm // SGU_CHUNK

    def body(att_ref, ga_ref, ub_ref, vb_ref, gb_ref, gta_ref, gtb_ref, x_ref, t_ref,
             wpa_ref, wpb_ref, wout_ref, bg_ref, fg_ref, lg_ref, lb_ref, ws_ref, bst_ref,
             dout_ref, datt_ref, dzt_ref, dzs_ref, gwout_hbm, gwpa_hbm, gwpb_hbm,
             gbg_ref, gfg_ref, loss_ref, gws_ref, gbs_ref, glg_ref, glb_ref,
             acc_out, acc_pa, acc_pb, sg_scr, mix_scr, dvn_scr, bs_acc, sems):
        i = pl.program_id(0)

        @pl.when(i == 0)
        def _():
            for r in (acc_out, acc_pa, acc_pb, gbg_ref, gfg_ref, loss_ref, gws_ref, glg_ref, glb_ref, bs_acc):
                r[...] = jnp.zeros_like(r)

        u, du, dv, rstd, xh, vn = _sgu_core(ub_ref[...], vb_ref[...], lg_ref[...], lb_ref[...])
        vnb = vn.astype(BF)
        tri = _tri()
        blocks = [(g, slice(n * SGU_CHUNK, (n + 1) * SGU_CHUNK), slice(g * 128, (g + 1) * 128))
                  for g in range(N_GROUPS) for n in range(chunks)]
        wts = [jnp.where(tri, ws_ref[g], 0.0) for g in range(N_GROUPS)]
        for g, rs, cs in blocks:
            mixed = _dot(wts[g].astype(BF), vnb[rs, cs]) + bst_ref[:, g:g + 1]
            mix_scr[rs, cs] = mixed
            sg_scr[rs, cs] = u[rs, cs] * mixed

        att = att_ref[...]
        sg = sg_scr[...]
        sa, dsa = _silu_and_grad(ga_ref[...])
        sb, dsb = _silu_and_grad(gb_ref[...])
        ya = (att * sa).astype(BF)
        yb = (sg * sb).astype(BF)
        pa = _dot(ya, wpa_ref[...])
        pb = _dot(yb, wpb_ref[...])
        ga = _sigmoid(gta_ref[...] + bg_ref[:, 0:D_MODEL])
        gb = _sigmoid(gtb_ref[...] + bg_ref[:, D_MODEL:2 * D_MODEL])
        merged = (ga * pa + gb * pb).astype(BF)
        out = x_ref[...] + _dot(merged, wout_ref[...])
        r2 = lax.rsqrt(jnp.mean(out * out, axis=-1, keepdims=True) + EPS)
        nrm = out * r2
        fg = fg_ref[...]
        err = nrm * fg - t_ref[...]
        loss_ref[...] += 0.5 * jnp.sum(jnp.mean(err * err, axis=-1, keepdims=True))
        dy = err * (1.0 / D_MODEL)
        gfg_ref[...] += jnp.sum(dy * nrm, axis=0, keepdims=True)
        dn = dy * fg
        d_out = r2 * (dn - nrm * jnp.mean(dn * nrm, axis=-1, keepdims=True))
        dout_ref[...] = d_out
        d_outb = d_out.astype(BF)
        acc_out[...] += _dot_tn(merged, d_outb)
        dm = _dot_nt(d_outb, wout_ref[...])
        d_pa = (dm * ga).astype(BF)
        d_pb = (dm * gb).astype(BF)
        d_gta = dm * pa * (ga * (1.0 - ga))
        d_gtb = dm * pb * (gb * (1.0 - gb))
        gbg_ref[:, 0:D_MODEL] += jnp.sum(d_gta, axis=0, keepdims=True)
        gbg_ref[:, D_MODEL:2 * D_MODEL] += jnp.sum(d_gtb, axis=0, keepdims=True)
        dzt_ref[:, 2 * D_A:2 * D_A + D_MODEL] = d_gta.astype(BF)
        dzt_ref[:, 2 * D_A + D_MODEL:] = d_gtb.astype(BF)
        acc_pa[...] += _dot_tn(ya, d_pa)
        acc_pb[...] += _dot_tn(yb, d_pb)
        d_ya = _dot_nt(d_pa, wpa_ref[...])
        d_yb = _dot_nt(d_pb, wpb_ref[...])
        d_att = (d_ya * sa).astype(BF)
        for hd in range(N_HEADS):
            datt_ref[hd] = d_att[:, hd * HEAD_DIM:(hd + 1) * HEAD_DIM]
        dzt_ref[:, 0:D_A] = (d_ya * att * dsa).astype(BF)
        dzt_ref[:, D_A:2 * D_A] = (d_yb * sg * dsb).astype(BF)

        dsg = d_yb * sb
        dzs_ref[:, 0:D_B] = (dsg * mix_scr[...] * du).astype(BF)
        dmix = dsg * u
        for g, rs, cs in blocks:
            dmb = dmix[rs, cs].astype(BF)
            bs_acc[:, cs] += dmix[rs, cs]
            gws_ref[g] += _dot_nt(dmb, vnb[rs, cs])
            dvn_scr[rs, cs] = _dot(wts[g].T.astype(BF), dmb)
        dvn = dvn_scr[...]
        glg_ref[...] += jnp.sum(dvn * xh, axis=0, keepdims=True)
        glb_ref[...] += jnp.sum(dvn, axis=0, keepdims=True)
        dxh = dvn * lg_ref[...]
        dvv = rstd * (dxh - jnp.mean(dxh, axis=-1, keepdims=True)
                      - xh * jnp.mean(dxh * xh, axis=-1, keepdims=True))
        dzs_ref[:, D_B:2 * D_B] = (dvv * dv).astype(BF)

        @pl.when(i == nt - 1)
        def _():
            cps = [pltpu.make_async_copy(acc_out, gwout_hbm, sems.at[0]),
                   pltpu.make_async_copy(acc_pa, gwpa_hbm, sems.at[1]),
                   pltpu.make_async_copy(acc_pb, gwpb_hbm, sems.at[2])]
            for cp in cps:
                cp.start()
            lane = lax.broadcasted_iota(jnp.int32, (SGU_CHUNK, 128), 1)
            cols = jnp.zeros((SGU_CHUNK, 128), F32)
            for g in range(N_GROUPS):
                gws_ref[g] = jnp.where(tri, gws_ref[g], 0.0)
                col = jnp.sum(bs_acc[:, g * 128:(g + 1) * 128], axis=-1, keepdims=True)
                cols = jnp.where(lane == g, col, cols)
            gbs_ref[...] = cols
            for cp in cps:
                cp.wait()

    c2 = lambda i: (0, 0)
    c3 = lambda i: (0, 0, 0)
    zcol = lambda w, blk: pl.BlockSpec((tm, w), lambda i: (i, blk))
    row = lambda w: pl.BlockSpec((tm, w), lambda i: (i, 0))
    return pl.pallas_call(
        body, name="tail", grid=(nt,),
        out_shape=(jax.ShapeDtypeStruct((S, D_MODEL), F32), jax.ShapeDtypeStruct((N_HEADS, S, HEAD_DIM), BF),
                   jax.ShapeDtypeStruct((S, 3072), BF), jax.ShapeDtypeStruct((S, 2 * D_B), BF),
                   jax.ShapeDtypeStruct((D_MODEL, D_MODEL), F32), jax.ShapeDtypeStruct((D_A, D_MODEL), F32),
                   jax.ShapeDtypeStruct((D_B, D_MODEL), F32),
                   jax.ShapeDtypeStruct((1, 2 * D_MODEL), F32), jax.ShapeDtypeStruct((1, D_MODEL), F32),
                   jax.ShapeDtypeStruct((1, 128), F32),
                   jax.ShapeDtypeStruct((N_GROUPS, 128, 128), F32), jax.ShapeDtypeStruct((SGU_CHUNK, 128), F32),
                   jax.ShapeDtypeStruct((1, D_B), F32), jax.ShapeDtypeStruct((1, D_B), F32)),
        in_specs=[row(D_A), zcol(512, 0), zcol(512, 1), zcol(512, 2), zcol(512, 3),
                  zcol(D_MODEL, 2), zcol(D_MODEL, 3), row(D_MODEL), row(D_MODEL),
                  pl.BlockSpec((D_A, D_MODEL), c2), pl.BlockSpec((D_B, D_MODEL), c2),
                  pl.BlockSpec((D_MODEL, D_MODEL), c2),
                  pl.BlockSpec((1, 2 * D_MODEL), c2), pl.BlockSpec((1, D_MODEL), c2),
                  pl.BlockSpec((1, D_B), c2), pl.BlockSpec((1, D_B), c2),
                  pl.BlockSpec((N_GROUPS, 128, 128), c3), pl.BlockSpec((128, N_GROUPS), c2)],
        out_specs=[row(D_MODEL), pl.BlockSpec((N_HEADS, tm, HEAD_DIM), lambda i: (0, i, 0)),
                   row(3072), row(2 * D_B), _ANY, _ANY, _ANY,
                   pl.BlockSpec((1, 2 * D_MODEL), c2), pl.BlockSpec((1, D_MODEL), c2),
                   pl.BlockSpec((1, 128), c2),
                   pl.BlockSpec((N_GROUPS, 128, 128), c3), pl.BlockSpec((SGU_CHUNK, 128), c2),
                   pl.BlockSpec((1, D_B), c2), pl.BlockSpec((1, D_B), c2)],
        scratch_shapes=[pltpu.VMEM((D_MODEL, D_MODEL), F32), pltpu.VMEM((D_A, D_MODEL), F32),
                        pltpu.VMEM((D_B, D_MODEL), F32),
                        pltpu.VMEM((tm, D_B), F32), pltpu.VMEM((tm, D_B), F32), pltpu.VMEM((tm, D_B), F32),
                        pltpu.VMEM((SGU_CHUNK, D_B), F32), pltpu.SemaphoreType.DMA((3,))],
        compiler_params=_params(58, dimension_semantics=("arbitrary",)),
    )(att, zrest, zrest, zrest, zrest, zrest, zrest, x, target, w_pa, w_pb, w_out, b_gate, final_g,
      ln_g, ln_b, w_s, b_s_t)


_DZ_MAP = ((0, 0), (1, 0), (2, 0), (3, 0), (4, 0), (4, 1), (3, 1), (3, 2), (3, 3), (3, 4), (3, 5))


def _dh_gradx(dq, dk, dv, dzt, dzs, w_in_bf, x, norm_g, d_out, tm=512, after=()):
    S = x.shape[0]

    def body(dq_ref, dk_ref, dv_ref, dzt_ref, dzs_ref, w_hbm, x_ref, g_ref, dout_ref, gx_ref, gn_ref, w_scr, sems):
        i = pl.program_id(0)

        def fetch(j):
            cols = pl.ds(j * 512, 512)
            return pltpu.make_async_copy(w_hbm.at[:, cols], w_scr.at[:, cols], sems.at[j])

        @pl.when(i == 0)
        def _():
            for j in range(len(_DZ_MAP)):
                fetch(j).start()
            gn_ref[...] = jnp.zeros_like(gn_ref)

        def step(first):
            pieces = (dq_ref, dk_ref, dv_ref, dzt_ref, dzs_ref)
            dh = jnp.zeros((tm, D_MODEL), F32)
            for j, (pc, blk) in enumerate(_DZ_MAP):
                if first:
                    fetch(j).wait()
                dh += _dot_nt(pieces[pc][:, blk * 512:(blk + 1) * 512], w_scr[:, j * 512:(j + 1) * 512])
            xv = x_ref[...]
            r = lax.rsqrt(jnp.mean(xv * xv, axis=-1, keepdims=True) + EPS)
            nrm = xv * r
            gn_ref[...] += jnp.sum(dh * nrm, axis=0, keepdims=True)
            dn = dh * g_ref[...]
            gx_ref[...] = r * (dn - nrm * jnp.mean(dn * nrm, axis=-1, keepdims=True)) + dout_ref[...]

        pl.when(i == 0)(functools.partial(step, True))
        pl.when(i > 0)(functools.partial(step, False))

    row = lambda w: pl.BlockSpec((tm, w), lambda i: (i, 0))
    c2 = lambda i: (0, 0)
    return pl.pallas_call(
        _after(body, 9, after), name="dh_gradx", grid=(S // tm,),
        out_shape=(jax.ShapeDtypeStruct((S, D_MODEL), F32), jax.ShapeDtypeStruct((1, D_MODEL), F32)),
        in_specs=[row(512), row(512), row(512), row(3072), row(1024), _ANY, row(D_MODEL),
                  pl.BlockSpec((1, D_MODEL), c2), row(D_MODEL)]
        + [_ANY] * len(after),
        out_specs=[row(D_MODEL), pl.BlockSpec((1, D_MODEL), c2)],
        scratch_shapes=[pltpu.VMEM((D_MODEL, D_IN), BF), pltpu.SemaphoreType.DMA((len(_DZ_MAP),))],
        compiler_params=_params(48, dimension_semantics=("arbitrary",)),
    )(dq, dk, dv, dzt, dzs, w_in_bf, x, norm_g, d_out, *after)


def _gw_in(ht, dq, dk, dv, dzt, dzs, tn=512, after=()):
    S = ht.shape[1]
    per = 512 // tn
    cols = tuple((pc, per * blk + h) for pc, blk in _DZ_MAP for h in range(per))

    def body(ht_ref, dq_ref, dk_ref, dv_ref, dzt_ref, dzs_ref, o_ref, ob_ref):
        j = pl.program_id(0)
        pieces = (dq_ref, dk_ref, dv_ref, dzt_ref, dzs_ref)
        for pc in range(5):
            hit = functools.reduce(jnp.logical_or, [j == jj for jj, (p, _) in enumerate(cols) if p == pc])

            @pl.when(hit)
            def _(pc=pc):
                g = _dot(ht_ref[...], pieces[pc][...])
                o_ref[...] = g
                ob_ref[...] = g.astype(BF)

    def piece_spec(pc):
        cur = next(blk for p, blk in cols if p == pc)
        held = []
        for p, blk in cols:
            cur = blk if p == pc else cur
            held.append(cur)

        def index_map(j):
            blk = jnp.int32(held[0])
            for jj in range(1, len(held)):
                if held[jj] != held[jj - 1]:
                    blk = jnp.where(j >= jj, jnp.int32(held[jj]), blk)
            return (0, blk)

        return pl.BlockSpec((S, tn), index_map)

    return pl.pallas_call(
        _after(body, 6, after), name="gw_in", grid=(len(cols),),
        out_shape=(jax.ShapeDtypeStruct((D_MODEL, D_IN), F32), jax.ShapeDtypeStruct((D_MODEL, D_IN), BF)),
        in_specs=[pl.BlockSpec((D_MODEL, S), lambda j: (0, 0), pipeline_mode=pl.Buffered(1))]
        + [piece_spec(pc) for pc in range(5)]
        + [_ANY] * len(after),
        out_specs=[pl.BlockSpec((D_MODEL, tn), lambda j: (0, j)), pl.BlockSpec((D_MODEL, tn), lambda j: (0, j))],
        compiler_params=_params(56, dimension_semantics=("arbitrary",)),
    )(ht, dq, dk, dv, dzt, dzs, *after)


_HBM = pl.BlockSpec(memory_space=pltpu.HBM)
_SEM = pl.BlockSpec(memory_space=pltpu.SEMAPHORE)
_ANY = pl.BlockSpec(memory_space=pl.ANY)
_EFFECT = pltpu.SideEffectType.DATAFLOW_SIDE_EFFECTING


def _in_hbm(a):
    return pltpu.with_memory_space_constraint(a, pltpu.HBM)


def _after(body, n_in, after):
    if not after:
        return body
    return lambda *refs: body(*refs[:n_in], *refs[n_in + len(after):])


class _Started:
    def __init__(self, send, recv, bufs, token):
        self.send, self.recv, self.bufs, self.token = send, recv, bufs, token


def _split_start(name, bufs, n_copies, copies, after=()):
    nb = len(bufs)

    def body(*refs):
        refs = refs[:nb] + refs[nb + len(after):]
        for cp in copies(refs[:nb], refs[nb], refs[nb + 1]):
            cp.start()
        refs[-1][...] = jnp.zeros_like(refs[-1])

    outs = pl.pallas_call(
        body, name=name,
        out_shape=(pltpu.SemaphoreType.DMA((n_copies,)), pltpu.SemaphoreType.DMA((n_copies,)),
                   *[pltpu.HBM(b.shape, b.dtype) for b in bufs], jax.ShapeDtypeStruct((8, 128), F32)),
        in_specs=[_HBM] * nb + [_ANY] * len(after),
        out_specs=(_SEM, _SEM, *[_HBM] * nb, pl.BlockSpec(memory_space=pltpu.VMEM)),
        input_output_aliases={k: 2 + k for k in range(nb)},
        compiler_params=_params(1, has_side_effects=_EFFECT),
    )(*[_in_hbm(b) for b in bufs], *after)
    return _Started(outs[0], outs[1], list(outs[2:2 + nb]), outs[-1])


def _split_wait(name, started, copies, after):
    nb = len(started.bufs)

    def body(*refs):
        for cp in copies(refs[:nb], refs[nb], refs[nb + 1]):
            cp.wait_send()
            cp.wait_recv()

    return list(pl.pallas_call(
        body, name=name,
        out_shape=tuple(pltpu.HBM(b.shape, b.dtype) for b in started.bufs),
        in_specs=[_HBM] * nb + [_SEM, _SEM, _ANY],
        out_specs=tuple([_HBM] * nb),
        input_output_aliases={k: k for k in range(nb)},
        compiler_params=_params(1, has_side_effects=_EFFECT),
    )(*started.bufs, started.send, started.recv, after))


def _x1_copies(ws):
    def copies(refs, send_sems, recv_sems):
        x, y, c, _ = _mesh_pos()
        out = []
        for k, w in enumerate(ws):
            for s in range(N_SHARD):
                out.append(pltpu.make_async_remote_copy(
                    src_ref=_UNITS[w](refs[k], s, 1 - c), dst_ref=refs[len(ws) + k].at[s],
                    send_sem=send_sems.at[N_SHARD * k + s], recv_sem=recv_sems.at[N_SHARD * k + s],
                    device_id=(x, y, 1 - c), device_id_type=MESH))
        return out
    return copies


def _x2_copies(n):
    def copies(refs, send_sems, recv_sems):
        x, y, c, chips = _mesh_pos()
        out = []
        for j, (cx, cy) in enumerate(chips):
            for k in range(n):
                out.append(pltpu.make_async_remote_copy(
                    src_ref=refs[k].at[2 * cx + cy], dst_ref=refs[n + k].at[j],
                    send_sem=send_sems.at[3 * k + j], recv_sem=recv_sems.at[3 * k + j],
                    device_id=(cx, cy, c), device_id_type=MESH))
        return out
    return copies


def _x3_copies(ws):
    def copies(refs, send_sems, recv_sems):
        x, y, c, _ = _mesh_pos()
        out = []
        for k, w in enumerate(ws):
            rows = _HALF_ROWS[w]
            mine = refs[k].at[pl.ds(_mo(c * rows, rows), rows), :]
            out.append(pltpu.make_async_remote_copy(
                src_ref=mine, dst_ref=mine, send_sem=send_sems.at[k], recv_sem=recv_sems.at[k],
                device_id=(x, y, 1 - c), device_id_type=MESH))
        return out
    return copies


def _x1_lands(ws, dtype=F32):
    return [lax.empty((N_SHARD,) + _UNIT_SHAPES[w], dtype) for w in ws]


def _x2_lands(ws):
    return [lax.empty((3,) + _UNIT_SHAPES[w], BF) for w in ws]


def _grad_add1(w, g, recv, pos):
    ur, uc = _UNIT_SHAPES[w]

    def body(pos_ref, g_ref, r_ref, own_ref, csb_ref):
        v = g_ref[...] + r_ref[0].astype(F32)
        csb_ref[0] = v.astype(BF)

        @pl.when(pl.program_id(0) == pos_ref[1])
        def _():
            own_ref[...] = v

    u3 = lambda s, pos: (s, 0, 0)
    return pl.pallas_call(
        body, name=f"grad_add1_{w}",
        grid_spec=pltpu.PrefetchScalarGridSpec(
            num_scalar_prefetch=1, grid=(N_SHARD,),
            in_specs=[pl.BlockSpec((ur, uc), lambda s, pos: (pos[0], s)), pl.BlockSpec((1, ur, uc), u3)],
            out_specs=[pl.BlockSpec((ur, uc), lambda s, pos: (0, 0)), pl.BlockSpec((1, ur, uc), u3)]),
        out_shape=(jax.ShapeDtypeStruct((ur, uc), F32), jax.ShapeDtypeStruct((N_SHARD, ur, uc), BF)),
        compiler_params=_params(40, dimension_semantics=("arbitrary",)),
    )(pos, g, recv)


def _grad_add1_group(ws, gs, recvs):
    n = len(ws)

    def body(*refs):
        c = lax.axis_index("c")
        for k, w in enumerate(ws):
            g, r, cs, csb = refs[k], refs[n + k], refs[2 * n + k], refs[3 * n + k]
            for s in range(N_SHARD):
                v = _UNITS[w](g, s, c)[...] + r[s]
                cs[s] = v
                csb[s] = v.astype(BF)

    vm = pl.BlockSpec(memory_space=pltpu.VMEM)
    outs = pl.pallas_call(
        body, name="grad_add1_group",
        out_shape=tuple(jax.ShapeDtypeStruct((N_SHARD,) + _UNIT_SHAPES[w], dt) for dt in (F32, BF) for w in ws),
        in_specs=[vm] * (2 * n), out_specs=[vm] * (2 * n),
        compiler_params=_params(32),
    )(*gs, *recvs)
    return list(outs[:n]), list(outs[n:])


def _grad_add2_group(ws, css, recvs):
    n = len(ws)

    def body(*refs):
        x, y, c, _ = _mesh_pos()
        for k, w in enumerate(ws):
            cs, r, o = refs[k], refs[n + k], refs[2 * n + k]
            rows = _HALF_ROWS[w]
            total = ((cs[2 * x + y] + r[0].astype(F32)) + r[1].astype(F32)) + r[2].astype(F32)
            o[pl.ds(_mo(c * rows, rows), rows), :] = total

    vm = pl.BlockSpec(memory_space=pltpu.VMEM)
    return list(pl.pallas_call(
        body, name="grad_add2_group",
        out_shape=tuple(jax.ShapeDtypeStruct(_SHARD_SHAPES[w], F32) for w in ws),
        in_specs=[vm] * (2 * n), out_specs=[vm] * n,
        compiler_params=_params(32),
    )(*css, *recvs))


def _grad_add2(w, own, recv, pos):
    ur, uc = _UNIT_SHAPES[w]
    nt = 4
    tr = ur // nt

    def body(pos_ref, own_ref, r_ref, o_ref):
        o_ref[...] = ((own_ref[...] + r_ref[0].astype(F32)) + r_ref[1].astype(F32)) + r_ref[2].astype(F32)

    return pl.pallas_call(
        body, name=f"grad_add2_{w}",
        grid_spec=pltpu.PrefetchScalarGridSpec(
            num_scalar_prefetch=1, grid=(nt,),
            in_specs=[pl.BlockSpec((tr, uc), lambda t, pos: (t, 0)),
                      pl.BlockSpec((3, tr, uc), lambda t, pos: (0, t, 0))],
            out_specs=pl.BlockSpec((tr, uc), lambda t, pos: (pos[0] * nt + t, 0))),
        out_shape=jax.ShapeDtypeStruct(_SHARD_SHAPES[w], F32),
        compiler_params=_params(32, dimension_semantics=("arbitrary",)),
    )(pos, own, recv)


def _adamw_math(w, g, m, v):
    m = ADAM_B1 * m + (1.0 - ADAM_B1) * g
    v = ADAM_B2 * v + (1.0 - ADAM_B2) * (g * g)
    m_hat = m / ADAM_C1
    v_hat = v / ADAM_C2
    delta = -ADAM_LR * (m_hat / (jnp.sqrt(v_hat) + ADAM_EPS) + ADAM_WD * w)
    return delta, m, v


def _adamw_group(ws_, gs, ms, vs, after=()):
    n = len(ws_)

    def body(*refs):
        for k in range(n):
            w, g, m, v = (refs[j * n + k] for j in range(4))
            d, nm, nv, gc = (refs[(4 + j) * n + k] for j in range(4))
            gv = g[...]
            d[...], nm[...], nv[...] = _adamw_math(w[...], gv, m[...], v[...])
            gc[...] = gv

    vm = pl.BlockSpec(memory_space=pltpu.VMEM)
    outs = pl.pallas_call(
        _after(body, 4 * n, after), name="adamw_group",
        out_shape=tuple(jax.ShapeDtypeStruct(a.shape, F32) for _ in range(4) for a in ws_),
        in_specs=[vm] * (4 * n) + [_ANY] * len(after), out_specs=[vm] * (4 * n),
        compiler_params=_params(32),
    )(*ws_, *gs, *ms, *vs, *after)
    return [tuple(outs[j * n + k] for j in range(4)) for k in range(n)]


def _adamw(name, w, g, m, v, tr=256, after=()):
    rows, cols = w.shape

    def body(w_ref, g_ref, m_ref, v_ref, d_ref, nm_ref, nv_ref, gc_ref):
        gv = g_ref[...]
        d_ref[...], nm_ref[...], nv_ref[...] = _adamw_math(w_ref[...], gv, m_ref[...], v_ref[...])
        gc_ref[...] = gv

    spec = pl.BlockSpec((tr, cols), lambda i: (i, 0))
    return pl.pallas_call(
        _after(body, 4, after), name=name, grid=(rows // tr,),
        out_shape=tuple(jax.ShapeDtypeStruct((rows, cols), F32) for _ in range(4)),
        in_specs=[spec] * 4 + [_ANY] * len(after), out_specs=[spec] * 4,
        compiler_params=_params(32, dimension_semantics=("arbitrary",)),
    )(w, g, m, v, *after)


_REL_PAD = 384
_VEC_FIELDS = (("norm_g", 0, D_MODEL), ("b_gate", 1024, 2 * D_MODEL), ("sgu_ln_g", 3072, D_B),
               ("sgu_ln_b", 3584, D_B), ("b_s", 4096, N_GROUPS * 128), ("final_g", 4608, D_MODEL))
_LOSS_OFF = 5632
_REL_OFF = 5760
_NV = _REL_OFF + N_HEADS * _REL_PAD
_N_FIELDS = len(_VEC_FIELDS) + 2


_B_S_FIELD = [f[0] for f in _VEC_FIELDS].index("b_s")


def _assemble_row(dst, fields, transposed_b_s):
    for f, (_, off, n) in enumerate(_VEC_FIELDS):
        if transposed_b_s and f == _B_S_FIELD:
            t = fields[f][...].T
            for g in range(N_GROUPS):
                dst[:, off + 128 * g:off + 128 * (g + 1)] = t[g:g + 1, :]
        else:
            dst[:, off:off + n] = fields[f][...]
    for r in range(N_HEADS):
        dst[:, _REL_OFF + _REL_PAD * r:_REL_OFF + _REL_PAD * (r + 1)] = fields[len(_VEC_FIELDS)][r:r + 1, :]


def _small_reduce(grads, loss_row, after=()):
    n_in = _N_FIELDS + 1

    def body(*refs):
        g_refs, loss_ref = refs[:_N_FIELDS], refs[_N_FIELDS]
        out_v, out_w = refs[n_in:n_in + 2]
        mine_v, gath_v, gath_w, send_sems, recv_sems = refs[n_in + 2:]
        x, y, c, chips = _mesh_pos()
        me, sibling = (x, y, c), (x, y, 1 - c)

        _assemble_row(mine_v, g_refs, True)
        mine_v[:, _LOSS_OFF:_LOSS_OFF + 128] = loss_ref[...]
        mine_w = g_refs[-1]
        my_k = 4 * x + 2 * y + c
        gath_v[my_k] = mine_v[...]
        gath_w[my_k] = mine_w[...]

        def copy(k, gath, block, to, src=None):
            dst = gath.at[4 * block[0] + 2 * block[1] + block[2]]
            return pltpu.make_async_remote_copy(
                src_ref=dst if src is None else src, dst_ref=dst,
                send_sem=send_sems.at[k], recv_sem=recv_sems.at[k], device_id=to, device_id_type=MESH)

        bufs = ((gath_v, mine_v), (gath_w, mine_w))
        first, passed = [], []
        for b, (gath, mine) in enumerate(bufs):
            first.append(copy(7 * b, gath, me, sibling, src=mine))
            first += [copy(7 * b + 1 + j, gath, me, (*chip, c), src=mine) for j, chip in enumerate(chips)]
        for cp in first:
            cp.start()
        for b, (gath, _) in enumerate(bufs):
            for j, chip in enumerate(chips):
                copy(7 * b + 1 + j, gath, (*chip, c), me).wait_recv()
                cp = copy(7 * b + 4 + j, gath, (*chip, c), sibling)
                cp.start()
                passed.append(cp)
        for b, (gath, _) in enumerate(bufs):
            copy(7 * b, gath, sibling, me).wait_recv()
            for j, chip in enumerate(chips):
                copy(7 * b + 4 + j, gath, (*chip, 1 - c), me).wait_recv()
        for cp in first + passed:
            cp.wait_send()

        tot_v, tot_w = gath_v[0], gath_w[0]
        for k in range(1, 8):
            tot_v = tot_v + gath_v[k]
            tot_w = tot_w + gath_w[k]
        out_v[...] = tot_v
        out_w[...] = tot_w

    vm = pl.BlockSpec(memory_space=pltpu.VMEM)
    return pl.pallas_call(
        _after(body, n_in, after), name="small_reduce",
        out_shape=(jax.ShapeDtypeStruct((1, _NV), F32), jax.ShapeDtypeStruct((N_GROUPS * 128, 128), F32)),
        in_specs=[vm] * n_in + [_ANY] * len(after), out_specs=[vm] * 2,
        scratch_shapes=[pltpu.VMEM((1, _NV), F32), pltpu.VMEM((8, 1, _NV), F32),
                        pltpu.VMEM((8, N_GROUPS * 128, 128), F32),
                        pltpu.SemaphoreType.DMA((14,)), pltpu.SemaphoreType.DMA((14,))],
        compiler_params=_params(32),
    )(*grads, loss_row, *after)


def _small_adamw(tot_v, tot_w, params):
    n_in = 2 + 3 * _N_FIELDS

    def body(*refs):
        tv_ref, tw_ref = refs[:2]
        p_refs = [refs[2 + k * _N_FIELDS:2 + (k + 1) * _N_FIELDS] for k in range(3)]
        outs = refs[n_in:n_in + 4 * _N_FIELDS + 1]
        wmv = refs[-1]
        for k in range(3):
            _assemble_row(wmv.at[k], p_refs[k], False)
            wmv[k, :, _LOSS_OFF:_LOSS_OFF + 128] = jnp.zeros((1, 128), F32)
        tot_v, tot_w = tv_ref[...], tw_ref[...]
        res_v = (tot_v,) + _adamw_math(wmv[0], tot_v, wmv[1], wmv[2])
        res_w = (tot_w,) + _adamw_math(p_refs[0][-1][...], tot_w, p_refs[1][-1][...], p_refs[2][-1][...])
        for kind in range(4):
            o = outs[kind * _N_FIELDS:(kind + 1) * _N_FIELDS]
            for f, (_, off, n) in enumerate(_VEC_FIELDS):
                o[f][...] = res_v[kind][:, off:off + n]
            for r in range(N_HEADS):
                o[len(_VEC_FIELDS)][r:r + 1, :] = res_v[kind][:, _REL_OFF + _REL_PAD * r:_REL_OFF + _REL_PAD * (r + 1)]
            o[-1][...] = res_w[kind]
        outs[-1][...] = tot_v[:, _LOSS_OFF:_LOSS_OFF + 128]

    field_shapes = [(1, n) for _, _, n in _VEC_FIELDS] + [(N_HEADS, _REL_PAD), (N_GROUPS * 128, 128)]
    vm = pl.BlockSpec(memory_space=pltpu.VMEM)
    operands = [tot_v, tot_w] + [a for p in params for a in p]
    assert len(operands) == n_in
    outs = pl.pallas_call(
        body, name="small_adamw",
        out_shape=tuple(jax.ShapeDtypeStruct(s, F32) for _ in range(4) for s in field_shapes)
        + (jax.ShapeDtypeStruct((1, 128), F32),),
        in_specs=[vm] * n_in, out_specs=[vm] * (4 * _N_FIELDS + 1),
        scratch_shapes=[pltpu.VMEM((3, 1, _NV), F32)],
        compiler_params=_params(32),
    )(*operands)
    return [outs[k * _N_FIELDS:(k + 1) * _N_FIELDS] for k in range(4)], outs[-1]


def _small_fields(norm_g, b_gate, ln_g, ln_b, b_s, final_g, rel_bias, w_s):
    rel = jnp.pad(rel_bias.reshape(N_HEADS, N_REL), ((0, 0), (0, _REL_PAD - N_REL)))
    return (norm_g, b_gate, ln_g, ln_b, b_s.reshape(1, N_GROUPS * 128), final_g.reshape(1, D_MODEL),
            rel, w_s.reshape(N_GROUPS * 128, 128))


def _small_outputs(fields):
    n_g, b_g, l_g, l_b, b_s, f_g, rel, w_s = fields
    return (n_g, b_g, rel[:, :N_REL].reshape(1, N_HEADS, N_REL), l_g, l_b,
            w_s.reshape(1, N_GROUPS, 128, 128), b_s.reshape(1, N_GROUPS, 128), f_g.reshape(D_MODEL))


def _bias_row(rel_bias):
    hi = rel_bias[:, N_REL - 1:N_REL]
    lo = rel_bias[:, 0:1]
    return jnp.concatenate([jnp.broadcast_to(hi, (N_HEADS, 384)), rel_bias[:, ::-1],
                            jnp.broadcast_to(lo, (N_HEADS, 191)), jnp.broadcast_to(hi, (N_HEADS, 192))], axis=1)


def kernel(x, norm_g, w_in, b_gate, rel_bias, sgu_ln_g, sgu_ln_b, w_s, b_s, w_pa, w_pb, w_out, final_g, loss_target, m_norm_g, m_w_in, m_b_gate, m_rel_bias, m_sgu_ln_g, m_sgu_ln_b, m_w_s, m_b_s, m_w_pa, m_w_pb, m_w_out, m_final_g, v_norm_g, v_w_in, v_b_gate, v_rel_bias, v_sgu_ln_g, v_sgu_ln_b, v_w_s, v_b_s, v_w_pa, v_w_pb, v_w_out, v_final_g):
    S = x.shape[1]
    xs = x.reshape(S, D_MODEL)
    tgt = loss_target.reshape(S, D_MODEL)
    big_w = (w_in[0], w_pa[0], w_pb[0], w_out[0])
    big_m = (m_w_in[0], m_w_pa[0], m_w_pb[0], m_w_out[0])
    big_v = (v_w_in[0], v_w_pa[0], v_w_pb[0], v_w_out[0])
    rel = rel_bias[0]
    ws = w_s[0]
    bst = b_s[0].T
    fg = final_g.reshape(1, D_MODEL)
    pos = jnp.stack([lax.axis_index("c"), 2 * lax.axis_index("x") + lax.axis_index("y")]).astype(jnp.int32)

    staged = _stage_weights((1, 2, 3), big_w[1:], pos)
    w_in_bf, = _ag_weights((0,), big_w[:1])
    ag_s = _split_start("ag_small_start", staged, 9, _gather_copies((1, 2, 3)), after=(w_in_bf,))

    ht, q3, k3, v3, zrest = _inproj_fwd(xs, norm_g, w_in_bf, after=(ag_s.token,))
    gp = _bias_row(rel)
    att, lse, band_bias = _attn_fwd(q3, k3, v3, gp)
    w_pa_bf, w_pb_bf, w_out_bf = _split_wait("ag_small_wait", ag_s, _gather_copies((1, 2, 3)), att)
    (d_out, d_att, dzt, dzs, gw_out, gw_pa, gw_pb, g_bgate, g_final, loss_row,
     g_ws, g_bs_t, g_lng, g_lnb) = _tail_sgu(
        att, zrest, xs, tgt, w_pa_bf, w_pb_bf, w_out_bf, b_gate, fg, sgu_ln_g, sgu_ln_b, ws, bst)
    ws_s, ws_i = (1, 2, 3), (0,)

    x1s = _split_start("gx1s_start", [gw_pa, gw_pb, gw_out] + _x1_lands(ws_s), 12, _x1_copies(ws_s))
    dq, dk, dv, d_gp = _attn_bwd(q3, k3, v3, d_att, lse, band_bias, after=(x1s.token,))
    got = _split_wait("gx1s_wait", x1s, _x1_copies(ws_s), dq)
    cs_s, csb_s = _grad_add1_group(ws_s, got[:3], got[3:])

    x2s = _split_start("gx2s_start", csb_s + _x2_lands(ws_s), 9, _x2_copies(3))
    gw_in, gw_in_bf = _gw_in(ht, dq, dk, dv, dzt, dzs, after=(x2s.token,))
    x1i = _split_start("gx1i_start", [gw_in_bf] + _x1_lands(ws_i, BF), 4, _x1_copies(ws_i))
    got = _split_wait("gx2s_wait", x2s, _x2_copies(3), x1i.token)
    halves_s = _grad_add2_group(ws_s, cs_s, got[3:])
    x3s = _split_start("gx3s_start", halves_s, 3, _x3_copies(ws_s))
    got = _split_wait("gx1i_wait", x1i, _x1_copies(ws_i), x3s.token)
    sum_i = _grad_add1(0, gw_in, got[1], pos)

    x2i = _split_start("gx2i_start", [sum_i[1]] + _x2_lands(ws_i), 3, _x2_copies(1))
    grad_x, g_norm = _dh_gradx(dq, dk, dv, dzt, dzs, w_in_bf, xs, norm_g, d_out, after=(x2i.token,))
    g_shards_s = _split_wait("gx3s_wait", x3s, _x3_copies(ws_s), grad_x)
    got = _split_wait("gx2i_wait", x2i, _x2_copies(1), grad_x)
    half_i = _grad_add2(0, sum_i[0], got[1], pos)
    x3i = _split_start("gx3i_start", [half_i], 1, _x3_copies(ws_i))
    big = [None] * 4
    big[1:] = _adamw_group(big_w[1:], g_shards_s, big_m[1:], big_v[1:], after=(x3i.token,))

    g_rel = jnp.pad(d_gp[:, 384:384 + N_REL][:, ::-1], ((0, 0), (0, _REL_PAD - N_REL)))
    small_grads = (g_norm, g_bgate, g_lng, g_lnb, g_bs_t, g_final, g_rel, g_ws.reshape(N_GROUPS * 128, 128))
    small_params = (_small_fields(norm_g, b_gate, sgu_ln_g, sgu_ln_b, b_s, final_g, rel_bias, w_s),
                    _small_fields(m_norm_g, m_b_gate, m_sgu_ln_g, m_sgu_ln_b, m_b_s, m_final_g, m_rel_bias, m_w_s),
                    _small_fields(v_norm_g, v_b_gate, v_sgu_ln_g, v_sgu_ln_b, v_b_s, v_final_g, v_rel_bias, v_w_s))
    tot_v, tot_w = _small_reduce(small_grads, loss_row, after=(x3i.token,))
    (gsum, sdelta, sm, sv), loss_out = _small_adamw(tot_v, tot_w, small_params)

    g_shard_i, = _split_wait("gx3i_wait", x3i, _x3_copies(ws_i), loss_out)
    big[0] = _adamw("adamw_w_in", big_w[0], g_shard_i, big_m[0], big_v[0])
    sg_out, sd_out, sm_out, sv_out = (_small_outputs(f) for f in (gsum, sdelta, sm, sv))
    loss = loss_out[0, 0]

    def assemble(small, bigs):
        n_g, b_g, r_b, l_g, l_b, w_s_, b_s_, f_g = small
        b_in, b_pa, b_pb, b_out = (b[None] for b in bigs)
        return (n_g, b_in, b_g, r_b, l_g, l_b, w_s_, b_s_, b_pa, b_pb, b_out, f_g)

    grads_out = assemble(sg_out, [b[3] for b in big])
    delta_out = assemble(sd_out, [b[0] for b in big])
    m_out = assemble(sm_out, [b[1] for b in big])
    v_out = assemble(sv_out, [b[2] for b in big])
    return (loss, grad_x.reshape(1, S, D_MODEL), *grads_out, *delta_out, *m_out, *v_out)
```

```python
import functools
import math

import jax
import jax.numpy as jnp
from jax import lax
from jax.experimental import pallas as pl
from jax.experimental.pallas import tpu as pltpu

F32 = jnp.float32
BF = jnp.bfloat16
MESH = pl.DeviceIdType.MESH

D_MODEL = 1024
D_A = 512
D_B = 512
D_IN = 5632
N_HEADS = 8
HEAD_DIM = 64
CHUNK = 64
N_PREV = 8
SGU_CHUNK = 128
N_GROUPS = 4
N_REL = 257
EPS = 1e-6
NEG_INF = -1e30
SCALE = HEAD_DIM ** -0.5

QB = 2 * CHUNK
KB = (N_PREV + 2) * CHUNK
PADK = N_PREV * CHUNK
ROLL_W = 1024
N_RING = KB // QB
KEEP = N_RING - 1

ADAM_LR = 0.001
ADAM_B1 = 0.9
ADAM_B2 = 0.999
ADAM_EPS = 1e-08
ADAM_WD = 0.01
ADAM_STEP = 10
ADAM_C1 = 1.0 - ADAM_B1 ** ADAM_STEP
ADAM_C2 = 1.0 - ADAM_B2 ** ADAM_STEP

N_SHARD = 4
SHARD_IN = D_IN // N_SHARD
MIB = 1024 * 1024


VMEM_RESERVE_MIB = 60


def _params(vmem_mib, **kw):
    assert vmem_mib <= VMEM_RESERVE_MIB
    return pltpu.CompilerParams(vmem_limit_bytes=VMEM_RESERVE_MIB * MIB, **kw)


def _sigmoid(x):
    return 1.0 / (1.0 + jnp.exp(-x))


def _silu_and_grad(x):
    s = _sigmoid(x)
    return x * s, s * (1.0 + x * (1.0 - s))


_GELU_C = math.sqrt(2.0 / math.pi)
_GELU_A = 0.044715


def _gelu_and_grad(x):
    x2 = x * x
    t = jnp.tanh(_GELU_C * (x + _GELU_A * (x2 * x)))
    cdf = 0.5 * (1.0 + t)
    grad = cdf + 0.5 * x * (1.0 - t * t) * (_GELU_C * (1.0 + 3.0 * _GELU_A * x2))
    return x * cdf, grad


def _dot(a, b):
    return jnp.dot(a, b, preferred_element_type=F32)


def _dot_nt(a, b):
    return lax.dot_general(a, b, (((1,), (1,)), ((), ())), preferred_element_type=F32)


def _dot_tn(a, b):
    return lax.dot_general(a, b, (((0,), (0,)), ((), ())), preferred_element_type=F32)


def _mo(v, m):
    return v if isinstance(v, int) else pl.multiple_of(v, m)


def _unit_in(ref, s, p):
    return ref.at[pl.ds(_mo(p * 512, 512), 512), pl.ds(_mo(s * SHARD_IN, 128), SHARD_IN)]


def _unit_p(ref, s, p):
    return ref.at[pl.ds(_mo(p * 256, 256), 256), pl.ds(_mo(s * 256, 128), 256)]


def _unit_out(ref, s, p):
    return ref.at[pl.ds(_mo(s * 256 + p * 128, 128), 128), :]


_UNITS = (_unit_in, _unit_p, _unit_p, _unit_out)
_HALF_ROWS = (512, 256, 256, 128)
_UNIT_SHAPES = ((512, SHARD_IN), (256, 256), (256, 256), (128, D_MODEL))
_FULL_SHAPES = ((D_MODEL, D_IN), (D_A, D_MODEL), (D_B, D_MODEL), (D_MODEL, D_MODEL))
_SHARD_SHAPES = ((D_MODEL, SHARD_IN), (D_A, 256), (D_B, 256), (256, D_MODEL))


def _mesh_pos():
    x, y, c = lax.axis_index("x"), lax.axis_index("y"), lax.axis_index("c")
    chips = [(1 - x, y), (x, 1 - y), (1 - x, 1 - y)]
    return x, y, c, chips


def _ag_weights(ws, shards, later_ws, later_shards, gp):
    n, m = len(ws), len(later_ws)

    def body(*refs):
        ins, later_ins, gp_ref = refs[:n], refs[n:n + m], refs[n + m]
        o = n + m + 1
        outs, later_outs, bias_ref = refs[o:o + n], refs[o + n:o + n + m], refs[o + n + m]
        o += n + m + 1
        stage, later_stage = refs[o:o + n], refs[o + n:o + n + m]
        send_sems, recv_sems, local_sems, later_sems = refs[o + n + m:]
        x, y, c, chips = _mesh_pos()
        s_me = 2 * x + y
        sibling = (x, y, 1 - c)
        def rows_of(k, p):
            rows = _HALF_ROWS[ws[k]]
            return pl.ds(_mo(p * rows, rows), rows)

        def half(k, p):
            return stage[k].at[rows_of(k, p), :]

        def unit(k, s, p):
            return _UNITS[ws[k]](outs[k], s, p)

        def rcopy(k, i, src, dst, to):
            return pltpu.make_async_remote_copy(src_ref=src, dst_ref=dst, send_sem=send_sems.at[k, i],
                                                recv_sem=recv_sems.at[k, i], device_id=to, device_id_type=MESH)

        for k in range(n):
            stage[k][rows_of(k, c), :] = ins[k][rows_of(k, c), :].astype(BF)
        sends = []
        for j, (cx, cy) in enumerate(chips):
            for k in range(n):
                cp = rcopy(k, j, half(k, c), unit(k, s_me, c), (cx, cy, c))
                cp.start()
                sends.append(cp)
        for k in range(n):
            stage[k][rows_of(k, 1 - c), :] = ins[k][rows_of(k, 1 - c), :].astype(BF)
        local = []
        for k in range(n):
            for p in range(2):
                cp = pltpu.make_async_copy(half(k, p), unit(k, s_me, p), local_sems.at[k, p])
                cp.start()
                local.append(cp)
        for k, w in enumerate(later_ws):
            later_stage[k][...] = later_ins[k][...].astype(BF)
            cp = pltpu.make_async_copy(later_stage[k], _shard_of(later_outs[k], w, s_me), later_sems.at[k])
            cp.start()
            local.append(cp)
        keep = _struct_mask()
        for h in range(N_HEADS):
            bias_ref[h] = jnp.where(keep, _skew_table(gp_ref[h:h + 1, :])[:, :KB], NEG_INF)
        for j, (cx, cy) in enumerate(chips):
            for k in range(n):
                landed = unit(k, 2 * cx + cy, c)
                rcopy(k, j, landed, landed, (cx, cy, c)).wait_recv()
                cp = rcopy(k, 3 + j, landed, landed, sibling)
                cp.start()
                sends.append(cp)
        for j, (cx, cy) in enumerate(chips):
            for k in range(n):
                other = unit(k, 2 * cx + cy, 1 - c)
                rcopy(k, 3 + j, other, other, sibling).wait_recv()
        for cp in sends:
            cp.wait_send()
        for cp in local:
            cp.wait()

    vm = pl.BlockSpec(memory_space=pltpu.VMEM)
    outs = pl.pallas_call(
        body, name="ag_weights",
        out_shape=tuple(jax.ShapeDtypeStruct(_FULL_SHAPES[w], BF) for w in tuple(ws) + tuple(later_ws))
        + (jax.ShapeDtypeStruct((N_HEADS, QB, KB), F32),),
        in_specs=[vm] * (n + m + 1), out_specs=[_ANY] * (n + m) + [vm],
        scratch_shapes=[pltpu.VMEM(_SHARD_SHAPES[w], BF) for w in tuple(ws) + tuple(later_ws)]
        + [pltpu.SemaphoreType.DMA((n, 6)), pltpu.SemaphoreType.DMA((n, 6)), pltpu.SemaphoreType.DMA((n, 2)),
           pltpu.SemaphoreType.DMA((m,))],
        compiler_params=_params(48),
    )(*shards, *later_shards, gp)
    return list(outs[:n]), list(outs[n:n + m]), outs[-1]


def _shard_of(ref, w, s):
    if w == 0:
        return ref.at[:, pl.ds(_mo(s * SHARD_IN, 128), SHARD_IN)]
    if w == 3:
        return ref.at[pl.ds(_mo(s * 256, 256), 256), :]
    return ref.at[:, pl.ds(_mo(s * 256, 128), 256)]


def _gather_copies(ws):
    def copies(refs, send_sems, recv_sems):
        x, y, c, chips = _mesh_pos()
        out = []
        for j, (cx, cy) in enumerate(chips):
            for k, w in enumerate(ws):
                mine = _shard_of(refs[k], w, 2 * x + y)
                out.append(pltpu.make_async_remote_copy(
                    src_ref=mine, dst_ref=mine, send_sem=send_sems.at[3 * k + j], recv_sem=recv_sems.at[3 * k + j],
                    device_id=(cx, cy, c), device_id_type=MESH))
        return out
    return copies


def _inproj_fwd(x, norm_g, w_in_bf, tm=512, after=()):
    S = x.shape[0]

    def body(x_ref, g_ref, w_ref, ht_ref, q_ref, k_ref, v_ref, zr_ref):
        xv = x_ref[...]
        r = lax.rsqrt(jnp.mean(xv * xv, axis=-1, keepdims=True) + EPS)
        hf = (xv * r) * g_ref[...]
        ht_ref[...] = hf.T.astype(BF)
        h = hf.astype(BF)
        heads = (q_ref, k_ref, v_ref)
        for j in range(D_IN // 512):
            z = _dot(h, w_ref[:, j * 512:(j + 1) * 512])
            if j < 3:
                zb = z.astype(BF)
                for hd in range(N_HEADS):
                    heads[j][hd] = zb[:, hd * HEAD_DIM:(hd + 1) * HEAD_DIM]
            else:
                zr_ref[:, (j - 3) * 512:(j - 2) * 512] = z

    head_major = jax.ShapeDtypeStruct((N_HEADS, S, HEAD_DIM), BF)
    head_spec = pl.BlockSpec((N_HEADS, tm, HEAD_DIM), lambda i: (0, i, 0))
    return pl.pallas_call(
        _after(body, 3, after), name="inproj_fwd", grid=(S // tm,),
        out_shape=(jax.ShapeDtypeStruct((D_MODEL, S), BF), head_major, head_major, head_major,
                   jax.ShapeDtypeStruct((S, D_IN - 3 * D_A), F32)),
        in_specs=[pl.BlockSpec((tm, D_MODEL), lambda i: (i, 0)),
                  pl.BlockSpec((1, D_MODEL), lambda i: (0, 0)),
                  pl.BlockSpec((D_MODEL, D_IN), lambda i: (0, 0), pipeline_mode=pl.Buffered(1))]
        + [_ANY] * len(after),
        out_specs=[pl.BlockSpec((D_MODEL, tm), lambda i: (0, i)),
                   head_spec, head_spec, head_spec,
                   pl.BlockSpec((tm, D_IN - 3 * D_A), lambda i: (i, 0))],
        compiler_params=_params(52, dimension_semantics=("arbitrary",)),
    )(x, norm_g, w_in_bf, *after)


def _skew_table(gp_row):
    row = lax.broadcasted_iota(jnp.int32, (QB, ROLL_W), 0)
    t = jnp.broadcast_to(gp_row, (QB, ROLL_W))
    for b in range(7):
        t = jnp.where(((row >> b) & 1) == 1, pltpu.roll(t, 1 << b, axis=1), t)
    return t


def _unskew_sum(d):
    row = lax.broadcasted_iota(jnp.int32, (QB, ROLL_W), 0)
    for b in range(7):
        d = jnp.where(((row >> b) & 1) == 1, pltpu.roll(d, ROLL_W - (1 << b), axis=1), d)
    return jnp.sum(d, axis=0, keepdims=True)


def _struct_mask():
    a = lax.broadcasted_iota(jnp.int32, (QB, KB), 0) // CHUNK
    b = lax.broadcasted_iota(jnp.int32, (QB, KB), 1) // CHUNK
    return (b >= a) & (b <= a + N_PREV)


def _load_kv(k_hbm, v_hbm, k_scr, v_scr, sems, S, meanwhile=lambda: None):
    zeros = jnp.zeros((N_HEADS, PADK, HEAD_DIM), BF)
    k_scr[:, 0:PADK, :] = zeros
    v_scr[:, 0:PADK, :] = zeros
    ck = pltpu.make_async_copy(k_hbm, k_scr.at[:, pl.ds(PADK, S), :], sems.at[0])
    cv = pltpu.make_async_copy(v_hbm, v_scr.at[:, pl.ds(PADK, S), :], sems.at[1])
    ck.start()
    cv.start()
    meanwhile()
    ck.wait()
    cv.wait()


_BATCH_NT = (((2,), (2,)), ((0,), (0,)))
_BATCH_NN = (((2,), (1,)), ((0,), (0,)))
_BATCH_TN = (((1,), (1,)), ((0,), (0,)))


def _bdot(a, b, dims):
    return lax.dot_general(a, b, dims, preferred_element_type=F32)


def _scaled(q):
    return q * jnp.asarray(SCALE, BF)


def _scores(qs, kb, bias, i, front):
    s = _bdot(qs, kb, _BATCH_NT) + bias
    if front:
        col = lax.broadcasted_iota(jnp.int32, (1, 1, KB), 2)
        s = jnp.where(col >= PADK - i * QB, s, NEG_INF)
    return s


def _attn_fwd(q3, k3, v3, bias):
    S = q3.shape[1]

    def body(q_ref, k_hbm, v_hbm, bias_ref, o_ref, lse_ref, k_scr, v_scr, sems):
        i = pl.program_id(0)

        @pl.when(i == 0)
        def _():
            _load_kv(k_hbm, v_hbm, k_scr, v_scr, sems, S)

        def step(front):
            start = pl.multiple_of(i * QB, QB)
            kb = k_scr[:, pl.ds(start, KB), :]
            vb = v_scr[:, pl.ds(start, KB), :]
            s = _scores(_scaled(q_ref[...]), kb, bias_ref[...], i, front)
            m = jnp.max(s, axis=-1, keepdims=True)
            e = jnp.exp(s - m)
            l = jnp.sum(e, axis=-1, keepdims=True)
            p = e * (1.0 / l)
            o = _bdot(p.astype(BF), vb, _BATCH_NN)
            lse_ref[...] = jnp.broadcast_to(m + jnp.log(l), (N_HEADS, QB, 128))
            for h in range(N_HEADS):
                o_ref[:, h * HEAD_DIM:(h + 1) * HEAD_DIM] = o[h]

        pl.when(i < KEEP)(functools.partial(step, True))
        pl.when(i >= KEEP)(functools.partial(step, False))

    kv_scr = pltpu.VMEM((N_HEADS, S + PADK, HEAD_DIM), BF)
    return pl.pallas_call(
        body, name="attn_fwd", grid=(S // QB,),
        out_shape=(jax.ShapeDtypeStruct((S, D_A), F32), jax.ShapeDtypeStruct((N_HEADS, S, 128), F32)),
        in_specs=[pl.BlockSpec((N_HEADS, QB, HEAD_DIM), lambda i: (0, i, 0)),
                  pl.BlockSpec(memory_space=pl.ANY), pl.BlockSpec(memory_space=pl.ANY),
                  pl.BlockSpec((N_HEADS, QB, KB), lambda i: (0, 0, 0))],
        out_specs=[pl.BlockSpec((QB, D_A), lambda i: (i, 0)),
                   pl.BlockSpec((N_HEADS, QB, 128), lambda i: (0, i, 0))],
        scratch_shapes=[kv_scr, kv_scr, pltpu.SemaphoreType.DMA((2,))],
        compiler_params=_params(48, dimension_semantics=("arbitrary",)),
    )(q3, k3, v3, bias)


def _attn_bwd(q3, k3, v3, d_att3, lse, bias, after=()):
    S = q3.shape[1]
    nq = S // QB

    def body(q_ref, do_ref, k_hbm, v_hbm, lse_ref, bias_ref, dq_ref, dk_ref, dv_ref, dgp_ref,
             k_scr, v_scr, dk_acc, dv_acc, dbias_acc, pad_scr, sems):
        i = pl.program_id(0)

        @pl.when(i == 0)
        def _():
            def clear():
                dk_acc[...] = jnp.zeros_like(dk_acc)
                dv_acc[...] = jnp.zeros_like(dv_acc)
                dbias_acc[...] = jnp.zeros_like(dbias_acc)
            _load_kv(k_hbm, v_hbm, k_scr, v_scr, sems, S, clear)

        def step(front):
            start = pl.multiple_of(i * QB, QB)
            kb = k_scr[:, pl.ds(start, KB), :]
            vb = v_scr[:, pl.ds(start, KB), :]
            qs = _scaled(q_ref[...])
            do = do_ref[...]
            p = jnp.exp(_scores(qs, kb, bias_ref[...], i, front) - jnp.tile(lse_ref[...], (1, 1, KB // 128)))
            dp = _bdot(do, vb, _BATCH_NT)
            ds = p * (dp - jnp.sum(dp * p, axis=-1, keepdims=True))
            dbias_acc[...] += ds
            dsb = ds.astype(BF)
            dq = _bdot(dsb, kb, _BATCH_NN) * SCALE
            for h in range(N_HEADS):
                dq_ref[:, h * HEAD_DIM:(h + 1) * HEAD_DIM] = dq[h].astype(BF)
            dk_acc[...] += _bdot(dsb, qs, _BATCH_TN)
            dv_acc[...] += _bdot(p.astype(BF), do, _BATCH_TN)

        pl.when(i < KEEP)(functools.partial(step, True))
        pl.when((i >= KEEP) & (i < nq))(functools.partial(step, False))

        for h in range(N_HEADS):
            hs = slice(h * HEAD_DIM, (h + 1) * HEAD_DIM)
            dk_ref[:, hs] = dk_acc[h, 0:QB, :].astype(BF)
            dv_ref[:, hs] = dv_acc[h, 0:QB, :].astype(BF)
        dk_acc[:, 0:KB - QB, :] = dk_acc[:, QB:KB, :]
        dv_acc[:, 0:KB - QB, :] = dv_acc[:, QB:KB, :]
        dk_acc[:, KB - QB:KB, :] = jnp.zeros((N_HEADS, QB, HEAD_DIM), F32)
        dv_acc[:, KB - QB:KB, :] = jnp.zeros((N_HEADS, QB, HEAD_DIM), F32)

        @pl.when(i == nq + KEEP - 1)
        def _():
            lane = lax.broadcasted_iota(jnp.int32, (1, ROLL_W), 1)
            hi = (lane < 384) | (lane >= 832)
            lo = (lane > 640) & (lane < 832)
            pad_scr[...] = jnp.zeros_like(pad_scr)
            for h in range(N_HEADS):
                pad_scr[:, 0:KB] = dbias_acc[h]
                g = _unskew_sum(pad_scr[...])
                s_hi = jnp.sum(jnp.where(hi, g, 0.0), axis=-1, keepdims=True)
                s_lo = jnp.sum(jnp.where(lo, g, 0.0), axis=-1, keepdims=True)
                g = jnp.where(lane == 384, g + s_hi, g)
                g = jnp.where(lane == 640, g + s_lo, g)
                dgp_ref[h:h + 1, :] = g

    last = nq - 1
    kv_scr = pltpu.VMEM((N_HEADS, S + PADK, HEAD_DIM), BF)
    return pl.pallas_call(
        _after(body, 6, after), name="attn_bwd", grid=(nq + KEEP,),
        out_shape=(jax.ShapeDtypeStruct((S, D_A), BF), jax.ShapeDtypeStruct((S, D_A), BF),
                   jax.ShapeDtypeStruct((S, D_A), BF), jax.ShapeDtypeStruct((N_HEADS, ROLL_W), F32)),
        in_specs=[pl.BlockSpec((N_HEADS, QB, HEAD_DIM), lambda i: (0, jnp.minimum(i, last), 0)),
                  pl.BlockSpec((N_HEADS, QB, HEAD_DIM), lambda i: (0, jnp.minimum(i, last), 0)),
                  pl.BlockSpec(memory_space=pl.ANY), pl.BlockSpec(memory_space=pl.ANY),
                  pl.BlockSpec((N_HEADS, QB, 128), lambda i: (0, jnp.minimum(i, last), 0)),
                  pl.BlockSpec((N_HEADS, QB, KB), lambda i: (0, 0, 0))] + [_ANY] * len(after),
        out_specs=[pl.BlockSpec((QB, D_A), lambda i: (jnp.minimum(i, last), 0)),
                   pl.BlockSpec((QB, D_A), lambda i: (jnp.maximum(i - KEEP, 0), 0)),
                   pl.BlockSpec((QB, D_A), lambda i: (jnp.maximum(i - KEEP, 0), 0)),
                   pl.BlockSpec((N_HEADS, ROLL_W), lambda i: (0, 0))],
        scratch_shapes=[kv_scr, kv_scr,
                        pltpu.VMEM((N_HEADS, KB, HEAD_DIM), F32), pltpu.VMEM((N_HEADS, KB, HEAD_DIM), F32),
                        pltpu.VMEM((N_HEADS, QB, KB), F32), pltpu.VMEM((QB, ROLL_W), F32),
                        pltpu.SemaphoreType.DMA((2,))],
        compiler_params=_params(56, dimension_semantics=("arbitrary",)),
    )(q3, d_att3, k3, v3, lse, bias, *after)


def _sgu_core(ub, vb, lg, lb):
    u, du = _gelu_and_grad(ub)
    v, dv = _gelu_and_grad(vb)
    mu = jnp.mean(v, axis=-1, keepdims=True)
    vc = v - mu
    rstd = lax.rsqrt(jnp.mean(vc * vc, axis=-1, keepdims=True) + EPS)
    xh = vc * rstd
    vn = xh * lg + lb
    return u, du, dv, rstd, xh, vn


def _tri():
    r = lax.broadcasted_iota(jnp.int32, (SGU_CHUNK, SGU_CHUNK), 0)
    c = lax.broadcasted_iota(jnp.int32, (SGU_CHUNK, SGU_CHUNK), 1)
    return r >= c


def _tail_sgu(att, zrest, x, target, w_pa, w_pb, w_out, b_gate, final_g, ln_g, ln_b, w_s, b_s_t, tm=256):
    S = x.shape[0]
    nt = S // tm
    chunks = tm // SGU_CHUNK

    def body(att_ref, ga_ref, ub_ref, vb_ref, gb_ref, gta_ref, gtb_ref, x_ref, t_ref,
             wpa_ref, wpb_ref, wout_ref, bg_ref, fg_ref, lg_ref, lb_ref, ws_ref, bst_ref,
             dout_ref, datt_ref, dzt_ref, dzs_ref, gwout_hbm, gwpa_hbm, gwpb_hbm,
             gbg_ref, gfg_ref, loss_ref, gws_ref, gbs_ref, glg_ref, glb_ref,
             acc_out, acc_pa, acc_pb, sg_scr, mix_scr, dvn_scr, bs_acc, sems):
        i = pl.program_id(0)

        @pl.when(i == 0)
        def _():
            for r in (acc_out, acc_pa, acc_pb, gbg_ref, gfg_ref, loss_ref, gws_ref, glg_ref, glb_ref, bs_acc):
                r[...] = jnp.zeros_like(r)

        u, du, dv, rstd, xh, vn = _sgu_core(ub_ref[...], vb_ref[...], lg_ref[...], lb_ref[...])
        vnb = vn.astype(BF)
        tri = _tri()
        blocks = [(g, slice(n * SGU_CHUNK, (n + 1) * SGU_CHUNK), slice(g * 128, (g + 1) * 128))
                  for g in range(N_GROUPS) for n in range(chunks)]
        wts = [jnp.where(tri, ws_ref[g], 0.0) for g in range(N_GROUPS)]
        for g, rs, cs in blocks:
            mixed = _dot(wts[g].astype(BF), vnb[rs, cs]) + bst_ref[:, g:g + 1]
            mix_scr[rs, cs] = mixed
            sg_scr[rs, cs] = u[rs, cs] * mixed

        att = att_ref[...]
        sg = sg_scr[...]
        sa, dsa = _silu_and_grad(ga_ref[...])
        sb, dsb = _silu_and_grad(gb_ref[...])
        ya = (att * sa).astype(BF)
        yb = (sg * sb).astype(BF)
        pa = _dot(ya, wpa_ref[...])
        pb = _dot(yb, wpb_ref[...])
        ga = _sigmoid(gta_ref[...] + bg_ref[:, 0:D_MODEL])
        gb = _sigmoid(gtb_ref[...] + bg_ref[:, D_MODEL:2 * D_MODEL])
        merged = (ga * pa + gb * pb).astype(BF)
        out = x_ref[...] + _dot(merged, wout_ref[...])
        r2 = lax.rsqrt(jnp.mean(out * out, axis=-1, keepdims=True) + EPS)
        nrm = out * r2
        fg = fg_ref[...]
        err = nrm * fg - t_ref[...]
        loss_ref[...] += 0.5 * jnp.sum(jnp.mean(err * err, axis=-1, keepdims=True))
        dy = err * (1.0 / D_MODEL)
        gfg_ref[...] += jnp.sum(dy * nrm, axis=0, keepdims=True)
        dn = dy * fg
        d_out = r2 * (dn - nrm * jnp.mean(dn * nrm, axis=-1, keepdims=True))
        dout_ref[...] = d_out
        d_outb = d_out.astype(BF)
        acc_out[...] += _dot_tn(merged, d_outb)
        dm = _dot_nt(d_outb, wout_ref[...])
        d_pa = (dm * ga).astype(BF)
        d_pb = (dm * gb).astype(BF)
        d_gta = dm * pa * (ga * (1.0 - ga))
        d_gtb = dm * pb * (gb * (1.0 - gb))
        gbg_ref[:, 0:D_MODEL] += jnp.sum(d_gta, axis=0, keepdims=True)
        gbg_ref[:, D_MODEL:2 * D_MODEL] += jnp.sum(d_gtb, axis=0, keepdims=True)
        dzt_ref[:, 2 * D_A:2 * D_A + D_MODEL] = d_gta.astype(BF)
        dzt_ref[:, 2 * D_A + D_MODEL:] = d_gtb.astype(BF)
        acc_pa[...] += _dot_tn(ya, d_pa)
        acc_pb[...] += _dot_tn(yb, d_pb)
        d_ya = _dot_nt(d_pa, wpa_ref[...])
        d_yb = _dot_nt(d_pb, wpb_ref[...])
        d_att = (d_ya * sa).astype(BF)
        for hd in range(N_HEADS):
            datt_ref[hd] = d_att[:, hd * HEAD_DIM:(hd + 1) * HEAD_DIM]
        dzt_ref[:, 0:D_A] = (d_ya * att * dsa).astype(BF)
        dzt_ref[:, D_A:2 * D_A] = (d_yb * sg * dsb).astype(BF)

        dsg = d_yb * sb
        dzs_ref[:, 0:D_B] = (dsg * mix_scr[...] * du).astype(BF)
        dmix = dsg * u
        for g, rs, cs in blocks:
            dmb = dmix[rs, cs].astype(BF)
            bs_acc[:, cs] += dmix[rs, cs]
            gws_ref[g] += _dot_nt(dmb, vnb[rs, cs])
            dvn_scr[rs, cs] = _dot(wts[g].T.astype(BF), dmb)
        dvn = dvn_scr[...]
        glg_ref[...] += jnp.sum(dvn * xh, axis=0, keepdims=True)
        glb_ref[...] += jnp.sum(dvn, axis=0, keepdims=True)
        dxh = dvn * lg_ref[...]
        dvv = rstd * (dxh - jnp.mean(dxh, axis=-1, keepdims=True)
                      - xh * jnp.mean(dxh * xh, axis=-1, keepdims=True))
        dzs_ref[:, D_B:2 * D_B] = (dvv * dv).astype(BF)

        @pl.when(i == nt - 1)
        def _():
            cps = [pltpu.make_async_copy(acc_out, gwout_hbm, sems.at[0]),
                   pltpu.make_async_copy(acc_pa, gwpa_hbm, sems.at[1]),
                   pltpu.make_async_copy(acc_pb, gwpb_hbm, sems.at[2])]
            for cp in cps:
                cp.start()
            lane = lax.broadcasted_iota(jnp.int32, (SGU_CHUNK, 128), 1)
            cols = jnp.zeros((SGU_CHUNK, 128), F32)
            for g in range(N_GROUPS):
                gws_ref[g] = jnp.where(tri, gws_ref[g], 0.0)
                col = jnp.sum(bs_acc[:, g * 128:(g + 1) * 128], axis=-1, keepdims=True)
                cols = jnp.where(lane == g, col, cols)
            gbs_ref[...] = cols
            for cp in cps:
                cp.wait()

    c2 = lambda i: (0, 0)
    c3 = lambda i: (0, 0, 0)
    zcol = lambda w, blk: pl.BlockSpec((tm, w), lambda i: (i, blk))
    row = lambda w: pl.BlockSpec((tm, w), lambda i: (i, 0))
    return pl.pallas_call(
        body, name="tail", grid=(nt,),
        out_shape=(jax.ShapeDtypeStruct((S, D_MODEL), F32), jax.ShapeDtypeStruct((N_HEADS, S, HEAD_DIM), BF),
                   jax.ShapeDtypeStruct((S, 3072), BF), jax.ShapeDtypeStruct((S, 2 * D_B), BF),
                   jax.ShapeDtypeStruct((D_MODEL, D_MODEL), F32), jax.ShapeDtypeStruct((D_A, D_MODEL), F32),
                   jax.ShapeDtypeStruct((D_B, D_MODEL), F32),
                   jax.ShapeDtypeStruct((1, 2 * D_MODEL), F32), jax.ShapeDtypeStruct((1, D_MODEL), F32),
                   jax.ShapeDtypeStruct((1, 128), F32),
                   jax.ShapeDtypeStruct((N_GROUPS, 128, 128), F32), jax.ShapeDtypeStruct((SGU_CHUNK, 128), F32),
                   jax.ShapeDtypeStruct((1, D_B), F32), jax.ShapeDtypeStruct((1, D_B), F32)),
        in_specs=[row(D_A), zcol(512, 0), zcol(512, 1), zcol(512, 2), zcol(512, 3),
                  zcol(D_MODEL, 2), zcol(D_MODEL, 3), row(D_MODEL), row(D_MODEL),
                  pl.BlockSpec((D_A, D_MODEL), c2), pl.BlockSpec((D_B, D_MODEL), c2),
                  pl.BlockSpec((D_MODEL, D_MODEL), c2),
                  pl.BlockSpec((1, 2 * D_MODEL), c2), pl.BlockSpec((1, D_MODEL), c2),
                  pl.BlockSpec((1, D_B), c2), pl.BlockSpec((1, D_B), c2),
                  pl.BlockSpec((N_GROUPS, 128, 128), c3), pl.BlockSpec((128, N_GROUPS), c2)],
        out_specs=[row(D_MODEL), pl.BlockSpec((N_HEADS, tm, HEAD_DIM), lambda i: (0, i, 0)),
                   row(3072), row(2 * D_B), _ANY, _ANY, _ANY,
                   pl.BlockSpec((1, 2 * D_MODEL), c2), pl.BlockSpec((1, D_MODEL), c2),
                   pl.BlockSpec((1, 128), c2),
                   pl.BlockSpec((N_GROUPS, 128, 128), c3), pl.BlockSpec((SGU_CHUNK, 128), c2),
                   pl.BlockSpec((1, D_B), c2), pl.BlockSpec((1, D_B), c2)],
        scratch_shapes=[pltpu.VMEM((D_MODEL, D_MODEL), F32), pltpu.VMEM((D_A, D_MODEL), F32),
                        pltpu.VMEM((D_B, D_MODEL), F32),
                        pltpu.VMEM((tm, D_B), F32), pltpu.VMEM((tm, D_B), F32), pltpu.VMEM((tm, D_B), F32),
                        pltpu.VMEM((SGU_CHUNK, D_B), F32), pltpu.SemaphoreType.DMA((3,))],
        compiler_params=_params(58, dimension_semantics=("arbitrary",)),
    )(att, zrest, zrest, zrest, zrest, zrest, zrest, x, target, w_pa, w_pb, w_out, b_gate, final_g,
      ln_g, ln_b, w_s, b_s_t)


_DZ_MAP = ((0, 0), (1, 0), (2, 0), (3, 0), (4, 0), (4, 1), (3, 1), (3, 2), (3, 3), (3, 4), (3, 5))


def _dh_gradx(dq, dk, dv, dzt, dzs, w_in_bf, x, norm_g, d_out, tm=512, after=()):
    S = x.shape[0]

    def body(dq_ref, dk_ref, dv_ref, dzt_ref, dzs_ref, w_ref, x_ref, g_ref, dout_ref, gx_ref, gn_ref):
        i = pl.program_id(0)

        @pl.when(i == 0)
        def _():
            gn_ref[...] = jnp.zeros_like(gn_ref)

        pieces = (dq_ref, dk_ref, dv_ref, dzt_ref, dzs_ref)
        dh = jnp.zeros((tm, D_MODEL), F32)
        for j, (pc, blk) in enumerate(_DZ_MAP):
            dh += _dot_nt(pieces[pc][:, blk * 512:(blk + 1) * 512], w_ref[:, j * 512:(j + 1) * 512])
        xv = x_ref[...]
        r = lax.rsqrt(jnp.mean(xv * xv, axis=-1, keepdims=True) + EPS)
        nrm = xv * r
        gn_ref[...] += jnp.sum(dh * nrm, axis=0, keepdims=True)
        dn = dh * g_ref[...]
        gx_ref[...] = r * (dn - nrm * jnp.mean(dn * nrm, axis=-1, keepdims=True)) + dout_ref[...]

    row = lambda w: pl.BlockSpec((tm, w), lambda i: (i, 0))
    c2 = lambda i: (0, 0)
    return pl.pallas_call(
        _after(body, 9, after), name="dh_gradx", grid=(S // tm,),
        out_shape=(jax.ShapeDtypeStruct((S, D_MODEL), F32), jax.ShapeDtypeStruct((1, D_MODEL), F32)),
        in_specs=[row(512), row(512), row(512), row(3072), row(1024),
                  pl.BlockSpec((D_MODEL, D_IN), c2, pipeline_mode=pl.Buffered(1)), row(D_MODEL),
                  pl.BlockSpec((1, D_MODEL), c2), row(D_MODEL)]
        + [_ANY] * len(after),
        out_specs=[row(D_MODEL), pl.BlockSpec((1, D_MODEL), c2)],
        compiler_params=_params(48, dimension_semantics=("arbitrary",)),
    )(dq, dk, dv, dzt, dzs, w_in_bf, x, norm_g, d_out, *after)


def _gw_in(ht, dq, dk, dv, dzt, dzs, tn=512, after=()):
    S = ht.shape[1]
    per = 512 // tn
    cols = tuple((pc, per * blk + h) for pc, blk in _DZ_MAP for h in range(per))

    def body(ht_ref, dq_ref, dk_ref, dv_ref, dzt_ref, dzs_ref, o_ref, ob_ref):
        j = pl.program_id(0)
        pieces = (dq_ref, dk_ref, dv_ref, dzt_ref, dzs_ref)
        for pc in range(5):
            hit = functools.reduce(jnp.logical_or, [j == jj for jj, (p, _) in enumerate(cols) if p == pc])

            @pl.when(hit)
            def _(pc=pc):
                g = _dot(ht_ref[...], pieces[pc][...])
                o_ref[...] = g
                ob_ref[...] = g.astype(BF)

    def piece_spec(pc):
        cur = next(blk for p, blk in cols if p == pc)
        held = []
        for p, blk in cols:
            cur = blk if p == pc else cur
            held.append(cur)

        def index_map(j):
            blk = jnp.int32(held[0])
            for jj in range(1, len(held)):
                if held[jj] != held[jj - 1]:
                    blk = jnp.where(j >= jj, jnp.int32(held[jj]), blk)
            return (0, blk)

        return pl.BlockSpec((S, tn), index_map)

    return pl.pallas_call(
        _after(body, 6, after), name="gw_in", grid=(len(cols),),
        out_shape=(jax.ShapeDtypeStruct((D_MODEL, D_IN), F32), jax.ShapeDtypeStruct((D_MODEL, D_IN), BF)),
        in_specs=[pl.BlockSpec((D_MODEL, S), lambda j: (0, 0), pipeline_mode=pl.Buffered(1))]
        + [piece_spec(pc) for pc in range(5)]
        + [_ANY] * len(after),
        out_specs=[pl.BlockSpec((D_MODEL, tn), lambda j: (0, j)), pl.BlockSpec((D_MODEL, tn), lambda j: (0, j))],
        compiler_params=_params(56, dimension_semantics=("arbitrary",)),
    )(ht, dq, dk, dv, dzt, dzs, *after)


_HBM = pl.BlockSpec(memory_space=pltpu.HBM)
_SEM = pl.BlockSpec(memory_space=pltpu.SEMAPHORE)
_ANY = pl.BlockSpec(memory_space=pl.ANY)
_EFFECT = pltpu.SideEffectType.DATAFLOW_SIDE_EFFECTING


def _in_hbm(a):
    return pltpu.with_memory_space_constraint(a, pltpu.HBM)


def _after(body, n_in, after):
    if not after:
        return body
    return lambda *refs: body(*refs[:n_in], *refs[n_in + len(after):])


class _Started:
    def __init__(self, send, recv, bufs, token):
        self.send, self.recv, self.bufs, self.token = send, recv, bufs, token


def _split_start(name, bufs, n_copies, copies, after=()):
    nb = len(bufs)

    def body(*refs):
        refs = refs[:nb] + refs[nb + len(after):]
        for cp in copies(refs[:nb], refs[nb], refs[nb + 1]):
            cp.start()
        refs[-1][...] = jnp.zeros_like(refs[-1])

    outs = pl.pallas_call(
        body, name=name,
        out_shape=(pltpu.SemaphoreType.DMA((n_copies,)), pltpu.SemaphoreType.DMA((n_copies,)),
                   *[pltpu.HBM(b.shape, b.dtype) for b in bufs], jax.ShapeDtypeStruct((8, 128), F32)),
        in_specs=[_HBM] * nb + [_ANY] * len(after),
        out_specs=(_SEM, _SEM, *[_HBM] * nb, pl.BlockSpec(memory_space=pltpu.VMEM)),
        input_output_aliases={k: 2 + k for k in range(nb)},
        compiler_params=_params(1, has_side_effects=_EFFECT),
    )(*[_in_hbm(b) for b in bufs], *after)
    return _Started(outs[0], outs[1], list(outs[2:2 + nb]), outs[-1])


def _split_wait(name, started, copies, after):
    nb = len(started.bufs)

    def body(*refs):
        for cp in copies(refs[:nb], refs[nb], refs[nb + 1]):
            cp.wait_send()
            cp.wait_recv()

    return list(pl.pallas_call(
        body, name=name,
        out_shape=tuple(pltpu.HBM(b.shape, b.dtype) for b in started.bufs),
        in_specs=[_HBM] * nb + [_SEM, _SEM, _ANY],
        out_specs=tuple([_HBM] * nb),
        input_output_aliases={k: k for k in range(nb)},
        compiler_params=_params(1, has_side_effects=_EFFECT),
    )(*started.bufs, started.send, started.recv, after))


def _x1_copies(ws):
    def copies(refs, send_sems, recv_sems):
        x, y, c, _ = _mesh_pos()
        out = []
        for k, w in enumerate(ws):
            for s in range(N_SHARD):
                out.append(pltpu.make_async_remote_copy(
                    src_ref=_UNITS[w](refs[k], s, 1 - c), dst_ref=refs[len(ws) + k].at[s],
                    send_sem=send_sems.at[N_SHARD * k + s], recv_sem=recv_sems.at[N_SHARD * k + s],
                    device_id=(x, y, 1 - c), device_id_type=MESH))
        return out
    return copies


def _x2_copies(n):
    def copies(refs, send_sems, recv_sems):
        x, y, c, chips = _mesh_pos()
        out = []
        for j, (cx, cy) in enumerate(chips):
            for k in range(n):
                out.append(pltpu.make_async_remote_copy(
                    src_ref=refs[k].at[2 * cx + cy], dst_ref=refs[n + k].at[j],
                    send_sem=send_sems.at[3 * k + j], recv_sem=recv_sems.at[3 * k + j],
                    device_id=(cx, cy, c), device_id_type=MESH))
        return out
    return copies


def _x3_copies(ws):
    def copies(refs, send_sems, recv_sems):
        x, y, c, _ = _mesh_pos()
        out = []
        for k, w in enumerate(ws):
            rows = _HALF_ROWS[w]
            mine = refs[k].at[pl.ds(_mo(c * rows, rows), rows), :]
            out.append(pltpu.make_async_remote_copy(
                src_ref=mine, dst_ref=mine, send_sem=send_sems.at[k], recv_sem=recv_sems.at[k],
                device_id=(x, y, 1 - c), device_id_type=MESH))
        return out
    return copies


def _x1_lands(ws, dtype=F32):
    return [lax.empty((N_SHARD,) + _UNIT_SHAPES[w], dtype) for w in ws]


def _x2_lands(ws):
    return [lax.empty((3,) + _UNIT_SHAPES[w], BF) for w in ws]


def _grad_add1(w, g, recv, pos):
    ur, uc = _UNIT_SHAPES[w]

    def body(pos_ref, g_ref, r_ref, own_ref, csb_ref):
        v = g_ref[...] + r_ref[0].astype(F32)
        csb_ref[0] = v.astype(BF)

        @pl.when(pl.program_id(0) == pos_ref[1])
        def _():
            own_ref[...] = v

    u3 = lambda s, pos: (s, 0, 0)
    return pl.pallas_call(
        body, name=f"grad_add1_{w}",
        grid_spec=pltpu.PrefetchScalarGridSpec(
            num_scalar_prefetch=1, grid=(N_SHARD,),
            in_specs=[pl.BlockSpec((ur, uc), lambda s, pos: (pos[0], s)), pl.BlockSpec((1, ur, uc), u3)],
            out_specs=[pl.BlockSpec((ur, uc), lambda s, pos: (0, 0)), pl.BlockSpec((1, ur, uc), u3)]),
        out_shape=(jax.ShapeDtypeStruct((ur, uc), F32), jax.ShapeDtypeStruct((N_SHARD, ur, uc), BF)),
        compiler_params=_params(40, dimension_semantics=("arbitrary",)),
    )(pos, g, recv)


def _grad_add1_group(ws, gs, recvs):
    n = len(ws)

    def body(*refs):
        c = lax.axis_index("c")
        for k, w in enumerate(ws):
            g, r, cs, csb = refs[k], refs[n + k], refs[2 * n + k], refs[3 * n + k]
            for s in range(N_SHARD):
                v = _UNITS[w](g, s, c)[...] + r[s]
                cs[s] = v
                csb[s] = v.astype(BF)

    vm = pl.BlockSpec(memory_space=pltpu.VMEM)
    outs = pl.pallas_call(
        body, name="grad_add1_group",
        out_shape=tuple(jax.ShapeDtypeStruct((N_SHARD,) + _UNIT_SHAPES[w], dt) for dt in (F32, BF) for w in ws),
        in_specs=[vm] * (2 * n), out_specs=[vm] * (2 * n),
        compiler_params=_params(32),
    )(*gs, *recvs)
    return list(outs[:n]), list(outs[n:])


def _grad_add2_group(ws, css, recvs):
    n = len(ws)

    def body(*refs):
        x, y, c, _ = _mesh_pos()
        for k, w in enumerate(ws):
            cs, r, o = refs[k], refs[n + k], refs[2 * n + k]
            rows = _HALF_ROWS[w]
            total = ((cs[2 * x + y] + r[0].astype(F32)) + r[1].astype(F32)) + r[2].astype(F32)
            o[pl.ds(_mo(c * rows, rows), rows), :] = total

    vm = pl.BlockSpec(memory_space=pltpu.VMEM)
    return list(pl.pallas_call(
        body, name="grad_add2_group",
        out_shape=tuple(jax.ShapeDtypeStruct(_SHARD_SHAPES[w], F32) for w in ws),
        in_specs=[vm] * (2 * n), out_specs=[vm] * n,
        compiler_params=_params(32),
    )(*css, *recvs))


def _grad_add2(w, own, recv, pos):
    ur, uc = _UNIT_SHAPES[w]
    nt = 4
    tr = ur // nt

    def body(pos_ref, own_ref, r_ref, o_ref):
        o_ref[...] = ((own_ref[...] + r_ref[0].astype(F32)) + r_ref[1].astype(F32)) + r_ref[2].astype(F32)

    return pl.pallas_call(
        body, name=f"grad_add2_{w}",
        grid_spec=pltpu.PrefetchScalarGridSpec(
            num_scalar_prefetch=1, grid=(nt,),
            in_specs=[pl.BlockSpec((tr, uc), lambda t, pos: (t, 0)),
                      pl.BlockSpec((3, tr, uc), lambda t, pos: (0, t, 0))],
            out_specs=pl.BlockSpec((tr, uc), lambda t, pos: (pos[0] * nt + t, 0))),
        out_shape=jax.ShapeDtypeStruct(_SHARD_SHAPES[w], F32),
        compiler_params=_params(32, dimension_semantics=("arbitrary",)),
    )(pos, own, recv)


def _adamw_math(w, g, m, v):
    m = ADAM_B1 * m + (1.0 - ADAM_B1) * g
    v = ADAM_B2 * v + (1.0 - ADAM_B2) * (g * g)
    m_hat = m / ADAM_C1
    v_hat = v / ADAM_C2
    delta = -ADAM_LR * (m_hat / (jnp.sqrt(v_hat) + ADAM_EPS) + ADAM_WD * w)
    return delta, m, v


def _adamw_group(ws_, gs, ms, vs, after=()):
    n = len(ws_)

    def body(*refs):
        for k in range(n):
            w, g, m, v = (refs[j * n + k] for j in range(4))
            d, nm, nv, gc = (refs[(4 + j) * n + k] for j in range(4))
            gv = g[...]
            d[...], nm[...], nv[...] = _adamw_math(w[...], gv, m[...], v[...])
            gc[...] = gv

    vm = pl.BlockSpec(memory_space=pltpu.VMEM)
    outs = pl.pallas_call(
        _after(body, 4 * n, after), name="adamw_group",
        out_shape=tuple(jax.ShapeDtypeStruct(a.shape, F32) for _ in range(4) for a in ws_),
        in_specs=[vm] * (4 * n) + [_ANY] * len(after), out_specs=[vm] * (4 * n),
        compiler_params=_params(32),
    )(*ws_, *gs, *ms, *vs, *after)
    return [tuple(outs[j * n + k] for j in range(4)) for k in range(n)]


def _adamw(name, w, g, m, v, tr=256, after=()):
    rows, cols = w.shape

    def body(w_ref, g_ref, m_ref, v_ref, d_ref, nm_ref, nv_ref, gc_ref):
        gv = g_ref[...]
        d_ref[...], nm_ref[...], nv_ref[...] = _adamw_math(w_ref[...], gv, m_ref[...], v_ref[...])
        gc_ref[...] = gv

    spec = pl.BlockSpec((tr, cols), lambda i: (i, 0))
    return pl.pallas_call(
        _after(body, 4, after), name=name, grid=(rows // tr,),
        out_shape=tuple(jax.ShapeDtypeStruct((rows, cols), F32) for _ in range(4)),
        in_specs=[spec] * 4 + [_ANY] * len(after), out_specs=[spec] * 4,
        compiler_params=_params(32, dimension_semantics=("arbitrary",)),
    )(w, g, m, v, *after)


_REL_PAD = 384
_VEC_FIELDS = (("norm_g", 0, D_MODEL), ("b_gate", 1024, 2 * D_MODEL), ("sgu_ln_g", 3072, D_B),
               ("sgu_ln_b", 3584, D_B), ("b_s", 4096, N_GROUPS * 128), ("final_g", 4608, D_MODEL))
_LOSS_OFF = 5632
_REL_OFF = 5760
_NV = _REL_OFF + N_HEADS * _REL_PAD
_N_FIELDS = len(_VEC_FIELDS) + 2


_B_S_FIELD = [f[0] for f in _VEC_FIELDS].index("b_s")


def _assemble_row(dst, fields, transposed_b_s):
    for f, (_, off, n) in enumerate(_VEC_FIELDS):
        if transposed_b_s and f == _B_S_FIELD:
            t = fields[f][...].T
            for g in range(N_GROUPS):
                dst[:, off + 128 * g:off + 128 * (g + 1)] = t[g:g + 1, :]
        else:
            dst[:, off:off + n] = fields[f][...]
    for r in range(N_HEADS):
        dst[:, _REL_OFF + _REL_PAD * r:_REL_OFF + _REL_PAD * (r + 1)] = fields[len(_VEC_FIELDS)][r:r + 1, :]


def _small_reduce(grads, loss_row, after=()):
    n_in = _N_FIELDS + 1

    def body(*refs):
        g_refs, loss_ref = refs[:_N_FIELDS], refs[_N_FIELDS]
        out_v, out_w = refs[n_in:n_in + 2]
        mine_v, gath_v, gath_w, send_sems, recv_sems = refs[n_in + 2:]
        x, y, c, chips = _mesh_pos()
        me, sibling = (x, y, c), (x, y, 1 - c)

        _assemble_row(mine_v, g_refs, True)
        mine_v[:, _LOSS_OFF:_LOSS_OFF + 128] = loss_ref[...]
        mine_w = g_refs[-1]
        my_k = 4 * x + 2 * y + c
        gath_v[my_k] = mine_v[...]
        gath_w[my_k] = mine_w[...]

        def copy(k, gath, block, to, src=None):
            dst = gath.at[4 * block[0] + 2 * block[1] + block[2]]
            return pltpu.make_async_remote_copy(
                src_ref=dst if src is None else src, dst_ref=dst,
                send_sem=send_sems.at[k], recv_sem=recv_sems.at[k], device_id=to, device_id_type=MESH)

        bufs = ((gath_v, mine_v), (gath_w, mine_w))
        first, passed = [], []
        for b, (gath, mine) in enumerate(bufs):
            first.append(copy(7 * b, gath, me, sibling, src=mine))
            first += [copy(7 * b + 1 + j, gath, me, (*chip, c), src=mine) for j, chip in enumerate(chips)]
        for cp in first:
            cp.start()
        for b, (gath, _) in enumerate(bufs):
            for j, chip in enumerate(chips):
                copy(7 * b + 1 + j, gath, (*chip, c), me).wait_recv()
                cp = copy(7 * b + 4 + j, gath, (*chip, c), sibling)
                cp.start()
                passed.append(cp)
        for b, (gath, _) in enumerate(bufs):
            copy(7 * b, gath, sibling, me).wait_recv()
            for j, chip in enumerate(chips):
                copy(7 * b + 4 + j, gath, (*chip, 1 - c), me).wait_recv()
        for cp in first + passed:
            cp.wait_send()

        tot_v, tot_w = gath_v[0], gath_w[0]
        for k in range(1, 8):
            tot_v = tot_v + gath_v[k]
            tot_w = tot_w + gath_w[k]
        out_v[...] = tot_v
        out_w[...] = tot_w

    vm = pl.BlockSpec(memory_space=pltpu.VMEM)
    return pl.pallas_call(
        _after(body, n_in, after), name="small_reduce",
        out_shape=(jax.ShapeDtypeStruct((1, _NV), F32), jax.ShapeDtypeStruct((N_GROUPS * 128, 128), F32)),
        in_specs=[vm] * n_in + [_ANY] * len(after), out_specs=[vm] * 2,
        scratch_shapes=[pltpu.VMEM((1, _NV), F32), pltpu.VMEM((8, 1, _NV), F32),
                        pltpu.VMEM((8, N_GROUPS * 128, 128), F32),
                        pltpu.SemaphoreType.DMA((14,)), pltpu.SemaphoreType.DMA((14,))],
        compiler_params=_params(32),
    )(*grads, loss_row, *after)


def _small_adamw(tot_v, tot_w, params):
    n_in = 2 + 3 * _N_FIELDS

    def body(*refs):
        tv_ref, tw_ref = refs[:2]
        p_refs = [refs[2 + k * _N_FIELDS:2 + (k + 1) * _N_FIELDS] for k in range(3)]
        outs = refs[n_in:n_in + 4 * _N_FIELDS + 1]
        wmv = refs[-1]
        for k in range(3):
            _assemble_row(wmv.at[k], p_refs[k], False)
            wmv[k, :, _LOSS_OFF:_LOSS_OFF + 128] = jnp.zeros((1, 128), F32)
        tot_v, tot_w = tv_ref[...], tw_ref[...]
        res_v = (tot_v,) + _adamw_math(wmv[0], tot_v, wmv[1], wmv[2])
        res_w = (tot_w,) + _adamw_math(p_refs[0][-1][...], tot_w, p_refs[1][-1][...], p_refs[2][-1][...])
        for kind in range(4):
            o = outs[kind * _N_FIELDS:(kind + 1) * _N_FIELDS]
            for f, (_, off, n) in enumerate(_VEC_FIELDS):
                o[f][...] = res_v[kind][:, off:off + n]
            for r in range(N_HEADS):
                o[len(_VEC_FIELDS)][r:r + 1, :] = res_v[kind][:, _REL_OFF + _REL_PAD * r:_REL_OFF + _REL_PAD * (r + 1)]
            o[-1][...] = res_w[kind]
        outs[-1][...] = tot_v[:, _LOSS_OFF:_LOSS_OFF + 128]

    field_shapes = [(1, n) for _, _, n in _VEC_FIELDS] + [(N_HEADS, _REL_PAD), (N_GROUPS * 128, 128)]
    vm = pl.BlockSpec(memory_space=pltpu.VMEM)
    operands = [tot_v, tot_w] + [a for p in params for a in p]
    assert len(operands) == n_in
    outs = pl.pallas_call(
        body, name="small_adamw",
        out_shape=tuple(jax.ShapeDtypeStruct(s, F32) for _ in range(4) for s in field_shapes)
        + (jax.ShapeDtypeStruct((1, 128), F32),),
        in_specs=[vm] * n_in, out_specs=[vm] * (4 * _N_FIELDS + 1),
        scratch_shapes=[pltpu.VMEM((3, 1, _NV), F32)],
        compiler_params=_params(32),
    )(*operands)
    return [outs[k * _N_FIELDS:(k + 1) * _N_FIELDS] for k in range(4)], outs[-1]


def _small_fields(norm_g, b_gate, ln_g, ln_b, b_s, final_g, rel_bias, w_s):
    rel = jnp.pad(rel_bias.reshape(N_HEADS, N_REL), ((0, 0), (0, _REL_PAD - N_REL)))
    return (norm_g, b_gate, ln_g, ln_b, b_s.reshape(1, N_GROUPS * 128), final_g.reshape(1, D_MODEL),
            rel, w_s.reshape(N_GROUPS * 128, 128))


def _small_outputs(fields):
    n_g, b_g, l_g, l_b, b_s, f_g, rel, w_s = fields
    return (n_g, b_g, rel[:, :N_REL].reshape(1, N_HEADS, N_REL), l_g, l_b,
            w_s.reshape(1, N_GROUPS, 128, 128), b_s.reshape(1, N_GROUPS, 128), f_g.reshape(D_MODEL))


def _bias_row(rel_bias):
    hi = rel_bias[:, N_REL - 1:N_REL]
    lo = rel_bias[:, 0:1]
    return jnp.concatenate([jnp.broadcast_to(hi, (N_HEADS, 384)), rel_bias[:, ::-1],
                            jnp.broadcast_to(lo, (N_HEADS, 191)), jnp.broadcast_to(hi, (N_HEADS, 192))], axis=1)


def kernel(x, norm_g, w_in, b_gate, rel_bias, sgu_ln_g, sgu_ln_b, w_s, b_s, w_pa, w_pb, w_out, final_g, loss_target, m_norm_g, m_w_in, m_b_gate, m_rel_bias, m_sgu_ln_g, m_sgu_ln_b, m_w_s, m_b_s, m_w_pa, m_w_pb, m_w_out, m_final_g, v_norm_g, v_w_in, v_b_gate, v_rel_bias, v_sgu_ln_g, v_sgu_ln_b, v_w_s, v_b_s, v_w_pa, v_w_pb, v_w_out, v_final_g):
    S = x.shape[1]
    xs = x.reshape(S, D_MODEL)
    tgt = loss_target.reshape(S, D_MODEL)
    big_w = (w_in[0], w_pa[0], w_pb[0], w_out[0])
    big_m = (m_w_in[0], m_w_pa[0], m_w_pb[0], m_w_out[0])
    big_v = (v_w_in[0], v_w_pa[0], v_w_pb[0], v_w_out[0])
    rel = rel_bias[0]
    ws = w_s[0]
    bst = b_s[0].T
    fg = final_g.reshape(1, D_MODEL)
    pos = jnp.stack([lax.axis_index("c"), 2 * lax.axis_index("x") + lax.axis_index("y")]).astype(jnp.int32)

    (w_in_bf,), staged, band_bias = _ag_weights((0,), big_w[:1], (1, 2, 3), big_w[1:], _bias_row(rel))
    ag_s = _split_start("ag_small_start", staged, 9, _gather_copies((1, 2, 3)), after=(w_in_bf,))

    ht, q3, k3, v3, zrest = _inproj_fwd(xs, norm_g, w_in_bf, after=(ag_s.token,))
    att, lse = _attn_fwd(q3, k3, v3, band_bias)
    w_pa_bf, w_pb_bf, w_out_bf = _split_wait("ag_small_wait", ag_s, _gather_copies((1, 2, 3)), att)
    (d_out, d_att, dzt, dzs, gw_out, gw_pa, gw_pb, g_bgate, g_final, loss_row,
     g_ws, g_bs_t, g_lng, g_lnb) = _tail_sgu(
        att, zrest, xs, tgt, w_pa_bf, w_pb_bf, w_out_bf, b_gate, fg, sgu_ln_g, sgu_ln_b, ws, bst)
    ws_s, ws_i = (1, 2, 3), (0,)

    x1s = _split_start("gx1s_start", [gw_pa, gw_pb, gw_out] + _x1_lands(ws_s), 12, _x1_copies(ws_s))
    dq, dk, dv, d_gp = _attn_bwd(q3, k3, v3, d_att, lse, band_bias, after=(x1s.token,))
    got = _split_wait("gx1s_wait", x1s, _x1_copies(ws_s), dq)
    cs_s, csb_s = _grad_add1_group(ws_s, got[:3], got[3:])

    x2s = _split_start("gx2s_start", csb_s + _x2_lands(ws_s), 9, _x2_copies(3))
    gw_in, gw_in_bf = _gw_in(ht, dq, dk, dv, dzt, dzs, after=(x2s.token,))
    x1i = _split_start("gx1i_start", [gw_in_bf] + _x1_lands(ws_i, BF), 4, _x1_copies(ws_i))
    got = _split_wait("gx2s_wait", x2s, _x2_copies(3), x1i.token)
    halves_s = _grad_add2_group(ws_s, cs_s, got[3:])
    x3s = _split_start("gx3s_start", halves_s, 3, _x3_copies(ws_s))
    got = _split_wait("gx1i_wait", x1i, _x1_copies(ws_i), x3s.token)
    sum_i = _grad_add1(0, gw_in, got[1], pos)

    x2i = _split_start("gx2i_start", [sum_i[1]] + _x2_lands(ws_i), 3, _x2_copies(1))
    grad_x, g_norm = _dh_gradx(dq, dk, dv, dzt, dzs, w_in_bf, xs, norm_g, d_out, after=(x2i.token,))
    g_shards_s = _split_wait("gx3s_wait", x3s, _x3_copies(ws_s), grad_x)
    got = _split_wait("gx2i_wait", x2i, _x2_copies(1), grad_x)
    half_i = _grad_add2(0, sum_i[0], got[1], pos)
    x3i = _split_start("gx3i_start", [half_i], 1, _x3_copies(ws_i))
    big = [None] * 4
    big[1:] = _adamw_group(big_w[1:], g_shards_s, big_m[1:], big_v[1:], after=(x3i.token,))

    g_rel = jnp.pad(d_gp[:, 384:384 + N_REL][:, ::-1], ((0, 0), (0, _REL_PAD - N_REL)))
    small_grads = (g_norm, g_bgate, g_lng, g_lnb, g_bs_t, g_final, g_rel, g_ws.reshape(N_GROUPS * 128, 128))
    small_params = (_small_fields(norm_g, b_gate, sgu_ln_g, sgu_ln_b, b_s, final_g, rel_bias, w_s),
                    _small_fields(m_norm_g, m_b_gate, m_sgu_ln_g, m_sgu_ln_b, m_b_s, m_final_g, m_rel_bias, m_w_s),
                    _small_fields(v_norm_g, v_b_gate, v_sgu_ln_g, v_sgu_ln_b, v_b_s, v_final_g, v_rel_bias, v_w_s))
    tot_v, tot_w = _small_reduce(small_grads, loss_row, after=(x3i.token,))
    (gsum, sdelta, sm, sv), loss_out = _small_adamw(tot_v, tot_w, small_params)

    g_shard_i, = _split_wait("gx3i_wait", x3i, _x3_copies(ws_i), loss_out)
    big[0] = _adamw("adamw_w_in", big_w[0], g_shard_i, big_m[0], big_v[0])
    sg_out, sd_out, sm_out, sv_out = (_small_outputs(f) for f in (gsum, sdelta, sm, sv))
    loss = loss_out[0, 0]

    def assemble(small, bigs):
        n_g, b_g, r_b, l_g, l_b, w_s_, b_s_, f_g = small
        b_in, b_pa, b_pb, b_out = (b[None] for b in bigs)
        return (n_g, b_in, b_g, r_b, l_g, l_b, w_s_, b_s_, b_pa, b_pb, b_out, f_g)

    grads_out = assemble(sg_out, [b[3] for b in big])
    delta_out = assemble(sd_out, [b[0] for b in big])
    m_out = assemble(sm_out, [b[1] for b in big])
    v_out = assemble(sv_out, [b[2] for b in big])
    return (loss, grad_x.reshape(1, S, D_MODEL), *grads_out, *delta_out, *m_out, *v_out)
```

```python
import functools
import math

import jax
import jax.numpy as jnp
from jax import lax
from jax.experimental import pallas as pl
from jax.experimental.pallas import tpu as pltpu

F32 = jnp.float32
BF = jnp.bfloat16
MESH = pl.DeviceIdType.MESH

D_MODEL = 1024
D_A = 512
D_B = 512
D_IN = 5632
N_HEADS = 8
HEAD_DIM = 64
CHUNK = 64
N_PREV = 8
SGU_CHUNK = 128
N_GROUPS = 4
N_REL = 257
EPS = 1e-6
NEG_INF = -1e30
SCALE = HEAD_DIM ** -0.5

QB = 2 * CHUNK
KB = (N_PREV + 2) * CHUNK
PADK = N_PREV * CHUNK
ROLL_W = 1024
KEEP = KB // QB - 1
Q_PER_STEP = 2

ADAM_LR = 0.001
ADAM_B1 = 0.9
ADAM_B2 = 0.999
ADAM_EPS = 1e-08
ADAM_WD = 0.01
ADAM_STEP = 10
ADAM_C1 = 1.0 - ADAM_B1 ** ADAM_STEP
ADAM_C2 = 1.0 - ADAM_B2 ** ADAM_STEP

N_SHARD = 4
SHARD_IN = D_IN // N_SHARD
MIB = 1024 * 1024


VMEM_RESERVE_MIB = 60


def _params(vmem_mib, **kw):
    assert vmem_mib <= VMEM_RESERVE_MIB
    return pltpu.CompilerParams(vmem_limit_bytes=VMEM_RESERVE_MIB * MIB, **kw)


def _sigmoid(x):
    return 1.0 / (1.0 + jnp.exp(-x))


def _silu_and_grad(x):
    s = _sigmoid(x)
    return x * s, s * (1.0 + x * (1.0 - s))


_GELU_C = math.sqrt(2.0 / math.pi)
_GELU_A = 0.044715


def _gelu_and_grad(x):
    x2 = x * x
    t = jnp.tanh(_GELU_C * (x + _GELU_A * (x2 * x)))
    cdf = 0.5 * (1.0 + t)
    grad = cdf + 0.5 * x * (1.0 - t * t) * (_GELU_C * (1.0 + 3.0 * _GELU_A * x2))
    return x * cdf, grad


def _dot(a, b):
    return jnp.dot(a, b, preferred_element_type=F32)


def _dot_nt(a, b):
    return lax.dot_general(a, b, (((1,), (1,)), ((), ())), preferred_element_type=F32)


def _dot_tn(a, b):
    return lax.dot_general(a, b, (((0,), (0,)), ((), ())), preferred_element_type=F32)


def _mo(v, m):
    return v if isinstance(v, int) else pl.multiple_of(v, m)


def _unit_in(ref, s, p):
    return ref.at[pl.ds(_mo(p * 512, 512), 512), pl.ds(_mo(s * SHARD_IN, 128), SHARD_IN)]


def _unit_p(ref, s, p):
    return ref.at[pl.ds(_mo(p * 256, 256), 256), pl.ds(_mo(s * 256, 128), 256)]


def _unit_out(ref, s, p):
    return ref.at[pl.ds(_mo(s * 256 + p * 128, 128), 128), :]


_UNITS = (_unit_in, _unit_p, _unit_p, _unit_out)
_HALF_ROWS = (512, 256, 256, 128)
_UNIT_SHAPES = ((512, SHARD_IN), (256, 256), (256, 256), (128, D_MODEL))
_FULL_SHAPES = ((D_MODEL, D_IN), (D_A, D_MODEL), (D_B, D_MODEL), (D_MODEL, D_MODEL))
_SHARD_SHAPES = ((D_MODEL, SHARD_IN), (D_A, 256), (D_B, 256), (256, D_MODEL))


def _mesh_pos():
    x, y, c = lax.axis_index("x"), lax.axis_index("y"), lax.axis_index("c")
    chips = [(1 - x, y), (x, 1 - y), (1 - x, 1 - y)]
    return x, y, c, chips


def _ag_weights(ws, shards, later_ws, later_shards, gp):
    n, m = len(ws), len(later_ws)

    def body(*refs):
        ins, later_ins, gp_ref = refs[:n], refs[n:n + m], refs[n + m]
        o = n + m + 1
        outs, later_outs, bias_ref = refs[o:o + n], refs[o + n:o + n + m], refs[o + n + m]
        o += n + m + 1
        stage, later_stage = refs[o:o + n], refs[o + n:o + n + m]
        send_sems, recv_sems, local_sems, later_sems = refs[o + n + m:]
        x, y, c, chips = _mesh_pos()
        s_me = 2 * x + y
        sibling = (x, y, 1 - c)
        def rows_of(k, p):
            rows = _HALF_ROWS[ws[k]]
            return pl.ds(_mo(p * rows, rows), rows)

        def half(k, p):
            return stage[k].at[rows_of(k, p), :]

        def unit(k, s, p):
            return _UNITS[ws[k]](outs[k], s, p)

        def rcopy(k, i, src, dst, to):
            return pltpu.make_async_remote_copy(src_ref=src, dst_ref=dst, send_sem=send_sems.at[k, i],
                                                recv_sem=recv_sems.at[k, i], device_id=to, device_id_type=MESH)

        for k in range(n):
            stage[k][rows_of(k, c), :] = ins[k][rows_of(k, c), :].astype(BF)
        sends = []
        for j, (cx, cy) in enumerate(chips):
            for k in range(n):
                cp = rcopy(k, j, half(k, c), unit(k, s_me, c), (cx, cy, c))
                cp.start()
                sends.append(cp)
        for k in range(n):
            stage[k][rows_of(k, 1 - c), :] = ins[k][rows_of(k, 1 - c), :].astype(BF)
        local = []
        for k in range(n):
            for p in range(2):
                cp = pltpu.make_async_copy(half(k, p), unit(k, s_me, p), local_sems.at[k, p])
                cp.start()
                local.append(cp)
        for k, w in enumerate(later_ws):
            later_stage[k][...] = later_ins[k][...].astype(BF)
            cp = pltpu.make_async_copy(later_stage[k], _shard_of(later_outs[k], w, s_me), later_sems.at[k])
            cp.start()
            local.append(cp)
        keep = _struct_mask()
        for h in range(N_HEADS):
            bias_ref[h] = jnp.where(keep, _skew_table(gp_ref[h:h + 1, :])[:, :KB], NEG_INF)
        for j, (cx, cy) in enumerate(chips):
            for k in range(n):
                landed = unit(k, 2 * cx + cy, c)
                rcopy(k, j, landed, landed, (cx, cy, c)).wait_recv()
                cp = rcopy(k, 3 + j, landed, landed, sibling)
                cp.start()
                sends.append(cp)
        for j, (cx, cy) in enumerate(chips):
            for k in range(n):
                other = unit(k, 2 * cx + cy, 1 - c)
                rcopy(k, 3 + j, other, other, sibling).wait_recv()
        for cp in sends:
            cp.wait_send()
        for cp in local:
            cp.wait()

    vm = pl.BlockSpec(memory_space=pltpu.VMEM)
    outs = pl.pallas_call(
        body, name="ag_weights",
        out_shape=tuple(jax.ShapeDtypeStruct(_FULL_SHAPES[w], BF) for w in tuple(ws) + tuple(later_ws))
        + (jax.ShapeDtypeStruct((N_HEADS, QB, KB), F32),),
        in_specs=[vm] * (n + m + 1), out_specs=[_ANY] * (n + m) + [vm],
        scratch_shapes=[pltpu.VMEM(_SHARD_SHAPES[w], BF) for w in tuple(ws) + tuple(later_ws)]
        + [pltpu.SemaphoreType.DMA((n, 6)), pltpu.SemaphoreType.DMA((n, 6)), pltpu.SemaphoreType.DMA((n, 2)),
           pltpu.SemaphoreType.DMA((m,))],
        compiler_params=_params(48),
    )(*shards, *later_shards, gp)
    return list(outs[:n]), list(outs[n:n + m]), outs[-1]


def _shard_of(ref, w, s):
    if w == 0:
        return ref.at[:, pl.ds(_mo(s * SHARD_IN, 128), SHARD_IN)]
    if w == 3:
        return ref.at[pl.ds(_mo(s * 256, 256), 256), :]
    return ref.at[:, pl.ds(_mo(s * 256, 128), 256)]


def _gather_copies(ws):
    def copies(refs, send_sems, recv_sems):
        x, y, c, chips = _mesh_pos()
        out = []
        for j, (cx, cy) in enumerate(chips):
            for k, w in enumerate(ws):
                mine = _shard_of(refs[k], w, 2 * x + y)
                out.append(pltpu.make_async_remote_copy(
                    src_ref=mine, dst_ref=mine, send_sem=send_sems.at[3 * k + j], recv_sem=recv_sems.at[3 * k + j],
                    device_id=(cx, cy, c), device_id_type=MESH))
        return out
    return copies


def _inproj_fwd(x, norm_g, w_in_bf, tm=512, after=()):
    S = x.shape[0]

    def body(x_ref, g_ref, w_ref, ht_ref, q_ref, k_ref, v_ref, zr_ref):
        xv = x_ref[...]
        r = lax.rsqrt(jnp.mean(xv * xv, axis=-1, keepdims=True) + EPS)
        hf = (xv * r) * g_ref[...]
        ht_ref[...] = hf.T.astype(BF)
        h = hf.astype(BF)
        heads = (q_ref, k_ref, v_ref)
        for j in range(D_IN // 512):
            z = _dot(h, w_ref[:, j * 512:(j + 1) * 512])
            if j < 3:
                zb = z.astype(BF)
                for hd in range(N_HEADS):
                    heads[j][hd] = zb[:, hd * HEAD_DIM:(hd + 1) * HEAD_DIM]
            else:
                zr_ref[:, (j - 3) * 512:(j - 2) * 512] = z

    head_major = jax.ShapeDtypeStruct((N_HEADS, S, HEAD_DIM), BF)
    head_spec = pl.BlockSpec((N_HEADS, tm, HEAD_DIM), lambda i: (0, i, 0))
    return pl.pallas_call(
        _after(body, 3, after), name="inproj_fwd", grid=(S // tm,),
        out_shape=(jax.ShapeDtypeStruct((D_MODEL, S), BF), head_major, head_major, head_major,
                   jax.ShapeDtypeStruct((S, D_IN - 3 * D_A), F32)),
        in_specs=[pl.BlockSpec((tm, D_MODEL), lambda i: (i, 0)),
                  pl.BlockSpec((1, D_MODEL), lambda i: (0, 0)),
                  pl.BlockSpec((D_MODEL, D_IN), lambda i: (0, 0), pipeline_mode=pl.Buffered(1))]
        + [_ANY] * len(after),
        out_specs=[pl.BlockSpec((D_MODEL, tm), lambda i: (0, i)),
                   head_spec, head_spec, head_spec,
                   pl.BlockSpec((tm, D_IN - 3 * D_A), lambda i: (i, 0))],
        compiler_params=_params(52, dimension_semantics=("arbitrary",)),
    )(x, norm_g, w_in_bf, *after)


def _skew_table(gp_row):
    row = lax.broadcasted_iota(jnp.int32, (QB, ROLL_W), 0)
    t = jnp.broadcast_to(gp_row, (QB, ROLL_W))
    for b in range(7):
        t = jnp.where(((row >> b) & 1) == 1, pltpu.roll(t, 1 << b, axis=1), t)
    return t


def _unskew_sum(d):
    row = lax.broadcasted_iota(jnp.int32, (QB, ROLL_W), 0)
    for b in range(7):
        d = jnp.where(((row >> b) & 1) == 1, pltpu.roll(d, ROLL_W - (1 << b), axis=1), d)
    return jnp.sum(d, axis=0, keepdims=True)


def _struct_mask():
    a = lax.broadcasted_iota(jnp.int32, (QB, KB), 0) // CHUNK
    b = lax.broadcasted_iota(jnp.int32, (QB, KB), 1) // CHUNK
    return (b >= a) & (b <= a + N_PREV)


def _load_kv(k_hbm, v_hbm, k_scr, v_scr, sems, S, meanwhile=lambda: None):
    zeros = jnp.zeros((N_HEADS, PADK, HEAD_DIM), BF)
    k_scr[:, 0:PADK, :] = zeros
    v_scr[:, 0:PADK, :] = zeros
    ck = pltpu.make_async_copy(k_hbm, k_scr.at[:, pl.ds(PADK, S), :], sems.at[0])
    cv = pltpu.make_async_copy(v_hbm, v_scr.at[:, pl.ds(PADK, S), :], sems.at[1])
    ck.start()
    cv.start()
    meanwhile()
    ck.wait()
    cv.wait()


_BATCH_NT = (((2,), (2,)), ((0,), (0,)))
_BATCH_NN = (((2,), (1,)), ((0,), (0,)))
_BATCH_TN = (((1,), (1,)), ((0,), (0,)))


def _bdot(a, b, dims):
    return lax.dot_general(a, b, dims, preferred_element_type=F32)


def _scaled(q):
    return q * jnp.asarray(SCALE, BF)


def _scores(qs, kb, bias, i, front):
    s = _bdot(qs, kb, _BATCH_NT) + bias
    if front:
        col = lax.broadcasted_iota(jnp.int32, (1, 1, KB), 2)
        s = jnp.where(col >= PADK - i * QB, s, NEG_INF)
    return s


def _attn_fwd(q3, k3, v3, bias):
    S = q3.shape[1]

    def body(q_ref, k_hbm, v_hbm, bias_ref, o_ref, lse_ref, k_scr, v_scr, sems):
        @pl.when(pl.program_id(0) == 0)
        def _():
            _load_kv(k_hbm, v_hbm, k_scr, v_scr, sems, S)

        def step(i, rows, front):
            start = pl.multiple_of(i * QB, QB)
            kb = k_scr[:, pl.ds(start, KB), :]
            vb = v_scr[:, pl.ds(start, KB), :]
            s = _scores(_scaled(q_ref[:, rows, :]), kb, bias_ref[...], i, front)
            m = jnp.max(s, axis=-1, keepdims=True)
            e = jnp.exp(s - m)
            l = jnp.sum(e, axis=-1, keepdims=True)
            p = e * (1.0 / l)
            o = _bdot(p.astype(BF), vb, _BATCH_NN)
            lse_ref[:, rows, :] = jnp.broadcast_to(m + jnp.log(l), (N_HEADS, QB, 128))
            for h in range(N_HEADS):
                o_ref[rows, h * HEAD_DIM:(h + 1) * HEAD_DIM] = o[h]

        def block(j, carry):
            i = pl.program_id(0) * Q_PER_STEP + j
            rows = pl.ds(pl.multiple_of(j * QB, QB), QB)
            pl.when(i < KEEP)(functools.partial(step, i, rows, True))
            pl.when(i >= KEEP)(functools.partial(step, i, rows, False))
            return carry

        lax.fori_loop(0, Q_PER_STEP, block, 0)

    rows_per_step = Q_PER_STEP * QB
    kv_scr = pltpu.VMEM((N_HEADS, S + PADK, HEAD_DIM), BF)
    return pl.pallas_call(
        body, name="attn_fwd", grid=(S // rows_per_step,),
        out_shape=(jax.ShapeDtypeStruct((S, D_A), F32), jax.ShapeDtypeStruct((N_HEADS, S, 128), F32)),
        in_specs=[pl.BlockSpec((N_HEADS, rows_per_step, HEAD_DIM), lambda g: (0, g, 0)),
                  pl.BlockSpec(memory_space=pl.ANY), pl.BlockSpec(memory_space=pl.ANY),
                  pl.BlockSpec((N_HEADS, QB, KB), lambda g: (0, 0, 0))],
        out_specs=[pl.BlockSpec((rows_per_step, D_A), lambda g: (g, 0)),
                   pl.BlockSpec((N_HEADS, rows_per_step, 128), lambda g: (0, g, 0))],
        scratch_shapes=[kv_scr, kv_scr, pltpu.SemaphoreType.DMA((2,))],
        compiler_params=_params(48, dimension_semantics=("arbitrary",)),
    )(q3, k3, v3, bias)


def _attn_bwd(q3, k3, v3, d_att3, lse, bias, after=()):
    S = q3.shape[1]
    nq = S // QB

    def body(q_ref, do_ref, k_hbm, v_hbm, lse_ref, bias_ref, dq_ref, dk_ref, dv_ref, dgp_ref,
             k_scr, v_scr, dk_acc, dv_acc, dbias_acc, pad_scr, sems):
        @pl.when(pl.program_id(0) == 0)
        def _():
            def clear():
                dk_acc[...] = jnp.zeros_like(dk_acc)
                dv_acc[...] = jnp.zeros_like(dv_acc)
                dbias_acc[...] = jnp.zeros_like(dbias_acc)
            _load_kv(k_hbm, v_hbm, k_scr, v_scr, sems, S, clear)

        def step(i, rows, front):
            start = pl.multiple_of(i * QB, QB)
            kb = k_scr[:, pl.ds(start, KB), :]
            vb = v_scr[:, pl.ds(start, KB), :]
            qs = _scaled(q_ref[:, rows, :])
            do = do_ref[:, rows, :]
            p = jnp.exp(_scores(qs, kb, bias_ref[...], i, front) - jnp.tile(lse_ref[:, rows, :], (1, 1, KB // 128)))
            dp = _bdot(do, vb, _BATCH_NT)
            ds = p * (dp - jnp.sum(dp * p, axis=-1, keepdims=True))
            dbias_acc[...] += ds
            dsb = ds.astype(BF)
            dq = _bdot(dsb, kb, _BATCH_NN) * SCALE
            for h in range(N_HEADS):
                dq_ref[rows, h * HEAD_DIM:(h + 1) * HEAD_DIM] = dq[h].astype(BF)
            dk_acc[...] += _bdot(dsb, qs, _BATCH_TN)
            dv_acc[...] += _bdot(p.astype(BF), do, _BATCH_TN)

        def block(j, carry):
            i = pl.program_id(0) * Q_PER_STEP + j
            rows = pl.ds(pl.multiple_of(j * QB, QB), QB)
            pl.when(i < KEEP)(functools.partial(step, i, rows, True))
            pl.when((i >= KEEP) & (i < nq))(functools.partial(step, i, rows, False))
            for h in range(N_HEADS):
                hs = slice(h * HEAD_DIM, (h + 1) * HEAD_DIM)
                dk_ref[rows, hs] = dk_acc[h, 0:QB, :].astype(BF)
                dv_ref[rows, hs] = dv_acc[h, 0:QB, :].astype(BF)
            dk_acc[:, 0:KB - QB, :] = dk_acc[:, QB:KB, :]
            dv_acc[:, 0:KB - QB, :] = dv_acc[:, QB:KB, :]
            dk_acc[:, KB - QB:KB, :] = jnp.zeros((N_HEADS, QB, HEAD_DIM), F32)
            dv_acc[:, KB - QB:KB, :] = jnp.zeros((N_HEADS, QB, HEAD_DIM), F32)
            return carry

        lax.fori_loop(0, Q_PER_STEP, block, 0)

        @pl.when(pl.program_id(0) == n_steps - 1)
        def _():
            lane = lax.broadcasted_iota(jnp.int32, (1, ROLL_W), 1)
            hi = (lane < 384) | (lane >= 832)
            lo = (lane > 640) & (lane < 832)
            pad_scr[...] = jnp.zeros_like(pad_scr)
            for h in range(N_HEADS):
                pad_scr[:, 0:KB] = dbias_acc[h]
                g = _unskew_sum(pad_scr[...])
                s_hi = jnp.sum(jnp.where(hi, g, 0.0), axis=-1, keepdims=True)
                s_lo = jnp.sum(jnp.where(lo, g, 0.0), axis=-1, keepdims=True)
                g = jnp.where(lane == 384, g + s_hi, g)
                g = jnp.where(lane == 640, g + s_lo, g)
                dgp_ref[h:h + 1, :] = g

    assert nq % Q_PER_STEP == 0 and KEEP % Q_PER_STEP == 0
    rows_per_step = Q_PER_STEP * QB
    n_steps = (nq + KEEP) // Q_PER_STEP
    last = nq // Q_PER_STEP - 1
    lag = KEEP // Q_PER_STEP
    kv_scr = pltpu.VMEM((N_HEADS, S + PADK, HEAD_DIM), BF)
    return pl.pallas_call(
        _after(body, 6, after), name="attn_bwd", grid=(n_steps,),
        out_shape=(jax.ShapeDtypeStruct((S, D_A), BF), jax.ShapeDtypeStruct((S, D_A), BF),
                   jax.ShapeDtypeStruct((S, D_A), BF), jax.ShapeDtypeStruct((N_HEADS, ROLL_W), F32)),
        in_specs=[pl.BlockSpec((N_HEADS, rows_per_step, HEAD_DIM), lambda g: (0, jnp.minimum(g, last), 0)),
                  pl.BlockSpec((N_HEADS, rows_per_step, HEAD_DIM), lambda g: (0, jnp.minimum(g, last), 0)),
                  pl.BlockSpec(memory_space=pl.ANY), pl.BlockSpec(memory_space=pl.ANY),
                  pl.BlockSpec((N_HEADS, rows_per_step, 128), lambda g: (0, jnp.minimum(g, last), 0)),
                  pl.BlockSpec((N_HEADS, QB, KB), lambda g: (0, 0, 0))] + [_ANY] * len(after),
        out_specs=[pl.BlockSpec((rows_per_step, D_A), lambda g: (jnp.minimum(g, last), 0)),
                   pl.BlockSpec((rows_per_step, D_A), lambda g: (jnp.maximum(g - lag, 0), 0)),
                   pl.BlockSpec((rows_per_step, D_A), lambda g: (jnp.maximum(g - lag, 0), 0)),
                   pl.BlockSpec((N_HEADS, ROLL_W), lambda g: (0, 0))],
        scratch_shapes=[kv_scr, kv_scr,
                        pltpu.VMEM((N_HEADS, KB, HEAD_DIM), F32), pltpu.VMEM((N_HEADS, KB, HEAD_DIM), F32),
                        pltpu.VMEM((N_HEADS, QB, KB), F32), pltpu.VMEM((QB, ROLL_W), F32),
                        pltpu.SemaphoreType.DMA((2,))],
        compiler_params=_params(56, dimension_semantics=("arbitrary",)),
    )(q3, d_att3, k3, v3, lse, bias, *after)


def _sgu_core(ub, vb, lg, lb):
    u, du = _gelu_and_grad(ub)
    v, dv = _gelu_and_grad(vb)
    mu = jnp.mean(v, axis=-1, keepdims=True)
    vc = v - mu
    rstd = lax.rsqrt(jnp.mean(vc * vc, axis=-1, keepdims=True) + EPS)
    xh = vc * rstd
    vn = xh * lg + lb
    return u, du, dv, rstd, xh, vn


def _tri():
    r = lax.broadcasted_iota(jnp.int32, (SGU_CHUNK, SGU_CHUNK), 0)
    c = lax.broadcasted_iota(jnp.int32, (SGU_CHUNK, SGU_CHUNK), 1)
    return r >= c


def _tail_sgu(att, zrest, x, target, w_pa, w_pb, w_out, b_gate, final_g, ln_g, ln_b, w_s, b_s_t, tm=256):
    S = x.shape[0]
    nt = S // tm
    chunks = tm // SGU_CHUNK

    def body(att_ref, ga_ref, ub_ref, vb_ref, gb_ref, gta_ref, gtb_ref, x_ref, t_ref,
             wpa_ref, wpb_ref, wout_ref, bg_ref, fg_ref, lg_ref, lb_ref, ws_ref, bst_ref,
             dout_ref, datt_ref, dzt_ref, dzs_ref, gwout_hbm, gwpa_hbm, gwpb_hbm,
             gbg_ref, gfg_ref, loss_ref, gws_ref, gbs_ref, glg_ref, glb_ref,
             acc_out, acc_pa, acc_pb, sg_scr, mix_scr, dvn_scr, bs_acc, sems):
        i = pl.program_id(0)

        @pl.when(i == 0)
        def _():
            for r in (acc_out, acc_pa, acc_pb, gbg_ref, gfg_ref, loss_ref, gws_ref, glg_ref, glb_ref, bs_acc):
                r[...] = jnp.zeros_like(r)

        u, du, dv, rstd, xh, vn = _sgu_core(ub_ref[...], vb_ref[...], lg_ref[...], lb_ref[...])
        vnb = vn.astype(BF)
        tri = _tri()
        blocks = [(g, slice(n * SGU_CHUNK, (n + 1) * SGU_CHUNK), slice(g * 128, (g + 1) * 128))
                  for g in range(N_GROUPS) for n in range(chunks)]
        wts = [jnp.where(tri, ws_ref[g], 0.0) for g in range(N_GROUPS)]
        for g, rs, cs in blocks:
            mixed = _dot(wts[g].astype(BF), vnb[rs, cs]) + bst_ref[:, g:g + 1]
            mix_scr[rs, cs] = mixed
            sg_scr[rs, cs] = u[rs, cs] * mixed

        att = att_ref[...]
        sg = sg_scr[...]
        sa, dsa = _silu_and_grad(ga_ref[...])
        sb, dsb = _silu_and_grad(gb_ref[...])
        ya = (att * sa).astype(BF)
        yb = (sg * sb).astype(BF)
        pa = _dot(ya, wpa_ref[...])
        pb = _dot(yb, wpb_ref[...])
        ga = _sigmoid(gta_ref[...] + bg_ref[:, 0:D_MODEL])
        gb = _sigmoid(gtb_ref[...] + bg_ref[:, D_MODEL:2 * D_MODEL])
        merged = (ga * pa + gb * pb).astype(BF)
        out = x_ref[...] + _dot(merged, wout_ref[...])
        r2 = lax.rsqrt(jnp.mean(out * out, axis=-1, keepdims=True) + EPS)
        nrm = out * r2
        fg = fg_ref[...]
        err = nrm * fg - t_ref[...]
        loss_ref[...] += 0.5 * jnp.sum(jnp.mean(err * err, axis=-1, keepdims=True))
        dy = err * (1.0 / D_MODEL)
        gfg_ref[...] += jnp.sum(dy * nrm, axis=0, keepdims=True)
        dn = dy * fg
        d_out = r2 * (dn - nrm * jnp.mean(dn * nrm, axis=-1, keepdims=True))
        dout_ref[...] = d_out
        d_outb = d_out.astype(BF)
        acc_out[...] += _dot_tn(merged, d_outb)
        dm = _dot_nt(d_outb, wout_ref[...])
        d_pa = (dm * ga).astype(BF)
        d_pb = (dm * gb).astype(BF)
        d_gta = dm * pa * (ga * (1.0 - ga))
        d_gtb = dm * pb * (gb * (1.0 - gb))
        gbg_ref[:, 0:D_MODEL] += jnp.sum(d_gta, axis=0, keepdims=True)
        gbg_ref[:, D_MODEL:2 * D_MODEL] += jnp.sum(d_gtb, axis=0, keepdims=True)
        dzt_ref[:, 2 * D_A:2 * D_A + D_MODEL] = d_gta.astype(BF)
        dzt_ref[:, 2 * D_A + D_MODEL:] = d_gtb.astype(BF)
        acc_pa[...] += _dot_tn(ya, d_pa)
        acc_pb[...] += _dot_tn(yb, d_pb)
        d_ya = _dot_nt(d_pa, wpa_ref[...])
        d_yb = _dot_nt(d_pb, wpb_ref[...])
        d_att = (d_ya * sa).astype(BF)
        for hd in range(N_HEADS):
            datt_ref[hd] = d_att[:, hd * HEAD_DIM:(hd + 1) * HEAD_DIM]
        dzt_ref[:, 0:D_A] = (d_ya * att * dsa).astype(BF)
        dzt_ref[:, D_A:2 * D_A] = (d_yb * sg * dsb).astype(BF)

        dsg = d_yb * sb
        dzs_ref[:, 0:D_B] = (dsg * mix_scr[...] * du).astype(BF)
        dmix = dsg * u
        for g, rs, cs in blocks:
            dmb = dmix[rs, cs].astype(BF)
            bs_acc[:, cs] += dmix[rs, cs]
            gws_ref[g] += _dot_nt(dmb, vnb[rs, cs])
            dvn_scr[rs, cs] = _dot(wts[g].T.astype(BF), dmb)
        dvn = dvn_scr[...]
        glg_ref[...] += jnp.sum(dvn * xh, axis=0, keepdims=True)
        glb_ref[...] += jnp.sum(dvn, axis=0, keepdims=True)
        dxh = dvn * lg_ref[...]
        dvv = rstd * (dxh - jnp.mean(dxh, axis=-1, keepdims=True)
                      - xh * jnp.mean(dxh * xh, axis=-1, keepdims=True))
        dzs_ref[:, D_B:2 * D_B] = (dvv * dv).astype(BF)

        @pl.when(i == nt - 1)
        def _():
            cps = [pltpu.make_async_copy(acc_out, gwout_hbm, sems.at[0]),
                   pltpu.make_async_copy(acc_pa, gwpa_hbm, sems.at[1]),
                   pltpu.make_async_copy(acc_pb, gwpb_hbm, sems.at[2])]
            for cp in cps:
                cp.start()
            lane = lax.broadcasted_iota(jnp.int32, (SGU_CHUNK, 128), 1)
            cols = jnp.zeros((SGU_CHUNK, 128), F32)
            for g in range(N_GROUPS):
                gws_ref[g] = jnp.where(tri, gws_ref[g], 0.0)
                col = jnp.sum(bs_acc[:, g * 128:(g + 1) * 128], axis=-1, keepdims=True)
                cols = jnp.where(lane == g, col, cols)
            gbs_ref[...] = cols
            for cp in cps:
                cp.wait()

    c2 = lambda i: (0, 0)
    c3 = lambda i: (0, 0, 0)
    zcol = lambda w, blk: pl.BlockSpec((tm, w), lambda i: (i, blk))
    row = lambda w: pl.BlockSpec((tm, w), lambda i: (i, 0))
    return pl.pallas_call(
        body, name="tail", grid=(nt,),
        out_shape=(jax.ShapeDtypeStruct((S, D_MODEL), F32), jax.ShapeDtypeStruct((N_HEADS, S, HEAD_DIM), BF),
                   jax.ShapeDtypeStruct((S, 3072), BF), jax.ShapeDtypeStruct((S, 2 * D_B), BF),
                   jax.ShapeDtypeStruct((D_MODEL, D_MODEL), F32), jax.ShapeDtypeStruct((D_A, D_MODEL), F32),
                   jax.ShapeDtypeStruct((D_B, D_MODEL), F32),
                   jax.ShapeDtypeStruct((1, 2 * D_MODEL), F32), jax.ShapeDtypeStruct((1, D_MODEL), F32),
                   jax.ShapeDtypeStruct((1, 128), F32),
                   jax.ShapeDtypeStruct((N_GROUPS, 128, 128), F32), jax.ShapeDtypeStruct((SGU_CHUNK, 128), F32),
                   jax.ShapeDtypeStruct((1, D_B), F32), jax.ShapeDtypeStruct((1, D_B), F32)),
        in_specs=[row(D_A), zcol(512, 0), zcol(512, 1), zcol(512, 2), zcol(512, 3),
                  zcol(D_MODEL, 2), zcol(D_MODEL, 3), row(D_MODEL), row(D_MODEL),
                  pl.BlockSpec((D_A, D_MODEL), c2), pl.BlockSpec((D_B, D_MODEL), c2),
                  pl.BlockSpec((D_MODEL, D_MODEL), c2),
                  pl.BlockSpec((1, 2 * D_MODEL), c2), pl.BlockSpec((1, D_MODEL), c2),
                  pl.BlockSpec((1, D_B), c2), pl.BlockSpec((1, D_B), c2),
                  pl.BlockSpec((N_GROUPS, 128, 128), c3), pl.BlockSpec((128, N_GROUPS), c2)],
        out_specs=[row(D_MODEL), pl.BlockSpec((N_HEADS, tm, HEAD_DIM), lambda i: (0, i, 0)),
                   row(3072), row(2 * D_B), _ANY, _ANY, _ANY,
                   pl.BlockSpec((1, 2 * D_MODEL), c2), pl.BlockSpec((1, D_MODEL), c2),
                   pl.BlockSpec((1, 128), c2),
                   pl.BlockSpec((N_GROUPS, 128, 128), c3), pl.BlockSpec((SGU_CHUNK, 128), c2),
                   pl.BlockSpec((1, D_B), c2), pl.BlockSpec((1, D_B), c2)],
        scratch_shapes=[pltpu.VMEM((D_MODEL, D_MODEL), F32), pltpu.VMEM((D_A, D_MODEL), F32),
                        pltpu.VMEM((D_B, D_MODEL), F32),
                        pltpu.VMEM((tm, D_B), F32), pltpu.VMEM((tm, D_B), F32), pltpu.VMEM((tm, D_B), F32),
                        pltpu.VMEM((SGU_CHUNK, D_B), F32), pltpu.SemaphoreType.DMA((3,))],
        compiler_params=_params(58, dimension_semantics=("arbitrary",)),
    )(att, zrest, zrest, zrest, zrest, zrest, zrest, x, target, w_pa, w_pb, w_out, b_gate, final_g,
      ln_g, ln_b, w_s, b_s_t)


_DZ_MAP = ((0, 0), (1, 0), (2, 0), (3, 0), (4, 0), (4, 1), (3, 1), (3, 2), (3, 3), (3, 4), (3, 5))


def _dh_gradx(dq, dk, dv, dzt, dzs, w_in_bf, x, norm_g, d_out, tm=512, after=()):
    S = x.shape[0]

    def body(dq_ref, dk_ref, dv_ref, dzt_ref, dzs_ref, w_ref, x_ref, g_ref, dout_ref, gx_ref, gn_ref):
        i = pl.program_id(0)

        @pl.when(i == 0)
        def _():
            gn_ref[...] = jnp.zeros_like(gn_ref)

        pieces = (dq_ref, dk_ref, dv_ref, dzt_ref, dzs_ref)
        dh = jnp.zeros((tm, D_MODEL), F32)
        for j, (pc, blk) in enumerate(_DZ_MAP):
            dh += _dot_nt(pieces[pc][:, blk * 512:(blk + 1) * 512], w_ref[:, j * 512:(j + 1) * 512])
        xv = x_ref[...]
        r = lax.rsqrt(jnp.mean(xv * xv, axis=-1, keepdims=True) + EPS)
        nrm = xv * r
        gn_ref[...] += jnp.sum(dh * nrm, axis=0, keepdims=True)
        dn = dh * g_ref[...]
        gx_ref[...] = r * (dn - nrm * jnp.mean(dn * nrm, axis=-1, keepdims=True)) + dout_ref[...]

    row = lambda w: pl.BlockSpec((tm, w), lambda i: (i, 0))
    c2 = lambda i: (0, 0)
    return pl.pallas_call(
        _after(body, 9, after), name="dh_gradx", grid=(S // tm,),
        out_shape=(jax.ShapeDtypeStruct((S, D_MODEL), F32), jax.ShapeDtypeStruct((1, D_MODEL), F32)),
        in_specs=[row(512), row(512), row(512), row(3072), row(1024),
                  pl.BlockSpec((D_MODEL, D_IN), c2, pipeline_mode=pl.Buffered(1)), row(D_MODEL),
                  pl.BlockSpec((1, D_MODEL), c2), row(D_MODEL)]
        + [_ANY] * len(after),
        out_specs=[row(D_MODEL), pl.BlockSpec((1, D_MODEL), c2)],
        compiler_params=_params(48, dimension_semantics=("arbitrary",)),
    )(dq, dk, dv, dzt, dzs, w_in_bf, x, norm_g, d_out, *after)


def _gw_in(ht, dq, dk, dv, dzt, dzs, tn=512, after=()):
    S = ht.shape[1]
    per = 512 // tn
    cols = tuple((pc, per * blk + h) for pc, blk in _DZ_MAP for h in range(per))

    def body(ht_ref, dq_ref, dk_ref, dv_ref, dzt_ref, dzs_ref, o_ref, ob_ref):
        j = pl.program_id(0)
        pieces = (dq_ref, dk_ref, dv_ref, dzt_ref, dzs_ref)
        for pc in range(5):
            hit = functools.reduce(jnp.logical_or, [j == jj for jj, (p, _) in enumerate(cols) if p == pc])

            @pl.when(hit)
            def _(pc=pc):
                g = _dot(ht_ref[...], pieces[pc][...])
                o_ref[...] = g
                ob_ref[...] = g.astype(BF)

    def piece_spec(pc):
        cur = next(blk for p, blk in cols if p == pc)
        held = []
        for p, blk in cols:
            cur = blk if p == pc else cur
            held.append(cur)

        def index_map(j):
            blk = jnp.int32(held[0])
            for jj in range(1, len(held)):
                if held[jj] != held[jj - 1]:
                    blk = jnp.where(j >= jj, jnp.int32(held[jj]), blk)
            return (0, blk)

        return pl.BlockSpec((S, tn), index_map)

    return pl.pallas_call(
        _after(body, 6, after), name="gw_in", grid=(len(cols),),
        out_shape=(jax.ShapeDtypeStruct((D_MODEL, D_IN), F32), jax.ShapeDtypeStruct((D_MODEL, D_IN), BF)),
        in_specs=[pl.BlockSpec((D_MODEL, S), lambda j: (0, 0), pipeline_mode=pl.Buffered(1))]
        + [piece_spec(pc) for pc in range(5)]
        + [_ANY] * len(after),
        out_specs=[pl.BlockSpec((D_MODEL, tn), lambda j: (0, j)), pl.BlockSpec((D_MODEL, tn), lambda j: (0, j))],
        compiler_params=_params(56, dimension_semantics=("arbitrary",)),
    )(ht, dq, dk, dv, dzt, dzs, *after)


_HBM = pl.BlockSpec(memory_space=pltpu.HBM)
_SEM = pl.BlockSpec(memory_space=pltpu.SEMAPHORE)
_ANY = pl.BlockSpec(memory_space=pl.ANY)
_EFFECT = pltpu.SideEffectType.DATAFLOW_SIDE_EFFECTING


def _in_hbm(a):
    return pltpu.with_memory_space_constraint(a, pltpu.HBM)


def _after(body, n_in, after):
    if not after:
        return body
    return lambda *refs: body(*refs[:n_in], *refs[n_in + len(after):])


class _Started:
    def __init__(self, send, recv, bufs, token):
        self.send, self.recv, self.bufs, self.token = send, recv, bufs, token


def _split_start(name, bufs, n_copies, copies, after=()):
    nb = len(bufs)

    def body(*refs):
        refs = refs[:nb] + refs[nb + len(after):]
        for cp in copies(refs[:nb], refs[nb], refs[nb + 1]):
            cp.start()
        refs[-1][...] = jnp.zeros_like(refs[-1])

    outs = pl.pallas_call(
        body, name=name,
        out_shape=(pltpu.SemaphoreType.DMA((n_copies,)), pltpu.SemaphoreType.DMA((n_copies,)),
                   *[pltpu.HBM(b.shape, b.dtype) for b in bufs], jax.ShapeDtypeStruct((8, 128), F32)),
        in_specs=[_HBM] * nb + [_ANY] * len(after),
        out_specs=(_SEM, _SEM, *[_HBM] * nb, pl.BlockSpec(memory_space=pltpu.VMEM)),
        input_output_aliases={k: 2 + k for k in range(nb)},
        compiler_params=_params(1, has_side_effects=_EFFECT),
    )(*[_in_hbm(b) for b in bufs], *after)
    return _Started(outs[0], outs[1], list(outs[2:2 + nb]), outs[-1])


def _split_wait(name, started, copies, after):
    nb = len(started.bufs)

    def body(*refs):
        for cp in copies(refs[:nb], refs[nb], refs[nb + 1]):
            cp.wait_send()
            cp.wait_recv()

    return list(pl.pallas_call(
        body, name=name,
        out_shape=tuple(pltpu.HBM(b.shape, b.dtype) for b in started.bufs),
        in_specs=[_HBM] * nb + [_SEM, _SEM, _ANY],
        out_specs=tuple([_HBM] * nb),
        input_output_aliases={k: k for k in range(nb)},
        compiler_params=_params(1, has_side_effects=_EFFECT),
    )(*started.bufs, started.send, started.recv, after))


def _x1_copies(ws):
    def copies(refs, send_sems, recv_sems):
        x, y, c, _ = _mesh_pos()
        out = []
        for k, w in enumerate(ws):
            for s in range(N_SHARD):
                out.append(pltpu.make_async_remote_copy(
                    src_ref=_UNITS[w](refs[k], s, 1 - c), dst_ref=refs[len(ws) + k].at[s],
                    send_sem=send_sems.at[N_SHARD * k + s], recv_sem=recv_sems.at[N_SHARD * k + s],
                    device_id=(x, y, 1 - c), device_id_type=MESH))
        return out
    return copies


def _x2_copies(n):
    def copies(refs, send_sems, recv_sems):
        x, y, c, chips = _mesh_pos()
        out = []
        for j, (cx, cy) in enumerate(chips):
            for k in range(n):
                out.append(pltpu.make_async_remote_copy(
                    src_ref=refs[k].at[2 * cx + cy], dst_ref=refs[n + k].at[j],
                    send_sem=send_sems.at[3 * k + j], recv_sem=recv_sems.at[3 * k + j],
                    device_id=(cx, cy, c), device_id_type=MESH))
        return out
    return copies


def _x3_copies(ws):
    def copies(refs, send_sems, recv_sems):
        x, y, c, _ = _mesh_pos()
        out = []
        for k, w in enumerate(ws):
            rows = _HALF_ROWS[w]
            mine = refs[k].at[pl.ds(_mo(c * rows, rows), rows), :]
            out.append(pltpu.make_async_remote_copy(
                src_ref=mine, dst_ref=mine, send_sem=send_sems.at[k], recv_sem=recv_sems.at[k],
                device_id=(x, y, 1 - c), device_id_type=MESH))
        return out
    return copies


def _x1_lands(ws, dtype=F32):
    return [lax.empty((N_SHARD,) + _UNIT_SHAPES[w], dtype) for w in ws]


def _x2_lands(ws):
    return [lax.empty((3,) + _UNIT_SHAPES[w], BF) for w in ws]


def _grad_add1(w, g, recv, pos):
    ur, uc = _UNIT_SHAPES[w]

    def body(pos_ref, g_ref, r_ref, own_ref, csb_ref):
        v = g_ref[...] + r_ref[0].astype(F32)
        csb_ref[0] = v.astype(BF)

        @pl.when(pl.program_id(0) == pos_ref[1])
        def _():
            own_ref[...] = v

    u3 = lambda s, pos: (s, 0, 0)
    return pl.pallas_call(
        body, name=f"grad_add1_{w}",
        grid_spec=pltpu.PrefetchScalarGridSpec(
            num_scalar_prefetch=1, grid=(N_SHARD,),
            in_specs=[pl.BlockSpec((ur, uc), lambda s, pos: (pos[0], s)), pl.BlockSpec((1, ur, uc), u3)],
            out_specs=[pl.BlockSpec((ur, uc), lambda s, pos: (0, 0)), pl.BlockSpec((1, ur, uc), u3)]),
        out_shape=(jax.ShapeDtypeStruct((ur, uc), F32), jax.ShapeDtypeStruct((N_SHARD, ur, uc), BF)),
        compiler_params=_params(40, dimension_semantics=("arbitrary",)),
    )(pos, g, recv)


def _grad_add1_group(ws, gs, recvs):
    n = len(ws)

    def body(*refs):
        c = lax.axis_index("c")
        for k, w in enumerate(ws):
            g, r, cs, csb = refs[k], refs[n + k], refs[2 * n + k], refs[3 * n + k]
            for s in range(N_SHARD):
                v = _UNITS[w](g, s, c)[...] + r[s]
                cs[s] = v
                csb[s] = v.astype(BF)

    vm = pl.BlockSpec(memory_space=pltpu.VMEM)
    outs = pl.pallas_call(
        body, name="grad_add1_group",
        out_shape=tuple(jax.ShapeDtypeStruct((N_SHARD,) + _UNIT_SHAPES[w], dt) for dt in (F32, BF) for w in ws),
        in_specs=[vm] * (2 * n), out_specs=[vm] * (2 * n),
        compiler_params=_params(32),
    )(*gs, *recvs)
    return list(outs[:n]), list(outs[n:])


def _grad_add2_group(ws, css, recvs):
    n = len(ws)

    def body(*refs):
        x, y, c, _ = _mesh_pos()
        for k, w in enumerate(ws):
            cs, r, o = refs[k], refs[n + k], refs[2 * n + k]
            rows = _HALF_ROWS[w]
            total = ((cs[2 * x + y] + r[0].astype(F32)) + r[1].astype(F32)) + r[2].astype(F32)
            o[pl.ds(_mo(c * rows, rows), rows), :] = total

    vm = pl.BlockSpec(memory_space=pltpu.VMEM)
    return list(pl.pallas_call(
        body, name="grad_add2_group",
        out_shape=tuple(jax.ShapeDtypeStruct(_SHARD_SHAPES[w], F32) for w in ws),
        in_specs=[vm] * (2 * n), out_specs=[vm] * n,
        compiler_params=_params(32),
    )(*css, *recvs))


def _grad_add2(w, own, recv, pos):
    ur, uc = _UNIT_SHAPES[w]
    nt = 4
    tr = ur // nt

    def body(pos_ref, own_ref, r_ref, o_ref):
        o_ref[...] = ((own_ref[...] + r_ref[0].astype(F32)) + r_ref[1].astype(F32)) + r_ref[2].astype(F32)

    return pl.pallas_call(
        body, name=f"grad_add2_{w}",
        grid_spec=pltpu.PrefetchScalarGridSpec(
            num_scalar_prefetch=1, grid=(nt,),
            in_specs=[pl.BlockSpec((tr, uc), lambda t, pos: (t, 0)),
                      pl.BlockSpec((3, tr, uc), lambda t, pos: (0, t, 0))],
            out_specs=pl.BlockSpec((tr, uc), lambda t, pos: (pos[0] * nt + t, 0))),
        out_shape=jax.ShapeDtypeStruct(_SHARD_SHAPES[w], F32),
        compiler_params=_params(32, dimension_semantics=("arbitrary",)),
    )(pos, own, recv)


def _adamw_math(w, g, m, v):
    m = ADAM_B1 * m + (1.0 - ADAM_B1) * g
    v = ADAM_B2 * v + (1.0 - ADAM_B2) * (g * g)
    m_hat = m / ADAM_C1
    v_hat = v / ADAM_C2
    delta = -ADAM_LR * (m_hat / (jnp.sqrt(v_hat) + ADAM_EPS) + ADAM_WD * w)
    return delta, m, v


def _adamw_group(ws_, gs, ms, vs, after=()):
    n = len(ws_)

    def body(*refs):
        for k in range(n):
            w, g, m, v = (refs[j * n + k] for j in range(4))
            d, nm, nv, gc = (refs[(4 + j) * n + k] for j in range(4))
            gv = g[...]
            d[...], nm[...], nv[...] = _adamw_math(w[...], gv, m[...], v[...])
            gc[...] = gv

    vm = pl.BlockSpec(memory_space=pltpu.VMEM)
    outs = pl.pallas_call(
        _after(body, 4 * n, after), name="adamw_group",
        out_shape=tuple(jax.ShapeDtypeStruct(a.shape, F32) for _ in range(4) for a in ws_),
        in_specs=[vm] * (4 * n) + [_ANY] * len(after), out_specs=[vm] * (4 * n),
        compiler_params=_params(32),
    )(*ws_, *gs, *ms, *vs, *after)
    return [tuple(outs[j * n + k] for j in range(4)) for k in range(n)]


def _adamw(name, w, g, m, v, tr=256, after=()):
    rows, cols = w.shape

    def body(w_ref, g_ref, m_ref, v_ref, d_ref, nm_ref, nv_ref, gc_ref):
        gv = g_ref[...]
        d_ref[...], nm_ref[...], nv_ref[...] = _adamw_math(w_ref[...], gv, m_ref[...], v_ref[...])
        gc_ref[...] = gv

    spec = pl.BlockSpec((tr, cols), lambda i: (i, 0))
    return pl.pallas_call(
        _after(body, 4, after), name=name, grid=(rows // tr,),
        out_shape=tuple(jax.ShapeDtypeStruct((rows, cols), F32) for _ in range(4)),
        in_specs=[spec] * 4 + [_ANY] * len(after), out_specs=[spec] * 4,
        compiler_params=_params(32, dimension_semantics=("arbitrary",)),
    )(w, g, m, v, *after)


_REL_PAD = 384
_VEC_FIELDS = (("norm_g", 0, D_MODEL), ("b_gate", 1024, 2 * D_MODEL), ("sgu_ln_g", 3072, D_B),
               ("sgu_ln_b", 3584, D_B), ("b_s", 4096, N_GROUPS * 128), ("final_g", 4608, D_MODEL))
_LOSS_OFF = 5632
_REL_OFF = 5760
_NV = _REL_OFF + N_HEADS * _REL_PAD
_N_FIELDS = len(_VEC_FIELDS) + 2


_B_S_FIELD = [f[0] for f in _VEC_FIELDS].index("b_s")


def _assemble_row(dst, fields, transposed_b_s):
    for f, (_, off, n) in enumerate(_VEC_FIELDS):
        if transposed_b_s and f == _B_S_FIELD:
            t = fields[f][...].T
            for g in range(N_GROUPS):
                dst[:, off + 128 * g:off + 128 * (g + 1)] = t[g:g + 1, :]
        else:
            dst[:, off:off + n] = fields[f][...]
    for r in range(N_HEADS):
        dst[:, _REL_OFF + _REL_PAD * r:_REL_OFF + _REL_PAD * (r + 1)] = fields[len(_VEC_FIELDS)][r:r + 1, :]


def _small_reduce(grads, loss_row, after=()):
    n_in = _N_FIELDS + 1

    def body(*refs):
        g_refs, loss_ref = refs[:_N_FIELDS], refs[_N_FIELDS]
        out_v, out_w = refs[n_in:n_in + 2]
        mine_v, gath_v, gath_w, send_sems, recv_sems = refs[n_in + 2:]
        x, y, c, chips = _mesh_pos()
        me, sibling = (x, y, c), (x, y, 1 - c)

        _assemble_row(mine_v, g_refs, True)
        mine_v[:, _LOSS_OFF:_LOSS_OFF + 128] = loss_ref[...]
        mine_w = g_refs[-1]
        my_k = 4 * x + 2 * y + c
        gath_v[my_k] = mine_v[...]
        gath_w[my_k] = mine_w[...]

        def copy(k, gath, block, to, src=None):
            dst = gath.at[4 * block[0] + 2 * block[1] + block[2]]
            return pltpu.make_async_remote_copy(
                src_ref=dst if src is None else src, dst_ref=dst,
                send_sem=send_sems.at[k], recv_sem=recv_sems.at[k], device_id=to, device_id_type=MESH)

        bufs = ((gath_v, mine_v), (gath_w, mine_w))
        first, passed = [], []
        for b, (gath, mine) in enumerate(bufs):
            first.append(copy(7 * b, gath, me, sibling, src=mine))
            first += [copy(7 * b + 1 + j, gath, me, (*chip, c), src=mine) for j, chip in enumerate(chips)]
        for cp in first:
            cp.start()
        for b, (gath, _) in enumerate(bufs):
            for j, chip in enumerate(chips):
                copy(7 * b + 1 + j, gath, (*chip, c), me).wait_recv()
                cp = copy(7 * b + 4 + j, gath, (*chip, c), sibling)
                cp.start()
                passed.append(cp)
        for b, (gath, _) in enumerate(bufs):
            copy(7 * b, gath, sibling, me).wait_recv()
            for j, chip in enumerate(chips):
                copy(7 * b + 4 + j, gath, (*chip, 1 - c), me).wait_recv()
        for cp in first + passed:
            cp.wait_send()

        tot_v, tot_w = gath_v[0], gath_w[0]
        for k in range(1, 8):
            tot_v = tot_v + gath_v[k]
            tot_w = tot_w + gath_w[k]
        out_v[...] = tot_v
        out_w[...] = tot_w

    vm = pl.BlockSpec(memory_space=pltpu.VMEM)
    return pl.pallas_call(
        _after(body, n_in, after), name="small_reduce",
        out_shape=(jax.ShapeDtypeStruct((1, _NV), F32), jax.ShapeDtypeStruct((N_GROUPS * 128, 128), F32)),
        in_specs=[vm] * n_in + [_ANY] * len(after), out_specs=[vm] * 2,
        scratch_shapes=[pltpu.VMEM((1, _NV), F32), pltpu.VMEM((8, 1, _NV), F32),
                        pltpu.VMEM((8, N_GROUPS * 128, 128), F32),
                        pltpu.SemaphoreType.DMA((14,)), pltpu.SemaphoreType.DMA((14,))],
        compiler_params=_params(32),
    )(*grads, loss_row, *after)


def _small_adamw(tot_v, tot_w, params):
    n_in = 2 + 3 * _N_FIELDS

    def body(*refs):
        tv_ref, tw_ref = refs[:2]
        p_refs = [refs[2 + k * _N_FIELDS:2 + (k + 1) * _N_FIELDS] for k in range(3)]
        outs = refs[n_in:n_in + 4 * _N_FIELDS + 1]
        wmv = refs[-1]
        for k in range(3):
            _assemble_row(wmv.at[k], p_refs[k], False)
            wmv[k, :, _LOSS_OFF:_LOSS_OFF + 128] = jnp.zeros((1, 128), F32)
        tot_v, tot_w = tv_ref[...], tw_ref[...]
        res_v = (tot_v,) + _adamw_math(wmv[0], tot_v, wmv[1], wmv[2])
        res_w = (tot_w,) + _adamw_math(p_refs[0][-1][...], tot_w, p_refs[1][-1][...], p_refs[2][-1][...])
        for kind in range(4):
            o = outs[kind * _N_FIELDS:(kind + 1) * _N_FIELDS]
            for f, (_, off, n) in enumerate(_VEC_FIELDS):
                o[f][...] = res_v[kind][:, off:off + n]
            for r in range(N_HEADS):
                o[len(_VEC_FIELDS)][r:r + 1, :] = res_v[kind][:, _REL_OFF + _REL_PAD * r:_REL_OFF + _REL_PAD * (r + 1)]
            o[-1][...] = res_w[kind]
        outs[-1][...] = tot_v[:, _LOSS_OFF:_LOSS_OFF + 128]

    field_shapes = [(1, n) for _, _, n in _VEC_FIELDS] + [(N_HEADS, _REL_PAD), (N_GROUPS * 128, 128)]
    vm = pl.BlockSpec(memory_space=pltpu.VMEM)
    operands = [tot_v, tot_w] + [a for p in params for a in p]
    assert len(operands) == n_in
    outs = pl.pallas_call(
        body, name="small_adamw",
        out_shape=tuple(jax.ShapeDtypeStruct(s, F32) for _ in range(4) for s in field_shapes)
        + (jax.ShapeDtypeStruct((1, 128), F32),),
        in_specs=[vm] * n_in, out_specs=[vm] * (4 * _N_FIELDS + 1),
        scratch_shapes=[pltpu.VMEM((3, 1, _NV), F32)],
        compiler_params=_params(32),
    )(*operands)
    return [outs[k * _N_FIELDS:(k + 1) * _N_FIELDS] for k in range(4)], outs[-1]


def _small_fields(norm_g, b_gate, ln_g, ln_b, b_s, final_g, rel_bias, w_s):
    rel = jnp.pad(rel_bias.reshape(N_HEADS, N_REL), ((0, 0), (0, _REL_PAD - N_REL)))
    return (norm_g, b_gate, ln_g, ln_b, b_s.reshape(1, N_GROUPS * 128), final_g.reshape(1, D_MODEL),
            rel, w_s.reshape(N_GROUPS * 128, 128))


def _small_outputs(fields):
    n_g, b_g, l_g, l_b, b_s, f_g, rel, w_s = fields
    return (n_g, b_g, rel[:, :N_REL].reshape(1, N_HEADS, N_REL), l_g, l_b,
            w_s.reshape(1, N_GROUPS, 128, 128), b_s.reshape(1, N_GROUPS, 128), f_g.reshape(D_MODEL))


def _bias_row(rel_bias):
    hi = rel_bias[:, N_REL - 1:N_REL]
    lo = rel_bias[:, 0:1]
    return jnp.concatenate([jnp.broadcast_to(hi, (N_HEADS, 384)), rel_bias[:, ::-1],
                            jnp.broadcast_to(lo, (N_HEADS, 191)), jnp.broadcast_to(hi, (N_HEADS, 192))], axis=1)


def kernel(x, norm_g, w_in, b_gate, rel_bias, sgu_ln_g, sgu_ln_b, w_s, b_s, w_pa, w_pb, w_out, final_g, loss_target, m_norm_g, m_w_in, m_b_gate, m_rel_bias, m_sgu_ln_g, m_sgu_ln_b, m_w_s, m_b_s, m_w_pa, m_w_pb, m_w_out, m_final_g, v_norm_g, v_w_in, v_b_gate, v_rel_bias, v_sgu_ln_g, v_sgu_ln_b, v_w_s, v_b_s, v_w_pa, v_w_pb, v_w_out, v_final_g):
    S = x.shape[1]
    xs = x.reshape(S, D_MODEL)
    tgt = loss_target.reshape(S, D_MODEL)
    big_w = (w_in[0], w_pa[0], w_pb[0], w_out[0])
    big_m = (m_w_in[0], m_w_pa[0], m_w_pb[0], m_w_out[0])
    big_v = (v_w_in[0], v_w_pa[0], v_w_pb[0], v_w_out[0])
    rel = rel_bias[0]
    ws = w_s[0]
    bst = b_s[0].T
    fg = final_g.reshape(1, D_MODEL)
    pos = jnp.stack([lax.axis_index("c"), 2 * lax.axis_index("x") + lax.axis_index("y")]).astype(jnp.int32)

    (w_in_bf,), staged, band_bias = _ag_weights((0,), big_w[:1], (1, 2, 3), big_w[1:], _bias_row(rel))
    ag_s = _split_start("ag_small_start", staged, 9, _gather_copies((1, 2, 3)), after=(w_in_bf,))

    ht, q3, k3, v3, zrest = _inproj_fwd(xs, norm_g, w_in_bf, after=(ag_s.token,))
    att, lse = _attn_fwd(q3, k3, v3, band_bias)
    w_pa_bf, w_pb_bf, w_out_bf = _split_wait("ag_small_wait", ag_s, _gather_copies((1, 2, 3)), att)
    (d_out, d_att, dzt, dzs, gw_out, gw_pa, gw_pb, g_bgate, g_final, loss_row,
     g_ws, g_bs_t, g_lng, g_lnb) = _tail_sgu(
        att, zrest, xs, tgt, w_pa_bf, w_pb_bf, w_out_bf, b_gate, fg, sgu_ln_g, sgu_ln_b, ws, bst)
    ws_s, ws_i = (1, 2, 3), (0,)

    x1s = _split_start("gx1s_start", [gw_pa, gw_pb, gw_out] + _x1_lands(ws_s), 12, _x1_copies(ws_s))
    dq, dk, dv, d_gp = _attn_bwd(q3, k3, v3, d_att, lse, band_bias, after=(x1s.token,))
    got = _split_wait("gx1s_wait", x1s, _x1_copies(ws_s), dq)
    cs_s, csb_s = _grad_add1_group(ws_s, got[:3], got[3:])

    x2s = _split_start("gx2s_start", csb_s + _x2_lands(ws_s), 9, _x2_copies(3))
    gw_in, gw_in_bf = _gw_in(ht, dq, dk, dv, dzt, dzs, after=(x2s.token,))
    x1i = _split_start("gx1i_start", [gw_in_bf] + _x1_lands(ws_i, BF), 4, _x1_copies(ws_i))
    got = _split_wait("gx2s_wait", x2s, _x2_copies(3), x1i.token)
    halves_s = _grad_add2_group(ws_s, cs_s, got[3:])
    x3s = _split_start("gx3s_start", halves_s, 3, _x3_copies(ws_s))
    got = _split_wait("gx1i_wait", x1i, _x1_copies(ws_i), x3s.token)
    sum_i = _grad_add1(0, gw_in, got[1], pos)

    x2i = _split_start("gx2i_start", [sum_i[1]] + _x2_lands(ws_i), 3, _x2_copies(1))
    grad_x, g_norm = _dh_gradx(dq, dk, dv, dzt, dzs, w_in_bf, xs, norm_g, d_out, after=(x2i.token,))
    g_shards_s = _split_wait("gx3s_wait", x3s, _x3_copies(ws_s), grad_x)
    got = _split_wait("gx2i_wait", x2i, _x2_copies(1), grad_x)
    half_i = _grad_add2(0, sum_i[0], got[1], pos)
    x3i = _split_start("gx3i_start", [half_i], 1, _x3_copies(ws_i))
    big = [None] * 4
    big[1:] = _adamw_group(big_w[1:], g_shards_s, big_m[1:], big_v[1:], after=(x3i.token,))

    g_rel = jnp.pad(d_gp[:, 384:384 + N_REL][:, ::-1], ((0, 0), (0, _REL_PAD - N_REL)))
    small_grads = (g_norm, g_bgate, g_lng, g_lnb, g_bs_t, g_final, g_rel, g_ws.reshape(N_GROUPS * 128, 128))
    small_params = (_small_fields(norm_g, b_gate, sgu_ln_g, sgu_ln_b, b_s, final_g, rel_bias, w_s),
                    _small_fields(m_norm_g, m_b_gate, m_sgu_ln_g, m_sgu_ln_b, m_b_s, m_final_g, m_rel_bias, m_w_s),
                    _small_fields(v_norm_g, v_b_gate, v_sgu_ln_g, v_sgu_ln_b, v_b_s, v_final_g, v_rel_bias, v_w_s))
    tot_v, tot_w = _small_reduce(small_grads, loss_row, after=(x3i.token,))
    (gsum, sdelta, sm, sv), loss_out = _small_adamw(tot_v, tot_w, small_params)

    g_shard_i, = _split_wait("gx3i_wait", x3i, _x3_copies(ws_i), loss_out)
    big[0] = _adamw("adamw_w_in", big_w[0], g_shard_i, big_m[0], big_v[0])
    sg_out, sd_out, sm_out, sv_out = (_small_outputs(f) for f in (gsum, sdelta, sm, sv))
    loss = loss_out[0, 0]

    def assemble(small, bigs):
        n_g, b_g, r_b, l_g, l_b, w_s_, b_s_, f_g = small
        b_in, b_pa, b_pb, b_out = (b[None] for b in bigs)
        return (n_g, b_in, b_g, r_b, l_g, l_b, w_s_, b_s_, b_pa, b_pb, b_out, f_g)

    grads_out = assemble(sg_out, [b[3] for b in big])
    delta_out = assemble(sd_out, [b[0] for b in big])
    m_out = assemble(sm_out, [b[1] for b in big])
    v_out = assemble(sv_out, [b[2] for b in big])
    return (loss, grad_x.reshape(1, S, D_MODEL), *grads_out, *delta_out, *m_out, *v_out)
```

```python
import functools
import math

import jax
import jax.numpy as jnp
from jax import lax
from jax.experimental import pallas as pl
from jax.experimental.pallas import tpu as pltpu

F32 = jnp.float32
BF = jnp.bfloat16
MESH = pl.DeviceIdType.MESH

D_MODEL = 1024
D_A = 512
D_B = 512
D_IN = 5632
N_HEADS = 8
HEAD_DIM = 64
CHUNK = 64
N_PREV = 8
SGU_CHUNK = 128
N_GROUPS = 4
N_REL = 257
EPS = 1e-6
NEG_INF = -1e30
SCALE = HEAD_DIM ** -0.5

QB = 2 * CHUNK
KB = (N_PREV + 2) * CHUNK
PADK = N_PREV * CHUNK
ROLL_W = 1024
KEEP = KB // QB - 1
Q_PER_STEP = 2

ADAM_LR = 0.001
ADAM_B1 = 0.9
ADAM_B2 = 0.999
ADAM_EPS = 1e-08
ADAM_WD = 0.01
ADAM_STEP = 10
ADAM_C1 = 1.0 - ADAM_B1 ** ADAM_STEP
ADAM_C2 = 1.0 - ADAM_B2 ** ADAM_STEP

N_SHARD = 4
SHARD_IN = D_IN // N_SHARD
MIB = 1024 * 1024


V7X_VMEM_MIB = 64
VMEM_RESERVE_MIB = V7X_VMEM_MIB - 4


def _params(vmem_mib, **kw):
    assert vmem_mib <= VMEM_RESERVE_MIB
    return pltpu.CompilerParams(vmem_limit_bytes=VMEM_RESERVE_MIB * MIB, **kw)


def _sigmoid(x):
    return 1.0 / (1.0 + jnp.exp(-x))


def _silu_and_grad(x):
    s = _sigmoid(x)
    return x * s, s * (1.0 + x * (1.0 - s))


_GELU_C = math.sqrt(2.0 / math.pi)
_GELU_A = 0.044715


def _gelu_and_grad(x):
    x2 = x * x
    t = jnp.tanh(_GELU_C * (x + _GELU_A * (x2 * x)))
    cdf = 0.5 * (1.0 + t)
    grad = cdf + 0.5 * x * (1.0 - t * t) * (_GELU_C * (1.0 + 3.0 * _GELU_A * x2))
    return x * cdf, grad


def _dot(a, b):
    return jnp.dot(a, b, preferred_element_type=F32)


def _dot_nt(a, b):
    return lax.dot_general(a, b, (((1,), (1,)), ((), ())), preferred_element_type=F32)


def _dot_tn(a, b):
    return lax.dot_general(a, b, (((0,), (0,)), ((), ())), preferred_element_type=F32)


def _mo(v, m):
    return v if isinstance(v, int) else pl.multiple_of(v, m)


def _unit_in(ref, s, p):
    return ref.at[pl.ds(_mo(p * 512, 512), 512), pl.ds(_mo(s * SHARD_IN, 128), SHARD_IN)]


def _unit_p(ref, s, p):
    return ref.at[pl.ds(_mo(p * 256, 256), 256), pl.ds(_mo(s * 256, 128), 256)]


def _unit_out(ref, s, p):
    return ref.at[pl.ds(_mo(s * 256 + p * 128, 128), 128), :]


_UNITS = (_unit_in, _unit_p, _unit_p, _unit_out)
_HALF_ROWS = (512, 256, 256, 128)
_UNIT_SHAPES = ((512, SHARD_IN), (256, 256), (256, 256), (128, D_MODEL))
_FULL_SHAPES = ((D_MODEL, D_IN), (D_A, D_MODEL), (D_B, D_MODEL), (D_MODEL, D_MODEL))
_SHARD_SHAPES = ((D_MODEL, SHARD_IN), (D_A, 256), (D_B, 256), (256, D_MODEL))


def _mesh_pos():
    x, y, c = lax.axis_index("x"), lax.axis_index("y"), lax.axis_index("c")
    chips = [(1 - x, y), (x, 1 - y), (1 - x, 1 - y)]
    return x, y, c, chips


def _ag_weights(ws, shards, later_ws, later_shards, gp):
    n, m = len(ws), len(later_ws)

    def body(*refs):
        ins, later_ins, gp_ref = refs[:n], refs[n:n + m], refs[n + m]
        o = n + m + 1
        outs, later_outs, bias_ref = refs[o:o + n], refs[o + n:o + n + m], refs[o + n + m]
        o += n + m + 1
        stage, later_stage = refs[o:o + n], refs[o + n:o + n + m]
        send_sems, recv_sems, local_sems, later_sems = refs[o + n + m:]
        x, y, c, chips = _mesh_pos()
        s_me = 2 * x + y
        sibling = (x, y, 1 - c)
        def rows_of(k, p):
            rows = _HALF_ROWS[ws[k]]
            return pl.ds(_mo(p * rows, rows), rows)

        def half(k, p):
            return stage[k].at[rows_of(k, p), :]

        def unit(k, s, p):
            return _UNITS[ws[k]](outs[k], s, p)

        def rcopy(k, i, src, dst, to):
            return pltpu.make_async_remote_copy(src_ref=src, dst_ref=dst, send_sem=send_sems.at[k, i],
                                                recv_sem=recv_sems.at[k, i], device_id=to, device_id_type=MESH)

        for k in range(n):
            stage[k][rows_of(k, c), :] = ins[k][rows_of(k, c), :].astype(BF)
        sends = []
        for j, (cx, cy) in enumerate(chips):
            for k in range(n):
                cp = rcopy(k, j, half(k, c), unit(k, s_me, c), (cx, cy, c))
                cp.start()
                sends.append(cp)
        for k in range(n):
            stage[k][rows_of(k, 1 - c), :] = ins[k][rows_of(k, 1 - c), :].astype(BF)
        local = []
        for k in range(n):
            for p in range(2):
                cp = pltpu.make_async_copy(half(k, p), unit(k, s_me, p), local_sems.at[k, p])
                cp.start()
                local.append(cp)
        for k, w in enumerate(later_ws):
            later_stage[k][...] = later_ins[k][...].astype(BF)
            cp = pltpu.make_async_copy(later_stage[k], _shard_of(later_outs[k], w, s_me), later_sems.at[k])
            cp.start()
            local.append(cp)
        keep = _struct_mask()
        for h in range(N_HEADS):
            bias_ref[h] = jnp.where(keep, _skew_table(gp_ref[h:h + 1, :])[:, :KB], NEG_INF)
        for j, (cx, cy) in enumerate(chips):
            for k in range(n):
                landed = unit(k, 2 * cx + cy, c)
                rcopy(k, j, landed, landed, (cx, cy, c)).wait_recv()
                cp = rcopy(k, 3 + j, landed, landed, sibling)
                cp.start()
                sends.append(cp)
        for j, (cx, cy) in enumerate(chips):
            for k in range(n):
                other = unit(k, 2 * cx + cy, 1 - c)
                rcopy(k, 3 + j, other, other, sibling).wait_recv()
        for cp in sends:
            cp.wait_send()
        for cp in local:
            cp.wait()

    vm = pl.BlockSpec(memory_space=pltpu.VMEM)
    outs = pl.pallas_call(
        body, name="ag_weights",
        out_shape=tuple(jax.ShapeDtypeStruct(_FULL_SHAPES[w], BF) for w in tuple(ws) + tuple(later_ws))
        + (jax.ShapeDtypeStruct((N_HEADS, QB, KB), F32),),
        in_specs=[vm] * (n + m + 1), out_specs=[_ANY] * (n + m) + [vm],
        scratch_shapes=[pltpu.VMEM(_SHARD_SHAPES[w], BF) for w in tuple(ws) + tuple(later_ws)]
        + [pltpu.SemaphoreType.DMA((n, 6)), pltpu.SemaphoreType.DMA((n, 6)), pltpu.SemaphoreType.DMA((n, 2)),
           pltpu.SemaphoreType.DMA((m,))],
        compiler_params=_params(48),
    )(*shards, *later_shards, gp)
    return list(outs[:n]), list(outs[n:n + m]), outs[-1]


def _shard_of(ref, w, s):
    if w == 0:
        return ref.at[:, pl.ds(_mo(s * SHARD_IN, 128), SHARD_IN)]
    if w == 3:
        return ref.at[pl.ds(_mo(s * 256, 256), 256), :]
    return ref.at[:, pl.ds(_mo(s * 256, 128), 256)]


def _gather_copies(ws):
    def copies(refs, send_sems, recv_sems):
        x, y, c, chips = _mesh_pos()
        out = []
        for j, (cx, cy) in enumerate(chips):
            for k, w in enumerate(ws):
                mine = _shard_of(refs[k], w, 2 * x + y)
                out.append(pltpu.make_async_remote_copy(
                    src_ref=mine, dst_ref=mine, send_sem=send_sems.at[3 * k + j], recv_sem=recv_sems.at[3 * k + j],
                    device_id=(cx, cy, c), device_id_type=MESH))
        return out
    return copies


def _inproj_fwd(x, norm_g, w_in_bf, tm=512, after=()):
    S = x.shape[0]

    def body(x_ref, g_ref, w_ref, ht_ref, q_ref, k_ref, v_ref, zr_ref):
        xv = x_ref[...]
        r = lax.rsqrt(jnp.mean(xv * xv, axis=-1, keepdims=True) + EPS)
        hf = (xv * r) * g_ref[...]
        ht_ref[...] = hf.T.astype(BF)
        h = hf.astype(BF)
        heads = (q_ref, k_ref, v_ref)
        for j in range(D_IN // 512):
            z = _dot(h, w_ref[:, j * 512:(j + 1) * 512])
            if j < 3:
                zb = z.astype(BF)
                for hd in range(N_HEADS):
                    heads[j][hd] = zb[:, hd * HEAD_DIM:(hd + 1) * HEAD_DIM]
            else:
                zr_ref[:, (j - 3) * 512:(j - 2) * 512] = z

    head_major = jax.ShapeDtypeStruct((N_HEADS, S, HEAD_DIM), BF)
    head_spec = pl.BlockSpec((N_HEADS, tm, HEAD_DIM), lambda i: (0, i, 0))
    return pl.pallas_call(
        _after(body, 3, after), name="inproj_fwd", grid=(S // tm,),
        out_shape=(jax.ShapeDtypeStruct((D_MODEL, S), BF), head_major, head_major, head_major,
                   jax.ShapeDtypeStruct((S, D_IN - 3 * D_A), F32)),
        in_specs=[pl.BlockSpec((tm, D_MODEL), lambda i: (i, 0)),
                  pl.BlockSpec((1, D_MODEL), lambda i: (0, 0)),
                  pl.BlockSpec((D_MODEL, D_IN), lambda i: (0, 0), pipeline_mode=pl.Buffered(1))]
        + [_ANY] * len(after),
        out_specs=[pl.BlockSpec((D_MODEL, tm), lambda i: (0, i)),
                   head_spec, head_spec, head_spec,
                   pl.BlockSpec((tm, D_IN - 3 * D_A), lambda i: (i, 0))],
        compiler_params=_params(52, dimension_semantics=("arbitrary",)),
    )(x, norm_g, w_in_bf, *after)


def _skew_table(gp_row):
    row = lax.broadcasted_iota(jnp.int32, (QB, ROLL_W), 0)
    t = jnp.broadcast_to(gp_row, (QB, ROLL_W))
    for b in range(7):
        t = jnp.where(((row >> b) & 1) == 1, pltpu.roll(t, 1 << b, axis=1), t)
    return t


def _unskew_sum(d):
    row = lax.broadcasted_iota(jnp.int32, (QB, ROLL_W), 0)
    for b in range(7):
        d = jnp.where(((row >> b) & 1) == 1, pltpu.roll(d, ROLL_W - (1 << b), axis=1), d)
    return jnp.sum(d, axis=0, keepdims=True)


def _struct_mask():
    a = lax.broadcasted_iota(jnp.int32, (QB, KB), 0) // CHUNK
    b = lax.broadcasted_iota(jnp.int32, (QB, KB), 1) // CHUNK
    return (b >= a) & (b <= a + N_PREV)


def _load_kv(k_hbm, v_hbm, k_scr, v_scr, sems, S, meanwhile=lambda: None):
    zeros = jnp.zeros((N_HEADS, PADK, HEAD_DIM), BF)
    k_scr[:, 0:PADK, :] = zeros
    v_scr[:, 0:PADK, :] = zeros
    ck = pltpu.make_async_copy(k_hbm, k_scr.at[:, pl.ds(PADK, S), :], sems.at[0])
    cv = pltpu.make_async_copy(v_hbm, v_scr.at[:, pl.ds(PADK, S), :], sems.at[1])
    ck.start()
    cv.start()
    meanwhile()
    ck.wait()
    cv.wait()


_BATCH_NT = (((2,), (2,)), ((0,), (0,)))
_BATCH_NN = (((2,), (1,)), ((0,), (0,)))
_BATCH_TN = (((1,), (1,)), ((0,), (0,)))


def _bdot(a, b, dims):
    return lax.dot_general(a, b, dims, preferred_element_type=F32)


def _scaled(q):
    return q * jnp.asarray(SCALE, BF)


def _scores(qs, kb, bias, i, front):
    s = _bdot(qs, kb, _BATCH_NT) + bias
    if front:
        col = lax.broadcasted_iota(jnp.int32, (1, 1, KB), 2)
        s = jnp.where(col >= PADK - i * QB, s, NEG_INF)
    return s


def _attn_fwd(q3, k3, v3, bias):
    S = q3.shape[1]

    def body(q_ref, k_hbm, v_hbm, bias_ref, o_ref, lse_ref, k_scr, v_scr, sems):
        @pl.when(pl.program_id(0) == 0)
        def _():
            _load_kv(k_hbm, v_hbm, k_scr, v_scr, sems, S)

        def step(i, rows, front):
            start = pl.multiple_of(i * QB, QB)
            kb = k_scr[:, pl.ds(start, KB), :]
            vb = v_scr[:, pl.ds(start, KB), :]
            s = _scores(_scaled(q_ref[:, rows, :]), kb, bias_ref[...], i, front)
            m = jnp.max(s, axis=-1, keepdims=True)
            e = jnp.exp(s - m)
            l = jnp.sum(e, axis=-1, keepdims=True)
            p = e * (1.0 / l)
            o = _bdot(p.astype(BF), vb, _BATCH_NN)
            lse_ref[:, rows, :] = jnp.broadcast_to(m + jnp.log(l), (N_HEADS, QB, 128))
            for h in range(N_HEADS):
                o_ref[rows, h * HEAD_DIM:(h + 1) * HEAD_DIM] = o[h]

        def block(j, carry):
            i = pl.program_id(0) * Q_PER_STEP + j
            rows = pl.ds(pl.multiple_of(j * QB, QB), QB)
            pl.when(i < KEEP)(functools.partial(step, i, rows, True))
            pl.when(i >= KEEP)(functools.partial(step, i, rows, False))
            return carry

        lax.fori_loop(0, Q_PER_STEP, block, 0)

    rows_per_step = Q_PER_STEP * QB
    kv_scr = pltpu.VMEM((N_HEADS, S + PADK, HEAD_DIM), BF)
    return pl.pallas_call(
        body, name="attn_fwd", grid=(S // rows_per_step,),
        out_shape=(jax.ShapeDtypeStruct((S, D_A), F32), jax.ShapeDtypeStruct((N_HEADS, S, 128), F32)),
        in_specs=[pl.BlockSpec((N_HEADS, rows_per_step, HEAD_DIM), lambda g: (0, g, 0)),
                  pl.BlockSpec(memory_space=pl.ANY), pl.BlockSpec(memory_space=pl.ANY),
                  pl.BlockSpec((N_HEADS, QB, KB), lambda g: (0, 0, 0))],
        out_specs=[pl.BlockSpec((rows_per_step, D_A), lambda g: (g, 0)),
                   pl.BlockSpec((N_HEADS, rows_per_step, 128), lambda g: (0, g, 0))],
        scratch_shapes=[kv_scr, kv_scr, pltpu.SemaphoreType.DMA((2,))],
        compiler_params=_params(48, dimension_semantics=("arbitrary",)),
    )(q3, k3, v3, bias)


def _attn_bwd(q3, k3, v3, d_att3, lse, bias, after=()):
    S = q3.shape[1]
    nq = S // QB

    def body(q_ref, do_ref, k_hbm, v_hbm, lse_ref, bias_ref, dq_ref, dk_ref, dv_ref, dgp_ref,
             k_scr, v_scr, dk_acc, dv_acc, dbias_acc, pad_scr, sems):
        @pl.when(pl.program_id(0) == 0)
        def _():
            def clear():
                dk_acc[...] = jnp.zeros_like(dk_acc)
                dv_acc[...] = jnp.zeros_like(dv_acc)
                dbias_acc[...] = jnp.zeros_like(dbias_acc)
            _load_kv(k_hbm, v_hbm, k_scr, v_scr, sems, S, clear)

        def step(i, rows, front):
            start = pl.multiple_of(i * QB, QB)
            kb = k_scr[:, pl.ds(start, KB), :]
            vb = v_scr[:, pl.ds(start, KB), :]
            qs = _scaled(q_ref[:, rows, :])
            do = do_ref[:, rows, :]
            p = jnp.exp(_scores(qs, kb, bias_ref[...], i, front) - jnp.tile(lse_ref[:, rows, :], (1, 1, KB // 128)))
            dp = _bdot(do, vb, _BATCH_NT)
            ds = p * (dp - jnp.sum(dp * p, axis=-1, keepdims=True))
            dbias_acc[...] += ds
            dsb = ds.astype(BF)
            dq = _bdot(dsb, kb, _BATCH_NN) * SCALE
            for h in range(N_HEADS):
                dq_ref[rows, h * HEAD_DIM:(h + 1) * HEAD_DIM] = dq[h].astype(BF)
            dk_acc[...] += _bdot(dsb, qs, _BATCH_TN)
            dv_acc[...] += _bdot(p.astype(BF), do, _BATCH_TN)

        def block(j, carry):
            i = pl.program_id(0) * Q_PER_STEP + j
            rows = pl.ds(pl.multiple_of(j * QB, QB), QB)
            pl.when(i < KEEP)(functools.partial(step, i, rows, True))
            pl.when((i >= KEEP) & (i < nq))(functools.partial(step, i, rows, False))
            for h in range(N_HEADS):
                hs = slice(h * HEAD_DIM, (h + 1) * HEAD_DIM)
                dk_ref[rows, hs] = dk_acc[h, 0:QB, :].astype(BF)
                dv_ref[rows, hs] = dv_acc[h, 0:QB, :].astype(BF)
            dk_acc[:, 0:KB - QB, :] = dk_acc[:, QB:KB, :]
            dv_acc[:, 0:KB - QB, :] = dv_acc[:, QB:KB, :]
            dk_acc[:, KB - QB:KB, :] = jnp.zeros((N_HEADS, QB, HEAD_DIM), F32)
            dv_acc[:, KB - QB:KB, :] = jnp.zeros((N_HEADS, QB, HEAD_DIM), F32)
            return carry

        lax.fori_loop(0, Q_PER_STEP, block, 0)

        @pl.when(pl.program_id(0) == n_steps - 1)
        def _():
            lane = lax.broadcasted_iota(jnp.int32, (1, ROLL_W), 1)
            hi = (lane < 384) | (lane >= 832)
            lo = (lane > 640) & (lane < 832)
            pad_scr[...] = jnp.zeros_like(pad_scr)
            for h in range(N_HEADS):
                pad_scr[:, 0:KB] = dbias_acc[h]
                g = _unskew_sum(pad_scr[...])
                s_hi = jnp.sum(jnp.where(hi, g, 0.0), axis=-1, keepdims=True)
                s_lo = jnp.sum(jnp.where(lo, g, 0.0), axis=-1, keepdims=True)
                g = jnp.where(lane == 384, g + s_hi, g)
                g = jnp.where(lane == 640, g + s_lo, g)
                dgp_ref[h:h + 1, :] = g

    assert nq % Q_PER_STEP == 0 and KEEP % Q_PER_STEP == 0
    rows_per_step = Q_PER_STEP * QB
    n_steps = (nq + KEEP) // Q_PER_STEP
    last = nq // Q_PER_STEP - 1
    lag = KEEP // Q_PER_STEP
    kv_scr = pltpu.VMEM((N_HEADS, S + PADK, HEAD_DIM), BF)
    return pl.pallas_call(
        _after(body, 6, after), name="attn_bwd", grid=(n_steps,),
        out_shape=(jax.ShapeDtypeStruct((S, D_A), BF), jax.ShapeDtypeStruct((S, D_A), BF),
                   jax.ShapeDtypeStruct((S, D_A), BF), jax.ShapeDtypeStruct((N_HEADS, ROLL_W), F32)),
        in_specs=[pl.BlockSpec((N_HEADS, rows_per_step, HEAD_DIM), lambda g: (0, jnp.minimum(g, last), 0)),
                  pl.BlockSpec((N_HEADS, rows_per_step, HEAD_DIM), lambda g: (0, jnp.minimum(g, last), 0)),
                  pl.BlockSpec(memory_space=pl.ANY), pl.BlockSpec(memory_space=pl.ANY),
                  pl.BlockSpec((N_HEADS, rows_per_step, 128), lambda g: (0, jnp.minimum(g, last), 0)),
                  pl.BlockSpec((N_HEADS, QB, KB), lambda g: (0, 0, 0))] + [_ANY] * len(after),
        out_specs=[pl.BlockSpec((rows_per_step, D_A), lambda g: (jnp.minimum(g, last), 0)),
                   pl.BlockSpec((rows_per_step, D_A), lambda g: (jnp.maximum(g - lag, 0), 0)),
                   pl.BlockSpec((rows_per_step, D_A), lambda g: (jnp.maximum(g - lag, 0), 0)),
                   pl.BlockSpec((N_HEADS, ROLL_W), lambda g: (0, 0))],
        scratch_shapes=[kv_scr, kv_scr,
                        pltpu.VMEM((N_HEADS, KB, HEAD_DIM), F32), pltpu.VMEM((N_HEADS, KB, HEAD_DIM), F32),
                        pltpu.VMEM((N_HEADS, QB, KB), F32), pltpu.VMEM((QB, ROLL_W), F32),
                        pltpu.SemaphoreType.DMA((2,))],
        compiler_params=_params(56, dimension_semantics=("arbitrary",)),
    )(q3, d_att3, k3, v3, lse, bias, *after)


def _sgu_core(ub, vb, lg, lb):
    u, du = _gelu_and_grad(ub)
    v, dv = _gelu_and_grad(vb)
    mu = jnp.mean(v, axis=-1, keepdims=True)
    vc = v - mu
    rstd = lax.rsqrt(jnp.mean(vc * vc, axis=-1, keepdims=True) + EPS)
    xh = vc * rstd
    vn = xh * lg + lb
    return u, du, dv, rstd, xh, vn


def _tri():
    r = lax.broadcasted_iota(jnp.int32, (SGU_CHUNK, SGU_CHUNK), 0)
    c = lax.broadcasted_iota(jnp.int32, (SGU_CHUNK, SGU_CHUNK), 1)
    return r >= c


def _tail_sgu(att, zrest, x, target, w_pa, w_pb, w_out, b_gate, final_g, ln_g, ln_b, w_s, b_s_t, tm=256):
    S = x.shape[0]
    nt = S // tm
    chunks = tm // SGU_CHUNK

    def body(att_ref, ga_ref, ub_ref, vb_ref, gb_ref, gta_ref, gtb_ref, x_ref, t_ref,
             wpa_ref, wpb_ref, wout_ref, bg_ref, fg_ref, lg_ref, lb_ref, ws_ref, bst_ref,
             dout_ref, datt_ref, dzt_ref, dzs_ref, gwout_hbm, gwpa_hbm, gwpb_hbm,
             gbg_ref, gfg_ref, loss_ref, gws_ref, gbs_ref, glg_ref, glb_ref,
             acc_out, acc_pa, acc_pb, sg_scr, mix_scr, dvn_scr, bs_acc, sems):
        i = pl.program_id(0)

        @pl.when(i == 0)
        def _():
            for r in (acc_out, acc_pa, acc_pb, gbg_ref, gfg_ref, loss_ref, gws_ref, glg_ref, glb_ref, bs_acc):
                r[...] = jnp.zeros_like(r)

        u, du, dv, rstd, xh, vn = _sgu_core(ub_ref[...], vb_ref[...], lg_ref[...], lb_ref[...])
        vnb = vn.astype(BF)
        tri = _tri()
        blocks = [(g, slice(n * SGU_CHUNK, (n + 1) * SGU_CHUNK), slice(g * 128, (g + 1) * 128))
                  for g in range(N_GROUPS) for n in range(chunks)]
        wts = [jnp.where(tri, ws_ref[g], 0.0) for g in range(N_GROUPS)]
        for g, rs, cs in blocks:
            mixed = _dot(wts[g].astype(BF), vnb[rs, cs]) + bst_ref[:, g:g + 1]
            mix_scr[rs, cs] = mixed
            sg_scr[rs, cs] = u[rs, cs] * mixed

        att = att_ref[...]
        sg = sg_scr[...]
        sa, dsa = _silu_and_grad(ga_ref[...])
        sb, dsb = _silu_and_grad(gb_ref[...])
        ya = (att * sa).astype(BF)
        yb = (sg * sb).astype(BF)
        pa = _dot(ya, wpa_ref[...])
        pb = _dot(yb, wpb_ref[...])
        ga = _sigmoid(gta_ref[...] + bg_ref[:, 0:D_MODEL])
        gb = _sigmoid(gtb_ref[...] + bg_ref[:, D_MODEL:2 * D_MODEL])
        merged = (ga * pa + gb * pb).astype(BF)
        out = x_ref[...] + _dot(merged, wout_ref[...])
        r2 = lax.rsqrt(jnp.mean(out * out, axis=-1, keepdims=True) + EPS)
        nrm = out * r2
        fg = fg_ref[...]
        err = nrm * fg - t_ref[...]
        loss_ref[...] += 0.5 * jnp.sum(jnp.mean(err * err, axis=-1, keepdims=True))
        dy = err * (1.0 / D_MODEL)
        gfg_ref[...] += jnp.sum(dy * nrm, axis=0, keepdims=True)
        dn = dy * fg
        d_out = r2 * (dn - nrm * jnp.mean(dn * nrm, axis=-1, keepdims=True))
        dout_ref[...] = d_out
        d_outb = d_out.astype(BF)
        acc_out[...] += _dot_tn(merged, d_outb)
        dm = _dot_nt(d_outb, wout_ref[...])
        d_pa = (dm * ga).astype(BF)
        d_pb = (dm * gb).astype(BF)
        d_gta = dm * pa * (ga * (1.0 - ga))
        d_gtb = dm * pb * (gb * (1.0 - gb))
        gbg_ref[:, 0:D_MODEL] += jnp.sum(d_gta, axis=0, keepdims=True)
        gbg_ref[:, D_MODEL:2 * D_MODEL] += jnp.sum(d_gtb, axis=0, keepdims=True)
        dzt_ref[:, 2 * D_A:2 * D_A + D_MODEL] = d_gta.astype(BF)
        dzt_ref[:, 2 * D_A + D_MODEL:] = d_gtb.astype(BF)
        acc_pa[...] += _dot_tn(ya, d_pa)
        acc_pb[...] += _dot_tn(yb, d_pb)
        d_ya = _dot_nt(d_pa, wpa_ref[...])
        d_yb = _dot_nt(d_pb, wpb_ref[...])
        d_att = (d_ya * sa).astype(BF)
        for hd in range(N_HEADS):
            datt_ref[hd] = d_att[:, hd * HEAD_DIM:(hd + 1) * HEAD_DIM]
        dzt_ref[:, 0:D_A] = (d_ya * att * dsa).astype(BF)
        dzt_ref[:, D_A:2 * D_A] = (d_yb * sg * dsb).astype(BF)

        dsg = d_yb * sb
        dzs_ref[:, 0:D_B] = (dsg * mix_scr[...] * du).astype(BF)
        dmix = dsg * u
        for g, rs, cs in blocks:
            dmb = dmix[rs, cs].astype(BF)
            bs_acc[:, cs] += dmix[rs, cs]
            gws_ref[g] += _dot_nt(dmb, vnb[rs, cs])
            dvn_scr[rs, cs] = _dot(wts[g].T.astype(BF), dmb)
        dvn = dvn_scr[...]
        glg_ref[...] += jnp.sum(dvn * xh, axis=0, keepdims=True)
        glb_ref[...] += jnp.sum(dvn, axis=0, keepdims=True)
        dxh = dvn * lg_ref[...]
        dvv = rstd * (dxh - jnp.mean(dxh, axis=-1, keepdims=True)
                      - xh * jnp.mean(dxh * xh, axis=-1, keepdims=True))
        dzs_ref[:, D_B:2 * D_B] = (dvv * dv).astype(BF)

        @pl.when(i == nt - 1)
        def _():
            cps = [pltpu.make_async_copy(acc_out, gwout_hbm, sems.at[0]),
                   pltpu.make_async_copy(acc_pa, gwpa_hbm, sems.at[1]),
                   pltpu.make_async_copy(acc_pb, gwpb_hbm, sems.at[2])]
            for cp in cps:
                cp.start()
            lane = lax.broadcasted_iota(jnp.int32, (SGU_CHUNK, 128), 1)
            cols = jnp.zeros((SGU_CHUNK, 128), F32)
            for g in range(N_GROUPS):
                gws_ref[g] = jnp.where(tri, gws_ref[g], 0.0)
                col = jnp.sum(bs_acc[:, g * 128:(g + 1) * 128], axis=-1, keepdims=True)
                cols = jnp.where(lane == g, col, cols)
            gbs_ref[...] = cols
            for cp in cps:
                cp.wait()

    c2 = lambda i: (0, 0)
    c3 = lambda i: (0, 0, 0)
    zcol = lambda w, blk: pl.BlockSpec((tm, w), lambda i: (i, blk))
    row = lambda w: pl.BlockSpec((tm, w), lambda i: (i, 0))
    return pl.pallas_call(
        body, name="tail", grid=(nt,),
        out_shape=(jax.ShapeDtypeStruct((S, D_MODEL), F32), jax.ShapeDtypeStruct((N_HEADS, S, HEAD_DIM), BF),
                   jax.ShapeDtypeStruct((S, 3072), BF), jax.ShapeDtypeStruct((S, 2 * D_B), BF),
                   jax.ShapeDtypeStruct((D_MODEL, D_MODEL), F32), jax.ShapeDtypeStruct((D_A, D_MODEL), F32),
                   jax.ShapeDtypeStruct((D_B, D_MODEL), F32),
                   jax.ShapeDtypeStruct((1, 2 * D_MODEL), F32), jax.ShapeDtypeStruct((1, D_MODEL), F32),
                   jax.ShapeDtypeStruct((1, 128), F32),
                   jax.ShapeDtypeStruct((N_GROUPS, 128, 128), F32), jax.ShapeDtypeStruct((SGU_CHUNK, 128), F32),
                   jax.ShapeDtypeStruct((1, D_B), F32), jax.ShapeDtypeStruct((1, D_B), F32)),
        in_specs=[row(D_A), zcol(512, 0), zcol(512, 1), zcol(512, 2), zcol(512, 3),
                  zcol(D_MODEL, 2), zcol(D_MODEL, 3), row(D_MODEL), row(D_MODEL),
                  pl.BlockSpec((D_A, D_MODEL), c2), pl.BlockSpec((D_B, D_MODEL), c2),
                  pl.BlockSpec((D_MODEL, D_MODEL), c2),
                  pl.BlockSpec((1, 2 * D_MODEL), c2), pl.BlockSpec((1, D_MODEL), c2),
                  pl.BlockSpec((1, D_B), c2), pl.BlockSpec((1, D_B), c2),
                  pl.BlockSpec((N_GROUPS, 128, 128), c3), pl.BlockSpec((128, N_GROUPS), c2)],
        out_specs=[row(D_MODEL), pl.BlockSpec((N_HEADS, tm, HEAD_DIM), lambda i: (0, i, 0)),
                   row(3072), row(2 * D_B), _ANY, _ANY, _ANY,
                   pl.BlockSpec((1, 2 * D_MODEL), c2), pl.BlockSpec((1, D_MODEL), c2),
                   pl.BlockSpec((1, 128), c2),
                   pl.BlockSpec((N_GROUPS, 128, 128), c3), pl.BlockSpec((SGU_CHUNK, 128), c2),
                   pl.BlockSpec((1, D_B), c2), pl.BlockSpec((1, D_B), c2)],
        scratch_shapes=[pltpu.VMEM((D_MODEL, D_MODEL), F32), pltpu.VMEM((D_A, D_MODEL), F32),
                        pltpu.VMEM((D_B, D_MODEL), F32),
                        pltpu.VMEM((tm, D_B), F32), pltpu.VMEM((tm, D_B), F32), pltpu.VMEM((tm, D_B), F32),
                        pltpu.VMEM((SGU_CHUNK, D_B), F32), pltpu.SemaphoreType.DMA((3,))],
        compiler_params=_params(58, dimension_semantics=("arbitrary",)),
    )(att, zrest, zrest, zrest, zrest, zrest, zrest, x, target, w_pa, w_pb, w_out, b_gate, final_g,
      ln_g, ln_b, w_s, b_s_t)


_DZ_MAP = ((0, 0), (1, 0), (2, 0), (3, 0), (4, 0), (4, 1), (3, 1), (3, 2), (3, 3), (3, 4), (3, 5))


def _dh_gradx(dq, dk, dv, dzt, dzs, w_in_bf, x, norm_g, d_out, tm=512, after=()):
    S = x.shape[0]

    def body(dq_ref, dk_ref, dv_ref, dzt_ref, dzs_ref, w_ref, x_ref, g_ref, dout_ref, gx_ref, gn_ref):
        i = pl.program_id(0)

        @pl.when(i == 0)
        def _():
            gn_ref[...] = jnp.zeros_like(gn_ref)

        pieces = (dq_ref, dk_ref, dv_ref, dzt_ref, dzs_ref)
        dh = jnp.zeros((tm, D_MODEL), F32)
        for j, (pc, blk) in enumerate(_DZ_MAP):
            dh += _dot_nt(pieces[pc][:, blk * 512:(blk + 1) * 512], w_ref[:, j * 512:(j + 1) * 512])
        xv = x_ref[...]
        r = lax.rsqrt(jnp.mean(xv * xv, axis=-1, keepdims=True) + EPS)
        nrm = xv * r
        gn_ref[...] += jnp.sum(dh * nrm, axis=0, keepdims=True)
        dn = dh * g_ref[...]
        gx_ref[...] = r * (dn - nrm * jnp.mean(dn * nrm, axis=-1, keepdims=True)) + dout_ref[...]

    row = lambda w: pl.BlockSpec((tm, w), lambda i: (i, 0))
    c2 = lambda i: (0, 0)
    return pl.pallas_call(
        _after(body, 9, after), name="dh_gradx", grid=(S // tm,),
        out_shape=(jax.ShapeDtypeStruct((S, D_MODEL), F32), jax.ShapeDtypeStruct((1, D_MODEL), F32)),
        in_specs=[row(512), row(512), row(512), row(3072), row(1024),
                  pl.BlockSpec((D_MODEL, D_IN), c2, pipeline_mode=pl.Buffered(1)), row(D_MODEL),
                  pl.BlockSpec((1, D_MODEL), c2), row(D_MODEL)]
        + [_ANY] * len(after),
        out_specs=[row(D_MODEL), pl.BlockSpec((1, D_MODEL), c2)],
        compiler_params=_params(48, dimension_semantics=("arbitrary",)),
    )(dq, dk, dv, dzt, dzs, w_in_bf, x, norm_g, d_out, *after)


def _gw_in(ht, dq, dk, dv, dzt, dzs, tn=512, after=()):
    S = ht.shape[1]
    per = 512 // tn
    cols = tuple((pc, per * blk + h) for pc, blk in _DZ_MAP for h in range(per))

    def body(ht_ref, dq_ref, dk_ref, dv_ref, dzt_ref, dzs_ref, o_ref, ob_ref):
        j = pl.program_id(0)
        pieces = (dq_ref, dk_ref, dv_ref, dzt_ref, dzs_ref)
        for pc in range(5):
            hit = functools.reduce(jnp.logical_or, [j == jj for jj, (p, _) in enumerate(cols) if p == pc])

            @pl.when(hit)
            def _(pc=pc):
                g = _dot(ht_ref[...], pieces[pc][...])
                o_ref[...] = g
                ob_ref[...] = g.astype(BF)

    def piece_spec(pc):
        cur = next(blk for p, blk in cols if p == pc)
        held = []
        for p, blk in cols:
            cur = blk if p == pc else cur
            held.append(cur)

        def index_map(j):
            blk = jnp.int32(held[0])
            for jj in range(1, len(held)):
                if held[jj] != held[jj - 1]:
                    blk = jnp.where(j >= jj, jnp.int32(held[jj]), blk)
            return (0, blk)

        return pl.BlockSpec((S, tn), index_map)

    return pl.pallas_call(
        _after(body, 6, after), name="gw_in", grid=(len(cols),),
        out_shape=(jax.ShapeDtypeStruct((D_MODEL, D_IN), F32), jax.ShapeDtypeStruct((D_MODEL, D_IN), BF)),
        in_specs=[pl.BlockSpec((D_MODEL, S), lambda j: (0, 0), pipeline_mode=pl.Buffered(1))]
        + [piece_spec(pc) for pc in range(5)]
        + [_ANY] * len(after),
        out_specs=[pl.BlockSpec((D_MODEL, tn), lambda j: (0, j)), pl.BlockSpec((D_MODEL, tn), lambda j: (0, j))],
        compiler_params=_params(56, dimension_semantics=("arbitrary",)),
    )(ht, dq, dk, dv, dzt, dzs, *after)


_HBM = pl.BlockSpec(memory_space=pltpu.HBM)
_SEM = pl.BlockSpec(memory_space=pltpu.SEMAPHORE)
_ANY = pl.BlockSpec(memory_space=pl.ANY)
_EFFECT = pltpu.SideEffectType.DATAFLOW_SIDE_EFFECTING


def _in_hbm(a):
    return pltpu.with_memory_space_constraint(a, pltpu.HBM)


def _after(body, n_in, after):
    if not after:
        return body
    return lambda *refs: body(*refs[:n_in], *refs[n_in + len(after):])


class _Started:
    def __init__(self, send, recv, bufs, token):
        self.send, self.recv, self.bufs, self.token = send, recv, bufs, token


def _split_start(name, bufs, n_copies, copies, after=()):
    nb = len(bufs)

    def body(*refs):
        refs = refs[:nb] + refs[nb + len(after):]
        for cp in copies(refs[:nb], refs[nb], refs[nb + 1]):
            cp.start()
        refs[-1][...] = jnp.zeros_like(refs[-1])

    outs = pl.pallas_call(
        body, name=name,
        out_shape=(pltpu.SemaphoreType.DMA((n_copies,)), pltpu.SemaphoreType.DMA((n_copies,)),
                   *[pltpu.HBM(b.shape, b.dtype) for b in bufs], jax.ShapeDtypeStruct((8, 128), F32)),
        in_specs=[_HBM] * nb + [_ANY] * len(after),
        out_specs=(_SEM, _SEM, *[_HBM] * nb, pl.BlockSpec(memory_space=pltpu.VMEM)),
        input_output_aliases={k: 2 + k for k in range(nb)},
        compiler_params=_params(1, has_side_effects=_EFFECT),
    )(*[_in_hbm(b) for b in bufs], *after)
    return _Started(outs[0], outs[1], list(outs[2:2 + nb]), outs[-1])


def _split_wait(name, started, copies, after):
    nb = len(started.bufs)

    def body(*refs):
        for cp in copies(refs[:nb], refs[nb], refs[nb + 1]):
            cp.wait_send()
            cp.wait_recv()

    return list(pl.pallas_call(
        body, name=name,
        out_shape=tuple(pltpu.HBM(b.shape, b.dtype) for b in started.bufs),
        in_specs=[_HBM] * nb + [_SEM, _SEM, _ANY],
        out_specs=tuple([_HBM] * nb),
        input_output_aliases={k: k for k in range(nb)},
        compiler_params=_params(1, has_side_effects=_EFFECT),
    )(*started.bufs, started.send, started.recv, after))


def _x1_copies(ws):
    def copies(refs, send_sems, recv_sems):
        x, y, c, _ = _mesh_pos()
        out = []
        for k, w in enumerate(ws):
            for s in range(N_SHARD):
                out.append(pltpu.make_async_remote_copy(
                    src_ref=_UNITS[w](refs[k], s, 1 - c), dst_ref=refs[len(ws) + k].at[s],
                    send_sem=send_sems.at[N_SHARD * k + s], recv_sem=recv_sems.at[N_SHARD * k + s],
                    device_id=(x, y, 1 - c), device_id_type=MESH))
        return out
    return copies


def _x2_copies(n):
    def copies(refs, send_sems, recv_sems):
        x, y, c, chips = _mesh_pos()
        out = []
        for j, (cx, cy) in enumerate(chips):
            for k in range(n):
                out.append(pltpu.make_async_remote_copy(
                    src_ref=refs[k].at[2 * cx + cy], dst_ref=refs[n + k].at[j],
                    send_sem=send_sems.at[3 * k + j], recv_sem=recv_sems.at[3 * k + j],
                    device_id=(cx, cy, c), device_id_type=MESH))
        return out
    return copies


def _x3_copies(ws):
    def copies(refs, send_sems, recv_sems):
        x, y, c, _ = _mesh_pos()
        out = []
        for k, w in enumerate(ws):
            rows = _HALF_ROWS[w]
            mine = refs[k].at[pl.ds(_mo(c * rows, rows), rows), :]
            out.append(pltpu.make_async_remote_copy(
                src_ref=mine, dst_ref=mine, send_sem=send_sems.at[k], recv_sem=recv_sems.at[k],
                device_id=(x, y, 1 - c), device_id_type=MESH))
        return out
    return copies


def _x1_lands(ws, dtype=F32):
    return [lax.empty((N_SHARD,) + _UNIT_SHAPES[w], dtype) for w in ws]


def _x2_lands(ws):
    return [lax.empty((3,) + _UNIT_SHAPES[w], BF) for w in ws]


def _grad_add1(w, g, recv, pos):
    ur, uc = _UNIT_SHAPES[w]

    def body(pos_ref, g_ref, r_ref, own_ref, csb_ref):
        v = g_ref[...] + r_ref[0].astype(F32)
        csb_ref[0] = v.astype(BF)

        @pl.when(pl.program_id(0) == pos_ref[1])
        def _():
            own_ref[...] = v

    u3 = lambda s, pos: (s, 0, 0)
    return pl.pallas_call(
        body, name=f"grad_add1_{w}",
        grid_spec=pltpu.PrefetchScalarGridSpec(
            num_scalar_prefetch=1, grid=(N_SHARD,),
            in_specs=[pl.BlockSpec((ur, uc), lambda s, pos: (pos[0], s)), pl.BlockSpec((1, ur, uc), u3)],
            out_specs=[pl.BlockSpec((ur, uc), lambda s, pos: (0, 0)), pl.BlockSpec((1, ur, uc), u3)]),
        out_shape=(jax.ShapeDtypeStruct((ur, uc), F32), jax.ShapeDtypeStruct((N_SHARD, ur, uc), BF)),
        compiler_params=_params(40, dimension_semantics=("arbitrary",)),
    )(pos, g, recv)


def _grad_add1_group(ws, gs, recvs):
    n = len(ws)

    def body(*refs):
        c = lax.axis_index("c")
        for k, w in enumerate(ws):
            g, r, cs, csb = refs[k], refs[n + k], refs[2 * n + k], refs[3 * n + k]
            for s in range(N_SHARD):
                v = _UNITS[w](g, s, c)[...] + r[s]
                cs[s] = v
                csb[s] = v.astype(BF)

    vm = pl.BlockSpec(memory_space=pltpu.VMEM)
    outs = pl.pallas_call(
        body, name="grad_add1_group",
        out_shape=tuple(jax.ShapeDtypeStruct((N_SHARD,) + _UNIT_SHAPES[w], dt) for dt in (F32, BF) for w in ws),
        in_specs=[vm] * (2 * n), out_specs=[vm] * (2 * n),
        compiler_params=_params(32),
    )(*gs, *recvs)
    return list(outs[:n]), list(outs[n:])


def _grad_add2_group(ws, css, recvs):
    n = len(ws)

    def body(*refs):
        x, y, c, _ = _mesh_pos()
        for k, w in enumerate(ws):
            cs, r, o = refs[k], refs[n + k], refs[2 * n + k]
            rows = _HALF_ROWS[w]
            total = ((cs[2 * x + y] + r[0].astype(F32)) + r[1].astype(F32)) + r[2].astype(F32)
            o[pl.ds(_mo(c * rows, rows), rows), :] = total

    vm = pl.BlockSpec(memory_space=pltpu.VMEM)
    return list(pl.pallas_call(
        body, name="grad_add2_group",
        out_shape=tuple(jax.ShapeDtypeStruct(_SHARD_SHAPES[w], F32) for w in ws),
        in_specs=[vm] * (2 * n), out_specs=[vm] * n,
        compiler_params=_params(32),
    )(*css, *recvs))


def _grad_add2(w, own, recv, pos):
    ur, uc = _UNIT_SHAPES[w]
    nt = 4
    tr = ur // nt

    def body(pos_ref, own_ref, r_ref, o_ref):
        o_ref[...] = ((own_ref[...] + r_ref[0].astype(F32)) + r_ref[1].astype(F32)) + r_ref[2].astype(F32)

    return pl.pallas_call(
        body, name=f"grad_add2_{w}",
        grid_spec=pltpu.PrefetchScalarGridSpec(
            num_scalar_prefetch=1, grid=(nt,),
            in_specs=[pl.BlockSpec((tr, uc), lambda t, pos: (t, 0)),
                      pl.BlockSpec((3, tr, uc), lambda t, pos: (0, t, 0))],
            out_specs=pl.BlockSpec((tr, uc), lambda t, pos: (pos[0] * nt + t, 0))),
        out_shape=jax.ShapeDtypeStruct(_SHARD_SHAPES[w], F32),
        compiler_params=_params(32, dimension_semantics=("arbitrary",)),
    )(pos, own, recv)


def _adamw_math(w, g, m, v):
    m = ADAM_B1 * m + (1.0 - ADAM_B1) * g
    v = ADAM_B2 * v + (1.0 - ADAM_B2) * (g * g)
    m_hat = m / ADAM_C1
    v_hat = v / ADAM_C2
    delta = -ADAM_LR * (m_hat / (jnp.sqrt(v_hat) + ADAM_EPS) + ADAM_WD * w)
    return delta, m, v


def _adamw_group(ws_, gs, ms, vs, after=()):
    n = len(ws_)

    def body(*refs):
        for k in range(n):
            w, g, m, v = (refs[j * n + k] for j in range(4))
            d, nm, nv, gc = (refs[(4 + j) * n + k] for j in range(4))
            gv = g[...]
            d[...], nm[...], nv[...] = _adamw_math(w[...], gv, m[...], v[...])
            gc[...] = gv

    vm = pl.BlockSpec(memory_space=pltpu.VMEM)
    outs = pl.pallas_call(
        _after(body, 4 * n, after), name="adamw_group",
        out_shape=tuple(jax.ShapeDtypeStruct(a.shape, F32) for _ in range(4) for a in ws_),
        in_specs=[vm] * (4 * n) + [_ANY] * len(after), out_specs=[vm] * (4 * n),
        compiler_params=_params(32),
    )(*ws_, *gs, *ms, *vs, *after)
    return [tuple(outs[j * n + k] for j in range(4)) for k in range(n)]


def _adamw(name, w, g, m, v, tr=256, after=()):
    rows, cols = w.shape

    def body(w_ref, g_ref, m_ref, v_ref, d_ref, nm_ref, nv_ref, gc_ref):
        gv = g_ref[...]
        d_ref[...], nm_ref[...], nv_ref[...] = _adamw_math(w_ref[...], gv, m_ref[...], v_ref[...])
        gc_ref[...] = gv

    spec = pl.BlockSpec((tr, cols), lambda i: (i, 0))
    return pl.pallas_call(
        _after(body, 4, after), name=name, grid=(rows // tr,),
        out_shape=tuple(jax.ShapeDtypeStruct((rows, cols), F32) for _ in range(4)),
        in_specs=[spec] * 4 + [_ANY] * len(after), out_specs=[spec] * 4,
        compiler_params=_params(32, dimension_semantics=("arbitrary",)),
    )(w, g, m, v, *after)


_REL_PAD = 384
_VEC_FIELDS = (("norm_g", 0, D_MODEL), ("b_gate", 1024, 2 * D_MODEL), ("sgu_ln_g", 3072, D_B),
               ("sgu_ln_b", 3584, D_B), ("b_s", 4096, N_GROUPS * 128), ("final_g", 4608, D_MODEL))
_LOSS_OFF = 5632
_REL_OFF = 5760
_NV = _REL_OFF + N_HEADS * _REL_PAD
_N_FIELDS = len(_VEC_FIELDS) + 2


_B_S_FIELD = [f[0] for f in _VEC_FIELDS].index("b_s")


def _assemble_row(dst, fields, transposed_b_s):
    for f, (_, off, n) in enumerate(_VEC_FIELDS):
        if transposed_b_s and f == _B_S_FIELD:
            t = fields[f][...].T
            for g in range(N_GROUPS):
                dst[:, off + 128 * g:off + 128 * (g + 1)] = t[g:g + 1, :]
        else:
            dst[:, off:off + n] = fields[f][...]
    for r in range(N_HEADS):
        dst[:, _REL_OFF + _REL_PAD * r:_REL_OFF + _REL_PAD * (r + 1)] = fields[len(_VEC_FIELDS)][r:r + 1, :]


def _small_reduce(grads, loss_row, after=()):
    n_in = _N_FIELDS + 1

    def body(*refs):
        g_refs, loss_ref = refs[:_N_FIELDS], refs[_N_FIELDS]
        out_v, out_w = refs[n_in:n_in + 2]
        mine_v, mine_w, gath_v, gath_w, send_sems, recv_sems = refs[n_in + 2:]
        x, y, c, chips = _mesh_pos()
        me, sibling = (x, y, c), (x, y, 1 - c)

        _assemble_row(mine_v, g_refs, True)
        mine_v[:, _LOSS_OFF:_LOSS_OFF + 128] = loss_ref[...]
        mine_w[...] = g_refs[-1][...].astype(BF)
        my_k = 4 * x + 2 * y + c
        gath_v[my_k] = mine_v[...]
        gath_w[my_k] = mine_w[...]

        def copy(k, gath, block, to, src=None):
            dst = gath.at[4 * block[0] + 2 * block[1] + block[2]]
            return pltpu.make_async_remote_copy(
                src_ref=dst if src is None else src, dst_ref=dst,
                send_sem=send_sems.at[k], recv_sem=recv_sems.at[k], device_id=to, device_id_type=MESH)

        bufs = ((gath_v, mine_v), (gath_w, mine_w))
        first, passed = [], []
        for b, (gath, mine) in enumerate(bufs):
            first.append(copy(7 * b, gath, me, sibling, src=mine))
            first += [copy(7 * b + 1 + j, gath, me, (*chip, c), src=mine) for j, chip in enumerate(chips)]
        for cp in first:
            cp.start()
        for b, (gath, _) in enumerate(bufs):
            for j, chip in enumerate(chips):
                copy(7 * b + 1 + j, gath, (*chip, c), me).wait_recv()
                cp = copy(7 * b + 4 + j, gath, (*chip, c), sibling)
                cp.start()
                passed.append(cp)
        for b, (gath, _) in enumerate(bufs):
            copy(7 * b, gath, sibling, me).wait_recv()
            for j, chip in enumerate(chips):
                copy(7 * b + 4 + j, gath, (*chip, 1 - c), me).wait_recv()
        for cp in first + passed:
            cp.wait_send()

        tot_v, tot_w = gath_v[0], gath_w[0].astype(F32)
        for k in range(1, 8):
            tot_v = tot_v + gath_v[k]
            tot_w = tot_w + gath_w[k].astype(F32)
        out_v[...] = tot_v
        out_w[...] = tot_w

    vm = pl.BlockSpec(memory_space=pltpu.VMEM)
    return pl.pallas_call(
        _after(body, n_in, after), name="small_reduce",
        out_shape=(jax.ShapeDtypeStruct((1, _NV), F32), jax.ShapeDtypeStruct((N_GROUPS * 128, 128), F32)),
        in_specs=[vm] * n_in + [_ANY] * len(after), out_specs=[vm] * 2,
        scratch_shapes=[pltpu.VMEM((1, _NV), F32), pltpu.VMEM((N_GROUPS * 128, 128), BF),
                        pltpu.VMEM((8, 1, _NV), F32), pltpu.VMEM((8, N_GROUPS * 128, 128), BF),
                        pltpu.SemaphoreType.DMA((14,)), pltpu.SemaphoreType.DMA((14,))],
        compiler_params=_params(32),
    )(*grads, loss_row, *after)


def _small_adamw(tot_v, tot_w, params):
    n_in = 2 + 3 * _N_FIELDS

    def body(*refs):
        tv_ref, tw_ref = refs[:2]
        p_refs = [refs[2 + k * _N_FIELDS:2 + (k + 1) * _N_FIELDS] for k in range(3)]
        outs = refs[n_in:n_in + 4 * _N_FIELDS + 1]
        wmv = refs[-1]
        for k in range(3):
            _assemble_row(wmv.at[k], p_refs[k], False)
            wmv[k, :, _LOSS_OFF:_LOSS_OFF + 128] = jnp.zeros((1, 128), F32)
        tot_v, tot_w = tv_ref[...], tw_ref[...]
        res_v = (tot_v,) + _adamw_math(wmv[0], tot_v, wmv[1], wmv[2])
        res_w = (tot_w,) + _adamw_math(p_refs[0][-1][...], tot_w, p_refs[1][-1][...], p_refs[2][-1][...])
        for kind in range(4):
            o = outs[kind * _N_FIELDS:(kind + 1) * _N_FIELDS]
            for f, (_, off, n) in enumerate(_VEC_FIELDS):
                o[f][...] = res_v[kind][:, off:off + n]
            for r in range(N_HEADS):
                o[len(_VEC_FIELDS)][r:r + 1, :] = res_v[kind][:, _REL_OFF + _REL_PAD * r:_REL_OFF + _REL_PAD * (r + 1)]
            o[-1][...] = res_w[kind]
        outs[-1][...] = tot_v[:, _LOSS_OFF:_LOSS_OFF + 128]

    field_shapes = [(1, n) for _, _, n in _VEC_FIELDS] + [(N_HEADS, _REL_PAD), (N_GROUPS * 128, 128)]
    vm = pl.BlockSpec(memory_space=pltpu.VMEM)
    operands = [tot_v, tot_w] + [a for p in params for a in p]
    assert len(operands) == n_in
    outs = pl.pallas_call(
        body, name="small_adamw",
        out_shape=tuple(jax.ShapeDtypeStruct(s, F32) for _ in range(4) for s in field_shapes)
        + (jax.ShapeDtypeStruct((1, 128), F32),),
        in_specs=[vm] * n_in, out_specs=[vm] * (4 * _N_FIELDS + 1),
        scratch_shapes=[pltpu.VMEM((3, 1, _NV), F32)],
        compiler_params=_params(32),
    )(*operands)
    return [outs[k * _N_FIELDS:(k + 1) * _N_FIELDS] for k in range(4)], outs[-1]


def _small_fields(norm_g, b_gate, ln_g, ln_b, b_s, final_g, rel_bias, w_s):
    rel = jnp.pad(rel_bias.reshape(N_HEADS, N_REL), ((0, 0), (0, _REL_PAD - N_REL)))
    return (norm_g, b_gate, ln_g, ln_b, b_s.reshape(1, N_GROUPS * 128), final_g.reshape(1, D_MODEL),
            rel, w_s.reshape(N_GROUPS * 128, 128))


def _small_outputs(fields):
    n_g, b_g, l_g, l_b, b_s, f_g, rel, w_s = fields
    return (n_g, b_g, rel[:, :N_REL].reshape(1, N_HEADS, N_REL), l_g, l_b,
            w_s.reshape(1, N_GROUPS, 128, 128), b_s.reshape(1, N_GROUPS, 128), f_g.reshape(D_MODEL))


def _bias_row(rel_bias):
    hi = rel_bias[:, N_REL - 1:N_REL]
    lo = rel_bias[:, 0:1]
    return jnp.concatenate([jnp.broadcast_to(hi, (N_HEADS, 384)), rel_bias[:, ::-1],
                            jnp.broadcast_to(lo, (N_HEADS, 191)), jnp.broadcast_to(hi, (N_HEADS, 192))], axis=1)


def kernel(x, norm_g, w_in, b_gate, rel_bias, sgu_ln_g, sgu_ln_b, w_s, b_s, w_pa, w_pb, w_out, final_g, loss_target, m_norm_g, m_w_in, m_b_gate, m_rel_bias, m_sgu_ln_g, m_sgu_ln_b, m_w_s, m_b_s, m_w_pa, m_w_pb, m_w_out, m_final_g, v_norm_g, v_w_in, v_b_gate, v_rel_bias, v_sgu_ln_g, v_sgu_ln_b, v_w_s, v_b_s, v_w_pa, v_w_pb, v_w_out, v_final_g):
    S = x.shape[1]
    xs = x.reshape(S, D_MODEL)
    tgt = loss_target.reshape(S, D_MODEL)
    big_w = (w_in[0], w_pa[0], w_pb[0], w_out[0])
    big_m = (m_w_in[0], m_w_pa[0], m_w_pb[0], m_w_out[0])
    big_v = (v_w_in[0], v_w_pa[0], v_w_pb[0], v_w_out[0])
    rel = rel_bias[0]
    ws = w_s[0]
    bst = b_s[0].T
    fg = final_g.reshape(1, D_MODEL)
    pos = jnp.stack([lax.axis_index("c"), 2 * lax.axis_index("x") + lax.axis_index("y")]).astype(jnp.int32)

    (w_in_bf,), staged, band_bias = _ag_weights((0,), big_w[:1], (1, 2, 3), big_w[1:], _bias_row(rel))
    ag_s = _split_start("ag_small_start", staged, 9, _gather_copies((1, 2, 3)), after=(w_in_bf,))

    ht, q3, k3, v3, zrest = _inproj_fwd(xs, norm_g, w_in_bf, after=(ag_s.token,))
    att, lse = _attn_fwd(q3, k3, v3, band_bias)
    w_pa_bf, w_pb_bf, w_out_bf = _split_wait("ag_small_wait", ag_s, _gather_copies((1, 2, 3)), att)
    (d_out, d_att, dzt, dzs, gw_out, gw_pa, gw_pb, g_bgate, g_final, loss_row,
     g_ws, g_bs_t, g_lng, g_lnb) = _tail_sgu(
        att, zrest, xs, tgt, w_pa_bf, w_pb_bf, w_out_bf, b_gate, fg, sgu_ln_g, sgu_ln_b, ws, bst)
    ws_s, ws_i = (1, 2, 3), (0,)

    x1s = _split_start("gx1s_start", [gw_pa, gw_pb, gw_out] + _x1_lands(ws_s), 12, _x1_copies(ws_s))
    dq, dk, dv, d_gp = _attn_bwd(q3, k3, v3, d_att, lse, band_bias, after=(x1s.token,))
    got = _split_wait("gx1s_wait", x1s, _x1_copies(ws_s), dq)
    cs_s, csb_s = _grad_add1_group(ws_s, got[:3], got[3:])

    x2s = _split_start("gx2s_start", csb_s + _x2_lands(ws_s), 9, _x2_copies(3))
    gw_in, gw_in_bf = _gw_in(ht, dq, dk, dv, dzt, dzs, after=(x2s.token,))
    x1i = _split_start("gx1i_start", [gw_in_bf] + _x1_lands(ws_i, BF), 4, _x1_copies(ws_i))
    got = _split_wait("gx2s_wait", x2s, _x2_copies(3), x1i.token)
    halves_s = _grad_add2_group(ws_s, cs_s, got[3:])
    x3s = _split_start("gx3s_start", halves_s, 3, _x3_copies(ws_s))
    got = _split_wait("gx1i_wait", x1i, _x1_copies(ws_i), x3s.token)
    sum_i = _grad_add1(0, gw_in, got[1], pos)

    x2i = _split_start("gx2i_start", [sum_i[1]] + _x2_lands(ws_i), 3, _x2_copies(1))
    grad_x, g_norm = _dh_gradx(dq, dk, dv, dzt, dzs, w_in_bf, xs, norm_g, d_out, after=(x2i.token,))
    g_shards_s = _split_wait("gx3s_wait", x3s, _x3_copies(ws_s), grad_x)
    got = _split_wait("gx2i_wait", x2i, _x2_copies(1), grad_x)
    half_i = _grad_add2(0, sum_i[0], got[1], pos)
    x3i = _split_start("gx3i_start", [half_i], 1, _x3_copies(ws_i))
    big = [None] * 4
    big[1:] = _adamw_group(big_w[1:], g_shards_s, big_m[1:], big_v[1:], after=(x3i.token,))

    g_rel = jnp.pad(d_gp[:, 384:384 + N_REL][:, ::-1], ((0, 0), (0, _REL_PAD - N_REL)))
    small_grads = (g_norm, g_bgate, g_lng, g_lnb, g_bs_t, g_final, g_rel, g_ws.reshape(N_GROUPS * 128, 128))
    small_params = (_small_fields(norm_g, b_gate, sgu_ln_g, sgu_ln_b, b_s, final_g, rel_bias, w_s),
                    _small_fields(m_norm_g, m_b_gate, m_sgu_ln_g, m_sgu_ln_b, m_b_s, m_final_g, m_rel_bias, m_w_s),
                    _small_fields(v_norm_g, v_b_gate, v_sgu_ln_g, v_sgu_ln_b, v_b_s, v_final_g, v_rel_bias, v_w_s))
    tot_v, tot_w = _small_reduce(small_grads, loss_row, after=(x3i.token,))
    (gsum, sdelta, sm, sv), loss_out = _small_adamw(tot_v, tot_w, small_params)

    g_shard_i, = _split_wait("gx3i_wait", x3i, _x3_copies(ws_i), loss_out)
    big[0] = _adamw("adamw_w_in", big_w[0], g_shard_i, big_m[0], big_v[0])
    sg_out, sd_out, sm_out, sv_out = (_small_outputs(f) for f in (gsum, sdelta, sm, sv))
    loss = loss_out[0, 0]

    def assemble(small, bigs):
        n_g, b_g, r_b, l_g, l_b, w_s_, b_s_, f_g = small
        b_in, b_pa, b_pb, b_out = (b[None] for b in bigs)
        return (n_g, b_in, b_g, r_b, l_g, l_b, w_s_, b_s_, b_pa, b_pb, b_out, f_g)

    grads_out = assemble(sg_out, [b[3] for b in big])
    delta_out = assemble(sd_out, [b[0] for b in big])
    m_out = assemble(sm_out, [b[1] for b in big])
    v_out = assemble(sv_out, [b[2] for b in big])
    return (loss, grad_x.reshape(1, S, D_MODEL), *grads_out, *delta_out, *m_out, *v_out)
```

```python
import functools
import math

import jax
import jax.numpy as jnp
from jax import lax
from jax.experimental import pallas as pl
from jax.experimental.pallas import tpu as pltpu

F32 = jnp.float32
BF = jnp.bfloat16
MESH = pl.DeviceIdType.MESH

D_MODEL = 1024
D_A = 512
D_B = 512
D_IN = 5632
N_HEADS = 8
HEAD_DIM = 64
CHUNK = 64
N_PREV = 8
SGU_CHUNK = 128
N_GROUPS = 4
N_REL = 257
EPS = 1e-6
NEG_INF = -1e30
SCALE = HEAD_DIM ** -0.5

QB = 2 * CHUNK
KB = (N_PREV + 2) * CHUNK
PADK = N_PREV * CHUNK
ROLL_W = 1024
KEEP = KB // QB - 1
Q_PER_STEP = 2

ADAM_LR = 0.001
ADAM_B1 = 0.9
ADAM_B2 = 0.999
ADAM_EPS = 1e-08
ADAM_WD = 0.01
ADAM_STEP = 10
ADAM_C1 = 1.0 - ADAM_B1 ** ADAM_STEP
ADAM_C2 = 1.0 - ADAM_B2 ** ADAM_STEP

N_SHARD = 4
SHARD_IN = D_IN // N_SHARD
MIB = 1024 * 1024


V7X_VMEM_MIB = 64
VMEM_RESERVE_MIB = V7X_VMEM_MIB - 4


def _params(vmem_mib, **kw):
    assert vmem_mib <= VMEM_RESERVE_MIB
    return pltpu.CompilerParams(vmem_limit_bytes=VMEM_RESERVE_MIB * MIB, **kw)


def _sigmoid(x):
    return 1.0 / (1.0 + jnp.exp(-x))


def _silu_and_grad(x):
    s = _sigmoid(x)
    return x * s, s * (1.0 + x * (1.0 - s))


_GELU_C = math.sqrt(2.0 / math.pi)
_GELU_A = 0.044715


def _gelu_and_grad(x):
    x2 = x * x
    t = jnp.tanh(_GELU_C * (x + _GELU_A * (x2 * x)))
    cdf = 0.5 * (1.0 + t)
    grad = cdf + 0.5 * x * (1.0 - t * t) * (_GELU_C * (1.0 + 3.0 * _GELU_A * x2))
    return x * cdf, grad


def _dot(a, b):
    return jnp.dot(a, b, preferred_element_type=F32)


def _dot_nt(a, b):
    return lax.dot_general(a, b, (((1,), (1,)), ((), ())), preferred_element_type=F32)


def _dot_tn(a, b):
    return lax.dot_general(a, b, (((0,), (0,)), ((), ())), preferred_element_type=F32)


def _mo(v, m):
    return v if isinstance(v, int) else pl.multiple_of(v, m)


def _unit_in(ref, s, p):
    return ref.at[pl.ds(_mo(p * 512, 512), 512), pl.ds(_mo(s * SHARD_IN, 128), SHARD_IN)]


def _unit_p(ref, s, p):
    return ref.at[pl.ds(_mo(p * 256, 256), 256), pl.ds(_mo(s * 256, 128), 256)]


def _unit_out(ref, s, p):
    return ref.at[pl.ds(_mo(s * 256 + p * 128, 128), 128), :]


_UNITS = (_unit_in, _unit_p, _unit_p, _unit_out)
_HALF_ROWS = (512, 256, 256, 128)
_UNIT_SHAPES = ((512, SHARD_IN), (256, 256), (256, 256), (128, D_MODEL))
_FULL_SHAPES = ((D_MODEL, D_IN), (D_A, D_MODEL), (D_B, D_MODEL), (D_MODEL, D_MODEL))
_SHARD_SHAPES = ((D_MODEL, SHARD_IN), (D_A, 256), (D_B, 256), (256, D_MODEL))


def _mesh_pos():
    x, y, c = lax.axis_index("x"), lax.axis_index("y"), lax.axis_index("c")
    chips = [(1 - x, y), (x, 1 - y), (1 - x, 1 - y)]
    return x, y, c, chips


def _ag_weights(ws, shards, later_ws, later_shards, gp):
    n, m = len(ws), len(later_ws)

    def body(*refs):
        ins, later_ins, gp_ref = refs[:n], refs[n:n + m], refs[n + m]
        o = n + m + 1
        outs, later_outs, bias_ref = refs[o:o + n], refs[o + n:o + n + m], refs[o + n + m]
        o += n + m + 1
        stage, later_stage = refs[o:o + n], refs[o + n:o + n + m]
        send_sems, recv_sems, local_sems, later_sems = refs[o + n + m:]
        x, y, c, chips = _mesh_pos()
        s_me = 2 * x + y
        sibling = (x, y, 1 - c)
        def rows_of(k, p):
            rows = _HALF_ROWS[ws[k]]
            return pl.ds(_mo(p * rows, rows), rows)

        def half(k, p):
            return stage[k].at[rows_of(k, p), :]

        def unit(k, s, p):
            return _UNITS[ws[k]](outs[k], s, p)

        def rcopy(k, i, src, dst, to):
            return pltpu.make_async_remote_copy(src_ref=src, dst_ref=dst, send_sem=send_sems.at[k, i],
                                                recv_sem=recv_sems.at[k, i], device_id=to, device_id_type=MESH)

        for k in range(n):
            stage[k][rows_of(k, c), :] = ins[k][rows_of(k, c), :].astype(BF)
        sends = []
        for j, (cx, cy) in enumerate(chips):
            for k in range(n):
                cp = rcopy(k, j, half(k, c), unit(k, s_me, c), (cx, cy, c))
                cp.start()
                sends.append(cp)
        for k in range(n):
            stage[k][rows_of(k, 1 - c), :] = ins[k][rows_of(k, 1 - c), :].astype(BF)
        local = []
        for k in range(n):
            for p in range(2):
                cp = pltpu.make_async_copy(half(k, p), unit(k, s_me, p), local_sems.at[k, p])
                cp.start()
                local.append(cp)
        for k, w in enumerate(later_ws):
            later_stage[k][...] = later_ins[k][...].astype(BF)
            cp = pltpu.make_async_copy(later_stage[k], _shard_of(later_outs[k], w, s_me), later_sems.at[k])
            cp.start()
            local.append(cp)
        keep = _struct_mask()
        for h in range(N_HEADS):
            bias_ref[h] = jnp.where(keep, _skew_table(gp_ref[h:h + 1, :])[:, :KB], NEG_INF)
        for j, (cx, cy) in enumerate(chips):
            for k in range(n):
                landed = unit(k, 2 * cx + cy, c)
                rcopy(k, j, landed, landed, (cx, cy, c)).wait_recv()
                cp = rcopy(k, 3 + j, landed, landed, sibling)
                cp.start()
                sends.append(cp)
        for j, (cx, cy) in enumerate(chips):
            for k in range(n):
                other = unit(k, 2 * cx + cy, 1 - c)
                rcopy(k, 3 + j, other, other, sibling).wait_recv()
        for cp in sends:
            cp.wait_send()
        for cp in local:
            cp.wait()

    vm = pl.BlockSpec(memory_space=pltpu.VMEM)
    outs = pl.pallas_call(
        body, name="ag_weights",
        out_shape=tuple(jax.ShapeDtypeStruct(_FULL_SHAPES[w], BF) for w in tuple(ws) + tuple(later_ws))
        + (jax.ShapeDtypeStruct((N_HEADS, QB, KB), F32),),
        in_specs=[vm] * (n + m + 1), out_specs=[_ANY] * (n + m) + [vm],
        scratch_shapes=[pltpu.VMEM(_SHARD_SHAPES[w], BF) for w in tuple(ws) + tuple(later_ws)]
        + [pltpu.SemaphoreType.DMA((n, 6)), pltpu.SemaphoreType.DMA((n, 6)), pltpu.SemaphoreType.DMA((n, 2)),
           pltpu.SemaphoreType.DMA((m,))],
        compiler_params=_params(48),
    )(*shards, *later_shards, gp)
    return list(outs[:n]), list(outs[n:n + m]), outs[-1]


def _shard_of(ref, w, s):
    if w == 0:
        return ref.at[:, pl.ds(_mo(s * SHARD_IN, 128), SHARD_IN)]
    if w == 3:
        return ref.at[pl.ds(_mo(s * 256, 256), 256), :]
    return ref.at[:, pl.ds(_mo(s * 256, 128), 256)]


def _gather_copies(ws):
    def copies(refs, send_sems, recv_sems):
        x, y, c, chips = _mesh_pos()
        out = []
        for j, (cx, cy) in enumerate(chips):
            for k, w in enumerate(ws):
                mine = _shard_of(refs[k], w, 2 * x + y)
                out.append(pltpu.make_async_remote_copy(
                    src_ref=mine, dst_ref=mine, send_sem=send_sems.at[3 * k + j], recv_sem=recv_sems.at[3 * k + j],
                    device_id=(cx, cy, c), device_id_type=MESH))
        return out
    return copies


def _inproj_fwd(x, norm_g, w_in_bf, tm=512, after=()):
    S = x.shape[0]

    def body(x_ref, g_ref, w_ref, ht_ref, q_ref, k_ref, v_ref, zr_ref):
        xv = x_ref[...]
        r = lax.rsqrt(jnp.mean(xv * xv, axis=-1, keepdims=True) + EPS)
        hf = (xv * r) * g_ref[...]
        ht_ref[...] = hf.T.astype(BF)
        h = hf.astype(BF)
        heads = (q_ref, k_ref, v_ref)
        for j in range(D_IN // 512):
            z = _dot(h, w_ref[:, j * 512:(j + 1) * 512])
            if j < 3:
                zb = z.astype(BF)
                for hd in range(N_HEADS):
                    heads[j][hd] = zb[:, hd * HEAD_DIM:(hd + 1) * HEAD_DIM]
            else:
                zr_ref[:, (j - 3) * 512:(j - 2) * 512] = z

    head_major = jax.ShapeDtypeStruct((N_HEADS, S, HEAD_DIM), BF)
    head_spec = pl.BlockSpec((N_HEADS, tm, HEAD_DIM), lambda i: (0, i, 0))
    return pl.pallas_call(
        _after(body, 3, after), name="inproj_fwd", grid=(S // tm,),
        out_shape=(jax.ShapeDtypeStruct((D_MODEL, S), BF), head_major, head_major, head_major,
                   jax.ShapeDtypeStruct((S, D_IN - 3 * D_A), F32)),
        in_specs=[pl.BlockSpec((tm, D_MODEL), lambda i: (i, 0)),
                  pl.BlockSpec((1, D_MODEL), lambda i: (0, 0)),
                  pl.BlockSpec((D_MODEL, D_IN), lambda i: (0, 0), pipeline_mode=pl.Buffered(1))]
        + [_ANY] * len(after),
        out_specs=[pl.BlockSpec((D_MODEL, tm), lambda i: (0, i)),
                   head_spec, head_spec, head_spec,
                   pl.BlockSpec((tm, D_IN - 3 * D_A), lambda i: (i, 0))],
        compiler_params=_params(52, dimension_semantics=("arbitrary",)),
    )(x, norm_g, w_in_bf, *after)


def _skew_table(gp_row):
    row = lax.broadcasted_iota(jnp.int32, (QB, ROLL_W), 0)
    t = jnp.broadcast_to(gp_row, (QB, ROLL_W))
    for b in range(7):
        t = jnp.where(((row >> b) & 1) == 1, pltpu.roll(t, 1 << b, axis=1), t)
    return t


def _unskew_sum(d):
    row = lax.broadcasted_iota(jnp.int32, (QB, ROLL_W), 0)
    for b in range(7):
        d = jnp.where(((row >> b) & 1) == 1, pltpu.roll(d, ROLL_W - (1 << b), axis=1), d)
    return jnp.sum(d, axis=0, keepdims=True)


def _struct_mask():
    a = lax.broadcasted_iota(jnp.int32, (QB, KB), 0) // CHUNK
    b = lax.broadcasted_iota(jnp.int32, (QB, KB), 1) // CHUNK
    return (b >= a) & (b <= a + N_PREV)


def _load_kv(k_hbm, v_hbm, k_scr, v_scr, sems, S, meanwhile=lambda: None):
    zeros = jnp.zeros((N_HEADS, PADK, HEAD_DIM), BF)
    k_scr[:, 0:PADK, :] = zeros
    v_scr[:, 0:PADK, :] = zeros
    ck = pltpu.make_async_copy(k_hbm, k_scr.at[:, pl.ds(PADK, S), :], sems.at[0])
    cv = pltpu.make_async_copy(v_hbm, v_scr.at[:, pl.ds(PADK, S), :], sems.at[1])
    ck.start()
    cv.start()
    meanwhile()
    ck.wait()
    cv.wait()


_BATCH_NT = (((2,), (2,)), ((0,), (0,)))
_BATCH_NN = (((2,), (1,)), ((0,), (0,)))
_BATCH_TN = (((1,), (1,)), ((0,), (0,)))


def _bdot(a, b, dims):
    return lax.dot_general(a, b, dims, preferred_element_type=F32)


def _scaled(q):
    return q * jnp.asarray(SCALE, BF)


def _scores(qs, kb, bias, i, front):
    s = _bdot(qs, kb, _BATCH_NT) + bias
    if front:
        col = lax.broadcasted_iota(jnp.int32, (1, 1, KB), 2)
        s = jnp.where(col >= PADK - i * QB, s, NEG_INF)
    return s


def _attn_fwd(q3, k3, v3, bias):
    S = q3.shape[1]

    def body(q_ref, k_hbm, v_hbm, bias_ref, o_ref, lse_ref, k_scr, v_scr, sems):
        @pl.when(pl.program_id(0) == 0)
        def _():
            _load_kv(k_hbm, v_hbm, k_scr, v_scr, sems, S)

        def step(i, rows, front):
            start = pl.multiple_of(i * QB, QB)
            kb = k_scr[:, pl.ds(start, KB), :]
            vb = v_scr[:, pl.ds(start, KB), :]
            s = _scores(_scaled(q_ref[:, rows, :]), kb, bias_ref[...], i, front)
            m = jnp.max(s, axis=-1, keepdims=True)
            e = jnp.exp(s - m)
            l = jnp.sum(e, axis=-1, keepdims=True)
            p = e * (1.0 / l)
            o = _bdot(p.astype(BF), vb, _BATCH_NN)
            lse_ref[:, rows, :] = jnp.broadcast_to(m + jnp.log(l), (N_HEADS, QB, 128))
            for h in range(N_HEADS):
                o_ref[rows, h * HEAD_DIM:(h + 1) * HEAD_DIM] = o[h]

        def block(j, carry):
            i = pl.program_id(0) * Q_PER_STEP + j
            rows = pl.ds(pl.multiple_of(j * QB, QB), QB)
            pl.when(i < KEEP)(functools.partial(step, i, rows, True))
            pl.when(i >= KEEP)(functools.partial(step, i, rows, False))
            return carry

        lax.fori_loop(0, Q_PER_STEP, block, 0)

    rows_per_step = Q_PER_STEP * QB
    kv_scr = pltpu.VMEM((N_HEADS, S + PADK, HEAD_DIM), BF)
    return pl.pallas_call(
        body, name="attn_fwd", grid=(S // rows_per_step,),
        out_shape=(jax.ShapeDtypeStruct((S, D_A), F32), jax.ShapeDtypeStruct((N_HEADS, S, 128), F32)),
        in_specs=[pl.BlockSpec((N_HEADS, rows_per_step, HEAD_DIM), lambda g: (0, g, 0)),
                  pl.BlockSpec(memory_space=pl.ANY), pl.BlockSpec(memory_space=pl.ANY),
                  pl.BlockSpec((N_HEADS, QB, KB), lambda g: (0, 0, 0))],
        out_specs=[pl.BlockSpec((rows_per_step, D_A), lambda g: (g, 0)),
                   pl.BlockSpec((N_HEADS, rows_per_step, 128), lambda g: (0, g, 0))],
        scratch_shapes=[kv_scr, kv_scr, pltpu.SemaphoreType.DMA((2,))],
        compiler_params=_params(48, dimension_semantics=("arbitrary",)),
    )(q3, k3, v3, bias)


def _attn_bwd(q3, k3, v3, d_att3, lse, bias, after=()):
    S = q3.shape[1]
    nq = S // QB

    def body(q_ref, do_ref, k_hbm, v_hbm, lse_ref, bias_ref, dq_ref, dk_ref, dv_ref, dgp_ref,
             k_scr, v_scr, dk_acc, dv_acc, dbias_acc, pad_scr, sems):
        @pl.when(pl.program_id(0) == 0)
        def _():
            def clear():
                dk_acc[...] = jnp.zeros_like(dk_acc)
                dv_acc[...] = jnp.zeros_like(dv_acc)
                dbias_acc[...] = jnp.zeros_like(dbias_acc)
            _load_kv(k_hbm, v_hbm, k_scr, v_scr, sems, S, clear)

        def step(i, rows, front):
            start = pl.multiple_of(i * QB, QB)
            kb = k_scr[:, pl.ds(start, KB), :]
            vb = v_scr[:, pl.ds(start, KB), :]
            qs = _scaled(q_ref[:, rows, :])
            do = do_ref[:, rows, :]
            p = jnp.exp(_scores(qs, kb, bias_ref[...], i, front) - jnp.tile(lse_ref[:, rows, :], (1, 1, KB // 128)))
            dp = _bdot(do, vb, _BATCH_NT)
            ds = p * (dp - jnp.sum(dp * p, axis=-1, keepdims=True))
            dbias_acc[...] += ds
            dsb = ds.astype(BF)
            dq = _bdot(dsb, kb, _BATCH_NN) * SCALE
            for h in range(N_HEADS):
                dq_ref[rows, h * HEAD_DIM:(h + 1) * HEAD_DIM] = dq[h].astype(BF)
            dk_acc[...] += _bdot(dsb, qs, _BATCH_TN)
            dv_acc[...] += _bdot(p.astype(BF), do, _BATCH_TN)

        def block(j, carry):
            i = pl.program_id(0) * Q_PER_STEP + j
            rows = pl.ds(pl.multiple_of(j * QB, QB), QB)
            pl.when(i < KEEP)(functools.partial(step, i, rows, True))
            pl.when((i >= KEEP) & (i < nq))(functools.partial(step, i, rows, False))
            for h in range(N_HEADS):
                hs = slice(h * HEAD_DIM, (h + 1) * HEAD_DIM)
                dk_ref[rows, hs] = dk_acc[h, 0:QB, :].astype(BF)
                dv_ref[rows, hs] = dv_acc[h, 0:QB, :].astype(BF)
            dk_acc[:, 0:KB - QB, :] = dk_acc[:, QB:KB, :]
            dv_acc[:, 0:KB - QB, :] = dv_acc[:, QB:KB, :]
            dk_acc[:, KB - QB:KB, :] = jnp.zeros((N_HEADS, QB, HEAD_DIM), F32)
            dv_acc[:, KB - QB:KB, :] = jnp.zeros((N_HEADS, QB, HEAD_DIM), F32)
            return carry

        lax.fori_loop(0, Q_PER_STEP, block, 0)

        @pl.when(pl.program_id(0) == n_steps - 1)
        def _():
            lane = lax.broadcasted_iota(jnp.int32, (1, ROLL_W), 1)
            hi = (lane < 384) | (lane >= 832)
            lo = (lane > 640) & (lane < 832)
            pad_scr[...] = jnp.zeros_like(pad_scr)
            for h in range(N_HEADS):
                pad_scr[:, 0:KB] = dbias_acc[h]
                g = _unskew_sum(pad_scr[...])
                s_hi = jnp.sum(jnp.where(hi, g, 0.0), axis=-1, keepdims=True)
                s_lo = jnp.sum(jnp.where(lo, g, 0.0), axis=-1, keepdims=True)
                g = jnp.where(lane == 384, g + s_hi, g)
                g = jnp.where(lane == 640, g + s_lo, g)
                dgp_ref[h:h + 1, :] = g

    assert nq % Q_PER_STEP == 0 and KEEP % Q_PER_STEP == 0
    rows_per_step = Q_PER_STEP * QB
    n_steps = (nq + KEEP) // Q_PER_STEP
    last = nq // Q_PER_STEP - 1
    lag = KEEP // Q_PER_STEP
    kv_scr = pltpu.VMEM((N_HEADS, S + PADK, HEAD_DIM), BF)
    return pl.pallas_call(
        _after(body, 6, after), name="attn_bwd", grid=(n_steps,),
        out_shape=(jax.ShapeDtypeStruct((S, D_A), BF), jax.ShapeDtypeStruct((S, D_A), BF),
                   jax.ShapeDtypeStruct((S, D_A), BF), jax.ShapeDtypeStruct((N_HEADS, ROLL_W), F32)),
        in_specs=[pl.BlockSpec((N_HEADS, rows_per_step, HEAD_DIM), lambda g: (0, jnp.minimum(g, last), 0)),
                  pl.BlockSpec((N_HEADS, rows_per_step, HEAD_DIM), lambda g: (0, jnp.minimum(g, last), 0)),
                  pl.BlockSpec(memory_space=pl.ANY), pl.BlockSpec(memory_space=pl.ANY),
                  pl.BlockSpec((N_HEADS, rows_per_step, 128), lambda g: (0, jnp.minimum(g, last), 0)),
                  pl.BlockSpec((N_HEADS, QB, KB), lambda g: (0, 0, 0))] + [_ANY] * len(after),
        out_specs=[pl.BlockSpec((rows_per_step, D_A), lambda g: (jnp.minimum(g, last), 0)),
                   pl.BlockSpec((rows_per_step, D_A), lambda g: (jnp.maximum(g - lag, 0), 0)),
                   pl.BlockSpec((rows_per_step, D_A), lambda g: (jnp.maximum(g - lag, 0), 0)),
                   pl.BlockSpec((N_HEADS, ROLL_W), lambda g: (0, 0))],
        scratch_shapes=[kv_scr, kv_scr,
                        pltpu.VMEM((N_HEADS, KB, HEAD_DIM), F32), pltpu.VMEM((N_HEADS, KB, HEAD_DIM), F32),
                        pltpu.VMEM((N_HEADS, QB, KB), F32), pltpu.VMEM((QB, ROLL_W), F32),
                        pltpu.SemaphoreType.DMA((2,))],
        compiler_params=_params(56, dimension_semantics=("arbitrary",)),
    )(q3, d_att3, k3, v3, lse, bias, *after)


def _sgu_core(ub, vb, lg, lb):
    u, du = _gelu_and_grad(ub)
    v, dv = _gelu_and_grad(vb)
    mu = jnp.mean(v, axis=-1, keepdims=True)
    vc = v - mu
    rstd = lax.rsqrt(jnp.mean(vc * vc, axis=-1, keepdims=True) + EPS)
    xh = vc * rstd
    vn = xh * lg + lb
    return u, du, dv, rstd, xh, vn


def _tri():
    r = lax.broadcasted_iota(jnp.int32, (SGU_CHUNK, SGU_CHUNK), 0)
    c = lax.broadcasted_iota(jnp.int32, (SGU_CHUNK, SGU_CHUNK), 1)
    return r >= c


def _tail_sgu(att, zrest, x, target, w_pa, w_pb, w_out, b_gate, final_g, ln_g, ln_b, w_s, b_s_t, tm=256):
    S = x.shape[0]
    nt = S // tm
    chunks = tm // SGU_CHUNK

    def body(att_ref, ga_ref, ub_ref, vb_ref, gb_ref, gta_ref, gtb_ref, x_ref, t_ref,
             wpa_ref, wpb_ref, wout_ref, bg_ref, fg_ref, lg_ref, lb_ref, ws_ref, bst_ref,
             dout_ref, datt_ref, dzt_ref, dzs_ref, gwout_hbm, gwpa_hbm, gwpb_hbm,
             gbg_ref, gfg_ref, loss_ref, gws_ref, gbs_ref, glg_ref, glb_ref,
             acc_out, acc_pa, acc_pb, sg_scr, mix_scr, dvn_scr, bs_acc, sems):
        i = pl.program_id(0)

        @pl.when(i == 0)
        def _():
            for r in (acc_out, acc_pa, acc_pb, gbg_ref, gfg_ref, loss_ref, gws_ref, glg_ref, glb_ref, bs_acc):
                r[...] = jnp.zeros_like(r)

        u, du, dv, rstd, xh, vn = _sgu_core(ub_ref[...], vb_ref[...], lg_ref[...], lb_ref[...])
        vnb = vn.astype(BF)
        tri = _tri()
        blocks = [(g, slice(n * SGU_CHUNK, (n + 1) * SGU_CHUNK), slice(g * 128, (g + 1) * 128))
                  for g in range(N_GROUPS) for n in range(chunks)]
        wts = [jnp.where(tri, ws_ref[g], 0.0) for g in range(N_GROUPS)]
        for g, rs, cs in blocks:
            mixed = _dot(wts[g].astype(BF), vnb[rs, cs]) + bst_ref[:, g:g + 1]
            mix_scr[rs, cs] = mixed
            sg_scr[rs, cs] = u[rs, cs] * mixed

        att = att_ref[...]
        sg = sg_scr[...]
        sa, dsa = _silu_and_grad(ga_ref[...])
        sb, dsb = _silu_and_grad(gb_ref[...])
        ya = (att * sa).astype(BF)
        yb = (sg * sb).astype(BF)
        pa = _dot(ya, wpa_ref[...])
        pb = _dot(yb, wpb_ref[...])
        ga = _sigmoid(gta_ref[...] + bg_ref[:, 0:D_MODEL])
        gb = _sigmoid(gtb_ref[...] + bg_ref[:, D_MODEL:2 * D_MODEL])
        merged = (ga * pa + gb * pb).astype(BF)
        out = x_ref[...] + _dot(merged, wout_ref[...])
        r2 = lax.rsqrt(jnp.mean(out * out, axis=-1, keepdims=True) + EPS)
        nrm = out * r2
        fg = fg_ref[...]
        err = nrm * fg - t_ref[...]
        loss_ref[...] += 0.5 * jnp.sum(jnp.mean(err * err, axis=-1, keepdims=True))
        dy = err * (1.0 / D_MODEL)
        gfg_ref[...] += jnp.sum(dy * nrm, axis=0, keepdims=True)
        dn = dy * fg
        d_out = r2 * (dn - nrm * jnp.mean(dn * nrm, axis=-1, keepdims=True))
        dout_ref[...] = d_out
        d_outb = d_out.astype(BF)
        acc_out[...] += _dot_tn(merged, d_outb)
        dm = _dot_nt(d_outb, wout_ref[...])
        d_pa = (dm * ga).astype(BF)
        d_pb = (dm * gb).astype(BF)
        d_gta = dm * pa * (ga * (1.0 - ga))
        d_gtb = dm * pb * (gb * (1.0 - gb))
        gbg_ref[:, 0:D_MODEL] += jnp.sum(d_gta, axis=0, keepdims=True)
        gbg_ref[:, D_MODEL:2 * D_MODEL] += jnp.sum(d_gtb, axis=0, keepdims=True)
        dzt_ref[:, 2 * D_A:2 * D_A + D_MODEL] = d_gta.astype(BF)
        dzt_ref[:, 2 * D_A + D_MODEL:] = d_gtb.astype(BF)
        acc_pa[...] += _dot_tn(ya, d_pa)
        acc_pb[...] += _dot_tn(yb, d_pb)
        d_ya = _dot_nt(d_pa, wpa_ref[...])
        d_yb = _dot_nt(d_pb, wpb_ref[...])
        d_att = (d_ya * sa).astype(BF)
        for hd in range(N_HEADS):
            datt_ref[hd] = d_att[:, hd * HEAD_DIM:(hd + 1) * HEAD_DIM]
        dzt_ref[:, 0:D_A] = (d_ya * att * dsa).astype(BF)
        dzt_ref[:, D_A:2 * D_A] = (d_yb * sg * dsb).astype(BF)

        dsg = d_yb * sb
        dzs_ref[:, 0:D_B] = (dsg * mix_scr[...] * du).astype(BF)
        dmix = dsg * u
        for g, rs, cs in blocks:
            dmb = dmix[rs, cs].astype(BF)
            bs_acc[:, cs] += dmix[rs, cs]
            gws_ref[g] += _dot_nt(dmb, vnb[rs, cs])
            dvn_scr[rs, cs] = _dot(wts[g].T.astype(BF), dmb)
        dvn = dvn_scr[...]
        glg_ref[...] += jnp.sum(dvn * xh, axis=0, keepdims=True)
        glb_ref[...] += jnp.sum(dvn, axis=0, keepdims=True)
        dxh = dvn * lg_ref[...]
        dvv = rstd * (dxh - jnp.mean(dxh, axis=-1, keepdims=True)
                      - xh * jnp.mean(dxh * xh, axis=-1, keepdims=True))
        dzs_ref[:, D_B:2 * D_B] = (dvv * dv).astype(BF)

        @pl.when(i == nt - 1)
        def _():
            cps = [pltpu.make_async_copy(acc_out, gwout_hbm, sems.at[0]),
                   pltpu.make_async_copy(acc_pa, gwpa_hbm, sems.at[1]),
                   pltpu.make_async_copy(acc_pb, gwpb_hbm, sems.at[2])]
            for cp in cps:
                cp.start()
            lane = lax.broadcasted_iota(jnp.int32, (SGU_CHUNK, 128), 1)
            cols = jnp.zeros((SGU_CHUNK, 128), F32)
            for g in range(N_GROUPS):
                gws_ref[g] = jnp.where(tri, gws_ref[g], 0.0)
                col = jnp.sum(bs_acc[:, g * 128:(g + 1) * 128], axis=-1, keepdims=True)
                cols = jnp.where(lane == g, col, cols)
            gbs_ref[...] = cols
            for cp in cps:
                cp.wait()

    c2 = lambda i: (0, 0)
    c3 = lambda i: (0, 0, 0)
    zcol = lambda w, blk: pl.BlockSpec((tm, w), lambda i: (i, blk))
    row = lambda w: pl.BlockSpec((tm, w), lambda i: (i, 0))
    return pl.pallas_call(
        body, name="tail", grid=(nt,),
        out_shape=(jax.ShapeDtypeStruct((S, D_MODEL), F32), jax.ShapeDtypeStruct((N_HEADS, S, HEAD_DIM), BF),
                   jax.ShapeDtypeStruct((S, 3072), BF), jax.ShapeDtypeStruct((S, 2 * D_B), BF),
                   jax.ShapeDtypeStruct((D_MODEL, D_MODEL), F32), jax.ShapeDtypeStruct((D_A, D_MODEL), F32),
                   jax.ShapeDtypeStruct((D_B, D_MODEL), F32),
                   jax.ShapeDtypeStruct((1, 2 * D_MODEL), F32), jax.ShapeDtypeStruct((1, D_MODEL), F32),
                   jax.ShapeDtypeStruct((1, 128), F32),
                   jax.ShapeDtypeStruct((N_GROUPS, 128, 128), F32), jax.ShapeDtypeStruct((SGU_CHUNK, 128), F32),
                   jax.ShapeDtypeStruct((1, D_B), F32), jax.ShapeDtypeStruct((1, D_B), F32)),
        in_specs=[row(D_A), zcol(512, 0), zcol(512, 1), zcol(512, 2), zcol(512, 3),
                  zcol(D_MODEL, 2), zcol(D_MODEL, 3), row(D_MODEL), row(D_MODEL),
                  pl.BlockSpec((D_A, D_MODEL), c2), pl.BlockSpec((D_B, D_MODEL), c2),
                  pl.BlockSpec((D_MODEL, D_MODEL), c2),
                  pl.BlockSpec((1, 2 * D_MODEL), c2), pl.BlockSpec((1, D_MODEL), c2),
                  pl.BlockSpec((1, D_B), c2), pl.BlockSpec((1, D_B), c2),
                  pl.BlockSpec((N_GROUPS, 128, 128), c3), pl.BlockSpec((128, N_GROUPS), c2)],
        out_specs=[row(D_MODEL), pl.BlockSpec((N_HEADS, tm, HEAD_DIM), lambda i: (0, i, 0)),
                   row(3072), row(2 * D_B), _ANY, _ANY, _ANY,
                   pl.BlockSpec((1, 2 * D_MODEL), c2), pl.BlockSpec((1, D_MODEL), c2),
                   pl.BlockSpec((1, 128), c2),
                   pl.BlockSpec((N_GROUPS, 128, 128), c3), pl.BlockSpec((SGU_CHUNK, 128), c2),
                   pl.BlockSpec((1, D_B), c2), pl.BlockSpec((1, D_B), c2)],
        scratch_shapes=[pltpu.VMEM((D_MODEL, D_MODEL), F32), pltpu.VMEM((D_A, D_MODEL), F32),
                        pltpu.VMEM((D_B, D_MODEL), F32),
                        pltpu.VMEM((tm, D_B), F32), pltpu.VMEM((tm, D_B), F32), pltpu.VMEM((tm, D_B), F32),
                        pltpu.VMEM((SGU_CHUNK, D_B), F32), pltpu.SemaphoreType.DMA((3,))],
        compiler_params=_params(58, dimension_semantics=("arbitrary",)),
    )(att, zrest, zrest, zrest, zrest, zrest, zrest, x, target, w_pa, w_pb, w_out, b_gate, final_g,
      ln_g, ln_b, w_s, b_s_t)


_DZ_MAP = ((0, 0), (1, 0), (2, 0), (3, 0), (4, 0), (4, 1), (3, 1), (3, 2), (3, 3), (3, 4), (3, 5))


def _dh_gradx(dq, dk, dv, dzt, dzs, w_in_bf, x, norm_g, d_out, tm=512, after=()):
    S = x.shape[0]

    def body(dq_ref, dk_ref, dv_ref, dzt_ref, dzs_ref, w_ref, x_ref, g_ref, dout_ref, gx_ref, gn_ref):
        i = pl.program_id(0)

        @pl.when(i == 0)
        def _():
            gn_ref[...] = jnp.zeros_like(gn_ref)

        pieces = (dq_ref, dk_ref, dv_ref, dzt_ref, dzs_ref)
        dh = jnp.zeros((tm, D_MODEL), F32)
        for j, (pc, blk) in enumerate(_DZ_MAP):
            dh += _dot_nt(pieces[pc][:, blk * 512:(blk + 1) * 512], w_ref[:, j * 512:(j + 1) * 512])
        xv = x_ref[...]
        r = lax.rsqrt(jnp.mean(xv * xv, axis=-1, keepdims=True) + EPS)
        nrm = xv * r
        gn_ref[...] += jnp.sum(dh * nrm, axis=0, keepdims=True)
        dn = dh * g_ref[...]
        gx_ref[...] = r * (dn - nrm * jnp.mean(dn * nrm, axis=-1, keepdims=True)) + dout_ref[...]

    row = lambda w: pl.BlockSpec((tm, w), lambda i: (i, 0))
    c2 = lambda i: (0, 0)
    return pl.pallas_call(
        _after(body, 9, after), name="dh_gradx", grid=(S // tm,),
        out_shape=(jax.ShapeDtypeStruct((S, D_MODEL), F32), jax.ShapeDtypeStruct((1, D_MODEL), F32)),
        in_specs=[row(512), row(512), row(512), row(3072), row(1024),
                  pl.BlockSpec((D_MODEL, D_IN), c2, pipeline_mode=pl.Buffered(1)), row(D_MODEL),
                  pl.BlockSpec((1, D_MODEL), c2), row(D_MODEL)]
        + [_ANY] * len(after),
        out_specs=[row(D_MODEL), pl.BlockSpec((1, D_MODEL), c2)],
        compiler_params=_params(48, dimension_semantics=("arbitrary",)),
    )(dq, dk, dv, dzt, dzs, w_in_bf, x, norm_g, d_out, *after)


def _gw_in(ht, dq, dk, dv, dzt, dzs, tn=512, after=()):
    S = ht.shape[1]
    per = 512 // tn
    cols = tuple((pc, per * blk + h) for pc, blk in _DZ_MAP for h in range(per))

    def body(ht_ref, dq_ref, dk_ref, dv_ref, dzt_ref, dzs_ref, o_ref, ob_ref):
        j = pl.program_id(0)
        pieces = (dq_ref, dk_ref, dv_ref, dzt_ref, dzs_ref)
        for pc in range(5):
            hit = functools.reduce(jnp.logical_or, [j == jj for jj, (p, _) in enumerate(cols) if p == pc])

            @pl.when(hit)
            def _(pc=pc):
                g = _dot(ht_ref[...], pieces[pc][...])
                o_ref[...] = g
                ob_ref[...] = g.astype(BF)

    def piece_spec(pc):
        cur = next(blk for p, blk in cols if p == pc)
        held = []
        for p, blk in cols:
            cur = blk if p == pc else cur
            held.append(cur)

        def index_map(j):
            blk = jnp.int32(held[0])
            for jj in range(1, len(held)):
                if held[jj] != held[jj - 1]:
                    blk = jnp.where(j >= jj, jnp.int32(held[jj]), blk)
            return (0, blk)

        return pl.BlockSpec((S, tn), index_map)

    return pl.pallas_call(
        _after(body, 6, after), name="gw_in", grid=(len(cols),),
        out_shape=(jax.ShapeDtypeStruct((D_MODEL, D_IN), F32), jax.ShapeDtypeStruct((D_MODEL, D_IN), BF)),
        in_specs=[pl.BlockSpec((D_MODEL, S), lambda j: (0, 0), pipeline_mode=pl.Buffered(1))]
        + [piece_spec(pc) for pc in range(5)]
        + [_ANY] * len(after),
        out_specs=[pl.BlockSpec((D_MODEL, tn), lambda j: (0, j)), pl.BlockSpec((D_MODEL, tn), lambda j: (0, j))],
        compiler_params=_params(56, dimension_semantics=("arbitrary",)),
    )(ht, dq, dk, dv, dzt, dzs, *after)


_HBM = pl.BlockSpec(memory_space=pltpu.HBM)
_SEM = pl.BlockSpec(memory_space=pltpu.SEMAPHORE)
_ANY = pl.BlockSpec(memory_space=pl.ANY)
_EFFECT = pltpu.SideEffectType.DATAFLOW_SIDE_EFFECTING


def _in_hbm(a):
    return pltpu.with_memory_space_constraint(a, pltpu.HBM)


def _after(body, n_in, after):
    if not after:
        return body
    return lambda *refs: body(*refs[:n_in], *refs[n_in + len(after):])


class _Started:
    def __init__(self, send, recv, bufs, token):
        self.send, self.recv, self.bufs, self.token = send, recv, bufs, token


SIBLING_BARRIER_ID = 7


def _split_start(name, bufs, n_copies, copies, after=(), sibling_only=False):
    nb = len(bufs)

    def body(*refs):
        if sibling_only:
            barrier = pltpu.get_barrier_semaphore()
            x, y, c, _ = _mesh_pos()
            pl.semaphore_signal(barrier, inc=1, device_id=(x, y, 1 - c), device_id_type=MESH)
            pl.semaphore_wait(barrier, 1)
        refs = refs[:nb] + refs[nb + len(after):]
        for cp in copies(refs[:nb], refs[nb], refs[nb + 1]):
            cp.start()
        refs[-1][...] = jnp.zeros_like(refs[-1])

    outs = pl.pallas_call(
        body, name=name,
        out_shape=(pltpu.SemaphoreType.DMA((n_copies,)), pltpu.SemaphoreType.DMA((n_copies,)),
                   *[pltpu.HBM(b.shape, b.dtype) for b in bufs], jax.ShapeDtypeStruct((8, 128), F32)),
        in_specs=[_HBM] * nb + [_ANY] * len(after),
        out_specs=(_SEM, _SEM, *[_HBM] * nb, pl.BlockSpec(memory_space=pltpu.VMEM)),
        input_output_aliases={k: 2 + k for k in range(nb)},
        compiler_params=_params(1, has_side_effects=_EFFECT,
                                **({"collective_id": SIBLING_BARRIER_ID} if sibling_only else {})),
    )(*[_in_hbm(b) for b in bufs], *after)
    return _Started(outs[0], outs[1], list(outs[2:2 + nb]), outs[-1])


def _split_wait(name, started, copies, after):
    nb = len(started.bufs)

    def body(*refs):
        for cp in copies(refs[:nb], refs[nb], refs[nb + 1]):
            cp.wait_send()
            cp.wait_recv()

    return list(pl.pallas_call(
        body, name=name,
        out_shape=tuple(pltpu.HBM(b.shape, b.dtype) for b in started.bufs),
        in_specs=[_HBM] * nb + [_SEM, _SEM, _ANY],
        out_specs=tuple([_HBM] * nb),
        input_output_aliases={k: k for k in range(nb)},
        compiler_params=_params(1, has_side_effects=_EFFECT),
    )(*started.bufs, started.send, started.recv, after))


def _x1_copies(ws):
    def copies(refs, send_sems, recv_sems):
        x, y, c, _ = _mesh_pos()
        out = []
        for k, w in enumerate(ws):
            for s in range(N_SHARD):
                out.append(pltpu.make_async_remote_copy(
                    src_ref=_UNITS[w](refs[k], s, 1 - c), dst_ref=refs[len(ws) + k].at[s],
                    send_sem=send_sems.at[N_SHARD * k + s], recv_sem=recv_sems.at[N_SHARD * k + s],
                    device_id=(x, y, 1 - c), device_id_type=MESH))
        return out
    return copies


def _x2_copies(n):
    def copies(refs, send_sems, recv_sems):
        x, y, c, chips = _mesh_pos()
        out = []
        for j, (cx, cy) in enumerate(chips):
            for k in range(n):
                out.append(pltpu.make_async_remote_copy(
                    src_ref=refs[k].at[2 * cx + cy], dst_ref=refs[n + k].at[j],
                    send_sem=send_sems.at[3 * k + j], recv_sem=recv_sems.at[3 * k + j],
                    device_id=(cx, cy, c), device_id_type=MESH))
        return out
    return copies


def _x3_copies(ws):
    def copies(refs, send_sems, recv_sems):
        x, y, c, _ = _mesh_pos()
        out = []
        for k, w in enumerate(ws):
            rows = _HALF_ROWS[w]
            mine = refs[k].at[pl.ds(_mo(c * rows, rows), rows), :]
            out.append(pltpu.make_async_remote_copy(
                src_ref=mine, dst_ref=mine, send_sem=send_sems.at[k], recv_sem=recv_sems.at[k],
                device_id=(x, y, 1 - c), device_id_type=MESH))
        return out
    return copies


def _x1_lands(ws, dtype=F32):
    return [lax.empty((N_SHARD,) + _UNIT_SHAPES[w], dtype) for w in ws]


def _x2_lands(ws):
    return [lax.empty((3,) + _UNIT_SHAPES[w], BF) for w in ws]


def _grad_add1(w, g, recv, pos):
    ur, uc = _UNIT_SHAPES[w]

    def body(pos_ref, g_ref, r_ref, own_ref, csb_ref):
        v = g_ref[...] + r_ref[0].astype(F32)
        csb_ref[0] = v.astype(BF)

        @pl.when(pl.program_id(0) == pos_ref[1])
        def _():
            own_ref[...] = v

    u3 = lambda s, pos: (s, 0, 0)
    return pl.pallas_call(
        body, name=f"grad_add1_{w}",
        grid_spec=pltpu.PrefetchScalarGridSpec(
            num_scalar_prefetch=1, grid=(N_SHARD,),
            in_specs=[pl.BlockSpec((ur, uc), lambda s, pos: (pos[0], s)), pl.BlockSpec((1, ur, uc), u3)],
            out_specs=[pl.BlockSpec((ur, uc), lambda s, pos: (0, 0)), pl.BlockSpec((1, ur, uc), u3)]),
        out_shape=(jax.ShapeDtypeStruct((ur, uc), F32), jax.ShapeDtypeStruct((N_SHARD, ur, uc), BF)),
        compiler_params=_params(40, dimension_semantics=("arbitrary",)),
    )(pos, g, recv)


def _grad_add1_group(ws, gs, recvs):
    n = len(ws)

    def body(*refs):
        c = lax.axis_index("c")
        for k, w in enumerate(ws):
            g, r, cs, csb = refs[k], refs[n + k], refs[2 * n + k], refs[3 * n + k]
            for s in range(N_SHARD):
                v = _UNITS[w](g, s, c)[...] + r[s]
                cs[s] = v
                csb[s] = v.astype(BF)

    vm = pl.BlockSpec(memory_space=pltpu.VMEM)
    outs = pl.pallas_call(
        body, name="grad_add1_group",
        out_shape=tuple(jax.ShapeDtypeStruct((N_SHARD,) + _UNIT_SHAPES[w], dt) for dt in (F32, BF) for w in ws),
        in_specs=[vm] * (2 * n), out_specs=[vm] * (2 * n),
        compiler_params=_params(32),
    )(*gs, *recvs)
    return list(outs[:n]), list(outs[n:])


def _grad_add2_group(ws, css, recvs):
    n = len(ws)

    def body(*refs):
        x, y, c, _ = _mesh_pos()
        for k, w in enumerate(ws):
            cs, r, o = refs[k], refs[n + k], refs[2 * n + k]
            rows = _HALF_ROWS[w]
            total = ((cs[2 * x + y] + r[0].astype(F32)) + r[1].astype(F32)) + r[2].astype(F32)
            o[pl.ds(_mo(c * rows, rows), rows), :] = total

    vm = pl.BlockSpec(memory_space=pltpu.VMEM)
    return list(pl.pallas_call(
        body, name="grad_add2_group",
        out_shape=tuple(jax.ShapeDtypeStruct(_SHARD_SHAPES[w], F32) for w in ws),
        in_specs=[vm] * (2 * n), out_specs=[vm] * n,
        compiler_params=_params(32),
    )(*css, *recvs))


def _grad_add2(w, own, recv, pos):
    ur, uc = _UNIT_SHAPES[w]
    nt = 4
    tr = ur // nt

    def body(pos_ref, own_ref, r_ref, o_ref):
        o_ref[...] = ((own_ref[...] + r_ref[0].astype(F32)) + r_ref[1].astype(F32)) + r_ref[2].astype(F32)

    return pl.pallas_call(
        body, name=f"grad_add2_{w}",
        grid_spec=pltpu.PrefetchScalarGridSpec(
            num_scalar_prefetch=1, grid=(nt,),
            in_specs=[pl.BlockSpec((tr, uc), lambda t, pos: (t, 0)),
                      pl.BlockSpec((3, tr, uc), lambda t, pos: (0, t, 0))],
            out_specs=pl.BlockSpec((tr, uc), lambda t, pos: (pos[0] * nt + t, 0))),
        out_shape=jax.ShapeDtypeStruct(_SHARD_SHAPES[w], F32),
        compiler_params=_params(32, dimension_semantics=("arbitrary",)),
    )(pos, own, recv)


def _adamw_math(w, g, m, v):
    m = ADAM_B1 * m + (1.0 - ADAM_B1) * g
    v = ADAM_B2 * v + (1.0 - ADAM_B2) * (g * g)
    m_hat = m / ADAM_C1
    v_hat = v / ADAM_C2
    delta = -ADAM_LR * (m_hat / (jnp.sqrt(v_hat) + ADAM_EPS) + ADAM_WD * w)
    return delta, m, v


def _adamw_group(ws_, gs, ms, vs, after=()):
    n = len(ws_)

    def body(*refs):
        for k in range(n):
            w, g, m, v = (refs[j * n + k] for j in range(4))
            d, nm, nv, gc = (refs[(4 + j) * n + k] for j in range(4))
            gv = g[...]
            d[...], nm[...], nv[...] = _adamw_math(w[...], gv, m[...], v[...])
            gc[...] = gv

    vm = pl.BlockSpec(memory_space=pltpu.VMEM)
    outs = pl.pallas_call(
        _after(body, 4 * n, after), name="adamw_group",
        out_shape=tuple(jax.ShapeDtypeStruct(a.shape, F32) for _ in range(4) for a in ws_),
        in_specs=[vm] * (4 * n) + [_ANY] * len(after), out_specs=[vm] * (4 * n),
        compiler_params=_params(32),
    )(*ws_, *gs, *ms, *vs, *after)
    return [tuple(outs[j * n + k] for j in range(4)) for k in range(n)]


def _adamw(name, w, g, m, v, tr=256, after=()):
    rows, cols = w.shape

    def body(w_ref, g_ref, m_ref, v_ref, d_ref, nm_ref, nv_ref, gc_ref):
        gv = g_ref[...]
        d_ref[...], nm_ref[...], nv_ref[...] = _adamw_math(w_ref[...], gv, m_ref[...], v_ref[...])
        gc_ref[...] = gv

    spec = pl.BlockSpec((tr, cols), lambda i: (i, 0))
    return pl.pallas_call(
        _after(body, 4, after), name=name, grid=(rows // tr,),
        out_shape=tuple(jax.ShapeDtypeStruct((rows, cols), F32) for _ in range(4)),
        in_specs=[spec] * 4 + [_ANY] * len(after), out_specs=[spec] * 4,
        compiler_params=_params(32, dimension_semantics=("arbitrary",)),
    )(w, g, m, v, *after)


_REL_PAD = 384
_VEC_FIELDS = (("norm_g", 0, D_MODEL), ("b_gate", 1024, 2 * D_MODEL), ("sgu_ln_g", 3072, D_B),
               ("sgu_ln_b", 3584, D_B), ("b_s", 4096, N_GROUPS * 128), ("final_g", 4608, D_MODEL))
_LOSS_OFF = 5632
_REL_OFF = 5760
_NV = _REL_OFF + N_HEADS * _REL_PAD
_N_FIELDS = len(_VEC_FIELDS) + 2


_B_S_FIELD = [f[0] for f in _VEC_FIELDS].index("b_s")


def _assemble_row(dst, fields, transposed_b_s):
    for f, (_, off, n) in enumerate(_VEC_FIELDS):
        if transposed_b_s and f == _B_S_FIELD:
            t = fields[f][...].T
            for g in range(N_GROUPS):
                dst[:, off + 128 * g:off + 128 * (g + 1)] = t[g:g + 1, :]
        else:
            dst[:, off:off + n] = fields[f][...]
    for r in range(N_HEADS):
        dst[:, _REL_OFF + _REL_PAD * r:_REL_OFF + _REL_PAD * (r + 1)] = fields[len(_VEC_FIELDS)][r:r + 1, :]


def _small_reduce(grads, loss_row, after=()):
    n_in = _N_FIELDS + 1

    def body(*refs):
        g_refs, loss_ref = refs[:_N_FIELDS], refs[_N_FIELDS]
        out_v, out_w = refs[n_in:n_in + 2]
        mine_v, mine_w, gath_v, gath_w, send_sems, recv_sems = refs[n_in + 2:]
        x, y, c, chips = _mesh_pos()
        me, sibling = (x, y, c), (x, y, 1 - c)

        _assemble_row(mine_v, g_refs, True)
        mine_v[:, _LOSS_OFF:_LOSS_OFF + 128] = loss_ref[...]
        mine_w[...] = g_refs[-1][...].astype(BF)
        my_k = 4 * x + 2 * y + c
        gath_v[my_k] = mine_v[...]
        gath_w[my_k] = mine_w[...]

        def copy(k, gath, block, to, src=None):
            dst = gath.at[4 * block[0] + 2 * block[1] + block[2]]
            return pltpu.make_async_remote_copy(
                src_ref=dst if src is None else src, dst_ref=dst,
                send_sem=send_sems.at[k], recv_sem=recv_sems.at[k], device_id=to, device_id_type=MESH)

        bufs = ((gath_v, mine_v), (gath_w, mine_w))
        first, passed = [], []
        for b, (gath, mine) in enumerate(bufs):
            first.append(copy(7 * b, gath, me, sibling, src=mine))
            first += [copy(7 * b + 1 + j, gath, me, (*chip, c), src=mine) for j, chip in enumerate(chips)]
        for cp in first:
            cp.start()
        for b, (gath, _) in enumerate(bufs):
            for j, chip in enumerate(chips):
                copy(7 * b + 1 + j, gath, (*chip, c), me).wait_recv()
                cp = copy(7 * b + 4 + j, gath, (*chip, c), sibling)
                cp.start()
                passed.append(cp)
        for b, (gath, _) in enumerate(bufs):
            copy(7 * b, gath, sibling, me).wait_recv()
            for j, chip in enumerate(chips):
                copy(7 * b + 4 + j, gath, (*chip, 1 - c), me).wait_recv()
        for cp in first + passed:
            cp.wait_send()

        tot_v, tot_w = gath_v[0], gath_w[0].astype(F32)
        for k in range(1, 8):
            tot_v = tot_v + gath_v[k]
            tot_w = tot_w + gath_w[k].astype(F32)
        out_v[...] = tot_v
        out_w[...] = tot_w

    vm = pl.BlockSpec(memory_space=pltpu.VMEM)
    return pl.pallas_call(
        _after(body, n_in, after), name="small_reduce",
        out_shape=(jax.ShapeDtypeStruct((1, _NV), F32), jax.ShapeDtypeStruct((N_GROUPS * 128, 128), F32)),
        in_specs=[vm] * n_in + [_ANY] * len(after), out_specs=[vm] * 2,
        scratch_shapes=[pltpu.VMEM((1, _NV), F32), pltpu.VMEM((N_GROUPS * 128, 128), BF),
                        pltpu.VMEM((8, 1, _NV), F32), pltpu.VMEM((8, N_GROUPS * 128, 128), BF),
                        pltpu.SemaphoreType.DMA((14,)), pltpu.SemaphoreType.DMA((14,))],
        compiler_params=_params(32),
    )(*grads, loss_row, *after)


def _small_adamw(tot_v, tot_w, params):
    n_in = 2 + 3 * _N_FIELDS

    def body(*refs):
        tv_ref, tw_ref = refs[:2]
        p_refs = [refs[2 + k * _N_FIELDS:2 + (k + 1) * _N_FIELDS] for k in range(3)]
        outs = refs[n_in:n_in + 4 * _N_FIELDS + 1]
        wmv = refs[-1]
        for k in range(3):
            _assemble_row(wmv.at[k], p_refs[k], False)
            wmv[k, :, _LOSS_OFF:_LOSS_OFF + 128] = jnp.zeros((1, 128), F32)
        tot_v, tot_w = tv_ref[...], tw_ref[...]
        res_v = (tot_v,) + _adamw_math(wmv[0], tot_v, wmv[1], wmv[2])
        res_w = (tot_w,) + _adamw_math(p_refs[0][-1][...], tot_w, p_refs[1][-1][...], p_refs[2][-1][...])
        for kind in range(4):
            o = outs[kind * _N_FIELDS:(kind + 1) * _N_FIELDS]
            for f, (_, off, n) in enumerate(_VEC_FIELDS):
                o[f][...] = res_v[kind][:, off:off + n]
            for r in range(N_HEADS):
                o[len(_VEC_FIELDS)][r:r + 1, :] = res_v[kind][:, _REL_OFF + _REL_PAD * r:_REL_OFF + _REL_PAD * (r + 1)]
            o[-1][...] = res_w[kind]
        outs[-1][...] = tot_v[:, _LOSS_OFF:_LOSS_OFF + 128]

    field_shapes = [(1, n) for _, _, n in _VEC_FIELDS] + [(N_HEADS, _REL_PAD), (N_GROUPS * 128, 128)]
    vm = pl.BlockSpec(memory_space=pltpu.VMEM)
    operands = [tot_v, tot_w] + [a for p in params for a in p]
    assert len(operands) == n_in
    outs = pl.pallas_call(
        body, name="small_adamw",
        out_shape=tuple(jax.ShapeDtypeStruct(s, F32) for _ in range(4) for s in field_shapes)
        + (jax.ShapeDtypeStruct((1, 128), F32),),
        in_specs=[vm] * n_in, out_specs=[vm] * (4 * _N_FIELDS + 1),
        scratch_shapes=[pltpu.VMEM((3, 1, _NV), F32)],
        compiler_params=_params(32),
    )(*operands)
    return [outs[k * _N_FIELDS:(k + 1) * _N_FIELDS] for k in range(4)], outs[-1]


def _small_fields(norm_g, b_gate, ln_g, ln_b, b_s, final_g, rel_bias, w_s):
    rel = jnp.pad(rel_bias.reshape(N_HEADS, N_REL), ((0, 0), (0, _REL_PAD - N_REL)))
    return (norm_g, b_gate, ln_g, ln_b, b_s.reshape(1, N_GROUPS * 128), final_g.reshape(1, D_MODEL),
            rel, w_s.reshape(N_GROUPS * 128, 128))


def _small_outputs(fields):
    n_g, b_g, l_g, l_b, b_s, f_g, rel, w_s = fields
    return (n_g, b_g, rel[:, :N_REL].reshape(1, N_HEADS, N_REL), l_g, l_b,
            w_s.reshape(1, N_GROUPS, 128, 128), b_s.reshape(1, N_GROUPS, 128), f_g.reshape(D_MODEL))


def _bias_row(rel_bias):
    hi = rel_bias[:, N_REL - 1:N_REL]
    lo = rel_bias[:, 0:1]
    return jnp.concatenate([jnp.broadcast_to(hi, (N_HEADS, 384)), rel_bias[:, ::-1],
                            jnp.broadcast_to(lo, (N_HEADS, 191)), jnp.broadcast_to(hi, (N_HEADS, 192))], axis=1)


def kernel(x, norm_g, w_in, b_gate, rel_bias, sgu_ln_g, sgu_ln_b, w_s, b_s, w_pa, w_pb, w_out, final_g, loss_target, m_norm_g, m_w_in, m_b_gate, m_rel_bias, m_sgu_ln_g, m_sgu_ln_b, m_w_s, m_b_s, m_w_pa, m_w_pb, m_w_out, m_final_g, v_norm_g, v_w_in, v_b_gate, v_rel_bias, v_sgu_ln_g, v_sgu_ln_b, v_w_s, v_b_s, v_w_pa, v_w_pb, v_w_out, v_final_g):
    S = x.shape[1]
    xs = x.reshape(S, D_MODEL)
    tgt = loss_target.reshape(S, D_MODEL)
    big_w = (w_in[0], w_pa[0], w_pb[0], w_out[0])
    big_m = (m_w_in[0], m_w_pa[0], m_w_pb[0], m_w_out[0])
    big_v = (v_w_in[0], v_w_pa[0], v_w_pb[0], v_w_out[0])
    rel = rel_bias[0]
    ws = w_s[0]
    bst = b_s[0].T
    fg = final_g.reshape(1, D_MODEL)
    pos = jnp.stack([lax.axis_index("c"), 2 * lax.axis_index("x") + lax.axis_index("y")]).astype(jnp.int32)

    (w_in_bf,), staged, band_bias = _ag_weights((0,), big_w[:1], (1, 2, 3), big_w[1:], _bias_row(rel))
    ag_s = _split_start("ag_small_start", staged, 9, _gather_copies((1, 2, 3)), after=(w_in_bf,))

    ht, q3, k3, v3, zrest = _inproj_fwd(xs, norm_g, w_in_bf, after=(ag_s.token,))
    att, lse = _attn_fwd(q3, k3, v3, band_bias)
    w_pa_bf, w_pb_bf, w_out_bf = _split_wait("ag_small_wait", ag_s, _gather_copies((1, 2, 3)), att)
    (d_out, d_att, dzt, dzs, gw_out, gw_pa, gw_pb, g_bgate, g_final, loss_row,
     g_ws, g_bs_t, g_lng, g_lnb) = _tail_sgu(
        att, zrest, xs, tgt, w_pa_bf, w_pb_bf, w_out_bf, b_gate, fg, sgu_ln_g, sgu_ln_b, ws, bst)
    ws_s, ws_i = (1, 2, 3), (0,)

    x1s = _split_start("gx1s_start", [gw_pa, gw_pb, gw_out] + _x1_lands(ws_s), 12, _x1_copies(ws_s),
                       sibling_only=True)
    dq, dk, dv, d_gp = _attn_bwd(q3, k3, v3, d_att, lse, band_bias, after=(x1s.token,))
    got = _split_wait("gx1s_wait", x1s, _x1_copies(ws_s), dq)
    cs_s, csb_s = _grad_add1_group(ws_s, got[:3], got[3:])

    x2s = _split_start("gx2s_start", csb_s + _x2_lands(ws_s), 9, _x2_copies(3))
    gw_in, gw_in_bf = _gw_in(ht, dq, dk, dv, dzt, dzs, after=(x2s.token,))
    x1i = _split_start("gx1i_start", [gw_in_bf] + _x1_lands(ws_i, BF), 4, _x1_copies(ws_i), sibling_only=True)
    got = _split_wait("gx2s_wait", x2s, _x2_copies(3), x1i.token)
    halves_s = _grad_add2_group(ws_s, cs_s, got[3:])
    x3s = _split_start("gx3s_start", halves_s, 3, _x3_copies(ws_s), sibling_only=True)
    got = _split_wait("gx1i_wait", x1i, _x1_copies(ws_i), x3s.token)
    sum_i = _grad_add1(0, gw_in, got[1], pos)

    x2i = _split_start("gx2i_start", [sum_i[1]] + _x2_lands(ws_i), 3, _x2_copies(1))
    grad_x, g_norm = _dh_gradx(dq, dk, dv, dzt, dzs, w_in_bf, xs, norm_g, d_out, after=(x2i.token,))
    g_shards_s = _split_wait("gx3s_wait", x3s, _x3_copies(ws_s), grad_x)
    got = _split_wait("gx2i_wait", x2i, _x2_copies(1), grad_x)
    half_i = _grad_add2(0, sum_i[0], got[1], pos)
    x3i = _split_start("gx3i_start", [half_i], 1, _x3_copies(ws_i), sibling_only=True)
    big = [None] * 4
    big[1:] = _adamw_group(big_w[1:], g_shards_s, big_m[1:], big_v[1:], after=(x3i.token,))

    g_rel = jnp.pad(d_gp[:, 384:384 + N_REL][:, ::-1], ((0, 0), (0, _REL_PAD - N_REL)))
    small_grads = (g_norm, g_bgate, g_lng, g_lnb, g_bs_t, g_final, g_rel, g_ws.reshape(N_GROUPS * 128, 128))
    small_params = (_small_fields(norm_g, b_gate, sgu_ln_g, sgu_ln_b, b_s, final_g, rel_bias, w_s),
                    _small_fields(m_norm_g, m_b_gate, m_sgu_ln_g, m_sgu_ln_b, m_b_s, m_final_g, m_rel_bias, m_w_s),
                    _small_fields(v_norm_g, v_b_gate, v_sgu_ln_g, v_sgu_ln_b, v_b_s, v_final_g, v_rel_bias, v_w_s))
    tot_v, tot_w = _small_reduce(small_grads, loss_row, after=(x3i.token,))
    (gsum, sdelta, sm, sv), loss_out = _small_adamw(tot_v, tot_w, small_params)

    g_shard_i, = _split_wait("gx3i_wait", x3i, _x3_copies(ws_i), loss_out)
    big[0] = _adamw("adamw_w_in", big_w[0], g_shard_i, big_m[0], big_v[0])
    sg_out, sd_out, sm_out, sv_out = (_small_outputs(f) for f in (gsum, sdelta, sm, sv))
    loss = loss_out[0, 0]

    def assemble(small, bigs):
        n_g, b_g, r_b, l_g, l_b, w_s_, b_s_, f_g = small
        b_in, b_pa, b_pb, b_out = (b[None] for b in bigs)
        return (n_g, b_in, b_g, r_b, l_g, l_b, w_s_, b_s_, b_pa, b_pb, b_out, f_g)

    grads_out = assemble(sg_out, [b[3] for b in big])
    delta_out = assemble(sd_out, [b[0] for b in big])
    m_out = assemble(sm_out, [b[1] for b in big])
    v_out = assemble(sv_out, [b[2] for b in big])
    return (loss, grad_x.reshape(1, S, D_MODEL), *grads_out, *delta_out, *m_out, *v_out)
```

```python
import functools
import math

import jax
import jax.numpy as jnp
from jax import lax
from jax.experimental import pallas as pl
from jax.experimental.pallas import tpu as pltpu

F32 = jnp.float32
BF = jnp.bfloat16
MESH = pl.DeviceIdType.MESH

D_MODEL = 1024
D_A = 512
D_B = 512
D_IN = 5632
N_HEADS = 8
HEAD_DIM = 64
CHUNK = 64
N_PREV = 8
SGU_CHUNK = 128
N_GROUPS = 4
N_REL = 257
EPS = 1e-6
NEG_INF = -1e30
SCALE = HEAD_DIM ** -0.5

QB = 2 * CHUNK
KB = (N_PREV + 2) * CHUNK
PADK = N_PREV * CHUNK
ROLL_W = 1024
KEEP = KB // QB - 1
Q_PER_STEP = 2

ADAM_LR = 0.001
ADAM_B1 = 0.9
ADAM_B2 = 0.999
ADAM_EPS = 1e-08
ADAM_WD = 0.01
ADAM_STEP = 10
ADAM_C1 = 1.0 - ADAM_B1 ** ADAM_STEP
ADAM_C2 = 1.0 - ADAM_B2 ** ADAM_STEP

N_SHARD = 4
SHARD_IN = D_IN // N_SHARD
MIB = 1024 * 1024


V7X_VMEM_MIB = 64
VMEM_RESERVE_MIB = V7X_VMEM_MIB - 4


def _params(vmem_mib, **kw):
    assert vmem_mib <= VMEM_RESERVE_MIB
    return pltpu.CompilerParams(vmem_limit_bytes=VMEM_RESERVE_MIB * MIB, **kw)


def _sigmoid(x):
    return 1.0 / (1.0 + jnp.exp(-x))


def _silu_and_grad(x):
    s = _sigmoid(x)
    return x * s, s * (1.0 + x * (1.0 - s))


_GELU_C = math.sqrt(2.0 / math.pi)
_GELU_A = 0.044715


def _gelu_and_grad(x):
    x2 = x * x
    t = jnp.tanh(_GELU_C * (x + _GELU_A * (x2 * x)))
    cdf = 0.5 * (1.0 + t)
    grad = cdf + 0.5 * x * (1.0 - t * t) * (_GELU_C * (1.0 + 3.0 * _GELU_A * x2))
    return x * cdf, grad


def _dot(a, b):
    return jnp.dot(a, b, preferred_element_type=F32)


def _dot_nt(a, b):
    return lax.dot_general(a, b, (((1,), (1,)), ((), ())), preferred_element_type=F32)


def _dot_tn(a, b):
    return lax.dot_general(a, b, (((0,), (0,)), ((), ())), preferred_element_type=F32)


def _mo(v, m):
    return v if isinstance(v, int) else pl.multiple_of(v, m)


def _unit_in(ref, s, p):
    return ref.at[pl.ds(_mo(p * 512, 512), 512), pl.ds(_mo(s * SHARD_IN, 128), SHARD_IN)]


def _unit_p(ref, s, p):
    return ref.at[pl.ds(_mo(p * 256, 256), 256), pl.ds(_mo(s * 256, 128), 256)]


def _unit_out(ref, s, p):
    return ref.at[pl.ds(_mo(s * 256 + p * 128, 128), 128), :]


_UNITS = (_unit_in, _unit_p, _unit_p, _unit_out)
_HALF_ROWS = (512, 256, 256, 128)
_UNIT_SHAPES = ((512, SHARD_IN), (256, 256), (256, 256), (128, D_MODEL))
_FULL_SHAPES = ((D_MODEL, D_IN), (D_A, D_MODEL), (D_B, D_MODEL), (D_MODEL, D_MODEL))
_SHARD_SHAPES = ((D_MODEL, SHARD_IN), (D_A, 256), (D_B, 256), (256, D_MODEL))


def _mesh_pos():
    x, y, c = lax.axis_index("x"), lax.axis_index("y"), lax.axis_index("c")
    chips = [(1 - x, y), (x, 1 - y), (1 - x, 1 - y)]
    return x, y, c, chips


def _ag_weights(ws, shards, later_ws, later_shards, gp):
    n, m = len(ws), len(later_ws)

    def body(*refs):
        ins, later_ins, gp_ref = refs[:n], refs[n:n + m], refs[n + m]
        o = n + m + 1
        outs, later_outs, bias_ref = refs[o:o + n], refs[o + n:o + n + m], refs[o + n + m]
        o += n + m + 1
        stage, later_stage = refs[o:o + n], refs[o + n:o + n + m]
        send_sems, recv_sems, local_sems, later_sems = refs[o + n + m:]
        x, y, c, chips = _mesh_pos()
        s_me = 2 * x + y
        sibling = (x, y, 1 - c)
        def rows_of(k, p):
            rows = _HALF_ROWS[ws[k]]
            return pl.ds(_mo(p * rows, rows), rows)

        def half(k, p):
            return stage[k].at[rows_of(k, p), :]

        def unit(k, s, p):
            return _UNITS[ws[k]](outs[k], s, p)

        def rcopy(k, i, src, dst, to):
            return pltpu.make_async_remote_copy(src_ref=src, dst_ref=dst, send_sem=send_sems.at[k, i],
                                                recv_sem=recv_sems.at[k, i], device_id=to, device_id_type=MESH)

        for k in range(n):
            stage[k][rows_of(k, c), :] = ins[k][rows_of(k, c), :].astype(BF)
        sends = []
        for j, (cx, cy) in enumerate(chips):
            for k in range(n):
                cp = rcopy(k, j, half(k, c), unit(k, s_me, c), (cx, cy, c))
                cp.start()
                sends.append(cp)
        for k in range(n):
            stage[k][rows_of(k, 1 - c), :] = ins[k][rows_of(k, 1 - c), :].astype(BF)
        local = []
        for k in range(n):
            for p in range(2):
                cp = pltpu.make_async_copy(half(k, p), unit(k, s_me, p), local_sems.at[k, p])
                cp.start()
                local.append(cp)
        for k, w in enumerate(later_ws):
            later_stage[k][...] = later_ins[k][...].astype(BF)
            cp = pltpu.make_async_copy(later_stage[k], _shard_of(later_outs[k], w, s_me), later_sems.at[k])
            cp.start()
            local.append(cp)
        keep = _struct_mask()
        for h in range(N_HEADS):
            bias_ref[h] = jnp.where(keep, _skew_table(gp_ref[h:h + 1, :])[:, :KB], NEG_INF)
        for j, (cx, cy) in enumerate(chips):
            for k in range(n):
                landed = unit(k, 2 * cx + cy, c)
                rcopy(k, j, landed, landed, (cx, cy, c)).wait_recv()
                cp = rcopy(k, 3 + j, landed, landed, sibling)
                cp.start()
                sends.append(cp)
        for j, (cx, cy) in enumerate(chips):
            for k in range(n):
                other = unit(k, 2 * cx + cy, 1 - c)
                rcopy(k, 3 + j, other, other, sibling).wait_recv()
        for cp in sends:
            cp.wait_send()
        for cp in local:
            cp.wait()

    vm = pl.BlockSpec(memory_space=pltpu.VMEM)
    outs = pl.pallas_call(
        body, name="ag_weights",
        out_shape=tuple(jax.ShapeDtypeStruct(_FULL_SHAPES[w], BF) for w in tuple(ws) + tuple(later_ws))
        + (jax.ShapeDtypeStruct((N_HEADS, QB, KB), F32),),
        in_specs=[vm] * (n + m + 1), out_specs=[_ANY] * (n + m) + [vm],
        scratch_shapes=[pltpu.VMEM(_SHARD_SHAPES[w], BF) for w in tuple(ws) + tuple(later_ws)]
        + [pltpu.SemaphoreType.DMA((n, 6)), pltpu.SemaphoreType.DMA((n, 6)), pltpu.SemaphoreType.DMA((n, 2)),
           pltpu.SemaphoreType.DMA((m,))],
        compiler_params=_params(48),
    )(*shards, *later_shards, gp)
    return list(outs[:n]), list(outs[n:n + m]), outs[-1]


def _shard_of(ref, w, s):
    if w == 0:
        return ref.at[:, pl.ds(_mo(s * SHARD_IN, 128), SHARD_IN)]
    if w == 3:
        return ref.at[pl.ds(_mo(s * 256, 256), 256), :]
    return ref.at[:, pl.ds(_mo(s * 256, 128), 256)]


def _gather_copies(ws):
    def copies(refs, send_sems, recv_sems):
        x, y, c, chips = _mesh_pos()
        out = []
        for j, (cx, cy) in enumerate(chips):
            for k, w in enumerate(ws):
                mine = _shard_of(refs[k], w, 2 * x + y)
                out.append(pltpu.make_async_remote_copy(
                    src_ref=mine, dst_ref=mine, send_sem=send_sems.at[3 * k + j], recv_sem=recv_sems.at[3 * k + j],
                    device_id=(cx, cy, c), device_id_type=MESH))
        return out
    return copies


def _inproj_fwd(x, norm_g, w_in_bf, tm=512, after=()):
    S = x.shape[0]

    def body(x_ref, g_ref, w_ref, ht_ref, q_ref, k_ref, v_ref, zr_ref):
        xv = x_ref[...]
        r = lax.rsqrt(jnp.mean(xv * xv, axis=-1, keepdims=True) + EPS)
        hf = (xv * r) * g_ref[...]
        ht_ref[...] = hf.T.astype(BF)
        h = hf.astype(BF)
        heads = (q_ref, k_ref, v_ref)
        for j in range(D_IN // 512):
            z = _dot(h, w_ref[:, j * 512:(j + 1) * 512])
            if j < 3:
                zb = z.astype(BF)
                for hd in range(N_HEADS):
                    heads[j][hd] = zb[:, hd * HEAD_DIM:(hd + 1) * HEAD_DIM]
            else:
                zr_ref[:, (j - 3) * 512:(j - 2) * 512] = z

    head_major = jax.ShapeDtypeStruct((N_HEADS, S, HEAD_DIM), BF)
    head_spec = pl.BlockSpec((N_HEADS, tm, HEAD_DIM), lambda i: (0, i, 0))
    return pl.pallas_call(
        _after(body, 3, after), name="inproj_fwd", grid=(S // tm,),
        out_shape=(jax.ShapeDtypeStruct((D_MODEL, S), BF), head_major, head_major, head_major,
                   jax.ShapeDtypeStruct((S, D_IN - 3 * D_A), F32)),
        in_specs=[pl.BlockSpec((tm, D_MODEL), lambda i: (i, 0)),
                  pl.BlockSpec((1, D_MODEL), lambda i: (0, 0)),
                  pl.BlockSpec((D_MODEL, D_IN), lambda i: (0, 0), pipeline_mode=pl.Buffered(1))]
        + [_ANY] * len(after),
        out_specs=[pl.BlockSpec((D_MODEL, tm), lambda i: (0, i)),
                   head_spec, head_spec, head_spec,
                   pl.BlockSpec((tm, D_IN - 3 * D_A), lambda i: (i, 0))],
        compiler_params=_params(52, dimension_semantics=("arbitrary",)),
    )(x, norm_g, w_in_bf, *after)


def _skew_table(gp_row):
    row = lax.broadcasted_iota(jnp.int32, (QB, ROLL_W), 0)
    t = jnp.broadcast_to(gp_row, (QB, ROLL_W))
    for b in range(7):
        t = jnp.where(((row >> b) & 1) == 1, pltpu.roll(t, 1 << b, axis=1), t)
    return t


def _unskew_sum(d):
    row = lax.broadcasted_iota(jnp.int32, (QB, ROLL_W), 0)
    for b in range(7):
        d = jnp.where(((row >> b) & 1) == 1, pltpu.roll(d, ROLL_W - (1 << b), axis=1), d)
    return jnp.sum(d, axis=0, keepdims=True)


def _struct_mask():
    a = lax.broadcasted_iota(jnp.int32, (QB, KB), 0) // CHUNK
    b = lax.broadcasted_iota(jnp.int32, (QB, KB), 1) // CHUNK
    return (b >= a) & (b <= a + N_PREV)


def _load_kv(k_hbm, v_hbm, k_scr, v_scr, sems, S, meanwhile=lambda: None):
    zeros = jnp.zeros((N_HEADS, PADK, HEAD_DIM), BF)
    k_scr[:, 0:PADK, :] = zeros
    v_scr[:, 0:PADK, :] = zeros
    ck = pltpu.make_async_copy(k_hbm, k_scr.at[:, pl.ds(PADK, S), :], sems.at[0])
    cv = pltpu.make_async_copy(v_hbm, v_scr.at[:, pl.ds(PADK, S), :], sems.at[1])
    ck.start()
    cv.start()
    meanwhile()
    ck.wait()
    cv.wait()


_BATCH_NT = (((2,), (2,)), ((0,), (0,)))
_BATCH_NN = (((2,), (1,)), ((0,), (0,)))
_BATCH_TN = (((1,), (1,)), ((0,), (0,)))


def _bdot(a, b, dims):
    return lax.dot_general(a, b, dims, preferred_element_type=F32)


def _scaled(q):
    return q * jnp.asarray(SCALE, BF)


def _scores(qs, kb, bias, i, front):
    s = _bdot(qs, kb, _BATCH_NT) + bias
    if front:
        col = lax.broadcasted_iota(jnp.int32, (1, 1, KB), 2)
        s = jnp.where(col >= PADK - i * QB, s, NEG_INF)
    return s


def _attn_fwd(q3, k3, v3, bias):
    S = q3.shape[1]

    def body(q_ref, k_hbm, v_hbm, bias_ref, o_ref, lse_ref, k_scr, v_scr, sems):
        @pl.when(pl.program_id(0) == 0)
        def _():
            _load_kv(k_hbm, v_hbm, k_scr, v_scr, sems, S)

        def step(i, rows, front):
            start = pl.multiple_of(i * QB, QB)
            kb = k_scr[:, pl.ds(start, KB), :]
            vb = v_scr[:, pl.ds(start, KB), :]
            s = _scores(_scaled(q_ref[:, rows, :]), kb, bias_ref[...], i, front)
            m = jnp.max(s, axis=-1, keepdims=True)
            e = jnp.exp(s - m)
            l = jnp.sum(e, axis=-1, keepdims=True)
            p = e * (1.0 / l)
            o = _bdot(p.astype(BF), vb, _BATCH_NN)
            lse_ref[:, rows, :] = jnp.broadcast_to(m + jnp.log(l), (N_HEADS, QB, 128))
            for h in range(N_HEADS):
                o_ref[rows, h * HEAD_DIM:(h + 1) * HEAD_DIM] = o[h]

        def block(j, carry):
            i = pl.program_id(0) * Q_PER_STEP + j
            rows = pl.ds(pl.multiple_of(j * QB, QB), QB)
            pl.when(i < KEEP)(functools.partial(step, i, rows, True))
            pl.when(i >= KEEP)(functools.partial(step, i, rows, False))
            return carry

        lax.fori_loop(0, Q_PER_STEP, block, 0)

    rows_per_step = Q_PER_STEP * QB
    kv_scr = pltpu.VMEM((N_HEADS, S + PADK, HEAD_DIM), BF)
    return pl.pallas_call(
        body, name="attn_fwd", grid=(S // rows_per_step,),
        out_shape=(jax.ShapeDtypeStruct((S, D_A), F32), jax.ShapeDtypeStruct((N_HEADS, S, 128), F32)),
        in_specs=[pl.BlockSpec((N_HEADS, rows_per_step, HEAD_DIM), lambda g: (0, g, 0)),
                  pl.BlockSpec(memory_space=pl.ANY), pl.BlockSpec(memory_space=pl.ANY),
                  pl.BlockSpec((N_HEADS, QB, KB), lambda g: (0, 0, 0))],
        out_specs=[pl.BlockSpec((rows_per_step, D_A), lambda g: (g, 0)),
                   pl.BlockSpec((N_HEADS, rows_per_step, 128), lambda g: (0, g, 0))],
        scratch_shapes=[kv_scr, kv_scr, pltpu.SemaphoreType.DMA((2,))],
        compiler_params=_params(48, dimension_semantics=("arbitrary",)),
    )(q3, k3, v3, bias)


def _attn_bwd(q3, k3, v3, d_att3, lse, bias, after=()):
    S = q3.shape[1]
    nq = S // QB

    def body(q_ref, do_ref, k_hbm, v_hbm, lse_ref, bias_ref, dq_ref, dk_ref, dv_ref, dgp_ref,
             k_scr, v_scr, dk_acc, dv_acc, dbias_acc, pad_scr, sems):
        @pl.when(pl.program_id(0) == 0)
        def _():
            def clear():
                dk_acc[...] = jnp.zeros_like(dk_acc)
                dv_acc[...] = jnp.zeros_like(dv_acc)
                dbias_acc[...] = jnp.zeros_like(dbias_acc)
            _load_kv(k_hbm, v_hbm, k_scr, v_scr, sems, S, clear)

        def step(i, rows, front):
            start = pl.multiple_of(i * QB, QB)
            kb = k_scr[:, pl.ds(start, KB), :]
            vb = v_scr[:, pl.ds(start, KB), :]
            qs = _scaled(q_ref[:, rows, :])
            do = do_ref[:, rows, :]
            p = jnp.exp(_scores(qs, kb, bias_ref[...], i, front) - jnp.tile(lse_ref[:, rows, :], (1, 1, KB // 128)))
            dp = _bdot(do, vb, _BATCH_NT)
            ds = p * (dp - jnp.sum(dp * p, axis=-1, keepdims=True))
            dbias_acc[...] += ds
            dsb = ds.astype(BF)
            dq = _bdot(dsb, kb, _BATCH_NN) * SCALE
            for h in range(N_HEADS):
                dq_ref[rows, h * HEAD_DIM:(h + 1) * HEAD_DIM] = dq[h].astype(BF)
            dk_acc[...] += _bdot(dsb, qs, _BATCH_TN)
            dv_acc[...] += _bdot(p.astype(BF), do, _BATCH_TN)

        def block(j, carry):
            i = pl.program_id(0) * Q_PER_STEP + j
            rows = pl.ds(pl.multiple_of(j * QB, QB), QB)
            pl.when(i < KEEP)(functools.partial(step, i, rows, True))
            pl.when((i >= KEEP) & (i < nq))(functools.partial(step, i, rows, False))
            for h in range(N_HEADS):
                hs = slice(h * HEAD_DIM, (h + 1) * HEAD_DIM)
                dk_ref[rows, hs] = dk_acc[h, 0:QB, :].astype(BF)
                dv_ref[rows, hs] = dv_acc[h, 0:QB, :].astype(BF)
            dk_acc[:, 0:KB - QB, :] = dk_acc[:, QB:KB, :]
            dv_acc[:, 0:KB - QB, :] = dv_acc[:, QB:KB, :]
            dk_acc[:, KB - QB:KB, :] = jnp.zeros((N_HEADS, QB, HEAD_DIM), F32)
            dv_acc[:, KB - QB:KB, :] = jnp.zeros((N_HEADS, QB, HEAD_DIM), F32)
            return carry

        lax.fori_loop(0, Q_PER_STEP, block, 0)

        @pl.when(pl.program_id(0) == n_steps - 1)
        def _():
            lane = lax.broadcasted_iota(jnp.int32, (1, ROLL_W), 1)
            hi = (lane < 384) | (lane >= 832)
            lo = (lane > 640) & (lane < 832)
            pad_scr[...] = jnp.zeros_like(pad_scr)
            for h in range(N_HEADS):
                pad_scr[:, 0:KB] = dbias_acc[h]
                g = _unskew_sum(pad_scr[...])
                s_hi = jnp.sum(jnp.where(hi, g, 0.0), axis=-1, keepdims=True)
                s_lo = jnp.sum(jnp.where(lo, g, 0.0), axis=-1, keepdims=True)
                g = jnp.where(lane == 384, g + s_hi, g)
                g = jnp.where(lane == 640, g + s_lo, g)
                dgp_ref[h:h + 1, :] = g

    assert nq % Q_PER_STEP == 0 and KEEP % Q_PER_STEP == 0
    rows_per_step = Q_PER_STEP * QB
    n_steps = (nq + KEEP) // Q_PER_STEP
    last = nq // Q_PER_STEP - 1
    lag = KEEP // Q_PER_STEP
    kv_scr = pltpu.VMEM((N_HEADS, S + PADK, HEAD_DIM), BF)
    return pl.pallas_call(
        _after(body, 6, after), name="attn_bwd", grid=(n_steps,),
        out_shape=(jax.ShapeDtypeStruct((S, D_A), BF), jax.ShapeDtypeStruct((S, D_A), BF),
                   jax.ShapeDtypeStruct((S, D_A), BF), jax.ShapeDtypeStruct((N_HEADS, ROLL_W), F32)),
        in_specs=[pl.BlockSpec((N_HEADS, rows_per_step, HEAD_DIM), lambda g: (0, jnp.minimum(g, last), 0)),
                  pl.BlockSpec((N_HEADS, rows_per_step, HEAD_DIM), lambda g: (0, jnp.minimum(g, last), 0)),
                  pl.BlockSpec(memory_space=pl.ANY), pl.BlockSpec(memory_space=pl.ANY),
                  pl.BlockSpec((N_HEADS, rows_per_step, 128), lambda g: (0, jnp.minimum(g, last), 0)),
                  pl.BlockSpec((N_HEADS, QB, KB), lambda g: (0, 0, 0))] + [_ANY] * len(after),
        out_specs=[pl.BlockSpec((rows_per_step, D_A), lambda g: (jnp.minimum(g, last), 0)),
                   pl.BlockSpec((rows_per_step, D_A), lambda g: (jnp.maximum(g - lag, 0), 0)),
                   pl.BlockSpec((rows_per_step, D_A), lambda g: (jnp.maximum(g - lag, 0), 0)),
                   pl.BlockSpec((N_HEADS, ROLL_W), lambda g: (0, 0))],
        scratch_shapes=[kv_scr, kv_scr,
                        pltpu.VMEM((N_HEADS, KB, HEAD_DIM), F32), pltpu.VMEM((N_HEADS, KB, HEAD_DIM), F32),
                        pltpu.VMEM((N_HEADS, QB, KB), F32), pltpu.VMEM((QB, ROLL_W), F32),
                        pltpu.SemaphoreType.DMA((2,))],
        compiler_params=_params(56, dimension_semantics=("arbitrary",)),
    )(q3, d_att3, k3, v3, lse, bias, *after)


def _sgu_core(ub, vb, lg, lb):
    u, du = _gelu_and_grad(ub)
    v, dv = _gelu_and_grad(vb)
    mu = jnp.mean(v, axis=-1, keepdims=True)
    vc = v - mu
    rstd = lax.rsqrt(jnp.mean(vc * vc, axis=-1, keepdims=True) + EPS)
    xh = vc * rstd
    vn = xh * lg + lb
    return u, du, dv, rstd, xh, vn


def _tri():
    r = lax.broadcasted_iota(jnp.int32, (SGU_CHUNK, SGU_CHUNK), 0)
    c = lax.broadcasted_iota(jnp.int32, (SGU_CHUNK, SGU_CHUNK), 1)
    return r >= c


def _tail_sgu(att, zrest, x, target, w_pa, w_pb, w_out, b_gate, final_g, ln_g, ln_b, w_s, b_s_t, tm=256):
    S = x.shape[0]
    nt = S // tm
    chunks = tm // SGU_CHUNK

    def body(att_ref, ga_ref, ub_ref, vb_ref, gb_ref, gta_ref, gtb_ref, x_ref, t_ref,
             wpa_ref, wpb_ref, wout_ref, bg_ref, fg_ref, lg_ref, lb_ref, ws_ref, bst_ref,
             dout_ref, datt_ref, dzt_ref, dzs_ref, gwout_hbm, gwpa_hbm, gwpb_hbm,
             gbg_ref, gfg_ref, loss_ref, gws_ref, gbs_ref, glg_ref, glb_ref,
             acc_out, acc_pa, acc_pb, sg_scr, mix_scr, dvn_scr, bs_acc, sems):
        i = pl.program_id(0)

        @pl.when(i == 0)
        def _():
            for r in (acc_out, acc_pa, acc_pb, gbg_ref, gfg_ref, loss_ref, gws_ref, glg_ref, glb_ref, bs_acc):
                r[...] = jnp.zeros_like(r)

        u, du, dv, rstd, xh, vn = _sgu_core(ub_ref[...], vb_ref[...], lg_ref[...], lb_ref[...])
        vnb = vn.astype(BF)
        tri = _tri()
        blocks = [(g, slice(n * SGU_CHUNK, (n + 1) * SGU_CHUNK), slice(g * 128, (g + 1) * 128))
                  for g in range(N_GROUPS) for n in range(chunks)]
        wts = [jnp.where(tri, ws_ref[g], 0.0) for g in range(N_GROUPS)]
        for g, rs, cs in blocks:
            mixed = _dot(wts[g].astype(BF), vnb[rs, cs]) + bst_ref[:, g:g + 1]
            mix_scr[rs, cs] = mixed
            sg_scr[rs, cs] = u[rs, cs] * mixed

        att = att_ref[...]
        sg = sg_scr[...]
        sa, dsa = _silu_and_grad(ga_ref[...])
        sb, dsb = _silu_and_grad(gb_ref[...])
        ya = (att * sa).astype(BF)
        yb = (sg * sb).astype(BF)
        pa = _dot(ya, wpa_ref[...])
        pb = _dot(yb, wpb_ref[...])
        ga = _sigmoid(gta_ref[...] + bg_ref[:, 0:D_MODEL])
        gb = _sigmoid(gtb_ref[...] + bg_ref[:, D_MODEL:2 * D_MODEL])
        merged = (ga * pa + gb * pb).astype(BF)
        out = x_ref[...] + _dot(merged, wout_ref[...])
        r2 = lax.rsqrt(jnp.mean(out * out, axis=-1, keepdims=True) + EPS)
        nrm = out * r2
        fg = fg_ref[...]
        err = nrm * fg - t_ref[...]
        loss_ref[...] += 0.5 * jnp.sum(jnp.mean(err * err, axis=-1, keepdims=True))
        dy = err * (1.0 / D_MODEL)
        gfg_ref[...] += jnp.sum(dy * nrm, axis=0, keepdims=True)
        dn = dy * fg
        d_out = r2 * (dn - nrm * jnp.mean(dn * nrm, axis=-1, keepdims=True))
        dout_ref[...] = d_out
        d_outb = d_out.astype(BF)
        acc_out[...] += _dot_tn(merged, d_outb)
        dm = _dot_nt(d_outb, wout_ref[...])
        d_pa = (dm * ga).astype(BF)
        d_pb = (dm * gb).astype(BF)
        d_gta = dm * pa * (ga * (1.0 - ga))
        d_gtb = dm * pb * (gb * (1.0 - gb))
        gbg_ref[:, 0:D_MODEL] += jnp.sum(d_gta, axis=0, keepdims=True)
        gbg_ref[:, D_MODEL:2 * D_MODEL] += jnp.sum(d_gtb, axis=0, keepdims=True)
        dzt_ref[:, 2 * D_A:2 * D_A + D_MODEL] = d_gta.astype(BF)
        dzt_ref[:, 2 * D_A + D_MODEL:] = d_gtb.astype(BF)
        acc_pa[...] += _dot_tn(ya, d_pa)
        acc_pb[...] += _dot_tn(yb, d_pb)
        d_ya = _dot_nt(d_pa, wpa_ref[...])
        d_yb = _dot_nt(d_pb, wpb_ref[...])
        d_att = (d_ya * sa).astype(BF)
        for hd in range(N_HEADS):
            datt_ref[hd] = d_att[:, hd * HEAD_DIM:(hd + 1) * HEAD_DIM]
        dzt_ref[:, 0:D_A] = (d_ya * att * dsa).astype(BF)
        dzt_ref[:, D_A:2 * D_A] = (d_yb * sg * dsb).astype(BF)

        dsg = d_yb * sb
        dzs_ref[:, 0:D_B] = (dsg * mix_scr[...] * du).astype(BF)
        dmix = dsg * u
        for g, rs, cs in blocks:
            dmb = dmix[rs, cs].astype(BF)
            bs_acc[:, cs] += dmix[rs, cs]
            gws_ref[g] += _dot_nt(dmb, vnb[rs, cs])
            dvn_scr[rs, cs] = _dot(wts[g].T.astype(BF), dmb)
        dvn = dvn_scr[...]
        glg_ref[...] += jnp.sum(dvn * xh, axis=0, keepdims=True)
        glb_ref[...] += jnp.sum(dvn, axis=0, keepdims=True)
        dxh = dvn * lg_ref[...]
        dvv = rstd * (dxh - jnp.mean(dxh, axis=-1, keepdims=True)
                      - xh * jnp.mean(dxh * xh, axis=-1, keepdims=True))
        dzs_ref[:, D_B:2 * D_B] = (dvv * dv).astype(BF)

        @pl.when(i == nt - 1)
        def _():
            cps = [pltpu.make_async_copy(acc_out, gwout_hbm, sems.at[0]),
                   pltpu.make_async_copy(acc_pa, gwpa_hbm, sems.at[1]),
                   pltpu.make_async_copy(acc_pb, gwpb_hbm, sems.at[2])]
            for cp in cps:
                cp.start()
            lane = lax.broadcasted_iota(jnp.int32, (SGU_CHUNK, 128), 1)
            cols = jnp.zeros((SGU_CHUNK, 128), F32)
            for g in range(N_GROUPS):
                gws_ref[g] = jnp.where(tri, gws_ref[g], 0.0)
                col = jnp.sum(bs_acc[:, g * 128:(g + 1) * 128], axis=-1, keepdims=True)
                cols = jnp.where(lane == g, col, cols)
            gbs_ref[...] = cols
            for cp in cps:
                cp.wait()

    c2 = lambda i: (0, 0)
    c3 = lambda i: (0, 0, 0)
    zcol = lambda w, blk: pl.BlockSpec((tm, w), lambda i: (i, blk))
    row = lambda w: pl.BlockSpec((tm, w), lambda i: (i, 0))
    return pl.pallas_call(
        body, name="tail", grid=(nt,),
        out_shape=(jax.ShapeDtypeStruct((S, D_MODEL), F32), jax.ShapeDtypeStruct((N_HEADS, S, HEAD_DIM), BF),
                   jax.ShapeDtypeStruct((S, 3072), BF), jax.ShapeDtypeStruct((S, 2 * D_B), BF),
                   jax.ShapeDtypeStruct((D_MODEL, D_MODEL), F32), jax.ShapeDtypeStruct((D_A, D_MODEL), F32),
                   jax.ShapeDtypeStruct((D_B, D_MODEL), F32),
                   jax.ShapeDtypeStruct((1, 2 * D_MODEL), F32), jax.ShapeDtypeStruct((1, D_MODEL), F32),
                   jax.ShapeDtypeStruct((1, 128), F32),
                   jax.ShapeDtypeStruct((N_GROUPS, 128, 128), F32), jax.ShapeDtypeStruct((SGU_CHUNK, 128), F32),
                   jax.ShapeDtypeStruct((1, D_B), F32), jax.ShapeDtypeStruct((1, D_B), F32)),
        in_specs=[row(D_A), zcol(512, 0), zcol(512, 1), zcol(512, 2), zcol(512, 3),
                  zcol(D_MODEL, 2), zcol(D_MODEL, 3), row(D_MODEL), row(D_MODEL),
                  pl.BlockSpec((D_A, D_MODEL), c2), pl.BlockSpec((D_B, D_MODEL), c2),
                  pl.BlockSpec((D_MODEL, D_MODEL), c2),
                  pl.BlockSpec((1, 2 * D_MODEL), c2), pl.BlockSpec((1, D_MODEL), c2),
                  pl.BlockSpec((1, D_B), c2), pl.BlockSpec((1, D_B), c2),
                  pl.BlockSpec((N_GROUPS, 128, 128), c3), pl.BlockSpec((128, N_GROUPS), c2)],
        out_specs=[row(D_MODEL), pl.BlockSpec((N_HEADS, tm, HEAD_DIM), lambda i: (0, i, 0)),
                   row(3072), row(2 * D_B), _ANY, _ANY, _ANY,
                   pl.BlockSpec((1, 2 * D_MODEL), c2), pl.BlockSpec((1, D_MODEL), c2),
                   pl.BlockSpec((1, 128), c2),
                   pl.BlockSpec((N_GROUPS, 128, 128), c3), pl.BlockSpec((SGU_CHUNK, 128), c2),
                   pl.BlockSpec((1, D_B), c2), pl.BlockSpec((1, D_B), c2)],
        scratch_shapes=[pltpu.VMEM((D_MODEL, D_MODEL), F32), pltpu.VMEM((D_A, D_MODEL), F32),
                        pltpu.VMEM((D_B, D_MODEL), F32),
                        pltpu.VMEM((tm, D_B), F32), pltpu.VMEM((tm, D_B), F32), pltpu.VMEM((tm, D_B), F32),
                        pltpu.VMEM((SGU_CHUNK, D_B), F32), pltpu.SemaphoreType.DMA((3,))],
        compiler_params=_params(58, dimension_semantics=("arbitrary",)),
    )(att, zrest, zrest, zrest, zrest, zrest, zrest, x, target, w_pa, w_pb, w_out, b_gate, final_g,
      ln_g, ln_b, w_s, b_s_t)


_DZ_MAP = ((0, 0), (1, 0), (2, 0), (3, 0), (4, 0), (4, 1), (3, 1), (3, 2), (3, 3), (3, 4), (3, 5))


def _dh_gradx(dq, dk, dv, dzt, dzs, w_in_bf, x, norm_g, d_out, tm=512, after=()):
    S = x.shape[0]

    def body(dq_ref, dk_ref, dv_ref, dzt_ref, dzs_ref, w_ref, x_ref, g_ref, dout_ref, gx_ref, gn_ref):
        i = pl.program_id(0)

        @pl.when(i == 0)
        def _():
            gn_ref[...] = jnp.zeros_like(gn_ref)

        pieces = (dq_ref, dk_ref, dv_ref, dzt_ref, dzs_ref)
        dh = jnp.zeros((tm, D_MODEL), F32)
        for j, (pc, blk) in enumerate(_DZ_MAP):
            dh += _dot_nt(pieces[pc][:, blk * 512:(blk + 1) * 512], w_ref[:, j * 512:(j + 1) * 512])
        xv = x_ref[...]
        r = lax.rsqrt(jnp.mean(xv * xv, axis=-1, keepdims=True) + EPS)
        nrm = xv * r
        gn_ref[...] += jnp.sum(dh * nrm, axis=0, keepdims=True)
        dn = dh * g_ref[...]
        gx_ref[...] = r * (dn - nrm * jnp.mean(dn * nrm, axis=-1, keepdims=True)) + dout_ref[...]

    row = lambda w: pl.BlockSpec((tm, w), lambda i: (i, 0))
    c2 = lambda i: (0, 0)
    return pl.pallas_call(
        _after(body, 9, after), name="dh_gradx", grid=(S // tm,),
        out_shape=(jax.ShapeDtypeStruct((S, D_MODEL), F32), jax.ShapeDtypeStruct((1, D_MODEL), F32)),
        in_specs=[row(512), row(512), row(512), row(3072), row(1024),
                  pl.BlockSpec((D_MODEL, D_IN), c2, pipeline_mode=pl.Buffered(1)), row(D_MODEL),
                  pl.BlockSpec((1, D_MODEL), c2), row(D_MODEL)]
        + [_ANY] * len(after),
        out_specs=[row(D_MODEL), pl.BlockSpec((1, D_MODEL), c2)],
        compiler_params=_params(48, dimension_semantics=("arbitrary",)),
    )(dq, dk, dv, dzt, dzs, w_in_bf, x, norm_g, d_out, *after)


def _gw_in(ht, dq, dk, dv, dzt, dzs, tn=512, after=()):
    S = ht.shape[1]
    per = 512 // tn
    cols = tuple((pc, per * blk + h) for pc, blk in _DZ_MAP for h in range(per))

    def body(ht_ref, dq_ref, dk_ref, dv_ref, dzt_ref, dzs_ref, o_ref, ob_ref):
        j = pl.program_id(0)
        pieces = (dq_ref, dk_ref, dv_ref, dzt_ref, dzs_ref)
        for pc in range(5):
            hit = functools.reduce(jnp.logical_or, [j == jj for jj, (p, _) in enumerate(cols) if p == pc])

            @pl.when(hit)
            def _(pc=pc):
                g = _dot(ht_ref[...], pieces[pc][...])
                o_ref[...] = g
                ob_ref[...] = g.astype(BF)

    def piece_spec(pc):
        cur = next(blk for p, blk in cols if p == pc)
        held = []
        for p, blk in cols:
            cur = blk if p == pc else cur
            held.append(cur)

        def index_map(j):
            blk = jnp.int32(held[0])
            for jj in range(1, len(held)):
                if held[jj] != held[jj - 1]:
                    blk = jnp.where(j >= jj, jnp.int32(held[jj]), blk)
            return (0, blk)

        return pl.BlockSpec((S, tn), index_map)

    return pl.pallas_call(
        _after(body, 6, after), name="gw_in", grid=(len(cols),),
        out_shape=(jax.ShapeDtypeStruct((D_MODEL, D_IN), F32), jax.ShapeDtypeStruct((D_MODEL, D_IN), BF)),
        in_specs=[pl.BlockSpec((D_MODEL, S), lambda j: (0, 0), pipeline_mode=pl.Buffered(1))]
        + [piece_spec(pc) for pc in range(5)]
        + [_ANY] * len(after),
        out_specs=[pl.BlockSpec((D_MODEL, tn), lambda j: (0, j)), pl.BlockSpec((D_MODEL, tn), lambda j: (0, j))],
        compiler_params=_params(56, dimension_semantics=("arbitrary",)),
    )(ht, dq, dk, dv, dzt, dzs, *after)


_HBM = pl.BlockSpec(memory_space=pltpu.HBM)
_SEM = pl.BlockSpec(memory_space=pltpu.SEMAPHORE)
_ANY = pl.BlockSpec(memory_space=pl.ANY)
_EFFECT = pltpu.SideEffectType.DATAFLOW_SIDE_EFFECTING


def _in_hbm(a):
    return pltpu.with_memory_space_constraint(a, pltpu.HBM)


def _after(body, n_in, after):
    if not after:
        return body
    return lambda *refs: body(*refs[:n_in], *refs[n_in + len(after):])


class _Started:
    def __init__(self, send, recv, bufs, token):
        self.send, self.recv, self.bufs, self.token = send, recv, bufs, token


_PEER_SETS = {"sibling": 7, "chips": 8}


def _peers(kind):
    x, y, c, chips = _mesh_pos()
    return [(x, y, 1 - c)] if kind == "sibling" else [(cx, cy, c) for cx, cy in chips]


def _split_start(name, bufs, n_copies, copies, peers, after=()):
    nb = len(bufs)

    def body(*refs):
        barrier = pltpu.get_barrier_semaphore()
        targets = _peers(peers)
        for peer in targets:
            pl.semaphore_signal(barrier, inc=1, device_id=peer, device_id_type=MESH)
        pl.semaphore_wait(barrier, len(targets))
        refs = refs[:nb] + refs[nb + len(after):]
        for cp in copies(refs[:nb], refs[nb], refs[nb + 1]):
            cp.start()
        refs[-1][...] = jnp.zeros_like(refs[-1])

    outs = pl.pallas_call(
        body, name=name,
        out_shape=(pltpu.SemaphoreType.DMA((n_copies,)), pltpu.SemaphoreType.DMA((n_copies,)),
                   *[pltpu.HBM(b.shape, b.dtype) for b in bufs], jax.ShapeDtypeStruct((8, 128), F32)),
        in_specs=[_HBM] * nb + [_ANY] * len(after),
        out_specs=(_SEM, _SEM, *[_HBM] * nb, pl.BlockSpec(memory_space=pltpu.VMEM)),
        input_output_aliases={k: 2 + k for k in range(nb)},
        compiler_params=_params(1, has_side_effects=_EFFECT, collective_id=_PEER_SETS[peers]),
    )(*[_in_hbm(b) for b in bufs], *after)
    return _Started(outs[0], outs[1], list(outs[2:2 + nb]), outs[-1])


def _split_wait(name, started, copies, after):
    nb = len(started.bufs)

    def body(*refs):
        for cp in copies(refs[:nb], refs[nb], refs[nb + 1]):
            cp.wait_send()
            cp.wait_recv()

    return list(pl.pallas_call(
        body, name=name,
        out_shape=tuple(pltpu.HBM(b.shape, b.dtype) for b in started.bufs),
        in_specs=[_HBM] * nb + [_SEM, _SEM, _ANY],
        out_specs=tuple([_HBM] * nb),
        input_output_aliases={k: k for k in range(nb)},
        compiler_params=_params(1, has_side_effects=_EFFECT),
    )(*started.bufs, started.send, started.recv, after))


def _x1_copies(ws):
    def copies(refs, send_sems, recv_sems):
        x, y, c, _ = _mesh_pos()
        out = []
        for k, w in enumerate(ws):
            for s in range(N_SHARD):
                out.append(pltpu.make_async_remote_copy(
                    src_ref=_UNITS[w](refs[k], s, 1 - c), dst_ref=refs[len(ws) + k].at[s],
                    send_sem=send_sems.at[N_SHARD * k + s], recv_sem=recv_sems.at[N_SHARD * k + s],
                    device_id=(x, y, 1 - c), device_id_type=MESH))
        return out
    return copies


def _x2_copies(n):
    def copies(refs, send_sems, recv_sems):
        x, y, c, chips = _mesh_pos()
        out = []
        for j, (cx, cy) in enumerate(chips):
            for k in range(n):
                out.append(pltpu.make_async_remote_copy(
                    src_ref=refs[k].at[2 * cx + cy], dst_ref=refs[n + k].at[j],
                    send_sem=send_sems.at[3 * k + j], recv_sem=recv_sems.at[3 * k + j],
                    device_id=(cx, cy, c), device_id_type=MESH))
        return out
    return copies


def _x3_copies(ws):
    def copies(refs, send_sems, recv_sems):
        x, y, c, _ = _mesh_pos()
        out = []
        for k, w in enumerate(ws):
            rows = _HALF_ROWS[w]
            mine = refs[k].at[pl.ds(_mo(c * rows, rows), rows), :]
            out.append(pltpu.make_async_remote_copy(
                src_ref=mine, dst_ref=mine, send_sem=send_sems.at[k], recv_sem=recv_sems.at[k],
                device_id=(x, y, 1 - c), device_id_type=MESH))
        return out
    return copies


def _x1_lands(ws, dtype=F32):
    return [lax.empty((N_SHARD,) + _UNIT_SHAPES[w], dtype) for w in ws]


def _x2_lands(ws):
    return [lax.empty((3,) + _UNIT_SHAPES[w], BF) for w in ws]


def _grad_add1(w, g, recv, pos):
    ur, uc = _UNIT_SHAPES[w]

    def body(pos_ref, g_ref, r_ref, own_ref, csb_ref):
        v = g_ref[...] + r_ref[0].astype(F32)
        csb_ref[0] = v.astype(BF)

        @pl.when(pl.program_id(0) == pos_ref[1])
        def _():
            own_ref[...] = v

    u3 = lambda s, pos: (s, 0, 0)
    return pl.pallas_call(
        body, name=f"grad_add1_{w}",
        grid_spec=pltpu.PrefetchScalarGridSpec(
            num_scalar_prefetch=1, grid=(N_SHARD,),
            in_specs=[pl.BlockSpec((ur, uc), lambda s, pos: (pos[0], s)), pl.BlockSpec((1, ur, uc), u3)],
            out_specs=[pl.BlockSpec((ur, uc), lambda s, pos: (0, 0)), pl.BlockSpec((1, ur, uc), u3)]),
        out_shape=(jax.ShapeDtypeStruct((ur, uc), F32), jax.ShapeDtypeStruct((N_SHARD, ur, uc), BF)),
        compiler_params=_params(40, dimension_semantics=("arbitrary",)),
    )(pos, g, recv)


def _grad_add1_group(ws, gs, recvs):
    n = len(ws)

    def body(*refs):
        c = lax.axis_index("c")
        for k, w in enumerate(ws):
            g, r, cs, csb = refs[k], refs[n + k], refs[2 * n + k], refs[3 * n + k]
            for s in range(N_SHARD):
                v = _UNITS[w](g, s, c)[...] + r[s]
                cs[s] = v
                csb[s] = v.astype(BF)

    vm = pl.BlockSpec(memory_space=pltpu.VMEM)
    outs = pl.pallas_call(
        body, name="grad_add1_group",
        out_shape=tuple(jax.ShapeDtypeStruct((N_SHARD,) + _UNIT_SHAPES[w], dt) for dt in (F32, BF) for w in ws),
        in_specs=[vm] * (2 * n), out_specs=[vm] * (2 * n),
        compiler_params=_params(32),
    )(*gs, *recvs)
    return list(outs[:n]), list(outs[n:])


def _grad_add2_group(ws, css, recvs):
    n = len(ws)

    def body(*refs):
        x, y, c, _ = _mesh_pos()
        for k, w in enumerate(ws):
            cs, r, o = refs[k], refs[n + k], refs[2 * n + k]
            rows = _HALF_ROWS[w]
            total = ((cs[2 * x + y] + r[0].astype(F32)) + r[1].astype(F32)) + r[2].astype(F32)
            o[pl.ds(_mo(c * rows, rows), rows), :] = total

    vm = pl.BlockSpec(memory_space=pltpu.VMEM)
    return list(pl.pallas_call(
        body, name="grad_add2_group",
        out_shape=tuple(jax.ShapeDtypeStruct(_SHARD_SHAPES[w], F32) for w in ws),
        in_specs=[vm] * (2 * n), out_specs=[vm] * n,
        compiler_params=_params(32),
    )(*css, *recvs))


def _grad_add2(w, own, recv, pos):
    ur, uc = _UNIT_SHAPES[w]
    nt = 4
    tr = ur // nt

    def body(pos_ref, own_ref, r_ref, o_ref):
        o_ref[...] = ((own_ref[...] + r_ref[0].astype(F32)) + r_ref[1].astype(F32)) + r_ref[2].astype(F32)

    return pl.pallas_call(
        body, name=f"grad_add2_{w}",
        grid_spec=pltpu.PrefetchScalarGridSpec(
            num_scalar_prefetch=1, grid=(nt,),
            in_specs=[pl.BlockSpec((tr, uc), lambda t, pos: (t, 0)),
                      pl.BlockSpec((3, tr, uc), lambda t, pos: (0, t, 0))],
            out_specs=pl.BlockSpec((tr, uc), lambda t, pos: (pos[0] * nt + t, 0))),
        out_shape=jax.ShapeDtypeStruct(_SHARD_SHAPES[w], F32),
        compiler_params=_params(32, dimension_semantics=("arbitrary",)),
    )(pos, own, recv)


def _adamw_math(w, g, m, v):
    m = ADAM_B1 * m + (1.0 - ADAM_B1) * g
    v = ADAM_B2 * v + (1.0 - ADAM_B2) * (g * g)
    m_hat = m / ADAM_C1
    v_hat = v / ADAM_C2
    delta = -ADAM_LR * (m_hat / (jnp.sqrt(v_hat) + ADAM_EPS) + ADAM_WD * w)
    return delta, m, v


def _adamw_group(ws_, gs, ms, vs, after=()):
    n = len(ws_)

    def body(*refs):
        for k in range(n):
            w, g, m, v = (refs[j * n + k] for j in range(4))
            d, nm, nv, gc = (refs[(4 + j) * n + k] for j in range(4))
            gv = g[...]
            d[...], nm[...], nv[...] = _adamw_math(w[...], gv, m[...], v[...])
            gc[...] = gv

    vm = pl.BlockSpec(memory_space=pltpu.VMEM)
    outs = pl.pallas_call(
        _after(body, 4 * n, after), name="adamw_group",
        out_shape=tuple(jax.ShapeDtypeStruct(a.shape, F32) for _ in range(4) for a in ws_),
        in_specs=[vm] * (4 * n) + [_ANY] * len(after), out_specs=[vm] * (4 * n),
        compiler_params=_params(32),
    )(*ws_, *gs, *ms, *vs, *after)
    return [tuple(outs[j * n + k] for j in range(4)) for k in range(n)]


def _adamw(name, w, g, m, v, tr=256, after=()):
    rows, cols = w.shape

    def body(w_ref, g_ref, m_ref, v_ref, d_ref, nm_ref, nv_ref, gc_ref):
        gv = g_ref[...]
        d_ref[...], nm_ref[...], nv_ref[...] = _adamw_math(w_ref[...], gv, m_ref[...], v_ref[...])
        gc_ref[...] = gv

    spec = pl.BlockSpec((tr, cols), lambda i: (i, 0))
    return pl.pallas_call(
        _after(body, 4, after), name=name, grid=(rows // tr,),
        out_shape=tuple(jax.ShapeDtypeStruct((rows, cols), F32) for _ in range(4)),
        in_specs=[spec] * 4 + [_ANY] * len(after), out_specs=[spec] * 4,
        compiler_params=_params(32, dimension_semantics=("arbitrary",)),
    )(w, g, m, v, *after)


_REL_PAD = 384
_VEC_FIELDS = (("norm_g", 0, D_MODEL), ("b_gate", 1024, 2 * D_MODEL), ("sgu_ln_g", 3072, D_B),
               ("sgu_ln_b", 3584, D_B), ("b_s", 4096, N_GROUPS * 128), ("final_g", 4608, D_MODEL))
_LOSS_OFF = 5632
_REL_OFF = 5760
_NV = _REL_OFF + N_HEADS * _REL_PAD
_N_FIELDS = len(_VEC_FIELDS) + 2


_B_S_FIELD = [f[0] for f in _VEC_FIELDS].index("b_s")


def _assemble_row(dst, fields, transposed_b_s):
    for f, (_, off, n) in enumerate(_VEC_FIELDS):
        if transposed_b_s and f == _B_S_FIELD:
            t = fields[f][...].T
            for g in range(N_GROUPS):
                dst[:, off + 128 * g:off + 128 * (g + 1)] = t[g:g + 1, :]
        else:
            dst[:, off:off + n] = fields[f][...]
    for r in range(N_HEADS):
        dst[:, _REL_OFF + _REL_PAD * r:_REL_OFF + _REL_PAD * (r + 1)] = fields[len(_VEC_FIELDS)][r:r + 1, :]


def _small_reduce(grads, loss_row, after=()):
    n_in = _N_FIELDS + 1

    def body(*refs):
        g_refs, loss_ref = refs[:_N_FIELDS], refs[_N_FIELDS]
        out_v, out_w = refs[n_in:n_in + 2]
        mine_v, mine_w, gath_v, gath_w, send_sems, recv_sems = refs[n_in + 2:]
        x, y, c, chips = _mesh_pos()
        me, sibling = (x, y, c), (x, y, 1 - c)

        _assemble_row(mine_v, g_refs, True)
        mine_v[:, _LOSS_OFF:_LOSS_OFF + 128] = loss_ref[...]
        mine_w[...] = g_refs[-1][...].astype(BF)
        my_k = 4 * x + 2 * y + c
        gath_v[my_k] = mine_v[...]
        gath_w[my_k] = mine_w[...]

        def copy(k, gath, block, to, src=None):
            dst = gath.at[4 * block[0] + 2 * block[1] + block[2]]
            return pltpu.make_async_remote_copy(
                src_ref=dst if src is None else src, dst_ref=dst,
                send_sem=send_sems.at[k], recv_sem=recv_sems.at[k], device_id=to, device_id_type=MESH)

        bufs = ((gath_v, mine_v), (gath_w, mine_w))
        first, passed = [], []
        for b, (gath, mine) in enumerate(bufs):
            first.append(copy(7 * b, gath, me, sibling, src=mine))
            first += [copy(7 * b + 1 + j, gath, me, (*chip, c), src=mine) for j, chip in enumerate(chips)]
        for cp in first:
            cp.start()
        for b, (gath, _) in enumerate(bufs):
            for j, chip in enumerate(chips):
                copy(7 * b + 1 + j, gath, (*chip, c), me).wait_recv()
                cp = copy(7 * b + 4 + j, gath, (*chip, c), sibling)
                cp.start()
                passed.append(cp)
        for b, (gath, _) in enumerate(bufs):
            copy(7 * b, gath, sibling, me).wait_recv()
            for j, chip in enumerate(chips):
                copy(7 * b + 4 + j, gath, (*chip, 1 - c), me).wait_recv()
        for cp in first + passed:
            cp.wait_send()

        tot_v, tot_w = gath_v[0], gath_w[0].astype(F32)
        for k in range(1, 8):
            tot_v = tot_v + gath_v[k]
            tot_w = tot_w + gath_w[k].astype(F32)
        out_v[...] = tot_v
        out_w[...] = tot_w

    vm = pl.BlockSpec(memory_space=pltpu.VMEM)
    return pl.pallas_call(
        _after(body, n_in, after), name="small_reduce",
        out_shape=(jax.ShapeDtypeStruct((1, _NV), F32), jax.ShapeDtypeStruct((N_GROUPS * 128, 128), F32)),
        in_specs=[vm] * n_in + [_ANY] * len(after), out_specs=[vm] * 2,
        scratch_shapes=[pltpu.VMEM((1, _NV), F32), pltpu.VMEM((N_GROUPS * 128, 128), BF),
                        pltpu.VMEM((8, 1, _NV), F32), pltpu.VMEM((8, N_GROUPS * 128, 128), BF),
                        pltpu.SemaphoreType.DMA((14,)), pltpu.SemaphoreType.DMA((14,))],
        compiler_params=_params(32),
    )(*grads, loss_row, *after)


def _small_adamw(tot_v, tot_w, params):
    n_in = 2 + 3 * _N_FIELDS

    def body(*refs):
        tv_ref, tw_ref = refs[:2]
        p_refs = [refs[2 + k * _N_FIELDS:2 + (k + 1) * _N_FIELDS] for k in range(3)]
        outs = refs[n_in:n_in + 4 * _N_FIELDS + 1]
        wmv = refs[-1]
        for k in range(3):
            _assemble_row(wmv.at[k], p_refs[k], False)
            wmv[k, :, _LOSS_OFF:_LOSS_OFF + 128] = jnp.zeros((1, 128), F32)
        tot_v, tot_w = tv_ref[...], tw_ref[...]
        res_v = (tot_v,) + _adamw_math(wmv[0], tot_v, wmv[1], wmv[2])
        res_w = (tot_w,) + _adamw_math(p_refs[0][-1][...], tot_w, p_refs[1][-1][...], p_refs[2][-1][...])
        for kind in range(4):
            o = outs[kind * _N_FIELDS:(kind + 1) * _N_FIELDS]
            for f, (_, off, n) in enumerate(_VEC_FIELDS):
                o[f][...] = res_v[kind][:, off:off + n]
            for r in range(N_HEADS):
                o[len(_VEC_FIELDS)][r:r + 1, :] = res_v[kind][:, _REL_OFF + _REL_PAD * r:_REL_OFF + _REL_PAD * (r + 1)]
            o[-1][...] = res_w[kind]
        outs[-1][...] = tot_v[:, _LOSS_OFF:_LOSS_OFF + 128]

    field_shapes = [(1, n) for _, _, n in _VEC_FIELDS] + [(N_HEADS, _REL_PAD), (N_GROUPS * 128, 128)]
    vm = pl.BlockSpec(memory_space=pltpu.VMEM)
    operands = [tot_v, tot_w] + [a for p in params for a in p]
    assert len(operands) == n_in
    outs = pl.pallas_call(
        body, name="small_adamw",
        out_shape=tuple(jax.ShapeDtypeStruct(s, F32) for _ in range(4) for s in field_shapes)
        + (jax.ShapeDtypeStruct((1, 128), F32),),
        in_specs=[vm] * n_in, out_specs=[vm] * (4 * _N_FIELDS + 1),
        scratch_shapes=[pltpu.VMEM((3, 1, _NV), F32)],
        compiler_params=_params(32),
    )(*operands)
    return [outs[k * _N_FIELDS:(k + 1) * _N_FIELDS] for k in range(4)], outs[-1]


def _small_fields(norm_g, b_gate, ln_g, ln_b, b_s, final_g, rel_bias, w_s):
    rel = jnp.pad(rel_bias.reshape(N_HEADS, N_REL), ((0, 0), (0, _REL_PAD - N_REL)))
    return (norm_g, b_gate, ln_g, ln_b, b_s.reshape(1, N_GROUPS * 128), final_g.reshape(1, D_MODEL),
            rel, w_s.reshape(N_GROUPS * 128, 128))


def _small_outputs(fields):
    n_g, b_g, l_g, l_b, b_s, f_g, rel, w_s = fields
    return (n_g, b_g, rel[:, :N_REL].reshape(1, N_HEADS, N_REL), l_g, l_b,
            w_s.reshape(1, N_GROUPS, 128, 128), b_s.reshape(1, N_GROUPS, 128), f_g.reshape(D_MODEL))


def _bias_row(rel_bias):
    hi = rel_bias[:, N_REL - 1:N_REL]
    lo = rel_bias[:, 0:1]
    return jnp.concatenate([jnp.broadcast_to(hi, (N_HEADS, 384)), rel_bias[:, ::-1],
                            jnp.broadcast_to(lo, (N_HEADS, 191)), jnp.broadcast_to(hi, (N_HEADS, 192))], axis=1)


def kernel(x, norm_g, w_in, b_gate, rel_bias, sgu_ln_g, sgu_ln_b, w_s, b_s, w_pa, w_pb, w_out, final_g, loss_target, m_norm_g, m_w_in, m_b_gate, m_rel_bias, m_sgu_ln_g, m_sgu_ln_b, m_w_s, m_b_s, m_w_pa, m_w_pb, m_w_out, m_final_g, v_norm_g, v_w_in, v_b_gate, v_rel_bias, v_sgu_ln_g, v_sgu_ln_b, v_w_s, v_b_s, v_w_pa, v_w_pb, v_w_out, v_final_g):
    S = x.shape[1]
    xs = x.reshape(S, D_MODEL)
    tgt = loss_target.reshape(S, D_MODEL)
    big_w = (w_in[0], w_pa[0], w_pb[0], w_out[0])
    big_m = (m_w_in[0], m_w_pa[0], m_w_pb[0], m_w_out[0])
    big_v = (v_w_in[0], v_w_pa[0], v_w_pb[0], v_w_out[0])
    rel = rel_bias[0]
    ws = w_s[0]
    bst = b_s[0].T
    fg = final_g.reshape(1, D_MODEL)
    pos = jnp.stack([lax.axis_index("c"), 2 * lax.axis_index("x") + lax.axis_index("y")]).astype(jnp.int32)

    (w_in_bf,), staged, band_bias = _ag_weights((0,), big_w[:1], (1, 2, 3), big_w[1:], _bias_row(rel))
    ag_s = _split_start("ag_small_start", staged, 9, _gather_copies((1, 2, 3)), "chips", after=(w_in_bf,))

    ht, q3, k3, v3, zrest = _inproj_fwd(xs, norm_g, w_in_bf, after=(ag_s.token,))
    att, lse = _attn_fwd(q3, k3, v3, band_bias)
    w_pa_bf, w_pb_bf, w_out_bf = _split_wait("ag_small_wait", ag_s, _gather_copies((1, 2, 3)), att)
    (d_out, d_att, dzt, dzs, gw_out, gw_pa, gw_pb, g_bgate, g_final, loss_row,
     g_ws, g_bs_t, g_lng, g_lnb) = _tail_sgu(
        att, zrest, xs, tgt, w_pa_bf, w_pb_bf, w_out_bf, b_gate, fg, sgu_ln_g, sgu_ln_b, ws, bst)
    ws_s, ws_i = (1, 2, 3), (0,)

    x1s = _split_start("gx1s_start", [gw_pa, gw_pb, gw_out] + _x1_lands(ws_s), 12, _x1_copies(ws_s), "sibling")
    dq, dk, dv, d_gp = _attn_bwd(q3, k3, v3, d_att, lse, band_bias, after=(x1s.token,))
    got = _split_wait("gx1s_wait", x1s, _x1_copies(ws_s), dq)
    cs_s, csb_s = _grad_add1_group(ws_s, got[:3], got[3:])

    x2s = _split_start("gx2s_start", csb_s + _x2_lands(ws_s), 9, _x2_copies(3), "chips")
    gw_in, gw_in_bf = _gw_in(ht, dq, dk, dv, dzt, dzs, after=(x2s.token,))
    x1i = _split_start("gx1i_start", [gw_in_bf] + _x1_lands(ws_i, BF), 4, _x1_copies(ws_i), "sibling")
    got = _split_wait("gx2s_wait", x2s, _x2_copies(3), x1i.token)
    halves_s = _grad_add2_group(ws_s, cs_s, got[3:])
    x3s = _split_start("gx3s_start", halves_s, 3, _x3_copies(ws_s), "sibling")
    got = _split_wait("gx1i_wait", x1i, _x1_copies(ws_i), x3s.token)
    sum_i = _grad_add1(0, gw_in, got[1], pos)

    x2i = _split_start("gx2i_start", [sum_i[1]] + _x2_lands(ws_i), 3, _x2_copies(1), "chips")
    grad_x, g_norm = _dh_gradx(dq, dk, dv, dzt, dzs, w_in_bf, xs, norm_g, d_out, after=(x2i.token,))
    g_shards_s = _split_wait("gx3s_wait", x3s, _x3_copies(ws_s), grad_x)
    got = _split_wait("gx2i_wait", x2i, _x2_copies(1), grad_x)
    half_i = _grad_add2(0, sum_i[0], got[1], pos)
    x3i = _split_start("gx3i_start", [half_i], 1, _x3_copies(ws_i), "sibling")
    big = [None] * 4
    big[1:] = _adamw_group(big_w[1:], g_shards_s, big_m[1:], big_v[1:], after=(x3i.token,))

    g_rel = jnp.pad(d_gp[:, 384:384 + N_REL][:, ::-1], ((0, 0), (0, _REL_PAD - N_REL)))
    small_grads = (g_norm, g_bgate, g_lng, g_lnb, g_bs_t, g_final, g_rel, g_ws.reshape(N_GROUPS * 128, 128))
    small_params = (_small_fields(norm_g, b_gate, sgu_ln_g, sgu_ln_b, b_s, final_g, rel_bias, w_s),
                    _small_fields(m_norm_g, m_b_gate, m_sgu_ln_g, m_sgu_ln_b, m_b_s, m_final_g, m_rel_bias, m_w_s),
                    _small_fields(v_norm_g, v_b_gate, v_sgu_ln_g, v_sgu_ln_b, v_b_s, v_final_g, v_rel_bias, v_w_s))
    tot_v, tot_w = _small_reduce(small_grads, loss_row, after=(x3i.token,))
    (gsum, sdelta, sm, sv), loss_out = _small_adamw(tot_v, tot_w, small_params)

    g_shard_i, = _split_wait("gx3i_wait", x3i, _x3_copies(ws_i), loss_out)
    big[0] = _adamw("adamw_w_in", big_w[0], g_shard_i, big_m[0], big_v[0])
    sg_out, sd_out, sm_out, sv_out = (_small_outputs(f) for f in (gsum, sdelta, sm, sv))
    loss = loss_out[0, 0]

    def assemble(small, bigs):
        n_g, b_g, r_b, l_g, l_b, w_s_, b_s_, f_g = small
        b_in, b_pa, b_pb, b_out = (b[None] for b in bigs)
        return (n_g, b_in, b_g, r_b, l_g, l_b, w_s_, b_s_, b_pa, b_pb, b_out, f_g)

    grads_out = assemble(sg_out, [b[3] for b in big])
    delta_out = assemble(sd_out, [b[0] for b in big])
    m_out = assemble(sm_out, [b[1] for b in big])
    v_out = assemble(sv_out, [b[2] for b in big])
    return (loss, grad_x.reshape(1, S, D_MODEL), *grads_out, *delta_out, *m_out, *v_out)
```

```python
import functools
import math

import jax
import jax.numpy as jnp
from jax import lax
from jax.experimental import pallas as pl
from jax.experimental.pallas import tpu as pltpu

F32 = jnp.float32
BF = jnp.bfloat16
MESH = pl.DeviceIdType.MESH

D_MODEL = 1024
D_A = 512
D_B = 512
D_IN = 5632
N_HEADS = 8
HEAD_DIM = 64
CHUNK = 64
N_PREV = 8
SGU_CHUNK = 128
N_GROUPS = 4
N_REL = 257
EPS = 1e-6
NEG_INF = -1e30
SCALE = HEAD_DIM ** -0.5

QB = 2 * CHUNK
KB = (N_PREV + 2) * CHUNK
PADK = N_PREV * CHUNK
ROLL_W = 1024
KEEP = KB // QB - 1
Q_PER_STEP = 2

ADAM_LR = 0.001
ADAM_B1 = 0.9
ADAM_B2 = 0.999
ADAM_EPS = 1e-08
ADAM_WD = 0.01
ADAM_STEP = 10
ADAM_C1 = 1.0 - ADAM_B1 ** ADAM_STEP
ADAM_C2 = 1.0 - ADAM_B2 ** ADAM_STEP

N_SHARD = 4
SHARD_IN = D_IN // N_SHARD
MIB = 1024 * 1024


V7X_VMEM_MIB = 64
VMEM_RESERVE_MIB = V7X_VMEM_MIB - 4


def _params(vmem_mib, **kw):
    assert vmem_mib <= VMEM_RESERVE_MIB
    return pltpu.CompilerParams(vmem_limit_bytes=VMEM_RESERVE_MIB * MIB, **kw)


def _sigmoid(x):
    return 1.0 / (1.0 + jnp.exp(-x))


def _silu_and_grad(x):
    s = _sigmoid(x)
    return x * s, s * (1.0 + x * (1.0 - s))


_GELU_C = math.sqrt(2.0 / math.pi)
_GELU_A = 0.044715


def _gelu_and_grad(x):
    x2 = x * x
    t = jnp.tanh(_GELU_C * (x + _GELU_A * (x2 * x)))
    cdf = 0.5 * (1.0 + t)
    grad = cdf + 0.5 * x * (1.0 - t * t) * (_GELU_C * (1.0 + 3.0 * _GELU_A * x2))
    return x * cdf, grad


def _dot(a, b):
    return jnp.dot(a, b, preferred_element_type=F32)


def _dot_nt(a, b):
    return lax.dot_general(a, b, (((1,), (1,)), ((), ())), preferred_element_type=F32)


def _dot_tn(a, b):
    return lax.dot_general(a, b, (((0,), (0,)), ((), ())), preferred_element_type=F32)


def _mo(v, m):
    return v if isinstance(v, int) else pl.multiple_of(v, m)


def _unit_in(ref, s, p):
    return ref.at[pl.ds(_mo(p * 512, 512), 512), pl.ds(_mo(s * SHARD_IN, 128), SHARD_IN)]


def _unit_p(ref, s, p):
    return ref.at[pl.ds(_mo(p * 256, 256), 256), pl.ds(_mo(s * 256, 128), 256)]


def _unit_out(ref, s, p):
    return ref.at[pl.ds(_mo(s * 256 + p * 128, 128), 128), :]


_UNITS = (_unit_in, _unit_p, _unit_p, _unit_out)
_HALF_ROWS = (512, 256, 256, 128)
_UNIT_SHAPES = ((512, SHARD_IN), (256, 256), (256, 256), (128, D_MODEL))
_FULL_SHAPES = ((D_MODEL, D_IN), (D_A, D_MODEL), (D_B, D_MODEL), (D_MODEL, D_MODEL))
_SHARD_SHAPES = ((D_MODEL, SHARD_IN), (D_A, 256), (D_B, 256), (256, D_MODEL))


def _mesh_pos():
    x, y, c = lax.axis_index("x"), lax.axis_index("y"), lax.axis_index("c")
    chips = [(1 - x, y), (x, 1 - y), (1 - x, 1 - y)]
    return x, y, c, chips


def _ag_weights(ws, shards, later_ws, later_shards, gp):
    n, m = len(ws), len(later_ws)

    def body(*refs):
        ins, later_ins, gp_ref = refs[:n], refs[n:n + m], refs[n + m]
        o = n + m + 1
        outs, later_outs, bias_ref = refs[o:o + n], refs[o + n:o + n + m], refs[o + n + m]
        o += n + m + 1
        stage, later_stage = refs[o:o + n], refs[o + n:o + n + m]
        send_sems, recv_sems, local_sems, later_sems = refs[o + n + m:]
        x, y, c, chips = _mesh_pos()
        s_me = 2 * x + y
        sibling = (x, y, 1 - c)
        def rows_of(k, p):
            rows = _HALF_ROWS[ws[k]]
            return pl.ds(_mo(p * rows, rows), rows)

        def half(k, p):
            return stage[k].at[rows_of(k, p), :]

        def unit(k, s, p):
            return _UNITS[ws[k]](outs[k], s, p)

        def rcopy(k, i, src, dst, to):
            return pltpu.make_async_remote_copy(src_ref=src, dst_ref=dst, send_sem=send_sems.at[k, i],
                                                recv_sem=recv_sems.at[k, i], device_id=to, device_id_type=MESH)

        peers_entered = _signal_peers("both")
        for k in range(n):
            stage[k][rows_of(k, c), :] = ins[k][rows_of(k, c), :].astype(BF)
        peers_entered()
        sends = []
        for j, (cx, cy) in enumerate(chips):
            for k in range(n):
                cp = rcopy(k, j, half(k, c), unit(k, s_me, c), (cx, cy, c))
                cp.start()
                sends.append(cp)
        for k in range(n):
            stage[k][rows_of(k, 1 - c), :] = ins[k][rows_of(k, 1 - c), :].astype(BF)
        local = []
        for k in range(n):
            for p in range(2):
                cp = pltpu.make_async_copy(half(k, p), unit(k, s_me, p), local_sems.at[k, p])
                cp.start()
                local.append(cp)
        for k, w in enumerate(later_ws):
            later_stage[k][...] = later_ins[k][...].astype(BF)
            cp = pltpu.make_async_copy(later_stage[k], _shard_of(later_outs[k], w, s_me), later_sems.at[k])
            cp.start()
            local.append(cp)
        keep = _struct_mask()
        for h in range(N_HEADS):
            bias_ref[h] = jnp.where(keep, _skew_table(gp_ref[h:h + 1, :])[:, :KB], NEG_INF)
        for j, (cx, cy) in enumerate(chips):
            for k in range(n):
                landed = unit(k, 2 * cx + cy, c)
                rcopy(k, j, landed, landed, (cx, cy, c)).wait_recv()
                cp = rcopy(k, 3 + j, landed, landed, sibling)
                cp.start()
                sends.append(cp)
        for j, (cx, cy) in enumerate(chips):
            for k in range(n):
                other = unit(k, 2 * cx + cy, 1 - c)
                rcopy(k, 3 + j, other, other, sibling).wait_recv()
        for cp in sends:
            cp.wait_send()
        for cp in local:
            cp.wait()

    vm = pl.BlockSpec(memory_space=pltpu.VMEM)
    outs = pl.pallas_call(
        body, name="ag_weights",
        out_shape=tuple(jax.ShapeDtypeStruct(_FULL_SHAPES[w], BF) for w in tuple(ws) + tuple(later_ws))
        + (jax.ShapeDtypeStruct((N_HEADS, QB, KB), F32),),
        in_specs=[vm] * (n + m + 1), out_specs=[_ANY] * (n + m) + [vm],
        scratch_shapes=[pltpu.VMEM(_SHARD_SHAPES[w], BF) for w in tuple(ws) + tuple(later_ws)]
        + [pltpu.SemaphoreType.DMA((n, 6)), pltpu.SemaphoreType.DMA((n, 6)), pltpu.SemaphoreType.DMA((n, 2)),
           pltpu.SemaphoreType.DMA((m,))],
        compiler_params=_params(48, collective_id=_PEER_SETS["both"]),
    )(*shards, *later_shards, gp)
    return list(outs[:n]), list(outs[n:n + m]), outs[-1]


def _shard_of(ref, w, s):
    if w == 0:
        return ref.at[:, pl.ds(_mo(s * SHARD_IN, 128), SHARD_IN)]
    if w == 3:
        return ref.at[pl.ds(_mo(s * 256, 256), 256), :]
    return ref.at[:, pl.ds(_mo(s * 256, 128), 256)]


def _gather_copies(ws):
    def copies(refs, send_sems, recv_sems):
        x, y, c, chips = _mesh_pos()
        out = []
        for j, (cx, cy) in enumerate(chips):
            for k, w in enumerate(ws):
                mine = _shard_of(refs[k], w, 2 * x + y)
                out.append(pltpu.make_async_remote_copy(
                    src_ref=mine, dst_ref=mine, send_sem=send_sems.at[3 * k + j], recv_sem=recv_sems.at[3 * k + j],
                    device_id=(cx, cy, c), device_id_type=MESH))
        return out
    return copies


def _inproj_fwd(x, norm_g, w_in_bf, tm=512, after=()):
    S = x.shape[0]

    def body(x_ref, g_ref, w_ref, ht_ref, q_ref, k_ref, v_ref, zr_ref):
        xv = x_ref[...]
        r = lax.rsqrt(jnp.mean(xv * xv, axis=-1, keepdims=True) + EPS)
        hf = (xv * r) * g_ref[...]
        ht_ref[...] = hf.T.astype(BF)
        h = hf.astype(BF)
        heads = (q_ref, k_ref, v_ref)
        for j in range(D_IN // 512):
            z = _dot(h, w_ref[:, j * 512:(j + 1) * 512])
            if j < 3:
                zb = z.astype(BF)
                for hd in range(N_HEADS):
                    heads[j][hd] = zb[:, hd * HEAD_DIM:(hd + 1) * HEAD_DIM]
            else:
                zr_ref[:, (j - 3) * 512:(j - 2) * 512] = z

    head_major = jax.ShapeDtypeStruct((N_HEADS, S, HEAD_DIM), BF)
    head_spec = pl.BlockSpec((N_HEADS, tm, HEAD_DIM), lambda i: (0, i, 0))
    return pl.pallas_call(
        _after(body, 3, after), name="inproj_fwd", grid=(S // tm,),
        out_shape=(jax.ShapeDtypeStruct((D_MODEL, S), BF), head_major, head_major, head_major,
                   jax.ShapeDtypeStruct((S, D_IN - 3 * D_A), F32)),
        in_specs=[pl.BlockSpec((tm, D_MODEL), lambda i: (i, 0)),
                  pl.BlockSpec((1, D_MODEL), lambda i: (0, 0)),
                  pl.BlockSpec((D_MODEL, D_IN), lambda i: (0, 0), pipeline_mode=pl.Buffered(1))]
        + [_ANY] * len(after),
        out_specs=[pl.BlockSpec((D_MODEL, tm), lambda i: (0, i)),
                   head_spec, head_spec, head_spec,
                   pl.BlockSpec((tm, D_IN - 3 * D_A), lambda i: (i, 0))],
        compiler_params=_params(52, dimension_semantics=("arbitrary",)),
    )(x, norm_g, w_in_bf, *after)


def _skew_table(gp_row):
    row = lax.broadcasted_iota(jnp.int32, (QB, ROLL_W), 0)
    t = jnp.broadcast_to(gp_row, (QB, ROLL_W))
    for b in range(7):
        t = jnp.where(((row >> b) & 1) == 1, pltpu.roll(t, 1 << b, axis=1), t)
    return t


def _unskew_sum(d):
    row = lax.broadcasted_iota(jnp.int32, (QB, ROLL_W), 0)
    for b in range(7):
        d = jnp.where(((row >> b) & 1) == 1, pltpu.roll(d, ROLL_W - (1 << b), axis=1), d)
    return jnp.sum(d, axis=0, keepdims=True)


def _struct_mask():
    a = lax.broadcasted_iota(jnp.int32, (QB, KB), 0) // CHUNK
    b = lax.broadcasted_iota(jnp.int32, (QB, KB), 1) // CHUNK
    return (b >= a) & (b <= a + N_PREV)


def _load_kv(k_hbm, v_hbm, k_scr, v_scr, sems, S, meanwhile=lambda: None):
    zeros = jnp.zeros((N_HEADS, PADK, HEAD_DIM), BF)
    k_scr[:, 0:PADK, :] = zeros
    v_scr[:, 0:PADK, :] = zeros
    ck = pltpu.make_async_copy(k_hbm, k_scr.at[:, pl.ds(PADK, S), :], sems.at[0])
    cv = pltpu.make_async_copy(v_hbm, v_scr.at[:, pl.ds(PADK, S), :], sems.at[1])
    ck.start()
    cv.start()
    meanwhile()
    ck.wait()
    cv.wait()


_BATCH_NT = (((2,), (2,)), ((0,), (0,)))
_BATCH_NN = (((2,), (1,)), ((0,), (0,)))
_BATCH_TN = (((1,), (1,)), ((0,), (0,)))


def _bdot(a, b, dims):
    return lax.dot_general(a, b, dims, preferred_element_type=F32)


def _scaled(q):
    return q * jnp.asarray(SCALE, BF)


def _scores(qs, kb, bias, i, front):
    s = _bdot(qs, kb, _BATCH_NT) + bias
    if front:
        col = lax.broadcasted_iota(jnp.int32, (1, 1, KB), 2)
        s = jnp.where(col >= PADK - i * QB, s, NEG_INF)
    return s


def _attn_fwd(q3, k3, v3, bias):
    S = q3.shape[1]

    def body(q_ref, k_hbm, v_hbm, bias_ref, o_ref, lse_ref, k_scr, v_scr, sems):
        @pl.when(pl.program_id(0) == 0)
        def _():
            _load_kv(k_hbm, v_hbm, k_scr, v_scr, sems, S)

        def step(i, rows, front):
            start = pl.multiple_of(i * QB, QB)
            kb = k_scr[:, pl.ds(start, KB), :]
            vb = v_scr[:, pl.ds(start, KB), :]
            s = _scores(_scaled(q_ref[:, rows, :]), kb, bias_ref[...], i, front)
            m = jnp.max(s, axis=-1, keepdims=True)
            e = jnp.exp(s - m)
            l = jnp.sum(e, axis=-1, keepdims=True)
            p = e * (1.0 / l)
            o = _bdot(p.astype(BF), vb, _BATCH_NN)
            lse_ref[:, rows, :] = jnp.broadcast_to(m + jnp.log(l), (N_HEADS, QB, 128))
            for h in range(N_HEADS):
                o_ref[rows, h * HEAD_DIM:(h + 1) * HEAD_DIM] = o[h]

        def block(j, carry):
            i = pl.program_id(0) * Q_PER_STEP + j
            rows = pl.ds(pl.multiple_of(j * QB, QB), QB)
            pl.when(i < KEEP)(functools.partial(step, i, rows, True))
            pl.when(i >= KEEP)(functools.partial(step, i, rows, False))
            return carry

        lax.fori_loop(0, Q_PER_STEP, block, 0)

    rows_per_step = Q_PER_STEP * QB
    kv_scr = pltpu.VMEM((N_HEADS, S + PADK, HEAD_DIM), BF)
    return pl.pallas_call(
        body, name="attn_fwd", grid=(S // rows_per_step,),
        out_shape=(jax.ShapeDtypeStruct((S, D_A), F32), jax.ShapeDtypeStruct((N_HEADS, S, 128), F32)),
        in_specs=[pl.BlockSpec((N_HEADS, rows_per_step, HEAD_DIM), lambda g: (0, g, 0)),
                  pl.BlockSpec(memory_space=pl.ANY), pl.BlockSpec(memory_space=pl.ANY),
                  pl.BlockSpec((N_HEADS, QB, KB), lambda g: (0, 0, 0))],
        out_specs=[pl.BlockSpec((rows_per_step, D_A), lambda g: (g, 0)),
                   pl.BlockSpec((N_HEADS, rows_per_step, 128), lambda g: (0, g, 0))],
        scratch_shapes=[kv_scr, kv_scr, pltpu.SemaphoreType.DMA((2,))],
        compiler_params=_params(48, dimension_semantics=("arbitrary",)),
    )(q3, k3, v3, bias)


def _attn_bwd(q3, k3, v3, d_att3, lse, bias, after=()):
    S = q3.shape[1]
    nq = S // QB

    def body(q_ref, do_ref, k_hbm, v_hbm, lse_ref, bias_ref, dq_ref, dk_ref, dv_ref, dgp_ref,
             k_scr, v_scr, dk_acc, dv_acc, dbias_acc, pad_scr, sems):
        @pl.when(pl.program_id(0) == 0)
        def _():
            def clear():
                dk_acc[...] = jnp.zeros_like(dk_acc)
                dv_acc[...] = jnp.zeros_like(dv_acc)
                dbias_acc[...] = jnp.zeros_like(dbias_acc)
            _load_kv(k_hbm, v_hbm, k_scr, v_scr, sems, S, clear)

        def step(i, rows, front):
            start = pl.multiple_of(i * QB, QB)
            kb = k_scr[:, pl.ds(start, KB), :]
            vb = v_scr[:, pl.ds(start, KB), :]
            qs = _scaled(q_ref[:, rows, :])
            do = do_ref[:, rows, :]
            p = jnp.exp(_scores(qs, kb, bias_ref[...], i, front) - jnp.tile(lse_ref[:, rows, :], (1, 1, KB // 128)))
            dp = _bdot(do, vb, _BATCH_NT)
            ds = p * (dp - jnp.sum(dp * p, axis=-1, keepdims=True))
            dbias_acc[...] += ds
            dsb = ds.astype(BF)
            dq = _bdot(dsb, kb, _BATCH_NN) * SCALE
            for h in range(N_HEADS):
                dq_ref[rows, h * HEAD_DIM:(h + 1) * HEAD_DIM] = dq[h].astype(BF)
            dk_acc[...] += _bdot(dsb, qs, _BATCH_TN)
            dv_acc[...] += _bdot(p.astype(BF), do, _BATCH_TN)

        def block(j, carry):
            i = pl.program_id(0) * Q_PER_STEP + j
            rows = pl.ds(pl.multiple_of(j * QB, QB), QB)
            pl.when(i < KEEP)(functools.partial(step, i, rows, True))
            pl.when((i >= KEEP) & (i < nq))(functools.partial(step, i, rows, False))
            for h in range(N_HEADS):
                hs = slice(h * HEAD_DIM, (h + 1) * HEAD_DIM)
                dk_ref[rows, hs] = dk_acc[h, 0:QB, :].astype(BF)
                dv_ref[rows, hs] = dv_acc[h, 0:QB, :].astype(BF)
            dk_acc[:, 0:KB - QB, :] = dk_acc[:, QB:KB, :]
            dv_acc[:, 0:KB - QB, :] = dv_acc[:, QB:KB, :]
            dk_acc[:, KB - QB:KB, :] = jnp.zeros((N_HEADS, QB, HEAD_DIM), F32)
            dv_acc[:, KB - QB:KB, :] = jnp.zeros((N_HEADS, QB, HEAD_DIM), F32)
            return carry

        lax.fori_loop(0, Q_PER_STEP, block, 0)

        @pl.when(pl.program_id(0) == n_steps - 1)
        def _():
            lane = lax.broadcasted_iota(jnp.int32, (1, ROLL_W), 1)
            hi = (lane < 384) | (lane >= 832)
            lo = (lane > 640) & (lane < 832)
            pad_scr[...] = jnp.zeros_like(pad_scr)
            for h in range(N_HEADS):
                pad_scr[:, 0:KB] = dbias_acc[h]
                g = _unskew_sum(pad_scr[...])
                s_hi = jnp.sum(jnp.where(hi, g, 0.0), axis=-1, keepdims=True)
                s_lo = jnp.sum(jnp.where(lo, g, 0.0), axis=-1, keepdims=True)
                g = jnp.where(lane == 384, g + s_hi, g)
                g = jnp.where(lane == 640, g + s_lo, g)
                dgp_ref[h:h + 1, :] = g

    assert nq % Q_PER_STEP == 0 and KEEP % Q_PER_STEP == 0
    rows_per_step = Q_PER_STEP * QB
    n_steps = (nq + KEEP) // Q_PER_STEP
    last = nq // Q_PER_STEP - 1
    lag = KEEP // Q_PER_STEP
    kv_scr = pltpu.VMEM((N_HEADS, S + PADK, HEAD_DIM), BF)
    return pl.pallas_call(
        _after(body, 6, after), name="attn_bwd", grid=(n_steps,),
        out_shape=(jax.ShapeDtypeStruct((S, D_A), BF), jax.ShapeDtypeStruct((S, D_A), BF),
                   jax.ShapeDtypeStruct((S, D_A), BF), jax.ShapeDtypeStruct((N_HEADS, ROLL_W), F32)),
        in_specs=[pl.BlockSpec((N_HEADS, rows_per_step, HEAD_DIM), lambda g: (0, jnp.minimum(g, last), 0)),
                  pl.BlockSpec((N_HEADS, rows_per_step, HEAD_DIM), lambda g: (0, jnp.minimum(g, last), 0)),
                  pl.BlockSpec(memory_space=pl.ANY), pl.BlockSpec(memory_space=pl.ANY),
                  pl.BlockSpec((N_HEADS, rows_per_step, 128), lambda g: (0, jnp.minimum(g, last), 0)),
                  pl.BlockSpec((N_HEADS, QB, KB), lambda g: (0, 0, 0))] + [_ANY] * len(after),
        out_specs=[pl.BlockSpec((rows_per_step, D_A), lambda g: (jnp.minimum(g, last), 0)),
                   pl.BlockSpec((rows_per_step, D_A), lambda g: (jnp.maximum(g - lag, 0), 0)),
                   pl.BlockSpec((rows_per_step, D_A), lambda g: (jnp.maximum(g - lag, 0), 0)),
                   pl.BlockSpec((N_HEADS, ROLL_W), lambda g: (0, 0))],
        scratch_shapes=[kv_scr, kv_scr,
                        pltpu.VMEM((N_HEADS, KB, HEAD_DIM), F32), pltpu.VMEM((N_HEADS, KB, HEAD_DIM), F32),
                        pltpu.VMEM((N_HEADS, QB, KB), F32), pltpu.VMEM((QB, ROLL_W), F32),
                        pltpu.SemaphoreType.DMA((2,))],
        compiler_params=_params(56, dimension_semantics=("arbitrary",)),
    )(q3, d_att3, k3, v3, lse, bias, *after)


def _sgu_core(ub, vb, lg, lb):
    u, du = _gelu_and_grad(ub)
    v, dv = _gelu_and_grad(vb)
    mu = jnp.mean(v, axis=-1, keepdims=True)
    vc = v - mu
    rstd = lax.rsqrt(jnp.mean(vc * vc, axis=-1, keepdims=True) + EPS)
    xh = vc * rstd
    vn = xh * lg + lb
    return u, du, dv, rstd, xh, vn


def _tri():
    r = lax.broadcasted_iota(jnp.int32, (SGU_CHUNK, SGU_CHUNK), 0)
    c = lax.broadcasted_iota(jnp.int32, (SGU_CHUNK, SGU_CHUNK), 1)
    return r >= c


def _tail_sgu(att, zrest, x, target, w_pa, w_pb, w_out, b_gate, final_g, ln_g, ln_b, w_s, b_s_t, tm=256):
    S = x.shape[0]
    nt = S // tm
    chunks = tm // SGU_CHUNK

    def body(att_ref, ga_ref, ub_ref, vb_ref, gb_ref, gta_ref, gtb_ref, x_ref, t_ref,
             wpa_ref, wpb_ref, wout_ref, bg_ref, fg_ref, lg_ref, lb_ref, ws_ref, bst_ref,
             dout_ref, datt_ref, dzt_ref, dzs_ref, gwout_hbm, gwpa_hbm, gwpb_hbm,
             gbg_ref, gfg_ref, loss_ref, gws_ref, gbs_ref, glg_ref, glb_ref,
             acc_out, acc_pa, acc_pb, sg_scr, mix_scr, dvn_scr, bs_acc, sems):
        i = pl.program_id(0)

        @pl.when(i == 0)
        def _():
            for r in (acc_out, acc_pa, acc_pb, gbg_ref, gfg_ref, loss_ref, gws_ref, glg_ref, glb_ref, bs_acc):
                r[...] = jnp.zeros_like(r)

        u, du, dv, rstd, xh, vn = _sgu_core(ub_ref[...], vb_ref[...], lg_ref[...], lb_ref[...])
        vnb = vn.astype(BF)
        tri = _tri()
        blocks = [(g, slice(n * SGU_CHUNK, (n + 1) * SGU_CHUNK), slice(g * 128, (g + 1) * 128))
                  for g in range(N_GROUPS) for n in range(chunks)]
        wts = [jnp.where(tri, ws_ref[g], 0.0) for g in range(N_GROUPS)]
        for g, rs, cs in blocks:
            mixed = _dot(wts[g].astype(BF), vnb[rs, cs]) + bst_ref[:, g:g + 1]
            mix_scr[rs, cs] = mixed
            sg_scr[rs, cs] = u[rs, cs] * mixed

        att = att_ref[...]
        sg = sg_scr[...]
        sa, dsa = _silu_and_grad(ga_ref[...])
        sb, dsb = _silu_and_grad(gb_ref[...])
        ya = (att * sa).astype(BF)
        yb = (sg * sb).astype(BF)
        pa = _dot(ya, wpa_ref[...])
        pb = _dot(yb, wpb_ref[...])
        ga = _sigmoid(gta_ref[...] + bg_ref[:, 0:D_MODEL])
        gb = _sigmoid(gtb_ref[...] + bg_ref[:, D_MODEL:2 * D_MODEL])
        merged = (ga * pa + gb * pb).astype(BF)
        out = x_ref[...] + _dot(merged, wout_ref[...])
        r2 = lax.rsqrt(jnp.mean(out * out, axis=-1, keepdims=True) + EPS)
        nrm = out * r2
        fg = fg_ref[...]
        err = nrm * fg - t_ref[...]
        loss_ref[...] += 0.5 * jnp.sum(jnp.mean(err * err, axis=-1, keepdims=True))
        dy = err * (1.0 / D_MODEL)
        gfg_ref[...] += jnp.sum(dy * nrm, axis=0, keepdims=True)
        dn = dy * fg
        d_out = r2 * (dn - nrm * jnp.mean(dn * nrm, axis=-1, keepdims=True))
        dout_ref[...] = d_out
        d_outb = d_out.astype(BF)
        acc_out[...] += _dot_tn(merged, d_outb)
        dm = _dot_nt(d_outb, wout_ref[...])
        d_pa = (dm * ga).astype(BF)
        d_pb = (dm * gb).astype(BF)
        d_gta = dm * pa * (ga * (1.0 - ga))
        d_gtb = dm * pb * (gb * (1.0 - gb))
        gbg_ref[:, 0:D_MODEL] += jnp.sum(d_gta, axis=0, keepdims=True)
        gbg_ref[:, D_MODEL:2 * D_MODEL] += jnp.sum(d_gtb, axis=0, keepdims=True)
        dzt_ref[:, 2 * D_A:2 * D_A + D_MODEL] = d_gta.astype(BF)
        dzt_ref[:, 2 * D_A + D_MODEL:] = d_gtb.astype(BF)
        acc_pa[...] += _dot_tn(ya, d_pa)
        acc_pb[...] += _dot_tn(yb, d_pb)
        d_ya = _dot_nt(d_pa, wpa_ref[...])
        d_yb = _dot_nt(d_pb, wpb_ref[...])
        d_att = (d_ya * sa).astype(BF)
        for hd in range(N_HEADS):
            datt_ref[hd] = d_att[:, hd * HEAD_DIM:(hd + 1) * HEAD_DIM]
        dzt_ref[:, 0:D_A] = (d_ya * att * dsa).astype(BF)
        dzt_ref[:, D_A:2 * D_A] = (d_yb * sg * dsb).astype(BF)

        dsg = d_yb * sb
        dzs_ref[:, 0:D_B] = (dsg * mix_scr[...] * du).astype(BF)
        dmix = dsg * u
        for g, rs, cs in blocks:
            dmb = dmix[rs, cs].astype(BF)
            bs_acc[:, cs] += dmix[rs, cs]
            gws_ref[g] += _dot_nt(dmb, vnb[rs, cs])
            dvn_scr[rs, cs] = _dot(wts[g].T.astype(BF), dmb)
        dvn = dvn_scr[...]
        glg_ref[...] += jnp.sum(dvn * xh, axis=0, keepdims=True)
        glb_ref[...] += jnp.sum(dvn, axis=0, keepdims=True)
        dxh = dvn * lg_ref[...]
        dvv = rstd * (dxh - jnp.mean(dxh, axis=-1, keepdims=True)
                      - xh * jnp.mean(dxh * xh, axis=-1, keepdims=True))
        dzs_ref[:, D_B:2 * D_B] = (dvv * dv).astype(BF)

        @pl.when(i == nt - 1)
        def _():
            cps = [pltpu.make_async_copy(acc_out, gwout_hbm, sems.at[0]),
                   pltpu.make_async_copy(acc_pa, gwpa_hbm, sems.at[1]),
                   pltpu.make_async_copy(acc_pb, gwpb_hbm, sems.at[2])]
            for cp in cps:
                cp.start()
            lane = lax.broadcasted_iota(jnp.int32, (SGU_CHUNK, 128), 1)
            cols = jnp.zeros((SGU_CHUNK, 128), F32)
            for g in range(N_GROUPS):
                gws_ref[g] = jnp.where(tri, gws_ref[g], 0.0)
                col = jnp.sum(bs_acc[:, g * 128:(g + 1) * 128], axis=-1, keepdims=True)
                cols = jnp.where(lane == g, col, cols)
            gbs_ref[...] = cols
            for cp in cps:
                cp.wait()

    c2 = lambda i: (0, 0)
    c3 = lambda i: (0, 0, 0)
    zcol = lambda w, blk: pl.BlockSpec((tm, w), lambda i: (i, blk))
    row = lambda w: pl.BlockSpec((tm, w), lambda i: (i, 0))
    return pl.pallas_call(
        body, name="tail", grid=(nt,),
        out_shape=(jax.ShapeDtypeStruct((S, D_MODEL), F32), jax.ShapeDtypeStruct((N_HEADS, S, HEAD_DIM), BF),
                   jax.ShapeDtypeStruct((S, 3072), BF), jax.ShapeDtypeStruct((S, 2 * D_B), BF),
                   jax.ShapeDtypeStruct((D_MODEL, D_MODEL), F32), jax.ShapeDtypeStruct((D_A, D_MODEL), F32),
                   jax.ShapeDtypeStruct((D_B, D_MODEL), F32),
                   jax.ShapeDtypeStruct((1, 2 * D_MODEL), F32), jax.ShapeDtypeStruct((1, D_MODEL), F32),
                   jax.ShapeDtypeStruct((1, 128), F32),
                   jax.ShapeDtypeStruct((N_GROUPS, 128, 128), F32), jax.ShapeDtypeStruct((SGU_CHUNK, 128), F32),
                   jax.ShapeDtypeStruct((1, D_B), F32), jax.ShapeDtypeStruct((1, D_B), F32)),
        in_specs=[row(D_A), zcol(512, 0), zcol(512, 1), zcol(512, 2), zcol(512, 3),
                  zcol(D_MODEL, 2), zcol(D_MODEL, 3), row(D_MODEL), row(D_MODEL),
                  pl.BlockSpec((D_A, D_MODEL), c2), pl.BlockSpec((D_B, D_MODEL), c2),
                  pl.BlockSpec((D_MODEL, D_MODEL), c2),
                  pl.BlockSpec((1, 2 * D_MODEL), c2), pl.BlockSpec((1, D_MODEL), c2),
                  pl.BlockSpec((1, D_B), c2), pl.BlockSpec((1, D_B), c2),
                  pl.BlockSpec((N_GROUPS, 128, 128), c3), pl.BlockSpec((128, N_GROUPS), c2)],
        out_specs=[row(D_MODEL), pl.BlockSpec((N_HEADS, tm, HEAD_DIM), lambda i: (0, i, 0)),
                   row(3072), row(2 * D_B), _ANY, _ANY, _ANY,
                   pl.BlockSpec((1, 2 * D_MODEL), c2), pl.BlockSpec((1, D_MODEL), c2),
                   pl.BlockSpec((1, 128), c2),
                   pl.BlockSpec((N_GROUPS, 128, 128), c3), pl.BlockSpec((SGU_CHUNK, 128), c2),
                   pl.BlockSpec((1, D_B), c2), pl.BlockSpec((1, D_B), c2)],
        scratch_shapes=[pltpu.VMEM((D_MODEL, D_MODEL), F32), pltpu.VMEM((D_A, D_MODEL), F32),
                        pltpu.VMEM((D_B, D_MODEL), F32),
                        pltpu.VMEM((tm, D_B), F32), pltpu.VMEM((tm, D_B), F32), pltpu.VMEM((tm, D_B), F32),
                        pltpu.VMEM((SGU_CHUNK, D_B), F32), pltpu.SemaphoreType.DMA((3,))],
        compiler_params=_params(58, dimension_semantics=("arbitrary",)),
    )(att, zrest, zrest, zrest, zrest, zrest, zrest, x, target, w_pa, w_pb, w_out, b_gate, final_g,
      ln_g, ln_b, w_s, b_s_t)


_DZ_MAP = ((0, 0), (1, 0), (2, 0), (3, 0), (4, 0), (4, 1), (3, 1), (3, 2), (3, 3), (3, 4), (3, 5))


def _dh_gradx(dq, dk, dv, dzt, dzs, w_in_bf, x, norm_g, d_out, tm=512, after=()):
    S = x.shape[0]

    def body(dq_ref, dk_ref, dv_ref, dzt_ref, dzs_ref, w_ref, x_ref, g_ref, dout_ref, gx_ref, gn_ref):
        i = pl.program_id(0)

        @pl.when(i == 0)
        def _():
            gn_ref[...] = jnp.zeros_like(gn_ref)

        pieces = (dq_ref, dk_ref, dv_ref, dzt_ref, dzs_ref)
        dh = jnp.zeros((tm, D_MODEL), F32)
        for j, (pc, blk) in enumerate(_DZ_MAP):
            dh += _dot_nt(pieces[pc][:, blk * 512:(blk + 1) * 512], w_ref[:, j * 512:(j + 1) * 512])
        xv = x_ref[...]
        r = lax.rsqrt(jnp.mean(xv * xv, axis=-1, keepdims=True) + EPS)
        nrm = xv * r
        gn_ref[...] += jnp.sum(dh * nrm, axis=0, keepdims=True)
        dn = dh * g_ref[...]
        gx_ref[...] = r * (dn - nrm * jnp.mean(dn * nrm, axis=-1, keepdims=True)) + dout_ref[...]

    row = lambda w: pl.BlockSpec((tm, w), lambda i: (i, 0))
    c2 = lambda i: (0, 0)
    return pl.pallas_call(
        _after(body, 9, after), name="dh_gradx", grid=(S // tm,),
        out_shape=(jax.ShapeDtypeStruct((S, D_MODEL), F32), jax.ShapeDtypeStruct((1, D_MODEL), F32)),
        in_specs=[row(512), row(512), row(512), row(3072), row(1024),
                  pl.BlockSpec((D_MODEL, D_IN), c2, pipeline_mode=pl.Buffered(1)), row(D_MODEL),
                  pl.BlockSpec((1, D_MODEL), c2), row(D_MODEL)]
        + [_ANY] * len(after),
        out_specs=[row(D_MODEL), pl.BlockSpec((1, D_MODEL), c2)],
        compiler_params=_params(48, dimension_semantics=("arbitrary",)),
    )(dq, dk, dv, dzt, dzs, w_in_bf, x, norm_g, d_out, *after)


def _gw_in(ht, dq, dk, dv, dzt, dzs, tn=512, after=()):
    S = ht.shape[1]
    per = 512 // tn
    cols = tuple((pc, per * blk + h) for pc, blk in _DZ_MAP for h in range(per))

    def body(ht_ref, dq_ref, dk_ref, dv_ref, dzt_ref, dzs_ref, o_ref, ob_ref):
        j = pl.program_id(0)
        pieces = (dq_ref, dk_ref, dv_ref, dzt_ref, dzs_ref)
        for pc in range(5):
            hit = functools.reduce(jnp.logical_or, [j == jj for jj, (p, _) in enumerate(cols) if p == pc])

            @pl.when(hit)
            def _(pc=pc):
                g = _dot(ht_ref[...], pieces[pc][...])
                o_ref[...] = g
                ob_ref[...] = g.astype(BF)

    def piece_spec(pc):
        cur = next(blk for p, blk in cols if p == pc)
        held = []
        for p, blk in cols:
            cur = blk if p == pc else cur
            held.append(cur)

        def index_map(j):
            blk = jnp.int32(held[0])
            for jj in range(1, len(held)):
                if held[jj] != held[jj - 1]:
                    blk = jnp.where(j >= jj, jnp.int32(held[jj]), blk)
            return (0, blk)

        return pl.BlockSpec((S, tn), index_map)

    return pl.pallas_call(
        _after(body, 6, after), name="gw_in", grid=(len(cols),),
        out_shape=(jax.ShapeDtypeStruct((D_MODEL, D_IN), F32), jax.ShapeDtypeStruct((D_MODEL, D_IN), BF)),
        in_specs=[pl.BlockSpec((D_MODEL, S), lambda j: (0, 0), pipeline_mode=pl.Buffered(1))]
        + [piece_spec(pc) for pc in range(5)]
        + [_ANY] * len(after),
        out_specs=[pl.BlockSpec((D_MODEL, tn), lambda j: (0, j)), pl.BlockSpec((D_MODEL, tn), lambda j: (0, j))],
        compiler_params=_params(56, dimension_semantics=("arbitrary",)),
    )(ht, dq, dk, dv, dzt, dzs, *after)


_HBM = pl.BlockSpec(memory_space=pltpu.HBM)
_SEM = pl.BlockSpec(memory_space=pltpu.SEMAPHORE)
_ANY = pl.BlockSpec(memory_space=pl.ANY)
_EFFECT = pltpu.SideEffectType.DATAFLOW_SIDE_EFFECTING


def _in_hbm(a):
    return pltpu.with_memory_space_constraint(a, pltpu.HBM)


def _after(body, n_in, after):
    if not after:
        return body
    return lambda *refs: body(*refs[:n_in], *refs[n_in + len(after):])


class _Started:
    def __init__(self, send, recv, bufs, token):
        self.send, self.recv, self.bufs, self.token = send, recv, bufs, token


_PEER_SETS = {"sibling": 7, "chips": 8, "both": 9}


def _peers(kind):
    x, y, c, chips = _mesh_pos()
    return ([(x, y, 1 - c)] if kind in ("sibling", "both") else []) + (
        [(cx, cy, c) for cx, cy in chips] if kind in ("chips", "both") else [])


def _signal_peers(kind):
    barrier = pltpu.get_barrier_semaphore()
    targets = _peers(kind)
    for peer in targets:
        pl.semaphore_signal(barrier, inc=1, device_id=peer, device_id_type=MESH)
    return lambda: pl.semaphore_wait(barrier, len(targets))


def _split_start(name, bufs, n_copies, copies, peers, after=()):
    nb = len(bufs)

    def body(*refs):
        _signal_peers(peers)()
        refs = refs[:nb] + refs[nb + len(after):]
        for cp in copies(refs[:nb], refs[nb], refs[nb + 1]):
            cp.start()
        refs[-1][...] = jnp.zeros_like(refs[-1])

    outs = pl.pallas_call(
        body, name=name,
        out_shape=(pltpu.SemaphoreType.DMA((n_copies,)), pltpu.SemaphoreType.DMA((n_copies,)),
                   *[pltpu.HBM(b.shape, b.dtype) for b in bufs], jax.ShapeDtypeStruct((8, 128), F32)),
        in_specs=[_HBM] * nb + [_ANY] * len(after),
        out_specs=(_SEM, _SEM, *[_HBM] * nb, pl.BlockSpec(memory_space=pltpu.VMEM)),
        input_output_aliases={k: 2 + k for k in range(nb)},
        compiler_params=_params(1, has_side_effects=_EFFECT, collective_id=_PEER_SETS[peers]),
    )(*[_in_hbm(b) for b in bufs], *after)
    return _Started(outs[0], outs[1], list(outs[2:2 + nb]), outs[-1])


def _split_wait(name, started, copies, after):
    nb = len(started.bufs)

    def body(*refs):
        for cp in copies(refs[:nb], refs[nb], refs[nb + 1]):
            cp.wait_send()
            cp.wait_recv()

    return list(pl.pallas_call(
        body, name=name,
        out_shape=tuple(pltpu.HBM(b.shape, b.dtype) for b in started.bufs),
        in_specs=[_HBM] * nb + [_SEM, _SEM, _ANY],
        out_specs=tuple([_HBM] * nb),
        input_output_aliases={k: k for k in range(nb)},
        compiler_params=_params(1, has_side_effects=_EFFECT),
    )(*started.bufs, started.send, started.recv, after))


def _x1_copies(ws):
    def copies(refs, send_sems, recv_sems):
        x, y, c, _ = _mesh_pos()
        out = []
        for k, w in enumerate(ws):
            for s in range(N_SHARD):
                out.append(pltpu.make_async_remote_copy(
                    src_ref=_UNITS[w](refs[k], s, 1 - c), dst_ref=refs[len(ws) + k].at[s],
                    send_sem=send_sems.at[N_SHARD * k + s], recv_sem=recv_sems.at[N_SHARD * k + s],
                    device_id=(x, y, 1 - c), device_id_type=MESH))
        return out
    return copies


def _x2_copies(n):
    def copies(refs, send_sems, recv_sems):
        x, y, c, chips = _mesh_pos()
        out = []
        for j, (cx, cy) in enumerate(chips):
            for k in range(n):
                out.append(pltpu.make_async_remote_copy(
                    src_ref=refs[k].at[2 * cx + cy], dst_ref=refs[n + k].at[j],
                    send_sem=send_sems.at[3 * k + j], recv_sem=recv_sems.at[3 * k + j],
                    device_id=(cx, cy, c), device_id_type=MESH))
        return out
    return copies


def _x3_copies(ws):
    def copies(refs, send_sems, recv_sems):
        x, y, c, _ = _mesh_pos()
        out = []
        for k, w in enumerate(ws):
            rows = _HALF_ROWS[w]
            mine = refs[k].at[pl.ds(_mo(c * rows, rows), rows), :]
            out.append(pltpu.make_async_remote_copy(
                src_ref=mine, dst_ref=mine, send_sem=send_sems.at[k], recv_sem=recv_sems.at[k],
                device_id=(x, y, 1 - c), device_id_type=MESH))
        return out
    return copies


def _x1_lands(ws, dtype=F32):
    return [lax.empty((N_SHARD,) + _UNIT_SHAPES[w], dtype) for w in ws]


def _x2_lands(ws):
    return [lax.empty((3,) + _UNIT_SHAPES[w], BF) for w in ws]


def _grad_add1(w, g, recv, pos):
    ur, uc = _UNIT_SHAPES[w]

    def body(pos_ref, g_ref, r_ref, own_ref, csb_ref):
        v = g_ref[...] + r_ref[0].astype(F32)
        csb_ref[0] = v.astype(BF)

        @pl.when(pl.program_id(0) == pos_ref[1])
        def _():
            own_ref[...] = v

    u3 = lambda s, pos: (s, 0, 0)
    return pl.pallas_call(
        body, name=f"grad_add1_{w}",
        grid_spec=pltpu.PrefetchScalarGridSpec(
            num_scalar_prefetch=1, grid=(N_SHARD,),
            in_specs=[pl.BlockSpec((ur, uc), lambda s, pos: (pos[0], s)), pl.BlockSpec((1, ur, uc), u3)],
            out_specs=[pl.BlockSpec((ur, uc), lambda s, pos: (0, 0)), pl.BlockSpec((1, ur, uc), u3)]),
        out_shape=(jax.ShapeDtypeStruct((ur, uc), F32), jax.ShapeDtypeStruct((N_SHARD, ur, uc), BF)),
        compiler_params=_params(40, dimension_semantics=("arbitrary",)),
    )(pos, g, recv)


def _grad_add1_group(ws, gs, recvs):
    n = len(ws)

    def body(*refs):
        c = lax.axis_index("c")
        for k, w in enumerate(ws):
            g, r, cs, csb = refs[k], refs[n + k], refs[2 * n + k], refs[3 * n + k]
            for s in range(N_SHARD):
                v = _UNITS[w](g, s, c)[...] + r[s]
                cs[s] = v
                csb[s] = v.astype(BF)

    vm = pl.BlockSpec(memory_space=pltpu.VMEM)
    outs = pl.pallas_call(
        body, name="grad_add1_group",
        out_shape=tuple(jax.ShapeDtypeStruct((N_SHARD,) + _UNIT_SHAPES[w], dt) for dt in (F32, BF) for w in ws),
        in_specs=[vm] * (2 * n), out_specs=[vm] * (2 * n),
        compiler_params=_params(32),
    )(*gs, *recvs)
    return list(outs[:n]), list(outs[n:])


def _grad_add2_group(ws, css, recvs):
    n = len(ws)

    def body(*refs):
        x, y, c, _ = _mesh_pos()
        for k, w in enumerate(ws):
            cs, r, o = refs[k], refs[n + k], refs[2 * n + k]
            rows = _HALF_ROWS[w]
            total = ((cs[2 * x + y] + r[0].astype(F32)) + r[1].astype(F32)) + r[2].astype(F32)
            o[pl.ds(_mo(c * rows, rows), rows), :] = total

    vm = pl.BlockSpec(memory_space=pltpu.VMEM)
    return list(pl.pallas_call(
        body, name="grad_add2_group",
        out_shape=tuple(jax.ShapeDtypeStruct(_SHARD_SHAPES[w], F32) for w in ws),
        in_specs=[vm] * (2 * n), out_specs=[vm] * n,
        compiler_params=_params(32),
    )(*css, *recvs))


def _grad_add2(w, own, recv, pos):
    ur, uc = _UNIT_SHAPES[w]
    nt = 4
    tr = ur // nt

    def body(pos_ref, own_ref, r_ref, o_ref):
        o_ref[...] = ((own_ref[...] + r_ref[0].astype(F32)) + r_ref[1].astype(F32)) + r_ref[2].astype(F32)

    return pl.pallas_call(
        body, name=f"grad_add2_{w}",
        grid_spec=pltpu.PrefetchScalarGridSpec(
            num_scalar_prefetch=1, grid=(nt,),
            in_specs=[pl.BlockSpec((tr, uc), lambda t, pos: (t, 0)),
                      pl.BlockSpec((3, tr, uc), lambda t, pos: (0, t, 0))],
            out_specs=pl.BlockSpec((tr, uc), lambda t, pos: (pos[0] * nt + t, 0))),
        out_shape=jax.ShapeDtypeStruct(_SHARD_SHAPES[w], F32),
        compiler_params=_params(32, dimension_semantics=("arbitrary",)),
    )(pos, own, recv)


def _adamw_math(w, g, m, v):
    m = ADAM_B1 * m + (1.0 - ADAM_B1) * g
    v = ADAM_B2 * v + (1.0 - ADAM_B2) * (g * g)
    m_hat = m / ADAM_C1
    v_hat = v / ADAM_C2
    delta = -ADAM_LR * (m_hat / (jnp.sqrt(v_hat) + ADAM_EPS) + ADAM_WD * w)
    return delta, m, v


def _adamw_group(ws_, gs, ms, vs, after=()):
    n = len(ws_)

    def body(*refs):
        for k in range(n):
            w, g, m, v = (refs[j * n + k] for j in range(4))
            d, nm, nv, gc = (refs[(4 + j) * n + k] for j in range(4))
            gv = g[...]
            d[...], nm[...], nv[...] = _adamw_math(w[...], gv, m[...], v[...])
            gc[...] = gv

    vm = pl.BlockSpec(memory_space=pltpu.VMEM)
    outs = pl.pallas_call(
        _after(body, 4 * n, after), name="adamw_group",
        out_shape=tuple(jax.ShapeDtypeStruct(a.shape, F32) for _ in range(4) for a in ws_),
        in_specs=[vm] * (4 * n) + [_ANY] * len(after), out_specs=[vm] * (4 * n),
        compiler_params=_params(32),
    )(*ws_, *gs, *ms, *vs, *after)
    return [tuple(outs[j * n + k] for j in range(4)) for k in range(n)]


def _adamw(name, w, g, m, v, tr=256, after=()):
    rows, cols = w.shape

    def body(w_ref, g_ref, m_ref, v_ref, d_ref, nm_ref, nv_ref, gc_ref):
        gv = g_ref[...]
        d_ref[...], nm_ref[...], nv_ref[...] = _adamw_math(w_ref[...], gv, m_ref[...], v_ref[...])
        gc_ref[...] = gv

    spec = pl.BlockSpec((tr, cols), lambda i: (i, 0))
    return pl.pallas_call(
        _after(body, 4, after), name=name, grid=(rows // tr,),
        out_shape=tuple(jax.ShapeDtypeStruct((rows, cols), F32) for _ in range(4)),
        in_specs=[spec] * 4 + [_ANY] * len(after), out_specs=[spec] * 4,
        compiler_params=_params(32, dimension_semantics=("arbitrary",)),
    )(w, g, m, v, *after)


_REL_PAD = 384
_VEC_FIELDS = (("norm_g", 0, D_MODEL), ("b_gate", 1024, 2 * D_MODEL), ("sgu_ln_g", 3072, D_B),
               ("sgu_ln_b", 3584, D_B), ("b_s", 4096, N_GROUPS * 128), ("final_g", 4608, D_MODEL))
_LOSS_OFF = 5632
_REL_OFF = 5760
_NV = _REL_OFF + N_HEADS * _REL_PAD
_N_FIELDS = len(_VEC_FIELDS) + 2


_B_S_FIELD = [f[0] for f in _VEC_FIELDS].index("b_s")


def _assemble_row(dst, fields, transposed_b_s):
    for f, (_, off, n) in enumerate(_VEC_FIELDS):
        if transposed_b_s and f == _B_S_FIELD:
            t = fields[f][...].T
            for g in range(N_GROUPS):
                dst[:, off + 128 * g:off + 128 * (g + 1)] = t[g:g + 1, :]
        else:
            dst[:, off:off + n] = fields[f][...]
    for r in range(N_HEADS):
        dst[:, _REL_OFF + _REL_PAD * r:_REL_OFF + _REL_PAD * (r + 1)] = fields[len(_VEC_FIELDS)][r:r + 1, :]


def _small_reduce(grads, loss_row, after=()):
    n_in = _N_FIELDS + 1

    def body(*refs):
        g_refs, loss_ref = refs[:_N_FIELDS], refs[_N_FIELDS]
        out_v, out_w = refs[n_in:n_in + 2]
        mine_v, mine_w, gath_v, gath_w, send_sems, recv_sems = refs[n_in + 2:]
        x, y, c, chips = _mesh_pos()
        me, sibling = (x, y, c), (x, y, 1 - c)

        peers_entered = _signal_peers("both")
        _assemble_row(mine_v, g_refs, True)
        mine_v[:, _LOSS_OFF:_LOSS_OFF + 128] = loss_ref[...]
        mine_w[...] = g_refs[-1][...].astype(BF)
        peers_entered()
        my_k = 4 * x + 2 * y + c
        gath_v[my_k] = mine_v[...]
        gath_w[my_k] = mine_w[...]

        def copy(k, gath, block, to, src=None):
            dst = gath.at[4 * block[0] + 2 * block[1] + block[2]]
            return pltpu.make_async_remote_copy(
                src_ref=dst if src is None else src, dst_ref=dst,
                send_sem=send_sems.at[k], recv_sem=recv_sems.at[k], device_id=to, device_id_type=MESH)

        bufs = ((gath_v, mine_v), (gath_w, mine_w))
        first, passed = [], []
        for b, (gath, mine) in enumerate(bufs):
            first.append(copy(7 * b, gath, me, sibling, src=mine))
            first += [copy(7 * b + 1 + j, gath, me, (*chip, c), src=mine) for j, chip in enumerate(chips)]
        for cp in first:
            cp.start()
        for b, (gath, _) in enumerate(bufs):
            for j, chip in enumerate(chips):
                copy(7 * b + 1 + j, gath, (*chip, c), me).wait_recv()
                cp = copy(7 * b + 4 + j, gath, (*chip, c), sibling)
                cp.start()
                passed.append(cp)
        for b, (gath, _) in enumerate(bufs):
            copy(7 * b, gath, sibling, me).wait_recv()
            for j, chip in enumerate(chips):
                copy(7 * b + 4 + j, gath, (*chip, 1 - c), me).wait_recv()
        for cp in first + passed:
            cp.wait_send()

        tot_v, tot_w = gath_v[0], gath_w[0].astype(F32)
        for k in range(1, 8):
            tot_v = tot_v + gath_v[k]
            tot_w = tot_w + gath_w[k].astype(F32)
        out_v[...] = tot_v
        out_w[...] = tot_w

    vm = pl.BlockSpec(memory_space=pltpu.VMEM)
    return pl.pallas_call(
        _after(body, n_in, after), name="small_reduce",
        out_shape=(jax.ShapeDtypeStruct((1, _NV), F32), jax.ShapeDtypeStruct((N_GROUPS * 128, 128), F32)),
        in_specs=[vm] * n_in + [_ANY] * len(after), out_specs=[vm] * 2,
        scratch_shapes=[pltpu.VMEM((1, _NV), F32), pltpu.VMEM((N_GROUPS * 128, 128), BF),
                        pltpu.VMEM((8, 1, _NV), F32), pltpu.VMEM((8, N_GROUPS * 128, 128), BF),
                        pltpu.SemaphoreType.DMA((14,)), pltpu.SemaphoreType.DMA((14,))],
        compiler_params=_params(32, collective_id=_PEER_SETS["both"]),
    )(*grads, loss_row, *after)


def _small_adamw(tot_v, tot_w, params):
    n_in = 2 + 3 * _N_FIELDS

    def body(*refs):
        tv_ref, tw_ref = refs[:2]
        p_refs = [refs[2 + k * _N_FIELDS:2 + (k + 1) * _N_FIELDS] for k in range(3)]
        outs = refs[n_in:n_in + 4 * _N_FIELDS + 1]
        wmv = refs[-1]
        for k in range(3):
            _assemble_row(wmv.at[k], p_refs[k], False)
            wmv[k, :, _LOSS_OFF:_LOSS_OFF + 128] = jnp.zeros((1, 128), F32)
        tot_v, tot_w = tv_ref[...], tw_ref[...]
        res_v = (tot_v,) + _adamw_math(wmv[0], tot_v, wmv[1], wmv[2])
        res_w = (tot_w,) + _adamw_math(p_refs[0][-1][...], tot_w, p_refs[1][-1][...], p_refs[2][-1][...])
        for kind in range(4):
            o = outs[kind * _N_FIELDS:(kind + 1) * _N_FIELDS]
            for f, (_, off, n) in enumerate(_VEC_FIELDS):
                o[f][...] = res_v[kind][:, off:off + n]
            for r in range(N_HEADS):
                o[len(_VEC_FIELDS)][r:r + 1, :] = res_v[kind][:, _REL_OFF + _REL_PAD * r:_REL_OFF + _REL_PAD * (r + 1)]
            o[-1][...] = res_w[kind]
        outs[-1][...] = tot_v[:, _LOSS_OFF:_LOSS_OFF + 128]

    field_shapes = [(1, n) for _, _, n in _VEC_FIELDS] + [(N_HEADS, _REL_PAD), (N_GROUPS * 128, 128)]
    vm = pl.BlockSpec(memory_space=pltpu.VMEM)
    operands = [tot_v, tot_w] + [a for p in params for a in p]
    assert len(operands) == n_in
    outs = pl.pallas_call(
        body, name="small_adamw",
        out_shape=tuple(jax.ShapeDtypeStruct(s, F32) for _ in range(4) for s in field_shapes)
        + (jax.ShapeDtypeStruct((1, 128), F32),),
        in_specs=[vm] * n_in, out_specs=[vm] * (4 * _N_FIELDS + 1),
        scratch_shapes=[pltpu.VMEM((3, 1, _NV), F32)],
        compiler_params=_params(32),
    )(*operands)
    return [outs[k * _N_FIELDS:(k + 1) * _N_FIELDS] for k in range(4)], outs[-1]


def _small_fields(norm_g, b_gate, ln_g, ln_b, b_s, final_g, rel_bias, w_s):
    rel = jnp.pad(rel_bias.reshape(N_HEADS, N_REL), ((0, 0), (0, _REL_PAD - N_REL)))
    return (norm_g, b_gate, ln_g, ln_b, b_s.reshape(1, N_GROUPS * 128), final_g.reshape(1, D_MODEL),
            rel, w_s.reshape(N_GROUPS * 128, 128))


def _small_outputs(fields):
    n_g, b_g, l_g, l_b, b_s, f_g, rel, w_s = fields
    return (n_g, b_g, rel[:, :N_REL].reshape(1, N_HEADS, N_REL), l_g, l_b,
            w_s.reshape(1, N_GROUPS, 128, 128), b_s.reshape(1, N_GROUPS, 128), f_g.reshape(D_MODEL))


def _bias_row(rel_bias):
    hi = rel_bias[:, N_REL - 1:N_REL]
    lo = rel_bias[:, 0:1]
    return jnp.concatenate([jnp.broadcast_to(hi, (N_HEADS, 384)), rel_bias[:, ::-1],
                            jnp.broadcast_to(lo, (N_HEADS, 191)), jnp.broadcast_to(hi, (N_HEADS, 192))], axis=1)


def kernel(x, norm_g, w_in, b_gate, rel_bias, sgu_ln_g, sgu_ln_b, w_s, b_s, w_pa, w_pb, w_out, final_g, loss_target, m_norm_g, m_w_in, m_b_gate, m_rel_bias, m_sgu_ln_g, m_sgu_ln_b, m_w_s, m_b_s, m_w_pa, m_w_pb, m_w_out, m_final_g, v_norm_g, v_w_in, v_b_gate, v_rel_bias, v_sgu_ln_g, v_sgu_ln_b, v_w_s, v_b_s, v_w_pa, v_w_pb, v_w_out, v_final_g):
    S = x.shape[1]
    xs = x.reshape(S, D_MODEL)
    tgt = loss_target.reshape(S, D_MODEL)
    big_w = (w_in[0], w_pa[0], w_pb[0], w_out[0])
    big_m = (m_w_in[0], m_w_pa[0], m_w_pb[0], m_w_out[0])
    big_v = (v_w_in[0], v_w_pa[0], v_w_pb[0], v_w_out[0])
    rel = rel_bias[0]
    ws = w_s[0]
    bst = b_s[0].T
    fg = final_g.reshape(1, D_MODEL)
    pos = jnp.stack([lax.axis_index("c"), 2 * lax.axis_index("x") + lax.axis_index("y")]).astype(jnp.int32)

    (w_in_bf,), staged, band_bias = _ag_weights((0,), big_w[:1], (1, 2, 3), big_w[1:], _bias_row(rel))
    ag_s = _split_start("ag_small_start", staged, 9, _gather_copies((1, 2, 3)), "chips", after=(w_in_bf,))

    ht, q3, k3, v3, zrest = _inproj_fwd(xs, norm_g, w_in_bf, after=(ag_s.token,))
    att, lse = _attn_fwd(q3, k3, v3, band_bias)
    w_pa_bf, w_pb_bf, w_out_bf = _split_wait("ag_small_wait", ag_s, _gather_copies((1, 2, 3)), att)
    (d_out, d_att, dzt, dzs, gw_out, gw_pa, gw_pb, g_bgate, g_final, loss_row,
     g_ws, g_bs_t, g_lng, g_lnb) = _tail_sgu(
        att, zrest, xs, tgt, w_pa_bf, w_pb_bf, w_out_bf, b_gate, fg, sgu_ln_g, sgu_ln_b, ws, bst)
    ws_s, ws_i = (1, 2, 3), (0,)

    x1s = _split_start("gx1s_start", [gw_pa, gw_pb, gw_out] + _x1_lands(ws_s), 12, _x1_copies(ws_s), "sibling")
    dq, dk, dv, d_gp = _attn_bwd(q3, k3, v3, d_att, lse, band_bias, after=(x1s.token,))
    got = _split_wait("gx1s_wait", x1s, _x1_copies(ws_s), dq)
    cs_s, csb_s = _grad_add1_group(ws_s, got[:3], got[3:])

    x2s = _split_start("gx2s_start", csb_s + _x2_lands(ws_s), 9, _x2_copies(3), "chips")
    gw_in, gw_in_bf = _gw_in(ht, dq, dk, dv, dzt, dzs, after=(x2s.token,))
    x1i = _split_start("gx1i_start", [gw_in_bf] + _x1_lands(ws_i, BF), 4, _x1_copies(ws_i), "sibling")
    got = _split_wait("gx2s_wait", x2s, _x2_copies(3), x1i.token)
    halves_s = _grad_add2_group(ws_s, cs_s, got[3:])
    x3s = _split_start("gx3s_start", halves_s, 3, _x3_copies(ws_s), "sibling")
    got = _split_wait("gx1i_wait", x1i, _x1_copies(ws_i), x3s.token)
    sum_i = _grad_add1(0, gw_in, got[1], pos)

    x2i = _split_start("gx2i_start", [sum_i[1]] + _x2_lands(ws_i), 3, _x2_copies(1), "chips")
    grad_x, g_norm = _dh_gradx(dq, dk, dv, dzt, dzs, w_in_bf, xs, norm_g, d_out, after=(x2i.token,))
    g_shards_s = _split_wait("gx3s_wait", x3s, _x3_copies(ws_s), grad_x)
    got = _split_wait("gx2i_wait", x2i, _x2_copies(1), grad_x)
    half_i = _grad_add2(0, sum_i[0], got[1], pos)
    x3i = _split_start("gx3i_start", [half_i], 1, _x3_copies(ws_i), "sibling")
    big = [None] * 4
    big[1:] = _adamw_group(big_w[1:], g_shards_s, big_m[1:], big_v[1:], after=(x3i.token,))

    g_rel = jnp.pad(d_gp[:, 384:384 + N_REL][:, ::-1], ((0, 0), (0, _REL_PAD - N_REL)))
    small_grads = (g_norm, g_bgate, g_lng, g_lnb, g_bs_t, g_final, g_rel, g_ws.reshape(N_GROUPS * 128, 128))
    small_params = (_small_fields(norm_g, b_gate, sgu_ln_g, sgu_ln_b, b_s, final_g, rel_bias, w_s),
                    _small_fields(m_norm_g, m_b_gate, m_sgu_ln_g, m_sgu_ln_b, m_b_s, m_final_g, m_rel_bias, m_w_s),
                    _small_fields(v_norm_g, v_b_gate, v_sgu_ln_g, v_sgu_ln_b, v_b_s, v_final_g, v_rel_bias, v_w_s))
    tot_v, tot_w = _small_reduce(small_grads, loss_row, after=(x3i.token,))
    (gsum, sdelta, sm, sv), loss_out = _small_adamw(tot_v, tot_w, small_params)

    g_shard_i, = _split_wait("gx3i_wait", x3i, _x3_copies(ws_i), loss_out)
    big[0] = _adamw("adamw_w_in", big_w[0], g_shard_i, big_m[0], big_v[0])
    sg_out, sd_out, sm_out, sv_out = (_small_outputs(f) for f in (gsum, sdelta, sm, sv))
    loss = loss_out[0, 0]

    def assemble(small, bigs):
        n_g, b_g, r_b, l_g, l_b, w_s_, b_s_, f_g = small
        b_in, b_pa, b_pb, b_out = (b[None] for b in bigs)
        return (n_g, b_in, b_g, r_b, l_g, l_b, w_s_, b_s_, b_pa, b_pb, b_out, f_g)

    grads_out = assemble(sg_out, [b[3] for b in big])
    delta_out = assemble(sd_out, [b[0] for b in big])
    m_out = assemble(sm_out, [b[1] for b in big])
    v_out = assemble(sv_out, [b[2] for b in big])
    return (loss, grad_x.reshape(1, S, D_MODEL), *grads_out, *delta_out, *m_out, *v_out)
```

```python
import functools
import math

import jax
import jax.numpy as jnp
from jax import lax
from jax.experimental import pallas as pl
from jax.experimental.pallas import tpu as pltpu

F32 = jnp.float32
BF = jnp.bfloat16
MESH = pl.DeviceIdType.MESH

D_MODEL = 1024
D_A = 512
D_B = 512
D_IN = 5632
N_HEADS = 8
HEAD_DIM = 64
CHUNK = 64
N_PREV = 8
SGU_CHUNK = 128
N_GROUPS = 4
N_REL = 257
EPS = 1e-6
NEG_INF = -1e30
SCALE = HEAD_DIM ** -0.5

QB = 2 * CHUNK
KB = (N_PREV + 2) * CHUNK
PADK = N_PREV * CHUNK
ROLL_W = 1024
KEEP = KB // QB - 1
Q_PER_STEP = 2

ADAM_LR = 0.001
ADAM_B1 = 0.9
ADAM_B2 = 0.999
ADAM_EPS = 1e-08
ADAM_WD = 0.01
ADAM_STEP = 10
ADAM_C1 = 1.0 - ADAM_B1 ** ADAM_STEP
ADAM_C2 = 1.0 - ADAM_B2 ** ADAM_STEP

AG_PIECES = 4
N_SHARD = 4
SHARD_IN = D_IN // N_SHARD
MIB = 1024 * 1024


V7X_VMEM_MIB = 64
VMEM_RESERVE_MIB = V7X_VMEM_MIB - 4


def _params(vmem_mib, **kw):
    assert vmem_mib <= VMEM_RESERVE_MIB
    return pltpu.CompilerParams(vmem_limit_bytes=VMEM_RESERVE_MIB * MIB, **kw)


def _sigmoid(x):
    return 1.0 / (1.0 + jnp.exp(-x))


def _silu_and_grad(x):
    s = _sigmoid(x)
    return x * s, s * (1.0 + x * (1.0 - s))


_GELU_C = math.sqrt(2.0 / math.pi)
_GELU_A = 0.044715


def _gelu_and_grad(x):
    x2 = x * x
    t = jnp.tanh(_GELU_C * (x + _GELU_A * (x2 * x)))
    cdf = 0.5 * (1.0 + t)
    grad = cdf + 0.5 * x * (1.0 - t * t) * (_GELU_C * (1.0 + 3.0 * _GELU_A * x2))
    return x * cdf, grad


def _dot(a, b):
    return jnp.dot(a, b, preferred_element_type=F32)


def _dot_nt(a, b):
    return lax.dot_general(a, b, (((1,), (1,)), ((), ())), preferred_element_type=F32)


def _dot_tn(a, b):
    return lax.dot_general(a, b, (((0,), (0,)), ((), ())), preferred_element_type=F32)


def _mo(v, m):
    return v if isinstance(v, int) else pl.multiple_of(v, m)


def _unit_in(ref, s, p):
    return ref.at[pl.ds(_mo(p * 512, 512), 512), pl.ds(_mo(s * SHARD_IN, 128), SHARD_IN)]


def _unit_p(ref, s, p):
    return ref.at[pl.ds(_mo(p * 256, 256), 256), pl.ds(_mo(s * 256, 128), 256)]


def _unit_out(ref, s, p):
    return ref.at[pl.ds(_mo(s * 256 + p * 128, 128), 128), :]


_UNITS = (_unit_in, _unit_p, _unit_p, _unit_out)
_HALF_ROWS = (512, 256, 256, 128)
_UNIT_SHAPES = ((512, SHARD_IN), (256, 256), (256, 256), (128, D_MODEL))
_FULL_SHAPES = ((D_MODEL, D_IN), (D_A, D_MODEL), (D_B, D_MODEL), (D_MODEL, D_MODEL))
_SHARD_SHAPES = ((D_MODEL, SHARD_IN), (D_A, 256), (D_B, 256), (256, D_MODEL))


def _mesh_pos():
    x, y, c = lax.axis_index("x"), lax.axis_index("y"), lax.axis_index("c")
    chips = [(1 - x, y), (x, 1 - y), (1 - x, 1 - y)]
    return x, y, c, chips


def _ag_weights(ws, shards, later_ws, later_shards, gp):
    n, m = len(ws), len(later_ws)

    def body(*refs):
        ins, later_ins, gp_ref = refs[:n], refs[n:n + m], refs[n + m]
        o = n + m + 1
        outs, later_outs, bias_ref = refs[o:o + n], refs[o + n:o + n + m], refs[o + n + m]
        o += n + m + 1
        stage, later_stage = refs[o:o + n], refs[o + n:o + n + m]
        send_sems, recv_sems, local_sems, later_sems = refs[o + n + m:]
        x, y, c, chips = _mesh_pos()
        s_me = 2 * x + y
        sibling = (x, y, 1 - c)
        def rows_of(k, p):
            rows = _HALF_ROWS[ws[k]]
            return pl.ds(_mo(p * rows, rows), rows)

        def half(k, p):
            return stage[k].at[rows_of(k, p), :]

        def unit(k, s, p):
            return _UNITS[ws[k]](outs[k], s, p)

        def rcopy(k, i, src, dst, to):
            return pltpu.make_async_remote_copy(src_ref=src, dst_ref=dst, send_sem=send_sems.at[k, i],
                                                recv_sem=recv_sems.at[k, i], device_id=to, device_id_type=MESH)

        peers_entered = _signal_peers("both")
        for k in range(n):
            stage[k][rows_of(k, c), :] = ins[k][rows_of(k, c), :].astype(BF)
        peers_entered()
        def piece(ref, k, q):
            rows = _HALF_ROWS[ws[k]] // AG_PIECES
            return ref.at[pl.ds(q * rows, rows), :]

        sends = []
        for q in range(AG_PIECES):
            for j, (cx, cy) in enumerate(chips):
                for k in range(n):
                    cp = rcopy(k, j * AG_PIECES + q, piece(half(k, c), k, q), piece(unit(k, s_me, c), k, q),
                               (cx, cy, c))
                    cp.start()
                    sends.append(cp)
        for k in range(n):
            stage[k][rows_of(k, 1 - c), :] = ins[k][rows_of(k, 1 - c), :].astype(BF)
        local = []
        for k in range(n):
            for p in range(2):
                cp = pltpu.make_async_copy(half(k, p), unit(k, s_me, p), local_sems.at[k, p])
                cp.start()
                local.append(cp)
        for k, w in enumerate(later_ws):
            later_stage[k][...] = later_ins[k][...].astype(BF)
            cp = pltpu.make_async_copy(later_stage[k], _shard_of(later_outs[k], w, s_me), later_sems.at[k])
            cp.start()
            local.append(cp)
        keep = _struct_mask()
        for h in range(N_HEADS):
            bias_ref[h] = jnp.where(keep, _skew_table(gp_ref[h:h + 1, :])[:, :KB], NEG_INF)
        for q in range(AG_PIECES):
            for j, (cx, cy) in enumerate(chips):
                for k in range(n):
                    landed = piece(unit(k, 2 * cx + cy, c), k, q)
                    rcopy(k, j * AG_PIECES + q, landed, landed, (cx, cy, c)).wait_recv()
                    cp = rcopy(k, (3 + j) * AG_PIECES + q, landed, landed, sibling)
                    cp.start()
                    sends.append(cp)
        for q in range(AG_PIECES):
            for j, (cx, cy) in enumerate(chips):
                for k in range(n):
                    other = piece(unit(k, 2 * cx + cy, 1 - c), k, q)
                    rcopy(k, (3 + j) * AG_PIECES + q, other, other, sibling).wait_recv()
        for cp in sends:
            cp.wait_send()
        for cp in local:
            cp.wait()

    vm = pl.BlockSpec(memory_space=pltpu.VMEM)
    outs = pl.pallas_call(
        body, name="ag_weights",
        out_shape=tuple(jax.ShapeDtypeStruct(_FULL_SHAPES[w], BF) for w in tuple(ws) + tuple(later_ws))
        + (jax.ShapeDtypeStruct((N_HEADS, QB, KB), F32),),
        in_specs=[vm] * (n + m + 1), out_specs=[_ANY] * (n + m) + [vm],
        scratch_shapes=[pltpu.VMEM(_SHARD_SHAPES[w], BF) for w in tuple(ws) + tuple(later_ws)]
        + [pltpu.SemaphoreType.DMA((n, 6 * AG_PIECES)), pltpu.SemaphoreType.DMA((n, 6 * AG_PIECES)),
           pltpu.SemaphoreType.DMA((n, 2)), pltpu.SemaphoreType.DMA((m,))],
        compiler_params=_params(48, collective_id=_PEER_SETS["both"]),
    )(*shards, *later_shards, gp)
    return list(outs[:n]), list(outs[n:n + m]), outs[-1]


def _shard_of(ref, w, s):
    if w == 0:
        return ref.at[:, pl.ds(_mo(s * SHARD_IN, 128), SHARD_IN)]
    if w == 3:
        return ref.at[pl.ds(_mo(s * 256, 256), 256), :]
    return ref.at[:, pl.ds(_mo(s * 256, 128), 256)]


def _gather_copies(ws):
    def copies(refs, send_sems, recv_sems):
        x, y, c, chips = _mesh_pos()
        out = []
        for j, (cx, cy) in enumerate(chips):
            for k, w in enumerate(ws):
                mine = _shard_of(refs[k], w, 2 * x + y)
                out.append(pltpu.make_async_remote_copy(
                    src_ref=mine, dst_ref=mine, send_sem=send_sems.at[3 * k + j], recv_sem=recv_sems.at[3 * k + j],
                    device_id=(cx, cy, c), device_id_type=MESH))
        return out
    return copies


def _inproj_fwd(x, norm_g, w_in_bf, tm=512, after=()):
    S = x.shape[0]

    def body(x_ref, g_ref, w_ref, ht_ref, q_ref, k_ref, v_ref, zr_ref):
        xv = x_ref[...]
        r = lax.rsqrt(jnp.mean(xv * xv, axis=-1, keepdims=True) + EPS)
        hf = (xv * r) * g_ref[...]
        ht_ref[...] = hf.T.astype(BF)
        h = hf.astype(BF)
        heads = (q_ref, k_ref, v_ref)
        for j in range(D_IN // 512):
            z = _dot(h, w_ref[:, j * 512:(j + 1) * 512])
            if j < 3:
                zb = z.astype(BF)
                for hd in range(N_HEADS):
                    heads[j][hd] = zb[:, hd * HEAD_DIM:(hd + 1) * HEAD_DIM]
            else:
                zr_ref[:, (j - 3) * 512:(j - 2) * 512] = z

    head_major = jax.ShapeDtypeStruct((N_HEADS, S, HEAD_DIM), BF)
    head_spec = pl.BlockSpec((N_HEADS, tm, HEAD_DIM), lambda i: (0, i, 0))
    return pl.pallas_call(
        _after(body, 3, after), name="inproj_fwd", grid=(S // tm,),
        out_shape=(jax.ShapeDtypeStruct((D_MODEL, S), BF), head_major, head_major, head_major,
                   jax.ShapeDtypeStruct((S, D_IN - 3 * D_A), F32)),
        in_specs=[pl.BlockSpec((tm, D_MODEL), lambda i: (i, 0)),
                  pl.BlockSpec((1, D_MODEL), lambda i: (0, 0)),
                  pl.BlockSpec((D_MODEL, D_IN), lambda i: (0, 0), pipeline_mode=pl.Buffered(1))]
        + [_ANY] * len(after),
        out_specs=[pl.BlockSpec((D_MODEL, tm), lambda i: (0, i)),
                   head_spec, head_spec, head_spec,
                   pl.BlockSpec((tm, D_IN - 3 * D_A), lambda i: (i, 0))],
        compiler_params=_params(52, dimension_semantics=("arbitrary",)),
    )(x, norm_g, w_in_bf, *after)


def _skew_table(gp_row):
    row = lax.broadcasted_iota(jnp.int32, (QB, ROLL_W), 0)
    t = jnp.broadcast_to(gp_row, (QB, ROLL_W))
    for b in range(7):
        t = jnp.where(((row >> b) & 1) == 1, pltpu.roll(t, 1 << b, axis=1), t)
    return t


def _unskew_sum(d):
    row = lax.broadcasted_iota(jnp.int32, (QB, ROLL_W), 0)
    for b in range(7):
        d = jnp.where(((row >> b) & 1) == 1, pltpu.roll(d, ROLL_W - (1 << b), axis=1), d)
    return jnp.sum(d, axis=0, keepdims=True)


def _struct_mask():
    a = lax.broadcasted_iota(jnp.int32, (QB, KB), 0) // CHUNK
    b = lax.broadcasted_iota(jnp.int32, (QB, KB), 1) // CHUNK
    return (b >= a) & (b <= a + N_PREV)


def _load_kv(k_hbm, v_hbm, k_scr, v_scr, sems, S, meanwhile=lambda: None):
    zeros = jnp.zeros((N_HEADS, PADK, HEAD_DIM), BF)
    k_scr[:, 0:PADK, :] = zeros
    v_scr[:, 0:PADK, :] = zeros
    ck = pltpu.make_async_copy(k_hbm, k_scr.at[:, pl.ds(PADK, S), :], sems.at[0])
    cv = pltpu.make_async_copy(v_hbm, v_scr.at[:, pl.ds(PADK, S), :], sems.at[1])
    ck.start()
    cv.start()
    meanwhile()
    ck.wait()
    cv.wait()


_BATCH_NT = (((2,), (2,)), ((0,), (0,)))
_BATCH_NN = (((2,), (1,)), ((0,), (0,)))
_BATCH_TN = (((1,), (1,)), ((0,), (0,)))


def _bdot(a, b, dims):
    return lax.dot_general(a, b, dims, preferred_element_type=F32)


def _scaled(q):
    return q * jnp.asarray(SCALE, BF)


def _scores(qs, kb, bias, i, front):
    s = _bdot(qs, kb, _BATCH_NT) + bias
    if front:
        col = lax.broadcasted_iota(jnp.int32, (1, 1, KB), 2)
        s = jnp.where(col >= PADK - i * QB, s, NEG_INF)
    return s


def _attn_fwd(q3, k3, v3, bias):
    S = q3.shape[1]

    def body(q_ref, k_hbm, v_hbm, bias_ref, o_ref, lse_ref, k_scr, v_scr, sems):
        @pl.when(pl.program_id(0) == 0)
        def _():
            _load_kv(k_hbm, v_hbm, k_scr, v_scr, sems, S)

        def step(i, rows, front):
            start = pl.multiple_of(i * QB, QB)
            kb = k_scr[:, pl.ds(start, KB), :]
            vb = v_scr[:, pl.ds(start, KB), :]
            s = _scores(_scaled(q_ref[:, rows, :]), kb, bias_ref[...], i, front)
            m = jnp.max(s, axis=-1, keepdims=True)
            e = jnp.exp(s - m)
            l = jnp.sum(e, axis=-1, keepdims=True)
            p = e * (1.0 / l)
            o = _bdot(p.astype(BF), vb, _BATCH_NN)
            lse_ref[:, rows, :] = jnp.broadcast_to(m + jnp.log(l), (N_HEADS, QB, 128))
            for h in range(N_HEADS):
                o_ref[rows, h * HEAD_DIM:(h + 1) * HEAD_DIM] = o[h]

        def block(j, carry):
            i = pl.program_id(0) * Q_PER_STEP + j
            rows = pl.ds(pl.multiple_of(j * QB, QB), QB)
            pl.when(i < KEEP)(functools.partial(step, i, rows, True))
            pl.when(i >= KEEP)(functools.partial(step, i, rows, False))
            return carry

        lax.fori_loop(0, Q_PER_STEP, block, 0)

    rows_per_step = Q_PER_STEP * QB
    kv_scr = pltpu.VMEM((N_HEADS, S + PADK, HEAD_DIM), BF)
    return pl.pallas_call(
        body, name="attn_fwd", grid=(S // rows_per_step,),
        out_shape=(jax.ShapeDtypeStruct((S, D_A), F32), jax.ShapeDtypeStruct((N_HEADS, S, 128), F32)),
        in_specs=[pl.BlockSpec((N_HEADS, rows_per_step, HEAD_DIM), lambda g: (0, g, 0)),
                  pl.BlockSpec(memory_space=pl.ANY), pl.BlockSpec(memory_space=pl.ANY),
                  pl.BlockSpec((N_HEADS, QB, KB), lambda g: (0, 0, 0))],
        out_specs=[pl.BlockSpec((rows_per_step, D_A), lambda g: (g, 0)),
                   pl.BlockSpec((N_HEADS, rows_per_step, 128), lambda g: (0, g, 0))],
        scratch_shapes=[kv_scr, kv_scr, pltpu.SemaphoreType.DMA((2,))],
        compiler_params=_params(48, dimension_semantics=("arbitrary",)),
    )(q3, k3, v3, bias)


def _attn_bwd(q3, k3, v3, d_att3, lse, bias, after=()):
    S = q3.shape[1]
    nq = S // QB

    def body(q_ref, do_ref, k_hbm, v_hbm, lse_ref, bias_ref, dq_ref, dk_ref, dv_ref, dgp_ref,
             k_scr, v_scr, dk_acc, dv_acc, dbias_acc, pad_scr, sems):
        @pl.when(pl.program_id(0) == 0)
        def _():
            def clear():
                dk_acc[...] = jnp.zeros_like(dk_acc)
                dv_acc[...] = jnp.zeros_like(dv_acc)
                dbias_acc[...] = jnp.zeros_like(dbias_acc)
            _load_kv(k_hbm, v_hbm, k_scr, v_scr, sems, S, clear)

        def step(i, rows, front):
            start = pl.multiple_of(i * QB, QB)
            kb = k_scr[:, pl.ds(start, KB), :]
            vb = v_scr[:, pl.ds(start, KB), :]
            qs = _scaled(q_ref[:, rows, :])
            do = do_ref[:, rows, :]
            p = jnp.exp(_scores(qs, kb, bias_ref[...], i, front) - jnp.tile(lse_ref[:, rows, :], (1, 1, KB // 128)))
            dp = _bdot(do, vb, _BATCH_NT)
            ds = p * (dp - jnp.sum(dp * p, axis=-1, keepdims=True))
            dbias_acc[...] += ds
            dsb = ds.astype(BF)
            dq = _bdot(dsb, kb, _BATCH_NN) * SCALE
            for h in range(N_HEADS):
                dq_ref[rows, h * HEAD_DIM:(h + 1) * HEAD_DIM] = dq[h].astype(BF)
            dk_acc[...] += _bdot(dsb, qs, _BATCH_TN)
            dv_acc[...] += _bdot(p.astype(BF), do, _BATCH_TN)

        def block(j, carry):
            i = pl.program_id(0) * Q_PER_STEP + j
            rows = pl.ds(pl.multiple_of(j * QB, QB), QB)
            pl.when(i < KEEP)(functools.partial(step, i, rows, True))
            pl.when((i >= KEEP) & (i < nq))(functools.partial(step, i, rows, False))
            for h in range(N_HEADS):
                hs = slice(h * HEAD_DIM, (h + 1) * HEAD_DIM)
                dk_ref[rows, hs] = dk_acc[h, 0:QB, :].astype(BF)
                dv_ref[rows, hs] = dv_acc[h, 0:QB, :].astype(BF)
            dk_acc[:, 0:KB - QB, :] = dk_acc[:, QB:KB, :]
            dv_acc[:, 0:KB - QB, :] = dv_acc[:, QB:KB, :]
            dk_acc[:, KB - QB:KB, :] = jnp.zeros((N_HEADS, QB, HEAD_DIM), F32)
            dv_acc[:, KB - QB:KB, :] = jnp.zeros((N_HEADS, QB, HEAD_DIM), F32)
            return carry

        lax.fori_loop(0, Q_PER_STEP, block, 0)

        @pl.when(pl.program_id(0) == n_steps - 1)
        def _():
            lane = lax.broadcasted_iota(jnp.int32, (1, ROLL_W), 1)
            hi = (lane < 384) | (lane >= 832)
            lo = (lane > 640) & (lane < 832)
            pad_scr[...] = jnp.zeros_like(pad_scr)
            for h in range(N_HEADS):
                pad_scr[:, 0:KB] = dbias_acc[h]
                g = _unskew_sum(pad_scr[...])
                s_hi = jnp.sum(jnp.where(hi, g, 0.0), axis=-1, keepdims=True)
                s_lo = jnp.sum(jnp.where(lo, g, 0.0), axis=-1, keepdims=True)
                g = jnp.where(lane == 384, g + s_hi, g)
                g = jnp.where(lane == 640, g + s_lo, g)
                dgp_ref[h:h + 1, :] = g

    assert nq % Q_PER_STEP == 0 and KEEP % Q_PER_STEP == 0
    rows_per_step = Q_PER_STEP * QB
    n_steps = (nq + KEEP) // Q_PER_STEP
    last = nq // Q_PER_STEP - 1
    lag = KEEP // Q_PER_STEP
    kv_scr = pltpu.VMEM((N_HEADS, S + PADK, HEAD_DIM), BF)
    return pl.pallas_call(
        _after(body, 6, after), name="attn_bwd", grid=(n_steps,),
        out_shape=(jax.ShapeDtypeStruct((S, D_A), BF), jax.ShapeDtypeStruct((S, D_A), BF),
                   jax.ShapeDtypeStruct((S, D_A), BF), jax.ShapeDtypeStruct((N_HEADS, ROLL_W), F32)),
        in_specs=[pl.BlockSpec((N_HEADS, rows_per_step, HEAD_DIM), lambda g: (0, jnp.minimum(g, last), 0)),
                  pl.BlockSpec((N_HEADS, rows_per_step, HEAD_DIM), lambda g: (0, jnp.minimum(g, last), 0)),
                  pl.BlockSpec(memory_space=pl.ANY), pl.BlockSpec(memory_space=pl.ANY),
                  pl.BlockSpec((N_HEADS, rows_per_step, 128), lambda g: (0, jnp.minimum(g, last), 0)),
                  pl.BlockSpec((N_HEADS, QB, KB), lambda g: (0, 0, 0))] + [_ANY] * len(after),
        out_specs=[pl.BlockSpec((rows_per_step, D_A), lambda g: (jnp.minimum(g, last), 0)),
                   pl.BlockSpec((rows_per_step, D_A), lambda g: (jnp.maximum(g - lag, 0), 0)),
                   pl.BlockSpec((rows_per_step, D_A), lambda g: (jnp.maximum(g - lag, 0), 0)),
                   pl.BlockSpec((N_HEADS, ROLL_W), lambda g: (0, 0))],
        scratch_shapes=[kv_scr, kv_scr,
                        pltpu.VMEM((N_HEADS, KB, HEAD_DIM), F32), pltpu.VMEM((N_HEADS, KB, HEAD_DIM), F32),
                        pltpu.VMEM((N_HEADS, QB, KB), F32), pltpu.VMEM((QB, ROLL_W), F32),
                        pltpu.SemaphoreType.DMA((2,))],
        compiler_params=_params(56, dimension_semantics=("arbitrary",)),
    )(q3, d_att3, k3, v3, lse, bias, *after)


def _sgu_core(ub, vb, lg, lb):
    u, du = _gelu_and_grad(ub)
    v, dv = _gelu_and_grad(vb)
    mu = jnp.mean(v, axis=-1, keepdims=True)
    vc = v - mu
    rstd = lax.rsqrt(jnp.mean(vc * vc, axis=-1, keepdims=True) + EPS)
    xh = vc * rstd
    vn = xh * lg + lb
    return u, du, dv, rstd, xh, vn


def _tri():
    r = lax.broadcasted_iota(jnp.int32, (SGU_CHUNK, SGU_CHUNK), 0)
    c = lax.broadcasted_iota(jnp.int32, (SGU_CHUNK, SGU_CHUNK), 1)
    return r >= c


def _tail_sgu(att, zrest, x, target, w_pa, w_pb, w_out, b_gate, final_g, ln_g, ln_b, w_s, b_s_t, tm=256):
    S = x.shape[0]
    nt = S // tm
    chunks = tm // SGU_CHUNK

    def body(att_ref, ga_ref, ub_ref, vb_ref, gb_ref, gta_ref, gtb_ref, x_ref, t_ref,
             wpa_ref, wpb_ref, wout_ref, bg_ref, fg_ref, lg_ref, lb_ref, ws_ref, bst_ref,
             dout_ref, datt_ref, dzt_ref, dzs_ref, gwout_hbm, gwpa_hbm, gwpb_hbm,
             gbg_ref, gfg_ref, loss_ref, gws_ref, gbs_ref, glg_ref, glb_ref,
             acc_out, acc_pa, acc_pb, sg_scr, mix_scr, dvn_scr, bs_acc, sems):
        i = pl.program_id(0)

        @pl.when(i == 0)
        def _():
            for r in (acc_out, acc_pa, acc_pb, gbg_ref, gfg_ref, loss_ref, gws_ref, glg_ref, glb_ref, bs_acc):
                r[...] = jnp.zeros_like(r)

        u, du, dv, rstd, xh, vn = _sgu_core(ub_ref[...], vb_ref[...], lg_ref[...], lb_ref[...])
        vnb = vn.astype(BF)
        tri = _tri()
        blocks = [(g, slice(n * SGU_CHUNK, (n + 1) * SGU_CHUNK), slice(g * 128, (g + 1) * 128))
                  for g in range(N_GROUPS) for n in range(chunks)]
        wts = [jnp.where(tri, ws_ref[g], 0.0) for g in range(N_GROUPS)]
        for g, rs, cs in blocks:
            mixed = _dot(wts[g].astype(BF), vnb[rs, cs]) + bst_ref[:, g:g + 1]
            mix_scr[rs, cs] = mixed
            sg_scr[rs, cs] = u[rs, cs] * mixed

        att = att_ref[...]
        sg = sg_scr[...]
        sa, dsa = _silu_and_grad(ga_ref[...])
        sb, dsb = _silu_and_grad(gb_ref[...])
        ya = (att * sa).astype(BF)
        yb = (sg * sb).astype(BF)
        pa = _dot(ya, wpa_ref[...])
        pb = _dot(yb, wpb_ref[...])
        ga = _sigmoid(gta_ref[...] + bg_ref[:, 0:D_MODEL])
        gb = _sigmoid(gtb_ref[...] + bg_ref[:, D_MODEL:2 * D_MODEL])
        merged = (ga * pa + gb * pb).astype(BF)
        out = x_ref[...] + _dot(merged, wout_ref[...])
        r2 = lax.rsqrt(jnp.mean(out * out, axis=-1, keepdims=True) + EPS)
        nrm = out * r2
        fg = fg_ref[...]
        err = nrm * fg - t_ref[...]
        loss_ref[...] += 0.5 * jnp.sum(jnp.mean(err * err, axis=-1, keepdims=True))
        dy = err * (1.0 / D_MODEL)
        gfg_ref[...] += jnp.sum(dy * nrm, axis=0, keepdims=True)
        dn = dy * fg
        d_out = r2 * (dn - nrm * jnp.mean(dn * nrm, axis=-1, keepdims=True))
        dout_ref[...] = d_out
        d_outb = d_out.astype(BF)
        acc_out[...] += _dot_tn(merged, d_outb)
        dm = _dot_nt(d_outb, wout_ref[...])
        d_pa = (dm * ga).astype(BF)
        d_pb = (dm * gb).astype(BF)
        d_gta = dm * pa * (ga * (1.0 - ga))
        d_gtb = dm * pb * (gb * (1.0 - gb))
        gbg_ref[:, 0:D_MODEL] += jnp.sum(d_gta, axis=0, keepdims=True)
        gbg_ref[:, D_MODEL:2 * D_MODEL] += jnp.sum(d_gtb, axis=0, keepdims=True)
        dzt_ref[:, 2 * D_A:2 * D_A + D_MODEL] = d_gta.astype(BF)
        dzt_ref[:, 2 * D_A + D_MODEL:] = d_gtb.astype(BF)
        acc_pa[...] += _dot_tn(ya, d_pa)
        acc_pb[...] += _dot_tn(yb, d_pb)
        d_ya = _dot_nt(d_pa, wpa_ref[...])
        d_yb = _dot_nt(d_pb, wpb_ref[...])
        d_att = (d_ya * sa).astype(BF)
        for hd in range(N_HEADS):
            datt_ref[hd] = d_att[:, hd * HEAD_DIM:(hd + 1) * HEAD_DIM]
        dzt_ref[:, 0:D_A] = (d_ya * att * dsa).astype(BF)
        dzt_ref[:, D_A:2 * D_A] = (d_yb * sg * dsb).astype(BF)

        dsg = d_yb * sb
        dzs_ref[:, 0:D_B] = (dsg * mix_scr[...] * du).astype(BF)
        dmix = dsg * u
        for g, rs, cs in blocks:
            dmb = dmix[rs, cs].astype(BF)
            bs_acc[:, cs] += dmix[rs, cs]
            gws_ref[g] += _dot_nt(dmb, vnb[rs, cs])
            dvn_scr[rs, cs] = _dot(wts[g].T.astype(BF), dmb)
        dvn = dvn_scr[...]
        glg_ref[...] += jnp.sum(dvn * xh, axis=0, keepdims=True)
        glb_ref[...] += jnp.sum(dvn, axis=0, keepdims=True)
        dxh = dvn * lg_ref[...]
        dvv = rstd * (dxh - jnp.mean(dxh, axis=-1, keepdims=True)
                      - xh * jnp.mean(dxh * xh, axis=-1, keepdims=True))
        dzs_ref[:, D_B:2 * D_B] = (dvv * dv).astype(BF)

        @pl.when(i == nt - 1)
        def _():
            cps = [pltpu.make_async_copy(acc_out, gwout_hbm, sems.at[0]),
                   pltpu.make_async_copy(acc_pa, gwpa_hbm, sems.at[1]),
                   pltpu.make_async_copy(acc_pb, gwpb_hbm, sems.at[2])]
            for cp in cps:
                cp.start()
            lane = lax.broadcasted_iota(jnp.int32, (SGU_CHUNK, 128), 1)
            cols = jnp.zeros((SGU_CHUNK, 128), F32)
            for g in range(N_GROUPS):
                gws_ref[g] = jnp.where(tri, gws_ref[g], 0.0)
                col = jnp.sum(bs_acc[:, g * 128:(g + 1) * 128], axis=-1, keepdims=True)
                cols = jnp.where(lane == g, col, cols)
            gbs_ref[...] = cols
            for cp in cps:
                cp.wait()

    c2 = lambda i: (0, 0)
    c3 = lambda i: (0, 0, 0)
    zcol = lambda w, blk: pl.BlockSpec((tm, w), lambda i: (i, blk))
    row = lambda w: pl.BlockSpec((tm, w), lambda i: (i, 0))
    return pl.pallas_call(
        body, name="tail", grid=(nt,),
        out_shape=(jax.ShapeDtypeStruct((S, D_MODEL), F32), jax.ShapeDtypeStruct((N_HEADS, S, HEAD_DIM), BF),
                   jax.ShapeDtypeStruct((S, 3072), BF), jax.ShapeDtypeStruct((S, 2 * D_B), BF),
                   jax.ShapeDtypeStruct((D_MODEL, D_MODEL), F32), jax.ShapeDtypeStruct((D_A, D_MODEL), F32),
                   jax.ShapeDtypeStruct((D_B, D_MODEL), F32),
                   jax.ShapeDtypeStruct((1, 2 * D_MODEL), F32), jax.ShapeDtypeStruct((1, D_MODEL), F32),
                   jax.ShapeDtypeStruct((1, 128), F32),
                   jax.ShapeDtypeStruct((N_GROUPS, 128, 128), F32), jax.ShapeDtypeStruct((SGU_CHUNK, 128), F32),
                   jax.ShapeDtypeStruct((1, D_B), F32), jax.ShapeDtypeStruct((1, D_B), F32)),
        in_specs=[row(D_A), zcol(512, 0), zcol(512, 1), zcol(512, 2), zcol(512, 3),
                  zcol(D_MODEL, 2), zcol(D_MODEL, 3), row(D_MODEL), row(D_MODEL),
                  pl.BlockSpec((D_A, D_MODEL), c2), pl.BlockSpec((D_B, D_MODEL), c2),
                  pl.BlockSpec((D_MODEL, D_MODEL), c2),
                  pl.BlockSpec((1, 2 * D_MODEL), c2), pl.BlockSpec((1, D_MODEL), c2),
                  pl.BlockSpec((1, D_B), c2), pl.BlockSpec((1, D_B), c2),
                  pl.BlockSpec((N_GROUPS, 128, 128), c3), pl.BlockSpec((128, N_GROUPS), c2)],
        out_specs=[row(D_MODEL), pl.BlockSpec((N_HEADS, tm, HEAD_DIM), lambda i: (0, i, 0)),
                   row(3072), row(2 * D_B), _ANY, _ANY, _ANY,
                   pl.BlockSpec((1, 2 * D_MODEL), c2), pl.BlockSpec((1, D_MODEL), c2),
                   pl.BlockSpec((1, 128), c2),
                   pl.BlockSpec((N_GROUPS, 128, 128), c3), pl.BlockSpec((SGU_CHUNK, 128), c2),
                   pl.BlockSpec((1, D_B), c2), pl.BlockSpec((1, D_B), c2)],
        scratch_shapes=[pltpu.VMEM((D_MODEL, D_MODEL), F32), pltpu.VMEM((D_A, D_MODEL), F32),
                        pltpu.VMEM((D_B, D_MODEL), F32),
                        pltpu.VMEM((tm, D_B), F32), pltpu.VMEM((tm, D_B), F32), pltpu.VMEM((tm, D_B), F32),
                        pltpu.VMEM((SGU_CHUNK, D_B), F32), pltpu.SemaphoreType.DMA((3,))],
        compiler_params=_params(58, dimension_semantics=("arbitrary",)),
    )(att, zrest, zrest, zrest, zrest, zrest, zrest, x, target, w_pa, w_pb, w_out, b_gate, final_g,
      ln_g, ln_b, w_s, b_s_t)


_DZ_MAP = ((0, 0), (1, 0), (2, 0), (3, 0), (4, 0), (4, 1), (3, 1), (3, 2), (3, 3), (3, 4), (3, 5))


def _dh_gradx(dq, dk, dv, dzt, dzs, w_in_bf, x, norm_g, d_out, tm=512, after=()):
    S = x.shape[0]

    def body(dq_ref, dk_ref, dv_ref, dzt_ref, dzs_ref, w_ref, x_ref, g_ref, dout_ref, gx_ref, gn_ref):
        i = pl.program_id(0)

        @pl.when(i == 0)
        def _():
            gn_ref[...] = jnp.zeros_like(gn_ref)

        pieces = (dq_ref, dk_ref, dv_ref, dzt_ref, dzs_ref)
        dh = jnp.zeros((tm, D_MODEL), F32)
        for j, (pc, blk) in enumerate(_DZ_MAP):
            dh += _dot_nt(pieces[pc][:, blk * 512:(blk + 1) * 512], w_ref[:, j * 512:(j + 1) * 512])
        xv = x_ref[...]
        r = lax.rsqrt(jnp.mean(xv * xv, axis=-1, keepdims=True) + EPS)
        nrm = xv * r
        gn_ref[...] += jnp.sum(dh * nrm, axis=0, keepdims=True)
        dn = dh * g_ref[...]
        gx_ref[...] = r * (dn - nrm * jnp.mean(dn * nrm, axis=-1, keepdims=True)) + dout_ref[...]

    row = lambda w: pl.BlockSpec((tm, w), lambda i: (i, 0))
    c2 = lambda i: (0, 0)
    return pl.pallas_call(
        _after(body, 9, after), name="dh_gradx", grid=(S // tm,),
        out_shape=(jax.ShapeDtypeStruct((S, D_MODEL), F32), jax.ShapeDtypeStruct((1, D_MODEL), F32)),
        in_specs=[row(512), row(512), row(512), row(3072), row(1024),
                  pl.BlockSpec((D_MODEL, D_IN), c2, pipeline_mode=pl.Buffered(1)), row(D_MODEL),
                  pl.BlockSpec((1, D_MODEL), c2), row(D_MODEL)]
        + [_ANY] * len(after),
        out_specs=[row(D_MODEL), pl.BlockSpec((1, D_MODEL), c2)],
        compiler_params=_params(48, dimension_semantics=("arbitrary",)),
    )(dq, dk, dv, dzt, dzs, w_in_bf, x, norm_g, d_out, *after)


def _gw_in(ht, dq, dk, dv, dzt, dzs, tn=512, after=()):
    S = ht.shape[1]
    per = 512 // tn
    cols = tuple((pc, per * blk + h) for pc, blk in _DZ_MAP for h in range(per))

    def body(ht_ref, dq_ref, dk_ref, dv_ref, dzt_ref, dzs_ref, o_ref, ob_ref):
        j = pl.program_id(0)
        pieces = (dq_ref, dk_ref, dv_ref, dzt_ref, dzs_ref)
        for pc in range(5):
            hit = functools.reduce(jnp.logical_or, [j == jj for jj, (p, _) in enumerate(cols) if p == pc])

            @pl.when(hit)
            def _(pc=pc):
                g = _dot(ht_ref[...], pieces[pc][...])
                o_ref[...] = g
                ob_ref[...] = g.astype(BF)

    def piece_spec(pc):
        cur = next(blk for p, blk in cols if p == pc)
        held = []
        for p, blk in cols:
            cur = blk if p == pc else cur
            held.append(cur)

        def index_map(j):
            blk = jnp.int32(held[0])
            for jj in range(1, len(held)):
                if held[jj] != held[jj - 1]:
                    blk = jnp.where(j >= jj, jnp.int32(held[jj]), blk)
            return (0, blk)

        return pl.BlockSpec((S, tn), index_map)

    return pl.pallas_call(
        _after(body, 6, after), name="gw_in", grid=(len(cols),),
        out_shape=(jax.ShapeDtypeStruct((D_MODEL, D_IN), F32), jax.ShapeDtypeStruct((D_MODEL, D_IN), BF)),
        in_specs=[pl.BlockSpec((D_MODEL, S), lambda j: (0, 0), pipeline_mode=pl.Buffered(1))]
        + [piece_spec(pc) for pc in range(5)]
        + [_ANY] * len(after),
        out_specs=[pl.BlockSpec((D_MODEL, tn), lambda j: (0, j)), pl.BlockSpec((D_MODEL, tn), lambda j: (0, j))],
        compiler_params=_params(56, dimension_semantics=("arbitrary",)),
    )(ht, dq, dk, dv, dzt, dzs, *after)


_HBM = pl.BlockSpec(memory_space=pltpu.HBM)
_SEM = pl.BlockSpec(memory_space=pltpu.SEMAPHORE)
_ANY = pl.BlockSpec(memory_space=pl.ANY)
_EFFECT = pltpu.SideEffectType.DATAFLOW_SIDE_EFFECTING


def _in_hbm(a):
    return pltpu.with_memory_space_constraint(a, pltpu.HBM)


def _after(body, n_in, after):
    if not after:
        return body
    return lambda *refs: body(*refs[:n_in], *refs[n_in + len(after):])


class _Started:
    def __init__(self, send, recv, bufs, token):
        self.send, self.recv, self.bufs, self.token = send, recv, bufs, token


_PEER_SETS = {"sibling": 7, "chips": 8, "both": 9}


def _peers(kind):
    x, y, c, chips = _mesh_pos()
    return ([(x, y, 1 - c)] if kind in ("sibling", "both") else []) + (
        [(cx, cy, c) for cx, cy in chips] if kind in ("chips", "both") else [])


def _signal_peers(kind):
    barrier = pltpu.get_barrier_semaphore()
    targets = _peers(kind)
    for peer in targets:
        pl.semaphore_signal(barrier, inc=1, device_id=peer, device_id_type=MESH)
    return lambda: pl.semaphore_wait(barrier, len(targets))


def _split_start(name, bufs, n_copies, copies, peers, after=()):
    nb = len(bufs)

    def body(*refs):
        _signal_peers(peers)()
        refs = refs[:nb] + refs[nb + len(after):]
        for cp in copies(refs[:nb], refs[nb], refs[nb + 1]):
            cp.start()
        refs[-1][...] = jnp.zeros_like(refs[-1])

    outs = pl.pallas_call(
        body, name=name,
        out_shape=(pltpu.SemaphoreType.DMA((n_copies,)), pltpu.SemaphoreType.DMA((n_copies,)),
                   *[pltpu.HBM(b.shape, b.dtype) for b in bufs], jax.ShapeDtypeStruct((8, 128), F32)),
        in_specs=[_HBM] * nb + [_ANY] * len(after),
        out_specs=(_SEM, _SEM, *[_HBM] * nb, pl.BlockSpec(memory_space=pltpu.VMEM)),
        input_output_aliases={k: 2 + k for k in range(nb)},
        compiler_params=_params(1, has_side_effects=_EFFECT, collective_id=_PEER_SETS[peers]),
    )(*[_in_hbm(b) for b in bufs], *after)
    return _Started(outs[0], outs[1], list(outs[2:2 + nb]), outs[-1])


def _split_wait(name, started, copies, after):
    nb = len(started.bufs)

    def body(*refs):
        for cp in copies(refs[:nb], refs[nb], refs[nb + 1]):
            cp.wait_send()
            cp.wait_recv()

    return list(pl.pallas_call(
        body, name=name,
        out_shape=tuple(pltpu.HBM(b.shape, b.dtype) for b in started.bufs),
        in_specs=[_HBM] * nb + [_SEM, _SEM, _ANY],
        out_specs=tuple([_HBM] * nb),
        input_output_aliases={k: k for k in range(nb)},
        compiler_params=_params(1, has_side_effects=_EFFECT),
    )(*started.bufs, started.send, started.recv, after))


def _x1_copies(ws):
    def copies(refs, send_sems, recv_sems):
        x, y, c, _ = _mesh_pos()
        out = []
        for k, w in enumerate(ws):
            for s in range(N_SHARD):
                out.append(pltpu.make_async_remote_copy(
                    src_ref=_UNITS[w](refs[k], s, 1 - c), dst_ref=refs[len(ws) + k].at[s],
                    send_sem=send_sems.at[N_SHARD * k + s], recv_sem=recv_sems.at[N_SHARD * k + s],
                    device_id=(x, y, 1 - c), device_id_type=MESH))
        return out
    return copies


def _x2_copies(n):
    def copies(refs, send_sems, recv_sems):
        x, y, c, chips = _mesh_pos()
        out = []
        for j, (cx, cy) in enumerate(chips):
            for k in range(n):
                out.append(pltpu.make_async_remote_copy(
                    src_ref=refs[k].at[2 * cx + cy], dst_ref=refs[n + k].at[j],
                    send_sem=send_sems.at[3 * k + j], recv_sem=recv_sems.at[3 * k + j],
                    device_id=(cx, cy, c), device_id_type=MESH))
        return out
    return copies


def _x3_copies(ws):
    def copies(refs, send_sems, recv_sems):
        x, y, c, _ = _mesh_pos()
        out = []
        for k, w in enumerate(ws):
            rows = _HALF_ROWS[w]
            mine = refs[k].at[pl.ds(_mo(c * rows, rows), rows), :]
            out.append(pltpu.make_async_remote_copy(
                src_ref=mine, dst_ref=mine, send_sem=send_sems.at[k], recv_sem=recv_sems.at[k],
                device_id=(x, y, 1 - c), device_id_type=MESH))
        return out
    return copies


def _x1_lands(ws, dtype=F32):
    return [lax.empty((N_SHARD,) + _UNIT_SHAPES[w], dtype) for w in ws]


def _x2_lands(ws):
    return [lax.empty((3,) + _UNIT_SHAPES[w], BF) for w in ws]


def _grad_add1(w, g, recv, pos):
    ur, uc = _UNIT_SHAPES[w]

    def body(pos_ref, g_ref, r_ref, own_ref, csb_ref):
        v = g_ref[...] + r_ref[0].astype(F32)
        csb_ref[0] = v.astype(BF)

        @pl.when(pl.program_id(0) == pos_ref[1])
        def _():
            own_ref[...] = v

    u3 = lambda s, pos: (s, 0, 0)
    return pl.pallas_call(
        body, name=f"grad_add1_{w}",
        grid_spec=pltpu.PrefetchScalarGridSpec(
            num_scalar_prefetch=1, grid=(N_SHARD,),
            in_specs=[pl.BlockSpec((ur, uc), lambda s, pos: (pos[0], s)), pl.BlockSpec((1, ur, uc), u3)],
            out_specs=[pl.BlockSpec((ur, uc), lambda s, pos: (0, 0)), pl.BlockSpec((1, ur, uc), u3)]),
        out_shape=(jax.ShapeDtypeStruct((ur, uc), F32), jax.ShapeDtypeStruct((N_SHARD, ur, uc), BF)),
        compiler_params=_params(40, dimension_semantics=("arbitrary",)),
    )(pos, g, recv)


def _grad_add1_group(ws, gs, recvs):
    n = len(ws)

    def body(*refs):
        c = lax.axis_index("c")
        for k, w in enumerate(ws):
            g, r, cs, csb = refs[k], refs[n + k], refs[2 * n + k], refs[3 * n + k]
            for s in range(N_SHARD):
                v = _UNITS[w](g, s, c)[...] + r[s]
                cs[s] = v
                csb[s] = v.astype(BF)

    vm = pl.BlockSpec(memory_space=pltpu.VMEM)
    outs = pl.pallas_call(
        body, name="grad_add1_group",
        out_shape=tuple(jax.ShapeDtypeStruct((N_SHARD,) + _UNIT_SHAPES[w], dt) for dt in (F32, BF) for w in ws),
        in_specs=[vm] * (2 * n), out_specs=[vm] * (2 * n),
        compiler_params=_params(32),
    )(*gs, *recvs)
    return list(outs[:n]), list(outs[n:])


def _grad_add2_group(ws, css, recvs):
    n = len(ws)

    def body(*refs):
        x, y, c, _ = _mesh_pos()
        for k, w in enumerate(ws):
            cs, r, o = refs[k], refs[n + k], refs[2 * n + k]
            rows = _HALF_ROWS[w]
            total = ((cs[2 * x + y] + r[0].astype(F32)) + r[1].astype(F32)) + r[2].astype(F32)
            o[pl.ds(_mo(c * rows, rows), rows), :] = total

    vm = pl.BlockSpec(memory_space=pltpu.VMEM)
    return list(pl.pallas_call(
        body, name="grad_add2_group",
        out_shape=tuple(jax.ShapeDtypeStruct(_SHARD_SHAPES[w], F32) for w in ws),
        in_specs=[vm] * (2 * n), out_specs=[vm] * n,
        compiler_params=_params(32),
    )(*css, *recvs))


def _grad_add2(w, own, recv, pos):
    ur, uc = _UNIT_SHAPES[w]
    nt = 4
    tr = ur // nt

    def body(pos_ref, own_ref, r_ref, o_ref):
        o_ref[...] = ((own_ref[...] + r_ref[0].astype(F32)) + r_ref[1].astype(F32)) + r_ref[2].astype(F32)

    return pl.pallas_call(
        body, name=f"grad_add2_{w}",
        grid_spec=pltpu.PrefetchScalarGridSpec(
            num_scalar_prefetch=1, grid=(nt,),
            in_specs=[pl.BlockSpec((tr, uc), lambda t, pos: (t, 0)),
                      pl.BlockSpec((3, tr, uc), lambda t, pos: (0, t, 0))],
            out_specs=pl.BlockSpec((tr, uc), lambda t, pos: (pos[0] * nt + t, 0))),
        out_shape=jax.ShapeDtypeStruct(_SHARD_SHAPES[w], F32),
        compiler_params=_params(32, dimension_semantics=("arbitrary",)),
    )(pos, own, recv)


def _adamw_math(w, g, m, v):
    m = ADAM_B1 * m + (1.0 - ADAM_B1) * g
    v = ADAM_B2 * v + (1.0 - ADAM_B2) * (g * g)
    m_hat = m / ADAM_C1
    v_hat = v / ADAM_C2
    delta = -ADAM_LR * (m_hat / (jnp.sqrt(v_hat) + ADAM_EPS) + ADAM_WD * w)
    return delta, m, v


def _adamw_group(ws_, gs, ms, vs, after=()):
    n = len(ws_)

    def body(*refs):
        for k in range(n):
            w, g, m, v = (refs[j * n + k] for j in range(4))
            d, nm, nv, gc = (refs[(4 + j) * n + k] for j in range(4))
            gv = g[...]
            d[...], nm[...], nv[...] = _adamw_math(w[...], gv, m[...], v[...])
            gc[...] = gv

    vm = pl.BlockSpec(memory_space=pltpu.VMEM)
    outs = pl.pallas_call(
        _after(body, 4 * n, after), name="adamw_group",
        out_shape=tuple(jax.ShapeDtypeStruct(a.shape, F32) for _ in range(4) for a in ws_),
        in_specs=[vm] * (4 * n) + [_ANY] * len(after), out_specs=[vm] * (4 * n),
        compiler_params=_params(32),
    )(*ws_, *gs, *ms, *vs, *after)
    return [tuple(outs[j * n + k] for j in range(4)) for k in range(n)]


def _adamw(name, w, g, m, v, tr=256, after=()):
    rows, cols = w.shape

    def body(w_ref, g_ref, m_ref, v_ref, d_ref, nm_ref, nv_ref, gc_ref):
        gv = g_ref[...]
        d_ref[...], nm_ref[...], nv_ref[...] = _adamw_math(w_ref[...], gv, m_ref[...], v_ref[...])
        gc_ref[...] = gv

    spec = pl.BlockSpec((tr, cols), lambda i: (i, 0))
    return pl.pallas_call(
        _after(body, 4, after), name=name, grid=(rows // tr,),
        out_shape=tuple(jax.ShapeDtypeStruct((rows, cols), F32) for _ in range(4)),
        in_specs=[spec] * 4 + [_ANY] * len(after), out_specs=[spec] * 4,
        compiler_params=_params(32, dimension_semantics=("arbitrary",)),
    )(w, g, m, v, *after)


_REL_PAD = 384
_VEC_FIELDS = (("norm_g", 0, D_MODEL), ("b_gate", 1024, 2 * D_MODEL), ("sgu_ln_g", 3072, D_B),
               ("sgu_ln_b", 3584, D_B), ("b_s", 4096, N_GROUPS * 128), ("final_g", 4608, D_MODEL))
_LOSS_OFF = 5632
_REL_OFF = 5760
_NV = _REL_OFF + N_HEADS * _REL_PAD
_N_FIELDS = len(_VEC_FIELDS) + 2


_B_S_FIELD = [f[0] for f in _VEC_FIELDS].index("b_s")


def _assemble_row(dst, fields, transposed_b_s):
    for f, (_, off, n) in enumerate(_VEC_FIELDS):
        if transposed_b_s and f == _B_S_FIELD:
            t = fields[f][...].T
            for g in range(N_GROUPS):
                dst[:, off + 128 * g:off + 128 * (g + 1)] = t[g:g + 1, :]
        else:
            dst[:, off:off + n] = fields[f][...]
    for r in range(N_HEADS):
        dst[:, _REL_OFF + _REL_PAD * r:_REL_OFF + _REL_PAD * (r + 1)] = fields[len(_VEC_FIELDS)][r:r + 1, :]


def _small_reduce(grads, loss_row, after=()):
    n_in = _N_FIELDS + 1

    def body(*refs):
        g_refs, loss_ref = refs[:_N_FIELDS], refs[_N_FIELDS]
        out_v, out_w = refs[n_in:n_in + 2]
        mine_v, mine_w, gath_v, gath_w, send_sems, recv_sems = refs[n_in + 2:]
        x, y, c, chips = _mesh_pos()
        me, sibling = (x, y, c), (x, y, 1 - c)

        peers_entered = _signal_peers("both")
        _assemble_row(mine_v, g_refs, True)
        mine_v[:, _LOSS_OFF:_LOSS_OFF + 128] = loss_ref[...]
        mine_w[...] = g_refs[-1][...].astype(BF)
        peers_entered()
        my_k = 4 * x + 2 * y + c
        gath_v[my_k] = mine_v[...]
        gath_w[my_k] = mine_w[...]

        def copy(k, gath, block, to, src=None):
            dst = gath.at[4 * block[0] + 2 * block[1] + block[2]]
            return pltpu.make_async_remote_copy(
                src_ref=dst if src is None else src, dst_ref=dst,
                send_sem=send_sems.at[k], recv_sem=recv_sems.at[k], device_id=to, device_id_type=MESH)

        bufs = ((gath_v, mine_v), (gath_w, mine_w))
        first, passed = [], []
        for b, (gath, mine) in enumerate(bufs):
            first.append(copy(7 * b, gath, me, sibling, src=mine))
            first += [copy(7 * b + 1 + j, gath, me, (*chip, c), src=mine) for j, chip in enumerate(chips)]
        for cp in first:
            cp.start()
        for b, (gath, _) in enumerate(bufs):
            for j, chip in enumerate(chips):
                copy(7 * b + 1 + j, gath, (*chip, c), me).wait_recv()
                cp = copy(7 * b + 4 + j, gath, (*chip, c), sibling)
                cp.start()
                passed.append(cp)
        for b, (gath, _) in enumerate(bufs):
            copy(7 * b, gath, sibling, me).wait_recv()
            for j, chip in enumerate(chips):
                copy(7 * b + 4 + j, gath, (*chip, 1 - c), me).wait_recv()
        for cp in first + passed:
            cp.wait_send()

        tot_v, tot_w = gath_v[0], gath_w[0].astype(F32)
        for k in range(1, 8):
            tot_v = tot_v + gath_v[k]
            tot_w = tot_w + gath_w[k].astype(F32)
        out_v[...] = tot_v
        out_w[...] = tot_w

    vm = pl.BlockSpec(memory_space=pltpu.VMEM)
    return pl.pallas_call(
        _after(body, n_in, after), name="small_reduce",
        out_shape=(jax.ShapeDtypeStruct((1, _NV), F32), jax.ShapeDtypeStruct((N_GROUPS * 128, 128), F32)),
        in_specs=[vm] * n_in + [_ANY] * len(after), out_specs=[vm] * 2,
        scratch_shapes=[pltpu.VMEM((1, _NV), F32), pltpu.VMEM((N_GROUPS * 128, 128), BF),
                        pltpu.VMEM((8, 1, _NV), F32), pltpu.VMEM((8, N_GROUPS * 128, 128), BF),
                        pltpu.SemaphoreType.DMA((14,)), pltpu.SemaphoreType.DMA((14,))],
        compiler_params=_params(32, collective_id=_PEER_SETS["both"]),
    )(*grads, loss_row, *after)


def _small_adamw(tot_v, tot_w, params):
    n_in = 2 + 3 * _N_FIELDS

    def body(*refs):
        tv_ref, tw_ref = refs[:2]
        p_refs = [refs[2 + k * _N_FIELDS:2 + (k + 1) * _N_FIELDS] for k in range(3)]
        outs = refs[n_in:n_in + 4 * _N_FIELDS + 1]
        wmv = refs[-1]
        for k in range(3):
            _assemble_row(wmv.at[k], p_refs[k], False)
            wmv[k, :, _LOSS_OFF:_LOSS_OFF + 128] = jnp.zeros((1, 128), F32)
        tot_v, tot_w = tv_ref[...], tw_ref[...]
        res_v = (tot_v,) + _adamw_math(wmv[0], tot_v, wmv[1], wmv[2])
        res_w = (tot_w,) + _adamw_math(p_refs[0][-1][...], tot_w, p_refs[1][-1][...], p_refs[2][-1][...])
        for kind in range(4):
            o = outs[kind * _N_FIELDS:(kind + 1) * _N_FIELDS]
            for f, (_, off, n) in enumerate(_VEC_FIELDS):
                o[f][...] = res_v[kind][:, off:off + n]
            for r in range(N_HEADS):
                o[len(_VEC_FIELDS)][r:r + 1, :] = res_v[kind][:, _REL_OFF + _REL_PAD * r:_REL_OFF + _REL_PAD * (r + 1)]
            o[-1][...] = res_w[kind]
        outs[-1][...] = tot_v[:, _LOSS_OFF:_LOSS_OFF + 128]

    field_shapes = [(1, n) for _, _, n in _VEC_FIELDS] + [(N_HEADS, _REL_PAD), (N_GROUPS * 128, 128)]
    vm = pl.BlockSpec(memory_space=pltpu.VMEM)
    operands = [tot_v, tot_w] + [a for p in params for a in p]
    assert len(operands) == n_in
    outs = pl.pallas_call(
        body, name="small_adamw",
        out_shape=tuple(jax.ShapeDtypeStruct(s, F32) for _ in range(4) for s in field_shapes)
        + (jax.ShapeDtypeStruct((1, 128), F32),),
        in_specs=[vm] * n_in, out_specs=[vm] * (4 * _N_FIELDS + 1),
        scratch_shapes=[pltpu.VMEM((3, 1, _NV), F32)],
        compiler_params=_params(32),
    )(*operands)
    return [outs[k * _N_FIELDS:(k + 1) * _N_FIELDS] for k in range(4)], outs[-1]


def _small_fields(norm_g, b_gate, ln_g, ln_b, b_s, final_g, rel_bias, w_s):
    rel = jnp.pad(rel_bias.reshape(N_HEADS, N_REL), ((0, 0), (0, _REL_PAD - N_REL)))
    return (norm_g, b_gate, ln_g, ln_b, b_s.reshape(1, N_GROUPS * 128), final_g.reshape(1, D_MODEL),
            rel, w_s.reshape(N_GROUPS * 128, 128))


def _small_outputs(fields):
    n_g, b_g, l_g, l_b, b_s, f_g, rel, w_s = fields
    return (n_g, b_g, rel[:, :N_REL].reshape(1, N_HEADS, N_REL), l_g, l_b,
            w_s.reshape(1, N_GROUPS, 128, 128), b_s.reshape(1, N_GROUPS, 128), f_g.reshape(D_MODEL))


def _bias_row(rel_bias):
    hi = rel_bias[:, N_REL - 1:N_REL]
    lo = rel_bias[:, 0:1]
    return jnp.concatenate([jnp.broadcast_to(hi, (N_HEADS, 384)), rel_bias[:, ::-1],
                            jnp.broadcast_to(lo, (N_HEADS, 191)), jnp.broadcast_to(hi, (N_HEADS, 192))], axis=1)


def kernel(x, norm_g, w_in, b_gate, rel_bias, sgu_ln_g, sgu_ln_b, w_s, b_s, w_pa, w_pb, w_out, final_g, loss_target, m_norm_g, m_w_in, m_b_gate, m_rel_bias, m_sgu_ln_g, m_sgu_ln_b, m_w_s, m_b_s, m_w_pa, m_w_pb, m_w_out, m_final_g, v_norm_g, v_w_in, v_b_gate, v_rel_bias, v_sgu_ln_g, v_sgu_ln_b, v_w_s, v_b_s, v_w_pa, v_w_pb, v_w_out, v_final_g):
    S = x.shape[1]
    xs = x.reshape(S, D_MODEL)
    tgt = loss_target.reshape(S, D_MODEL)
    big_w = (w_in[0], w_pa[0], w_pb[0], w_out[0])
    big_m = (m_w_in[0], m_w_pa[0], m_w_pb[0], m_w_out[0])
    big_v = (v_w_in[0], v_w_pa[0], v_w_pb[0], v_w_out[0])
    rel = rel_bias[0]
    ws = w_s[0]
    bst = b_s[0].T
    fg = final_g.reshape(1, D_MODEL)
    pos = jnp.stack([lax.axis_index("c"), 2 * lax.axis_index("x") + lax.axis_index("y")]).astype(jnp.int32)

    (w_in_bf,), staged, band_bias = _ag_weights((0,), big_w[:1], (1, 2, 3), big_w[1:], _bias_row(rel))
    ag_s = _split_start("ag_small_start", staged, 9, _gather_copies((1, 2, 3)), "chips", after=(w_in_bf,))

    ht, q3, k3, v3, zrest = _inproj_fwd(xs, norm_g, w_in_bf, after=(ag_s.token,))
    att, lse = _attn_fwd(q3, k3, v3, band_bias)
    w_pa_bf, w_pb_bf, w_out_bf = _split_wait("ag_small_wait", ag_s, _gather_copies((1, 2, 3)), att)
    (d_out, d_att, dzt, dzs, gw_out, gw_pa, gw_pb, g_bgate, g_final, loss_row,
     g_ws, g_bs_t, g_lng, g_lnb) = _tail_sgu(
        att, zrest, xs, tgt, w_pa_bf, w_pb_bf, w_out_bf, b_gate, fg, sgu_ln_g, sgu_ln_b, ws, bst)
    ws_s, ws_i = (1, 2, 3), (0,)

    x1s = _split_start("gx1s_start", [gw_pa, gw_pb, gw_out] + _x1_lands(ws_s), 12, _x1_copies(ws_s), "sibling")
    dq, dk, dv, d_gp = _attn_bwd(q3, k3, v3, d_att, lse, band_bias, after=(x1s.token,))
    got = _split_wait("gx1s_wait", x1s, _x1_copies(ws_s), dq)
    cs_s, csb_s = _grad_add1_group(ws_s, got[:3], got[3:])

    x2s = _split_start("gx2s_start", csb_s + _x2_lands(ws_s), 9, _x2_copies(3), "chips")
    gw_in, gw_in_bf = _gw_in(ht, dq, dk, dv, dzt, dzs, after=(x2s.token,))
    x1i = _split_start("gx1i_start", [gw_in_bf] + _x1_lands(ws_i, BF), 4, _x1_copies(ws_i), "sibling")
    got = _split_wait("gx2s_wait", x2s, _x2_copies(3), x1i.token)
    halves_s = _grad_add2_group(ws_s, cs_s, got[3:])
    x3s = _split_start("gx3s_start", halves_s, 3, _x3_copies(ws_s), "sibling")
    got = _split_wait("gx1i_wait", x1i, _x1_copies(ws_i), x3s.token)
    sum_i = _grad_add1(0, gw_in, got[1], pos)

    x2i = _split_start("gx2i_start", [sum_i[1]] + _x2_lands(ws_i), 3, _x2_copies(1), "chips")
    grad_x, g_norm = _dh_gradx(dq, dk, dv, dzt, dzs, w_in_bf, xs, norm_g, d_out, after=(x2i.token,))
    g_shards_s = _split_wait("gx3s_wait", x3s, _x3_copies(ws_s), grad_x)
    got = _split_wait("gx2i_wait", x2i, _x2_copies(1), grad_x)
    half_i = _grad_add2(0, sum_i[0], got[1], pos)
    x3i = _split_start("gx3i_start", [half_i], 1, _x3_copies(ws_i), "sibling")
    big = [None] * 4
    big[1:] = _adamw_group(big_w[1:], g_shards_s, big_m[1:], big_v[1:], after=(x3i.token,))

    g_rel = jnp.pad(d_gp[:, 384:384 + N_REL][:, ::-1], ((0, 0), (0, _REL_PAD - N_REL)))
    small_grads = (g_norm, g_bgate, g_lng, g_lnb, g_bs_t, g_final, g_rel, g_ws.reshape(N_GROUPS * 128, 128))
    small_params = (_small_fields(norm_g, b_gate, sgu_ln_g, sgu_ln_b, b_s, final_g, rel_bias, w_s),
                    _small_fields(m_norm_g, m_b_gate, m_sgu_ln_g, m_sgu_ln_b, m_b_s, m_final_g, m_rel_bias, m_w_s),
                    _small_fields(v_norm_g, v_b_gate, v_sgu_ln_g, v_sgu_ln_b, v_b_s, v_final_g, v_rel_bias, v_w_s))
    tot_v, tot_w = _small_reduce(small_grads, loss_row, after=(x3i.token,))
    (gsum, sdelta, sm, sv), loss_out = _small_adamw(tot_v, tot_w, small_params)

    g_shard_i, = _split_wait("gx3i_wait", x3i, _x3_copies(ws_i), loss_out)
    big[0] = _adamw("adamw_w_in", big_w[0], g_shard_i, big_m[0], big_v[0])
    sg_out, sd_out, sm_out, sv_out = (_small_outputs(f) for f in (gsum, sdelta, sm, sv))
    loss = loss_out[0, 0]

    def assemble(small, bigs):
        n_g, b_g, r_b, l_g, l_b, w_s_, b_s_, f_g = small
        b_in, b_pa, b_pb, b_out = (b[None] for b in bigs)
        return (n_g, b_in, b_g, r_b, l_g, l_b, w_s_, b_s_, b_pa, b_pb, b_out, f_g)

    grads_out = assemble(sg_out, [b[3] for b in big])
    delta_out = assemble(sd_out, [b[0] for b in big])
    m_out = assemble(sm_out, [b[1] for b in big])
    v_out = assemble(sv_out, [b[2] for b in big])
    return (loss, grad_x.reshape(1, S, D_MODEL), *grads_out, *delta_out, *m_out, *v_out)
```

```python
import functools
import math

import jax
import jax.numpy as jnp
from jax import lax
from jax.experimental import pallas as pl
from jax.experimental.pallas import tpu as pltpu

F32 = jnp.float32
BF = jnp.bfloat16
MESH = pl.DeviceIdType.MESH

D_MODEL = 1024
D_A = 512
D_B = 512
D_IN = 5632
N_HEADS = 8
HEAD_DIM = 64
CHUNK = 64
N_PREV = 8
SGU_CHUNK = 128
N_GROUPS = 4
N_REL = 257
EPS = 1e-6
NEG_INF = -1e30
SCALE = HEAD_DIM ** -0.5

QB = 2 * CHUNK
KB = (N_PREV + 2) * CHUNK
PADK = N_PREV * CHUNK
ROLL_W = 1024
KEEP = KB // QB - 1
Q_PER_STEP = 2

ADAM_LR = 0.001
ADAM_B1 = 0.9
ADAM_B2 = 0.999
ADAM_EPS = 1e-08
ADAM_WD = 0.01
ADAM_STEP = 10
ADAM_C1 = 1.0 - ADAM_B1 ** ADAM_STEP
ADAM_C2 = 1.0 - ADAM_B2 ** ADAM_STEP

AG_PIECES = 4
N_SHARD = 4
SHARD_IN = D_IN // N_SHARD
MIB = 1024 * 1024


V7X_VMEM_MIB = 64
VMEM_RESERVE_MIB = V7X_VMEM_MIB - 4


def _params(vmem_mib, **kw):
    assert vmem_mib <= VMEM_RESERVE_MIB
    return pltpu.CompilerParams(vmem_limit_bytes=VMEM_RESERVE_MIB * MIB, **kw)


def _sigmoid(x):
    return 1.0 / (1.0 + jnp.exp(-x))


def _silu_and_grad(x):
    s = _sigmoid(x)
    return x * s, s * (1.0 + x * (1.0 - s))


_GELU_C = math.sqrt(2.0 / math.pi)
_GELU_A = 0.044715


def _gelu_and_grad(x):
    x2 = x * x
    t = jnp.tanh(_GELU_C * (x + _GELU_A * (x2 * x)))
    cdf = 0.5 * (1.0 + t)
    grad = cdf + 0.5 * x * (1.0 - t * t) * (_GELU_C * (1.0 + 3.0 * _GELU_A * x2))
    return x * cdf, grad


def _dot(a, b):
    return jnp.dot(a, b, preferred_element_type=F32)


def _dot_nt(a, b):
    return lax.dot_general(a, b, (((1,), (1,)), ((), ())), preferred_element_type=F32)


def _dot_tn(a, b):
    return lax.dot_general(a, b, (((0,), (0,)), ((), ())), preferred_element_type=F32)


def _mo(v, m):
    return v if isinstance(v, int) else pl.multiple_of(v, m)


def _unit_in(ref, s, p):
    return ref.at[pl.ds(_mo(p * 512, 512), 512), pl.ds(_mo(s * SHARD_IN, 128), SHARD_IN)]


def _unit_p(ref, s, p):
    return ref.at[pl.ds(_mo(p * 256, 256), 256), pl.ds(_mo(s * 256, 128), 256)]


def _unit_out(ref, s, p):
    return ref.at[pl.ds(_mo(s * 256 + p * 128, 128), 128), :]


_UNITS = (_unit_in, _unit_p, _unit_p, _unit_out)
_HALF_ROWS = (512, 256, 256, 128)
_UNIT_SHAPES = ((512, SHARD_IN), (256, 256), (256, 256), (128, D_MODEL))
_FULL_SHAPES = ((D_MODEL, D_IN), (D_A, D_MODEL), (D_B, D_MODEL), (D_MODEL, D_MODEL))
_SHARD_SHAPES = ((D_MODEL, SHARD_IN), (D_A, 256), (D_B, 256), (256, D_MODEL))


def _mesh_pos():
    x, y, c = lax.axis_index("x"), lax.axis_index("y"), lax.axis_index("c")
    chips = [(1 - x, y), (x, 1 - y), (1 - x, 1 - y)]
    return x, y, c, chips


def _ag_weights(ws, shards, later_ws, later_shards, gp):
    n, m = len(ws), len(later_ws)

    def body(*refs):
        ins, later_ins, gp_ref = refs[:n], refs[n:n + m], refs[n + m]
        o = n + m + 1
        outs, later_outs, bias_ref = refs[o:o + n], refs[o + n:o + n + m], refs[o + n + m]
        o += n + m + 1
        stage, later_stage = refs[o:o + n], refs[o + n:o + n + m]
        send_sems, recv_sems, local_sems, later_sems = refs[o + n + m:]
        x, y, c, chips = _mesh_pos()
        s_me = 2 * x + y
        sibling = (x, y, 1 - c)
        def rows_of(k, p):
            rows = _HALF_ROWS[ws[k]]
            return pl.ds(_mo(p * rows, rows), rows)

        def half(k, p):
            return stage[k].at[rows_of(k, p), :]

        def unit(k, s, p):
            return _UNITS[ws[k]](outs[k], s, p)

        def rcopy(k, i, src, dst, to):
            return pltpu.make_async_remote_copy(src_ref=src, dst_ref=dst, send_sem=send_sems.at[k, i],
                                                recv_sem=recv_sems.at[k, i], device_id=to, device_id_type=MESH)

        peers_entered = _signal_peers("both")
        for k in range(n):
            stage[k][rows_of(k, c), :] = ins[k][rows_of(k, c), :].astype(BF)
        peers_entered()
        def piece(ref, k, q):
            rows = _HALF_ROWS[ws[k]] // AG_PIECES
            return ref.at[pl.ds(q * rows, rows), :]

        sends = []
        for q in range(AG_PIECES):
            for j, (cx, cy) in enumerate(chips):
                for k in range(n):
                    cp = rcopy(k, j * AG_PIECES + q, piece(half(k, c), k, q), piece(unit(k, s_me, c), k, q),
                               (cx, cy, c))
                    cp.start()
                    sends.append(cp)
        for k in range(n):
            stage[k][rows_of(k, 1 - c), :] = ins[k][rows_of(k, 1 - c), :].astype(BF)
        local = []
        for k in range(n):
            for p in range(2):
                cp = pltpu.make_async_copy(half(k, p), unit(k, s_me, p), local_sems.at[k, p])
                cp.start()
                local.append(cp)
        for k, w in enumerate(later_ws):
            later_stage[k][...] = later_ins[k][...].astype(BF)
            cp = pltpu.make_async_copy(later_stage[k], _shard_of(later_outs[k], w, s_me), later_sems.at[k])
            cp.start()
            local.append(cp)
        keep = _struct_mask()
        for h in range(N_HEADS):
            bias_ref[h] = jnp.where(keep, _skew_table(gp_ref[h:h + 1, :])[:, :KB], NEG_INF)
        for q in range(AG_PIECES):
            for j, (cx, cy) in enumerate(chips):
                for k in range(n):
                    landed = piece(unit(k, 2 * cx + cy, c), k, q)
                    rcopy(k, j * AG_PIECES + q, landed, landed, (cx, cy, c)).wait_recv()
                    cp = rcopy(k, (3 + j) * AG_PIECES + q, landed, landed, sibling)
                    cp.start()
                    sends.append(cp)
        for q in range(AG_PIECES):
            for j, (cx, cy) in enumerate(chips):
                for k in range(n):
                    other = piece(unit(k, 2 * cx + cy, 1 - c), k, q)
                    rcopy(k, (3 + j) * AG_PIECES + q, other, other, sibling).wait_recv()
        for cp in sends:
            cp.wait_send()
        for cp in local:
            cp.wait()

    vm = pl.BlockSpec(memory_space=pltpu.VMEM)
    outs = pl.pallas_call(
        body, name="ag_weights",
        out_shape=tuple(jax.ShapeDtypeStruct(_FULL_SHAPES[w], BF) for w in tuple(ws) + tuple(later_ws))
        + (jax.ShapeDtypeStruct((N_HEADS, QB, KB), F32),),
        in_specs=[vm] * (n + m + 1), out_specs=[_ANY] * (n + m) + [vm],
        scratch_shapes=[pltpu.VMEM(_SHARD_SHAPES[w], BF) for w in tuple(ws) + tuple(later_ws)]
        + [pltpu.SemaphoreType.DMA((n, 6 * AG_PIECES)), pltpu.SemaphoreType.DMA((n, 6 * AG_PIECES)),
           pltpu.SemaphoreType.DMA((n, 2)), pltpu.SemaphoreType.DMA((m,))],
        compiler_params=_params(48, collective_id=_PEER_SETS["both"]),
    )(*shards, *later_shards, gp)
    return list(outs[:n]), list(outs[n:n + m]), outs[-1]


def _shard_of(ref, w, s):
    if w == 0:
        return ref.at[:, pl.ds(_mo(s * SHARD_IN, 128), SHARD_IN)]
    if w == 3:
        return ref.at[pl.ds(_mo(s * 256, 256), 256), :]
    return ref.at[:, pl.ds(_mo(s * 256, 128), 256)]


def _gather_copies(ws):
    def copies(refs, send_sems, recv_sems):
        x, y, c, chips = _mesh_pos()
        out = []
        for j, (cx, cy) in enumerate(chips):
            for k, w in enumerate(ws):
                mine = _shard_of(refs[k], w, 2 * x + y)
                out.append(pltpu.make_async_remote_copy(
                    src_ref=mine, dst_ref=mine, send_sem=send_sems.at[3 * k + j], recv_sem=recv_sems.at[3 * k + j],
                    device_id=(cx, cy, c), device_id_type=MESH))
        return out
    return copies


def _inproj_fwd(x, norm_g, w_in_bf, tm=512, after=()):
    S = x.shape[0]

    def body(x_ref, g_ref, w_ref, ht_ref, q_ref, k_ref, v_ref, zr_ref):
        xv = x_ref[...]
        r = lax.rsqrt(jnp.mean(xv * xv, axis=-1, keepdims=True) + EPS)
        hf = (xv * r) * g_ref[...]
        ht_ref[...] = hf.T.astype(BF)
        h = hf.astype(BF)
        heads = (q_ref, k_ref, v_ref)
        for j in range(D_IN // 512):
            z = _dot(h, w_ref[:, j * 512:(j + 1) * 512])
            if j < 3:
                zb = z.astype(BF)
                for hd in range(N_HEADS):
                    heads[j][hd] = zb[:, hd * HEAD_DIM:(hd + 1) * HEAD_DIM]
            else:
                zr_ref[:, (j - 3) * 512:(j - 2) * 512] = z

    head_major = jax.ShapeDtypeStruct((N_HEADS, S, HEAD_DIM), BF)
    head_spec = pl.BlockSpec((N_HEADS, tm, HEAD_DIM), lambda i: (0, i, 0))
    return pl.pallas_call(
        _after(body, 3, after), name="inproj_fwd", grid=(S // tm,),
        out_shape=(jax.ShapeDtypeStruct((D_MODEL, S), BF), head_major, head_major, head_major,
                   jax.ShapeDtypeStruct((S, D_IN - 3 * D_A), F32)),
        in_specs=[pl.BlockSpec((tm, D_MODEL), lambda i: (i, 0)),
                  pl.BlockSpec((1, D_MODEL), lambda i: (0, 0)),
                  pl.BlockSpec((D_MODEL, D_IN), lambda i: (0, 0), pipeline_mode=pl.Buffered(1))]
        + [_ANY] * len(after),
        out_specs=[pl.BlockSpec((D_MODEL, tm), lambda i: (0, i)),
                   head_spec, head_spec, head_spec,
                   pl.BlockSpec((tm, D_IN - 3 * D_A), lambda i: (i, 0))],
        compiler_params=_params(52, dimension_semantics=("arbitrary",)),
    )(x, norm_g, w_in_bf, *after)


def _skew_table(gp_row):
    row = lax.broadcasted_iota(jnp.int32, (QB, ROLL_W), 0)
    t = jnp.broadcast_to(gp_row, (QB, ROLL_W))
    for b in range(7):
        t = jnp.where(((row >> b) & 1) == 1, pltpu.roll(t, 1 << b, axis=1), t)
    return t


def _unskew_sum(d):
    row = lax.broadcasted_iota(jnp.int32, (QB, ROLL_W), 0)
    for b in range(7):
        d = jnp.where(((row >> b) & 1) == 1, pltpu.roll(d, ROLL_W - (1 << b), axis=1), d)
    return jnp.sum(d, axis=0, keepdims=True)


def _struct_mask():
    a = lax.broadcasted_iota(jnp.int32, (QB, KB), 0) // CHUNK
    b = lax.broadcasted_iota(jnp.int32, (QB, KB), 1) // CHUNK
    return (b >= a) & (b <= a + N_PREV)


def _load_kv(k_hbm, v_hbm, k_scr, v_scr, sems, S, meanwhile=lambda: None):
    zeros = jnp.zeros((N_HEADS, PADK, HEAD_DIM), BF)
    k_scr[:, 0:PADK, :] = zeros
    v_scr[:, 0:PADK, :] = zeros
    ck = pltpu.make_async_copy(k_hbm, k_scr.at[:, pl.ds(PADK, S), :], sems.at[0])
    cv = pltpu.make_async_copy(v_hbm, v_scr.at[:, pl.ds(PADK, S), :], sems.at[1])
    ck.start()
    cv.start()
    meanwhile()
    ck.wait()
    cv.wait()


_BATCH_NT = (((2,), (2,)), ((0,), (0,)))
_BATCH_NN = (((2,), (1,)), ((0,), (0,)))
_BATCH_TN = (((1,), (1,)), ((0,), (0,)))


def _bdot(a, b, dims):
    return lax.dot_general(a, b, dims, preferred_element_type=F32)


def _scaled(q):
    return q * jnp.asarray(SCALE, BF)


def _scores(qs, kb, bias, i, front):
    s = _bdot(qs, kb, _BATCH_NT) + bias
    if front:
        col = lax.broadcasted_iota(jnp.int32, (1, 1, KB), 2)
        s = jnp.where(col >= PADK - i * QB, s, NEG_INF)
    return s


def _attn_fwd(q3, k3, v3, bias):
    S = q3.shape[1]

    def body(q_ref, k_hbm, v_hbm, bias_ref, o_ref, lse_ref, k_scr, v_scr, sems):
        @pl.when(pl.program_id(0) == 0)
        def _():
            _load_kv(k_hbm, v_hbm, k_scr, v_scr, sems, S)

        def step(i, rows, front):
            start = pl.multiple_of(i * QB, QB)
            kb = k_scr[:, pl.ds(start, KB), :]
            vb = v_scr[:, pl.ds(start, KB), :]
            s = _scores(_scaled(q_ref[:, rows, :]), kb, bias_ref[...], i, front)
            m = jnp.max(s, axis=-1, keepdims=True)
            e = jnp.exp(s - m)
            l = jnp.sum(e, axis=-1, keepdims=True)
            p = e * (1.0 / l)
            o = _bdot(p.astype(BF), vb, _BATCH_NN)
            lse_ref[:, rows, :] = jnp.broadcast_to(m + jnp.log(l), (N_HEADS, QB, 128))
            for h in range(N_HEADS):
                o_ref[rows, h * HEAD_DIM:(h + 1) * HEAD_DIM] = o[h]

        def block(j, carry):
            i = pl.program_id(0) * Q_PER_STEP + j
            rows = pl.ds(pl.multiple_of(j * QB, QB), QB)
            pl.when(i < KEEP)(functools.partial(step, i, rows, True))
            pl.when(i >= KEEP)(functools.partial(step, i, rows, False))
            return carry

        lax.fori_loop(0, Q_PER_STEP, block, 0)

    rows_per_step = Q_PER_STEP * QB
    kv_scr = pltpu.VMEM((N_HEADS, S + PADK, HEAD_DIM), BF)
    return pl.pallas_call(
        body, name="attn_fwd", grid=(S // rows_per_step,),
        out_shape=(jax.ShapeDtypeStruct((S, D_A), F32), jax.ShapeDtypeStruct((N_HEADS, S, 128), F32)),
        in_specs=[pl.BlockSpec((N_HEADS, rows_per_step, HEAD_DIM), lambda g: (0, g, 0)),
                  pl.BlockSpec(memory_space=pl.ANY), pl.BlockSpec(memory_space=pl.ANY),
                  pl.BlockSpec((N_HEADS, QB, KB), lambda g: (0, 0, 0))],
        out_specs=[pl.BlockSpec((rows_per_step, D_A), lambda g: (g, 0)),
                   pl.BlockSpec((N_HEADS, rows_per_step, 128), lambda g: (0, g, 0))],
        scratch_shapes=[kv_scr, kv_scr, pltpu.SemaphoreType.DMA((2,))],
        compiler_params=_params(48, dimension_semantics=("arbitrary",)),
    )(q3, k3, v3, bias)


def _attn_bwd(q3, k3, v3, d_att3, lse, bias, after=()):
    S = q3.shape[1]
    nq = S // QB

    def body(q_ref, do_ref, k_hbm, v_hbm, lse_ref, bias_ref, dq_ref, dk_ref, dv_ref, dgp_ref,
             k_scr, v_scr, dk_acc, dv_acc, dbias_acc, pad_scr, sems):
        @pl.when(pl.program_id(0) == 0)
        def _():
            def clear():
                dk_acc[...] = jnp.zeros_like(dk_acc)
                dv_acc[...] = jnp.zeros_like(dv_acc)
                dbias_acc[...] = jnp.zeros_like(dbias_acc)
            _load_kv(k_hbm, v_hbm, k_scr, v_scr, sems, S, clear)

        def step(i, rows, front):
            start = pl.multiple_of(i * QB, QB)
            kb = k_scr[:, pl.ds(start, KB), :]
            vb = v_scr[:, pl.ds(start, KB), :]
            qs = _scaled(q_ref[:, rows, :])
            do = do_ref[:, rows, :]
            p = jnp.exp(_scores(qs, kb, bias_ref[...], i, front) - jnp.tile(lse_ref[:, rows, :], (1, 1, KB // 128)))
            dp = _bdot(do, vb, _BATCH_NT)
            ds = p * (dp - jnp.sum(dp * p, axis=-1, keepdims=True))
            dbias_acc[...] += ds
            dsb = ds.astype(BF)
            dq = _bdot(dsb, kb, _BATCH_NN) * SCALE
            for h in range(N_HEADS):
                dq_ref[rows, h * HEAD_DIM:(h + 1) * HEAD_DIM] = dq[h].astype(BF)
            dk_acc[...] += _bdot(dsb, qs, _BATCH_TN)
            dv_acc[...] += _bdot(p.astype(BF), do, _BATCH_TN)

        def block(j, carry):
            i = pl.program_id(0) * Q_PER_STEP + j
            rows = pl.ds(pl.multiple_of(j * QB, QB), QB)
            pl.when(i < KEEP)(functools.partial(step, i, rows, True))
            pl.when((i >= KEEP) & (i < nq))(functools.partial(step, i, rows, False))
            for h in range(N_HEADS):
                hs = slice(h * HEAD_DIM, (h + 1) * HEAD_DIM)
                dk_ref[rows, hs] = dk_acc[h, 0:QB, :].astype(BF)
                dv_ref[rows, hs] = dv_acc[h, 0:QB, :].astype(BF)
            dk_acc[:, 0:KB - QB, :] = dk_acc[:, QB:KB, :]
            dv_acc[:, 0:KB - QB, :] = dv_acc[:, QB:KB, :]
            dk_acc[:, KB - QB:KB, :] = jnp.zeros((N_HEADS, QB, HEAD_DIM), F32)
            dv_acc[:, KB - QB:KB, :] = jnp.zeros((N_HEADS, QB, HEAD_DIM), F32)
            return carry

        lax.fori_loop(0, Q_PER_STEP, block, 0)

        @pl.when(pl.program_id(0) == n_steps - 1)
        def _():
            lane = lax.broadcasted_iota(jnp.int32, (1, ROLL_W), 1)
            hi = (lane < 384) | (lane >= 832)
            lo = (lane > 640) & (lane < 832)
            pad_scr[...] = jnp.zeros_like(pad_scr)
            for h in range(N_HEADS):
                pad_scr[:, 0:KB] = dbias_acc[h]
                g = _unskew_sum(pad_scr[...])
                s_hi = jnp.sum(jnp.where(hi, g, 0.0), axis=-1, keepdims=True)
                s_lo = jnp.sum(jnp.where(lo, g, 0.0), axis=-1, keepdims=True)
                g = jnp.where(lane == 384, g + s_hi, g)
                g = jnp.where(lane == 640, g + s_lo, g)
                dgp_ref[h:h + 1, :] = g

    assert nq % Q_PER_STEP == 0 and KEEP % Q_PER_STEP == 0
    rows_per_step = Q_PER_STEP * QB
    n_steps = (nq + KEEP) // Q_PER_STEP
    last = nq // Q_PER_STEP - 1
    lag = KEEP // Q_PER_STEP
    kv_scr = pltpu.VMEM((N_HEADS, S + PADK, HEAD_DIM), BF)
    return pl.pallas_call(
        _after(body, 6, after), name="attn_bwd", grid=(n_steps,),
        out_shape=(jax.ShapeDtypeStruct((S, D_A), BF), jax.ShapeDtypeStruct((S, D_A), BF),
                   jax.ShapeDtypeStruct((S, D_A), BF), jax.ShapeDtypeStruct((N_HEADS, ROLL_W), F32)),
        in_specs=[pl.BlockSpec((N_HEADS, rows_per_step, HEAD_DIM), lambda g: (0, jnp.minimum(g, last), 0)),
                  pl.BlockSpec((N_HEADS, rows_per_step, HEAD_DIM), lambda g: (0, jnp.minimum(g, last), 0)),
                  pl.BlockSpec(memory_space=pl.ANY), pl.BlockSpec(memory_space=pl.ANY),
                  pl.BlockSpec((N_HEADS, rows_per_step, 128), lambda g: (0, jnp.minimum(g, last), 0)),
                  pl.BlockSpec((N_HEADS, QB, KB), lambda g: (0, 0, 0))] + [_ANY] * len(after),
        out_specs=[pl.BlockSpec((rows_per_step, D_A), lambda g: (jnp.minimum(g, last), 0)),
                   pl.BlockSpec((rows_per_step, D_A), lambda g: (jnp.maximum(g - lag, 0), 0)),
                   pl.BlockSpec((rows_per_step, D_A), lambda g: (jnp.maximum(g - lag, 0), 0)),
                   pl.BlockSpec((N_HEADS, ROLL_W), lambda g: (0, 0))],
        scratch_shapes=[kv_scr, kv_scr,
                        pltpu.VMEM((N_HEADS, KB, HEAD_DIM), F32), pltpu.VMEM((N_HEADS, KB, HEAD_DIM), F32),
                        pltpu.VMEM((N_HEADS, QB, KB), F32), pltpu.VMEM((QB, ROLL_W), F32),
                        pltpu.SemaphoreType.DMA((2,))],
        compiler_params=_params(56, dimension_semantics=("arbitrary",)),
    )(q3, d_att3, k3, v3, lse, bias, *after)


def _sgu_core(ub, vb, lg, lb):
    u, du = _gelu_and_grad(ub)
    v, dv = _gelu_and_grad(vb)
    mu = jnp.mean(v, axis=-1, keepdims=True)
    vc = v - mu
    rstd = lax.rsqrt(jnp.mean(vc * vc, axis=-1, keepdims=True) + EPS)
    xh = vc * rstd
    vn = xh * lg + lb
    return u, du, dv, rstd, xh, vn


def _tri():
    r = lax.broadcasted_iota(jnp.int32, (SGU_CHUNK, SGU_CHUNK), 0)
    c = lax.broadcasted_iota(jnp.int32, (SGU_CHUNK, SGU_CHUNK), 1)
    return r >= c


def _tail_sgu(att, zrest, x, target, w_pa, w_pb, w_out, b_gate, final_g, ln_g, ln_b, w_s, b_s_t, tm=256):
    S = x.shape[0]
    nt = S // tm
    chunks = tm // SGU_CHUNK

    def body(att_ref, ga_ref, ub_ref, vb_ref, gb_ref, gta_ref, gtb_ref, x_ref, t_ref,
             wpa_ref, wpb_ref, wout_ref, bg_ref, fg_ref, lg_ref, lb_ref, ws_ref, bst_ref,
             dout_ref, datt_ref, dzt_ref, dzs_ref, gwout_hbm, gwpa_hbm, gwpb_hbm,
             gbg_ref, gfg_ref, loss_ref, gws_ref, gbs_ref, glg_ref, glb_ref,
             acc_out, acc_pa, acc_pb, sg_scr, mix_scr, dvn_scr, bs_acc, sems):
        i = pl.program_id(0)

        @pl.when(i == 0)
        def _():
            for r in (acc_out, acc_pa, acc_pb, gbg_ref, gfg_ref, loss_ref, gws_ref, glg_ref, glb_ref, bs_acc):
                r[...] = jnp.zeros_like(r)

        u, du, dv, rstd, xh, vn = _sgu_core(ub_ref[...], vb_ref[...], lg_ref[...], lb_ref[...])
        vnb = vn.astype(BF)
        tri = _tri()
        blocks = [(g, slice(n * SGU_CHUNK, (n + 1) * SGU_CHUNK), slice(g * 128, (g + 1) * 128))
                  for g in range(N_GROUPS) for n in range(chunks)]
        wts = [jnp.where(tri, ws_ref[g], 0.0) for g in range(N_GROUPS)]
        for g, rs, cs in blocks:
            mixed = _dot(wts[g].astype(BF), vnb[rs, cs]) + bst_ref[:, g:g + 1]
            mix_scr[rs, cs] = mixed
            sg_scr[rs, cs] = u[rs, cs] * mixed

        att = att_ref[...]
        sg = sg_scr[...]
        sa, dsa = _silu_and_grad(ga_ref[...])
        sb, dsb = _silu_and_grad(gb_ref[...])
        ya = (att * sa).astype(BF)
        yb = (sg * sb).astype(BF)
        pa = _dot(ya, wpa_ref[...])
        pb = _dot(yb, wpb_ref[...])
        ga = _sigmoid(gta_ref[...] + bg_ref[:, 0:D_MODEL])
        gb = _sigmoid(gtb_ref[...] + bg_ref[:, D_MODEL:2 * D_MODEL])
        merged = (ga * pa + gb * pb).astype(BF)
        out = x_ref[...] + _dot(merged, wout_ref[...])
        r2 = lax.rsqrt(jnp.mean(out * out, axis=-1, keepdims=True) + EPS)
        nrm = out * r2
        fg = fg_ref[...]
        err = nrm * fg - t_ref[...]
        loss_ref[...] += 0.5 * jnp.sum(jnp.mean(err * err, axis=-1, keepdims=True))
        dy = err * (1.0 / D_MODEL)
        gfg_ref[...] += jnp.sum(dy * nrm, axis=0, keepdims=True)
        dn = dy * fg
        d_out = r2 * (dn - nrm * jnp.mean(dn * nrm, axis=-1, keepdims=True))
        dout_ref[...] = d_out
        d_outb = d_out.astype(BF)
        acc_out[...] += _dot_tn(merged, d_outb)
        dm = _dot_nt(d_outb, wout_ref[...])
        d_pa = (dm * ga).astype(BF)
        d_pb = (dm * gb).astype(BF)
        d_gta = dm * pa * (ga * (1.0 - ga))
        d_gtb = dm * pb * (gb * (1.0 - gb))
        gbg_ref[:, 0:D_MODEL] += jnp.sum(d_gta, axis=0, keepdims=True)
        gbg_ref[:, D_MODEL:2 * D_MODEL] += jnp.sum(d_gtb, axis=0, keepdims=True)
        dzt_ref[:, 2 * D_A:2 * D_A + D_MODEL] = d_gta.astype(BF)
        dzt_ref[:, 2 * D_A + D_MODEL:] = d_gtb.astype(BF)
        acc_pa[...] += _dot_tn(ya, d_pa)
        acc_pb[...] += _dot_tn(yb, d_pb)
        d_ya = _dot_nt(d_pa, wpa_ref[...])
        d_yb = _dot_nt(d_pb, wpb_ref[...])
        d_att = (d_ya * sa).astype(BF)
        for hd in range(N_HEADS):
            datt_ref[hd] = d_att[:, hd * HEAD_DIM:(hd + 1) * HEAD_DIM]
        dzt_ref[:, 0:D_A] = (d_ya * att * dsa).astype(BF)
        dzt_ref[:, D_A:2 * D_A] = (d_yb * sg * dsb).astype(BF)

        dsg = d_yb * sb
        dzs_ref[:, 0:D_B] = (dsg * mix_scr[...] * du).astype(BF)
        dmix = dsg * u
        for g, rs, cs in blocks:
            dmb = dmix[rs, cs].astype(BF)
            bs_acc[:, cs] += dmix[rs, cs]
            gws_ref[g] += _dot_nt(dmb, vnb[rs, cs])
            dvn_scr[rs, cs] = _dot(wts[g].T.astype(BF), dmb)
        dvn = dvn_scr[...]
        glg_ref[...] += jnp.sum(dvn * xh, axis=0, keepdims=True)
        glb_ref[...] += jnp.sum(dvn, axis=0, keepdims=True)
        dxh = dvn * lg_ref[...]
        dvv = rstd * (dxh - jnp.mean(dxh, axis=-1, keepdims=True)
                      - xh * jnp.mean(dxh * xh, axis=-1, keepdims=True))
        dzs_ref[:, D_B:2 * D_B] = (dvv * dv).astype(BF)

        @pl.when(i == nt - 1)
        def _():
            cps = [pltpu.make_async_copy(acc_out, gwout_hbm, sems.at[0]),
                   pltpu.make_async_copy(acc_pa, gwpa_hbm, sems.at[1]),
                   pltpu.make_async_copy(acc_pb, gwpb_hbm, sems.at[2])]
            for cp in cps:
                cp.start()
            lane = lax.broadcasted_iota(jnp.int32, (SGU_CHUNK, 128), 1)
            cols = jnp.zeros((SGU_CHUNK, 128), F32)
            for g in range(N_GROUPS):
                gws_ref[g] = jnp.where(tri, gws_ref[g], 0.0)
                col = jnp.sum(bs_acc[:, g * 128:(g + 1) * 128], axis=-1, keepdims=True)
                cols = jnp.where(lane == g, col, cols)
            gbs_ref[...] = cols
            for cp in cps:
                cp.wait()

    c2 = lambda i: (0, 0)
    c3 = lambda i: (0, 0, 0)
    zcol = lambda w, blk: pl.BlockSpec((tm, w), lambda i: (i, blk))
    row = lambda w: pl.BlockSpec((tm, w), lambda i: (i, 0))
    return pl.pallas_call(
        body, name="tail", grid=(nt,),
        out_shape=(jax.ShapeDtypeStruct((S, D_MODEL), F32), jax.ShapeDtypeStruct((N_HEADS, S, HEAD_DIM), BF),
                   jax.ShapeDtypeStruct((S, 3072), BF), jax.ShapeDtypeStruct((S, 2 * D_B), BF),
                   jax.ShapeDtypeStruct((D_MODEL, D_MODEL), F32), jax.ShapeDtypeStruct((D_A, D_MODEL), F32),
                   jax.ShapeDtypeStruct((D_B, D_MODEL), F32),
                   jax.ShapeDtypeStruct((1, 2 * D_MODEL), F32), jax.ShapeDtypeStruct((1, D_MODEL), F32),
                   jax.ShapeDtypeStruct((1, 128), F32),
                   jax.ShapeDtypeStruct((N_GROUPS, 128, 128), F32), jax.ShapeDtypeStruct((SGU_CHUNK, 128), F32),
                   jax.ShapeDtypeStruct((1, D_B), F32), jax.ShapeDtypeStruct((1, D_B), F32)),
        in_specs=[row(D_A), zcol(512, 0), zcol(512, 1), zcol(512, 2), zcol(512, 3),
                  zcol(D_MODEL, 2), zcol(D_MODEL, 3), row(D_MODEL), row(D_MODEL),
                  pl.BlockSpec((D_A, D_MODEL), c2), pl.BlockSpec((D_B, D_MODEL), c2),
                  pl.BlockSpec((D_MODEL, D_MODEL), c2),
                  pl.BlockSpec((1, 2 * D_MODEL), c2), pl.BlockSpec((1, D_MODEL), c2),
                  pl.BlockSpec((1, D_B), c2), pl.BlockSpec((1, D_B), c2),
                  pl.BlockSpec((N_GROUPS, 128, 128), c3), pl.BlockSpec((128, N_GROUPS), c2)],
        out_specs=[row(D_MODEL), pl.BlockSpec((N_HEADS, tm, HEAD_DIM), lambda i: (0, i, 0)),
                   row(3072), row(2 * D_B), _ANY, _ANY, _ANY,
                   pl.BlockSpec((1, 2 * D_MODEL), c2), pl.BlockSpec((1, D_MODEL), c2),
                   pl.BlockSpec((1, 128), c2),
                   pl.BlockSpec((N_GROUPS, 128, 128), c3), pl.BlockSpec((SGU_CHUNK, 128), c2),
                   pl.BlockSpec((1, D_B), c2), pl.BlockSpec((1, D_B), c2)],
        scratch_shapes=[pltpu.VMEM((D_MODEL, D_MODEL), F32), pltpu.VMEM((D_A, D_MODEL), F32),
                        pltpu.VMEM((D_B, D_MODEL), F32),
                        pltpu.VMEM((tm, D_B), F32), pltpu.VMEM((tm, D_B), F32), pltpu.VMEM((tm, D_B), F32),
                        pltpu.VMEM((SGU_CHUNK, D_B), F32), pltpu.SemaphoreType.DMA((3,))],
        compiler_params=_params(58, dimension_semantics=("arbitrary",)),
    )(att, zrest, zrest, zrest, zrest, zrest, zrest, x, target, w_pa, w_pb, w_out, b_gate, final_g,
      ln_g, ln_b, w_s, b_s_t)


_DZ_MAP = ((0, 0), (1, 0), (2, 0), (3, 0), (4, 0), (4, 1), (3, 1), (3, 2), (3, 3), (3, 4), (3, 5))


def _dh_gradx(dq, dk, dv, dzt, dzs, w_in_bf, x, norm_g, d_out, tm=512, after=()):
    S = x.shape[0]

    def body(dq_ref, dk_ref, dv_ref, dzt_ref, dzs_ref, w_ref, x_ref, g_ref, dout_ref, gx_ref, gn_ref):
        i = pl.program_id(0)

        @pl.when(i == 0)
        def _():
            gn_ref[...] = jnp.zeros_like(gn_ref)

        pieces = (dq_ref, dk_ref, dv_ref, dzt_ref, dzs_ref)
        dh = jnp.zeros((tm, D_MODEL), F32)
        for j, (pc, blk) in enumerate(_DZ_MAP):
            dh += _dot_nt(pieces[pc][:, blk * 512:(blk + 1) * 512], w_ref[:, j * 512:(j + 1) * 512])
        xv = x_ref[...]
        r = lax.rsqrt(jnp.mean(xv * xv, axis=-1, keepdims=True) + EPS)
        nrm = xv * r
        gn_ref[...] += jnp.sum(dh * nrm, axis=0, keepdims=True)
        dn = dh * g_ref[...]
        gx_ref[...] = r * (dn - nrm * jnp.mean(dn * nrm, axis=-1, keepdims=True)) + dout_ref[...]

    row = lambda w: pl.BlockSpec((tm, w), lambda i: (i, 0))
    c2 = lambda i: (0, 0)
    return pl.pallas_call(
        _after(body, 9, after), name="dh_gradx", grid=(S // tm,),
        out_shape=(jax.ShapeDtypeStruct((S, D_MODEL), F32), jax.ShapeDtypeStruct((1, D_MODEL), F32)),
        in_specs=[row(512), row(512), row(512), row(3072), row(1024),
                  pl.BlockSpec((D_MODEL, D_IN), c2, pipeline_mode=pl.Buffered(1)), row(D_MODEL),
                  pl.BlockSpec((1, D_MODEL), c2), row(D_MODEL)]
        + [_ANY] * len(after),
        out_specs=[row(D_MODEL), pl.BlockSpec((1, D_MODEL), c2)],
        compiler_params=_params(48, dimension_semantics=("arbitrary",)),
    )(dq, dk, dv, dzt, dzs, w_in_bf, x, norm_g, d_out, *after)


def _gw_in(ht, dq, dk, dv, dzt, dzs, tn=512, after=()):
    S = ht.shape[1]
    per = 512 // tn
    cols = tuple((pc, per * blk + h) for pc, blk in _DZ_MAP for h in range(per))

    def body(ht_ref, dq_ref, dk_ref, dv_ref, dzt_ref, dzs_ref, o_ref, ob_ref):
        j = pl.program_id(0)
        pieces = (dq_ref, dk_ref, dv_ref, dzt_ref, dzs_ref)
        for pc in range(5):
            hit = functools.reduce(jnp.logical_or, [j == jj for jj, (p, _) in enumerate(cols) if p == pc])

            @pl.when(hit)
            def _(pc=pc):
                g = _dot(ht_ref[...], pieces[pc][...])
                o_ref[...] = g
                ob_ref[...] = g.astype(BF)

    def piece_spec(pc):
        cur = next(blk for p, blk in cols if p == pc)
        held = []
        for p, blk in cols:
            cur = blk if p == pc else cur
            held.append(cur)

        def index_map(j):
            blk = jnp.int32(held[0])
            for jj in range(1, len(held)):
                if held[jj] != held[jj - 1]:
                    blk = jnp.where(j >= jj, jnp.int32(held[jj]), blk)
            return (0, blk)

        return pl.BlockSpec((S, tn), index_map)

    return pl.pallas_call(
        _after(body, 6, after), name="gw_in", grid=(len(cols),),
        out_shape=(jax.ShapeDtypeStruct((D_MODEL, D_IN), F32), jax.ShapeDtypeStruct((D_MODEL, D_IN), BF)),
        in_specs=[pl.BlockSpec((D_MODEL, S), lambda j: (0, 0), pipeline_mode=pl.Buffered(1))]
        + [piece_spec(pc) for pc in range(5)]
        + [_ANY] * len(after),
        out_specs=[pl.BlockSpec((D_MODEL, tn), lambda j: (0, j)), pl.BlockSpec((D_MODEL, tn), lambda j: (0, j))],
        compiler_params=_params(56, dimension_semantics=("arbitrary",)),
    )(ht, dq, dk, dv, dzt, dzs, *after)


_HBM = pl.BlockSpec(memory_space=pltpu.HBM)
_SEM = pl.BlockSpec(memory_space=pltpu.SEMAPHORE)
_ANY = pl.BlockSpec(memory_space=pl.ANY)
_EFFECT = pltpu.SideEffectType.DATAFLOW_SIDE_EFFECTING


def _in_hbm(a):
    return pltpu.with_memory_space_constraint(a, pltpu.HBM)


def _after(body, n_in, after):
    if not after:
        return body
    return lambda *refs: body(*refs[:n_in], *refs[n_in + len(after):])


class _Started:
    def __init__(self, send, recv, bufs, token):
        self.send, self.recv, self.bufs, self.token = send, recv, bufs, token


_PEER_SETS = {"sibling": 7, "chips": 8, "both": 9}


def _peers(kind):
    x, y, c, chips = _mesh_pos()
    return ([(x, y, 1 - c)] if kind in ("sibling", "both") else []) + (
        [(cx, cy, c) for cx, cy in chips] if kind in ("chips", "both") else [])


def _signal_peers(kind):
    barrier = pltpu.get_barrier_semaphore()
    targets = _peers(kind)
    for peer in targets:
        pl.semaphore_signal(barrier, inc=1, device_id=peer, device_id_type=MESH)
    return lambda: pl.semaphore_wait(barrier, len(targets))


def _split_start(name, bufs, n_copies, copies, peers, after=()):
    nb = len(bufs)

    def body(*refs):
        _signal_peers(peers)()
        refs = refs[:nb] + refs[nb + len(after):]
        for cp in copies(refs[:nb], refs[nb], refs[nb + 1]):
            cp.start()
        refs[-1][...] = jnp.zeros_like(refs[-1])

    outs = pl.pallas_call(
        body, name=name,
        out_shape=(pltpu.SemaphoreType.DMA((n_copies,)), pltpu.SemaphoreType.DMA((n_copies,)),
                   *[pltpu.HBM(b.shape, b.dtype) for b in bufs], jax.ShapeDtypeStruct((8, 128), F32)),
        in_specs=[_HBM] * nb + [_ANY] * len(after),
        out_specs=(_SEM, _SEM, *[_HBM] * nb, pl.BlockSpec(memory_space=pltpu.VMEM)),
        input_output_aliases={k: 2 + k for k in range(nb)},
        compiler_params=_params(1, has_side_effects=_EFFECT, collective_id=_PEER_SETS[peers]),
    )(*[_in_hbm(b) for b in bufs], *after)
    return _Started(outs[0], outs[1], list(outs[2:2 + nb]), outs[-1])


def _split_wait(name, started, copies, after):
    nb = len(started.bufs)

    def body(*refs):
        for cp in copies(refs[:nb], refs[nb], refs[nb + 1]):
            cp.wait_send()
            cp.wait_recv()

    return list(pl.pallas_call(
        body, name=name,
        out_shape=tuple(pltpu.HBM(b.shape, b.dtype) for b in started.bufs),
        in_specs=[_HBM] * nb + [_SEM, _SEM, _ANY],
        out_specs=tuple([_HBM] * nb),
        input_output_aliases={k: k for k in range(nb)},
        compiler_params=_params(1, has_side_effects=_EFFECT),
    )(*started.bufs, started.send, started.recv, after))


def _x1_copies(ws):
    def copies(refs, send_sems, recv_sems):
        x, y, c, _ = _mesh_pos()
        out = []
        for k, w in enumerate(ws):
            for s in range(N_SHARD):
                out.append(pltpu.make_async_remote_copy(
                    src_ref=_UNITS[w](refs[k], s, 1 - c), dst_ref=refs[len(ws) + k].at[s],
                    send_sem=send_sems.at[N_SHARD * k + s], recv_sem=recv_sems.at[N_SHARD * k + s],
                    device_id=(x, y, 1 - c), device_id_type=MESH))
        return out
    return copies


def _x2_copies(n):
    def copies(refs, send_sems, recv_sems):
        x, y, c, chips = _mesh_pos()
        out = []
        for j, (cx, cy) in enumerate(chips):
            for k in range(n):
                out.append(pltpu.make_async_remote_copy(
                    src_ref=refs[k].at[2 * cx + cy], dst_ref=refs[n + k].at[j],
                    send_sem=send_sems.at[3 * k + j], recv_sem=recv_sems.at[3 * k + j],
                    device_id=(cx, cy, c), device_id_type=MESH))
        return out
    return copies


def _x3_copies(ws):
    def copies(refs, send_sems, recv_sems):
        x, y, c, _ = _mesh_pos()
        out = []
        for k, w in enumerate(ws):
            rows = _HALF_ROWS[w]
            mine = refs[k].at[pl.ds(_mo(c * rows, rows), rows), :]
            out.append(pltpu.make_async_remote_copy(
                src_ref=mine, dst_ref=mine, send_sem=send_sems.at[k], recv_sem=recv_sems.at[k],
                device_id=(x, y, 1 - c), device_id_type=MESH))
        return out
    return copies


def _x1_lands(ws, dtype=F32):
    return [lax.empty((N_SHARD,) + _UNIT_SHAPES[w], dtype) for w in ws]


def _x2_lands(ws):
    return [lax.empty((3,) + _UNIT_SHAPES[w], BF) for w in ws]


def _grad_add1(w, g, recv, pos):
    ur, uc = _UNIT_SHAPES[w]

    def body(pos_ref, g_ref, r_ref, csb_ref):
        csb_ref[0] = (g_ref[...] + r_ref[0].astype(F32)).astype(BF)

    u3 = lambda k, pos: (pos[2 + k], 0, 0)
    return pl.pallas_call(
        body, name=f"grad_add1_{w}",
        grid_spec=pltpu.PrefetchScalarGridSpec(
            num_scalar_prefetch=1, grid=(N_SHARD - 1,),
            in_specs=[pl.BlockSpec((ur, uc), lambda k, pos: (pos[0], pos[2 + k])), pl.BlockSpec((1, ur, uc), u3)],
            out_specs=pl.BlockSpec((1, ur, uc), u3)),
        out_shape=jax.ShapeDtypeStruct((N_SHARD, ur, uc), BF),
        compiler_params=_params(40, dimension_semantics=("arbitrary",)),
    )(pos, g, recv)


def _grad_add1_group(ws, gs, recvs):
    n = len(ws)

    def body(*refs):
        c = lax.axis_index("c")
        for k, w in enumerate(ws):
            g, r, cs, csb = refs[k], refs[n + k], refs[2 * n + k], refs[3 * n + k]
            for s in range(N_SHARD):
                v = _UNITS[w](g, s, c)[...] + r[s]
                cs[s] = v
                csb[s] = v.astype(BF)

    vm = pl.BlockSpec(memory_space=pltpu.VMEM)
    outs = pl.pallas_call(
        body, name="grad_add1_group",
        out_shape=tuple(jax.ShapeDtypeStruct((N_SHARD,) + _UNIT_SHAPES[w], dt) for dt in (F32, BF) for w in ws),
        in_specs=[vm] * (2 * n), out_specs=[vm] * (2 * n),
        compiler_params=_params(32),
    )(*gs, *recvs)
    return list(outs[:n]), list(outs[n:])


def _grad_add2_group(ws, css, recvs):
    n = len(ws)

    def body(*refs):
        x, y, c, _ = _mesh_pos()
        for k, w in enumerate(ws):
            cs, r, o = refs[k], refs[n + k], refs[2 * n + k]
            rows = _HALF_ROWS[w]
            total = ((cs[2 * x + y] + r[0].astype(F32)) + r[1].astype(F32)) + r[2].astype(F32)
            o[pl.ds(_mo(c * rows, rows), rows), :] = total

    vm = pl.BlockSpec(memory_space=pltpu.VMEM)
    return list(pl.pallas_call(
        body, name="grad_add2_group",
        out_shape=tuple(jax.ShapeDtypeStruct(_SHARD_SHAPES[w], F32) for w in ws),
        in_specs=[vm] * (2 * n), out_specs=[vm] * n,
        compiler_params=_params(32),
    )(*css, *recvs))


def _grad_add2(w, g, recv1, recv2, pos):
    ur, uc = _UNIT_SHAPES[w]
    nt = 4
    tr = ur // nt

    def body(pos_ref, g_ref, r1_ref, r2_ref, o_ref):
        own = g_ref[...] + r1_ref[0].astype(F32)
        o_ref[...] = ((own + r2_ref[0].astype(F32)) + r2_ref[1].astype(F32)) + r2_ref[2].astype(F32)

    mine = lambda t, pos: (pos[0] * nt + t, 0)
    return pl.pallas_call(
        body, name=f"grad_add2_{w}",
        grid_spec=pltpu.PrefetchScalarGridSpec(
            num_scalar_prefetch=1, grid=(nt,),
            in_specs=[pl.BlockSpec((tr, uc), lambda t, pos: (pos[0] * nt + t, pos[1])),
                      pl.BlockSpec((1, tr, uc), lambda t, pos: (pos[1], t, 0)),
                      pl.BlockSpec((3, tr, uc), lambda t, pos: (0, t, 0))],
            out_specs=pl.BlockSpec((tr, uc), mine)),
        out_shape=jax.ShapeDtypeStruct(_SHARD_SHAPES[w], F32),
        compiler_params=_params(32, dimension_semantics=("arbitrary",)),
    )(pos, g, recv1, recv2)


def _adamw_math(w, g, m, v):
    m = ADAM_B1 * m + (1.0 - ADAM_B1) * g
    v = ADAM_B2 * v + (1.0 - ADAM_B2) * (g * g)
    m_hat = m / ADAM_C1
    v_hat = v / ADAM_C2
    delta = -ADAM_LR * (m_hat / (jnp.sqrt(v_hat) + ADAM_EPS) + ADAM_WD * w)
    return delta, m, v


def _adamw_group(ws_, gs, ms, vs, after=()):
    n = len(ws_)

    def body(*refs):
        for k in range(n):
            w, g, m, v = (refs[j * n + k] for j in range(4))
            d, nm, nv, gc = (refs[(4 + j) * n + k] for j in range(4))
            gv = g[...]
            d[...], nm[...], nv[...] = _adamw_math(w[...], gv, m[...], v[...])
            gc[...] = gv

    vm = pl.BlockSpec(memory_space=pltpu.VMEM)
    outs = pl.pallas_call(
        _after(body, 4 * n, after), name="adamw_group",
        out_shape=tuple(jax.ShapeDtypeStruct(a.shape, F32) for _ in range(4) for a in ws_),
        in_specs=[vm] * (4 * n) + [_ANY] * len(after), out_specs=[vm] * (4 * n),
        compiler_params=_params(32),
    )(*ws_, *gs, *ms, *vs, *after)
    return [tuple(outs[j * n + k] for j in range(4)) for k in range(n)]


def _adamw(name, w, g, m, v, tr=256, after=()):
    rows, cols = w.shape

    def body(w_ref, g_ref, m_ref, v_ref, d_ref, nm_ref, nv_ref, gc_ref):
        gv = g_ref[...]
        d_ref[...], nm_ref[...], nv_ref[...] = _adamw_math(w_ref[...], gv, m_ref[...], v_ref[...])
        gc_ref[...] = gv

    spec = pl.BlockSpec((tr, cols), lambda i: (i, 0))
    return pl.pallas_call(
        _after(body, 4, after), name=name, grid=(rows // tr,),
        out_shape=tuple(jax.ShapeDtypeStruct((rows, cols), F32) for _ in range(4)),
        in_specs=[spec] * 4 + [_ANY] * len(after), out_specs=[spec] * 4,
        compiler_params=_params(32, dimension_semantics=("arbitrary",)),
    )(w, g, m, v, *after)


_REL_PAD = 384
_VEC_FIELDS = (("norm_g", 0, D_MODEL), ("b_gate", 1024, 2 * D_MODEL), ("sgu_ln_g", 3072, D_B),
               ("sgu_ln_b", 3584, D_B), ("b_s", 4096, N_GROUPS * 128), ("final_g", 4608, D_MODEL))
_LOSS_OFF = 5632
_REL_OFF = 5760
_NV = _REL_OFF + N_HEADS * _REL_PAD
_N_FIELDS = len(_VEC_FIELDS) + 2


_B_S_FIELD = [f[0] for f in _VEC_FIELDS].index("b_s")


def _assemble_row(dst, fields, transposed_b_s):
    for f, (_, off, n) in enumerate(_VEC_FIELDS):
        if transposed_b_s and f == _B_S_FIELD:
            t = fields[f][...].T
            for g in range(N_GROUPS):
                dst[:, off + 128 * g:off + 128 * (g + 1)] = t[g:g + 1, :]
        else:
            dst[:, off:off + n] = fields[f][...]
    for r in range(N_HEADS):
        dst[:, _REL_OFF + _REL_PAD * r:_REL_OFF + _REL_PAD * (r + 1)] = fields[len(_VEC_FIELDS)][r:r + 1, :]


def _small_reduce(grads, loss_row, after=()):
    n_in = _N_FIELDS + 1

    def body(*refs):
        g_refs, loss_ref = refs[:_N_FIELDS], refs[_N_FIELDS]
        out_v, out_w = refs[n_in:n_in + 2]
        mine_v, mine_w, gath_v, gath_w, send_sems, recv_sems = refs[n_in + 2:]
        x, y, c, chips = _mesh_pos()
        me, sibling = (x, y, c), (x, y, 1 - c)

        peers_entered = _signal_peers("both")
        _assemble_row(mine_v, g_refs, True)
        mine_v[:, _LOSS_OFF:_LOSS_OFF + 128] = loss_ref[...]
        mine_w[...] = g_refs[-1][...].astype(BF)
        peers_entered()
        my_k = 4 * x + 2 * y + c
        gath_v[my_k] = mine_v[...]
        gath_w[my_k] = mine_w[...]

        def copy(k, gath, block, to, src=None):
            dst = gath.at[4 * block[0] + 2 * block[1] + block[2]]
            return pltpu.make_async_remote_copy(
                src_ref=dst if src is None else src, dst_ref=dst,
                send_sem=send_sems.at[k], recv_sem=recv_sems.at[k], device_id=to, device_id_type=MESH)

        bufs = ((gath_v, mine_v), (gath_w, mine_w))
        first, passed = [], []
        for b, (gath, mine) in enumerate(bufs):
            first.append(copy(7 * b, gath, me, sibling, src=mine))
            first += [copy(7 * b + 1 + j, gath, me, (*chip, c), src=mine) for j, chip in enumerate(chips)]
        for cp in first:
            cp.start()
        for b, (gath, _) in enumerate(bufs):
            for j, chip in enumerate(chips):
                copy(7 * b + 1 + j, gath, (*chip, c), me).wait_recv()
                cp = copy(7 * b + 4 + j, gath, (*chip, c), sibling)
                cp.start()
                passed.append(cp)
        for b, (gath, _) in enumerate(bufs):
            copy(7 * b, gath, sibling, me).wait_recv()
            for j, chip in enumerate(chips):
                copy(7 * b + 4 + j, gath, (*chip, 1 - c), me).wait_recv()
        for cp in first + passed:
            cp.wait_send()

        tot_v, tot_w = gath_v[0], gath_w[0].astype(F32)
        for k in range(1, 8):
            tot_v = tot_v + gath_v[k]
            tot_w = tot_w + gath_w[k].astype(F32)
        out_v[...] = tot_v
        out_w[...] = tot_w

    vm = pl.BlockSpec(memory_space=pltpu.VMEM)
    return pl.pallas_call(
        _after(body, n_in, after), name="small_reduce",
        out_shape=(jax.ShapeDtypeStruct((1, _NV), F32), jax.ShapeDtypeStruct((N_GROUPS * 128, 128), F32)),
        in_specs=[vm] * n_in + [_ANY] * len(after), out_specs=[vm] * 2,
        scratch_shapes=[pltpu.VMEM((1, _NV), F32), pltpu.VMEM((N_GROUPS * 128, 128), BF),
                        pltpu.VMEM((8, 1, _NV), F32), pltpu.VMEM((8, N_GROUPS * 128, 128), BF),
                        pltpu.SemaphoreType.DMA((14,)), pltpu.SemaphoreType.DMA((14,))],
        compiler_params=_params(32, collective_id=_PEER_SETS["both"]),
    )(*grads, loss_row, *after)


def _small_adamw(tot_v, tot_w, params):
    n_in = 2 + 3 * _N_FIELDS

    def body(*refs):
        tv_ref, tw_ref = refs[:2]
        p_refs = [refs[2 + k * _N_FIELDS:2 + (k + 1) * _N_FIELDS] for k in range(3)]
        outs = refs[n_in:n_in + 4 * _N_FIELDS + 1]
        wmv = refs[-1]
        for k in range(3):
            _assemble_row(wmv.at[k], p_refs[k], False)
            wmv[k, :, _LOSS_OFF:_LOSS_OFF + 128] = jnp.zeros((1, 128), F32)
        tot_v, tot_w = tv_ref[...], tw_ref[...]
        res_v = (tot_v,) + _adamw_math(wmv[0], tot_v, wmv[1], wmv[2])
        res_w = (tot_w,) + _adamw_math(p_refs[0][-1][...], tot_w, p_refs[1][-1][...], p_refs[2][-1][...])
        for kind in range(4):
            o = outs[kind * _N_FIELDS:(kind + 1) * _N_FIELDS]
            for f, (_, off, n) in enumerate(_VEC_FIELDS):
                o[f][...] = res_v[kind][:, off:off + n]
            for r in range(N_HEADS):
                o[len(_VEC_FIELDS)][r:r + 1, :] = res_v[kind][:, _REL_OFF + _REL_PAD * r:_REL_OFF + _REL_PAD * (r + 1)]
            o[-1][...] = res_w[kind]
        outs[-1][...] = tot_v[:, _LOSS_OFF:_LOSS_OFF + 128]

    field_shapes = [(1, n) for _, _, n in _VEC_FIELDS] + [(N_HEADS, _REL_PAD), (N_GROUPS * 128, 128)]
    vm = pl.BlockSpec(memory_space=pltpu.VMEM)
    operands = [tot_v, tot_w] + [a for p in params for a in p]
    assert len(operands) == n_in
    outs = pl.pallas_call(
        body, name="small_adamw",
        out_shape=tuple(jax.ShapeDtypeStruct(s, F32) for _ in range(4) for s in field_shapes)
        + (jax.ShapeDtypeStruct((1, 128), F32),),
        in_specs=[vm] * n_in, out_specs=[vm] * (4 * _N_FIELDS + 1),
        scratch_shapes=[pltpu.VMEM((3, 1, _NV), F32)],
        compiler_params=_params(32),
    )(*operands)
    return [outs[k * _N_FIELDS:(k + 1) * _N_FIELDS] for k in range(4)], outs[-1]


def _small_fields(norm_g, b_gate, ln_g, ln_b, b_s, final_g, rel_bias, w_s):
    rel = jnp.pad(rel_bias.reshape(N_HEADS, N_REL), ((0, 0), (0, _REL_PAD - N_REL)))
    return (norm_g, b_gate, ln_g, ln_b, b_s.reshape(1, N_GROUPS * 128), final_g.reshape(1, D_MODEL),
            rel, w_s.reshape(N_GROUPS * 128, 128))


def _small_outputs(fields):
    n_g, b_g, l_g, l_b, b_s, f_g, rel, w_s = fields
    return (n_g, b_g, rel[:, :N_REL].reshape(1, N_HEADS, N_REL), l_g, l_b,
            w_s.reshape(1, N_GROUPS, 128, 128), b_s.reshape(1, N_GROUPS, 128), f_g.reshape(D_MODEL))


def _bias_row(rel_bias):
    hi = rel_bias[:, N_REL - 1:N_REL]
    lo = rel_bias[:, 0:1]
    return jnp.concatenate([jnp.broadcast_to(hi, (N_HEADS, 384)), rel_bias[:, ::-1],
                            jnp.broadcast_to(lo, (N_HEADS, 191)), jnp.broadcast_to(hi, (N_HEADS, 192))], axis=1)


def kernel(x, norm_g, w_in, b_gate, rel_bias, sgu_ln_g, sgu_ln_b, w_s, b_s, w_pa, w_pb, w_out, final_g, loss_target, m_norm_g, m_w_in, m_b_gate, m_rel_bias, m_sgu_ln_g, m_sgu_ln_b, m_w_s, m_b_s, m_w_pa, m_w_pb, m_w_out, m_final_g, v_norm_g, v_w_in, v_b_gate, v_rel_bias, v_sgu_ln_g, v_sgu_ln_b, v_w_s, v_b_s, v_w_pa, v_w_pb, v_w_out, v_final_g):
    S = x.shape[1]
    xs = x.reshape(S, D_MODEL)
    tgt = loss_target.reshape(S, D_MODEL)
    big_w = (w_in[0], w_pa[0], w_pb[0], w_out[0])
    big_m = (m_w_in[0], m_w_pa[0], m_w_pb[0], m_w_out[0])
    big_v = (v_w_in[0], v_w_pa[0], v_w_pb[0], v_w_out[0])
    rel = rel_bias[0]
    ws = w_s[0]
    bst = b_s[0].T
    fg = final_g.reshape(1, D_MODEL)
    chip = 2 * lax.axis_index("x") + lax.axis_index("y")
    pos = jnp.stack([lax.axis_index("c"), chip] + [(chip + k) % N_SHARD for k in range(1, N_SHARD)]).astype(jnp.int32)

    (w_in_bf,), staged, band_bias = _ag_weights((0,), big_w[:1], (1, 2, 3), big_w[1:], _bias_row(rel))
    ag_s = _split_start("ag_small_start", staged, 9, _gather_copies((1, 2, 3)), "chips", after=(w_in_bf,))

    ht, q3, k3, v3, zrest = _inproj_fwd(xs, norm_g, w_in_bf, after=(ag_s.token,))
    att, lse = _attn_fwd(q3, k3, v3, band_bias)
    w_pa_bf, w_pb_bf, w_out_bf = _split_wait("ag_small_wait", ag_s, _gather_copies((1, 2, 3)), att)
    (d_out, d_att, dzt, dzs, gw_out, gw_pa, gw_pb, g_bgate, g_final, loss_row,
     g_ws, g_bs_t, g_lng, g_lnb) = _tail_sgu(
        att, zrest, xs, tgt, w_pa_bf, w_pb_bf, w_out_bf, b_gate, fg, sgu_ln_g, sgu_ln_b, ws, bst)
    ws_s, ws_i = (1, 2, 3), (0,)

    x1s = _split_start("gx1s_start", [gw_pa, gw_pb, gw_out] + _x1_lands(ws_s), 12, _x1_copies(ws_s), "sibling")
    dq, dk, dv, d_gp = _attn_bwd(q3, k3, v3, d_att, lse, band_bias, after=(x1s.token,))
    got = _split_wait("gx1s_wait", x1s, _x1_copies(ws_s), dq)
    cs_s, csb_s = _grad_add1_group(ws_s, got[:3], got[3:])

    x2s = _split_start("gx2s_start", csb_s + _x2_lands(ws_s), 9, _x2_copies(3), "chips")
    gw_in, gw_in_bf = _gw_in(ht, dq, dk, dv, dzt, dzs, after=(x2s.token,))
    x1i = _split_start("gx1i_start", [gw_in_bf] + _x1_lands(ws_i, BF), 4, _x1_copies(ws_i), "sibling")
    got = _split_wait("gx2s_wait", x2s, _x2_copies(3), x1i.token)
    halves_s = _grad_add2_group(ws_s, cs_s, got[3:])
    x3s = _split_start("gx3s_start", halves_s, 3, _x3_copies(ws_s), "sibling")
    recv1_i = _split_wait("gx1i_wait", x1i, _x1_copies(ws_i), x3s.token)[1]
    csb_i = _grad_add1(0, gw_in, recv1_i, pos)

    x2i = _split_start("gx2i_start", [csb_i] + _x2_lands(ws_i), 3, _x2_copies(1), "chips")
    grad_x, g_norm = _dh_gradx(dq, dk, dv, dzt, dzs, w_in_bf, xs, norm_g, d_out, after=(x2i.token,))
    g_shards_s = _split_wait("gx3s_wait", x3s, _x3_copies(ws_s), grad_x)
    got = _split_wait("gx2i_wait", x2i, _x2_copies(1), grad_x)
    half_i = _grad_add2(0, gw_in, recv1_i, got[1], pos)
    x3i = _split_start("gx3i_start", [half_i], 1, _x3_copies(ws_i), "sibling")
    big = [None] * 4
    big[1:] = _adamw_group(big_w[1:], g_shards_s, big_m[1:], big_v[1:], after=(x3i.token,))

    g_rel = jnp.pad(d_gp[:, 384:384 + N_REL][:, ::-1], ((0, 0), (0, _REL_PAD - N_REL)))
    small_grads = (g_norm, g_bgate, g_lng, g_lnb, g_bs_t, g_final, g_rel, g_ws.reshape(N_GROUPS * 128, 128))
    small_params = (_small_fields(norm_g, b_gate, sgu_ln_g, sgu_ln_b, b_s, final_g, rel_bias, w_s),
                    _small_fields(m_norm_g, m_b_gate, m_sgu_ln_g, m_sgu_ln_b, m_b_s, m_final_g, m_rel_bias, m_w_s),
                    _small_fields(v_norm_g, v_b_gate, v_sgu_ln_g, v_sgu_ln_b, v_b_s, v_final_g, v_rel_bias, v_w_s))
    tot_v, tot_w = _small_reduce(small_grads, loss_row, after=(x3i.token,))
    (gsum, sdelta, sm, sv), loss_out = _small_adamw(tot_v, tot_w, small_params)

    g_shard_i, = _split_wait("gx3i_wait", x3i, _x3_copies(ws_i), loss_out)
    big[0] = _adamw("adamw_w_in", big_w[0], g_shard_i, big_m[0], big_v[0])
    sg_out, sd_out, sm_out, sv_out = (_small_outputs(f) for f in (gsum, sdelta, sm, sv))
    loss = loss_out[0, 0]

    def assemble(small, bigs):
        n_g, b_g, r_b, l_g, l_b, w_s_, b_s_, f_g = small
        b_in, b_pa, b_pb, b_out = (b[None] for b in bigs)
        return (n_g, b_in, b_g, r_b, l_g, l_b, w_s_, b_s_, b_pa, b_pb, b_out, f_g)

    grads_out = assemble(sg_out, [b[3] for b in big])
    delta_out = assemble(sd_out, [b[0] for b in big])
    m_out = assemble(sm_out, [b[1] for b in big])
    v_out = assemble(sv_out, [b[2] for b in big])
    return (loss, grad_x.reshape(1, S, D_MODEL), *grads_out, *delta_out, *m_out, *v_out)
```

```python
import functools
import math

import jax
import jax.numpy as jnp
from jax import lax
from jax.experimental import pallas as pl
from jax.experimental.pallas import tpu as pltpu

F32 = jnp.float32
BF = jnp.bfloat16
MESH = pl.DeviceIdType.MESH

D_MODEL = 1024
D_A = 512
D_B = 512
D_IN = 5632
N_HEADS = 8
HEAD_DIM = 64
CHUNK = 64
N_PREV = 8
SGU_CHUNK = 128
N_GROUPS = 4
N_REL = 257
EPS = 1e-6
NEG_INF = -1e30
SCALE = HEAD_DIM ** -0.5

QB = 2 * CHUNK
KB = (N_PREV + 2) * CHUNK
PADK = N_PREV * CHUNK
ROLL_W = 1024
KEEP = KB // QB - 1
Q_PER_STEP = 2

ADAM_LR = 0.001
ADAM_B1 = 0.9
ADAM_B2 = 0.999
ADAM_EPS = 1e-08
ADAM_WD = 0.01
ADAM_STEP = 10
ADAM_C1 = 1.0 - ADAM_B1 ** ADAM_STEP
ADAM_C2 = 1.0 - ADAM_B2 ** ADAM_STEP

AG_PIECES = 4
N_SHARD = 4
SHARD_IN = D_IN // N_SHARD
MIB = 1024 * 1024


V7X_VMEM_MIB = 64
VMEM_RESERVE_MIB = V7X_VMEM_MIB - 4


def _params(vmem_mib, **kw):
    assert vmem_mib <= VMEM_RESERVE_MIB
    return pltpu.CompilerParams(vmem_limit_bytes=VMEM_RESERVE_MIB * MIB, **kw)


def _sigmoid(x):
    return 1.0 / (1.0 + jnp.exp(-x))


def _silu_and_grad(x):
    s = _sigmoid(x)
    return x * s, s * (1.0 + x * (1.0 - s))


_GELU_C = math.sqrt(2.0 / math.pi)
_GELU_A = 0.044715


def _gelu_and_grad(x):
    x2 = x * x
    t = jnp.tanh(_GELU_C * (x + _GELU_A * (x2 * x)))
    cdf = 0.5 * (1.0 + t)
    grad = cdf + 0.5 * x * (1.0 - t * t) * (_GELU_C * (1.0 + 3.0 * _GELU_A * x2))
    return x * cdf, grad


def _dot(a, b):
    return jnp.dot(a, b, preferred_element_type=F32)


def _dot_nt(a, b):
    return lax.dot_general(a, b, (((1,), (1,)), ((), ())), preferred_element_type=F32)


def _dot_tn(a, b):
    return lax.dot_general(a, b, (((0,), (0,)), ((), ())), preferred_element_type=F32)


def _mo(v, m):
    return v if isinstance(v, int) else pl.multiple_of(v, m)


def _unit_in(ref, s, p):
    return ref.at[pl.ds(_mo(p * 512, 512), 512), pl.ds(_mo(s * SHARD_IN, 128), SHARD_IN)]


def _unit_p(ref, s, p):
    return ref.at[pl.ds(_mo(p * 256, 256), 256), pl.ds(_mo(s * 256, 128), 256)]


def _unit_out(ref, s, p):
    return ref.at[pl.ds(_mo(s * 256 + p * 128, 128), 128), :]


_UNITS = (_unit_in, _unit_p, _unit_p, _unit_out)
_HALF_ROWS = (512, 256, 256, 128)
_UNIT_SHAPES = ((512, SHARD_IN), (256, 256), (256, 256), (128, D_MODEL))
_FULL_SHAPES = ((D_MODEL, D_IN), (D_A, D_MODEL), (D_B, D_MODEL), (D_MODEL, D_MODEL))
_SHARD_SHAPES = ((D_MODEL, SHARD_IN), (D_A, 256), (D_B, 256), (256, D_MODEL))


def _mesh_pos():
    x, y, c = lax.axis_index("x"), lax.axis_index("y"), lax.axis_index("c")
    chips = [(1 - x, y), (x, 1 - y), (1 - x, 1 - y)]
    return x, y, c, chips


def _ag_weights(ws, shards, later_ws, later_shards, gp):
    n, m = len(ws), len(later_ws)

    def body(*refs):
        ins, later_ins, gp_ref = refs[:n], refs[n:n + m], refs[n + m]
        o = n + m + 1
        outs, later_outs, bias_ref = refs[o:o + n], refs[o + n:o + n + m], refs[o + n + m]
        o += n + m + 1
        stage, later_stage = refs[o:o + n], refs[o + n:o + n + m]
        send_sems, recv_sems, local_sems, later_sems = refs[o + n + m:]
        x, y, c, chips = _mesh_pos()
        s_me = 2 * x + y
        sibling = (x, y, 1 - c)
        def rows_of(k, p):
            rows = _HALF_ROWS[ws[k]]
            return pl.ds(_mo(p * rows, rows), rows)

        def half(k, p):
            return stage[k].at[rows_of(k, p), :]

        def unit(k, s, p):
            return _UNITS[ws[k]](outs[k], s, p)

        def rcopy(k, i, src, dst, to):
            return pltpu.make_async_remote_copy(src_ref=src, dst_ref=dst, send_sem=send_sems.at[k, i],
                                                recv_sem=recv_sems.at[k, i], device_id=to, device_id_type=MESH)

        peers_entered = _signal_peers("both")
        for k in range(n):
            stage[k][rows_of(k, c), :] = ins[k][rows_of(k, c), :].astype(BF)
        peers_entered()
        def piece(ref, k, q):
            rows = _HALF_ROWS[ws[k]] // AG_PIECES
            return ref.at[pl.ds(q * rows, rows), :]

        sends = []
        for q in range(AG_PIECES):
            for j, (cx, cy) in enumerate(chips):
                for k in range(n):
                    cp = rcopy(k, j * AG_PIECES + q, piece(half(k, c), k, q), piece(unit(k, s_me, c), k, q),
                               (cx, cy, c))
                    cp.start()
                    sends.append(cp)
        for k in range(n):
            stage[k][rows_of(k, 1 - c), :] = ins[k][rows_of(k, 1 - c), :].astype(BF)
        local = []
        for k in range(n):
            for p in range(2):
                cp = pltpu.make_async_copy(half(k, p), unit(k, s_me, p), local_sems.at[k, p])
                cp.start()
                local.append(cp)
        for k, w in enumerate(later_ws):
            later_stage[k][...] = later_ins[k][...].astype(BF)
            cp = pltpu.make_async_copy(later_stage[k], _shard_of(later_outs[k], w, s_me), later_sems.at[k])
            cp.start()
            local.append(cp)
        keep = _struct_mask()
        for h in range(N_HEADS):
            bias_ref[h] = jnp.where(keep, _skew_table(gp_ref[h:h + 1, :])[:, :KB], NEG_INF)
        for q in range(AG_PIECES):
            for j, (cx, cy) in enumerate(chips):
                for k in range(n):
                    landed = piece(unit(k, 2 * cx + cy, c), k, q)
                    rcopy(k, j * AG_PIECES + q, landed, landed, (cx, cy, c)).wait_recv()
                    cp = rcopy(k, (3 + j) * AG_PIECES + q, landed, landed, sibling)
                    cp.start()
                    sends.append(cp)
        for q in range(AG_PIECES):
            for j, (cx, cy) in enumerate(chips):
                for k in range(n):
                    other = piece(unit(k, 2 * cx + cy, 1 - c), k, q)
                    rcopy(k, (3 + j) * AG_PIECES + q, other, other, sibling).wait_recv()
        for cp in sends:
            cp.wait_send()
        for cp in local:
            cp.wait()

    vm = pl.BlockSpec(memory_space=pltpu.VMEM)
    outs = pl.pallas_call(
        body, name="ag_weights",
        out_shape=tuple(jax.ShapeDtypeStruct(_FULL_SHAPES[w], BF) for w in tuple(ws) + tuple(later_ws))
        + (jax.ShapeDtypeStruct((N_HEADS, QB, KB), F32),),
        in_specs=[vm] * (n + m + 1), out_specs=[_ANY] * (n + m) + [vm],
        scratch_shapes=[pltpu.VMEM(_SHARD_SHAPES[w], BF) for w in tuple(ws) + tuple(later_ws)]
        + [pltpu.SemaphoreType.DMA((n, 6 * AG_PIECES)), pltpu.SemaphoreType.DMA((n, 6 * AG_PIECES)),
           pltpu.SemaphoreType.DMA((n, 2)), pltpu.SemaphoreType.DMA((m,))],
        compiler_params=_params(48, collective_id=_PEER_SETS["both"]),
    )(*shards, *later_shards, gp)
    return list(outs[:n]), list(outs[n:n + m]), outs[-1]


def _shard_of(ref, w, s):
    if w == 0:
        return ref.at[:, pl.ds(_mo(s * SHARD_IN, 128), SHARD_IN)]
    if w == 3:
        return ref.at[pl.ds(_mo(s * 256, 256), 256), :]
    return ref.at[:, pl.ds(_mo(s * 256, 128), 256)]


def _gather_copies(ws):
    def copies(refs, send_sems, recv_sems):
        x, y, c, chips = _mesh_pos()
        out = []
        for j, (cx, cy) in enumerate(chips):
            for k, w in enumerate(ws):
                mine = _shard_of(refs[k], w, 2 * x + y)
                out.append(pltpu.make_async_remote_copy(
                    src_ref=mine, dst_ref=mine, send_sem=send_sems.at[3 * k + j], recv_sem=recv_sems.at[3 * k + j],
                    device_id=(cx, cy, c), device_id_type=MESH))
        return out
    return copies


def _inproj_fwd(x, norm_g, w_in_bf, tm=512, after=()):
    S = x.shape[0]

    def body(x_ref, g_ref, w_ref, ht_ref, q_ref, k_ref, v_ref, zr_ref):
        xv = x_ref[...]
        r = lax.rsqrt(jnp.mean(xv * xv, axis=-1, keepdims=True) + EPS)
        hf = (xv * r) * g_ref[...]
        ht_ref[...] = hf.T.astype(BF)
        h = hf.astype(BF)
        heads = (q_ref, k_ref, v_ref)
        for j in range(D_IN // 512):
            z = _dot(h, w_ref[:, j * 512:(j + 1) * 512])
            if j < 3:
                zb = z.astype(BF)
                for hd in range(N_HEADS):
                    heads[j][hd] = zb[:, hd * HEAD_DIM:(hd + 1) * HEAD_DIM]
            else:
                zr_ref[:, (j - 3) * 512:(j - 2) * 512] = z

    head_major = jax.ShapeDtypeStruct((N_HEADS, S, HEAD_DIM), BF)
    head_spec = pl.BlockSpec((N_HEADS, tm, HEAD_DIM), lambda i: (0, i, 0))
    return pl.pallas_call(
        _after(body, 3, after), name="inproj_fwd", grid=(S // tm,),
        out_shape=(jax.ShapeDtypeStruct((D_MODEL, S), BF), head_major, head_major, head_major,
                   jax.ShapeDtypeStruct((S, D_IN - 3 * D_A), F32)),
        in_specs=[pl.BlockSpec((tm, D_MODEL), lambda i: (i, 0)),
                  pl.BlockSpec((1, D_MODEL), lambda i: (0, 0)),
                  pl.BlockSpec((D_MODEL, D_IN), lambda i: (0, 0), pipeline_mode=pl.Buffered(1))]
        + [_ANY] * len(after),
        out_specs=[pl.BlockSpec((D_MODEL, tm), lambda i: (0, i)),
                   head_spec, head_spec, head_spec,
                   pl.BlockSpec((tm, D_IN - 3 * D_A), lambda i: (i, 0))],
        compiler_params=_params(52, dimension_semantics=("arbitrary",)),
    )(x, norm_g, w_in_bf, *after)


def _skew_table(gp_row):
    row = lax.broadcasted_iota(jnp.int32, (QB, ROLL_W), 0)
    t = jnp.broadcast_to(gp_row, (QB, ROLL_W))
    for b in range(7):
        t = jnp.where(((row >> b) & 1) == 1, pltpu.roll(t, 1 << b, axis=1), t)
    return t


def _unskew_sum(d):
    row = lax.broadcasted_iota(jnp.int32, (QB, ROLL_W), 0)
    for b in range(7):
        d = jnp.where(((row >> b) & 1) == 1, pltpu.roll(d, ROLL_W - (1 << b), axis=1), d)
    return jnp.sum(d, axis=0, keepdims=True)


def _struct_mask():
    a = lax.broadcasted_iota(jnp.int32, (QB, KB), 0) // CHUNK
    b = lax.broadcasted_iota(jnp.int32, (QB, KB), 1) // CHUNK
    return (b >= a) & (b <= a + N_PREV)


def _load_kv(k_hbm, v_hbm, k_scr, v_scr, sems, S, meanwhile=lambda: None):
    zeros = jnp.zeros((N_HEADS, PADK, HEAD_DIM), BF)
    k_scr[:, 0:PADK, :] = zeros
    v_scr[:, 0:PADK, :] = zeros
    ck = pltpu.make_async_copy(k_hbm, k_scr.at[:, pl.ds(PADK, S), :], sems.at[0])
    cv = pltpu.make_async_copy(v_hbm, v_scr.at[:, pl.ds(PADK, S), :], sems.at[1])
    ck.start()
    cv.start()
    meanwhile()
    ck.wait()
    cv.wait()


_BATCH_NT = (((2,), (2,)), ((0,), (0,)))
_BATCH_NN = (((2,), (1,)), ((0,), (0,)))
_BATCH_TN = (((1,), (1,)), ((0,), (0,)))


def _bdot(a, b, dims):
    return lax.dot_general(a, b, dims, preferred_element_type=F32)


def _scaled(q):
    return q * jnp.asarray(SCALE, BF)


def _scores(qs, kb, bias, i, front):
    s = _bdot(qs, kb, _BATCH_NT) + bias
    if front:
        col = lax.broadcasted_iota(jnp.int32, (1, 1, KB), 2)
        s = jnp.where(col >= PADK - i * QB, s, NEG_INF)
    return s


def _attn_fwd(q3, k3, v3, bias):
    S = q3.shape[1]

    def body(q_ref, k_hbm, v_hbm, bias_ref, o_ref, lse_ref, k_scr, v_scr, sems):
        @pl.when(pl.program_id(0) == 0)
        def _():
            _load_kv(k_hbm, v_hbm, k_scr, v_scr, sems, S)

        def step(i, rows, front):
            start = pl.multiple_of(i * QB, QB)
            kb = k_scr[:, pl.ds(start, KB), :]
            vb = v_scr[:, pl.ds(start, KB), :]
            s = _scores(_scaled(q_ref[:, rows, :]), kb, bias_ref[...], i, front)
            m = jnp.max(s, axis=-1, keepdims=True)
            e = jnp.exp(s - m)
            l = jnp.sum(e, axis=-1, keepdims=True)
            p = e * (1.0 / l)
            o = _bdot(p.astype(BF), vb, _BATCH_NN)
            lse_ref[:, rows, :] = jnp.broadcast_to(m + jnp.log(l), (N_HEADS, QB, 128))
            for h in range(N_HEADS):
                o_ref[rows, h * HEAD_DIM:(h + 1) * HEAD_DIM] = o[h]

        def block(j, carry):
            i = pl.program_id(0) * Q_PER_STEP + j
            rows = pl.ds(pl.multiple_of(j * QB, QB), QB)
            pl.when(i < KEEP)(functools.partial(step, i, rows, True))
            pl.when(i >= KEEP)(functools.partial(step, i, rows, False))
            return carry

        lax.fori_loop(0, Q_PER_STEP, block, 0)

    rows_per_step = Q_PER_STEP * QB
    kv_scr = pltpu.VMEM((N_HEADS, S + PADK, HEAD_DIM), BF)
    return pl.pallas_call(
        body, name="attn_fwd", grid=(S // rows_per_step,),
        out_shape=(jax.ShapeDtypeStruct((S, D_A), F32), jax.ShapeDtypeStruct((N_HEADS, S, 128), F32)),
        in_specs=[pl.BlockSpec((N_HEADS, rows_per_step, HEAD_DIM), lambda g: (0, g, 0)),
                  pl.BlockSpec(memory_space=pl.ANY), pl.BlockSpec(memory_space=pl.ANY),
                  pl.BlockSpec((N_HEADS, QB, KB), lambda g: (0, 0, 0))],
        out_specs=[pl.BlockSpec((rows_per_step, D_A), lambda g: (g, 0)),
                   pl.BlockSpec((N_HEADS, rows_per_step, 128), lambda g: (0, g, 0))],
        scratch_shapes=[kv_scr, kv_scr, pltpu.SemaphoreType.DMA((2,))],
        compiler_params=_params(48, dimension_semantics=("arbitrary",)),
    )(q3, k3, v3, bias)


def _attn_bwd(q3, k3, v3, d_att3, lse, bias, after=()):
    S = q3.shape[1]
    nq = S // QB

    def body(q_ref, do_ref, k_hbm, v_hbm, lse_ref, bias_ref, dq_ref, dk_ref, dv_ref, dgp_ref,
             k_scr, v_scr, dk_acc, dv_acc, dbias_acc, pad_scr, sems):
        @pl.when(pl.program_id(0) == 0)
        def _():
            def clear():
                dk_acc[...] = jnp.zeros_like(dk_acc)
                dv_acc[...] = jnp.zeros_like(dv_acc)
                dbias_acc[...] = jnp.zeros_like(dbias_acc)
            _load_kv(k_hbm, v_hbm, k_scr, v_scr, sems, S, clear)

        def step(i, rows, front):
            start = pl.multiple_of(i * QB, QB)
            kb = k_scr[:, pl.ds(start, KB), :]
            vb = v_scr[:, pl.ds(start, KB), :]
            qs = _scaled(q_ref[:, rows, :])
            do = do_ref[:, rows, :]
            p = jnp.exp(_scores(qs, kb, bias_ref[...], i, front) - jnp.tile(lse_ref[:, rows, :], (1, 1, KB // 128)))
            dp = _bdot(do, vb, _BATCH_NT)
            ds = p * (dp - jnp.sum(dp * p, axis=-1, keepdims=True))
            dbias_acc[...] += ds
            dsb = ds.astype(BF)
            dq = _bdot(dsb, kb, _BATCH_NN) * SCALE
            for h in range(N_HEADS):
                dq_ref[rows, h * HEAD_DIM:(h + 1) * HEAD_DIM] = dq[h].astype(BF)
            dk_acc[...] += _bdot(dsb, qs, _BATCH_TN)
            dv_acc[...] += _bdot(p.astype(BF), do, _BATCH_TN)

        def block(j, carry):
            i = pl.program_id(0) * Q_PER_STEP + j
            rows = pl.ds(pl.multiple_of(j * QB, QB), QB)
            pl.when(i < KEEP)(functools.partial(step, i, rows, True))
            pl.when((i >= KEEP) & (i < nq))(functools.partial(step, i, rows, False))
            for h in range(N_HEADS):
                hs = slice(h * HEAD_DIM, (h + 1) * HEAD_DIM)
                dk_ref[rows, hs] = dk_acc[h, 0:QB, :].astype(BF)
                dv_ref[rows, hs] = dv_acc[h, 0:QB, :].astype(BF)
            dk_acc[:, 0:KB - QB, :] = dk_acc[:, QB:KB, :]
            dv_acc[:, 0:KB - QB, :] = dv_acc[:, QB:KB, :]
            dk_acc[:, KB - QB:KB, :] = jnp.zeros((N_HEADS, QB, HEAD_DIM), F32)
            dv_acc[:, KB - QB:KB, :] = jnp.zeros((N_HEADS, QB, HEAD_DIM), F32)
            return carry

        lax.fori_loop(0, Q_PER_STEP, block, 0)

        @pl.when(pl.program_id(0) == n_steps - 1)
        def _():
            lane = lax.broadcasted_iota(jnp.int32, (1, ROLL_W), 1)
            hi = (lane < 384) | (lane >= 832)
            lo = (lane > 640) & (lane < 832)
            pad_scr[...] = jnp.zeros_like(pad_scr)
            for h in range(N_HEADS):
                pad_scr[:, 0:KB] = dbias_acc[h]
                g = _unskew_sum(pad_scr[...])
                s_hi = jnp.sum(jnp.where(hi, g, 0.0), axis=-1, keepdims=True)
                s_lo = jnp.sum(jnp.where(lo, g, 0.0), axis=-1, keepdims=True)
                g = jnp.where(lane == 384, g + s_hi, g)
                g = jnp.where(lane == 640, g + s_lo, g)
                dgp_ref[h:h + 1, :] = g

    assert nq % Q_PER_STEP == 0 and KEEP % Q_PER_STEP == 0
    rows_per_step = Q_PER_STEP * QB
    n_steps = (nq + KEEP) // Q_PER_STEP
    last = nq // Q_PER_STEP - 1
    lag = KEEP // Q_PER_STEP
    kv_scr = pltpu.VMEM((N_HEADS, S + PADK, HEAD_DIM), BF)
    return pl.pallas_call(
        _after(body, 6, after), name="attn_bwd", grid=(n_steps,),
        out_shape=(jax.ShapeDtypeStruct((S, D_A), BF), jax.ShapeDtypeStruct((S, D_A), BF),
                   jax.ShapeDtypeStruct((S, D_A), BF), jax.ShapeDtypeStruct((N_HEADS, ROLL_W), F32)),
        in_specs=[pl.BlockSpec((N_HEADS, rows_per_step, HEAD_DIM), lambda g: (0, jnp.minimum(g, last), 0)),
                  pl.BlockSpec((N_HEADS, rows_per_step, HEAD_DIM), lambda g: (0, jnp.minimum(g, last), 0)),
                  pl.BlockSpec(memory_space=pl.ANY), pl.BlockSpec(memory_space=pl.ANY),
                  pl.BlockSpec((N_HEADS, rows_per_step, 128), lambda g: (0, jnp.minimum(g, last), 0)),
                  pl.BlockSpec((N_HEADS, QB, KB), lambda g: (0, 0, 0))] + [_ANY] * len(after),
        out_specs=[pl.BlockSpec((rows_per_step, D_A), lambda g: (jnp.minimum(g, last), 0)),
                   pl.BlockSpec((rows_per_step, D_A), lambda g: (jnp.maximum(g - lag, 0), 0)),
                   pl.BlockSpec((rows_per_step, D_A), lambda g: (jnp.maximum(g - lag, 0), 0)),
                   pl.BlockSpec((N_HEADS, ROLL_W), lambda g: (0, 0))],
        scratch_shapes=[kv_scr, kv_scr,
                        pltpu.VMEM((N_HEADS, KB, HEAD_DIM), F32), pltpu.VMEM((N_HEADS, KB, HEAD_DIM), F32),
                        pltpu.VMEM((N_HEADS, QB, KB), F32), pltpu.VMEM((QB, ROLL_W), F32),
                        pltpu.SemaphoreType.DMA((2,))],
        compiler_params=_params(56, dimension_semantics=("arbitrary",)),
    )(q3, d_att3, k3, v3, lse, bias, *after)


def _sgu_core(ub, vb, lg, lb):
    u, du = _gelu_and_grad(ub)
    v, dv = _gelu_and_grad(vb)
    mu = jnp.mean(v, axis=-1, keepdims=True)
    vc = v - mu
    rstd = lax.rsqrt(jnp.mean(vc * vc, axis=-1, keepdims=True) + EPS)
    xh = vc * rstd
    vn = xh * lg + lb
    return u, du, dv, rstd, xh, vn


def _tri():
    r = lax.broadcasted_iota(jnp.int32, (SGU_CHUNK, SGU_CHUNK), 0)
    c = lax.broadcasted_iota(jnp.int32, (SGU_CHUNK, SGU_CHUNK), 1)
    return r >= c


def _tail_sgu(att, zrest, x, target, w_pa, w_pb, w_out, b_gate, final_g, ln_g, ln_b, w_s, b_s_t, tm=256):
    S = x.shape[0]
    nt = S // tm
    chunks = tm // SGU_CHUNK

    def body(att_ref, ga_ref, ub_ref, vb_ref, gb_ref, gta_ref, gtb_ref, x_ref, t_ref,
             wpa_ref, wpb_ref, wout_ref, bg_ref, fg_ref, lg_ref, lb_ref, ws_ref, bst_ref,
             dout_ref, datt_ref, dzt_ref, dzs_ref, gwout_hbm, gwpa_hbm, gwpb_hbm,
             gbg_ref, gfg_ref, loss_ref, gws_ref, gbs_ref, glg_ref, glb_ref,
             acc_out, acc_pa, acc_pb, sg_scr, mix_scr, dvn_scr, bs_acc, sems):
        i = pl.program_id(0)

        @pl.when(i == 0)
        def _():
            for r in (acc_out, acc_pa, acc_pb, gbg_ref, gfg_ref, loss_ref, gws_ref, glg_ref, glb_ref, bs_acc):
                r[...] = jnp.zeros_like(r)

        u, du, dv, rstd, xh, vn = _sgu_core(ub_ref[...], vb_ref[...], lg_ref[...], lb_ref[...])
        vnb = vn.astype(BF)
        tri = _tri()
        blocks = [(g, slice(n * SGU_CHUNK, (n + 1) * SGU_CHUNK), slice(g * 128, (g + 1) * 128))
                  for g in range(N_GROUPS) for n in range(chunks)]
        wts = [jnp.where(tri, ws_ref[g], 0.0) for g in range(N_GROUPS)]
        for g, rs, cs in blocks:
            mixed = _dot(wts[g].astype(BF), vnb[rs, cs]) + bst_ref[:, g:g + 1]
            mix_scr[rs, cs] = mixed
            sg_scr[rs, cs] = u[rs, cs] * mixed

        att = att_ref[...]
        sg = sg_scr[...]
        sa, dsa = _silu_and_grad(ga_ref[...])
        sb, dsb = _silu_and_grad(gb_ref[...])
        ya = (att * sa).astype(BF)
        yb = (sg * sb).astype(BF)
        pa = _dot(ya, wpa_ref[...])
        pb = _dot(yb, wpb_ref[...])
        ga = _sigmoid(gta_ref[...] + bg_ref[:, 0:D_MODEL])
        gb = _sigmoid(gtb_ref[...] + bg_ref[:, D_MODEL:2 * D_MODEL])
        merged = (ga * pa + gb * pb).astype(BF)
        out = x_ref[...] + _dot(merged, wout_ref[...])
        r2 = lax.rsqrt(jnp.mean(out * out, axis=-1, keepdims=True) + EPS)
        nrm = out * r2
        fg = fg_ref[...]
        err = nrm * fg - t_ref[...]
        loss_ref[...] += 0.5 * jnp.sum(jnp.mean(err * err, axis=-1, keepdims=True))
        dy = err * (1.0 / D_MODEL)
        gfg_ref[...] += jnp.sum(dy * nrm, axis=0, keepdims=True)
        dn = dy * fg
        d_out = r2 * (dn - nrm * jnp.mean(dn * nrm, axis=-1, keepdims=True))
        dout_ref[...] = d_out
        d_outb = d_out.astype(BF)
        acc_out[...] += _dot_tn(merged, d_outb)
        dm = _dot_nt(d_outb, wout_ref[...])
        d_pa = (dm * ga).astype(BF)
        d_pb = (dm * gb).astype(BF)
        d_gta = dm * pa * (ga * (1.0 - ga))
        d_gtb = dm * pb * (gb * (1.0 - gb))
        gbg_ref[:, 0:D_MODEL] += jnp.sum(d_gta, axis=0, keepdims=True)
        gbg_ref[:, D_MODEL:2 * D_MODEL] += jnp.sum(d_gtb, axis=0, keepdims=True)
        dzt_ref[:, 2 * D_A:2 * D_A + D_MODEL] = d_gta.astype(BF)
        dzt_ref[:, 2 * D_A + D_MODEL:] = d_gtb.astype(BF)
        acc_pa[...] += _dot_tn(ya, d_pa)
        acc_pb[...] += _dot_tn(yb, d_pb)
        d_ya = _dot_nt(d_pa, wpa_ref[...])
        d_yb = _dot_nt(d_pb, wpb_ref[...])
        d_att = (d_ya * sa).astype(BF)
        for hd in range(N_HEADS):
            datt_ref[hd] = d_att[:, hd * HEAD_DIM:(hd + 1) * HEAD_DIM]
        dzt_ref[:, 0:D_A] = (d_ya * att * dsa).astype(BF)
        dzt_ref[:, D_A:2 * D_A] = (d_yb * sg * dsb).astype(BF)

        dsg = d_yb * sb
        dzs_ref[:, 0:D_B] = (dsg * mix_scr[...] * du).astype(BF)
        dmix = dsg * u
        for g, rs, cs in blocks:
            dmb = dmix[rs, cs].astype(BF)
            bs_acc[:, cs] += dmix[rs, cs]
            gws_ref[g] += _dot_nt(dmb, vnb[rs, cs])
            dvn_scr[rs, cs] = _dot(wts[g].T.astype(BF), dmb)
        dvn = dvn_scr[...]
        glg_ref[...] += jnp.sum(dvn * xh, axis=0, keepdims=True)
        glb_ref[...] += jnp.sum(dvn, axis=0, keepdims=True)
        dxh = dvn * lg_ref[...]
        dvv = rstd * (dxh - jnp.mean(dxh, axis=-1, keepdims=True)
                      - xh * jnp.mean(dxh * xh, axis=-1, keepdims=True))
        dzs_ref[:, D_B:2 * D_B] = (dvv * dv).astype(BF)

        @pl.when(i == nt - 1)
        def _():
            cps = [pltpu.make_async_copy(acc_out, gwout_hbm, sems.at[0]),
                   pltpu.make_async_copy(acc_pa, gwpa_hbm, sems.at[1]),
                   pltpu.make_async_copy(acc_pb, gwpb_hbm, sems.at[2])]
            for cp in cps:
                cp.start()
            lane = lax.broadcasted_iota(jnp.int32, (SGU_CHUNK, 128), 1)
            cols = jnp.zeros((SGU_CHUNK, 128), F32)
            for g in range(N_GROUPS):
                gws_ref[g] = jnp.where(tri, gws_ref[g], 0.0)
                col = jnp.sum(bs_acc[:, g * 128:(g + 1) * 128], axis=-1, keepdims=True)
                cols = jnp.where(lane == g, col, cols)
            gbs_ref[...] = cols
            for cp in cps:
                cp.wait()

    c2 = lambda i: (0, 0)
    c3 = lambda i: (0, 0, 0)
    zcol = lambda w, blk: pl.BlockSpec((tm, w), lambda i: (i, blk))
    row = lambda w: pl.BlockSpec((tm, w), lambda i: (i, 0))
    return pl.pallas_call(
        body, name="tail", grid=(nt,),
        out_shape=(jax.ShapeDtypeStruct((S, D_MODEL), F32), jax.ShapeDtypeStruct((N_HEADS, S, HEAD_DIM), BF),
                   jax.ShapeDtypeStruct((S, 3072), BF), jax.ShapeDtypeStruct((S, 2 * D_B), BF),
                   jax.ShapeDtypeStruct((D_MODEL, D_MODEL), F32), jax.ShapeDtypeStruct((D_A, D_MODEL), F32),
                   jax.ShapeDtypeStruct((D_B, D_MODEL), F32),
                   jax.ShapeDtypeStruct((1, 2 * D_MODEL), F32), jax.ShapeDtypeStruct((1, D_MODEL), F32),
                   jax.ShapeDtypeStruct((1, 128), F32),
                   jax.ShapeDtypeStruct((N_GROUPS, 128, 128), F32), jax.ShapeDtypeStruct((SGU_CHUNK, 128), F32),
                   jax.ShapeDtypeStruct((1, D_B), F32), jax.ShapeDtypeStruct((1, D_B), F32)),
        in_specs=[row(D_A), zcol(512, 0), zcol(512, 1), zcol(512, 2), zcol(512, 3),
                  zcol(D_MODEL, 2), zcol(D_MODEL, 3), row(D_MODEL), row(D_MODEL),
                  pl.BlockSpec((D_A, D_MODEL), c2), pl.BlockSpec((D_B, D_MODEL), c2),
                  pl.BlockSpec((D_MODEL, D_MODEL), c2),
                  pl.BlockSpec((1, 2 * D_MODEL), c2), pl.BlockSpec((1, D_MODEL), c2),
                  pl.BlockSpec((1, D_B), c2), pl.BlockSpec((1, D_B), c2),
                  pl.BlockSpec((N_GROUPS, 128, 128), c3), pl.BlockSpec((128, N_GROUPS), c2)],
        out_specs=[row(D_MODEL), pl.BlockSpec((N_HEADS, tm, HEAD_DIM), lambda i: (0, i, 0)),
                   row(3072), row(2 * D_B), _ANY, _ANY, _ANY,
                   pl.BlockSpec((1, 2 * D_MODEL), c2), pl.BlockSpec((1, D_MODEL), c2),
                   pl.BlockSpec((1, 128), c2),
                   pl.BlockSpec((N_GROUPS, 128, 128), c3), pl.BlockSpec((SGU_CHUNK, 128), c2),
                   pl.BlockSpec((1, D_B), c2), pl.BlockSpec((1, D_B), c2)],
        scratch_shapes=[pltpu.VMEM((D_MODEL, D_MODEL), F32), pltpu.VMEM((D_A, D_MODEL), F32),
                        pltpu.VMEM((D_B, D_MODEL), F32),
                        pltpu.VMEM((tm, D_B), F32), pltpu.VMEM((tm, D_B), F32), pltpu.VMEM((tm, D_B), F32),
                        pltpu.VMEM((SGU_CHUNK, D_B), F32), pltpu.SemaphoreType.DMA((3,))],
        compiler_params=_params(58, dimension_semantics=("arbitrary",)),
    )(att, zrest, zrest, zrest, zrest, zrest, zrest, x, target, w_pa, w_pb, w_out, b_gate, final_g,
      ln_g, ln_b, w_s, b_s_t)


_DZ_MAP = ((0, 0), (1, 0), (2, 0), (3, 0), (4, 0), (4, 1), (3, 1), (3, 2), (3, 3), (3, 4), (3, 5))


def _dh_gradx(dq, dk, dv, dzt, dzs, w_in_bf, x, norm_g, d_out, tm=512, after=()):
    S = x.shape[0]

    def body(dq_ref, dk_ref, dv_ref, dzt_ref, dzs_ref, w_ref, x_ref, g_ref, dout_ref, gx_ref, gn_ref):
        i = pl.program_id(0)

        @pl.when(i == 0)
        def _():
            gn_ref[...] = jnp.zeros_like(gn_ref)

        pieces = (dq_ref, dk_ref, dv_ref, dzt_ref, dzs_ref)
        dh = jnp.zeros((tm, D_MODEL), F32)
        for j, (pc, blk) in enumerate(_DZ_MAP):
            dh += _dot_nt(pieces[pc][:, blk * 512:(blk + 1) * 512], w_ref[:, j * 512:(j + 1) * 512])
        xv = x_ref[...]
        r = lax.rsqrt(jnp.mean(xv * xv, axis=-1, keepdims=True) + EPS)
        nrm = xv * r
        gn_ref[...] += jnp.sum(dh * nrm, axis=0, keepdims=True)
        dn = dh * g_ref[...]
        gx_ref[...] = r * (dn - nrm * jnp.mean(dn * nrm, axis=-1, keepdims=True)) + dout_ref[...]

    row = lambda w: pl.BlockSpec((tm, w), lambda i: (i, 0))
    c2 = lambda i: (0, 0)
    return pl.pallas_call(
        _after(body, 9, after), name="dh_gradx", grid=(S // tm,),
        out_shape=(jax.ShapeDtypeStruct((S, D_MODEL), F32), jax.ShapeDtypeStruct((1, D_MODEL), F32)),
        in_specs=[row(512), row(512), row(512), row(3072), row(1024),
                  pl.BlockSpec((D_MODEL, D_IN), c2, pipeline_mode=pl.Buffered(1)), row(D_MODEL),
                  pl.BlockSpec((1, D_MODEL), c2), row(D_MODEL)]
        + [_ANY] * len(after),
        out_specs=[row(D_MODEL), pl.BlockSpec((1, D_MODEL), c2)],
        compiler_params=_params(48, dimension_semantics=("arbitrary",)),
    )(dq, dk, dv, dzt, dzs, w_in_bf, x, norm_g, d_out, *after)


def _gw_in(ht, dq, dk, dv, dzt, dzs, tn=512, after=()):
    S = ht.shape[1]
    per = 512 // tn
    cols = tuple((pc, per * blk + h) for pc, blk in _DZ_MAP for h in range(per))

    def body(ht_ref, dq_ref, dk_ref, dv_ref, dzt_ref, dzs_ref, o_ref, ob_ref):
        j = pl.program_id(0)
        pieces = (dq_ref, dk_ref, dv_ref, dzt_ref, dzs_ref)
        for pc in range(5):
            hit = functools.reduce(jnp.logical_or, [j == jj for jj, (p, _) in enumerate(cols) if p == pc])

            @pl.when(hit)
            def _(pc=pc):
                g = _dot(ht_ref[...], pieces[pc][...])
                o_ref[...] = g
                ob_ref[...] = g.astype(BF)

    def piece_spec(pc):
        cur = next(blk for p, blk in cols if p == pc)
        held = []
        for p, blk in cols:
            cur = blk if p == pc else cur
            held.append(cur)

        def index_map(j):
            blk = jnp.int32(held[0])
            for jj in range(1, len(held)):
                if held[jj] != held[jj - 1]:
                    blk = jnp.where(j >= jj, jnp.int32(held[jj]), blk)
            return (0, blk)

        return pl.BlockSpec((S, tn), index_map)

    return pl.pallas_call(
        _after(body, 6, after), name="gw_in", grid=(len(cols),),
        out_shape=(jax.ShapeDtypeStruct((D_MODEL, D_IN), F32), jax.ShapeDtypeStruct((D_MODEL, D_IN), BF)),
        in_specs=[pl.BlockSpec((D_MODEL, S), lambda j: (0, 0), pipeline_mode=pl.Buffered(1))]
        + [piece_spec(pc) for pc in range(5)]
        + [_ANY] * len(after),
        out_specs=[pl.BlockSpec((D_MODEL, tn), lambda j: (0, j)), pl.BlockSpec((D_MODEL, tn), lambda j: (0, j))],
        compiler_params=_params(56, dimension_semantics=("arbitrary",)),
    )(ht, dq, dk, dv, dzt, dzs, *after)


_HBM = pl.BlockSpec(memory_space=pltpu.HBM)
_SEM = pl.BlockSpec(memory_space=pltpu.SEMAPHORE)
_ANY = pl.BlockSpec(memory_space=pl.ANY)
_EFFECT = pltpu.SideEffectType.DATAFLOW_SIDE_EFFECTING


def _in_hbm(a):
    return pltpu.with_memory_space_constraint(a, pltpu.HBM)


def _after(body, n_in, after):
    if not after:
        return body
    return lambda *refs: body(*refs[:n_in], *refs[n_in + len(after):])


class _Started:
    def __init__(self, send, recv, bufs, token):
        self.send, self.recv, self.bufs, self.token = send, recv, bufs, token


_PEER_SETS = {"sibling": 7, "chips": 8, "both": 9}


def _peers(kind):
    x, y, c, chips = _mesh_pos()
    return ([(x, y, 1 - c)] if kind in ("sibling", "both") else []) + (
        [(cx, cy, c) for cx, cy in chips] if kind in ("chips", "both") else [])


def _signal_peers(kind):
    barrier = pltpu.get_barrier_semaphore()
    targets = _peers(kind)
    for peer in targets:
        pl.semaphore_signal(barrier, inc=1, device_id=peer, device_id_type=MESH)
    return lambda: pl.semaphore_wait(barrier, len(targets))


def _split_start(name, bufs, n_copies, copies, peers, after=()):
    nb = len(bufs)

    def body(*refs):
        _signal_peers(peers)()
        refs = refs[:nb] + refs[nb + len(after):]
        for cp in copies(refs[:nb], refs[nb], refs[nb + 1]):
            cp.start()
        refs[-1][...] = jnp.zeros_like(refs[-1])

    outs = pl.pallas_call(
        body, name=name,
        out_shape=(pltpu.SemaphoreType.DMA((n_copies,)), pltpu.SemaphoreType.DMA((n_copies,)),
                   *[pltpu.HBM(b.shape, b.dtype) for b in bufs], jax.ShapeDtypeStruct((8, 128), F32)),
        in_specs=[_HBM] * nb + [_ANY] * len(after),
        out_specs=(_SEM, _SEM, *[_HBM] * nb, pl.BlockSpec(memory_space=pltpu.VMEM)),
        input_output_aliases={k: 2 + k for k in range(nb)},
        compiler_params=_params(1, has_side_effects=_EFFECT, collective_id=_PEER_SETS[peers]),
    )(*[_in_hbm(b) for b in bufs], *after)
    return _Started(outs[0], outs[1], list(outs[2:2 + nb]), outs[-1])


def _split_wait(name, started, copies, after):
    nb = len(started.bufs)

    def body(*refs):
        for cp in copies(refs[:nb], refs[nb], refs[nb + 1]):
            cp.wait_send()
            cp.wait_recv()

    return list(pl.pallas_call(
        body, name=name,
        out_shape=tuple(pltpu.HBM(b.shape, b.dtype) for b in started.bufs),
        in_specs=[_HBM] * nb + [_SEM, _SEM, _ANY],
        out_specs=tuple([_HBM] * nb),
        input_output_aliases={k: k for k in range(nb)},
        compiler_params=_params(1, has_side_effects=_EFFECT),
    )(*started.bufs, started.send, started.recv, after))


def _x1_copies(ws):
    def copies(refs, send_sems, recv_sems):
        x, y, c, _ = _mesh_pos()
        out = []
        for k, w in enumerate(ws):
            for s in range(N_SHARD):
                out.append(pltpu.make_async_remote_copy(
                    src_ref=_UNITS[w](refs[k], s, 1 - c), dst_ref=refs[len(ws) + k].at[s],
                    send_sem=send_sems.at[N_SHARD * k + s], recv_sem=recv_sems.at[N_SHARD * k + s],
                    device_id=(x, y, 1 - c), device_id_type=MESH))
        return out
    return copies


def _x2_copies(n):
    def copies(refs, send_sems, recv_sems):
        x, y, c, chips = _mesh_pos()
        out = []
        for j, (cx, cy) in enumerate(chips):
            for k in range(n):
                out.append(pltpu.make_async_remote_copy(
                    src_ref=refs[k].at[2 * cx + cy], dst_ref=refs[n + k].at[j],
                    send_sem=send_sems.at[3 * k + j], recv_sem=recv_sems.at[3 * k + j],
                    device_id=(cx, cy, c), device_id_type=MESH))
        return out
    return copies


def _x3_copies(ws):
    def copies(refs, send_sems, recv_sems):
        x, y, c, _ = _mesh_pos()
        out = []
        for k, w in enumerate(ws):
            rows = _HALF_ROWS[w]
            mine = refs[k].at[pl.ds(_mo(c * rows, rows), rows), :]
            out.append(pltpu.make_async_remote_copy(
                src_ref=mine, dst_ref=mine, send_sem=send_sems.at[k], recv_sem=recv_sems.at[k],
                device_id=(x, y, 1 - c), device_id_type=MESH))
        return out
    return copies


def _x1_lands(ws, dtype=F32):
    return [lax.empty((N_SHARD,) + _UNIT_SHAPES[w], dtype) for w in ws]


def _x2_lands(ws):
    return [lax.empty((3,) + _UNIT_SHAPES[w], BF) for w in ws]


def _grad_add1(w, g, recv, pos):
    ur, uc = _UNIT_SHAPES[w]

    def body(pos_ref, g_ref, r_ref, csb_ref):
        csb_ref[0] = (g_ref[...] + r_ref[0].astype(F32)).astype(BF)

    u3 = lambda k, pos: (pos[2 + k], 0, 0)
    return pl.pallas_call(
        body, name=f"grad_add1_{w}",
        grid_spec=pltpu.PrefetchScalarGridSpec(
            num_scalar_prefetch=1, grid=(N_SHARD - 1,),
            in_specs=[pl.BlockSpec((ur, uc), lambda k, pos: (pos[0], pos[2 + k])), pl.BlockSpec((1, ur, uc), u3)],
            out_specs=pl.BlockSpec((1, ur, uc), u3)),
        out_shape=jax.ShapeDtypeStruct((N_SHARD, ur, uc), BF),
        compiler_params=_params(40, dimension_semantics=("arbitrary",)),
    )(pos, g, recv)


def _grad_add1_group(ws, gs, recvs):
    n = len(ws)

    def body(*refs):
        c = lax.axis_index("c")
        for k, w in enumerate(ws):
            g, r, cs, csb = refs[k], refs[n + k], refs[2 * n + k], refs[3 * n + k]
            for s in range(N_SHARD):
                v = _UNITS[w](g, s, c)[...] + r[s]
                cs[s] = v
                csb[s] = v.astype(BF)

    vm = pl.BlockSpec(memory_space=pltpu.VMEM)
    outs = pl.pallas_call(
        body, name="grad_add1_group",
        out_shape=tuple(jax.ShapeDtypeStruct((N_SHARD,) + _UNIT_SHAPES[w], dt) for dt in (F32, BF) for w in ws),
        in_specs=[vm] * (2 * n), out_specs=[vm] * (2 * n),
        compiler_params=_params(32),
    )(*gs, *recvs)
    return list(outs[:n]), list(outs[n:])


def _grad_add2_group(ws, css, recvs):
    n = len(ws)

    def body(*refs):
        x, y, c, _ = _mesh_pos()
        for k, w in enumerate(ws):
            cs, r, o = refs[k], refs[n + k], refs[2 * n + k]
            rows = _HALF_ROWS[w]
            total = ((cs[2 * x + y] + r[0].astype(F32)) + r[1].astype(F32)) + r[2].astype(F32)
            o[pl.ds(_mo(c * rows, rows), rows), :] = total

    vm = pl.BlockSpec(memory_space=pltpu.VMEM)
    return list(pl.pallas_call(
        body, name="grad_add2_group",
        out_shape=tuple(jax.ShapeDtypeStruct(_SHARD_SHAPES[w], F32) for w in ws),
        in_specs=[vm] * (2 * n), out_specs=[vm] * n,
        compiler_params=_params(32),
    )(*css, *recvs))


def _grad_add2(w, g, recv1, recv2, pos):
    ur, uc = _UNIT_SHAPES[w]
    nt = 4
    tr = ur // nt

    def body(pos_ref, g_ref, r1_ref, r2_ref, o_ref):
        own = g_ref[...] + r1_ref[0].astype(F32)
        o_ref[...] = ((own + r2_ref[0].astype(F32)) + r2_ref[1].astype(F32)) + r2_ref[2].astype(F32)

    mine = lambda t, pos: (pos[0] * nt + t, 0)
    return pl.pallas_call(
        body, name=f"grad_add2_{w}",
        grid_spec=pltpu.PrefetchScalarGridSpec(
            num_scalar_prefetch=1, grid=(nt,),
            in_specs=[pl.BlockSpec((tr, uc), lambda t, pos: (pos[0] * nt + t, pos[1])),
                      pl.BlockSpec((1, tr, uc), lambda t, pos: (pos[1], t, 0)),
                      pl.BlockSpec((3, tr, uc), lambda t, pos: (0, t, 0))],
            out_specs=pl.BlockSpec((tr, uc), mine)),
        out_shape=jax.ShapeDtypeStruct(_SHARD_SHAPES[w], F32),
        compiler_params=_params(32, dimension_semantics=("arbitrary",)),
    )(pos, g, recv1, recv2)


def _adamw_math(w, g, m, v):
    m = ADAM_B1 * m + (1.0 - ADAM_B1) * g
    v = ADAM_B2 * v + (1.0 - ADAM_B2) * (g * g)
    m_hat = m / ADAM_C1
    v_hat = v / ADAM_C2
    delta = -ADAM_LR * (m_hat / (jnp.sqrt(v_hat) + ADAM_EPS) + ADAM_WD * w)
    return delta, m, v


ADAMW_STEPS = 4


def _adamw(ws_, gs, ms, vs):
    n = len(ws_)

    def body(*refs):
        for k in range(n):
            w, g, m, v = (refs[j * n + k] for j in range(4))
            d, nm, nv, gc = (refs[(4 + j) * n + k] for j in range(4))
            gv = g[...]
            d[...], nm[...], nv[...] = _adamw_math(w[...], gv, m[...], v[...])
            gc[...] = gv

    specs = [pl.BlockSpec((a.shape[0] // ADAMW_STEPS, a.shape[1]), lambda i: (i, 0)) for a in ws_] * 4
    outs = pl.pallas_call(
        body, name="adamw", grid=(ADAMW_STEPS,),
        out_shape=tuple(jax.ShapeDtypeStruct(a.shape, F32) for _ in range(4) for a in ws_),
        in_specs=specs, out_specs=specs,
        compiler_params=_params(40, dimension_semantics=("arbitrary",)),
    )(*ws_, *gs, *ms, *vs)
    return [tuple(outs[j * n + k] for j in range(4)) for k in range(n)]


_REL_PAD = 384
_VEC_FIELDS = (("norm_g", 0, D_MODEL), ("b_gate", 1024, 2 * D_MODEL), ("sgu_ln_g", 3072, D_B),
               ("sgu_ln_b", 3584, D_B), ("b_s", 4096, N_GROUPS * 128), ("final_g", 4608, D_MODEL))
_LOSS_OFF = 5632
_REL_OFF = 5760
_NV = _REL_OFF + N_HEADS * _REL_PAD
_N_FIELDS = len(_VEC_FIELDS) + 2


_B_S_FIELD = [f[0] for f in _VEC_FIELDS].index("b_s")


def _assemble_row(dst, fields, transposed_b_s):
    for f, (_, off, n) in enumerate(_VEC_FIELDS):
        if transposed_b_s and f == _B_S_FIELD:
            t = fields[f][...].T
            for g in range(N_GROUPS):
                dst[:, off + 128 * g:off + 128 * (g + 1)] = t[g:g + 1, :]
        else:
            dst[:, off:off + n] = fields[f][...]
    for r in range(N_HEADS):
        dst[:, _REL_OFF + _REL_PAD * r:_REL_OFF + _REL_PAD * (r + 1)] = fields[len(_VEC_FIELDS)][r:r + 1, :]


def _small_reduce(grads, loss_row, after=()):
    n_in = _N_FIELDS + 1

    def body(*refs):
        g_refs, loss_ref = refs[:_N_FIELDS], refs[_N_FIELDS]
        out_v, out_w = refs[n_in:n_in + 2]
        mine_v, mine_w, gath_v, gath_w, send_sems, recv_sems = refs[n_in + 2:]
        x, y, c, chips = _mesh_pos()
        me, sibling = (x, y, c), (x, y, 1 - c)

        peers_entered = _signal_peers("both")
        _assemble_row(mine_v, g_refs, True)
        mine_v[:, _LOSS_OFF:_LOSS_OFF + 128] = loss_ref[...]
        mine_w[...] = g_refs[-1][...].astype(BF)
        peers_entered()
        my_k = 4 * x + 2 * y + c
        gath_v[my_k] = mine_v[...]
        gath_w[my_k] = mine_w[...]

        def copy(k, gath, block, to, src=None):
            dst = gath.at[4 * block[0] + 2 * block[1] + block[2]]
            return pltpu.make_async_remote_copy(
                src_ref=dst if src is None else src, dst_ref=dst,
                send_sem=send_sems.at[k], recv_sem=recv_sems.at[k], device_id=to, device_id_type=MESH)

        bufs = ((gath_v, mine_v), (gath_w, mine_w))
        first, passed = [], []
        for b, (gath, mine) in enumerate(bufs):
            first.append(copy(7 * b, gath, me, sibling, src=mine))
            first += [copy(7 * b + 1 + j, gath, me, (*chip, c), src=mine) for j, chip in enumerate(chips)]
        for cp in first:
            cp.start()
        for b, (gath, _) in enumerate(bufs):
            for j, chip in enumerate(chips):
                copy(7 * b + 1 + j, gath, (*chip, c), me).wait_recv()
                cp = copy(7 * b + 4 + j, gath, (*chip, c), sibling)
                cp.start()
                passed.append(cp)
        for b, (gath, _) in enumerate(bufs):
            copy(7 * b, gath, sibling, me).wait_recv()
            for j, chip in enumerate(chips):
                copy(7 * b + 4 + j, gath, (*chip, 1 - c), me).wait_recv()
        for cp in first + passed:
            cp.wait_send()

        tot_v, tot_w = gath_v[0], gath_w[0].astype(F32)
        for k in range(1, 8):
            tot_v = tot_v + gath_v[k]
            tot_w = tot_w + gath_w[k].astype(F32)
        out_v[...] = tot_v
        out_w[...] = tot_w

    vm = pl.BlockSpec(memory_space=pltpu.VMEM)
    return pl.pallas_call(
        _after(body, n_in, after), name="small_reduce",
        out_shape=(jax.ShapeDtypeStruct((1, _NV), F32), jax.ShapeDtypeStruct((N_GROUPS * 128, 128), F32)),
        in_specs=[vm] * n_in + [_ANY] * len(after), out_specs=[vm] * 2,
        scratch_shapes=[pltpu.VMEM((1, _NV), F32), pltpu.VMEM((N_GROUPS * 128, 128), BF),
                        pltpu.VMEM((8, 1, _NV), F32), pltpu.VMEM((8, N_GROUPS * 128, 128), BF),
                        pltpu.SemaphoreType.DMA((14,)), pltpu.SemaphoreType.DMA((14,))],
        compiler_params=_params(32, collective_id=_PEER_SETS["both"]),
    )(*grads, loss_row, *after)


def _small_adamw(tot_v, tot_w, params):
    n_in = 2 + 3 * _N_FIELDS

    def body(*refs):
        tv_ref, tw_ref = refs[:2]
        p_refs = [refs[2 + k * _N_FIELDS:2 + (k + 1) * _N_FIELDS] for k in range(3)]
        outs = refs[n_in:n_in + 4 * _N_FIELDS + 1]
        wmv = refs[-1]
        for k in range(3):
            _assemble_row(wmv.at[k], p_refs[k], False)
            wmv[k, :, _LOSS_OFF:_LOSS_OFF + 128] = jnp.zeros((1, 128), F32)
        tot_v, tot_w = tv_ref[...], tw_ref[...]
        res_v = (tot_v,) + _adamw_math(wmv[0], tot_v, wmv[1], wmv[2])
        res_w = (tot_w,) + _adamw_math(p_refs[0][-1][...], tot_w, p_refs[1][-1][...], p_refs[2][-1][...])
        for kind in range(4):
            o = outs[kind * _N_FIELDS:(kind + 1) * _N_FIELDS]
            for f, (_, off, n) in enumerate(_VEC_FIELDS):
                o[f][...] = res_v[kind][:, off:off + n]
            for r in range(N_HEADS):
                o[len(_VEC_FIELDS)][r:r + 1, :] = res_v[kind][:, _REL_OFF + _REL_PAD * r:_REL_OFF + _REL_PAD * (r + 1)]
            o[-1][...] = res_w[kind]
        outs[-1][...] = tot_v[:, _LOSS_OFF:_LOSS_OFF + 128]

    field_shapes = [(1, n) for _, _, n in _VEC_FIELDS] + [(N_HEADS, _REL_PAD), (N_GROUPS * 128, 128)]
    vm = pl.BlockSpec(memory_space=pltpu.VMEM)
    operands = [tot_v, tot_w] + [a for p in params for a in p]
    assert len(operands) == n_in
    outs = pl.pallas_call(
        body, name="small_adamw",
        out_shape=tuple(jax.ShapeDtypeStruct(s, F32) for _ in range(4) for s in field_shapes)
        + (jax.ShapeDtypeStruct((1, 128), F32),),
        in_specs=[vm] * n_in, out_specs=[vm] * (4 * _N_FIELDS + 1),
        scratch_shapes=[pltpu.VMEM((3, 1, _NV), F32)],
        compiler_params=_params(32),
    )(*operands)
    return [outs[k * _N_FIELDS:(k + 1) * _N_FIELDS] for k in range(4)], outs[-1]


def _small_fields(norm_g, b_gate, ln_g, ln_b, b_s, final_g, rel_bias, w_s):
    rel = jnp.pad(rel_bias.reshape(N_HEADS, N_REL), ((0, 0), (0, _REL_PAD - N_REL)))
    return (norm_g, b_gate, ln_g, ln_b, b_s.reshape(1, N_GROUPS * 128), final_g.reshape(1, D_MODEL),
            rel, w_s.reshape(N_GROUPS * 128, 128))


def _small_outputs(fields):
    n_g, b_g, l_g, l_b, b_s, f_g, rel, w_s = fields
    return (n_g, b_g, rel[:, :N_REL].reshape(1, N_HEADS, N_REL), l_g, l_b,
            w_s.reshape(1, N_GROUPS, 128, 128), b_s.reshape(1, N_GROUPS, 128), f_g.reshape(D_MODEL))


def _bias_row(rel_bias):
    hi = rel_bias[:, N_REL - 1:N_REL]
    lo = rel_bias[:, 0:1]
    return jnp.concatenate([jnp.broadcast_to(hi, (N_HEADS, 384)), rel_bias[:, ::-1],
                            jnp.broadcast_to(lo, (N_HEADS, 191)), jnp.broadcast_to(hi, (N_HEADS, 192))], axis=1)


def kernel(x, norm_g, w_in, b_gate, rel_bias, sgu_ln_g, sgu_ln_b, w_s, b_s, w_pa, w_pb, w_out, final_g, loss_target, m_norm_g, m_w_in, m_b_gate, m_rel_bias, m_sgu_ln_g, m_sgu_ln_b, m_w_s, m_b_s, m_w_pa, m_w_pb, m_w_out, m_final_g, v_norm_g, v_w_in, v_b_gate, v_rel_bias, v_sgu_ln_g, v_sgu_ln_b, v_w_s, v_b_s, v_w_pa, v_w_pb, v_w_out, v_final_g):
    S = x.shape[1]
    xs = x.reshape(S, D_MODEL)
    tgt = loss_target.reshape(S, D_MODEL)
    big_w = (w_in[0], w_pa[0], w_pb[0], w_out[0])
    big_m = (m_w_in[0], m_w_pa[0], m_w_pb[0], m_w_out[0])
    big_v = (v_w_in[0], v_w_pa[0], v_w_pb[0], v_w_out[0])
    rel = rel_bias[0]
    ws = w_s[0]
    bst = b_s[0].T
    fg = final_g.reshape(1, D_MODEL)
    chip = 2 * lax.axis_index("x") + lax.axis_index("y")
    pos = jnp.stack([lax.axis_index("c"), chip] + [(chip + k) % N_SHARD for k in range(1, N_SHARD)]).astype(jnp.int32)

    (w_in_bf,), staged, band_bias = _ag_weights((0,), big_w[:1], (1, 2, 3), big_w[1:], _bias_row(rel))
    ag_s = _split_start("ag_small_start", staged, 9, _gather_copies((1, 2, 3)), "chips", after=(w_in_bf,))

    ht, q3, k3, v3, zrest = _inproj_fwd(xs, norm_g, w_in_bf, after=(ag_s.token,))
    att, lse = _attn_fwd(q3, k3, v3, band_bias)
    w_pa_bf, w_pb_bf, w_out_bf = _split_wait("ag_small_wait", ag_s, _gather_copies((1, 2, 3)), att)
    (d_out, d_att, dzt, dzs, gw_out, gw_pa, gw_pb, g_bgate, g_final, loss_row,
     g_ws, g_bs_t, g_lng, g_lnb) = _tail_sgu(
        att, zrest, xs, tgt, w_pa_bf, w_pb_bf, w_out_bf, b_gate, fg, sgu_ln_g, sgu_ln_b, ws, bst)
    ws_s, ws_i = (1, 2, 3), (0,)

    x1s = _split_start("gx1s_start", [gw_pa, gw_pb, gw_out] + _x1_lands(ws_s), 12, _x1_copies(ws_s), "sibling")
    dq, dk, dv, d_gp = _attn_bwd(q3, k3, v3, d_att, lse, band_bias, after=(x1s.token,))
    got = _split_wait("gx1s_wait", x1s, _x1_copies(ws_s), dq)
    cs_s, csb_s = _grad_add1_group(ws_s, got[:3], got[3:])

    x2s = _split_start("gx2s_start", csb_s + _x2_lands(ws_s), 9, _x2_copies(3), "chips")
    gw_in, gw_in_bf = _gw_in(ht, dq, dk, dv, dzt, dzs, after=(x2s.token,))
    x1i = _split_start("gx1i_start", [gw_in_bf] + _x1_lands(ws_i, BF), 4, _x1_copies(ws_i), "sibling")
    got = _split_wait("gx2s_wait", x2s, _x2_copies(3), x1i.token)
    halves_s = _grad_add2_group(ws_s, cs_s, got[3:])
    x3s = _split_start("gx3s_start", halves_s, 3, _x3_copies(ws_s), "sibling")
    recv1_i = _split_wait("gx1i_wait", x1i, _x1_copies(ws_i), x3s.token)[1]
    csb_i = _grad_add1(0, gw_in, recv1_i, pos)

    x2i = _split_start("gx2i_start", [csb_i] + _x2_lands(ws_i), 3, _x2_copies(1), "chips")
    grad_x, g_norm = _dh_gradx(dq, dk, dv, dzt, dzs, w_in_bf, xs, norm_g, d_out, after=(x2i.token,))
    g_shards_s = _split_wait("gx3s_wait", x3s, _x3_copies(ws_s), grad_x)
    got = _split_wait("gx2i_wait", x2i, _x2_copies(1), grad_x)
    half_i = _grad_add2(0, gw_in, recv1_i, got[1], pos)
    x3i = _split_start("gx3i_start", [half_i], 1, _x3_copies(ws_i), "sibling")

    g_rel = jnp.pad(d_gp[:, 384:384 + N_REL][:, ::-1], ((0, 0), (0, _REL_PAD - N_REL)))
    small_grads = (g_norm, g_bgate, g_lng, g_lnb, g_bs_t, g_final, g_rel, g_ws.reshape(N_GROUPS * 128, 128))
    small_params = (_small_fields(norm_g, b_gate, sgu_ln_g, sgu_ln_b, b_s, final_g, rel_bias, w_s),
                    _small_fields(m_norm_g, m_b_gate, m_sgu_ln_g, m_sgu_ln_b, m_b_s, m_final_g, m_rel_bias, m_w_s),
                    _small_fields(v_norm_g, v_b_gate, v_sgu_ln_g, v_sgu_ln_b, v_b_s, v_final_g, v_rel_bias, v_w_s))
    tot_v, tot_w = _small_reduce(small_grads, loss_row, after=(x3i.token,))
    (gsum, sdelta, sm, sv), loss_out = _small_adamw(tot_v, tot_w, small_params)

    g_shard_i, = _split_wait("gx3i_wait", x3i, _x3_copies(ws_i), loss_out)
    big = _adamw(big_w, [g_shard_i] + g_shards_s, big_m, big_v)
    sg_out, sd_out, sm_out, sv_out = (_small_outputs(f) for f in (gsum, sdelta, sm, sv))
    loss = loss_out[0, 0]

    def assemble(small, bigs):
        n_g, b_g, r_b, l_g, l_b, w_s_, b_s_, f_g = small
        b_in, b_pa, b_pb, b_out = (b[None] for b in bigs)
        return (n_g, b_in, b_g, r_b, l_g, l_b, w_s_, b_s_, b_pa, b_pb, b_out, f_g)

    grads_out = assemble(sg_out, [b[3] for b in big])
    delta_out = assemble(sd_out, [b[0] for b in big])
    m_out = assemble(sm_out, [b[1] for b in big])
    v_out = assemble(sv_out, [b[2] for b in big])
    return (loss, grad_x.reshape(1, S, D_MODEL), *grads_out, *delta_out, *m_out, *v_out)
```

```python
import functools
import math

import jax
import jax.numpy as jnp
from jax import lax
from jax.experimental import pallas as pl
from jax.experimental.pallas import tpu as pltpu

F32 = jnp.float32
BF = jnp.bfloat16
MESH = pl.DeviceIdType.MESH

D_MODEL = 1024
D_A = 512
D_B = 512
D_IN = 5632
N_HEADS = 8
HEAD_DIM = 64
CHUNK = 64
N_PREV = 8
SGU_CHUNK = 128
N_GROUPS = 4
N_REL = 257
EPS = 1e-6
NEG_INF = -1e30
SCALE = HEAD_DIM ** -0.5

QB = 2 * CHUNK
KB = (N_PREV + 2) * CHUNK
PADK = N_PREV * CHUNK
ROLL_W = 1024
KEEP = KB // QB - 1
Q_PER_STEP = 2

ADAM_LR = 0.001
ADAM_B1 = 0.9
ADAM_B2 = 0.999
ADAM_EPS = 1e-08
ADAM_WD = 0.01
ADAM_STEP = 10
ADAM_C1 = 1.0 - ADAM_B1 ** ADAM_STEP
ADAM_C2 = 1.0 - ADAM_B2 ** ADAM_STEP

AG_PIECES = 4
N_SHARD = 4
SHARD_IN = D_IN // N_SHARD
MIB = 1024 * 1024


V7X_VMEM_MIB = 64
VMEM_RESERVE_MIB = V7X_VMEM_MIB - 4


def _params(vmem_mib, **kw):
    assert vmem_mib <= VMEM_RESERVE_MIB
    return pltpu.CompilerParams(vmem_limit_bytes=VMEM_RESERVE_MIB * MIB, **kw)


def _sigmoid(x):
    return 1.0 / (1.0 + jnp.exp(-x))


def _silu_and_grad(x):
    s = _sigmoid(x)
    return x * s, s * (1.0 + x * (1.0 - s))


_GELU_C = math.sqrt(2.0 / math.pi)
_GELU_A = 0.044715


def _gelu_and_grad(x):
    x2 = x * x
    t = jnp.tanh(_GELU_C * (x + _GELU_A * (x2 * x)))
    cdf = 0.5 * (1.0 + t)
    grad = cdf + 0.5 * x * (1.0 - t * t) * (_GELU_C * (1.0 + 3.0 * _GELU_A * x2))
    return x * cdf, grad


def _dot(a, b):
    return jnp.dot(a, b, preferred_element_type=F32)


def _dot_nt(a, b):
    return lax.dot_general(a, b, (((1,), (1,)), ((), ())), preferred_element_type=F32)


def _dot_tn(a, b):
    return lax.dot_general(a, b, (((0,), (0,)), ((), ())), preferred_element_type=F32)


def _mo(v, m):
    return v if isinstance(v, int) else pl.multiple_of(v, m)


def _unit_in(ref, s, p):
    return ref.at[pl.ds(_mo(p * 512, 512), 512), pl.ds(_mo(s * SHARD_IN, 128), SHARD_IN)]


def _unit_p(ref, s, p):
    return ref.at[pl.ds(_mo(p * 256, 256), 256), pl.ds(_mo(s * 256, 128), 256)]


def _unit_out(ref, s, p):
    return ref.at[pl.ds(_mo(s * 256 + p * 128, 128), 128), :]


_UNITS = (_unit_in, _unit_p, _unit_p, _unit_out)
_HALF_ROWS = (512, 256, 256, 128)
_UNIT_SHAPES = ((512, SHARD_IN), (256, 256), (256, 256), (128, D_MODEL))
_FULL_SHAPES = ((D_MODEL, D_IN), (D_A, D_MODEL), (D_B, D_MODEL), (D_MODEL, D_MODEL))
_SHARD_SHAPES = ((D_MODEL, SHARD_IN), (D_A, 256), (D_B, 256), (256, D_MODEL))


def _mesh_pos():
    x, y, c = lax.axis_index("x"), lax.axis_index("y"), lax.axis_index("c")
    chips = [(1 - x, y), (x, 1 - y), (1 - x, 1 - y)]
    return x, y, c, chips


def _ag_weights(ws, shards, later_ws, later_shards, gp):
    n, m = len(ws), len(later_ws)

    def body(*refs):
        ins, later_ins, gp_ref = refs[:n], refs[n:n + m], refs[n + m]
        o = n + m + 1
        outs, later_outs, bias_ref = refs[o:o + n], refs[o + n:o + n + m], refs[o + n + m]
        o += n + m + 1
        stage, later_stage = refs[o:o + n], refs[o + n:o + n + m]
        send_sems, recv_sems, local_sems, later_sems = refs[o + n + m:]
        x, y, c, chips = _mesh_pos()
        s_me = 2 * x + y
        sibling = (x, y, 1 - c)
        def rows_of(k, p):
            rows = _HALF_ROWS[ws[k]]
            return pl.ds(_mo(p * rows, rows), rows)

        def half(k, p):
            return stage[k].at[rows_of(k, p), :]

        def unit(k, s, p):
            return _UNITS[ws[k]](outs[k], s, p)

        def rcopy(k, i, src, dst, to):
            return pltpu.make_async_remote_copy(src_ref=src, dst_ref=dst, send_sem=send_sems.at[k, i],
                                                recv_sem=recv_sems.at[k, i], device_id=to, device_id_type=MESH)

        peers_entered = _signal_peers("both")
        for k in range(n):
            stage[k][rows_of(k, c), :] = ins[k][rows_of(k, c), :].astype(BF)
        peers_entered()
        def piece(ref, k, q):
            rows = _HALF_ROWS[ws[k]] // AG_PIECES
            return ref.at[pl.ds(q * rows, rows), :]

        sends = []
        for q in range(AG_PIECES):
            for j, (cx, cy) in enumerate(chips):
                for k in range(n):
                    cp = rcopy(k, j * AG_PIECES + q, piece(half(k, c), k, q), piece(unit(k, s_me, c), k, q),
                               (cx, cy, c))
                    cp.start()
                    sends.append(cp)
        for k in range(n):
            stage[k][rows_of(k, 1 - c), :] = ins[k][rows_of(k, 1 - c), :].astype(BF)
        local = []
        for k in range(n):
            for p in range(2):
                cp = pltpu.make_async_copy(half(k, p), unit(k, s_me, p), local_sems.at[k, p])
                cp.start()
                local.append(cp)
        for k, w in enumerate(later_ws):
            later_stage[k][...] = later_ins[k][...].astype(BF)
            cp = pltpu.make_async_copy(later_stage[k], _shard_of(later_outs[k], w, s_me), later_sems.at[k])
            cp.start()
            local.append(cp)
        keep = _struct_mask()
        for h in range(N_HEADS):
            bias_ref[h] = jnp.where(keep, _skew_table(gp_ref[h:h + 1, :])[:, :KB], NEG_INF)
        for q in range(AG_PIECES):
            for j, (cx, cy) in enumerate(chips):
                for k in range(n):
                    landed = piece(unit(k, 2 * cx + cy, c), k, q)
                    rcopy(k, j * AG_PIECES + q, landed, landed, (cx, cy, c)).wait_recv()
                    cp = rcopy(k, (3 + j) * AG_PIECES + q, landed, landed, sibling)
                    cp.start()
                    sends.append(cp)
        for q in range(AG_PIECES):
            for j, (cx, cy) in enumerate(chips):
                for k in range(n):
                    other = piece(unit(k, 2 * cx + cy, 1 - c), k, q)
                    rcopy(k, (3 + j) * AG_PIECES + q, other, other, sibling).wait_recv()
        for cp in sends:
            cp.wait_send()
        for cp in local:
            cp.wait()

    vm = pl.BlockSpec(memory_space=pltpu.VMEM)
    outs = pl.pallas_call(
        body, name="ag_weights",
        out_shape=tuple(jax.ShapeDtypeStruct(_FULL_SHAPES[w], BF) for w in tuple(ws) + tuple(later_ws))
        + (jax.ShapeDtypeStruct((N_HEADS, QB, KB), F32),),
        in_specs=[vm] * (n + m + 1), out_specs=[_ANY] * (n + m) + [vm],
        scratch_shapes=[pltpu.VMEM(_SHARD_SHAPES[w], BF) for w in tuple(ws) + tuple(later_ws)]
        + [pltpu.SemaphoreType.DMA((n, 6 * AG_PIECES)), pltpu.SemaphoreType.DMA((n, 6 * AG_PIECES)),
           pltpu.SemaphoreType.DMA((n, 2)), pltpu.SemaphoreType.DMA((m,))],
        compiler_params=_params(48, collective_id=_PEER_SETS["both"]),
    )(*shards, *later_shards, gp)
    return list(outs[:n]), list(outs[n:n + m]), outs[-1]


def _shard_of(ref, w, s):
    if w == 0:
        return ref.at[:, pl.ds(_mo(s * SHARD_IN, 128), SHARD_IN)]
    if w == 3:
        return ref.at[pl.ds(_mo(s * 256, 256), 256), :]
    return ref.at[:, pl.ds(_mo(s * 256, 128), 256)]


def _gather_copies(ws):
    def copies(refs, send_sems, recv_sems):
        x, y, c, chips = _mesh_pos()
        out = []
        for j, (cx, cy) in enumerate(chips):
            for k, w in enumerate(ws):
                mine = _shard_of(refs[k], w, 2 * x + y)
                out.append(pltpu.make_async_remote_copy(
                    src_ref=mine, dst_ref=mine, send_sem=send_sems.at[3 * k + j], recv_sem=recv_sems.at[3 * k + j],
                    device_id=(cx, cy, c), device_id_type=MESH))
        return out
    return copies


def _inproj_fwd(x, norm_g, w_in_bf, tm=512, after=()):
    S = x.shape[0]

    def body(x_ref, g_ref, w_ref, ht_ref, q_ref, k_ref, v_ref, zr_ref):
        xv = x_ref[...]
        r = lax.rsqrt(jnp.mean(xv * xv, axis=-1, keepdims=True) + EPS)
        hf = (xv * r) * g_ref[...]
        ht_ref[...] = hf.T.astype(BF)
        h = hf.astype(BF)
        heads = (q_ref, k_ref, v_ref)
        for j in range(D_IN // 512):
            z = _dot(h, w_ref[:, j * 512:(j + 1) * 512])
            if j < 3:
                zb = z.astype(BF)
                for hd in range(N_HEADS):
                    heads[j][hd] = zb[:, hd * HEAD_DIM:(hd + 1) * HEAD_DIM]
            else:
                zr_ref[:, (j - 3) * 512:(j - 2) * 512] = z

    head_major = jax.ShapeDtypeStruct((N_HEADS, S, HEAD_DIM), BF)
    head_spec = pl.BlockSpec((N_HEADS, tm, HEAD_DIM), lambda i: (0, i, 0))
    return pl.pallas_call(
        _after(body, 3, after), name="inproj_fwd", grid=(S // tm,),
        out_shape=(jax.ShapeDtypeStruct((D_MODEL, S), BF), head_major, head_major, head_major,
                   jax.ShapeDtypeStruct((S, D_IN - 3 * D_A), F32)),
        in_specs=[pl.BlockSpec((tm, D_MODEL), lambda i: (i, 0)),
                  pl.BlockSpec((1, D_MODEL), lambda i: (0, 0)),
                  pl.BlockSpec((D_MODEL, D_IN), lambda i: (0, 0), pipeline_mode=pl.Buffered(1))]
        + [_ANY] * len(after),
        out_specs=[pl.BlockSpec((D_MODEL, tm), lambda i: (0, i)),
                   head_spec, head_spec, head_spec,
                   pl.BlockSpec((tm, D_IN - 3 * D_A), lambda i: (i, 0))],
        compiler_params=_params(52, dimension_semantics=("arbitrary",)),
    )(x, norm_g, w_in_bf, *after)


def _skew_table(gp_row):
    row = lax.broadcasted_iota(jnp.int32, (QB, ROLL_W), 0)
    t = jnp.broadcast_to(gp_row, (QB, ROLL_W))
    for b in range(7):
        t = jnp.where(((row >> b) & 1) == 1, pltpu.roll(t, 1 << b, axis=1), t)
    return t


def _unskew_sum(d):
    row = lax.broadcasted_iota(jnp.int32, (QB, ROLL_W), 0)
    for b in range(7):
        d = jnp.where(((row >> b) & 1) == 1, pltpu.roll(d, ROLL_W - (1 << b), axis=1), d)
    return jnp.sum(d, axis=0, keepdims=True)


def _struct_mask():
    a = lax.broadcasted_iota(jnp.int32, (QB, KB), 0) // CHUNK
    b = lax.broadcasted_iota(jnp.int32, (QB, KB), 1) // CHUNK
    return (b >= a) & (b <= a + N_PREV)


def _load_kv(k_hbm, v_hbm, k_scr, v_scr, sems, S, meanwhile=lambda: None):
    zeros = jnp.zeros((N_HEADS, PADK, HEAD_DIM), BF)
    k_scr[:, 0:PADK, :] = zeros
    v_scr[:, 0:PADK, :] = zeros
    ck = pltpu.make_async_copy(k_hbm, k_scr.at[:, pl.ds(PADK, S), :], sems.at[0])
    cv = pltpu.make_async_copy(v_hbm, v_scr.at[:, pl.ds(PADK, S), :], sems.at[1])
    ck.start()
    cv.start()
    meanwhile()
    ck.wait()
    cv.wait()


_BATCH_NT = (((2,), (2,)), ((0,), (0,)))
_BATCH_NN = (((2,), (1,)), ((0,), (0,)))
_BATCH_TN = (((1,), (1,)), ((0,), (0,)))


def _bdot(a, b, dims):
    return lax.dot_general(a, b, dims, preferred_element_type=F32)


def _scaled(q):
    return q * jnp.asarray(SCALE, BF)


def _scores(qs, kb, bias, i, front):
    s = _bdot(qs, kb, _BATCH_NT) + bias
    if front:
        col = lax.broadcasted_iota(jnp.int32, (1, 1, KB), 2)
        s = jnp.where(col >= PADK - i * QB, s, NEG_INF)
    return s


def _attn_fwd(q3, k3, v3, bias):
    S = q3.shape[1]

    def body(q_ref, k_hbm, v_hbm, bias_ref, o_ref, lse_ref, k_scr, v_scr, sems):
        @pl.when(pl.program_id(0) == 0)
        def _():
            _load_kv(k_hbm, v_hbm, k_scr, v_scr, sems, S)

        def step(i, rows, front):
            start = pl.multiple_of(i * QB, QB)
            kb = k_scr[:, pl.ds(start, KB), :]
            vb = v_scr[:, pl.ds(start, KB), :]
            s = _scores(_scaled(q_ref[:, rows, :]), kb, bias_ref[...], i, front)
            m = jnp.max(s, axis=-1, keepdims=True)
            e = jnp.exp(s - m)
            l = jnp.sum(e, axis=-1, keepdims=True)
            p = e * (1.0 / l)
            o = _bdot(p.astype(BF), vb, _BATCH_NN)
            lse_ref[:, rows, :] = jnp.broadcast_to(m + jnp.log(l), (N_HEADS, QB, 128))
            for h in range(N_HEADS):
                o_ref[rows, h * HEAD_DIM:(h + 1) * HEAD_DIM] = o[h]

        def block(j, carry):
            i = pl.program_id(0) * Q_PER_STEP + j
            rows = pl.ds(pl.multiple_of(j * QB, QB), QB)
            pl.when(i < KEEP)(functools.partial(step, i, rows, True))
            pl.when(i >= KEEP)(functools.partial(step, i, rows, False))
            return carry

        lax.fori_loop(0, Q_PER_STEP, block, 0)

    rows_per_step = Q_PER_STEP * QB
    kv_scr = pltpu.VMEM((N_HEADS, S + PADK, HEAD_DIM), BF)
    return pl.pallas_call(
        body, name="attn_fwd", grid=(S // rows_per_step,),
        out_shape=(jax.ShapeDtypeStruct((S, D_A), F32), jax.ShapeDtypeStruct((N_HEADS, S, 128), F32)),
        in_specs=[pl.BlockSpec((N_HEADS, rows_per_step, HEAD_DIM), lambda g: (0, g, 0)),
                  pl.BlockSpec(memory_space=pl.ANY), pl.BlockSpec(memory_space=pl.ANY),
                  pl.BlockSpec((N_HEADS, QB, KB), lambda g: (0, 0, 0))],
        out_specs=[pl.BlockSpec((rows_per_step, D_A), lambda g: (g, 0)),
                   pl.BlockSpec((N_HEADS, rows_per_step, 128), lambda g: (0, g, 0))],
        scratch_shapes=[kv_scr, kv_scr, pltpu.SemaphoreType.DMA((2,))],
        compiler_params=_params(48, dimension_semantics=("arbitrary",)),
    )(q3, k3, v3, bias)


def _attn_bwd(q3, k3, v3, d_att3, lse, bias, after=()):
    S = q3.shape[1]
    nq = S // QB

    def body(q_ref, do_ref, k_hbm, v_hbm, lse_ref, bias_ref, dq_ref, dk_ref, dv_ref, dgp_ref,
             k_scr, v_scr, dk_acc, dv_acc, dbias_acc, pad_scr, sems):
        @pl.when(pl.program_id(0) == 0)
        def _():
            def clear():
                dk_acc[...] = jnp.zeros_like(dk_acc)
                dv_acc[...] = jnp.zeros_like(dv_acc)
                dbias_acc[...] = jnp.zeros_like(dbias_acc)
            _load_kv(k_hbm, v_hbm, k_scr, v_scr, sems, S, clear)

        def step(i, rows, front):
            start = pl.multiple_of(i * QB, QB)
            kb = k_scr[:, pl.ds(start, KB), :]
            vb = v_scr[:, pl.ds(start, KB), :]
            qs = _scaled(q_ref[:, rows, :])
            do = do_ref[:, rows, :]
            p = jnp.exp(_scores(qs, kb, bias_ref[...], i, front) - jnp.tile(lse_ref[:, rows, :], (1, 1, KB // 128)))
            dp = _bdot(do, vb, _BATCH_NT)
            ds = p * (dp - jnp.sum(dp * p, axis=-1, keepdims=True))
            dbias_acc[...] += ds
            dsb = ds.astype(BF)
            dq = _bdot(dsb, kb, _BATCH_NN) * SCALE
            for h in range(N_HEADS):
                dq_ref[rows, h * HEAD_DIM:(h + 1) * HEAD_DIM] = dq[h].astype(BF)
            dk_acc[...] += _bdot(dsb, qs, _BATCH_TN)
            dv_acc[...] += _bdot(p.astype(BF), do, _BATCH_TN)

        def block(j, carry):
            i = pl.program_id(0) * Q_PER_STEP + j
            rows = pl.ds(pl.multiple_of(j * QB, QB), QB)
            pl.when(i < KEEP)(functools.partial(step, i, rows, True))
            pl.when((i >= KEEP) & (i < nq))(functools.partial(step, i, rows, False))
            for h in range(N_HEADS):
                hs = slice(h * HEAD_DIM, (h + 1) * HEAD_DIM)
                dk_ref[rows, hs] = dk_acc[h, 0:QB, :].astype(BF)
                dv_ref[rows, hs] = dv_acc[h, 0:QB, :].astype(BF)
            dk_acc[:, 0:KB - QB, :] = dk_acc[:, QB:KB, :]
            dv_acc[:, 0:KB - QB, :] = dv_acc[:, QB:KB, :]
            dk_acc[:, KB - QB:KB, :] = jnp.zeros((N_HEADS, QB, HEAD_DIM), F32)
            dv_acc[:, KB - QB:KB, :] = jnp.zeros((N_HEADS, QB, HEAD_DIM), F32)
            return carry

        lax.fori_loop(0, Q_PER_STEP, block, 0)

        @pl.when(pl.program_id(0) == n_steps - 1)
        def _():
            lane = lax.broadcasted_iota(jnp.int32, (1, ROLL_W), 1)
            hi = (lane < 384) | (lane >= 832)
            lo = (lane > 640) & (lane < 832)
            pad_scr[...] = jnp.zeros_like(pad_scr)
            for h in range(N_HEADS):
                pad_scr[:, 0:KB] = dbias_acc[h]
                g = _unskew_sum(pad_scr[...])
                s_hi = jnp.sum(jnp.where(hi, g, 0.0), axis=-1, keepdims=True)
                s_lo = jnp.sum(jnp.where(lo, g, 0.0), axis=-1, keepdims=True)
                g = jnp.where(lane == 384, g + s_hi, g)
                g = jnp.where(lane == 640, g + s_lo, g)
                dgp_ref[h:h + 1, :] = g

    assert nq % Q_PER_STEP == 0 and KEEP % Q_PER_STEP == 0
    rows_per_step = Q_PER_STEP * QB
    n_steps = (nq + KEEP) // Q_PER_STEP
    last = nq // Q_PER_STEP - 1
    lag = KEEP // Q_PER_STEP
    kv_scr = pltpu.VMEM((N_HEADS, S + PADK, HEAD_DIM), BF)
    return pl.pallas_call(
        _after(body, 6, after), name="attn_bwd", grid=(n_steps,),
        out_shape=(jax.ShapeDtypeStruct((S, D_A), BF), jax.ShapeDtypeStruct((S, D_A), BF),
                   jax.ShapeDtypeStruct((S, D_A), BF), jax.ShapeDtypeStruct((N_HEADS, ROLL_W), F32)),
        in_specs=[pl.BlockSpec((N_HEADS, rows_per_step, HEAD_DIM), lambda g: (0, jnp.minimum(g, last), 0)),
                  pl.BlockSpec((N_HEADS, rows_per_step, HEAD_DIM), lambda g: (0, jnp.minimum(g, last), 0)),
                  pl.BlockSpec(memory_space=pl.ANY), pl.BlockSpec(memory_space=pl.ANY),
                  pl.BlockSpec((N_HEADS, rows_per_step, 128), lambda g: (0, jnp.minimum(g, last), 0)),
                  pl.BlockSpec((N_HEADS, QB, KB), lambda g: (0, 0, 0))] + [_ANY] * len(after),
        out_specs=[pl.BlockSpec((rows_per_step, D_A), lambda g: (jnp.minimum(g, last), 0)),
                   pl.BlockSpec((rows_per_step, D_A), lambda g: (jnp.maximum(g - lag, 0), 0)),
                   pl.BlockSpec((rows_per_step, D_A), lambda g: (jnp.maximum(g - lag, 0), 0)),
                   pl.BlockSpec((N_HEADS, ROLL_W), lambda g: (0, 0))],
        scratch_shapes=[kv_scr, kv_scr,
                        pltpu.VMEM((N_HEADS, KB, HEAD_DIM), F32), pltpu.VMEM((N_HEADS, KB, HEAD_DIM), F32),
                        pltpu.VMEM((N_HEADS, QB, KB), F32), pltpu.VMEM((QB, ROLL_W), F32),
                        pltpu.SemaphoreType.DMA((2,))],
        compiler_params=_params(56, dimension_semantics=("arbitrary",)),
    )(q3, d_att3, k3, v3, lse, bias, *after)


def _sgu_core(ub, vb, lg, lb):
    u, du = _gelu_and_grad(ub)
    v, dv = _gelu_and_grad(vb)
    mu = jnp.mean(v, axis=-1, keepdims=True)
    vc = v - mu
    rstd = lax.rsqrt(jnp.mean(vc * vc, axis=-1, keepdims=True) + EPS)
    xh = vc * rstd
    vn = xh * lg + lb
    return u, du, dv, rstd, xh, vn


def _tri():
    r = lax.broadcasted_iota(jnp.int32, (SGU_CHUNK, SGU_CHUNK), 0)
    c = lax.broadcasted_iota(jnp.int32, (SGU_CHUNK, SGU_CHUNK), 1)
    return r >= c


def _tail_sgu(att, zrest, x, target, w_pa, w_pb, w_out, b_gate, final_g, ln_g, ln_b, w_s, b_s_t, tm=256):
    S = x.shape[0]
    nt = S // tm
    chunks = tm // SGU_CHUNK

    def body(att_ref, ga_ref, ub_ref, vb_ref, gb_ref, gta_ref, gtb_ref, x_ref, t_ref,
             wpa_ref, wpb_ref, wout_ref, bg_ref, fg_ref, lg_ref, lb_ref, ws_ref, bst_ref,
             dout_ref, datt_ref, dzt_ref, dzs_ref, gwout_hbm, gwpa_hbm, gwpb_hbm,
             gbg_ref, gfg_ref, loss_ref, gws_ref, gbs_ref, glg_ref, glb_ref,
             acc_out, acc_pa, acc_pb, sg_scr, mix_scr, dvn_scr, bs_acc, sems):
        i = pl.program_id(0)

        @pl.when(i == 0)
        def _():
            for r in (acc_out, acc_pa, acc_pb, gbg_ref, gfg_ref, loss_ref, gws_ref, glg_ref, glb_ref, bs_acc):
                r[...] = jnp.zeros_like(r)

        u, du, dv, rstd, xh, vn = _sgu_core(ub_ref[...], vb_ref[...], lg_ref[...], lb_ref[...])
        vnb = vn.astype(BF)
        tri = _tri()
        blocks = [(g, slice(n * SGU_CHUNK, (n + 1) * SGU_CHUNK), slice(g * 128, (g + 1) * 128))
                  for g in range(N_GROUPS) for n in range(chunks)]
        wts = [jnp.where(tri, ws_ref[g], 0.0) for g in range(N_GROUPS)]
        for g, rs, cs in blocks:
            mixed = _dot(wts[g].astype(BF), vnb[rs, cs]) + bst_ref[:, g:g + 1]
            mix_scr[rs, cs] = mixed
            sg_scr[rs, cs] = u[rs, cs] * mixed

        att = att_ref[...]
        sg = sg_scr[...]
        sa, dsa = _silu_and_grad(ga_ref[...])
        sb, dsb = _silu_and_grad(gb_ref[...])
        ya = (att * sa).astype(BF)
        yb = (sg * sb).astype(BF)
        pa = _dot(ya, wpa_ref[...])
        pb = _dot(yb, wpb_ref[...])
        ga = _sigmoid(gta_ref[...] + bg_ref[:, 0:D_MODEL])
        gb = _sigmoid(gtb_ref[...] + bg_ref[:, D_MODEL:2 * D_MODEL])
        merged = (ga * pa + gb * pb).astype(BF)
        out = x_ref[...] + _dot(merged, wout_ref[...])
        r2 = lax.rsqrt(jnp.mean(out * out, axis=-1, keepdims=True) + EPS)
        nrm = out * r2
        fg = fg_ref[...]
        err = nrm * fg - t_ref[...]
        loss_ref[...] += 0.5 * jnp.sum(jnp.mean(err * err, axis=-1, keepdims=True))
        dy = err * (1.0 / D_MODEL)
        gfg_ref[...] += jnp.sum(dy * nrm, axis=0, keepdims=True)
        dn = dy * fg
        d_out = r2 * (dn - nrm * jnp.mean(dn * nrm, axis=-1, keepdims=True))
        dout_ref[...] = d_out
        d_outb = d_out.astype(BF)
        acc_out[...] += _dot_tn(merged, d_outb)
        dm = _dot_nt(d_outb, wout_ref[...])
        d_pa = (dm * ga).astype(BF)
        d_pb = (dm * gb).astype(BF)
        d_gta = dm * pa * (ga * (1.0 - ga))
        d_gtb = dm * pb * (gb * (1.0 - gb))
        gbg_ref[:, 0:D_MODEL] += jnp.sum(d_gta, axis=0, keepdims=True)
        gbg_ref[:, D_MODEL:2 * D_MODEL] += jnp.sum(d_gtb, axis=0, keepdims=True)
        dzt_ref[:, 2 * D_A:2 * D_A + D_MODEL] = d_gta.astype(BF)
        dzt_ref[:, 2 * D_A + D_MODEL:] = d_gtb.astype(BF)
        acc_pa[...] += _dot_tn(ya, d_pa)
        acc_pb[...] += _dot_tn(yb, d_pb)
        d_ya = _dot_nt(d_pa, wpa_ref[...])
        d_yb = _dot_nt(d_pb, wpb_ref[...])
        d_att = (d_ya * sa).astype(BF)
        for hd in range(N_HEADS):
            datt_ref[hd] = d_att[:, hd * HEAD_DIM:(hd + 1) * HEAD_DIM]
        dzt_ref[:, 0:D_A] = (d_ya * att * dsa).astype(BF)
        dzt_ref[:, D_A:2 * D_A] = (d_yb * sg * dsb).astype(BF)

        dsg = d_yb * sb
        dzs_ref[:, 0:D_B] = (dsg * mix_scr[...] * du).astype(BF)
        dmix = dsg * u
        for g, rs, cs in blocks:
            dmb = dmix[rs, cs].astype(BF)
            bs_acc[:, cs] += dmix[rs, cs]
            gws_ref[g] += _dot_nt(dmb, vnb[rs, cs])
            dvn_scr[rs, cs] = _dot(wts[g].T.astype(BF), dmb)
        dvn = dvn_scr[...]
        glg_ref[...] += jnp.sum(dvn * xh, axis=0, keepdims=True)
        glb_ref[...] += jnp.sum(dvn, axis=0, keepdims=True)
        dxh = dvn * lg_ref[...]
        dvv = rstd * (dxh - jnp.mean(dxh, axis=-1, keepdims=True)
                      - xh * jnp.mean(dxh * xh, axis=-1, keepdims=True))
        dzs_ref[:, D_B:2 * D_B] = (dvv * dv).astype(BF)

        @pl.when(i == nt - 1)
        def _():
            cps = [pltpu.make_async_copy(acc_out, gwout_hbm, sems.at[0]),
                   pltpu.make_async_copy(acc_pa, gwpa_hbm, sems.at[1]),
                   pltpu.make_async_copy(acc_pb, gwpb_hbm, sems.at[2])]
            for cp in cps:
                cp.start()
            lane = lax.broadcasted_iota(jnp.int32, (SGU_CHUNK, 128), 1)
            cols = jnp.zeros((SGU_CHUNK, 128), F32)
            for g in range(N_GROUPS):
                gws_ref[g] = jnp.where(tri, gws_ref[g], 0.0)
                col = jnp.sum(bs_acc[:, g * 128:(g + 1) * 128], axis=-1, keepdims=True)
                cols = jnp.where(lane == g, col, cols)
            gbs_ref[...] = cols
            for cp in cps:
                cp.wait()

    c2 = lambda i: (0, 0)
    c3 = lambda i: (0, 0, 0)
    zcol = lambda w, blk: pl.BlockSpec((tm, w), lambda i: (i, blk))
    row = lambda w: pl.BlockSpec((tm, w), lambda i: (i, 0))
    return pl.pallas_call(
        body, name="tail", grid=(nt,),
        out_shape=(jax.ShapeDtypeStruct((S, D_MODEL), F32), jax.ShapeDtypeStruct((N_HEADS, S, HEAD_DIM), BF),
                   jax.ShapeDtypeStruct((S, 3072), BF), jax.ShapeDtypeStruct((S, 2 * D_B), BF),
                   jax.ShapeDtypeStruct((D_MODEL, D_MODEL), F32), jax.ShapeDtypeStruct((D_A, D_MODEL), F32),
                   jax.ShapeDtypeStruct((D_B, D_MODEL), F32),
                   jax.ShapeDtypeStruct((1, 2 * D_MODEL), F32), jax.ShapeDtypeStruct((1, D_MODEL), F32),
                   jax.ShapeDtypeStruct((1, 128), F32),
                   jax.ShapeDtypeStruct((N_GROUPS, 128, 128), F32), jax.ShapeDtypeStruct((SGU_CHUNK, 128), F32),
                   jax.ShapeDtypeStruct((1, D_B), F32), jax.ShapeDtypeStruct((1, D_B), F32)),
        in_specs=[row(D_A), zcol(512, 0), zcol(512, 1), zcol(512, 2), zcol(512, 3),
                  zcol(D_MODEL, 2), zcol(D_MODEL, 3), row(D_MODEL), row(D_MODEL),
                  pl.BlockSpec((D_A, D_MODEL), c2), pl.BlockSpec((D_B, D_MODEL), c2),
                  pl.BlockSpec((D_MODEL, D_MODEL), c2),
                  pl.BlockSpec((1, 2 * D_MODEL), c2), pl.BlockSpec((1, D_MODEL), c2),
                  pl.BlockSpec((1, D_B), c2), pl.BlockSpec((1, D_B), c2),
                  pl.BlockSpec((N_GROUPS, 128, 128), c3), pl.BlockSpec((128, N_GROUPS), c2)],
        out_specs=[row(D_MODEL), pl.BlockSpec((N_HEADS, tm, HEAD_DIM), lambda i: (0, i, 0)),
                   row(3072), row(2 * D_B), _ANY, _ANY, _ANY,
                   pl.BlockSpec((1, 2 * D_MODEL), c2), pl.BlockSpec((1, D_MODEL), c2),
                   pl.BlockSpec((1, 128), c2),
                   pl.BlockSpec((N_GROUPS, 128, 128), c3), pl.BlockSpec((SGU_CHUNK, 128), c2),
                   pl.BlockSpec((1, D_B), c2), pl.BlockSpec((1, D_B), c2)],
        scratch_shapes=[pltpu.VMEM((D_MODEL, D_MODEL), F32), pltpu.VMEM((D_A, D_MODEL), F32),
                        pltpu.VMEM((D_B, D_MODEL), F32),
                        pltpu.VMEM((tm, D_B), F32), pltpu.VMEM((tm, D_B), F32), pltpu.VMEM((tm, D_B), F32),
                        pltpu.VMEM((SGU_CHUNK, D_B), F32), pltpu.SemaphoreType.DMA((3,))],
        compiler_params=_params(58, dimension_semantics=("arbitrary",)),
    )(att, zrest, zrest, zrest, zrest, zrest, zrest, x, target, w_pa, w_pb, w_out, b_gate, final_g,
      ln_g, ln_b, w_s, b_s_t)


_DZ_MAP = ((0, 0), (1, 0), (2, 0), (3, 0), (4, 0), (4, 1), (3, 1), (3, 2), (3, 3), (3, 4), (3, 5))


def _dh_gradx(dq, dk, dv, dzt, dzs, w_in_bf, x, norm_g, d_out, tm=512, after=()):
    S = x.shape[0]

    def body(dq_ref, dk_ref, dv_ref, dzt_ref, dzs_ref, w_ref, x_ref, g_ref, dout_ref, gx_ref, gn_ref):
        i = pl.program_id(0)

        @pl.when(i == 0)
        def _():
            gn_ref[...] = jnp.zeros_like(gn_ref)

        pieces = (dq_ref, dk_ref, dv_ref, dzt_ref, dzs_ref)
        dh = jnp.zeros((tm, D_MODEL), F32)
        for j, (pc, blk) in enumerate(_DZ_MAP):
            dh += _dot_nt(pieces[pc][:, blk * 512:(blk + 1) * 512], w_ref[:, j * 512:(j + 1) * 512])
        xv = x_ref[...]
        r = lax.rsqrt(jnp.mean(xv * xv, axis=-1, keepdims=True) + EPS)
        nrm = xv * r
        gn_ref[...] += jnp.sum(dh * nrm, axis=0, keepdims=True)
        dn = dh * g_ref[...]
        gx_ref[...] = r * (dn - nrm * jnp.mean(dn * nrm, axis=-1, keepdims=True)) + dout_ref[...]

    row = lambda w: pl.BlockSpec((tm, w), lambda i: (i, 0))
    c2 = lambda i: (0, 0)
    return pl.pallas_call(
        _after(body, 9, after), name="dh_gradx", grid=(S // tm,),
        out_shape=(jax.ShapeDtypeStruct((S, D_MODEL), F32), jax.ShapeDtypeStruct((1, D_MODEL), F32)),
        in_specs=[row(512), row(512), row(512), row(3072), row(1024),
                  pl.BlockSpec((D_MODEL, D_IN), c2, pipeline_mode=pl.Buffered(1)), row(D_MODEL),
                  pl.BlockSpec((1, D_MODEL), c2), row(D_MODEL)]
        + [_ANY] * len(after),
        out_specs=[row(D_MODEL), pl.BlockSpec((1, D_MODEL), c2)],
        compiler_params=_params(48, dimension_semantics=("arbitrary",)),
    )(dq, dk, dv, dzt, dzs, w_in_bf, x, norm_g, d_out, *after)


def _gw_in(ht, dq, dk, dv, dzt, dzs, tn=512, after=()):
    S = ht.shape[1]
    per = 512 // tn
    cols = tuple((pc, per * blk + h) for pc, blk in _DZ_MAP for h in range(per))

    def body(ht_ref, dq_ref, dk_ref, dv_ref, dzt_ref, dzs_ref, o_ref, ob_ref):
        j = pl.program_id(0)
        pieces = (dq_ref, dk_ref, dv_ref, dzt_ref, dzs_ref)
        for pc in range(5):
            hit = functools.reduce(jnp.logical_or, [j == jj for jj, (p, _) in enumerate(cols) if p == pc])

            @pl.when(hit)
            def _(pc=pc):
                g = _dot(ht_ref[...], pieces[pc][...])
                o_ref[...] = g
                ob_ref[...] = g.astype(BF)

    def piece_spec(pc):
        cur = next(blk for p, blk in cols if p == pc)
        held = []
        for p, blk in cols:
            cur = blk if p == pc else cur
            held.append(cur)

        def index_map(j):
            blk = jnp.int32(held[0])
            for jj in range(1, len(held)):
                if held[jj] != held[jj - 1]:
                    blk = jnp.where(j >= jj, jnp.int32(held[jj]), blk)
            return (0, blk)

        return pl.BlockSpec((S, tn), index_map)

    return pl.pallas_call(
        _after(body, 6, after), name="gw_in", grid=(len(cols),),
        out_shape=(jax.ShapeDtypeStruct((D_MODEL, D_IN), F32), jax.ShapeDtypeStruct((D_MODEL, D_IN), BF)),
        in_specs=[pl.BlockSpec((D_MODEL, S), lambda j: (0, 0), pipeline_mode=pl.Buffered(1))]
        + [piece_spec(pc) for pc in range(5)]
        + [_ANY] * len(after),
        out_specs=[pl.BlockSpec((D_MODEL, tn), lambda j: (0, j)), pl.BlockSpec((D_MODEL, tn), lambda j: (0, j))],
        compiler_params=_params(56, dimension_semantics=("arbitrary",)),
    )(ht, dq, dk, dv, dzt, dzs, *after)


_HBM = pl.BlockSpec(memory_space=pltpu.HBM)
_SEM = pl.BlockSpec(memory_space=pltpu.SEMAPHORE)
_ANY = pl.BlockSpec(memory_space=pl.ANY)
_EFFECT = pltpu.SideEffectType.DATAFLOW_SIDE_EFFECTING


def _in_hbm(a):
    return pltpu.with_memory_space_constraint(a, pltpu.HBM)


def _after(body, n_in, after):
    if not after:
        return body
    return lambda *refs: body(*refs[:n_in], *refs[n_in + len(after):])


class _Started:
    def __init__(self, send, recv, bufs, token):
        self.send, self.recv, self.bufs, self.token = send, recv, bufs, token


_PEER_SETS = {"sibling": 7, "chips": 8, "both": 9}


def _peers(kind):
    x, y, c, chips = _mesh_pos()
    return ([(x, y, 1 - c)] if kind in ("sibling", "both") else []) + (
        [(cx, cy, c) for cx, cy in chips] if kind in ("chips", "both") else [])


def _signal_peers(kind):
    barrier = pltpu.get_barrier_semaphore()
    targets = _peers(kind)
    for peer in targets:
        pl.semaphore_signal(barrier, inc=1, device_id=peer, device_id_type=MESH)
    return lambda: pl.semaphore_wait(barrier, len(targets))


def _split_start(name, bufs, n_copies, copies, peers, after=()):
    nb = len(bufs)

    def body(*refs):
        _signal_peers(peers)()
        refs = refs[:nb] + refs[nb + len(after):]
        for cp in copies(refs[:nb], refs[nb], refs[nb + 1]):
            cp.start()
        refs[-1][...] = jnp.zeros_like(refs[-1])

    outs = pl.pallas_call(
        body, name=name,
        out_shape=(pltpu.SemaphoreType.DMA((n_copies,)), pltpu.SemaphoreType.DMA((n_copies,)),
                   *[pltpu.HBM(b.shape, b.dtype) for b in bufs], jax.ShapeDtypeStruct((8, 128), F32)),
        in_specs=[_HBM] * nb + [_ANY] * len(after),
        out_specs=(_SEM, _SEM, *[_HBM] * nb, pl.BlockSpec(memory_space=pltpu.VMEM)),
        input_output_aliases={k: 2 + k for k in range(nb)},
        compiler_params=_params(1, has_side_effects=_EFFECT, collective_id=_PEER_SETS[peers]),
    )(*[_in_hbm(b) for b in bufs], *after)
    return _Started(outs[0], outs[1], list(outs[2:2 + nb]), outs[-1])


def _split_wait(name, started, copies, after):
    nb = len(started.bufs)

    def body(*refs):
        for cp in copies(refs[:nb], refs[nb], refs[nb + 1]):
            cp.wait_send()
            cp.wait_recv()

    return list(pl.pallas_call(
        body, name=name,
        out_shape=tuple(pltpu.HBM(b.shape, b.dtype) for b in started.bufs),
        in_specs=[_HBM] * nb + [_SEM, _SEM, _ANY],
        out_specs=tuple([_HBM] * nb),
        input_output_aliases={k: k for k in range(nb)},
        compiler_params=_params(1, has_side_effects=_EFFECT),
    )(*started.bufs, started.send, started.recv, after))


def _x1_copies(ws):
    def copies(refs, send_sems, recv_sems):
        x, y, c, _ = _mesh_pos()
        out = []
        for k, w in enumerate(ws):
            for s in range(N_SHARD):
                out.append(pltpu.make_async_remote_copy(
                    src_ref=_UNITS[w](refs[k], s, 1 - c), dst_ref=refs[len(ws) + k].at[s],
                    send_sem=send_sems.at[N_SHARD * k + s], recv_sem=recv_sems.at[N_SHARD * k + s],
                    device_id=(x, y, 1 - c), device_id_type=MESH))
        return out
    return copies


def _x2_copies(n):
    def copies(refs, send_sems, recv_sems):
        x, y, c, chips = _mesh_pos()
        out = []
        for j, (cx, cy) in enumerate(chips):
            for k in range(n):
                out.append(pltpu.make_async_remote_copy(
                    src_ref=refs[k].at[2 * cx + cy], dst_ref=refs[n + k].at[j],
                    send_sem=send_sems.at[3 * k + j], recv_sem=recv_sems.at[3 * k + j],
                    device_id=(cx, cy, c), device_id_type=MESH))
        return out
    return copies


def _x3_copies(ws):
    def copies(refs, send_sems, recv_sems):
        x, y, c, _ = _mesh_pos()
        out = []
        for k, w in enumerate(ws):
            rows = _HALF_ROWS[w]
            mine = refs[k].at[pl.ds(_mo(c * rows, rows), rows), :]
            out.append(pltpu.make_async_remote_copy(
                src_ref=mine, dst_ref=mine, send_sem=send_sems.at[k], recv_sem=recv_sems.at[k],
                device_id=(x, y, 1 - c), device_id_type=MESH))
        return out
    return copies


def _x1_lands(ws, dtype=F32):
    return [lax.empty((N_SHARD,) + _UNIT_SHAPES[w], dtype) for w in ws]


def _x2_lands(ws):
    return [lax.empty((3,) + _UNIT_SHAPES[w], BF) for w in ws]


def _grad_add1(w, g, recv, pos):
    ur, uc = _UNIT_SHAPES[w]

    def body(pos_ref, g_ref, r_ref, csb_ref):
        csb_ref[0] = (g_ref[...] + r_ref[0].astype(F32)).astype(BF)

    u3 = lambda k, pos: (pos[2 + k], 0, 0)
    return pl.pallas_call(
        body, name=f"grad_add1_{w}",
        grid_spec=pltpu.PrefetchScalarGridSpec(
            num_scalar_prefetch=1, grid=(N_SHARD - 1,),
            in_specs=[pl.BlockSpec((ur, uc), lambda k, pos: (pos[0], pos[2 + k])), pl.BlockSpec((1, ur, uc), u3)],
            out_specs=pl.BlockSpec((1, ur, uc), u3)),
        out_shape=jax.ShapeDtypeStruct((N_SHARD, ur, uc), BF),
        compiler_params=_params(40, dimension_semantics=("arbitrary",)),
    )(pos, g, recv)


def _grad_add1_group(ws, gs, recvs, pos):
    n = len(ws)

    def body(pos_ref, *refs):
        s = pl.program_id(0)
        for k in range(n):
            g, r, own, csb = refs[k], refs[n + k], refs[2 * n + k], refs[3 * n + k]
            v = g[...] + r[0]
            csb[0] = v.astype(BF)

            @pl.when(s == pos_ref[1])
            def _(own=own, v=v):
                own[...] = v

    def g_spec(w):
        if w == 3:
            return pl.BlockSpec(_UNIT_SHAPES[w], lambda s, pos: (2 * s + pos[0], 0))
        return pl.BlockSpec(_UNIT_SHAPES[w], lambda s, pos: (pos[0], s))

    slot = lambda w: pl.BlockSpec((1,) + _UNIT_SHAPES[w], lambda s, pos: (s, 0, 0))
    outs = pl.pallas_call(
        body, name="grad_add1_group",
        grid_spec=pltpu.PrefetchScalarGridSpec(
            num_scalar_prefetch=1, grid=(N_SHARD,),
            in_specs=[g_spec(w) for w in ws] + [slot(w) for w in ws],
            out_specs=[pl.BlockSpec(_UNIT_SHAPES[w], lambda s, pos: (0, 0)) for w in ws] + [slot(w) for w in ws]),
        out_shape=tuple(jax.ShapeDtypeStruct(_UNIT_SHAPES[w], F32) for w in ws)
        + tuple(jax.ShapeDtypeStruct((N_SHARD,) + _UNIT_SHAPES[w], BF) for w in ws),
        compiler_params=_params(32, dimension_semantics=("arbitrary",)),
    )(pos, *gs, *recvs)
    return list(outs[:n]), list(outs[n:])


def _grad_add2_group(ws, owns, recvs):
    n = len(ws)

    def body(*refs):
        c = lax.axis_index("c")
        for k, w in enumerate(ws):
            own, r, o = refs[k], refs[n + k], refs[2 * n + k]
            rows = _HALF_ROWS[w]
            total = ((own[...] + r[0].astype(F32)) + r[1].astype(F32)) + r[2].astype(F32)
            o[pl.ds(_mo(c * rows, rows), rows), :] = total

    vm = pl.BlockSpec(memory_space=pltpu.VMEM)
    return list(pl.pallas_call(
        body, name="grad_add2_group",
        out_shape=tuple(jax.ShapeDtypeStruct(_SHARD_SHAPES[w], F32) for w in ws),
        in_specs=[vm] * (2 * n), out_specs=[vm] * n,
        compiler_params=_params(32),
    )(*owns, *recvs))


def _grad_add2(w, g, recv1, recv2, pos):
    ur, uc = _UNIT_SHAPES[w]
    nt = 4
    tr = ur // nt

    def body(pos_ref, g_ref, r1_ref, r2_ref, o_ref):
        own = g_ref[...] + r1_ref[0].astype(F32)
        o_ref[...] = ((own + r2_ref[0].astype(F32)) + r2_ref[1].astype(F32)) + r2_ref[2].astype(F32)

    mine = lambda t, pos: (pos[0] * nt + t, 0)
    return pl.pallas_call(
        body, name=f"grad_add2_{w}",
        grid_spec=pltpu.PrefetchScalarGridSpec(
            num_scalar_prefetch=1, grid=(nt,),
            in_specs=[pl.BlockSpec((tr, uc), lambda t, pos: (pos[0] * nt + t, pos[1])),
                      pl.BlockSpec((1, tr, uc), lambda t, pos: (pos[1], t, 0)),
                      pl.BlockSpec((3, tr, uc), lambda t, pos: (0, t, 0))],
            out_specs=pl.BlockSpec((tr, uc), mine)),
        out_shape=jax.ShapeDtypeStruct(_SHARD_SHAPES[w], F32),
        compiler_params=_params(32, dimension_semantics=("arbitrary",)),
    )(pos, g, recv1, recv2)


def _adamw_math(w, g, m, v):
    m = ADAM_B1 * m + (1.0 - ADAM_B1) * g
    v = ADAM_B2 * v + (1.0 - ADAM_B2) * (g * g)
    m_hat = m / ADAM_C1
    v_hat = v / ADAM_C2
    delta = -ADAM_LR * (m_hat / (jnp.sqrt(v_hat) + ADAM_EPS) + ADAM_WD * w)
    return delta, m, v


ADAMW_STEPS = 4


def _adamw(ws_, gs, ms, vs):
    n = len(ws_)

    def body(*refs):
        for k in range(n):
            w, g, m, v = (refs[j * n + k] for j in range(4))
            d, nm, nv, gc = (refs[(4 + j) * n + k] for j in range(4))
            gv = g[...]
            d[...], nm[...], nv[...] = _adamw_math(w[...], gv, m[...], v[...])
            gc[...] = gv

    specs = [pl.BlockSpec((a.shape[0] // ADAMW_STEPS, a.shape[1]), lambda i: (i, 0)) for a in ws_] * 4
    outs = pl.pallas_call(
        body, name="adamw", grid=(ADAMW_STEPS,),
        out_shape=tuple(jax.ShapeDtypeStruct(a.shape, F32) for _ in range(4) for a in ws_),
        in_specs=specs, out_specs=specs,
        compiler_params=_params(40, dimension_semantics=("arbitrary",)),
    )(*ws_, *gs, *ms, *vs)
    return [tuple(outs[j * n + k] for j in range(4)) for k in range(n)]


_REL_PAD = 384
_VEC_FIELDS = (("norm_g", 0, D_MODEL), ("b_gate", 1024, 2 * D_MODEL), ("sgu_ln_g", 3072, D_B),
               ("sgu_ln_b", 3584, D_B), ("b_s", 4096, N_GROUPS * 128), ("final_g", 4608, D_MODEL))
_LOSS_OFF = 5632
_REL_OFF = 5760
_NV = _REL_OFF + N_HEADS * _REL_PAD
_N_FIELDS = len(_VEC_FIELDS) + 2


_B_S_FIELD = [f[0] for f in _VEC_FIELDS].index("b_s")


def _assemble_row(dst, fields, transposed_b_s):
    for f, (_, off, n) in enumerate(_VEC_FIELDS):
        if transposed_b_s and f == _B_S_FIELD:
            t = fields[f][...].T
            for g in range(N_GROUPS):
                dst[:, off + 128 * g:off + 128 * (g + 1)] = t[g:g + 1, :]
        else:
            dst[:, off:off + n] = fields[f][...]
    for r in range(N_HEADS):
        dst[:, _REL_OFF + _REL_PAD * r:_REL_OFF + _REL_PAD * (r + 1)] = fields[len(_VEC_FIELDS)][r:r + 1, :]


def _small_reduce(grads, loss_row, after=()):
    n_in = _N_FIELDS + 1

    def body(*refs):
        g_refs, loss_ref = refs[:_N_FIELDS], refs[_N_FIELDS]
        out_v, out_w = refs[n_in:n_in + 2]
        mine_v, mine_w, gath_v, gath_w, send_sems, recv_sems = refs[n_in + 2:]
        x, y, c, chips = _mesh_pos()
        me, sibling = (x, y, c), (x, y, 1 - c)

        peers_entered = _signal_peers("both")
        _assemble_row(mine_v, g_refs, True)
        mine_v[:, _LOSS_OFF:_LOSS_OFF + 128] = loss_ref[...]
        mine_w[...] = g_refs[-1][...].astype(BF)
        peers_entered()
        my_k = 4 * x + 2 * y + c
        gath_v[my_k] = mine_v[...]
        gath_w[my_k] = mine_w[...]

        def copy(k, gath, block, to, src=None):
            dst = gath.at[4 * block[0] + 2 * block[1] + block[2]]
            return pltpu.make_async_remote_copy(
                src_ref=dst if src is None else src, dst_ref=dst,
                send_sem=send_sems.at[k], recv_sem=recv_sems.at[k], device_id=to, device_id_type=MESH)

        bufs = ((gath_v, mine_v), (gath_w, mine_w))
        first, passed = [], []
        for b, (gath, mine) in enumerate(bufs):
            first.append(copy(7 * b, gath, me, sibling, src=mine))
            first += [copy(7 * b + 1 + j, gath, me, (*chip, c), src=mine) for j, chip in enumerate(chips)]
        for cp in first:
            cp.start()
        for b, (gath, _) in enumerate(bufs):
            for j, chip in enumerate(chips):
                copy(7 * b + 1 + j, gath, (*chip, c), me).wait_recv()
                cp = copy(7 * b + 4 + j, gath, (*chip, c), sibling)
                cp.start()
                passed.append(cp)
        for b, (gath, _) in enumerate(bufs):
            copy(7 * b, gath, sibling, me).wait_recv()
            for j, chip in enumerate(chips):
                copy(7 * b + 4 + j, gath, (*chip, 1 - c), me).wait_recv()
        for cp in first + passed:
            cp.wait_send()

        tot_v, tot_w = gath_v[0], gath_w[0].astype(F32)
        for k in range(1, 8):
            tot_v = tot_v + gath_v[k]
            tot_w = tot_w + gath_w[k].astype(F32)
        out_v[...] = tot_v
        out_w[...] = tot_w

    vm = pl.BlockSpec(memory_space=pltpu.VMEM)
    return pl.pallas_call(
        _after(body, n_in, after), name="small_reduce",
        out_shape=(jax.ShapeDtypeStruct((1, _NV), F32), jax.ShapeDtypeStruct((N_GROUPS * 128, 128), F32)),
        in_specs=[vm] * n_in + [_ANY] * len(after), out_specs=[vm] * 2,
        scratch_shapes=[pltpu.VMEM((1, _NV), F32), pltpu.VMEM((N_GROUPS * 128, 128), BF),
                        pltpu.VMEM((8, 1, _NV), F32), pltpu.VMEM((8, N_GROUPS * 128, 128), BF),
                        pltpu.SemaphoreType.DMA((14,)), pltpu.SemaphoreType.DMA((14,))],
        compiler_params=_params(32, collective_id=_PEER_SETS["both"]),
    )(*grads, loss_row, *after)


def _small_adamw(tot_v, tot_w, params):
    n_in = 2 + 3 * _N_FIELDS

    def body(*refs):
        tv_ref, tw_ref = refs[:2]
        p_refs = [refs[2 + k * _N_FIELDS:2 + (k + 1) * _N_FIELDS] for k in range(3)]
        outs = refs[n_in:n_in + 4 * _N_FIELDS + 1]
        wmv = refs[-1]
        for k in range(3):
            _assemble_row(wmv.at[k], p_refs[k], False)
            wmv[k, :, _LOSS_OFF:_LOSS_OFF + 128] = jnp.zeros((1, 128), F32)
        tot_v, tot_w = tv_ref[...], tw_ref[...]
        res_v = (tot_v,) + _adamw_math(wmv[0], tot_v, wmv[1], wmv[2])
        res_w = (tot_w,) + _adamw_math(p_refs[0][-1][...], tot_w, p_refs[1][-1][...], p_refs[2][-1][...])
        for kind in range(4):
            o = outs[kind * _N_FIELDS:(kind + 1) * _N_FIELDS]
            for f, (_, off, n) in enumerate(_VEC_FIELDS):
                o[f][...] = res_v[kind][:, off:off + n]
            for r in range(N_HEADS):
                o[len(_VEC_FIELDS)][r:r + 1, :] = res_v[kind][:, _REL_OFF + _REL_PAD * r:_REL_OFF + _REL_PAD * (r + 1)]
            o[-1][...] = res_w[kind]
        outs[-1][...] = tot_v[:, _LOSS_OFF:_LOSS_OFF + 128]

    field_shapes = [(1, n) for _, _, n in _VEC_FIELDS] + [(N_HEADS, _REL_PAD), (N_GROUPS * 128, 128)]
    vm = pl.BlockSpec(memory_space=pltpu.VMEM)
    operands = [tot_v, tot_w] + [a for p in params for a in p]
    assert len(operands) == n_in
    outs = pl.pallas_call(
        body, name="small_adamw",
        out_shape=tuple(jax.ShapeDtypeStruct(s, F32) for _ in range(4) for s in field_shapes)
        + (jax.ShapeDtypeStruct((1, 128), F32),),
        in_specs=[vm] * n_in, out_specs=[vm] * (4 * _N_FIELDS + 1),
        scratch_shapes=[pltpu.VMEM((3, 1, _NV), F32)],
        compiler_params=_params(32),
    )(*operands)
    return [outs[k * _N_FIELDS:(k + 1) * _N_FIELDS] for k in range(4)], outs[-1]


def _small_fields(norm_g, b_gate, ln_g, ln_b, b_s, final_g, rel_bias, w_s):
    rel = jnp.pad(rel_bias.reshape(N_HEADS, N_REL), ((0, 0), (0, _REL_PAD - N_REL)))
    return (norm_g, b_gate, ln_g, ln_b, b_s.reshape(1, N_GROUPS * 128), final_g.reshape(1, D_MODEL),
            rel, w_s.reshape(N_GROUPS * 128, 128))


def _small_outputs(fields):
    n_g, b_g, l_g, l_b, b_s, f_g, rel, w_s = fields
    return (n_g, b_g, rel[:, :N_REL].reshape(1, N_HEADS, N_REL), l_g, l_b,
            w_s.reshape(1, N_GROUPS, 128, 128), b_s.reshape(1, N_GROUPS, 128), f_g.reshape(D_MODEL))


def _bias_row(rel_bias):
    hi = rel_bias[:, N_REL - 1:N_REL]
    lo = rel_bias[:, 0:1]
    return jnp.concatenate([jnp.broadcast_to(hi, (N_HEADS, 384)), rel_bias[:, ::-1],
                            jnp.broadcast_to(lo, (N_HEADS, 191)), jnp.broadcast_to(hi, (N_HEADS, 192))], axis=1)


def kernel(x, norm_g, w_in, b_gate, rel_bias, sgu_ln_g, sgu_ln_b, w_s, b_s, w_pa, w_pb, w_out, final_g, loss_target, m_norm_g, m_w_in, m_b_gate, m_rel_bias, m_sgu_ln_g, m_sgu_ln_b, m_w_s, m_b_s, m_w_pa, m_w_pb, m_w_out, m_final_g, v_norm_g, v_w_in, v_b_gate, v_rel_bias, v_sgu_ln_g, v_sgu_ln_b, v_w_s, v_b_s, v_w_pa, v_w_pb, v_w_out, v_final_g):
    S = x.shape[1]
    xs = x.reshape(S, D_MODEL)
    tgt = loss_target.reshape(S, D_MODEL)
    big_w = (w_in[0], w_pa[0], w_pb[0], w_out[0])
    big_m = (m_w_in[0], m_w_pa[0], m_w_pb[0], m_w_out[0])
    big_v = (v_w_in[0], v_w_pa[0], v_w_pb[0], v_w_out[0])
    rel = rel_bias[0]
    ws = w_s[0]
    bst = b_s[0].T
    fg = final_g.reshape(1, D_MODEL)
    chip = 2 * lax.axis_index("x") + lax.axis_index("y")
    pos = jnp.stack([lax.axis_index("c"), chip] + [(chip + k) % N_SHARD for k in range(1, N_SHARD)]).astype(jnp.int32)

    (w_in_bf,), staged, band_bias = _ag_weights((0,), big_w[:1], (1, 2, 3), big_w[1:], _bias_row(rel))
    ag_s = _split_start("ag_small_start", staged, 9, _gather_copies((1, 2, 3)), "chips", after=(w_in_bf,))

    ht, q3, k3, v3, zrest = _inproj_fwd(xs, norm_g, w_in_bf, after=(ag_s.token,))
    att, lse = _attn_fwd(q3, k3, v3, band_bias)
    w_pa_bf, w_pb_bf, w_out_bf = _split_wait("ag_small_wait", ag_s, _gather_copies((1, 2, 3)), att)
    (d_out, d_att, dzt, dzs, gw_out, gw_pa, gw_pb, g_bgate, g_final, loss_row,
     g_ws, g_bs_t, g_lng, g_lnb) = _tail_sgu(
        att, zrest, xs, tgt, w_pa_bf, w_pb_bf, w_out_bf, b_gate, fg, sgu_ln_g, sgu_ln_b, ws, bst)
    ws_s, ws_i = (1, 2, 3), (0,)

    x1s = _split_start("gx1s_start", [gw_pa, gw_pb, gw_out] + _x1_lands(ws_s), 12, _x1_copies(ws_s), "sibling")
    dq, dk, dv, d_gp = _attn_bwd(q3, k3, v3, d_att, lse, band_bias, after=(x1s.token,))
    got = _split_wait("gx1s_wait", x1s, _x1_copies(ws_s), dq)
    own_s, csb_s = _grad_add1_group(ws_s, got[:3], got[3:], pos)

    x2s = _split_start("gx2s_start", csb_s + _x2_lands(ws_s), 9, _x2_copies(3), "chips")
    gw_in, gw_in_bf = _gw_in(ht, dq, dk, dv, dzt, dzs, after=(x2s.token,))
    x1i = _split_start("gx1i_start", [gw_in_bf] + _x1_lands(ws_i, BF), 4, _x1_copies(ws_i), "sibling")
    got = _split_wait("gx2s_wait", x2s, _x2_copies(3), x1i.token)
    halves_s = _grad_add2_group(ws_s, own_s, got[3:])
    x3s = _split_start("gx3s_start", halves_s, 3, _x3_copies(ws_s), "sibling")
    recv1_i = _split_wait("gx1i_wait", x1i, _x1_copies(ws_i), x3s.token)[1]
    csb_i = _grad_add1(0, gw_in, recv1_i, pos)

    x2i = _split_start("gx2i_start", [csb_i] + _x2_lands(ws_i), 3, _x2_copies(1), "chips")
    grad_x, g_norm = _dh_gradx(dq, dk, dv, dzt, dzs, w_in_bf, xs, norm_g, d_out, after=(x2i.token,))
    g_shards_s = _split_wait("gx3s_wait", x3s, _x3_copies(ws_s), grad_x)
    got = _split_wait("gx2i_wait", x2i, _x2_copies(1), grad_x)
    half_i = _grad_add2(0, gw_in, recv1_i, got[1], pos)
    x3i = _split_start("gx3i_start", [half_i], 1, _x3_copies(ws_i), "sibling")

    g_rel = jnp.pad(d_gp[:, 384:384 + N_REL][:, ::-1], ((0, 0), (0, _REL_PAD - N_REL)))
    small_grads = (g_norm, g_bgate, g_lng, g_lnb, g_bs_t, g_final, g_rel, g_ws.reshape(N_GROUPS * 128, 128))
    small_params = (_small_fields(norm_g, b_gate, sgu_ln_g, sgu_ln_b, b_s, final_g, rel_bias, w_s),
                    _small_fields(m_norm_g, m_b_gate, m_sgu_ln_g, m_sgu_ln_b, m_b_s, m_final_g, m_rel_bias, m_w_s),
                    _small_fields(v_norm_g, v_b_gate, v_sgu_ln_g, v_sgu_ln_b, v_b_s, v_final_g, v_rel_bias, v_w_s))
    tot_v, tot_w = _small_reduce(small_grads, loss_row, after=(x3i.token,))
    (gsum, sdelta, sm, sv), loss_out = _small_adamw(tot_v, tot_w, small_params)

    g_shard_i, = _split_wait("gx3i_wait", x3i, _x3_copies(ws_i), loss_out)
    big = _adamw(big_w, [g_shard_i] + g_shards_s, big_m, big_v)
    sg_out, sd_out, sm_out, sv_out = (_small_outputs(f) for f in (gsum, sdelta, sm, sv))
    loss = loss_out[0, 0]

    def assemble(small, bigs):
        n_g, b_g, r_b, l_g, l_b, w_s_, b_s_, f_g = small
        b_in, b_pa, b_pb, b_out = (b[None] for b in bigs)
        return (n_g, b_in, b_g, r_b, l_g, l_b, w_s_, b_s_, b_pa, b_pb, b_out, f_g)

    grads_out = assemble(sg_out, [b[3] for b in big])
    delta_out = assemble(sd_out, [b[0] for b in big])
    m_out = assemble(sm_out, [b[1] for b in big])
    v_out = assemble(sv_out, [b[2] for b in big])
    return (loss, grad_x.reshape(1, S, D_MODEL), *grads_out, *delta_out, *m_out, *v_out)
```

```python
import functools
import math

import jax
import jax.numpy as jnp
from jax import lax
from jax.experimental import pallas as pl
from jax.experimental.pallas import tpu as pltpu

F32 = jnp.float32
BF = jnp.bfloat16
MESH = pl.DeviceIdType.MESH

D_MODEL = 1024
D_A = 512
D_B = 512
D_IN = 5632
N_HEADS = 8
HEAD_DIM = 64
CHUNK = 64
N_PREV = 8
SGU_CHUNK = 128
N_GROUPS = 4
N_REL = 257
EPS = 1e-6
NEG_INF = -1e30
SCALE = HEAD_DIM ** -0.5

QB = 2 * CHUNK
KB = (N_PREV + 2) * CHUNK
PADK = N_PREV * CHUNK
ROLL_W = 1024
KEEP = KB // QB - 1
Q_PER_STEP = 2

ADAM_LR = 0.001
ADAM_B1 = 0.9
ADAM_B2 = 0.999
ADAM_EPS = 1e-08
ADAM_WD = 0.01
ADAM_STEP = 10
ADAM_C1 = 1.0 - ADAM_B1 ** ADAM_STEP
ADAM_C2 = 1.0 - ADAM_B2 ** ADAM_STEP

AG_PIECES = 4
N_SHARD = 4
SHARD_IN = D_IN // N_SHARD
MIB = 1024 * 1024


V7X_VMEM_MIB = 64
VMEM_RESERVE_MIB = V7X_VMEM_MIB - 4


def _params(vmem_mib, **kw):
    assert vmem_mib <= VMEM_RESERVE_MIB
    return pltpu.CompilerParams(vmem_limit_bytes=VMEM_RESERVE_MIB * MIB, **kw)


def _sigmoid(x):
    return 1.0 / (1.0 + jnp.exp(-x))


def _silu_and_grad(x):
    s = _sigmoid(x)
    return x * s, s * (1.0 + x * (1.0 - s))


_GELU_C = math.sqrt(2.0 / math.pi)
_GELU_A = 0.044715


def _gelu_and_grad(x):
    x2 = x * x
    t = jnp.tanh(_GELU_C * (x + _GELU_A * (x2 * x)))
    cdf = 0.5 * (1.0 + t)
    grad = cdf + 0.5 * x * (1.0 - t * t) * (_GELU_C * (1.0 + 3.0 * _GELU_A * x2))
    return x * cdf, grad


def _dot(a, b):
    return jnp.dot(a, b, preferred_element_type=F32)


def _dot_nt(a, b):
    return lax.dot_general(a, b, (((1,), (1,)), ((), ())), preferred_element_type=F32)


def _dot_tn(a, b):
    return lax.dot_general(a, b, (((0,), (0,)), ((), ())), preferred_element_type=F32)


def _mo(v, m):
    return v if isinstance(v, int) else pl.multiple_of(v, m)


def _unit_in(ref, s, p):
    return ref.at[pl.ds(_mo(p * 512, 512), 512), pl.ds(_mo(s * SHARD_IN, 128), SHARD_IN)]


def _unit_p(ref, s, p):
    return ref.at[pl.ds(_mo(p * 256, 256), 256), pl.ds(_mo(s * 256, 128), 256)]


def _unit_out(ref, s, p):
    return ref.at[pl.ds(_mo(s * 256 + p * 128, 128), 128), :]


_UNITS = (_unit_in, _unit_p, _unit_p, _unit_out)
_HALF_ROWS = (512, 256, 256, 128)
_UNIT_SHAPES = ((512, SHARD_IN), (256, 256), (256, 256), (128, D_MODEL))
_FULL_SHAPES = ((D_MODEL, D_IN), (D_A, D_MODEL), (D_B, D_MODEL), (D_MODEL, D_MODEL))
_SHARD_SHAPES = ((D_MODEL, SHARD_IN), (D_A, 256), (D_B, 256), (256, D_MODEL))


def _mesh_pos():
    x, y, c = lax.axis_index("x"), lax.axis_index("y"), lax.axis_index("c")
    chips = [(1 - x, y), (x, 1 - y), (1 - x, 1 - y)]
    return x, y, c, chips


def _ag_weights(ws, shards, later_ws, later_shards, gp):
    n, m = len(ws), len(later_ws)

    def body(*refs):
        ins, later_ins, gp_ref = refs[:n], refs[n:n + m], refs[n + m]
        o = n + m + 1
        outs, later_outs, bias_ref = refs[o:o + n], refs[o + n:o + n + m], refs[o + n + m]
        o += n + m + 1
        stage, later_stage = refs[o:o + n], refs[o + n:o + n + m]
        send_sems, recv_sems, local_sems, later_sems = refs[o + n + m:]
        x, y, c, chips = _mesh_pos()
        s_me = 2 * x + y
        sibling = (x, y, 1 - c)
        def rows_of(k, p):
            rows = _HALF_ROWS[ws[k]]
            return pl.ds(_mo(p * rows, rows), rows)

        def half(k, p):
            return stage[k].at[rows_of(k, p), :]

        def unit(k, s, p):
            return _UNITS[ws[k]](outs[k], s, p)

        def rcopy(k, i, src, dst, to):
            return pltpu.make_async_remote_copy(src_ref=src, dst_ref=dst, send_sem=send_sems.at[k, i],
                                                recv_sem=recv_sems.at[k, i], device_id=to, device_id_type=MESH)

        peers_entered = _signal_peers("both")
        for k in range(n):
            stage[k][rows_of(k, c), :] = ins[k][rows_of(k, c), :].astype(BF)
        peers_entered()
        def piece(ref, k, q):
            rows = _HALF_ROWS[ws[k]] // AG_PIECES
            return ref.at[pl.ds(q * rows, rows), :]

        sends = []
        for q in range(AG_PIECES):
            for j, (cx, cy) in enumerate(chips):
                for k in range(n):
                    cp = rcopy(k, j * AG_PIECES + q, piece(half(k, c), k, q), piece(unit(k, s_me, c), k, q),
                               (cx, cy, c))
                    cp.start()
                    sends.append(cp)
        for k in range(n):
            stage[k][rows_of(k, 1 - c), :] = ins[k][rows_of(k, 1 - c), :].astype(BF)
        local = []
        for k in range(n):
            for p in range(2):
                cp = pltpu.make_async_copy(half(k, p), unit(k, s_me, p), local_sems.at[k, p])
                cp.start()
                local.append(cp)
        for k, w in enumerate(later_ws):
            later_stage[k][...] = later_ins[k][...].astype(BF)
            cp = pltpu.make_async_copy(later_stage[k], _shard_of(later_outs[k], w, s_me), later_sems.at[k])
            cp.start()
            local.append(cp)
        keep = _struct_mask()
        for h in range(N_HEADS):
            bias_ref[h] = jnp.where(keep, _skew_table(gp_ref[h:h + 1, :])[:, :KB], NEG_INF)
        for q in range(AG_PIECES):
            for j, (cx, cy) in enumerate(chips):
                for k in range(n):
                    landed = piece(unit(k, 2 * cx + cy, c), k, q)
                    rcopy(k, j * AG_PIECES + q, landed, landed, (cx, cy, c)).wait_recv()
                    cp = rcopy(k, (3 + j) * AG_PIECES + q, landed, landed, sibling)
                    cp.start()
                    sends.append(cp)
        for q in range(AG_PIECES):
            for j, (cx, cy) in enumerate(chips):
                for k in range(n):
                    other = piece(unit(k, 2 * cx + cy, 1 - c), k, q)
                    rcopy(k, (3 + j) * AG_PIECES + q, other, other, sibling).wait_recv()
        for cp in sends:
            cp.wait_send()
        for cp in local:
            cp.wait()

    vm = pl.BlockSpec(memory_space=pltpu.VMEM)
    outs = pl.pallas_call(
        body, name="ag_weights",
        out_shape=tuple(jax.ShapeDtypeStruct(_FULL_SHAPES[w], BF) for w in tuple(ws) + tuple(later_ws))
        + (jax.ShapeDtypeStruct((N_HEADS, QB, KB), F32),),
        in_specs=[vm] * (n + m + 1), out_specs=[_ANY] * (n + m) + [vm],
        scratch_shapes=[pltpu.VMEM(_SHARD_SHAPES[w], BF) for w in tuple(ws) + tuple(later_ws)]
        + [pltpu.SemaphoreType.DMA((n, 6 * AG_PIECES)), pltpu.SemaphoreType.DMA((n, 6 * AG_PIECES)),
           pltpu.SemaphoreType.DMA((n, 2)), pltpu.SemaphoreType.DMA((m,))],
        compiler_params=_params(48, collective_id=_PEER_SETS["both"]),
    )(*shards, *later_shards, gp)
    return list(outs[:n]), list(outs[n:n + m]), outs[-1]


def _shard_of(ref, w, s):
    if w == 0:
        return ref.at[:, pl.ds(_mo(s * SHARD_IN, 128), SHARD_IN)]
    if w == 3:
        return ref.at[pl.ds(_mo(s * 256, 256), 256), :]
    return ref.at[:, pl.ds(_mo(s * 256, 128), 256)]


def _gather_copies(ws):
    def copies(refs, send_sems, recv_sems):
        x, y, c, chips = _mesh_pos()
        out = []
        for j, (cx, cy) in enumerate(chips):
            for k, w in enumerate(ws):
                mine = _shard_of(refs[k], w, 2 * x + y)
                out.append(pltpu.make_async_remote_copy(
                    src_ref=mine, dst_ref=mine, send_sem=send_sems.at[3 * k + j], recv_sem=recv_sems.at[3 * k + j],
                    device_id=(cx, cy, c), device_id_type=MESH))
        return out
    return copies


def _inproj_fwd(x, norm_g, w_in_bf, tm=512, after=()):
    S = x.shape[0]

    def body(x_ref, g_ref, w_ref, ht_ref, q_ref, k_ref, v_ref, zr_ref):
        xv = x_ref[...]
        r = lax.rsqrt(jnp.mean(xv * xv, axis=-1, keepdims=True) + EPS)
        hf = (xv * r) * g_ref[...]
        ht_ref[...] = hf.T.astype(BF)
        h = hf.astype(BF)
        heads = (q_ref, k_ref, v_ref)
        for j in range(D_IN // 512):
            z = _dot(h, w_ref[:, j * 512:(j + 1) * 512])
            if j < 3:
                zb = z.astype(BF)
                for hd in range(N_HEADS):
                    heads[j][hd] = zb[:, hd * HEAD_DIM:(hd + 1) * HEAD_DIM]
            else:
                zr_ref[:, (j - 3) * 512:(j - 2) * 512] = z

    head_major = jax.ShapeDtypeStruct((N_HEADS, S, HEAD_DIM), BF)
    head_spec = pl.BlockSpec((N_HEADS, tm, HEAD_DIM), lambda i: (0, i, 0))
    return pl.pallas_call(
        _after(body, 3, after), name="inproj_fwd", grid=(S // tm,),
        out_shape=(jax.ShapeDtypeStruct((D_MODEL, S), BF), head_major, head_major, head_major,
                   jax.ShapeDtypeStruct((S, D_IN - 3 * D_A), F32)),
        in_specs=[pl.BlockSpec((tm, D_MODEL), lambda i: (i, 0)),
                  pl.BlockSpec((1, D_MODEL), lambda i: (0, 0)),
                  pl.BlockSpec((D_MODEL, D_IN), lambda i: (0, 0), pipeline_mode=pl.Buffered(1))]
        + [_ANY] * len(after),
        out_specs=[pl.BlockSpec((D_MODEL, tm), lambda i: (0, i)),
                   head_spec, head_spec, head_spec,
                   pl.BlockSpec((tm, D_IN - 3 * D_A), lambda i: (i, 0))],
        compiler_params=_params(52, dimension_semantics=("arbitrary",)),
    )(x, norm_g, w_in_bf, *after)


def _skew_table(gp_row):
    row = lax.broadcasted_iota(jnp.int32, (QB, ROLL_W), 0)
    t = jnp.broadcast_to(gp_row, (QB, ROLL_W))
    for b in range(7):
        t = jnp.where(((row >> b) & 1) == 1, pltpu.roll(t, 1 << b, axis=1), t)
    return t


def _unskew_sum(d):
    row = lax.broadcasted_iota(jnp.int32, (QB, ROLL_W), 0)
    for b in range(7):
        d = jnp.where(((row >> b) & 1) == 1, pltpu.roll(d, ROLL_W - (1 << b), axis=1), d)
    return jnp.sum(d, axis=0, keepdims=True)


def _struct_mask():
    a = lax.broadcasted_iota(jnp.int32, (QB, KB), 0) // CHUNK
    b = lax.broadcasted_iota(jnp.int32, (QB, KB), 1) // CHUNK
    return (b >= a) & (b <= a + N_PREV)


def _load_kv(k_hbm, v_hbm, k_scr, v_scr, sems, S, meanwhile=lambda: None):
    zeros = jnp.zeros((N_HEADS, PADK, HEAD_DIM), BF)
    k_scr[:, 0:PADK, :] = zeros
    v_scr[:, 0:PADK, :] = zeros
    ck = pltpu.make_async_copy(k_hbm, k_scr.at[:, pl.ds(PADK, S), :], sems.at[0])
    cv = pltpu.make_async_copy(v_hbm, v_scr.at[:, pl.ds(PADK, S), :], sems.at[1])
    ck.start()
    cv.start()
    meanwhile()
    ck.wait()
    cv.wait()


_BATCH_NT = (((2,), (2,)), ((0,), (0,)))
_BATCH_NN = (((2,), (1,)), ((0,), (0,)))
_BATCH_TN = (((1,), (1,)), ((0,), (0,)))


def _bdot(a, b, dims):
    return lax.dot_general(a, b, dims, preferred_element_type=F32)


def _scaled(q):
    return q * jnp.asarray(SCALE, BF)


def _scores(qs, kb, bias, i, front):
    s = _bdot(qs, kb, _BATCH_NT) + bias
    if front:
        col = lax.broadcasted_iota(jnp.int32, (1, 1, KB), 2)
        s = jnp.where(col >= PADK - i * QB, s, NEG_INF)
    return s


def _attn_fwd(q3, k3, v3, bias):
    S = q3.shape[1]

    def body(q_ref, k_hbm, v_hbm, bias_ref, o_ref, lse_ref, k_scr, v_scr, sems):
        @pl.when(pl.program_id(0) == 0)
        def _():
            _load_kv(k_hbm, v_hbm, k_scr, v_scr, sems, S)

        def step(i, rows, front):
            start = pl.multiple_of(i * QB, QB)
            kb = k_scr[:, pl.ds(start, KB), :]
            vb = v_scr[:, pl.ds(start, KB), :]
            s = _scores(_scaled(q_ref[:, rows, :]), kb, bias_ref[...], i, front)
            m = jnp.max(s, axis=-1, keepdims=True)
            e = jnp.exp(s - m)
            l = jnp.sum(e, axis=-1, keepdims=True)
            p = e * (1.0 / l)
            o = _bdot(p.astype(BF), vb, _BATCH_NN)
            lse_ref[:, rows, :] = jnp.broadcast_to(m + jnp.log(l), (N_HEADS, QB, 128))
            for h in range(N_HEADS):
                o_ref[rows, h * HEAD_DIM:(h + 1) * HEAD_DIM] = o[h]

        def block(j, carry):
            i = pl.program_id(0) * Q_PER_STEP + j
            rows = pl.ds(pl.multiple_of(j * QB, QB), QB)
            pl.when(i < KEEP)(functools.partial(step, i, rows, True))
            pl.when(i >= KEEP)(functools.partial(step, i, rows, False))
            return carry

        lax.fori_loop(0, Q_PER_STEP, block, 0)

    rows_per_step = Q_PER_STEP * QB
    kv_scr = pltpu.VMEM((N_HEADS, S + PADK, HEAD_DIM), BF)
    return pl.pallas_call(
        body, name="attn_fwd", grid=(S // rows_per_step,),
        out_shape=(jax.ShapeDtypeStruct((S, D_A), F32), jax.ShapeDtypeStruct((N_HEADS, S, 128), F32)),
        in_specs=[pl.BlockSpec((N_HEADS, rows_per_step, HEAD_DIM), lambda g: (0, g, 0)),
                  pl.BlockSpec(memory_space=pl.ANY), pl.BlockSpec(memory_space=pl.ANY),
                  pl.BlockSpec((N_HEADS, QB, KB), lambda g: (0, 0, 0))],
        out_specs=[pl.BlockSpec((rows_per_step, D_A), lambda g: (g, 0)),
                   pl.BlockSpec((N_HEADS, rows_per_step, 128), lambda g: (0, g, 0))],
        scratch_shapes=[kv_scr, kv_scr, pltpu.SemaphoreType.DMA((2,))],
        compiler_params=_params(48, dimension_semantics=("arbitrary",)),
    )(q3, k3, v3, bias)


def _attn_bwd(q3, k3, v3, d_att3, lse, bias, after=()):
    S = q3.shape[1]
    nq = S // QB

    def body(q_ref, do_ref, k_hbm, v_hbm, lse_ref, bias_ref, dq_ref, dk_ref, dv_ref, dgp_ref,
             k_scr, v_scr, dk_acc, dv_acc, dbias_acc, pad_scr, sems):
        @pl.when(pl.program_id(0) == 0)
        def _():
            def clear():
                dk_acc[...] = jnp.zeros_like(dk_acc)
                dv_acc[...] = jnp.zeros_like(dv_acc)
                dbias_acc[...] = jnp.zeros_like(dbias_acc)
            _load_kv(k_hbm, v_hbm, k_scr, v_scr, sems, S, clear)

        def step(i, rows, front):
            start = pl.multiple_of(i * QB, QB)
            kb = k_scr[:, pl.ds(start, KB), :]
            vb = v_scr[:, pl.ds(start, KB), :]
            qs = _scaled(q_ref[:, rows, :])
            do = do_ref[:, rows, :]
            p = jnp.exp(_scores(qs, kb, bias_ref[...], i, front) - jnp.tile(lse_ref[:, rows, :], (1, 1, KB // 128)))
            dp = _bdot(do, vb, _BATCH_NT)
            ds = p * (dp - jnp.sum(dp * p, axis=-1, keepdims=True))
            dbias_acc[...] += ds
            dsb = ds.astype(BF)
            dq = _bdot(dsb, kb, _BATCH_NN) * SCALE
            for h in range(N_HEADS):
                dq_ref[rows, h * HEAD_DIM:(h + 1) * HEAD_DIM] = dq[h].astype(BF)
            dk_acc[...] += _bdot(dsb, qs, _BATCH_TN)
            dv_acc[...] += _bdot(p.astype(BF), do, _BATCH_TN)

        def block(j, carry):
            i = pl.program_id(0) * Q_PER_STEP + j
            rows = pl.ds(pl.multiple_of(j * QB, QB), QB)
            pl.when(i < KEEP)(functools.partial(step, i, rows, True))
            pl.when((i >= KEEP) & (i < nq))(functools.partial(step, i, rows, False))
            for h in range(N_HEADS):
                hs = slice(h * HEAD_DIM, (h + 1) * HEAD_DIM)
                dk_ref[rows, hs] = dk_acc[h, 0:QB, :].astype(BF)
                dv_ref[rows, hs] = dv_acc[h, 0:QB, :].astype(BF)
            dk_acc[:, 0:KB - QB, :] = dk_acc[:, QB:KB, :]
            dv_acc[:, 0:KB - QB, :] = dv_acc[:, QB:KB, :]
            dk_acc[:, KB - QB:KB, :] = jnp.zeros((N_HEADS, QB, HEAD_DIM), F32)
            dv_acc[:, KB - QB:KB, :] = jnp.zeros((N_HEADS, QB, HEAD_DIM), F32)
            return carry

        lax.fori_loop(0, Q_PER_STEP, block, 0)

        @pl.when(pl.program_id(0) == n_steps - 1)
        def _():
            lane = lax.broadcasted_iota(jnp.int32, (1, ROLL_W), 1)
            hi = (lane < 384) | (lane >= 832)
            lo = (lane > 640) & (lane < 832)
            pad_scr[...] = jnp.zeros_like(pad_scr)
            for h in range(N_HEADS):
                pad_scr[:, 0:KB] = dbias_acc[h]
                g = _unskew_sum(pad_scr[...])
                s_hi = jnp.sum(jnp.where(hi, g, 0.0), axis=-1, keepdims=True)
                s_lo = jnp.sum(jnp.where(lo, g, 0.0), axis=-1, keepdims=True)
                g = jnp.where(lane == 384, g + s_hi, g)
                g = jnp.where(lane == 640, g + s_lo, g)
                dgp_ref[h:h + 1, :] = g

    assert nq % Q_PER_STEP == 0 and KEEP % Q_PER_STEP == 0
    rows_per_step = Q_PER_STEP * QB
    n_steps = (nq + KEEP) // Q_PER_STEP
    last = nq // Q_PER_STEP - 1
    lag = KEEP // Q_PER_STEP
    kv_scr = pltpu.VMEM((N_HEADS, S + PADK, HEAD_DIM), BF)
    return pl.pallas_call(
        _after(body, 6, after), name="attn_bwd", grid=(n_steps,),
        out_shape=(jax.ShapeDtypeStruct((S, D_A), BF), jax.ShapeDtypeStruct((S, D_A), BF),
                   jax.ShapeDtypeStruct((S, D_A), BF), jax.ShapeDtypeStruct((N_HEADS, ROLL_W), F32)),
        in_specs=[pl.BlockSpec((N_HEADS, rows_per_step, HEAD_DIM), lambda g: (0, jnp.minimum(g, last), 0)),
                  pl.BlockSpec((N_HEADS, rows_per_step, HEAD_DIM), lambda g: (0, jnp.minimum(g, last), 0)),
                  pl.BlockSpec(memory_space=pl.ANY), pl.BlockSpec(memory_space=pl.ANY),
                  pl.BlockSpec((N_HEADS, rows_per_step, 128), lambda g: (0, jnp.minimum(g, last), 0)),
                  pl.BlockSpec((N_HEADS, QB, KB), lambda g: (0, 0, 0))] + [_ANY] * len(after),
        out_specs=[pl.BlockSpec((rows_per_step, D_A), lambda g: (jnp.minimum(g, last), 0)),
                   pl.BlockSpec((rows_per_step, D_A), lambda g: (jnp.maximum(g - lag, 0), 0)),
                   pl.BlockSpec((rows_per_step, D_A), lambda g: (jnp.maximum(g - lag, 0), 0)),
                   pl.BlockSpec((N_HEADS, ROLL_W), lambda g: (0, 0))],
        scratch_shapes=[kv_scr, kv_scr,
                        pltpu.VMEM((N_HEADS, KB, HEAD_DIM), F32), pltpu.VMEM((N_HEADS, KB, HEAD_DIM), F32),
                        pltpu.VMEM((N_HEADS, QB, KB), F32), pltpu.VMEM((QB, ROLL_W), F32),
                        pltpu.SemaphoreType.DMA((2,))],
        compiler_params=_params(56, dimension_semantics=("arbitrary",)),
    )(q3, d_att3, k3, v3, lse, bias, *after)


def _sgu_core(ub, vb, lg, lb):
    u, du = _gelu_and_grad(ub)
    v, dv = _gelu_and_grad(vb)
    mu = jnp.mean(v, axis=-1, keepdims=True)
    vc = v - mu
    rstd = lax.rsqrt(jnp.mean(vc * vc, axis=-1, keepdims=True) + EPS)
    xh = vc * rstd
    vn = xh * lg + lb
    return u, du, dv, rstd, xh, vn


def _tri():
    r = lax.broadcasted_iota(jnp.int32, (SGU_CHUNK, SGU_CHUNK), 0)
    c = lax.broadcasted_iota(jnp.int32, (SGU_CHUNK, SGU_CHUNK), 1)
    return r >= c


def _tail_sgu(att, zrest, x, target, w_pa, w_pb, w_out, b_gate, final_g, ln_g, ln_b, w_s, b_s_t, tm=256):
    S = x.shape[0]
    nt = S // tm
    chunks = tm // SGU_CHUNK

    def body(att_ref, ga_ref, ub_ref, vb_ref, gb_ref, gta_ref, gtb_ref, x_ref, t_ref,
             wpa_ref, wpb_ref, wout_ref, bg_ref, fg_ref, lg_ref, lb_ref, ws_ref, bst_ref,
             dout_ref, datt_ref, dzt_ref, dzs_ref, gwout_hbm, gwpa_hbm, gwpb_hbm,
             gbg_ref, gfg_ref, loss_ref, gws_ref, gbs_ref, glg_ref, glb_ref,
             acc_out, acc_pa, acc_pb, sg_scr, mix_scr, dvn_scr, bs_acc, sems):
        i = pl.program_id(0)

        @pl.when(i == 0)
        def _():
            for r in (acc_out, acc_pa, acc_pb, gbg_ref, gfg_ref, loss_ref, gws_ref, glg_ref, glb_ref, bs_acc):
                r[...] = jnp.zeros_like(r)

        u, du, dv, rstd, xh, vn = _sgu_core(ub_ref[...], vb_ref[...], lg_ref[...], lb_ref[...])
        vnb = vn.astype(BF)
        tri = _tri()
        blocks = [(g, slice(n * SGU_CHUNK, (n + 1) * SGU_CHUNK), slice(g * 128, (g + 1) * 128))
                  for g in range(N_GROUPS) for n in range(chunks)]
        wts = [jnp.where(tri, ws_ref[g], 0.0) for g in range(N_GROUPS)]
        for g, rs, cs in blocks:
            mixed = _dot(wts[g].astype(BF), vnb[rs, cs]) + bst_ref[:, g:g + 1]
            mix_scr[rs, cs] = mixed
            sg_scr[rs, cs] = u[rs, cs] * mixed

        att = att_ref[...]
        sg = sg_scr[...]
        sa, dsa = _silu_and_grad(ga_ref[...])
        sb, dsb = _silu_and_grad(gb_ref[...])
        ya = (att * sa).astype(BF)
        yb = (sg * sb).astype(BF)
        pa = _dot(ya, wpa_ref[...])
        pb = _dot(yb, wpb_ref[...])
        ga = _sigmoid(gta_ref[...] + bg_ref[:, 0:D_MODEL])
        gb = _sigmoid(gtb_ref[...] + bg_ref[:, D_MODEL:2 * D_MODEL])
        merged = (ga * pa + gb * pb).astype(BF)
        out = x_ref[...] + _dot(merged, wout_ref[...])
        r2 = lax.rsqrt(jnp.mean(out * out, axis=-1, keepdims=True) + EPS)
        nrm = out * r2
        fg = fg_ref[...]
        err = nrm * fg - t_ref[...]
        loss_ref[...] += 0.5 * jnp.sum(jnp.mean(err * err, axis=-1, keepdims=True))
        dy = err * (1.0 / D_MODEL)
        gfg_ref[...] += jnp.sum(dy * nrm, axis=0, keepdims=True)
        dn = dy * fg
        d_out = r2 * (dn - nrm * jnp.mean(dn * nrm, axis=-1, keepdims=True))
        dout_ref[...] = d_out
        d_outb = d_out.astype(BF)
        acc_out[...] += _dot_tn(merged, d_outb)
        dm = _dot_nt(d_outb, wout_ref[...])
        d_pa = (dm * ga).astype(BF)
        d_pb = (dm * gb).astype(BF)
        d_gta = dm * pa * (ga * (1.0 - ga))
        d_gtb = dm * pb * (gb * (1.0 - gb))
        gbg_ref[:, 0:D_MODEL] += jnp.sum(d_gta, axis=0, keepdims=True)
        gbg_ref[:, D_MODEL:2 * D_MODEL] += jnp.sum(d_gtb, axis=0, keepdims=True)
        dzt_ref[:, 2 * D_A:2 * D_A + D_MODEL] = d_gta.astype(BF)
        dzt_ref[:, 2 * D_A + D_MODEL:] = d_gtb.astype(BF)
        acc_pa[...] += _dot_tn(ya, d_pa)
        acc_pb[...] += _dot_tn(yb, d_pb)
        d_ya = _dot_nt(d_pa, wpa_ref[...])
        d_yb = _dot_nt(d_pb, wpb_ref[...])
        d_att = (d_ya * sa).astype(BF)
        for hd in range(N_HEADS):
            datt_ref[hd] = d_att[:, hd * HEAD_DIM:(hd + 1) * HEAD_DIM]
        dzt_ref[:, 0:D_A] = (d_ya * att * dsa).astype(BF)
        dzt_ref[:, D_A:2 * D_A] = (d_yb * sg * dsb).astype(BF)

        dsg = d_yb * sb
        dzs_ref[:, 0:D_B] = (dsg * mix_scr[...] * du).astype(BF)
        dmix = dsg * u
        for g, rs, cs in blocks:
            dmb = dmix[rs, cs].astype(BF)
            bs_acc[:, cs] += dmix[rs, cs]
            gws_ref[g] += _dot_nt(dmb, vnb[rs, cs])
            dvn_scr[rs, cs] = _dot(wts[g].T.astype(BF), dmb)
        dvn = dvn_scr[...]
        glg_ref[...] += jnp.sum(dvn * xh, axis=0, keepdims=True)
        glb_ref[...] += jnp.sum(dvn, axis=0, keepdims=True)
        dxh = dvn * lg_ref[...]
        dvv = rstd * (dxh - jnp.mean(dxh, axis=-1, keepdims=True)
                      - xh * jnp.mean(dxh * xh, axis=-1, keepdims=True))
        dzs_ref[:, D_B:2 * D_B] = (dvv * dv).astype(BF)

        @pl.when(i == nt - 1)
        def _():
            cps = [pltpu.make_async_copy(acc_out, gwout_hbm, sems.at[0]),
                   pltpu.make_async_copy(acc_pa, gwpa_hbm, sems.at[1]),
                   pltpu.make_async_copy(acc_pb, gwpb_hbm, sems.at[2])]
            for cp in cps:
                cp.start()
            lane = lax.broadcasted_iota(jnp.int32, (SGU_CHUNK, 128), 1)
            cols = jnp.zeros((SGU_CHUNK, 128), F32)
            for g in range(N_GROUPS):
                gws_ref[g] = jnp.where(tri, gws_ref[g], 0.0)
                col = jnp.sum(bs_acc[:, g * 128:(g + 1) * 128], axis=-1, keepdims=True)
                cols = jnp.where(lane == g, col, cols)
            gbs_ref[...] = cols
            for cp in cps:
                cp.wait()

    c2 = lambda i: (0, 0)
    c3 = lambda i: (0, 0, 0)
    zcol = lambda w, blk: pl.BlockSpec((tm, w), lambda i: (i, blk))
    row = lambda w: pl.BlockSpec((tm, w), lambda i: (i, 0))
    return pl.pallas_call(
        body, name="tail", grid=(nt,),
        out_shape=(jax.ShapeDtypeStruct((S, D_MODEL), F32), jax.ShapeDtypeStruct((N_HEADS, S, HEAD_DIM), BF),
                   jax.ShapeDtypeStruct((S, 3072), BF), jax.ShapeDtypeStruct((S, 2 * D_B), BF),
                   jax.ShapeDtypeStruct((D_MODEL, D_MODEL), F32), jax.ShapeDtypeStruct((D_A, D_MODEL), F32),
                   jax.ShapeDtypeStruct((D_B, D_MODEL), F32),
                   jax.ShapeDtypeStruct((1, 2 * D_MODEL), F32), jax.ShapeDtypeStruct((1, D_MODEL), F32),
                   jax.ShapeDtypeStruct((1, 128), F32),
                   jax.ShapeDtypeStruct((N_GROUPS, 128, 128), F32), jax.ShapeDtypeStruct((SGU_CHUNK, 128), F32),
                   jax.ShapeDtypeStruct((1, D_B), F32), jax.ShapeDtypeStruct((1, D_B), F32)),
        in_specs=[row(D_A), zcol(512, 0), zcol(512, 1), zcol(512, 2), zcol(512, 3),
                  zcol(D_MODEL, 2), zcol(D_MODEL, 3), row(D_MODEL), row(D_MODEL),
                  pl.BlockSpec((D_A, D_MODEL), c2), pl.BlockSpec((D_B, D_MODEL), c2),
                  pl.BlockSpec((D_MODEL, D_MODEL), c2),
                  pl.BlockSpec((1, 2 * D_MODEL), c2), pl.BlockSpec((1, D_MODEL), c2),
                  pl.BlockSpec((1, D_B), c2), pl.BlockSpec((1, D_B), c2),
                  pl.BlockSpec((N_GROUPS, 128, 128), c3), pl.BlockSpec((128, N_GROUPS), c2)],
        out_specs=[row(D_MODEL), pl.BlockSpec((N_HEADS, tm, HEAD_DIM), lambda i: (0, i, 0)),
                   row(3072), row(2 * D_B), _ANY, _ANY, _ANY,
                   pl.BlockSpec((1, 2 * D_MODEL), c2), pl.BlockSpec((1, D_MODEL), c2),
                   pl.BlockSpec((1, 128), c2),
                   pl.BlockSpec((N_GROUPS, 128, 128), c3), pl.BlockSpec((SGU_CHUNK, 128), c2),
                   pl.BlockSpec((1, D_B), c2), pl.BlockSpec((1, D_B), c2)],
        scratch_shapes=[pltpu.VMEM((D_MODEL, D_MODEL), F32), pltpu.VMEM((D_A, D_MODEL), F32),
                        pltpu.VMEM((D_B, D_MODEL), F32),
                        pltpu.VMEM((tm, D_B), F32), pltpu.VMEM((tm, D_B), F32), pltpu.VMEM((tm, D_B), F32),
                        pltpu.VMEM((SGU_CHUNK, D_B), F32), pltpu.SemaphoreType.DMA((3,))],
        compiler_params=_params(58, dimension_semantics=("arbitrary",)),
    )(att, zrest, zrest, zrest, zrest, zrest, zrest, x, target, w_pa, w_pb, w_out, b_gate, final_g,
      ln_g, ln_b, w_s, b_s_t)


_DZ_MAP = ((0, 0), (1, 0), (2, 0), (3, 0), (4, 0), (4, 1), (3, 1), (3, 2), (3, 3), (3, 4), (3, 5))


def _dh_gradx(dq, dk, dv, dzt, dzs, w_in_bf, x, norm_g, d_out, tm=512, after=()):
    S = x.shape[0]

    def body(dq_ref, dk_ref, dv_ref, dzt_ref, dzs_ref, w_ref, x_ref, g_ref, dout_ref, gx_ref, gn_ref):
        i = pl.program_id(0)

        @pl.when(i == 0)
        def _():
            gn_ref[...] = jnp.zeros_like(gn_ref)

        pieces = (dq_ref, dk_ref, dv_ref, dzt_ref, dzs_ref)
        dh = jnp.zeros((tm, D_MODEL), F32)
        for j, (pc, blk) in enumerate(_DZ_MAP):
            dh += _dot_nt(pieces[pc][:, blk * 512:(blk + 1) * 512], w_ref[:, j * 512:(j + 1) * 512])
        xv = x_ref[...]
        r = lax.rsqrt(jnp.mean(xv * xv, axis=-1, keepdims=True) + EPS)
        nrm = xv * r
        gn_ref[...] += jnp.sum(dh * nrm, axis=0, keepdims=True)
        dn = dh * g_ref[...]
        gx_ref[...] = r * (dn - nrm * jnp.mean(dn * nrm, axis=-1, keepdims=True)) + dout_ref[...]

    row = lambda w: pl.BlockSpec((tm, w), lambda i: (i, 0))
    c2 = lambda i: (0, 0)
    return pl.pallas_call(
        _after(body, 9, after), name="dh_gradx", grid=(S // tm,),
        out_shape=(jax.ShapeDtypeStruct((S, D_MODEL), F32), jax.ShapeDtypeStruct((1, D_MODEL), F32)),
        in_specs=[row(512), row(512), row(512), row(3072), row(1024),
                  pl.BlockSpec((D_MODEL, D_IN), c2, pipeline_mode=pl.Buffered(1)), row(D_MODEL),
                  pl.BlockSpec((1, D_MODEL), c2), row(D_MODEL)]
        + [_ANY] * len(after),
        out_specs=[row(D_MODEL), pl.BlockSpec((1, D_MODEL), c2)],
        compiler_params=_params(48, dimension_semantics=("arbitrary",)),
    )(dq, dk, dv, dzt, dzs, w_in_bf, x, norm_g, d_out, *after)


def _gw_in(ht, dq, dk, dv, dzt, dzs, tn=512, after=()):
    S = ht.shape[1]
    per = 512 // tn
    cols = tuple((pc, per * blk + h) for pc, blk in _DZ_MAP for h in range(per))

    def body(ht_ref, dq_ref, dk_ref, dv_ref, dzt_ref, dzs_ref, o_ref, ob_ref):
        j = pl.program_id(0)
        pieces = (dq_ref, dk_ref, dv_ref, dzt_ref, dzs_ref)
        for pc in range(5):
            hit = functools.reduce(jnp.logical_or, [j == jj for jj, (p, _) in enumerate(cols) if p == pc])

            @pl.when(hit)
            def _(pc=pc):
                g = _dot(ht_ref[...], pieces[pc][...])
                o_ref[...] = g
                ob_ref[...] = g.astype(BF)

    def piece_spec(pc):
        cur = next(blk for p, blk in cols if p == pc)
        held = []
        for p, blk in cols:
            cur = blk if p == pc else cur
            held.append(cur)

        def index_map(j):
            blk = jnp.int32(held[0])
            for jj in range(1, len(held)):
                if held[jj] != held[jj - 1]:
                    blk = jnp.where(j >= jj, jnp.int32(held[jj]), blk)
            return (0, blk)

        return pl.BlockSpec((S, tn), index_map)

    return pl.pallas_call(
        _after(body, 6, after), name="gw_in", grid=(len(cols),),
        out_shape=(jax.ShapeDtypeStruct((D_MODEL, D_IN), F32), jax.ShapeDtypeStruct((D_MODEL, D_IN), BF)),
        in_specs=[pl.BlockSpec((D_MODEL, S), lambda j: (0, 0), pipeline_mode=pl.Buffered(1))]
        + [piece_spec(pc) for pc in range(5)]
        + [_ANY] * len(after),
        out_specs=[pl.BlockSpec((D_MODEL, tn), lambda j: (0, j)), pl.BlockSpec((D_MODEL, tn), lambda j: (0, j))],
        compiler_params=_params(56, dimension_semantics=("arbitrary",)),
    )(ht, dq, dk, dv, dzt, dzs, *after)


_HBM = pl.BlockSpec(memory_space=pltpu.HBM)
_SEM = pl.BlockSpec(memory_space=pltpu.SEMAPHORE)
_ANY = pl.BlockSpec(memory_space=pl.ANY)
_EFFECT = pltpu.SideEffectType.DATAFLOW_SIDE_EFFECTING


def _in_hbm(a):
    return pltpu.with_memory_space_constraint(a, pltpu.HBM)


def _after(body, n_in, after):
    if not after:
        return body
    return lambda *refs: body(*refs[:n_in], *refs[n_in + len(after):])


class _Started:
    def __init__(self, send, recv, bufs, token):
        self.send, self.recv, self.bufs, self.token = send, recv, bufs, token


_PEER_SETS = {"sibling": 7, "chips": 8, "both": 9}


def _peers(kind):
    x, y, c, chips = _mesh_pos()
    return ([(x, y, 1 - c)] if kind in ("sibling", "both") else []) + (
        [(cx, cy, c) for cx, cy in chips] if kind in ("chips", "both") else [])


def _signal_peers(kind):
    barrier = pltpu.get_barrier_semaphore()
    targets = _peers(kind)
    for peer in targets:
        pl.semaphore_signal(barrier, inc=1, device_id=peer, device_id_type=MESH)
    return lambda: pl.semaphore_wait(barrier, len(targets))


def _split_start(name, bufs, n_copies, copies, peers, after=()):
    nb = len(bufs)

    def body(*refs):
        _signal_peers(peers)()
        refs = refs[:nb] + refs[nb + len(after):]
        for cp in copies(refs[:nb], refs[nb], refs[nb + 1]):
            cp.start()
        refs[-1][...] = jnp.zeros_like(refs[-1])

    outs = pl.pallas_call(
        body, name=name,
        out_shape=(pltpu.SemaphoreType.DMA((n_copies,)), pltpu.SemaphoreType.DMA((n_copies,)),
                   *[pltpu.HBM(b.shape, b.dtype) for b in bufs], jax.ShapeDtypeStruct((8, 128), F32)),
        in_specs=[_HBM] * nb + [_ANY] * len(after),
        out_specs=(_SEM, _SEM, *[_HBM] * nb, pl.BlockSpec(memory_space=pltpu.VMEM)),
        input_output_aliases={k: 2 + k for k in range(nb)},
        compiler_params=_params(1, has_side_effects=_EFFECT, collective_id=_PEER_SETS[peers]),
    )(*[_in_hbm(b) for b in bufs], *after)
    return _Started(outs[0], outs[1], list(outs[2:2 + nb]), outs[-1])


def _split_wait(name, started, copies, after):
    nb = len(started.bufs)

    def body(*refs):
        for cp in copies(refs[:nb], refs[nb], refs[nb + 1]):
            cp.wait_send()
            cp.wait_recv()

    return list(pl.pallas_call(
        body, name=name,
        out_shape=tuple(pltpu.HBM(b.shape, b.dtype) for b in started.bufs),
        in_specs=[_HBM] * nb + [_SEM, _SEM, _ANY],
        out_specs=tuple([_HBM] * nb),
        input_output_aliases={k: k for k in range(nb)},
        compiler_params=_params(1, has_side_effects=_EFFECT),
    )(*started.bufs, started.send, started.recv, after))


def _x1_copies(ws):
    def copies(refs, send_sems, recv_sems):
        x, y, c, _ = _mesh_pos()
        out = []
        for k, w in enumerate(ws):
            for s in range(N_SHARD):
                out.append(pltpu.make_async_remote_copy(
                    src_ref=_UNITS[w](refs[k], s, 1 - c), dst_ref=refs[len(ws) + k].at[s],
                    send_sem=send_sems.at[N_SHARD * k + s], recv_sem=recv_sems.at[N_SHARD * k + s],
                    device_id=(x, y, 1 - c), device_id_type=MESH))
        return out
    return copies


def _x2_copies(n):
    def copies(refs, send_sems, recv_sems):
        x, y, c, chips = _mesh_pos()
        out = []
        for j, (cx, cy) in enumerate(chips):
            for k in range(n):
                out.append(pltpu.make_async_remote_copy(
                    src_ref=refs[k].at[2 * cx + cy], dst_ref=refs[n + k].at[j],
                    send_sem=send_sems.at[3 * k + j], recv_sem=recv_sems.at[3 * k + j],
                    device_id=(cx, cy, c), device_id_type=MESH))
        return out
    return copies


def _x3_copies(ws):
    def copies(refs, send_sems, recv_sems):
        x, y, c, _ = _mesh_pos()
        out = []
        for k, w in enumerate(ws):
            rows = _HALF_ROWS[w]
            mine = refs[k].at[pl.ds(_mo(c * rows, rows), rows), :]
            out.append(pltpu.make_async_remote_copy(
                src_ref=mine, dst_ref=mine, send_sem=send_sems.at[k], recv_sem=recv_sems.at[k],
                device_id=(x, y, 1 - c), device_id_type=MESH))
        return out
    return copies


def _x1_lands(ws, dtype=F32):
    return [lax.empty((N_SHARD,) + _UNIT_SHAPES[w], dtype) for w in ws]


def _x2_lands(ws):
    return [lax.empty((3,) + _UNIT_SHAPES[w], BF) for w in ws]


def _grad_add1(w, g, recv, pos):
    ur, uc = _UNIT_SHAPES[w]
    nt = 2
    tr = ur // nt

    def body(pos_ref, g_ref, r_ref, csb_ref):
        csb_ref[0] = (g_ref[...] + r_ref[0].astype(F32)).astype(BF)

    u3 = lambda k, t, pos: (pos[2 + k], t, 0)
    return pl.pallas_call(
        body, name=f"grad_add1_{w}",
        grid_spec=pltpu.PrefetchScalarGridSpec(
            num_scalar_prefetch=1, grid=(N_SHARD - 1, nt),
            in_specs=[pl.BlockSpec((tr, uc), lambda k, t, pos: (pos[0] * nt + t, pos[2 + k])),
                      pl.BlockSpec((1, tr, uc), u3)],
            out_specs=pl.BlockSpec((1, tr, uc), u3)),
        out_shape=jax.ShapeDtypeStruct((N_SHARD, ur, uc), BF),
        compiler_params=_params(40, dimension_semantics=("arbitrary", "arbitrary")),
    )(pos, g, recv)


def _grad_add1_group(ws, gs, recvs, pos):
    n = len(ws)

    def body(pos_ref, *refs):
        s = pl.program_id(0)
        for k in range(n):
            g, r, own, csb = refs[k], refs[n + k], refs[2 * n + k], refs[3 * n + k]
            v = g[...] + r[0]
            csb[0] = v.astype(BF)

            @pl.when(s == pos_ref[1])
            def _(own=own, v=v):
                own[...] = v

    def g_spec(w):
        if w == 3:
            return pl.BlockSpec(_UNIT_SHAPES[w], lambda s, pos: (2 * s + pos[0], 0))
        return pl.BlockSpec(_UNIT_SHAPES[w], lambda s, pos: (pos[0], s))

    slot = lambda w: pl.BlockSpec((1,) + _UNIT_SHAPES[w], lambda s, pos: (s, 0, 0))
    outs = pl.pallas_call(
        body, name="grad_add1_group",
        grid_spec=pltpu.PrefetchScalarGridSpec(
            num_scalar_prefetch=1, grid=(N_SHARD,),
            in_specs=[g_spec(w) for w in ws] + [slot(w) for w in ws],
            out_specs=[pl.BlockSpec(_UNIT_SHAPES[w], lambda s, pos: (0, 0)) for w in ws] + [slot(w) for w in ws]),
        out_shape=tuple(jax.ShapeDtypeStruct(_UNIT_SHAPES[w], F32) for w in ws)
        + tuple(jax.ShapeDtypeStruct((N_SHARD,) + _UNIT_SHAPES[w], BF) for w in ws),
        compiler_params=_params(32, dimension_semantics=("arbitrary",)),
    )(pos, *gs, *recvs)
    return list(outs[:n]), list(outs[n:])


def _grad_add2_group(ws, owns, recvs):
    n = len(ws)

    def body(*refs):
        c = lax.axis_index("c")
        for k, w in enumerate(ws):
            own, r, o = refs[k], refs[n + k], refs[2 * n + k]
            rows = _HALF_ROWS[w]
            total = ((own[...] + r[0].astype(F32)) + r[1].astype(F32)) + r[2].astype(F32)
            o[pl.ds(_mo(c * rows, rows), rows), :] = total

    vm = pl.BlockSpec(memory_space=pltpu.VMEM)
    return list(pl.pallas_call(
        body, name="grad_add2_group",
        out_shape=tuple(jax.ShapeDtypeStruct(_SHARD_SHAPES[w], F32) for w in ws),
        in_specs=[vm] * (2 * n), out_specs=[vm] * n,
        compiler_params=_params(32),
    )(*owns, *recvs))


def _grad_add2(w, g, recv1, recv2, pos):
    ur, uc = _UNIT_SHAPES[w]
    nt = 4
    tr = ur // nt

    def body(pos_ref, g_ref, r1_ref, r2_ref, o_ref):
        own = g_ref[...] + r1_ref[0].astype(F32)
        o_ref[...] = ((own + r2_ref[0].astype(F32)) + r2_ref[1].astype(F32)) + r2_ref[2].astype(F32)

    mine = lambda t, pos: (pos[0] * nt + t, 0)
    return pl.pallas_call(
        body, name=f"grad_add2_{w}",
        grid_spec=pltpu.PrefetchScalarGridSpec(
            num_scalar_prefetch=1, grid=(nt,),
            in_specs=[pl.BlockSpec((tr, uc), lambda t, pos: (pos[0] * nt + t, pos[1])),
                      pl.BlockSpec((1, tr, uc), lambda t, pos: (pos[1], t, 0)),
                      pl.BlockSpec((3, tr, uc), lambda t, pos: (0, t, 0))],
            out_specs=pl.BlockSpec((tr, uc), mine)),
        out_shape=jax.ShapeDtypeStruct(_SHARD_SHAPES[w], F32),
        compiler_params=_params(32, dimension_semantics=("arbitrary",)),
    )(pos, g, recv1, recv2)


def _adamw_math(w, g, m, v):
    m = ADAM_B1 * m + (1.0 - ADAM_B1) * g
    v = ADAM_B2 * v + (1.0 - ADAM_B2) * (g * g)
    m_hat = m / ADAM_C1
    v_hat = v / ADAM_C2
    delta = -ADAM_LR * (m_hat / (jnp.sqrt(v_hat) + ADAM_EPS) + ADAM_WD * w)
    return delta, m, v


ADAMW_STEPS = 8


def _adamw(ws_, gs, ms, vs):
    n = len(ws_)

    def body(*refs):
        for k in range(n):
            w, g, m, v = (refs[j * n + k] for j in range(4))
            d, nm, nv, gc = (refs[(4 + j) * n + k] for j in range(4))
            gv = g[...]
            d[...], nm[...], nv[...] = _adamw_math(w[...], gv, m[...], v[...])
            gc[...] = gv

    specs = [pl.BlockSpec((a.shape[0] // ADAMW_STEPS, a.shape[1]), lambda i: (i, 0)) for a in ws_] * 4
    outs = pl.pallas_call(
        body, name="adamw", grid=(ADAMW_STEPS,),
        out_shape=tuple(jax.ShapeDtypeStruct(a.shape, F32) for _ in range(4) for a in ws_),
        in_specs=specs, out_specs=specs,
        compiler_params=_params(40, dimension_semantics=("arbitrary",)),
    )(*ws_, *gs, *ms, *vs)
    return [tuple(outs[j * n + k] for j in range(4)) for k in range(n)]


_REL_PAD = 384
_VEC_FIELDS = (("norm_g", 0, D_MODEL), ("b_gate", 1024, 2 * D_MODEL), ("sgu_ln_g", 3072, D_B),
               ("sgu_ln_b", 3584, D_B), ("b_s", 4096, N_GROUPS * 128), ("final_g", 4608, D_MODEL))
_LOSS_OFF = 5632
_REL_OFF = 5760
_NV = _REL_OFF + N_HEADS * _REL_PAD
_N_FIELDS = len(_VEC_FIELDS) + 2


_B_S_FIELD = [f[0] for f in _VEC_FIELDS].index("b_s")


def _assemble_row(dst, fields, transposed_b_s):
    for f, (_, off, n) in enumerate(_VEC_FIELDS):
        if transposed_b_s and f == _B_S_FIELD:
            t = fields[f][...].T
            for g in range(N_GROUPS):
                dst[:, off + 128 * g:off + 128 * (g + 1)] = t[g:g + 1, :]
        else:
            dst[:, off:off + n] = fields[f][...]
    for r in range(N_HEADS):
        dst[:, _REL_OFF + _REL_PAD * r:_REL_OFF + _REL_PAD * (r + 1)] = fields[len(_VEC_FIELDS)][r:r + 1, :]


def _small_reduce(grads, loss_row, after=()):
    n_in = _N_FIELDS + 1

    def body(*refs):
        g_refs, loss_ref = refs[:_N_FIELDS], refs[_N_FIELDS]
        out_v, out_w = refs[n_in:n_in + 2]
        mine_v, mine_w, gath_v, gath_w, send_sems, recv_sems = refs[n_in + 2:]
        x, y, c, chips = _mesh_pos()
        me, sibling = (x, y, c), (x, y, 1 - c)

        peers_entered = _signal_peers("both")
        _assemble_row(mine_v, g_refs, True)
        mine_v[:, _LOSS_OFF:_LOSS_OFF + 128] = loss_ref[...]
        mine_w[...] = g_refs[-1][...].astype(BF)
        peers_entered()
        my_k = 4 * x + 2 * y + c
        gath_v[my_k] = mine_v[...]
        gath_w[my_k] = mine_w[...]

        def copy(k, gath, block, to, src=None):
            dst = gath.at[4 * block[0] + 2 * block[1] + block[2]]
            return pltpu.make_async_remote_copy(
                src_ref=dst if src is None else src, dst_ref=dst,
                send_sem=send_sems.at[k], recv_sem=recv_sems.at[k], device_id=to, device_id_type=MESH)

        bufs = ((gath_v, mine_v), (gath_w, mine_w))
        first, passed = [], []
        for b, (gath, mine) in enumerate(bufs):
            first.append(copy(7 * b, gath, me, sibling, src=mine))
            first += [copy(7 * b + 1 + j, gath, me, (*chip, c), src=mine) for j, chip in enumerate(chips)]
        for cp in first:
            cp.start()
        for b, (gath, _) in enumerate(bufs):
            for j, chip in enumerate(chips):
                copy(7 * b + 1 + j, gath, (*chip, c), me).wait_recv()
                cp = copy(7 * b + 4 + j, gath, (*chip, c), sibling)
                cp.start()
                passed.append(cp)
        for b, (gath, _) in enumerate(bufs):
            copy(7 * b, gath, sibling, me).wait_recv()
            for j, chip in enumerate(chips):
                copy(7 * b + 4 + j, gath, (*chip, 1 - c), me).wait_recv()
        for cp in first + passed:
            cp.wait_send()

        tot_v, tot_w = gath_v[0], gath_w[0].astype(F32)
        for k in range(1, 8):
            tot_v = tot_v + gath_v[k]
            tot_w = tot_w + gath_w[k].astype(F32)
        out_v[...] = tot_v
        out_w[...] = tot_w

    vm = pl.BlockSpec(memory_space=pltpu.VMEM)
    return pl.pallas_call(
        _after(body, n_in, after), name="small_reduce",
        out_shape=(jax.ShapeDtypeStruct((1, _NV), F32), jax.ShapeDtypeStruct((N_GROUPS * 128, 128), F32)),
        in_specs=[vm] * n_in + [_ANY] * len(after), out_specs=[vm] * 2,
        scratch_shapes=[pltpu.VMEM((1, _NV), F32), pltpu.VMEM((N_GROUPS * 128, 128), BF),
                        pltpu.VMEM((8, 1, _NV), F32), pltpu.VMEM((8, N_GROUPS * 128, 128), BF),
                        pltpu.SemaphoreType.DMA((14,)), pltpu.SemaphoreType.DMA((14,))],
        compiler_params=_params(32, collective_id=_PEER_SETS["both"]),
    )(*grads, loss_row, *after)


def _small_adamw(tot_v, tot_w, params):
    n_in = 2 + 3 * _N_FIELDS

    def body(*refs):
        tv_ref, tw_ref = refs[:2]
        p_refs = [refs[2 + k * _N_FIELDS:2 + (k + 1) * _N_FIELDS] for k in range(3)]
        outs = refs[n_in:n_in + 4 * _N_FIELDS + 1]
        wmv = refs[-1]
        for k in range(3):
            _assemble_row(wmv.at[k], p_refs[k], False)
            wmv[k, :, _LOSS_OFF:_LOSS_OFF + 128] = jnp.zeros((1, 128), F32)
        tot_v, tot_w = tv_ref[...], tw_ref[...]
        res_v = (tot_v,) + _adamw_math(wmv[0], tot_v, wmv[1], wmv[2])
        res_w = (tot_w,) + _adamw_math(p_refs[0][-1][...], tot_w, p_refs[1][-1][...], p_refs[2][-1][...])
        for kind in range(4):
            o = outs[kind * _N_FIELDS:(kind + 1) * _N_FIELDS]
            for f, (_, off, n) in enumerate(_VEC_FIELDS):
                o[f][...] = res_v[kind][:, off:off + n]
            for r in range(N_HEADS):
                o[len(_VEC_FIELDS)][r:r + 1, :] = res_v[kind][:, _REL_OFF + _REL_PAD * r:_REL_OFF + _REL_PAD * (r + 1)]
            o[-1][...] = res_w[kind]
        outs[-1][...] = tot_v[:, _LOSS_OFF:_LOSS_OFF + 128]

    field_shapes = [(1, n) for _, _, n in _VEC_FIELDS] + [(N_HEADS, _REL_PAD), (N_GROUPS * 128, 128)]
    vm = pl.BlockSpec(memory_space=pltpu.VMEM)
    operands = [tot_v, tot_w] + [a for p in params for a in p]
    assert len(operands) == n_in
    outs = pl.pallas_call(
        body, name="small_adamw",
        out_shape=tuple(jax.ShapeDtypeStruct(s, F32) for _ in range(4) for s in field_shapes)
        + (jax.ShapeDtypeStruct((1, 128), F32),),
        in_specs=[vm] * n_in, out_specs=[vm] * (4 * _N_FIELDS + 1),
        scratch_shapes=[pltpu.VMEM((3, 1, _NV), F32)],
        compiler_params=_params(32),
    )(*operands)
    return [outs[k * _N_FIELDS:(k + 1) * _N_FIELDS] for k in range(4)], outs[-1]


def _small_fields(norm_g, b_gate, ln_g, ln_b, b_s, final_g, rel_bias, w_s):
    rel = jnp.pad(rel_bias.reshape(N_HEADS, N_REL), ((0, 0), (0, _REL_PAD - N_REL)))
    return (norm_g, b_gate, ln_g, ln_b, b_s.reshape(1, N_GROUPS * 128), final_g.reshape(1, D_MODEL),
            rel, w_s.reshape(N_GROUPS * 128, 128))


def _small_outputs(fields):
    n_g, b_g, l_g, l_b, b_s, f_g, rel, w_s = fields
    return (n_g, b_g, rel[:, :N_REL].reshape(1, N_HEADS, N_REL), l_g, l_b,
            w_s.reshape(1, N_GROUPS, 128, 128), b_s.reshape(1, N_GROUPS, 128), f_g.reshape(D_MODEL))


def _bias_row(rel_bias):
    hi = rel_bias[:, N_REL - 1:N_REL]
    lo = rel_bias[:, 0:1]
    return jnp.concatenate([jnp.broadcast_to(hi, (N_HEADS, 384)), rel_bias[:, ::-1],
                            jnp.broadcast_to(lo, (N_HEADS, 191)), jnp.broadcast_to(hi, (N_HEADS, 192))], axis=1)


def kernel(x, norm_g, w_in, b_gate, rel_bias, sgu_ln_g, sgu_ln_b, w_s, b_s, w_pa, w_pb, w_out, final_g, loss_target, m_norm_g, m_w_in, m_b_gate, m_rel_bias, m_sgu_ln_g, m_sgu_ln_b, m_w_s, m_b_s, m_w_pa, m_w_pb, m_w_out, m_final_g, v_norm_g, v_w_in, v_b_gate, v_rel_bias, v_sgu_ln_g, v_sgu_ln_b, v_w_s, v_b_s, v_w_pa, v_w_pb, v_w_out, v_final_g):
    S = x.shape[1]
    xs = x.reshape(S, D_MODEL)
    tgt = loss_target.reshape(S, D_MODEL)
    big_w = (w_in[0], w_pa[0], w_pb[0], w_out[0])
    big_m = (m_w_in[0], m_w_pa[0], m_w_pb[0], m_w_out[0])
    big_v = (v_w_in[0], v_w_pa[0], v_w_pb[0], v_w_out[0])
    rel = rel_bias[0]
    ws = w_s[0]
    bst = b_s[0].T
    fg = final_g.reshape(1, D_MODEL)
    chip = 2 * lax.axis_index("x") + lax.axis_index("y")
    pos = jnp.stack([lax.axis_index("c"), chip] + [(chip + k) % N_SHARD for k in range(1, N_SHARD)]).astype(jnp.int32)

    (w_in_bf,), staged, band_bias = _ag_weights((0,), big_w[:1], (1, 2, 3), big_w[1:], _bias_row(rel))
    ag_s = _split_start("ag_small_start", staged, 9, _gather_copies((1, 2, 3)), "chips", after=(w_in_bf,))

    ht, q3, k3, v3, zrest = _inproj_fwd(xs, norm_g, w_in_bf, after=(ag_s.token,))
    att, lse = _attn_fwd(q3, k3, v3, band_bias)
    w_pa_bf, w_pb_bf, w_out_bf = _split_wait("ag_small_wait", ag_s, _gather_copies((1, 2, 3)), att)
    (d_out, d_att, dzt, dzs, gw_out, gw_pa, gw_pb, g_bgate, g_final, loss_row,
     g_ws, g_bs_t, g_lng, g_lnb) = _tail_sgu(
        att, zrest, xs, tgt, w_pa_bf, w_pb_bf, w_out_bf, b_gate, fg, sgu_ln_g, sgu_ln_b, ws, bst)
    ws_s, ws_i = (1, 2, 3), (0,)

    x1s = _split_start("gx1s_start", [gw_pa, gw_pb, gw_out] + _x1_lands(ws_s), 12, _x1_copies(ws_s), "sibling")
    dq, dk, dv, d_gp = _attn_bwd(q3, k3, v3, d_att, lse, band_bias, after=(x1s.token,))
    got = _split_wait("gx1s_wait", x1s, _x1_copies(ws_s), dq)
    own_s, csb_s = _grad_add1_group(ws_s, got[:3], got[3:], pos)

    x2s = _split_start("gx2s_start", csb_s + _x2_lands(ws_s), 9, _x2_copies(3), "chips")
    gw_in, gw_in_bf = _gw_in(ht, dq, dk, dv, dzt, dzs, after=(x2s.token,))
    x1i = _split_start("gx1i_start", [gw_in_bf] + _x1_lands(ws_i, BF), 4, _x1_copies(ws_i), "sibling")
    got = _split_wait("gx2s_wait", x2s, _x2_copies(3), x1i.token)
    halves_s = _grad_add2_group(ws_s, own_s, got[3:])
    x3s = _split_start("gx3s_start", halves_s, 3, _x3_copies(ws_s), "sibling")
    recv1_i = _split_wait("gx1i_wait", x1i, _x1_copies(ws_i), x3s.token)[1]
    csb_i = _grad_add1(0, gw_in, recv1_i, pos)

    x2i = _split_start("gx2i_start", [csb_i] + _x2_lands(ws_i), 3, _x2_copies(1), "chips")
    grad_x, g_norm = _dh_gradx(dq, dk, dv, dzt, dzs, w_in_bf, xs, norm_g, d_out, after=(x2i.token,))
    g_shards_s = _split_wait("gx3s_wait", x3s, _x3_copies(ws_s), grad_x)
    got = _split_wait("gx2i_wait", x2i, _x2_copies(1), grad_x)
    half_i = _grad_add2(0, gw_in, recv1_i, got[1], pos)
    x3i = _split_start("gx3i_start", [half_i], 1, _x3_copies(ws_i), "sibling")

    g_rel = jnp.pad(d_gp[:, 384:384 + N_REL][:, ::-1], ((0, 0), (0, _REL_PAD - N_REL)))
    small_grads = (g_norm, g_bgate, g_lng, g_lnb, g_bs_t, g_final, g_rel, g_ws.reshape(N_GROUPS * 128, 128))
    small_params = (_small_fields(norm_g, b_gate, sgu_ln_g, sgu_ln_b, b_s, final_g, rel_bias, w_s),
                    _small_fields(m_norm_g, m_b_gate, m_sgu_ln_g, m_sgu_ln_b, m_b_s, m_final_g, m_rel_bias, m_w_s),
                    _small_fields(v_norm_g, v_b_gate, v_sgu_ln_g, v_sgu_ln_b, v_b_s, v_final_g, v_rel_bias, v_w_s))
    tot_v, tot_w = _small_reduce(small_grads, loss_row, after=(x3i.token,))
    (gsum, sdelta, sm, sv), loss_out = _small_adamw(tot_v, tot_w, small_params)

    g_shard_i, = _split_wait("gx3i_wait", x3i, _x3_copies(ws_i), loss_out)
    big = _adamw(big_w, [g_shard_i] + g_shards_s, big_m, big_v)
    sg_out, sd_out, sm_out, sv_out = (_small_outputs(f) for f in (gsum, sdelta, sm, sv))
    loss = loss_out[0, 0]

    def assemble(small, bigs):
        n_g, b_g, r_b, l_g, l_b, w_s_, b_s_, f_g = small
        b_in, b_pa, b_pb, b_out = (b[None] for b in bigs)
        return (n_g, b_in, b_g, r_b, l_g, l_b, w_s_, b_s_, b_pa, b_pb, b_out, f_g)

    grads_out = assemble(sg_out, [b[3] for b in big])
    delta_out = assemble(sd_out, [b[0] for b in big])
    m_out = assemble(sm_out, [b[1] for b in big])
    v_out = assemble(sv_out, [b[2] for b in big])
    return (loss, grad_x.reshape(1, S, D_MODEL), *grads_out, *delta_out, *m_out, *v_out)
```

```python
import functools
import math

import jax
import jax.numpy as jnp
from jax import lax
from jax.experimental import pallas as pl
from jax.experimental.pallas import tpu as pltpu

F32 = jnp.float32
BF = jnp.bfloat16
MESH = pl.DeviceIdType.MESH

D_MODEL = 1024
D_A = 512
D_B = 512
D_IN = 5632
N_HEADS = 8
HEAD_DIM = 64
CHUNK = 64
N_PREV = 8
SGU_CHUNK = 128
N_GROUPS = 4
N_REL = 257
EPS = 1e-6
NEG_INF = -1e30
SCALE = HEAD_DIM ** -0.5

QB = 2 * CHUNK
KB = (N_PREV + 2) * CHUNK
PADK = N_PREV * CHUNK
ROLL_W = 1024
KEEP = KB // QB - 1
Q_PER_STEP = 2

ADAM_LR = 0.001
ADAM_B1 = 0.9
ADAM_B2 = 0.999
ADAM_EPS = 1e-08
ADAM_WD = 0.01
ADAM_STEP = 10
ADAM_C1 = 1.0 - ADAM_B1 ** ADAM_STEP
ADAM_C2 = 1.0 - ADAM_B2 ** ADAM_STEP

AG_PIECES = 4
N_SHARD = 4
SHARD_IN = D_IN // N_SHARD
MIB = 1024 * 1024


V7X_VMEM_MIB = 64
VMEM_RESERVE_MIB = V7X_VMEM_MIB - 4


def _params(vmem_mib, **kw):
    assert vmem_mib <= VMEM_RESERVE_MIB
    return pltpu.CompilerParams(vmem_limit_bytes=VMEM_RESERVE_MIB * MIB, **kw)


def _sigmoid(x):
    return 1.0 / (1.0 + jnp.exp(-x))


def _silu_and_grad(x):
    s = _sigmoid(x)
    return x * s, s * (1.0 + x * (1.0 - s))


_GELU_C = math.sqrt(2.0 / math.pi)
_GELU_A = 0.044715


def _gelu_and_grad(x):
    x2 = x * x
    t = jnp.tanh(_GELU_C * (x + _GELU_A * (x2 * x)))
    cdf = 0.5 * (1.0 + t)
    grad = cdf + 0.5 * x * (1.0 - t * t) * (_GELU_C * (1.0 + 3.0 * _GELU_A * x2))
    return x * cdf, grad


def _dot(a, b):
    return jnp.dot(a, b, preferred_element_type=F32)


def _dot_nt(a, b):
    return lax.dot_general(a, b, (((1,), (1,)), ((), ())), preferred_element_type=F32)


def _dot_tn(a, b):
    return lax.dot_general(a, b, (((0,), (0,)), ((), ())), preferred_element_type=F32)


def _mo(v, m):
    return v if isinstance(v, int) else pl.multiple_of(v, m)


def _unit_in(ref, s, p):
    return ref.at[pl.ds(_mo(p * 512, 512), 512), pl.ds(_mo(s * SHARD_IN, 128), SHARD_IN)]


def _unit_p(ref, s, p):
    return ref.at[pl.ds(_mo(p * 256, 256), 256), pl.ds(_mo(s * 256, 128), 256)]


def _unit_out(ref, s, p):
    return ref.at[pl.ds(_mo(s * 256 + p * 128, 128), 128), :]


_UNITS = (_unit_in, _unit_p, _unit_p, _unit_out)
_HALF_ROWS = (512, 256, 256, 128)
_UNIT_SHAPES = ((512, SHARD_IN), (256, 256), (256, 256), (128, D_MODEL))
_FULL_SHAPES = ((D_MODEL, D_IN), (D_A, D_MODEL), (D_B, D_MODEL), (D_MODEL, D_MODEL))
_SHARD_SHAPES = ((D_MODEL, SHARD_IN), (D_A, 256), (D_B, 256), (256, D_MODEL))


def _mesh_pos():
    x, y, c = lax.axis_index("x"), lax.axis_index("y"), lax.axis_index("c")
    chips = [(1 - x, y), (x, 1 - y), (1 - x, 1 - y)]
    return x, y, c, chips


def _ag_weights(ws, shards, later_ws, later_shards, gp):
    n, m = len(ws), len(later_ws)

    def body(*refs):
        ins, later_ins, gp_ref = refs[:n], refs[n:n + m], refs[n + m]
        o = n + m + 1
        outs, later_outs, bias_ref = refs[o:o + n], refs[o + n:o + n + m], refs[o + n + m]
        o += n + m + 1
        stage, later_stage = refs[o:o + n], refs[o + n:o + n + m]
        send_sems, recv_sems, local_sems, later_sems = refs[o + n + m:]
        x, y, c, chips = _mesh_pos()
        s_me = 2 * x + y
        sibling = (x, y, 1 - c)
        def rows_of(k, p):
            rows = _HALF_ROWS[ws[k]]
            return pl.ds(_mo(p * rows, rows), rows)

        def half(k, p):
            return stage[k].at[rows_of(k, p), :]

        def unit(k, s, p):
            return _UNITS[ws[k]](outs[k], s, p)

        def rcopy(k, i, src, dst, to):
            return pltpu.make_async_remote_copy(src_ref=src, dst_ref=dst, send_sem=send_sems.at[k, i],
                                                recv_sem=recv_sems.at[k, i], device_id=to, device_id_type=MESH)

        peers_entered = _signal_peers("both")
        for k in range(n):
            stage[k][rows_of(k, c), :] = ins[k][rows_of(k, c), :].astype(BF)
        peers_entered()
        def piece(ref, k, q):
            rows = _HALF_ROWS[ws[k]] // AG_PIECES
            return ref.at[pl.ds(q * rows, rows), :]

        sends = []
        for q in range(AG_PIECES):
            for j, (cx, cy) in enumerate(chips):
                for k in range(n):
                    cp = rcopy(k, j * AG_PIECES + q, piece(half(k, c), k, q), piece(unit(k, s_me, c), k, q),
                               (cx, cy, c))
                    cp.start()
                    sends.append(cp)
        for k in range(n):
            stage[k][rows_of(k, 1 - c), :] = ins[k][rows_of(k, 1 - c), :].astype(BF)
        local = []
        for k in range(n):
            for p in range(2):
                cp = pltpu.make_async_copy(half(k, p), unit(k, s_me, p), local_sems.at[k, p])
                cp.start()
                local.append(cp)
        for k, w in enumerate(later_ws):
            later_stage[k][...] = later_ins[k][...].astype(BF)
            cp = pltpu.make_async_copy(later_stage[k], _shard_of(later_outs[k], w, s_me), later_sems.at[k])
            cp.start()
            local.append(cp)
        keep = _struct_mask()
        for h in range(N_HEADS):
            bias_ref[h] = jnp.where(keep, _skew_table(gp_ref[h:h + 1, :])[:, :KB], NEG_INF)
        for q in range(AG_PIECES):
            for j, (cx, cy) in enumerate(chips):
                for k in range(n):
                    landed = piece(unit(k, 2 * cx + cy, c), k, q)
                    rcopy(k, j * AG_PIECES + q, landed, landed, (cx, cy, c)).wait_recv()
                    cp = rcopy(k, (3 + j) * AG_PIECES + q, landed, landed, sibling)
                    cp.start()
                    sends.append(cp)
        for q in range(AG_PIECES):
            for j, (cx, cy) in enumerate(chips):
                for k in range(n):
                    other = piece(unit(k, 2 * cx + cy, 1 - c), k, q)
                    rcopy(k, (3 + j) * AG_PIECES + q, other, other, sibling).wait_recv()
        for cp in sends:
            cp.wait_send()
        for cp in local:
            cp.wait()

    vm = pl.BlockSpec(memory_space=pltpu.VMEM)
    outs = pl.pallas_call(
        body, name="ag_weights",
        out_shape=tuple(jax.ShapeDtypeStruct(_FULL_SHAPES[w], BF) for w in tuple(ws) + tuple(later_ws))
        + (jax.ShapeDtypeStruct((N_HEADS, QB, KB), F32),),
        in_specs=[vm] * (n + m + 1), out_specs=[_ANY] * (n + m) + [vm],
        scratch_shapes=[pltpu.VMEM(_SHARD_SHAPES[w], BF) for w in tuple(ws) + tuple(later_ws)]
        + [pltpu.SemaphoreType.DMA((n, 6 * AG_PIECES)), pltpu.SemaphoreType.DMA((n, 6 * AG_PIECES)),
           pltpu.SemaphoreType.DMA((n, 2)), pltpu.SemaphoreType.DMA((m,))],
        compiler_params=_params(48, collective_id=_PEER_SETS["both"]),
    )(*shards, *later_shards, gp)
    return list(outs[:n]), list(outs[n:n + m]), outs[-1]


def _shard_of(ref, w, s):
    if w == 0:
        return ref.at[:, pl.ds(_mo(s * SHARD_IN, 128), SHARD_IN)]
    if w == 3:
        return ref.at[pl.ds(_mo(s * 256, 256), 256), :]
    return ref.at[:, pl.ds(_mo(s * 256, 128), 256)]


def _gather_copies(ws):
    def copies(refs, send_sems, recv_sems):
        x, y, c, chips = _mesh_pos()
        out = []
        for j, (cx, cy) in enumerate(chips):
            for k, w in enumerate(ws):
                mine = _shard_of(refs[k], w, 2 * x + y)
                out.append(pltpu.make_async_remote_copy(
                    src_ref=mine, dst_ref=mine, send_sem=send_sems.at[3 * k + j], recv_sem=recv_sems.at[3 * k + j],
                    device_id=(cx, cy, c), device_id_type=MESH))
        return out
    return copies


def _inproj_fwd(x, norm_g, w_in_bf, tm=512, after=()):
    S = x.shape[0]

    def body(x_ref, g_ref, w_ref, ht_ref, q_ref, k_ref, v_ref, zr_ref):
        xv = x_ref[...]
        r = lax.rsqrt(jnp.mean(xv * xv, axis=-1, keepdims=True) + EPS)
        hf = (xv * r) * g_ref[...]
        ht_ref[...] = hf.T.astype(BF)
        h = hf.astype(BF)
        heads = (q_ref, k_ref, v_ref)
        for j in range(D_IN // 512):
            z = _dot(h, w_ref[:, j * 512:(j + 1) * 512])
            if j < 3:
                zb = z.astype(BF)
                for hd in range(N_HEADS):
                    heads[j][hd] = zb[:, hd * HEAD_DIM:(hd + 1) * HEAD_DIM]
            else:
                zr_ref[:, (j - 3) * 512:(j - 2) * 512] = z

    head_major = jax.ShapeDtypeStruct((N_HEADS, S, HEAD_DIM), BF)
    head_spec = pl.BlockSpec((N_HEADS, tm, HEAD_DIM), lambda i: (0, i, 0))
    return pl.pallas_call(
        _after(body, 3, after), name="inproj_fwd", grid=(S // tm,),
        out_shape=(jax.ShapeDtypeStruct((D_MODEL, S), BF), head_major, head_major, head_major,
                   jax.ShapeDtypeStruct((S, D_IN - 3 * D_A), F32)),
        in_specs=[pl.BlockSpec((tm, D_MODEL), lambda i: (i, 0)),
                  pl.BlockSpec((1, D_MODEL), lambda i: (0, 0)),
                  pl.BlockSpec((D_MODEL, D_IN), lambda i: (0, 0), pipeline_mode=pl.Buffered(1))]
        + [_ANY] * len(after),
        out_specs=[pl.BlockSpec((D_MODEL, tm), lambda i: (0, i)),
                   head_spec, head_spec, head_spec,
                   pl.BlockSpec((tm, D_IN - 3 * D_A), lambda i: (i, 0))],
        compiler_params=_params(52, dimension_semantics=("arbitrary",)),
    )(x, norm_g, w_in_bf, *after)


def _skew_table(gp_row):
    row = lax.broadcasted_iota(jnp.int32, (QB, ROLL_W), 0)
    t = jnp.broadcast_to(gp_row, (QB, ROLL_W))
    for b in range(7):
        t = jnp.where(((row >> b) & 1) == 1, pltpu.roll(t, 1 << b, axis=1), t)
    return t


def _unskew_sum(d):
    row = lax.broadcasted_iota(jnp.int32, (QB, ROLL_W), 0)
    for b in range(7):
        d = jnp.where(((row >> b) & 1) == 1, pltpu.roll(d, ROLL_W - (1 << b), axis=1), d)
    return jnp.sum(d, axis=0, keepdims=True)


def _struct_mask():
    a = lax.broadcasted_iota(jnp.int32, (QB, KB), 0) // CHUNK
    b = lax.broadcasted_iota(jnp.int32, (QB, KB), 1) // CHUNK
    return (b >= a) & (b <= a + N_PREV)


def _load_kv(k_hbm, v_hbm, k_scr, v_scr, sems, S, meanwhile=lambda: None):
    zeros = jnp.zeros((N_HEADS, PADK, HEAD_DIM), BF)
    k_scr[:, 0:PADK, :] = zeros
    v_scr[:, 0:PADK, :] = zeros
    ck = pltpu.make_async_copy(k_hbm, k_scr.at[:, pl.ds(PADK, S), :], sems.at[0])
    cv = pltpu.make_async_copy(v_hbm, v_scr.at[:, pl.ds(PADK, S), :], sems.at[1])
    ck.start()
    cv.start()
    meanwhile()
    ck.wait()
    cv.wait()


_BATCH_NT = (((2,), (2,)), ((0,), (0,)))
_BATCH_NN = (((2,), (1,)), ((0,), (0,)))
_BATCH_TN = (((1,), (1,)), ((0,), (0,)))


def _bdot(a, b, dims):
    return lax.dot_general(a, b, dims, preferred_element_type=F32)


def _scaled(q):
    return q * jnp.asarray(SCALE, BF)


def _scores(qs, kb, bias, i, front):
    s = _bdot(qs, kb, _BATCH_NT) + bias
    if front:
        col = lax.broadcasted_iota(jnp.int32, (1, 1, KB), 2)
        s = jnp.where(col >= PADK - i * QB, s, NEG_INF)
    return s


def _attn_fwd(q3, k3, v3, bias):
    S = q3.shape[1]

    def body(q_ref, k_hbm, v_hbm, bias_ref, o_ref, lse_ref, k_scr, v_scr, sems):
        @pl.when(pl.program_id(0) == 0)
        def _():
            _load_kv(k_hbm, v_hbm, k_scr, v_scr, sems, S)

        def step(i, rows, front):
            start = pl.multiple_of(i * QB, QB)
            kb = k_scr[:, pl.ds(start, KB), :]
            vb = v_scr[:, pl.ds(start, KB), :]
            s = _scores(_scaled(q_ref[:, rows, :]), kb, bias_ref[...], i, front)
            m = jnp.max(s, axis=-1, keepdims=True)
            e = jnp.exp(s - m)
            l = jnp.sum(e, axis=-1, keepdims=True)
            p = e * (1.0 / l)
            o = _bdot(p.astype(BF), vb, _BATCH_NN)
            lse_ref[:, rows, :] = jnp.broadcast_to(m + jnp.log(l), (N_HEADS, QB, 128))
            for h in range(N_HEADS):
                o_ref[rows, h * HEAD_DIM:(h + 1) * HEAD_DIM] = o[h]

        def block(j, carry):
            i = pl.program_id(0) * Q_PER_STEP + j
            rows = pl.ds(pl.multiple_of(j * QB, QB), QB)
            pl.when(i < KEEP)(functools.partial(step, i, rows, True))
            pl.when(i >= KEEP)(functools.partial(step, i, rows, False))
            return carry

        lax.fori_loop(0, Q_PER_STEP, block, 0)

    rows_per_step = Q_PER_STEP * QB
    kv_scr = pltpu.VMEM((N_HEADS, S + PADK, HEAD_DIM), BF)
    return pl.pallas_call(
        body, name="attn_fwd", grid=(S // rows_per_step,),
        out_shape=(jax.ShapeDtypeStruct((S, D_A), F32), jax.ShapeDtypeStruct((N_HEADS, S, 128), F32)),
        in_specs=[pl.BlockSpec((N_HEADS, rows_per_step, HEAD_DIM), lambda g: (0, g, 0)),
                  pl.BlockSpec(memory_space=pl.ANY), pl.BlockSpec(memory_space=pl.ANY),
                  pl.BlockSpec((N_HEADS, QB, KB), lambda g: (0, 0, 0))],
        out_specs=[pl.BlockSpec((rows_per_step, D_A), lambda g: (g, 0)),
                   pl.BlockSpec((N_HEADS, rows_per_step, 128), lambda g: (0, g, 0))],
        scratch_shapes=[kv_scr, kv_scr, pltpu.SemaphoreType.DMA((2,))],
        compiler_params=_params(48, dimension_semantics=("arbitrary",)),
    )(q3, k3, v3, bias)


def _attn_bwd(q3, k3, v3, d_att3, lse, bias, after=()):
    S = q3.shape[1]
    nq = S // QB

    def body(q_ref, do_ref, k_hbm, v_hbm, lse_ref, bias_ref, dq_ref, dk_ref, dv_ref, dgp_ref,
             k_scr, v_scr, dk_acc, dv_acc, dbias_acc, pad_scr, sems):
        @pl.when(pl.program_id(0) == 0)
        def _():
            def clear():
                dk_acc[...] = jnp.zeros_like(dk_acc)
                dv_acc[...] = jnp.zeros_like(dv_acc)
                dbias_acc[...] = jnp.zeros_like(dbias_acc)
            _load_kv(k_hbm, v_hbm, k_scr, v_scr, sems, S, clear)

        def step(i, rows, front):
            start = pl.multiple_of(i * QB, QB)
            kb = k_scr[:, pl.ds(start, KB), :]
            vb = v_scr[:, pl.ds(start, KB), :]
            qs = _scaled(q_ref[:, rows, :])
            do = do_ref[:, rows, :]
            p = jnp.exp(_scores(qs, kb, bias_ref[...], i, front) - jnp.tile(lse_ref[:, rows, :], (1, 1, KB // 128)))
            dp = _bdot(do, vb, _BATCH_NT)
            ds = p * (dp - jnp.sum(dp * p, axis=-1, keepdims=True))
            dbias_acc[...] += ds
            dsb = ds.astype(BF)
            dq = _bdot(dsb, kb, _BATCH_NN) * SCALE
            for h in range(N_HEADS):
                dq_ref[rows, h * HEAD_DIM:(h + 1) * HEAD_DIM] = dq[h].astype(BF)
            dk_acc[...] += _bdot(dsb, qs, _BATCH_TN)
            dv_acc[...] += _bdot(p.astype(BF), do, _BATCH_TN)

        def block(j, carry):
            i = pl.program_id(0) * Q_PER_STEP + j
            rows = pl.ds(pl.multiple_of(j * QB, QB), QB)
            pl.when(i < KEEP)(functools.partial(step, i, rows, True))
            pl.when((i >= KEEP) & (i < nq))(functools.partial(step, i, rows, False))
            for h in range(N_HEADS):
                hs = slice(h * HEAD_DIM, (h + 1) * HEAD_DIM)
                dk_ref[rows, hs] = dk_acc[h, 0:QB, :].astype(BF)
                dv_ref[rows, hs] = dv_acc[h, 0:QB, :].astype(BF)
            dk_acc[:, 0:KB - QB, :] = dk_acc[:, QB:KB, :]
            dv_acc[:, 0:KB - QB, :] = dv_acc[:, QB:KB, :]
            dk_acc[:, KB - QB:KB, :] = jnp.zeros((N_HEADS, QB, HEAD_DIM), F32)
            dv_acc[:, KB - QB:KB, :] = jnp.zeros((N_HEADS, QB, HEAD_DIM), F32)
            return carry

        lax.fori_loop(0, Q_PER_STEP, block, 0)

        @pl.when(pl.program_id(0) == n_steps - 1)
        def _():
            lane = lax.broadcasted_iota(jnp.int32, (1, ROLL_W), 1)
            hi = (lane < 384) | (lane >= 832)
            lo = (lane > 640) & (lane < 832)
            pad_scr[...] = jnp.zeros_like(pad_scr)
            for h in range(N_HEADS):
                pad_scr[:, 0:KB] = dbias_acc[h]
                g = _unskew_sum(pad_scr[...])
                s_hi = jnp.sum(jnp.where(hi, g, 0.0), axis=-1, keepdims=True)
                s_lo = jnp.sum(jnp.where(lo, g, 0.0), axis=-1, keepdims=True)
                g = jnp.where(lane == 384, g + s_hi, g)
                g = jnp.where(lane == 640, g + s_lo, g)
                dgp_ref[h:h + 1, :] = g

    assert nq % Q_PER_STEP == 0 and KEEP % Q_PER_STEP == 0
    rows_per_step = Q_PER_STEP * QB
    n_steps = (nq + KEEP) // Q_PER_STEP
    last = nq // Q_PER_STEP - 1
    lag = KEEP // Q_PER_STEP
    kv_scr = pltpu.VMEM((N_HEADS, S + PADK, HEAD_DIM), BF)
    return pl.pallas_call(
        _after(body, 6, after), name="attn_bwd", grid=(n_steps,),
        out_shape=(jax.ShapeDtypeStruct((S, D_A), BF), jax.ShapeDtypeStruct((S, D_A), BF),
                   jax.ShapeDtypeStruct((S, D_A), BF), jax.ShapeDtypeStruct((N_HEADS, ROLL_W), F32)),
        in_specs=[pl.BlockSpec((N_HEADS, rows_per_step, HEAD_DIM), lambda g: (0, jnp.minimum(g, last), 0)),
                  pl.BlockSpec((N_HEADS, rows_per_step, HEAD_DIM), lambda g: (0, jnp.minimum(g, last), 0)),
                  pl.BlockSpec(memory_space=pl.ANY), pl.BlockSpec(memory_space=pl.ANY),
                  pl.BlockSpec((N_HEADS, rows_per_step, 128), lambda g: (0, jnp.minimum(g, last), 0)),
                  pl.BlockSpec((N_HEADS, QB, KB), lambda g: (0, 0, 0))] + [_ANY] * len(after),
        out_specs=[pl.BlockSpec((rows_per_step, D_A), lambda g: (jnp.minimum(g, last), 0)),
                   pl.BlockSpec((rows_per_step, D_A), lambda g: (jnp.maximum(g - lag, 0), 0)),
                   pl.BlockSpec((rows_per_step, D_A), lambda g: (jnp.maximum(g - lag, 0), 0)),
                   pl.BlockSpec((N_HEADS, ROLL_W), lambda g: (0, 0))],
        scratch_shapes=[kv_scr, kv_scr,
                        pltpu.VMEM((N_HEADS, KB, HEAD_DIM), F32), pltpu.VMEM((N_HEADS, KB, HEAD_DIM), F32),
                        pltpu.VMEM((N_HEADS, QB, KB), F32), pltpu.VMEM((QB, ROLL_W), F32),
                        pltpu.SemaphoreType.DMA((2,))],
        compiler_params=_params(56, dimension_semantics=("arbitrary",)),
    )(q3, d_att3, k3, v3, lse, bias, *after)


def _sgu_core(ub, vb, lg, lb):
    u, du = _gelu_and_grad(ub)
    v, dv = _gelu_and_grad(vb)
    mu = jnp.mean(v, axis=-1, keepdims=True)
    vc = v - mu
    rstd = lax.rsqrt(jnp.mean(vc * vc, axis=-1, keepdims=True) + EPS)
    xh = vc * rstd
    vn = xh * lg + lb
    return u, du, dv, rstd, xh, vn


def _tri():
    r = lax.broadcasted_iota(jnp.int32, (SGU_CHUNK, SGU_CHUNK), 0)
    c = lax.broadcasted_iota(jnp.int32, (SGU_CHUNK, SGU_CHUNK), 1)
    return r >= c


def _tail_sgu(att, zrest, x, target, w_pa, w_pb, w_out, b_gate, final_g, ln_g, ln_b, w_s, b_s_t, tm=256):
    S = x.shape[0]
    nt = S // tm
    chunks = tm // SGU_CHUNK

    def body(att_ref, ga_ref, ub_ref, vb_ref, gb_ref, gta_ref, gtb_ref, x_ref, t_ref,
             wpa_ref, wpb_ref, wout_ref, bg_ref, fg_ref, lg_ref, lb_ref, ws_ref, bst_ref,
             dout_ref, datt_ref, dzt_ref, dzs_ref, gwout_hbm, gwpa_hbm, gwpb_hbm,
             gbg_ref, gfg_ref, loss_ref, gws_ref, gbs_ref, glg_ref, glb_ref,
             acc_out, acc_pa, acc_pb, sg_scr, mix_scr, dvn_scr, bs_acc, sems):
        i = pl.program_id(0)

        @pl.when(i == 0)
        def _():
            for r in (acc_out, acc_pa, acc_pb, gbg_ref, gfg_ref, loss_ref, gws_ref, glg_ref, glb_ref, bs_acc):
                r[...] = jnp.zeros_like(r)

        u, du, dv, rstd, xh, vn = _sgu_core(ub_ref[...], vb_ref[...], lg_ref[...], lb_ref[...])
        vnb = vn.astype(BF)
        tri = _tri()
        blocks = [(g, slice(n * SGU_CHUNK, (n + 1) * SGU_CHUNK), slice(g * 128, (g + 1) * 128))
                  for g in range(N_GROUPS) for n in range(chunks)]
        wts = [jnp.where(tri, ws_ref[g], 0.0) for g in range(N_GROUPS)]
        for g, rs, cs in blocks:
            mixed = _dot(wts[g].astype(BF), vnb[rs, cs]) + bst_ref[:, g:g + 1]
            mix_scr[rs, cs] = mixed
            sg_scr[rs, cs] = u[rs, cs] * mixed

        att = att_ref[...]
        sg = sg_scr[...]
        sa, dsa = _silu_and_grad(ga_ref[...])
        sb, dsb = _silu_and_grad(gb_ref[...])
        ya = (att * sa).astype(BF)
        yb = (sg * sb).astype(BF)
        pa = _dot(ya, wpa_ref[...])
        pb = _dot(yb, wpb_ref[...])
        ga = _sigmoid(gta_ref[...] + bg_ref[:, 0:D_MODEL])
        gb = _sigmoid(gtb_ref[...] + bg_ref[:, D_MODEL:2 * D_MODEL])
        merged = (ga * pa + gb * pb).astype(BF)
        out = x_ref[...] + _dot(merged, wout_ref[...])
        r2 = lax.rsqrt(jnp.mean(out * out, axis=-1, keepdims=True) + EPS)
        nrm = out * r2
        fg = fg_ref[...]
        err = nrm * fg - t_ref[...]
        loss_ref[...] += 0.5 * jnp.sum(jnp.mean(err * err, axis=-1, keepdims=True))
        dy = err * (1.0 / D_MODEL)
        gfg_ref[...] += jnp.sum(dy * nrm, axis=0, keepdims=True)
        dn = dy * fg
        d_out = r2 * (dn - nrm * jnp.mean(dn * nrm, axis=-1, keepdims=True))
        dout_ref[...] = d_out
        d_outb = d_out.astype(BF)
        acc_out[...] += _dot_tn(merged, d_outb)
        dm = _dot_nt(d_outb, wout_ref[...])
        d_pa = (dm * ga).astype(BF)
        d_pb = (dm * gb).astype(BF)
        d_gta = dm * pa * (ga * (1.0 - ga))
        d_gtb = dm * pb * (gb * (1.0 - gb))
        gbg_ref[:, 0:D_MODEL] += jnp.sum(d_gta, axis=0, keepdims=True)
        gbg_ref[:, D_MODEL:2 * D_MODEL] += jnp.sum(d_gtb, axis=0, keepdims=True)
        dzt_ref[:, 2 * D_A:2 * D_A + D_MODEL] = d_gta.astype(BF)
        dzt_ref[:, 2 * D_A + D_MODEL:] = d_gtb.astype(BF)
        acc_pa[...] += _dot_tn(ya, d_pa)
        acc_pb[...] += _dot_tn(yb, d_pb)
        d_ya = _dot_nt(d_pa, wpa_ref[...])
        d_yb = _dot_nt(d_pb, wpb_ref[...])
        d_att = (d_ya * sa).astype(BF)
        for hd in range(N_HEADS):
            datt_ref[hd] = d_att[:, hd * HEAD_DIM:(hd + 1) * HEAD_DIM]
        dzt_ref[:, 0:D_A] = (d_ya * att * dsa).astype(BF)
        dzt_ref[:, D_A:2 * D_A] = (d_yb * sg * dsb).astype(BF)

        dsg = d_yb * sb
        dzs_ref[:, 0:D_B] = (dsg * mix_scr[...] * du).astype(BF)
        dmix = dsg * u
        for g, rs, cs in blocks:
            dmb = dmix[rs, cs].astype(BF)
            bs_acc[:, cs] += dmix[rs, cs]
            gws_ref[g] += _dot_nt(dmb, vnb[rs, cs])
            dvn_scr[rs, cs] = _dot(wts[g].T.astype(BF), dmb)
        dvn = dvn_scr[...]
        glg_ref[...] += jnp.sum(dvn * xh, axis=0, keepdims=True)
        glb_ref[...] += jnp.sum(dvn, axis=0, keepdims=True)
        dxh = dvn * lg_ref[...]
        dvv = rstd * (dxh - jnp.mean(dxh, axis=-1, keepdims=True)
                      - xh * jnp.mean(dxh * xh, axis=-1, keepdims=True))
        dzs_ref[:, D_B:2 * D_B] = (dvv * dv).astype(BF)

        @pl.when(i == nt - 1)
        def _():
            cps = [pltpu.make_async_copy(acc_out, gwout_hbm, sems.at[0]),
                   pltpu.make_async_copy(acc_pa, gwpa_hbm, sems.at[1]),
                   pltpu.make_async_copy(acc_pb, gwpb_hbm, sems.at[2])]
            for cp in cps:
                cp.start()
            lane = lax.broadcasted_iota(jnp.int32, (SGU_CHUNK, 128), 1)
            cols = jnp.zeros((SGU_CHUNK, 128), F32)
            for g in range(N_GROUPS):
                gws_ref[g] = jnp.where(tri, gws_ref[g], 0.0)
                col = jnp.sum(bs_acc[:, g * 128:(g + 1) * 128], axis=-1, keepdims=True)
                cols = jnp.where(lane == g, col, cols)
            gbs_ref[...] = cols
            for cp in cps:
                cp.wait()

    c2 = lambda i: (0, 0)
    c3 = lambda i: (0, 0, 0)
    zcol = lambda w, blk: pl.BlockSpec((tm, w), lambda i: (i, blk))
    row = lambda w: pl.BlockSpec((tm, w), lambda i: (i, 0))
    return pl.pallas_call(
        body, name="tail", grid=(nt,),
        out_shape=(jax.ShapeDtypeStruct((S, D_MODEL), F32), jax.ShapeDtypeStruct((N_HEADS, S, HEAD_DIM), BF),
                   jax.ShapeDtypeStruct((S, 3072), BF), jax.ShapeDtypeStruct((S, 2 * D_B), BF),
                   jax.ShapeDtypeStruct((D_MODEL, D_MODEL), F32), jax.ShapeDtypeStruct((D_A, D_MODEL), F32),
                   jax.ShapeDtypeStruct((D_B, D_MODEL), F32),
                   jax.ShapeDtypeStruct((1, 2 * D_MODEL), F32), jax.ShapeDtypeStruct((1, D_MODEL), F32),
                   jax.ShapeDtypeStruct((1, 128), F32),
                   jax.ShapeDtypeStruct((N_GROUPS, 128, 128), F32), jax.ShapeDtypeStruct((SGU_CHUNK, 128), F32),
                   jax.ShapeDtypeStruct((1, D_B), F32), jax.ShapeDtypeStruct((1, D_B), F32)),
        in_specs=[row(D_A), zcol(512, 0), zcol(512, 1), zcol(512, 2), zcol(512, 3),
                  zcol(D_MODEL, 2), zcol(D_MODEL, 3), row(D_MODEL), row(D_MODEL),
                  pl.BlockSpec((D_A, D_MODEL), c2), pl.BlockSpec((D_B, D_MODEL), c2),
                  pl.BlockSpec((D_MODEL, D_MODEL), c2),
                  pl.BlockSpec((1, 2 * D_MODEL), c2), pl.BlockSpec((1, D_MODEL), c2),
                  pl.BlockSpec((1, D_B), c2), pl.BlockSpec((1, D_B), c2),
                  pl.BlockSpec((N_GROUPS, 128, 128), c3), pl.BlockSpec((128, N_GROUPS), c2)],
        out_specs=[row(D_MODEL), pl.BlockSpec((N_HEADS, tm, HEAD_DIM), lambda i: (0, i, 0)),
                   row(3072), row(2 * D_B), _ANY, _ANY, _ANY,
                   pl.BlockSpec((1, 2 * D_MODEL), c2), pl.BlockSpec((1, D_MODEL), c2),
                   pl.BlockSpec((1, 128), c2),
                   pl.BlockSpec((N_GROUPS, 128, 128), c3), pl.BlockSpec((SGU_CHUNK, 128), c2),
                   pl.BlockSpec((1, D_B), c2), pl.BlockSpec((1, D_B), c2)],
        scratch_shapes=[pltpu.VMEM((D_MODEL, D_MODEL), F32), pltpu.VMEM((D_A, D_MODEL), F32),
                        pltpu.VMEM((D_B, D_MODEL), F32),
                        pltpu.VMEM((tm, D_B), F32), pltpu.VMEM((tm, D_B), F32), pltpu.VMEM((tm, D_B), F32),
                        pltpu.VMEM((SGU_CHUNK, D_B), F32), pltpu.SemaphoreType.DMA((3,))],
        compiler_params=_params(58, dimension_semantics=("arbitrary",)),
    )(att, zrest, zrest, zrest, zrest, zrest, zrest, x, target, w_pa, w_pb, w_out, b_gate, final_g,
      ln_g, ln_b, w_s, b_s_t)


_DZ_MAP = ((0, 0), (1, 0), (2, 0), (3, 0), (4, 0), (4, 1), (3, 1), (3, 2), (3, 3), (3, 4), (3, 5))


def _dh_gradx(dq, dk, dv, dzt, dzs, w_in_bf, x, norm_g, d_out, tm=512, after=()):
    S = x.shape[0]

    def body(dq_ref, dk_ref, dv_ref, dzt_ref, dzs_ref, w_ref, x_ref, g_ref, dout_ref, gx_ref, gn_ref):
        i = pl.program_id(0)

        @pl.when(i == 0)
        def _():
            gn_ref[...] = jnp.zeros_like(gn_ref)

        pieces = (dq_ref, dk_ref, dv_ref, dzt_ref, dzs_ref)
        dh = jnp.zeros((tm, D_MODEL), F32)
        for j, (pc, blk) in enumerate(_DZ_MAP):
            dh += _dot_nt(pieces[pc][:, blk * 512:(blk + 1) * 512], w_ref[:, j * 512:(j + 1) * 512])
        xv = x_ref[...]
        r = lax.rsqrt(jnp.mean(xv * xv, axis=-1, keepdims=True) + EPS)
        nrm = xv * r
        gn_ref[...] += jnp.sum(dh * nrm, axis=0, keepdims=True)
        dn = dh * g_ref[...]
        gx_ref[...] = r * (dn - nrm * jnp.mean(dn * nrm, axis=-1, keepdims=True)) + dout_ref[...]

    row = lambda w: pl.BlockSpec((tm, w), lambda i: (i, 0))
    c2 = lambda i: (0, 0)
    return pl.pallas_call(
        _after(body, 9, after), name="dh_gradx", grid=(S // tm,),
        out_shape=(jax.ShapeDtypeStruct((S, D_MODEL), F32), jax.ShapeDtypeStruct((1, D_MODEL), F32)),
        in_specs=[row(512), row(512), row(512), row(3072), row(1024),
                  pl.BlockSpec((D_MODEL, D_IN), c2, pipeline_mode=pl.Buffered(1)), row(D_MODEL),
                  pl.BlockSpec((1, D_MODEL), c2), row(D_MODEL)]
        + [_ANY] * len(after),
        out_specs=[row(D_MODEL), pl.BlockSpec((1, D_MODEL), c2)],
        compiler_params=_params(48, dimension_semantics=("arbitrary",)),
    )(dq, dk, dv, dzt, dzs, w_in_bf, x, norm_g, d_out, *after)


def _gw_in(ht, dq, dk, dv, dzt, dzs, tn=512, after=()):
    S = ht.shape[1]
    per = 512 // tn
    cols = tuple((pc, per * blk + h) for pc, blk in _DZ_MAP for h in range(per))

    def body(ht_ref, dq_ref, dk_ref, dv_ref, dzt_ref, dzs_ref, o_ref, ob_ref):
        j = pl.program_id(0)
        pieces = (dq_ref, dk_ref, dv_ref, dzt_ref, dzs_ref)
        for pc in range(5):
            hit = functools.reduce(jnp.logical_or, [j == jj for jj, (p, _) in enumerate(cols) if p == pc])

            @pl.when(hit)
            def _(pc=pc):
                g = _dot(ht_ref[...], pieces[pc][...])
                o_ref[...] = g
                ob_ref[...] = g.astype(BF)

    def piece_spec(pc):
        cur = next(blk for p, blk in cols if p == pc)
        held = []
        for p, blk in cols:
            cur = blk if p == pc else cur
            held.append(cur)

        def index_map(j):
            blk = jnp.int32(held[0])
            for jj in range(1, len(held)):
                if held[jj] != held[jj - 1]:
                    blk = jnp.where(j >= jj, jnp.int32(held[jj]), blk)
            return (0, blk)

        return pl.BlockSpec((S, tn), index_map)

    return pl.pallas_call(
        _after(body, 6, after), name="gw_in", grid=(len(cols),),
        out_shape=(jax.ShapeDtypeStruct((D_MODEL, D_IN), F32), jax.ShapeDtypeStruct((D_MODEL, D_IN), BF)),
        in_specs=[pl.BlockSpec((D_MODEL, S), lambda j: (0, 0), pipeline_mode=pl.Buffered(1))]
        + [piece_spec(pc) for pc in range(5)]
        + [_ANY] * len(after),
        out_specs=[pl.BlockSpec((D_MODEL, tn), lambda j: (0, j)), pl.BlockSpec((D_MODEL, tn), lambda j: (0, j))],
        compiler_params=_params(56, dimension_semantics=("arbitrary",)),
    )(ht, dq, dk, dv, dzt, dzs, *after)


_HBM = pl.BlockSpec(memory_space=pltpu.HBM)
_SEM = pl.BlockSpec(memory_space=pltpu.SEMAPHORE)
_ANY = pl.BlockSpec(memory_space=pl.ANY)
_EFFECT = pltpu.SideEffectType.DATAFLOW_SIDE_EFFECTING


def _in_hbm(a):
    return pltpu.with_memory_space_constraint(a, pltpu.HBM)


def _after(body, n_in, after):
    if not after:
        return body
    return lambda *refs: body(*refs[:n_in], *refs[n_in + len(after):])


class _Started:
    def __init__(self, send, recv, bufs, token):
        self.send, self.recv, self.bufs, self.token = send, recv, bufs, token


_PEER_SETS = {"sibling": 7, "chips": 8, "both": 9}


def _peers(kind):
    x, y, c, chips = _mesh_pos()
    return ([(x, y, 1 - c)] if kind in ("sibling", "both") else []) + (
        [(cx, cy, c) for cx, cy in chips] if kind in ("chips", "both") else [])


def _signal_peers(kind):
    barrier = pltpu.get_barrier_semaphore()
    targets = _peers(kind)
    for peer in targets:
        pl.semaphore_signal(barrier, inc=1, device_id=peer, device_id_type=MESH)
    return lambda: pl.semaphore_wait(barrier, len(targets))


def _split_start(name, bufs, n_copies, copies, peers, after=()):
    nb = len(bufs)

    def body(*refs):
        _signal_peers(peers)()
        refs = refs[:nb] + refs[nb + len(after):]
        for cp in copies(refs[:nb], refs[nb], refs[nb + 1]):
            cp.start()
        refs[-1][...] = jnp.zeros_like(refs[-1])

    outs = pl.pallas_call(
        body, name=name,
        out_shape=(pltpu.SemaphoreType.DMA((n_copies,)), pltpu.SemaphoreType.DMA((n_copies,)),
                   *[pltpu.HBM(b.shape, b.dtype) for b in bufs], jax.ShapeDtypeStruct((8, 128), F32)),
        in_specs=[_HBM] * nb + [_ANY] * len(after),
        out_specs=(_SEM, _SEM, *[_HBM] * nb, pl.BlockSpec(memory_space=pltpu.VMEM)),
        input_output_aliases={k: 2 + k for k in range(nb)},
        compiler_params=_params(1, has_side_effects=_EFFECT, collective_id=_PEER_SETS[peers]),
    )(*[_in_hbm(b) for b in bufs], *after)
    return _Started(outs[0], outs[1], list(outs[2:2 + nb]), outs[-1])


def _split_wait(name, started, copies, after):
    nb = len(started.bufs)
    after = tuple(after) if isinstance(after, (tuple, list)) else (after,)

    def body(*refs):
        for cp in copies(refs[:nb], refs[nb], refs[nb + 1]):
            cp.wait_send()
            cp.wait_recv()

    return list(pl.pallas_call(
        body, name=name,
        out_shape=tuple(pltpu.HBM(b.shape, b.dtype) for b in started.bufs),
        in_specs=[_HBM] * nb + [_SEM, _SEM] + [_ANY] * len(after),
        out_specs=tuple([_HBM] * nb),
        input_output_aliases={k: k for k in range(nb)},
        compiler_params=_params(1, has_side_effects=_EFFECT),
    )(*started.bufs, started.send, started.recv, *after))


def _x1_copies(ws):
    def copies(refs, send_sems, recv_sems):
        x, y, c, _ = _mesh_pos()
        out = []
        for k, w in enumerate(ws):
            for s in range(N_SHARD):
                out.append(pltpu.make_async_remote_copy(
                    src_ref=_UNITS[w](refs[k], s, 1 - c), dst_ref=refs[len(ws) + k].at[s],
                    send_sem=send_sems.at[N_SHARD * k + s], recv_sem=recv_sems.at[N_SHARD * k + s],
                    device_id=(x, y, 1 - c), device_id_type=MESH))
        return out
    return copies


def _x2_copies(n):
    def copies(refs, send_sems, recv_sems):
        x, y, c, chips = _mesh_pos()
        out = []
        for j, (cx, cy) in enumerate(chips):
            for k in range(n):
                out.append(pltpu.make_async_remote_copy(
                    src_ref=refs[k].at[2 * cx + cy], dst_ref=refs[n + k].at[j],
                    send_sem=send_sems.at[3 * k + j], recv_sem=recv_sems.at[3 * k + j],
                    device_id=(cx, cy, c), device_id_type=MESH))
        return out
    return copies


def _x3_copies(ws):
    def copies(refs, send_sems, recv_sems):
        x, y, c, _ = _mesh_pos()
        out = []
        for k, w in enumerate(ws):
            rows = _HALF_ROWS[w]
            mine = refs[k].at[pl.ds(_mo(c * rows, rows), rows), :]
            out.append(pltpu.make_async_remote_copy(
                src_ref=mine, dst_ref=mine, send_sem=send_sems.at[k], recv_sem=recv_sems.at[k],
                device_id=(x, y, 1 - c), device_id_type=MESH))
        return out
    return copies


def _x1_lands(ws, dtype=F32):
    return [lax.empty((N_SHARD,) + _UNIT_SHAPES[w], dtype) for w in ws]


def _x2_lands(ws):
    return [lax.empty((3,) + _UNIT_SHAPES[w], BF) for w in ws]


def _grad_add1(w, g, recv, pos):
    ur, uc = _UNIT_SHAPES[w]

    def body(pos_ref, g_ref, r_ref, csb_ref):
        csb_ref[0] = (g_ref[...] + r_ref[0].astype(F32)).astype(BF)

    u3 = lambda k, pos: (pos[2 + k], 0, 0)
    return pl.pallas_call(
        body, name=f"grad_add1_{w}",
        grid_spec=pltpu.PrefetchScalarGridSpec(
            num_scalar_prefetch=1, grid=(N_SHARD - 1,),
            in_specs=[pl.BlockSpec((ur, uc), lambda k, pos: (pos[0], pos[2 + k])), pl.BlockSpec((1, ur, uc), u3)],
            out_specs=pl.BlockSpec((1, ur, uc), u3)),
        out_shape=jax.ShapeDtypeStruct((N_SHARD, ur, uc), BF),
        compiler_params=_params(40, dimension_semantics=("arbitrary",)),
    )(pos, g, recv)


def _grad_add1_group(ws, gs, recvs, pos):
    n = len(ws)

    def body(pos_ref, *refs):
        s = pl.program_id(0)
        for k in range(n):
            g, r, own, csb = refs[k], refs[n + k], refs[2 * n + k], refs[3 * n + k]
            v = g[...] + r[0]
            csb[0] = v.astype(BF)

            @pl.when(s == pos_ref[1])
            def _(own=own, v=v):
                own[...] = v

    def g_spec(w):
        if w == 3:
            return pl.BlockSpec(_UNIT_SHAPES[w], lambda s, pos: (2 * s + pos[0], 0))
        return pl.BlockSpec(_UNIT_SHAPES[w], lambda s, pos: (pos[0], s))

    slot = lambda w: pl.BlockSpec((1,) + _UNIT_SHAPES[w], lambda s, pos: (s, 0, 0))
    outs = pl.pallas_call(
        body, name="grad_add1_group",
        grid_spec=pltpu.PrefetchScalarGridSpec(
            num_scalar_prefetch=1, grid=(N_SHARD,),
            in_specs=[g_spec(w) for w in ws] + [slot(w) for w in ws],
            out_specs=[pl.BlockSpec(_UNIT_SHAPES[w], lambda s, pos: (0, 0)) for w in ws] + [slot(w) for w in ws]),
        out_shape=tuple(jax.ShapeDtypeStruct(_UNIT_SHAPES[w], F32) for w in ws)
        + tuple(jax.ShapeDtypeStruct((N_SHARD,) + _UNIT_SHAPES[w], BF) for w in ws),
        compiler_params=_params(32, dimension_semantics=("arbitrary",)),
    )(pos, *gs, *recvs)
    return list(outs[:n]), list(outs[n:])


def _grad_add2_group(ws, owns, recvs):
    n = len(ws)

    def body(*refs):
        c = lax.axis_index("c")
        for k, w in enumerate(ws):
            own, r, o = refs[k], refs[n + k], refs[2 * n + k]
            rows = _HALF_ROWS[w]
            total = ((own[...] + r[0].astype(F32)) + r[1].astype(F32)) + r[2].astype(F32)
            o[pl.ds(_mo(c * rows, rows), rows), :] = total

    vm = pl.BlockSpec(memory_space=pltpu.VMEM)
    return list(pl.pallas_call(
        body, name="grad_add2_group",
        out_shape=tuple(jax.ShapeDtypeStruct(_SHARD_SHAPES[w], F32) for w in ws),
        in_specs=[vm] * (2 * n), out_specs=[vm] * n,
        compiler_params=_params(32),
    )(*owns, *recvs))


def _grad_add2(w, g, recv1, recv2, pos):
    ur, uc = _UNIT_SHAPES[w]
    nt = 4
    tr = ur // nt

    def body(pos_ref, g_ref, r1_ref, r2_ref, o_ref):
        own = g_ref[...] + r1_ref[0].astype(F32)
        o_ref[...] = ((own + r2_ref[0].astype(F32)) + r2_ref[1].astype(F32)) + r2_ref[2].astype(F32)

    mine = lambda t, pos: (pos[0] * nt + t, 0)
    return pl.pallas_call(
        body, name=f"grad_add2_{w}",
        grid_spec=pltpu.PrefetchScalarGridSpec(
            num_scalar_prefetch=1, grid=(nt,),
            in_specs=[pl.BlockSpec((tr, uc), lambda t, pos: (pos[0] * nt + t, pos[1])),
                      pl.BlockSpec((1, tr, uc), lambda t, pos: (pos[1], t, 0)),
                      pl.BlockSpec((3, tr, uc), lambda t, pos: (0, t, 0))],
            out_specs=pl.BlockSpec((tr, uc), mine)),
        out_shape=jax.ShapeDtypeStruct(_SHARD_SHAPES[w], F32),
        compiler_params=_params(32, dimension_semantics=("arbitrary",)),
    )(pos, g, recv1, recv2)


def _adamw_math(w, g, m, v):
    m = ADAM_B1 * m + (1.0 - ADAM_B1) * g
    v = ADAM_B2 * v + (1.0 - ADAM_B2) * (g * g)
    m_hat = m / ADAM_C1
    v_hat = v / ADAM_C2
    delta = -ADAM_LR * (m_hat / (jnp.sqrt(v_hat) + ADAM_EPS) + ADAM_WD * w)
    return delta, m, v


ADAMW_STEPS = 4


def _adamw(ws_, gs, ms, vs):
    n = len(ws_)

    def body(*refs):
        for k in range(n):
            w, g, m, v = (refs[j * n + k] for j in range(4))
            d, nm, nv, gc = (refs[(4 + j) * n + k] for j in range(4))
            gv = g[...]
            d[...], nm[...], nv[...] = _adamw_math(w[...], gv, m[...], v[...])
            gc[...] = gv

    specs = [pl.BlockSpec((a.shape[0] // ADAMW_STEPS, a.shape[1]), lambda i: (i, 0)) for a in ws_] * 4
    outs = pl.pallas_call(
        body, name="adamw", grid=(ADAMW_STEPS,),
        out_shape=tuple(jax.ShapeDtypeStruct(a.shape, F32) for _ in range(4) for a in ws_),
        in_specs=specs, out_specs=specs,
        compiler_params=_params(40, dimension_semantics=("arbitrary",)),
    )(*ws_, *gs, *ms, *vs)
    return [tuple(outs[j * n + k] for j in range(4)) for k in range(n)]


_REL_PAD = 384
_VEC_FIELDS = (("norm_g", 0, D_MODEL), ("b_gate", 1024, 2 * D_MODEL), ("sgu_ln_g", 3072, D_B),
               ("sgu_ln_b", 3584, D_B), ("b_s", 4096, N_GROUPS * 128), ("final_g", 4608, D_MODEL))
_LOSS_OFF = 5632
_REL_OFF = 5760
_NV = _REL_OFF + N_HEADS * _REL_PAD
_N_FIELDS = len(_VEC_FIELDS) + 2


_B_S_FIELD = [f[0] for f in _VEC_FIELDS].index("b_s")


def _assemble_row(dst, fields, transposed_b_s):
    for f, (_, off, n) in enumerate(_VEC_FIELDS):
        if transposed_b_s and f == _B_S_FIELD:
            t = fields[f][...].T
            for g in range(N_GROUPS):
                dst[:, off + 128 * g:off + 128 * (g + 1)] = t[g:g + 1, :]
        else:
            dst[:, off:off + n] = fields[f][...]
    for r in range(N_HEADS):
        dst[:, _REL_OFF + _REL_PAD * r:_REL_OFF + _REL_PAD * (r + 1)] = fields[len(_VEC_FIELDS)][r:r + 1, :]


def _small_reduce(grads, loss_row, after=()):
    n_in = _N_FIELDS + 1

    def body(*refs):
        g_refs, loss_ref = refs[:_N_FIELDS], refs[_N_FIELDS]
        out_v, out_w = refs[n_in:n_in + 2]
        mine_v, mine_w, gath_v, gath_w, send_sems, recv_sems = refs[n_in + 2:]
        x, y, c, chips = _mesh_pos()
        me, sibling = (x, y, c), (x, y, 1 - c)

        peers_entered = _signal_peers("both")
        _assemble_row(mine_v, g_refs, True)
        mine_v[:, _LOSS_OFF:_LOSS_OFF + 128] = loss_ref[...]
        mine_w[...] = g_refs[-1][...].astype(BF)
        peers_entered()
        my_k = 4 * x + 2 * y + c
        gath_v[my_k] = mine_v[...]
        gath_w[my_k] = mine_w[...]

        def copy(k, gath, block, to, src=None):
            dst = gath.at[4 * block[0] + 2 * block[1] + block[2]]
            return pltpu.make_async_remote_copy(
                src_ref=dst if src is None else src, dst_ref=dst,
                send_sem=send_sems.at[k], recv_sem=recv_sems.at[k], device_id=to, device_id_type=MESH)

        bufs = ((gath_v, mine_v), (gath_w, mine_w))
        first, passed = [], []
        for b, (gath, mine) in enumerate(bufs):
            first.append(copy(7 * b, gath, me, sibling, src=mine))
            first += [copy(7 * b + 1 + j, gath, me, (*chip, c), src=mine) for j, chip in enumerate(chips)]
        for cp in first:
            cp.start()
        for b, (gath, _) in enumerate(bufs):
            for j, chip in enumerate(chips):
                copy(7 * b + 1 + j, gath, (*chip, c), me).wait_recv()
                cp = copy(7 * b + 4 + j, gath, (*chip, c), sibling)
                cp.start()
                passed.append(cp)
        for b, (gath, _) in enumerate(bufs):
            copy(7 * b, gath, sibling, me).wait_recv()
            for j, chip in enumerate(chips):
                copy(7 * b + 4 + j, gath, (*chip, 1 - c), me).wait_recv()
        for cp in first + passed:
            cp.wait_send()

        tot_v, tot_w = gath_v[0], gath_w[0].astype(F32)
        for k in range(1, 8):
            tot_v = tot_v + gath_v[k]
            tot_w = tot_w + gath_w[k].astype(F32)
        out_v[...] = tot_v
        out_w[...] = tot_w

    vm = pl.BlockSpec(memory_space=pltpu.VMEM)
    return pl.pallas_call(
        _after(body, n_in, after), name="small_reduce",
        out_shape=(jax.ShapeDtypeStruct((1, _NV), F32), jax.ShapeDtypeStruct((N_GROUPS * 128, 128), F32)),
        in_specs=[vm] * n_in + [_ANY] * len(after), out_specs=[vm] * 2,
        scratch_shapes=[pltpu.VMEM((1, _NV), F32), pltpu.VMEM((N_GROUPS * 128, 128), BF),
                        pltpu.VMEM((8, 1, _NV), F32), pltpu.VMEM((8, N_GROUPS * 128, 128), BF),
                        pltpu.SemaphoreType.DMA((14,)), pltpu.SemaphoreType.DMA((14,))],
        compiler_params=_params(32, collective_id=_PEER_SETS["both"]),
    )(*grads, loss_row, *after)


def _small_adamw(tot_v, tot_w, params):
    n_in = 2 + 3 * _N_FIELDS

    def body(*refs):
        tv_ref, tw_ref = refs[:2]
        p_refs = [refs[2 + k * _N_FIELDS:2 + (k + 1) * _N_FIELDS] for k in range(3)]
        outs = refs[n_in:n_in + 4 * _N_FIELDS + 1]
        wmv = refs[-1]
        for k in range(3):
            _assemble_row(wmv.at[k], p_refs[k], False)
            wmv[k, :, _LOSS_OFF:_LOSS_OFF + 128] = jnp.zeros((1, 128), F32)
        tot_v, tot_w = tv_ref[...], tw_ref[...]
        res_v = (tot_v,) + _adamw_math(wmv[0], tot_v, wmv[1], wmv[2])
        res_w = (tot_w,) + _adamw_math(p_refs[0][-1][...], tot_w, p_refs[1][-1][...], p_refs[2][-1][...])
        for kind in range(4):
            o = outs[kind * _N_FIELDS:(kind + 1) * _N_FIELDS]
            for f, (_, off, n) in enumerate(_VEC_FIELDS):
                o[f][...] = res_v[kind][:, off:off + n]
            for r in range(N_HEADS):
                o[len(_VEC_FIELDS)][r:r + 1, :] = res_v[kind][:, _REL_OFF + _REL_PAD * r:_REL_OFF + _REL_PAD * (r + 1)]
            o[-1][...] = res_w[kind]
        outs[-1][...] = tot_v[:, _LOSS_OFF:_LOSS_OFF + 128]

    field_shapes = [(1, n) for _, _, n in _VEC_FIELDS] + [(N_HEADS, _REL_PAD), (N_GROUPS * 128, 128)]
    vm = pl.BlockSpec(memory_space=pltpu.VMEM)
    operands = [tot_v, tot_w] + [a for p in params for a in p]
    assert len(operands) == n_in
    outs = pl.pallas_call(
        body, name="small_adamw",
        out_shape=tuple(jax.ShapeDtypeStruct(s, F32) for _ in range(4) for s in field_shapes)
        + (jax.ShapeDtypeStruct((1, 128), F32),),
        in_specs=[vm] * n_in, out_specs=[vm] * (4 * _N_FIELDS + 1),
        scratch_shapes=[pltpu.VMEM((3, 1, _NV), F32)],
        compiler_params=_params(32),
    )(*operands)
    return [outs[k * _N_FIELDS:(k + 1) * _N_FIELDS] for k in range(4)], outs[-1]


def _small_fields(norm_g, b_gate, ln_g, ln_b, b_s, final_g, rel_bias, w_s):
    rel = jnp.pad(rel_bias.reshape(N_HEADS, N_REL), ((0, 0), (0, _REL_PAD - N_REL)))
    return (norm_g, b_gate, ln_g, ln_b, b_s.reshape(1, N_GROUPS * 128), final_g.reshape(1, D_MODEL),
            rel, w_s.reshape(N_GROUPS * 128, 128))


def _small_outputs(fields):
    n_g, b_g, l_g, l_b, b_s, f_g, rel, w_s = fields
    return (n_g, b_g, rel[:, :N_REL].reshape(1, N_HEADS, N_REL), l_g, l_b,
            w_s.reshape(1, N_GROUPS, 128, 128), b_s.reshape(1, N_GROUPS, 128), f_g.reshape(D_MODEL))


def _bias_row(rel_bias):
    hi = rel_bias[:, N_REL - 1:N_REL]
    lo = rel_bias[:, 0:1]
    return jnp.concatenate([jnp.broadcast_to(hi, (N_HEADS, 384)), rel_bias[:, ::-1],
                            jnp.broadcast_to(lo, (N_HEADS, 191)), jnp.broadcast_to(hi, (N_HEADS, 192))], axis=1)


def kernel(x, norm_g, w_in, b_gate, rel_bias, sgu_ln_g, sgu_ln_b, w_s, b_s, w_pa, w_pb, w_out, final_g, loss_target, m_norm_g, m_w_in, m_b_gate, m_rel_bias, m_sgu_ln_g, m_sgu_ln_b, m_w_s, m_b_s, m_w_pa, m_w_pb, m_w_out, m_final_g, v_norm_g, v_w_in, v_b_gate, v_rel_bias, v_sgu_ln_g, v_sgu_ln_b, v_w_s, v_b_s, v_w_pa, v_w_pb, v_w_out, v_final_g):
    S = x.shape[1]
    xs = x.reshape(S, D_MODEL)
    tgt = loss_target.reshape(S, D_MODEL)
    big_w = (w_in[0], w_pa[0], w_pb[0], w_out[0])
    big_m = (m_w_in[0], m_w_pa[0], m_w_pb[0], m_w_out[0])
    big_v = (v_w_in[0], v_w_pa[0], v_w_pb[0], v_w_out[0])
    rel = rel_bias[0]
    ws = w_s[0]
    bst = b_s[0].T
    fg = final_g.reshape(1, D_MODEL)
    chip = 2 * lax.axis_index("x") + lax.axis_index("y")
    pos = jnp.stack([lax.axis_index("c"), chip] + [(chip + k) % N_SHARD for k in range(1, N_SHARD)]).astype(jnp.int32)

    (w_in_bf,), staged, band_bias = _ag_weights((0,), big_w[:1], (1, 2, 3), big_w[1:], _bias_row(rel))
    ag_s = _split_start("ag_small_start", staged, 9, _gather_copies((1, 2, 3)), "chips", after=(w_in_bf,))

    ht, q3, k3, v3, zrest = _inproj_fwd(xs, norm_g, w_in_bf, after=(ag_s.token,))
    att, lse = _attn_fwd(q3, k3, v3, band_bias)
    w_pa_bf, w_pb_bf, w_out_bf = _split_wait("ag_small_wait", ag_s, _gather_copies((1, 2, 3)), att)
    (d_out, d_att, dzt, dzs, gw_out, gw_pa, gw_pb, g_bgate, g_final, loss_row,
     g_ws, g_bs_t, g_lng, g_lnb) = _tail_sgu(
        att, zrest, xs, tgt, w_pa_bf, w_pb_bf, w_out_bf, b_gate, fg, sgu_ln_g, sgu_ln_b, ws, bst)
    ws_s, ws_i = (1, 2, 3), (0,)

    x1s = _split_start("gx1s_start", [gw_pa, gw_pb, gw_out] + _x1_lands(ws_s), 12, _x1_copies(ws_s), "sibling")
    dq, dk, dv, d_gp = _attn_bwd(q3, k3, v3, d_att, lse, band_bias, after=(x1s.token,))
    got = _split_wait("gx1s_wait", x1s, _x1_copies(ws_s), dq)
    own_s, csb_s = _grad_add1_group(ws_s, got[:3], got[3:], pos)

    x2s = _split_start("gx2s_start", csb_s + _x2_lands(ws_s), 9, _x2_copies(3), "chips")
    gw_in, gw_in_bf = _gw_in(ht, dq, dk, dv, dzt, dzs, after=(x2s.token,))
    x1i = _split_start("gx1i_start", [gw_in_bf] + _x1_lands(ws_i, BF), 4, _x1_copies(ws_i), "sibling")
    got = _split_wait("gx2s_wait", x2s, _x2_copies(3), x1i.token)
    halves_s = _grad_add2_group(ws_s, own_s, got[3:])
    x3s = _split_start("gx3s_start", halves_s, 3, _x3_copies(ws_s), "sibling")
    g_rel = jnp.pad(d_gp[:, 384:384 + N_REL][:, ::-1], ((0, 0), (0, _REL_PAD - N_REL)))
    small_params = (_small_fields(norm_g, b_gate, sgu_ln_g, sgu_ln_b, b_s, final_g, rel_bias, w_s),
                    _small_fields(m_norm_g, m_b_gate, m_sgu_ln_g, m_sgu_ln_b, m_b_s, m_final_g, m_rel_bias, m_w_s),
                    _small_fields(v_norm_g, v_b_gate, v_sgu_ln_g, v_sgu_ln_b, v_b_s, v_final_g, v_rel_bias, v_w_s))
    relayouts = (g_rel,) + tuple(fields[-2] for fields in small_params)
    recv1_i = _split_wait("gx1i_wait", x1i, _x1_copies(ws_i), (x3s.token,) + relayouts)[1]
    csb_i = _grad_add1(0, gw_in, recv1_i, pos)

    x2i = _split_start("gx2i_start", [csb_i] + _x2_lands(ws_i), 3, _x2_copies(1), "chips")
    grad_x, g_norm = _dh_gradx(dq, dk, dv, dzt, dzs, w_in_bf, xs, norm_g, d_out, after=(x2i.token,))
    g_shards_s = _split_wait("gx3s_wait", x3s, _x3_copies(ws_s), grad_x)
    got = _split_wait("gx2i_wait", x2i, _x2_copies(1), grad_x)
    half_i = _grad_add2(0, gw_in, recv1_i, got[1], pos)
    x3i = _split_start("gx3i_start", [half_i], 1, _x3_copies(ws_i), "sibling")

    small_grads = (g_norm, g_bgate, g_lng, g_lnb, g_bs_t, g_final, g_rel, g_ws.reshape(N_GROUPS * 128, 128))
    tot_v, tot_w = _small_reduce(small_grads, loss_row, after=(x3i.token,))
    (gsum, sdelta, sm, sv), loss_out = _small_adamw(tot_v, tot_w, small_params)

    g_shard_i, = _split_wait("gx3i_wait", x3i, _x3_copies(ws_i), loss_out)
    big = _adamw(big_w, [g_shard_i] + g_shards_s, big_m, big_v)
    sg_out, sd_out, sm_out, sv_out = (_small_outputs(f) for f in (gsum, sdelta, sm, sv))
    loss = loss_out[0, 0]

    def assemble(small, bigs):
        n_g, b_g, r_b, l_g, l_b, w_s_, b_s_, f_g = small
        b_in, b_pa, b_pb, b_out = (b[None] for b in bigs)
        return (n_g, b_in, b_g, r_b, l_g, l_b, w_s_, b_s_, b_pa, b_pb, b_out, f_g)

    grads_out = assemble(sg_out, [b[3] for b in big])
    delta_out = assemble(sd_out, [b[0] for b in big])
    m_out = assemble(sm_out, [b[1] for b in big])
    v_out = assemble(sv_out, [b[2] for b in big])
    return (loss, grad_x.reshape(1, S, D_MODEL), *grads_out, *delta_out, *m_out, *v_out)
```

```python
import functools
import math

import jax
import jax.numpy as jnp
from jax import lax
from jax.experimental import pallas as pl
from jax.experimental.pallas import tpu as pltpu

F32 = jnp.float32
BF = jnp.bfloat16
MESH = pl.DeviceIdType.MESH

D_MODEL = 1024
D_A = 512
D_B = 512
D_IN = 5632
N_HEADS = 8
HEAD_DIM = 64
CHUNK = 64
N_PREV = 8
SGU_CHUNK = 128
N_GROUPS = 4
N_REL = 257
EPS = 1e-6
NEG_INF = -1e30
SCALE = HEAD_DIM ** -0.5

QB = 2 * CHUNK
KB = (N_PREV + 2) * CHUNK
PADK = N_PREV * CHUNK
ROLL_W = 1024
KEEP = KB // QB - 1
Q_PER_STEP = 2

ADAM_LR = 0.001
ADAM_B1 = 0.9
ADAM_B2 = 0.999
ADAM_EPS = 1e-08
ADAM_WD = 0.01
ADAM_STEP = 10
ADAM_C1 = 1.0 - ADAM_B1 ** ADAM_STEP
ADAM_C2 = 1.0 - ADAM_B2 ** ADAM_STEP

AG_PIECES = 4
N_SHARD = 4
SHARD_IN = D_IN // N_SHARD
MIB = 1024 * 1024


V7X_VMEM_MIB = 64
VMEM_RESERVE_MIB = V7X_VMEM_MIB - 4


def _params(vmem_mib, **kw):
    assert vmem_mib <= VMEM_RESERVE_MIB
    return pltpu.CompilerParams(vmem_limit_bytes=VMEM_RESERVE_MIB * MIB, **kw)


def _sigmoid(x):
    return 1.0 / (1.0 + jnp.exp(-x))


def _silu_and_grad(x):
    s = _sigmoid(x)
    return x * s, s * (1.0 + x * (1.0 - s))


_GELU_C = math.sqrt(2.0 / math.pi)
_GELU_A = 0.044715


def _gelu_and_grad(x):
    x2 = x * x
    t = jnp.tanh(_GELU_C * (x + _GELU_A * (x2 * x)))
    cdf = 0.5 * (1.0 + t)
    grad = cdf + 0.5 * x * (1.0 - t * t) * (_GELU_C * (1.0 + 3.0 * _GELU_A * x2))
    return x * cdf, grad


def _dot(a, b):
    return jnp.dot(a, b, preferred_element_type=F32)


def _dot_nt(a, b):
    return lax.dot_general(a, b, (((1,), (1,)), ((), ())), preferred_element_type=F32)


def _dot_tn(a, b):
    return lax.dot_general(a, b, (((0,), (0,)), ((), ())), preferred_element_type=F32)


def _mo(v, m):
    return v if isinstance(v, int) else pl.multiple_of(v, m)


def _unit_in(ref, s, p):
    return ref.at[pl.ds(_mo(p * 512, 512), 512), pl.ds(_mo(s * SHARD_IN, 128), SHARD_IN)]


def _unit_p(ref, s, p):
    return ref.at[pl.ds(_mo(p * 256, 256), 256), pl.ds(_mo(s * 256, 128), 256)]


def _unit_out(ref, s, p):
    return ref.at[pl.ds(_mo(s * 256 + p * 128, 128), 128), :]


_UNITS = (_unit_in, _unit_p, _unit_p, _unit_out)
_HALF_ROWS = (512, 256, 256, 128)
_UNIT_SHAPES = ((512, SHARD_IN), (256, 256), (256, 256), (128, D_MODEL))
_FULL_SHAPES = ((D_MODEL, D_IN), (D_A, D_MODEL), (D_B, D_MODEL), (D_MODEL, D_MODEL))
_SHARD_SHAPES = ((D_MODEL, SHARD_IN), (D_A, 256), (D_B, 256), (256, D_MODEL))


def _mesh_pos():
    x, y, c = lax.axis_index("x"), lax.axis_index("y"), lax.axis_index("c")
    chips = [(1 - x, y), (x, 1 - y), (1 - x, 1 - y)]
    return x, y, c, chips


def _bias_rows(rel_ref, pad_ref):
    pad_ref[...] = jnp.zeros(pad_ref.shape, F32)
    pad_ref[:, :N_REL] = rel_ref[...]
    r = pad_ref[...]
    m = lax.broadcasted_iota(jnp.int32, (pad_ref.shape[1], ROLL_W), 1)
    m = jnp.where(m >= ROLL_W - 192, m - ROLL_W, m)
    pick = (lax.broadcasted_iota(jnp.int32, m.shape, 0) == jnp.clip(512 - m, -128, 128) + 128).astype(BF)
    hi = r.astype(BF)
    mid = (r - hi.astype(F32)).astype(BF)
    lo = ((r - hi.astype(F32)) - mid.astype(F32)).astype(BF)
    return (_dot(hi, pick) + _dot(mid, pick)) + _dot(lo, pick)


def _ag_weights(ws, shards, later_ws, later_shards, rel):
    n, m = len(ws), len(later_ws)

    def body(*refs):
        ins, later_ins, rel_ref = refs[:n], refs[n:n + m], refs[n + m]
        o = n + m + 1
        outs, later_outs, bias_ref = refs[o:o + n], refs[o + n:o + n + m], refs[o + n + m]
        o += n + m + 1
        stage, later_stage = refs[o:o + n], refs[o + n:o + n + m]
        send_sems, recv_sems, local_sems, later_sems, rel_pad, gp_ref = refs[o + n + m:]
        x, y, c, chips = _mesh_pos()
        s_me = 2 * x + y
        sibling = (x, y, 1 - c)
        def rows_of(k, p):
            rows = _HALF_ROWS[ws[k]]
            return pl.ds(_mo(p * rows, rows), rows)

        def half(k, p):
            return stage[k].at[rows_of(k, p), :]

        def unit(k, s, p):
            return _UNITS[ws[k]](outs[k], s, p)

        def rcopy(k, i, src, dst, to):
            return pltpu.make_async_remote_copy(src_ref=src, dst_ref=dst, send_sem=send_sems.at[k, i],
                                                recv_sem=recv_sems.at[k, i], device_id=to, device_id_type=MESH)

        peers_entered = _signal_peers("both")
        for k in range(n):
            stage[k][rows_of(k, c), :] = ins[k][rows_of(k, c), :].astype(BF)
        peers_entered()
        def piece(ref, k, q):
            rows = _HALF_ROWS[ws[k]] // AG_PIECES
            return ref.at[pl.ds(q * rows, rows), :]

        sends = []
        for q in range(AG_PIECES):
            for j, (cx, cy) in enumerate(chips):
                for k in range(n):
                    cp = rcopy(k, j * AG_PIECES + q, piece(half(k, c), k, q), piece(unit(k, s_me, c), k, q),
                               (cx, cy, c))
                    cp.start()
                    sends.append(cp)
        for k in range(n):
            stage[k][rows_of(k, 1 - c), :] = ins[k][rows_of(k, 1 - c), :].astype(BF)
        local = []
        for k in range(n):
            for p in range(2):
                cp = pltpu.make_async_copy(half(k, p), unit(k, s_me, p), local_sems.at[k, p])
                cp.start()
                local.append(cp)
        for k, w in enumerate(later_ws):
            later_stage[k][...] = later_ins[k][...].astype(BF)
            cp = pltpu.make_async_copy(later_stage[k], _shard_of(later_outs[k], w, s_me), later_sems.at[k])
            cp.start()
            local.append(cp)
        keep = _struct_mask()
        gp_ref[...] = _bias_rows(rel_ref, rel_pad)
        for h in range(N_HEADS):
            bias_ref[h] = jnp.where(keep, _skew_table(gp_ref[h:h + 1, :])[:, :KB], NEG_INF)
        for q in range(AG_PIECES):
            for j, (cx, cy) in enumerate(chips):
                for k in range(n):
                    landed = piece(unit(k, 2 * cx + cy, c), k, q)
                    rcopy(k, j * AG_PIECES + q, landed, landed, (cx, cy, c)).wait_recv()
                    cp = rcopy(k, (3 + j) * AG_PIECES + q, landed, landed, sibling)
                    cp.start()
                    sends.append(cp)
        for q in range(AG_PIECES):
            for j, (cx, cy) in enumerate(chips):
                for k in range(n):
                    other = piece(unit(k, 2 * cx + cy, 1 - c), k, q)
                    rcopy(k, (3 + j) * AG_PIECES + q, other, other, sibling).wait_recv()
        for cp in sends:
            cp.wait_send()
        for cp in local:
            cp.wait()

    vm = pl.BlockSpec(memory_space=pltpu.VMEM)
    outs = pl.pallas_call(
        body, name="ag_weights",
        out_shape=tuple(jax.ShapeDtypeStruct(_FULL_SHAPES[w], BF) for w in tuple(ws) + tuple(later_ws))
        + (jax.ShapeDtypeStruct((N_HEADS, QB, KB), F32),),
        in_specs=[vm] * (n + m + 1), out_specs=[_ANY] * (n + m) + [vm],
        scratch_shapes=[pltpu.VMEM(_SHARD_SHAPES[w], BF) for w in tuple(ws) + tuple(later_ws)]
        + [pltpu.SemaphoreType.DMA((n, 6 * AG_PIECES)), pltpu.SemaphoreType.DMA((n, 6 * AG_PIECES)),
           pltpu.SemaphoreType.DMA((n, 2)), pltpu.SemaphoreType.DMA((m,)),
           pltpu.VMEM((N_HEADS, _REL_PAD), F32), pltpu.VMEM((N_HEADS, ROLL_W), F32)],
        compiler_params=_params(48, collective_id=_PEER_SETS["both"]),
    )(*shards, *later_shards, rel)
    return list(outs[:n]), list(outs[n:n + m]), outs[-1]


def _shard_of(ref, w, s):
    if w == 0:
        return ref.at[:, pl.ds(_mo(s * SHARD_IN, 128), SHARD_IN)]
    if w == 3:
        return ref.at[pl.ds(_mo(s * 256, 256), 256), :]
    return ref.at[:, pl.ds(_mo(s * 256, 128), 256)]


def _gather_copies(ws):
    def copies(refs, send_sems, recv_sems):
        x, y, c, chips = _mesh_pos()
        out = []
        for j, (cx, cy) in enumerate(chips):
            for k, w in enumerate(ws):
                mine = _shard_of(refs[k], w, 2 * x + y)
                out.append(pltpu.make_async_remote_copy(
                    src_ref=mine, dst_ref=mine, send_sem=send_sems.at[3 * k + j], recv_sem=recv_sems.at[3 * k + j],
                    device_id=(cx, cy, c), device_id_type=MESH))
        return out
    return copies


def _inproj_fwd(x, norm_g, w_in_bf, tm=512, after=()):
    S = x.shape[0]

    def body(x_ref, g_ref, w_ref, ht_ref, q_ref, k_ref, v_ref, zr_ref):
        xv = x_ref[...]
        r = lax.rsqrt(jnp.mean(xv * xv, axis=-1, keepdims=True) + EPS)
        hf = (xv * r) * g_ref[...]
        ht_ref[...] = hf.T.astype(BF)
        h = hf.astype(BF)
        heads = (q_ref, k_ref, v_ref)
        for j in range(D_IN // 512):
            z = _dot(h, w_ref[:, j * 512:(j + 1) * 512])
            if j < 3:
                zb = z.astype(BF)
                for hd in range(N_HEADS):
                    heads[j][hd] = zb[:, hd * HEAD_DIM:(hd + 1) * HEAD_DIM]
            else:
                zr_ref[:, (j - 3) * 512:(j - 2) * 512] = z

    head_major = jax.ShapeDtypeStruct((N_HEADS, S, HEAD_DIM), BF)
    head_spec = pl.BlockSpec((N_HEADS, tm, HEAD_DIM), lambda i: (0, i, 0))
    return pl.pallas_call(
        _after(body, 3, after), name="inproj_fwd", grid=(S // tm,),
        out_shape=(jax.ShapeDtypeStruct((D_MODEL, S), BF), head_major, head_major, head_major,
                   jax.ShapeDtypeStruct((S, D_IN - 3 * D_A), F32)),
        in_specs=[pl.BlockSpec((tm, D_MODEL), lambda i: (i, 0)),
                  pl.BlockSpec((1, D_MODEL), lambda i: (0, 0)),
                  pl.BlockSpec((D_MODEL, D_IN), lambda i: (0, 0), pipeline_mode=pl.Buffered(1))]
        + [_ANY] * len(after),
        out_specs=[pl.BlockSpec((D_MODEL, tm), lambda i: (0, i)),
                   head_spec, head_spec, head_spec,
                   pl.BlockSpec((tm, D_IN - 3 * D_A), lambda i: (i, 0))],
        compiler_params=_params(52, dimension_semantics=("arbitrary",)),
    )(x, norm_g, w_in_bf, *after)


def _skew_table(gp_row):
    row = lax.broadcasted_iota(jnp.int32, (QB, ROLL_W), 0)
    t = jnp.broadcast_to(gp_row, (QB, ROLL_W))
    for b in range(7):
        t = jnp.where(((row >> b) & 1) == 1, pltpu.roll(t, 1 << b, axis=1), t)
    return t


def _unskew_sum(d):
    row = lax.broadcasted_iota(jnp.int32, (QB, ROLL_W), 0)
    for b in range(7):
        d = jnp.where(((row >> b) & 1) == 1, pltpu.roll(d, ROLL_W - (1 << b), axis=1), d)
    return jnp.sum(d, axis=0, keepdims=True)


def _struct_mask():
    a = lax.broadcasted_iota(jnp.int32, (QB, KB), 0) // CHUNK
    b = lax.broadcasted_iota(jnp.int32, (QB, KB), 1) // CHUNK
    return (b >= a) & (b <= a + N_PREV)


def _load_kv(k_hbm, v_hbm, k_scr, v_scr, sems, S, meanwhile=lambda: None):
    zeros = jnp.zeros((N_HEADS, PADK, HEAD_DIM), BF)
    k_scr[:, 0:PADK, :] = zeros
    v_scr[:, 0:PADK, :] = zeros
    ck = pltpu.make_async_copy(k_hbm, k_scr.at[:, pl.ds(PADK, S), :], sems.at[0])
    cv = pltpu.make_async_copy(v_hbm, v_scr.at[:, pl.ds(PADK, S), :], sems.at[1])
    ck.start()
    cv.start()
    meanwhile()
    ck.wait()
    cv.wait()


_BATCH_NT = (((2,), (2,)), ((0,), (0,)))
_BATCH_NN = (((2,), (1,)), ((0,), (0,)))
_BATCH_TN = (((1,), (1,)), ((0,), (0,)))


def _bdot(a, b, dims):
    return lax.dot_general(a, b, dims, preferred_element_type=F32)


def _scaled(q):
    return q * jnp.asarray(SCALE, BF)


def _scores(qs, kb, bias, i, front):
    s = _bdot(qs, kb, _BATCH_NT) + bias
    if front:
        col = lax.broadcasted_iota(jnp.int32, (1, 1, KB), 2)
        s = jnp.where(col >= PADK - i * QB, s, NEG_INF)
    return s


def _attn_fwd(q3, k3, v3, bias):
    S = q3.shape[1]

    def body(q_ref, k_hbm, v_hbm, bias_ref, o_ref, lse_ref, k_scr, v_scr, sems):
        @pl.when(pl.program_id(0) == 0)
        def _():
            _load_kv(k_hbm, v_hbm, k_scr, v_scr, sems, S)

        def step(i, rows, front):
            start = pl.multiple_of(i * QB, QB)
            kb = k_scr[:, pl.ds(start, KB), :]
            vb = v_scr[:, pl.ds(start, KB), :]
            s = _scores(_scaled(q_ref[:, rows, :]), kb, bias_ref[...], i, front)
            m = jnp.max(s, axis=-1, keepdims=True)
            e = jnp.exp(s - m)
            l = jnp.sum(e, axis=-1, keepdims=True)
            p = e * (1.0 / l)
            o = _bdot(p.astype(BF), vb, _BATCH_NN)
            lse_ref[:, rows, :] = jnp.broadcast_to(m + jnp.log(l), (N_HEADS, QB, 128))
            for h in range(N_HEADS):
                o_ref[rows, h * HEAD_DIM:(h + 1) * HEAD_DIM] = o[h]

        def block(j, carry):
            i = pl.program_id(0) * Q_PER_STEP + j
            rows = pl.ds(pl.multiple_of(j * QB, QB), QB)
            pl.when(i < KEEP)(functools.partial(step, i, rows, True))
            pl.when(i >= KEEP)(functools.partial(step, i, rows, False))
            return carry

        lax.fori_loop(0, Q_PER_STEP, block, 0)

    rows_per_step = Q_PER_STEP * QB
    kv_scr = pltpu.VMEM((N_HEADS, S + PADK, HEAD_DIM), BF)
    return pl.pallas_call(
        body, name="attn_fwd", grid=(S // rows_per_step,),
        out_shape=(jax.ShapeDtypeStruct((S, D_A), F32), jax.ShapeDtypeStruct((N_HEADS, S, 128), F32)),
        in_specs=[pl.BlockSpec((N_HEADS, rows_per_step, HEAD_DIM), lambda g: (0, g, 0)),
                  pl.BlockSpec(memory_space=pl.ANY), pl.BlockSpec(memory_space=pl.ANY),
                  pl.BlockSpec((N_HEADS, QB, KB), lambda g: (0, 0, 0))],
        out_specs=[pl.BlockSpec((rows_per_step, D_A), lambda g: (g, 0)),
                   pl.BlockSpec((N_HEADS, rows_per_step, 128), lambda g: (0, g, 0))],
        scratch_shapes=[kv_scr, kv_scr, pltpu.SemaphoreType.DMA((2,))],
        compiler_params=_params(48, dimension_semantics=("arbitrary",)),
    )(q3, k3, v3, bias)


def _attn_bwd(q3, k3, v3, d_att3, lse, bias, after=()):
    S = q3.shape[1]
    nq = S // QB

    def body(q_ref, do_ref, k_hbm, v_hbm, lse_ref, bias_ref, dq_ref, dk_ref, dv_ref, dgp_ref,
             k_scr, v_scr, dk_acc, dv_acc, dbias_acc, pad_scr, sems):
        @pl.when(pl.program_id(0) == 0)
        def _():
            def clear():
                dk_acc[...] = jnp.zeros_like(dk_acc)
                dv_acc[...] = jnp.zeros_like(dv_acc)
                dbias_acc[...] = jnp.zeros_like(dbias_acc)
            _load_kv(k_hbm, v_hbm, k_scr, v_scr, sems, S, clear)

        def step(i, rows, front):
            start = pl.multiple_of(i * QB, QB)
            kb = k_scr[:, pl.ds(start, KB), :]
            vb = v_scr[:, pl.ds(start, KB), :]
            qs = _scaled(q_ref[:, rows, :])
            do = do_ref[:, rows, :]
            p = jnp.exp(_scores(qs, kb, bias_ref[...], i, front) - jnp.tile(lse_ref[:, rows, :], (1, 1, KB // 128)))
            dp = _bdot(do, vb, _BATCH_NT)
            ds = p * (dp - jnp.sum(dp * p, axis=-1, keepdims=True))
            dbias_acc[...] += ds
            dsb = ds.astype(BF)
            dq = _bdot(dsb, kb, _BATCH_NN) * SCALE
            for h in range(N_HEADS):
                dq_ref[rows, h * HEAD_DIM:(h + 1) * HEAD_DIM] = dq[h].astype(BF)
            dk_acc[...] += _bdot(dsb, qs, _BATCH_TN)
            dv_acc[...] += _bdot(p.astype(BF), do, _BATCH_TN)

        def block(j, carry):
            i = pl.program_id(0) * Q_PER_STEP + j
            rows = pl.ds(pl.multiple_of(j * QB, QB), QB)
            pl.when(i < KEEP)(functools.partial(step, i, rows, True))
            pl.when((i >= KEEP) & (i < nq))(functools.partial(step, i, rows, False))
            for h in range(N_HEADS):
                hs = slice(h * HEAD_DIM, (h + 1) * HEAD_DIM)
                dk_ref[rows, hs] = dk_acc[h, 0:QB, :].astype(BF)
                dv_ref[rows, hs] = dv_acc[h, 0:QB, :].astype(BF)
            dk_acc[:, 0:KB - QB, :] = dk_acc[:, QB:KB, :]
            dv_acc[:, 0:KB - QB, :] = dv_acc[:, QB:KB, :]
            dk_acc[:, KB - QB:KB, :] = jnp.zeros((N_HEADS, QB, HEAD_DIM), F32)
            dv_acc[:, KB - QB:KB, :] = jnp.zeros((N_HEADS, QB, HEAD_DIM), F32)
            return carry

        lax.fori_loop(0, Q_PER_STEP, block, 0)

        @pl.when(pl.program_id(0) == n_steps - 1)
        def _():
            lane = lax.broadcasted_iota(jnp.int32, (1, ROLL_W), 1)
            hi = (lane < 384) | (lane >= 832)
            lo = (lane > 640) & (lane < 832)
            pad_scr[...] = jnp.zeros_like(pad_scr)
            for h in range(N_HEADS):
                pad_scr[:, 0:KB] = dbias_acc[h]
                g = _unskew_sum(pad_scr[...])
                s_hi = jnp.sum(jnp.where(hi, g, 0.0), axis=-1, keepdims=True)
                s_lo = jnp.sum(jnp.where(lo, g, 0.0), axis=-1, keepdims=True)
                g = jnp.where(lane == 384, g + s_hi, g)
                g = jnp.where(lane == 640, g + s_lo, g)
                dgp_ref[h:h + 1, :] = g

    assert nq % Q_PER_STEP == 0 and KEEP % Q_PER_STEP == 0
    rows_per_step = Q_PER_STEP * QB
    n_steps = (nq + KEEP) // Q_PER_STEP
    last = nq // Q_PER_STEP - 1
    lag = KEEP // Q_PER_STEP
    kv_scr = pltpu.VMEM((N_HEADS, S + PADK, HEAD_DIM), BF)
    return pl.pallas_call(
        _after(body, 6, after), name="attn_bwd", grid=(n_steps,),
        out_shape=(jax.ShapeDtypeStruct((S, D_A), BF), jax.ShapeDtypeStruct((S, D_A), BF),
                   jax.ShapeDtypeStruct((S, D_A), BF), jax.ShapeDtypeStruct((N_HEADS, ROLL_W), F32)),
        in_specs=[pl.BlockSpec((N_HEADS, rows_per_step, HEAD_DIM), lambda g: (0, jnp.minimum(g, last), 0)),
                  pl.BlockSpec((N_HEADS, rows_per_step, HEAD_DIM), lambda g: (0, jnp.minimum(g, last), 0)),
                  pl.BlockSpec(memory_space=pl.ANY), pl.BlockSpec(memory_space=pl.ANY),
                  pl.BlockSpec((N_HEADS, rows_per_step, 128), lambda g: (0, jnp.minimum(g, last), 0)),
                  pl.BlockSpec((N_HEADS, QB, KB), lambda g: (0, 0, 0))] + [_ANY] * len(after),
        out_specs=[pl.BlockSpec((rows_per_step, D_A), lambda g: (jnp.minimum(g, last), 0)),
                   pl.BlockSpec((rows_per_step, D_A), lambda g: (jnp.maximum(g - lag, 0), 0)),
                   pl.BlockSpec((rows_per_step, D_A), lambda g: (jnp.maximum(g - lag, 0), 0)),
                   pl.BlockSpec((N_HEADS, ROLL_W), lambda g: (0, 0))],
        scratch_shapes=[kv_scr, kv_scr,
                        pltpu.VMEM((N_HEADS, KB, HEAD_DIM), F32), pltpu.VMEM((N_HEADS, KB, HEAD_DIM), F32),
                        pltpu.VMEM((N_HEADS, QB, KB), F32), pltpu.VMEM((QB, ROLL_W), F32),
                        pltpu.SemaphoreType.DMA((2,))],
        compiler_params=_params(56, dimension_semantics=("arbitrary",)),
    )(q3, d_att3, k3, v3, lse, bias, *after)


def _sgu_core(ub, vb, lg, lb):
    u, du = _gelu_and_grad(ub)
    v, dv = _gelu_and_grad(vb)
    mu = jnp.mean(v, axis=-1, keepdims=True)
    vc = v - mu
    rstd = lax.rsqrt(jnp.mean(vc * vc, axis=-1, keepdims=True) + EPS)
    xh = vc * rstd
    vn = xh * lg + lb
    return u, du, dv, rstd, xh, vn


def _tri():
    r = lax.broadcasted_iota(jnp.int32, (SGU_CHUNK, SGU_CHUNK), 0)
    c = lax.broadcasted_iota(jnp.int32, (SGU_CHUNK, SGU_CHUNK), 1)
    return r >= c


def _tail_sgu(att, zrest, x, target, w_pa, w_pb, w_out, b_gate, final_g, ln_g, ln_b, w_s, b_s_t, tm=256):
    S = x.shape[0]
    nt = S // tm
    chunks = tm // SGU_CHUNK

    def body(att_ref, ga_ref, ub_ref, vb_ref, gb_ref, gta_ref, gtb_ref, x_ref, t_ref,
             wpa_ref, wpb_ref, wout_ref, bg_ref, fg_ref, lg_ref, lb_ref, ws_ref, bst_ref,
             dout_ref, datt_ref, dzt_ref, dzs_ref, gwout_hbm, gwpa_hbm, gwpb_hbm,
             gbg_ref, gfg_ref, loss_ref, gws_ref, gbs_ref, glg_ref, glb_ref,
             acc_out, acc_pa, acc_pb, sg_scr, mix_scr, dvn_scr, bs_acc, sems):
        i = pl.program_id(0)

        @pl.when(i == 0)
        def _():
            for r in (acc_out, acc_pa, acc_pb, gbg_ref, gfg_ref, loss_ref, gws_ref, glg_ref, glb_ref, bs_acc):
                r[...] = jnp.zeros_like(r)

        u, du, dv, rstd, xh, vn = _sgu_core(ub_ref[...], vb_ref[...], lg_ref[...], lb_ref[...])
        vnb = vn.astype(BF)
        tri = _tri()
        blocks = [(g, slice(n * SGU_CHUNK, (n + 1) * SGU_CHUNK), slice(g * 128, (g + 1) * 128))
                  for g in range(N_GROUPS) for n in range(chunks)]
        wts = [jnp.where(tri, ws_ref[g], 0.0) for g in range(N_GROUPS)]
        for g, rs, cs in blocks:
            mixed = _dot(wts[g].astype(BF), vnb[rs, cs]) + bst_ref[:, g:g + 1]
            mix_scr[rs, cs] = mixed
            sg_scr[rs, cs] = u[rs, cs] * mixed

        att = att_ref[...]
        sg = sg_scr[...]
        sa, dsa = _silu_and_grad(ga_ref[...])
        sb, dsb = _silu_and_grad(gb_ref[...])
        ya = (att * sa).astype(BF)
        yb = (sg * sb).astype(BF)
        pa = _dot(ya, wpa_ref[...])
        pb = _dot(yb, wpb_ref[...])
        ga = _sigmoid(gta_ref[...] + bg_ref[:, 0:D_MODEL])
        gb = _sigmoid(gtb_ref[...] + bg_ref[:, D_MODEL:2 * D_MODEL])
        merged = (ga * pa + gb * pb).astype(BF)
        out = x_ref[...] + _dot(merged, wout_ref[...])
        r2 = lax.rsqrt(jnp.mean(out * out, axis=-1, keepdims=True) + EPS)
        nrm = out * r2
        fg = fg_ref[...]
        err = nrm * fg - t_ref[...]
        loss_ref[...] += 0.5 * jnp.sum(jnp.mean(err * err, axis=-1, keepdims=True))
        dy = err * (1.0 / D_MODEL)
        gfg_ref[...] += jnp.sum(dy * nrm, axis=0, keepdims=True)
        dn = dy * fg
        d_out = r2 * (dn - nrm * jnp.mean(dn * nrm, axis=-1, keepdims=True))
        dout_ref[...] = d_out
        d_outb = d_out.astype(BF)
        acc_out[...] += _dot_tn(merged, d_outb)
        dm = _dot_nt(d_outb, wout_ref[...])
        d_pa = (dm * ga).astype(BF)
        d_pb = (dm * gb).astype(BF)
        d_gta = dm * pa * (ga * (1.0 - ga))
        d_gtb = dm * pb * (gb * (1.0 - gb))
        gbg_ref[:, 0:D_MODEL] += jnp.sum(d_gta, axis=0, keepdims=True)
        gbg_ref[:, D_MODEL:2 * D_MODEL] += jnp.sum(d_gtb, axis=0, keepdims=True)
        dzt_ref[:, 2 * D_A:2 * D_A + D_MODEL] = d_gta.astype(BF)
        dzt_ref[:, 2 * D_A + D_MODEL:] = d_gtb.astype(BF)
        acc_pa[...] += _dot_tn(ya, d_pa)
        acc_pb[...] += _dot_tn(yb, d_pb)
        d_ya = _dot_nt(d_pa, wpa_ref[...])
        d_yb = _dot_nt(d_pb, wpb_ref[...])
        d_att = (d_ya * sa).astype(BF)
        for hd in range(N_HEADS):
            datt_ref[hd] = d_att[:, hd * HEAD_DIM:(hd + 1) * HEAD_DIM]
        dzt_ref[:, 0:D_A] = (d_ya * att * dsa).astype(BF)
        dzt_ref[:, D_A:2 * D_A] = (d_yb * sg * dsb).astype(BF)

        dsg = d_yb * sb
        dzs_ref[:, 0:D_B] = (dsg * mix_scr[...] * du).astype(BF)
        dmix = dsg * u
        for g, rs, cs in blocks:
            dmb = dmix[rs, cs].astype(BF)
            bs_acc[:, cs] += dmix[rs, cs]
            gws_ref[g] += _dot_nt(dmb, vnb[rs, cs])
            dvn_scr[rs, cs] = _dot(wts[g].T.astype(BF), dmb)
        dvn = dvn_scr[...]
        glg_ref[...] += jnp.sum(dvn * xh, axis=0, keepdims=True)
        glb_ref[...] += jnp.sum(dvn, axis=0, keepdims=True)
        dxh = dvn * lg_ref[...]
        dvv = rstd * (dxh - jnp.mean(dxh, axis=-1, keepdims=True)
                      - xh * jnp.mean(dxh * xh, axis=-1, keepdims=True))
        dzs_ref[:, D_B:2 * D_B] = (dvv * dv).astype(BF)

        @pl.when(i == nt - 1)
        def _():
            cps = [pltpu.make_async_copy(acc_out, gwout_hbm, sems.at[0]),
                   pltpu.make_async_copy(acc_pa, gwpa_hbm, sems.at[1]),
                   pltpu.make_async_copy(acc_pb, gwpb_hbm, sems.at[2])]
            for cp in cps:
                cp.start()
            lane = lax.broadcasted_iota(jnp.int32, (SGU_CHUNK, 128), 1)
            cols = jnp.zeros((SGU_CHUNK, 128), F32)
            for g in range(N_GROUPS):
                gws_ref[g] = jnp.where(tri, gws_ref[g], 0.0)
                col = jnp.sum(bs_acc[:, g * 128:(g + 1) * 128], axis=-1, keepdims=True)
                cols = jnp.where(lane == g, col, cols)
            gbs_ref[...] = cols
            for cp in cps:
                cp.wait()

    c2 = lambda i: (0, 0)
    c3 = lambda i: (0, 0, 0)
    zcol = lambda w, blk: pl.BlockSpec((tm, w), lambda i: (i, blk))
    row = lambda w: pl.BlockSpec((tm, w), lambda i: (i, 0))
    return pl.pallas_call(
        body, name="tail", grid=(nt,),
        out_shape=(jax.ShapeDtypeStruct((S, D_MODEL), F32), jax.ShapeDtypeStruct((N_HEADS, S, HEAD_DIM), BF),
                   jax.ShapeDtypeStruct((S, 3072), BF), jax.ShapeDtypeStruct((S, 2 * D_B), BF),
                   jax.ShapeDtypeStruct((D_MODEL, D_MODEL), F32), jax.ShapeDtypeStruct((D_A, D_MODEL), F32),
                   jax.ShapeDtypeStruct((D_B, D_MODEL), F32),
                   jax.ShapeDtypeStruct((1, 2 * D_MODEL), F32), jax.ShapeDtypeStruct((1, D_MODEL), F32),
                   jax.ShapeDtypeStruct((1, 128), F32),
                   jax.ShapeDtypeStruct((N_GROUPS, 128, 128), F32), jax.ShapeDtypeStruct((SGU_CHUNK, 128), F32),
                   jax.ShapeDtypeStruct((1, D_B), F32), jax.ShapeDtypeStruct((1, D_B), F32)),
        in_specs=[row(D_A), zcol(512, 0), zcol(512, 1), zcol(512, 2), zcol(512, 3),
                  zcol(D_MODEL, 2), zcol(D_MODEL, 3), row(D_MODEL), row(D_MODEL),
                  pl.BlockSpec((D_A, D_MODEL), c2), pl.BlockSpec((D_B, D_MODEL), c2),
                  pl.BlockSpec((D_MODEL, D_MODEL), c2),
                  pl.BlockSpec((1, 2 * D_MODEL), c2), pl.BlockSpec((1, D_MODEL), c2),
                  pl.BlockSpec((1, D_B), c2), pl.BlockSpec((1, D_B), c2),
                  pl.BlockSpec((N_GROUPS, 128, 128), c3), pl.BlockSpec((128, N_GROUPS), c2)],
        out_specs=[row(D_MODEL), pl.BlockSpec((N_HEADS, tm, HEAD_DIM), lambda i: (0, i, 0)),
                   row(3072), row(2 * D_B), _ANY, _ANY, _ANY,
                   pl.BlockSpec((1, 2 * D_MODEL), c2), pl.BlockSpec((1, D_MODEL), c2),
                   pl.BlockSpec((1, 128), c2),
                   pl.BlockSpec((N_GROUPS, 128, 128), c3), pl.BlockSpec((SGU_CHUNK, 128), c2),
                   pl.BlockSpec((1, D_B), c2), pl.BlockSpec((1, D_B), c2)],
        scratch_shapes=[pltpu.VMEM((D_MODEL, D_MODEL), F32), pltpu.VMEM((D_A, D_MODEL), F32),
                        pltpu.VMEM((D_B, D_MODEL), F32),
                        pltpu.VMEM((tm, D_B), F32), pltpu.VMEM((tm, D_B), F32), pltpu.VMEM((tm, D_B), F32),
                        pltpu.VMEM((SGU_CHUNK, D_B), F32), pltpu.SemaphoreType.DMA((3,))],
        compiler_params=_params(58, dimension_semantics=("arbitrary",)),
    )(att, zrest, zrest, zrest, zrest, zrest, zrest, x, target, w_pa, w_pb, w_out, b_gate, final_g,
      ln_g, ln_b, w_s, b_s_t)


_DZ_MAP = ((0, 0), (1, 0), (2, 0), (3, 0), (4, 0), (4, 1), (3, 1), (3, 2), (3, 3), (3, 4), (3, 5))


def _dh_gradx(dq, dk, dv, dzt, dzs, w_in_bf, x, norm_g, d_out, tm=512, after=()):
    S = x.shape[0]

    def body(dq_ref, dk_ref, dv_ref, dzt_ref, dzs_ref, w_ref, x_ref, g_ref, dout_ref, gx_ref, gn_ref):
        i = pl.program_id(0)

        @pl.when(i == 0)
        def _():
            gn_ref[...] = jnp.zeros_like(gn_ref)

        pieces = (dq_ref, dk_ref, dv_ref, dzt_ref, dzs_ref)
        dh = jnp.zeros((tm, D_MODEL), F32)
        for j, (pc, blk) in enumerate(_DZ_MAP):
            dh += _dot_nt(pieces[pc][:, blk * 512:(blk + 1) * 512], w_ref[:, j * 512:(j + 1) * 512])
        xv = x_ref[...]
        r = lax.rsqrt(jnp.mean(xv * xv, axis=-1, keepdims=True) + EPS)
        nrm = xv * r
        gn_ref[...] += jnp.sum(dh * nrm, axis=0, keepdims=True)
        dn = dh * g_ref[...]
        gx_ref[...] = r * (dn - nrm * jnp.mean(dn * nrm, axis=-1, keepdims=True)) + dout_ref[...]

    row = lambda w: pl.BlockSpec((tm, w), lambda i: (i, 0))
    c2 = lambda i: (0, 0)
    return pl.pallas_call(
        _after(body, 9, after), name="dh_gradx", grid=(S // tm,),
        out_shape=(jax.ShapeDtypeStruct((S, D_MODEL), F32), jax.ShapeDtypeStruct((1, D_MODEL), F32)),
        in_specs=[row(512), row(512), row(512), row(3072), row(1024),
                  pl.BlockSpec((D_MODEL, D_IN), c2, pipeline_mode=pl.Buffered(1)), row(D_MODEL),
                  pl.BlockSpec((1, D_MODEL), c2), row(D_MODEL)]
        + [_ANY] * len(after),
        out_specs=[row(D_MODEL), pl.BlockSpec((1, D_MODEL), c2)],
        compiler_params=_params(48, dimension_semantics=("arbitrary",)),
    )(dq, dk, dv, dzt, dzs, w_in_bf, x, norm_g, d_out, *after)


def _gw_in(ht, dq, dk, dv, dzt, dzs, tn=512, after=()):
    S = ht.shape[1]
    per = 512 // tn
    cols = tuple((pc, per * blk + h) for pc, blk in _DZ_MAP for h in range(per))

    def body(ht_ref, dq_ref, dk_ref, dv_ref, dzt_ref, dzs_ref, o_ref, ob_ref):
        j = pl.program_id(0)
        pieces = (dq_ref, dk_ref, dv_ref, dzt_ref, dzs_ref)
        for pc in range(5):
            hit = functools.reduce(jnp.logical_or, [j == jj for jj, (p, _) in enumerate(cols) if p == pc])

            @pl.when(hit)
            def _(pc=pc):
                g = _dot(ht_ref[...], pieces[pc][...])
                o_ref[...] = g
                ob_ref[...] = g.astype(BF)

    def piece_spec(pc):
        cur = next(blk for p, blk in cols if p == pc)
        held = []
        for p, blk in cols:
            cur = blk if p == pc else cur
            held.append(cur)

        def index_map(j):
            blk = jnp.int32(held[0])
            for jj in range(1, len(held)):
                if held[jj] != held[jj - 1]:
                    blk = jnp.where(j >= jj, jnp.int32(held[jj]), blk)
            return (0, blk)

        return pl.BlockSpec((S, tn), index_map)

    return pl.pallas_call(
        _after(body, 6, after), name="gw_in", grid=(len(cols),),
        out_shape=(jax.ShapeDtypeStruct((D_MODEL, D_IN), F32), jax.ShapeDtypeStruct((D_MODEL, D_IN), BF)),
        in_specs=[pl.BlockSpec((D_MODEL, S), lambda j: (0, 0), pipeline_mode=pl.Buffered(1))]
        + [piece_spec(pc) for pc in range(5)]
        + [_ANY] * len(after),
        out_specs=[pl.BlockSpec((D_MODEL, tn), lambda j: (0, j)), pl.BlockSpec((D_MODEL, tn), lambda j: (0, j))],
        compiler_params=_params(56, dimension_semantics=("arbitrary",)),
    )(ht, dq, dk, dv, dzt, dzs, *after)


_HBM = pl.BlockSpec(memory_space=pltpu.HBM)
_SEM = pl.BlockSpec(memory_space=pltpu.SEMAPHORE)
_ANY = pl.BlockSpec(memory_space=pl.ANY)
_EFFECT = pltpu.SideEffectType.DATAFLOW_SIDE_EFFECTING


def _in_hbm(a):
    return pltpu.with_memory_space_constraint(a, pltpu.HBM)


def _after(body, n_in, after):
    if not after:
        return body
    return lambda *refs: body(*refs[:n_in], *refs[n_in + len(after):])


class _Started:
    def __init__(self, send, recv, bufs, token):
        self.send, self.recv, self.bufs, self.token = send, recv, bufs, token


_PEER_SETS = {"sibling": 7, "chips": 8, "both": 9}


def _peers(kind):
    x, y, c, chips = _mesh_pos()
    return ([(x, y, 1 - c)] if kind in ("sibling", "both") else []) + (
        [(cx, cy, c) for cx, cy in chips] if kind in ("chips", "both") else [])


def _signal_peers(kind):
    barrier = pltpu.get_barrier_semaphore()
    targets = _peers(kind)
    for peer in targets:
        pl.semaphore_signal(barrier, inc=1, device_id=peer, device_id_type=MESH)
    return lambda: pl.semaphore_wait(barrier, len(targets))


def _split_start(name, bufs, n_copies, copies, peers, after=()):
    nb = len(bufs)

    def body(*refs):
        _signal_peers(peers)()
        refs = refs[:nb] + refs[nb + len(after):]
        for cp in copies(refs[:nb], refs[nb], refs[nb + 1]):
            cp.start()
        refs[-1][...] = jnp.zeros_like(refs[-1])

    outs = pl.pallas_call(
        body, name=name,
        out_shape=(pltpu.SemaphoreType.DMA((n_copies,)), pltpu.SemaphoreType.DMA((n_copies,)),
                   *[pltpu.HBM(b.shape, b.dtype) for b in bufs], jax.ShapeDtypeStruct((8, 128), F32)),
        in_specs=[_HBM] * nb + [_ANY] * len(after),
        out_specs=(_SEM, _SEM, *[_HBM] * nb, pl.BlockSpec(memory_space=pltpu.VMEM)),
        input_output_aliases={k: 2 + k for k in range(nb)},
        compiler_params=_params(1, has_side_effects=_EFFECT, collective_id=_PEER_SETS[peers]),
    )(*[_in_hbm(b) for b in bufs], *after)
    return _Started(outs[0], outs[1], list(outs[2:2 + nb]), outs[-1])


def _split_wait(name, started, copies, after):
    nb = len(started.bufs)
    after = tuple(after) if isinstance(after, (tuple, list)) else (after,)

    def body(*refs):
        for cp in copies(refs[:nb], refs[nb], refs[nb + 1]):
            cp.wait_send()
            cp.wait_recv()

    return list(pl.pallas_call(
        body, name=name,
        out_shape=tuple(pltpu.HBM(b.shape, b.dtype) for b in started.bufs),
        in_specs=[_HBM] * nb + [_SEM, _SEM] + [_ANY] * len(after),
        out_specs=tuple([_HBM] * nb),
        input_output_aliases={k: k for k in range(nb)},
        compiler_params=_params(1, has_side_effects=_EFFECT),
    )(*started.bufs, started.send, started.recv, *after))


def _x1_copies(ws):
    def copies(refs, send_sems, recv_sems):
        x, y, c, _ = _mesh_pos()
        out = []
        for k, w in enumerate(ws):
            for s in range(N_SHARD):
                out.append(pltpu.make_async_remote_copy(
                    src_ref=_UNITS[w](refs[k], s, 1 - c), dst_ref=refs[len(ws) + k].at[s],
                    send_sem=send_sems.at[N_SHARD * k + s], recv_sem=recv_sems.at[N_SHARD * k + s],
                    device_id=(x, y, 1 - c), device_id_type=MESH))
        return out
    return copies


def _x2_copies(n):
    def copies(refs, send_sems, recv_sems):
        x, y, c, chips = _mesh_pos()
        out = []
        for j, (cx, cy) in enumerate(chips):
            for k in range(n):
                out.append(pltpu.make_async_remote_copy(
                    src_ref=refs[k].at[2 * cx + cy], dst_ref=refs[n + k].at[j],
                    send_sem=send_sems.at[3 * k + j], recv_sem=recv_sems.at[3 * k + j],
                    device_id=(cx, cy, c), device_id_type=MESH))
        return out
    return copies


def _x3_copies(ws):
    def copies(refs, send_sems, recv_sems):
        x, y, c, _ = _mesh_pos()
        out = []
        for k, w in enumerate(ws):
            rows = _HALF_ROWS[w]
            mine = refs[k].at[pl.ds(_mo(c * rows, rows), rows), :]
            out.append(pltpu.make_async_remote_copy(
                src_ref=mine, dst_ref=mine, send_sem=send_sems.at[k], recv_sem=recv_sems.at[k],
                device_id=(x, y, 1 - c), device_id_type=MESH))
        return out
    return copies


def _x1_lands(ws, dtype=F32):
    return [lax.empty((N_SHARD,) + _UNIT_SHAPES[w], dtype) for w in ws]


def _x2_lands(ws):
    return [lax.empty((3,) + _UNIT_SHAPES[w], BF) for w in ws]


def _grad_add1(w, g, recv, pos):
    ur, uc = _UNIT_SHAPES[w]

    def body(pos_ref, g_ref, r_ref, csb_ref):
        csb_ref[0] = (g_ref[...] + r_ref[0].astype(F32)).astype(BF)

    u3 = lambda k, pos: (pos[2 + k], 0, 0)
    return pl.pallas_call(
        body, name=f"grad_add1_{w}",
        grid_spec=pltpu.PrefetchScalarGridSpec(
            num_scalar_prefetch=1, grid=(N_SHARD - 1,),
            in_specs=[pl.BlockSpec((ur, uc), lambda k, pos: (pos[0], pos[2 + k])), pl.BlockSpec((1, ur, uc), u3)],
            out_specs=pl.BlockSpec((1, ur, uc), u3)),
        out_shape=jax.ShapeDtypeStruct((N_SHARD, ur, uc), BF),
        compiler_params=_params(40, dimension_semantics=("arbitrary",)),
    )(pos, g, recv)


def _grad_add1_group(ws, gs, recvs, pos):
    n = len(ws)

    def body(pos_ref, *refs):
        s = pl.program_id(0)
        for k in range(n):
            g, r, own, csb = refs[k], refs[n + k], refs[2 * n + k], refs[3 * n + k]
            v = g[...] + r[0]
            csb[0] = v.astype(BF)

            @pl.when(s == pos_ref[1])
            def _(own=own, v=v):
                own[...] = v

    def g_spec(w):
        if w == 3:
            return pl.BlockSpec(_UNIT_SHAPES[w], lambda s, pos: (2 * s + pos[0], 0))
        return pl.BlockSpec(_UNIT_SHAPES[w], lambda s, pos: (pos[0], s))

    slot = lambda w: pl.BlockSpec((1,) + _UNIT_SHAPES[w], lambda s, pos: (s, 0, 0))
    outs = pl.pallas_call(
        body, name="grad_add1_group",
        grid_spec=pltpu.PrefetchScalarGridSpec(
            num_scalar_prefetch=1, grid=(N_SHARD,),
            in_specs=[g_spec(w) for w in ws] + [slot(w) for w in ws],
            out_specs=[pl.BlockSpec(_UNIT_SHAPES[w], lambda s, pos: (0, 0)) for w in ws] + [slot(w) for w in ws]),
        out_shape=tuple(jax.ShapeDtypeStruct(_UNIT_SHAPES[w], F32) for w in ws)
        + tuple(jax.ShapeDtypeStruct((N_SHARD,) + _UNIT_SHAPES[w], BF) for w in ws),
        compiler_params=_params(32, dimension_semantics=("arbitrary",)),
    )(pos, *gs, *recvs)
    return list(outs[:n]), list(outs[n:])


def _grad_add2_group(ws, owns, recvs):
    n = len(ws)

    def body(*refs):
        c = lax.axis_index("c")
        for k, w in enumerate(ws):
            own, r, o = refs[k], refs[n + k], refs[2 * n + k]
            rows = _HALF_ROWS[w]
            total = ((own[...] + r[0].astype(F32)) + r[1].astype(F32)) + r[2].astype(F32)
            o[pl.ds(_mo(c * rows, rows), rows), :] = total

    vm = pl.BlockSpec(memory_space=pltpu.VMEM)
    return list(pl.pallas_call(
        body, name="grad_add2_group",
        out_shape=tuple(jax.ShapeDtypeStruct(_SHARD_SHAPES[w], F32) for w in ws),
        in_specs=[vm] * (2 * n), out_specs=[vm] * n,
        compiler_params=_params(32),
    )(*owns, *recvs))


def _grad_add2(w, g, recv1, recv2, pos):
    ur, uc = _UNIT_SHAPES[w]
    nt = 4
    tr = ur // nt

    def body(pos_ref, g_ref, r1_ref, r2_ref, o_ref):
        own = g_ref[...] + r1_ref[0].astype(F32)
        o_ref[...] = ((own + r2_ref[0].astype(F32)) + r2_ref[1].astype(F32)) + r2_ref[2].astype(F32)

    mine = lambda t, pos: (pos[0] * nt + t, 0)
    return pl.pallas_call(
        body, name=f"grad_add2_{w}",
        grid_spec=pltpu.PrefetchScalarGridSpec(
            num_scalar_prefetch=1, grid=(nt,),
            in_specs=[pl.BlockSpec((tr, uc), lambda t, pos: (pos[0] * nt + t, pos[1])),
                      pl.BlockSpec((1, tr, uc), lambda t, pos: (pos[1], t, 0)),
                      pl.BlockSpec((3, tr, uc), lambda t, pos: (0, t, 0))],
            out_specs=pl.BlockSpec((tr, uc), mine)),
        out_shape=jax.ShapeDtypeStruct(_SHARD_SHAPES[w], F32),
        compiler_params=_params(32, dimension_semantics=("arbitrary",)),
    )(pos, g, recv1, recv2)


def _adamw_math(w, g, m, v):
    m = ADAM_B1 * m + (1.0 - ADAM_B1) * g
    v = ADAM_B2 * v + (1.0 - ADAM_B2) * (g * g)
    m_hat = m / ADAM_C1
    v_hat = v / ADAM_C2
    delta = -ADAM_LR * (m_hat / (jnp.sqrt(v_hat) + ADAM_EPS) + ADAM_WD * w)
    return delta, m, v


ADAMW_STEPS = 4


def _adamw(ws_, gs, ms, vs):
    n = len(ws_)

    def body(*refs):
        for k in range(n):
            w, g, m, v = (refs[j * n + k] for j in range(4))
            d, nm, nv, gc = (refs[(4 + j) * n + k] for j in range(4))
            gv = g[...]
            d[...], nm[...], nv[...] = _adamw_math(w[...], gv, m[...], v[...])
            gc[...] = gv

    specs = [pl.BlockSpec((a.shape[0] // ADAMW_STEPS, a.shape[1]), lambda i: (i, 0)) for a in ws_] * 4
    outs = pl.pallas_call(
        body, name="adamw", grid=(ADAMW_STEPS,),
        out_shape=tuple(jax.ShapeDtypeStruct(a.shape, F32) for _ in range(4) for a in ws_),
        in_specs=specs, out_specs=specs,
        compiler_params=_params(40, dimension_semantics=("arbitrary",)),
    )(*ws_, *gs, *ms, *vs)
    return [tuple(outs[j * n + k] for j in range(4)) for k in range(n)]


_REL_PAD = 384
_VEC_FIELDS = (("norm_g", 0, D_MODEL), ("b_gate", 1024, 2 * D_MODEL), ("sgu_ln_g", 3072, D_B),
               ("sgu_ln_b", 3584, D_B), ("b_s", 4096, N_GROUPS * 128), ("final_g", 4608, D_MODEL))
_LOSS_OFF = 5632
_REL_OFF = 5760
_NV = _REL_OFF + N_HEADS * _REL_PAD
_N_FIELDS = len(_VEC_FIELDS) + 2


_B_S_FIELD = [f[0] for f in _VEC_FIELDS].index("b_s")


def _assemble_row(dst, fields, transposed_b_s):
    for f, (_, off, n) in enumerate(_VEC_FIELDS):
        if transposed_b_s and f == _B_S_FIELD:
            t = fields[f][...].T
            for g in range(N_GROUPS):
                dst[:, off + 128 * g:off + 128 * (g + 1)] = t[g:g + 1, :]
        else:
            dst[:, off:off + n] = fields[f][...]
    for r in range(N_HEADS):
        dst[:, _REL_OFF + _REL_PAD * r:_REL_OFF + _REL_PAD * (r + 1)] = fields[len(_VEC_FIELDS)][r:r + 1, :]


def _small_reduce(grads, loss_row, after=()):
    n_in = _N_FIELDS + 1

    def body(*refs):
        g_refs, loss_ref = refs[:_N_FIELDS], refs[_N_FIELDS]
        out_v, out_w = refs[n_in:n_in + 2]
        mine_v, mine_w, gath_v, gath_w, send_sems, recv_sems = refs[n_in + 2:]
        x, y, c, chips = _mesh_pos()
        me, sibling = (x, y, c), (x, y, 1 - c)

        peers_entered = _signal_peers("both")
        _assemble_row(mine_v, g_refs, True)
        mine_v[:, _LOSS_OFF:_LOSS_OFF + 128] = loss_ref[...]
        mine_w[...] = g_refs[-1][...].astype(BF)
        peers_entered()
        my_k = 4 * x + 2 * y + c
        gath_v[my_k] = mine_v[...]
        gath_w[my_k] = mine_w[...]

        def copy(k, gath, block, to, src=None):
            dst = gath.at[4 * block[0] + 2 * block[1] + block[2]]
            return pltpu.make_async_remote_copy(
                src_ref=dst if src is None else src, dst_ref=dst,
                send_sem=send_sems.at[k], recv_sem=recv_sems.at[k], device_id=to, device_id_type=MESH)

        bufs = ((gath_v, mine_v), (gath_w, mine_w))
        first, passed = [], []
        for b, (gath, mine) in enumerate(bufs):
            first.append(copy(7 * b, gath, me, sibling, src=mine))
            first += [copy(7 * b + 1 + j, gath, me, (*chip, c), src=mine) for j, chip in enumerate(chips)]
        for cp in first:
            cp.start()
        for b, (gath, _) in enumerate(bufs):
            for j, chip in enumerate(chips):
                copy(7 * b + 1 + j, gath, (*chip, c), me).wait_recv()
                cp = copy(7 * b + 4 + j, gath, (*chip, c), sibling)
                cp.start()
                passed.append(cp)
        for b, (gath, _) in enumerate(bufs):
            copy(7 * b, gath, sibling, me).wait_recv()
            for j, chip in enumerate(chips):
                copy(7 * b + 4 + j, gath, (*chip, 1 - c), me).wait_recv()
        for cp in first + passed:
            cp.wait_send()

        tot_v, tot_w = gath_v[0], gath_w[0].astype(F32)
        for k in range(1, 8):
            tot_v = tot_v + gath_v[k]
            tot_w = tot_w + gath_w[k].astype(F32)
        out_v[...] = tot_v
        out_w[...] = tot_w

    vm = pl.BlockSpec(memory_space=pltpu.VMEM)
    return pl.pallas_call(
        _after(body, n_in, after), name="small_reduce",
        out_shape=(jax.ShapeDtypeStruct((1, _NV), F32), jax.ShapeDtypeStruct((N_GROUPS * 128, 128), F32)),
        in_specs=[vm] * n_in + [_ANY] * len(after), out_specs=[vm] * 2,
        scratch_shapes=[pltpu.VMEM((1, _NV), F32), pltpu.VMEM((N_GROUPS * 128, 128), BF),
                        pltpu.VMEM((8, 1, _NV), F32), pltpu.VMEM((8, N_GROUPS * 128, 128), BF),
                        pltpu.SemaphoreType.DMA((14,)), pltpu.SemaphoreType.DMA((14,))],
        compiler_params=_params(32, collective_id=_PEER_SETS["both"]),
    )(*grads, loss_row, *after)


def _small_adamw(tot_v, tot_w, params):
    n_in = 2 + 3 * _N_FIELDS

    def body(*refs):
        tv_ref, tw_ref = refs[:2]
        p_refs = [refs[2 + k * _N_FIELDS:2 + (k + 1) * _N_FIELDS] for k in range(3)]
        outs = refs[n_in:n_in + 4 * _N_FIELDS + 1]
        wmv = refs[-1]
        for k in range(3):
            _assemble_row(wmv.at[k], p_refs[k], False)
            wmv[k, :, _LOSS_OFF:_LOSS_OFF + 128] = jnp.zeros((1, 128), F32)
        tot_v, tot_w = tv_ref[...], tw_ref[...]
        res_v = (tot_v,) + _adamw_math(wmv[0], tot_v, wmv[1], wmv[2])
        res_w = (tot_w,) + _adamw_math(p_refs[0][-1][...], tot_w, p_refs[1][-1][...], p_refs[2][-1][...])
        for kind in range(4):
            o = outs[kind * _N_FIELDS:(kind + 1) * _N_FIELDS]
            for f, (_, off, n) in enumerate(_VEC_FIELDS):
                o[f][...] = res_v[kind][:, off:off + n]
            for r in range(N_HEADS):
                o[len(_VEC_FIELDS)][r:r + 1, :] = res_v[kind][:, _REL_OFF + _REL_PAD * r:_REL_OFF + _REL_PAD * (r + 1)]
            o[-1][...] = res_w[kind]
        outs[-1][...] = tot_v[:, _LOSS_OFF:_LOSS_OFF + 128]

    field_shapes = [(1, n) for _, _, n in _VEC_FIELDS] + [(N_HEADS, _REL_PAD), (N_GROUPS * 128, 128)]
    vm = pl.BlockSpec(memory_space=pltpu.VMEM)
    operands = [tot_v, tot_w] + [a for p in params for a in p]
    assert len(operands) == n_in
    outs = pl.pallas_call(
        body, name="small_adamw",
        out_shape=tuple(jax.ShapeDtypeStruct(s, F32) for _ in range(4) for s in field_shapes)
        + (jax.ShapeDtypeStruct((1, 128), F32),),
        in_specs=[vm] * n_in, out_specs=[vm] * (4 * _N_FIELDS + 1),
        scratch_shapes=[pltpu.VMEM((3, 1, _NV), F32)],
        compiler_params=_params(32),
    )(*operands)
    return [outs[k * _N_FIELDS:(k + 1) * _N_FIELDS] for k in range(4)], outs[-1]


def _small_fields(norm_g, b_gate, ln_g, ln_b, b_s, final_g, rel_bias, w_s):
    rel = jnp.pad(rel_bias.reshape(N_HEADS, N_REL), ((0, 0), (0, _REL_PAD - N_REL)))
    return (norm_g, b_gate, ln_g, ln_b, b_s.reshape(1, N_GROUPS * 128), final_g.reshape(1, D_MODEL),
            rel, w_s.reshape(N_GROUPS * 128, 128))


def _small_outputs(fields):
    n_g, b_g, l_g, l_b, b_s, f_g, rel, w_s = fields
    return (n_g, b_g, rel[:, :N_REL].reshape(1, N_HEADS, N_REL), l_g, l_b,
            w_s.reshape(1, N_GROUPS, 128, 128), b_s.reshape(1, N_GROUPS, 128), f_g.reshape(D_MODEL))


def kernel(x, norm_g, w_in, b_gate, rel_bias, sgu_ln_g, sgu_ln_b, w_s, b_s, w_pa, w_pb, w_out, final_g, loss_target, m_norm_g, m_w_in, m_b_gate, m_rel_bias, m_sgu_ln_g, m_sgu_ln_b, m_w_s, m_b_s, m_w_pa, m_w_pb, m_w_out, m_final_g, v_norm_g, v_w_in, v_b_gate, v_rel_bias, v_sgu_ln_g, v_sgu_ln_b, v_w_s, v_b_s, v_w_pa, v_w_pb, v_w_out, v_final_g):
    S = x.shape[1]
    xs = x.reshape(S, D_MODEL)
    tgt = loss_target.reshape(S, D_MODEL)
    big_w = (w_in[0], w_pa[0], w_pb[0], w_out[0])
    big_m = (m_w_in[0], m_w_pa[0], m_w_pb[0], m_w_out[0])
    big_v = (v_w_in[0], v_w_pa[0], v_w_pb[0], v_w_out[0])
    rel = rel_bias[0]
    ws = w_s[0]
    bst = b_s[0].T
    fg = final_g.reshape(1, D_MODEL)
    chip = 2 * lax.axis_index("x") + lax.axis_index("y")
    pos = jnp.stack([lax.axis_index("c"), chip] + [(chip + k) % N_SHARD for k in range(1, N_SHARD)]).astype(jnp.int32)

    (w_in_bf,), staged, band_bias = _ag_weights((0,), big_w[:1], (1, 2, 3), big_w[1:], rel)
    ag_s = _split_start("ag_small_start", staged, 9, _gather_copies((1, 2, 3)), "chips", after=(w_in_bf,))

    ht, q3, k3, v3, zrest = _inproj_fwd(xs, norm_g, w_in_bf, after=(ag_s.token,))
    att, lse = _attn_fwd(q3, k3, v3, band_bias)
    w_pa_bf, w_pb_bf, w_out_bf = _split_wait("ag_small_wait", ag_s, _gather_copies((1, 2, 3)), att)
    (d_out, d_att, dzt, dzs, gw_out, gw_pa, gw_pb, g_bgate, g_final, loss_row,
     g_ws, g_bs_t, g_lng, g_lnb) = _tail_sgu(
        att, zrest, xs, tgt, w_pa_bf, w_pb_bf, w_out_bf, b_gate, fg, sgu_ln_g, sgu_ln_b, ws, bst)
    ws_s, ws_i = (1, 2, 3), (0,)

    x1s = _split_start("gx1s_start", [gw_pa, gw_pb, gw_out] + _x1_lands(ws_s), 12, _x1_copies(ws_s), "sibling")
    dq, dk, dv, d_gp = _attn_bwd(q3, k3, v3, d_att, lse, band_bias, after=(x1s.token,))
    got = _split_wait("gx1s_wait", x1s, _x1_copies(ws_s), dq)
    own_s, csb_s = _grad_add1_group(ws_s, got[:3], got[3:], pos)

    x2s = _split_start("gx2s_start", csb_s + _x2_lands(ws_s), 9, _x2_copies(3), "chips")
    gw_in, gw_in_bf = _gw_in(ht, dq, dk, dv, dzt, dzs, after=(x2s.token,))
    x1i = _split_start("gx1i_start", [gw_in_bf] + _x1_lands(ws_i, BF), 4, _x1_copies(ws_i), "sibling")
    got = _split_wait("gx2s_wait", x2s, _x2_copies(3), x1i.token)
    halves_s = _grad_add2_group(ws_s, own_s, got[3:])
    x3s = _split_start("gx3s_start", halves_s, 3, _x3_copies(ws_s), "sibling")
    g_rel = jnp.pad(d_gp[:, 384:384 + N_REL][:, ::-1], ((0, 0), (0, _REL_PAD - N_REL)))
    small_params = (_small_fields(norm_g, b_gate, sgu_ln_g, sgu_ln_b, b_s, final_g, rel_bias, w_s),
                    _small_fields(m_norm_g, m_b_gate, m_sgu_ln_g, m_sgu_ln_b, m_b_s, m_final_g, m_rel_bias, m_w_s),
                    _small_fields(v_norm_g, v_b_gate, v_sgu_ln_g, v_sgu_ln_b, v_b_s, v_final_g, v_rel_bias, v_w_s))
    relayouts = (g_rel,) + tuple(fields[-2] for fields in small_params)
    recv1_i = _split_wait("gx1i_wait", x1i, _x1_copies(ws_i), (x3s.token,) + relayouts)[1]
    csb_i = _grad_add1(0, gw_in, recv1_i, pos)

    x2i = _split_start("gx2i_start", [csb_i] + _x2_lands(ws_i), 3, _x2_copies(1), "chips")
    grad_x, g_norm = _dh_gradx(dq, dk, dv, dzt, dzs, w_in_bf, xs, norm_g, d_out, after=(x2i.token,))
    g_shards_s = _split_wait("gx3s_wait", x3s, _x3_copies(ws_s), grad_x)
    got = _split_wait("gx2i_wait", x2i, _x2_copies(1), grad_x)
    half_i = _grad_add2(0, gw_in, recv1_i, got[1], pos)
    x3i = _split_start("gx3i_start", [half_i], 1, _x3_copies(ws_i), "sibling")

    small_grads = (g_norm, g_bgate, g_lng, g_lnb, g_bs_t, g_final, g_rel, g_ws.reshape(N_GROUPS * 128, 128))
    tot_v, tot_w = _small_reduce(small_grads, loss_row, after=(x3i.token,))
    (gsum, sdelta, sm, sv), loss_out = _small_adamw(tot_v, tot_w, small_params)

    g_shard_i, = _split_wait("gx3i_wait", x3i, _x3_copies(ws_i), loss_out)
    big = _adamw(big_w, [g_shard_i] + g_shards_s, big_m, big_v)
    sg_out, sd_out, sm_out, sv_out = (_small_outputs(f) for f in (gsum, sdelta, sm, sv))
    loss = loss_out[0, 0]

    def assemble(small, bigs):
        n_g, b_g, r_b, l_g, l_b, w_s_, b_s_, f_g = small
        b_in, b_pa, b_pb, b_out = (b[None] for b in bigs)
        return (n_g, b_in, b_g, r_b, l_g, l_b, w_s_, b_s_, b_pa, b_pb, b_out, f_g)

    grads_out = assemble(sg_out, [b[3] for b in big])
    delta_out = assemble(sd_out, [b[0] for b in big])
    m_out = assemble(sm_out, [b[1] for b in big])
    v_out = assemble(sv_out, [b[2] for b in big])
    return (loss, grad_x.reshape(1, S, D_MODEL), *grads_out, *delta_out, *m_out, *v_out)
```

```python
import functools
import math

import jax
import jax.numpy as jnp
from jax import lax
from jax.experimental import pallas as pl
from jax.experimental.pallas import tpu as pltpu

F32 = jnp.float32
BF = jnp.bfloat16
MESH = pl.DeviceIdType.MESH

D_MODEL = 1024
D_A = 512
D_B = 512
D_IN = 5632
N_HEADS = 8
HEAD_DIM = 64
CHUNK = 64
N_PREV = 8
SGU_CHUNK = 128
N_GROUPS = 4
N_REL = 257
EPS = 1e-6
NEG_INF = -1e30
SCALE = HEAD_DIM ** -0.5

QB = 2 * CHUNK
KB = (N_PREV + 2) * CHUNK
PADK = N_PREV * CHUNK
ROLL_W = 1024
KEEP = KB // QB - 1
Q_PER_STEP = 2

ADAM_LR = 0.001
ADAM_B1 = 0.9
ADAM_B2 = 0.999
ADAM_EPS = 1e-08
ADAM_WD = 0.01
ADAM_STEP = 10
ADAM_C1 = 1.0 - ADAM_B1 ** ADAM_STEP
ADAM_C2 = 1.0 - ADAM_B2 ** ADAM_STEP

AG_PIECES = 4
N_SHARD = 4
SHARD_IN = D_IN // N_SHARD
MIB = 1024 * 1024


V7X_VMEM_MIB = 64
VMEM_RESERVE_MIB = V7X_VMEM_MIB - 4


def _params(vmem_mib, **kw):
    assert vmem_mib <= VMEM_RESERVE_MIB
    return pltpu.CompilerParams(vmem_limit_bytes=VMEM_RESERVE_MIB * MIB, **kw)


def _sigmoid(x):
    return 1.0 / (1.0 + jnp.exp(-x))


def _silu_and_grad(x):
    s = _sigmoid(x)
    return x * s, s * (1.0 + x * (1.0 - s))


_GELU_C = math.sqrt(2.0 / math.pi)
_GELU_A = 0.044715


def _gelu_and_grad(x):
    x2 = x * x
    t = jnp.tanh(_GELU_C * (x + _GELU_A * (x2 * x)))
    cdf = 0.5 * (1.0 + t)
    grad = cdf + 0.5 * x * (1.0 - t * t) * (_GELU_C * (1.0 + 3.0 * _GELU_A * x2))
    return x * cdf, grad


def _dot(a, b):
    return jnp.dot(a, b, preferred_element_type=F32)


def _dot_nt(a, b):
    return lax.dot_general(a, b, (((1,), (1,)), ((), ())), preferred_element_type=F32)


def _dot_tn(a, b):
    return lax.dot_general(a, b, (((0,), (0,)), ((), ())), preferred_element_type=F32)


def _mo(v, m):
    return v if isinstance(v, int) else pl.multiple_of(v, m)


def _unit_in(ref, s, p):
    return ref.at[pl.ds(_mo(p * 512, 512), 512), pl.ds(_mo(s * SHARD_IN, 128), SHARD_IN)]


def _unit_p(ref, s, p):
    return ref.at[pl.ds(_mo(p * 256, 256), 256), pl.ds(_mo(s * 256, 128), 256)]


def _unit_out(ref, s, p):
    return ref.at[pl.ds(_mo(s * 256 + p * 128, 128), 128), :]


_UNITS = (_unit_in, _unit_p, _unit_p, _unit_out)
_HALF_ROWS = (512, 256, 256, 128)
_UNIT_SHAPES = ((512, SHARD_IN), (256, 256), (256, 256), (128, D_MODEL))
_FULL_SHAPES = ((D_MODEL, D_IN), (D_A, D_MODEL), (D_B, D_MODEL), (D_MODEL, D_MODEL))
_SHARD_SHAPES = ((D_MODEL, SHARD_IN), (D_A, 256), (D_B, 256), (256, D_MODEL))


def _mesh_pos():
    x, y, c = lax.axis_index("x"), lax.axis_index("y"), lax.axis_index("c")
    chips = [(1 - x, y), (x, 1 - y), (1 - x, 1 - y)]
    return x, y, c, chips


def _ag_weights(ws, shards, later_ws, later_shards, gp):
    n, m = len(ws), len(later_ws)

    def body(*refs):
        ins, later_ins, gp_ref = refs[:n], refs[n:n + m], refs[n + m]
        o = n + m + 1
        outs, later_outs, bias_ref = refs[o:o + n], refs[o + n:o + n + m], refs[o + n + m]
        o += n + m + 1
        stage, later_stage = refs[o:o + n], refs[o + n:o + n + m]
        send_sems, recv_sems, local_sems, later_sems = refs[o + n + m:]
        x, y, c, chips = _mesh_pos()
        s_me = 2 * x + y
        sibling = (x, y, 1 - c)
        def rows_of(k, p):
            rows = _HALF_ROWS[ws[k]]
            return pl.ds(_mo(p * rows, rows), rows)

        def half(k, p):
            return stage[k].at[rows_of(k, p), :]

        def unit(k, s, p):
            return _UNITS[ws[k]](outs[k], s, p)

        def rcopy(k, i, src, dst, to):
            return pltpu.make_async_remote_copy(src_ref=src, dst_ref=dst, send_sem=send_sems.at[k, i],
                                                recv_sem=recv_sems.at[k, i], device_id=to, device_id_type=MESH)

        peers_entered = _signal_peers("both")
        for k in range(n):
            stage[k][rows_of(k, c), :] = ins[k][rows_of(k, c), :].astype(BF)
        peers_entered()
        def piece(ref, k, q):
            rows = _HALF_ROWS[ws[k]] // AG_PIECES
            return ref.at[pl.ds(q * rows, rows), :]

        sends = []
        for q in range(AG_PIECES):
            for j, (cx, cy) in enumerate(chips):
                for k in range(n):
                    cp = rcopy(k, j * AG_PIECES + q, piece(half(k, c), k, q), piece(unit(k, s_me, c), k, q),
                               (cx, cy, c))
                    cp.start()
                    sends.append(cp)
        for k in range(n):
            stage[k][rows_of(k, 1 - c), :] = ins[k][rows_of(k, 1 - c), :].astype(BF)
        local = []
        for k in range(n):
            for p in range(2):
                cp = pltpu.make_async_copy(half(k, p), unit(k, s_me, p), local_sems.at[k, p])
                cp.start()
                local.append(cp)
        for k, w in enumerate(later_ws):
            later_stage[k][...] = later_ins[k][...].astype(BF)
            cp = pltpu.make_async_copy(later_stage[k], _shard_of(later_outs[k], w, s_me), later_sems.at[k])
            cp.start()
            local.append(cp)
        keep = _struct_mask()
        for h in range(N_HEADS):
            bias_ref[h] = jnp.where(keep, _skew_table(gp_ref[h:h + 1, :])[:, :KB], NEG_INF)
        for q in range(AG_PIECES):
            for j, (cx, cy) in enumerate(chips):
                for k in range(n):
                    landed = piece(unit(k, 2 * cx + cy, c), k, q)
                    rcopy(k, j * AG_PIECES + q, landed, landed, (cx, cy, c)).wait_recv()
                    cp = rcopy(k, (3 + j) * AG_PIECES + q, landed, landed, sibling)
                    cp.start()
                    sends.append(cp)
        for q in range(AG_PIECES):
            for j, (cx, cy) in enumerate(chips):
                for k in range(n):
                    other = piece(unit(k, 2 * cx + cy, 1 - c), k, q)
                    rcopy(k, (3 + j) * AG_PIECES + q, other, other, sibling).wait_recv()
        for cp in sends:
            cp.wait_send()
        for cp in local:
            cp.wait()

    vm = pl.BlockSpec(memory_space=pltpu.VMEM)
    outs = pl.pallas_call(
        body, name="ag_weights",
        out_shape=tuple(jax.ShapeDtypeStruct(_FULL_SHAPES[w], BF) for w in tuple(ws) + tuple(later_ws))
        + (jax.ShapeDtypeStruct((N_HEADS, QB, KB), F32),),
        in_specs=[vm] * (n + m + 1), out_specs=[_ANY] * (n + m) + [vm],
        scratch_shapes=[pltpu.VMEM(_SHARD_SHAPES[w], BF) for w in tuple(ws) + tuple(later_ws)]
        + [pltpu.SemaphoreType.DMA((n, 6 * AG_PIECES)), pltpu.SemaphoreType.DMA((n, 6 * AG_PIECES)),
           pltpu.SemaphoreType.DMA((n, 2)), pltpu.SemaphoreType.DMA((m,))],
        compiler_params=_params(48, collective_id=_PEER_SETS["both"]),
    )(*shards, *later_shards, gp)
    return list(outs[:n]), list(outs[n:n + m]), outs[-1]


def _shard_of(ref, w, s):
    if w == 0:
        return ref.at[:, pl.ds(_mo(s * SHARD_IN, 128), SHARD_IN)]
    if w == 3:
        return ref.at[pl.ds(_mo(s * 256, 256), 256), :]
    return ref.at[:, pl.ds(_mo(s * 256, 128), 256)]


def _gather_copies(ws):
    def copies(refs, send_sems, recv_sems):
        x, y, c, chips = _mesh_pos()
        out = []
        for j, (cx, cy) in enumerate(chips):
            for k, w in enumerate(ws):
                mine = _shard_of(refs[k], w, 2 * x + y)
                out.append(pltpu.make_async_remote_copy(
                    src_ref=mine, dst_ref=mine, send_sem=send_sems.at[3 * k + j], recv_sem=recv_sems.at[3 * k + j],
                    device_id=(cx, cy, c), device_id_type=MESH))
        return out
    return copies


def _inproj_fwd(x, norm_g, w_in_bf, tm=512, after=()):
    S = x.shape[0]

    def body(x_ref, g_ref, w_ref, ht_ref, q_ref, k_ref, v_ref, zr_ref):
        xv = x_ref[...]
        r = lax.rsqrt(jnp.mean(xv * xv, axis=-1, keepdims=True) + EPS)
        hf = (xv * r) * g_ref[...]
        ht_ref[...] = hf.T.astype(BF)
        h = hf.astype(BF)
        heads = (q_ref, k_ref, v_ref)
        for j in range(D_IN // 512):
            z = _dot(h, w_ref[:, j * 512:(j + 1) * 512])
            if j < 3:
                zb = z.astype(BF)
                for hd in range(N_HEADS):
                    heads[j][hd] = zb[:, hd * HEAD_DIM:(hd + 1) * HEAD_DIM]
            else:
                zr_ref[:, (j - 3) * 512:(j - 2) * 512] = z

    head_major = jax.ShapeDtypeStruct((N_HEADS, S, HEAD_DIM), BF)
    head_spec = pl.BlockSpec((N_HEADS, tm, HEAD_DIM), lambda i: (0, i, 0))
    return pl.pallas_call(
        _after(body, 3, after), name="inproj_fwd", grid=(S // tm,),
        out_shape=(jax.ShapeDtypeStruct((D_MODEL, S), BF), head_major, head_major, head_major,
                   jax.ShapeDtypeStruct((S, D_IN - 3 * D_A), F32)),
        in_specs=[pl.BlockSpec((tm, D_MODEL), lambda i: (i, 0)),
                  pl.BlockSpec((1, D_MODEL), lambda i: (0, 0)),
                  pl.BlockSpec((D_MODEL, D_IN), lambda i: (0, 0), pipeline_mode=pl.Buffered(1))]
        + [_ANY] * len(after),
        out_specs=[pl.BlockSpec((D_MODEL, tm), lambda i: (0, i)),
                   head_spec, head_spec, head_spec,
                   pl.BlockSpec((tm, D_IN - 3 * D_A), lambda i: (i, 0))],
        compiler_params=_params(52, dimension_semantics=("arbitrary",)),
    )(x, norm_g, w_in_bf, *after)


def _skew_table(gp_row):
    row = lax.broadcasted_iota(jnp.int32, (QB, ROLL_W), 0)
    t = jnp.broadcast_to(gp_row, (QB, ROLL_W))
    for b in range(7):
        t = jnp.where(((row >> b) & 1) == 1, pltpu.roll(t, 1 << b, axis=1), t)
    return t


def _unskew_sum(d):
    half = QB // 2
    while half >= 8:
        d = d[0:half] + pltpu.roll(d[half:2 * half], ROLL_W - half, axis=1)
        half //= 2
    row = lax.broadcasted_iota(jnp.int32, (8, ROLL_W), 0)
    for b in range(3):
        d = jnp.where(((row >> b) & 1) == 1, pltpu.roll(d, ROLL_W - (1 << b), axis=1), d)
    return jnp.sum(d, axis=0, keepdims=True)


def _struct_mask():
    a = lax.broadcasted_iota(jnp.int32, (QB, KB), 0) // CHUNK
    b = lax.broadcasted_iota(jnp.int32, (QB, KB), 1) // CHUNK
    return (b >= a) & (b <= a + N_PREV)


def _load_kv(k_hbm, v_hbm, k_scr, v_scr, sems, S, meanwhile=lambda: None):
    zeros = jnp.zeros((N_HEADS, PADK, HEAD_DIM), BF)
    k_scr[:, 0:PADK, :] = zeros
    v_scr[:, 0:PADK, :] = zeros
    ck = pltpu.make_async_copy(k_hbm, k_scr.at[:, pl.ds(PADK, S), :], sems.at[0])
    cv = pltpu.make_async_copy(v_hbm, v_scr.at[:, pl.ds(PADK, S), :], sems.at[1])
    ck.start()
    cv.start()
    meanwhile()
    ck.wait()
    cv.wait()


_BATCH_NT = (((2,), (2,)), ((0,), (0,)))
_BATCH_NN = (((2,), (1,)), ((0,), (0,)))
_BATCH_TN = (((1,), (1,)), ((0,), (0,)))


def _bdot(a, b, dims):
    return lax.dot_general(a, b, dims, preferred_element_type=F32)


def _scaled(q):
    return q * jnp.asarray(SCALE, BF)


def _scores(qs, kb, bias, i, front):
    s = _bdot(qs, kb, _BATCH_NT) + bias
    if front:
        col = lax.broadcasted_iota(jnp.int32, (1, 1, KB), 2)
        s = jnp.where(col >= PADK - i * QB, s, NEG_INF)
    return s


def _attn_fwd(q3, k3, v3, bias):
    S = q3.shape[1]

    def body(q_ref, k_hbm, v_hbm, bias_ref, o_ref, lse_ref, k_scr, v_scr, sems):
        @pl.when(pl.program_id(0) == 0)
        def _():
            _load_kv(k_hbm, v_hbm, k_scr, v_scr, sems, S)

        def step(i, rows, front):
            start = pl.multiple_of(i * QB, QB)
            kb = k_scr[:, pl.ds(start, KB), :]
            vb = v_scr[:, pl.ds(start, KB), :]
            s = _scores(_scaled(q_ref[:, rows, :]), kb, bias_ref[...], i, front)
            m = jnp.max(s, axis=-1, keepdims=True)
            e = jnp.exp(s - m)
            l = jnp.sum(e, axis=-1, keepdims=True)
            p = e * (1.0 / l)
            o = _bdot(p.astype(BF), vb, _BATCH_NN)
            lse_ref[:, rows, :] = jnp.broadcast_to(m + jnp.log(l), (N_HEADS, QB, 128))
            for h in range(N_HEADS):
                o_ref[rows, h * HEAD_DIM:(h + 1) * HEAD_DIM] = o[h]

        def block(j, carry):
            i = pl.program_id(0) * Q_PER_STEP + j
            rows = pl.ds(pl.multiple_of(j * QB, QB), QB)
            pl.when(i < KEEP)(functools.partial(step, i, rows, True))
            pl.when(i >= KEEP)(functools.partial(step, i, rows, False))
            return carry

        lax.fori_loop(0, Q_PER_STEP, block, 0)

    rows_per_step = Q_PER_STEP * QB
    kv_scr = pltpu.VMEM((N_HEADS, S + PADK, HEAD_DIM), BF)
    return pl.pallas_call(
        body, name="attn_fwd", grid=(S // rows_per_step,),
        out_shape=(jax.ShapeDtypeStruct((S, D_A), F32), jax.ShapeDtypeStruct((N_HEADS, S, 128), F32)),
        in_specs=[pl.BlockSpec((N_HEADS, rows_per_step, HEAD_DIM), lambda g: (0, g, 0)),
                  pl.BlockSpec(memory_space=pl.ANY), pl.BlockSpec(memory_space=pl.ANY),
                  pl.BlockSpec((N_HEADS, QB, KB), lambda g: (0, 0, 0))],
        out_specs=[pl.BlockSpec((rows_per_step, D_A), lambda g: (g, 0)),
                   pl.BlockSpec((N_HEADS, rows_per_step, 128), lambda g: (0, g, 0))],
        scratch_shapes=[kv_scr, kv_scr, pltpu.SemaphoreType.DMA((2,))],
        compiler_params=_params(48, dimension_semantics=("arbitrary",)),
    )(q3, k3, v3, bias)


def _attn_bwd(q3, k3, v3, d_att3, lse, bias, after=()):
    S = q3.shape[1]
    nq = S // QB

    def body(q_ref, do_ref, k_hbm, v_hbm, lse_ref, bias_ref, dq_ref, dk_ref, dv_ref, dgp_ref,
             k_scr, v_scr, dk_acc, dv_acc, dbias_acc, pad_scr, sems):
        @pl.when(pl.program_id(0) == 0)
        def _():
            def clear():
                dk_acc[...] = jnp.zeros_like(dk_acc)
                dv_acc[...] = jnp.zeros_like(dv_acc)
                dbias_acc[...] = jnp.zeros_like(dbias_acc)
            _load_kv(k_hbm, v_hbm, k_scr, v_scr, sems, S, clear)

        def step(i, rows, front):
            start = pl.multiple_of(i * QB, QB)
            kb = k_scr[:, pl.ds(start, KB), :]
            vb = v_scr[:, pl.ds(start, KB), :]
            qs = _scaled(q_ref[:, rows, :])
            do = do_ref[:, rows, :]
            p = jnp.exp(_scores(qs, kb, bias_ref[...], i, front) - jnp.tile(lse_ref[:, rows, :], (1, 1, KB // 128)))
            dp = _bdot(do, vb, _BATCH_NT)
            ds = p * (dp - jnp.sum(dp * p, axis=-1, keepdims=True))
            dbias_acc[...] += ds
            dsb = ds.astype(BF)
            dq = _bdot(dsb, kb, _BATCH_NN) * SCALE
            for h in range(N_HEADS):
                dq_ref[rows, h * HEAD_DIM:(h + 1) * HEAD_DIM] = dq[h].astype(BF)
            dk_acc[...] += _bdot(dsb, qs, _BATCH_TN)
            dv_acc[...] += _bdot(p.astype(BF), do, _BATCH_TN)

        def block(j, carry):
            i = pl.program_id(0) * Q_PER_STEP + j
            rows = pl.ds(pl.multiple_of(j * QB, QB), QB)
            pl.when(i < KEEP)(functools.partial(step, i, rows, True))
            pl.when((i >= KEEP) & (i < nq))(functools.partial(step, i, rows, False))
            for h in range(N_HEADS):
                hs = slice(h * HEAD_DIM, (h + 1) * HEAD_DIM)
                dk_ref[rows, hs] = dk_acc[h, 0:QB, :].astype(BF)
                dv_ref[rows, hs] = dv_acc[h, 0:QB, :].astype(BF)
            dk_acc[:, 0:KB - QB, :] = dk_acc[:, QB:KB, :]
            dv_acc[:, 0:KB - QB, :] = dv_acc[:, QB:KB, :]
            dk_acc[:, KB - QB:KB, :] = jnp.zeros((N_HEADS, QB, HEAD_DIM), F32)
            dv_acc[:, KB - QB:KB, :] = jnp.zeros((N_HEADS, QB, HEAD_DIM), F32)
            return carry

        lax.fori_loop(0, Q_PER_STEP, block, 0)

        @pl.when(pl.program_id(0) == n_steps - 1)
        def _():
            lane = lax.broadcasted_iota(jnp.int32, (1, ROLL_W), 1)
            hi = (lane < 384) | (lane >= 832)
            lo = (lane > 640) & (lane < 832)
            pad_scr[...] = jnp.zeros_like(pad_scr)
            for h in range(N_HEADS):
                pad_scr[:, 0:KB] = dbias_acc[h]
                g = _unskew_sum(pad_scr[...])
                s_hi = jnp.sum(jnp.where(hi, g, 0.0), axis=-1, keepdims=True)
                s_lo = jnp.sum(jnp.where(lo, g, 0.0), axis=-1, keepdims=True)
                g = jnp.where(lane == 384, g + s_hi, g)
                g = jnp.where(lane == 640, g + s_lo, g)
                dgp_ref[h:h + 1, :] = g

    assert nq % Q_PER_STEP == 0 and KEEP % Q_PER_STEP == 0
    rows_per_step = Q_PER_STEP * QB
    n_steps = (nq + KEEP) // Q_PER_STEP
    last = nq // Q_PER_STEP - 1
    lag = KEEP // Q_PER_STEP
    kv_scr = pltpu.VMEM((N_HEADS, S + PADK, HEAD_DIM), BF)
    return pl.pallas_call(
        _after(body, 6, after), name="attn_bwd", grid=(n_steps,),
        out_shape=(jax.ShapeDtypeStruct((S, D_A), BF), jax.ShapeDtypeStruct((S, D_A), BF),
                   jax.ShapeDtypeStruct((S, D_A), BF), jax.ShapeDtypeStruct((N_HEADS, ROLL_W), F32)),
        in_specs=[pl.BlockSpec((N_HEADS, rows_per_step, HEAD_DIM), lambda g: (0, jnp.minimum(g, last), 0)),
                  pl.BlockSpec((N_HEADS, rows_per_step, HEAD_DIM), lambda g: (0, jnp.minimum(g, last), 0)),
                  pl.BlockSpec(memory_space=pl.ANY), pl.BlockSpec(memory_space=pl.ANY),
                  pl.BlockSpec((N_HEADS, rows_per_step, 128), lambda g: (0, jnp.minimum(g, last), 0)),
                  pl.BlockSpec((N_HEADS, QB, KB), lambda g: (0, 0, 0))] + [_ANY] * len(after),
        out_specs=[pl.BlockSpec((rows_per_step, D_A), lambda g: (jnp.minimum(g, last), 0)),
                   pl.BlockSpec((rows_per_step, D_A), lambda g: (jnp.maximum(g - lag, 0), 0)),
                   pl.BlockSpec((rows_per_step, D_A), lambda g: (jnp.maximum(g - lag, 0), 0)),
                   pl.BlockSpec((N_HEADS, ROLL_W), lambda g: (0, 0))],
        scratch_shapes=[kv_scr, kv_scr,
                        pltpu.VMEM((N_HEADS, KB, HEAD_DIM), F32), pltpu.VMEM((N_HEADS, KB, HEAD_DIM), F32),
                        pltpu.VMEM((N_HEADS, QB, KB), F32), pltpu.VMEM((QB, ROLL_W), F32),
                        pltpu.SemaphoreType.DMA((2,))],
        compiler_params=_params(56, dimension_semantics=("arbitrary",)),
    )(q3, d_att3, k3, v3, lse, bias, *after)


def _sgu_core(ub, vb, lg, lb):
    u, du = _gelu_and_grad(ub)
    v, dv = _gelu_and_grad(vb)
    mu = jnp.mean(v, axis=-1, keepdims=True)
    vc = v - mu
    rstd = lax.rsqrt(jnp.mean(vc * vc, axis=-1, keepdims=True) + EPS)
    xh = vc * rstd
    vn = xh * lg + lb
    return u, du, dv, rstd, xh, vn


def _tri():
    r = lax.broadcasted_iota(jnp.int32, (SGU_CHUNK, SGU_CHUNK), 0)
    c = lax.broadcasted_iota(jnp.int32, (SGU_CHUNK, SGU_CHUNK), 1)
    return r >= c


def _tail_sgu(att, zrest, x, target, w_pa, w_pb, w_out, b_gate, final_g, ln_g, ln_b, w_s, b_s_t, tm=256):
    S = x.shape[0]
    nt = S // tm
    chunks = tm // SGU_CHUNK

    def body(att_ref, ga_ref, ub_ref, vb_ref, gb_ref, gta_ref, gtb_ref, x_ref, t_ref,
             wpa_ref, wpb_ref, wout_ref, bg_ref, fg_ref, lg_ref, lb_ref, ws_ref, bst_ref,
             dout_ref, datt_ref, dzt_ref, dzs_ref, gwout_hbm, gwpa_hbm, gwpb_hbm,
             gbg_ref, gfg_ref, loss_ref, gws_ref, gbs_ref, glg_ref, glb_ref,
             acc_out, acc_pa, acc_pb, sg_scr, mix_scr, dvn_scr, bs_acc, sems):
        i = pl.program_id(0)

        @pl.when(i == 0)
        def _():
            for r in (acc_out, acc_pa, acc_pb, gbg_ref, gfg_ref, loss_ref, gws_ref, glg_ref, glb_ref, bs_acc):
                r[...] = jnp.zeros_like(r)

        u, du, dv, rstd, xh, vn = _sgu_core(ub_ref[...], vb_ref[...], lg_ref[...], lb_ref[...])
        vnb = vn.astype(BF)
        tri = _tri()
        blocks = [(g, slice(n * SGU_CHUNK, (n + 1) * SGU_CHUNK), slice(g * 128, (g + 1) * 128))
                  for g in range(N_GROUPS) for n in range(chunks)]
        wts = [jnp.where(tri, ws_ref[g], 0.0) for g in range(N_GROUPS)]
        for g, rs, cs in blocks:
            mixed = _dot(wts[g].astype(BF), vnb[rs, cs]) + bst_ref[:, g:g + 1]
            mix_scr[rs, cs] = mixed
            sg_scr[rs, cs] = u[rs, cs] * mixed

        att = att_ref[...]
        sg = sg_scr[...]
        sa, dsa = _silu_and_grad(ga_ref[...])
        sb, dsb = _silu_and_grad(gb_ref[...])
        ya = (att * sa).astype(BF)
        yb = (sg * sb).astype(BF)
        pa = _dot(ya, wpa_ref[...])
        pb = _dot(yb, wpb_ref[...])
        ga = _sigmoid(gta_ref[...] + bg_ref[:, 0:D_MODEL])
        gb = _sigmoid(gtb_ref[...] + bg_ref[:, D_MODEL:2 * D_MODEL])
        merged = (ga * pa + gb * pb).astype(BF)
        out = x_ref[...] + _dot(merged, wout_ref[...])
        r2 = lax.rsqrt(jnp.mean(out * out, axis=-1, keepdims=True) + EPS)
        nrm = out * r2
        fg = fg_ref[...]
        err = nrm * fg - t_ref[...]
        loss_ref[...] += 0.5 * jnp.sum(jnp.mean(err * err, axis=-1, keepdims=True))
        dy = err * (1.0 / D_MODEL)
        gfg_ref[...] += jnp.sum(dy * nrm, axis=0, keepdims=True)
        dn = dy * fg
        d_out = r2 * (dn - nrm * jnp.mean(dn * nrm, axis=-1, keepdims=True))
        dout_ref[...] = d_out
        d_outb = d_out.astype(BF)
        acc_out[...] += _dot_tn(merged, d_outb)
        dm = _dot_nt(d_outb, wout_ref[...])
        d_pa = (dm * ga).astype(BF)
        d_pb = (dm * gb).astype(BF)
        d_gta = dm * pa * (ga * (1.0 - ga))
        d_gtb = dm * pb * (gb * (1.0 - gb))
        gbg_ref[:, 0:D_MODEL] += jnp.sum(d_gta, axis=0, keepdims=True)
        gbg_ref[:, D_MODEL:2 * D_MODEL] += jnp.sum(d_gtb, axis=0, keepdims=True)
        dzt_ref[:, 2 * D_A:2 * D_A + D_MODEL] = d_gta.astype(BF)
        dzt_ref[:, 2 * D_A + D_MODEL:] = d_gtb.astype(BF)
        acc_pa[...] += _dot_tn(ya, d_pa)
        acc_pb[...] += _dot_tn(yb, d_pb)
        d_ya = _dot_nt(d_pa, wpa_ref[...])
        d_yb = _dot_nt(d_pb, wpb_ref[...])
        d_att = (d_ya * sa).astype(BF)
        for hd in range(N_HEADS):
            datt_ref[hd] = d_att[:, hd * HEAD_DIM:(hd + 1) * HEAD_DIM]
        dzt_ref[:, 0:D_A] = (d_ya * att * dsa).astype(BF)
        dzt_ref[:, D_A:2 * D_A] = (d_yb * sg * dsb).astype(BF)

        dsg = d_yb * sb
        dzs_ref[:, 0:D_B] = (dsg * mix_scr[...] * du).astype(BF)
        dmix = dsg * u
        for g, rs, cs in blocks:
            dmb = dmix[rs, cs].astype(BF)
            bs_acc[:, cs] += dmix[rs, cs]
            gws_ref[g] += _dot_nt(dmb, vnb[rs, cs])
            dvn_scr[rs, cs] = _dot(wts[g].T.astype(BF), dmb)
        dvn = dvn_scr[...]
        glg_ref[...] += jnp.sum(dvn * xh, axis=0, keepdims=True)
        glb_ref[...] += jnp.sum(dvn, axis=0, keepdims=True)
        dxh = dvn * lg_ref[...]
        dvv = rstd * (dxh - jnp.mean(dxh, axis=-1, keepdims=True)
                      - xh * jnp.mean(dxh * xh, axis=-1, keepdims=True))
        dzs_ref[:, D_B:2 * D_B] = (dvv * dv).astype(BF)

        @pl.when(i == nt - 1)
        def _():
            cps = [pltpu.make_async_copy(acc_out, gwout_hbm, sems.at[0]),
                   pltpu.make_async_copy(acc_pa, gwpa_hbm, sems.at[1]),
                   pltpu.make_async_copy(acc_pb, gwpb_hbm, sems.at[2])]
            for cp in cps:
                cp.start()
            lane = lax.broadcasted_iota(jnp.int32, (SGU_CHUNK, 128), 1)
            cols = jnp.zeros((SGU_CHUNK, 128), F32)
            for g in range(N_GROUPS):
                gws_ref[g] = jnp.where(tri, gws_ref[g], 0.0)
                col = jnp.sum(bs_acc[:, g * 128:(g + 1) * 128], axis=-1, keepdims=True)
                cols = jnp.where(lane == g, col, cols)
            gbs_ref[...] = cols
            for cp in cps:
                cp.wait()

    c2 = lambda i: (0, 0)
    c3 = lambda i: (0, 0, 0)
    zcol = lambda w, blk: pl.BlockSpec((tm, w), lambda i: (i, blk))
    row = lambda w: pl.BlockSpec((tm, w), lambda i: (i, 0))
    return pl.pallas_call(
        body, name="tail", grid=(nt,),
        out_shape=(jax.ShapeDtypeStruct((S, D_MODEL), F32), jax.ShapeDtypeStruct((N_HEADS, S, HEAD_DIM), BF),
                   jax.ShapeDtypeStruct((S, 3072), BF), jax.ShapeDtypeStruct((S, 2 * D_B), BF),
                   jax.ShapeDtypeStruct((D_MODEL, D_MODEL), F32), jax.ShapeDtypeStruct((D_A, D_MODEL), F32),
                   jax.ShapeDtypeStruct((D_B, D_MODEL), F32),
                   jax.ShapeDtypeStruct((1, 2 * D_MODEL), F32), jax.ShapeDtypeStruct((1, D_MODEL), F32),
                   jax.ShapeDtypeStruct((1, 128), F32),
                   jax.ShapeDtypeStruct((N_GROUPS, 128, 128), F32), jax.ShapeDtypeStruct((SGU_CHUNK, 128), F32),
                   jax.ShapeDtypeStruct((1, D_B), F32), jax.ShapeDtypeStruct((1, D_B), F32)),
        in_specs=[row(D_A), zcol(512, 0), zcol(512, 1), zcol(512, 2), zcol(512, 3),
                  zcol(D_MODEL, 2), zcol(D_MODEL, 3), row(D_MODEL), row(D_MODEL),
                  pl.BlockSpec((D_A, D_MODEL), c2), pl.BlockSpec((D_B, D_MODEL), c2),
                  pl.BlockSpec((D_MODEL, D_MODEL), c2),
                  pl.BlockSpec((1, 2 * D_MODEL), c2), pl.BlockSpec((1, D_MODEL), c2),
                  pl.BlockSpec((1, D_B), c2), pl.BlockSpec((1, D_B), c2),
                  pl.BlockSpec((N_GROUPS, 128, 128), c3), pl.BlockSpec((128, N_GROUPS), c2)],
        out_specs=[row(D_MODEL), pl.BlockSpec((N_HEADS, tm, HEAD_DIM), lambda i: (0, i, 0)),
                   row(3072), row(2 * D_B), _ANY, _ANY, _ANY,
                   pl.BlockSpec((1, 2 * D_MODEL), c2), pl.BlockSpec((1, D_MODEL), c2),
                   pl.BlockSpec((1, 128), c2),
                   pl.BlockSpec((N_GROUPS, 128, 128), c3), pl.BlockSpec((SGU_CHUNK, 128), c2),
                   pl.BlockSpec((1, D_B), c2), pl.BlockSpec((1, D_B), c2)],
        scratch_shapes=[pltpu.VMEM((D_MODEL, D_MODEL), F32), pltpu.VMEM((D_A, D_MODEL), F32),
                        pltpu.VMEM((D_B, D_MODEL), F32),
                        pltpu.VMEM((tm, D_B), F32), pltpu.VMEM((tm, D_B), F32), pltpu.VMEM((tm, D_B), F32),
                        pltpu.VMEM((SGU_CHUNK, D_B), F32), pltpu.SemaphoreType.DMA((3,))],
        compiler_params=_params(58, dimension_semantics=("arbitrary",)),
    )(att, zrest, zrest, zrest, zrest, zrest, zrest, x, target, w_pa, w_pb, w_out, b_gate, final_g,
      ln_g, ln_b, w_s, b_s_t)


_DZ_MAP = ((0, 0), (1, 0), (2, 0), (3, 0), (4, 0), (4, 1), (3, 1), (3, 2), (3, 3), (3, 4), (3, 5))


def _dh_gradx(dq, dk, dv, dzt, dzs, w_in_bf, x, norm_g, d_out, tm=512, after=()):
    S = x.shape[0]

    def body(dq_ref, dk_ref, dv_ref, dzt_ref, dzs_ref, w_ref, x_ref, g_ref, dout_ref, gx_ref, gn_ref):
        i = pl.program_id(0)

        @pl.when(i == 0)
        def _():
            gn_ref[...] = jnp.zeros_like(gn_ref)

        pieces = (dq_ref, dk_ref, dv_ref, dzt_ref, dzs_ref)
        dh = jnp.zeros((tm, D_MODEL), F32)
        for j, (pc, blk) in enumerate(_DZ_MAP):
            dh += _dot_nt(pieces[pc][:, blk * 512:(blk + 1) * 512], w_ref[:, j * 512:(j + 1) * 512])
        xv = x_ref[...]
        r = lax.rsqrt(jnp.mean(xv * xv, axis=-1, keepdims=True) + EPS)
        nrm = xv * r
        gn_ref[...] += jnp.sum(dh * nrm, axis=0, keepdims=True)
        dn = dh * g_ref[...]
        gx_ref[...] = r * (dn - nrm * jnp.mean(dn * nrm, axis=-1, keepdims=True)) + dout_ref[...]

    row = lambda w: pl.BlockSpec((tm, w), lambda i: (i, 0))
    c2 = lambda i: (0, 0)
    return pl.pallas_call(
        _after(body, 9, after), name="dh_gradx", grid=(S // tm,),
        out_shape=(jax.ShapeDtypeStruct((S, D_MODEL), F32), jax.ShapeDtypeStruct((1, D_MODEL), F32)),
        in_specs=[row(512), row(512), row(512), row(3072), row(1024),
                  pl.BlockSpec((D_MODEL, D_IN), c2, pipeline_mode=pl.Buffered(1)), row(D_MODEL),
                  pl.BlockSpec((1, D_MODEL), c2), row(D_MODEL)]
        + [_ANY] * len(after),
        out_specs=[row(D_MODEL), pl.BlockSpec((1, D_MODEL), c2)],
        compiler_params=_params(48, dimension_semantics=("arbitrary",)),
    )(dq, dk, dv, dzt, dzs, w_in_bf, x, norm_g, d_out, *after)


def _gw_in(ht, dq, dk, dv, dzt, dzs, tn=512, after=()):
    S = ht.shape[1]
    per = 512 // tn
    cols = tuple((pc, per * blk + h) for pc, blk in _DZ_MAP for h in range(per))

    def body(ht_ref, dq_ref, dk_ref, dv_ref, dzt_ref, dzs_ref, o_ref, ob_ref):
        j = pl.program_id(0)
        pieces = (dq_ref, dk_ref, dv_ref, dzt_ref, dzs_ref)
        for pc in range(5):
            hit = functools.reduce(jnp.logical_or, [j == jj for jj, (p, _) in enumerate(cols) if p == pc])

            @pl.when(hit)
            def _(pc=pc):
                g = _dot(ht_ref[...], pieces[pc][...])
                o_ref[...] = g
                ob_ref[...] = g.astype(BF)

    def piece_spec(pc):
        cur = next(blk for p, blk in cols if p == pc)
        held = []
        for p, blk in cols:
            cur = blk if p == pc else cur
            held.append(cur)

        def index_map(j):
            blk = jnp.int32(held[0])
            for jj in range(1, len(held)):
                if held[jj] != held[jj - 1]:
                    blk = jnp.where(j >= jj, jnp.int32(held[jj]), blk)
            return (0, blk)

        return pl.BlockSpec((S, tn), index_map)

    return pl.pallas_call(
        _after(body, 6, after), name="gw_in", grid=(len(cols),),
        out_shape=(jax.ShapeDtypeStruct((D_MODEL, D_IN), F32), jax.ShapeDtypeStruct((D_MODEL, D_IN), BF)),
        in_specs=[pl.BlockSpec((D_MODEL, S), lambda j: (0, 0), pipeline_mode=pl.Buffered(1))]
        + [piece_spec(pc) for pc in range(5)]
        + [_ANY] * len(after),
        out_specs=[pl.BlockSpec((D_MODEL, tn), lambda j: (0, j)), pl.BlockSpec((D_MODEL, tn), lambda j: (0, j))],
        compiler_params=_params(56, dimension_semantics=("arbitrary",)),
    )(ht, dq, dk, dv, dzt, dzs, *after)


_HBM = pl.BlockSpec(memory_space=pltpu.HBM)
_SEM = pl.BlockSpec(memory_space=pltpu.SEMAPHORE)
_ANY = pl.BlockSpec(memory_space=pl.ANY)
_EFFECT = pltpu.SideEffectType.DATAFLOW_SIDE_EFFECTING


def _in_hbm(a):
    return pltpu.with_memory_space_constraint(a, pltpu.HBM)


def _after(body, n_in, after):
    if not after:
        return body
    return lambda *refs: body(*refs[:n_in], *refs[n_in + len(after):])


class _Started:
    def __init__(self, send, recv, bufs, token):
        self.send, self.recv, self.bufs, self.token = send, recv, bufs, token


_PEER_SETS = {"sibling": 7, "chips": 8, "both": 9}


def _peers(kind):
    x, y, c, chips = _mesh_pos()
    return ([(x, y, 1 - c)] if kind in ("sibling", "both") else []) + (
        [(cx, cy, c) for cx, cy in chips] if kind in ("chips", "both") else [])


def _signal_peers(kind):
    barrier = pltpu.get_barrier_semaphore()
    targets = _peers(kind)
    for peer in targets:
        pl.semaphore_signal(barrier, inc=1, device_id=peer, device_id_type=MESH)
    return lambda: pl.semaphore_wait(barrier, len(targets))


def _split_start(name, bufs, n_copies, copies, peers, after=()):
    nb = len(bufs)

    def body(*refs):
        _signal_peers(peers)()
        refs = refs[:nb] + refs[nb + len(after):]
        for cp in copies(refs[:nb], refs[nb], refs[nb + 1]):
            cp.start()
        refs[-1][...] = jnp.zeros_like(refs[-1])

    outs = pl.pallas_call(
        body, name=name,
        out_shape=(pltpu.SemaphoreType.DMA((n_copies,)), pltpu.SemaphoreType.DMA((n_copies,)),
                   *[pltpu.HBM(b.shape, b.dtype) for b in bufs], jax.ShapeDtypeStruct((8, 128), F32)),
        in_specs=[_HBM] * nb + [_ANY] * len(after),
        out_specs=(_SEM, _SEM, *[_HBM] * nb, pl.BlockSpec(memory_space=pltpu.VMEM)),
        input_output_aliases={k: 2 + k for k in range(nb)},
        compiler_params=_params(1, has_side_effects=_EFFECT, collective_id=_PEER_SETS[peers]),
    )(*[_in_hbm(b) for b in bufs], *after)
    return _Started(outs[0], outs[1], list(outs[2:2 + nb]), outs[-1])


def _split_wait(name, started, copies, after):
    nb = len(started.bufs)
    after = tuple(after) if isinstance(after, (tuple, list)) else (after,)

    def body(*refs):
        for cp in copies(refs[:nb], refs[nb], refs[nb + 1]):
            cp.wait_send()
            cp.wait_recv()

    return list(pl.pallas_call(
        body, name=name,
        out_shape=tuple(pltpu.HBM(b.shape, b.dtype) for b in started.bufs),
        in_specs=[_HBM] * nb + [_SEM, _SEM] + [_ANY] * len(after),
        out_specs=tuple([_HBM] * nb),
        input_output_aliases={k: k for k in range(nb)},
        compiler_params=_params(1, has_side_effects=_EFFECT),
    )(*started.bufs, started.send, started.recv, *after))


def _x1_copies(ws):
    def copies(refs, send_sems, recv_sems):
        x, y, c, _ = _mesh_pos()
        out = []
        for k, w in enumerate(ws):
            for s in range(N_SHARD):
                out.append(pltpu.make_async_remote_copy(
                    src_ref=_UNITS[w](refs[k], s, 1 - c), dst_ref=refs[len(ws) + k].at[s],
                    send_sem=send_sems.at[N_SHARD * k + s], recv_sem=recv_sems.at[N_SHARD * k + s],
                    device_id=(x, y, 1 - c), device_id_type=MESH))
        return out
    return copies


def _x2_copies(n):
    def copies(refs, send_sems, recv_sems):
        x, y, c, chips = _mesh_pos()
        out = []
        for j, (cx, cy) in enumerate(chips):
            for k in range(n):
                out.append(pltpu.make_async_remote_copy(
                    src_ref=refs[k].at[2 * cx + cy], dst_ref=refs[n + k].at[j],
                    send_sem=send_sems.at[3 * k + j], recv_sem=recv_sems.at[3 * k + j],
                    device_id=(cx, cy, c), device_id_type=MESH))
        return out
    return copies


def _x3_copies(ws):
    def copies(refs, send_sems, recv_sems):
        x, y, c, _ = _mesh_pos()
        out = []
        for k, w in enumerate(ws):
            rows = _HALF_ROWS[w]
            mine = refs[k].at[pl.ds(_mo(c * rows, rows), rows), :]
            out.append(pltpu.make_async_remote_copy(
                src_ref=mine, dst_ref=mine, send_sem=send_sems.at[k], recv_sem=recv_sems.at[k],
                device_id=(x, y, 1 - c), device_id_type=MESH))
        return out
    return copies


def _x1_lands(ws, dtype=F32):
    return [lax.empty((N_SHARD,) + _UNIT_SHAPES[w], dtype) for w in ws]


def _x2_lands(ws):
    return [lax.empty((3,) + _UNIT_SHAPES[w], BF) for w in ws]


def _grad_add1(w, g, recv, pos):
    ur, uc = _UNIT_SHAPES[w]

    def body(pos_ref, g_ref, r_ref, csb_ref):
        csb_ref[0] = (g_ref[...] + r_ref[0].astype(F32)).astype(BF)

    u3 = lambda k, pos: (pos[2 + k], 0, 0)
    return pl.pallas_call(
        body, name=f"grad_add1_{w}",
        grid_spec=pltpu.PrefetchScalarGridSpec(
            num_scalar_prefetch=1, grid=(N_SHARD - 1,),
            in_specs=[pl.BlockSpec((ur, uc), lambda k, pos: (pos[0], pos[2 + k])), pl.BlockSpec((1, ur, uc), u3)],
            out_specs=pl.BlockSpec((1, ur, uc), u3)),
        out_shape=jax.ShapeDtypeStruct((N_SHARD, ur, uc), BF),
        compiler_params=_params(40, dimension_semantics=("arbitrary",)),
    )(pos, g, recv)


def _grad_add1_group(ws, gs, recvs, pos):
    n = len(ws)

    def body(pos_ref, *refs):
        s = pl.program_id(0)
        for k in range(n):
            g, r, own, csb = refs[k], refs[n + k], refs[2 * n + k], refs[3 * n + k]
            v = g[...] + r[0]
            csb[0] = v.astype(BF)

            @pl.when(s == pos_ref[1])
            def _(own=own, v=v):
                own[...] = v

    def g_spec(w):
        if w == 3:
            return pl.BlockSpec(_UNIT_SHAPES[w], lambda s, pos: (2 * s + pos[0], 0))
        return pl.BlockSpec(_UNIT_SHAPES[w], lambda s, pos: (pos[0], s))

    slot = lambda w: pl.BlockSpec((1,) + _UNIT_SHAPES[w], lambda s, pos: (s, 0, 0))
    outs = pl.pallas_call(
        body, name="grad_add1_group",
        grid_spec=pltpu.PrefetchScalarGridSpec(
            num_scalar_prefetch=1, grid=(N_SHARD,),
            in_specs=[g_spec(w) for w in ws] + [slot(w) for w in ws],
            out_specs=[pl.BlockSpec(_UNIT_SHAPES[w], lambda s, pos: (0, 0)) for w in ws] + [slot(w) for w in ws]),
        out_shape=tuple(jax.ShapeDtypeStruct(_UNIT_SHAPES[w], F32) for w in ws)
        + tuple(jax.ShapeDtypeStruct((N_SHARD,) + _UNIT_SHAPES[w], BF) for w in ws),
        compiler_params=_params(32, dimension_semantics=("arbitrary",)),
    )(pos, *gs, *recvs)
    return list(outs[:n]), list(outs[n:])


def _grad_add2_group(ws, owns, recvs):
    n = len(ws)

    def body(*refs):
        c = lax.axis_index("c")
        for k, w in enumerate(ws):
            own, r, o = refs[k], refs[n + k], refs[2 * n + k]
            rows = _HALF_ROWS[w]
            total = ((own[...] + r[0].astype(F32)) + r[1].astype(F32)) + r[2].astype(F32)
            o[pl.ds(_mo(c * rows, rows), rows), :] = total

    vm = pl.BlockSpec(memory_space=pltpu.VMEM)
    return list(pl.pallas_call(
        body, name="grad_add2_group",
        out_shape=tuple(jax.ShapeDtypeStruct(_SHARD_SHAPES[w], F32) for w in ws),
        in_specs=[vm] * (2 * n), out_specs=[vm] * n,
        compiler_params=_params(32),
    )(*owns, *recvs))


def _grad_add2(w, g, recv1, recv2, pos):
    ur, uc = _UNIT_SHAPES[w]
    nt = 4
    tr = ur // nt

    def body(pos_ref, g_ref, r1_ref, r2_ref, o_ref):
        own = g_ref[...] + r1_ref[0].astype(F32)
        o_ref[...] = ((own + r2_ref[0].astype(F32)) + r2_ref[1].astype(F32)) + r2_ref[2].astype(F32)

    mine = lambda t, pos: (pos[0] * nt + t, 0)
    return pl.pallas_call(
        body, name=f"grad_add2_{w}",
        grid_spec=pltpu.PrefetchScalarGridSpec(
            num_scalar_prefetch=1, grid=(nt,),
            in_specs=[pl.BlockSpec((tr, uc), lambda t, pos: (pos[0] * nt + t, pos[1])),
                      pl.BlockSpec((1, tr, uc), lambda t, pos: (pos[1], t, 0)),
                      pl.BlockSpec((3, tr, uc), lambda t, pos: (0, t, 0))],
            out_specs=pl.BlockSpec((tr, uc), mine)),
        out_shape=jax.ShapeDtypeStruct(_SHARD_SHAPES[w], F32),
        compiler_params=_params(32, dimension_semantics=("arbitrary",)),
    )(pos, g, recv1, recv2)


def _adamw_math(w, g, m, v):
    m = ADAM_B1 * m + (1.0 - ADAM_B1) * g
    v = ADAM_B2 * v + (1.0 - ADAM_B2) * (g * g)
    m_hat = m / ADAM_C1
    v_hat = v / ADAM_C2
    delta = -ADAM_LR * (m_hat / (jnp.sqrt(v_hat) + ADAM_EPS) + ADAM_WD * w)
    return delta, m, v


ADAMW_STEPS = 4


def _adamw(ws_, gs, ms, vs):
    n = len(ws_)

    def body(*refs):
        for k in range(n):
            w, g, m, v = (refs[j * n + k] for j in range(4))
            d, nm, nv, gc = (refs[(4 + j) * n + k] for j in range(4))
            gv = g[...]
            d[...], nm[...], nv[...] = _adamw_math(w[...], gv, m[...], v[...])
            gc[...] = gv

    specs = [pl.BlockSpec((a.shape[0] // ADAMW_STEPS, a.shape[1]), lambda i: (i, 0)) for a in ws_] * 4
    outs = pl.pallas_call(
        body, name="adamw", grid=(ADAMW_STEPS,),
        out_shape=tuple(jax.ShapeDtypeStruct(a.shape, F32) for _ in range(4) for a in ws_),
        in_specs=specs, out_specs=specs,
        compiler_params=_params(40, dimension_semantics=("arbitrary",)),
    )(*ws_, *gs, *ms, *vs)
    return [tuple(outs[j * n + k] for j in range(4)) for k in range(n)]


_REL_PAD = 384
_VEC_FIELDS = (("norm_g", 0, D_MODEL), ("b_gate", 1024, 2 * D_MODEL), ("sgu_ln_g", 3072, D_B),
               ("sgu_ln_b", 3584, D_B), ("b_s", 4096, N_GROUPS * 128), ("final_g", 4608, D_MODEL))
_LOSS_OFF = 5632
_REL_OFF = 5760
_NV = _REL_OFF + N_HEADS * _REL_PAD
_N_FIELDS = len(_VEC_FIELDS) + 2


_B_S_FIELD = [f[0] for f in _VEC_FIELDS].index("b_s")


def _assemble_row(dst, fields, transposed_b_s):
    for f, (_, off, n) in enumerate(_VEC_FIELDS):
        if transposed_b_s and f == _B_S_FIELD:
            t = fields[f][...].T
            for g in range(N_GROUPS):
                dst[:, off + 128 * g:off + 128 * (g + 1)] = t[g:g + 1, :]
        else:
            dst[:, off:off + n] = fields[f][...]
    for r in range(N_HEADS):
        dst[:, _REL_OFF + _REL_PAD * r:_REL_OFF + _REL_PAD * (r + 1)] = fields[len(_VEC_FIELDS)][r:r + 1, :]


def _small_reduce(grads, loss_row, after=()):
    n_in = _N_FIELDS + 1

    def body(*refs):
        g_refs, loss_ref = refs[:_N_FIELDS], refs[_N_FIELDS]
        out_v, out_w = refs[n_in:n_in + 2]
        mine_v, mine_w, gath_v, gath_w, send_sems, recv_sems = refs[n_in + 2:]
        x, y, c, chips = _mesh_pos()
        me, sibling = (x, y, c), (x, y, 1 - c)

        peers_entered = _signal_peers("both")
        _assemble_row(mine_v, g_refs, True)
        mine_v[:, _LOSS_OFF:_LOSS_OFF + 128] = loss_ref[...]
        mine_w[...] = g_refs[-1][...].astype(BF)
        peers_entered()
        my_k = 4 * x + 2 * y + c
        gath_v[my_k] = mine_v[...]
        gath_w[my_k] = mine_w[...]

        def copy(k, gath, block, to, src=None):
            dst = gath.at[4 * block[0] + 2 * block[1] + block[2]]
            return pltpu.make_async_remote_copy(
                src_ref=dst if src is None else src, dst_ref=dst,
                send_sem=send_sems.at[k], recv_sem=recv_sems.at[k], device_id=to, device_id_type=MESH)

        bufs = ((gath_v, mine_v), (gath_w, mine_w))
        first, passed = [], []
        for b, (gath, mine) in enumerate(bufs):
            first.append(copy(7 * b, gath, me, sibling, src=mine))
            first += [copy(7 * b + 1 + j, gath, me, (*chip, c), src=mine) for j, chip in enumerate(chips)]
        for cp in first:
            cp.start()
        for b, (gath, _) in enumerate(bufs):
            for j, chip in enumerate(chips):
                copy(7 * b + 1 + j, gath, (*chip, c), me).wait_recv()
                cp = copy(7 * b + 4 + j, gath, (*chip, c), sibling)
                cp.start()
                passed.append(cp)
        for b, (gath, _) in enumerate(bufs):
            copy(7 * b, gath, sibling, me).wait_recv()
            for j, chip in enumerate(chips):
                copy(7 * b + 4 + j, gath, (*chip, 1 - c), me).wait_recv()
        for cp in first + passed:
            cp.wait_send()

        tot_v, tot_w = gath_v[0], gath_w[0].astype(F32)
        for k in range(1, 8):
            tot_v = tot_v + gath_v[k]
            tot_w = tot_w + gath_w[k].astype(F32)
        out_v[...] = tot_v
        out_w[...] = tot_w

    vm = pl.BlockSpec(memory_space=pltpu.VMEM)
    return pl.pallas_call(
        _after(body, n_in, after), name="small_reduce",
        out_shape=(jax.ShapeDtypeStruct((1, _NV), F32), jax.ShapeDtypeStruct((N_GROUPS * 128, 128), F32)),
        in_specs=[vm] * n_in + [_ANY] * len(after), out_specs=[vm] * 2,
        scratch_shapes=[pltpu.VMEM((1, _NV), F32), pltpu.VMEM((N_GROUPS * 128, 128), BF),
                        pltpu.VMEM((8, 1, _NV), F32), pltpu.VMEM((8, N_GROUPS * 128, 128), BF),
                        pltpu.SemaphoreType.DMA((14,)), pltpu.SemaphoreType.DMA((14,))],
        compiler_params=_params(32, collective_id=_PEER_SETS["both"]),
    )(*grads, loss_row, *after)


def _small_adamw(tot_v, tot_w, params):
    n_in = 2 + 3 * _N_FIELDS

    def body(*refs):
        tv_ref, tw_ref = refs[:2]
        p_refs = [refs[2 + k * _N_FIELDS:2 + (k + 1) * _N_FIELDS] for k in range(3)]
        outs = refs[n_in:n_in + 4 * _N_FIELDS + 1]
        wmv = refs[-1]
        for k in range(3):
            _assemble_row(wmv.at[k], p_refs[k], False)
            wmv[k, :, _LOSS_OFF:_LOSS_OFF + 128] = jnp.zeros((1, 128), F32)
        tot_v, tot_w = tv_ref[...], tw_ref[...]
        res_v = (tot_v,) + _adamw_math(wmv[0], tot_v, wmv[1], wmv[2])
        res_w = (tot_w,) + _adamw_math(p_refs[0][-1][...], tot_w, p_refs[1][-1][...], p_refs[2][-1][...])
        for kind in range(4):
            o = outs[kind * _N_FIELDS:(kind + 1) * _N_FIELDS]
            for f, (_, off, n) in enumerate(_VEC_FIELDS):
                o[f][...] = res_v[kind][:, off:off + n]
            for r in range(N_HEADS):
                o[len(_VEC_FIELDS)][r:r + 1, :] = res_v[kind][:, _REL_OFF + _REL_PAD * r:_REL_OFF + _REL_PAD * (r + 1)]
            o[-1][...] = res_w[kind]
        outs[-1][...] = tot_v[:, _LOSS_OFF:_LOSS_OFF + 128]

    field_shapes = [(1, n) for _, _, n in _VEC_FIELDS] + [(N_HEADS, _REL_PAD), (N_GROUPS * 128, 128)]
    vm = pl.BlockSpec(memory_space=pltpu.VMEM)
    operands = [tot_v, tot_w] + [a for p in params for a in p]
    assert len(operands) == n_in
    outs = pl.pallas_call(
        body, name="small_adamw",
        out_shape=tuple(jax.ShapeDtypeStruct(s, F32) for _ in range(4) for s in field_shapes)
        + (jax.ShapeDtypeStruct((1, 128), F32),),
        in_specs=[vm] * n_in, out_specs=[vm] * (4 * _N_FIELDS + 1),
        scratch_shapes=[pltpu.VMEM((3, 1, _NV), F32)],
        compiler_params=_params(32),
    )(*operands)
    return [outs[k * _N_FIELDS:(k + 1) * _N_FIELDS] for k in range(4)], outs[-1]


def _small_fields(norm_g, b_gate, ln_g, ln_b, b_s, final_g, rel_bias, w_s):
    rel = jnp.pad(rel_bias.reshape(N_HEADS, N_REL), ((0, 0), (0, _REL_PAD - N_REL)))
    return (norm_g, b_gate, ln_g, ln_b, b_s.reshape(1, N_GROUPS * 128), final_g.reshape(1, D_MODEL),
            rel, w_s.reshape(N_GROUPS * 128, 128))


def _small_outputs(fields):
    n_g, b_g, l_g, l_b, b_s, f_g, rel, w_s = fields
    return (n_g, b_g, rel[:, :N_REL].reshape(1, N_HEADS, N_REL), l_g, l_b,
            w_s.reshape(1, N_GROUPS, 128, 128), b_s.reshape(1, N_GROUPS, 128), f_g.reshape(D_MODEL))


def _bias_row(rel_bias):
    hi = rel_bias[:, N_REL - 1:N_REL]
    lo = rel_bias[:, 0:1]
    return jnp.concatenate([jnp.broadcast_to(hi, (N_HEADS, 384)), rel_bias[:, ::-1],
                            jnp.broadcast_to(lo, (N_HEADS, 191)), jnp.broadcast_to(hi, (N_HEADS, 192))], axis=1)


def kernel(x, norm_g, w_in, b_gate, rel_bias, sgu_ln_g, sgu_ln_b, w_s, b_s, w_pa, w_pb, w_out, final_g, loss_target, m_norm_g, m_w_in, m_b_gate, m_rel_bias, m_sgu_ln_g, m_sgu_ln_b, m_w_s, m_b_s, m_w_pa, m_w_pb, m_w_out, m_final_g, v_norm_g, v_w_in, v_b_gate, v_rel_bias, v_sgu_ln_g, v_sgu_ln_b, v_w_s, v_b_s, v_w_pa, v_w_pb, v_w_out, v_final_g):
    S = x.shape[1]
    xs = x.reshape(S, D_MODEL)
    tgt = loss_target.reshape(S, D_MODEL)
    big_w = (w_in[0], w_pa[0], w_pb[0], w_out[0])
    big_m = (m_w_in[0], m_w_pa[0], m_w_pb[0], m_w_out[0])
    big_v = (v_w_in[0], v_w_pa[0], v_w_pb[0], v_w_out[0])
    rel = rel_bias[0]
    ws = w_s[0]
    bst = b_s[0].T
    fg = final_g.reshape(1, D_MODEL)
    chip = 2 * lax.axis_index("x") + lax.axis_index("y")
    pos = jnp.stack([lax.axis_index("c"), chip] + [(chip + k) % N_SHARD for k in range(1, N_SHARD)]).astype(jnp.int32)

    (w_in_bf,), staged, band_bias = _ag_weights((0,), big_w[:1], (1, 2, 3), big_w[1:], _bias_row(rel))
    ag_s = _split_start("ag_small_start", staged, 9, _gather_copies((1, 2, 3)), "chips", after=(w_in_bf,))

    ht, q3, k3, v3, zrest = _inproj_fwd(xs, norm_g, w_in_bf, after=(ag_s.token,))
    att, lse = _attn_fwd(q3, k3, v3, band_bias)
    w_pa_bf, w_pb_bf, w_out_bf = _split_wait("ag_small_wait", ag_s, _gather_copies((1, 2, 3)), att)
    (d_out, d_att, dzt, dzs, gw_out, gw_pa, gw_pb, g_bgate, g_final, loss_row,
     g_ws, g_bs_t, g_lng, g_lnb) = _tail_sgu(
        att, zrest, xs, tgt, w_pa_bf, w_pb_bf, w_out_bf, b_gate, fg, sgu_ln_g, sgu_ln_b, ws, bst)
    ws_s, ws_i = (1, 2, 3), (0,)

    x1s = _split_start("gx1s_start", [gw_pa, gw_pb, gw_out] + _x1_lands(ws_s), 12, _x1_copies(ws_s), "sibling")
    dq, dk, dv, d_gp = _attn_bwd(q3, k3, v3, d_att, lse, band_bias, after=(x1s.token,))
    got = _split_wait("gx1s_wait", x1s, _x1_copies(ws_s), dq)
    own_s, csb_s = _grad_add1_group(ws_s, got[:3], got[3:], pos)

    x2s = _split_start("gx2s_start", csb_s + _x2_lands(ws_s), 9, _x2_copies(3), "chips")
    gw_in, gw_in_bf = _gw_in(ht, dq, dk, dv, dzt, dzs, after=(x2s.token,))
    x1i = _split_start("gx1i_start", [gw_in_bf] + _x1_lands(ws_i, BF), 4, _x1_copies(ws_i), "sibling")
    got = _split_wait("gx2s_wait", x2s, _x2_copies(3), x1i.token)
    halves_s = _grad_add2_group(ws_s, own_s, got[3:])
    x3s = _split_start("gx3s_start", halves_s, 3, _x3_copies(ws_s), "sibling")
    g_rel = jnp.pad(d_gp[:, 384:384 + N_REL][:, ::-1], ((0, 0), (0, _REL_PAD - N_REL)))
    small_params = (_small_fields(norm_g, b_gate, sgu_ln_g, sgu_ln_b, b_s, final_g, rel_bias, w_s),
                    _small_fields(m_norm_g, m_b_gate, m_sgu_ln_g, m_sgu_ln_b, m_b_s, m_final_g, m_rel_bias, m_w_s),
                    _small_fields(v_norm_g, v_b_gate, v_sgu_ln_g, v_sgu_ln_b, v_b_s, v_final_g, v_rel_bias, v_w_s))
    relayouts = (g_rel,) + tuple(fields[-2] for fields in small_params)
    recv1_i = _split_wait("gx1i_wait", x1i, _x1_copies(ws_i), (x3s.token,) + relayouts)[1]
    csb_i = _grad_add1(0, gw_in, recv1_i, pos)

    x2i = _split_start("gx2i_start", [csb_i] + _x2_lands(ws_i), 3, _x2_copies(1), "chips")
    grad_x, g_norm = _dh_gradx(dq, dk, dv, dzt, dzs, w_in_bf, xs, norm_g, d_out, after=(x2i.token,))
    g_shards_s = _split_wait("gx3s_wait", x3s, _x3_copies(ws_s), grad_x)
    got = _split_wait("gx2i_wait", x2i, _x2_copies(1), grad_x)
    half_i = _grad_add2(0, gw_in, recv1_i, got[1], pos)
    x3i = _split_start("gx3i_start", [half_i], 1, _x3_copies(ws_i), "sibling")

    small_grads = (g_norm, g_bgate, g_lng, g_lnb, g_bs_t, g_final, g_rel, g_ws.reshape(N_GROUPS * 128, 128))
    tot_v, tot_w = _small_reduce(small_grads, loss_row, after=(x3i.token,))
    (gsum, sdelta, sm, sv), loss_out = _small_adamw(tot_v, tot_w, small_params)

    g_shard_i, = _split_wait("gx3i_wait", x3i, _x3_copies(ws_i), loss_out)
    big = _adamw(big_w, [g_shard_i] + g_shards_s, big_m, big_v)
    sg_out, sd_out, sm_out, sv_out = (_small_outputs(f) for f in (gsum, sdelta, sm, sv))
    loss = loss_out[0, 0]

    def assemble(small, bigs):
        n_g, b_g, r_b, l_g, l_b, w_s_, b_s_, f_g = small
        b_in, b_pa, b_pb, b_out = (b[None] for b in bigs)
        return (n_g, b_in, b_g, r_b, l_g, l_b, w_s_, b_s_, b_pa, b_pb, b_out, f_g)

    grads_out = assemble(sg_out, [b[3] for b in big])
    delta_out = assemble(sd_out, [b[0] for b in big])
    m_out = assemble(sm_out, [b[1] for b in big])
    v_out = assemble(sv_out, [b[2] for b in big])
    return (loss, grad_x.reshape(1, S, D_MODEL), *grads_out, *delta_out, *m_out, *v_out)
```

```python
import functools
import math

import jax
import jax.numpy as jnp
from jax import lax
from jax.experimental import pallas as pl
from jax.experimental.pallas import tpu as pltpu

F32 = jnp.float32
BF = jnp.bfloat16
MESH = pl.DeviceIdType.MESH

D_MODEL = 1024
D_A = 512
D_B = 512
D_IN = 5632
N_HEADS = 8
HEAD_DIM = 64
CHUNK = 64
N_PREV = 8
SGU_CHUNK = 128
N_GROUPS = 4
N_REL = 257
EPS = 1e-6
NEG_INF = -1e30
SCALE = HEAD_DIM ** -0.5

QB = 2 * CHUNK
KB = (N_PREV + 2) * CHUNK
PADK = N_PREV * CHUNK
ROLL_W = 1024
KEEP = KB // QB - 1
Q_PER_STEP = 2

ADAM_LR = 0.001
ADAM_B1 = 0.9
ADAM_B2 = 0.999
ADAM_EPS = 1e-08
ADAM_WD = 0.01
ADAM_STEP = 10
ADAM_C1 = 1.0 - ADAM_B1 ** ADAM_STEP
ADAM_C2 = 1.0 - ADAM_B2 ** ADAM_STEP

AG_PIECES = 4
N_SHARD = 4
SHARD_IN = D_IN // N_SHARD
MIB = 1024 * 1024


V7X_VMEM_MIB = 64
VMEM_RESERVE_MIB = V7X_VMEM_MIB - 4


def _params(vmem_mib, **kw):
    assert vmem_mib <= VMEM_RESERVE_MIB
    return pltpu.CompilerParams(vmem_limit_bytes=VMEM_RESERVE_MIB * MIB, **kw)


def _sigmoid(x):
    return 1.0 / (1.0 + jnp.exp(-x))


def _silu_and_grad(x):
    s = _sigmoid(x)
    return x * s, s * (1.0 + x * (1.0 - s))


_GELU_C = math.sqrt(2.0 / math.pi)
_GELU_A = 0.044715


def _gelu_and_grad(x):
    x2 = x * x
    t = jnp.tanh(_GELU_C * (x + _GELU_A * (x2 * x)))
    cdf = 0.5 * (1.0 + t)
    grad = cdf + 0.5 * x * (1.0 - t * t) * (_GELU_C * (1.0 + 3.0 * _GELU_A * x2))
    return x * cdf, grad


def _dot(a, b):
    return jnp.dot(a, b, preferred_element_type=F32)


def _dot_nt(a, b):
    return lax.dot_general(a, b, (((1,), (1,)), ((), ())), preferred_element_type=F32)


def _dot_tn(a, b):
    return lax.dot_general(a, b, (((0,), (0,)), ((), ())), preferred_element_type=F32)


def _mo(v, m):
    return v if isinstance(v, int) else pl.multiple_of(v, m)


def _unit_in(ref, s, p):
    return ref.at[pl.ds(_mo(p * 512, 512), 512), pl.ds(_mo(s * SHARD_IN, 128), SHARD_IN)]


def _unit_p(ref, s, p):
    return ref.at[pl.ds(_mo(p * 256, 256), 256), pl.ds(_mo(s * 256, 128), 256)]


def _unit_out(ref, s, p):
    return ref.at[pl.ds(_mo(s * 256 + p * 128, 128), 128), :]


_UNITS = (_unit_in, _unit_p, _unit_p, _unit_out)
_HALF_ROWS = (512, 256, 256, 128)
_UNIT_SHAPES = ((512, SHARD_IN), (256, 256), (256, 256), (128, D_MODEL))
_FULL_SHAPES = ((D_MODEL, D_IN), (D_A, D_MODEL), (D_B, D_MODEL), (D_MODEL, D_MODEL))
_SHARD_SHAPES = ((D_MODEL, SHARD_IN), (D_A, 256), (D_B, 256), (256, D_MODEL))


def _mesh_pos():
    x, y, c = lax.axis_index("x"), lax.axis_index("y"), lax.axis_index("c")
    chips = [(1 - x, y), (x, 1 - y), (1 - x, 1 - y)]
    return x, y, c, chips


def _bias_rows(rel_ref, pad_ref):
    pad_ref[...] = jnp.zeros(pad_ref.shape, F32)
    pad_ref[:, :N_REL] = rel_ref[...]
    r = pad_ref[...]
    m = lax.broadcasted_iota(jnp.int32, (pad_ref.shape[1], ROLL_W), 1)
    m = jnp.where(m >= ROLL_W - 192, m - ROLL_W, m)
    pick = (lax.broadcasted_iota(jnp.int32, m.shape, 0) == jnp.clip(512 - m, -128, 128) + 128).astype(BF)
    hi = r.astype(BF)
    mid = (r - hi.astype(F32)).astype(BF)
    lo = ((r - hi.astype(F32)) - mid.astype(F32)).astype(BF)
    return (_dot(hi, pick) + _dot(mid, pick)) + _dot(lo, pick)


def _ag_weights(ws, shards, later_ws, later_shards, rel):
    n, m = len(ws), len(later_ws)

    def body(*refs):
        ins, later_ins, rel_ref = refs[:n], refs[n:n + m], refs[n + m]
        o = n + m + 1
        outs, later_outs, bias_ref = refs[o:o + n], refs[o + n:o + n + m], refs[o + n + m]
        o += n + m + 1
        stage, later_stage = refs[o:o + n], refs[o + n:o + n + m]
        send_sems, recv_sems, local_sems, later_sems, rel_pad, gp_ref = refs[o + n + m:]
        x, y, c, chips = _mesh_pos()
        s_me = 2 * x + y
        sibling = (x, y, 1 - c)
        def rows_of(k, p):
            rows = _HALF_ROWS[ws[k]]
            return pl.ds(_mo(p * rows, rows), rows)

        def half(k, p):
            return stage[k].at[rows_of(k, p), :]

        def unit(k, s, p):
            return _UNITS[ws[k]](outs[k], s, p)

        def rcopy(k, i, src, dst, to):
            return pltpu.make_async_remote_copy(src_ref=src, dst_ref=dst, send_sem=send_sems.at[k, i],
                                                recv_sem=recv_sems.at[k, i], device_id=to, device_id_type=MESH)

        peers_entered = _signal_peers("both")
        for k in range(n):
            stage[k][rows_of(k, c), :] = ins[k][rows_of(k, c), :].astype(BF)
        peers_entered()
        def piece(ref, k, q):
            rows = _HALF_ROWS[ws[k]] // AG_PIECES
            return ref.at[pl.ds(q * rows, rows), :]

        sends = []
        for q in range(AG_PIECES):
            for j, (cx, cy) in enumerate(chips):
                for k in range(n):
                    cp = rcopy(k, j * AG_PIECES + q, piece(half(k, c), k, q), piece(unit(k, s_me, c), k, q),
                               (cx, cy, c))
                    cp.start()
                    sends.append(cp)
        for k in range(n):
            stage[k][rows_of(k, 1 - c), :] = ins[k][rows_of(k, 1 - c), :].astype(BF)
        local = []
        for k in range(n):
            for p in range(2):
                cp = pltpu.make_async_copy(half(k, p), unit(k, s_me, p), local_sems.at[k, p])
                cp.start()
                local.append(cp)
        for k, w in enumerate(later_ws):
            later_stage[k][...] = later_ins[k][...].astype(BF)
            cp = pltpu.make_async_copy(later_stage[k], _shard_of(later_outs[k], w, s_me), later_sems.at[k])
            cp.start()
            local.append(cp)
        keep = _struct_mask()
        gp_ref[...] = _bias_rows(rel_ref, rel_pad)
        for h in range(N_HEADS):
            bias_ref[h] = jnp.where(keep, _skew_table(gp_ref[h:h + 1, :])[:, :KB], NEG_INF)
        for q in range(AG_PIECES):
            for j, (cx, cy) in enumerate(chips):
                for k in range(n):
                    landed = piece(unit(k, 2 * cx + cy, c), k, q)
                    rcopy(k, j * AG_PIECES + q, landed, landed, (cx, cy, c)).wait_recv()
                    cp = rcopy(k, (3 + j) * AG_PIECES + q, landed, landed, sibling)
                    cp.start()
                    sends.append(cp)
        for q in range(AG_PIECES):
            for j, (cx, cy) in enumerate(chips):
                for k in range(n):
                    other = piece(unit(k, 2 * cx + cy, 1 - c), k, q)
                    rcopy(k, (3 + j) * AG_PIECES + q, other, other, sibling).wait_recv()
        for cp in sends:
            cp.wait_send()
        for cp in local:
            cp.wait()

    vm = pl.BlockSpec(memory_space=pltpu.VMEM)
    outs = pl.pallas_call(
        body, name="ag_weights",
        out_shape=tuple(jax.ShapeDtypeStruct(_FULL_SHAPES[w], BF) for w in tuple(ws) + tuple(later_ws))
        + (jax.ShapeDtypeStruct((N_HEADS, QB, KB), F32),),
        in_specs=[vm] * (n + m + 1), out_specs=[_ANY] * (n + m) + [vm],
        scratch_shapes=[pltpu.VMEM(_SHARD_SHAPES[w], BF) for w in tuple(ws) + tuple(later_ws)]
        + [pltpu.SemaphoreType.DMA((n, 6 * AG_PIECES)), pltpu.SemaphoreType.DMA((n, 6 * AG_PIECES)),
           pltpu.SemaphoreType.DMA((n, 2)), pltpu.SemaphoreType.DMA((m,)),
           pltpu.VMEM((N_HEADS, _REL_PAD), F32), pltpu.VMEM((N_HEADS, ROLL_W), F32)],
        compiler_params=_params(48, collective_id=_PEER_SETS["both"]),
    )(*shards, *later_shards, rel)
    return list(outs[:n]), list(outs[n:n + m]), outs[-1]


def _shard_of(ref, w, s):
    if w == 0:
        return ref.at[:, pl.ds(_mo(s * SHARD_IN, 128), SHARD_IN)]
    if w == 3:
        return ref.at[pl.ds(_mo(s * 256, 256), 256), :]
    return ref.at[:, pl.ds(_mo(s * 256, 128), 256)]


def _gather_copies(ws):
    def copies(refs, send_sems, recv_sems):
        x, y, c, chips = _mesh_pos()
        out = []
        for j, (cx, cy) in enumerate(chips):
            for k, w in enumerate(ws):
                mine = _shard_of(refs[k], w, 2 * x + y)
                out.append(pltpu.make_async_remote_copy(
                    src_ref=mine, dst_ref=mine, send_sem=send_sems.at[3 * k + j], recv_sem=recv_sems.at[3 * k + j],
                    device_id=(cx, cy, c), device_id_type=MESH))
        return out
    return copies


def _inproj_fwd(x, norm_g, w_in_bf, tm=512, after=()):
    S = x.shape[0]

    def body(x_ref, g_ref, w_ref, ht_ref, q_ref, k_ref, v_ref, zr_ref):
        xv = x_ref[...]
        r = lax.rsqrt(jnp.mean(xv * xv, axis=-1, keepdims=True) + EPS)
        hf = (xv * r) * g_ref[...]
        ht_ref[...] = hf.T.astype(BF)
        h = hf.astype(BF)
        heads = (q_ref, k_ref, v_ref)
        for j in range(D_IN // 512):
            z = _dot(h, w_ref[:, j * 512:(j + 1) * 512])
            if j < 3:
                zb = z.astype(BF)
                for hd in range(N_HEADS):
                    heads[j][hd] = zb[:, hd * HEAD_DIM:(hd + 1) * HEAD_DIM]
            else:
                zr_ref[:, (j - 3) * 512:(j - 2) * 512] = z

    head_major = jax.ShapeDtypeStruct((N_HEADS, S, HEAD_DIM), BF)
    head_spec = pl.BlockSpec((N_HEADS, tm, HEAD_DIM), lambda i: (0, i, 0))
    return pl.pallas_call(
        _after(body, 3, after), name="inproj_fwd", grid=(S // tm,),
        out_shape=(jax.ShapeDtypeStruct((D_MODEL, S), BF), head_major, head_major, head_major,
                   jax.ShapeDtypeStruct((S, D_IN - 3 * D_A), F32)),
        in_specs=[pl.BlockSpec((tm, D_MODEL), lambda i: (i, 0)),
                  pl.BlockSpec((1, D_MODEL), lambda i: (0, 0)),
                  pl.BlockSpec((D_MODEL, D_IN), lambda i: (0, 0), pipeline_mode=pl.Buffered(1))]
        + [_ANY] * len(after),
        out_specs=[pl.BlockSpec((D_MODEL, tm), lambda i: (0, i)),
                   head_spec, head_spec, head_spec,
                   pl.BlockSpec((tm, D_IN - 3 * D_A), lambda i: (i, 0))],
        compiler_params=_params(52, dimension_semantics=("arbitrary",)),
    )(x, norm_g, w_in_bf, *after)


def _skew_table(gp_row):
    row = lax.broadcasted_iota(jnp.int32, (QB, ROLL_W), 0)
    t = jnp.broadcast_to(gp_row, (QB, ROLL_W))
    for b in range(7):
        t = jnp.where(((row >> b) & 1) == 1, pltpu.roll(t, 1 << b, axis=1), t)
    return t


def _unskew_sum(d):
    half = QB // 2
    while half >= 8:
        d = d[0:half] + pltpu.roll(d[half:2 * half], ROLL_W - half, axis=1)
        half //= 2
    row = lax.broadcasted_iota(jnp.int32, (8, ROLL_W), 0)
    for b in range(3):
        d = jnp.where(((row >> b) & 1) == 1, pltpu.roll(d, ROLL_W - (1 << b), axis=1), d)
    return jnp.sum(d, axis=0, keepdims=True)


def _struct_mask():
    a = lax.broadcasted_iota(jnp.int32, (QB, KB), 0) // CHUNK
    b = lax.broadcasted_iota(jnp.int32, (QB, KB), 1) // CHUNK
    return (b >= a) & (b <= a + N_PREV)


def _load_kv(k_hbm, v_hbm, k_scr, v_scr, sems, S, meanwhile=lambda: None):
    zeros = jnp.zeros((N_HEADS, PADK, HEAD_DIM), BF)
    k_scr[:, 0:PADK, :] = zeros
    v_scr[:, 0:PADK, :] = zeros
    ck = pltpu.make_async_copy(k_hbm, k_scr.at[:, pl.ds(PADK, S), :], sems.at[0])
    cv = pltpu.make_async_copy(v_hbm, v_scr.at[:, pl.ds(PADK, S), :], sems.at[1])
    ck.start()
    cv.start()
    meanwhile()
    ck.wait()
    cv.wait()


_BATCH_NT = (((2,), (2,)), ((0,), (0,)))
_BATCH_NN = (((2,), (1,)), ((0,), (0,)))
_BATCH_TN = (((1,), (1,)), ((0,), (0,)))


def _bdot(a, b, dims):
    return lax.dot_general(a, b, dims, preferred_element_type=F32)


def _scaled(q):
    return q * jnp.asarray(SCALE, BF)


def _scores(qs, kb, bias, i, front):
    s = _bdot(qs, kb, _BATCH_NT) + bias
    if front:
        col = lax.broadcasted_iota(jnp.int32, (1, 1, KB), 2)
        s = jnp.where(col >= PADK - i * QB, s, NEG_INF)
    return s


def _attn_fwd(q3, k3, v3, bias):
    S = q3.shape[1]

    def body(q_ref, k_hbm, v_hbm, bias_ref, o_ref, lse_ref, k_scr, v_scr, sems):
        @pl.when(pl.program_id(0) == 0)
        def _():
            _load_kv(k_hbm, v_hbm, k_scr, v_scr, sems, S)

        def step(i, rows, front):
            start = pl.multiple_of(i * QB, QB)
            kb = k_scr[:, pl.ds(start, KB), :]
            vb = v_scr[:, pl.ds(start, KB), :]
            s = _scores(_scaled(q_ref[:, rows, :]), kb, bias_ref[...], i, front)
            m = jnp.max(s, axis=-1, keepdims=True)
            e = jnp.exp(s - m)
            l = jnp.sum(e, axis=-1, keepdims=True)
            p = e * (1.0 / l)
            o = _bdot(p.astype(BF), vb, _BATCH_NN)
            lse_ref[:, rows, :] = jnp.broadcast_to(m + jnp.log(l), (N_HEADS, QB, 128))
            for h in range(N_HEADS):
                o_ref[rows, h * HEAD_DIM:(h + 1) * HEAD_DIM] = o[h]

        def block(j, carry):
            i = pl.program_id(0) * Q_PER_STEP + j
            rows = pl.ds(pl.multiple_of(j * QB, QB), QB)
            pl.when(i < KEEP)(functools.partial(step, i, rows, True))
            pl.when(i >= KEEP)(functools.partial(step, i, rows, False))
            return carry

        lax.fori_loop(0, Q_PER_STEP, block, 0)

    rows_per_step = Q_PER_STEP * QB
    kv_scr = pltpu.VMEM((N_HEADS, S + PADK, HEAD_DIM), BF)
    return pl.pallas_call(
        body, name="attn_fwd", grid=(S // rows_per_step,),
        out_shape=(jax.ShapeDtypeStruct((S, D_A), F32), jax.ShapeDtypeStruct((N_HEADS, S, 128), F32)),
        in_specs=[pl.BlockSpec((N_HEADS, rows_per_step, HEAD_DIM), lambda g: (0, g, 0)),
                  pl.BlockSpec(memory_space=pl.ANY), pl.BlockSpec(memory_space=pl.ANY),
                  pl.BlockSpec((N_HEADS, QB, KB), lambda g: (0, 0, 0))],
        out_specs=[pl.BlockSpec((rows_per_step, D_A), lambda g: (g, 0)),
                   pl.BlockSpec((N_HEADS, rows_per_step, 128), lambda g: (0, g, 0))],
        scratch_shapes=[kv_scr, kv_scr, pltpu.SemaphoreType.DMA((2,))],
        compiler_params=_params(48, dimension_semantics=("arbitrary",)),
    )(q3, k3, v3, bias)


def _attn_bwd(q3, k3, v3, d_att3, lse, bias, after=()):
    S = q3.shape[1]
    nq = S // QB

    def body(q_ref, do_ref, k_hbm, v_hbm, lse_ref, bias_ref, dq_ref, dk_ref, dv_ref, dgp_ref,
             k_scr, v_scr, dk_acc, dv_acc, dbias_acc, pad_scr, sems):
        @pl.when(pl.program_id(0) == 0)
        def _():
            def clear():
                dk_acc[...] = jnp.zeros_like(dk_acc)
                dv_acc[...] = jnp.zeros_like(dv_acc)
                dbias_acc[...] = jnp.zeros_like(dbias_acc)
            _load_kv(k_hbm, v_hbm, k_scr, v_scr, sems, S, clear)

        def step(i, rows, front):
            start = pl.multiple_of(i * QB, QB)
            kb = k_scr[:, pl.ds(start, KB), :]
            vb = v_scr[:, pl.ds(start, KB), :]
            qs = _scaled(q_ref[:, rows, :])
            do = do_ref[:, rows, :]
            p = jnp.exp(_scores(qs, kb, bias_ref[...], i, front) - jnp.tile(lse_ref[:, rows, :], (1, 1, KB // 128)))
            dp = _bdot(do, vb, _BATCH_NT)
            ds = p * (dp - jnp.sum(dp * p, axis=-1, keepdims=True))
            dbias_acc[...] += ds
            dsb = ds.astype(BF)
            dq = _bdot(dsb, kb, _BATCH_NN) * SCALE
            for h in range(N_HEADS):
                dq_ref[rows, h * HEAD_DIM:(h + 1) * HEAD_DIM] = dq[h].astype(BF)
            dk_acc[...] += _bdot(dsb, qs, _BATCH_TN)
            dv_acc[...] += _bdot(p.astype(BF), do, _BATCH_TN)

        def block(j, carry):
            i = pl.program_id(0) * Q_PER_STEP + j
            rows = pl.ds(pl.multiple_of(j * QB, QB), QB)
            pl.when(i < KEEP)(functools.partial(step, i, rows, True))
            pl.when((i >= KEEP) & (i < nq))(functools.partial(step, i, rows, False))
            for h in range(N_HEADS):
                hs = slice(h * HEAD_DIM, (h + 1) * HEAD_DIM)
                dk_ref[rows, hs] = dk_acc[h, 0:QB, :].astype(BF)
                dv_ref[rows, hs] = dv_acc[h, 0:QB, :].astype(BF)
            dk_acc[:, 0:KB - QB, :] = dk_acc[:, QB:KB, :]
            dv_acc[:, 0:KB - QB, :] = dv_acc[:, QB:KB, :]
            dk_acc[:, KB - QB:KB, :] = jnp.zeros((N_HEADS, QB, HEAD_DIM), F32)
            dv_acc[:, KB - QB:KB, :] = jnp.zeros((N_HEADS, QB, HEAD_DIM), F32)
            return carry

        lax.fori_loop(0, Q_PER_STEP, block, 0)

        @pl.when(pl.program_id(0) == n_steps - 1)
        def _():
            lane = lax.broadcasted_iota(jnp.int32, (1, ROLL_W), 1)
            hi = (lane < 384) | (lane >= 832)
            lo = (lane > 640) & (lane < 832)
            pad_scr[...] = jnp.zeros_like(pad_scr)
            for h in range(N_HEADS):
                pad_scr[:, 0:KB] = dbias_acc[h]
                g = _unskew_sum(pad_scr[...])
                s_hi = jnp.sum(jnp.where(hi, g, 0.0), axis=-1, keepdims=True)
                s_lo = jnp.sum(jnp.where(lo, g, 0.0), axis=-1, keepdims=True)
                g = jnp.where(lane == 384, g + s_hi, g)
                g = jnp.where(lane == 640, g + s_lo, g)
                dgp_ref[h:h + 1, :] = g

    assert nq % Q_PER_STEP == 0 and KEEP % Q_PER_STEP == 0
    rows_per_step = Q_PER_STEP * QB
    n_steps = (nq + KEEP) // Q_PER_STEP
    last = nq // Q_PER_STEP - 1
    lag = KEEP // Q_PER_STEP
    kv_scr = pltpu.VMEM((N_HEADS, S + PADK, HEAD_DIM), BF)
    return pl.pallas_call(
        _after(body, 6, after), name="attn_bwd", grid=(n_steps,),
        out_shape=(jax.ShapeDtypeStruct((S, D_A), BF), jax.ShapeDtypeStruct((S, D_A), BF),
                   jax.ShapeDtypeStruct((S, D_A), BF), jax.ShapeDtypeStruct((N_HEADS, ROLL_W), F32)),
        in_specs=[pl.BlockSpec((N_HEADS, rows_per_step, HEAD_DIM), lambda g: (0, jnp.minimum(g, last), 0)),
                  pl.BlockSpec((N_HEADS, rows_per_step, HEAD_DIM), lambda g: (0, jnp.minimum(g, last), 0)),
                  pl.BlockSpec(memory_space=pl.ANY), pl.BlockSpec(memory_space=pl.ANY),
                  pl.BlockSpec((N_HEADS, rows_per_step, 128), lambda g: (0, jnp.minimum(g, last), 0)),
                  pl.BlockSpec((N_HEADS, QB, KB), lambda g: (0, 0, 0))] + [_ANY] * len(after),
        out_specs=[pl.BlockSpec((rows_per_step, D_A), lambda g: (jnp.minimum(g, last), 0)),
                   pl.BlockSpec((rows_per_step, D_A), lambda g: (jnp.maximum(g - lag, 0), 0)),
                   pl.BlockSpec((rows_per_step, D_A), lambda g: (jnp.maximum(g - lag, 0), 0)),
                   pl.BlockSpec((N_HEADS, ROLL_W), lambda g: (0, 0))],
        scratch_shapes=[kv_scr, kv_scr,
                        pltpu.VMEM((N_HEADS, KB, HEAD_DIM), F32), pltpu.VMEM((N_HEADS, KB, HEAD_DIM), F32),
                        pltpu.VMEM((N_HEADS, QB, KB), F32), pltpu.VMEM((QB, ROLL_W), F32),
                        pltpu.SemaphoreType.DMA((2,))],
        compiler_params=_params(56, dimension_semantics=("arbitrary",)),
    )(q3, d_att3, k3, v3, lse, bias, *after)


def _sgu_core(ub, vb, lg, lb):
    u, du = _gelu_and_grad(ub)
    v, dv = _gelu_and_grad(vb)
    mu = jnp.mean(v, axis=-1, keepdims=True)
    vc = v - mu
    rstd = lax.rsqrt(jnp.mean(vc * vc, axis=-1, keepdims=True) + EPS)
    xh = vc * rstd
    vn = xh * lg + lb
    return u, du, dv, rstd, xh, vn


def _tri():
    r = lax.broadcasted_iota(jnp.int32, (SGU_CHUNK, SGU_CHUNK), 0)
    c = lax.broadcasted_iota(jnp.int32, (SGU_CHUNK, SGU_CHUNK), 1)
    return r >= c


def _tail_sgu(att, zrest, x, target, w_pa, w_pb, w_out, b_gate, final_g, ln_g, ln_b, w_s, b_s_t, tm=256):
    S = x.shape[0]
    nt = S // tm
    chunks = tm // SGU_CHUNK

    def body(att_ref, ga_ref, ub_ref, vb_ref, gb_ref, gta_ref, gtb_ref, x_ref, t_ref,
             wpa_ref, wpb_ref, wout_ref, bg_ref, fg_ref, lg_ref, lb_ref, ws_ref, bst_ref,
             dout_ref, datt_ref, dzt_ref, dzs_ref, gwout_hbm, gwpa_hbm, gwpb_hbm,
             gbg_ref, gfg_ref, loss_ref, gws_ref, gbs_ref, glg_ref, glb_ref,
             acc_out, acc_pa, acc_pb, sg_scr, mix_scr, dvn_scr, bs_acc, sems):
        i = pl.program_id(0)

        @pl.when(i == 0)
        def _():
            for r in (acc_out, acc_pa, acc_pb, gbg_ref, gfg_ref, loss_ref, gws_ref, glg_ref, glb_ref, bs_acc):
                r[...] = jnp.zeros_like(r)

        u, du, dv, rstd, xh, vn = _sgu_core(ub_ref[...], vb_ref[...], lg_ref[...], lb_ref[...])
        vnb = vn.astype(BF)
        tri = _tri()
        blocks = [(g, slice(n * SGU_CHUNK, (n + 1) * SGU_CHUNK), slice(g * 128, (g + 1) * 128))
                  for g in range(N_GROUPS) for n in range(chunks)]
        wts = [jnp.where(tri, ws_ref[g], 0.0) for g in range(N_GROUPS)]
        for g, rs, cs in blocks:
            mixed = _dot(wts[g].astype(BF), vnb[rs, cs]) + bst_ref[:, g:g + 1]
            mix_scr[rs, cs] = mixed
            sg_scr[rs, cs] = u[rs, cs] * mixed

        att = att_ref[...]
        sg = sg_scr[...]
        sa, dsa = _silu_and_grad(ga_ref[...])
        sb, dsb = _silu_and_grad(gb_ref[...])
        ya = (att * sa).astype(BF)
        yb = (sg * sb).astype(BF)
        pa = _dot(ya, wpa_ref[...])
        pb = _dot(yb, wpb_ref[...])
        ga = _sigmoid(gta_ref[...] + bg_ref[:, 0:D_MODEL])
        gb = _sigmoid(gtb_ref[...] + bg_ref[:, D_MODEL:2 * D_MODEL])
        merged = (ga * pa + gb * pb).astype(BF)
        out = x_ref[...] + _dot(merged, wout_ref[...])
        r2 = lax.rsqrt(jnp.mean(out * out, axis=-1, keepdims=True) + EPS)
        nrm = out * r2
        fg = fg_ref[...]
        err = nrm * fg - t_ref[...]
        loss_ref[...] += 0.5 * jnp.sum(jnp.mean(err * err, axis=-1, keepdims=True))
        dy = err * (1.0 / D_MODEL)
        gfg_ref[...] += jnp.sum(dy * nrm, axis=0, keepdims=True)
        dn = dy * fg
        d_out = r2 * (dn - nrm * jnp.mean(dn * nrm, axis=-1, keepdims=True))
        dout_ref[...] = d_out
        d_outb = d_out.astype(BF)
        acc_out[...] += _dot_tn(merged, d_outb)
        dm = _dot_nt(d_outb, wout_ref[...])
        d_pa = (dm * ga).astype(BF)
        d_pb = (dm * gb).astype(BF)
        d_gta = dm * pa * (ga * (1.0 - ga))
        d_gtb = dm * pb * (gb * (1.0 - gb))
        gbg_ref[:, 0:D_MODEL] += jnp.sum(d_gta, axis=0, keepdims=True)
        gbg_ref[:, D_MODEL:2 * D_MODEL] += jnp.sum(d_gtb, axis=0, keepdims=True)
        dzt_ref[:, 2 * D_A:2 * D_A + D_MODEL] = d_gta.astype(BF)
        dzt_ref[:, 2 * D_A + D_MODEL:] = d_gtb.astype(BF)
        acc_pa[...] += _dot_tn(ya, d_pa)
        acc_pb[...] += _dot_tn(yb, d_pb)
        d_ya = _dot_nt(d_pa, wpa_ref[...])
        d_yb = _dot_nt(d_pb, wpb_ref[...])
        d_att = (d_ya * sa).astype(BF)
        for hd in range(N_HEADS):
            datt_ref[hd] = d_att[:, hd * HEAD_DIM:(hd + 1) * HEAD_DIM]
        dzt_ref[:, 0:D_A] = (d_ya * att * dsa).astype(BF)
        dzt_ref[:, D_A:2 * D_A] = (d_yb * sg * dsb).astype(BF)

        dsg = d_yb * sb
        dzs_ref[:, 0:D_B] = (dsg * mix_scr[...] * du).astype(BF)
        dmix = dsg * u
        for g, rs, cs in blocks:
            dmb = dmix[rs, cs].astype(BF)
            bs_acc[:, cs] += dmix[rs, cs]
            gws_ref[g] += _dot_nt(dmb, vnb[rs, cs])
            dvn_scr[rs, cs] = _dot(wts[g].T.astype(BF), dmb)
        dvn = dvn_scr[...]
        glg_ref[...] += jnp.sum(dvn * xh, axis=0, keepdims=True)
        glb_ref[...] += jnp.sum(dvn, axis=0, keepdims=True)
        dxh = dvn * lg_ref[...]
        dvv = rstd * (dxh - jnp.mean(dxh, axis=-1, keepdims=True)
                      - xh * jnp.mean(dxh * xh, axis=-1, keepdims=True))
        dzs_ref[:, D_B:2 * D_B] = (dvv * dv).astype(BF)

        @pl.when(i == nt - 1)
        def _():
            cps = [pltpu.make_async_copy(acc_out, gwout_hbm, sems.at[0]),
                   pltpu.make_async_copy(acc_pa, gwpa_hbm, sems.at[1]),
                   pltpu.make_async_copy(acc_pb, gwpb_hbm, sems.at[2])]
            for cp in cps:
                cp.start()
            lane = lax.broadcasted_iota(jnp.int32, (SGU_CHUNK, 128), 1)
            cols = jnp.zeros((SGU_CHUNK, 128), F32)
            for g in range(N_GROUPS):
                gws_ref[g] = jnp.where(tri, gws_ref[g], 0.0)
                col = jnp.sum(bs_acc[:, g * 128:(g + 1) * 128], axis=-1, keepdims=True)
                cols = jnp.where(lane == g, col, cols)
            gbs_ref[...] = cols
            for cp in cps:
                cp.wait()

    c2 = lambda i: (0, 0)
    c3 = lambda i: (0, 0, 0)
    zcol = lambda w, blk: pl.BlockSpec((tm, w), lambda i: (i, blk))
    row = lambda w: pl.BlockSpec((tm, w), lambda i: (i, 0))
    return pl.pallas_call(
        body, name="tail", grid=(nt,),
        out_shape=(jax.ShapeDtypeStruct((S, D_MODEL), F32), jax.ShapeDtypeStruct((N_HEADS, S, HEAD_DIM), BF),
                   jax.ShapeDtypeStruct((S, 3072), BF), jax.ShapeDtypeStruct((S, 2 * D_B), BF),
                   jax.ShapeDtypeStruct((D_MODEL, D_MODEL), F32), jax.ShapeDtypeStruct((D_A, D_MODEL), F32),
                   jax.ShapeDtypeStruct((D_B, D_MODEL), F32),
                   jax.ShapeDtypeStruct((1, 2 * D_MODEL), F32), jax.ShapeDtypeStruct((1, D_MODEL), F32),
                   jax.ShapeDtypeStruct((1, 128), F32),
                   jax.ShapeDtypeStruct((N_GROUPS, 128, 128), F32), jax.ShapeDtypeStruct((SGU_CHUNK, 128), F32),
                   jax.ShapeDtypeStruct((1, D_B), F32), jax.ShapeDtypeStruct((1, D_B), F32)),
        in_specs=[row(D_A), zcol(512, 0), zcol(512, 1), zcol(512, 2), zcol(512, 3),
                  zcol(D_MODEL, 2), zcol(D_MODEL, 3), row(D_MODEL), row(D_MODEL),
                  pl.BlockSpec((D_A, D_MODEL), c2), pl.BlockSpec((D_B, D_MODEL), c2),
                  pl.BlockSpec((D_MODEL, D_MODEL), c2),
                  pl.BlockSpec((1, 2 * D_MODEL), c2), pl.BlockSpec((1, D_MODEL), c2),
                  pl.BlockSpec((1, D_B), c2), pl.BlockSpec((1, D_B), c2),
                  pl.BlockSpec((N_GROUPS, 128, 128), c3), pl.BlockSpec((128, N_GROUPS), c2)],
        out_specs=[row(D_MODEL), pl.BlockSpec((N_HEADS, tm, HEAD_DIM), lambda i: (0, i, 0)),
                   row(3072), row(2 * D_B), _ANY, _ANY, _ANY,
                   pl.BlockSpec((1, 2 * D_MODEL), c2), pl.BlockSpec((1, D_MODEL), c2),
                   pl.BlockSpec((1, 128), c2),
                   pl.BlockSpec((N_GROUPS, 128, 128), c3), pl.BlockSpec((SGU_CHUNK, 128), c2),
                   pl.BlockSpec((1, D_B), c2), pl.BlockSpec((1, D_B), c2)],
        scratch_shapes=[pltpu.VMEM((D_MODEL, D_MODEL), F32), pltpu.VMEM((D_A, D_MODEL), F32),
                        pltpu.VMEM((D_B, D_MODEL), F32),
                        pltpu.VMEM((tm, D_B), F32), pltpu.VMEM((tm, D_B), F32), pltpu.VMEM((tm, D_B), F32),
                        pltpu.VMEM((SGU_CHUNK, D_B), F32), pltpu.SemaphoreType.DMA((3,))],
        compiler_params=_params(58, dimension_semantics=("arbitrary",)),
    )(att, zrest, zrest, zrest, zrest, zrest, zrest, x, target, w_pa, w_pb, w_out, b_gate, final_g,
      ln_g, ln_b, w_s, b_s_t)


_DZ_MAP = ((0, 0), (1, 0), (2, 0), (3, 0), (4, 0), (4, 1), (3, 1), (3, 2), (3, 3), (3, 4), (3, 5))


def _dh_gradx(dq, dk, dv, dzt, dzs, w_in_bf, x, norm_g, d_out, tm=512, after=()):
    S = x.shape[0]

    def body(dq_ref, dk_ref, dv_ref, dzt_ref, dzs_ref, w_ref, x_ref, g_ref, dout_ref, gx_ref, gn_ref):
        i = pl.program_id(0)

        @pl.when(i == 0)
        def _():
            gn_ref[...] = jnp.zeros_like(gn_ref)

        pieces = (dq_ref, dk_ref, dv_ref, dzt_ref, dzs_ref)
        dh = jnp.zeros((tm, D_MODEL), F32)
        for j, (pc, blk) in enumerate(_DZ_MAP):
            dh += _dot_nt(pieces[pc][:, blk * 512:(blk + 1) * 512], w_ref[:, j * 512:(j + 1) * 512])
        xv = x_ref[...]
        r = lax.rsqrt(jnp.mean(xv * xv, axis=-1, keepdims=True) + EPS)
        nrm = xv * r
        gn_ref[...] += jnp.sum(dh * nrm, axis=0, keepdims=True)
        dn = dh * g_ref[...]
        gx_ref[...] = r * (dn - nrm * jnp.mean(dn * nrm, axis=-1, keepdims=True)) + dout_ref[...]

    row = lambda w: pl.BlockSpec((tm, w), lambda i: (i, 0))
    c2 = lambda i: (0, 0)
    return pl.pallas_call(
        _after(body, 9, after), name="dh_gradx", grid=(S // tm,),
        out_shape=(jax.ShapeDtypeStruct((S, D_MODEL), F32), jax.ShapeDtypeStruct((1, D_MODEL), F32)),
        in_specs=[row(512), row(512), row(512), row(3072), row(1024),
                  pl.BlockSpec((D_MODEL, D_IN), c2, pipeline_mode=pl.Buffered(1)), row(D_MODEL),
                  pl.BlockSpec((1, D_MODEL), c2), row(D_MODEL)]
        + [_ANY] * len(after),
        out_specs=[row(D_MODEL), pl.BlockSpec((1, D_MODEL), c2)],
        compiler_params=_params(48, dimension_semantics=("arbitrary",)),
    )(dq, dk, dv, dzt, dzs, w_in_bf, x, norm_g, d_out, *after)


def _gw_in(ht, dq, dk, dv, dzt, dzs, tn=512, after=()):
    S = ht.shape[1]
    per = 512 // tn
    cols = tuple((pc, per * blk + h) for pc, blk in _DZ_MAP for h in range(per))

    def body(ht_ref, dq_ref, dk_ref, dv_ref, dzt_ref, dzs_ref, o_ref, ob_ref):
        j = pl.program_id(0)
        pieces = (dq_ref, dk_ref, dv_ref, dzt_ref, dzs_ref)
        for pc in range(5):
            hit = functools.reduce(jnp.logical_or, [j == jj for jj, (p, _) in enumerate(cols) if p == pc])

            @pl.when(hit)
            def _(pc=pc):
                g = _dot(ht_ref[...], pieces[pc][...])
                o_ref[...] = g
                ob_ref[...] = g.astype(BF)

    def piece_spec(pc):
        cur = next(blk for p, blk in cols if p == pc)
        held = []
        for p, blk in cols:
            cur = blk if p == pc else cur
            held.append(cur)

        def index_map(j):
            blk = jnp.int32(held[0])
            for jj in range(1, len(held)):
                if held[jj] != held[jj - 1]:
                    blk = jnp.where(j >= jj, jnp.int32(held[jj]), blk)
            return (0, blk)

        return pl.BlockSpec((S, tn), index_map)

    return pl.pallas_call(
        _after(body, 6, after), name="gw_in", grid=(len(cols),),
        out_shape=(jax.ShapeDtypeStruct((D_MODEL, D_IN), F32), jax.ShapeDtypeStruct((D_MODEL, D_IN), BF)),
        in_specs=[pl.BlockSpec((D_MODEL, S), lambda j: (0, 0), pipeline_mode=pl.Buffered(1))]
        + [piece_spec(pc) for pc in range(5)]
        + [_ANY] * len(after),
        out_specs=[pl.BlockSpec((D_MODEL, tn), lambda j: (0, j)), pl.BlockSpec((D_MODEL, tn), lambda j: (0, j))],
        compiler_params=_params(56, dimension_semantics=("arbitrary",)),
    )(ht, dq, dk, dv, dzt, dzs, *after)


_HBM = pl.BlockSpec(memory_space=pltpu.HBM)
_SEM = pl.BlockSpec(memory_space=pltpu.SEMAPHORE)
_ANY = pl.BlockSpec(memory_space=pl.ANY)
_EFFECT = pltpu.SideEffectType.DATAFLOW_SIDE_EFFECTING


def _in_hbm(a):
    return pltpu.with_memory_space_constraint(a, pltpu.HBM)


def _after(body, n_in, after):
    if not after:
        return body
    return lambda *refs: body(*refs[:n_in], *refs[n_in + len(after):])


class _Started:
    def __init__(self, send, recv, bufs, token):
        self.send, self.recv, self.bufs, self.token = send, recv, bufs, token


_PEER_SETS = {"sibling": 7, "chips": 8, "both": 9}


def _peers(kind):
    x, y, c, chips = _mesh_pos()
    return ([(x, y, 1 - c)] if kind in ("sibling", "both") else []) + (
        [(cx, cy, c) for cx, cy in chips] if kind in ("chips", "both") else [])


def _signal_peers(kind):
    barrier = pltpu.get_barrier_semaphore()
    targets = _peers(kind)
    for peer in targets:
        pl.semaphore_signal(barrier, inc=1, device_id=peer, device_id_type=MESH)
    return lambda: pl.semaphore_wait(barrier, len(targets))


def _split_start(name, bufs, n_copies, copies, peers, after=()):
    nb = len(bufs)

    def body(*refs):
        _signal_peers(peers)()
        refs = refs[:nb] + refs[nb + len(after):]
        for cp in copies(refs[:nb], refs[nb], refs[nb + 1]):
            cp.start()
        refs[-1][...] = jnp.zeros_like(refs[-1])

    outs = pl.pallas_call(
        body, name=name,
        out_shape=(pltpu.SemaphoreType.DMA((n_copies,)), pltpu.SemaphoreType.DMA((n_copies,)),
                   *[pltpu.HBM(b.shape, b.dtype) for b in bufs], jax.ShapeDtypeStruct((8, 128), F32)),
        in_specs=[_HBM] * nb + [_ANY] * len(after),
        out_specs=(_SEM, _SEM, *[_HBM] * nb, pl.BlockSpec(memory_space=pltpu.VMEM)),
        input_output_aliases={k: 2 + k for k in range(nb)},
        compiler_params=_params(1, has_side_effects=_EFFECT, collective_id=_PEER_SETS[peers]),
    )(*[_in_hbm(b) for b in bufs], *after)
    return _Started(outs[0], outs[1], list(outs[2:2 + nb]), outs[-1])


def _split_wait(name, started, copies, after):
    nb = len(started.bufs)
    after = tuple(after) if isinstance(after, (tuple, list)) else (after,)

    def body(*refs):
        for cp in copies(refs[:nb], refs[nb], refs[nb + 1]):
            cp.wait_send()
            cp.wait_recv()

    return list(pl.pallas_call(
        body, name=name,
        out_shape=tuple(pltpu.HBM(b.shape, b.dtype) for b in started.bufs),
        in_specs=[_HBM] * nb + [_SEM, _SEM] + [_ANY] * len(after),
        out_specs=tuple([_HBM] * nb),
        input_output_aliases={k: k for k in range(nb)},
        compiler_params=_params(1, has_side_effects=_EFFECT),
    )(*started.bufs, started.send, started.recv, *after))


def _x1_copies(ws):
    def copies(refs, send_sems, recv_sems):
        x, y, c, _ = _mesh_pos()
        out = []
        for k, w in enumerate(ws):
            for s in range(N_SHARD):
                out.append(pltpu.make_async_remote_copy(
                    src_ref=_UNITS[w](refs[k], s, 1 - c), dst_ref=refs[len(ws) + k].at[s],
                    send_sem=send_sems.at[N_SHARD * k + s], recv_sem=recv_sems.at[N_SHARD * k + s],
                    device_id=(x, y, 1 - c), device_id_type=MESH))
        return out
    return copies


def _x2_copies(n):
    def copies(refs, send_sems, recv_sems):
        x, y, c, chips = _mesh_pos()
        out = []
        for j, (cx, cy) in enumerate(chips):
            for k in range(n):
                out.append(pltpu.make_async_remote_copy(
                    src_ref=refs[k].at[2 * cx + cy], dst_ref=refs[n + k].at[j],
                    send_sem=send_sems.at[3 * k + j], recv_sem=recv_sems.at[3 * k + j],
                    device_id=(cx, cy, c), device_id_type=MESH))
        return out
    return copies


def _x3_copies(ws):
    def copies(refs, send_sems, recv_sems):
        x, y, c, _ = _mesh_pos()
        out = []
        for k, w in enumerate(ws):
            rows = _HALF_ROWS[w]
            mine = refs[k].at[pl.ds(_mo(c * rows, rows), rows), :]
            out.append(pltpu.make_async_remote_copy(
                src_ref=mine, dst_ref=mine, send_sem=send_sems.at[k], recv_sem=recv_sems.at[k],
                device_id=(x, y, 1 - c), device_id_type=MESH))
        return out
    return copies


def _x1_lands(ws, dtype=F32):
    return [lax.empty((N_SHARD,) + _UNIT_SHAPES[w], dtype) for w in ws]


def _x2_lands(ws):
    return [lax.empty((3,) + _UNIT_SHAPES[w], BF) for w in ws]


def _grad_add1(w, g, recv, pos):
    ur, uc = _UNIT_SHAPES[w]

    def body(pos_ref, g_ref, r_ref, csb_ref):
        csb_ref[0] = (g_ref[...] + r_ref[0].astype(F32)).astype(BF)

    u3 = lambda k, pos: (pos[2 + k], 0, 0)
    return pl.pallas_call(
        body, name=f"grad_add1_{w}",
        grid_spec=pltpu.PrefetchScalarGridSpec(
            num_scalar_prefetch=1, grid=(N_SHARD - 1,),
            in_specs=[pl.BlockSpec((ur, uc), lambda k, pos: (pos[0], pos[2 + k])), pl.BlockSpec((1, ur, uc), u3)],
            out_specs=pl.BlockSpec((1, ur, uc), u3)),
        out_shape=jax.ShapeDtypeStruct((N_SHARD, ur, uc), BF),
        compiler_params=_params(40, dimension_semantics=("arbitrary",)),
    )(pos, g, recv)


def _grad_add1_group(ws, gs, recvs, pos):
    n = len(ws)

    def body(pos_ref, *refs):
        s = pl.program_id(0)
        for k in range(n):
            g, r, own, csb = refs[k], refs[n + k], refs[2 * n + k], refs[3 * n + k]
            v = g[...] + r[0]
            csb[0] = v.astype(BF)

            @pl.when(s == pos_ref[1])
            def _(own=own, v=v):
                own[...] = v

    def g_spec(w):
        if w == 3:
            return pl.BlockSpec(_UNIT_SHAPES[w], lambda s, pos: (2 * s + pos[0], 0))
        return pl.BlockSpec(_UNIT_SHAPES[w], lambda s, pos: (pos[0], s))

    slot = lambda w: pl.BlockSpec((1,) + _UNIT_SHAPES[w], lambda s, pos: (s, 0, 0))
    outs = pl.pallas_call(
        body, name="grad_add1_group",
        grid_spec=pltpu.PrefetchScalarGridSpec(
            num_scalar_prefetch=1, grid=(N_SHARD,),
            in_specs=[g_spec(w) for w in ws] + [slot(w) for w in ws],
            out_specs=[pl.BlockSpec(_UNIT_SHAPES[w], lambda s, pos: (0, 0)) for w in ws] + [slot(w) for w in ws]),
        out_shape=tuple(jax.ShapeDtypeStruct(_UNIT_SHAPES[w], F32) for w in ws)
        + tuple(jax.ShapeDtypeStruct((N_SHARD,) + _UNIT_SHAPES[w], BF) for w in ws),
        compiler_params=_params(32, dimension_semantics=("arbitrary",)),
    )(pos, *gs, *recvs)
    return list(outs[:n]), list(outs[n:])


def _grad_add2_group(ws, owns, recvs):
    n = len(ws)

    def body(*refs):
        c = lax.axis_index("c")
        for k, w in enumerate(ws):
            own, r, o = refs[k], refs[n + k], refs[2 * n + k]
            rows = _HALF_ROWS[w]
            total = ((own[...] + r[0].astype(F32)) + r[1].astype(F32)) + r[2].astype(F32)
            o[pl.ds(_mo(c * rows, rows), rows), :] = total

    vm = pl.BlockSpec(memory_space=pltpu.VMEM)
    return list(pl.pallas_call(
        body, name="grad_add2_group",
        out_shape=tuple(jax.ShapeDtypeStruct(_SHARD_SHAPES[w], F32) for w in ws),
        in_specs=[vm] * (2 * n), out_specs=[vm] * n,
        compiler_params=_params(32),
    )(*owns, *recvs))


def _grad_add2(w, g, recv1, recv2, pos):
    ur, uc = _UNIT_SHAPES[w]
    nt = 4
    tr = ur // nt

    def body(pos_ref, g_ref, r1_ref, r2_ref, o_ref):
        own = g_ref[...] + r1_ref[0].astype(F32)
        o_ref[...] = ((own + r2_ref[0].astype(F32)) + r2_ref[1].astype(F32)) + r2_ref[2].astype(F32)

    mine = lambda t, pos: (pos[0] * nt + t, 0)
    return pl.pallas_call(
        body, name=f"grad_add2_{w}",
        grid_spec=pltpu.PrefetchScalarGridSpec(
            num_scalar_prefetch=1, grid=(nt,),
            in_specs=[pl.BlockSpec((tr, uc), lambda t, pos: (pos[0] * nt + t, pos[1])),
                      pl.BlockSpec((1, tr, uc), lambda t, pos: (pos[1], t, 0)),
                      pl.BlockSpec((3, tr, uc), lambda t, pos: (0, t, 0))],
            out_specs=pl.BlockSpec((tr, uc), mine)),
        out_shape=jax.ShapeDtypeStruct(_SHARD_SHAPES[w], F32),
        compiler_params=_params(32, dimension_semantics=("arbitrary",)),
    )(pos, g, recv1, recv2)


def _adamw_math(w, g, m, v):
    m = ADAM_B1 * m + (1.0 - ADAM_B1) * g
    v = ADAM_B2 * v + (1.0 - ADAM_B2) * (g * g)
    m_hat = m / ADAM_C1
    v_hat = v / ADAM_C2
    delta = -ADAM_LR * (m_hat / (jnp.sqrt(v_hat) + ADAM_EPS) + ADAM_WD * w)
    return delta, m, v


ADAMW_STEPS = 4


def _adamw(ws_, gs, ms, vs):
    n = len(ws_)

    def body(*refs):
        for k in range(n):
            w, g, m, v = (refs[j * n + k] for j in range(4))
            d, nm, nv, gc = (refs[(4 + j) * n + k] for j in range(4))
            gv = g[...]
            d[...], nm[...], nv[...] = _adamw_math(w[...], gv, m[...], v[...])
            gc[...] = gv

    specs = [pl.BlockSpec((a.shape[0] // ADAMW_STEPS, a.shape[1]), lambda i: (i, 0)) for a in ws_] * 4
    outs = pl.pallas_call(
        body, name="adamw", grid=(ADAMW_STEPS,),
        out_shape=tuple(jax.ShapeDtypeStruct(a.shape, F32) for _ in range(4) for a in ws_),
        in_specs=specs, out_specs=specs,
        compiler_params=_params(40, dimension_semantics=("arbitrary",)),
    )(*ws_, *gs, *ms, *vs)
    return [tuple(outs[j * n + k] for j in range(4)) for k in range(n)]


_REL_PAD = 384
_VEC_FIELDS = (("norm_g", 0, D_MODEL), ("b_gate", 1024, 2 * D_MODEL), ("sgu_ln_g", 3072, D_B),
               ("sgu_ln_b", 3584, D_B), ("b_s", 4096, N_GROUPS * 128), ("final_g", 4608, D_MODEL))
_LOSS_OFF = 5632
_REL_OFF = 5760
_NV = _REL_OFF + N_HEADS * _REL_PAD
_N_FIELDS = len(_VEC_FIELDS) + 2


_B_S_FIELD = [f[0] for f in _VEC_FIELDS].index("b_s")


def _assemble_row(dst, fields, transposed_b_s):
    for f, (_, off, n) in enumerate(_VEC_FIELDS):
        if transposed_b_s and f == _B_S_FIELD:
            t = fields[f][...].T
            for g in range(N_GROUPS):
                dst[:, off + 128 * g:off + 128 * (g + 1)] = t[g:g + 1, :]
        else:
            dst[:, off:off + n] = fields[f][...]
    for r in range(N_HEADS):
        dst[:, _REL_OFF + _REL_PAD * r:_REL_OFF + _REL_PAD * (r + 1)] = fields[len(_VEC_FIELDS)][r:r + 1, :]


def _small_reduce(grads, loss_row, after=()):
    n_in = _N_FIELDS + 1

    def body(*refs):
        g_refs, loss_ref = refs[:_N_FIELDS], refs[_N_FIELDS]
        out_v, out_w = refs[n_in:n_in + 2]
        mine_v, mine_w, gath_v, gath_w, send_sems, recv_sems = refs[n_in + 2:]
        x, y, c, chips = _mesh_pos()
        me, sibling = (x, y, c), (x, y, 1 - c)

        peers_entered = _signal_peers("both")
        _assemble_row(mine_v, g_refs, True)
        mine_v[:, _LOSS_OFF:_LOSS_OFF + 128] = loss_ref[...]
        mine_w[...] = g_refs[-1][...].astype(BF)
        peers_entered()
        my_k = 4 * x + 2 * y + c
        gath_v[my_k] = mine_v[...]
        gath_w[my_k] = mine_w[...]

        def copy(k, gath, block, to, src=None):
            dst = gath.at[4 * block[0] + 2 * block[1] + block[2]]
            return pltpu.make_async_remote_copy(
                src_ref=dst if src is None else src, dst_ref=dst,
                send_sem=send_sems.at[k], recv_sem=recv_sems.at[k], device_id=to, device_id_type=MESH)

        bufs = ((gath_v, mine_v), (gath_w, mine_w))
        first, passed = [], []
        for b, (gath, mine) in enumerate(bufs):
            first.append(copy(7 * b, gath, me, sibling, src=mine))
            first += [copy(7 * b + 1 + j, gath, me, (*chip, c), src=mine) for j, chip in enumerate(chips)]
        for cp in first:
            cp.start()
        for b, (gath, _) in enumerate(bufs):
            for j, chip in enumerate(chips):
                copy(7 * b + 1 + j, gath, (*chip, c), me).wait_recv()
                cp = copy(7 * b + 4 + j, gath, (*chip, c), sibling)
                cp.start()
                passed.append(cp)
        for b, (gath, _) in enumerate(bufs):
            copy(7 * b, gath, sibling, me).wait_recv()
            for j, chip in enumerate(chips):
                copy(7 * b + 4 + j, gath, (*chip, 1 - c), me).wait_recv()
        for cp in first + passed:
            cp.wait_send()

        tot_v, tot_w = gath_v[0], gath_w[0].astype(F32)
        for k in range(1, 8):
            tot_v = tot_v + gath_v[k]
            tot_w = tot_w + gath_w[k].astype(F32)
        out_v[...] = tot_v
        out_w[...] = tot_w

    vm = pl.BlockSpec(memory_space=pltpu.VMEM)
    return pl.pallas_call(
        _after(body, n_in, after), name="small_reduce",
        out_shape=(jax.ShapeDtypeStruct((1, _NV), F32), jax.ShapeDtypeStruct((N_GROUPS * 128, 128), F32)),
        in_specs=[vm] * n_in + [_ANY] * len(after), out_specs=[vm] * 2,
        scratch_shapes=[pltpu.VMEM((1, _NV), F32), pltpu.VMEM((N_GROUPS * 128, 128), BF),
                        pltpu.VMEM((8, 1, _NV), F32), pltpu.VMEM((8, N_GROUPS * 128, 128), BF),
                        pltpu.SemaphoreType.DMA((14,)), pltpu.SemaphoreType.DMA((14,))],
        compiler_params=_params(32, collective_id=_PEER_SETS["both"]),
    )(*grads, loss_row, *after)


def _small_adamw(tot_v, tot_w, params):
    n_in = 2 + 3 * _N_FIELDS

    def body(*refs):
        tv_ref, tw_ref = refs[:2]
        p_refs = [refs[2 + k * _N_FIELDS:2 + (k + 1) * _N_FIELDS] for k in range(3)]
        outs = refs[n_in:n_in + 4 * _N_FIELDS + 1]
        wmv = refs[-1]
        for k in range(3):
            _assemble_row(wmv.at[k], p_refs[k], False)
            wmv[k, :, _LOSS_OFF:_LOSS_OFF + 128] = jnp.zeros((1, 128), F32)
        tot_v, tot_w = tv_ref[...], tw_ref[...]
        res_v = (tot_v,) + _adamw_math(wmv[0], tot_v, wmv[1], wmv[2])
        res_w = (tot_w,) + _adamw_math(p_refs[0][-1][...], tot_w, p_refs[1][-1][...], p_refs[2][-1][...])
        for kind in range(4):
            o = outs[kind * _N_FIELDS:(kind + 1) * _N_FIELDS]
            for f, (_, off, n) in enumerate(_VEC_FIELDS):
                o[f][...] = res_v[kind][:, off:off + n]
            for r in range(N_HEADS):
                o[len(_VEC_FIELDS)][r:r + 1, :] = res_v[kind][:, _REL_OFF + _REL_PAD * r:_REL_OFF + _REL_PAD * (r + 1)]
            o[-1][...] = res_w[kind]
        outs[-1][...] = tot_v[:, _LOSS_OFF:_LOSS_OFF + 128]

    field_shapes = [(1, n) for _, _, n in _VEC_FIELDS] + [(N_HEADS, _REL_PAD), (N_GROUPS * 128, 128)]
    vm = pl.BlockSpec(memory_space=pltpu.VMEM)
    operands = [tot_v, tot_w] + [a for p in params for a in p]
    assert len(operands) == n_in
    outs = pl.pallas_call(
        body, name="small_adamw",
        out_shape=tuple(jax.ShapeDtypeStruct(s, F32) for _ in range(4) for s in field_shapes)
        + (jax.ShapeDtypeStruct((1, 128), F32),),
        in_specs=[vm] * n_in, out_specs=[vm] * (4 * _N_FIELDS + 1),
        scratch_shapes=[pltpu.VMEM((3, 1, _NV), F32)],
        compiler_params=_params(32),
    )(*operands)
    return [outs[k * _N_FIELDS:(k + 1) * _N_FIELDS] for k in range(4)], outs[-1]


def _small_fields(norm_g, b_gate, ln_g, ln_b, b_s, final_g, rel_bias, w_s):
    rel = jnp.pad(rel_bias.reshape(N_HEADS, N_REL), ((0, 0), (0, _REL_PAD - N_REL)))
    return (norm_g, b_gate, ln_g, ln_b, b_s.reshape(1, N_GROUPS * 128), final_g.reshape(1, D_MODEL),
            rel, w_s.reshape(N_GROUPS * 128, 128))


def _small_outputs(fields):
    n_g, b_g, l_g, l_b, b_s, f_g, rel, w_s = fields
    return (n_g, b_g, rel[:, :N_REL].reshape(1, N_HEADS, N_REL), l_g, l_b,
            w_s.reshape(1, N_GROUPS, 128, 128), b_s.reshape(1, N_GROUPS, 128), f_g.reshape(D_MODEL))


def kernel(x, norm_g, w_in, b_gate, rel_bias, sgu_ln_g, sgu_ln_b, w_s, b_s, w_pa, w_pb, w_out, final_g, loss_target, m_norm_g, m_w_in, m_b_gate, m_rel_bias, m_sgu_ln_g, m_sgu_ln_b, m_w_s, m_b_s, m_w_pa, m_w_pb, m_w_out, m_final_g, v_norm_g, v_w_in, v_b_gate, v_rel_bias, v_sgu_ln_g, v_sgu_ln_b, v_w_s, v_b_s, v_w_pa, v_w_pb, v_w_out, v_final_g):
    S = x.shape[1]
    xs = x.reshape(S, D_MODEL)
    tgt = loss_target.reshape(S, D_MODEL)
    big_w = (w_in[0], w_pa[0], w_pb[0], w_out[0])
    big_m = (m_w_in[0], m_w_pa[0], m_w_pb[0], m_w_out[0])
    big_v = (v_w_in[0], v_w_pa[0], v_w_pb[0], v_w_out[0])
    rel = rel_bias[0]
    ws = w_s[0]
    bst = b_s[0].T
    fg = final_g.reshape(1, D_MODEL)
    chip = 2 * lax.axis_index("x") + lax.axis_index("y")
    pos = jnp.stack([lax.axis_index("c"), chip] + [(chip + k) % N_SHARD for k in range(1, N_SHARD)]).astype(jnp.int32)

    (w_in_bf,), staged, band_bias = _ag_weights((0,), big_w[:1], (1, 2, 3), big_w[1:], rel)
    ag_s = _split_start("ag_small_start", staged, 9, _gather_copies((1, 2, 3)), "chips", after=(w_in_bf,))

    ht, q3, k3, v3, zrest = _inproj_fwd(xs, norm_g, w_in_bf, after=(ag_s.token,))
    att, lse = _attn_fwd(q3, k3, v3, band_bias)
    w_pa_bf, w_pb_bf, w_out_bf = _split_wait("ag_small_wait", ag_s, _gather_copies((1, 2, 3)), att)
    (d_out, d_att, dzt, dzs, gw_out, gw_pa, gw_pb, g_bgate, g_final, loss_row,
     g_ws, g_bs_t, g_lng, g_lnb) = _tail_sgu(
        att, zrest, xs, tgt, w_pa_bf, w_pb_bf, w_out_bf, b_gate, fg, sgu_ln_g, sgu_ln_b, ws, bst)
    ws_s, ws_i = (1, 2, 3), (0,)

    x1s = _split_start("gx1s_start", [gw_pa, gw_pb, gw_out] + _x1_lands(ws_s), 12, _x1_copies(ws_s), "sibling")
    dq, dk, dv, d_gp = _attn_bwd(q3, k3, v3, d_att, lse, band_bias, after=(x1s.token,))
    got = _split_wait("gx1s_wait", x1s, _x1_copies(ws_s), dq)
    own_s, csb_s = _grad_add1_group(ws_s, got[:3], got[3:], pos)

    x2s = _split_start("gx2s_start", csb_s + _x2_lands(ws_s), 9, _x2_copies(3), "chips")
    gw_in, gw_in_bf = _gw_in(ht, dq, dk, dv, dzt, dzs, after=(x2s.token,))
    x1i = _split_start("gx1i_start", [gw_in_bf] + _x1_lands(ws_i, BF), 4, _x1_copies(ws_i), "sibling")
    got = _split_wait("gx2s_wait", x2s, _x2_copies(3), x1i.token)
    halves_s = _grad_add2_group(ws_s, own_s, got[3:])
    x3s = _split_start("gx3s_start", halves_s, 3, _x3_copies(ws_s), "sibling")
    g_rel = jnp.pad(d_gp[:, 384:384 + N_REL][:, ::-1], ((0, 0), (0, _REL_PAD - N_REL)))
    small_params = (_small_fields(norm_g, b_gate, sgu_ln_g, sgu_ln_b, b_s, final_g, rel_bias, w_s),
                    _small_fields(m_norm_g, m_b_gate, m_sgu_ln_g, m_sgu_ln_b, m_b_s, m_final_g, m_rel_bias, m_w_s),
                    _small_fields(v_norm_g, v_b_gate, v_sgu_ln_g, v_sgu_ln_b, v_b_s, v_final_g, v_rel_bias, v_w_s))
    relayouts = (g_rel,) + tuple(fields[-2] for fields in small_params)
    recv1_i = _split_wait("gx1i_wait", x1i, _x1_copies(ws_i), (x3s.token,) + relayouts)[1]
    csb_i = _grad_add1(0, gw_in, recv1_i, pos)

    x2i = _split_start("gx2i_start", [csb_i] + _x2_lands(ws_i), 3, _x2_copies(1), "chips")
    grad_x, g_norm = _dh_gradx(dq, dk, dv, dzt, dzs, w_in_bf, xs, norm_g, d_out, after=(x2i.token,))
    g_shards_s = _split_wait("gx3s_wait", x3s, _x3_copies(ws_s), grad_x)
    got = _split_wait("gx2i_wait", x2i, _x2_copies(1), grad_x)
    half_i = _grad_add2(0, gw_in, recv1_i, got[1], pos)
    x3i = _split_start("gx3i_start", [half_i], 1, _x3_copies(ws_i), "sibling")

    small_grads = (g_norm, g_bgate, g_lng, g_lnb, g_bs_t, g_final, g_rel, g_ws.reshape(N_GROUPS * 128, 128))
    tot_v, tot_w = _small_reduce(small_grads, loss_row, after=(x3i.token,))
    (gsum, sdelta, sm, sv), loss_out = _small_adamw(tot_v, tot_w, small_params)

    g_shard_i, = _split_wait("gx3i_wait", x3i, _x3_copies(ws_i), loss_out)
    big = _adamw(big_w, [g_shard_i] + g_shards_s, big_m, big_v)
    sg_out, sd_out, sm_out, sv_out = (_small_outputs(f) for f in (gsum, sdelta, sm, sv))
    loss = loss_out[0, 0]

    def assemble(small, bigs):
        n_g, b_g, r_b, l_g, l_b, w_s_, b_s_, f_g = small
        b_in, b_pa, b_pb, b_out = (b[None] for b in bigs)
        return (n_g, b_in, b_g, r_b, l_g, l_b, w_s_, b_s_, b_pa, b_pb, b_out, f_g)

    grads_out = assemble(sg_out, [b[3] for b in big])
    delta_out = assemble(sd_out, [b[0] for b in big])
    m_out = assemble(sm_out, [b[1] for b in big])
    v_out = assemble(sv_out, [b[2] for b in big])
    return (loss, grad_x.reshape(1, S, D_MODEL), *grads_out, *delta_out, *m_out, *v_out)
```

```python
import functools
import math

import jax
import jax.numpy as jnp
from jax import lax
from jax.experimental import pallas as pl
from jax.experimental.pallas import tpu as pltpu

F32 = jnp.float32
BF = jnp.bfloat16
MESH = pl.DeviceIdType.MESH

D_MODEL = 1024
D_A = 512
D_B = 512
D_IN = 5632
N_HEADS = 8
HEAD_DIM = 64
CHUNK = 64
N_PREV = 8
SGU_CHUNK = 128
N_GROUPS = 4
N_REL = 257
EPS = 1e-6
NEG_INF = -1e30
SCALE = HEAD_DIM ** -0.5

QB = 2 * CHUNK
KB = (N_PREV + 2) * CHUNK
PADK = N_PREV * CHUNK
ROLL_W = 1024
KEEP = KB // QB - 1
Q_PER_STEP = 2

ADAM_LR = 0.001
ADAM_B1 = 0.9
ADAM_B2 = 0.999
ADAM_EPS = 1e-08
ADAM_WD = 0.01
ADAM_STEP = 10
ADAM_C1 = 1.0 - ADAM_B1 ** ADAM_STEP
ADAM_C2 = 1.0 - ADAM_B2 ** ADAM_STEP

AG_PIECES = 4
N_SHARD = 4
SHARD_IN = D_IN // N_SHARD
MIB = 1024 * 1024


V7X_VMEM_MIB = 64
VMEM_RESERVE_MIB = V7X_VMEM_MIB - 4


def _params(vmem_mib, **kw):
    assert vmem_mib <= VMEM_RESERVE_MIB
    return pltpu.CompilerParams(vmem_limit_bytes=VMEM_RESERVE_MIB * MIB, **kw)


def _sigmoid(x):
    return 1.0 / (1.0 + jnp.exp(-x))


def _silu_and_grad(x):
    s = _sigmoid(x)
    return x * s, s * (1.0 + x * (1.0 - s))


_GELU_C = math.sqrt(2.0 / math.pi)
_GELU_A = 0.044715


def _gelu_and_grad(x):
    x2 = x * x
    t = jnp.tanh(_GELU_C * (x + _GELU_A * (x2 * x)))
    cdf = 0.5 * (1.0 + t)
    grad = cdf + 0.5 * x * (1.0 - t * t) * (_GELU_C * (1.0 + 3.0 * _GELU_A * x2))
    return x * cdf, grad


def _dot(a, b):
    return jnp.dot(a, b, preferred_element_type=F32)


def _dot_nt(a, b):
    return lax.dot_general(a, b, (((1,), (1,)), ((), ())), preferred_element_type=F32)


def _dot_tn(a, b):
    return lax.dot_general(a, b, (((0,), (0,)), ((), ())), preferred_element_type=F32)


def _mo(v, m):
    return v if isinstance(v, int) else pl.multiple_of(v, m)


def _unit_in(ref, s, p):
    return ref.at[pl.ds(_mo(p * 512, 512), 512), pl.ds(_mo(s * SHARD_IN, 128), SHARD_IN)]


def _unit_p(ref, s, p):
    return ref.at[pl.ds(_mo(p * 256, 256), 256), pl.ds(_mo(s * 256, 128), 256)]


def _unit_out(ref, s, p):
    return ref.at[pl.ds(_mo(s * 256 + p * 128, 128), 128), :]


_UNITS = (_unit_in, _unit_p, _unit_p, _unit_out)
_HALF_ROWS = (512, 256, 256, 128)
_UNIT_SHAPES = ((512, SHARD_IN), (256, 256), (256, 256), (128, D_MODEL))
_FULL_SHAPES = ((D_MODEL, D_IN), (D_A, D_MODEL), (D_B, D_MODEL), (D_MODEL, D_MODEL))
_SHARD_SHAPES = ((D_MODEL, SHARD_IN), (D_A, 256), (D_B, 256), (256, D_MODEL))


def _mesh_pos():
    x, y, c = lax.axis_index("x"), lax.axis_index("y"), lax.axis_index("c")
    chips = [(1 - x, y), (x, 1 - y), (1 - x, 1 - y)]
    return x, y, c, chips


def _bias_rows(rel_ref, pad_ref):
    pad_ref[...] = jnp.zeros(pad_ref.shape, F32)
    pad_ref[:, :N_REL] = rel_ref[...]
    r = pad_ref[...]
    m = lax.broadcasted_iota(jnp.int32, (pad_ref.shape[1], ROLL_W), 1)
    m = jnp.where(m >= ROLL_W - 192, m - ROLL_W, m)
    pick = (lax.broadcasted_iota(jnp.int32, m.shape, 0) == jnp.clip(512 - m, -128, 128) + 128).astype(BF)
    hi = r.astype(BF)
    mid = (r - hi.astype(F32)).astype(BF)
    lo = ((r - hi.astype(F32)) - mid.astype(F32)).astype(BF)
    return (_dot(hi, pick) + _dot(mid, pick)) + _dot(lo, pick)


def _ag_weights(ws, shards, later_ws, later_shards, rel):
    n, m = len(ws), len(later_ws)

    def body(*refs):
        ins, later_ins, rel_ref = refs[:n], refs[n:n + m], refs[n + m]
        o = n + m + 1
        outs, later_outs, bias_ref = refs[o:o + n], refs[o + n:o + n + m], refs[o + n + m]
        o += n + m + 1
        stage, later_stage = refs[o:o + n], refs[o + n:o + n + m]
        send_sems, recv_sems, local_sems, later_sems, rel_pad, gp_ref = refs[o + n + m:]
        x, y, c, chips = _mesh_pos()
        s_me = 2 * x + y
        sibling = (x, y, 1 - c)
        def rows_of(k, p):
            rows = _HALF_ROWS[ws[k]]
            return pl.ds(_mo(p * rows, rows), rows)

        def half(k, p):
            return stage[k].at[rows_of(k, p), :]

        def unit(k, s, p):
            return _UNITS[ws[k]](outs[k], s, p)

        def rcopy(k, i, src, dst, to):
            return pltpu.make_async_remote_copy(src_ref=src, dst_ref=dst, send_sem=send_sems.at[k, i],
                                                recv_sem=recv_sems.at[k, i], device_id=to, device_id_type=MESH)

        peers_entered = _signal_peers("both")
        for k in range(n):
            stage[k][rows_of(k, c), :] = ins[k][rows_of(k, c), :].astype(BF)
        peers_entered()
        def piece(ref, k, q):
            rows = _HALF_ROWS[ws[k]] // AG_PIECES
            return ref.at[pl.ds(q * rows, rows), :]

        sends = []
        for q in range(AG_PIECES):
            for j, (cx, cy) in enumerate(chips):
                for k in range(n):
                    cp = rcopy(k, j * AG_PIECES + q, piece(half(k, c), k, q), piece(unit(k, s_me, c), k, q),
                               (cx, cy, c))
                    cp.start()
                    sends.append(cp)
        for k in range(n):
            stage[k][rows_of(k, 1 - c), :] = ins[k][rows_of(k, 1 - c), :].astype(BF)
        local = []
        for k in range(n):
            for p in range(2):
                cp = pltpu.make_async_copy(half(k, p), unit(k, s_me, p), local_sems.at[k, p])
                cp.start()
                local.append(cp)
        for k, w in enumerate(later_ws):
            later_stage[k][...] = later_ins[k][...].astype(BF)
            cp = pltpu.make_async_copy(later_stage[k], _shard_of(later_outs[k], w, s_me), later_sems.at[k])
            cp.start()
            local.append(cp)
        keep = _struct_mask()
        gp_ref[...] = _bias_rows(rel_ref, rel_pad)
        for h in range(N_HEADS):
            bias_ref[h] = jnp.where(keep, _skew_table(gp_ref[h:h + 1, :])[:, :KB], NEG_INF)
        for q in range(AG_PIECES):
            for j, (cx, cy) in enumerate(chips):
                for k in range(n):
                    landed = piece(unit(k, 2 * cx + cy, c), k, q)
                    rcopy(k, j * AG_PIECES + q, landed, landed, (cx, cy, c)).wait_recv()
                    cp = rcopy(k, (3 + j) * AG_PIECES + q, landed, landed, sibling)
                    cp.start()
                    sends.append(cp)
        for q in range(AG_PIECES):
            for j, (cx, cy) in enumerate(chips):
                for k in range(n):
                    other = piece(unit(k, 2 * cx + cy, 1 - c), k, q)
                    rcopy(k, (3 + j) * AG_PIECES + q, other, other, sibling).wait_recv()
        for cp in sends:
            cp.wait_send()
        for cp in local:
            cp.wait()

    vm = pl.BlockSpec(memory_space=pltpu.VMEM)
    outs = pl.pallas_call(
        body, name="ag_weights",
        out_shape=tuple(jax.ShapeDtypeStruct(_FULL_SHAPES[w], BF) for w in tuple(ws) + tuple(later_ws))
        + (jax.ShapeDtypeStruct((N_HEADS, QB, KB), F32),),
        in_specs=[vm] * (n + m + 1), out_specs=[_ANY] * (n + m) + [vm],
        scratch_shapes=[pltpu.VMEM(_SHARD_SHAPES[w], BF) for w in tuple(ws) + tuple(later_ws)]
        + [pltpu.SemaphoreType.DMA((n, 6 * AG_PIECES)), pltpu.SemaphoreType.DMA((n, 6 * AG_PIECES)),
           pltpu.SemaphoreType.DMA((n, 2)), pltpu.SemaphoreType.DMA((m,)),
           pltpu.VMEM((N_HEADS, _REL_PAD), F32), pltpu.VMEM((N_HEADS, ROLL_W), F32)],
        compiler_params=_params(48, collective_id=_PEER_SETS["both"]),
    )(*shards, *later_shards, rel)
    return list(outs[:n]), list(outs[n:n + m]), outs[-1]


def _shard_of(ref, w, s):
    if w == 0:
        return ref.at[:, pl.ds(_mo(s * SHARD_IN, 128), SHARD_IN)]
    if w == 3:
        return ref.at[pl.ds(_mo(s * 256, 256), 256), :]
    return ref.at[:, pl.ds(_mo(s * 256, 128), 256)]


def _gather_copies(ws):
    def copies(refs, send_sems, recv_sems):
        x, y, c, chips = _mesh_pos()
        out = []
        for j, (cx, cy) in enumerate(chips):
            for k, w in enumerate(ws):
                mine = _shard_of(refs[k], w, 2 * x + y)
                out.append(pltpu.make_async_remote_copy(
                    src_ref=mine, dst_ref=mine, send_sem=send_sems.at[3 * k + j], recv_sem=recv_sems.at[3 * k + j],
                    device_id=(cx, cy, c), device_id_type=MESH))
        return out
    return copies


def _inproj_fwd(x, norm_g, w_in_bf, tm=512, after=()):
    S = x.shape[0]

    def body(x_ref, g_ref, w_ref, ht_ref, q_ref, k_ref, v_ref, zr_ref):
        xv = x_ref[...]
        r = lax.rsqrt(jnp.mean(xv * xv, axis=-1, keepdims=True) + EPS)
        hf = (xv * r) * g_ref[...]
        ht_ref[...] = hf.T.astype(BF)
        h = hf.astype(BF)
        heads = (q_ref, k_ref, v_ref)
        for j in range(D_IN // 512):
            z = _dot(h, w_ref[:, j * 512:(j + 1) * 512])
            if j < 3:
                zb = z.astype(BF)
                for hd in range(N_HEADS):
                    heads[j][hd] = zb[:, hd * HEAD_DIM:(hd + 1) * HEAD_DIM]
            else:
                zr_ref[:, (j - 3) * 512:(j - 2) * 512] = z

    head_major = jax.ShapeDtypeStruct((N_HEADS, S, HEAD_DIM), BF)
    head_spec = pl.BlockSpec((N_HEADS, tm, HEAD_DIM), lambda i: (0, i, 0))
    return pl.pallas_call(
        _after(body, 3, after), name="inproj_fwd", grid=(S // tm,),
        out_shape=(jax.ShapeDtypeStruct((D_MODEL, S), BF), head_major, head_major, head_major,
                   jax.ShapeDtypeStruct((S, D_IN - 3 * D_A), F32)),
        in_specs=[pl.BlockSpec((tm, D_MODEL), lambda i: (i, 0)),
                  pl.BlockSpec((1, D_MODEL), lambda i: (0, 0)),
                  pl.BlockSpec((D_MODEL, D_IN), lambda i: (0, 0), pipeline_mode=pl.Buffered(1))]
        + [_ANY] * len(after),
        out_specs=[pl.BlockSpec((D_MODEL, tm), lambda i: (0, i)),
                   head_spec, head_spec, head_spec,
                   pl.BlockSpec((tm, D_IN - 3 * D_A), lambda i: (i, 0))],
        compiler_params=_params(52, dimension_semantics=("arbitrary",)),
    )(x, norm_g, w_in_bf, *after)


def _skew_table(gp_row):
    row = lax.broadcasted_iota(jnp.int32, (QB, ROLL_W), 0)
    t = jnp.broadcast_to(gp_row, (QB, ROLL_W))
    for b in range(7):
        t = jnp.where(((row >> b) & 1) == 1, pltpu.roll(t, 1 << b, axis=1), t)
    return t


def _unskew_sum(d):
    half = QB // 2
    while half >= 8:
        d = d[0:half] + pltpu.roll(d[half:2 * half], ROLL_W - half, axis=1)
        half //= 2
    row = lax.broadcasted_iota(jnp.int32, (8, ROLL_W), 0)
    for b in range(3):
        d = jnp.where(((row >> b) & 1) == 1, pltpu.roll(d, ROLL_W - (1 << b), axis=1), d)
    return jnp.sum(d, axis=0, keepdims=True)


def _struct_mask():
    a = lax.broadcasted_iota(jnp.int32, (QB, KB), 0) // CHUNK
    b = lax.broadcasted_iota(jnp.int32, (QB, KB), 1) // CHUNK
    return (b >= a) & (b <= a + N_PREV)


def _load_kv(k_hbm, v_hbm, k_scr, v_scr, sems, S, meanwhile=lambda: None):
    zeros = jnp.zeros((N_HEADS, PADK, HEAD_DIM), BF)
    k_scr[:, 0:PADK, :] = zeros
    v_scr[:, 0:PADK, :] = zeros
    ck = pltpu.make_async_copy(k_hbm, k_scr.at[:, pl.ds(PADK, S), :], sems.at[0])
    cv = pltpu.make_async_copy(v_hbm, v_scr.at[:, pl.ds(PADK, S), :], sems.at[1])
    ck.start()
    cv.start()
    meanwhile()
    ck.wait()
    cv.wait()


_BATCH_NT = (((2,), (2,)), ((0,), (0,)))
_BATCH_NN = (((2,), (1,)), ((0,), (0,)))
_BATCH_TN = (((1,), (1,)), ((0,), (0,)))


def _bdot(a, b, dims):
    return lax.dot_general(a, b, dims, preferred_element_type=F32)


def _scaled(q):
    return q * jnp.asarray(SCALE, BF)


def _scores(qs, kb, bias, i, front):
    s = _bdot(qs, kb, _BATCH_NT) + bias
    if front:
        col = lax.broadcasted_iota(jnp.int32, (1, 1, KB), 2)
        s = jnp.where(col >= PADK - i * QB, s, NEG_INF)
    return s


def _attn_fwd(q3, k3, v3, bias):
    S = q3.shape[1]

    def body(q_ref, k_hbm, v_hbm, bias_ref, o_ref, lse_ref, k_scr, v_scr, sems):
        @pl.when(pl.program_id(0) == 0)
        def _():
            _load_kv(k_hbm, v_hbm, k_scr, v_scr, sems, S)

        def step(i, rows, front):
            start = pl.multiple_of(i * QB, QB)
            kb = k_scr[:, pl.ds(start, KB), :]
            vb = v_scr[:, pl.ds(start, KB), :]
            s = _scores(_scaled(q_ref[:, rows, :]), kb, bias_ref[...], i, front)
            m = jnp.max(s, axis=-1, keepdims=True)
            e = jnp.exp(s - m)
            l = jnp.sum(e, axis=-1, keepdims=True)
            p = e * (1.0 / l)
            o = _bdot(p.astype(BF), vb, _BATCH_NN)
            lse_ref[:, rows, :] = jnp.broadcast_to(m + jnp.log(l), (N_HEADS, QB, 128))
            for h in range(N_HEADS):
                o_ref[rows, h * HEAD_DIM:(h + 1) * HEAD_DIM] = o[h]

        def block(j, carry):
            i = pl.program_id(0) * Q_PER_STEP + j
            rows = pl.ds(pl.multiple_of(j * QB, QB), QB)
            pl.when(i < KEEP)(functools.partial(step, i, rows, True))
            pl.when(i >= KEEP)(functools.partial(step, i, rows, False))
            return carry

        lax.fori_loop(0, Q_PER_STEP, block, 0)

    rows_per_step = Q_PER_STEP * QB
    kv_scr = pltpu.VMEM((N_HEADS, S + PADK, HEAD_DIM), BF)
    return pl.pallas_call(
        body, name="attn_fwd", grid=(S // rows_per_step,),
        out_shape=(jax.ShapeDtypeStruct((S, D_A), F32), jax.ShapeDtypeStruct((N_HEADS, S, 128), F32)),
        in_specs=[pl.BlockSpec((N_HEADS, rows_per_step, HEAD_DIM), lambda g: (0, g, 0)),
                  pl.BlockSpec(memory_space=pl.ANY), pl.BlockSpec(memory_space=pl.ANY),
                  pl.BlockSpec((N_HEADS, QB, KB), lambda g: (0, 0, 0))],
        out_specs=[pl.BlockSpec((rows_per_step, D_A), lambda g: (g, 0)),
                   pl.BlockSpec((N_HEADS, rows_per_step, 128), lambda g: (0, g, 0))],
        scratch_shapes=[kv_scr, kv_scr, pltpu.SemaphoreType.DMA((2,))],
        compiler_params=_params(48, dimension_semantics=("arbitrary",)),
    )(q3, k3, v3, bias)


def _attn_bwd(q3, k3, v3, d_att3, lse, bias, after=()):
    S = q3.shape[1]
    nq = S // QB

    def body(q_ref, do_ref, k_hbm, v_hbm, lse_ref, bias_ref, dq_ref, dk_ref, dv_ref, dgp_ref,
             k_scr, v_scr, dk_acc, dv_acc, dbias_acc, pad_scr, sems):
        @pl.when(pl.program_id(0) == 0)
        def _():
            def clear():
                dk_acc[...] = jnp.zeros_like(dk_acc)
                dv_acc[...] = jnp.zeros_like(dv_acc)
                dbias_acc[...] = jnp.zeros_like(dbias_acc)
            _load_kv(k_hbm, v_hbm, k_scr, v_scr, sems, S, clear)

        def step(i, rows, front):
            start = pl.multiple_of(i * QB, QB)
            kb = k_scr[:, pl.ds(start, KB), :]
            vb = v_scr[:, pl.ds(start, KB), :]
            qs = _scaled(q_ref[:, rows, :])
            do = do_ref[:, rows, :]
            p = jnp.exp(_scores(qs, kb, bias_ref[...], i, front) - jnp.tile(lse_ref[:, rows, :], (1, 1, KB // 128)))
            dp = _bdot(do, vb, _BATCH_NT)
            ds = p * (dp - jnp.sum(dp * p, axis=-1, keepdims=True))
            dbias_acc[...] += ds
            dsb = ds.astype(BF)
            dq = _bdot(dsb, kb, _BATCH_NN) * SCALE
            for h in range(N_HEADS):
                dq_ref[rows, h * HEAD_DIM:(h + 1) * HEAD_DIM] = dq[h].astype(BF)
            dk_acc[...] += _bdot(dsb, qs, _BATCH_TN)
            dv_acc[...] += _bdot(p.astype(BF), do, _BATCH_TN)

        def block(j, carry):
            i = pl.program_id(0) * Q_PER_STEP + j
            rows = pl.ds(pl.multiple_of(j * QB, QB), QB)
            pl.when(i < KEEP)(functools.partial(step, i, rows, True))
            pl.when((i >= KEEP) & (i < nq))(functools.partial(step, i, rows, False))
            for h in range(N_HEADS):
                hs = slice(h * HEAD_DIM, (h + 1) * HEAD_DIM)
                dk_ref[rows, hs] = dk_acc[h, 0:QB, :].astype(BF)
                dv_ref[rows, hs] = dv_acc[h, 0:QB, :].astype(BF)
            dk_acc[:, 0:KB - QB, :] = dk_acc[:, QB:KB, :]
            dv_acc[:, 0:KB - QB, :] = dv_acc[:, QB:KB, :]
            dk_acc[:, KB - QB:KB, :] = jnp.zeros((N_HEADS, QB, HEAD_DIM), F32)
            dv_acc[:, KB - QB:KB, :] = jnp.zeros((N_HEADS, QB, HEAD_DIM), F32)
            return carry

        lax.fori_loop(0, Q_PER_STEP, block, 0)

        @pl.when(pl.program_id(0) == n_steps - 1)
        def _():
            lane = lax.broadcasted_iota(jnp.int32, (1, ROLL_W), 1)
            hi = (lane < 384) | (lane >= 832)
            lo = (lane > 640) & (lane < 832)
            pad_scr[...] = jnp.zeros_like(pad_scr)
            for h in range(N_HEADS):
                pad_scr[:, 0:KB] = dbias_acc[h]
                g = _unskew_sum(pad_scr[...])
                s_hi = jnp.sum(jnp.where(hi, g, 0.0), axis=-1, keepdims=True)
                s_lo = jnp.sum(jnp.where(lo, g, 0.0), axis=-1, keepdims=True)
                g = jnp.where(lane == 384, g + s_hi, g)
                g = jnp.where(lane == 640, g + s_lo, g)
                dgp_ref[h:h + 1, :] = g

    assert nq % Q_PER_STEP == 0 and KEEP % Q_PER_STEP == 0
    rows_per_step = Q_PER_STEP * QB
    n_steps = (nq + KEEP) // Q_PER_STEP
    last = nq // Q_PER_STEP - 1
    lag = KEEP // Q_PER_STEP
    kv_scr = pltpu.VMEM((N_HEADS, S + PADK, HEAD_DIM), BF)
    return pl.pallas_call(
        _after(body, 6, after), name="attn_bwd", grid=(n_steps,),
        out_shape=(jax.ShapeDtypeStruct((S, D_A), BF), jax.ShapeDtypeStruct((S, D_A), BF),
                   jax.ShapeDtypeStruct((S, D_A), BF), jax.ShapeDtypeStruct((N_HEADS, ROLL_W), F32)),
        in_specs=[pl.BlockSpec((N_HEADS, rows_per_step, HEAD_DIM), lambda g: (0, jnp.minimum(g, last), 0)),
                  pl.BlockSpec((N_HEADS, rows_per_step, HEAD_DIM), lambda g: (0, jnp.minimum(g, last), 0)),
                  pl.BlockSpec(memory_space=pl.ANY), pl.BlockSpec(memory_space=pl.ANY),
                  pl.BlockSpec((N_HEADS, rows_per_step, 128), lambda g: (0, jnp.minimum(g, last), 0)),
                  pl.BlockSpec((N_HEADS, QB, KB), lambda g: (0, 0, 0))] + [_ANY] * len(after),
        out_specs=[pl.BlockSpec((rows_per_step, D_A), lambda g: (jnp.minimum(g, last), 0)),
                   pl.BlockSpec((rows_per_step, D_A), lambda g: (jnp.maximum(g - lag, 0), 0)),
                   pl.BlockSpec((rows_per_step, D_A), lambda g: (jnp.maximum(g - lag, 0), 0)),
                   pl.BlockSpec((N_HEADS, ROLL_W), lambda g: (0, 0))],
        scratch_shapes=[kv_scr, kv_scr,
                        pltpu.VMEM((N_HEADS, KB, HEAD_DIM), F32), pltpu.VMEM((N_HEADS, KB, HEAD_DIM), F32),
                        pltpu.VMEM((N_HEADS, QB, KB), F32), pltpu.VMEM((QB, ROLL_W), F32),
                        pltpu.SemaphoreType.DMA((2,))],
        compiler_params=_params(56, dimension_semantics=("arbitrary",)),
    )(q3, d_att3, k3, v3, lse, bias, *after)


def _sgu_core(ub, vb, lg, lb):
    u, du = _gelu_and_grad(ub)
    v, dv = _gelu_and_grad(vb)
    mu = jnp.mean(v, axis=-1, keepdims=True)
    vc = v - mu
    rstd = lax.rsqrt(jnp.mean(vc * vc, axis=-1, keepdims=True) + EPS)
    xh = vc * rstd
    vn = xh * lg + lb
    return u, du, dv, rstd, xh, vn


def _tri():
    r = lax.broadcasted_iota(jnp.int32, (SGU_CHUNK, SGU_CHUNK), 0)
    c = lax.broadcasted_iota(jnp.int32, (SGU_CHUNK, SGU_CHUNK), 1)
    return r >= c


def _tail_sgu(att, zrest, x, target, w_pa, w_pb, w_out, b_gate, final_g, ln_g, ln_b, w_s, b_s_t, tm=256):
    S = x.shape[0]
    nt = S // tm
    chunks = tm // SGU_CHUNK

    def body(att_ref, ga_ref, ub_ref, vb_ref, gb_ref, gta_ref, gtb_ref, x_ref, t_ref,
             wpa_ref, wpb_ref, wout_ref, bg_ref, fg_ref, lg_ref, lb_ref, ws_ref, bst_ref,
             dout_ref, datt_ref, dzt_ref, dzs_ref, gwout_hbm, gwpa_hbm, gwpb_hbm,
             gbg_ref, gfg_ref, loss_ref, gws_ref, gbs_ref, glg_ref, glb_ref,
             acc_out, acc_pa, acc_pb, sg_scr, mix_scr, dvn_scr, bs_acc, sems):
        i = pl.program_id(0)

        @pl.when(i == 0)
        def _():
            for r in (acc_out, acc_pa, acc_pb, gbg_ref, gfg_ref, loss_ref, gws_ref, glg_ref, glb_ref, bs_acc):
                r[...] = jnp.zeros_like(r)

        u, du, dv, rstd, xh, vn = _sgu_core(ub_ref[...], vb_ref[...], lg_ref[...], lb_ref[...])
        vnb = vn.astype(BF)
        tri = _tri()
        blocks = [(g, slice(n * SGU_CHUNK, (n + 1) * SGU_CHUNK), slice(g * 128, (g + 1) * 128))
                  for g in range(N_GROUPS) for n in range(chunks)]
        wts = [jnp.where(tri, ws_ref[g], 0.0) for g in range(N_GROUPS)]
        for g, rs, cs in blocks:
            mixed = _dot(wts[g].astype(BF), vnb[rs, cs]) + bst_ref[:, g:g + 1]
            mix_scr[rs, cs] = mixed
            sg_scr[rs, cs] = u[rs, cs] * mixed

        att = att_ref[...]
        sg = sg_scr[...]
        sa, dsa = _silu_and_grad(ga_ref[...])
        sb, dsb = _silu_and_grad(gb_ref[...])
        ya = (att * sa).astype(BF)
        yb = (sg * sb).astype(BF)
        pa = _dot(ya, wpa_ref[...])
        pb = _dot(yb, wpb_ref[...])
        ga = _sigmoid(gta_ref[...] + bg_ref[:, 0:D_MODEL])
        gb = _sigmoid(gtb_ref[...] + bg_ref[:, D_MODEL:2 * D_MODEL])
        merged = (ga * pa + gb * pb).astype(BF)
        out = x_ref[...] + _dot(merged, wout_ref[...])
        r2 = lax.rsqrt(jnp.mean(out * out, axis=-1, keepdims=True) + EPS)
        nrm = out * r2
        fg = fg_ref[...]
        err = nrm * fg - t_ref[...]
        loss_ref[...] += 0.5 * jnp.sum(jnp.mean(err * err, axis=-1, keepdims=True))
        dy = err * (1.0 / D_MODEL)
        gfg_ref[...] += jnp.sum(dy * nrm, axis=0, keepdims=True)
        dn = dy * fg
        d_out = r2 * (dn - nrm * jnp.mean(dn * nrm, axis=-1, keepdims=True))
        dout_ref[...] = d_out
        d_outb = d_out.astype(BF)
        acc_out[...] += _dot_tn(merged, d_outb)
        dm = _dot_nt(d_outb, wout_ref[...])
        d_pa = (dm * ga).astype(BF)
        d_pb = (dm * gb).astype(BF)
        d_gta = dm * pa * (ga * (1.0 - ga))
        d_gtb = dm * pb * (gb * (1.0 - gb))
        gbg_ref[:, 0:D_MODEL] += jnp.sum(d_gta, axis=0, keepdims=True)
        gbg_ref[:, D_MODEL:2 * D_MODEL] += jnp.sum(d_gtb, axis=0, keepdims=True)
        dzt_ref[:, 2 * D_A:2 * D_A + D_MODEL] = d_gta.astype(BF)
        dzt_ref[:, 2 * D_A + D_MODEL:] = d_gtb.astype(BF)
        acc_pa[...] += _dot_tn(ya, d_pa)
        acc_pb[...] += _dot_tn(yb, d_pb)
        d_ya = _dot_nt(d_pa, wpa_ref[...])
        d_yb = _dot_nt(d_pb, wpb_ref[...])
        d_att = (d_ya * sa).astype(BF)
        for hd in range(N_HEADS):
            datt_ref[hd] = d_att[:, hd * HEAD_DIM:(hd + 1) * HEAD_DIM]
        dzt_ref[:, 0:D_A] = (d_ya * att * dsa).astype(BF)
        dzt_ref[:, D_A:2 * D_A] = (d_yb * sg * dsb).astype(BF)

        dsg = d_yb * sb
        dzs_ref[:, 0:D_B] = (dsg * mix_scr[...] * du).astype(BF)
        dmix = dsg * u
        for g, rs, cs in blocks:
            dmb = dmix[rs, cs].astype(BF)
            bs_acc[:, cs] += dmix[rs, cs]
            gws_ref[g] += _dot_nt(dmb, vnb[rs, cs])
            dvn_scr[rs, cs] = _dot(wts[g].T.astype(BF), dmb)
        dvn = dvn_scr[...]
        glg_ref[...] += jnp.sum(dvn * xh, axis=0, keepdims=True)
        glb_ref[...] += jnp.sum(dvn, axis=0, keepdims=True)
        dxh = dvn * lg_ref[...]
        dvv = rstd * (dxh - jnp.mean(dxh, axis=-1, keepdims=True)
                      - xh * jnp.mean(dxh * xh, axis=-1, keepdims=True))
        dzs_ref[:, D_B:2 * D_B] = (dvv * dv).astype(BF)

        @pl.when(i == nt - 1)
        def _():
            cps = [pltpu.make_async_copy(acc_out, gwout_hbm, sems.at[0]),
                   pltpu.make_async_copy(acc_pa, gwpa_hbm, sems.at[1]),
                   pltpu.make_async_copy(acc_pb, gwpb_hbm, sems.at[2])]
            for cp in cps:
                cp.start()
            lane = lax.broadcasted_iota(jnp.int32, (SGU_CHUNK, 128), 1)
            cols = jnp.zeros((SGU_CHUNK, 128), F32)
            for g in range(N_GROUPS):
                gws_ref[g] = jnp.where(tri, gws_ref[g], 0.0)
                col = jnp.sum(bs_acc[:, g * 128:(g + 1) * 128], axis=-1, keepdims=True)
                cols = jnp.where(lane == g, col, cols)
            gbs_ref[...] = cols
            for cp in cps:
                cp.wait()

    c2 = lambda i: (0, 0)
    c3 = lambda i: (0, 0, 0)
    zcol = lambda w, blk: pl.BlockSpec((tm, w), lambda i: (i, blk))
    row = lambda w: pl.BlockSpec((tm, w), lambda i: (i, 0))
    return pl.pallas_call(
        body, name="tail", grid=(nt,),
        out_shape=(jax.ShapeDtypeStruct((S, D_MODEL), F32), jax.ShapeDtypeStruct((N_HEADS, S, HEAD_DIM), BF),
                   jax.ShapeDtypeStruct((S, 3072), BF), jax.ShapeDtypeStruct((S, 2 * D_B), BF),
                   jax.ShapeDtypeStruct((D_MODEL, D_MODEL), F32), jax.ShapeDtypeStruct((D_A, D_MODEL), F32),
                   jax.ShapeDtypeStruct((D_B, D_MODEL), F32),
                   jax.ShapeDtypeStruct((1, 2 * D_MODEL), F32), jax.ShapeDtypeStruct((1, D_MODEL), F32),
                   jax.ShapeDtypeStruct((1, 128), F32),
                   jax.ShapeDtypeStruct((N_GROUPS, 128, 128), F32), jax.ShapeDtypeStruct((SGU_CHUNK, 128), F32),
                   jax.ShapeDtypeStruct((1, D_B), F32), jax.ShapeDtypeStruct((1, D_B), F32)),
        in_specs=[row(D_A), zcol(512, 0), zcol(512, 1), zcol(512, 2), zcol(512, 3),
                  zcol(D_MODEL, 2), zcol(D_MODEL, 3), row(D_MODEL), row(D_MODEL),
                  pl.BlockSpec((D_A, D_MODEL), c2), pl.BlockSpec((D_B, D_MODEL), c2),
                  pl.BlockSpec((D_MODEL, D_MODEL), c2),
                  pl.BlockSpec((1, 2 * D_MODEL), c2), pl.BlockSpec((1, D_MODEL), c2),
                  pl.BlockSpec((1, D_B), c2), pl.BlockSpec((1, D_B), c2),
                  pl.BlockSpec((N_GROUPS, 128, 128), c3), pl.BlockSpec((128, N_GROUPS), c2)],
        out_specs=[row(D_MODEL), pl.BlockSpec((N_HEADS, tm, HEAD_DIM), lambda i: (0, i, 0)),
                   row(3072), row(2 * D_B), _ANY, _ANY, _ANY,
                   pl.BlockSpec((1, 2 * D_MODEL), c2), pl.BlockSpec((1, D_MODEL), c2),
                   pl.BlockSpec((1, 128), c2),
                   pl.BlockSpec((N_GROUPS, 128, 128), c3), pl.BlockSpec((SGU_CHUNK, 128), c2),
                   pl.BlockSpec((1, D_B), c2), pl.BlockSpec((1, D_B), c2)],
        scratch_shapes=[pltpu.VMEM((D_MODEL, D_MODEL), F32), pltpu.VMEM((D_A, D_MODEL), F32),
                        pltpu.VMEM((D_B, D_MODEL), F32),
                        pltpu.VMEM((tm, D_B), F32), pltpu.VMEM((tm, D_B), F32), pltpu.VMEM((tm, D_B), F32),
                        pltpu.VMEM((SGU_CHUNK, D_B), F32), pltpu.SemaphoreType.DMA((3,))],
        compiler_params=_params(58, dimension_semantics=("arbitrary",)),
    )(att, zrest, zrest, zrest, zrest, zrest, zrest, x, target, w_pa, w_pb, w_out, b_gate, final_g,
      ln_g, ln_b, w_s, b_s_t)


_DZ_MAP = ((0, 0), (1, 0), (2, 0), (3, 0), (4, 0), (4, 1), (3, 1), (3, 2), (3, 3), (3, 4), (3, 5))


def _dh_gradx(dq, dk, dv, dzt, dzs, w_in_bf, x, norm_g, d_out, tm=512, after=()):
    S = x.shape[0]

    def body(dq_ref, dk_ref, dv_ref, dzt_ref, dzs_ref, w_ref, x_ref, g_ref, dout_ref, gx_ref, gn_ref):
        i = pl.program_id(0)

        @pl.when(i == 0)
        def _():
            gn_ref[...] = jnp.zeros_like(gn_ref)

        pieces = (dq_ref, dk_ref, dv_ref, dzt_ref, dzs_ref)
        dh = jnp.zeros((tm, D_MODEL), F32)
        for j, (pc, blk) in enumerate(_DZ_MAP):
            dh += _dot_nt(pieces[pc][:, blk * 512:(blk + 1) * 512], w_ref[:, j * 512:(j + 1) * 512])
        xv = x_ref[...]
        r = lax.rsqrt(jnp.mean(xv * xv, axis=-1, keepdims=True) + EPS)
        nrm = xv * r
        gn_ref[...] += jnp.sum(dh * nrm, axis=0, keepdims=True)
        dn = dh * g_ref[...]
        gx_ref[...] = r * (dn - nrm * jnp.mean(dn * nrm, axis=-1, keepdims=True)) + dout_ref[...]

    row = lambda w: pl.BlockSpec((tm, w), lambda i: (i, 0))
    c2 = lambda i: (0, 0)
    return pl.pallas_call(
        _after(body, 9, after), name="dh_gradx", grid=(S // tm,),
        out_shape=(jax.ShapeDtypeStruct((S, D_MODEL), F32), jax.ShapeDtypeStruct((1, D_MODEL), F32)),
        in_specs=[row(512), row(512), row(512), row(3072), row(1024),
                  pl.BlockSpec((D_MODEL, D_IN), c2, pipeline_mode=pl.Buffered(1)), row(D_MODEL),
                  pl.BlockSpec((1, D_MODEL), c2), row(D_MODEL)]
        + [_ANY] * len(after),
        out_specs=[row(D_MODEL), pl.BlockSpec((1, D_MODEL), c2)],
        compiler_params=_params(48, dimension_semantics=("arbitrary",)),
    )(dq, dk, dv, dzt, dzs, w_in_bf, x, norm_g, d_out, *after)


def _gw_in(ht, dq, dk, dv, dzt, dzs, tn=512, after=()):
    S = ht.shape[1]
    per = 512 // tn
    cols = tuple((pc, per * blk + h) for pc, blk in _DZ_MAP for h in range(per))

    def body(ht_ref, dq_ref, dk_ref, dv_ref, dzt_ref, dzs_ref, o_ref, ob_ref):
        j = pl.program_id(0)
        pieces = (dq_ref, dk_ref, dv_ref, dzt_ref, dzs_ref)
        for pc in range(5):
            hit = functools.reduce(jnp.logical_or, [j == jj for jj, (p, _) in enumerate(cols) if p == pc])

            @pl.when(hit)
            def _(pc=pc):
                g = _dot(ht_ref[...], pieces[pc][...])
                o_ref[...] = g
                ob_ref[...] = g.astype(BF)

    def piece_spec(pc):
        cur = next(blk for p, blk in cols if p == pc)
        held = []
        for p, blk in cols:
            cur = blk if p == pc else cur
            held.append(cur)

        def index_map(j):
            blk = jnp.int32(held[0])
            for jj in range(1, len(held)):
                if held[jj] != held[jj - 1]:
                    blk = jnp.where(j >= jj, jnp.int32(held[jj]), blk)
            return (0, blk)

        return pl.BlockSpec((S, tn), index_map)

    return pl.pallas_call(
        _after(body, 6, after), name="gw_in", grid=(len(cols),),
        out_shape=(jax.ShapeDtypeStruct((D_MODEL, D_IN), F32), jax.ShapeDtypeStruct((D_MODEL, D_IN), BF)),
        in_specs=[pl.BlockSpec((D_MODEL, S), lambda j: (0, 0), pipeline_mode=pl.Buffered(1))]
        + [piece_spec(pc) for pc in range(5)]
        + [_ANY] * len(after),
        out_specs=[pl.BlockSpec((D_MODEL, tn), lambda j: (0, j)), pl.BlockSpec((D_MODEL, tn), lambda j: (0, j))],
        compiler_params=_params(56, dimension_semantics=("arbitrary",)),
    )(ht, dq, dk, dv, dzt, dzs, *after)


_HBM = pl.BlockSpec(memory_space=pltpu.HBM)
_SEM = pl.BlockSpec(memory_space=pltpu.SEMAPHORE)
_ANY = pl.BlockSpec(memory_space=pl.ANY)
_EFFECT = pltpu.SideEffectType.DATAFLOW_SIDE_EFFECTING


def _in_hbm(a):
    return pltpu.with_memory_space_constraint(a, pltpu.HBM)


def _after(body, n_in, after):
    if not after:
        return body
    return lambda *refs: body(*refs[:n_in], *refs[n_in + len(after):])


class _Started:
    def __init__(self, send, recv, bufs, token):
        self.send, self.recv, self.bufs, self.token = send, recv, bufs, token


_PEER_SETS = {"sibling": 7, "chips": 8, "both": 9}


def _peers(kind):
    x, y, c, chips = _mesh_pos()
    return ([(x, y, 1 - c)] if kind in ("sibling", "both") else []) + (
        [(cx, cy, c) for cx, cy in chips] if kind in ("chips", "both") else [])


def _signal_peers(kind):
    barrier = pltpu.get_barrier_semaphore()
    targets = _peers(kind)
    for peer in targets:
        pl.semaphore_signal(barrier, inc=1, device_id=peer, device_id_type=MESH)
    return lambda: pl.semaphore_wait(barrier, len(targets))


def _split_start(name, bufs, n_copies, copies, peers, after=()):
    nb = len(bufs)

    def body(*refs):
        _signal_peers(peers)()
        refs = refs[:nb] + refs[nb + len(after):]
        for cp in copies(refs[:nb], refs[nb], refs[nb + 1]):
            cp.start()
        refs[-1][...] = jnp.zeros_like(refs[-1])

    outs = pl.pallas_call(
        body, name=name,
        out_shape=(pltpu.SemaphoreType.DMA((n_copies,)), pltpu.SemaphoreType.DMA((n_copies,)),
                   *[pltpu.HBM(b.shape, b.dtype) for b in bufs], jax.ShapeDtypeStruct((8, 128), F32)),
        in_specs=[_HBM] * nb + [_ANY] * len(after),
        out_specs=(_SEM, _SEM, *[_HBM] * nb, pl.BlockSpec(memory_space=pltpu.VMEM)),
        input_output_aliases={k: 2 + k for k in range(nb)},
        compiler_params=_params(1, has_side_effects=_EFFECT, collective_id=_PEER_SETS[peers]),
    )(*[_in_hbm(b) for b in bufs], *after)
    return _Started(outs[0], outs[1], list(outs[2:2 + nb]), outs[-1])


def _split_wait(name, started, copies, after):
    nb = len(started.bufs)
    after = tuple(after) if isinstance(after, (tuple, list)) else (after,)

    def body(*refs):
        for cp in copies(refs[:nb], refs[nb], refs[nb + 1]):
            cp.wait_send()
            cp.wait_recv()

    return list(pl.pallas_call(
        body, name=name,
        out_shape=tuple(pltpu.HBM(b.shape, b.dtype) for b in started.bufs),
        in_specs=[_HBM] * nb + [_SEM, _SEM] + [_ANY] * len(after),
        out_specs=tuple([_HBM] * nb),
        input_output_aliases={k: k for k in range(nb)},
        compiler_params=_params(1, has_side_effects=_EFFECT),
    )(*started.bufs, started.send, started.recv, *after))


def _x1_copies(ws):
    def copies(refs, send_sems, recv_sems):
        x, y, c, _ = _mesh_pos()
        out = []
        for k, w in enumerate(ws):
            for s in range(N_SHARD):
                out.append(pltpu.make_async_remote_copy(
                    src_ref=_UNITS[w](refs[k], s, 1 - c), dst_ref=refs[len(ws) + k].at[s],
                    send_sem=send_sems.at[N_SHARD * k + s], recv_sem=recv_sems.at[N_SHARD * k + s],
                    device_id=(x, y, 1 - c), device_id_type=MESH))
        return out
    return copies


def _x2_copies(n):
    def copies(refs, send_sems, recv_sems):
        x, y, c, chips = _mesh_pos()
        out = []
        for j, (cx, cy) in enumerate(chips):
            for k in range(n):
                out.append(pltpu.make_async_remote_copy(
                    src_ref=refs[k].at[2 * cx + cy], dst_ref=refs[n + k].at[j],
                    send_sem=send_sems.at[3 * k + j], recv_sem=recv_sems.at[3 * k + j],
                    device_id=(cx, cy, c), device_id_type=MESH))
        return out
    return copies


def _x3_copies(ws):
    def copies(refs, send_sems, recv_sems):
        x, y, c, _ = _mesh_pos()
        out = []
        for k, w in enumerate(ws):
            rows = _HALF_ROWS[w]
            mine = refs[k].at[pl.ds(_mo(c * rows, rows), rows), :]
            out.append(pltpu.make_async_remote_copy(
                src_ref=mine, dst_ref=mine, send_sem=send_sems.at[k], recv_sem=recv_sems.at[k],
                device_id=(x, y, 1 - c), device_id_type=MESH))
        return out
    return copies


def _x1_lands(ws, dtype=F32):
    return [lax.empty((N_SHARD,) + _UNIT_SHAPES[w], dtype) for w in ws]


def _x2_lands(ws):
    return [lax.empty((3,) + _UNIT_SHAPES[w], BF) for w in ws]


def _grad_add1(w, g, recv, pos):
    ur, uc = _UNIT_SHAPES[w]

    def body(pos_ref, g_ref, r_ref, csb_ref):
        csb_ref[0] = (g_ref[...] + r_ref[0].astype(F32)).astype(BF)

    u3 = lambda k, pos: (pos[2 + k], 0, 0)
    return pl.pallas_call(
        body, name=f"grad_add1_{w}",
        grid_spec=pltpu.PrefetchScalarGridSpec(
            num_scalar_prefetch=1, grid=(N_SHARD - 1,),
            in_specs=[pl.BlockSpec((ur, uc), lambda k, pos: (pos[0], pos[2 + k])), pl.BlockSpec((1, ur, uc), u3)],
            out_specs=pl.BlockSpec((1, ur, uc), u3)),
        out_shape=jax.ShapeDtypeStruct((N_SHARD, ur, uc), BF),
        compiler_params=_params(40, dimension_semantics=("arbitrary",)),
    )(pos, g, recv)


def _grad_add1_group(ws, gs, recvs, pos):
    n = len(ws)

    def body(pos_ref, *refs):
        s = pl.program_id(0)
        for k in range(n):
            g, r, own, csb = refs[k], refs[n + k], refs[2 * n + k], refs[3 * n + k]
            v = g[...] + r[0]
            csb[0] = v.astype(BF)

            @pl.when(s == pos_ref[1])
            def _(own=own, v=v):
                own[...] = v

    def g_spec(w):
        if w == 3:
            return pl.BlockSpec(_UNIT_SHAPES[w], lambda s, pos: (2 * s + pos[0], 0))
        return pl.BlockSpec(_UNIT_SHAPES[w], lambda s, pos: (pos[0], s))

    slot = lambda w: pl.BlockSpec((1,) + _UNIT_SHAPES[w], lambda s, pos: (s, 0, 0))
    outs = pl.pallas_call(
        body, name="grad_add1_group",
        grid_spec=pltpu.PrefetchScalarGridSpec(
            num_scalar_prefetch=1, grid=(N_SHARD,),
            in_specs=[g_spec(w) for w in ws] + [slot(w) for w in ws],
            out_specs=[pl.BlockSpec(_UNIT_SHAPES[w], lambda s, pos: (0, 0)) for w in ws] + [slot(w) for w in ws]),
        out_shape=tuple(jax.ShapeDtypeStruct(_UNIT_SHAPES[w], F32) for w in ws)
        + tuple(jax.ShapeDtypeStruct((N_SHARD,) + _UNIT_SHAPES[w], BF) for w in ws),
        compiler_params=_params(32, dimension_semantics=("arbitrary",)),
    )(pos, *gs, *recvs)
    return list(outs[:n]), list(outs[n:])


def _grad_add2_group(ws, owns, recvs):
    n = len(ws)

    def body(*refs):
        c = lax.axis_index("c")
        for k, w in enumerate(ws):
            own, r, o = refs[k], refs[n + k], refs[2 * n + k]
            rows = _HALF_ROWS[w]
            total = ((own[...] + r[0].astype(F32)) + r[1].astype(F32)) + r[2].astype(F32)
            o[pl.ds(_mo(c * rows, rows), rows), :] = total

    vm = pl.BlockSpec(memory_space=pltpu.VMEM)
    return list(pl.pallas_call(
        body, name="grad_add2_group",
        out_shape=tuple(jax.ShapeDtypeStruct(_SHARD_SHAPES[w], F32) for w in ws),
        in_specs=[vm] * (2 * n), out_specs=[vm] * n,
        compiler_params=_params(32),
    )(*owns, *recvs))


def _grad_add2(w, g, recv1, recv2, pos):
    ur, uc = _UNIT_SHAPES[w]
    nt = 4
    tr = ur // nt

    def body(pos_ref, g_ref, r1_ref, r2_ref, o_ref):
        own = g_ref[...] + r1_ref[0].astype(F32)
        o_ref[...] = ((own + r2_ref[0].astype(F32)) + r2_ref[1].astype(F32)) + r2_ref[2].astype(F32)

    mine = lambda t, pos: (pos[0] * nt + t, 0)
    return pl.pallas_call(
        body, name=f"grad_add2_{w}",
        grid_spec=pltpu.PrefetchScalarGridSpec(
            num_scalar_prefetch=1, grid=(nt,),
            in_specs=[pl.BlockSpec((tr, uc), lambda t, pos: (pos[0] * nt + t, pos[1])),
                      pl.BlockSpec((1, tr, uc), lambda t, pos: (pos[1], t, 0)),
                      pl.BlockSpec((3, tr, uc), lambda t, pos: (0, t, 0))],
            out_specs=pl.BlockSpec((tr, uc), mine)),
        out_shape=jax.ShapeDtypeStruct(_SHARD_SHAPES[w], F32),
        compiler_params=_params(32, dimension_semantics=("arbitrary",)),
    )(pos, g, recv1, recv2)


def _adamw_math(w, g, m, v):
    m = ADAM_B1 * m + (1.0 - ADAM_B1) * g
    v = ADAM_B2 * v + (1.0 - ADAM_B2) * (g * g)
    m_hat = m / ADAM_C1
    v_hat = v / ADAM_C2
    delta = -ADAM_LR * (m_hat / (jnp.sqrt(v_hat) + ADAM_EPS) + ADAM_WD * w)
    return delta, m, v


ADAMW_STEPS = 4


def _adamw(ws_, gs, ms, vs):
    n = len(ws_)

    def body(*refs):
        for k in range(n):
            w, g, m, v = (refs[j * n + k] for j in range(4))
            d, nm, nv, gc = (refs[(4 + j) * n + k] for j in range(4))
            gv = g[...]
            d[...], nm[...], nv[...] = _adamw_math(w[...], gv, m[...], v[...])
            gc[...] = gv

    specs = [pl.BlockSpec((a.shape[0] // ADAMW_STEPS, a.shape[1]), lambda i: (i, 0)) for a in ws_] * 4
    outs = pl.pallas_call(
        body, name="adamw", grid=(ADAMW_STEPS,),
        out_shape=tuple(jax.ShapeDtypeStruct(a.shape, F32) for _ in range(4) for a in ws_),
        in_specs=specs, out_specs=specs,
        compiler_params=_params(40, dimension_semantics=("arbitrary",)),
    )(*ws_, *gs, *ms, *vs)
    return [tuple(outs[j * n + k] for j in range(4)) for k in range(n)]


_REL_PAD = 384
_VEC_FIELDS = (("norm_g", 0, D_MODEL), ("b_gate", 1024, 2 * D_MODEL), ("sgu_ln_g", 3072, D_B),
               ("sgu_ln_b", 3584, D_B), ("b_s", 4096, N_GROUPS * 128), ("final_g", 4608, D_MODEL))
_LOSS_OFF = 5632
_REL_OFF = 5760
_NV = _REL_OFF + N_HEADS * _REL_PAD
_N_FIELDS = len(_VEC_FIELDS) + 2


_B_S_FIELD = [f[0] for f in _VEC_FIELDS].index("b_s")


def _assemble_row(dst, fields, transposed_b_s):
    for f, (_, off, n) in enumerate(_VEC_FIELDS):
        if transposed_b_s and f == _B_S_FIELD:
            t = fields[f][...].T
            for g in range(N_GROUPS):
                dst[:, off + 128 * g:off + 128 * (g + 1)] = t[g:g + 1, :]
        else:
            dst[:, off:off + n] = fields[f][...]
    for r in range(N_HEADS):
        dst[:, _REL_OFF + _REL_PAD * r:_REL_OFF + _REL_PAD * (r + 1)] = fields[len(_VEC_FIELDS)][r:r + 1, :]


def _small_reduce(grads, loss_row, after=()):
    n_in = _N_FIELDS + 1

    def body(*refs):
        g_refs, loss_ref = refs[:_N_FIELDS], refs[_N_FIELDS]
        out_v, out_w = refs[n_in:n_in + 2]
        mine_v, mine_w, gath_v, gath_w, send_sems, recv_sems = refs[n_in + 2:]
        x, y, c, chips = _mesh_pos()
        me, sibling = (x, y, c), (x, y, 1 - c)

        peers_entered = _signal_peers("both")
        _assemble_row(mine_v, g_refs, True)
        mine_v[:, _LOSS_OFF:_LOSS_OFF + 128] = loss_ref[...]
        mine_w[...] = g_refs[-1][...].astype(BF)
        peers_entered()
        my_k = 4 * x + 2 * y + c
        gath_v[my_k] = mine_v[...]
        gath_w[my_k] = mine_w[...]

        def copy(k, gath, block, to, src=None):
            dst = gath.at[4 * block[0] + 2 * block[1] + block[2]]
            return pltpu.make_async_remote_copy(
                src_ref=dst if src is None else src, dst_ref=dst,
                send_sem=send_sems.at[k], recv_sem=recv_sems.at[k], device_id=to, device_id_type=MESH)

        bufs = ((gath_v, mine_v), (gath_w, mine_w))
        first, passed = [], []
        for b, (gath, mine) in enumerate(bufs):
            first.append(copy(7 * b, gath, me, sibling, src=mine))
            first += [copy(7 * b + 1 + j, gath, me, (*chip, c), src=mine) for j, chip in enumerate(chips)]
        for cp in first:
            cp.start()
        for b, (gath, _) in enumerate(bufs):
            for j, chip in enumerate(chips):
                copy(7 * b + 1 + j, gath, (*chip, c), me).wait_recv()
                cp = copy(7 * b + 4 + j, gath, (*chip, c), sibling)
                cp.start()
                passed.append(cp)
        for b, (gath, _) in enumerate(bufs):
            copy(7 * b, gath, sibling, me).wait_recv()
            for j, chip in enumerate(chips):
                copy(7 * b + 4 + j, gath, (*chip, 1 - c), me).wait_recv()
        for cp in first + passed:
            cp.wait_send()

        tot_v, tot_w = gath_v[0], gath_w[0].astype(F32)
        for k in range(1, 8):
            tot_v = tot_v + gath_v[k]
            tot_w = tot_w + gath_w[k].astype(F32)
        out_v[...] = tot_v
        out_w[...] = tot_w

    vm = pl.BlockSpec(memory_space=pltpu.VMEM)
    return pl.pallas_call(
        _after(body, n_in, after), name="small_reduce",
        out_shape=(jax.ShapeDtypeStruct((1, _NV), F32), jax.ShapeDtypeStruct((N_GROUPS * 128, 128), F32)),
        in_specs=[vm] * n_in + [_ANY] * len(after), out_specs=[vm] * 2,
        scratch_shapes=[pltpu.VMEM((1, _NV), F32), pltpu.VMEM((N_GROUPS * 128, 128), BF),
                        pltpu.VMEM((8, 1, _NV), F32), pltpu.VMEM((8, N_GROUPS * 128, 128), BF),
                        pltpu.SemaphoreType.DMA((14,)), pltpu.SemaphoreType.DMA((14,))],
        compiler_params=_params(32, collective_id=_PEER_SETS["both"]),
    )(*grads, loss_row, *after)


def _small_adamw(tot_v, tot_w, params):
    n_in = 2 + 3 * _N_FIELDS

    def body(*refs):
        tv_ref, tw_ref = refs[:2]
        p_refs = [refs[2 + k * _N_FIELDS:2 + (k + 1) * _N_FIELDS] for k in range(3)]
        outs = refs[n_in:n_in + 4 * _N_FIELDS + 1]
        wmv = refs[-1]
        for k in range(3):
            _assemble_row(wmv.at[k], p_refs[k], False)
            wmv[k, :, _LOSS_OFF:_LOSS_OFF + 128] = jnp.zeros((1, 128), F32)
        tot_v, tot_w = tv_ref[...], tw_ref[...]
        res_v = (tot_v,) + _adamw_math(wmv[0], tot_v, wmv[1], wmv[2])
        res_w = (tot_w,) + _adamw_math(p_refs[0][-1][...], tot_w, p_refs[1][-1][...], p_refs[2][-1][...])
        for kind in range(4):
            o = outs[kind * _N_FIELDS:(kind + 1) * _N_FIELDS]
            for f, (_, off, n) in enumerate(_VEC_FIELDS):
                o[f][...] = res_v[kind][:, off:off + n]
            for r in range(N_HEADS):
                o[len(_VEC_FIELDS)][r:r + 1, :] = res_v[kind][:, _REL_OFF + _REL_PAD * r:_REL_OFF + _REL_PAD * r + N_REL]
            o[-1][...] = res_w[kind]
        outs[-1][...] = tot_v[:, _LOSS_OFF:_LOSS_OFF + 128]

    field_shapes = [(1, n) for _, _, n in _VEC_FIELDS] + [(N_HEADS, N_REL), (N_GROUPS * 128, 128)]
    vm = pl.BlockSpec(memory_space=pltpu.VMEM)
    operands = [tot_v, tot_w] + [a for p in params for a in p]
    assert len(operands) == n_in
    outs = pl.pallas_call(
        body, name="small_adamw",
        out_shape=tuple(jax.ShapeDtypeStruct(s, F32) for _ in range(4) for s in field_shapes)
        + (jax.ShapeDtypeStruct((1, 128), F32),),
        in_specs=[vm] * n_in, out_specs=[vm] * (4 * _N_FIELDS + 1),
        scratch_shapes=[pltpu.VMEM((3, 1, _NV), F32)],
        compiler_params=_params(32),
    )(*operands)
    return [outs[k * _N_FIELDS:(k + 1) * _N_FIELDS] for k in range(4)], outs[-1]


def _small_fields(norm_g, b_gate, ln_g, ln_b, b_s, final_g, rel_bias, w_s):
    rel = jnp.pad(rel_bias.reshape(N_HEADS, N_REL), ((0, 0), (0, _REL_PAD - N_REL)))
    return (norm_g, b_gate, ln_g, ln_b, b_s.reshape(1, N_GROUPS * 128), final_g.reshape(1, D_MODEL),
            rel, w_s.reshape(N_GROUPS * 128, 128))


def _small_outputs(fields):
    n_g, b_g, l_g, l_b, b_s, f_g, rel, w_s = fields
    return (n_g, b_g, rel.reshape(1, N_HEADS, N_REL), l_g, l_b,
            w_s.reshape(1, N_GROUPS, 128, 128), b_s.reshape(1, N_GROUPS, 128), f_g.reshape(D_MODEL))


def kernel(x, norm_g, w_in, b_gate, rel_bias, sgu_ln_g, sgu_ln_b, w_s, b_s, w_pa, w_pb, w_out, final_g, loss_target, m_norm_g, m_w_in, m_b_gate, m_rel_bias, m_sgu_ln_g, m_sgu_ln_b, m_w_s, m_b_s, m_w_pa, m_w_pb, m_w_out, m_final_g, v_norm_g, v_w_in, v_b_gate, v_rel_bias, v_sgu_ln_g, v_sgu_ln_b, v_w_s, v_b_s, v_w_pa, v_w_pb, v_w_out, v_final_g):
    S = x.shape[1]
    xs = x.reshape(S, D_MODEL)
    tgt = loss_target.reshape(S, D_MODEL)
    big_w = (w_in[0], w_pa[0], w_pb[0], w_out[0])
    big_m = (m_w_in[0], m_w_pa[0], m_w_pb[0], m_w_out[0])
    big_v = (v_w_in[0], v_w_pa[0], v_w_pb[0], v_w_out[0])
    rel = rel_bias[0]
    ws = w_s[0]
    bst = b_s[0].T
    fg = final_g.reshape(1, D_MODEL)
    chip = 2 * lax.axis_index("x") + lax.axis_index("y")
    pos = jnp.stack([lax.axis_index("c"), chip] + [(chip + k) % N_SHARD for k in range(1, N_SHARD)]).astype(jnp.int32)

    (w_in_bf,), staged, band_bias = _ag_weights((0,), big_w[:1], (1, 2, 3), big_w[1:], rel)
    ag_s = _split_start("ag_small_start", staged, 9, _gather_copies((1, 2, 3)), "chips", after=(w_in_bf,))

    ht, q3, k3, v3, zrest = _inproj_fwd(xs, norm_g, w_in_bf, after=(ag_s.token,))
    att, lse = _attn_fwd(q3, k3, v3, band_bias)
    w_pa_bf, w_pb_bf, w_out_bf = _split_wait("ag_small_wait", ag_s, _gather_copies((1, 2, 3)), att)
    (d_out, d_att, dzt, dzs, gw_out, gw_pa, gw_pb, g_bgate, g_final, loss_row,
     g_ws, g_bs_t, g_lng, g_lnb) = _tail_sgu(
        att, zrest, xs, tgt, w_pa_bf, w_pb_bf, w_out_bf, b_gate, fg, sgu_ln_g, sgu_ln_b, ws, bst)
    ws_s, ws_i = (1, 2, 3), (0,)

    x1s = _split_start("gx1s_start", [gw_pa, gw_pb, gw_out] + _x1_lands(ws_s), 12, _x1_copies(ws_s), "sibling")
    dq, dk, dv, d_gp = _attn_bwd(q3, k3, v3, d_att, lse, band_bias, after=(x1s.token,))
    got = _split_wait("gx1s_wait", x1s, _x1_copies(ws_s), dq)
    own_s, csb_s = _grad_add1_group(ws_s, got[:3], got[3:], pos)

    x2s = _split_start("gx2s_start", csb_s + _x2_lands(ws_s), 9, _x2_copies(3), "chips")
    gw_in, gw_in_bf = _gw_in(ht, dq, dk, dv, dzt, dzs, after=(x2s.token,))
    x1i = _split_start("gx1i_start", [gw_in_bf] + _x1_lands(ws_i, BF), 4, _x1_copies(ws_i), "sibling")
    got = _split_wait("gx2s_wait", x2s, _x2_copies(3), x1i.token)
    halves_s = _grad_add2_group(ws_s, own_s, got[3:])
    x3s = _split_start("gx3s_start", halves_s, 3, _x3_copies(ws_s), "sibling")
    g_rel = jnp.pad(d_gp[:, 384:384 + N_REL][:, ::-1], ((0, 0), (0, _REL_PAD - N_REL)))
    small_params = (_small_fields(norm_g, b_gate, sgu_ln_g, sgu_ln_b, b_s, final_g, rel_bias, w_s),
                    _small_fields(m_norm_g, m_b_gate, m_sgu_ln_g, m_sgu_ln_b, m_b_s, m_final_g, m_rel_bias, m_w_s),
                    _small_fields(v_norm_g, v_b_gate, v_sgu_ln_g, v_sgu_ln_b, v_b_s, v_final_g, v_rel_bias, v_w_s))
    relayouts = (g_rel,) + tuple(fields[-2] for fields in small_params)
    recv1_i = _split_wait("gx1i_wait", x1i, _x1_copies(ws_i), (x3s.token,) + relayouts)[1]
    csb_i = _grad_add1(0, gw_in, recv1_i, pos)

    x2i = _split_start("gx2i_start", [csb_i] + _x2_lands(ws_i), 3, _x2_copies(1), "chips")
    grad_x, g_norm = _dh_gradx(dq, dk, dv, dzt, dzs, w_in_bf, xs, norm_g, d_out, after=(x2i.token,))
    g_shards_s = _split_wait("gx3s_wait", x3s, _x3_copies(ws_s), grad_x)
    got = _split_wait("gx2i_wait", x2i, _x2_copies(1), grad_x)
    half_i = _grad_add2(0, gw_in, recv1_i, got[1], pos)
    x3i = _split_start("gx3i_start", [half_i], 1, _x3_copies(ws_i), "sibling")

    small_grads = (g_norm, g_bgate, g_lng, g_lnb, g_bs_t, g_final, g_rel, g_ws.reshape(N_GROUPS * 128, 128))
    tot_v, tot_w = _small_reduce(small_grads, loss_row, after=(x3i.token,))
    (gsum, sdelta, sm, sv), loss_out = _small_adamw(tot_v, tot_w, small_params)

    g_shard_i, = _split_wait("gx3i_wait", x3i, _x3_copies(ws_i), loss_out)
    big = _adamw(big_w, [g_shard_i] + g_shards_s, big_m, big_v)
    sg_out, sd_out, sm_out, sv_out = (_small_outputs(f) for f in (gsum, sdelta, sm, sv))
    loss = loss_out[0, 0]

    def assemble(small, bigs):
        n_g, b_g, r_b, l_g, l_b, w_s_, b_s_, f_g = small
        b_in, b_pa, b_pb, b_out = (b[None] for b in bigs)
        return (n_g, b_in, b_g, r_b, l_g, l_b, w_s_, b_s_, b_pa, b_pb, b_out, f_g)

    grads_out = assemble(sg_out, [b[3] for b in big])
    delta_out = assemble(sd_out, [b[0] for b in big])
    m_out = assemble(sm_out, [b[1] for b in big])
    v_out = assemble(sv_out, [b[2] for b in big])
    return (loss, grad_x.reshape(1, S, D_MODEL), *grads_out, *delta_out, *m_out, *v_out)
```

```python
import functools
import math

import jax
import jax.numpy as jnp
from jax import lax
from jax.experimental import pallas as pl
from jax.experimental.pallas import tpu as pltpu

F32 = jnp.float32
BF = jnp.bfloat16
MESH = pl.DeviceIdType.MESH

D_MODEL = 1024
D_A = 512
D_B = 512
D_IN = 5632
N_HEADS = 8
HEAD_DIM = 64
CHUNK = 64
N_PREV = 8
SGU_CHUNK = 128
N_GROUPS = 4
N_REL = 257
EPS = 1e-6
NEG_INF = -1e30
SCALE = HEAD_DIM ** -0.5

QB = 2 * CHUNK
KB = (N_PREV + 2) * CHUNK
PADK = N_PREV * CHUNK
ROLL_W = 1024
KEEP = KB // QB - 1
Q_PER_STEP = 2

ADAM_LR = 0.001
ADAM_B1 = 0.9
ADAM_B2 = 0.999
ADAM_EPS = 1e-08
ADAM_WD = 0.01
ADAM_STEP = 10
ADAM_C1 = 1.0 - ADAM_B1 ** ADAM_STEP
ADAM_C2 = 1.0 - ADAM_B2 ** ADAM_STEP

AG_PIECES = 4
N_SHARD = 4
SHARD_IN = D_IN // N_SHARD
MIB = 1024 * 1024


V7X_VMEM_MIB = 64
VMEM_RESERVE_MIB = V7X_VMEM_MIB - 4


def _params(vmem_mib, **kw):
    assert vmem_mib <= VMEM_RESERVE_MIB
    return pltpu.CompilerParams(vmem_limit_bytes=VMEM_RESERVE_MIB * MIB, **kw)


def _sigmoid(x):
    return 1.0 / (1.0 + jnp.exp(-x))


def _silu_and_grad(x):
    s = _sigmoid(x)
    return x * s, s * (1.0 + x * (1.0 - s))


_GELU_C = math.sqrt(2.0 / math.pi)
_GELU_A = 0.044715


def _gelu_and_grad(x):
    x2 = x * x
    t = jnp.tanh(_GELU_C * (x + _GELU_A * (x2 * x)))
    cdf = 0.5 * (1.0 + t)
    grad = cdf + 0.5 * x * (1.0 - t * t) * (_GELU_C * (1.0 + 3.0 * _GELU_A * x2))
    return x * cdf, grad


def _dot(a, b):
    return jnp.dot(a, b, preferred_element_type=F32)


def _dot_nt(a, b):
    return lax.dot_general(a, b, (((1,), (1,)), ((), ())), preferred_element_type=F32)


def _dot_tn(a, b):
    return lax.dot_general(a, b, (((0,), (0,)), ((), ())), preferred_element_type=F32)


def _mo(v, m):
    return v if isinstance(v, int) else pl.multiple_of(v, m)


def _unit_in(ref, s, p):
    return ref.at[pl.ds(_mo(p * 512, 512), 512), pl.ds(_mo(s * SHARD_IN, 128), SHARD_IN)]


def _unit_p(ref, s, p):
    return ref.at[pl.ds(_mo(p * 256, 256), 256), pl.ds(_mo(s * 256, 128), 256)]


def _unit_out(ref, s, p):
    return ref.at[pl.ds(_mo(s * 256 + p * 128, 128), 128), :]


_UNITS = (_unit_in, _unit_p, _unit_p, _unit_out)
_HALF_ROWS = (512, 256, 256, 128)
_UNIT_SHAPES = ((512, SHARD_IN), (256, 256), (256, 256), (128, D_MODEL))
_FULL_SHAPES = ((D_MODEL, D_IN), (D_A, D_MODEL), (D_B, D_MODEL), (D_MODEL, D_MODEL))
_SHARD_SHAPES = ((D_MODEL, SHARD_IN), (D_A, 256), (D_B, 256), (256, D_MODEL))


def _mesh_pos():
    x, y, c = lax.axis_index("x"), lax.axis_index("y"), lax.axis_index("c")
    chips = [(1 - x, y), (x, 1 - y), (1 - x, 1 - y)]
    return x, y, c, chips


def _bias_rows(rel_ref, pad_ref):
    pad_ref[...] = jnp.zeros(pad_ref.shape, F32)
    pad_ref[:, :N_REL] = rel_ref[...]
    r = pad_ref[...]
    m = lax.broadcasted_iota(jnp.int32, (pad_ref.shape[1], ROLL_W), 1)
    m = jnp.where(m >= ROLL_W - 192, m - ROLL_W, m)
    pick = (lax.broadcasted_iota(jnp.int32, m.shape, 0) == jnp.clip(512 - m, -128, 128) + 128).astype(BF)
    hi = r.astype(BF)
    mid = (r - hi.astype(F32)).astype(BF)
    lo = ((r - hi.astype(F32)) - mid.astype(F32)).astype(BF)
    return (_dot(hi, pick) + _dot(mid, pick)) + _dot(lo, pick)


def _ag_weights(ws, shards, later_ws, later_shards, rel):
    n, m = len(ws), len(later_ws)

    def body(*refs):
        ins, later_ins, rel_ref = refs[:n], refs[n:n + m], refs[n + m]
        o = n + m + 1
        outs, later_outs, bias_ref = refs[o:o + n], refs[o + n:o + n + m], refs[o + n + m]
        o += n + m + 1
        stage, later_stage = refs[o:o + n], refs[o + n:o + n + m]
        send_sems, recv_sems, local_sems, later_sems, rel_pad, gp_ref = refs[o + n + m:]
        x, y, c, chips = _mesh_pos()
        s_me = 2 * x + y
        sibling = (x, y, 1 - c)
        def rows_of(k, p):
            rows = _HALF_ROWS[ws[k]]
            return pl.ds(_mo(p * rows, rows), rows)

        def half(k, p):
            return stage[k].at[rows_of(k, p), :]

        def unit(k, s, p):
            return _UNITS[ws[k]](outs[k], s, p)

        def rcopy(k, i, src, dst, to):
            return pltpu.make_async_remote_copy(src_ref=src, dst_ref=dst, send_sem=send_sems.at[k, i],
                                                recv_sem=recv_sems.at[k, i], device_id=to, device_id_type=MESH)

        peers_entered = _signal_peers("both")
        for k in range(n):
            stage[k][rows_of(k, c), :] = ins[k][rows_of(k, c), :].astype(BF)
        peers_entered()
        def piece(ref, k, q):
            rows = _HALF_ROWS[ws[k]] // AG_PIECES
            return ref.at[pl.ds(q * rows, rows), :]

        sends = []
        for q in range(AG_PIECES):
            for j, (cx, cy) in enumerate(chips):
                for k in range(n):
                    cp = rcopy(k, j * AG_PIECES + q, piece(half(k, c), k, q), piece(unit(k, s_me, c), k, q),
                               (cx, cy, c))
                    cp.start()
                    sends.append(cp)
        for k in range(n):
            stage[k][rows_of(k, 1 - c), :] = ins[k][rows_of(k, 1 - c), :].astype(BF)
        local = []
        for k in range(n):
            for p in range(2):
                cp = pltpu.make_async_copy(half(k, p), unit(k, s_me, p), local_sems.at[k, p])
                cp.start()
                local.append(cp)
        for k, w in enumerate(later_ws):
            later_stage[k][...] = later_ins[k][...].astype(BF)
            cp = pltpu.make_async_copy(later_stage[k], _shard_of(later_outs[k], w, s_me), later_sems.at[k])
            cp.start()
            local.append(cp)
        keep = _struct_mask()
        gp_ref[...] = _bias_rows(rel_ref, rel_pad)
        for h in range(N_HEADS):
            bias_ref[h] = jnp.where(keep, _skew_table(gp_ref[h:h + 1, :])[:, :KB], NEG_INF)
        for q in range(AG_PIECES):
            for j, (cx, cy) in enumerate(chips):
                for k in range(n):
                    landed = piece(unit(k, 2 * cx + cy, c), k, q)
                    rcopy(k, j * AG_PIECES + q, landed, landed, (cx, cy, c)).wait_recv()
                    cp = rcopy(k, (3 + j) * AG_PIECES + q, landed, landed, sibling)
                    cp.start()
                    sends.append(cp)
        for q in range(AG_PIECES):
            for j, (cx, cy) in enumerate(chips):
                for k in range(n):
                    other = piece(unit(k, 2 * cx + cy, 1 - c), k, q)
                    rcopy(k, (3 + j) * AG_PIECES + q, other, other, sibling).wait_recv()
        for cp in sends:
            cp.wait_send()
        for cp in local:
            cp.wait()

    vm = pl.BlockSpec(memory_space=pltpu.VMEM)
    outs = pl.pallas_call(
        body, name="ag_weights",
        out_shape=tuple(jax.ShapeDtypeStruct(_FULL_SHAPES[w], BF) for w in tuple(ws) + tuple(later_ws))
        + (jax.ShapeDtypeStruct((N_HEADS, QB, KB), F32),),
        in_specs=[vm] * (n + m + 1), out_specs=[_ANY] * (n + m) + [vm],
        scratch_shapes=[pltpu.VMEM(_SHARD_SHAPES[w], BF) for w in tuple(ws) + tuple(later_ws)]
        + [pltpu.SemaphoreType.DMA((n, 6 * AG_PIECES)), pltpu.SemaphoreType.DMA((n, 6 * AG_PIECES)),
           pltpu.SemaphoreType.DMA((n, 2)), pltpu.SemaphoreType.DMA((m,)),
           pltpu.VMEM((N_HEADS, _REL_PAD), F32), pltpu.VMEM((N_HEADS, ROLL_W), F32)],
        compiler_params=_params(48, collective_id=_PEER_SETS["both"]),
    )(*shards, *later_shards, rel)
    return list(outs[:n]), list(outs[n:n + m]), outs[-1]


def _shard_of(ref, w, s):
    if w == 0:
        return ref.at[:, pl.ds(_mo(s * SHARD_IN, 128), SHARD_IN)]
    if w == 3:
        return ref.at[pl.ds(_mo(s * 256, 256), 256), :]
    return ref.at[:, pl.ds(_mo(s * 256, 128), 256)]


def _gather_copies(ws):
    def copies(refs, send_sems, recv_sems):
        x, y, c, chips = _mesh_pos()
        out = []
        for j, (cx, cy) in enumerate(chips):
            for k, w in enumerate(ws):
                mine = _shard_of(refs[k], w, 2 * x + y)
                out.append(pltpu.make_async_remote_copy(
                    src_ref=mine, dst_ref=mine, send_sem=send_sems.at[3 * k + j], recv_sem=recv_sems.at[3 * k + j],
                    device_id=(cx, cy, c), device_id_type=MESH))
        return out
    return copies


def _inproj_fwd(x, norm_g, w_in_bf, tm=512, after=()):
    S = x.shape[0]

    def body(x_ref, g_ref, w_ref, ht_ref, q_ref, k_ref, v_ref, zr_ref):
        xv = x_ref[...]
        r = lax.rsqrt(jnp.mean(xv * xv, axis=-1, keepdims=True) + EPS)
        hf = (xv * r) * g_ref[...]
        ht_ref[...] = hf.T.astype(BF)
        h = hf.astype(BF)
        heads = (q_ref, k_ref, v_ref)
        for j in range(D_IN // 512):
            z = _dot(h, w_ref[:, j * 512:(j + 1) * 512])
            if j < 3:
                zb = z.astype(BF)
                for hd in range(N_HEADS):
                    heads[j][hd] = zb[:, hd * HEAD_DIM:(hd + 1) * HEAD_DIM]
            else:
                zr_ref[:, (j - 3) * 512:(j - 2) * 512] = z

    head_major = jax.ShapeDtypeStruct((N_HEADS, S, HEAD_DIM), BF)
    head_spec = pl.BlockSpec((N_HEADS, tm, HEAD_DIM), lambda i: (0, i, 0))
    return pl.pallas_call(
        _after(body, 3, after), name="inproj_fwd", grid=(S // tm,),
        out_shape=(jax.ShapeDtypeStruct((D_MODEL, S), BF), head_major, head_major, head_major,
                   jax.ShapeDtypeStruct((S, D_IN - 3 * D_A), F32)),
        in_specs=[pl.BlockSpec((tm, D_MODEL), lambda i: (i, 0)),
                  pl.BlockSpec((1, D_MODEL), lambda i: (0, 0)),
                  pl.BlockSpec((D_MODEL, D_IN), lambda i: (0, 0), pipeline_mode=pl.Buffered(1))]
        + [_ANY] * len(after),
        out_specs=[pl.BlockSpec((D_MODEL, tm), lambda i: (0, i)),
                   head_spec, head_spec, head_spec,
                   pl.BlockSpec((tm, D_IN - 3 * D_A), lambda i: (i, 0))],
        compiler_params=_params(52, dimension_semantics=("arbitrary",)),
    )(x, norm_g, w_in_bf, *after)


def _skew_table(gp_row):
    row = lax.broadcasted_iota(jnp.int32, (QB, ROLL_W), 0)
    t = jnp.broadcast_to(gp_row, (QB, ROLL_W))
    for b in range(7):
        t = jnp.where(((row >> b) & 1) == 1, pltpu.roll(t, 1 << b, axis=1), t)
    return t


def _unskew_sum(d):
    half = QB // 2
    while half >= 8:
        d = d[0:half] + pltpu.roll(d[half:2 * half], ROLL_W - half, axis=1)
        half //= 2
    row = lax.broadcasted_iota(jnp.int32, (8, ROLL_W), 0)
    for b in range(3):
        d = jnp.where(((row >> b) & 1) == 1, pltpu.roll(d, ROLL_W - (1 << b), axis=1), d)
    return jnp.sum(d, axis=0, keepdims=True)


def _struct_mask():
    a = lax.broadcasted_iota(jnp.int32, (QB, KB), 0) // CHUNK
    b = lax.broadcasted_iota(jnp.int32, (QB, KB), 1) // CHUNK
    return (b >= a) & (b <= a + N_PREV)


KV_REST_STEP = 2
KV_HEAD_ROWS = KV_REST_STEP * Q_PER_STEP * QB


def _kv_copies(k_hbm, v_hbm, k_scr, v_scr, sems, S):
    def part(src, dst, lo, n, sem):
        return pltpu.make_async_copy(src.at[:, pl.ds(lo, n), :], dst.at[:, pl.ds(PADK + lo, n), :], sem)

    rest = S - KV_HEAD_ROWS
    return ([part(k_hbm, k_scr, 0, KV_HEAD_ROWS, sems.at[0]), part(v_hbm, v_scr, 0, KV_HEAD_ROWS, sems.at[1])],
            [part(k_hbm, k_scr, KV_HEAD_ROWS, rest, sems.at[2]), part(v_hbm, v_scr, KV_HEAD_ROWS, rest, sems.at[3])])


def _load_kv(k_hbm, v_hbm, k_scr, v_scr, sems, S, meanwhile=lambda: None):
    copies = _kv_copies(k_hbm, v_hbm, k_scr, v_scr, sems, S)

    @pl.when(pl.program_id(0) == 0)
    def _():
        zeros = jnp.zeros((N_HEADS, PADK, HEAD_DIM), BF)
        k_scr[:, 0:PADK, :] = zeros
        v_scr[:, 0:PADK, :] = zeros
        for cp in copies[0] + copies[1]:
            cp.start()
        meanwhile()
        for cp in copies[0]:
            cp.wait()

    @pl.when(pl.program_id(0) == KV_REST_STEP)
    def _():
        for cp in copies[1]:
            cp.wait()


_BATCH_NT = (((2,), (2,)), ((0,), (0,)))
_BATCH_NN = (((2,), (1,)), ((0,), (0,)))
_BATCH_TN = (((1,), (1,)), ((0,), (0,)))


def _bdot(a, b, dims):
    return lax.dot_general(a, b, dims, preferred_element_type=F32)


def _scaled(q):
    return q * jnp.asarray(SCALE, BF)


def _scores(qs, kb, bias, i, front):
    s = _bdot(qs, kb, _BATCH_NT) + bias
    if front:
        col = lax.broadcasted_iota(jnp.int32, (1, 1, KB), 2)
        s = jnp.where(col >= PADK - i * QB, s, NEG_INF)
    return s


def _attn_fwd(q3, k3, v3, bias):
    S = q3.shape[1]

    def body(q_ref, k_hbm, v_hbm, bias_ref, o_ref, lse_ref, k_scr, v_scr, sems):
        _load_kv(k_hbm, v_hbm, k_scr, v_scr, sems, S)

        def step(i, rows, front):
            start = pl.multiple_of(i * QB, QB)
            kb = k_scr[:, pl.ds(start, KB), :]
            vb = v_scr[:, pl.ds(start, KB), :]
            s = _scores(_scaled(q_ref[:, rows, :]), kb, bias_ref[...], i, front)
            m = jnp.max(s, axis=-1, keepdims=True)
            e = jnp.exp(s - m)
            l = jnp.sum(e, axis=-1, keepdims=True)
            p = e * (1.0 / l)
            o = _bdot(p.astype(BF), vb, _BATCH_NN)
            lse_ref[:, rows, :] = jnp.broadcast_to(m + jnp.log(l), (N_HEADS, QB, 128))
            for h in range(N_HEADS):
                o_ref[rows, h * HEAD_DIM:(h + 1) * HEAD_DIM] = o[h]

        def block(j, carry):
            i = pl.program_id(0) * Q_PER_STEP + j
            rows = pl.ds(pl.multiple_of(j * QB, QB), QB)
            pl.when(i < KEEP)(functools.partial(step, i, rows, True))
            pl.when(i >= KEEP)(functools.partial(step, i, rows, False))
            return carry

        lax.fori_loop(0, Q_PER_STEP, block, 0)

    rows_per_step = Q_PER_STEP * QB
    kv_scr = pltpu.VMEM((N_HEADS, S + PADK, HEAD_DIM), BF)
    return pl.pallas_call(
        body, name="attn_fwd", grid=(S // rows_per_step,),
        out_shape=(jax.ShapeDtypeStruct((S, D_A), F32), jax.ShapeDtypeStruct((N_HEADS, S, 128), F32)),
        in_specs=[pl.BlockSpec((N_HEADS, rows_per_step, HEAD_DIM), lambda g: (0, g, 0)),
                  pl.BlockSpec(memory_space=pl.ANY), pl.BlockSpec(memory_space=pl.ANY),
                  pl.BlockSpec((N_HEADS, QB, KB), lambda g: (0, 0, 0))],
        out_specs=[pl.BlockSpec((rows_per_step, D_A), lambda g: (g, 0)),
                   pl.BlockSpec((N_HEADS, rows_per_step, 128), lambda g: (0, g, 0))],
        scratch_shapes=[kv_scr, kv_scr, pltpu.SemaphoreType.DMA((4,))],
        compiler_params=_params(48, dimension_semantics=("arbitrary",)),
    )(q3, k3, v3, bias)


def _attn_bwd(q3, k3, v3, d_att3, lse, bias, after=()):
    S = q3.shape[1]
    nq = S // QB

    def body(q_ref, do_ref, k_hbm, v_hbm, lse_ref, bias_ref, dq_ref, dk_ref, dv_ref, dgp_ref,
             k_scr, v_scr, dk_acc, dv_acc, dbias_acc, pad_scr, sems):
        def clear():
            dk_acc[...] = jnp.zeros_like(dk_acc)
            dv_acc[...] = jnp.zeros_like(dv_acc)
            dbias_acc[...] = jnp.zeros_like(dbias_acc)
        _load_kv(k_hbm, v_hbm, k_scr, v_scr, sems, S, clear)

        def step(i, rows, front):
            start = pl.multiple_of(i * QB, QB)
            kb = k_scr[:, pl.ds(start, KB), :]
            vb = v_scr[:, pl.ds(start, KB), :]
            qs = _scaled(q_ref[:, rows, :])
            do = do_ref[:, rows, :]
            p = jnp.exp(_scores(qs, kb, bias_ref[...], i, front) - jnp.tile(lse_ref[:, rows, :], (1, 1, KB // 128)))
            dp = _bdot(do, vb, _BATCH_NT)
            ds = p * (dp - jnp.sum(dp * p, axis=-1, keepdims=True))
            dbias_acc[...] += ds
            dsb = ds.astype(BF)
            dq = _bdot(dsb, kb, _BATCH_NN) * SCALE
            for h in range(N_HEADS):
                dq_ref[rows, h * HEAD_DIM:(h + 1) * HEAD_DIM] = dq[h].astype(BF)
            dk_acc[...] += _bdot(dsb, qs, _BATCH_TN)
            dv_acc[...] += _bdot(p.astype(BF), do, _BATCH_TN)

        def block(j, carry):
            i = pl.program_id(0) * Q_PER_STEP + j
            rows = pl.ds(pl.multiple_of(j * QB, QB), QB)
            pl.when(i < KEEP)(functools.partial(step, i, rows, True))
            pl.when((i >= KEEP) & (i < nq))(functools.partial(step, i, rows, False))
            for h in range(N_HEADS):
                hs = slice(h * HEAD_DIM, (h + 1) * HEAD_DIM)
                dk_ref[rows, hs] = dk_acc[h, 0:QB, :].astype(BF)
                dv_ref[rows, hs] = dv_acc[h, 0:QB, :].astype(BF)
            dk_acc[:, 0:KB - QB, :] = dk_acc[:, QB:KB, :]
            dv_acc[:, 0:KB - QB, :] = dv_acc[:, QB:KB, :]
            dk_acc[:, KB - QB:KB, :] = jnp.zeros((N_HEADS, QB, HEAD_DIM), F32)
            dv_acc[:, KB - QB:KB, :] = jnp.zeros((N_HEADS, QB, HEAD_DIM), F32)
            return carry

        lax.fori_loop(0, Q_PER_STEP, block, 0)

        @pl.when(pl.program_id(0) == n_steps - 1)
        def _():
            lane = lax.broadcasted_iota(jnp.int32, (1, ROLL_W), 1)
            hi = (lane < 384) | (lane >= 832)
            lo = (lane > 640) & (lane < 832)
            pad_scr[...] = jnp.zeros_like(pad_scr)
            for h in range(N_HEADS):
                pad_scr[:, 0:KB] = dbias_acc[h]
                g = _unskew_sum(pad_scr[...])
                s_hi = jnp.sum(jnp.where(hi, g, 0.0), axis=-1, keepdims=True)
                s_lo = jnp.sum(jnp.where(lo, g, 0.0), axis=-1, keepdims=True)
                g = jnp.where(lane == 384, g + s_hi, g)
                g = jnp.where(lane == 640, g + s_lo, g)
                dgp_ref[h:h + 1, :] = g

    assert nq % Q_PER_STEP == 0 and KEEP % Q_PER_STEP == 0
    rows_per_step = Q_PER_STEP * QB
    n_steps = (nq + KEEP) // Q_PER_STEP
    last = nq // Q_PER_STEP - 1
    lag = KEEP // Q_PER_STEP
    kv_scr = pltpu.VMEM((N_HEADS, S + PADK, HEAD_DIM), BF)
    return pl.pallas_call(
        _after(body, 6, after), name="attn_bwd", grid=(n_steps,),
        out_shape=(jax.ShapeDtypeStruct((S, D_A), BF), jax.ShapeDtypeStruct((S, D_A), BF),
                   jax.ShapeDtypeStruct((S, D_A), BF), jax.ShapeDtypeStruct((N_HEADS, ROLL_W), F32)),
        in_specs=[pl.BlockSpec((N_HEADS, rows_per_step, HEAD_DIM), lambda g: (0, jnp.minimum(g, last), 0)),
                  pl.BlockSpec((N_HEADS, rows_per_step, HEAD_DIM), lambda g: (0, jnp.minimum(g, last), 0)),
                  pl.BlockSpec(memory_space=pl.ANY), pl.BlockSpec(memory_space=pl.ANY),
                  pl.BlockSpec((N_HEADS, rows_per_step, 128), lambda g: (0, jnp.minimum(g, last), 0)),
                  pl.BlockSpec((N_HEADS, QB, KB), lambda g: (0, 0, 0))] + [_ANY] * len(after),
        out_specs=[pl.BlockSpec((rows_per_step, D_A), lambda g: (jnp.minimum(g, last), 0)),
                   pl.BlockSpec((rows_per_step, D_A), lambda g: (jnp.maximum(g - lag, 0), 0)),
                   pl.BlockSpec((rows_per_step, D_A), lambda g: (jnp.maximum(g - lag, 0), 0)),
                   pl.BlockSpec((N_HEADS, ROLL_W), lambda g: (0, 0))],
        scratch_shapes=[kv_scr, kv_scr,
                        pltpu.VMEM((N_HEADS, KB, HEAD_DIM), F32), pltpu.VMEM((N_HEADS, KB, HEAD_DIM), F32),
                        pltpu.VMEM((N_HEADS, QB, KB), F32), pltpu.VMEM((QB, ROLL_W), F32),
                        pltpu.SemaphoreType.DMA((4,))],
        compiler_params=_params(56, dimension_semantics=("arbitrary",)),
    )(q3, d_att3, k3, v3, lse, bias, *after)


def _sgu_core(ub, vb, lg, lb):
    u, du = _gelu_and_grad(ub)
    v, dv = _gelu_and_grad(vb)
    mu = jnp.mean(v, axis=-1, keepdims=True)
    vc = v - mu
    rstd = lax.rsqrt(jnp.mean(vc * vc, axis=-1, keepdims=True) + EPS)
    xh = vc * rstd
    vn = xh * lg + lb
    return u, du, dv, rstd, xh, vn


def _tri():
    r = lax.broadcasted_iota(jnp.int32, (SGU_CHUNK, SGU_CHUNK), 0)
    c = lax.broadcasted_iota(jnp.int32, (SGU_CHUNK, SGU_CHUNK), 1)
    return r >= c


def _tail_sgu(att, zrest, x, target, w_pa, w_pb, w_out, b_gate, final_g, ln_g, ln_b, w_s, b_s_t, tm=256):
    S = x.shape[0]
    nt = S // tm
    chunks = tm // SGU_CHUNK

    def body(att_ref, ga_ref, ub_ref, vb_ref, gb_ref, gta_ref, gtb_ref, x_ref, t_ref,
             wpa_ref, wpb_ref, wout_ref, bg_ref, fg_ref, lg_ref, lb_ref, ws_ref, bst_ref,
             dout_ref, datt_ref, dzt_ref, dzs_ref, gwout_hbm, gwpa_hbm, gwpb_hbm,
             gbg_ref, gfg_ref, loss_ref, gws_ref, gbs_ref, glg_ref, glb_ref,
             acc_out, acc_pa, acc_pb, sg_scr, mix_scr, dvn_scr, bs_acc, sems):
        i = pl.program_id(0)

        @pl.when(i == 0)
        def _():
            for r in (acc_out, acc_pa, acc_pb, gbg_ref, gfg_ref, loss_ref, gws_ref, glg_ref, glb_ref, bs_acc):
                r[...] = jnp.zeros_like(r)

        u, du, dv, rstd, xh, vn = _sgu_core(ub_ref[...], vb_ref[...], lg_ref[...], lb_ref[...])
        vnb = vn.astype(BF)
        tri = _tri()
        blocks = [(g, slice(n * SGU_CHUNK, (n + 1) * SGU_CHUNK), slice(g * 128, (g + 1) * 128))
                  for g in range(N_GROUPS) for n in range(chunks)]
        wts = [jnp.where(tri, ws_ref[g], 0.0) for g in range(N_GROUPS)]
        for g, rs, cs in blocks:
            mixed = _dot(wts[g].astype(BF), vnb[rs, cs]) + bst_ref[:, g:g + 1]
            mix_scr[rs, cs] = mixed
            sg_scr[rs, cs] = u[rs, cs] * mixed

        att = att_ref[...]
        sg = sg_scr[...]
        sa, dsa = _silu_and_grad(ga_ref[...])
        sb, dsb = _silu_and_grad(gb_ref[...])
        ya = (att * sa).astype(BF)
        yb = (sg * sb).astype(BF)
        pa = _dot(ya, wpa_ref[...])
        pb = _dot(yb, wpb_ref[...])
        ga = _sigmoid(gta_ref[...] + bg_ref[:, 0:D_MODEL])
        gb = _sigmoid(gtb_ref[...] + bg_ref[:, D_MODEL:2 * D_MODEL])
        merged = (ga * pa + gb * pb).astype(BF)
        out = x_ref[...] + _dot(merged, wout_ref[...])
        r2 = lax.rsqrt(jnp.mean(out * out, axis=-1, keepdims=True) + EPS)
        nrm = out * r2
        fg = fg_ref[...]
        err = nrm * fg - t_ref[...]
        loss_ref[...] += 0.5 * jnp.sum(jnp.mean(err * err, axis=-1, keepdims=True))
        dy = err * (1.0 / D_MODEL)
        gfg_ref[...] += jnp.sum(dy * nrm, axis=0, keepdims=True)
        dn = dy * fg
        d_out = r2 * (dn - nrm * jnp.mean(dn * nrm, axis=-1, keepdims=True))
        dout_ref[...] = d_out
        d_outb = d_out.astype(BF)
        acc_out[...] += _dot_tn(merged, d_outb)
        dm = _dot_nt(d_outb, wout_ref[...])
        d_pa = (dm * ga).astype(BF)
        d_pb = (dm * gb).astype(BF)
        d_gta = dm * pa * (ga * (1.0 - ga))
        d_gtb = dm * pb * (gb * (1.0 - gb))
        gbg_ref[:, 0:D_MODEL] += jnp.sum(d_gta, axis=0, keepdims=True)
        gbg_ref[:, D_MODEL:2 * D_MODEL] += jnp.sum(d_gtb, axis=0, keepdims=True)
        dzt_ref[:, 2 * D_A:2 * D_A + D_MODEL] = d_gta.astype(BF)
        dzt_ref[:, 2 * D_A + D_MODEL:] = d_gtb.astype(BF)
        acc_pa[...] += _dot_tn(ya, d_pa)
        acc_pb[...] += _dot_tn(yb, d_pb)
        d_ya = _dot_nt(d_pa, wpa_ref[...])
        d_yb = _dot_nt(d_pb, wpb_ref[...])
        d_att = (d_ya * sa).astype(BF)
        for hd in range(N_HEADS):
            datt_ref[hd] = d_att[:, hd * HEAD_DIM:(hd + 1) * HEAD_DIM]
        dzt_ref[:, 0:D_A] = (d_ya * att * dsa).astype(BF)
        dzt_ref[:, D_A:2 * D_A] = (d_yb * sg * dsb).astype(BF)

        dsg = d_yb * sb
        dzs_ref[:, 0:D_B] = (dsg * mix_scr[...] * du).astype(BF)
        dmix = dsg * u
        for g, rs, cs in blocks:
            dmb = dmix[rs, cs].astype(BF)
            bs_acc[:, cs] += dmix[rs, cs]
            gws_ref[g] += _dot_nt(dmb, vnb[rs, cs])
            dvn_scr[rs, cs] = _dot(wts[g].T.astype(BF), dmb)
        dvn = dvn_scr[...]
        glg_ref[...] += jnp.sum(dvn * xh, axis=0, keepdims=True)
        glb_ref[...] += jnp.sum(dvn, axis=0, keepdims=True)
        dxh = dvn * lg_ref[...]
        dvv = rstd * (dxh - jnp.mean(dxh, axis=-1, keepdims=True)
                      - xh * jnp.mean(dxh * xh, axis=-1, keepdims=True))
        dzs_ref[:, D_B:2 * D_B] = (dvv * dv).astype(BF)

        @pl.when(i == nt - 1)
        def _():
            cps = [pltpu.make_async_copy(acc_out, gwout_hbm, sems.at[0]),
                   pltpu.make_async_copy(acc_pa, gwpa_hbm, sems.at[1]),
                   pltpu.make_async_copy(acc_pb, gwpb_hbm, sems.at[2])]
            for cp in cps:
                cp.start()
            lane = lax.broadcasted_iota(jnp.int32, (SGU_CHUNK, 128), 1)
            cols = jnp.zeros((SGU_CHUNK, 128), F32)
            for g in range(N_GROUPS):
                gws_ref[g] = jnp.where(tri, gws_ref[g], 0.0)
                col = jnp.sum(bs_acc[:, g * 128:(g + 1) * 128], axis=-1, keepdims=True)
                cols = jnp.where(lane == g, col, cols)
            gbs_ref[...] = cols
            for cp in cps:
                cp.wait()

    c2 = lambda i: (0, 0)
    c3 = lambda i: (0, 0, 0)
    zcol = lambda w, blk: pl.BlockSpec((tm, w), lambda i: (i, blk))
    row = lambda w: pl.BlockSpec((tm, w), lambda i: (i, 0))
    return pl.pallas_call(
        body, name="tail", grid=(nt,),
        out_shape=(jax.ShapeDtypeStruct((S, D_MODEL), F32), jax.ShapeDtypeStruct((N_HEADS, S, HEAD_DIM), BF),
                   jax.ShapeDtypeStruct((S, 3072), BF), jax.ShapeDtypeStruct((S, 2 * D_B), BF),
                   jax.ShapeDtypeStruct((D_MODEL, D_MODEL), F32), jax.ShapeDtypeStruct((D_A, D_MODEL), F32),
                   jax.ShapeDtypeStruct((D_B, D_MODEL), F32),
                   jax.ShapeDtypeStruct((1, 2 * D_MODEL), F32), jax.ShapeDtypeStruct((1, D_MODEL), F32),
                   jax.ShapeDtypeStruct((1, 128), F32),
                   jax.ShapeDtypeStruct((N_GROUPS, 128, 128), F32), jax.ShapeDtypeStruct((SGU_CHUNK, 128), F32),
                   jax.ShapeDtypeStruct((1, D_B), F32), jax.ShapeDtypeStruct((1, D_B), F32)),
        in_specs=[row(D_A), zcol(512, 0), zcol(512, 1), zcol(512, 2), zcol(512, 3),
                  zcol(D_MODEL, 2), zcol(D_MODEL, 3), row(D_MODEL), row(D_MODEL),
                  pl.BlockSpec((D_A, D_MODEL), c2), pl.BlockSpec((D_B, D_MODEL), c2),
                  pl.BlockSpec((D_MODEL, D_MODEL), c2),
                  pl.BlockSpec((1, 2 * D_MODEL), c2), pl.BlockSpec((1, D_MODEL), c2),
                  pl.BlockSpec((1, D_B), c2), pl.BlockSpec((1, D_B), c2),
                  pl.BlockSpec((N_GROUPS, 128, 128), c3), pl.BlockSpec((128, N_GROUPS), c2)],
        out_specs=[row(D_MODEL), pl.BlockSpec((N_HEADS, tm, HEAD_DIM), lambda i: (0, i, 0)),
                   row(3072), row(2 * D_B), _ANY, _ANY, _ANY,
                   pl.BlockSpec((1, 2 * D_MODEL), c2), pl.BlockSpec((1, D_MODEL), c2),
                   pl.BlockSpec((1, 128), c2),
                   pl.BlockSpec((N_GROUPS, 128, 128), c3), pl.BlockSpec((SGU_CHUNK, 128), c2),
                   pl.BlockSpec((1, D_B), c2), pl.BlockSpec((1, D_B), c2)],
        scratch_shapes=[pltpu.VMEM((D_MODEL, D_MODEL), F32), pltpu.VMEM((D_A, D_MODEL), F32),
                        pltpu.VMEM((D_B, D_MODEL), F32),
                        pltpu.VMEM((tm, D_B), F32), pltpu.VMEM((tm, D_B), F32), pltpu.VMEM((tm, D_B), F32),
                        pltpu.VMEM((SGU_CHUNK, D_B), F32), pltpu.SemaphoreType.DMA((3,))],
        compiler_params=_params(58, dimension_semantics=("arbitrary",)),
    )(att, zrest, zrest, zrest, zrest, zrest, zrest, x, target, w_pa, w_pb, w_out, b_gate, final_g,
      ln_g, ln_b, w_s, b_s_t)


_DZ_MAP = ((0, 0), (1, 0), (2, 0), (3, 0), (4, 0), (4, 1), (3, 1), (3, 2), (3, 3), (3, 4), (3, 5))


def _dh_gradx(dq, dk, dv, dzt, dzs, w_in_bf, x, norm_g, d_out, tm=512, after=()):
    S = x.shape[0]

    def body(dq_ref, dk_ref, dv_ref, dzt_ref, dzs_ref, w_ref, x_ref, g_ref, dout_ref, gx_ref, gn_ref):
        i = pl.program_id(0)

        @pl.when(i == 0)
        def _():
            gn_ref[...] = jnp.zeros_like(gn_ref)

        pieces = (dq_ref, dk_ref, dv_ref, dzt_ref, dzs_ref)
        dh = jnp.zeros((tm, D_MODEL), F32)
        for j, (pc, blk) in enumerate(_DZ_MAP):
            dh += _dot_nt(pieces[pc][:, blk * 512:(blk + 1) * 512], w_ref[:, j * 512:(j + 1) * 512])
        xv = x_ref[...]
        r = lax.rsqrt(jnp.mean(xv * xv, axis=-1, keepdims=True) + EPS)
        nrm = xv * r
        gn_ref[...] += jnp.sum(dh * nrm, axis=0, keepdims=True)
        dn = dh * g_ref[...]
        gx_ref[...] = r * (dn - nrm * jnp.mean(dn * nrm, axis=-1, keepdims=True)) + dout_ref[...]

    row = lambda w: pl.BlockSpec((tm, w), lambda i: (i, 0))
    c2 = lambda i: (0, 0)
    return pl.pallas_call(
        _after(body, 9, after), name="dh_gradx", grid=(S // tm,),
        out_shape=(jax.ShapeDtypeStruct((S, D_MODEL), F32), jax.ShapeDtypeStruct((1, D_MODEL), F32)),
        in_specs=[row(512), row(512), row(512), row(3072), row(1024),
                  pl.BlockSpec((D_MODEL, D_IN), c2, pipeline_mode=pl.Buffered(1)), row(D_MODEL),
                  pl.BlockSpec((1, D_MODEL), c2), row(D_MODEL)]
        + [_ANY] * len(after),
        out_specs=[row(D_MODEL), pl.BlockSpec((1, D_MODEL), c2)],
        compiler_params=_params(48, dimension_semantics=("arbitrary",)),
    )(dq, dk, dv, dzt, dzs, w_in_bf, x, norm_g, d_out, *after)


def _gw_in(ht, dq, dk, dv, dzt, dzs, tn=512, after=()):
    S = ht.shape[1]
    per = 512 // tn
    cols = tuple((pc, per * blk + h) for pc, blk in _DZ_MAP for h in range(per))

    def body(ht_ref, dq_ref, dk_ref, dv_ref, dzt_ref, dzs_ref, o_ref, ob_ref):
        j = pl.program_id(0)
        pieces = (dq_ref, dk_ref, dv_ref, dzt_ref, dzs_ref)
        for pc in range(5):
            hit = functools.reduce(jnp.logical_or, [j == jj for jj, (p, _) in enumerate(cols) if p == pc])

            @pl.when(hit)
            def _(pc=pc):
                g = _dot(ht_ref[...], pieces[pc][...])
                o_ref[...] = g
                ob_ref[...] = g.astype(BF)

    def piece_spec(pc):
        cur = next(blk for p, blk in cols if p == pc)
        held = []
        for p, blk in cols:
            cur = blk if p == pc else cur
            held.append(cur)

        def index_map(j):
            blk = jnp.int32(held[0])
            for jj in range(1, len(held)):
                if held[jj] != held[jj - 1]:
                    blk = jnp.where(j >= jj, jnp.int32(held[jj]), blk)
            return (0, blk)

        return pl.BlockSpec((S, tn), index_map)

    return pl.pallas_call(
        _after(body, 6, after), name="gw_in", grid=(len(cols),),
        out_shape=(jax.ShapeDtypeStruct((D_MODEL, D_IN), F32), jax.ShapeDtypeStruct((D_MODEL, D_IN), BF)),
        in_specs=[pl.BlockSpec((D_MODEL, S), lambda j: (0, 0), pipeline_mode=pl.Buffered(1))]
        + [piece_spec(pc) for pc in range(5)]
        + [_ANY] * len(after),
        out_specs=[pl.BlockSpec((D_MODEL, tn), lambda j: (0, j)), pl.BlockSpec((D_MODEL, tn), lambda j: (0, j))],
        compiler_params=_params(56, dimension_semantics=("arbitrary",)),
    )(ht, dq, dk, dv, dzt, dzs, *after)


_HBM = pl.BlockSpec(memory_space=pltpu.HBM)
_SEM = pl.BlockSpec(memory_space=pltpu.SEMAPHORE)
_ANY = pl.BlockSpec(memory_space=pl.ANY)
_EFFECT = pltpu.SideEffectType.DATAFLOW_SIDE_EFFECTING


def _in_hbm(a):
    return pltpu.with_memory_space_constraint(a, pltpu.HBM)


def _after(body, n_in, after):
    if not after:
        return body
    return lambda *refs: body(*refs[:n_in], *refs[n_in + len(after):])


class _Started:
    def __init__(self, send, recv, bufs, token):
        self.send, self.recv, self.bufs, self.token = send, recv, bufs, token


_PEER_SETS = {"sibling": 7, "chips": 8, "both": 9}


def _peers(kind):
    x, y, c, chips = _mesh_pos()
    return ([(x, y, 1 - c)] if kind in ("sibling", "both") else []) + (
        [(cx, cy, c) for cx, cy in chips] if kind in ("chips", "both") else [])


def _signal_peers(kind):
    barrier = pltpu.get_barrier_semaphore()
    targets = _peers(kind)
    for peer in targets:
        pl.semaphore_signal(barrier, inc=1, device_id=peer, device_id_type=MESH)
    return lambda: pl.semaphore_wait(barrier, len(targets))


def _split_start(name, bufs, n_copies, copies, peers, after=()):
    nb = len(bufs)

    def body(*refs):
        _signal_peers(peers)()
        refs = refs[:nb] + refs[nb + len(after):]
        for cp in copies(refs[:nb], refs[nb], refs[nb + 1]):
            cp.start()
        refs[-1][...] = jnp.zeros_like(refs[-1])

    outs = pl.pallas_call(
        body, name=name,
        out_shape=(pltpu.SemaphoreType.DMA((n_copies,)), pltpu.SemaphoreType.DMA((n_copies,)),
                   *[pltpu.HBM(b.shape, b.dtype) for b in bufs], jax.ShapeDtypeStruct((8, 128), F32)),
        in_specs=[_HBM] * nb + [_ANY] * len(after),
        out_specs=(_SEM, _SEM, *[_HBM] * nb, pl.BlockSpec(memory_space=pltpu.VMEM)),
        input_output_aliases={k: 2 + k for k in range(nb)},
        compiler_params=_params(1, has_side_effects=_EFFECT, collective_id=_PEER_SETS[peers]),
    )(*[_in_hbm(b) for b in bufs], *after)
    return _Started(outs[0], outs[1], list(outs[2:2 + nb]), outs[-1])


def _split_wait(name, started, copies, after):
    nb = len(started.bufs)
    after = tuple(after) if isinstance(after, (tuple, list)) else (after,)

    def body(*refs):
        for cp in copies(refs[:nb], refs[nb], refs[nb + 1]):
            cp.wait_send()
            cp.wait_recv()

    return list(pl.pallas_call(
        body, name=name,
        out_shape=tuple(pltpu.HBM(b.shape, b.dtype) for b in started.bufs),
        in_specs=[_HBM] * nb + [_SEM, _SEM] + [_ANY] * len(after),
        out_specs=tuple([_HBM] * nb),
        input_output_aliases={k: k for k in range(nb)},
        compiler_params=_params(1, has_side_effects=_EFFECT),
    )(*started.bufs, started.send, started.recv, *after))


def _x1_copies(ws):
    def copies(refs, send_sems, recv_sems):
        x, y, c, _ = _mesh_pos()
        out = []
        for k, w in enumerate(ws):
            for s in range(N_SHARD):
                out.append(pltpu.make_async_remote_copy(
                    src_ref=_UNITS[w](refs[k], s, 1 - c), dst_ref=refs[len(ws) + k].at[s],
                    send_sem=send_sems.at[N_SHARD * k + s], recv_sem=recv_sems.at[N_SHARD * k + s],
                    device_id=(x, y, 1 - c), device_id_type=MESH))
        return out
    return copies


def _x2_copies(n):
    def copies(refs, send_sems, recv_sems):
        x, y, c, chips = _mesh_pos()
        out = []
        for j, (cx, cy) in enumerate(chips):
            for k in range(n):
                out.append(pltpu.make_async_remote_copy(
                    src_ref=refs[k].at[2 * cx + cy], dst_ref=refs[n + k].at[j],
                    send_sem=send_sems.at[3 * k + j], recv_sem=recv_sems.at[3 * k + j],
                    device_id=(cx, cy, c), device_id_type=MESH))
        return out
    return copies


def _x3_copies(ws):
    def copies(refs, send_sems, recv_sems):
        x, y, c, _ = _mesh_pos()
        out = []
        for k, w in enumerate(ws):
            rows = _HALF_ROWS[w]
            mine = refs[k].at[pl.ds(_mo(c * rows, rows), rows), :]
            out.append(pltpu.make_async_remote_copy(
                src_ref=mine, dst_ref=mine, send_sem=send_sems.at[k], recv_sem=recv_sems.at[k],
                device_id=(x, y, 1 - c), device_id_type=MESH))
        return out
    return copies


def _x1_lands(ws, dtype=F32):
    return [lax.empty((N_SHARD,) + _UNIT_SHAPES[w], dtype) for w in ws]


def _x2_lands(ws):
    return [lax.empty((3,) + _UNIT_SHAPES[w], BF) for w in ws]


def _grad_add1(w, g, recv, pos):
    ur, uc = _UNIT_SHAPES[w]

    def body(pos_ref, g_ref, r_ref, csb_ref):
        csb_ref[0] = (g_ref[...] + r_ref[0].astype(F32)).astype(BF)

    u3 = lambda k, pos: (pos[2 + k], 0, 0)
    return pl.pallas_call(
        body, name=f"grad_add1_{w}",
        grid_spec=pltpu.PrefetchScalarGridSpec(
            num_scalar_prefetch=1, grid=(N_SHARD - 1,),
            in_specs=[pl.BlockSpec((ur, uc), lambda k, pos: (pos[0], pos[2 + k])), pl.BlockSpec((1, ur, uc), u3)],
            out_specs=pl.BlockSpec((1, ur, uc), u3)),
        out_shape=jax.ShapeDtypeStruct((N_SHARD, ur, uc), BF),
        compiler_params=_params(40, dimension_semantics=("arbitrary",)),
    )(pos, g, recv)


def _grad_add1_group(ws, gs, recvs, pos):
    n = len(ws)

    def body(pos_ref, *refs):
        s = pl.program_id(0)
        for k in range(n):
            g, r, own, csb = refs[k], refs[n + k], refs[2 * n + k], refs[3 * n + k]
            v = g[...] + r[0]
            csb[0] = v.astype(BF)

            @pl.when(s == pos_ref[1])
            def _(own=own, v=v):
                own[...] = v

    def g_spec(w):
        if w == 3:
            return pl.BlockSpec(_UNIT_SHAPES[w], lambda s, pos: (2 * s + pos[0], 0))
        return pl.BlockSpec(_UNIT_SHAPES[w], lambda s, pos: (pos[0], s))

    slot = lambda w: pl.BlockSpec((1,) + _UNIT_SHAPES[w], lambda s, pos: (s, 0, 0))
    outs = pl.pallas_call(
        body, name="grad_add1_group",
        grid_spec=pltpu.PrefetchScalarGridSpec(
            num_scalar_prefetch=1, grid=(N_SHARD,),
            in_specs=[g_spec(w) for w in ws] + [slot(w) for w in ws],
            out_specs=[pl.BlockSpec(_UNIT_SHAPES[w], lambda s, pos: (0, 0)) for w in ws] + [slot(w) for w in ws]),
        out_shape=tuple(jax.ShapeDtypeStruct(_UNIT_SHAPES[w], F32) for w in ws)
        + tuple(jax.ShapeDtypeStruct((N_SHARD,) + _UNIT_SHAPES[w], BF) for w in ws),
        compiler_params=_params(32, dimension_semantics=("arbitrary",)),
    )(pos, *gs, *recvs)
    return list(outs[:n]), list(outs[n:])


def _grad_add2_group(ws, owns, recvs):
    n = len(ws)

    def body(*refs):
        c = lax.axis_index("c")
        for k, w in enumerate(ws):
            own, r, o = refs[k], refs[n + k], refs[2 * n + k]
            rows = _HALF_ROWS[w]
            total = ((own[...] + r[0].astype(F32)) + r[1].astype(F32)) + r[2].astype(F32)
            o[pl.ds(_mo(c * rows, rows), rows), :] = total

    vm = pl.BlockSpec(memory_space=pltpu.VMEM)
    return list(pl.pallas_call(
        body, name="grad_add2_group",
        out_shape=tuple(jax.ShapeDtypeStruct(_SHARD_SHAPES[w], F32) for w in ws),
        in_specs=[vm] * (2 * n), out_specs=[vm] * n,
        compiler_params=_params(32),
    )(*owns, *recvs))


def _grad_add2(w, g, recv1, recv2, pos):
    ur, uc = _UNIT_SHAPES[w]
    nt = 4
    tr = ur // nt

    def body(pos_ref, g_ref, r1_ref, r2_ref, o_ref):
        own = g_ref[...] + r1_ref[0].astype(F32)
        o_ref[...] = ((own + r2_ref[0].astype(F32)) + r2_ref[1].astype(F32)) + r2_ref[2].astype(F32)

    mine = lambda t, pos: (pos[0] * nt + t, 0)
    return pl.pallas_call(
        body, name=f"grad_add2_{w}",
        grid_spec=pltpu.PrefetchScalarGridSpec(
            num_scalar_prefetch=1, grid=(nt,),
            in_specs=[pl.BlockSpec((tr, uc), lambda t, pos: (pos[0] * nt + t, pos[1])),
                      pl.BlockSpec((1, tr, uc), lambda t, pos: (pos[1], t, 0)),
                      pl.BlockSpec((3, tr, uc), lambda t, pos: (0, t, 0))],
            out_specs=pl.BlockSpec((tr, uc), mine)),
        out_shape=jax.ShapeDtypeStruct(_SHARD_SHAPES[w], F32),
        compiler_params=_params(32, dimension_semantics=("arbitrary",)),
    )(pos, g, recv1, recv2)


def _adamw_math(w, g, m, v):
    m = ADAM_B1 * m + (1.0 - ADAM_B1) * g
    v = ADAM_B2 * v + (1.0 - ADAM_B2) * (g * g)
    m_hat = m / ADAM_C1
    v_hat = v / ADAM_C2
    delta = -ADAM_LR * (m_hat / (jnp.sqrt(v_hat) + ADAM_EPS) + ADAM_WD * w)
    return delta, m, v


ADAMW_STEPS = 4


def _adamw(ws_, gs, ms, vs):
    n = len(ws_)

    def body(*refs):
        for k in range(n):
            w, g, m, v = (refs[j * n + k] for j in range(4))
            d, nm, nv, gc = (refs[(4 + j) * n + k] for j in range(4))
            gv = g[...]
            d[...], nm[...], nv[...] = _adamw_math(w[...], gv, m[...], v[...])
            gc[...] = gv

    specs = [pl.BlockSpec((a.shape[0] // ADAMW_STEPS, a.shape[1]), lambda i: (i, 0)) for a in ws_] * 4
    outs = pl.pallas_call(
        body, name="adamw", grid=(ADAMW_STEPS,),
        out_shape=tuple(jax.ShapeDtypeStruct(a.shape, F32) for _ in range(4) for a in ws_),
        in_specs=specs, out_specs=specs,
        compiler_params=_params(40, dimension_semantics=("arbitrary",)),
    )(*ws_, *gs, *ms, *vs)
    return [tuple(outs[j * n + k] for j in range(4)) for k in range(n)]


_REL_PAD = 384
_VEC_FIELDS = (("norm_g", 0, D_MODEL), ("b_gate", 1024, 2 * D_MODEL), ("sgu_ln_g", 3072, D_B),
               ("sgu_ln_b", 3584, D_B), ("b_s", 4096, N_GROUPS * 128), ("final_g", 4608, D_MODEL))
_LOSS_OFF = 5632
_REL_OFF = 5760
_NV = _REL_OFF + N_HEADS * _REL_PAD
_N_FIELDS = len(_VEC_FIELDS) + 2


_B_S_FIELD = [f[0] for f in _VEC_FIELDS].index("b_s")


def _assemble_row(dst, fields, transposed_b_s):
    for f, (_, off, n) in enumerate(_VEC_FIELDS):
        if transposed_b_s and f == _B_S_FIELD:
            t = fields[f][...].T
            for g in range(N_GROUPS):
                dst[:, off + 128 * g:off + 128 * (g + 1)] = t[g:g + 1, :]
        else:
            dst[:, off:off + n] = fields[f][...]
    for r in range(N_HEADS):
        dst[:, _REL_OFF + _REL_PAD * r:_REL_OFF + _REL_PAD * (r + 1)] = fields[len(_VEC_FIELDS)][r:r + 1, :]


def _small_reduce(grads, loss_row, after=()):
    n_in = _N_FIELDS + 1

    def body(*refs):
        g_refs, loss_ref = refs[:_N_FIELDS], refs[_N_FIELDS]
        out_v, out_w = refs[n_in:n_in + 2]
        mine_v, mine_w, gath_v, gath_w, send_sems, recv_sems = refs[n_in + 2:]
        x, y, c, chips = _mesh_pos()
        me, sibling = (x, y, c), (x, y, 1 - c)

        peers_entered = _signal_peers("both")
        _assemble_row(mine_v, g_refs, True)
        mine_v[:, _LOSS_OFF:_LOSS_OFF + 128] = loss_ref[...]
        mine_w[...] = g_refs[-1][...].astype(BF)
        peers_entered()
        my_k = 4 * x + 2 * y + c
        gath_v[my_k] = mine_v[...]
        gath_w[my_k] = mine_w[...]

        def copy(k, gath, block, to, src=None):
            dst = gath.at[4 * block[0] + 2 * block[1] + block[2]]
            return pltpu.make_async_remote_copy(
                src_ref=dst if src is None else src, dst_ref=dst,
                send_sem=send_sems.at[k], recv_sem=recv_sems.at[k], device_id=to, device_id_type=MESH)

        bufs = ((gath_v, mine_v), (gath_w, mine_w))
        first, passed = [], []
        for b, (gath, mine) in enumerate(bufs):
            first.append(copy(7 * b, gath, me, sibling, src=mine))
            first += [copy(7 * b + 1 + j, gath, me, (*chip, c), src=mine) for j, chip in enumerate(chips)]
        for cp in first:
            cp.start()
        for b, (gath, _) in enumerate(bufs):
            for j, chip in enumerate(chips):
                copy(7 * b + 1 + j, gath, (*chip, c), me).wait_recv()
                cp = copy(7 * b + 4 + j, gath, (*chip, c), sibling)
                cp.start()
                passed.append(cp)
        for b, (gath, _) in enumerate(bufs):
            copy(7 * b, gath, sibling, me).wait_recv()
            for j, chip in enumerate(chips):
                copy(7 * b + 4 + j, gath, (*chip, 1 - c), me).wait_recv()
        for cp in first + passed:
            cp.wait_send()

        tot_v, tot_w = gath_v[0], gath_w[0].astype(F32)
        for k in range(1, 8):
            tot_v = tot_v + gath_v[k]
            tot_w = tot_w + gath_w[k].astype(F32)
        out_v[...] = tot_v
        out_w[...] = tot_w

    vm = pl.BlockSpec(memory_space=pltpu.VMEM)
    return pl.pallas_call(
        _after(body, n_in, after), name="small_reduce",
        out_shape=(jax.ShapeDtypeStruct((1, _NV), F32), jax.ShapeDtypeStruct((N_GROUPS * 128, 128), F32)),
        in_specs=[vm] * n_in + [_ANY] * len(after), out_specs=[vm] * 2,
        scratch_shapes=[pltpu.VMEM((1, _NV), F32), pltpu.VMEM((N_GROUPS * 128, 128), BF),
                        pltpu.VMEM((8, 1, _NV), F32), pltpu.VMEM((8, N_GROUPS * 128, 128), BF),
                        pltpu.SemaphoreType.DMA((14,)), pltpu.SemaphoreType.DMA((14,))],
        compiler_params=_params(32, collective_id=_PEER_SETS["both"]),
    )(*grads, loss_row, *after)


def _small_adamw(tot_v, tot_w, params):
    n_in = 2 + 3 * _N_FIELDS

    def body(*refs):
        tv_ref, tw_ref = refs[:2]
        p_refs = [refs[2 + k * _N_FIELDS:2 + (k + 1) * _N_FIELDS] for k in range(3)]
        outs = refs[n_in:n_in + 4 * _N_FIELDS + 1]
        wmv = refs[-1]
        for k in range(3):
            _assemble_row(wmv.at[k], p_refs[k], False)
            wmv[k, :, _LOSS_OFF:_LOSS_OFF + 128] = jnp.zeros((1, 128), F32)
        tot_v, tot_w = tv_ref[...], tw_ref[...]
        res_v = (tot_v,) + _adamw_math(wmv[0], tot_v, wmv[1], wmv[2])
        res_w = (tot_w,) + _adamw_math(p_refs[0][-1][...], tot_w, p_refs[1][-1][...], p_refs[2][-1][...])
        for kind in range(4):
            o = outs[kind * _N_FIELDS:(kind + 1) * _N_FIELDS]
            for f, (_, off, n) in enumerate(_VEC_FIELDS):
                o[f][...] = res_v[kind][:, off:off + n]
            for r in range(N_HEADS):
                o[len(_VEC_FIELDS)][r:r + 1, :] = res_v[kind][:, _REL_OFF + _REL_PAD * r:_REL_OFF + _REL_PAD * r + N_REL]
            o[-1][...] = res_w[kind]
        outs[-1][...] = tot_v[:, _LOSS_OFF:_LOSS_OFF + 128]

    field_shapes = [(1, n) for _, _, n in _VEC_FIELDS] + [(N_HEADS, N_REL), (N_GROUPS * 128, 128)]
    vm = pl.BlockSpec(memory_space=pltpu.VMEM)
    operands = [tot_v, tot_w] + [a for p in params for a in p]
    assert len(operands) == n_in
    outs = pl.pallas_call(
        body, name="small_adamw",
        out_shape=tuple(jax.ShapeDtypeStruct(s, F32) for _ in range(4) for s in field_shapes)
        + (jax.ShapeDtypeStruct((1, 128), F32),),
        in_specs=[vm] * n_in, out_specs=[vm] * (4 * _N_FIELDS + 1),
        scratch_shapes=[pltpu.VMEM((3, 1, _NV), F32)],
        compiler_params=_params(32),
    )(*operands)
    return [outs[k * _N_FIELDS:(k + 1) * _N_FIELDS] for k in range(4)], outs[-1]


def _small_fields(norm_g, b_gate, ln_g, ln_b, b_s, final_g, rel_bias, w_s):
    rel = jnp.pad(rel_bias.reshape(N_HEADS, N_REL), ((0, 0), (0, _REL_PAD - N_REL)))
    return (norm_g, b_gate, ln_g, ln_b, b_s.reshape(1, N_GROUPS * 128), final_g.reshape(1, D_MODEL),
            rel, w_s.reshape(N_GROUPS * 128, 128))


def _small_outputs(fields):
    n_g, b_g, l_g, l_b, b_s, f_g, rel, w_s = fields
    return (n_g, b_g, rel.reshape(1, N_HEADS, N_REL), l_g, l_b,
            w_s.reshape(1, N_GROUPS, 128, 128), b_s.reshape(1, N_GROUPS, 128), f_g.reshape(D_MODEL))


def kernel(x, norm_g, w_in, b_gate, rel_bias, sgu_ln_g, sgu_ln_b, w_s, b_s, w_pa, w_pb, w_out, final_g, loss_target, m_norm_g, m_w_in, m_b_gate, m_rel_bias, m_sgu_ln_g, m_sgu_ln_b, m_w_s, m_b_s, m_w_pa, m_w_pb, m_w_out, m_final_g, v_norm_g, v_w_in, v_b_gate, v_rel_bias, v_sgu_ln_g, v_sgu_ln_b, v_w_s, v_b_s, v_w_pa, v_w_pb, v_w_out, v_final_g):
    S = x.shape[1]
    xs = x.reshape(S, D_MODEL)
    tgt = loss_target.reshape(S, D_MODEL)
    big_w = (w_in[0], w_pa[0], w_pb[0], w_out[0])
    big_m = (m_w_in[0], m_w_pa[0], m_w_pb[0], m_w_out[0])
    big_v = (v_w_in[0], v_w_pa[0], v_w_pb[0], v_w_out[0])
    rel = rel_bias[0]
    ws = w_s[0]
    bst = b_s[0].T
    fg = final_g.reshape(1, D_MODEL)
    chip = 2 * lax.axis_index("x") + lax.axis_index("y")
    pos = jnp.stack([lax.axis_index("c"), chip] + [(chip + k) % N_SHARD for k in range(1, N_SHARD)]).astype(jnp.int32)

    (w_in_bf,), staged, band_bias = _ag_weights((0,), big_w[:1], (1, 2, 3), big_w[1:], rel)
    ag_s = _split_start("ag_small_start", staged, 9, _gather_copies((1, 2, 3)), "chips", after=(w_in_bf,))

    ht, q3, k3, v3, zrest = _inproj_fwd(xs, norm_g, w_in_bf, after=(ag_s.token,))
    att, lse = _attn_fwd(q3, k3, v3, band_bias)
    w_pa_bf, w_pb_bf, w_out_bf = _split_wait("ag_small_wait", ag_s, _gather_copies((1, 2, 3)), att)
    (d_out, d_att, dzt, dzs, gw_out, gw_pa, gw_pb, g_bgate, g_final, loss_row,
     g_ws, g_bs_t, g_lng, g_lnb) = _tail_sgu(
        att, zrest, xs, tgt, w_pa_bf, w_pb_bf, w_out_bf, b_gate, fg, sgu_ln_g, sgu_ln_b, ws, bst)
    ws_s, ws_i = (1, 2, 3), (0,)

    x1s = _split_start("gx1s_start", [gw_pa, gw_pb, gw_out] + _x1_lands(ws_s), 12, _x1_copies(ws_s), "sibling")
    dq, dk, dv, d_gp = _attn_bwd(q3, k3, v3, d_att, lse, band_bias, after=(x1s.token,))
    got = _split_wait("gx1s_wait", x1s, _x1_copies(ws_s), dq)
    own_s, csb_s = _grad_add1_group(ws_s, got[:3], got[3:], pos)

    x2s = _split_start("gx2s_start", csb_s + _x2_lands(ws_s), 9, _x2_copies(3), "chips")
    gw_in, gw_in_bf = _gw_in(ht, dq, dk, dv, dzt, dzs, after=(x2s.token,))
    x1i = _split_start("gx1i_start", [gw_in_bf] + _x1_lands(ws_i, BF), 4, _x1_copies(ws_i), "sibling")
    got = _split_wait("gx2s_wait", x2s, _x2_copies(3), x1i.token)
    halves_s = _grad_add2_group(ws_s, own_s, got[3:])
    x3s = _split_start("gx3s_start", halves_s, 3, _x3_copies(ws_s), "sibling")
    g_rel = jnp.pad(d_gp[:, 384:384 + N_REL][:, ::-1], ((0, 0), (0, _REL_PAD - N_REL)))
    small_params = (_small_fields(norm_g, b_gate, sgu_ln_g, sgu_ln_b, b_s, final_g, rel_bias, w_s),
                    _small_fields(m_norm_g, m_b_gate, m_sgu_ln_g, m_sgu_ln_b, m_b_s, m_final_g, m_rel_bias, m_w_s),
                    _small_fields(v_norm_g, v_b_gate, v_sgu_ln_g, v_sgu_ln_b, v_b_s, v_final_g, v_rel_bias, v_w_s))
    relayouts = (g_rel,) + tuple(fields[-2] for fields in small_params)
    recv1_i = _split_wait("gx1i_wait", x1i, _x1_copies(ws_i), (x3s.token,) + relayouts)[1]
    csb_i = _grad_add1(0, gw_in, recv1_i, pos)

    x2i = _split_start("gx2i_start", [csb_i] + _x2_lands(ws_i), 3, _x2_copies(1), "chips")
    grad_x, g_norm = _dh_gradx(dq, dk, dv, dzt, dzs, w_in_bf, xs, norm_g, d_out, after=(x2i.token,))
    g_shards_s = _split_wait("gx3s_wait", x3s, _x3_copies(ws_s), grad_x)
    got = _split_wait("gx2i_wait", x2i, _x2_copies(1), grad_x)
    half_i = _grad_add2(0, gw_in, recv1_i, got[1], pos)
    x3i = _split_start("gx3i_start", [half_i], 1, _x3_copies(ws_i), "sibling")

    small_grads = (g_norm, g_bgate, g_lng, g_lnb, g_bs_t, g_final, g_rel, g_ws.reshape(N_GROUPS * 128, 128))
    tot_v, tot_w = _small_reduce(small_grads, loss_row, after=(x3i.token,))
    (gsum, sdelta, sm, sv), loss_out = _small_adamw(tot_v, tot_w, small_params)

    g_shard_i, = _split_wait("gx3i_wait", x3i, _x3_copies(ws_i), loss_out)
    big = _adamw(big_w, [g_shard_i] + g_shards_s, big_m, big_v)
    sg_out, sd_out, sm_out, sv_out = (_small_outputs(f) for f in (gsum, sdelta, sm, sv))
    loss = loss_out[0, 0]

    def assemble(small, bigs):
        n_g, b_g, r_b, l_g, l_b, w_s_, b_s_, f_g = small
        b_in, b_pa, b_pb, b_out = (b[None] for b in bigs)
        return (n_g, b_in, b_g, r_b, l_g, l_b, w_s_, b_s_, b_pa, b_pb, b_out, f_g)

    grads_out = assemble(sg_out, [b[3] for b in big])
    delta_out = assemble(sd_out, [b[0] for b in big])
    m_out = assemble(sm_out, [b[1] for b in big])
    v_out = assemble(sv_out, [b[2] for b in big])
    return (loss, grad_x.reshape(1, S, D_MODEL), *grads_out, *delta_out, *m_out, *v_out)
```

```python
import functools
import math

import jax
import jax.numpy as jnp
from jax import lax
from jax.experimental import pallas as pl
from jax.experimental.pallas import tpu as pltpu

F32 = jnp.float32
BF = jnp.bfloat16
MESH = pl.DeviceIdType.MESH

D_MODEL = 1024
D_A = 512
D_B = 512
D_IN = 5632
N_HEADS = 8
HEAD_DIM = 64
CHUNK = 64
N_PREV = 8
SGU_CHUNK = 128
N_GROUPS = 4
N_REL = 257
EPS = 1e-6
NEG_INF = -1e30
SCALE = HEAD_DIM ** -0.5

QB = 2 * CHUNK
KB = (N_PREV + 2) * CHUNK
PADK = N_PREV * CHUNK
ROLL_W = 1024
KEEP = KB // QB - 1
Q_PER_STEP = 2

ADAM_LR = 0.001
ADAM_B1 = 0.9
ADAM_B2 = 0.999
ADAM_EPS = 1e-08
ADAM_WD = 0.01
ADAM_STEP = 10
ADAM_C1 = 1.0 - ADAM_B1 ** ADAM_STEP
ADAM_C2 = 1.0 - ADAM_B2 ** ADAM_STEP

AG_PIECES = 4
N_SHARD = 4
SHARD_IN = D_IN // N_SHARD
MIB = 1024 * 1024


V7X_VMEM_MIB = 64
VMEM_RESERVE_MIB = V7X_VMEM_MIB - 4


def _params(vmem_mib, **kw):
    assert vmem_mib <= VMEM_RESERVE_MIB
    return pltpu.CompilerParams(vmem_limit_bytes=VMEM_RESERVE_MIB * MIB, **kw)


def _sigmoid(x):
    return 1.0 / (1.0 + jnp.exp(-x))


def _silu_and_grad(x):
    s = _sigmoid(x)
    return x * s, s * (1.0 + x * (1.0 - s))


_GELU_C = math.sqrt(2.0 / math.pi)
_GELU_A = 0.044715


def _gelu_and_grad(x):
    x2 = x * x
    t = jnp.tanh(_GELU_C * (x + _GELU_A * (x2 * x)))
    cdf = 0.5 * (1.0 + t)
    grad = cdf + 0.5 * x * (1.0 - t * t) * (_GELU_C * (1.0 + 3.0 * _GELU_A * x2))
    return x * cdf, grad


def _dot(a, b):
    return jnp.dot(a, b, preferred_element_type=F32)


def _dot_nt(a, b):
    return lax.dot_general(a, b, (((1,), (1,)), ((), ())), preferred_element_type=F32)


def _dot_tn(a, b):
    return lax.dot_general(a, b, (((0,), (0,)), ((), ())), preferred_element_type=F32)


def _mo(v, m):
    return v if isinstance(v, int) else pl.multiple_of(v, m)


def _unit_in(ref, s, p):
    return ref.at[pl.ds(_mo(p * 512, 512), 512), pl.ds(_mo(s * SHARD_IN, 128), SHARD_IN)]


def _unit_p(ref, s, p):
    return ref.at[pl.ds(_mo(p * 256, 256), 256), pl.ds(_mo(s * 256, 128), 256)]


def _unit_out(ref, s, p):
    return ref.at[pl.ds(_mo(s * 256 + p * 128, 128), 128), :]


_UNITS = (_unit_in, _unit_p, _unit_p, _unit_out)
_HALF_ROWS = (512, 256, 256, 128)
_UNIT_SHAPES = ((512, SHARD_IN), (256, 256), (256, 256), (128, D_MODEL))
_FULL_SHAPES = ((D_MODEL, D_IN), (D_A, D_MODEL), (D_B, D_MODEL), (D_MODEL, D_MODEL))
_SHARD_SHAPES = ((D_MODEL, SHARD_IN), (D_A, 256), (D_B, 256), (256, D_MODEL))


def _mesh_pos():
    x, y, c = lax.axis_index("x"), lax.axis_index("y"), lax.axis_index("c")
    chips = [(1 - x, y), (x, 1 - y), (1 - x, 1 - y)]
    return x, y, c, chips


def _bias_rows(rel_ref, pad_ref):
    pad_ref[...] = jnp.zeros(pad_ref.shape, F32)
    pad_ref[:, :N_REL] = rel_ref[...]
    r = pad_ref[...]
    m = lax.broadcasted_iota(jnp.int32, (pad_ref.shape[1], ROLL_W), 1)
    m = jnp.where(m >= ROLL_W - 192, m - ROLL_W, m)
    pick = (lax.broadcasted_iota(jnp.int32, m.shape, 0) == jnp.clip(512 - m, -128, 128) + 128).astype(BF)
    hi = r.astype(BF)
    mid = (r - hi.astype(F32)).astype(BF)
    lo = ((r - hi.astype(F32)) - mid.astype(F32)).astype(BF)
    return (_dot(hi, pick) + _dot(mid, pick)) + _dot(lo, pick)


def _ag_weights(ws, shards, later_ws, later_shards, rel):
    n, m = len(ws), len(later_ws)

    def body(*refs):
        ins, later_ins, rel_ref = refs[:n], refs[n:n + m], refs[n + m]
        o = n + m + 1
        outs, later_outs, bias_ref = refs[o:o + n], refs[o + n:o + n + m], refs[o + n + m]
        o += n + m + 1
        stage, later_stage = refs[o:o + n], refs[o + n:o + n + m]
        send_sems, recv_sems, local_sems, later_sems, rel_pad, gp_ref = refs[o + n + m:]
        x, y, c, chips = _mesh_pos()
        s_me = 2 * x + y
        sibling = (x, y, 1 - c)
        def rows_of(k, p):
            rows = _HALF_ROWS[ws[k]]
            return pl.ds(_mo(p * rows, rows), rows)

        def half(k, p):
            return stage[k].at[rows_of(k, p), :]

        def unit(k, s, p):
            return _UNITS[ws[k]](outs[k], s, p)

        def rcopy(k, i, src, dst, to):
            return pltpu.make_async_remote_copy(src_ref=src, dst_ref=dst, send_sem=send_sems.at[k, i],
                                                recv_sem=recv_sems.at[k, i], device_id=to, device_id_type=MESH)

        peers_entered = _signal_peers("both")
        for k in range(n):
            stage[k][rows_of(k, c), :] = ins[k][rows_of(k, c), :].astype(BF)
        peers_entered()
        def piece(ref, k, q):
            rows = _HALF_ROWS[ws[k]] // AG_PIECES
            return ref.at[pl.ds(q * rows, rows), :]

        sends = []
        for q in range(AG_PIECES):
            for j, (cx, cy) in enumerate(chips):
                for k in range(n):
                    cp = rcopy(k, j * AG_PIECES + q, piece(half(k, c), k, q), piece(unit(k, s_me, c), k, q),
                               (cx, cy, c))
                    cp.start()
                    sends.append(cp)
        for k in range(n):
            stage[k][rows_of(k, 1 - c), :] = ins[k][rows_of(k, 1 - c), :].astype(BF)
        local = []
        for k in range(n):
            for p in range(2):
                cp = pltpu.make_async_copy(half(k, p), unit(k, s_me, p), local_sems.at[k, p])
                cp.start()
                local.append(cp)
        for k, w in enumerate(later_ws):
            later_stage[k][...] = later_ins[k][...].astype(BF)
            cp = pltpu.make_async_copy(later_stage[k], _shard_of(later_outs[k], w, s_me), later_sems.at[k])
            cp.start()
            local.append(cp)
        keep = _struct_mask()
        gp_ref[...] = _bias_rows(rel_ref, rel_pad)
        for h in range(N_HEADS):
            bias_ref[h] = jnp.where(keep, _skew_table(gp_ref[h:h + 1, :])[:, :KB], NEG_INF)
        for q in range(AG_PIECES):
            for j, (cx, cy) in enumerate(chips):
                for k in range(n):
                    landed = piece(unit(k, 2 * cx + cy, c), k, q)
                    rcopy(k, j * AG_PIECES + q, landed, landed, (cx, cy, c)).wait_recv()
                    cp = rcopy(k, (3 + j) * AG_PIECES + q, landed, landed, sibling)
                    cp.start()
                    sends.append(cp)
        for q in range(AG_PIECES):
            for j, (cx, cy) in enumerate(chips):
                for k in range(n):
                    other = piece(unit(k, 2 * cx + cy, 1 - c), k, q)
                    rcopy(k, (3 + j) * AG_PIECES + q, other, other, sibling).wait_recv()
        for cp in sends:
            cp.wait_send()
        for cp in local:
            cp.wait()

    vm = pl.BlockSpec(memory_space=pltpu.VMEM)
    outs = pl.pallas_call(
        body, name="ag_weights",
        out_shape=tuple(jax.ShapeDtypeStruct(_FULL_SHAPES[w], BF) for w in tuple(ws) + tuple(later_ws))
        + (jax.ShapeDtypeStruct((N_HEADS, QB, KB), F32),),
        in_specs=[vm] * (n + m + 1), out_specs=[_ANY] * (n + m) + [vm],
        scratch_shapes=[pltpu.VMEM(_SHARD_SHAPES[w], BF) for w in tuple(ws) + tuple(later_ws)]
        + [pltpu.SemaphoreType.DMA((n, 6 * AG_PIECES)), pltpu.SemaphoreType.DMA((n, 6 * AG_PIECES)),
           pltpu.SemaphoreType.DMA((n, 2)), pltpu.SemaphoreType.DMA((m,)),
           pltpu.VMEM((N_HEADS, _REL_PAD), F32), pltpu.VMEM((N_HEADS, ROLL_W), F32)],
        compiler_params=_params(48, collective_id=_PEER_SETS["both"]),
    )(*shards, *later_shards, rel)
    return list(outs[:n]), list(outs[n:n + m]), outs[-1]


def _shard_of(ref, w, s):
    if w == 0:
        return ref.at[:, pl.ds(_mo(s * SHARD_IN, 128), SHARD_IN)]
    if w == 3:
        return ref.at[pl.ds(_mo(s * 256, 256), 256), :]
    return ref.at[:, pl.ds(_mo(s * 256, 128), 256)]


def _gather_copies(ws):
    def copies(refs, send_sems, recv_sems):
        x, y, c, chips = _mesh_pos()
        out = []
        for j, (cx, cy) in enumerate(chips):
            for k, w in enumerate(ws):
                mine = _shard_of(refs[k], w, 2 * x + y)
                out.append(pltpu.make_async_remote_copy(
                    src_ref=mine, dst_ref=mine, send_sem=send_sems.at[3 * k + j], recv_sem=recv_sems.at[3 * k + j],
                    device_id=(cx, cy, c), device_id_type=MESH))
        return out
    return copies


W_IN_CHUNKS = D_IN // 512
_W_IN_SCRATCH = [pltpu.VMEM((D_MODEL, D_IN), BF), pltpu.SemaphoreType.DMA((W_IN_CHUNKS,))]


def _with_w_in(w_hbm, w_scr, sems, compute):
    chunks = [pltpu.make_async_copy(w_hbm.at[:, j * 512:(j + 1) * 512], w_scr.at[:, j * 512:(j + 1) * 512],
                                    sems.at[j]) for j in range(W_IN_CHUNKS)]

    @pl.when(pl.program_id(0) == 0)
    def _():
        for cp in chunks:
            cp.start()
        compute(lambda j: chunks[j].wait())

    @pl.when(pl.program_id(0) > 0)
    def _():
        compute(lambda j: None)


def _inproj_fwd(x, norm_g, w_in_bf, tm=512, after=()):
    S = x.shape[0]

    def body(x_ref, g_ref, w_hbm, ht_ref, q_ref, k_ref, v_ref, zr_ref, w_ref, w_sems):
        def compute(wait):
            xv = x_ref[...]
            r = lax.rsqrt(jnp.mean(xv * xv, axis=-1, keepdims=True) + EPS)
            hf = (xv * r) * g_ref[...]
            ht_ref[...] = hf.T.astype(BF)
            h = hf.astype(BF)
            heads = (q_ref, k_ref, v_ref)
            for j in range(W_IN_CHUNKS):
                wait(j)
                z = _dot(h, w_ref[:, j * 512:(j + 1) * 512])
                if j < 3:
                    zb = z.astype(BF)
                    for hd in range(N_HEADS):
                        heads[j][hd] = zb[:, hd * HEAD_DIM:(hd + 1) * HEAD_DIM]
                else:
                    zr_ref[:, (j - 3) * 512:(j - 2) * 512] = z

        _with_w_in(w_hbm, w_ref, w_sems, compute)

    head_major = jax.ShapeDtypeStruct((N_HEADS, S, HEAD_DIM), BF)
    head_spec = pl.BlockSpec((N_HEADS, tm, HEAD_DIM), lambda i: (0, i, 0))
    return pl.pallas_call(
        _after(body, 3, after), name="inproj_fwd", grid=(S // tm,),
        out_shape=(jax.ShapeDtypeStruct((D_MODEL, S), BF), head_major, head_major, head_major,
                   jax.ShapeDtypeStruct((S, D_IN - 3 * D_A), F32)),
        in_specs=[pl.BlockSpec((tm, D_MODEL), lambda i: (i, 0)),
                  pl.BlockSpec((1, D_MODEL), lambda i: (0, 0)),
                  _ANY]
        + [_ANY] * len(after),
        out_specs=[pl.BlockSpec((D_MODEL, tm), lambda i: (0, i)),
                   head_spec, head_spec, head_spec,
                   pl.BlockSpec((tm, D_IN - 3 * D_A), lambda i: (i, 0))],
        scratch_shapes=_W_IN_SCRATCH,
        compiler_params=_params(52, dimension_semantics=("arbitrary",)),
    )(x, norm_g, w_in_bf, *after)


def _skew_table(gp_row):
    row = lax.broadcasted_iota(jnp.int32, (QB, ROLL_W), 0)
    t = jnp.broadcast_to(gp_row, (QB, ROLL_W))
    for b in range(7):
        t = jnp.where(((row >> b) & 1) == 1, pltpu.roll(t, 1 << b, axis=1), t)
    return t


def _unskew_sum(d):
    half = QB // 2
    while half >= 8:
        d = d[0:half] + pltpu.roll(d[half:2 * half], ROLL_W - half, axis=1)
        half //= 2
    row = lax.broadcasted_iota(jnp.int32, (8, ROLL_W), 0)
    for b in range(3):
        d = jnp.where(((row >> b) & 1) == 1, pltpu.roll(d, ROLL_W - (1 << b), axis=1), d)
    return jnp.sum(d, axis=0, keepdims=True)


def _struct_mask():
    a = lax.broadcasted_iota(jnp.int32, (QB, KB), 0) // CHUNK
    b = lax.broadcasted_iota(jnp.int32, (QB, KB), 1) // CHUNK
    return (b >= a) & (b <= a + N_PREV)


KV_REST_STEP = 2
KV_HEAD_ROWS = KV_REST_STEP * Q_PER_STEP * QB


def _kv_copies(k_hbm, v_hbm, k_scr, v_scr, sems, S):
    def part(src, dst, lo, n, sem):
        return pltpu.make_async_copy(src.at[:, pl.ds(lo, n), :], dst.at[:, pl.ds(PADK + lo, n), :], sem)

    rest = S - KV_HEAD_ROWS
    return ([part(k_hbm, k_scr, 0, KV_HEAD_ROWS, sems.at[0]), part(v_hbm, v_scr, 0, KV_HEAD_ROWS, sems.at[1])],
            [part(k_hbm, k_scr, KV_HEAD_ROWS, rest, sems.at[2]), part(v_hbm, v_scr, KV_HEAD_ROWS, rest, sems.at[3])])


def _load_kv(k_hbm, v_hbm, k_scr, v_scr, sems, S, meanwhile=lambda: None):
    copies = _kv_copies(k_hbm, v_hbm, k_scr, v_scr, sems, S)

    @pl.when(pl.program_id(0) == 0)
    def _():
        zeros = jnp.zeros((N_HEADS, PADK, HEAD_DIM), BF)
        k_scr[:, 0:PADK, :] = zeros
        v_scr[:, 0:PADK, :] = zeros
        for cp in copies[0] + copies[1]:
            cp.start()
        meanwhile()
        for cp in copies[0]:
            cp.wait()

    @pl.when(pl.program_id(0) == KV_REST_STEP)
    def _():
        for cp in copies[1]:
            cp.wait()


_BATCH_NT = (((2,), (2,)), ((0,), (0,)))
_BATCH_NN = (((2,), (1,)), ((0,), (0,)))
_BATCH_TN = (((1,), (1,)), ((0,), (0,)))


def _bdot(a, b, dims):
    return lax.dot_general(a, b, dims, preferred_element_type=F32)


def _scaled(q):
    return q * jnp.asarray(SCALE, BF)


def _scores(qs, kb, bias, i, front):
    s = _bdot(qs, kb, _BATCH_NT) + bias
    if front:
        col = lax.broadcasted_iota(jnp.int32, (1, 1, KB), 2)
        s = jnp.where(col >= PADK - i * QB, s, NEG_INF)
    return s


def _attn_fwd(q3, k3, v3, bias):
    S = q3.shape[1]

    def body(q_ref, k_hbm, v_hbm, bias_ref, o_ref, lse_ref, k_scr, v_scr, sems):
        _load_kv(k_hbm, v_hbm, k_scr, v_scr, sems, S)

        def step(i, rows, front):
            start = pl.multiple_of(i * QB, QB)
            kb = k_scr[:, pl.ds(start, KB), :]
            vb = v_scr[:, pl.ds(start, KB), :]
            s = _scores(_scaled(q_ref[:, rows, :]), kb, bias_ref[...], i, front)
            m = jnp.max(s, axis=-1, keepdims=True)
            e = jnp.exp(s - m)
            l = jnp.sum(e, axis=-1, keepdims=True)
            p = e * (1.0 / l)
            o = _bdot(p.astype(BF), vb, _BATCH_NN)
            lse_ref[:, rows, :] = jnp.broadcast_to(m + jnp.log(l), (N_HEADS, QB, 128))
            for h in range(N_HEADS):
                o_ref[rows, h * HEAD_DIM:(h + 1) * HEAD_DIM] = o[h]

        def block(j, carry):
            i = pl.program_id(0) * Q_PER_STEP + j
            rows = pl.ds(pl.multiple_of(j * QB, QB), QB)
            pl.when(i < KEEP)(functools.partial(step, i, rows, True))
            pl.when(i >= KEEP)(functools.partial(step, i, rows, False))
            return carry

        lax.fori_loop(0, Q_PER_STEP, block, 0)

    rows_per_step = Q_PER_STEP * QB
    kv_scr = pltpu.VMEM((N_HEADS, S + PADK, HEAD_DIM), BF)
    return pl.pallas_call(
        body, name="attn_fwd", grid=(S // rows_per_step,),
        out_shape=(jax.ShapeDtypeStruct((S, D_A), F32), jax.ShapeDtypeStruct((N_HEADS, S, 128), F32)),
        in_specs=[pl.BlockSpec((N_HEADS, rows_per_step, HEAD_DIM), lambda g: (0, g, 0)),
                  pl.BlockSpec(memory_space=pl.ANY), pl.BlockSpec(memory_space=pl.ANY),
                  pl.BlockSpec((N_HEADS, QB, KB), lambda g: (0, 0, 0))],
        out_specs=[pl.BlockSpec((rows_per_step, D_A), lambda g: (g, 0)),
                   pl.BlockSpec((N_HEADS, rows_per_step, 128), lambda g: (0, g, 0))],
        scratch_shapes=[kv_scr, kv_scr, pltpu.SemaphoreType.DMA((4,))],
        compiler_params=_params(48, dimension_semantics=("arbitrary",)),
    )(q3, k3, v3, bias)


def _attn_bwd(q3, k3, v3, d_att3, lse, bias, after=()):
    S = q3.shape[1]
    nq = S // QB

    def body(q_ref, do_ref, k_hbm, v_hbm, lse_ref, bias_ref, dq_ref, dk_ref, dv_ref, dgp_ref,
             k_scr, v_scr, dk_acc, dv_acc, dbias_acc, pad_scr, sems):
        def clear():
            dk_acc[...] = jnp.zeros_like(dk_acc)
            dv_acc[...] = jnp.zeros_like(dv_acc)
            dbias_acc[...] = jnp.zeros_like(dbias_acc)
        _load_kv(k_hbm, v_hbm, k_scr, v_scr, sems, S, clear)

        def step(i, rows, front):
            start = pl.multiple_of(i * QB, QB)
            kb = k_scr[:, pl.ds(start, KB), :]
            vb = v_scr[:, pl.ds(start, KB), :]
            qs = _scaled(q_ref[:, rows, :])
            do = do_ref[:, rows, :]
            p = jnp.exp(_scores(qs, kb, bias_ref[...], i, front) - jnp.tile(lse_ref[:, rows, :], (1, 1, KB // 128)))
            dp = _bdot(do, vb, _BATCH_NT)
            ds = p * (dp - jnp.sum(dp * p, axis=-1, keepdims=True))
            dbias_acc[...] += ds
            dsb = ds.astype(BF)
            dq = _bdot(dsb, kb, _BATCH_NN) * SCALE
            for h in range(N_HEADS):
                dq_ref[rows, h * HEAD_DIM:(h + 1) * HEAD_DIM] = dq[h].astype(BF)
            dk_acc[...] += _bdot(dsb, qs, _BATCH_TN)
            dv_acc[...] += _bdot(p.astype(BF), do, _BATCH_TN)

        def block(j, carry):
            i = pl.program_id(0) * Q_PER_STEP + j
            rows = pl.ds(pl.multiple_of(j * QB, QB), QB)
            pl.when(i < KEEP)(functools.partial(step, i, rows, True))
            pl.when((i >= KEEP) & (i < nq))(functools.partial(step, i, rows, False))
            for h in range(N_HEADS):
                hs = slice(h * HEAD_DIM, (h + 1) * HEAD_DIM)
                dk_ref[rows, hs] = dk_acc[h, 0:QB, :].astype(BF)
                dv_ref[rows, hs] = dv_acc[h, 0:QB, :].astype(BF)
            dk_acc[:, 0:KB - QB, :] = dk_acc[:, QB:KB, :]
            dv_acc[:, 0:KB - QB, :] = dv_acc[:, QB:KB, :]
            dk_acc[:, KB - QB:KB, :] = jnp.zeros((N_HEADS, QB, HEAD_DIM), F32)
            dv_acc[:, KB - QB:KB, :] = jnp.zeros((N_HEADS, QB, HEAD_DIM), F32)
            return carry

        lax.fori_loop(0, Q_PER_STEP, block, 0)

        @pl.when(pl.program_id(0) == n_steps - 1)
        def _():
            lane = lax.broadcasted_iota(jnp.int32, (1, ROLL_W), 1)
            hi = (lane < 384) | (lane >= 832)
            lo = (lane > 640) & (lane < 832)
            pad_scr[...] = jnp.zeros_like(pad_scr)
            for h in range(N_HEADS):
                pad_scr[:, 0:KB] = dbias_acc[h]
                g = _unskew_sum(pad_scr[...])
                s_hi = jnp.sum(jnp.where(hi, g, 0.0), axis=-1, keepdims=True)
                s_lo = jnp.sum(jnp.where(lo, g, 0.0), axis=-1, keepdims=True)
                g = jnp.where(lane == 384, g + s_hi, g)
                g = jnp.where(lane == 640, g + s_lo, g)
                dgp_ref[h:h + 1, :] = g

    assert nq % Q_PER_STEP == 0 and KEEP % Q_PER_STEP == 0
    rows_per_step = Q_PER_STEP * QB
    n_steps = (nq + KEEP) // Q_PER_STEP
    last = nq // Q_PER_STEP - 1
    lag = KEEP // Q_PER_STEP
    kv_scr = pltpu.VMEM((N_HEADS, S + PADK, HEAD_DIM), BF)
    return pl.pallas_call(
        _after(body, 6, after), name="attn_bwd", grid=(n_steps,),
        out_shape=(jax.ShapeDtypeStruct((S, D_A), BF), jax.ShapeDtypeStruct((S, D_A), BF),
                   jax.ShapeDtypeStruct((S, D_A), BF), jax.ShapeDtypeStruct((N_HEADS, ROLL_W), F32)),
        in_specs=[pl.BlockSpec((N_HEADS, rows_per_step, HEAD_DIM), lambda g: (0, jnp.minimum(g, last), 0)),
                  pl.BlockSpec((N_HEADS, rows_per_step, HEAD_DIM), lambda g: (0, jnp.minimum(g, last), 0)),
                  pl.BlockSpec(memory_space=pl.ANY), pl.BlockSpec(memory_space=pl.ANY),
                  pl.BlockSpec((N_HEADS, rows_per_step, 128), lambda g: (0, jnp.minimum(g, last), 0)),
                  pl.BlockSpec((N_HEADS, QB, KB), lambda g: (0, 0, 0))] + [_ANY] * len(after),
        out_specs=[pl.BlockSpec((rows_per_step, D_A), lambda g: (jnp.minimum(g, last), 0)),
                   pl.BlockSpec((rows_per_step, D_A), lambda g: (jnp.maximum(g - lag, 0), 0)),
                   pl.BlockSpec((rows_per_step, D_A), lambda g: (jnp.maximum(g - lag, 0), 0)),
                   pl.BlockSpec((N_HEADS, ROLL_W), lambda g: (0, 0))],
        scratch_shapes=[kv_scr, kv_scr,
                        pltpu.VMEM((N_HEADS, KB, HEAD_DIM), F32), pltpu.VMEM((N_HEADS, KB, HEAD_DIM), F32),
                        pltpu.VMEM((N_HEADS, QB, KB), F32), pltpu.VMEM((QB, ROLL_W), F32),
                        pltpu.SemaphoreType.DMA((4,))],
        compiler_params=_params(56, dimension_semantics=("arbitrary",)),
    )(q3, d_att3, k3, v3, lse, bias, *after)


def _sgu_core(ub, vb, lg, lb):
    u, du = _gelu_and_grad(ub)
    v, dv = _gelu_and_grad(vb)
    mu = jnp.mean(v, axis=-1, keepdims=True)
    vc = v - mu
    rstd = lax.rsqrt(jnp.mean(vc * vc, axis=-1, keepdims=True) + EPS)
    xh = vc * rstd
    vn = xh * lg + lb
    return u, du, dv, rstd, xh, vn


def _tri():
    r = lax.broadcasted_iota(jnp.int32, (SGU_CHUNK, SGU_CHUNK), 0)
    c = lax.broadcasted_iota(jnp.int32, (SGU_CHUNK, SGU_CHUNK), 1)
    return r >= c


def _tail_sgu(att, zrest, x, target, w_pa, w_pb, w_out, b_gate, final_g, ln_g, ln_b, w_s, b_s_t, tm=256):
    S = x.shape[0]
    nt = S // tm
    chunks = tm // SGU_CHUNK

    def body(att_ref, ga_ref, ub_ref, vb_ref, gb_ref, gta_ref, gtb_ref, x_ref, t_ref,
             wpa_ref, wpb_ref, wout_ref, bg_ref, fg_ref, lg_ref, lb_ref, ws_ref, bst_ref,
             dout_ref, datt_ref, dzt_ref, dzs_ref, gwout_hbm, gwpa_hbm, gwpb_hbm,
             gbg_ref, gfg_ref, loss_ref, gws_ref, gbs_ref, glg_ref, glb_ref,
             acc_out, acc_pa, acc_pb, sg_scr, mix_scr, dvn_scr, bs_acc, sems):
        i = pl.program_id(0)

        @pl.when(i == 0)
        def _():
            for r in (acc_out, acc_pa, acc_pb, gbg_ref, gfg_ref, loss_ref, gws_ref, glg_ref, glb_ref, bs_acc):
                r[...] = jnp.zeros_like(r)

        u, du, dv, rstd, xh, vn = _sgu_core(ub_ref[...], vb_ref[...], lg_ref[...], lb_ref[...])
        vnb = vn.astype(BF)
        tri = _tri()
        blocks = [(g, slice(n * SGU_CHUNK, (n + 1) * SGU_CHUNK), slice(g * 128, (g + 1) * 128))
                  for g in range(N_GROUPS) for n in range(chunks)]
        wts = [jnp.where(tri, ws_ref[g], 0.0) for g in range(N_GROUPS)]
        for g, rs, cs in blocks:
            mixed = _dot(wts[g].astype(BF), vnb[rs, cs]) + bst_ref[:, g:g + 1]
            mix_scr[rs, cs] = mixed
            sg_scr[rs, cs] = u[rs, cs] * mixed

        att = att_ref[...]
        sg = sg_scr[...]
        sa, dsa = _silu_and_grad(ga_ref[...])
        sb, dsb = _silu_and_grad(gb_ref[...])
        ya = (att * sa).astype(BF)
        yb = (sg * sb).astype(BF)
        pa = _dot(ya, wpa_ref[...])
        pb = _dot(yb, wpb_ref[...])
        ga = _sigmoid(gta_ref[...] + bg_ref[:, 0:D_MODEL])
        gb = _sigmoid(gtb_ref[...] + bg_ref[:, D_MODEL:2 * D_MODEL])
        merged = (ga * pa + gb * pb).astype(BF)
        out = x_ref[...] + _dot(merged, wout_ref[...])
        r2 = lax.rsqrt(jnp.mean(out * out, axis=-1, keepdims=True) + EPS)
        nrm = out * r2
        fg = fg_ref[...]
        err = nrm * fg - t_ref[...]
        loss_ref[...] += 0.5 * jnp.sum(jnp.mean(err * err, axis=-1, keepdims=True))
        dy = err * (1.0 / D_MODEL)
        gfg_ref[...] += jnp.sum(dy * nrm, axis=0, keepdims=True)
        dn = dy * fg
        d_out = r2 * (dn - nrm * jnp.mean(dn * nrm, axis=-1, keepdims=True))
        dout_ref[...] = d_out
        d_outb = d_out.astype(BF)
        acc_out[...] += _dot_tn(merged, d_outb)
        dm = _dot_nt(d_outb, wout_ref[...])
        d_pa = (dm * ga).astype(BF)
        d_pb = (dm * gb).astype(BF)
        d_gta = dm * pa * (ga * (1.0 - ga))
        d_gtb = dm * pb * (gb * (1.0 - gb))
        gbg_ref[:, 0:D_MODEL] += jnp.sum(d_gta, axis=0, keepdims=True)
        gbg_ref[:, D_MODEL:2 * D_MODEL] += jnp.sum(d_gtb, axis=0, keepdims=True)
        dzt_ref[:, 2 * D_A:2 * D_A + D_MODEL] = d_gta.astype(BF)
        dzt_ref[:, 2 * D_A + D_MODEL:] = d_gtb.astype(BF)
        acc_pa[...] += _dot_tn(ya, d_pa)
        acc_pb[...] += _dot_tn(yb, d_pb)
        d_ya = _dot_nt(d_pa, wpa_ref[...])
        d_yb = _dot_nt(d_pb, wpb_ref[...])
        d_att = (d_ya * sa).astype(BF)
        for hd in range(N_HEADS):
            datt_ref[hd] = d_att[:, hd * HEAD_DIM:(hd + 1) * HEAD_DIM]
        dzt_ref[:, 0:D_A] = (d_ya * att * dsa).astype(BF)
        dzt_ref[:, D_A:2 * D_A] = (d_yb * sg * dsb).astype(BF)

        dsg = d_yb * sb
        dzs_ref[:, 0:D_B] = (dsg * mix_scr[...] * du).astype(BF)
        dmix = dsg * u
        for g, rs, cs in blocks:
            dmb = dmix[rs, cs].astype(BF)
            bs_acc[:, cs] += dmix[rs, cs]
            gws_ref[g] += _dot_nt(dmb, vnb[rs, cs])
            dvn_scr[rs, cs] = _dot(wts[g].T.astype(BF), dmb)
        dvn = dvn_scr[...]
        glg_ref[...] += jnp.sum(dvn * xh, axis=0, keepdims=True)
        glb_ref[...] += jnp.sum(dvn, axis=0, keepdims=True)
        dxh = dvn * lg_ref[...]
        dvv = rstd * (dxh - jnp.mean(dxh, axis=-1, keepdims=True)
                      - xh * jnp.mean(dxh * xh, axis=-1, keepdims=True))
        dzs_ref[:, D_B:2 * D_B] = (dvv * dv).astype(BF)

        @pl.when(i == nt - 1)
        def _():
            cps = [pltpu.make_async_copy(acc_out, gwout_hbm, sems.at[0]),
                   pltpu.make_async_copy(acc_pa, gwpa_hbm, sems.at[1]),
                   pltpu.make_async_copy(acc_pb, gwpb_hbm, sems.at[2])]
            for cp in cps:
                cp.start()
            lane = lax.broadcasted_iota(jnp.int32, (SGU_CHUNK, 128), 1)
            cols = jnp.zeros((SGU_CHUNK, 128), F32)
            for g in range(N_GROUPS):
                gws_ref[g] = jnp.where(tri, gws_ref[g], 0.0)
                col = jnp.sum(bs_acc[:, g * 128:(g + 1) * 128], axis=-1, keepdims=True)
                cols = jnp.where(lane == g, col, cols)
            gbs_ref[...] = cols
            for cp in cps:
                cp.wait()

    c2 = lambda i: (0, 0)
    c3 = lambda i: (0, 0, 0)
    zcol = lambda w, blk: pl.BlockSpec((tm, w), lambda i: (i, blk))
    row = lambda w: pl.BlockSpec((tm, w), lambda i: (i, 0))
    return pl.pallas_call(
        body, name="tail", grid=(nt,),
        out_shape=(jax.ShapeDtypeStruct((S, D_MODEL), F32), jax.ShapeDtypeStruct((N_HEADS, S, HEAD_DIM), BF),
                   jax.ShapeDtypeStruct((S, 3072), BF), jax.ShapeDtypeStruct((S, 2 * D_B), BF),
                   jax.ShapeDtypeStruct((D_MODEL, D_MODEL), F32), jax.ShapeDtypeStruct((D_A, D_MODEL), F32),
                   jax.ShapeDtypeStruct((D_B, D_MODEL), F32),
                   jax.ShapeDtypeStruct((1, 2 * D_MODEL), F32), jax.ShapeDtypeStruct((1, D_MODEL), F32),
                   jax.ShapeDtypeStruct((1, 128), F32),
                   jax.ShapeDtypeStruct((N_GROUPS, 128, 128), F32), jax.ShapeDtypeStruct((SGU_CHUNK, 128), F32),
                   jax.ShapeDtypeStruct((1, D_B), F32), jax.ShapeDtypeStruct((1, D_B), F32)),
        in_specs=[row(D_A), zcol(512, 0), zcol(512, 1), zcol(512, 2), zcol(512, 3),
                  zcol(D_MODEL, 2), zcol(D_MODEL, 3), row(D_MODEL), row(D_MODEL),
                  pl.BlockSpec((D_A, D_MODEL), c2), pl.BlockSpec((D_B, D_MODEL), c2),
                  pl.BlockSpec((D_MODEL, D_MODEL), c2),
                  pl.BlockSpec((1, 2 * D_MODEL), c2), pl.BlockSpec((1, D_MODEL), c2),
                  pl.BlockSpec((1, D_B), c2), pl.BlockSpec((1, D_B), c2),
                  pl.BlockSpec((N_GROUPS, 128, 128), c3), pl.BlockSpec((128, N_GROUPS), c2)],
        out_specs=[row(D_MODEL), pl.BlockSpec((N_HEADS, tm, HEAD_DIM), lambda i: (0, i, 0)),
                   row(3072), row(2 * D_B), _ANY, _ANY, _ANY,
                   pl.BlockSpec((1, 2 * D_MODEL), c2), pl.BlockSpec((1, D_MODEL), c2),
                   pl.BlockSpec((1, 128), c2),
                   pl.BlockSpec((N_GROUPS, 128, 128), c3), pl.BlockSpec((SGU_CHUNK, 128), c2),
                   pl.BlockSpec((1, D_B), c2), pl.BlockSpec((1, D_B), c2)],
        scratch_shapes=[pltpu.VMEM((D_MODEL, D_MODEL), F32), pltpu.VMEM((D_A, D_MODEL), F32),
                        pltpu.VMEM((D_B, D_MODEL), F32),
                        pltpu.VMEM((tm, D_B), F32), pltpu.VMEM((tm, D_B), F32), pltpu.VMEM((tm, D_B), F32),
                        pltpu.VMEM((SGU_CHUNK, D_B), F32), pltpu.SemaphoreType.DMA((3,))],
        compiler_params=_params(58, dimension_semantics=("arbitrary",)),
    )(att, zrest, zrest, zrest, zrest, zrest, zrest, x, target, w_pa, w_pb, w_out, b_gate, final_g,
      ln_g, ln_b, w_s, b_s_t)


_DZ_MAP = ((0, 0), (1, 0), (2, 0), (3, 0), (4, 0), (4, 1), (3, 1), (3, 2), (3, 3), (3, 4), (3, 5))


def _dh_gradx(dq, dk, dv, dzt, dzs, w_in_bf, x, norm_g, d_out, tm=512, after=()):
    S = x.shape[0]

    def body(dq_ref, dk_ref, dv_ref, dzt_ref, dzs_ref, w_hbm, x_ref, g_ref, dout_ref, gx_ref, gn_ref,
             w_ref, w_sems):
        i = pl.program_id(0)

        @pl.when(i == 0)
        def _():
            gn_ref[...] = jnp.zeros_like(gn_ref)

        def compute(wait):
            pieces = (dq_ref, dk_ref, dv_ref, dzt_ref, dzs_ref)
            dh = jnp.zeros((tm, D_MODEL), F32)
            for j, (pc, blk) in enumerate(_DZ_MAP):
                wait(j)
                dh += _dot_nt(pieces[pc][:, blk * 512:(blk + 1) * 512], w_ref[:, j * 512:(j + 1) * 512])
            xv = x_ref[...]
            r = lax.rsqrt(jnp.mean(xv * xv, axis=-1, keepdims=True) + EPS)
            nrm = xv * r
            gn_ref[...] += jnp.sum(dh * nrm, axis=0, keepdims=True)
            dn = dh * g_ref[...]
            gx_ref[...] = r * (dn - nrm * jnp.mean(dn * nrm, axis=-1, keepdims=True)) + dout_ref[...]

        _with_w_in(w_hbm, w_ref, w_sems, compute)

    row = lambda w: pl.BlockSpec((tm, w), lambda i: (i, 0))
    c2 = lambda i: (0, 0)
    return pl.pallas_call(
        _after(body, 9, after), name="dh_gradx", grid=(S // tm,),
        out_shape=(jax.ShapeDtypeStruct((S, D_MODEL), F32), jax.ShapeDtypeStruct((1, D_MODEL), F32)),
        in_specs=[row(512), row(512), row(512), row(3072), row(1024),
                  _ANY, row(D_MODEL),
                  pl.BlockSpec((1, D_MODEL), c2), row(D_MODEL)]
        + [_ANY] * len(after),
        out_specs=[row(D_MODEL), pl.BlockSpec((1, D_MODEL), c2)],
        scratch_shapes=_W_IN_SCRATCH,
        compiler_params=_params(48, dimension_semantics=("arbitrary",)),
    )(dq, dk, dv, dzt, dzs, w_in_bf, x, norm_g, d_out, *after)


def _gw_in(ht, dq, dk, dv, dzt, dzs, tn=512, after=()):
    S = ht.shape[1]
    per = 512 // tn
    cols = tuple((pc, per * blk + h) for pc, blk in _DZ_MAP for h in range(per))

    def body(ht_ref, dq_ref, dk_ref, dv_ref, dzt_ref, dzs_ref, o_ref, ob_ref):
        j = pl.program_id(0)
        pieces = (dq_ref, dk_ref, dv_ref, dzt_ref, dzs_ref)
        for pc in range(5):
            hit = functools.reduce(jnp.logical_or, [j == jj for jj, (p, _) in enumerate(cols) if p == pc])

            @pl.when(hit)
            def _(pc=pc):
                g = _dot(ht_ref[...], pieces[pc][...])
                o_ref[...] = g
                ob_ref[...] = g.astype(BF)

    def piece_spec(pc):
        cur = next(blk for p, blk in cols if p == pc)
        held = []
        for p, blk in cols:
            cur = blk if p == pc else cur
            held.append(cur)

        def index_map(j):
            blk = jnp.int32(held[0])
            for jj in range(1, len(held)):
                if held[jj] != held[jj - 1]:
                    blk = jnp.where(j >= jj, jnp.int32(held[jj]), blk)
            return (0, blk)

        return pl.BlockSpec((S, tn), index_map)

    return pl.pallas_call(
        _after(body, 6, after), name="gw_in", grid=(len(cols),),
        out_shape=(jax.ShapeDtypeStruct((D_MODEL, D_IN), F32), jax.ShapeDtypeStruct((D_MODEL, D_IN), BF)),
        in_specs=[pl.BlockSpec((D_MODEL, S), lambda j: (0, 0), pipeline_mode=pl.Buffered(1))]
        + [piece_spec(pc) for pc in range(5)]
        + [_ANY] * len(after),
        out_specs=[pl.BlockSpec((D_MODEL, tn), lambda j: (0, j)), pl.BlockSpec((D_MODEL, tn), lambda j: (0, j))],
        compiler_params=_params(56, dimension_semantics=("arbitrary",)),
    )(ht, dq, dk, dv, dzt, dzs, *after)


_HBM = pl.BlockSpec(memory_space=pltpu.HBM)
_SEM = pl.BlockSpec(memory_space=pltpu.SEMAPHORE)
_ANY = pl.BlockSpec(memory_space=pl.ANY)
_EFFECT = pltpu.SideEffectType.DATAFLOW_SIDE_EFFECTING


def _in_hbm(a):
    return pltpu.with_memory_space_constraint(a, pltpu.HBM)


def _after(body, n_in, after):
    if not after:
        return body
    return lambda *refs: body(*refs[:n_in], *refs[n_in + len(after):])


class _Started:
    def __init__(self, send, recv, bufs, token):
        self.send, self.recv, self.bufs, self.token = send, recv, bufs, token


_PEER_SETS = {"sibling": 7, "chips": 8, "both": 9}


def _peers(kind):
    x, y, c, chips = _mesh_pos()
    return ([(x, y, 1 - c)] if kind in ("sibling", "both") else []) + (
        [(cx, cy, c) for cx, cy in chips] if kind in ("chips", "both") else [])


def _signal_peers(kind):
    barrier = pltpu.get_barrier_semaphore()
    targets = _peers(kind)
    for peer in targets:
        pl.semaphore_signal(barrier, inc=1, device_id=peer, device_id_type=MESH)
    return lambda: pl.semaphore_wait(barrier, len(targets))


def _split_start(name, bufs, n_copies, copies, peers, after=()):
    nb = len(bufs)

    def body(*refs):
        _signal_peers(peers)()
        refs = refs[:nb] + refs[nb + len(after):]
        for cp in copies(refs[:nb], refs[nb], refs[nb + 1]):
            cp.start()
        refs[-1][...] = jnp.zeros_like(refs[-1])

    outs = pl.pallas_call(
        body, name=name,
        out_shape=(pltpu.SemaphoreType.DMA((n_copies,)), pltpu.SemaphoreType.DMA((n_copies,)),
                   *[pltpu.HBM(b.shape, b.dtype) for b in bufs], jax.ShapeDtypeStruct((8, 128), F32)),
        in_specs=[_HBM] * nb + [_ANY] * len(after),
        out_specs=(_SEM, _SEM, *[_HBM] * nb, pl.BlockSpec(memory_space=pltpu.VMEM)),
        input_output_aliases={k: 2 + k for k in range(nb)},
        compiler_params=_params(1, has_side_effects=_EFFECT, collective_id=_PEER_SETS[peers]),
    )(*[_in_hbm(b) for b in bufs], *after)
    return _Started(outs[0], outs[1], list(outs[2:2 + nb]), outs[-1])


def _split_wait(name, started, copies, after):
    nb = len(started.bufs)
    after = tuple(after) if isinstance(after, (tuple, list)) else (after,)

    def body(*refs):
        for cp in copies(refs[:nb], refs[nb], refs[nb + 1]):
            cp.wait_send()
            cp.wait_recv()

    return list(pl.pallas_call(
        body, name=name,
        out_shape=tuple(pltpu.HBM(b.shape, b.dtype) for b in started.bufs),
        in_specs=[_HBM] * nb + [_SEM, _SEM] + [_ANY] * len(after),
        out_specs=tuple([_HBM] * nb),
        input_output_aliases={k: k for k in range(nb)},
        compiler_params=_params(1, has_side_effects=_EFFECT),
    )(*started.bufs, started.send, started.recv, *after))


def _x1_copies(ws):
    def copies(refs, send_sems, recv_sems):
        x, y, c, _ = _mesh_pos()
        out = []
        for k, w in enumerate(ws):
            for s in range(N_SHARD):
                out.append(pltpu.make_async_remote_copy(
                    src_ref=_UNITS[w](refs[k], s, 1 - c), dst_ref=refs[len(ws) + k].at[s],
                    send_sem=send_sems.at[N_SHARD * k + s], recv_sem=recv_sems.at[N_SHARD * k + s],
                    device_id=(x, y, 1 - c), device_id_type=MESH))
        return out
    return copies


def _x2_copies(n):
    def copies(refs, send_sems, recv_sems):
        x, y, c, chips = _mesh_pos()
        out = []
        for j, (cx, cy) in enumerate(chips):
            for k in range(n):
                out.append(pltpu.make_async_remote_copy(
                    src_ref=refs[k].at[2 * cx + cy], dst_ref=refs[n + k].at[j],
                    send_sem=send_sems.at[3 * k + j], recv_sem=recv_sems.at[3 * k + j],
                    device_id=(cx, cy, c), device_id_type=MESH))
        return out
    return copies


def _x3_copies(ws):
    def copies(refs, send_sems, recv_sems):
        x, y, c, _ = _mesh_pos()
        out = []
        for k, w in enumerate(ws):
            rows = _HALF_ROWS[w]
            mine = refs[k].at[pl.ds(_mo(c * rows, rows), rows), :]
            out.append(pltpu.make_async_remote_copy(
                src_ref=mine, dst_ref=mine, send_sem=send_sems.at[k], recv_sem=recv_sems.at[k],
                device_id=(x, y, 1 - c), device_id_type=MESH))
        return out
    return copies


def _x1_lands(ws, dtype=F32):
    return [lax.empty((N_SHARD,) + _UNIT_SHAPES[w], dtype) for w in ws]


def _x2_lands(ws):
    return [lax.empty((3,) + _UNIT_SHAPES[w], BF) for w in ws]


def _grad_add1(w, g, recv, pos):
    ur, uc = _UNIT_SHAPES[w]

    def body(pos_ref, g_ref, r_ref, csb_ref):
        csb_ref[0] = (g_ref[...] + r_ref[0].astype(F32)).astype(BF)

    u3 = lambda k, pos: (pos[2 + k], 0, 0)
    return pl.pallas_call(
        body, name=f"grad_add1_{w}",
        grid_spec=pltpu.PrefetchScalarGridSpec(
            num_scalar_prefetch=1, grid=(N_SHARD - 1,),
            in_specs=[pl.BlockSpec((ur, uc), lambda k, pos: (pos[0], pos[2 + k])), pl.BlockSpec((1, ur, uc), u3)],
            out_specs=pl.BlockSpec((1, ur, uc), u3)),
        out_shape=jax.ShapeDtypeStruct((N_SHARD, ur, uc), BF),
        compiler_params=_params(40, dimension_semantics=("arbitrary",)),
    )(pos, g, recv)


def _grad_add1_group(ws, gs, recvs, pos):
    n = len(ws)

    def body(pos_ref, *refs):
        s = pl.program_id(0)
        for k in range(n):
            g, r, own, csb = refs[k], refs[n + k], refs[2 * n + k], refs[3 * n + k]
            v = g[...] + r[0]
            csb[0] = v.astype(BF)

            @pl.when(s == pos_ref[1])
            def _(own=own, v=v):
                own[...] = v

    def g_spec(w):
        if w == 3:
            return pl.BlockSpec(_UNIT_SHAPES[w], lambda s, pos: (2 * s + pos[0], 0))
        return pl.BlockSpec(_UNIT_SHAPES[w], lambda s, pos: (pos[0], s))

    slot = lambda w: pl.BlockSpec((1,) + _UNIT_SHAPES[w], lambda s, pos: (s, 0, 0))
    outs = pl.pallas_call(
        body, name="grad_add1_group",
        grid_spec=pltpu.PrefetchScalarGridSpec(
            num_scalar_prefetch=1, grid=(N_SHARD,),
            in_specs=[g_spec(w) for w in ws] + [slot(w) for w in ws],
            out_specs=[pl.BlockSpec(_UNIT_SHAPES[w], lambda s, pos: (0, 0)) for w in ws] + [slot(w) for w in ws]),
        out_shape=tuple(jax.ShapeDtypeStruct(_UNIT_SHAPES[w], F32) for w in ws)
        + tuple(jax.ShapeDtypeStruct((N_SHARD,) + _UNIT_SHAPES[w], BF) for w in ws),
        compiler_params=_params(32, dimension_semantics=("arbitrary",)),
    )(pos, *gs, *recvs)
    return list(outs[:n]), list(outs[n:])


def _grad_add2_group(ws, owns, recvs):
    n = len(ws)

    def body(*refs):
        c = lax.axis_index("c")
        for k, w in enumerate(ws):
            own, r, o = refs[k], refs[n + k], refs[2 * n + k]
            rows = _HALF_ROWS[w]
            total = ((own[...] + r[0].astype(F32)) + r[1].astype(F32)) + r[2].astype(F32)
            o[pl.ds(_mo(c * rows, rows), rows), :] = total

    vm = pl.BlockSpec(memory_space=pltpu.VMEM)
    return list(pl.pallas_call(
        body, name="grad_add2_group",
        out_shape=tuple(jax.ShapeDtypeStruct(_SHARD_SHAPES[w], F32) for w in ws),
        in_specs=[vm] * (2 * n), out_specs=[vm] * n,
        compiler_params=_params(32),
    )(*owns, *recvs))


def _grad_add2(w, g, recv1, recv2, pos):
    ur, uc = _UNIT_SHAPES[w]
    nt = 4
    tr = ur // nt

    def body(pos_ref, g_ref, r1_ref, r2_ref, o_ref):
        own = g_ref[...] + r1_ref[0].astype(F32)
        o_ref[...] = ((own + r2_ref[0].astype(F32)) + r2_ref[1].astype(F32)) + r2_ref[2].astype(F32)

    mine = lambda t, pos: (pos[0] * nt + t, 0)
    return pl.pallas_call(
        body, name=f"grad_add2_{w}",
        grid_spec=pltpu.PrefetchScalarGridSpec(
            num_scalar_prefetch=1, grid=(nt,),
            in_specs=[pl.BlockSpec((tr, uc), lambda t, pos: (pos[0] * nt + t, pos[1])),
                      pl.BlockSpec((1, tr, uc), lambda t, pos: (pos[1], t, 0)),
                      pl.BlockSpec((3, tr, uc), lambda t, pos: (0, t, 0))],
            out_specs=pl.BlockSpec((tr, uc), mine)),
        out_shape=jax.ShapeDtypeStruct(_SHARD_SHAPES[w], F32),
        compiler_params=_params(32, dimension_semantics=("arbitrary",)),
    )(pos, g, recv1, recv2)


def _adamw_math(w, g, m, v):
    m = ADAM_B1 * m + (1.0 - ADAM_B1) * g
    v = ADAM_B2 * v + (1.0 - ADAM_B2) * (g * g)
    m_hat = m / ADAM_C1
    v_hat = v / ADAM_C2
    delta = -ADAM_LR * (m_hat / (jnp.sqrt(v_hat) + ADAM_EPS) + ADAM_WD * w)
    return delta, m, v


ADAMW_STEPS = 4


def _adamw(ws_, gs, ms, vs):
    n = len(ws_)

    def body(*refs):
        for k in range(n):
            w, g, m, v = (refs[j * n + k] for j in range(4))
            d, nm, nv, gc = (refs[(4 + j) * n + k] for j in range(4))
            gv = g[...]
            d[...], nm[...], nv[...] = _adamw_math(w[...], gv, m[...], v[...])
            gc[...] = gv

    specs = [pl.BlockSpec((a.shape[0] // ADAMW_STEPS, a.shape[1]), lambda i: (i, 0)) for a in ws_] * 4
    outs = pl.pallas_call(
        body, name="adamw", grid=(ADAMW_STEPS,),
        out_shape=tuple(jax.ShapeDtypeStruct(a.shape, F32) for _ in range(4) for a in ws_),
        in_specs=specs, out_specs=specs,
        compiler_params=_params(40, dimension_semantics=("arbitrary",)),
    )(*ws_, *gs, *ms, *vs)
    return [tuple(outs[j * n + k] for j in range(4)) for k in range(n)]


_REL_PAD = 384
_VEC_FIELDS = (("norm_g", 0, D_MODEL), ("b_gate", 1024, 2 * D_MODEL), ("sgu_ln_g", 3072, D_B),
               ("sgu_ln_b", 3584, D_B), ("b_s", 4096, N_GROUPS * 128), ("final_g", 4608, D_MODEL))
_LOSS_OFF = 5632
_REL_OFF = 5760
_NV = _REL_OFF + N_HEADS * _REL_PAD
_N_FIELDS = len(_VEC_FIELDS) + 2


_B_S_FIELD = [f[0] for f in _VEC_FIELDS].index("b_s")


def _assemble_row(dst, fields, transposed_b_s):
    for f, (_, off, n) in enumerate(_VEC_FIELDS):
        if transposed_b_s and f == _B_S_FIELD:
            t = fields[f][...].T
            for g in range(N_GROUPS):
                dst[:, off + 128 * g:off + 128 * (g + 1)] = t[g:g + 1, :]
        else:
            dst[:, off:off + n] = fields[f][...]
    for r in range(N_HEADS):
        dst[:, _REL_OFF + _REL_PAD * r:_REL_OFF + _REL_PAD * (r + 1)] = fields[len(_VEC_FIELDS)][r:r + 1, :]


def _small_reduce(grads, loss_row, after=()):
    n_in = _N_FIELDS + 1

    def body(*refs):
        g_refs, loss_ref = refs[:_N_FIELDS], refs[_N_FIELDS]
        out_v, out_w = refs[n_in:n_in + 2]
        mine_v, mine_w, gath_v, gath_w, send_sems, recv_sems = refs[n_in + 2:]
        x, y, c, chips = _mesh_pos()
        me, sibling = (x, y, c), (x, y, 1 - c)

        peers_entered = _signal_peers("both")
        _assemble_row(mine_v, g_refs, True)
        mine_v[:, _LOSS_OFF:_LOSS_OFF + 128] = loss_ref[...]
        mine_w[...] = g_refs[-1][...].astype(BF)
        peers_entered()
        my_k = 4 * x + 2 * y + c
        gath_v[my_k] = mine_v[...]
        gath_w[my_k] = mine_w[...]

        def copy(k, gath, block, to, src=None):
            dst = gath.at[4 * block[0] + 2 * block[1] + block[2]]
            return pltpu.make_async_remote_copy(
                src_ref=dst if src is None else src, dst_ref=dst,
                send_sem=send_sems.at[k], recv_sem=recv_sems.at[k], device_id=to, device_id_type=MESH)

        bufs = ((gath_v, mine_v), (gath_w, mine_w))
        first, passed = [], []
        for b, (gath, mine) in enumerate(bufs):
            first.append(copy(7 * b, gath, me, sibling, src=mine))
            first += [copy(7 * b + 1 + j, gath, me, (*chip, c), src=mine) for j, chip in enumerate(chips)]
        for cp in first:
            cp.start()
        for b, (gath, _) in enumerate(bufs):
            for j, chip in enumerate(chips):
                copy(7 * b + 1 + j, gath, (*chip, c), me).wait_recv()
                cp = copy(7 * b + 4 + j, gath, (*chip, c), sibling)
                cp.start()
                passed.append(cp)
        for b, (gath, _) in enumerate(bufs):
            copy(7 * b, gath, sibling, me).wait_recv()
            for j, chip in enumerate(chips):
                copy(7 * b + 4 + j, gath, (*chip, 1 - c), me).wait_recv()
        for cp in first + passed:
            cp.wait_send()

        tot_v, tot_w = gath_v[0], gath_w[0].astype(F32)
        for k in range(1, 8):
            tot_v = tot_v + gath_v[k]
            tot_w = tot_w + gath_w[k].astype(F32)
        out_v[...] = tot_v
        out_w[...] = tot_w

    vm = pl.BlockSpec(memory_space=pltpu.VMEM)
    return pl.pallas_call(
        _after(body, n_in, after), name="small_reduce",
        out_shape=(jax.ShapeDtypeStruct((1, _NV), F32), jax.ShapeDtypeStruct((N_GROUPS * 128, 128), F32)),
        in_specs=[vm] * n_in + [_ANY] * len(after), out_specs=[vm] * 2,
        scratch_shapes=[pltpu.VMEM((1, _NV), F32), pltpu.VMEM((N_GROUPS * 128, 128), BF),
                        pltpu.VMEM((8, 1, _NV), F32), pltpu.VMEM((8, N_GROUPS * 128, 128), BF),
                        pltpu.SemaphoreType.DMA((14,)), pltpu.SemaphoreType.DMA((14,))],
        compiler_params=_params(32, collective_id=_PEER_SETS["both"]),
    )(*grads, loss_row, *after)


def _small_adamw(tot_v, tot_w, params):
    n_in = 2 + 3 * _N_FIELDS

    def body(*refs):
        tv_ref, tw_ref = refs[:2]
        p_refs = [refs[2 + k * _N_FIELDS:2 + (k + 1) * _N_FIELDS] for k in range(3)]
        outs = refs[n_in:n_in + 4 * _N_FIELDS + 1]
        wmv = refs[-1]
        for k in range(3):
            _assemble_row(wmv.at[k], p_refs[k], False)
            wmv[k, :, _LOSS_OFF:_LOSS_OFF + 128] = jnp.zeros((1, 128), F32)
        tot_v, tot_w = tv_ref[...], tw_ref[...]
        res_v = (tot_v,) + _adamw_math(wmv[0], tot_v, wmv[1], wmv[2])
        res_w = (tot_w,) + _adamw_math(p_refs[0][-1][...], tot_w, p_refs[1][-1][...], p_refs[2][-1][...])
        for kind in range(4):
            o = outs[kind * _N_FIELDS:(kind + 1) * _N_FIELDS]
            for f, (_, off, n) in enumerate(_VEC_FIELDS):
                o[f][...] = res_v[kind][:, off:off + n]
            for r in range(N_HEADS):
                o[len(_VEC_FIELDS)][r:r + 1, :] = res_v[kind][:, _REL_OFF + _REL_PAD * r:_REL_OFF + _REL_PAD * r + N_REL]
            o[-1][...] = res_w[kind]
        outs[-1][...] = tot_v[:, _LOSS_OFF:_LOSS_OFF + 128]

    field_shapes = [(1, n) for _, _, n in _VEC_FIELDS] + [(N_HEADS, N_REL), (N_GROUPS * 128, 128)]
    vm = pl.BlockSpec(memory_space=pltpu.VMEM)
    operands = [tot_v, tot_w] + [a for p in params for a in p]
    assert len(operands) == n_in
    outs = pl.pallas_call(
        body, name="small_adamw",
        out_shape=tuple(jax.ShapeDtypeStruct(s, F32) for _ in range(4) for s in field_shapes)
        + (jax.ShapeDtypeStruct((1, 128), F32),),
        in_specs=[vm] * n_in, out_specs=[vm] * (4 * _N_FIELDS + 1),
        scratch_shapes=[pltpu.VMEM((3, 1, _NV), F32)],
        compiler_params=_params(32),
    )(*operands)
    return [outs[k * _N_FIELDS:(k + 1) * _N_FIELDS] for k in range(4)], outs[-1]


def _small_fields(norm_g, b_gate, ln_g, ln_b, b_s, final_g, rel_bias, w_s):
    rel = jnp.pad(rel_bias.reshape(N_HEADS, N_REL), ((0, 0), (0, _REL_PAD - N_REL)))
    return (norm_g, b_gate, ln_g, ln_b, b_s.reshape(1, N_GROUPS * 128), final_g.reshape(1, D_MODEL),
            rel, w_s.reshape(N_GROUPS * 128, 128))


def _small_outputs(fields):
    n_g, b_g, l_g, l_b, b_s, f_g, rel, w_s = fields
    return (n_g, b_g, rel.reshape(1, N_HEADS, N_REL), l_g, l_b,
            w_s.reshape(1, N_GROUPS, 128, 128), b_s.reshape(1, N_GROUPS, 128), f_g.reshape(D_MODEL))


def kernel(x, norm_g, w_in, b_gate, rel_bias, sgu_ln_g, sgu_ln_b, w_s, b_s, w_pa, w_pb, w_out, final_g, loss_target, m_norm_g, m_w_in, m_b_gate, m_rel_bias, m_sgu_ln_g, m_sgu_ln_b, m_w_s, m_b_s, m_w_pa, m_w_pb, m_w_out, m_final_g, v_norm_g, v_w_in, v_b_gate, v_rel_bias, v_sgu_ln_g, v_sgu_ln_b, v_w_s, v_b_s, v_w_pa, v_w_pb, v_w_out, v_final_g):
    S = x.shape[1]
    xs = x.reshape(S, D_MODEL)
    tgt = loss_target.reshape(S, D_MODEL)
    big_w = (w_in[0], w_pa[0], w_pb[0], w_out[0])
    big_m = (m_w_in[0], m_w_pa[0], m_w_pb[0], m_w_out[0])
    big_v = (v_w_in[0], v_w_pa[0], v_w_pb[0], v_w_out[0])
    rel = rel_bias[0]
    ws = w_s[0]
    bst = b_s[0].T
    fg = final_g.reshape(1, D_MODEL)
    chip = 2 * lax.axis_index("x") + lax.axis_index("y")
    pos = jnp.stack([lax.axis_index("c"), chip] + [(chip + k) % N_SHARD for k in range(1, N_SHARD)]).astype(jnp.int32)

    (w_in_bf,), staged, band_bias = _ag_weights((0,), big_w[:1], (1, 2, 3), big_w[1:], rel)
    ag_s = _split_start("ag_small_start", staged, 9, _gather_copies((1, 2, 3)), "chips", after=(w_in_bf,))

    ht, q3, k3, v3, zrest = _inproj_fwd(xs, norm_g, w_in_bf, after=(ag_s.token,))
    att, lse = _attn_fwd(q3, k3, v3, band_bias)
    w_pa_bf, w_pb_bf, w_out_bf = _split_wait("ag_small_wait", ag_s, _gather_copies((1, 2, 3)), att)
    (d_out, d_att, dzt, dzs, gw_out, gw_pa, gw_pb, g_bgate, g_final, loss_row,
     g_ws, g_bs_t, g_lng, g_lnb) = _tail_sgu(
        att, zrest, xs, tgt, w_pa_bf, w_pb_bf, w_out_bf, b_gate, fg, sgu_ln_g, sgu_ln_b, ws, bst)
    ws_s, ws_i = (1, 2, 3), (0,)

    x1s = _split_start("gx1s_start", [gw_pa, gw_pb, gw_out] + _x1_lands(ws_s), 12, _x1_copies(ws_s), "sibling")
    dq, dk, dv, d_gp = _attn_bwd(q3, k3, v3, d_att, lse, band_bias, after=(x1s.token,))
    got = _split_wait("gx1s_wait", x1s, _x1_copies(ws_s), dq)
    own_s, csb_s = _grad_add1_group(ws_s, got[:3], got[3:], pos)

    x2s = _split_start("gx2s_start", csb_s + _x2_lands(ws_s), 9, _x2_copies(3), "chips")
    gw_in, gw_in_bf = _gw_in(ht, dq, dk, dv, dzt, dzs, after=(x2s.token,))
    x1i = _split_start("gx1i_start", [gw_in_bf] + _x1_lands(ws_i, BF), 4, _x1_copies(ws_i), "sibling")
    got = _split_wait("gx2s_wait", x2s, _x2_copies(3), x1i.token)
    halves_s = _grad_add2_group(ws_s, own_s, got[3:])
    x3s = _split_start("gx3s_start", halves_s, 3, _x3_copies(ws_s), "sibling")
    g_rel = jnp.pad(d_gp[:, 384:384 + N_REL][:, ::-1], ((0, 0), (0, _REL_PAD - N_REL)))
    small_params = (_small_fields(norm_g, b_gate, sgu_ln_g, sgu_ln_b, b_s, final_g, rel_bias, w_s),
                    _small_fields(m_norm_g, m_b_gate, m_sgu_ln_g, m_sgu_ln_b, m_b_s, m_final_g, m_rel_bias, m_w_s),
                    _small_fields(v_norm_g, v_b_gate, v_sgu_ln_g, v_sgu_ln_b, v_b_s, v_final_g, v_rel_bias, v_w_s))
    relayouts = (g_rel,) + tuple(fields[-2] for fields in small_params)
    recv1_i = _split_wait("gx1i_wait", x1i, _x1_copies(ws_i), (x3s.token,) + relayouts)[1]
    csb_i = _grad_add1(0, gw_in, recv1_i, pos)

    x2i = _split_start("gx2i_start", [csb_i] + _x2_lands(ws_i), 3, _x2_copies(1), "chips")
    grad_x, g_norm = _dh_gradx(dq, dk, dv, dzt, dzs, w_in_bf, xs, norm_g, d_out, after=(x2i.token,))
    g_shards_s = _split_wait("gx3s_wait", x3s, _x3_copies(ws_s), grad_x)
    got = _split_wait("gx2i_wait", x2i, _x2_copies(1), grad_x)
    half_i = _grad_add2(0, gw_in, recv1_i, got[1], pos)
    x3i = _split_start("gx3i_start", [half_i], 1, _x3_copies(ws_i), "sibling")

    small_grads = (g_norm, g_bgate, g_lng, g_lnb, g_bs_t, g_final, g_rel, g_ws.reshape(N_GROUPS * 128, 128))
    tot_v, tot_w = _small_reduce(small_grads, loss_row, after=(x3i.token,))
    (gsum, sdelta, sm, sv), loss_out = _small_adamw(tot_v, tot_w, small_params)

    g_shard_i, = _split_wait("gx3i_wait", x3i, _x3_copies(ws_i), loss_out)
    big = _adamw(big_w, [g_shard_i] + g_shards_s, big_m, big_v)
    sg_out, sd_out, sm_out, sv_out = (_small_outputs(f) for f in (gsum, sdelta, sm, sv))
    loss = loss_out[0, 0]

    def assemble(small, bigs):
        n_g, b_g, r_b, l_g, l_b, w_s_, b_s_, f_g = small
        b_in, b_pa, b_pb, b_out = (b[None] for b in bigs)
        return (n_g, b_in, b_g, r_b, l_g, l_b, w_s_, b_s_, b_pa, b_pb, b_out, f_g)

    grads_out = assemble(sg_out, [b[3] for b in big])
    delta_out = assemble(sd_out, [b[0] for b in big])
    m_out = assemble(sm_out, [b[1] for b in big])
    v_out = assemble(sv_out, [b[2] for b in big])
    return (loss, grad_x.reshape(1, S, D_MODEL), *grads_out, *delta_out, *m_out, *v_out)
```
